```python
import jax, jax.numpy as jnp
from jax import lax
import numpy as np

D_MODEL = 2048
BATCH = 8
SEQ = 4096
DEPTH = 1

A_HEADS = 8
A_HEAD_DIM = 128
A_WIDTH = A_HEADS * A_HEAD_DIM
CONV_WIDTH = 4
CHUNK_A = 64
B_GROUPS = 8
B_GROUP_DIM = 128
B_WIDTH = B_GROUPS * B_GROUP_DIM
CHUNK_B = 128
MIX_WIDTH = A_WIDTH + B_WIDTH
IN_SIZES = (3 * A_WIDTH, A_WIDTH, A_HEADS, A_HEADS, B_WIDTH, B_WIDTH, B_WIDTH)
IN_WIDTH = int(sum(IN_SIZES))
IN_SPLITS = tuple(int(s) for s in np.cumsum(IN_SIZES)[:-1])
EPS = 1e-6

kernel_name = "hymba_style_gdn_gmlp_hybrid"


def rms_norm(x, w):
    xf = x.astype(jnp.float32)
    y = xf * lax.rsqrt(jnp.mean(xf * xf, axis=-1, keepdims=True) + EPS)
    return (y * w.astype(jnp.float32)).astype(x.dtype)


def layer_norm(x, w, b):
    xf = x.astype(jnp.float32)
    mu = jnp.mean(xf, axis=-1, keepdims=True)
    var = jnp.mean(jnp.square(xf - mu), axis=-1, keepdims=True)
    y = (xf - mu) * lax.rsqrt(var + EPS)
    return (y * w.astype(jnp.float32) + b.astype(jnp.float32)).astype(x.dtype)


def l2_normalize(x):
    return x * lax.rsqrt(jnp.sum(x * x, axis=-1, keepdims=True) + EPS)


def causal_depthwise_conv(x, w):
    C = x.shape[-1]
    return lax.conv_general_dilated(
        x, w[:, None, :].astype(x.dtype), window_strides=(1,),
        padding=[(CONV_WIDTH - 1, 0)],
        dimension_numbers=("NWC", "WIO", "NWC"), feature_group_count=C)


def chunk_gated_delta_rule(q, k, v, g, beta):
    B, T, H, D = q.shape
    N, C = T // CHUNK_A, CHUNK_A
    chunk4 = lambda t: t.reshape(B, N, C, H, D).transpose(0, 3, 1, 2, 4)
    chunk3 = lambda t: t.reshape(B, N, C, H).transpose(0, 3, 1, 2)
    q = chunk4(q) * (D ** -0.5)
    k, v = chunk4(k), chunk4(v)
    beta, g = chunk3(beta), chunk3(g)
    kb = k * beta[..., None]
    vb = v * beta[..., None]
    gc = jnp.cumsum(g, axis=-1)
    incl = jnp.tril(jnp.ones((C, C), dtype=bool))
    strict = jnp.tril(jnp.ones((C, C), dtype=bool), k=-1)
    diff = gc[..., :, None] - gc[..., None, :]
    decay = jnp.where(incl, jnp.exp(jnp.where(incl, diff, 0.0)), 0.0)
    L = jnp.where(strict, jnp.einsum('bhnid,bhnjd->bhnij', kb, k) * decay, 0.0)
    rhs = jnp.concatenate([vb, kb * jnp.exp(gc)[..., None]], axis=-1)
    sol = lax.linalg.triangular_solve(L + jnp.eye(C, dtype=L.dtype), rhs,
                                      left_side=True, lower=True, unit_diagonal=True)
    u, w = sol[..., :D], sol[..., D:]
    attn = jnp.where(incl, jnp.einsum('bhnid,bhnjd->bhnij', q, k) * decay, 0.0)
    xs = tuple(jnp.moveaxis(t, 2, 0) for t in (q, k, u, w, attn, gc))

    def step(S, inp):
        qi, ki, ui, wi, ai, gi = inp
        v_new = ui - jnp.einsum('bhcd,bhde->bhce', wi, S)
        o = (jnp.einsum('bhcd,bhde->bhce', qi * jnp.exp(gi)[..., None], S)
             + jnp.einsum('bhij,bhje->bhie', ai, v_new))
        g_last = gi[..., -1]
        k_dec = ki * jnp.exp(g_last[..., None] - gi)[..., None]
        S = S * jnp.exp(g_last)[..., None, None] + jnp.einsum('bhcd,bhce->bhde', k_dec, v_new)
        return S, o

    S0 = jnp.zeros((B, H, D, D), dtype=jnp.float32)
    _, o = lax.scan(step, S0, xs)
    return o.transpose(1, 0, 3, 2, 4).reshape(B, T, H, D)


def chunked_causal_sgu(v, w_s, b_s):
    Bn, T, _ = v.shape
    N = T // CHUNK_B
    vc = v.reshape(Bn, N, CHUNK_B, B_GROUPS, B_GROUP_DIM)
    mask = jnp.tril(jnp.ones((CHUNK_B, CHUNK_B), dtype=bool))
    w_m = jnp.where(mask[None], w_s, 0.0).astype(v.dtype)
    s = jnp.einsum('gts,bnsgc->bntgc', w_m, vc) + b_s.T.astype(v.dtype)[None, None, :, :, None]
    return s.reshape(Bn, T, B_WIDTH)


def _fwd_setup_inputs(seed: int = 0) -> dict:
    key = jax.random.key(seed)
    ks = jax.random.split(key, 14)
    f32 = jnp.float32
    x = jax.random.normal(ks[0], (BATCH, SEQ, D_MODEL), f32)
    norm_w = 1.0 + 0.02 * jax.random.normal(ks[1], (DEPTH, D_MODEL), f32)
    w_in = jax.random.normal(ks[2], (DEPTH, D_MODEL, IN_WIDTH), f32) * D_MODEL ** -0.5
    conv_w = jax.random.normal(ks[3], (DEPTH, CONV_WIDTH, 3 * A_WIDTH), f32) * CONV_WIDTH ** -0.5
    a_log = jnp.log(jax.random.uniform(ks[4], (DEPTH, A_HEADS), f32, 1.0, 16.0))
    dt = jnp.exp(jax.random.uniform(ks[5], (DEPTH, A_HEADS), f32,
                                    np.log(1e-3).astype(np.float32), np.log(1e-1).astype(np.float32)))
    dt_bias = dt + jnp.log(-jnp.expm1(-dt))
    head_norm_w = 1.0 + 0.02 * jax.random.normal(ks[6], (DEPTH, A_HEAD_DIM), f32)
    sgu_ln_w = 1.0 + 0.02 * jax.random.normal(ks[7], (DEPTH, B_WIDTH), f32)
    sgu_ln_b = 0.02 * jax.random.normal(ks[8], (DEPTH, B_WIDTH), f32)
    w_spatial = jax.random.normal(ks[9], (DEPTH, B_GROUPS, CHUNK_B, CHUNK_B), f32) * CHUNK_B ** -0.5
    b_spatial = 1.0 + 0.02 * jax.random.normal(ks[10], (DEPTH, B_GROUPS, CHUNK_B), f32)
    w_out = jax.random.normal(ks[11], (DEPTH, MIX_WIDTH, D_MODEL), f32) * MIX_WIDTH ** -0.5
    final_norm_w = 1.0 + 0.02 * jax.random.normal(ks[12], (D_MODEL,), f32)
    return {"x": x, "norm_w": norm_w, "w_in": w_in, "conv_w": conv_w, "a_log": a_log,
            "dt_bias": dt_bias, "head_norm_w": head_norm_w, "sgu_ln_w": sgu_ln_w,
            "sgu_ln_b": sgu_ln_b, "w_spatial": w_spatial, "b_spatial": b_spatial,
            "w_out": w_out, "final_norm_w": final_norm_w}


def _fwd_reference(x, norm_w, w_in, conv_w, a_log, dt_bias, head_norm_w, sgu_ln_w,
              sgu_ln_b, w_spatial, b_spatial, w_out, final_norm_w):
    Bn, T, _ = x.shape
    h = x
    for l in range(DEPTH):
        xn = rms_norm(h, norm_w[l])
        proj = xn @ w_in[l].astype(x.dtype)
        qkv, z_a, b_raw, a_raw, u_b, v_b, z_b = jnp.split(proj, IN_SPLITS, axis=-1)

        qkv = jax.nn.silu(causal_depthwise_conv(qkv, conv_w[l])).astype(jnp.float32)
        q, k, v = jnp.split(qkv, 3, axis=-1)
        q = l2_normalize(q.reshape(Bn, T, A_HEADS, A_HEAD_DIM))
        k = l2_normalize(k.reshape(Bn, T, A_HEADS, A_HEAD_DIM))
        v = v.reshape(Bn, T, A_HEADS, A_HEAD_DIM)
        beta = jax.nn.sigmoid(b_raw.astype(jnp.float32))
        g = -jnp.exp(a_log[l].astype(jnp.float32)) * jax.nn.softplus(
            a_raw.astype(jnp.float32) + dt_bias[l].astype(jnp.float32))
        o_a = chunk_gated_delta_rule(q, k, v, g, beta)
        o_a = rms_norm(o_a, head_norm_w[l]).astype(x.dtype)
        o_a = (o_a * jax.nn.silu(z_a.reshape(Bn, T, A_HEADS, A_HEAD_DIM))).reshape(Bn, T, A_WIDTH)

        v_n = layer_norm(v_b, sgu_ln_w[l], sgu_ln_b[l])
        o_b = u_b * chunked_causal_sgu(v_n, w_spatial[l], b_spatial[l]) * jax.nn.silu(z_b)

        mix = jnp.concatenate([o_a, o_b], axis=-1) @ w_out[l].astype(x.dtype)
        h = h + mix
    return rms_norm(h, final_norm_w)


import jax as _jax
import jax.numpy as _jnp

TWIN_FORMAT = 'train_step'
FWD_PARAMS = ['x', 'norm_w', 'w_in', 'conv_w', 'a_log', 'dt_bias', 'head_norm_w', 'sgu_ln_w', 'sgu_ln_b', 'w_spatial', 'b_spatial', 'w_out', 'final_norm_w']
TWIN_WEIGHTS = ['norm_w', 'w_in', 'conv_w', 'a_log', 'dt_bias', 'head_norm_w', 'sgu_ln_w', 'sgu_ln_b', 'w_spatial', 'b_spatial', 'w_out', 'final_norm_w']
TWIN_DIFF_INPUT = 'x'
TWIN_INPUTS = ['x', 'norm_w', 'w_in', 'conv_w', 'a_log', 'dt_bias', 'head_norm_w', 'sgu_ln_w', 'sgu_ln_b', 'w_spatial', 'b_spatial', 'w_out', 'final_norm_w', 'loss_target', 'm_norm_w', 'm_w_in', 'm_conv_w', 'm_a_log', 'm_dt_bias', 'm_head_norm_w', 'm_sgu_ln_w', 'm_sgu_ln_b', 'm_w_spatial', 'm_b_spatial', 'm_w_out', 'm_final_norm_w', 'v_norm_w', 'v_w_in', 'v_conv_w', 'v_a_log', 'v_dt_bias', 'v_head_norm_w', 'v_sgu_ln_w', 'v_sgu_ln_b', 'v_w_spatial', 'v_b_spatial', 'v_w_out', 'v_final_norm_w']
TWIN_OUTPUTS = ['loss', 'grad_x', 'grad_norm_w', 'grad_w_in', 'grad_conv_w', 'grad_a_log', 'grad_dt_bias', 'grad_head_norm_w', 'grad_sgu_ln_w', 'grad_sgu_ln_b', 'grad_w_spatial', 'grad_b_spatial', 'grad_w_out', 'grad_final_norm_w', 'delta_norm_w', 'delta_w_in', 'delta_conv_w', 'delta_a_log', 'delta_dt_bias', 'delta_head_norm_w', 'delta_sgu_ln_w', 'delta_sgu_ln_b', 'delta_w_spatial', 'delta_b_spatial', 'delta_w_out', 'delta_final_norm_w', 'new_m_norm_w', 'new_m_w_in', 'new_m_conv_w', 'new_m_a_log', 'new_m_dt_bias', 'new_m_head_norm_w', 'new_m_sgu_ln_w', 'new_m_sgu_ln_b', 'new_m_w_spatial', 'new_m_b_spatial', 'new_m_w_out', 'new_m_final_norm_w', 'new_v_norm_w', 'new_v_w_in', 'new_v_conv_w', 'new_v_a_log', 'new_v_dt_bias', 'new_v_head_norm_w', 'new_v_sgu_ln_w', 'new_v_sgu_ln_b', 'new_v_w_spatial', 'new_v_b_spatial', 'new_v_w_out', 'new_v_final_norm_w']
TWIN_LEAF_KINDS = {'loss': 'loss', 'grad_x': 'grad_x', 'grad_norm_w': 'grad_w', 'grad_w_in': 'grad_w', 'grad_conv_w': 'grad_w', 'grad_a_log': 'grad_w', 'grad_dt_bias': 'grad_w', 'grad_head_norm_w': 'grad_w', 'grad_sgu_ln_w': 'grad_w', 'grad_sgu_ln_b': 'grad_w', 'grad_w_spatial': 'grad_w', 'grad_b_spatial': 'grad_w', 'grad_w_out': 'grad_w', 'grad_final_norm_w': 'grad_w', 'delta_norm_w': 'delta_w', 'delta_w_in': 'delta_w', 'delta_conv_w': 'delta_w', 'delta_a_log': 'delta_w', 'delta_dt_bias': 'delta_w', 'delta_head_norm_w': 'delta_w', 'delta_sgu_ln_w': 'delta_w', 'delta_sgu_ln_b': 'delta_w', 'delta_w_spatial': 'delta_w', 'delta_b_spatial': 'delta_w', 'delta_w_out': 'delta_w', 'delta_final_norm_w': 'delta_w', 'new_m_norm_w': 'new_m', 'new_m_w_in': 'new_m', 'new_m_conv_w': 'new_m', 'new_m_a_log': 'new_m', 'new_m_dt_bias': 'new_m', 'new_m_head_norm_w': 'new_m', 'new_m_sgu_ln_w': 'new_m', 'new_m_sgu_ln_b': 'new_m', 'new_m_w_spatial': 'new_m', 'new_m_b_spatial': 'new_m', 'new_m_w_out': 'new_m', 'new_m_final_norm_w': 'new_m', 'new_v_norm_w': 'new_v', 'new_v_w_in': 'new_v', 'new_v_conv_w': 'new_v', 'new_v_a_log': 'new_v', 'new_v_dt_bias': 'new_v', 'new_v_head_norm_w': 'new_v', 'new_v_sgu_ln_w': 'new_v', 'new_v_sgu_ln_b': 'new_v', 'new_v_w_spatial': 'new_v', 'new_v_b_spatial': 'new_v', 'new_v_w_out': 'new_v', 'new_v_final_norm_w': 'new_v'}


def _forward(args):
    return _fwd_reference(*[args[k] for k in FWD_PARAMS])


def _output_shape():
    def fwd():
        inp = _fwd_setup_inputs(0)
        return _fwd_reference(*[inp[k] for k in FWD_PARAMS])
    out = _jax.eval_shape(fwd)
    return out.shape, out.dtype

N_MICROBATCH = 1
ADAM_LR = 0.001
ADAM_B1 = 0.9
ADAM_B2 = 0.999
ADAM_EPS = 1e-08
ADAM_WD = 0.01
ADAM_STEP = 10
PER_EXAMPLE_BATCH_AXIS = {'x': 0, 'loss_target': 0}
SHARED_INPUTS = []
_WEIGHT_DTYPES = {'norm_w': _jnp.float32, 'w_in': _jnp.float32, 'conv_w': _jnp.float32, 'a_log': _jnp.float32, 'dt_bias': _jnp.float32, 'head_norm_w': _jnp.float32, 'sgu_ln_w': _jnp.float32, 'sgu_ln_b': _jnp.float32, 'w_spatial': _jnp.float32, 'b_spatial': _jnp.float32, 'w_out': _jnp.float32, 'final_norm_w': _jnp.float32}
MOMENT_SCALE = {'norm_w': 7.776763e-02, 'w_in': 4.168380e-02, 'conv_w': 3.392230e-02, 'a_log': 2.267524e-01, 'dt_bias': 2.074061e-01, 'head_norm_w': 1.259870e-01, 'sgu_ln_w': 2.910721e-02, 'sgu_ln_b': 3.093819e-02, 'w_spatial': 3.092508e-02, 'b_spatial': 4.373419e-02, 'w_out': 4.884033e-02, 'final_norm_w': 1.600503e+01}


def _to_microbatches(a, axis):
    t = _jnp.moveaxis(a, axis, 0)
    t = t.reshape((N_MICROBATCH, t.shape[0] // N_MICROBATCH) + t.shape[1:])
    return _jnp.moveaxis(t, 1, axis + 1)


def setup_inputs(seed: int = 0) -> dict:
    inp = _fwd_setup_inputs(seed)
    key = _jax.random.fold_in(_jax.random.key(seed), 7919)
    shape, _ = _output_shape()
    out = dict(inp)
    out["loss_target"] = _jax.random.normal(_jax.random.fold_in(key, 0), shape, _jnp.float32)
    for i, name in enumerate(TWIN_WEIGHTS):
        w = inp[name].astype(_jnp.float32)
        if MOMENT_SCALE is None:
            s = _jnp.sqrt(_jnp.mean(_jnp.square(w)) + 1e-30)
        else:
            s = MOMENT_SCALE[name]
        km, kv = _jax.random.split(_jax.random.fold_in(key, i + 1))
        out[name] = w
        out["m_" + name] = s * _jax.random.normal(km, w.shape, _jnp.float32)
        out["v_" + name] = (s * s) * _jax.random.uniform(kv, w.shape, _jnp.float32, 0.5, 1.5)
    if N_MICROBATCH > 1:
        for name, axis in PER_EXAMPLE_BATCH_AXIS.items():
            out[name] = _to_microbatches(out[name], axis)
    return {'x': out['x'], 'norm_w': out['norm_w'], 'w_in': out['w_in'], 'conv_w': out['conv_w'], 'a_log': out['a_log'], 'dt_bias': out['dt_bias'], 'head_norm_w': out['head_norm_w'], 'sgu_ln_w': out['sgu_ln_w'], 'sgu_ln_b': out['sgu_ln_b'], 'w_spatial': out['w_spatial'], 'b_spatial': out['b_spatial'], 'w_out': out['w_out'], 'final_norm_w': out['final_norm_w'], 'loss_target': out['loss_target'], 'm_norm_w': out['m_norm_w'], 'm_w_in': out['m_w_in'], 'm_conv_w': out['m_conv_w'], 'm_a_log': out['m_a_log'], 'm_dt_bias': out['m_dt_bias'], 'm_head_norm_w': out['m_head_norm_w'], 'm_sgu_ln_w': out['m_sgu_ln_w'], 'm_sgu_ln_b': out['m_sgu_ln_b'], 'm_w_spatial': out['m_w_spatial'], 'm_b_spatial': out['m_b_spatial'], 'm_w_out': out['m_w_out'], 'm_final_norm_w': out['m_final_norm_w'], 'v_norm_w': out['v_norm_w'], 'v_w_in': out['v_w_in'], 'v_conv_w': out['v_conv_w'], 'v_a_log': out['v_a_log'], 'v_dt_bias': out['v_dt_bias'], 'v_head_norm_w': out['v_head_norm_w'], 'v_sgu_ln_w': out['v_sgu_ln_w'], 'v_sgu_ln_b': out['v_sgu_ln_b'], 'v_w_spatial': out['v_w_spatial'], 'v_b_spatial': out['v_b_spatial'], 'v_w_out': out['v_w_out'], 'v_final_norm_w': out['v_final_norm_w']}


def _loss(weights, diff, rest, loss_target):
    with _jax.named_scope("forward"):
        args = {**rest, TWIN_DIFF_INPUT: diff, **{k: w.astype(_WEIGHT_DTYPES[k]) for k, w in weights.items()}}
        y = _forward(args)
    with _jax.named_scope("loss_head"):
        err = _jnp.square(y.astype(_jnp.float32) - loss_target)
        return 0.5 * _jnp.sum(_jnp.mean(err, axis=-1)) if err.ndim else 0.5 * err


def _adamw(w, g, m, v):
    m = ADAM_B1 * m + (1.0 - ADAM_B1) * g
    v = ADAM_B2 * v + (1.0 - ADAM_B2) * _jnp.square(g)
    m_hat = m / (1.0 - ADAM_B1 ** ADAM_STEP)
    v_hat = v / (1.0 - ADAM_B2 ** ADAM_STEP)
    delta = -ADAM_LR * (m_hat / (_jnp.sqrt(v_hat) + ADAM_EPS) + ADAM_WD * w)
    return delta, m, v


def reference(x, norm_w, w_in, conv_w, a_log, dt_bias, head_norm_w, sgu_ln_w, sgu_ln_b, w_spatial, b_spatial, w_out, final_norm_w, loss_target, m_norm_w, m_w_in, m_conv_w, m_a_log, m_dt_bias, m_head_norm_w, m_sgu_ln_w, m_sgu_ln_b, m_w_spatial, m_b_spatial, m_w_out, m_final_norm_w, v_norm_w, v_w_in, v_conv_w, v_a_log, v_dt_bias, v_head_norm_w, v_sgu_ln_w, v_sgu_ln_b, v_w_spatial, v_b_spatial, v_w_out, v_final_norm_w):
    given = dict(x=x, norm_w=norm_w, w_in=w_in, conv_w=conv_w, a_log=a_log, dt_bias=dt_bias, head_norm_w=head_norm_w, sgu_ln_w=sgu_ln_w, sgu_ln_b=sgu_ln_b, w_spatial=w_spatial, b_spatial=b_spatial, w_out=w_out, final_norm_w=final_norm_w, loss_target=loss_target, m_norm_w=m_norm_w, m_w_in=m_w_in, m_conv_w=m_conv_w, m_a_log=m_a_log, m_dt_bias=m_dt_bias, m_head_norm_w=m_head_norm_w, m_sgu_ln_w=m_sgu_ln_w, m_sgu_ln_b=m_sgu_ln_b, m_w_spatial=m_w_spatial, m_b_spatial=m_b_spatial, m_w_out=m_w_out, m_final_norm_w=m_final_norm_w, v_norm_w=v_norm_w, v_w_in=v_w_in, v_conv_w=v_conv_w, v_a_log=v_a_log, v_dt_bias=v_dt_bias, v_head_norm_w=v_head_norm_w, v_sgu_ln_w=v_sgu_ln_w, v_sgu_ln_b=v_sgu_ln_b, v_w_spatial=v_w_spatial, v_b_spatial=v_b_spatial, v_w_out=v_w_out, v_final_norm_w=v_final_norm_w)
    weights = {n: given[n] for n in TWIN_WEIGHTS}
    shared = {n: given[n] for n in SHARED_INPUTS}
    per_example = {n: given[n] for n in ['x']}
    grad_fn = _jax.value_and_grad(_loss, argnums=(0, 1))

    def one_microbatch(ex, loss_target):
        ex = dict(ex)
        diff = ex.pop(TWIN_DIFF_INPUT)
        return grad_fn(weights, diff, {**shared, **ex}, loss_target)

    if N_MICROBATCH == 1:
        loss, (grad_w, grad_x) = one_microbatch(per_example, given["loss_target"])
    else:
        def body(carry, xs):
            loss_sum, grad_sum = carry
            l_k, (gw_k, gx_k) = one_microbatch(xs[0], xs[1])
            with _jax.named_scope("update"):
                return (loss_sum + l_k, _jax.tree.map(_jnp.add, grad_sum, gw_k)), gx_k

        init = (_jnp.zeros((), _jnp.float32), _jax.tree.map(_jnp.zeros_like, weights))
        (loss, grad_w), grad_x = _jax.lax.scan(body, init, (per_example, given["loss_target"]))
    with _jax.named_scope("update"):
        delta_w, new_m, new_v = {}, {}, {}
        for n in TWIN_WEIGHTS:
            delta_w[n], new_m[n], new_v[n] = _adamw(weights[n], grad_w[n], given["m_" + n], given["v_" + n])
    return (loss, grad_x, *[grad_w[n] for n in TWIN_WEIGHTS], *[delta_w[n] for n in TWIN_WEIGHTS],
            *[new_m[n] for n in TWIN_WEIGHTS], *[new_v[n] for n in TWIN_WEIGHTS])
```

```python
import functools

import jax
import jax.numpy as jnp
from jax import lax
from jax.experimental import pallas as pl
from jax.experimental.pallas import tpu as pltpu

F32 = jnp.float32
BF16 = jnp.bfloat16
MXU = jnp.bfloat16
HI = lax.Precision.HIGHEST
EPS = 1e-6
CHUNK_A = 64
LANES = 128
MESH = pl.DeviceIdType.MESH
AXES = ("x", "y", "c")
N_DEV = 8

ADAM_LR = 0.001
ADAM_B1 = 0.9
ADAM_B2 = 0.999
ADAM_EPS = 1e-08
ADAM_WD = 0.01
ADAM_STEP = 10

S = jax.ShapeDtypeStruct
ARB = "arbitrary"


def _cp(*sem):
    return pltpu.CompilerParams(dimension_semantics=tuple(sem), vmem_limit_bytes=56 * 1024 * 1024)


def _tile(n, cap, mult):
    best = None
    t = mult
    while t <= min(n, cap):
        if n % t == 0:
            best = t
        t += mult
    return best if best is not None else n


def _mm(a, b):
    return jnp.dot(a.astype(MXU), b.astype(MXU), preferred_element_type=F32)


def _mm_nt(a, b):
    return lax.dot_general(a.astype(MXU), b.astype(MXU), (((1,), (1,)), ((), ())), preferred_element_type=F32)


def _mm_tn(a, b):
    return lax.dot_general(a.astype(MXU), b.astype(MXU), (((0,), (0,)), ((), ())), preferred_element_type=F32)


def _mmh(a, b):
    return jnp.dot(a, b, precision=HI, preferred_element_type=F32)


def _mmh_tn(a, b):
    return lax.dot_general(a, b, (((0,), (0,)), ((), ())), precision=HI, preferred_element_type=F32)


def _sigmoid(x):
    return 1.0 / (1.0 + jnp.exp(-x))


def _silu(x):
    return x * _sigmoid(x)


def _dsilu(x):
    s = _sigmoid(x)
    return s * (1.0 + x * (1.0 - s))


def _softplus(x):
    return jnp.maximum(x, 0.0) + jnp.log(1.0 + jnp.exp(-jnp.abs(x)))


def _in_proj(x, norm_w, w_main, w_ba):
    T, DM = x.shape
    NM = w_main.shape[1]
    tm = _tile(T, 512, 8)
    tn = _tile(NM, 1024, LANES)

    def body(x_ref, nw_ref, w_ref, wba_ref, proj_ref, ba_ref, xn_ref):
        @pl.when(pl.program_id(1) == 0)
        def _():
            xv = x_ref[...]
            r = lax.rsqrt(jnp.mean(xv * xv, axis=-1, keepdims=True) + EPS)
            xn = (xv * r * nw_ref[...]).astype(BF16)
            xn_ref[...] = xn
            ba_ref[...] = jnp.dot(xn.astype(MXU), wba_ref[...].astype(MXU), preferred_element_type=F32)

        proj_ref[...] = jnp.dot(xn_ref[...].astype(MXU), w_ref[...].astype(MXU), preferred_element_type=F32)

    return pl.pallas_call(
        body, name="in_proj", grid=(T // tm, NM // tn),
        in_specs=[pl.BlockSpec((tm, DM), lambda i, j: (i, 0)),
                  pl.BlockSpec((1, DM), lambda i, j: (0, 0)),
                  pl.BlockSpec((DM, tn), lambda i, j: (0, j)),
                  pl.BlockSpec((DM, LANES), lambda i, j: (0, 0))],
        out_specs=[pl.BlockSpec((tm, tn), lambda i, j: (i, j)),
                   pl.BlockSpec((tm, LANES), lambda i, j: (i, 0)),
                   pl.BlockSpec((tm, DM), lambda i, j: (i, 0))],
        out_shape=[S((T, NM), F32), S((T, LANES), F32), S((T, DM), BF16)],
        compiler_params=_cp(ARB, ARB),
    )(x, norm_w, w_main, w_ba)


def _prep_a_fwd(proj, ba, conv_w, alog_row, dtb_row, H, D):
    T = proj.shape[0]
    AW = H * D
    C3 = 3 * AW
    tb = _tile(T, 256, CHUNK_A)
    nch = tb // CHUNK_A
    nblk = T // tb
    scale = float(D) ** -0.5

    def body(x_ref, halo_ref, ba_ref, cw_ref, al_ref, dt_ref, q_ref, k_ref, v_ref, c_ref, gcol_ref, grow_ref):
        i = pl.program_id(0)
        xv = x_ref[...]
        halo = halo_ref[...] * (i > 0).astype(F32)
        xp = jnp.concatenate([halo, xv], axis=0)
        cw = cw_ref[...]
        c = cw[0:1, :] * xp[5:5 + tb]
        for j in range(1, 4):
            c = c + cw[j:j + 1, :] * xp[5 + j:5 + j + tb]
        c_ref[...] = c
        a = _silu(c)
        for h in range(H):
            qh = a[:, h * D:(h + 1) * D]
            kh = a[:, AW + h * D:AW + (h + 1) * D]
            qr = lax.rsqrt(jnp.sum(qh * qh, axis=-1, keepdims=True) + EPS)
            kr = lax.rsqrt(jnp.sum(kh * kh, axis=-1, keepdims=True) + EPS)
            q_ref[:, h * D:(h + 1) * D] = qh * (qr * scale)
            k_ref[:, h * D:(h + 1) * D] = kh * kr
        v_ref[...] = a[:, 2 * AW:]

        bav = ba_ref[...]
        lane = lax.broadcasted_iota(jnp.int32, (tb, LANES), 1)
        beta = _sigmoid(bav)
        g = -jnp.exp(al_ref[...]) * _softplus(bav + dt_ref[...])
        gates = jnp.where(lane < H, beta, jnp.where(lane < 2 * H, g, 0.0))
        ri = lax.broadcasted_iota(jnp.int32, (CHUNK_A, CHUNK_A), 0)
        ci = lax.broadcasted_iota(jnp.int32, (CHUNK_A, CHUNK_A), 1)
        tri = (ri >= ci).astype(F32)
        lane_c = lax.broadcasted_iota(jnp.int32, (CHUNK_A, LANES), 1)
        for cc in range(nch):
            gch = gates[cc * CHUNK_A:(cc + 1) * CHUNK_A]
            gc = pltpu.roll(_mmh(tri, gch), H, 1)
            full = jnp.where(lane_c < 2 * H, gch, jnp.where(lane_c < 3 * H, gc, 0.0))
            gcol_ref[cc * CHUNK_A:(cc + 1) * CHUNK_A, :] = full
            grow_ref[cc] = full.T[0:32, :]

    return pl.pallas_call(
        body, name="prep_a_fwd", grid=(nblk,),
        in_specs=[pl.BlockSpec((tb, C3), lambda i: (i, 0)),
                  pl.BlockSpec((8, C3), lambda i: (jnp.maximum(i * (tb // 8) - 1, 0), 0)),
                  pl.BlockSpec((tb, LANES), lambda i: (i, 0)),
                  pl.BlockSpec((4, C3), lambda i: (0, 0)),
                  pl.BlockSpec((1, LANES), lambda i: (0, 0)),
                  pl.BlockSpec((1, LANES), lambda i: (0, 0))],
        out_specs=[pl.BlockSpec((tb, AW), lambda i: (i, 0)),
                   pl.BlockSpec((tb, AW), lambda i: (i, 0)),
                   pl.BlockSpec((tb, AW), lambda i: (i, 0)),
                   pl.BlockSpec((tb, C3), lambda i: (i, 0)),
                   pl.BlockSpec((tb, LANES), lambda i: (i, 0)),
                   pl.BlockSpec((nch, 32, CHUNK_A), lambda i: (i, 0, 0))],
        out_shape=[S((T, AW), F32), S((T, AW), F32), S((T, AW), F32), S((T, C3), F32),
                   S((T, LANES), F32), S((T // CHUNK_A, 32, CHUNK_A), F32)],
        compiler_params=_cp(ARB),
    )(proj, proj, ba, conv_w, alog_row, dtb_row)


def _chunk_terms(q, k, v, gcolv, grow_ref, h, H):
    C = CHUNK_A
    lane = lax.broadcasted_iota(jnp.int32, (C, LANES), 1)

    def pick(idx):
        return jnp.sum(jnp.where(lane == idx, gcolv, 0.0), axis=1, keepdims=True)

    beta_c = pick(h)
    g_c = pick(H + h)
    gc_c = pick(2 * H + h)
    gc_r = grow_ref[0, pl.ds(2 * H + h, 1), :]
    ri = lax.broadcasted_iota(jnp.int32, (C, C), 0)
    ci = lax.broadcasted_iota(jnp.int32, (C, C), 1)
    incl = ri >= ci
    strict = ri > ci
    gam = jnp.where(incl, jnp.exp(jnp.where(incl, gc_c - gc_r, 0.0)), 0.0)
    e_c = jnp.exp(gc_c)
    kb = k * beta_c
    vb = v * beta_c
    lmat = jnp.where(strict, _mm_nt(kb, k) * gam, 0.0)
    attn = jnp.where(incl, _mm_nt(q, k) * gam, 0.0)
    gl = gc_r[:, C - 1:C]
    edec = jnp.exp(gl - gc_c)
    return dict(beta_c=beta_c, g_c=g_c, gc_c=gc_c, gc_r=gc_r, incl=incl, strict=strict, gam=gam, e_c=e_c,
                kb=kb, vb=vb, lmat=lmat, attn=attn, gl=gl, edec=edec, ri=ri, ci=ci)


def _inv_unit_lower(lmat):
    C = lmat.shape[0]
    ri = lax.broadcasted_iota(jnp.int32, (C, C), 0)
    ci = lax.broadcasted_iota(jnp.int32, (C, C), 1)
    eye = (ri == ci).astype(F32)
    x = -lmat
    a = eye + x
    n = 1
    while 2 * n < C:
        x = _mmh(x, x)
        a = a + _mmh(a, x)
        n *= 2
    return a


def _delta_fwd(q, k, v, gcol, grow, H, D):
    T = q.shape[0]
    C = CHUNK_A
    N = T // C

    def body(q_ref, k_ref, v_ref, gcol_ref, grow_ref, o_ref, ssave_ref, asave_ref, s_ref):
        n = pl.program_id(0)
        h = pl.program_id(1)

        @pl.when(n == 0)
        def _():
            s_ref[h] = jnp.zeros((D, D), F32)

        st = s_ref[h]
        ssave_ref[0, 0] = st
        qv, kv, vv = q_ref[...], k_ref[...], v_ref[...]
        t = _chunk_terms(qv, kv, vv, gcol_ref[...], grow_ref, h, H)
        a = _inv_unit_lower(t["lmat"])
        asave_ref[0, 0] = a
        u = _mmh(a, t["vb"])
        w = _mmh(a, t["kb"] * t["e_c"])
        v_new = u - _mm(w, st)
        o_ref[...] = _mm(qv * t["e_c"], st) + _mm(t["attn"], v_new)
        s_ref[h] = st * jnp.exp(t["gl"]) + _mm_tn(kv * t["edec"], v_new)

    blk = lambda: pl.BlockSpec((C, D), lambda n, h: (n, h))
    return pl.pallas_call(
        body, name="delta_fwd", grid=(N, H),
        in_specs=[blk(), blk(), blk(),
                  pl.BlockSpec((C, LANES), lambda n, h: (n, 0)),
                  pl.BlockSpec((1, 32, C), lambda n, h: (n, 0, 0))],
        out_specs=[blk(),
                   pl.BlockSpec((1, 1, D, D), lambda n, h: (n, h, 0, 0)),
                   pl.BlockSpec((1, 1, C, C), lambda n, h: (n, h, 0, 0))],
        out_shape=[S((T, H * D), F32), S((N, H, D, D), F32), S((N, H, C, C), F32)],
        scratch_shapes=[pltpu.VMEM((H, D, D), F32)],
        compiler_params=_cp(ARB, ARB),
    )(q, k, v, gcol, grow)


def _delta_bwd(q, k, v, gcol, grow, ba, ssave, asave, d_o, a_log, dt_bias, H, D):
    T = q.shape[0]
    C = CHUNK_A
    N = T // C

    def body(al_ref, dt_ref, q_ref, k_ref, v_ref, gcol_ref, grow_ref, ba_ref, ss_ref, as_ref, do_ref,
             dq_ref, dk_ref, dv_ref, dgate_ref, dpar_ref, ds_ref):
        step = pl.program_id(0)
        h = pl.program_id(1)

        @pl.when(step == 0)
        def _():
            ds_ref[h] = jnp.zeros((D, D), F32)

        @pl.when((step == 0) & (h == 0))
        def _():
            dpar_ref[...] = jnp.zeros_like(dpar_ref)

        @pl.when(h == 0)
        def _():
            dgate_ref[...] = jnp.zeros_like(dgate_ref)

        ds_next = ds_ref[h]
        st = ss_ref[0, 0]
        a = as_ref[0, 0]
        qv, kv, vv, dov = q_ref[...], k_ref[...], v_ref[...], do_ref[...]
        gcolv = gcol_ref[...]
        t = _chunk_terms(qv, kv, vv, gcolv, grow_ref, h, H)
        beta_c, e_c, gam, kb, vb = t["beta_c"], t["e_c"], t["gam"], t["kb"], t["vb"]
        incl, strict, attn, lmat, edec = t["incl"], t["strict"], t["attn"], t["lmat"], t["edec"]
        u = _mmh(a, vb)
        w = _mmh(a, kb * e_c)
        v_new = u - _mm(w, st)
        kdec = kv * edec
        egl = jnp.exp(t["gl"])

        ds_cur = egl * ds_next
        dgl = egl * jnp.sum(jnp.sum(st * ds_next, axis=1, keepdims=True), axis=0, keepdims=True)
        dkdec = _mm_nt(v_new, ds_next)
        dv_new = _mm(kdec, ds_next)
        dk = edec * dkdec
        r = jnp.sum(dkdec * kdec, axis=1, keepdims=True)
        dgc = -r
        dgl = dgl + jnp.sum(r, axis=0, keepdims=True)

        qe = qv * e_c
        t1 = _mm_nt(dov, st)
        dq = e_c * t1
        dgc = dgc + jnp.sum(t1 * qe, axis=1, keepdims=True)
        ds_cur = ds_cur + _mm_tn(qe, dov)
        dattn = jnp.where(incl, _mm_nt(dov, v_new), 0.0)
        dv_new = dv_new + _mm_tn(attn, dov)
        dqm = dattn * gam
        dq = dq + _mm(dqm, kv)
        dk = dk + _mm_tn(dqm, qv)
        z = dattn * attn

        dw = -_mm_nt(dv_new, st)
        ds_cur = ds_cur - _mm_tn(w, dv_new)

        dvb = _mmh_tn(a, dv_new)
        dekb = _mmh_tn(a, dw)
        dl = jnp.where(strict, -(_mm_nt(dvb, u) + _mm_nt(dekb, w)), 0.0)
        dp = dl * gam
        z = z + dl * lmat
        dkb = _mm(dp, kv) + e_c * dekb
        dk = dk + _mm_tn(dp, kb)
        dgc = dgc + jnp.sum(dekb * (e_c * kb), axis=1, keepdims=True)
        dk = dk + beta_c * dkb
        dbeta = jnp.sum(dkb * kv, axis=1, keepdims=True) + jnp.sum(dvb * vv, axis=1, keepdims=True)
        dv = beta_c * dvb

        dgc = dgc + jnp.sum(z, axis=1, keepdims=True) - jnp.sum(z.T, axis=1, keepdims=True)
        rowi = lax.broadcasted_iota(jnp.int32, (C, 1), 0)
        dgc = dgc + jnp.where(rowi == C - 1, dgl, 0.0)
        upper = (t["ri"] <= t["ci"]).astype(F32)
        dg = _mmh(upper, jnp.broadcast_to(dgc, (C, LANES)))[:, 0:1]

        ds_ref[h] = ds_cur
        dq_ref[...] = dq
        dk_ref[...] = dk
        dv_ref[...] = dv

        lane = lax.broadcasted_iota(jnp.int32, (C, LANES), 1)
        a_raw = jnp.sum(jnp.where(lane == H + h, ba_ref[...], 0.0), axis=1, keepdims=True)
        nexp = -jnp.exp(al_ref[0, h])
        d_braw = dbeta * beta_c * (1.0 - beta_c)
        d_araw = dg * nexp * _sigmoid(a_raw + dt_ref[0, h])
        dgate_ref[...] += jnp.where(lane == h, d_braw, 0.0) + jnp.where(lane == H + h, d_araw, 0.0)
        lane1 = lax.broadcasted_iota(jnp.int32, (1, LANES), 1)
        dal = jnp.sum(dg * t["g_c"], axis=0, keepdims=True)
        ddt = jnp.sum(d_araw, axis=0, keepdims=True)
        dpar_ref[0:1, :] += jnp.where(lane1 == h, dal, 0.0) + jnp.where(lane1 == H + h, ddt, 0.0)

    rev = lambda s: N - 1 - s
    blk = lambda: pl.BlockSpec((C, D), lambda s, h: (rev(s), h))
    smem = pl.BlockSpec(memory_space=pltpu.SMEM)
    return pl.pallas_call(
        body, name="delta_bwd", grid=(N, H),
        in_specs=[smem, smem, blk(), blk(), blk(),
                  pl.BlockSpec((C, LANES), lambda s, h: (rev(s), 0)),
                  pl.BlockSpec((1, 32, C), lambda s, h: (rev(s), 0, 0)),
                  pl.BlockSpec((C, LANES), lambda s, h: (rev(s), 0)),
                  pl.BlockSpec((1, 1, D, D), lambda s, h: (rev(s), h, 0, 0)),
                  pl.BlockSpec((1, 1, C, C), lambda s, h: (rev(s), h, 0, 0)),
                  blk()],
        out_specs=[blk(), blk(), blk(),
                   pl.BlockSpec((C, LANES), lambda s, h: (rev(s), 0)),
                   pl.BlockSpec((8, LANES), lambda s, h: (0, 0))],
        out_shape=[S((T, H * D), F32), S((T, H * D), F32), S((T, H * D), F32),
                   S((T, LANES), F32), S((8, LANES), F32)],
        scratch_shapes=[pltpu.VMEM((H, D, D), F32)],
        compiler_params=_cp(ARB, ARB),
    )(a_log, dt_bias, q, k, v, gcol, grow, ba, ssave, asave, d_o)


def _ln_stats(xv):
    mu = jnp.mean(xv, axis=-1, keepdims=True)
    xc = xv - mu
    var = jnp.mean(xc * xc, axis=-1, keepdims=True)
    rstd = lax.rsqrt(var + EPS)
    return xc * rstd, rstd


def _mix_fwd(o, proj, head_norm_w, ln_w, ln_b, w_sp, bs_t, H, D, G, P):
    T = o.shape[0]
    AW, BW = H * D, G * P
    MIX = AW + BW
    nb = AW // BW if AW % BW == 0 else None
    assert nb == 1, "group widths must match the projection column blocks"
    cb = 3

    def body(o_ref, za_ref, ub_ref, vb_ref, zb_ref, hw_ref, lw_ref, lb_ref, w_ref, bs_ref, out_ref):
        hw = hw_ref[...]
        for h in range(H):
            sl = slice(h * D, (h + 1) * D)
            oh = o_ref[:, sl]
            rs = lax.rsqrt(jnp.mean(oh * oh, axis=-1, keepdims=True) + EPS)
            out_ref[:, sl] = (oh * rs * hw * _silu(za_ref[:, sl])).astype(BF16)
        xhat, _ = _ln_stats(vb_ref[...])
        vn = xhat * lw_ref[...] + lb_ref[...]
        ri = lax.broadcasted_iota(jnp.int32, (P, P), 0)
        ci = lax.broadcasted_iota(jnp.int32, (P, P), 1)
        bsv = bs_ref[...]
        for g in range(G):
            sl = slice(g * P, (g + 1) * P)
            wm = jnp.where(ri >= ci, w_ref[g], 0.0)
            s = _mm(wm, vn[:, sl]) + bsv[:, g:g + 1]
            out_ref[:, AW + g * P:AW + (g + 1) * P] = (ub_ref[:, sl] * s * _silu(zb_ref[:, sl])).astype(BF16)

    row = lambda w: pl.BlockSpec((1, w), lambda i: (0, 0))
    return pl.pallas_call(
        body, name="mix_fwd", grid=(T // P,),
        in_specs=[pl.BlockSpec((P, AW), lambda i: (i, 0)),
                  pl.BlockSpec((P, AW), lambda i: (i, cb)),
                  pl.BlockSpec((P, BW), lambda i: (i, cb + 1)),
                  pl.BlockSpec((P, BW), lambda i: (i, cb + 2)),
                  pl.BlockSpec((P, BW), lambda i: (i, cb + 3)),
                  row(D), row(BW), row(BW),
                  pl.BlockSpec((G, P, P), lambda i: (0, 0, 0)),
                  pl.BlockSpec((P, G), lambda i: (0, 0))],
        out_specs=pl.BlockSpec((P, MIX), lambda i: (i, 0)),
        out_shape=S((T, MIX), BF16),
        compiler_params=_cp(ARB),
    )(o, proj, proj, proj, proj, head_norm_w, ln_w, ln_b, w_sp, bs_t)


def _mix_bwd(d_ocat, o, proj, head_norm_w, ln_w, ln_b, w_sp, bs_t, H, D, G, P):
    T = o.shape[0]
    AW, BW = H * D, G * P
    MIX = AW + BW
    cb = 3

    def body(dc_ref, o_ref, za_ref, ub_ref, vb_ref, zb_ref, hw_ref, lw_ref, lb_ref, w_ref, bs_ref,
             do_ref, drest_ref, dhw_ref, dln_ref, dw_ref, dbs_ref, dvn_ref):
        @pl.when(pl.program_id(0) == 0)
        def _():
            dhw_ref[...] = jnp.zeros_like(dhw_ref)
            dln_ref[...] = jnp.zeros_like(dln_ref)
            dw_ref[...] = jnp.zeros_like(dw_ref)
            dbs_ref[...] = jnp.zeros_like(dbs_ref)

        hw = hw_ref[...]
        dhw = jnp.zeros((1, D), F32)
        for h in range(H):
            sl = slice(h * D, (h + 1) * D)
            oh = o_ref[:, sl]
            za = za_ref[:, sl]
            doa = dc_ref[:, sl]
            rs = lax.rsqrt(jnp.mean(oh * oh, axis=-1, keepdims=True) + EPS)
            xh = oh * rs
            d_on = doa * _silu(za)
            drest_ref[:, sl] = (doa * (xh * hw) * _dsilu(za)).astype(BF16)
            dhw = dhw + jnp.sum(d_on * xh, axis=0, keepdims=True)
            dxh = d_on * hw
            do_ref[:, sl] = rs * (dxh - xh * jnp.mean(dxh * xh, axis=-1, keepdims=True))
        dhw_ref[0:1, :] += dhw

        xhat, rstd = _ln_stats(vb_ref[...])
        lw = lw_ref[...]
        vn = xhat * lw + lb_ref[...]
        ri = lax.broadcasted_iota(jnp.int32, (P, P), 0)
        ci = lax.broadcasted_iota(jnp.int32, (P, P), 1)
        lane = lax.broadcasted_iota(jnp.int32, (P, LANES), 1)
        bsv = bs_ref[...]
        dbs = jnp.zeros((P, LANES), F32)
        for g in range(G):
            sl = slice(g * P, (g + 1) * P)
            wm = jnp.where(ri >= ci, w_ref[g], 0.0)
            vng = vn[:, sl]
            s = _mm(wm, vng) + bsv[:, g:g + 1]
            dob = dc_ref[:, AW + g * P:AW + (g + 1) * P]
            ub = ub_ref[:, sl]
            zb = zb_ref[:, sl]
            szb = _silu(zb)
            drest_ref[:, AW + g * P:AW + (g + 1) * P] = (dob * s * szb).astype(BF16)
            drest_ref[:, AW + 2 * BW + g * P:AW + 2 * BW + (g + 1) * P] = (dob * ub * s * _dsilu(zb)).astype(BF16)
            ds = dob * ub * szb
            dvn_ref[:, sl] = _mm_tn(wm, ds)
            dw_ref[g] += jnp.where(ri >= ci, _mm_nt(ds, vng), 0.0)
            dbs = dbs + jnp.where(lane == g, jnp.sum(ds, axis=1, keepdims=True), 0.0)
        dbs_ref[...] += dbs
        dvn = dvn_ref[...]
        dln_ref[0:1, :] += jnp.sum(dvn * xhat, axis=0, keepdims=True)
        dln_ref[1:2, :] += jnp.sum(dvn, axis=0, keepdims=True)
        dxh = dvn * lw
        dvb = rstd * (dxh - jnp.mean(dxh, axis=-1, keepdims=True) - xhat * jnp.mean(dxh * xhat, axis=-1, keepdims=True))
        drest_ref[:, AW + BW:AW + 2 * BW] = dvb.astype(BF16)

    row = lambda w: pl.BlockSpec((1, w), lambda i: (0, 0))
    return pl.pallas_call(
        body, name="mix_bwd", grid=(T // P,),
        in_specs=[pl.BlockSpec((P, MIX), lambda i: (i, 0)),
                  pl.BlockSpec((P, AW), lambda i: (i, 0)),
                  pl.BlockSpec((P, AW), lambda i: (i, cb)),
                  pl.BlockSpec((P, BW), lambda i: (i, cb + 1)),
                  pl.BlockSpec((P, BW), lambda i: (i, cb + 2)),
                  pl.BlockSpec((P, BW), lambda i: (i, cb + 3)),
                  row(D), row(BW), row(BW),
                  pl.BlockSpec((G, P, P), lambda i: (0, 0, 0)),
                  pl.BlockSpec((P, G), lambda i: (0, 0))],
        out_specs=[pl.BlockSpec((P, AW), lambda i: (i, 0)),
                   pl.BlockSpec((P, AW + 3 * BW), lambda i: (i, 0)),
                   pl.BlockSpec((8, D), lambda i: (0, 0)),
                   pl.BlockSpec((8, BW), lambda i: (0, 0)),
                   pl.BlockSpec((G, P, P), lambda i: (0, 0, 0)),
                   pl.BlockSpec((P, LANES), lambda i: (0, 0))],
        out_shape=[S((T, AW), F32), S((T, AW + 3 * BW), BF16), S((8, D), F32), S((8, BW), F32),
                   S((G, P, P), F32), S((P, LANES), F32)],
        scratch_shapes=[pltpu.VMEM((P, BW), F32)],
        compiler_params=_cp(ARB),
    )(d_ocat, o, proj, proj, proj, proj, head_norm_w, ln_w, ln_b, w_sp, bs_t)


def _out_proj_loss(ocat, w_out, x, target, fnw):
    T, MIX = ocat.shape
    DM = x.shape[1]
    tm = _tile(T, 256, 8)

    def body(oc_ref, w_ref, x_ref, t_ref, fw_ref, dh_ref, doc_ref, loss_ref, gfw_ref):
        @pl.when(pl.program_id(0) == 0)
        def _():
            loss_ref[...] = jnp.zeros_like(loss_ref)
            gfw_ref[...] = jnp.zeros_like(gfw_ref)

        wv = w_ref[...]
        hh = x_ref[...] + jnp.dot(oc_ref[...].astype(MXU), wv.astype(MXU), preferred_element_type=F32)
        rs = lax.rsqrt(jnp.mean(hh * hh, axis=-1, keepdims=True) + EPS)
        hn = hh * rs
        fw = fw_ref[...]
        e = hn * fw - t_ref[...]
        row_loss = 0.5 * jnp.mean(e * e, axis=-1, keepdims=True)
        loss_ref[...] += jnp.sum(row_loss, axis=0, keepdims=True)
        dy = e * (1.0 / DM)
        gfw_ref[0:1, :] += jnp.sum(dy * hn, axis=0, keepdims=True)
        dhn = dy * fw
        dh = rs * (dhn - hn * jnp.mean(dhn * hn, axis=-1, keepdims=True))
        dh_ref[...] = dh
        doc_ref[...] = _mm_nt(dh, wv)

    return pl.pallas_call(
        body, name="out_proj_loss", grid=(T // tm,),
        in_specs=[pl.BlockSpec((tm, MIX), lambda i: (i, 0)),
                  pl.BlockSpec((MIX, DM), lambda i: (0, 0)),
                  pl.BlockSpec((tm, DM), lambda i: (i, 0)),
                  pl.BlockSpec((tm, DM), lambda i: (i, 0)),
                  pl.BlockSpec((1, DM), lambda i: (0, 0))],
        out_specs=[pl.BlockSpec((tm, DM), lambda i: (i, 0)),
                   pl.BlockSpec((tm, MIX), lambda i: (i, 0)),
                   pl.BlockSpec((8, LANES), lambda i: (0, 0)),
                   pl.BlockSpec((8, DM), lambda i: (0, 0))],
        out_shape=[S((T, DM), F32), S((T, MIX), F32), S((8, LANES), F32), S((8, DM), F32)],
        compiler_params=_cp(ARB),
    )(ocat, w_out, x, target, fnw)


def _grad_w(lhs, rhs, name):
    T, A = lhs.shape
    B = rhs.shape[1]
    ta = _tile(A, 512, LANES)
    tk = _tile(T, 512, 8)

    def body(l_ref, r_ref, out_ref):
        @pl.when(pl.program_id(1) == 0)
        def _():
            out_ref[...] = jnp.zeros_like(out_ref)
        out_ref[...] += _mm_tn(l_ref[...], r_ref[...])

    return pl.pallas_call(
        body, name=name, grid=(A // ta, T // tk),
        in_specs=[pl.BlockSpec((tk, ta), lambda i, k: (k, i)),
                  pl.BlockSpec((tk, B), lambda i, k: (k, 0))],
        out_specs=pl.BlockSpec((ta, B), lambda i, k: (i, 0)),
        out_shape=S((A, B), F32),
        compiler_params=_cp(ARB, ARB),
    )(lhs, rhs)


def _grad_w_in(xn, dmain, dba):
    T, DM = xn.shape
    NM = dmain.shape[1]
    tn = _tile(NM, 1024, LANES)
    tk = _tile(T, 512, 8)

    def body(xn_ref, dm_ref, dba_ref, gm_ref, gba_ref):
        j = pl.program_id(0)
        k = pl.program_id(1)

        @pl.when(k == 0)
        def _():
            gm_ref[...] = jnp.zeros_like(gm_ref)

        @pl.when((k == 0) & (j == 0))
        def _():
            gba_ref[...] = jnp.zeros_like(gba_ref)

        xv = xn_ref[...]
        gm_ref[...] += _mm_tn(xv, dm_ref[...])

        @pl.when(j == 0)
        def _():
            gba_ref[...] += _mm_tn(xv, dba_ref[...])

    return pl.pallas_call(
        body, name="grad_w_in", grid=(NM // tn, T // tk),
        in_specs=[pl.BlockSpec((tk, DM), lambda j, k: (k, 0)),
                  pl.BlockSpec((tk, tn), lambda j, k: (k, j)),
                  pl.BlockSpec((tk, LANES), lambda j, k: (k, 0))],
        out_specs=[pl.BlockSpec((DM, tn), lambda j, k: (0, j)),
                   pl.BlockSpec((DM, LANES), lambda j, k: (0, 0))],
        out_shape=[S((DM, NM), F32), S((DM, LANES), F32)],
        compiler_params=_cp(ARB, ARB),
    )(xn, dmain, dba)


def _dx(dmain, dba, w_main, w_ba, x, dh, norm_w):
    T, NM = dmain.shape
    DM = x.shape[1]
    tm = _tile(T, 512, 8)
    tk = _tile(NM, 1024, LANES)
    nk = NM // tk

    def body(dm_ref, dba_ref, w_ref, wba_ref, x_ref, dh_ref, nw_ref, gx_ref, gnw_ref, acc_ref):
        i = pl.program_id(0)
        k = pl.program_id(1)

        @pl.when((i == 0) & (k == 0))
        def _():
            gnw_ref[...] = jnp.zeros_like(gnw_ref)

        @pl.when(k == 0)
        def _():
            acc_ref[...] = _mm_nt(dba_ref[...], wba_ref[...])

        acc_ref[...] += _mm_nt(dm_ref[...], w_ref[...])

        @pl.when(k == nk - 1)
        def _():
            xv = x_ref[...]
            rs = lax.rsqrt(jnp.mean(xv * xv, axis=-1, keepdims=True) + EPS)
            xh = xv * rs
            dxn = acc_ref[...]
            gnw_ref[0:1, :] += jnp.sum(dxn * xh, axis=0, keepdims=True)
            dxh = dxn * nw_ref[...]
            gx_ref[...] = dh_ref[...] + rs * (dxh - xh * jnp.mean(dxh * xh, axis=-1, keepdims=True))

    return pl.pallas_call(
        body, name="dx", grid=(T // tm, nk),
        in_specs=[pl.BlockSpec((tm, tk), lambda i, k: (i, k)),
                  pl.BlockSpec((tm, LANES), lambda i, k: (i, 0)),
                  pl.BlockSpec((DM, tk), lambda i, k: (0, k)),
                  pl.BlockSpec((DM, LANES), lambda i, k: (0, 0)),
                  pl.BlockSpec((tm, DM), lambda i, k: (i, 0)),
                  pl.BlockSpec((tm, DM), lambda i, k: (i, 0)),
                  pl.BlockSpec((1, DM), lambda i, k: (0, 0))],
        out_specs=[pl.BlockSpec((tm, DM), lambda i, k: (i, 0)),
                   pl.BlockSpec((8, DM), lambda i, k: (0, 0))],
        out_shape=[S((T, DM), F32), S((8, DM), F32)],
        scratch_shapes=[pltpu.VMEM((tm, DM), F32)],
        compiler_params=_cp(ARB, ARB),
    )(dmain, dba, w_main, w_ba, x, dh, norm_w)


def _prep_a_bwd_pointwise(dq, dk, dv, c, H, D):
    T = c.shape[0]
    AW = H * D
    C3 = 3 * AW
    tb = _tile(T, 256, 8)
    scale = float(D) ** -0.5

    def body(dq_ref, dk_ref, dv_ref, c_ref, dc_ref):
        for h in range(H):
            for part, d_ref, sc in ((0, dq_ref, scale), (1, dk_ref, 1.0)):
                sl = slice(part * AW + h * D, part * AW + (h + 1) * D)
                cv = c_ref[:, sl]
                raw = _silu(cv)
                rs = lax.rsqrt(jnp.sum(raw * raw, axis=-1, keepdims=True) + EPS)
                nrm = raw * rs
                dn = d_ref[:, h * D:(h + 1) * D] * sc
                draw = rs * (dn - nrm * jnp.sum(dn * nrm, axis=-1, keepdims=True))
                dc_ref[:, sl] = draw * _dsilu(cv)
        dc_ref[:, 2 * AW:] = dv_ref[...] * _dsilu(c_ref[:, 2 * AW:])

    return pl.pallas_call(
        body, name="prep_a_bwd_pointwise", grid=(T // tb,),
        in_specs=[pl.BlockSpec((tb, AW), lambda i: (i, 0)),
                  pl.BlockSpec((tb, AW), lambda i: (i, 0)),
                  pl.BlockSpec((tb, AW), lambda i: (i, 0)),
                  pl.BlockSpec((tb, C3), lambda i: (i, 0))],
        out_specs=pl.BlockSpec((tb, C3), lambda i: (i, 0)),
        out_shape=S((T, C3), F32),
        compiler_params=_cp(ARB),
    )(dq, dk, dv, c)


def _conv_bwd(dc, proj, conv_w, C3):
    T = dc.shape[0]
    tb = _tile(T, 256, 8)
    nblk = T // tb
    r8 = tb // 8

    def body(dc_ref, dnext_ref, x_ref, halo_ref, cw_ref, dx_ref, gcw_ref):
        i = pl.program_id(0)

        @pl.when(i == 0)
        def _():
            gcw_ref[...] = jnp.zeros_like(gcw_ref)

        dcv = dc_ref[...]
        dnext = dnext_ref[...] * (i < nblk - 1).astype(F32)
        dcp = jnp.concatenate([dcv, dnext], axis=0)
        cw = cw_ref[...]
        dx = cw[3:4, :] * dcv
        for j in range(3):
            dx = dx + cw[j:j + 1, :] * dcp[3 - j:3 - j + tb]
        dx_ref[...] = dx.astype(BF16)
        halo = halo_ref[...] * (i > 0).astype(F32)
        xp = jnp.concatenate([halo, x_ref[...]], axis=0)
        for j in range(4):
            gcw_ref[j:j + 1, :] += jnp.sum(dcv * xp[5 + j:5 + j + tb], axis=0, keepdims=True)

    return pl.pallas_call(
        body, name="conv_bwd", grid=(nblk,),
        in_specs=[pl.BlockSpec((tb, C3), lambda i: (i, 0)),
                  pl.BlockSpec((8, C3), lambda i: (jnp.minimum((i + 1) * r8, T // 8 - 1), 0)),
                  pl.BlockSpec((tb, C3), lambda i: (i, 0)),
                  pl.BlockSpec((8, C3), lambda i: (jnp.maximum(i * r8 - 1, 0), 0)),
                  pl.BlockSpec((4, C3), lambda i: (0, 0))],
        out_specs=[pl.BlockSpec((tb, C3), lambda i: (i, 0)),
                   pl.BlockSpec((8, C3), lambda i: (0, 0))],
        out_shape=[S((T, C3), BF16), S((8, C3), F32)],
        compiler_params=_cp(ARB),
    )(dc, dc, proj, proj, conv_w)


def _adam_math(w, g, m, v):
    m2 = ADAM_B1 * m + (1.0 - ADAM_B1) * g
    v2 = ADAM_B2 * v + (1.0 - ADAM_B2) * (g * g)
    m_hat = m2 / (1.0 - ADAM_B1 ** ADAM_STEP)
    v_hat = v2 / (1.0 - ADAM_B2 ** ADAM_STEP)
    delta = -ADAM_LR * (m_hat / (jnp.sqrt(v_hat) + ADAM_EPS) + ADAM_WD * w)
    return delta, m2, v2


def _pair_sum(a, b, name):
    K, R, C = a.shape
    tr = _tile(R, 256, 16)

    def body(a_ref, b_ref, o_ref):
        o_ref[...] = (a_ref[...].astype(F32) + b_ref[...].astype(F32)).astype(BF16)

    spec = lambda: pl.BlockSpec((1, tr, C), lambda k, i: (k, i, 0))
    return pl.pallas_call(
        body, name=name, grid=(K, R // tr), in_specs=[spec(), spec()], out_specs=spec(),
        out_shape=S((K, R, C), BF16), compiler_params=_cp(ARB, ARB),
    )(a, b)


def _sum_adam(own, recv, w, m, v, name):
    R, C = w.shape
    tr = _tile(R, 256, 16)

    def body(own_ref, r_ref, w_ref, m_ref, v_ref, g_ref, d_ref, m2_ref, v2_ref):
        g = own_ref[...].astype(F32)
        for j in range(3):
            g = g + r_ref[j].astype(F32)
        g_ref[...] = g
        d_ref[...], m2_ref[...], v2_ref[...] = _adam_math(w_ref[...], g, m_ref[...], v_ref[...])

    spec = lambda: pl.BlockSpec((tr, C), lambda i: (i, 0))
    return pl.pallas_call(
        body, name=name, grid=(R // tr,),
        in_specs=[spec(), pl.BlockSpec((3, tr, C), lambda i: (0, i, 0)), spec(), spec(), spec()],
        out_specs=[spec(), spec(), spec(), spec()],
        out_shape=[S((R, C), F32)] * 4, compiler_params=_cp(ARB),
    )(own, recv, w, m, v)


def _adam_small(w, g, m, v):
    R, C = w.shape
    tr = _tile(R, 512, 8)

    def body(w_ref, g_ref, m_ref, v_ref, d_ref, m2_ref, v2_ref):
        d_ref[...], m2_ref[...], v2_ref[...] = _adam_math(w_ref[...], g_ref[...], m_ref[...], v_ref[...])

    spec = lambda: pl.BlockSpec((tr, C), lambda i: (i, 0))
    return pl.pallas_call(
        body, name="adam_small", grid=(R // tr,), in_specs=[spec()] * 4, out_specs=[spec()] * 3,
        out_shape=[S((R, C), F32)] * 3, compiler_params=_cp(ARB),
    )(w, g, m, v)


def _position():
    return lax.axis_index("x"), lax.axis_index("y"), lax.axis_index("c")


def _all_gather_weights(arrs):
    na = len(arrs)

    def body(*refs):
        ins, outs = refs[:na], refs[na:2 * na]
        send_sems, recv_sems, local_sems = refs[2 * na:]
        x, y, c = _position()
        me, sibling = (x, y, c), (x, y, 1 - c)
        chips = [(1 - x, y), (x, 1 - y), (1 - x, 1 - y)]

        def slot(a, p):
            return outs[a].at[4 * p[0] + 2 * p[1] + p[2]]

        def copy(a, kk, block, to, src=None):
            return pltpu.make_async_remote_copy(
                src_ref=slot(a, block) if src is None else src, dst_ref=slot(a, block),
                send_sem=send_sems.at[a, kk], recv_sem=recv_sems.at[a, kk], device_id=to, device_id_type=MESH)

        mine = [pltpu.make_async_copy(ins[a], slot(a, me), local_sems.at[a]) for a in range(na)]
        for cp in mine:
            cp.start()
        first = []
        for a in range(na):
            first.append(copy(a, 0, me, sibling, src=ins[a]))
            first += [copy(a, 1 + j, me, (*chip, c), src=ins[a]) for j, chip in enumerate(chips)]
        for cp in first:
            cp.start()
        passed = []
        for j, chip in enumerate(chips):
            for a in range(na):
                copy(a, 1 + j, (*chip, c), me).wait_recv()
                fw = copy(a, 4 + j, (*chip, c), sibling)
                fw.start()
                passed.append(fw)
        for a in range(na):
            copy(a, 0, sibling, me).wait_recv()
            for j, chip in enumerate(chips):
                copy(a, 4 + j, (*chip, 1 - c), me).wait_recv()
        for cp in first + passed:
            cp.wait_send()
        for cp in mine:
            cp.wait()

    any_spec = pl.BlockSpec(memory_space=pl.ANY)
    return pl.pallas_call(
        body, name="all_gather_weights",
        in_specs=[any_spec] * na, out_specs=[any_spec] * na,
        out_shape=[S((N_DEV,) + a.shape, a.dtype) for a in arrs],
        scratch_shapes=[pltpu.SemaphoreType.DMA((na, 7)), pltpu.SemaphoreType.DMA((na, 7)),
                        pltpu.SemaphoreType.DMA((na,))],
    )(*arrs)


def _exchange_sibling(arrs):
    na = len(arrs)

    def body(*refs):
        ins, outs = refs[:na], refs[na:2 * na]
        send_sems, recv_sems = refs[2 * na:]
        x, y, c = _position()
        cps = [pltpu.make_async_remote_copy(src_ref=ins[a], dst_ref=outs[a], send_sem=send_sems.at[a],
                                            recv_sem=recv_sems.at[a], device_id=(x, y, 1 - c), device_id_type=MESH)
               for a in range(na)]
        for cp in cps:
            cp.start()
        for cp in cps:
            cp.wait()

    any_spec = pl.BlockSpec(memory_space=pl.ANY)
    return pl.pallas_call(
        body, name="exchange_sibling", in_specs=[any_spec] * na, out_specs=[any_spec] * na,
        out_shape=[S(a.shape, a.dtype) for a in arrs],
        scratch_shapes=[pltpu.SemaphoreType.DMA((na,)), pltpu.SemaphoreType.DMA((na,))],
    )(*arrs)


def _exchange_chips(arrs):
    na = len(arrs)

    def body(*refs):
        ins, outs = refs[:na], refs[na:2 * na]
        send_sems, recv_sems = refs[2 * na:]
        x, y, c = _position()
        chips = [(1 - x, y), (x, 1 - y), (1 - x, 1 - y)]
        cps = []
        for a in range(na):
            for j, (px, py) in enumerate(chips):
                cps.append(pltpu.make_async_remote_copy(
                    src_ref=ins[a].at[2 * px + py], dst_ref=outs[a].at[j], send_sem=send_sems.at[a, j],
                    recv_sem=recv_sems.at[a, j], device_id=(px, py, c), device_id_type=MESH))
        for cp in cps:
            cp.start()
        for cp in cps:
            cp.wait()

    any_spec = pl.BlockSpec(memory_space=pl.ANY)
    return pl.pallas_call(
        body, name="exchange_chips", in_specs=[any_spec] * na, out_specs=[any_spec] * na,
        out_shape=[S((3,) + a.shape[1:], a.dtype) for a in arrs],
        scratch_shapes=[pltpu.SemaphoreType.DMA((na, 3)), pltpu.SemaphoreType.DMA((na, 3))],
    )(*arrs)


def _all_reduce_small(part):
    R, C = part.shape

    def body(p_ref, out_ref, gath_ref, send_sems, recv_sems):
        x, y, c = _position()
        me = 4 * x + 2 * y + c
        gath_ref[me] = p_ref[...]
        cps = []
        for k in range(1, N_DEV):
            px = 1 - x if k & 4 else x
            py = 1 - y if k & 2 else y
            pc = 1 - c if k & 1 else c
            cps.append(pltpu.make_async_remote_copy(
                src_ref=p_ref, dst_ref=gath_ref.at[me], send_sem=send_sems.at[k - 1], recv_sem=recv_sems.at[k - 1],
                device_id=(px, py, pc), device_id_type=MESH))
        for cp in cps:
            cp.start()
        for cp in cps:
            cp.wait()
        acc = gath_ref[0]
        for d in range(1, N_DEV):
            acc = acc + gath_ref[d]
        out_ref[...] = acc

    vm = pl.BlockSpec(memory_space=pltpu.VMEM)
    return pl.pallas_call(
        body, name="all_reduce_small", in_specs=[vm], out_specs=vm, out_shape=S((R, C), F32),
        scratch_shapes=[pltpu.VMEM((N_DEV, R, C), F32), pltpu.SemaphoreType.DMA((N_DEV - 1,)),
                        pltpu.SemaphoreType.DMA((N_DEV - 1,))],
    )(part)


def _pack(parts):
    rows = []
    for p in parts:
        f = p.reshape(-1).astype(F32)
        pad = (-f.shape[0]) % (8 * LANES)
        rows.append(jnp.pad(f, (0, pad)).reshape(-1, LANES))
    return jnp.concatenate(rows, axis=0)


def _unpack(buf, shapes):
    out, r = [], 0
    for shp in shapes:
        n = 1
        for s in shp:
            n *= s
        nr = -(-n // (8 * LANES)) * 8
        out.append(buf[r:r + nr].reshape(-1)[:n].reshape(shp))
        r += nr
    return out


def kernel(x, norm_w, w_in, conv_w, a_log, dt_bias, head_norm_w, sgu_ln_w, sgu_ln_b, w_spatial, b_spatial, w_out, final_norm_w, loss_target, m_norm_w, m_w_in, m_conv_w, m_a_log, m_dt_bias, m_head_norm_w, m_sgu_ln_w, m_sgu_ln_b, m_w_spatial, m_b_spatial, m_w_out, m_final_norm_w, v_norm_w, v_w_in, v_conv_w, v_a_log, v_dt_bias, v_head_norm_w, v_sgu_ln_w, v_sgu_ln_b, v_w_spatial, v_b_spatial, v_w_out, v_final_norm_w):
    T, DM = x.shape[1], x.shape[2]
    H, D = a_log.shape[1], head_norm_w.shape[1]
    G, P = w_spatial.shape[1], w_spatial.shape[2]
    AW, BW = H * D, G * P
    MIX = AW + BW
    WD = w_in.shape[2]
    IN = N_DEV * WD
    RO = w_out.shape[1]
    CW = conv_w.shape[2]
    sizes = (3 * AW, AW, H, H, BW, BW, BW)
    assert sum(sizes) == IN and 2 * H <= LANES and 3 * H <= 32 and N_DEV * RO == MIX and N_DEV * CW == 3 * AW
    offs = [0]
    for s in sizes:
        offs.append(offs[-1] + s)
    px, py, pc = _position()
    dev = 4 * px + 2 * py + pc
    chip = 2 * px + py

    x2, tgt = x[0], loss_target[0]

    g_win, g_wout, g_conv = _all_gather_weights(
        [w_in[0].astype(BF16), w_out[0].astype(BF16), conv_w[0]])
    w_full = g_win.transpose(1, 0, 2).reshape(DM, IN)
    w_main = jnp.concatenate([w_full[:, :offs[2]], w_full[:, offs[4]:]], axis=1)
    w_ba = jnp.pad(w_full[:, offs[2]:offs[4]], ((0, 0), (0, LANES - 2 * H)))
    w_out_full = g_wout.reshape(MIX, DM)
    conv_full = g_conv.transpose(1, 0, 2).reshape(4, 3 * AW)
    alog_row = jnp.pad(a_log, ((0, 0), (H, LANES - 2 * H)))
    dtb_row = jnp.pad(dt_bias, ((0, 0), (H, LANES - 2 * H)))
    bs_t = b_spatial[0].T

    proj, ba, xn = _in_proj(x2, norm_w, w_main, w_ba)
    q, k, v, c, gcol, grow = _prep_a_fwd(proj, ba, conv_full, alog_row, dtb_row, H, D)
    o, ssave, asave = _delta_fwd(q, k, v, gcol, grow, H, D)
    ocat = _mix_fwd(o, proj, head_norm_w, sgu_ln_w, sgu_ln_b, w_spatial[0], bs_t, H, D, G, P)
    dh, d_ocat, loss_acc, g_fnw = _out_proj_loss(ocat, w_out_full, x2, tgt, final_norm_w.reshape(1, DM))
    loss = lax.psum(loss_acc[0, 0], AXES)

    g_wout_part = _grad_w(ocat, dh, "grad_w_out")
    d_o, d_rest, g_hnw, g_ln, g_wsp, g_bs_t = _mix_bwd(
        d_ocat, o, proj, head_norm_w, sgu_ln_w, sgu_ln_b, w_spatial[0], bs_t, H, D, G, P)
    dq, dk, dv, dgate, dpar = _delta_bwd(q, k, v, gcol, grow, ba, ssave, asave, d_o, a_log, dt_bias, H, D)
    dc = _prep_a_bwd_pointwise(dq, dk, dv, c, H, D)
    d_qkv, g_conv_part = _conv_bwd(dc, proj, conv_full, 3 * AW)
    dmain = jnp.concatenate([d_qkv, d_rest], axis=1)
    dba = dgate.astype(BF16)
    g_main, g_ba = _grad_w_in(xn, dmain, dba)
    grad_x, g_nw = _dx(dmain, dba, w_main, w_ba, x2, dh, norm_w)

    g_full = jnp.concatenate([g_main[:, :offs[2]], g_ba[:, :2 * H], g_main[:, offs[2]:]], axis=1)
    g_win_blocks = g_full.astype(BF16).reshape(DM, 4, 2, WD).transpose(1, 2, 0, 3)
    g_wout_blocks = g_wout_part.astype(BF16).reshape(4, 2, RO, DM)
    keep = lambda b: lax.dynamic_index_in_dim(b, pc, axis=1, keepdims=False)
    give = lambda b: lax.dynamic_index_in_dim(b, 1 - pc, axis=1, keepdims=False)
    from_sib = _exchange_sibling([give(g_win_blocks), give(g_wout_blocks)])
    chip_win = _pair_sum(keep(g_win_blocks), from_sib[0], "pair_sum_w_in")
    chip_wout = _pair_sum(keep(g_wout_blocks), from_sib[1], "pair_sum_w_out")
    recv_win, recv_wout = _exchange_chips([chip_win, chip_wout])
    own = lambda b: lax.dynamic_index_in_dim(b, chip, axis=0, keepdims=False)
    grad_w_in, delta_w_in, new_m_w_in, new_v_w_in = _sum_adam(
        own(chip_win), recv_win, w_in[0], m_w_in[0], v_w_in[0], "sum_adam_w_in")
    grad_w_out, delta_w_out, new_m_w_out, new_v_w_out = _sum_adam(
        own(chip_wout), recv_wout, w_out[0], m_w_out[0], v_w_out[0], "sum_adam_w_out")

    small_shapes = [norm_w.shape, a_log.shape, dt_bias.shape, head_norm_w.shape, sgu_ln_w.shape, sgu_ln_b.shape,
                    w_spatial.shape, b_spatial.shape, final_norm_w.shape]
    parts = [g_nw[0], dpar[0, :H], dpar[0, H:2 * H], g_hnw[0], g_ln[0], g_ln[1], g_wsp, g_bs_t[:, :G].T, g_fnw[0],
             g_conv_part[:4]]
    red = _all_reduce_small(_pack(parts))
    grads_small = _unpack(red, small_shapes + [(4, 3 * AW)])
    g_conv_full = grads_small.pop()
    grad_conv = lax.dynamic_slice_in_dim(g_conv_full, dev * CW, CW, axis=1)[None]
    small_w = [norm_w, a_log, dt_bias, head_norm_w, sgu_ln_w, sgu_ln_b, w_spatial, b_spatial, final_norm_w, conv_w]
    small_m = [m_norm_w, m_a_log, m_dt_bias, m_head_norm_w, m_sgu_ln_w, m_sgu_ln_b, m_w_spatial, m_b_spatial,
               m_final_norm_w, m_conv_w]
    small_v = [v_norm_w, v_a_log, v_dt_bias, v_head_norm_w, v_sgu_ln_w, v_sgu_ln_b, v_w_spatial, v_b_spatial,
               v_final_norm_w, v_conv_w]
    small_g = grads_small + [grad_conv]
    shapes10 = [w.shape for w in small_w]
    d_p, m_p, v_p = _adam_small(_pack(small_w), _pack(small_g), _pack(small_m), _pack(small_v))
    d_s, m_s, v_s = _unpack(d_p, shapes10), _unpack(m_p, shapes10), _unpack(v_p, shapes10)

    def order(small, win, wout):
        return [small[0], win[None], small[9], small[1], small[2], small[3], small[4], small[5], small[6], small[7],
                wout[None], small[8]]

    grads = order(small_g, grad_w_in, grad_w_out)
    deltas = order(d_s, delta_w_in, delta_w_out)
    new_m = order(m_s, new_m_w_in, new_m_w_out)
    new_v = order(v_s, new_v_w_in, new_v_w_out)
    return (loss, grad_x[None], *grads, *deltas, *new_m, *new_v)
```

```python
import functools

import jax
import jax.numpy as jnp
from jax import lax
from jax.experimental import pallas as pl
from jax.experimental.pallas import tpu as pltpu

F32 = jnp.float32
BF16 = jnp.bfloat16
MXU = jnp.bfloat16
HI = lax.Precision.HIGHEST
EPS = 1e-6
CHUNK_A = 64
LANES = 128
MESH = pl.DeviceIdType.MESH
AXES = ("x", "y", "c")
N_DEV = 8

ADAM_LR = 0.001
ADAM_B1 = 0.9
ADAM_B2 = 0.999
ADAM_EPS = 1e-08
ADAM_WD = 0.01
ADAM_STEP = 10

S = jax.ShapeDtypeStruct
ARB = "arbitrary"


def _cp(*sem):
    return pltpu.CompilerParams(dimension_semantics=tuple(sem), vmem_limit_bytes=56 * 1024 * 1024)


def _tile(n, cap, mult):
    best = None
    t = mult
    while t <= min(n, cap):
        if n % t == 0:
            best = t
        t += mult
    return best if best is not None else n


def _mm(a, b):
    return jnp.dot(a.astype(MXU), b.astype(MXU), preferred_element_type=F32)


def _mm_nt(a, b):
    return lax.dot_general(a.astype(MXU), b.astype(MXU), (((1,), (1,)), ((), ())), preferred_element_type=F32)


def _mm_tn(a, b):
    return lax.dot_general(a.astype(MXU), b.astype(MXU), (((0,), (0,)), ((), ())), preferred_element_type=F32)


def _mmh(a, b):
    return jnp.dot(a, b, precision=HI, preferred_element_type=F32)


def _mmh_tn(a, b):
    return lax.dot_general(a, b, (((0,), (0,)), ((), ())), precision=HI, preferred_element_type=F32)


def _sigmoid(x):
    return 1.0 / (1.0 + jnp.exp(-x))


def _silu(x):
    return x * _sigmoid(x)


def _dsilu(x):
    s = _sigmoid(x)
    return s * (1.0 + x * (1.0 - s))


def _softplus(x):
    return jnp.maximum(x, 0.0) + jnp.log(1.0 + jnp.exp(-jnp.abs(x)))


def _in_proj(x, norm_w, w_main, w_ba):
    T, DM = x.shape
    NM = w_main.shape[1]
    tm = _tile(T, 512, 8)
    tn = _tile(NM, 1024, LANES)

    def body(x_ref, nw_ref, w_ref, wba_ref, proj_ref, ba_ref, xn_ref):
        @pl.when(pl.program_id(1) == 0)
        def _():
            xv = x_ref[...]
            r = lax.rsqrt(jnp.mean(xv * xv, axis=-1, keepdims=True) + EPS)
            xn = (xv * r * nw_ref[...]).astype(BF16)
            xn_ref[...] = xn
            ba_ref[...] = jnp.dot(xn.astype(MXU), wba_ref[...].astype(MXU), preferred_element_type=F32)

        proj_ref[...] = jnp.dot(xn_ref[...].astype(MXU), w_ref[...].astype(MXU), preferred_element_type=F32)

    return pl.pallas_call(
        body, name="in_proj", grid=(T // tm, NM // tn),
        in_specs=[pl.BlockSpec((tm, DM), lambda i, j: (i, 0)),
                  pl.BlockSpec((1, DM), lambda i, j: (0, 0)),
                  pl.BlockSpec((DM, tn), lambda i, j: (0, j)),
                  pl.BlockSpec((DM, LANES), lambda i, j: (0, 0))],
        out_specs=[pl.BlockSpec((tm, tn), lambda i, j: (i, j)),
                   pl.BlockSpec((tm, LANES), lambda i, j: (i, 0)),
                   pl.BlockSpec((tm, DM), lambda i, j: (i, 0))],
        out_shape=[S((T, NM), F32), S((T, LANES), F32), S((T, DM), BF16)],
        compiler_params=_cp(ARB, ARB),
    )(x, norm_w, w_main, w_ba)


def _prep_a_fwd(proj, ba, conv_w, alog_row, dtb_row, H, D):
    T = proj.shape[0]
    AW = H * D
    C3 = 3 * AW
    tb = _tile(T, 256, CHUNK_A)
    nch = tb // CHUNK_A
    nblk = T // tb
    scale = float(D) ** -0.5

    def body(x_ref, halo_ref, ba_ref, cw_ref, al_ref, dt_ref, q_ref, k_ref, v_ref, c_ref, gcol_ref, grow_ref):
        i = pl.program_id(0)
        xv = x_ref[...]
        halo = halo_ref[...] * (i > 0).astype(F32)
        xp = jnp.concatenate([halo, xv], axis=0)
        cw = cw_ref[...]
        c = cw[0:1, :] * xp[5:5 + tb]
        for j in range(1, 4):
            c = c + cw[j:j + 1, :] * xp[5 + j:5 + j + tb]
        c_ref[...] = c
        a = _silu(c)
        for h in range(H):
            qh = a[:, h * D:(h + 1) * D]
            kh = a[:, AW + h * D:AW + (h + 1) * D]
            qr = lax.rsqrt(jnp.sum(qh * qh, axis=-1, keepdims=True) + EPS)
            kr = lax.rsqrt(jnp.sum(kh * kh, axis=-1, keepdims=True) + EPS)
            q_ref[:, h * D:(h + 1) * D] = qh * (qr * scale)
            k_ref[:, h * D:(h + 1) * D] = kh * kr
        v_ref[...] = a[:, 2 * AW:]

        bav = ba_ref[...]
        lane = lax.broadcasted_iota(jnp.int32, (tb, LANES), 1)
        beta = _sigmoid(bav)
        g = -jnp.exp(al_ref[...]) * _softplus(bav + dt_ref[...])
        gates = jnp.where(lane < H, beta, jnp.where(lane < 2 * H, g, 0.0))
        ri = lax.broadcasted_iota(jnp.int32, (CHUNK_A, CHUNK_A), 0)
        ci = lax.broadcasted_iota(jnp.int32, (CHUNK_A, CHUNK_A), 1)
        tri = (ri >= ci).astype(F32)
        lane_c = lax.broadcasted_iota(jnp.int32, (CHUNK_A, LANES), 1)
        for cc in range(nch):
            gch = gates[cc * CHUNK_A:(cc + 1) * CHUNK_A]
            gc = pltpu.roll(_mmh(tri, gch), H, 1)
            full = jnp.where(lane_c < 2 * H, gch, jnp.where(lane_c < 3 * H, gc, 0.0))
            gcol_ref[cc * CHUNK_A:(cc + 1) * CHUNK_A, :] = full
            grow_ref[cc] = full.T[0:32, :]

    return pl.pallas_call(
        body, name="prep_a_fwd", grid=(nblk,),
        in_specs=[pl.BlockSpec((tb, C3), lambda i: (i, 0)),
                  pl.BlockSpec((8, C3), lambda i: (jnp.maximum(i * (tb // 8) - 1, 0), 0)),
                  pl.BlockSpec((tb, LANES), lambda i: (i, 0)),
                  pl.BlockSpec((4, C3), lambda i: (0, 0)),
                  pl.BlockSpec((1, LANES), lambda i: (0, 0)),
                  pl.BlockSpec((1, LANES), lambda i: (0, 0))],
        out_specs=[pl.BlockSpec((tb, AW), lambda i: (i, 0)),
                   pl.BlockSpec((tb, AW), lambda i: (i, 0)),
                   pl.BlockSpec((tb, AW), lambda i: (i, 0)),
                   pl.BlockSpec((tb, C3), lambda i: (i, 0)),
                   pl.BlockSpec((tb, LANES), lambda i: (i, 0)),
                   pl.BlockSpec((nch, 32, CHUNK_A), lambda i: (i, 0, 0))],
        out_shape=[S((T, AW), F32), S((T, AW), F32), S((T, AW), F32), S((T, C3), F32),
                   S((T, LANES), F32), S((T // CHUNK_A, 32, CHUNK_A), F32)],
        compiler_params=_cp(ARB),
    )(proj, proj, ba, conv_w, alog_row, dtb_row)


_NN = (((1,), (0,)), ((), ()))
_TN = (((0,), (0,)), ((), ()))


def _split(a):
    hi = a.astype(BF16)
    return hi, (a - hi.astype(F32)).astype(BF16)


def _mm3(a, b, dims=_NN):
    ah, al = a if isinstance(a, tuple) else _split(a)
    bh, bl = b if isinstance(b, tuple) else _split(b)
    dg = lambda p, r: lax.dot_general(p, r, dims, preferred_element_type=F32)
    return dg(ah, bh) + (dg(ah, bl) + dg(al, bh))


def _interleave(gens):
    gens = list(gens)
    while gens:
        alive = []
        for g in gens:
            try:
                next(g)
                alive.append(g)
            except StopIteration:
                pass
        gens = alive


def _chunk_terms(q, k, v, gcolv, growv, h, H):
    C = CHUNK_A
    beta_c = gcolv[:, h:h + 1]
    g_c = gcolv[:, H + h:H + h + 1]
    gc_c = gcolv[:, 2 * H + h:2 * H + h + 1]
    gc_r = growv[2 * H + h:2 * H + h + 1, :]
    ri = lax.broadcasted_iota(jnp.int32, (C, C), 0)
    ci = lax.broadcasted_iota(jnp.int32, (C, C), 1)
    incl = ri >= ci
    strict = ri > ci
    kb = k * beta_c
    vb = v * beta_c
    p_raw = _mm_nt(kb, k)
    qk_raw = _mm_nt(q, k)
    gam = jnp.where(incl, jnp.exp(jnp.where(incl, gc_c - gc_r, 0.0)), 0.0)
    e_c = jnp.exp(gc_c)
    gl = gc_r[:, C - 1:C]
    edec = jnp.exp(gl - gc_c)
    yield
    lmat = jnp.where(strict, p_raw * gam, 0.0)
    attn = jnp.where(incl, qk_raw * gam, 0.0)
    return dict(beta_c=beta_c, g_c=g_c, gc_c=gc_c, gc_r=gc_r, incl=incl, strict=strict, gam=gam, e_c=e_c,
                kb=kb, vb=vb, lmat=lmat, attn=attn, gl=gl, edec=edec, ri=ri, ci=ci)


def _inv_unit_lower(lmat):
    C = lmat.shape[0]
    ri = lax.broadcasted_iota(jnp.int32, (C, C), 0)
    ci = lax.broadcasted_iota(jnp.int32, (C, C), 1)
    eye = (ri == ci).astype(F32)
    x = -lmat
    a = eye + x
    n = 1
    while 2 * n < C:
        xs = _split(x)
        x = _mm3(xs, xs)
        yield
        a = a + _mm3(a, x)
        n *= 2
    yield
    return a


def _delta_fwd(q, k, v, gcol, grow, H, D):
    T = q.shape[0]
    C = CHUNK_A
    N = T // C
    AW = H * D

    def body(q_ref, k_ref, v_ref, gcol_ref, grow_ref, o_ref, vn_ref, ssave_ref, asave_ref, s_ref):
        @pl.when(pl.program_id(0) == 0)
        def _():
            s_ref[...] = jnp.zeros_like(s_ref)

        gcolv = gcol_ref[...]
        growv = grow_ref[0]

        def head(h):
            sl = slice(h * D, (h + 1) * D)
            st = s_ref[h]
            ssave_ref[0, h] = st
            qv, kv, vv = q_ref[:, sl], k_ref[:, sl], v_ref[:, sl]
            t = yield from _chunk_terms(qv, kv, vv, gcolv, growv, h, H)
            ks = _mm(t["kb"] * t["e_c"], st)
            o_inter = _mm(qv * t["e_c"], st)
            a = yield from _inv_unit_lower(t["lmat"])
            asave_ref[0, h] = a
            v_new = _mm3(a, t["vb"] - ks)
            yield
            vn_ref[:, sl] = v_new
            o_intra = _mm(t["attn"], v_new)
            s_upd = _mm_tn(kv * t["edec"], v_new)
            yield
            o_ref[:, sl] = o_inter + o_intra
            s_ref[h] = st * jnp.exp(t["gl"]) + s_upd

        _interleave(head(h) for h in range(H))

    blk = lambda: pl.BlockSpec((C, AW), lambda n: (n, 0))
    return pl.pallas_call(
        body, name="delta_fwd", grid=(N,),
        in_specs=[blk(), blk(), blk(),
                  pl.BlockSpec((C, LANES), lambda n: (n, 0)),
                  pl.BlockSpec((1, 32, C), lambda n: (n, 0, 0))],
        out_specs=[blk(), blk(),
                   pl.BlockSpec((1, H, D, D), lambda n: (n, 0, 0, 0)),
                   pl.BlockSpec((1, H, C, C), lambda n: (n, 0, 0, 0))],
        out_shape=[S((T, AW), F32), S((T, AW), F32), S((N, H, D, D), F32), S((N, H, C, C), F32)],
        scratch_shapes=[pltpu.VMEM((H, D, D), F32)],
        compiler_params=_cp(ARB),
    )(q, k, v, gcol, grow)


def _delta_bwd(q, k, v, gcol, grow, ba, vnew, ssave, asave, d_o, a_log, dt_bias, H, D):
    T = q.shape[0]
    C = CHUNK_A
    N = T // C
    AW = H * D

    def body(al_ref, dt_ref, q_ref, k_ref, v_ref, gcol_ref, grow_ref, ba_ref, vn_ref, ss_ref, as_ref, do_ref,
             dq_ref, dk_ref, dv_ref, dgate_ref, dpar_ref, ds_ref):
        @pl.when(pl.program_id(0) == 0)
        def _():
            ds_ref[...] = jnp.zeros_like(ds_ref)
            dpar_ref[...] = jnp.zeros_like(dpar_ref)

        gcolv = gcol_ref[...]
        growv = grow_ref[0]
        bav = ba_ref[...]
        lane = lax.broadcasted_iota(jnp.int32, (C, LANES), 1)
        lane1 = lax.broadcasted_iota(jnp.int32, (1, LANES), 1)
        rowi = lax.broadcasted_iota(jnp.int32, (C, 1), 0)
        acc = {"dgate": jnp.zeros((C, LANES), F32), "dpar": jnp.zeros((1, LANES), F32)}

        def head(h):
            sl = slice(h * D, (h + 1) * D)
            ds_next = ds_ref[h]
            st = ss_ref[0, h]
            a = as_ref[0, h]
            qv, kv, vv, dov, v_new = q_ref[:, sl], k_ref[:, sl], v_ref[:, sl], do_ref[:, sl], vn_ref[:, sl]
            t = yield from _chunk_terms(qv, kv, vv, gcolv, growv, h, H)
            beta_c, e_c, gam, kb = t["beta_c"], t["e_c"], t["gam"], t["kb"]
            incl, strict, attn, lmat, edec = t["incl"], t["strict"], t["attn"], t["lmat"], t["edec"]
            kdec = kv * edec
            egl = jnp.exp(t["gl"])
            qe = qv * e_c
            ekb = kb * e_c

            dkdec = _mm_nt(v_new, ds_next)
            dv_new_s = _mm(kdec, ds_next)
            t1 = _mm_nt(dov, st)
            ds_o = _mm_tn(qe, dov)
            dattn_raw = _mm_nt(dov, v_new)
            dv_new_o = _mm_tn(attn, dov)
            yield
            dgl = egl * jnp.sum(jnp.sum(st * ds_next, axis=1, keepdims=True), axis=0, keepdims=True)
            dk = edec * dkdec
            r = jnp.sum(dkdec * kdec, axis=1, keepdims=True)
            dgc = -r
            dgl = dgl + jnp.sum(r, axis=0, keepdims=True)
            dq = e_c * t1
            dgc = dgc + jnp.sum(t1 * qe, axis=1, keepdims=True)
            dattn = jnp.where(incl, dattn_raw, 0.0)
            dv_new = dv_new_s + dv_new_o
            dqm = dattn * gam
            z = dattn * attn
            dvb = _mm3(a, dv_new, _TN)
            dq_a = _mm(dqm, kv)
            dk_a = _mm_tn(dqm, qv)
            yield
            dq_ref[:, sl] = dq + dq_a
            dv_ref[:, sl] = beta_c * dvb
            ds_kb = _mm_tn(ekb, dvb)
            dekb_neg = _mm_nt(dvb, st)
            dl_neg = _mm_nt(dvb, v_new)
            yield
            ds_ref[h] = egl * ds_next + ds_o - ds_kb
            dekb = -dekb_neg
            dl = jnp.where(strict, -dl_neg, 0.0)
            dp = dl * gam
            z = z + dl * lmat
            dkb_p = _mm(dp, kv)
            dk_p = _mm_tn(dp, kb)
            dgc = dgc + jnp.sum(dekb * ekb, axis=1, keepdims=True)
            dgc = dgc + jnp.sum(z, axis=1, keepdims=True) - jnp.sum(z.T, axis=1, keepdims=True)
            dgc = dgc + jnp.where(rowi == C - 1, dgl, 0.0)
            upper = (t["ri"] <= t["ci"]).astype(F32)
            dg_b = _mm3(upper, jnp.broadcast_to(dgc, (C, LANES)))
            yield
            dkb = dkb_p + e_c * dekb
            dk_ref[:, sl] = dk + dk_a + dk_p + beta_c * dkb
            dbeta = jnp.sum(dkb * kv, axis=1, keepdims=True) + jnp.sum(dvb * vv, axis=1, keepdims=True)
            dg = dg_b[:, 0:1]
            a_raw = bav[:, H + h:H + h + 1]
            d_braw = dbeta * beta_c * (1.0 - beta_c)
            d_araw = dg * (-jnp.exp(al_ref[0, h])) * _sigmoid(a_raw + dt_ref[0, h])
            acc["dgate"] = acc["dgate"] + jnp.where(lane == h, d_braw, 0.0) + jnp.where(lane == H + h, d_araw, 0.0)
            dal = jnp.sum(dg * t["g_c"], axis=0, keepdims=True)
            ddt = jnp.sum(d_araw, axis=0, keepdims=True)
            acc["dpar"] = acc["dpar"] + jnp.where(lane1 == h, dal, 0.0) + jnp.where(lane1 == H + h, ddt, 0.0)

        _interleave(head(h) for h in range(H))
        dgate_ref[...] = acc["dgate"]
        dpar_ref[0:1, :] += acc["dpar"]

    rev = lambda s: N - 1 - s
    blk = lambda: pl.BlockSpec((C, AW), lambda s: (rev(s), 0))
    smem = pl.BlockSpec(memory_space=pltpu.SMEM)
    return pl.pallas_call(
        body, name="delta_bwd", grid=(N,),
        in_specs=[smem, smem, blk(), blk(), blk(),
                  pl.BlockSpec((C, LANES), lambda s: (rev(s), 0)),
                  pl.BlockSpec((1, 32, C), lambda s: (rev(s), 0, 0)),
                  pl.BlockSpec((C, LANES), lambda s: (rev(s), 0)),
                  blk(),
                  pl.BlockSpec((1, H, D, D), lambda s: (rev(s), 0, 0, 0)),
                  pl.BlockSpec((1, H, C, C), lambda s: (rev(s), 0, 0, 0)),
                  blk()],
        out_specs=[blk(), blk(), blk(),
                   pl.BlockSpec((C, LANES), lambda s: (rev(s), 0)),
                   pl.BlockSpec((8, LANES), lambda s: (0, 0))],
        out_shape=[S((T, AW), F32), S((T, AW), F32), S((T, AW), F32),
                   S((T, LANES), F32), S((8, LANES), F32)],
        scratch_shapes=[pltpu.VMEM((H, D, D), F32)],
        compiler_params=_cp(ARB),
    )(a_log, dt_bias, q, k, v, gcol, grow, ba, vnew, ssave, asave, d_o)


def _ln_stats(xv):
    mu = jnp.mean(xv, axis=-1, keepdims=True)
    xc = xv - mu
    var = jnp.mean(xc * xc, axis=-1, keepdims=True)
    rstd = lax.rsqrt(var + EPS)
    return xc * rstd, rstd


def _mix_fwd(o, proj, head_norm_w, ln_w, ln_b, w_sp, bs_t, H, D, G, P):
    T = o.shape[0]
    AW, BW = H * D, G * P
    MIX = AW + BW
    nb = AW // BW if AW % BW == 0 else None
    assert nb == 1, "group widths must match the projection column blocks"
    cb = 3

    def body(o_ref, za_ref, ub_ref, vb_ref, zb_ref, hw_ref, lw_ref, lb_ref, w_ref, bs_ref, out_ref):
        hw = hw_ref[...]
        for h in range(H):
            sl = slice(h * D, (h + 1) * D)
            oh = o_ref[:, sl]
            rs = lax.rsqrt(jnp.mean(oh * oh, axis=-1, keepdims=True) + EPS)
            out_ref[:, sl] = (oh * rs * hw * _silu(za_ref[:, sl])).astype(BF16)
        xhat, _ = _ln_stats(vb_ref[...])
        vn = xhat * lw_ref[...] + lb_ref[...]
        ri = lax.broadcasted_iota(jnp.int32, (P, P), 0)
        ci = lax.broadcasted_iota(jnp.int32, (P, P), 1)
        bsv = bs_ref[...]
        for g in range(G):
            sl = slice(g * P, (g + 1) * P)
            wm = jnp.where(ri >= ci, w_ref[g], 0.0)
            s = _mm(wm, vn[:, sl]) + bsv[:, g:g + 1]
            out_ref[:, AW + g * P:AW + (g + 1) * P] = (ub_ref[:, sl] * s * _silu(zb_ref[:, sl])).astype(BF16)

    row = lambda w: pl.BlockSpec((1, w), lambda i: (0, 0))
    return pl.pallas_call(
        body, name="mix_fwd", grid=(T // P,),
        in_specs=[pl.BlockSpec((P, AW), lambda i: (i, 0)),
                  pl.BlockSpec((P, AW), lambda i: (i, cb)),
                  pl.BlockSpec((P, BW), lambda i: (i, cb + 1)),
                  pl.BlockSpec((P, BW), lambda i: (i, cb + 2)),
                  pl.BlockSpec((P, BW), lambda i: (i, cb + 3)),
                  row(D), row(BW), row(BW),
                  pl.BlockSpec((G, P, P), lambda i: (0, 0, 0)),
                  pl.BlockSpec((P, G), lambda i: (0, 0))],
        out_specs=pl.BlockSpec((P, MIX), lambda i: (i, 0)),
        out_shape=S((T, MIX), BF16),
        compiler_params=_cp(ARB),
    )(o, proj, proj, proj, proj, head_norm_w, ln_w, ln_b, w_sp, bs_t)


def _mix_bwd(d_ocat, o, proj, head_norm_w, ln_w, ln_b, w_sp, bs_t, H, D, G, P):
    T = o.shape[0]
    AW, BW = H * D, G * P
    MIX = AW + BW
    cb = 3

    def body(dc_ref, o_ref, za_ref, ub_ref, vb_ref, zb_ref, hw_ref, lw_ref, lb_ref, w_ref, bs_ref,
             do_ref, drest_ref, dhw_ref, dln_ref, dw_ref, dbs_ref, dvn_ref):
        @pl.when(pl.program_id(0) == 0)
        def _():
            dhw_ref[...] = jnp.zeros_like(dhw_ref)
            dln_ref[...] = jnp.zeros_like(dln_ref)
            dw_ref[...] = jnp.zeros_like(dw_ref)
            dbs_ref[...] = jnp.zeros_like(dbs_ref)

        hw = hw_ref[...]
        dhw = jnp.zeros((1, D), F32)
        for h in range(H):
            sl = slice(h * D, (h + 1) * D)
            oh = o_ref[:, sl]
            za = za_ref[:, sl]
            doa = dc_ref[:, sl]
            rs = lax.rsqrt(jnp.mean(oh * oh, axis=-1, keepdims=True) + EPS)
            xh = oh * rs
            d_on = doa * _silu(za)
            drest_ref[:, sl] = (doa * (xh * hw) * _dsilu(za)).astype(BF16)
            dhw = dhw + jnp.sum(d_on * xh, axis=0, keepdims=True)
            dxh = d_on * hw
            do_ref[:, sl] = rs * (dxh - xh * jnp.mean(dxh * xh, axis=-1, keepdims=True))
        dhw_ref[0:1, :] += dhw

        xhat, rstd = _ln_stats(vb_ref[...])
        lw = lw_ref[...]
        vn = xhat * lw + lb_ref[...]
        ri = lax.broadcasted_iota(jnp.int32, (P, P), 0)
        ci = lax.broadcasted_iota(jnp.int32, (P, P), 1)
        lane = lax.broadcasted_iota(jnp.int32, (P, LANES), 1)
        bsv = bs_ref[...]
        dbs = jnp.zeros((P, LANES), F32)
        for g in range(G):
            sl = slice(g * P, (g + 1) * P)
            wm = jnp.where(ri >= ci, w_ref[g], 0.0)
            vng = vn[:, sl]
            s = _mm(wm, vng) + bsv[:, g:g + 1]
            dob = dc_ref[:, AW + g * P:AW + (g + 1) * P]
            ub = ub_ref[:, sl]
            zb = zb_ref[:, sl]
            szb = _silu(zb)
            drest_ref[:, AW + g * P:AW + (g + 1) * P] = (dob * s * szb).astype(BF16)
            drest_ref[:, AW + 2 * BW + g * P:AW + 2 * BW + (g + 1) * P] = (dob * ub * s * _dsilu(zb)).astype(BF16)
            ds = dob * ub * szb
            dvn_ref[:, sl] = _mm_tn(wm, ds)
            dw_ref[g] += jnp.where(ri >= ci, _mm_nt(ds, vng), 0.0)
            dbs = dbs + jnp.where(lane == g, jnp.sum(ds, axis=1, keepdims=True), 0.0)
        dbs_ref[...] += dbs
        dvn = dvn_ref[...]
        dln_ref[0:1, :] += jnp.sum(dvn * xhat, axis=0, keepdims=True)
        dln_ref[1:2, :] += jnp.sum(dvn, axis=0, keepdims=True)
        dxh = dvn * lw
        dvb = rstd * (dxh - jnp.mean(dxh, axis=-1, keepdims=True) - xhat * jnp.mean(dxh * xhat, axis=-1, keepdims=True))
        drest_ref[:, AW + BW:AW + 2 * BW] = dvb.astype(BF16)

    row = lambda w: pl.BlockSpec((1, w), lambda i: (0, 0))
    return pl.pallas_call(
        body, name="mix_bwd", grid=(T // P,),
        in_specs=[pl.BlockSpec((P, MIX), lambda i: (i, 0)),
                  pl.BlockSpec((P, AW), lambda i: (i, 0)),
                  pl.BlockSpec((P, AW), lambda i: (i, cb)),
                  pl.BlockSpec((P, BW), lambda i: (i, cb + 1)),
                  pl.BlockSpec((P, BW), lambda i: (i, cb + 2)),
                  pl.BlockSpec((P, BW), lambda i: (i, cb + 3)),
                  row(D), row(BW), row(BW),
                  pl.BlockSpec((G, P, P), lambda i: (0, 0, 0)),
                  pl.BlockSpec((P, G), lambda i: (0, 0))],
        out_specs=[pl.BlockSpec((P, AW), lambda i: (i, 0)),
                   pl.BlockSpec((P, AW + 3 * BW), lambda i: (i, 0)),
                   pl.BlockSpec((8, D), lambda i: (0, 0)),
                   pl.BlockSpec((8, BW), lambda i: (0, 0)),
                   pl.BlockSpec((G, P, P), lambda i: (0, 0, 0)),
                   pl.BlockSpec((P, LANES), lambda i: (0, 0))],
        out_shape=[S((T, AW), F32), S((T, AW + 3 * BW), BF16), S((8, D), F32), S((8, BW), F32),
                   S((G, P, P), F32), S((P, LANES), F32)],
        scratch_shapes=[pltpu.VMEM((P, BW), F32)],
        compiler_params=_cp(ARB),
    )(d_ocat, o, proj, proj, proj, proj, head_norm_w, ln_w, ln_b, w_sp, bs_t)


def _out_proj_loss(ocat, w_out, x, target, fnw):
    T, MIX = ocat.shape
    DM = x.shape[1]
    tm = _tile(T, 256, 8)

    def body(oc_ref, w_ref, x_ref, t_ref, fw_ref, dh_ref, doc_ref, loss_ref, gfw_ref):
        @pl.when(pl.program_id(0) == 0)
        def _():
            loss_ref[...] = jnp.zeros_like(loss_ref)
            gfw_ref[...] = jnp.zeros_like(gfw_ref)

        wv = w_ref[...]
        hh = x_ref[...] + jnp.dot(oc_ref[...].astype(MXU), wv.astype(MXU), preferred_element_type=F32)
        rs = lax.rsqrt(jnp.mean(hh * hh, axis=-1, keepdims=True) + EPS)
        hn = hh * rs
        fw = fw_ref[...]
        e = hn * fw - t_ref[...]
        row_loss = 0.5 * jnp.mean(e * e, axis=-1, keepdims=True)
        loss_ref[...] += jnp.sum(row_loss, axis=0, keepdims=True)
        dy = e * (1.0 / DM)
        gfw_ref[0:1, :] += jnp.sum(dy * hn, axis=0, keepdims=True)
        dhn = dy * fw
        dh = rs * (dhn - hn * jnp.mean(dhn * hn, axis=-1, keepdims=True))
        dh_ref[...] = dh
        doc_ref[...] = _mm_nt(dh, wv)

    return pl.pallas_call(
        body, name="out_proj_loss", grid=(T // tm,),
        in_specs=[pl.BlockSpec((tm, MIX), lambda i: (i, 0)),
                  pl.BlockSpec((MIX, DM), lambda i: (0, 0)),
                  pl.BlockSpec((tm, DM), lambda i: (i, 0)),
                  pl.BlockSpec((tm, DM), lambda i: (i, 0)),
                  pl.BlockSpec((1, DM), lambda i: (0, 0))],
        out_specs=[pl.BlockSpec((tm, DM), lambda i: (i, 0)),
                   pl.BlockSpec((tm, MIX), lambda i: (i, 0)),
                   pl.BlockSpec((8, LANES), lambda i: (0, 0)),
                   pl.BlockSpec((8, DM), lambda i: (0, 0))],
        out_shape=[S((T, DM), F32), S((T, MIX), F32), S((8, LANES), F32), S((8, DM), F32)],
        compiler_params=_cp(ARB),
    )(ocat, w_out, x, target, fnw)


def _grad_w(lhs, rhs, name):
    T, A = lhs.shape
    B = rhs.shape[1]
    ta = _tile(A, 512, LANES)
    tk = _tile(T, 512, 8)

    def body(l_ref, r_ref, out_ref):
        @pl.when(pl.program_id(1) == 0)
        def _():
            out_ref[...] = jnp.zeros_like(out_ref)
        out_ref[...] += _mm_tn(l_ref[...], r_ref[...])

    return pl.pallas_call(
        body, name=name, grid=(A // ta, T // tk),
        in_specs=[pl.BlockSpec((tk, ta), lambda i, k: (k, i)),
                  pl.BlockSpec((tk, B), lambda i, k: (k, 0))],
        out_specs=pl.BlockSpec((ta, B), lambda i, k: (i, 0)),
        out_shape=S((A, B), F32),
        compiler_params=_cp(ARB, ARB),
    )(lhs, rhs)


def _grad_w_in(xn, dmain, dba):
    T, DM = xn.shape
    NM = dmain.shape[1]
    tn = _tile(NM, 1024, LANES)
    tk = _tile(T, 512, 8)

    def body(xn_ref, dm_ref, dba_ref, gm_ref, gba_ref):
        j = pl.program_id(0)
        k = pl.program_id(1)

        @pl.when(k == 0)
        def _():
            gm_ref[...] = jnp.zeros_like(gm_ref)

        @pl.when((k == 0) & (j == 0))
        def _():
            gba_ref[...] = jnp.zeros_like(gba_ref)

        xv = xn_ref[...]
        gm_ref[...] += _mm_tn(xv, dm_ref[...])

        @pl.when(j == 0)
        def _():
            gba_ref[...] += _mm_tn(xv, dba_ref[...])

    return pl.pallas_call(
        body, name="grad_w_in", grid=(NM // tn, T // tk),
        in_specs=[pl.BlockSpec((tk, DM), lambda j, k: (k, 0)),
                  pl.BlockSpec((tk, tn), lambda j, k: (k, j)),
                  pl.BlockSpec((tk, LANES), lambda j, k: (k, 0))],
        out_specs=[pl.BlockSpec((DM, tn), lambda j, k: (0, j)),
                   pl.BlockSpec((DM, LANES), lambda j, k: (0, 0))],
        out_shape=[S((DM, NM), F32), S((DM, LANES), F32)],
        compiler_params=_cp(ARB, ARB),
    )(xn, dmain, dba)


def _dx(dmain, dba, w_main, w_ba, x, dh, norm_w):
    T, NM = dmain.shape
    DM = x.shape[1]
    tm = _tile(T, 512, 8)
    tk = _tile(NM, 1024, LANES)
    nk = NM // tk

    def body(dm_ref, dba_ref, w_ref, wba_ref, x_ref, dh_ref, nw_ref, gx_ref, gnw_ref, acc_ref):
        i = pl.program_id(0)
        k = pl.program_id(1)

        @pl.when((i == 0) & (k == 0))
        def _():
            gnw_ref[...] = jnp.zeros_like(gnw_ref)

        @pl.when(k == 0)
        def _():
            acc_ref[...] = _mm_nt(dba_ref[...], wba_ref[...])

        acc_ref[...] += _mm_nt(dm_ref[...], w_ref[...])

        @pl.when(k == nk - 1)
        def _():
            xv = x_ref[...]
            rs = lax.rsqrt(jnp.mean(xv * xv, axis=-1, keepdims=True) + EPS)
            xh = xv * rs
            dxn = acc_ref[...]
            gnw_ref[0:1, :] += jnp.sum(dxn * xh, axis=0, keepdims=True)
            dxh = dxn * nw_ref[...]
            gx_ref[...] = dh_ref[...] + rs * (dxh - xh * jnp.mean(dxh * xh, axis=-1, keepdims=True))

    return pl.pallas_call(
        body, name="dx", grid=(T // tm, nk),
        in_specs=[pl.BlockSpec((tm, tk), lambda i, k: (i, k)),
                  pl.BlockSpec((tm, LANES), lambda i, k: (i, 0)),
                  pl.BlockSpec((DM, tk), lambda i, k: (0, k)),
                  pl.BlockSpec((DM, LANES), lambda i, k: (0, 0)),
                  pl.BlockSpec((tm, DM), lambda i, k: (i, 0)),
                  pl.BlockSpec((tm, DM), lambda i, k: (i, 0)),
                  pl.BlockSpec((1, DM), lambda i, k: (0, 0))],
        out_specs=[pl.BlockSpec((tm, DM), lambda i, k: (i, 0)),
                   pl.BlockSpec((8, DM), lambda i, k: (0, 0))],
        out_shape=[S((T, DM), F32), S((8, DM), F32)],
        scratch_shapes=[pltpu.VMEM((tm, DM), F32)],
        compiler_params=_cp(ARB, ARB),
    )(dmain, dba, w_main, w_ba, x, dh, norm_w)


def _prep_a_bwd_pointwise(dq, dk, dv, c, H, D):
    T = c.shape[0]
    AW = H * D
    C3 = 3 * AW
    tb = _tile(T, 256, 8)
    scale = float(D) ** -0.5

    def body(dq_ref, dk_ref, dv_ref, c_ref, dc_ref):
        for h in range(H):
            for part, d_ref, sc in ((0, dq_ref, scale), (1, dk_ref, 1.0)):
                sl = slice(part * AW + h * D, part * AW + (h + 1) * D)
                cv = c_ref[:, sl]
                raw = _silu(cv)
                rs = lax.rsqrt(jnp.sum(raw * raw, axis=-1, keepdims=True) + EPS)
                nrm = raw * rs
                dn = d_ref[:, h * D:(h + 1) * D] * sc
                draw = rs * (dn - nrm * jnp.sum(dn * nrm, axis=-1, keepdims=True))
                dc_ref[:, sl] = draw * _dsilu(cv)
        dc_ref[:, 2 * AW:] = dv_ref[...] * _dsilu(c_ref[:, 2 * AW:])

    return pl.pallas_call(
        body, name="prep_a_bwd_pointwise", grid=(T // tb,),
        in_specs=[pl.BlockSpec((tb, AW), lambda i: (i, 0)),
                  pl.BlockSpec((tb, AW), lambda i: (i, 0)),
                  pl.BlockSpec((tb, AW), lambda i: (i, 0)),
                  pl.BlockSpec((tb, C3), lambda i: (i, 0))],
        out_specs=pl.BlockSpec((tb, C3), lambda i: (i, 0)),
        out_shape=S((T, C3), F32),
        compiler_params=_cp(ARB),
    )(dq, dk, dv, c)


def _conv_bwd(dc, proj, conv_w, C3):
    T = dc.shape[0]
    tb = _tile(T, 256, 8)
    nblk = T // tb
    r8 = tb // 8

    def body(dc_ref, dnext_ref, x_ref, halo_ref, cw_ref, dx_ref, gcw_ref):
        i = pl.program_id(0)

        @pl.when(i == 0)
        def _():
            gcw_ref[...] = jnp.zeros_like(gcw_ref)

        dcv = dc_ref[...]
        dnext = dnext_ref[...] * (i < nblk - 1).astype(F32)
        dcp = jnp.concatenate([dcv, dnext], axis=0)
        cw = cw_ref[...]
        dx = cw[3:4, :] * dcv
        for j in range(3):
            dx = dx + cw[j:j + 1, :] * dcp[3 - j:3 - j + tb]
        dx_ref[...] = dx.astype(BF16)
        halo = halo_ref[...] * (i > 0).astype(F32)
        xp = jnp.concatenate([halo, x_ref[...]], axis=0)
        for j in range(4):
            gcw_ref[j:j + 1, :] += jnp.sum(dcv * xp[5 + j:5 + j + tb], axis=0, keepdims=True)

    return pl.pallas_call(
        body, name="conv_bwd", grid=(nblk,),
        in_specs=[pl.BlockSpec((tb, C3), lambda i: (i, 0)),
                  pl.BlockSpec((8, C3), lambda i: (jnp.minimum((i + 1) * r8, T // 8 - 1), 0)),
                  pl.BlockSpec((tb, C3), lambda i: (i, 0)),
                  pl.BlockSpec((8, C3), lambda i: (jnp.maximum(i * r8 - 1, 0), 0)),
                  pl.BlockSpec((4, C3), lambda i: (0, 0))],
        out_specs=[pl.BlockSpec((tb, C3), lambda i: (i, 0)),
                   pl.BlockSpec((8, C3), lambda i: (0, 0))],
        out_shape=[S((T, C3), BF16), S((8, C3), F32)],
        compiler_params=_cp(ARB),
    )(dc, dc, proj, proj, conv_w)


def _adam_math(w, g, m, v):
    m2 = ADAM_B1 * m + (1.0 - ADAM_B1) * g
    v2 = ADAM_B2 * v + (1.0 - ADAM_B2) * (g * g)
    m_hat = m2 / (1.0 - ADAM_B1 ** ADAM_STEP)
    v_hat = v2 / (1.0 - ADAM_B2 ** ADAM_STEP)
    delta = -ADAM_LR * (m_hat / (jnp.sqrt(v_hat) + ADAM_EPS) + ADAM_WD * w)
    return delta, m2, v2


def _pair_sum(a, b, name):
    K, R, C = a.shape
    tr = _tile(R, 256, 16)

    def body(a_ref, b_ref, o_ref):
        o_ref[...] = (a_ref[...].astype(F32) + b_ref[...].astype(F32)).astype(BF16)

    spec = lambda: pl.BlockSpec((1, tr, C), lambda k, i: (k, i, 0))
    return pl.pallas_call(
        body, name=name, grid=(K, R // tr), in_specs=[spec(), spec()], out_specs=spec(),
        out_shape=S((K, R, C), BF16), compiler_params=_cp(ARB, ARB),
    )(a, b)


def _sum_adam(own, recv, w, m, v, name):
    R, C = w.shape
    tr = _tile(R, 256, 16)

    def body(own_ref, r_ref, w_ref, m_ref, v_ref, g_ref, d_ref, m2_ref, v2_ref):
        g = own_ref[...].astype(F32)
        for j in range(3):
            g = g + r_ref[j].astype(F32)
        g_ref[...] = g
        d_ref[...], m2_ref[...], v2_ref[...] = _adam_math(w_ref[...], g, m_ref[...], v_ref[...])

    spec = lambda: pl.BlockSpec((tr, C), lambda i: (i, 0))
    return pl.pallas_call(
        body, name=name, grid=(R // tr,),
        in_specs=[spec(), pl.BlockSpec((3, tr, C), lambda i: (0, i, 0)), spec(), spec(), spec()],
        out_specs=[spec(), spec(), spec(), spec()],
        out_shape=[S((R, C), F32)] * 4, compiler_params=_cp(ARB),
    )(own, recv, w, m, v)


def _adam_small(w, g, m, v):
    R, C = w.shape
    tr = _tile(R, 512, 8)

    def body(w_ref, g_ref, m_ref, v_ref, d_ref, m2_ref, v2_ref):
        d_ref[...], m2_ref[...], v2_ref[...] = _adam_math(w_ref[...], g_ref[...], m_ref[...], v_ref[...])

    spec = lambda: pl.BlockSpec((tr, C), lambda i: (i, 0))
    return pl.pallas_call(
        body, name="adam_small", grid=(R // tr,), in_specs=[spec()] * 4, out_specs=[spec()] * 3,
        out_shape=[S((R, C), F32)] * 3, compiler_params=_cp(ARB),
    )(w, g, m, v)


def _position():
    return lax.axis_index("x"), lax.axis_index("y"), lax.axis_index("c")


def _all_gather_weights(arrs):
    na = len(arrs)

    def body(*refs):
        ins, outs = refs[:na], refs[na:2 * na]
        send_sems, recv_sems, local_sems = refs[2 * na:]
        x, y, c = _position()
        me, sibling = (x, y, c), (x, y, 1 - c)
        chips = [(1 - x, y), (x, 1 - y), (1 - x, 1 - y)]

        def slot(a, p):
            return outs[a].at[4 * p[0] + 2 * p[1] + p[2]]

        def copy(a, kk, block, to, src=None):
            return pltpu.make_async_remote_copy(
                src_ref=slot(a, block) if src is None else src, dst_ref=slot(a, block),
                send_sem=send_sems.at[a, kk], recv_sem=recv_sems.at[a, kk], device_id=to, device_id_type=MESH)

        mine = [pltpu.make_async_copy(ins[a], slot(a, me), local_sems.at[a]) for a in range(na)]
        for cp in mine:
            cp.start()
        first = []
        for a in range(na):
            first.append(copy(a, 0, me, sibling, src=ins[a]))
            first += [copy(a, 1 + j, me, (*chip, c), src=ins[a]) for j, chip in enumerate(chips)]
        for cp in first:
            cp.start()
        passed = []
        for j, chip in enumerate(chips):
            for a in range(na):
                copy(a, 1 + j, (*chip, c), me).wait_recv()
                fw = copy(a, 4 + j, (*chip, c), sibling)
                fw.start()
                passed.append(fw)
        for a in range(na):
            copy(a, 0, sibling, me).wait_recv()
            for j, chip in enumerate(chips):
                copy(a, 4 + j, (*chip, 1 - c), me).wait_recv()
        for cp in first + passed:
            cp.wait_send()
        for cp in mine:
            cp.wait()

    any_spec = pl.BlockSpec(memory_space=pl.ANY)
    return pl.pallas_call(
        body, name="all_gather_weights",
        in_specs=[any_spec] * na, out_specs=[any_spec] * na,
        out_shape=[S((N_DEV,) + a.shape, a.dtype) for a in arrs],
        scratch_shapes=[pltpu.SemaphoreType.DMA((na, 7)), pltpu.SemaphoreType.DMA((na, 7)),
                        pltpu.SemaphoreType.DMA((na,))],
    )(*arrs)


def _exchange_sibling(arrs):
    na = len(arrs)

    def body(*refs):
        ins, outs = refs[:na], refs[na:2 * na]
        send_sems, recv_sems = refs[2 * na:]
        x, y, c = _position()
        cps = [pltpu.make_async_remote_copy(src_ref=ins[a], dst_ref=outs[a], send_sem=send_sems.at[a],
                                            recv_sem=recv_sems.at[a], device_id=(x, y, 1 - c), device_id_type=MESH)
               for a in range(na)]
        for cp in cps:
            cp.start()
        for cp in cps:
            cp.wait()

    any_spec = pl.BlockSpec(memory_space=pl.ANY)
    return pl.pallas_call(
        body, name="exchange_sibling", in_specs=[any_spec] * na, out_specs=[any_spec] * na,
        out_shape=[S(a.shape, a.dtype) for a in arrs],
        scratch_shapes=[pltpu.SemaphoreType.DMA((na,)), pltpu.SemaphoreType.DMA((na,))],
    )(*arrs)


def _exchange_chips(arrs):
    na = len(arrs)

    def body(*refs):
        ins, outs = refs[:na], refs[na:2 * na]
        send_sems, recv_sems = refs[2 * na:]
        x, y, c = _position()
        chips = [(1 - x, y), (x, 1 - y), (1 - x, 1 - y)]
        cps = []
        for a in range(na):
            for j, (px, py) in enumerate(chips):
                cps.append(pltpu.make_async_remote_copy(
                    src_ref=ins[a].at[2 * px + py], dst_ref=outs[a].at[j], send_sem=send_sems.at[a, j],
                    recv_sem=recv_sems.at[a, j], device_id=(px, py, c), device_id_type=MESH))
        for cp in cps:
            cp.start()
        for cp in cps:
            cp.wait()

    any_spec = pl.BlockSpec(memory_space=pl.ANY)
    return pl.pallas_call(
        body, name="exchange_chips", in_specs=[any_spec] * na, out_specs=[any_spec] * na,
        out_shape=[S((3,) + a.shape[1:], a.dtype) for a in arrs],
        scratch_shapes=[pltpu.SemaphoreType.DMA((na, 3)), pltpu.SemaphoreType.DMA((na, 3))],
    )(*arrs)


def _all_reduce_small(part):
    R, C = part.shape

    def body(p_ref, out_ref, gath_ref, send_sems, recv_sems):
        x, y, c = _position()
        me = 4 * x + 2 * y + c
        gath_ref[me] = p_ref[...]
        cps = []
        for k in range(1, N_DEV):
            px = 1 - x if k & 4 else x
            py = 1 - y if k & 2 else y
            pc = 1 - c if k & 1 else c
            cps.append(pltpu.make_async_remote_copy(
                src_ref=p_ref, dst_ref=gath_ref.at[me], send_sem=send_sems.at[k - 1], recv_sem=recv_sems.at[k - 1],
                device_id=(px, py, pc), device_id_type=MESH))
        for cp in cps:
            cp.start()
        for cp in cps:
            cp.wait()
        acc = gath_ref[0]
        for d in range(1, N_DEV):
            acc = acc + gath_ref[d]
        out_ref[...] = acc

    vm = pl.BlockSpec(memory_space=pltpu.VMEM)
    return pl.pallas_call(
        body, name="all_reduce_small", in_specs=[vm], out_specs=vm, out_shape=S((R, C), F32),
        scratch_shapes=[pltpu.VMEM((N_DEV, R, C), F32), pltpu.SemaphoreType.DMA((N_DEV - 1,)),
                        pltpu.SemaphoreType.DMA((N_DEV - 1,))],
    )(part)


def _pack(parts):
    rows = []
    for p in parts:
        f = p.reshape(-1).astype(F32)
        pad = (-f.shape[0]) % (8 * LANES)
        rows.append(jnp.pad(f, (0, pad)).reshape(-1, LANES))
    return jnp.concatenate(rows, axis=0)


def _unpack(buf, shapes):
    out, r = [], 0
    for shp in shapes:
        n = 1
        for s in shp:
            n *= s
        nr = -(-n // (8 * LANES)) * 8
        out.append(buf[r:r + nr].reshape(-1)[:n].reshape(shp))
        r += nr
    return out


def kernel(x, norm_w, w_in, conv_w, a_log, dt_bias, head_norm_w, sgu_ln_w, sgu_ln_b, w_spatial, b_spatial, w_out, final_norm_w, loss_target, m_norm_w, m_w_in, m_conv_w, m_a_log, m_dt_bias, m_head_norm_w, m_sgu_ln_w, m_sgu_ln_b, m_w_spatial, m_b_spatial, m_w_out, m_final_norm_w, v_norm_w, v_w_in, v_conv_w, v_a_log, v_dt_bias, v_head_norm_w, v_sgu_ln_w, v_sgu_ln_b, v_w_spatial, v_b_spatial, v_w_out, v_final_norm_w):
    T, DM = x.shape[1], x.shape[2]
    H, D = a_log.shape[1], head_norm_w.shape[1]
    G, P = w_spatial.shape[1], w_spatial.shape[2]
    AW, BW = H * D, G * P
    MIX = AW + BW
    WD = w_in.shape[2]
    IN = N_DEV * WD
    RO = w_out.shape[1]
    CW = conv_w.shape[2]
    sizes = (3 * AW, AW, H, H, BW, BW, BW)
    assert sum(sizes) == IN and 2 * H <= LANES and 3 * H <= 32 and N_DEV * RO == MIX and N_DEV * CW == 3 * AW
    offs = [0]
    for s in sizes:
        offs.append(offs[-1] + s)
    px, py, pc = _position()
    dev = 4 * px + 2 * py + pc
    chip = 2 * px + py

    x2, tgt = x[0], loss_target[0]

    g_win, g_wout, g_conv = _all_gather_weights(
        [w_in[0].astype(BF16), w_out[0].astype(BF16), conv_w[0]])
    w_full = g_win.transpose(1, 0, 2).reshape(DM, IN)
    w_main = jnp.concatenate([w_full[:, :offs[2]], w_full[:, offs[4]:]], axis=1)
    w_ba = jnp.pad(w_full[:, offs[2]:offs[4]], ((0, 0), (0, LANES - 2 * H)))
    w_out_full = g_wout.reshape(MIX, DM)
    conv_full = g_conv.transpose(1, 0, 2).reshape(4, 3 * AW)
    alog_row = jnp.pad(a_log, ((0, 0), (H, LANES - 2 * H)))
    dtb_row = jnp.pad(dt_bias, ((0, 0), (H, LANES - 2 * H)))
    bs_t = b_spatial[0].T

    proj, ba, xn = _in_proj(x2, norm_w, w_main, w_ba)
    q, k, v, c, gcol, grow = _prep_a_fwd(proj, ba, conv_full, alog_row, dtb_row, H, D)
    o, vnew, ssave, asave = _delta_fwd(q, k, v, gcol, grow, H, D)
    ocat = _mix_fwd(o, proj, head_norm_w, sgu_ln_w, sgu_ln_b, w_spatial[0], bs_t, H, D, G, P)
    dh, d_ocat, loss_acc, g_fnw = _out_proj_loss(ocat, w_out_full, x2, tgt, final_norm_w.reshape(1, DM))
    loss = lax.psum(loss_acc[0, 0], AXES)

    g_wout_part = _grad_w(ocat, dh, "grad_w_out")
    d_o, d_rest, g_hnw, g_ln, g_wsp, g_bs_t = _mix_bwd(
        d_ocat, o, proj, head_norm_w, sgu_ln_w, sgu_ln_b, w_spatial[0], bs_t, H, D, G, P)
    dq, dk, dv, dgate, dpar = _delta_bwd(q, k, v, gcol, grow, ba, vnew, ssave, asave, d_o, a_log, dt_bias, H, D)
    dc = _prep_a_bwd_pointwise(dq, dk, dv, c, H, D)
    d_qkv, g_conv_part = _conv_bwd(dc, proj, conv_full, 3 * AW)
    dmain = jnp.concatenate([d_qkv, d_rest], axis=1)
    dba = dgate.astype(BF16)
    g_main, g_ba = _grad_w_in(xn, dmain, dba)
    grad_x, g_nw = _dx(dmain, dba, w_main, w_ba, x2, dh, norm_w)

    g_full = jnp.concatenate([g_main[:, :offs[2]], g_ba[:, :2 * H], g_main[:, offs[2]:]], axis=1)
    g_win_blocks = g_full.astype(BF16).reshape(DM, 4, 2, WD).transpose(1, 2, 0, 3)
    g_wout_blocks = g_wout_part.astype(BF16).reshape(4, 2, RO, DM)
    keep = lambda b: lax.dynamic_index_in_dim(b, pc, axis=1, keepdims=False)
    give = lambda b: lax.dynamic_index_in_dim(b, 1 - pc, axis=1, keepdims=False)
    from_sib = _exchange_sibling([give(g_win_blocks), give(g_wout_blocks)])
    chip_win = _pair_sum(keep(g_win_blocks), from_sib[0], "pair_sum_w_in")
    chip_wout = _pair_sum(keep(g_wout_blocks), from_sib[1], "pair_sum_w_out")
    recv_win, recv_wout = _exchange_chips([chip_win, chip_wout])
    own = lambda b: lax.dynamic_index_in_dim(b, chip, axis=0, keepdims=False)
    grad_w_in, delta_w_in, new_m_w_in, new_v_w_in = _sum_adam(
        own(chip_win), recv_win, w_in[0], m_w_in[0], v_w_in[0], "sum_adam_w_in")
    grad_w_out, delta_w_out, new_m_w_out, new_v_w_out = _sum_adam(
        own(chip_wout), recv_wout, w_out[0], m_w_out[0], v_w_out[0], "sum_adam_w_out")

    small_shapes = [norm_w.shape, a_log.shape, dt_bias.shape, head_norm_w.shape, sgu_ln_w.shape, sgu_ln_b.shape,
                    w_spatial.shape, b_spatial.shape, final_norm_w.shape]
    parts = [g_nw[0], dpar[0, :H], dpar[0, H:2 * H], g_hnw[0], g_ln[0], g_ln[1], g_wsp, g_bs_t[:, :G].T, g_fnw[0],
             g_conv_part[:4]]
    red = _all_reduce_small(_pack(parts))
    grads_small = _unpack(red, small_shapes + [(4, 3 * AW)])
    g_conv_full = grads_small.pop()
    grad_conv = lax.dynamic_slice_in_dim(g_conv_full, dev * CW, CW, axis=1)[None]
    small_w = [norm_w, a_log, dt_bias, head_norm_w, sgu_ln_w, sgu_ln_b, w_spatial, b_spatial, final_norm_w, conv_w]
    small_m = [m_norm_w, m_a_log, m_dt_bias, m_head_norm_w, m_sgu_ln_w, m_sgu_ln_b, m_w_spatial, m_b_spatial,
               m_final_norm_w, m_conv_w]
    small_v = [v_norm_w, v_a_log, v_dt_bias, v_head_norm_w, v_sgu_ln_w, v_sgu_ln_b, v_w_spatial, v_b_spatial,
               v_final_norm_w, v_conv_w]
    small_g = grads_small + [grad_conv]
    shapes10 = [w.shape for w in small_w]
    d_p, m_p, v_p = _adam_small(_pack(small_w), _pack(small_g), _pack(small_m), _pack(small_v))
    d_s, m_s, v_s = _unpack(d_p, shapes10), _unpack(m_p, shapes10), _unpack(v_p, shapes10)

    def order(small, win, wout):
        return [small[0], win[None], small[9], small[1], small[2], small[3], small[4], small[5], small[6], small[7],
                wout[None], small[8]]

    grads = order(small_g, grad_w_in, grad_w_out)
    deltas = order(d_s, delta_w_in, delta_w_out)
    new_m = order(m_s, new_m_w_in, new_m_w_out)
    new_v = order(v_s, new_v_w_in, new_v_w_out)
    return (loss, grad_x[None], *grads, *deltas, *new_m, *new_v)
```

```python
import functools

import jax
import jax.numpy as jnp
from jax import lax
from jax.experimental import pallas as pl
from jax.experimental.pallas import tpu as pltpu

F32 = jnp.float32
BF16 = jnp.bfloat16
MXU = jnp.bfloat16
HI = lax.Precision.HIGHEST
EPS = 1e-6
CHUNK_A = 64
LANES = 128
MESH = pl.DeviceIdType.MESH
AXES = ("x", "y", "c")
N_DEV = 8

ADAM_LR = 0.001
ADAM_B1 = 0.9
ADAM_B2 = 0.999
ADAM_EPS = 1e-08
ADAM_WD = 0.01
ADAM_STEP = 10

S = jax.ShapeDtypeStruct
ARB = "arbitrary"


def _cp(*sem):
    return pltpu.CompilerParams(dimension_semantics=tuple(sem), vmem_limit_bytes=56 * 1024 * 1024)


def _tile(n, cap, mult):
    best = None
    t = mult
    while t <= min(n, cap):
        if n % t == 0:
            best = t
        t += mult
    return best if best is not None else n


def _mm(a, b):
    return jnp.dot(a.astype(MXU), b.astype(MXU), preferred_element_type=F32)


def _mm_nt(a, b):
    return lax.dot_general(a.astype(MXU), b.astype(MXU), (((1,), (1,)), ((), ())), preferred_element_type=F32)


def _mm_tn(a, b):
    return lax.dot_general(a.astype(MXU), b.astype(MXU), (((0,), (0,)), ((), ())), preferred_element_type=F32)


def _mmh(a, b):
    return jnp.dot(a, b, precision=HI, preferred_element_type=F32)


def _mmh_tn(a, b):
    return lax.dot_general(a, b, (((0,), (0,)), ((), ())), precision=HI, preferred_element_type=F32)


def _sigmoid(x):
    return 1.0 / (1.0 + jnp.exp(-x))


def _silu(x):
    return x * _sigmoid(x)


def _dsilu(x):
    s = _sigmoid(x)
    return s * (1.0 + x * (1.0 - s))


def _softplus(x):
    return jnp.maximum(x, 0.0) + jnp.log(1.0 + jnp.exp(-jnp.abs(x)))


def _pieces(wd, gate_lo, gate_hi, total):
    out = []
    for d in range(N_DEV):
        lo, hi = d * wd, (d + 1) * wd
        for dest, a, b, shift in (("main", 0, gate_lo, 0), ("gate", gate_lo, gate_hi, -gate_lo),
                                  ("main", gate_hi, total, gate_lo - gate_hi)):
            s0, s1 = max(lo, a), min(hi, b)
            if s0 < s1:
                out.append((d, s0 - lo, s1 - lo, dest, s0 + shift))
    return out


def _cast_bf16(a, name):
    R, C = a.shape
    tr = _tile(R, 256, 16)

    def body(a_ref, o_ref):
        o_ref[...] = a_ref[...].astype(BF16)

    spec = pl.BlockSpec((tr, C), lambda i: (i, 0))
    return pl.pallas_call(body, name=name, grid=(R // tr,), in_specs=[spec], out_specs=spec,
                          out_shape=S((R, C), BF16), compiler_params=_cp(ARB))(a)


def _relayout_w(g_win, gate_lo, gate_hi):
    _, DM, WD = g_win.shape
    total = N_DEV * WD
    NM = total - (gate_hi - gate_lo)
    tr = _tile(DM, 256, 16)
    plan = _pieces(WD, gate_lo, gate_hi, total)

    def body(g_ref, main_ref, gate_ref):
        gate_ref[...] = jnp.zeros_like(gate_ref)
        for d, s0, s1, dest, c0 in plan:
            dst = main_ref if dest == "main" else gate_ref
            dst[:, c0:c0 + (s1 - s0)] = g_ref[d, :, s0:s1]

    return pl.pallas_call(
        body, name="relayout_w", grid=(DM // tr,),
        in_specs=[pl.BlockSpec((N_DEV, tr, WD), lambda i: (0, i, 0))],
        out_specs=[pl.BlockSpec((tr, NM), lambda i: (i, 0)), pl.BlockSpec((tr, LANES), lambda i: (i, 0))],
        out_shape=[S((DM, NM), g_win.dtype), S((DM, LANES), g_win.dtype)],
        compiler_params=_cp(ARB),
    )(g_win)


def _relayout_g(g_main, g_gate, WD, gate_lo, gate_hi):
    DM = g_main.shape[0]
    total = N_DEV * WD
    tr = _tile(DM, 256, 16)
    plan = _pieces(WD, gate_lo, gate_hi, total)

    def body(m_ref, gate_ref, out_ref):
        for d, s0, s1, dest, c0 in plan:
            src = m_ref if dest == "main" else gate_ref
            out_ref[d, :, s0:s1] = src[:, c0:c0 + (s1 - s0)].astype(BF16)

    return pl.pallas_call(
        body, name="relayout_g", grid=(DM // tr,),
        in_specs=[pl.BlockSpec((tr, g_main.shape[1]), lambda i: (i, 0)), pl.BlockSpec((tr, LANES), lambda i: (i, 0))],
        out_specs=pl.BlockSpec((N_DEV, tr, WD), lambda i: (0, i, 0)),
        out_shape=S((N_DEV, DM, WD), BF16),
        compiler_params=_cp(ARB),
    )(g_main, g_gate)


def _in_proj(x, norm_w, w_main, w_ba):
    T, DM = x.shape
    NM = w_main.shape[1]
    tm = _tile(T, 512, 8)
    tn = _tile(NM, 1024, LANES)

    def body(x_ref, nw_ref, w_ref, wba_ref, proj_ref, ba_ref, xn_ref):
        @pl.when(pl.program_id(1) == 0)
        def _():
            xv = x_ref[...]
            r = lax.rsqrt(jnp.mean(xv * xv, axis=-1, keepdims=True) + EPS)
            xn = (xv * r * nw_ref[...]).astype(BF16)
            xn_ref[...] = xn
            ba_ref[...] = jnp.dot(xn.astype(MXU), wba_ref[...].astype(MXU), preferred_element_type=F32)

        proj_ref[...] = jnp.dot(xn_ref[...].astype(MXU), w_ref[...].astype(MXU), preferred_element_type=F32)

    return pl.pallas_call(
        body, name="in_proj", grid=(T // tm, NM // tn),
        in_specs=[pl.BlockSpec((tm, DM), lambda i, j: (i, 0)),
                  pl.BlockSpec((1, DM), lambda i, j: (0, 0)),
                  pl.BlockSpec((DM, tn), lambda i, j: (0, j)),
                  pl.BlockSpec((DM, LANES), lambda i, j: (0, 0))],
        out_specs=[pl.BlockSpec((tm, tn), lambda i, j: (i, j)),
                   pl.BlockSpec((tm, LANES), lambda i, j: (i, 0)),
                   pl.BlockSpec((tm, DM), lambda i, j: (i, 0))],
        out_shape=[S((T, NM), F32), S((T, LANES), F32), S((T, DM), BF16)],
        compiler_params=_cp(ARB, ARB),
    )(x, norm_w, w_main, w_ba)


def _prep_a_fwd(proj, ba, conv_w, alog_row, dtb_row, H, D):
    T = proj.shape[0]
    AW = H * D
    C3 = 3 * AW
    tb = _tile(T, 256, CHUNK_A)
    nch = tb // CHUNK_A
    nblk = T // tb
    scale = float(D) ** -0.5

    def body(x_ref, halo_ref, ba_ref, cw_ref, al_ref, dt_ref, q_ref, k_ref, v_ref, c_ref, gcol_ref, grow_ref):
        i = pl.program_id(0)
        xv = x_ref[...]
        halo = halo_ref[...] * (i > 0).astype(F32)
        xp = jnp.concatenate([halo, xv], axis=0)
        cw = cw_ref[...]
        c = cw[0:1, :] * xp[5:5 + tb]
        for j in range(1, 4):
            c = c + cw[j:j + 1, :] * xp[5 + j:5 + j + tb]
        c_ref[...] = c
        a = _silu(c)
        for h in range(H):
            qh = a[:, h * D:(h + 1) * D]
            kh = a[:, AW + h * D:AW + (h + 1) * D]
            qr = lax.rsqrt(jnp.sum(qh * qh, axis=-1, keepdims=True) + EPS)
            kr = lax.rsqrt(jnp.sum(kh * kh, axis=-1, keepdims=True) + EPS)
            q_ref[:, h * D:(h + 1) * D] = qh * (qr * scale)
            k_ref[:, h * D:(h + 1) * D] = kh * kr
        v_ref[...] = a[:, 2 * AW:]

        bav = ba_ref[...]
        lane = lax.broadcasted_iota(jnp.int32, (tb, LANES), 1)
        beta = _sigmoid(bav)
        g = -jnp.exp(al_ref[...]) * _softplus(bav + dt_ref[...])
        gates = jnp.where(lane < H, beta, jnp.where(lane < 2 * H, g, 0.0))
        ri = lax.broadcasted_iota(jnp.int32, (CHUNK_A, CHUNK_A), 0)
        ci = lax.broadcasted_iota(jnp.int32, (CHUNK_A, CHUNK_A), 1)
        tri = (ri >= ci).astype(F32)
        lane_c = lax.broadcasted_iota(jnp.int32, (CHUNK_A, LANES), 1)
        for cc in range(nch):
            gch = gates[cc * CHUNK_A:(cc + 1) * CHUNK_A]
            gc = pltpu.roll(_mmh(tri, gch), H, 1)
            full = jnp.where(lane_c < 2 * H, gch, jnp.where(lane_c < 3 * H, gc, 0.0))
            gcol_ref[cc * CHUNK_A:(cc + 1) * CHUNK_A, :] = full
            grow_ref[cc] = full.T[0:32, :]

    return pl.pallas_call(
        body, name="prep_a_fwd", grid=(nblk,),
        in_specs=[pl.BlockSpec((tb, C3), lambda i: (i, 0)),
                  pl.BlockSpec((8, C3), lambda i: (jnp.maximum(i * (tb // 8) - 1, 0), 0)),
                  pl.BlockSpec((tb, LANES), lambda i: (i, 0)),
                  pl.BlockSpec((4, C3), lambda i: (0, 0)),
                  pl.BlockSpec((1, LANES), lambda i: (0, 0)),
                  pl.BlockSpec((1, LANES), lambda i: (0, 0))],
        out_specs=[pl.BlockSpec((tb, AW), lambda i: (i, 0)),
                   pl.BlockSpec((tb, AW), lambda i: (i, 0)),
                   pl.BlockSpec((tb, AW), lambda i: (i, 0)),
                   pl.BlockSpec((tb, C3), lambda i: (i, 0)),
                   pl.BlockSpec((tb, LANES), lambda i: (i, 0)),
                   pl.BlockSpec((nch, 32, CHUNK_A), lambda i: (i, 0, 0))],
        out_shape=[S((T, AW), F32), S((T, AW), F32), S((T, AW), F32), S((T, C3), F32),
                   S((T, LANES), F32), S((T // CHUNK_A, 32, CHUNK_A), F32)],
        compiler_params=_cp(ARB),
    )(proj, proj, ba, conv_w, alog_row, dtb_row)


_NN = (((1,), (0,)), ((), ()))
_TN = (((0,), (0,)), ((), ()))


def _split(a):
    hi = a.astype(BF16)
    return hi, (a - hi.astype(F32)).astype(BF16)


def _mm3(a, b, dims=_NN):
    ah, al = a if isinstance(a, tuple) else _split(a)
    bh, bl = b if isinstance(b, tuple) else _split(b)
    dg = lambda p, r: lax.dot_general(p, r, dims, preferred_element_type=F32)
    return dg(ah, bh) + (dg(ah, bl) + dg(al, bh))


def _interleave(gens):
    gens = list(gens)
    while gens:
        alive = []
        for g in gens:
            try:
                next(g)
                alive.append(g)
            except StopIteration:
                pass
        gens = alive


def _chunk_terms(q, k, v, gcolv, growv, h, H):
    C = CHUNK_A
    beta_c = gcolv[:, h:h + 1]
    g_c = gcolv[:, H + h:H + h + 1]
    gc_c = gcolv[:, 2 * H + h:2 * H + h + 1]
    gc_r = growv[2 * H + h:2 * H + h + 1, :]
    ri = lax.broadcasted_iota(jnp.int32, (C, C), 0)
    ci = lax.broadcasted_iota(jnp.int32, (C, C), 1)
    incl = ri >= ci
    strict = ri > ci
    kb = k * beta_c
    vb = v * beta_c
    p_raw = _mm_nt(kb, k)
    qk_raw = _mm_nt(q, k)
    gam = jnp.where(incl, jnp.exp(jnp.where(incl, gc_c - gc_r, 0.0)), 0.0)
    e_c = jnp.exp(gc_c)
    gl = gc_r[:, C - 1:C]
    edec = jnp.exp(gl - gc_c)
    yield
    lmat = jnp.where(strict, p_raw * gam, 0.0)
    attn = jnp.where(incl, qk_raw * gam, 0.0)
    return dict(beta_c=beta_c, g_c=g_c, gc_c=gc_c, gc_r=gc_r, incl=incl, strict=strict, gam=gam, e_c=e_c,
                kb=kb, vb=vb, lmat=lmat, attn=attn, gl=gl, edec=edec, ri=ri, ci=ci)


def _inv_unit_lower(lmat):
    C = lmat.shape[0]
    ri = lax.broadcasted_iota(jnp.int32, (C, C), 0)
    ci = lax.broadcasted_iota(jnp.int32, (C, C), 1)
    eye = (ri == ci).astype(F32)
    x = -lmat
    a = eye + x
    n = 1
    while 2 * n < C:
        xs = _split(x)
        x = _mm3(xs, xs)
        yield
        a = a + _mm3(a, x)
        n *= 2
    yield
    return a


def _delta_fwd(q, k, v, gcol, grow, H, D):
    T = q.shape[0]
    C = CHUNK_A
    N = T // C
    AW = H * D

    def body(q_ref, k_ref, v_ref, gcol_ref, grow_ref, o_ref, vn_ref, ssave_ref, asave_ref, s_ref):
        @pl.when(pl.program_id(0) == 0)
        def _():
            s_ref[...] = jnp.zeros_like(s_ref)

        gcolv = gcol_ref[...]
        growv = grow_ref[0]

        def head(h):
            sl = slice(h * D, (h + 1) * D)
            st = s_ref[h]
            ssave_ref[0, h] = st
            qv, kv, vv = q_ref[:, sl], k_ref[:, sl], v_ref[:, sl]
            t = yield from _chunk_terms(qv, kv, vv, gcolv, growv, h, H)
            ks = _mm(t["kb"] * t["e_c"], st)
            o_inter = _mm(qv * t["e_c"], st)
            a = yield from _inv_unit_lower(t["lmat"])
            asave_ref[0, h] = a
            v_new = _mm3(a, t["vb"] - ks)
            yield
            vn_ref[:, sl] = v_new
            o_intra = _mm(t["attn"], v_new)
            s_upd = _mm_tn(kv * t["edec"], v_new)
            yield
            o_ref[:, sl] = o_inter + o_intra
            s_ref[h] = st * jnp.exp(t["gl"]) + s_upd

        _interleave(head(h) for h in range(H))

    blk = lambda: pl.BlockSpec((C, AW), lambda n: (n, 0))
    return pl.pallas_call(
        body, name="delta_fwd", grid=(N,),
        in_specs=[blk(), blk(), blk(),
                  pl.BlockSpec((C, LANES), lambda n: (n, 0)),
                  pl.BlockSpec((1, 32, C), lambda n: (n, 0, 0))],
        out_specs=[blk(), blk(),
                   pl.BlockSpec((1, H, D, D), lambda n: (n, 0, 0, 0)),
                   pl.BlockSpec((1, H, C, C), lambda n: (n, 0, 0, 0))],
        out_shape=[S((T, AW), F32), S((T, AW), F32), S((N, H, D, D), F32), S((N, H, C, C), F32)],
        scratch_shapes=[pltpu.VMEM((H, D, D), F32)],
        compiler_params=_cp(ARB),
    )(q, k, v, gcol, grow)


def _delta_bwd(q, k, v, gcol, grow, ba, vnew, ssave, asave, d_o, a_log, dt_bias, H, D):
    T = q.shape[0]
    C = CHUNK_A
    N = T // C
    AW = H * D

    def body(al_ref, dt_ref, q_ref, k_ref, v_ref, gcol_ref, grow_ref, ba_ref, vn_ref, ss_ref, as_ref, do_ref,
             dq_ref, dk_ref, dv_ref, dgate_ref, dpar_ref, ds_ref):
        @pl.when(pl.program_id(0) == 0)
        def _():
            ds_ref[...] = jnp.zeros_like(ds_ref)
            dpar_ref[...] = jnp.zeros_like(dpar_ref)

        gcolv = gcol_ref[...]
        growv = grow_ref[0]
        bav = ba_ref[...]
        lane = lax.broadcasted_iota(jnp.int32, (C, LANES), 1)
        lane1 = lax.broadcasted_iota(jnp.int32, (1, LANES), 1)
        rowi = lax.broadcasted_iota(jnp.int32, (C, 1), 0)
        acc = {"dgate": jnp.zeros((C, LANES), F32), "dpar": jnp.zeros((1, LANES), F32)}

        def head(h):
            sl = slice(h * D, (h + 1) * D)
            ds_next = ds_ref[h]
            st = ss_ref[0, h]
            a = as_ref[0, h]
            qv, kv, vv, dov, v_new = q_ref[:, sl], k_ref[:, sl], v_ref[:, sl], do_ref[:, sl], vn_ref[:, sl]
            t = yield from _chunk_terms(qv, kv, vv, gcolv, growv, h, H)
            beta_c, e_c, gam, kb = t["beta_c"], t["e_c"], t["gam"], t["kb"]
            incl, strict, attn, lmat, edec = t["incl"], t["strict"], t["attn"], t["lmat"], t["edec"]
            kdec = kv * edec
            egl = jnp.exp(t["gl"])
            qe = qv * e_c
            ekb = kb * e_c

            dkdec = _mm_nt(v_new, ds_next)
            dv_new_s = _mm(kdec, ds_next)
            t1 = _mm_nt(dov, st)
            ds_o = _mm_tn(qe, dov)
            dattn_raw = _mm_nt(dov, v_new)
            dv_new_o = _mm_tn(attn, dov)
            yield
            dgl = egl * jnp.sum(jnp.sum(st * ds_next, axis=1, keepdims=True), axis=0, keepdims=True)
            dk = edec * dkdec
            r = jnp.sum(dkdec * kdec, axis=1, keepdims=True)
            dgc = -r
            dgl = dgl + jnp.sum(r, axis=0, keepdims=True)
            dq = e_c * t1
            dgc = dgc + jnp.sum(t1 * qe, axis=1, keepdims=True)
            dattn = jnp.where(incl, dattn_raw, 0.0)
            dv_new = dv_new_s + dv_new_o
            dqm = dattn * gam
            z = dattn * attn
            dvb = _mm3(a, dv_new, _TN)
            dq_a = _mm(dqm, kv)
            dk_a = _mm_tn(dqm, qv)
            yield
            dq_ref[:, sl] = dq + dq_a
            dv_ref[:, sl] = beta_c * dvb
            ds_kb = _mm_tn(ekb, dvb)
            dekb_neg = _mm_nt(dvb, st)
            dl_neg = _mm_nt(dvb, v_new)
            yield
            ds_ref[h] = egl * ds_next + ds_o - ds_kb
            dekb = -dekb_neg
            dl = jnp.where(strict, -dl_neg, 0.0)
            dp = dl * gam
            z = z + dl * lmat
            dkb_p = _mm(dp, kv)
            dk_p = _mm_tn(dp, kb)
            dgc = dgc + jnp.sum(dekb * ekb, axis=1, keepdims=True)
            dgc = dgc + jnp.sum(z, axis=1, keepdims=True) - jnp.sum(z.T, axis=1, keepdims=True)
            dgc = dgc + jnp.where(rowi == C - 1, dgl, 0.0)
            upper = (t["ri"] <= t["ci"]).astype(F32)
            dg_b = _mm3(upper, jnp.broadcast_to(dgc, (C, LANES)))
            yield
            dkb = dkb_p + e_c * dekb
            dk_ref[:, sl] = dk + dk_a + dk_p + beta_c * dkb
            dbeta = jnp.sum(dkb * kv, axis=1, keepdims=True) + jnp.sum(dvb * vv, axis=1, keepdims=True)
            dg = dg_b[:, 0:1]
            a_raw = bav[:, H + h:H + h + 1]
            d_braw = dbeta * beta_c * (1.0 - beta_c)
            d_araw = dg * (-jnp.exp(al_ref[0, h])) * _sigmoid(a_raw + dt_ref[0, h])
            acc["dgate"] = acc["dgate"] + jnp.where(lane == h, d_braw, 0.0) + jnp.where(lane == H + h, d_araw, 0.0)
            dal = jnp.sum(dg * t["g_c"], axis=0, keepdims=True)
            ddt = jnp.sum(d_araw, axis=0, keepdims=True)
            acc["dpar"] = acc["dpar"] + jnp.where(lane1 == h, dal, 0.0) + jnp.where(lane1 == H + h, ddt, 0.0)

        _interleave(head(h) for h in range(H))
        dgate_ref[...] = acc["dgate"]
        dpar_ref[0:1, :] += acc["dpar"]

    rev = lambda s: N - 1 - s
    blk = lambda: pl.BlockSpec((C, AW), lambda s: (rev(s), 0))
    smem = pl.BlockSpec(memory_space=pltpu.SMEM)
    return pl.pallas_call(
        body, name="delta_bwd", grid=(N,),
        in_specs=[smem, smem, blk(), blk(), blk(),
                  pl.BlockSpec((C, LANES), lambda s: (rev(s), 0)),
                  pl.BlockSpec((1, 32, C), lambda s: (rev(s), 0, 0)),
                  pl.BlockSpec((C, LANES), lambda s: (rev(s), 0)),
                  blk(),
                  pl.BlockSpec((1, H, D, D), lambda s: (rev(s), 0, 0, 0)),
                  pl.BlockSpec((1, H, C, C), lambda s: (rev(s), 0, 0, 0)),
                  blk()],
        out_specs=[blk(), blk(), blk(),
                   pl.BlockSpec((C, LANES), lambda s: (rev(s), 0)),
                   pl.BlockSpec((8, LANES), lambda s: (0, 0))],
        out_shape=[S((T, AW), F32), S((T, AW), F32), S((T, AW), F32),
                   S((T, LANES), F32), S((8, LANES), F32)],
        scratch_shapes=[pltpu.VMEM((H, D, D), F32)],
        compiler_params=_cp(ARB),
    )(a_log, dt_bias, q, k, v, gcol, grow, ba, vnew, ssave, asave, d_o)


def _ln_stats(xv):
    mu = jnp.mean(xv, axis=-1, keepdims=True)
    xc = xv - mu
    var = jnp.mean(xc * xc, axis=-1, keepdims=True)
    rstd = lax.rsqrt(var + EPS)
    return xc * rstd, rstd


def _mix_fwd(o, proj, head_norm_w, ln_w, ln_b, w_sp, bs_t, H, D, G, P):
    T = o.shape[0]
    AW, BW = H * D, G * P
    MIX = AW + BW
    nb = AW // BW if AW % BW == 0 else None
    assert nb == 1, "group widths must match the projection column blocks"
    cb = 3

    def body(o_ref, za_ref, ub_ref, vb_ref, zb_ref, hw_ref, lw_ref, lb_ref, w_ref, bs_ref, out_ref):
        hw = hw_ref[...]
        for h in range(H):
            sl = slice(h * D, (h + 1) * D)
            oh = o_ref[:, sl]
            rs = lax.rsqrt(jnp.mean(oh * oh, axis=-1, keepdims=True) + EPS)
            out_ref[:, sl] = (oh * rs * hw * _silu(za_ref[:, sl])).astype(BF16)
        xhat, _ = _ln_stats(vb_ref[...])
        vn = xhat * lw_ref[...] + lb_ref[...]
        ri = lax.broadcasted_iota(jnp.int32, (P, P), 0)
        ci = lax.broadcasted_iota(jnp.int32, (P, P), 1)
        bsv = bs_ref[...]
        for g in range(G):
            sl = slice(g * P, (g + 1) * P)
            wm = jnp.where(ri >= ci, w_ref[g], 0.0)
            s = _mm(wm, vn[:, sl]) + bsv[:, g:g + 1]
            out_ref[:, AW + g * P:AW + (g + 1) * P] = (ub_ref[:, sl] * s * _silu(zb_ref[:, sl])).astype(BF16)

    row = lambda w: pl.BlockSpec((1, w), lambda i: (0, 0))
    return pl.pallas_call(
        body, name="mix_fwd", grid=(T // P,),
        in_specs=[pl.BlockSpec((P, AW), lambda i: (i, 0)),
                  pl.BlockSpec((P, AW), lambda i: (i, cb)),
                  pl.BlockSpec((P, BW), lambda i: (i, cb + 1)),
                  pl.BlockSpec((P, BW), lambda i: (i, cb + 2)),
                  pl.BlockSpec((P, BW), lambda i: (i, cb + 3)),
                  row(D), row(BW), row(BW),
                  pl.BlockSpec((G, P, P), lambda i: (0, 0, 0)),
                  pl.BlockSpec((P, G), lambda i: (0, 0))],
        out_specs=pl.BlockSpec((P, MIX), lambda i: (i, 0)),
        out_shape=S((T, MIX), BF16),
        compiler_params=_cp(ARB),
    )(o, proj, proj, proj, proj, head_norm_w, ln_w, ln_b, w_sp, bs_t)


def _mix_bwd(d_ocat, o, proj, head_norm_w, ln_w, ln_b, w_sp, bs_t, H, D, G, P):
    T = o.shape[0]
    AW, BW = H * D, G * P
    MIX = AW + BW
    cb = 3

    def body(dc_ref, o_ref, za_ref, ub_ref, vb_ref, zb_ref, hw_ref, lw_ref, lb_ref, w_ref, bs_ref,
             do_ref, dmain_ref, dhw_ref, dln_ref, dw_ref, dbs_ref, dvn_ref, drest_ref):
        j = pl.program_id(1)

        @pl.when((pl.program_id(0) == 0) & (j == 0))
        def _():
            dhw_ref[...] = jnp.zeros_like(dhw_ref)
            dln_ref[...] = jnp.zeros_like(dln_ref)
            dw_ref[...] = jnp.zeros_like(dw_ref)
            dbs_ref[...] = jnp.zeros_like(dbs_ref)

        @pl.when(j == 0)
        def _():
            compute(dc_ref, o_ref, za_ref, ub_ref, vb_ref, zb_ref, hw_ref, lw_ref, lb_ref, w_ref, bs_ref,
                    do_ref, dhw_ref, dln_ref, dw_ref, dbs_ref, dvn_ref, drest_ref)

        dmain_ref[...] = drest_ref[j]

    def compute(dc_ref, o_ref, za_ref, ub_ref, vb_ref, zb_ref, hw_ref, lw_ref, lb_ref, w_ref, bs_ref,
                do_ref, dhw_ref, dln_ref, dw_ref, dbs_ref, dvn_ref, drest_ref):
        hw = hw_ref[...]
        dhw = jnp.zeros((1, D), F32)
        for h in range(H):
            sl = slice(h * D, (h + 1) * D)
            oh = o_ref[:, sl]
            za = za_ref[:, sl]
            doa = dc_ref[:, sl]
            rs = lax.rsqrt(jnp.mean(oh * oh, axis=-1, keepdims=True) + EPS)
            xh = oh * rs
            d_on = doa * _silu(za)
            drest_ref[0, :, sl] = (doa * (xh * hw) * _dsilu(za)).astype(BF16)
            dhw = dhw + jnp.sum(d_on * xh, axis=0, keepdims=True)
            dxh = d_on * hw
            do_ref[:, sl] = rs * (dxh - xh * jnp.mean(dxh * xh, axis=-1, keepdims=True))
        dhw_ref[0:1, :] += dhw

        xhat, rstd = _ln_stats(vb_ref[...])
        lw = lw_ref[...]
        vn = xhat * lw + lb_ref[...]
        ri = lax.broadcasted_iota(jnp.int32, (P, P), 0)
        ci = lax.broadcasted_iota(jnp.int32, (P, P), 1)
        lane = lax.broadcasted_iota(jnp.int32, (P, LANES), 1)
        bsv = bs_ref[...]
        dbs = jnp.zeros((P, LANES), F32)
        for g in range(G):
            sl = slice(g * P, (g + 1) * P)
            wm = jnp.where(ri >= ci, w_ref[g], 0.0)
            vng = vn[:, sl]
            s = _mm(wm, vng) + bsv[:, g:g + 1]
            dob = dc_ref[:, AW + g * P:AW + (g + 1) * P]
            ub = ub_ref[:, sl]
            zb = zb_ref[:, sl]
            szb = _silu(zb)
            drest_ref[1, :, sl] = (dob * s * szb).astype(BF16)
            drest_ref[3, :, sl] = (dob * ub * s * _dsilu(zb)).astype(BF16)
            ds = dob * ub * szb
            dvn_ref[:, sl] = _mm_tn(wm, ds)
            dw_ref[g] += jnp.where(ri >= ci, _mm_nt(ds, vng), 0.0)
            dbs = dbs + jnp.where(lane == g, jnp.sum(ds, axis=1, keepdims=True), 0.0)
        dbs_ref[...] += dbs
        dvn = dvn_ref[...]
        dln_ref[0:1, :] += jnp.sum(dvn * xhat, axis=0, keepdims=True)
        dln_ref[1:2, :] += jnp.sum(dvn, axis=0, keepdims=True)
        dxh = dvn * lw
        dvb = rstd * (dxh - jnp.mean(dxh, axis=-1, keepdims=True) - xhat * jnp.mean(dxh * xhat, axis=-1, keepdims=True))
        drest_ref[2] = dvb.astype(BF16)

    assert AW == BW
    row = lambda w: pl.BlockSpec((1, w), lambda i, j: (0, 0))
    return pl.pallas_call(
        body, name="mix_bwd", grid=(T // P, 4),
        in_specs=[pl.BlockSpec((P, MIX), lambda i, j: (i, 0)),
                  pl.BlockSpec((P, AW), lambda i, j: (i, 0)),
                  pl.BlockSpec((P, AW), lambda i, j: (i, cb)),
                  pl.BlockSpec((P, BW), lambda i, j: (i, cb + 1)),
                  pl.BlockSpec((P, BW), lambda i, j: (i, cb + 2)),
                  pl.BlockSpec((P, BW), lambda i, j: (i, cb + 3)),
                  row(D), row(BW), row(BW),
                  pl.BlockSpec((G, P, P), lambda i, j: (0, 0, 0)),
                  pl.BlockSpec((P, G), lambda i, j: (0, 0))],
        out_specs=[pl.BlockSpec((P, AW), lambda i, j: (i, 0)),
                   pl.BlockSpec((P, AW), lambda i, j: (i, cb + j)),
                   pl.BlockSpec((8, D), lambda i, j: (0, 0)),
                   pl.BlockSpec((8, BW), lambda i, j: (0, 0)),
                   pl.BlockSpec((G, P, P), lambda i, j: (0, 0, 0)),
                   pl.BlockSpec((P, LANES), lambda i, j: (0, 0))],
        out_shape=[S((T, AW), F32), S((T, cb * AW + AW + 3 * BW), BF16), S((8, D), F32), S((8, BW), F32),
                   S((G, P, P), F32), S((P, LANES), F32)],
        scratch_shapes=[pltpu.VMEM((P, BW), F32), pltpu.VMEM((4, P, AW), BF16)],
        compiler_params=_cp(ARB, ARB),
    )(d_ocat, o, proj, proj, proj, proj, head_norm_w, ln_w, ln_b, w_sp, bs_t)


def _out_proj_loss(ocat, w_out, x, target, fnw):
    T, MIX = ocat.shape
    DM = x.shape[1]
    tm = _tile(T, 256, 8)

    def body(oc_ref, w_ref, x_ref, t_ref, fw_ref, dh_ref, doc_ref, loss_ref, gfw_ref):
        @pl.when(pl.program_id(0) == 0)
        def _():
            loss_ref[...] = jnp.zeros_like(loss_ref)
            gfw_ref[...] = jnp.zeros_like(gfw_ref)

        wv = w_ref[...]
        hh = x_ref[...] + jnp.dot(oc_ref[...].astype(MXU), wv.astype(MXU), preferred_element_type=F32)
        rs = lax.rsqrt(jnp.mean(hh * hh, axis=-1, keepdims=True) + EPS)
        hn = hh * rs
        fw = fw_ref[...]
        e = hn * fw - t_ref[...]
        row_loss = 0.5 * jnp.mean(e * e, axis=-1, keepdims=True)
        loss_ref[...] += jnp.sum(row_loss, axis=0, keepdims=True)
        dy = e * (1.0 / DM)
        gfw_ref[0:1, :] += jnp.sum(dy * hn, axis=0, keepdims=True)
        dhn = dy * fw
        dh = rs * (dhn - hn * jnp.mean(dhn * hn, axis=-1, keepdims=True))
        dh_ref[...] = dh
        doc_ref[...] = _mm_nt(dh, wv)

    return pl.pallas_call(
        body, name="out_proj_loss", grid=(T // tm,),
        in_specs=[pl.BlockSpec((tm, MIX), lambda i: (i, 0)),
                  pl.BlockSpec((MIX, DM), lambda i: (0, 0)),
                  pl.BlockSpec((tm, DM), lambda i: (i, 0)),
                  pl.BlockSpec((tm, DM), lambda i: (i, 0)),
                  pl.BlockSpec((1, DM), lambda i: (0, 0))],
        out_specs=[pl.BlockSpec((tm, DM), lambda i: (i, 0)),
                   pl.BlockSpec((tm, MIX), lambda i: (i, 0)),
                   pl.BlockSpec((8, LANES), lambda i: (0, 0)),
                   pl.BlockSpec((8, DM), lambda i: (0, 0))],
        out_shape=[S((T, DM), F32), S((T, MIX), F32), S((8, LANES), F32), S((8, DM), F32)],
        compiler_params=_cp(ARB),
    )(ocat, w_out, x, target, fnw)


def _grad_w(lhs, rhs, name):
    T, A = lhs.shape
    B = rhs.shape[1]
    ta = _tile(A, 512, LANES)
    tk = _tile(T, 512, 8)

    def body(l_ref, r_ref, out_ref):
        @pl.when(pl.program_id(1) == 0)
        def _():
            out_ref[...] = jnp.zeros_like(out_ref)
        out_ref[...] += _mm_tn(l_ref[...], r_ref[...])

    return pl.pallas_call(
        body, name=name, grid=(A // ta, T // tk),
        in_specs=[pl.BlockSpec((tk, ta), lambda i, k: (k, i)),
                  pl.BlockSpec((tk, B), lambda i, k: (k, 0))],
        out_specs=pl.BlockSpec((ta, B), lambda i, k: (i, 0)),
        out_shape=S((A, B), F32),
        compiler_params=_cp(ARB, ARB),
    )(lhs, rhs)


def _grad_w_in(xn, dmain, dba):
    T, DM = xn.shape
    NM = dmain.shape[1]
    tn = _tile(NM, 1024, LANES)
    tk = _tile(T, 512, 8)

    def body(xn_ref, dm_ref, dba_ref, gm_ref, gba_ref):
        j = pl.program_id(0)
        k = pl.program_id(1)

        @pl.when(k == 0)
        def _():
            gm_ref[...] = jnp.zeros_like(gm_ref)

        @pl.when((k == 0) & (j == 0))
        def _():
            gba_ref[...] = jnp.zeros_like(gba_ref)

        xv = xn_ref[...]
        gm_ref[...] += _mm_tn(xv, dm_ref[...])

        @pl.when(j == 0)
        def _():
            gba_ref[...] += _mm_tn(xv, dba_ref[...])

    return pl.pallas_call(
        body, name="grad_w_in", grid=(NM // tn, T // tk),
        in_specs=[pl.BlockSpec((tk, DM), lambda j, k: (k, 0)),
                  pl.BlockSpec((tk, tn), lambda j, k: (k, j)),
                  pl.BlockSpec((tk, LANES), lambda j, k: (k, 0))],
        out_specs=[pl.BlockSpec((DM, tn), lambda j, k: (0, j)),
                   pl.BlockSpec((DM, LANES), lambda j, k: (0, 0))],
        out_shape=[S((DM, NM), F32), S((DM, LANES), F32)],
        compiler_params=_cp(ARB, ARB),
    )(xn, dmain, dba)


def _dx(dmain, dba, w_main, w_ba, x, dh, norm_w):
    T, NM = dmain.shape
    DM = x.shape[1]
    tm = _tile(T, 512, 8)
    tk = _tile(NM, 1024, LANES)
    nk = NM // tk

    def body(dm_ref, dba_ref, w_ref, wba_ref, x_ref, dh_ref, nw_ref, gx_ref, gnw_ref, acc_ref):
        i = pl.program_id(0)
        k = pl.program_id(1)

        @pl.when((i == 0) & (k == 0))
        def _():
            gnw_ref[...] = jnp.zeros_like(gnw_ref)

        @pl.when(k == 0)
        def _():
            acc_ref[...] = _mm_nt(dba_ref[...], wba_ref[...])

        acc_ref[...] += _mm_nt(dm_ref[...], w_ref[...])

        @pl.when(k == nk - 1)
        def _():
            xv = x_ref[...]
            rs = lax.rsqrt(jnp.mean(xv * xv, axis=-1, keepdims=True) + EPS)
            xh = xv * rs
            dxn = acc_ref[...]
            gnw_ref[0:1, :] += jnp.sum(dxn * xh, axis=0, keepdims=True)
            dxh = dxn * nw_ref[...]
            gx_ref[...] = dh_ref[...] + rs * (dxh - xh * jnp.mean(dxh * xh, axis=-1, keepdims=True))

    return pl.pallas_call(
        body, name="dx", grid=(T // tm, nk),
        in_specs=[pl.BlockSpec((tm, tk), lambda i, k: (i, k)),
                  pl.BlockSpec((tm, LANES), lambda i, k: (i, 0)),
                  pl.BlockSpec((DM, tk), lambda i, k: (0, k)),
                  pl.BlockSpec((DM, LANES), lambda i, k: (0, 0)),
                  pl.BlockSpec((tm, DM), lambda i, k: (i, 0)),
                  pl.BlockSpec((tm, DM), lambda i, k: (i, 0)),
                  pl.BlockSpec((1, DM), lambda i, k: (0, 0))],
        out_specs=[pl.BlockSpec((tm, DM), lambda i, k: (i, 0)),
                   pl.BlockSpec((8, DM), lambda i, k: (0, 0))],
        out_shape=[S((T, DM), F32), S((8, DM), F32)],
        scratch_shapes=[pltpu.VMEM((tm, DM), F32)],
        compiler_params=_cp(ARB, ARB),
    )(dmain, dba, w_main, w_ba, x, dh, norm_w)


def _prep_a_bwd_pointwise(dq, dk, dv, c, H, D):
    T = c.shape[0]
    AW = H * D
    C3 = 3 * AW
    tb = _tile(T, 256, 8)
    scale = float(D) ** -0.5

    def body(dq_ref, dk_ref, dv_ref, c_ref, dc_ref):
        for h in range(H):
            for part, d_ref, sc in ((0, dq_ref, scale), (1, dk_ref, 1.0)):
                sl = slice(part * AW + h * D, part * AW + (h + 1) * D)
                cv = c_ref[:, sl]
                raw = _silu(cv)
                rs = lax.rsqrt(jnp.sum(raw * raw, axis=-1, keepdims=True) + EPS)
                nrm = raw * rs
                dn = d_ref[:, h * D:(h + 1) * D] * sc
                draw = rs * (dn - nrm * jnp.sum(dn * nrm, axis=-1, keepdims=True))
                dc_ref[:, sl] = draw * _dsilu(cv)
        dc_ref[:, 2 * AW:] = dv_ref[...] * _dsilu(c_ref[:, 2 * AW:])

    return pl.pallas_call(
        body, name="prep_a_bwd_pointwise", grid=(T // tb,),
        in_specs=[pl.BlockSpec((tb, AW), lambda i: (i, 0)),
                  pl.BlockSpec((tb, AW), lambda i: (i, 0)),
                  pl.BlockSpec((tb, AW), lambda i: (i, 0)),
                  pl.BlockSpec((tb, C3), lambda i: (i, 0))],
        out_specs=pl.BlockSpec((tb, C3), lambda i: (i, 0)),
        out_shape=S((T, C3), F32),
        compiler_params=_cp(ARB),
    )(dq, dk, dv, c)


def _conv_bwd(dc, proj, conv_w, dmain, C3):
    T = dc.shape[0]
    tb = _tile(T, 256, 8)
    nblk = T // tb
    r8 = tb // 8

    def body(dc_ref, dnext_ref, x_ref, halo_ref, cw_ref, dmain_in_ref, dx_ref, gcw_ref):
        del dmain_in_ref
        i = pl.program_id(0)

        @pl.when(i == 0)
        def _():
            gcw_ref[...] = jnp.zeros_like(gcw_ref)

        dcv = dc_ref[...]
        dnext = dnext_ref[...] * (i < nblk - 1).astype(F32)
        dcp = jnp.concatenate([dcv, dnext], axis=0)
        cw = cw_ref[...]
        dx = cw[3:4, :] * dcv
        for j in range(3):
            dx = dx + cw[j:j + 1, :] * dcp[3 - j:3 - j + tb]
        dx_ref[...] = dx.astype(BF16)
        halo = halo_ref[...] * (i > 0).astype(F32)
        xp = jnp.concatenate([halo, x_ref[...]], axis=0)
        for j in range(4):
            gcw_ref[j:j + 1, :] += jnp.sum(dcv * xp[5 + j:5 + j + tb], axis=0, keepdims=True)

    return pl.pallas_call(
        body, name="conv_bwd", grid=(nblk,),
        in_specs=[pl.BlockSpec((tb, C3), lambda i: (i, 0)),
                  pl.BlockSpec((8, C3), lambda i: (jnp.minimum((i + 1) * r8, T // 8 - 1), 0)),
                  pl.BlockSpec((tb, C3), lambda i: (i, 0)),
                  pl.BlockSpec((8, C3), lambda i: (jnp.maximum(i * r8 - 1, 0), 0)),
                  pl.BlockSpec((4, C3), lambda i: (0, 0)),
                  pl.BlockSpec(memory_space=pl.ANY)],
        out_specs=[pl.BlockSpec((tb, C3), lambda i: (i, 0)),
                   pl.BlockSpec((8, C3), lambda i: (0, 0))],
        out_shape=[S(dmain.shape, dmain.dtype), S((8, C3), F32)],
        input_output_aliases={5: 0},
        compiler_params=_cp(ARB),
    )(dc, dc, proj, proj, conv_w, dmain)


def _adam_math(w, g, m, v):
    m2 = ADAM_B1 * m + (1.0 - ADAM_B1) * g
    v2 = ADAM_B2 * v + (1.0 - ADAM_B2) * (g * g)
    m_hat = m2 / (1.0 - ADAM_B1 ** ADAM_STEP)
    v_hat = v2 / (1.0 - ADAM_B2 ** ADAM_STEP)
    delta = -ADAM_LR * (m_hat / (jnp.sqrt(v_hat) + ADAM_EPS) + ADAM_WD * w)
    return delta, m2, v2


def _pair_sum(blocks, recv, core, name):
    K, _, R, C = blocks.shape
    tr = _tile(R, 256, 16)

    def body(core_ref, a_ref, b_ref, o_ref):
        del core_ref
        o_ref[0] = (a_ref[0, 0].astype(F32) + b_ref[0].astype(F32)).astype(BF16)

    spec = lambda: pl.BlockSpec((1, tr, C), lambda k, i, core_ref: (k, i, 0))
    return pl.pallas_call(
        body, name=name,
        grid_spec=pltpu.PrefetchScalarGridSpec(
            num_scalar_prefetch=1, grid=(K, R // tr),
            in_specs=[pl.BlockSpec((1, 1, tr, C), lambda k, i, core_ref: (k, core_ref[0], i, 0)), spec()],
            out_specs=spec()),
        out_shape=S((K, R, C), BF16), compiler_params=_cp(ARB, ARB),
    )(core, blocks, recv)


def _sum_adam(chip_sums, recv, w, m, v, chip, name):
    R, C = w.shape
    tr = _tile(R, 256, 16)

    def body(chip_ref, own_ref, r_ref, w_ref, m_ref, v_ref, g_ref, d_ref, m2_ref, v2_ref):
        del chip_ref
        g = own_ref[0].astype(F32)
        for j in range(3):
            g = g + r_ref[j].astype(F32)
        g_ref[...] = g
        d_ref[...], m2_ref[...], v2_ref[...] = _adam_math(w_ref[...], g, m_ref[...], v_ref[...])

    spec = lambda: pl.BlockSpec((tr, C), lambda i, chip_ref: (i, 0))
    return pl.pallas_call(
        body, name=name,
        grid_spec=pltpu.PrefetchScalarGridSpec(
            num_scalar_prefetch=1, grid=(R // tr,),
            in_specs=[pl.BlockSpec((1, tr, C), lambda i, chip_ref: (chip_ref[0], i, 0)),
                      pl.BlockSpec((3, tr, C), lambda i, chip_ref: (0, i, 0)), spec(), spec(), spec()],
            out_specs=[spec(), spec(), spec(), spec()]),
        out_shape=[S((R, C), F32)] * 4, compiler_params=_cp(ARB),
    )(chip, chip_sums, recv, w, m, v)


def _adam_small(w, g, m, v):
    R, C = w.shape
    tr = _tile(R, 512, 8)

    def body(w_ref, g_ref, m_ref, v_ref, d_ref, m2_ref, v2_ref):
        d_ref[...], m2_ref[...], v2_ref[...] = _adam_math(w_ref[...], g_ref[...], m_ref[...], v_ref[...])

    spec = lambda: pl.BlockSpec((tr, C), lambda i: (i, 0))
    return pl.pallas_call(
        body, name="adam_small", grid=(R // tr,), in_specs=[spec()] * 4, out_specs=[spec()] * 3,
        out_shape=[S((R, C), F32)] * 3, compiler_params=_cp(ARB),
    )(w, g, m, v)


def _position():
    return lax.axis_index("x"), lax.axis_index("y"), lax.axis_index("c")


def _all_gather_weights(arrs):
    na = len(arrs)

    def body(*refs):
        ins, outs = refs[:na], refs[na:2 * na]
        send_sems, recv_sems, local_sems = refs[2 * na:]
        x, y, c = _position()
        me, sibling = (x, y, c), (x, y, 1 - c)
        chips = [(1 - x, y), (x, 1 - y), (1 - x, 1 - y)]

        def slot(a, p):
            return outs[a].at[4 * p[0] + 2 * p[1] + p[2]]

        def copy(a, kk, block, to, src=None):
            return pltpu.make_async_remote_copy(
                src_ref=slot(a, block) if src is None else src, dst_ref=slot(a, block),
                send_sem=send_sems.at[a, kk], recv_sem=recv_sems.at[a, kk], device_id=to, device_id_type=MESH)

        mine = [pltpu.make_async_copy(ins[a], slot(a, me), local_sems.at[a]) for a in range(na)]
        for cp in mine:
            cp.start()
        first = []
        for a in range(na):
            first.append(copy(a, 0, me, sibling, src=ins[a]))
            first += [copy(a, 1 + j, me, (*chip, c), src=ins[a]) for j, chip in enumerate(chips)]
        for cp in first:
            cp.start()
        passed = []
        for j, chip in enumerate(chips):
            for a in range(na):
                copy(a, 1 + j, (*chip, c), me).wait_recv()
                fw = copy(a, 4 + j, (*chip, c), sibling)
                fw.start()
                passed.append(fw)
        for a in range(na):
            copy(a, 0, sibling, me).wait_recv()
            for j, chip in enumerate(chips):
                copy(a, 4 + j, (*chip, 1 - c), me).wait_recv()
        for cp in first + passed:
            cp.wait_send()
        for cp in mine:
            cp.wait()

    any_spec = pl.BlockSpec(memory_space=pl.ANY)
    return pl.pallas_call(
        body, name="all_gather_weights",
        in_specs=[any_spec] * na, out_specs=[any_spec] * na,
        out_shape=[S((N_DEV,) + a.shape, a.dtype) for a in arrs],
        scratch_shapes=[pltpu.SemaphoreType.DMA((na, 7)), pltpu.SemaphoreType.DMA((na, 7)),
                        pltpu.SemaphoreType.DMA((na,))],
    )(*arrs)


def _exchange_sibling(arrs):
    na = len(arrs)
    K = arrs[0].shape[0]

    def body(*refs):
        ins, outs = refs[:na], refs[na:2 * na]
        send_sems, recv_sems = refs[2 * na:]
        x, y, c = _position()
        cps = [pltpu.make_async_remote_copy(src_ref=ins[a].at[k, 1 - c], dst_ref=outs[a].at[k],
                                            send_sem=send_sems.at[a, k], recv_sem=recv_sems.at[a, k],
                                            device_id=(x, y, 1 - c), device_id_type=MESH)
               for a in range(na) for k in range(K)]
        for cp in cps:
            cp.start()
        for cp in cps:
            cp.wait()

    any_spec = pl.BlockSpec(memory_space=pl.ANY)
    return pl.pallas_call(
        body, name="exchange_sibling", in_specs=[any_spec] * na, out_specs=[any_spec] * na,
        out_shape=[S((K,) + a.shape[2:], a.dtype) for a in arrs],
        scratch_shapes=[pltpu.SemaphoreType.DMA((na, K)), pltpu.SemaphoreType.DMA((na, K))],
    )(*arrs)


def _exchange_chips(arrs):
    na = len(arrs)

    def body(*refs):
        ins, outs = refs[:na], refs[na:2 * na]
        send_sems, recv_sems = refs[2 * na:]
        x, y, c = _position()
        chips = [(1 - x, y), (x, 1 - y), (1 - x, 1 - y)]
        cps = []
        for a in range(na):
            for j, (px, py) in enumerate(chips):
                cps.append(pltpu.make_async_remote_copy(
                    src_ref=ins[a].at[2 * px + py], dst_ref=outs[a].at[j], send_sem=send_sems.at[a, j],
                    recv_sem=recv_sems.at[a, j], device_id=(px, py, c), device_id_type=MESH))
        for cp in cps:
            cp.start()
        for cp in cps:
            cp.wait()

    any_spec = pl.BlockSpec(memory_space=pl.ANY)
    return pl.pallas_call(
        body, name="exchange_chips", in_specs=[any_spec] * na, out_specs=[any_spec] * na,
        out_shape=[S((3,) + a.shape[1:], a.dtype) for a in arrs],
        scratch_shapes=[pltpu.SemaphoreType.DMA((na, 3)), pltpu.SemaphoreType.DMA((na, 3))],
    )(*arrs)


def _all_reduce_small(part):
    R, C = part.shape

    def body(p_ref, out_ref, gath_ref, send_sems, recv_sems):
        x, y, c = _position()
        me = 4 * x + 2 * y + c
        gath_ref[me] = p_ref[...]
        cps = []
        for k in range(1, N_DEV):
            px = 1 - x if k & 4 else x
            py = 1 - y if k & 2 else y
            pc = 1 - c if k & 1 else c
            cps.append(pltpu.make_async_remote_copy(
                src_ref=p_ref, dst_ref=gath_ref.at[me], send_sem=send_sems.at[k - 1], recv_sem=recv_sems.at[k - 1],
                device_id=(px, py, pc), device_id_type=MESH))
        for cp in cps:
            cp.start()
        for cp in cps:
            cp.wait()
        acc = gath_ref[0]
        for d in range(1, N_DEV):
            acc = acc + gath_ref[d]
        out_ref[...] = acc

    vm = pl.BlockSpec(memory_space=pltpu.VMEM)
    return pl.pallas_call(
        body, name="all_reduce_small", in_specs=[vm], out_specs=vm, out_shape=S((R, C), F32),
        scratch_shapes=[pltpu.VMEM((N_DEV, R, C), F32), pltpu.SemaphoreType.DMA((N_DEV - 1,)),
                        pltpu.SemaphoreType.DMA((N_DEV - 1,))],
    )(part)


def _pack(parts):
    rows = []
    for p in parts:
        f = p.reshape(-1).astype(F32)
        pad = (-f.shape[0]) % (8 * LANES)
        rows.append(jnp.pad(f, (0, pad)).reshape(-1, LANES))
    return jnp.concatenate(rows, axis=0)


def _unpack(buf, shapes):
    out, r = [], 0
    for shp in shapes:
        n = 1
        for s in shp:
            n *= s
        nr = -(-n // (8 * LANES)) * 8
        out.append(buf[r:r + nr].reshape(-1)[:n].reshape(shp))
        r += nr
    return out


def kernel(x, norm_w, w_in, conv_w, a_log, dt_bias, head_norm_w, sgu_ln_w, sgu_ln_b, w_spatial, b_spatial, w_out, final_norm_w, loss_target, m_norm_w, m_w_in, m_conv_w, m_a_log, m_dt_bias, m_head_norm_w, m_sgu_ln_w, m_sgu_ln_b, m_w_spatial, m_b_spatial, m_w_out, m_final_norm_w, v_norm_w, v_w_in, v_conv_w, v_a_log, v_dt_bias, v_head_norm_w, v_sgu_ln_w, v_sgu_ln_b, v_w_spatial, v_b_spatial, v_w_out, v_final_norm_w):
    T, DM = x.shape[1], x.shape[2]
    H, D = a_log.shape[1], head_norm_w.shape[1]
    G, P = w_spatial.shape[1], w_spatial.shape[2]
    AW, BW = H * D, G * P
    MIX = AW + BW
    WD = w_in.shape[2]
    IN = N_DEV * WD
    RO = w_out.shape[1]
    CW = conv_w.shape[2]
    sizes = (3 * AW, AW, H, H, BW, BW, BW)
    assert sum(sizes) == IN and 2 * H <= LANES and 3 * H <= 32 and N_DEV * RO == MIX and N_DEV * CW == 3 * AW
    offs = [0]
    for s in sizes:
        offs.append(offs[-1] + s)
    px, py, pc = _position()
    dev = 4 * px + 2 * py + pc
    chip = 2 * px + py

    x2, tgt = x[0], loss_target[0]

    g_win, g_wout, g_conv = _all_gather_weights(
        [_cast_bf16(w_in[0], "cast_w_in"), _cast_bf16(w_out[0], "cast_w_out"), conv_w[0]])
    w_main, w_ba = _relayout_w(g_win, offs[2], offs[4])
    w_out_full = g_wout.reshape(MIX, DM)
    conv_full = g_conv.transpose(1, 0, 2).reshape(4, 3 * AW)
    alog_row = jnp.pad(a_log, ((0, 0), (H, LANES - 2 * H)))
    dtb_row = jnp.pad(dt_bias, ((0, 0), (H, LANES - 2 * H)))
    bs_t = b_spatial[0].T

    proj, ba, xn = _in_proj(x2, norm_w, w_main, w_ba)
    q, k, v, c, gcol, grow = _prep_a_fwd(proj, ba, conv_full, alog_row, dtb_row, H, D)
    o, vnew, ssave, asave = _delta_fwd(q, k, v, gcol, grow, H, D)
    ocat = _mix_fwd(o, proj, head_norm_w, sgu_ln_w, sgu_ln_b, w_spatial[0], bs_t, H, D, G, P)
    dh, d_ocat, loss_acc, g_fnw = _out_proj_loss(ocat, w_out_full, x2, tgt, final_norm_w.reshape(1, DM))
    loss = lax.psum(loss_acc[0, 0], AXES)

    g_wout_part = _grad_w(ocat, dh, "grad_w_out")
    d_o, dmain, g_hnw, g_ln, g_wsp, g_bs_t = _mix_bwd(
        d_ocat, o, proj, head_norm_w, sgu_ln_w, sgu_ln_b, w_spatial[0], bs_t, H, D, G, P)
    dq, dk, dv, dgate, dpar = _delta_bwd(q, k, v, gcol, grow, ba, vnew, ssave, asave, d_o, a_log, dt_bias, H, D)
    dc = _prep_a_bwd_pointwise(dq, dk, dv, c, H, D)
    dmain, g_conv_part = _conv_bwd(dc, proj, conv_full, dmain, 3 * AW)
    dba = dgate.astype(BF16)
    g_main, g_ba = _grad_w_in(xn, dmain, dba)
    grad_x, g_nw = _dx(dmain, dba, w_main, w_ba, x2, dh, norm_w)

    g_win_blocks = _relayout_g(g_main, g_ba, WD, offs[2], offs[4]).reshape(4, 2, DM, WD)
    g_wout_blocks = _cast_bf16(g_wout_part, "cast_g_w_out").reshape(4, 2, RO, DM)
    core_idx = jnp.reshape(pc, (1,)).astype(jnp.int32)
    chip_idx = jnp.reshape(chip, (1,)).astype(jnp.int32)
    from_sib = _exchange_sibling([g_win_blocks, g_wout_blocks])
    chip_win = _pair_sum(g_win_blocks, from_sib[0], core_idx, "pair_sum_w_in")
    chip_wout = _pair_sum(g_wout_blocks, from_sib[1], core_idx, "pair_sum_w_out")
    recv_win, recv_wout = _exchange_chips([chip_win, chip_wout])
    grad_w_in, delta_w_in, new_m_w_in, new_v_w_in = _sum_adam(
        chip_win, recv_win, w_in[0], m_w_in[0], v_w_in[0], chip_idx, "sum_adam_w_in")
    grad_w_out, delta_w_out, new_m_w_out, new_v_w_out = _sum_adam(
        chip_wout, recv_wout, w_out[0], m_w_out[0], v_w_out[0], chip_idx, "sum_adam_w_out")

    small_shapes = [norm_w.shape, a_log.shape, dt_bias.shape, head_norm_w.shape, sgu_ln_w.shape, sgu_ln_b.shape,
                    w_spatial.shape, b_spatial.shape, final_norm_w.shape]
    parts = [g_nw[0], dpar[0, :H], dpar[0, H:2 * H], g_hnw[0], g_ln[0], g_ln[1], g_wsp, g_bs_t[:, :G].T, g_fnw[0],
             g_conv_part[:4]]
    red = _all_reduce_small(_pack(parts))
    grads_small = _unpack(red, small_shapes + [(4, 3 * AW)])
    g_conv_full = grads_small.pop()
    grad_conv = lax.dynamic_slice_in_dim(g_conv_full, dev * CW, CW, axis=1)[None]
    small_w = [norm_w, a_log, dt_bias, head_norm_w, sgu_ln_w, sgu_ln_b, w_spatial, b_spatial, final_norm_w, conv_w]
    small_m = [m_norm_w, m_a_log, m_dt_bias, m_head_norm_w, m_sgu_ln_w, m_sgu_ln_b, m_w_spatial, m_b_spatial,
               m_final_norm_w, m_conv_w]
    small_v = [v_norm_w, v_a_log, v_dt_bias, v_head_norm_w, v_sgu_ln_w, v_sgu_ln_b, v_w_spatial, v_b_spatial,
               v_final_norm_w, v_conv_w]
    small_g = grads_small + [grad_conv]
    shapes10 = [w.shape for w in small_w]
    d_p, m_p, v_p = _adam_small(_pack(small_w), _pack(small_g), _pack(small_m), _pack(small_v))
    d_s, m_s, v_s = _unpack(d_p, shapes10), _unpack(m_p, shapes10), _unpack(v_p, shapes10)

    def order(small, win, wout):
        return [small[0], win[None], small[9], small[1], small[2], small[3], small[4], small[5], small[6], small[7],
                wout[None], small[8]]

    grads = order(small_g, grad_w_in, grad_w_out)
    deltas = order(d_s, delta_w_in, delta_w_out)
    new_m = order(m_s, new_m_w_in, new_m_w_out)
    new_v = order(v_s, new_v_w_in, new_v_w_out)
    return (loss, grad_x[None], *grads, *deltas, *new_m, *new_v)
```

```python
import functools

import jax
import jax.numpy as jnp
from jax import lax
from jax.experimental import pallas as pl
from jax.experimental.pallas import tpu as pltpu

F32 = jnp.float32
BF16 = jnp.bfloat16
MXU = jnp.bfloat16
HI = lax.Precision.HIGHEST
EPS = 1e-6
CHUNK_A = 64
LANES = 128
MESH = pl.DeviceIdType.MESH
AXES = ("x", "y", "c")
N_DEV = 8

ADAM_LR = 0.001
ADAM_B1 = 0.9
ADAM_B2 = 0.999
ADAM_EPS = 1e-08
ADAM_WD = 0.01
ADAM_STEP = 10

S = jax.ShapeDtypeStruct
ARB = "arbitrary"


def _cp(*sem):
    return pltpu.CompilerParams(dimension_semantics=tuple(sem), vmem_limit_bytes=56 * 1024 * 1024)


def _tile(n, cap, mult):
    best = None
    t = mult
    while t <= min(n, cap):
        if n % t == 0:
            best = t
        t += mult
    return best if best is not None else n


def _mm(a, b):
    return jnp.dot(a.astype(MXU), b.astype(MXU), preferred_element_type=F32)


def _mm_nt(a, b):
    return lax.dot_general(a.astype(MXU), b.astype(MXU), (((1,), (1,)), ((), ())), preferred_element_type=F32)


def _mm_tn(a, b):
    return lax.dot_general(a.astype(MXU), b.astype(MXU), (((0,), (0,)), ((), ())), preferred_element_type=F32)


def _mmh(a, b):
    return jnp.dot(a, b, precision=HI, preferred_element_type=F32)


def _mmh_tn(a, b):
    return lax.dot_general(a, b, (((0,), (0,)), ((), ())), precision=HI, preferred_element_type=F32)


def _sigmoid(x):
    return 1.0 / (1.0 + jnp.exp(-x))


def _silu(x):
    return x * _sigmoid(x)


def _dsilu(x):
    s = _sigmoid(x)
    return s * (1.0 + x * (1.0 - s))


def _softplus(x):
    return jnp.maximum(x, 0.0) + jnp.log(1.0 + jnp.exp(-jnp.abs(x)))


def _pieces(wd, gate_lo, gate_hi, total):
    out = []
    for d in range(N_DEV):
        lo, hi = d * wd, (d + 1) * wd
        for dest, a, b, shift in (("main", 0, gate_lo, 0), ("gate", gate_lo, gate_hi, -gate_lo),
                                  ("main", gate_hi, total, gate_lo - gate_hi)):
            s0, s1 = max(lo, a), min(hi, b)
            if s0 < s1:
                out.append((d, s0 - lo, s1 - lo, dest, s0 + shift))
    return out


def _cast_bf16(a, name):
    R, C = a.shape
    tr = _tile(R, 256, 16)

    def body(a_ref, o_ref):
        o_ref[...] = a_ref[...].astype(BF16)

    spec = pl.BlockSpec((tr, C), lambda i: (i, 0))
    return pl.pallas_call(body, name=name, grid=(R // tr,), in_specs=[spec], out_specs=spec,
                          out_shape=S((R, C), BF16), compiler_params=_cp(ARB))(a)


def _relayout_w(g_win, gate_lo, gate_hi):
    _, DM, WD = g_win.shape
    total = N_DEV * WD
    NM = total - (gate_hi - gate_lo)
    tr = _tile(DM, 256, 16)
    plan = _pieces(WD, gate_lo, gate_hi, total)

    def body(g_ref, main_ref, gate_ref):
        gate_ref[...] = jnp.zeros_like(gate_ref)
        for d, s0, s1, dest, c0 in plan:
            dst = main_ref if dest == "main" else gate_ref
            dst[:, c0:c0 + (s1 - s0)] = g_ref[d, :, s0:s1]

    return pl.pallas_call(
        body, name="relayout_w", grid=(DM // tr,),
        in_specs=[pl.BlockSpec((N_DEV, tr, WD), lambda i: (0, i, 0))],
        out_specs=[pl.BlockSpec((tr, NM), lambda i: (i, 0)), pl.BlockSpec((tr, LANES), lambda i: (i, 0))],
        out_shape=[S((DM, NM), g_win.dtype), S((DM, LANES), g_win.dtype)],
        compiler_params=_cp(ARB),
    )(g_win)


def _relayout_g(g_main, g_gate, WD, gate_lo, gate_hi):
    DM = g_main.shape[0]
    total = N_DEV * WD
    tr = _tile(DM, 256, 16)
    plan = _pieces(WD, gate_lo, gate_hi, total)

    def body(m_ref, gate_ref, out_ref):
        for d, s0, s1, dest, c0 in plan:
            src = m_ref if dest == "main" else gate_ref
            out_ref[d, :, s0:s1] = src[:, c0:c0 + (s1 - s0)].astype(BF16)

    return pl.pallas_call(
        body, name="relayout_g", grid=(DM // tr,),
        in_specs=[pl.BlockSpec((tr, g_main.shape[1]), lambda i: (i, 0)), pl.BlockSpec((tr, LANES), lambda i: (i, 0))],
        out_specs=pl.BlockSpec((N_DEV, tr, WD), lambda i: (0, i, 0)),
        out_shape=S((N_DEV, DM, WD), BF16),
        compiler_params=_cp(ARB),
    )(g_main, g_gate)


def _in_proj(x, norm_w, w_main, w_ba):
    T, DM = x.shape
    NM = w_main.shape[1]
    tm = _tile(T, 1024, 8)
    tn = _tile(NM, 1024, LANES)

    def body(x_ref, nw_ref, w_ref, wba_ref, proj_ref, ba_ref, xn_ref):
        @pl.when(pl.program_id(1) == 0)
        def _():
            xv = x_ref[...]
            r = lax.rsqrt(jnp.mean(xv * xv, axis=-1, keepdims=True) + EPS)
            xn = (xv * r * nw_ref[...]).astype(BF16)
            xn_ref[...] = xn
            ba_ref[...] = jnp.dot(xn.astype(MXU), wba_ref[...].astype(MXU), preferred_element_type=F32)

        proj_ref[...] = jnp.dot(xn_ref[...].astype(MXU), w_ref[...].astype(MXU), preferred_element_type=F32)

    return pl.pallas_call(
        body, name="in_proj", grid=(T // tm, NM // tn),
        in_specs=[pl.BlockSpec((tm, DM), lambda i, j: (i, 0)),
                  pl.BlockSpec((1, DM), lambda i, j: (0, 0)),
                  pl.BlockSpec((DM, tn), lambda i, j: (0, j)),
                  pl.BlockSpec((DM, LANES), lambda i, j: (0, 0))],
        out_specs=[pl.BlockSpec((tm, tn), lambda i, j: (i, j)),
                   pl.BlockSpec((tm, LANES), lambda i, j: (i, 0)),
                   pl.BlockSpec((tm, DM), lambda i, j: (i, 0))],
        out_shape=[S((T, NM), F32), S((T, LANES), F32), S((T, DM), BF16)],
        compiler_params=_cp(ARB, ARB),
    )(x, norm_w, w_main, w_ba)


def _prep_a_fwd(proj, ba, conv_w, alog_row, dtb_row, H, D):
    T = proj.shape[0]
    AW = H * D
    C3 = 3 * AW
    tb = _tile(T, 256, CHUNK_A)
    nch = tb // CHUNK_A
    nblk = T // tb
    scale = float(D) ** -0.5

    def body(x_ref, halo_ref, ba_ref, cw_ref, al_ref, dt_ref, q_ref, k_ref, v_ref, c_ref, gcol_ref, grow_ref):
        i = pl.program_id(0)
        xv = x_ref[...]
        halo = halo_ref[...] * (i > 0).astype(F32)
        xp = jnp.concatenate([halo, xv], axis=0)
        cw = cw_ref[...]
        c = cw[0:1, :] * xp[5:5 + tb]
        for j in range(1, 4):
            c = c + cw[j:j + 1, :] * xp[5 + j:5 + j + tb]
        c_ref[...] = c
        a = _silu(c)
        for h in range(H):
            qh = a[:, h * D:(h + 1) * D]
            kh = a[:, AW + h * D:AW + (h + 1) * D]
            qr = lax.rsqrt(jnp.sum(qh * qh, axis=-1, keepdims=True) + EPS)
            kr = lax.rsqrt(jnp.sum(kh * kh, axis=-1, keepdims=True) + EPS)
            q_ref[:, h * D:(h + 1) * D] = qh * (qr * scale)
            k_ref[:, h * D:(h + 1) * D] = kh * kr
        v_ref[...] = a[:, 2 * AW:]

        bav = ba_ref[...]
        lane = lax.broadcasted_iota(jnp.int32, (tb, LANES), 1)
        beta = _sigmoid(bav)
        g = -jnp.exp(al_ref[...]) * _softplus(bav + dt_ref[...])
        gates = jnp.where(lane < H, beta, jnp.where(lane < 2 * H, g, 0.0))
        ri = lax.broadcasted_iota(jnp.int32, (CHUNK_A, CHUNK_A), 0)
        ci = lax.broadcasted_iota(jnp.int32, (CHUNK_A, CHUNK_A), 1)
        tri = (ri >= ci).astype(F32)
        lane_c = lax.broadcasted_iota(jnp.int32, (CHUNK_A, LANES), 1)
        for cc in range(nch):
            gch = gates[cc * CHUNK_A:(cc + 1) * CHUNK_A]
            gc = pltpu.roll(_mmh(tri, gch), H, 1)
            full = jnp.where(lane_c < 2 * H, gch, jnp.where(lane_c < 3 * H, gc, 0.0))
            gcol_ref[cc * CHUNK_A:(cc + 1) * CHUNK_A, :] = full
            grow_ref[cc] = full.T[0:32, :]

    return pl.pallas_call(
        body, name="prep_a_fwd", grid=(nblk,),
        in_specs=[pl.BlockSpec((tb, C3), lambda i: (i, 0)),
                  pl.BlockSpec((8, C3), lambda i: (jnp.maximum(i * (tb // 8) - 1, 0), 0)),
                  pl.BlockSpec((tb, LANES), lambda i: (i, 0)),
                  pl.BlockSpec((4, C3), lambda i: (0, 0)),
                  pl.BlockSpec((1, LANES), lambda i: (0, 0)),
                  pl.BlockSpec((1, LANES), lambda i: (0, 0))],
        out_specs=[pl.BlockSpec((tb, AW), lambda i: (i, 0)),
                   pl.BlockSpec((tb, AW), lambda i: (i, 0)),
                   pl.BlockSpec((tb, AW), lambda i: (i, 0)),
                   pl.BlockSpec((tb, C3), lambda i: (i, 0)),
                   pl.BlockSpec((tb, LANES), lambda i: (i, 0)),
                   pl.BlockSpec((nch, 32, CHUNK_A), lambda i: (i, 0, 0))],
        out_shape=[S((T, AW), F32), S((T, AW), F32), S((T, AW), F32), S((T, C3), F32),
                   S((T, LANES), F32), S((T // CHUNK_A, 32, CHUNK_A), F32)],
        compiler_params=_cp(ARB),
    )(proj, proj, ba, conv_w, alog_row, dtb_row)


_NN = (((1,), (0,)), ((), ()))
_TN = (((0,), (0,)), ((), ()))


def _split(a):
    hi = a.astype(BF16)
    return hi, (a - hi.astype(F32)).astype(BF16)


def _mm3(a, b, dims=_NN):
    ah, al = a if isinstance(a, tuple) else _split(a)
    bh, bl = b if isinstance(b, tuple) else _split(b)
    dg = lambda p, r: lax.dot_general(p, r, dims, preferred_element_type=F32)
    return dg(ah, bh) + (dg(ah, bl) + dg(al, bh))


def _interleave(gens):
    gens = list(gens)
    while gens:
        alive = []
        for g in gens:
            try:
                next(g)
                alive.append(g)
            except StopIteration:
                pass
        gens = alive


def _chunk_terms(q, k, v, gcolv, growv, h, H):
    C = CHUNK_A
    beta_c = gcolv[:, h:h + 1]
    g_c = gcolv[:, H + h:H + h + 1]
    gc_c = gcolv[:, 2 * H + h:2 * H + h + 1]
    gc_r = growv[2 * H + h:2 * H + h + 1, :]
    ri = lax.broadcasted_iota(jnp.int32, (C, C), 0)
    ci = lax.broadcasted_iota(jnp.int32, (C, C), 1)
    incl = ri >= ci
    strict = ri > ci
    kb = k * beta_c
    vb = v * beta_c
    p_raw = _mm_nt(kb, k)
    qk_raw = _mm_nt(q, k)
    gam = jnp.where(incl, jnp.exp(jnp.where(incl, gc_c - gc_r, 0.0)), 0.0)
    e_c = jnp.exp(gc_c)
    gl = gc_r[:, C - 1:C]
    edec = jnp.exp(gl - gc_c)
    yield
    lmat = jnp.where(strict, p_raw * gam, 0.0)
    attn = jnp.where(incl, qk_raw * gam, 0.0)
    return dict(beta_c=beta_c, g_c=g_c, gc_c=gc_c, gc_r=gc_r, incl=incl, strict=strict, gam=gam, e_c=e_c,
                kb=kb, vb=vb, lmat=lmat, attn=attn, gl=gl, edec=edec, ri=ri, ci=ci)


def _inv_unit_lower(lmat):
    C = lmat.shape[0]
    ri = lax.broadcasted_iota(jnp.int32, (C, C), 0)
    ci = lax.broadcasted_iota(jnp.int32, (C, C), 1)
    eye = (ri == ci).astype(F32)
    x = -lmat
    a = eye + x
    n = 1
    while 2 * n < C:
        xs = _split(x)
        x = _mm3(xs, xs)
        yield
        a = a + _mm3(a, x)
        n *= 2
    yield
    return a


def _delta_fwd(q, k, v, gcol, grow, H, D):
    T = q.shape[0]
    C = CHUNK_A
    N = T // C
    AW = H * D

    def body(q_ref, k_ref, v_ref, gcol_ref, grow_ref, o_ref, vn_ref, ssave_ref, asave_ref, s_ref):
        @pl.when(pl.program_id(0) == 0)
        def _():
            s_ref[...] = jnp.zeros_like(s_ref)

        gcolv = gcol_ref[...]
        growv = grow_ref[0]

        def head(h):
            sl = slice(h * D, (h + 1) * D)
            st = s_ref[h]
            ssave_ref[0, h] = st
            qv, kv, vv = q_ref[:, sl], k_ref[:, sl], v_ref[:, sl]
            t = yield from _chunk_terms(qv, kv, vv, gcolv, growv, h, H)
            ks = _mm(t["kb"] * t["e_c"], st)
            o_inter = _mm(qv * t["e_c"], st)
            a = yield from _inv_unit_lower(t["lmat"])
            asave_ref[0, h] = a
            v_new = _mm3(a, t["vb"] - ks)
            yield
            vn_ref[:, sl] = v_new
            o_intra = _mm(t["attn"], v_new)
            s_upd = _mm_tn(kv * t["edec"], v_new)
            yield
            o_ref[:, sl] = o_inter + o_intra
            s_ref[h] = st * jnp.exp(t["gl"]) + s_upd

        _interleave(head(h) for h in range(H))

    blk = lambda: pl.BlockSpec((C, AW), lambda n: (n, 0))
    return pl.pallas_call(
        body, name="delta_fwd", grid=(N,),
        in_specs=[blk(), blk(), blk(),
                  pl.BlockSpec((C, LANES), lambda n: (n, 0)),
                  pl.BlockSpec((1, 32, C), lambda n: (n, 0, 0))],
        out_specs=[blk(), blk(),
                   pl.BlockSpec((1, H, D, D), lambda n: (n, 0, 0, 0)),
                   pl.BlockSpec((1, H, C, C), lambda n: (n, 0, 0, 0))],
        out_shape=[S((T, AW), F32), S((T, AW), F32), S((N, H, D, D), F32), S((N, H, C, C), F32)],
        scratch_shapes=[pltpu.VMEM((H, D, D), F32)],
        compiler_params=_cp(ARB),
    )(q, k, v, gcol, grow)


def _delta_bwd(q, k, v, gcol, grow, ba, vnew, ssave, asave, d_o, a_log, dt_bias, H, D):
    T = q.shape[0]
    C = CHUNK_A
    N = T // C
    AW = H * D

    def body(al_ref, dt_ref, q_ref, k_ref, v_ref, gcol_ref, grow_ref, ba_ref, vn_ref, ss_ref, as_ref, do_ref,
             dq_ref, dk_ref, dv_ref, dgate_ref, dpar_ref, ds_ref):
        @pl.when(pl.program_id(0) == 0)
        def _():
            ds_ref[...] = jnp.zeros_like(ds_ref)
            dpar_ref[...] = jnp.zeros_like(dpar_ref)

        gcolv = gcol_ref[...]
        growv = grow_ref[0]
        bav = ba_ref[...]
        lane = lax.broadcasted_iota(jnp.int32, (C, LANES), 1)
        lane1 = lax.broadcasted_iota(jnp.int32, (1, LANES), 1)
        rowi = lax.broadcasted_iota(jnp.int32, (C, 1), 0)
        acc = {"dgate": jnp.zeros((C, LANES), F32), "dpar": jnp.zeros((1, LANES), F32)}

        def head(h):
            sl = slice(h * D, (h + 1) * D)
            ds_next = ds_ref[h]
            st = ss_ref[0, h]
            a = as_ref[0, h]
            qv, kv, vv, dov, v_new = q_ref[:, sl], k_ref[:, sl], v_ref[:, sl], do_ref[:, sl], vn_ref[:, sl]
            t = yield from _chunk_terms(qv, kv, vv, gcolv, growv, h, H)
            beta_c, e_c, gam, kb = t["beta_c"], t["e_c"], t["gam"], t["kb"]
            incl, strict, attn, lmat, edec = t["incl"], t["strict"], t["attn"], t["lmat"], t["edec"]
            kdec = kv * edec
            egl = jnp.exp(t["gl"])
            qe = qv * e_c
            ekb = kb * e_c

            dkdec = _mm_nt(v_new, ds_next)
            dv_new_s = _mm(kdec, ds_next)
            t1 = _mm_nt(dov, st)
            ds_o = _mm_tn(qe, dov)
            dattn_raw = _mm_nt(dov, v_new)
            dv_new_o = _mm_tn(attn, dov)
            yield
            dgl = egl * jnp.sum(jnp.sum(st * ds_next, axis=1, keepdims=True), axis=0, keepdims=True)
            dk = edec * dkdec
            r = jnp.sum(dkdec * kdec, axis=1, keepdims=True)
            dgc = -r
            dgl = dgl + jnp.sum(r, axis=0, keepdims=True)
            dq = e_c * t1
            dgc = dgc + jnp.sum(t1 * qe, axis=1, keepdims=True)
            dattn = jnp.where(incl, dattn_raw, 0.0)
            dv_new = dv_new_s + dv_new_o
            dqm = dattn * gam
            z = dattn * attn
            dvb = _mm3(a, dv_new, _TN)
            dq_a = _mm(dqm, kv)
            dk_a = _mm_tn(dqm, qv)
            yield
            dq_ref[:, sl] = dq + dq_a
            dv_ref[:, sl] = beta_c * dvb
            ds_kb = _mm_tn(ekb, dvb)
            dekb_neg = _mm_nt(dvb, st)
            dl_neg = _mm_nt(dvb, v_new)
            yield
            ds_ref[h] = egl * ds_next + ds_o - ds_kb
            dekb = -dekb_neg
            dl = jnp.where(strict, -dl_neg, 0.0)
            dp = dl * gam
            z = z + dl * lmat
            dkb_p = _mm(dp, kv)
            dk_p = _mm_tn(dp, kb)
            dgc = dgc + jnp.sum(dekb * ekb, axis=1, keepdims=True)
            dgc = dgc + jnp.sum(z, axis=1, keepdims=True) - jnp.sum(z.T, axis=1, keepdims=True)
            dgc = dgc + jnp.where(rowi == C - 1, dgl, 0.0)
            upper = (t["ri"] <= t["ci"]).astype(F32)
            dg_b = _mm3(upper, jnp.broadcast_to(dgc, (C, LANES)))
            yield
            dkb = dkb_p + e_c * dekb
            dk_ref[:, sl] = dk + dk_a + dk_p + beta_c * dkb
            dbeta = jnp.sum(dkb * kv, axis=1, keepdims=True) + jnp.sum(dvb * vv, axis=1, keepdims=True)
            dg = dg_b[:, 0:1]
            a_raw = bav[:, H + h:H + h + 1]
            d_braw = dbeta * beta_c * (1.0 - beta_c)
            d_araw = dg * (-jnp.exp(al_ref[0, h])) * _sigmoid(a_raw + dt_ref[0, h])
            acc["dgate"] = acc["dgate"] + jnp.where(lane == h, d_braw, 0.0) + jnp.where(lane == H + h, d_araw, 0.0)
            dal = jnp.sum(dg * t["g_c"], axis=0, keepdims=True)
            ddt = jnp.sum(d_araw, axis=0, keepdims=True)
            acc["dpar"] = acc["dpar"] + jnp.where(lane1 == h, dal, 0.0) + jnp.where(lane1 == H + h, ddt, 0.0)

        _interleave(head(h) for h in range(H))
        dgate_ref[...] = acc["dgate"]
        dpar_ref[0:1, :] += acc["dpar"]

    rev = lambda s: N - 1 - s
    blk = lambda: pl.BlockSpec((C, AW), lambda s: (rev(s), 0))
    smem = pl.BlockSpec(memory_space=pltpu.SMEM)
    return pl.pallas_call(
        body, name="delta_bwd", grid=(N,),
        in_specs=[smem, smem, blk(), blk(), blk(),
                  pl.BlockSpec((C, LANES), lambda s: (rev(s), 0)),
                  pl.BlockSpec((1, 32, C), lambda s: (rev(s), 0, 0)),
                  pl.BlockSpec((C, LANES), lambda s: (rev(s), 0)),
                  blk(),
                  pl.BlockSpec((1, H, D, D), lambda s: (rev(s), 0, 0, 0)),
                  pl.BlockSpec((1, H, C, C), lambda s: (rev(s), 0, 0, 0)),
                  blk()],
        out_specs=[blk(), blk(), blk(),
                   pl.BlockSpec((C, LANES), lambda s: (rev(s), 0)),
                   pl.BlockSpec((8, LANES), lambda s: (0, 0))],
        out_shape=[S((T, AW), F32), S((T, AW), F32), S((T, AW), F32),
                   S((T, LANES), F32), S((8, LANES), F32)],
        scratch_shapes=[pltpu.VMEM((H, D, D), F32)],
        compiler_params=_cp(ARB),
    )(a_log, dt_bias, q, k, v, gcol, grow, ba, vnew, ssave, asave, d_o)


def _ln_stats(xv):
    mu = jnp.mean(xv, axis=-1, keepdims=True)
    xc = xv - mu
    var = jnp.mean(xc * xc, axis=-1, keepdims=True)
    rstd = lax.rsqrt(var + EPS)
    return xc * rstd, rstd


def _mix_fwd(o, proj, head_norm_w, ln_w, ln_b, w_sp, bs_t, H, D, G, P):
    T = o.shape[0]
    AW, BW = H * D, G * P
    MIX = AW + BW
    nb = AW // BW if AW % BW == 0 else None
    assert nb == 1, "group widths must match the projection column blocks"
    cb = 3

    def body(o_ref, za_ref, ub_ref, vb_ref, zb_ref, hw_ref, lw_ref, lb_ref, w_ref, bs_ref, out_ref):
        hw = hw_ref[...]
        for h in range(H):
            sl = slice(h * D, (h + 1) * D)
            oh = o_ref[:, sl]
            rs = lax.rsqrt(jnp.mean(oh * oh, axis=-1, keepdims=True) + EPS)
            out_ref[:, sl] = (oh * rs * hw * _silu(za_ref[:, sl])).astype(BF16)
        xhat, _ = _ln_stats(vb_ref[...])
        vn = xhat * lw_ref[...] + lb_ref[...]
        ri = lax.broadcasted_iota(jnp.int32, (P, P), 0)
        ci = lax.broadcasted_iota(jnp.int32, (P, P), 1)
        bsv = bs_ref[...]
        for g in range(G):
            sl = slice(g * P, (g + 1) * P)
            wm = jnp.where(ri >= ci, w_ref[g], 0.0)
            s = _mm(wm, vn[:, sl]) + bsv[:, g:g + 1]
            out_ref[:, AW + g * P:AW + (g + 1) * P] = (ub_ref[:, sl] * s * _silu(zb_ref[:, sl])).astype(BF16)

    row = lambda w: pl.BlockSpec((1, w), lambda i: (0, 0))
    return pl.pallas_call(
        body, name="mix_fwd", grid=(T // P,),
        in_specs=[pl.BlockSpec((P, AW), lambda i: (i, 0)),
                  pl.BlockSpec((P, AW), lambda i: (i, cb)),
                  pl.BlockSpec((P, BW), lambda i: (i, cb + 1)),
                  pl.BlockSpec((P, BW), lambda i: (i, cb + 2)),
                  pl.BlockSpec((P, BW), lambda i: (i, cb + 3)),
                  row(D), row(BW), row(BW),
                  pl.BlockSpec((G, P, P), lambda i: (0, 0, 0)),
                  pl.BlockSpec((P, G), lambda i: (0, 0))],
        out_specs=pl.BlockSpec((P, MIX), lambda i: (i, 0)),
        out_shape=S((T, MIX), BF16),
        compiler_params=_cp(ARB),
    )(o, proj, proj, proj, proj, head_norm_w, ln_w, ln_b, w_sp, bs_t)


def _mix_bwd(d_ocat, o, proj, head_norm_w, ln_w, ln_b, w_sp, bs_t, H, D, G, P):
    T = o.shape[0]
    AW, BW = H * D, G * P
    MIX = AW + BW
    cb = 3

    def body(dc_ref, o_ref, za_ref, ub_ref, vb_ref, zb_ref, hw_ref, lw_ref, lb_ref, w_ref, bs_ref,
             do_ref, dmain_ref, dhw_ref, dln_ref, dw_ref, dbs_ref, dvn_ref, drest_ref, out_sems):
        i = pl.program_id(0)
        slot = lax.rem(i, 2)

        def out_copy(step, s):
            return pltpu.make_async_copy(
                drest_ref.at[s], dmain_ref.at[pl.ds(step * P, P), pl.ds(cb * AW, AW + 3 * BW)], out_sems.at[s])

        @pl.when(i == 0)
        def _():
            dhw_ref[...] = jnp.zeros_like(dhw_ref)
            dln_ref[...] = jnp.zeros_like(dln_ref)
            dw_ref[...] = jnp.zeros_like(dw_ref)
            dbs_ref[...] = jnp.zeros_like(dbs_ref)

        @pl.when(i >= 2)
        def _():
            out_copy(i - 2, slot).wait()

        hw = hw_ref[...]
        dhw = jnp.zeros((1, D), F32)
        for h in range(H):
            sl = slice(h * D, (h + 1) * D)
            oh = o_ref[:, sl]
            za = za_ref[:, sl]
            doa = dc_ref[:, sl]
            rs = lax.rsqrt(jnp.mean(oh * oh, axis=-1, keepdims=True) + EPS)
            xh = oh * rs
            d_on = doa * _silu(za)
            drest_ref[slot, :, sl] = (doa * (xh * hw) * _dsilu(za)).astype(BF16)
            dhw = dhw + jnp.sum(d_on * xh, axis=0, keepdims=True)
            dxh = d_on * hw
            do_ref[:, sl] = rs * (dxh - xh * jnp.mean(dxh * xh, axis=-1, keepdims=True))
        dhw_ref[0:1, :] += dhw

        xhat, rstd = _ln_stats(vb_ref[...])
        lw = lw_ref[...]
        vn = xhat * lw + lb_ref[...]
        ri = lax.broadcasted_iota(jnp.int32, (P, P), 0)
        ci = lax.broadcasted_iota(jnp.int32, (P, P), 1)
        lane = lax.broadcasted_iota(jnp.int32, (P, LANES), 1)
        bsv = bs_ref[...]
        dbs = jnp.zeros((P, LANES), F32)
        for g in range(G):
            sl = slice(g * P, (g + 1) * P)
            wm = jnp.where(ri >= ci, w_ref[g], 0.0)
            vng = vn[:, sl]
            s = _mm(wm, vng) + bsv[:, g:g + 1]
            dob = dc_ref[:, AW + g * P:AW + (g + 1) * P]
            ub = ub_ref[:, sl]
            zb = zb_ref[:, sl]
            szb = _silu(zb)
            drest_ref[slot, :, AW + g * P:AW + (g + 1) * P] = (dob * s * szb).astype(BF16)
            drest_ref[slot, :, AW + 2 * BW + g * P:AW + 2 * BW + (g + 1) * P] = (
                dob * ub * s * _dsilu(zb)).astype(BF16)
            ds = dob * ub * szb
            dvn_ref[:, sl] = _mm_tn(wm, ds)
            dw_ref[g] += jnp.where(ri >= ci, _mm_nt(ds, vng), 0.0)
            dbs = dbs + jnp.where(lane == g, jnp.sum(ds, axis=1, keepdims=True), 0.0)
        dbs_ref[...] += dbs
        dvn = dvn_ref[...]
        dln_ref[0:1, :] += jnp.sum(dvn * xhat, axis=0, keepdims=True)
        dln_ref[1:2, :] += jnp.sum(dvn, axis=0, keepdims=True)
        dxh = dvn * lw
        dvb = rstd * (dxh - jnp.mean(dxh, axis=-1, keepdims=True) - xhat * jnp.mean(dxh * xhat, axis=-1, keepdims=True))
        drest_ref[slot, :, AW + BW:AW + 2 * BW] = dvb.astype(BF16)

        out_copy(i, slot).start()

        @pl.when(i == nstep - 1)
        def _():
            out_copy(i, slot).wait()
            if nstep > 1:
                out_copy(i - 1, 1 - slot).wait()

    nstep = T // P
    row = lambda w: pl.BlockSpec((1, w), lambda i: (0, 0))
    return pl.pallas_call(
        body, name="mix_bwd", grid=(nstep,),
        in_specs=[pl.BlockSpec((P, MIX), lambda i: (i, 0)),
                  pl.BlockSpec((P, AW), lambda i: (i, 0)),
                  pl.BlockSpec((P, AW), lambda i: (i, cb)),
                  pl.BlockSpec((P, BW), lambda i: (i, cb + 1)),
                  pl.BlockSpec((P, BW), lambda i: (i, cb + 2)),
                  pl.BlockSpec((P, BW), lambda i: (i, cb + 3)),
                  row(D), row(BW), row(BW),
                  pl.BlockSpec((G, P, P), lambda i: (0, 0, 0)),
                  pl.BlockSpec((P, G), lambda i: (0, 0))],
        out_specs=[pl.BlockSpec((P, AW), lambda i: (i, 0)),
                   pl.BlockSpec(memory_space=pl.ANY),
                   pl.BlockSpec((8, D), lambda i: (0, 0)),
                   pl.BlockSpec((8, BW), lambda i: (0, 0)),
                   pl.BlockSpec((G, P, P), lambda i: (0, 0, 0)),
                   pl.BlockSpec((P, LANES), lambda i: (0, 0))],
        out_shape=[S((T, AW), F32), S((T, cb * AW + AW + 3 * BW), BF16), S((8, D), F32), S((8, BW), F32),
                   S((G, P, P), F32), S((P, LANES), F32)],
        scratch_shapes=[pltpu.VMEM((P, BW), F32), pltpu.VMEM((2, P, AW + 3 * BW), BF16),
                        pltpu.SemaphoreType.DMA((2,))],
        compiler_params=_cp(ARB),
    )(d_ocat, o, proj, proj, proj, proj, head_norm_w, ln_w, ln_b, w_sp, bs_t)


def _out_proj_loss(ocat, w_out, x, target, fnw):
    T, MIX = ocat.shape
    DM = x.shape[1]
    tm = _tile(T, 256, 8)

    def body(oc_ref, w_ref, x_ref, t_ref, fw_ref, dh_ref, dhb_ref, doc_ref, loss_ref, gfw_ref):
        @pl.when(pl.program_id(0) == 0)
        def _():
            loss_ref[...] = jnp.zeros_like(loss_ref)
            gfw_ref[...] = jnp.zeros_like(gfw_ref)

        wv = w_ref[...]
        hh = x_ref[...] + jnp.dot(oc_ref[...].astype(MXU), wv.astype(MXU), preferred_element_type=F32)
        rs = lax.rsqrt(jnp.mean(hh * hh, axis=-1, keepdims=True) + EPS)
        hn = hh * rs
        fw = fw_ref[...]
        e = hn * fw - t_ref[...]
        row_loss = 0.5 * jnp.mean(e * e, axis=-1, keepdims=True)
        loss_ref[...] += jnp.sum(row_loss, axis=0, keepdims=True)
        dy = e * (1.0 / DM)
        gfw_ref[0:1, :] += jnp.sum(dy * hn, axis=0, keepdims=True)
        dhn = dy * fw
        dh = rs * (dhn - hn * jnp.mean(dhn * hn, axis=-1, keepdims=True))
        dh_ref[...] = dh
        dhb = dh.astype(BF16)
        dhb_ref[...] = dhb
        doc_ref[...] = _mm_nt(dhb, wv)

    return pl.pallas_call(
        body, name="out_proj_loss", grid=(T // tm,),
        in_specs=[pl.BlockSpec((tm, MIX), lambda i: (i, 0)),
                  pl.BlockSpec((MIX, DM), lambda i: (0, 0)),
                  pl.BlockSpec((tm, DM), lambda i: (i, 0)),
                  pl.BlockSpec((tm, DM), lambda i: (i, 0)),
                  pl.BlockSpec((1, DM), lambda i: (0, 0))],
        out_specs=[pl.BlockSpec((tm, DM), lambda i: (i, 0)),
                   pl.BlockSpec((tm, DM), lambda i: (i, 0)),
                   pl.BlockSpec((tm, MIX), lambda i: (i, 0)),
                   pl.BlockSpec((8, LANES), lambda i: (0, 0)),
                   pl.BlockSpec((8, DM), lambda i: (0, 0))],
        out_shape=[S((T, DM), F32), S((T, DM), BF16), S((T, MIX), F32), S((8, LANES), F32), S((8, DM), F32)],
        compiler_params=_cp(ARB),
    )(ocat, w_out, x, target, fnw)


def _grad_w(lhs, rhs, name):
    T, A = lhs.shape
    B = rhs.shape[1]
    ta = _tile(A, 512, LANES)
    tk = _tile(T, 1024, 16)
    nk = T // tk

    def body(l_ref, r_ref, out_ref, acc_ref):
        k = pl.program_id(1)
        part = _mm_tn(l_ref[...], r_ref[...])

        @pl.when(k == 0)
        def _():
            acc_ref[...] = part

        @pl.when(k > 0)
        def _():
            acc_ref[...] += part

        @pl.when(k == nk - 1)
        def _():
            out_ref[...] = acc_ref[...].astype(BF16)

    return pl.pallas_call(
        body, name=name, grid=(A // ta, nk),
        in_specs=[pl.BlockSpec((tk, ta), lambda i, k: (k, i)),
                  pl.BlockSpec((tk, B), lambda i, k: (k, 0))],
        out_specs=pl.BlockSpec((ta, B), lambda i, k: (i, 0)),
        out_shape=S((A, B), BF16),
        scratch_shapes=[pltpu.VMEM((ta, B), F32)],
        compiler_params=_cp(ARB, ARB),
    )(lhs, rhs)


def _grad_w_in(xn, dmain, dba):
    T, DM = xn.shape
    NM = dmain.shape[1]
    tn = _tile(NM, 1024, LANES)
    tk = _tile(T, 2048, 16)

    def body(xn_ref, dm_ref, dba_ref, gm_ref, gba_ref):
        j = pl.program_id(0)
        k = pl.program_id(1)

        @pl.when(k == 0)
        def _():
            gm_ref[...] = jnp.zeros_like(gm_ref)

        @pl.when((k == 0) & (j == 0))
        def _():
            gba_ref[...] = jnp.zeros_like(gba_ref)

        xv = xn_ref[...]
        gm_ref[...] += _mm_tn(xv, dm_ref[...])

        @pl.when(j == 0)
        def _():
            gba_ref[...] += _mm_tn(xv, dba_ref[...])

    return pl.pallas_call(
        body, name="grad_w_in", grid=(NM // tn, T // tk),
        in_specs=[pl.BlockSpec((tk, DM), lambda j, k: (k, 0)),
                  pl.BlockSpec((tk, tn), lambda j, k: (k, j)),
                  pl.BlockSpec((tk, LANES), lambda j, k: (k, 0))],
        out_specs=[pl.BlockSpec((DM, tn), lambda j, k: (0, j)),
                   pl.BlockSpec((DM, LANES), lambda j, k: (0, 0))],
        out_shape=[S((DM, NM), F32), S((DM, LANES), F32)],
        compiler_params=_cp(ARB, ARB),
    )(xn, dmain, dba)


def _dx(dmain, dba, w_main, w_ba, x, dh, norm_w):
    T, NM = dmain.shape
    DM = x.shape[1]
    tm = _tile(T, 512, 8)
    tk = _tile(NM, 1024, LANES)
    nk = NM // tk

    def body(dm_ref, dba_ref, w_ref, wba_ref, x_ref, dh_ref, nw_ref, gx_ref, gnw_ref, acc_ref):
        i = pl.program_id(0)
        k = pl.program_id(1)

        @pl.when((i == 0) & (k == 0))
        def _():
            gnw_ref[...] = jnp.zeros_like(gnw_ref)

        @pl.when(k == 0)
        def _():
            acc_ref[...] = _mm_nt(dba_ref[...], wba_ref[...])

        acc_ref[...] += _mm_nt(dm_ref[...], w_ref[...])

        @pl.when(k == nk - 1)
        def _():
            xv = x_ref[...]
            rs = lax.rsqrt(jnp.mean(xv * xv, axis=-1, keepdims=True) + EPS)
            xh = xv * rs
            dxn = acc_ref[...]
            gnw_ref[0:1, :] += jnp.sum(dxn * xh, axis=0, keepdims=True)
            dxh = dxn * nw_ref[...]
            gx_ref[...] = dh_ref[...] + rs * (dxh - xh * jnp.mean(dxh * xh, axis=-1, keepdims=True))

    return pl.pallas_call(
        body, name="dx", grid=(T // tm, nk),
        in_specs=[pl.BlockSpec((tm, tk), lambda i, k: (i, k)),
                  pl.BlockSpec((tm, LANES), lambda i, k: (i, 0)),
                  pl.BlockSpec((DM, tk), lambda i, k: (0, k)),
                  pl.BlockSpec((DM, LANES), lambda i, k: (0, 0)),
                  pl.BlockSpec((tm, DM), lambda i, k: (i, 0)),
                  pl.BlockSpec((tm, DM), lambda i, k: (i, 0)),
                  pl.BlockSpec((1, DM), lambda i, k: (0, 0))],
        out_specs=[pl.BlockSpec((tm, DM), lambda i, k: (i, 0)),
                   pl.BlockSpec((8, DM), lambda i, k: (0, 0))],
        out_shape=[S((T, DM), F32), S((8, DM), F32)],
        scratch_shapes=[pltpu.VMEM((tm, DM), F32)],
        compiler_params=_cp(ARB, ARB),
    )(dmain, dba, w_main, w_ba, x, dh, norm_w)


def _prep_a_bwd_pointwise(dq, dk, dv, c, H, D):
    T = c.shape[0]
    AW = H * D
    C3 = 3 * AW
    tb = _tile(T, 256, 8)
    scale = float(D) ** -0.5

    def body(dq_ref, dk_ref, dv_ref, c_ref, dc_ref):
        for h in range(H):
            for part, d_ref, sc in ((0, dq_ref, scale), (1, dk_ref, 1.0)):
                sl = slice(part * AW + h * D, part * AW + (h + 1) * D)
                cv = c_ref[:, sl]
                raw = _silu(cv)
                rs = lax.rsqrt(jnp.sum(raw * raw, axis=-1, keepdims=True) + EPS)
                nrm = raw * rs
                dn = d_ref[:, h * D:(h + 1) * D] * sc
                draw = rs * (dn - nrm * jnp.sum(dn * nrm, axis=-1, keepdims=True))
                dc_ref[:, sl] = draw * _dsilu(cv)
        dc_ref[:, 2 * AW:] = dv_ref[...] * _dsilu(c_ref[:, 2 * AW:])

    return pl.pallas_call(
        body, name="prep_a_bwd_pointwise", grid=(T // tb,),
        in_specs=[pl.BlockSpec((tb, AW), lambda i: (i, 0)),
                  pl.BlockSpec((tb, AW), lambda i: (i, 0)),
                  pl.BlockSpec((tb, AW), lambda i: (i, 0)),
                  pl.BlockSpec((tb, C3), lambda i: (i, 0))],
        out_specs=pl.BlockSpec((tb, C3), lambda i: (i, 0)),
        out_shape=S((T, C3), F32),
        compiler_params=_cp(ARB),
    )(dq, dk, dv, c)


def _conv_bwd(dc, proj, conv_w, dmain, C3):
    T = dc.shape[0]
    tb = _tile(T, 256, 8)
    nblk = T // tb
    r8 = tb // 8

    def body(dc_ref, dnext_ref, x_ref, halo_ref, cw_ref, dmain_in_ref, dx_ref, gcw_ref):
        del dmain_in_ref
        i = pl.program_id(0)

        @pl.when(i == 0)
        def _():
            gcw_ref[...] = jnp.zeros_like(gcw_ref)

        dcv = dc_ref[...]
        dnext = dnext_ref[...] * (i < nblk - 1).astype(F32)
        dcp = jnp.concatenate([dcv, dnext], axis=0)
        cw = cw_ref[...]
        dx = cw[3:4, :] * dcv
        for j in range(3):
            dx = dx + cw[j:j + 1, :] * dcp[3 - j:3 - j + tb]
        dx_ref[...] = dx.astype(BF16)
        halo = halo_ref[...] * (i > 0).astype(F32)
        xp = jnp.concatenate([halo, x_ref[...]], axis=0)
        for j in range(4):
            gcw_ref[j:j + 1, :] += jnp.sum(dcv * xp[5 + j:5 + j + tb], axis=0, keepdims=True)

    return pl.pallas_call(
        body, name="conv_bwd", grid=(nblk,),
        in_specs=[pl.BlockSpec((tb, C3), lambda i: (i, 0)),
                  pl.BlockSpec((8, C3), lambda i: (jnp.minimum((i + 1) * r8, T // 8 - 1), 0)),
                  pl.BlockSpec((tb, C3), lambda i: (i, 0)),
                  pl.BlockSpec((8, C3), lambda i: (jnp.maximum(i * r8 - 1, 0), 0)),
                  pl.BlockSpec((4, C3), lambda i: (0, 0)),
                  pl.BlockSpec(memory_space=pl.ANY)],
        out_specs=[pl.BlockSpec((tb, C3), lambda i: (i, 0)),
                   pl.BlockSpec((8, C3), lambda i: (0, 0))],
        out_shape=[S(dmain.shape, dmain.dtype), S((8, C3), F32)],
        input_output_aliases={5: 0},
        compiler_params=_cp(ARB),
    )(dc, dc, proj, proj, conv_w, dmain)


def _adam_math(w, g, m, v):
    m2 = ADAM_B1 * m + (1.0 - ADAM_B1) * g
    v2 = ADAM_B2 * v + (1.0 - ADAM_B2) * (g * g)
    m_hat = m2 / (1.0 - ADAM_B1 ** ADAM_STEP)
    v_hat = v2 / (1.0 - ADAM_B2 ** ADAM_STEP)
    delta = -ADAM_LR * (m_hat / (jnp.sqrt(v_hat) + ADAM_EPS) + ADAM_WD * w)
    return delta, m2, v2


def _pair_sum(blocks, recv, core, name):
    K, _, R, C = blocks.shape
    tr = _tile(R, 256, 16)

    def body(core_ref, a_ref, b_ref, o_ref):
        del core_ref
        o_ref[0] = (a_ref[0, 0].astype(F32) + b_ref[0].astype(F32)).astype(BF16)

    spec = lambda: pl.BlockSpec((1, tr, C), lambda k, i, core_ref: (k, i, 0))
    return pl.pallas_call(
        body, name=name,
        grid_spec=pltpu.PrefetchScalarGridSpec(
            num_scalar_prefetch=1, grid=(K, R // tr),
            in_specs=[pl.BlockSpec((1, 1, tr, C), lambda k, i, core_ref: (k, core_ref[0], i, 0)), spec()],
            out_specs=spec()),
        out_shape=S((K, R, C), BF16), compiler_params=_cp(ARB, ARB),
    )(core, blocks, recv)


def _sum_adam(chip_sums, recv, w, m, v, chip, name):
    R, C = w.shape
    tr = _tile(R, 256, 16)

    def body(chip_ref, own_ref, r_ref, w_ref, m_ref, v_ref, g_ref, d_ref, m2_ref, v2_ref):
        del chip_ref
        g = own_ref[0].astype(F32)
        for j in range(3):
            g = g + r_ref[j].astype(F32)
        g_ref[...] = g
        d_ref[...], m2_ref[...], v2_ref[...] = _adam_math(w_ref[...], g, m_ref[...], v_ref[...])

    spec = lambda: pl.BlockSpec((tr, C), lambda i, chip_ref: (i, 0))
    return pl.pallas_call(
        body, name=name,
        grid_spec=pltpu.PrefetchScalarGridSpec(
            num_scalar_prefetch=1, grid=(R // tr,),
            in_specs=[pl.BlockSpec((1, tr, C), lambda i, chip_ref: (chip_ref[0], i, 0)),
                      pl.BlockSpec((3, tr, C), lambda i, chip_ref: (0, i, 0)), spec(), spec(), spec()],
            out_specs=[spec(), spec(), spec(), spec()]),
        out_shape=[S((R, C), F32)] * 4, compiler_params=_cp(ARB),
    )(chip, chip_sums, recv, w, m, v)


def _adam_small(w, g, m, v):
    R, C = w.shape
    tr = _tile(R, 512, 8)

    def body(w_ref, g_ref, m_ref, v_ref, d_ref, m2_ref, v2_ref):
        d_ref[...], m2_ref[...], v2_ref[...] = _adam_math(w_ref[...], g_ref[...], m_ref[...], v_ref[...])

    spec = lambda: pl.BlockSpec((tr, C), lambda i: (i, 0))
    return pl.pallas_call(
        body, name="adam_small", grid=(R // tr,), in_specs=[spec()] * 4, out_specs=[spec()] * 3,
        out_shape=[S((R, C), F32)] * 3, compiler_params=_cp(ARB),
    )(w, g, m, v)


def _position():
    return lax.axis_index("x"), lax.axis_index("y"), lax.axis_index("c")


def _all_gather_weights(arrs):
    na = len(arrs)

    def body(*refs):
        ins, outs = refs[:na], refs[na:2 * na]
        send_sems, recv_sems, local_sems = refs[2 * na:]
        x, y, c = _position()
        me, sibling = (x, y, c), (x, y, 1 - c)
        chips = [(1 - x, y), (x, 1 - y), (1 - x, 1 - y)]

        def slot(a, p):
            return outs[a].at[4 * p[0] + 2 * p[1] + p[2]]

        def copy(a, kk, block, to, src=None):
            return pltpu.make_async_remote_copy(
                src_ref=slot(a, block) if src is None else src, dst_ref=slot(a, block),
                send_sem=send_sems.at[a, kk], recv_sem=recv_sems.at[a, kk], device_id=to, device_id_type=MESH)

        mine = [pltpu.make_async_copy(ins[a], slot(a, me), local_sems.at[a]) for a in range(na)]
        for cp in mine:
            cp.start()
        first = []
        for a in range(na):
            first.append(copy(a, 0, me, sibling, src=ins[a]))
            first += [copy(a, 1 + j, me, (*chip, c), src=ins[a]) for j, chip in enumerate(chips)]
        for cp in first:
            cp.start()
        passed = []
        for j, chip in enumerate(chips):
            for a in range(na):
                copy(a, 1 + j, (*chip, c), me).wait_recv()
                fw = copy(a, 4 + j, (*chip, c), sibling)
                fw.start()
                passed.append(fw)
        for a in range(na):
            copy(a, 0, sibling, me).wait_recv()
            for j, chip in enumerate(chips):
                copy(a, 4 + j, (*chip, 1 - c), me).wait_recv()
        for cp in first + passed:
            cp.wait_send()
        for cp in mine:
            cp.wait()

    any_spec = pl.BlockSpec(memory_space=pl.ANY)
    return pl.pallas_call(
        body, name="all_gather_weights",
        in_specs=[any_spec] * na, out_specs=[any_spec] * na,
        out_shape=[S((N_DEV,) + a.shape, a.dtype) for a in arrs],
        scratch_shapes=[pltpu.SemaphoreType.DMA((na, 7)), pltpu.SemaphoreType.DMA((na, 7)),
                        pltpu.SemaphoreType.DMA((na,))],
    )(*arrs)


def _exchange_sibling(arrs):
    na = len(arrs)
    K = arrs[0].shape[0]

    def body(*refs):
        ins, outs = refs[:na], refs[na:2 * na]
        send_sems, recv_sems = refs[2 * na:]
        x, y, c = _position()
        cps = [pltpu.make_async_remote_copy(src_ref=ins[a].at[k, 1 - c], dst_ref=outs[a].at[k],
                                            send_sem=send_sems.at[a, k], recv_sem=recv_sems.at[a, k],
                                            device_id=(x, y, 1 - c), device_id_type=MESH)
               for a in range(na) for k in range(K)]
        for cp in cps:
            cp.start()
        for cp in cps:
            cp.wait()

    any_spec = pl.BlockSpec(memory_space=pl.ANY)
    return pl.pallas_call(
        body, name="exchange_sibling", in_specs=[any_spec] * na, out_specs=[any_spec] * na,
        out_shape=[S((K,) + a.shape[2:], a.dtype) for a in arrs],
        scratch_shapes=[pltpu.SemaphoreType.DMA((na, K)), pltpu.SemaphoreType.DMA((na, K))],
    )(*arrs)


def _exchange_chips(arrs):
    na = len(arrs)

    def body(*refs):
        ins, outs = refs[:na], refs[na:2 * na]
        send_sems, recv_sems = refs[2 * na:]
        x, y, c = _position()
        chips = [(1 - x, y), (x, 1 - y), (1 - x, 1 - y)]
        cps = []
        for a in range(na):
            for j, (px, py) in enumerate(chips):
                cps.append(pltpu.make_async_remote_copy(
                    src_ref=ins[a].at[2 * px + py], dst_ref=outs[a].at[j], send_sem=send_sems.at[a, j],
                    recv_sem=recv_sems.at[a, j], device_id=(px, py, c), device_id_type=MESH))
        for cp in cps:
            cp.start()
        for cp in cps:
            cp.wait()

    any_spec = pl.BlockSpec(memory_space=pl.ANY)
    return pl.pallas_call(
        body, name="exchange_chips", in_specs=[any_spec] * na, out_specs=[any_spec] * na,
        out_shape=[S((3,) + a.shape[1:], a.dtype) for a in arrs],
        scratch_shapes=[pltpu.SemaphoreType.DMA((na, 3)), pltpu.SemaphoreType.DMA((na, 3))],
    )(*arrs)


def _all_reduce_small(part):
    R, C = part.shape

    def body(p_ref, out_ref, gath_ref, send_sems, recv_sems):
        x, y, c = _position()
        me = 4 * x + 2 * y + c
        gath_ref[me] = p_ref[...]
        cps = []
        for k in range(1, N_DEV):
            px = 1 - x if k & 4 else x
            py = 1 - y if k & 2 else y
            pc = 1 - c if k & 1 else c
            cps.append(pltpu.make_async_remote_copy(
                src_ref=p_ref, dst_ref=gath_ref.at[me], send_sem=send_sems.at[k - 1], recv_sem=recv_sems.at[k - 1],
                device_id=(px, py, pc), device_id_type=MESH))
        for cp in cps:
            cp.start()
        for cp in cps:
            cp.wait()
        acc = gath_ref[0]
        for d in range(1, N_DEV):
            acc = acc + gath_ref[d]
        out_ref[...] = acc

    vm = pl.BlockSpec(memory_space=pltpu.VMEM)
    return pl.pallas_call(
        body, name="all_reduce_small", in_specs=[vm], out_specs=vm, out_shape=S((R, C), F32),
        scratch_shapes=[pltpu.VMEM((N_DEV, R, C), F32), pltpu.SemaphoreType.DMA((N_DEV - 1,)),
                        pltpu.SemaphoreType.DMA((N_DEV - 1,))],
    )(part)


def _pack(parts):
    rows = []
    for p in parts:
        f = p.reshape(-1).astype(F32)
        pad = (-f.shape[0]) % (8 * LANES)
        rows.append(jnp.pad(f, (0, pad)).reshape(-1, LANES))
    return jnp.concatenate(rows, axis=0)


def _unpack(buf, shapes):
    out, r = [], 0
    for shp in shapes:
        n = 1
        for s in shp:
            n *= s
        nr = -(-n // (8 * LANES)) * 8
        out.append(buf[r:r + nr].reshape(-1)[:n].reshape(shp))
        r += nr
    return out


def kernel(x, norm_w, w_in, conv_w, a_log, dt_bias, head_norm_w, sgu_ln_w, sgu_ln_b, w_spatial, b_spatial, w_out, final_norm_w, loss_target, m_norm_w, m_w_in, m_conv_w, m_a_log, m_dt_bias, m_head_norm_w, m_sgu_ln_w, m_sgu_ln_b, m_w_spatial, m_b_spatial, m_w_out, m_final_norm_w, v_norm_w, v_w_in, v_conv_w, v_a_log, v_dt_bias, v_head_norm_w, v_sgu_ln_w, v_sgu_ln_b, v_w_spatial, v_b_spatial, v_w_out, v_final_norm_w):
    T, DM = x.shape[1], x.shape[2]
    H, D = a_log.shape[1], head_norm_w.shape[1]
    G, P = w_spatial.shape[1], w_spatial.shape[2]
    AW, BW = H * D, G * P
    MIX = AW + BW
    WD = w_in.shape[2]
    IN = N_DEV * WD
    RO = w_out.shape[1]
    CW = conv_w.shape[2]
    sizes = (3 * AW, AW, H, H, BW, BW, BW)
    assert sum(sizes) == IN and 2 * H <= LANES and 3 * H <= 32 and N_DEV * RO == MIX and N_DEV * CW == 3 * AW
    offs = [0]
    for s in sizes:
        offs.append(offs[-1] + s)
    px, py, pc = _position()
    dev = 4 * px + 2 * py + pc
    chip = 2 * px + py

    x2, tgt = x[0], loss_target[0]

    g_win, g_wout, g_conv = _all_gather_weights(
        [_cast_bf16(w_in[0], "cast_w_in"), _cast_bf16(w_out[0], "cast_w_out"), conv_w[0]])
    w_main, w_ba = _relayout_w(g_win, offs[2], offs[4])
    w_out_full = g_wout.reshape(MIX, DM)
    conv_full = g_conv.transpose(1, 0, 2).reshape(4, 3 * AW)
    alog_row = jnp.pad(a_log, ((0, 0), (H, LANES - 2 * H)))
    dtb_row = jnp.pad(dt_bias, ((0, 0), (H, LANES - 2 * H)))
    bs_t = b_spatial[0].T

    proj, ba, xn = _in_proj(x2, norm_w, w_main, w_ba)
    q, k, v, c, gcol, grow = _prep_a_fwd(proj, ba, conv_full, alog_row, dtb_row, H, D)
    o, vnew, ssave, asave = _delta_fwd(q, k, v, gcol, grow, H, D)
    ocat = _mix_fwd(o, proj, head_norm_w, sgu_ln_w, sgu_ln_b, w_spatial[0], bs_t, H, D, G, P)
    dh, dh_bf, d_ocat, loss_acc, g_fnw = _out_proj_loss(ocat, w_out_full, x2, tgt, final_norm_w.reshape(1, DM))
    loss = lax.psum(loss_acc[0, 0], AXES)

    g_wout_part = _grad_w(ocat, dh_bf, "grad_w_out")
    d_o, dmain, g_hnw, g_ln, g_wsp, g_bs_t = _mix_bwd(
        d_ocat, o, proj, head_norm_w, sgu_ln_w, sgu_ln_b, w_spatial[0], bs_t, H, D, G, P)
    dq, dk, dv, dgate, dpar = _delta_bwd(q, k, v, gcol, grow, ba, vnew, ssave, asave, d_o, a_log, dt_bias, H, D)
    dc = _prep_a_bwd_pointwise(dq, dk, dv, c, H, D)
    dmain, g_conv_part = _conv_bwd(dc, proj, conv_full, dmain, 3 * AW)
    dba = dgate.astype(BF16)
    g_main, g_ba = _grad_w_in(xn, dmain, dba)
    grad_x, g_nw = _dx(dmain, dba, w_main, w_ba, x2, dh, norm_w)

    g_win_blocks = _relayout_g(g_main, g_ba, WD, offs[2], offs[4]).reshape(4, 2, DM, WD)
    g_wout_blocks = g_wout_part.reshape(4, 2, RO, DM)
    core_idx = jnp.reshape(pc, (1,)).astype(jnp.int32)
    chip_idx = jnp.reshape(chip, (1,)).astype(jnp.int32)
    from_sib = _exchange_sibling([g_win_blocks, g_wout_blocks])
    chip_win = _pair_sum(g_win_blocks, from_sib[0], core_idx, "pair_sum_w_in")
    chip_wout = _pair_sum(g_wout_blocks, from_sib[1], core_idx, "pair_sum_w_out")
    recv_win, recv_wout = _exchange_chips([chip_win, chip_wout])
    grad_w_in, delta_w_in, new_m_w_in, new_v_w_in = _sum_adam(
        chip_win, recv_win, w_in[0], m_w_in[0], v_w_in[0], chip_idx, "sum_adam_w_in")
    grad_w_out, delta_w_out, new_m_w_out, new_v_w_out = _sum_adam(
        chip_wout, recv_wout, w_out[0], m_w_out[0], v_w_out[0], chip_idx, "sum_adam_w_out")

    small_shapes = [norm_w.shape, a_log.shape, dt_bias.shape, head_norm_w.shape, sgu_ln_w.shape, sgu_ln_b.shape,
                    w_spatial.shape, b_spatial.shape, final_norm_w.shape]
    parts = [g_nw[0], dpar[0, :H], dpar[0, H:2 * H], g_hnw[0], g_ln[0], g_ln[1], g_wsp, g_bs_t[:, :G].T, g_fnw[0],
             g_conv_part[:4]]
    red = _all_reduce_small(_pack(parts))
    grads_small = _unpack(red, small_shapes + [(4, 3 * AW)])
    g_conv_full = grads_small.pop()
    grad_conv = lax.dynamic_slice_in_dim(g_conv_full, dev * CW, CW, axis=1)[None]
    small_w = [norm_w, a_log, dt_bias, head_norm_w, sgu_ln_w, sgu_ln_b, w_spatial, b_spatial, final_norm_w, conv_w]
    small_m = [m_norm_w, m_a_log, m_dt_bias, m_head_norm_w, m_sgu_ln_w, m_sgu_ln_b, m_w_spatial, m_b_spatial,
               m_final_norm_w, m_conv_w]
    small_v = [v_norm_w, v_a_log, v_dt_bias, v_head_norm_w, v_sgu_ln_w, v_sgu_ln_b, v_w_spatial, v_b_spatial,
               v_final_norm_w, v_conv_w]
    small_g = grads_small + [grad_conv]
    shapes10 = [w.shape for w in small_w]
    d_p, m_p, v_p = _adam_small(_pack(small_w), _pack(small_g), _pack(small_m), _pack(small_v))
    d_s, m_s, v_s = _unpack(d_p, shapes10), _unpack(m_p, shapes10), _unpack(v_p, shapes10)

    def order(small, win, wout):
        return [small[0], win[None], small[9], small[1], small[2], small[3], small[4], small[5], small[6], small[7],
                wout[None], small[8]]

    grads = order(small_g, grad_w_in, grad_w_out)
    deltas = order(d_s, delta_w_in, delta_w_out)
    new_m = order(m_s, new_m_w_in, new_m_w_out)
    new_v = order(v_s, new_v_w_in, new_v_w_out)
    return (loss, grad_x[None], *grads, *deltas, *new_m, *new_v)
```

```python
import functools

import jax
import jax.numpy as jnp
from jax import lax
from jax.experimental import pallas as pl
from jax.experimental.pallas import tpu as pltpu

F32 = jnp.float32
BF16 = jnp.bfloat16
MXU = jnp.bfloat16
HI = lax.Precision.HIGHEST
EPS = 1e-6
CHUNK_A = 64
LANES = 128
MESH = pl.DeviceIdType.MESH
AXES = ("x", "y", "c")
N_DEV = 8

ADAM_LR = 0.001
ADAM_B1 = 0.9
ADAM_B2 = 0.999
ADAM_EPS = 1e-08
ADAM_WD = 0.01
ADAM_STEP = 10

S = jax.ShapeDtypeStruct
ARB = "arbitrary"


def _cp(*sem):
    return pltpu.CompilerParams(dimension_semantics=tuple(sem), vmem_limit_bytes=56 * 1024 * 1024)


def _tile(n, cap, mult):
    best = None
    t = mult
    while t <= min(n, cap):
        if n % t == 0:
            best = t
        t += mult
    return best if best is not None else n


def _mm(a, b):
    return jnp.dot(a.astype(MXU), b.astype(MXU), preferred_element_type=F32)


def _mm_nt(a, b):
    return lax.dot_general(a.astype(MXU), b.astype(MXU), (((1,), (1,)), ((), ())), preferred_element_type=F32)


def _mm_tn(a, b):
    return lax.dot_general(a.astype(MXU), b.astype(MXU), (((0,), (0,)), ((), ())), preferred_element_type=F32)


def _mmh(a, b):
    return jnp.dot(a, b, precision=HI, preferred_element_type=F32)


def _mmh_tn(a, b):
    return lax.dot_general(a, b, (((0,), (0,)), ((), ())), precision=HI, preferred_element_type=F32)


def _sigmoid(x):
    return 1.0 / (1.0 + jnp.exp(-x))


def _silu(x):
    return x * _sigmoid(x)


def _dsilu(x):
    s = _sigmoid(x)
    return s * (1.0 + x * (1.0 - s))


def _softplus(x):
    return jnp.maximum(x, 0.0) + jnp.log(1.0 + jnp.exp(-jnp.abs(x)))


def _pieces(wd, gate_lo, gate_hi, total):
    out = []
    for d in range(N_DEV):
        lo, hi = d * wd, (d + 1) * wd
        for dest, a, b, shift in (("main", 0, gate_lo, 0), ("gate", gate_lo, gate_hi, -gate_lo),
                                  ("main", gate_hi, total, gate_lo - gate_hi)):
            s0, s1 = max(lo, a), min(hi, b)
            if s0 < s1:
                out.append((d, s0 - lo, s1 - lo, dest, s0 + shift))
    return out


def _cast_bf16(a, name):
    R, C = a.shape
    tr = _tile(R, 256, 16)

    def body(a_ref, o_ref):
        o_ref[...] = a_ref[...].astype(BF16)

    spec = pl.BlockSpec((tr, C), lambda i: (i, 0))
    return pl.pallas_call(body, name=name, grid=(R // tr,), in_specs=[spec], out_specs=spec,
                          out_shape=S((R, C), BF16), compiler_params=_cp(ARB))(a)


def _relayout_w(g_win, gate_lo, gate_hi):
    _, DM, WD = g_win.shape
    total = N_DEV * WD
    NM = total - (gate_hi - gate_lo)
    tr = _tile(DM, 256, 16)
    plan = _pieces(WD, gate_lo, gate_hi, total)

    def body(g_ref, main_ref, gate_ref):
        gate_ref[...] = jnp.zeros_like(gate_ref)
        for d, s0, s1, dest, c0 in plan:
            dst = main_ref if dest == "main" else gate_ref
            dst[:, c0:c0 + (s1 - s0)] = g_ref[d, :, s0:s1]

    return pl.pallas_call(
        body, name="relayout_w", grid=(DM // tr,),
        in_specs=[pl.BlockSpec((N_DEV, tr, WD), lambda i: (0, i, 0))],
        out_specs=[pl.BlockSpec((tr, NM), lambda i: (i, 0)), pl.BlockSpec((tr, LANES), lambda i: (i, 0))],
        out_shape=[S((DM, NM), g_win.dtype), S((DM, LANES), g_win.dtype)],
        compiler_params=_cp(ARB),
    )(g_win)


def _relayout_g(g_main, g_gate, WD, gate_lo, gate_hi):
    DM = g_main.shape[0]
    total = N_DEV * WD
    tr = _tile(DM, 256, 16)
    plan = _pieces(WD, gate_lo, gate_hi, total)

    def body(m_ref, gate_ref, out_ref):
        for d, s0, s1, dest, c0 in plan:
            src = m_ref if dest == "main" else gate_ref
            out_ref[d, :, s0:s1] = src[:, c0:c0 + (s1 - s0)].astype(BF16)

    return pl.pallas_call(
        body, name="relayout_g", grid=(DM // tr,),
        in_specs=[pl.BlockSpec((tr, g_main.shape[1]), lambda i: (i, 0)), pl.BlockSpec((tr, LANES), lambda i: (i, 0))],
        out_specs=pl.BlockSpec((N_DEV, tr, WD), lambda i: (0, i, 0)),
        out_shape=S((N_DEV, DM, WD), BF16),
        compiler_params=_cp(ARB),
    )(g_main, g_gate)


def _in_proj(x, norm_w, w_main, w_ba, shards):
    T, DM = x.shape
    NM = w_main.shape[1]
    tm = _tile(T, 1024, 8)
    tn = _tile(NM, 1024, LANES)
    ni, nj = T // tm, NM // tn
    ns = len(shards)

    def body(x_ref, nw_ref, w_ref, wba_ref, *rest):
        srcs = rest[:ns]
        proj_ref, ba_ref, xn_ref = rest[ns:ns + 3]
        gath = rest[ns + 3:2 * ns + 3]
        send_sems, recv_sems, local_sems = rest[2 * ns + 3:]
        i = pl.program_id(0)
        me, cps = _broadcast_copies(srcs, gath, send_sems, recv_sems)
        cps = cps + [pltpu.make_async_copy(srcs[a], gath[a].at[me], local_sems.at[a]) for a in range(ns)]

        @pl.when((i == 0) & (pl.program_id(1) == 0))
        def _():
            for cp in cps:
                cp.start()

        @pl.when((i == ni - 1) & (pl.program_id(1) == nj - 1))
        def _():
            for cp in cps:
                cp.wait()

        @pl.when(pl.program_id(1) == 0)
        def _():
            xv = x_ref[...]
            r = lax.rsqrt(jnp.mean(xv * xv, axis=-1, keepdims=True) + EPS)
            xn = (xv * r * nw_ref[...]).astype(BF16)
            xn_ref[...] = xn
            ba_ref[...] = jnp.dot(xn.astype(MXU), wba_ref[...].astype(MXU), preferred_element_type=F32)

        proj_ref[...] = jnp.dot(xn_ref[...].astype(MXU), w_ref[...].astype(MXU), preferred_element_type=F32)

    any_spec = pl.BlockSpec(memory_space=pl.ANY)
    res = pl.pallas_call(
        body, name="in_proj", grid=(ni, nj),
        in_specs=[pl.BlockSpec((tm, DM), lambda i, j: (i, 0)),
                  pl.BlockSpec((1, DM), lambda i, j: (0, 0)),
                  pl.BlockSpec((DM, tn), lambda i, j: (0, j)),
                  pl.BlockSpec((DM, LANES), lambda i, j: (0, 0))] + [any_spec] * ns,
        out_specs=[pl.BlockSpec((tm, tn), lambda i, j: (i, j)),
                   pl.BlockSpec((tm, LANES), lambda i, j: (i, 0)),
                   pl.BlockSpec((tm, DM), lambda i, j: (i, 0))] + [any_spec] * ns,
        out_shape=[S((T, NM), F32), S((T, LANES), F32), S((T, DM), BF16)]
        + [S((N_DEV,) + a.shape, a.dtype) for a in shards],
        scratch_shapes=[pltpu.SemaphoreType.DMA((ns, N_DEV - 1)), pltpu.SemaphoreType.DMA((ns, N_DEV - 1)),
                        pltpu.SemaphoreType.DMA((ns,))],
        compiler_params=_cp(ARB, ARB),
    )(x, norm_w, w_main, w_ba, *shards)
    return res[0], res[1], res[2], res[3:]


def _prep_a_fwd(proj, ba, conv_w, alog_row, dtb_row, H, D):
    T = proj.shape[0]
    AW = H * D
    C3 = 3 * AW
    tb = _tile(T, 256, CHUNK_A)
    nch = tb // CHUNK_A
    nblk = T // tb
    scale = float(D) ** -0.5

    def body(x_ref, halo_ref, ba_ref, cw_ref, al_ref, dt_ref, q_ref, k_ref, v_ref, c_ref, gcol_ref, grow_ref):
        i = pl.program_id(0)
        xv = x_ref[...]
        halo = halo_ref[...] * (i > 0).astype(F32)
        xp = jnp.concatenate([halo, xv], axis=0)
        cw = cw_ref[...]
        c = cw[0:1, :] * xp[5:5 + tb]
        for j in range(1, 4):
            c = c + cw[j:j + 1, :] * xp[5 + j:5 + j + tb]
        c_ref[...] = c
        a = _silu(c)
        for h in range(H):
            qh = a[:, h * D:(h + 1) * D]
            kh = a[:, AW + h * D:AW + (h + 1) * D]
            qr = lax.rsqrt(jnp.sum(qh * qh, axis=-1, keepdims=True) + EPS)
            kr = lax.rsqrt(jnp.sum(kh * kh, axis=-1, keepdims=True) + EPS)
            q_ref[:, h * D:(h + 1) * D] = qh * (qr * scale)
            k_ref[:, h * D:(h + 1) * D] = kh * kr
        v_ref[...] = a[:, 2 * AW:]

        bav = ba_ref[...]
        lane = lax.broadcasted_iota(jnp.int32, (tb, LANES), 1)
        beta = _sigmoid(bav)
        g = -jnp.exp(al_ref[...]) * _softplus(bav + dt_ref[...])
        gates = jnp.where(lane < H, beta, jnp.where(lane < 2 * H, g, 0.0))
        ri = lax.broadcasted_iota(jnp.int32, (CHUNK_A, CHUNK_A), 0)
        ci = lax.broadcasted_iota(jnp.int32, (CHUNK_A, CHUNK_A), 1)
        tri = (ri >= ci).astype(F32)
        lane_c = lax.broadcasted_iota(jnp.int32, (CHUNK_A, LANES), 1)
        for cc in range(nch):
            gch = gates[cc * CHUNK_A:(cc + 1) * CHUNK_A]
            gc = pltpu.roll(_mmh(tri, gch), H, 1)
            full = jnp.where(lane_c < 2 * H, gch, jnp.where(lane_c < 3 * H, gc, 0.0))
            gcol_ref[cc * CHUNK_A:(cc + 1) * CHUNK_A, :] = full
            grow_ref[cc] = full.T[0:32, :]

    return pl.pallas_call(
        body, name="prep_a_fwd", grid=(nblk,),
        in_specs=[pl.BlockSpec((tb, C3), lambda i: (i, 0)),
                  pl.BlockSpec((8, C3), lambda i: (jnp.maximum(i * (tb // 8) - 1, 0), 0)),
                  pl.BlockSpec((tb, LANES), lambda i: (i, 0)),
                  pl.BlockSpec((4, C3), lambda i: (0, 0)),
                  pl.BlockSpec((1, LANES), lambda i: (0, 0)),
                  pl.BlockSpec((1, LANES), lambda i: (0, 0))],
        out_specs=[pl.BlockSpec((tb, AW), lambda i: (i, 0)),
                   pl.BlockSpec((tb, AW), lambda i: (i, 0)),
                   pl.BlockSpec((tb, AW), lambda i: (i, 0)),
                   pl.BlockSpec((tb, C3), lambda i: (i, 0)),
                   pl.BlockSpec((tb, LANES), lambda i: (i, 0)),
                   pl.BlockSpec((nch, 32, CHUNK_A), lambda i: (i, 0, 0))],
        out_shape=[S((T, AW), F32), S((T, AW), F32), S((T, AW), F32), S((T, C3), F32),
                   S((T, LANES), F32), S((T // CHUNK_A, 32, CHUNK_A), F32)],
        compiler_params=_cp(ARB),
    )(proj, proj, ba, conv_w, alog_row, dtb_row)


_NN = (((1,), (0,)), ((), ()))
_TN = (((0,), (0,)), ((), ()))


def _split(a):
    hi = a.astype(BF16)
    return hi, (a - hi.astype(F32)).astype(BF16)


def _mm3(a, b, dims=_NN):
    ah, al = a if isinstance(a, tuple) else _split(a)
    bh, bl = b if isinstance(b, tuple) else _split(b)
    dg = lambda p, r: lax.dot_general(p, r, dims, preferred_element_type=F32)
    return dg(ah, bh) + (dg(ah, bl) + dg(al, bh))


def _interleave(gens):
    gens = list(gens)
    while gens:
        alive = []
        for g in gens:
            try:
                next(g)
                alive.append(g)
            except StopIteration:
                pass
        gens = alive


def _chunk_terms(q, k, v, gcolv, growv, h, H):
    C = CHUNK_A
    beta_c = gcolv[:, h:h + 1]
    g_c = gcolv[:, H + h:H + h + 1]
    gc_c = gcolv[:, 2 * H + h:2 * H + h + 1]
    gc_r = growv[2 * H + h:2 * H + h + 1, :]
    ri = lax.broadcasted_iota(jnp.int32, (C, C), 0)
    ci = lax.broadcasted_iota(jnp.int32, (C, C), 1)
    incl = ri >= ci
    strict = ri > ci
    kb = k * beta_c
    vb = v * beta_c
    p_raw = _mm_nt(kb, k)
    qk_raw = _mm_nt(q, k)
    gam = jnp.where(incl, jnp.exp(jnp.where(incl, gc_c - gc_r, 0.0)), 0.0)
    e_c = jnp.exp(gc_c)
    gl = gc_r[:, C - 1:C]
    edec = jnp.exp(gl - gc_c)
    yield
    lmat = jnp.where(strict, p_raw * gam, 0.0)
    attn = jnp.where(incl, qk_raw * gam, 0.0)
    return dict(beta_c=beta_c, g_c=g_c, gc_c=gc_c, gc_r=gc_r, incl=incl, strict=strict, gam=gam, e_c=e_c,
                kb=kb, vb=vb, lmat=lmat, attn=attn, gl=gl, edec=edec, ri=ri, ci=ci)


def _inv_unit_lower(lmat):
    C = lmat.shape[0]
    ri = lax.broadcasted_iota(jnp.int32, (C, C), 0)
    ci = lax.broadcasted_iota(jnp.int32, (C, C), 1)
    eye = (ri == ci).astype(F32)
    x = -lmat
    a = eye + x
    n = 1
    while 2 * n < C:
        xs = _split(x)
        x = _mm3(xs, xs)
        yield
        a = a + _mm3(a, x)
        n *= 2
    yield
    return a


def _delta_fwd(q, k, v, gcol, grow, H, D):
    T = q.shape[0]
    C = CHUNK_A
    N = T // C
    AW = H * D

    def body(q_ref, k_ref, v_ref, gcol_ref, grow_ref, o_ref, vn_ref, ssave_ref, asave_ref, s_ref):
        @pl.when(pl.program_id(0) == 0)
        def _():
            s_ref[...] = jnp.zeros_like(s_ref)

        gcolv = gcol_ref[...]
        growv = grow_ref[0]

        def head(h):
            sl = slice(h * D, (h + 1) * D)
            st = s_ref[h]
            ssave_ref[0, h] = st
            qv, kv, vv = q_ref[:, sl], k_ref[:, sl], v_ref[:, sl]
            t = yield from _chunk_terms(qv, kv, vv, gcolv, growv, h, H)
            ks = _mm(t["kb"] * t["e_c"], st)
            o_inter = _mm(qv * t["e_c"], st)
            a = yield from _inv_unit_lower(t["lmat"])
            asave_ref[0, h] = a
            v_new = _mm3(a, t["vb"] - ks)
            yield
            vn_ref[:, sl] = v_new
            o_intra = _mm(t["attn"], v_new)
            s_upd = _mm_tn(kv * t["edec"], v_new)
            yield
            o_ref[:, sl] = o_inter + o_intra
            s_ref[h] = st * jnp.exp(t["gl"]) + s_upd

        _interleave(head(h) for h in range(H))

    blk = lambda: pl.BlockSpec((C, AW), lambda n: (n, 0))
    return pl.pallas_call(
        body, name="delta_fwd", grid=(N,),
        in_specs=[blk(), blk(), blk(),
                  pl.BlockSpec((C, LANES), lambda n: (n, 0)),
                  pl.BlockSpec((1, 32, C), lambda n: (n, 0, 0))],
        out_specs=[blk(), blk(),
                   pl.BlockSpec((1, H, D, D), lambda n: (n, 0, 0, 0)),
                   pl.BlockSpec((1, H, C, C), lambda n: (n, 0, 0, 0))],
        out_shape=[S((T, AW), F32), S((T, AW), F32), S((N, H, D, D), F32), S((N, H, C, C), F32)],
        scratch_shapes=[pltpu.VMEM((H, D, D), F32)],
        compiler_params=_cp(ARB),
    )(q, k, v, gcol, grow)


def _delta_bwd(q, k, v, gcol, grow, ba, vnew, ssave, asave, d_o, a_log, dt_bias, H, D):
    T = q.shape[0]
    C = CHUNK_A
    N = T // C
    AW = H * D

    def body(al_ref, dt_ref, q_ref, k_ref, v_ref, gcol_ref, grow_ref, ba_ref, vn_ref, ss_ref, as_ref, do_ref,
             dq_ref, dk_ref, dv_ref, dgate_ref, dpar_ref, ds_ref):
        @pl.when(pl.program_id(0) == 0)
        def _():
            ds_ref[...] = jnp.zeros_like(ds_ref)
            dpar_ref[...] = jnp.zeros_like(dpar_ref)

        gcolv = gcol_ref[...]
        growv = grow_ref[0]
        bav = ba_ref[...]
        lane = lax.broadcasted_iota(jnp.int32, (C, LANES), 1)
        lane1 = lax.broadcasted_iota(jnp.int32, (1, LANES), 1)
        rowi = lax.broadcasted_iota(jnp.int32, (C, 1), 0)
        acc = {"dgate": jnp.zeros((C, LANES), F32), "dpar": jnp.zeros((1, LANES), F32)}

        def head(h):
            sl = slice(h * D, (h + 1) * D)
            ds_next = ds_ref[h]
            st = ss_ref[0, h]
            a = as_ref[0, h]
            qv, kv, vv, dov, v_new = q_ref[:, sl], k_ref[:, sl], v_ref[:, sl], do_ref[:, sl], vn_ref[:, sl]
            t = yield from _chunk_terms(qv, kv, vv, gcolv, growv, h, H)
            beta_c, e_c, gam, kb = t["beta_c"], t["e_c"], t["gam"], t["kb"]
            incl, strict, attn, lmat, edec = t["incl"], t["strict"], t["attn"], t["lmat"], t["edec"]
            kdec = kv * edec
            egl = jnp.exp(t["gl"])
            qe = qv * e_c
            ekb = kb * e_c

            dkdec = _mm_nt(v_new, ds_next)
            dv_new_s = _mm(kdec, ds_next)
            t1 = _mm_nt(dov, st)
            ds_o = _mm_tn(qe, dov)
            dattn_raw = _mm_nt(dov, v_new)
            dv_new_o = _mm_tn(attn, dov)
            yield
            dgl = egl * jnp.sum(jnp.sum(st * ds_next, axis=1, keepdims=True), axis=0, keepdims=True)
            dk = edec * dkdec
            r = jnp.sum(dkdec * kdec, axis=1, keepdims=True)
            dgc = -r
            dgl = dgl + jnp.sum(r, axis=0, keepdims=True)
            dq = e_c * t1
            dgc = dgc + jnp.sum(t1 * qe, axis=1, keepdims=True)
            dattn = jnp.where(incl, dattn_raw, 0.0)
            dv_new = dv_new_s + dv_new_o
            dqm = dattn * gam
            z = dattn * attn
            dvb = _mm3(a, dv_new, _TN)
            dq_a = _mm(dqm, kv)
            dk_a = _mm_tn(dqm, qv)
            yield
            dq_ref[:, sl] = dq + dq_a
            dv_ref[:, sl] = beta_c * dvb
            ds_kb = _mm_tn(ekb, dvb)
            dekb_neg = _mm_nt(dvb, st)
            dl_neg = _mm_nt(dvb, v_new)
            yield
            ds_ref[h] = egl * ds_next + ds_o - ds_kb
            dekb = -dekb_neg
            dl = jnp.where(strict, -dl_neg, 0.0)
            dp = dl * gam
            z = z + dl * lmat
            dkb_p = _mm(dp, kv)
            dk_p = _mm_tn(dp, kb)
            dgc = dgc + jnp.sum(dekb * ekb, axis=1, keepdims=True)
            dgc = dgc + jnp.sum(z, axis=1, keepdims=True) - jnp.sum(z.T, axis=1, keepdims=True)
            dgc = dgc + jnp.where(rowi == C - 1, dgl, 0.0)
            upper = (t["ri"] <= t["ci"]).astype(F32)
            dg_b = _mm3(upper, jnp.broadcast_to(dgc, (C, LANES)))
            yield
            dkb = dkb_p + e_c * dekb
            dk_ref[:, sl] = dk + dk_a + dk_p + beta_c * dkb
            dbeta = jnp.sum(dkb * kv, axis=1, keepdims=True) + jnp.sum(dvb * vv, axis=1, keepdims=True)
            dg = dg_b[:, 0:1]
            a_raw = bav[:, H + h:H + h + 1]
            d_braw = dbeta * beta_c * (1.0 - beta_c)
            d_araw = dg * (-jnp.exp(al_ref[0, h])) * _sigmoid(a_raw + dt_ref[0, h])
            acc["dgate"] = acc["dgate"] + jnp.where(lane == h, d_braw, 0.0) + jnp.where(lane == H + h, d_araw, 0.0)
            dal = jnp.sum(dg * t["g_c"], axis=0, keepdims=True)
            ddt = jnp.sum(d_araw, axis=0, keepdims=True)
            acc["dpar"] = acc["dpar"] + jnp.where(lane1 == h, dal, 0.0) + jnp.where(lane1 == H + h, ddt, 0.0)

        _interleave(head(h) for h in range(H))
        dgate_ref[...] = acc["dgate"]
        dpar_ref[0:1, :] += acc["dpar"]

    rev = lambda s: N - 1 - s
    blk = lambda: pl.BlockSpec((C, AW), lambda s: (rev(s), 0))
    smem = pl.BlockSpec(memory_space=pltpu.SMEM)
    return pl.pallas_call(
        body, name="delta_bwd", grid=(N,),
        in_specs=[smem, smem, blk(), blk(), blk(),
                  pl.BlockSpec((C, LANES), lambda s: (rev(s), 0)),
                  pl.BlockSpec((1, 32, C), lambda s: (rev(s), 0, 0)),
                  pl.BlockSpec((C, LANES), lambda s: (rev(s), 0)),
                  blk(),
                  pl.BlockSpec((1, H, D, D), lambda s: (rev(s), 0, 0, 0)),
                  pl.BlockSpec((1, H, C, C), lambda s: (rev(s), 0, 0, 0)),
                  blk()],
        out_specs=[blk(), blk(), blk(),
                   pl.BlockSpec((C, LANES), lambda s: (rev(s), 0)),
                   pl.BlockSpec((8, LANES), lambda s: (0, 0))],
        out_shape=[S((T, AW), F32), S((T, AW), F32), S((T, AW), F32),
                   S((T, LANES), F32), S((8, LANES), F32)],
        scratch_shapes=[pltpu.VMEM((H, D, D), F32)],
        compiler_params=_cp(ARB),
    )(a_log, dt_bias, q, k, v, gcol, grow, ba, vnew, ssave, asave, d_o)


def _ln_stats(xv):
    mu = jnp.mean(xv, axis=-1, keepdims=True)
    xc = xv - mu
    var = jnp.mean(xc * xc, axis=-1, keepdims=True)
    rstd = lax.rsqrt(var + EPS)
    return xc * rstd, rstd


def _mix_fwd(o, proj, head_norm_w, ln_w, ln_b, w_sp, bs_t, H, D, G, P):
    T = o.shape[0]
    AW, BW = H * D, G * P
    MIX = AW + BW
    nb = AW // BW if AW % BW == 0 else None
    assert nb == 1, "group widths must match the projection column blocks"
    cb = 3

    def body(o_ref, za_ref, ub_ref, vb_ref, zb_ref, hw_ref, lw_ref, lb_ref, w_ref, bs_ref, out_ref):
        hw = hw_ref[...]
        for h in range(H):
            sl = slice(h * D, (h + 1) * D)
            oh = o_ref[:, sl]
            rs = lax.rsqrt(jnp.mean(oh * oh, axis=-1, keepdims=True) + EPS)
            out_ref[:, sl] = (oh * rs * hw * _silu(za_ref[:, sl])).astype(BF16)
        xhat, _ = _ln_stats(vb_ref[...])
        vn = xhat * lw_ref[...] + lb_ref[...]
        ri = lax.broadcasted_iota(jnp.int32, (P, P), 0)
        ci = lax.broadcasted_iota(jnp.int32, (P, P), 1)
        bsv = bs_ref[...]
        for g in range(G):
            sl = slice(g * P, (g + 1) * P)
            wm = jnp.where(ri >= ci, w_ref[g], 0.0)
            s = _mm(wm, vn[:, sl]) + bsv[:, g:g + 1]
            out_ref[:, AW + g * P:AW + (g + 1) * P] = (ub_ref[:, sl] * s * _silu(zb_ref[:, sl])).astype(BF16)

    row = lambda w: pl.BlockSpec((1, w), lambda i: (0, 0))
    return pl.pallas_call(
        body, name="mix_fwd", grid=(T // P,),
        in_specs=[pl.BlockSpec((P, AW), lambda i: (i, 0)),
                  pl.BlockSpec((P, AW), lambda i: (i, cb)),
                  pl.BlockSpec((P, BW), lambda i: (i, cb + 1)),
                  pl.BlockSpec((P, BW), lambda i: (i, cb + 2)),
                  pl.BlockSpec((P, BW), lambda i: (i, cb + 3)),
                  row(D), row(BW), row(BW),
                  pl.BlockSpec((G, P, P), lambda i: (0, 0, 0)),
                  pl.BlockSpec((P, G), lambda i: (0, 0))],
        out_specs=pl.BlockSpec((P, MIX), lambda i: (i, 0)),
        out_shape=S((T, MIX), BF16),
        compiler_params=_cp(ARB),
    )(o, proj, proj, proj, proj, head_norm_w, ln_w, ln_b, w_sp, bs_t)


def _mix_bwd(d_ocat, o, proj, head_norm_w, ln_w, ln_b, w_sp, bs_t, H, D, G, P):
    T = o.shape[0]
    AW, BW = H * D, G * P
    MIX = AW + BW
    cb = 3

    def body(dc_ref, o_ref, za_ref, ub_ref, vb_ref, zb_ref, hw_ref, lw_ref, lb_ref, w_ref, bs_ref,
             do_ref, dmain_ref, dhw_ref, dln_ref, dw_ref, dbs_ref, dvn_ref, drest_ref, out_sems):
        i = pl.program_id(0)
        slot = lax.rem(i, 2)

        def out_copy(step, s):
            return pltpu.make_async_copy(
                drest_ref.at[s], dmain_ref.at[pl.ds(step * P, P), pl.ds(cb * AW, AW + 3 * BW)], out_sems.at[s])

        @pl.when(i == 0)
        def _():
            dhw_ref[...] = jnp.zeros_like(dhw_ref)
            dln_ref[...] = jnp.zeros_like(dln_ref)
            dw_ref[...] = jnp.zeros_like(dw_ref)
            dbs_ref[...] = jnp.zeros_like(dbs_ref)

        @pl.when(i >= 2)
        def _():
            out_copy(i - 2, slot).wait()

        hw = hw_ref[...]
        dhw = jnp.zeros((1, D), F32)
        for h in range(H):
            sl = slice(h * D, (h + 1) * D)
            oh = o_ref[:, sl]
            za = za_ref[:, sl]
            doa = dc_ref[:, sl]
            rs = lax.rsqrt(jnp.mean(oh * oh, axis=-1, keepdims=True) + EPS)
            xh = oh * rs
            d_on = doa * _silu(za)
            drest_ref[slot, :, sl] = (doa * (xh * hw) * _dsilu(za)).astype(BF16)
            dhw = dhw + jnp.sum(d_on * xh, axis=0, keepdims=True)
            dxh = d_on * hw
            do_ref[:, sl] = rs * (dxh - xh * jnp.mean(dxh * xh, axis=-1, keepdims=True))
        dhw_ref[0:1, :] += dhw

        xhat, rstd = _ln_stats(vb_ref[...])
        lw = lw_ref[...]
        vn = xhat * lw + lb_ref[...]
        ri = lax.broadcasted_iota(jnp.int32, (P, P), 0)
        ci = lax.broadcasted_iota(jnp.int32, (P, P), 1)
        lane = lax.broadcasted_iota(jnp.int32, (P, LANES), 1)
        bsv = bs_ref[...]
        dbs = jnp.zeros((P, LANES), F32)
        for g in range(G):
            sl = slice(g * P, (g + 1) * P)
            wm = jnp.where(ri >= ci, w_ref[g], 0.0)
            vng = vn[:, sl]
            s = _mm(wm, vng) + bsv[:, g:g + 1]
            dob = dc_ref[:, AW + g * P:AW + (g + 1) * P]
            ub = ub_ref[:, sl]
            zb = zb_ref[:, sl]
            szb = _silu(zb)
            drest_ref[slot, :, AW + g * P:AW + (g + 1) * P] = (dob * s * szb).astype(BF16)
            drest_ref[slot, :, AW + 2 * BW + g * P:AW + 2 * BW + (g + 1) * P] = (
                dob * ub * s * _dsilu(zb)).astype(BF16)
            ds = dob * ub * szb
            dvn_ref[:, sl] = _mm_tn(wm, ds)
            dw_ref[g] += jnp.where(ri >= ci, _mm_nt(ds, vng), 0.0)
            dbs = dbs + jnp.where(lane == g, jnp.sum(ds, axis=1, keepdims=True), 0.0)
        dbs_ref[...] += dbs
        dvn = dvn_ref[...]
        dln_ref[0:1, :] += jnp.sum(dvn * xhat, axis=0, keepdims=True)
        dln_ref[1:2, :] += jnp.sum(dvn, axis=0, keepdims=True)
        dxh = dvn * lw
        dvb = rstd * (dxh - jnp.mean(dxh, axis=-1, keepdims=True) - xhat * jnp.mean(dxh * xhat, axis=-1, keepdims=True))
        drest_ref[slot, :, AW + BW:AW + 2 * BW] = dvb.astype(BF16)

        out_copy(i, slot).start()

        @pl.when(i == nstep - 1)
        def _():
            out_copy(i, slot).wait()
            if nstep > 1:
                out_copy(i - 1, 1 - slot).wait()

    nstep = T // P
    row = lambda w: pl.BlockSpec((1, w), lambda i: (0, 0))
    return pl.pallas_call(
        body, name="mix_bwd", grid=(nstep,),
        in_specs=[pl.BlockSpec((P, MIX), lambda i: (i, 0)),
                  pl.BlockSpec((P, AW), lambda i: (i, 0)),
                  pl.BlockSpec((P, AW), lambda i: (i, cb)),
                  pl.BlockSpec((P, BW), lambda i: (i, cb + 1)),
                  pl.BlockSpec((P, BW), lambda i: (i, cb + 2)),
                  pl.BlockSpec((P, BW), lambda i: (i, cb + 3)),
                  row(D), row(BW), row(BW),
                  pl.BlockSpec((G, P, P), lambda i: (0, 0, 0)),
                  pl.BlockSpec((P, G), lambda i: (0, 0))],
        out_specs=[pl.BlockSpec((P, AW), lambda i: (i, 0)),
                   pl.BlockSpec(memory_space=pl.ANY),
                   pl.BlockSpec((8, D), lambda i: (0, 0)),
                   pl.BlockSpec((8, BW), lambda i: (0, 0)),
                   pl.BlockSpec((G, P, P), lambda i: (0, 0, 0)),
                   pl.BlockSpec((P, LANES), lambda i: (0, 0))],
        out_shape=[S((T, AW), F32), S((T, cb * AW + AW + 3 * BW), BF16), S((8, D), F32), S((8, BW), F32),
                   S((G, P, P), F32), S((P, LANES), F32)],
        scratch_shapes=[pltpu.VMEM((P, BW), F32), pltpu.VMEM((2, P, AW + 3 * BW), BF16),
                        pltpu.SemaphoreType.DMA((2,))],
        compiler_params=_cp(ARB),
    )(d_ocat, o, proj, proj, proj, proj, head_norm_w, ln_w, ln_b, w_sp, bs_t)


def _out_proj_loss(ocat, w_out, x, target, fnw):
    T, MIX = ocat.shape
    DM = x.shape[1]
    tm = _tile(T, 256, 8)

    def body(oc_ref, w_ref, x_ref, t_ref, fw_ref, dh_ref, dhb_ref, doc_ref, loss_ref, gfw_ref):
        @pl.when(pl.program_id(0) == 0)
        def _():
            loss_ref[...] = jnp.zeros_like(loss_ref)
            gfw_ref[...] = jnp.zeros_like(gfw_ref)

        wv = w_ref[...]
        hh = x_ref[...] + jnp.dot(oc_ref[...].astype(MXU), wv.astype(MXU), preferred_element_type=F32)
        rs = lax.rsqrt(jnp.mean(hh * hh, axis=-1, keepdims=True) + EPS)
        hn = hh * rs
        fw = fw_ref[...]
        e = hn * fw - t_ref[...]
        row_loss = 0.5 * jnp.mean(e * e, axis=-1, keepdims=True)
        loss_ref[...] += jnp.sum(row_loss, axis=0, keepdims=True)
        dy = e * (1.0 / DM)
        gfw_ref[0:1, :] += jnp.sum(dy * hn, axis=0, keepdims=True)
        dhn = dy * fw
        dh = rs * (dhn - hn * jnp.mean(dhn * hn, axis=-1, keepdims=True))
        dh_ref[...] = dh
        dhb = dh.astype(BF16)
        dhb_ref[...] = dhb
        doc_ref[...] = _mm_nt(dhb, wv)

    return pl.pallas_call(
        body, name="out_proj_loss", grid=(T // tm,),
        in_specs=[pl.BlockSpec((tm, MIX), lambda i: (i, 0)),
                  pl.BlockSpec((MIX, DM), lambda i: (0, 0)),
                  pl.BlockSpec((tm, DM), lambda i: (i, 0)),
                  pl.BlockSpec((tm, DM), lambda i: (i, 0)),
                  pl.BlockSpec((1, DM), lambda i: (0, 0))],
        out_specs=[pl.BlockSpec((tm, DM), lambda i: (i, 0)),
                   pl.BlockSpec((tm, DM), lambda i: (i, 0)),
                   pl.BlockSpec((tm, MIX), lambda i: (i, 0)),
                   pl.BlockSpec((8, LANES), lambda i: (0, 0)),
                   pl.BlockSpec((8, DM), lambda i: (0, 0))],
        out_shape=[S((T, DM), F32), S((T, DM), BF16), S((T, MIX), F32), S((8, LANES), F32), S((8, DM), F32)],
        compiler_params=_cp(ARB),
    )(ocat, w_out, x, target, fnw)


def _grad_w(lhs, rhs, name):
    T, A = lhs.shape
    B = rhs.shape[1]
    ta = _tile(A, 512, LANES)
    tk = _tile(T, 1024, 16)
    nk = T // tk

    def body(l_ref, r_ref, out_ref, acc_ref):
        k = pl.program_id(1)
        part = _mm_tn(l_ref[...], r_ref[...])

        @pl.when(k == 0)
        def _():
            acc_ref[...] = part

        @pl.when(k > 0)
        def _():
            acc_ref[...] += part

        @pl.when(k == nk - 1)
        def _():
            out_ref[...] = acc_ref[...].astype(BF16)

    return pl.pallas_call(
        body, name=name, grid=(A // ta, nk),
        in_specs=[pl.BlockSpec((tk, ta), lambda i, k: (k, i)),
                  pl.BlockSpec((tk, B), lambda i, k: (k, 0))],
        out_specs=pl.BlockSpec((ta, B), lambda i, k: (i, 0)),
        out_shape=S((A, B), BF16),
        scratch_shapes=[pltpu.VMEM((ta, B), F32)],
        compiler_params=_cp(ARB, ARB),
    )(lhs, rhs)


def _grad_w_in(xn, dmain, dba):
    T, DM = xn.shape
    NM = dmain.shape[1]
    tn = _tile(NM, 1024, LANES)
    tk = _tile(T, 2048, 16)

    def body(xn_ref, dm_ref, dba_ref, gm_ref, gba_ref):
        j = pl.program_id(0)
        k = pl.program_id(1)

        @pl.when(k == 0)
        def _():
            gm_ref[...] = jnp.zeros_like(gm_ref)

        @pl.when((k == 0) & (j == 0))
        def _():
            gba_ref[...] = jnp.zeros_like(gba_ref)

        xv = xn_ref[...]
        gm_ref[...] += _mm_tn(xv, dm_ref[...])

        @pl.when(j == 0)
        def _():
            gba_ref[...] += _mm_tn(xv, dba_ref[...])

    return pl.pallas_call(
        body, name="grad_w_in", grid=(NM // tn, T // tk),
        in_specs=[pl.BlockSpec((tk, DM), lambda j, k: (k, 0)),
                  pl.BlockSpec((tk, tn), lambda j, k: (k, j)),
                  pl.BlockSpec((tk, LANES), lambda j, k: (k, 0))],
        out_specs=[pl.BlockSpec((DM, tn), lambda j, k: (0, j)),
                   pl.BlockSpec((DM, LANES), lambda j, k: (0, 0))],
        out_shape=[S((DM, NM), F32), S((DM, LANES), F32)],
        compiler_params=_cp(ARB, ARB),
    )(xn, dmain, dba)


def _dx(dmain, dba, w_main, w_ba, x, dh, norm_w, chip_sums, small):
    T, NM = dmain.shape
    DM = x.shape[1]
    tm = _tile(T, 512, 8)
    tk = _tile(NM, 1024, LANES)
    nk = NM // tk
    ni = T // tm
    na = len(chip_sums)

    def body(dm_ref, dba_ref, w_ref, wba_ref, x_ref, dh_ref, nw_ref, small_ref, *rest):
        ins = rest[:na]
        gx_ref, gnw_ref, gath_ref = rest[na:na + 3]
        outs = rest[na + 3:2 * na + 3]
        acc_ref, csend, crecv, ssend, srecv, lsem = rest[2 * na + 3:]
        i = pl.program_id(0)
        k = pl.program_id(1)
        me, small_cps = _broadcast_copies([small_ref], [gath_ref], ssend, srecv)
        cps = (_chip_exchange_copies(ins, outs, csend, crecv) + small_cps
               + [pltpu.make_async_copy(small_ref, gath_ref.at[me], lsem.at[0])])

        @pl.when((i == 0) & (k == 0))
        def _():
            gnw_ref[...] = jnp.zeros_like(gnw_ref)
            for cp in cps:
                cp.start()

        @pl.when(k == 0)
        def _():
            acc_ref[...] = _mm_nt(dba_ref[...], wba_ref[...])

        acc_ref[...] += _mm_nt(dm_ref[...], w_ref[...])

        @pl.when(k == nk - 1)
        def _():
            xv = x_ref[...]
            rs = lax.rsqrt(jnp.mean(xv * xv, axis=-1, keepdims=True) + EPS)
            xh = xv * rs
            dxn = acc_ref[...]
            gnw_ref[0:1, :] += jnp.sum(dxn * xh, axis=0, keepdims=True)
            dxh = dxn * nw_ref[...]
            gx_ref[...] = dh_ref[...] + rs * (dxh - xh * jnp.mean(dxh * xh, axis=-1, keepdims=True))

        @pl.when((i == ni - 1) & (k == nk - 1))
        def _():
            for cp in cps:
                cp.wait()

    any_spec = pl.BlockSpec(memory_space=pl.ANY)
    res = pl.pallas_call(
        body, name="dx", grid=(ni, nk),
        in_specs=[pl.BlockSpec((tm, tk), lambda i, k: (i, k)),
                  pl.BlockSpec((tm, LANES), lambda i, k: (i, 0)),
                  pl.BlockSpec((DM, tk), lambda i, k: (0, k)),
                  pl.BlockSpec((DM, LANES), lambda i, k: (0, 0)),
                  pl.BlockSpec((tm, DM), lambda i, k: (i, 0)),
                  pl.BlockSpec((tm, DM), lambda i, k: (i, 0)),
                  pl.BlockSpec((1, DM), lambda i, k: (0, 0)),
                  any_spec] + [any_spec] * na,
        out_specs=[pl.BlockSpec((tm, DM), lambda i, k: (i, 0)),
                   pl.BlockSpec((8, DM), lambda i, k: (0, 0)),
                   any_spec] + [any_spec] * na,
        out_shape=[S((T, DM), F32), S((8, DM), F32), S((N_DEV,) + small.shape, F32)]
        + [S((3,) + a.shape[1:], a.dtype) for a in chip_sums],
        scratch_shapes=[pltpu.VMEM((tm, DM), F32),
                        pltpu.SemaphoreType.DMA((na, 3)), pltpu.SemaphoreType.DMA((na, 3)),
                        pltpu.SemaphoreType.DMA((1, N_DEV - 1)), pltpu.SemaphoreType.DMA((1, N_DEV - 1)),
                        pltpu.SemaphoreType.DMA((1,))],
        compiler_params=_cp(ARB, ARB),
    )(dmain, dba, w_main, w_ba, x, dh, norm_w, small, *chip_sums)
    return res[0], res[1], res[2], res[3:]


def _sum_slots(gath):
    _, R, C = gath.shape
    tr = _tile(R, 512, 8)

    def body(g_ref, o_ref):
        tot = g_ref[0]
        for d in range(1, N_DEV):
            tot = tot + g_ref[d]
        o_ref[...] = tot

    return pl.pallas_call(
        body, name="sum_slots", grid=(R // tr,),
        in_specs=[pl.BlockSpec((N_DEV, tr, C), lambda i: (0, i, 0))],
        out_specs=pl.BlockSpec((tr, C), lambda i: (i, 0)),
        out_shape=S((R, C), F32), compiler_params=_cp(ARB),
    )(gath)


def _prep_a_bwd_pointwise(dq, dk, dv, c, H, D):
    T = c.shape[0]
    AW = H * D
    C3 = 3 * AW
    tb = _tile(T, 256, 8)
    scale = float(D) ** -0.5

    def body(dq_ref, dk_ref, dv_ref, c_ref, dc_ref):
        for h in range(H):
            for part, d_ref, sc in ((0, dq_ref, scale), (1, dk_ref, 1.0)):
                sl = slice(part * AW + h * D, part * AW + (h + 1) * D)
                cv = c_ref[:, sl]
                raw = _silu(cv)
                rs = lax.rsqrt(jnp.sum(raw * raw, axis=-1, keepdims=True) + EPS)
                nrm = raw * rs
                dn = d_ref[:, h * D:(h + 1) * D] * sc
                draw = rs * (dn - nrm * jnp.sum(dn * nrm, axis=-1, keepdims=True))
                dc_ref[:, sl] = draw * _dsilu(cv)
        dc_ref[:, 2 * AW:] = dv_ref[...] * _dsilu(c_ref[:, 2 * AW:])

    return pl.pallas_call(
        body, name="prep_a_bwd_pointwise", grid=(T // tb,),
        in_specs=[pl.BlockSpec((tb, AW), lambda i: (i, 0)),
                  pl.BlockSpec((tb, AW), lambda i: (i, 0)),
                  pl.BlockSpec((tb, AW), lambda i: (i, 0)),
                  pl.BlockSpec((tb, C3), lambda i: (i, 0))],
        out_specs=pl.BlockSpec((tb, C3), lambda i: (i, 0)),
        out_shape=S((T, C3), F32),
        compiler_params=_cp(ARB),
    )(dq, dk, dv, c)


def _conv_bwd(dc, proj, conv_w, dmain, C3):
    T = dc.shape[0]
    tb = _tile(T, 256, 8)
    nblk = T // tb
    r8 = tb // 8

    def body(dc_ref, dnext_ref, x_ref, halo_ref, cw_ref, dmain_in_ref, dx_ref, gcw_ref):
        del dmain_in_ref
        i = pl.program_id(0)

        @pl.when(i == 0)
        def _():
            gcw_ref[...] = jnp.zeros_like(gcw_ref)

        dcv = dc_ref[...]
        dnext = dnext_ref[...] * (i < nblk - 1).astype(F32)
        dcp = jnp.concatenate([dcv, dnext], axis=0)
        cw = cw_ref[...]
        dx = cw[3:4, :] * dcv
        for j in range(3):
            dx = dx + cw[j:j + 1, :] * dcp[3 - j:3 - j + tb]
        dx_ref[...] = dx.astype(BF16)
        halo = halo_ref[...] * (i > 0).astype(F32)
        xp = jnp.concatenate([halo, x_ref[...]], axis=0)
        for j in range(4):
            gcw_ref[j:j + 1, :] += jnp.sum(dcv * xp[5 + j:5 + j + tb], axis=0, keepdims=True)

    return pl.pallas_call(
        body, name="conv_bwd", grid=(nblk,),
        in_specs=[pl.BlockSpec((tb, C3), lambda i: (i, 0)),
                  pl.BlockSpec((8, C3), lambda i: (jnp.minimum((i + 1) * r8, T // 8 - 1), 0)),
                  pl.BlockSpec((tb, C3), lambda i: (i, 0)),
                  pl.BlockSpec((8, C3), lambda i: (jnp.maximum(i * r8 - 1, 0), 0)),
                  pl.BlockSpec((4, C3), lambda i: (0, 0)),
                  pl.BlockSpec(memory_space=pl.ANY)],
        out_specs=[pl.BlockSpec((tb, C3), lambda i: (i, 0)),
                   pl.BlockSpec((8, C3), lambda i: (0, 0))],
        out_shape=[S(dmain.shape, dmain.dtype), S((8, C3), F32)],
        input_output_aliases={5: 0},
        compiler_params=_cp(ARB),
    )(dc, dc, proj, proj, conv_w, dmain)


def _adam_math(w, g, m, v):
    m2 = ADAM_B1 * m + (1.0 - ADAM_B1) * g
    v2 = ADAM_B2 * v + (1.0 - ADAM_B2) * (g * g)
    m_hat = m2 / (1.0 - ADAM_B1 ** ADAM_STEP)
    v_hat = v2 / (1.0 - ADAM_B2 ** ADAM_STEP)
    delta = -ADAM_LR * (m_hat / (jnp.sqrt(v_hat) + ADAM_EPS) + ADAM_WD * w)
    return delta, m2, v2


def _pair_sum(blocks, recv, core, name):
    K, _, R, C = blocks.shape
    tr = _tile(R, 256, 16)

    def body(core_ref, a_ref, b_ref, o_ref):
        del core_ref
        o_ref[0] = (a_ref[0, 0].astype(F32) + b_ref[0].astype(F32)).astype(BF16)

    spec = lambda: pl.BlockSpec((1, tr, C), lambda k, i, core_ref: (k, i, 0))
    return pl.pallas_call(
        body, name=name,
        grid_spec=pltpu.PrefetchScalarGridSpec(
            num_scalar_prefetch=1, grid=(K, R // tr),
            in_specs=[pl.BlockSpec((1, 1, tr, C), lambda k, i, core_ref: (k, core_ref[0], i, 0)), spec()],
            out_specs=spec()),
        out_shape=S((K, R, C), BF16), compiler_params=_cp(ARB, ARB),
    )(core, blocks, recv)


def _sum_adam(chip_sums, recv, w, m, v, chip, name):
    R, C = w.shape
    tr = _tile(R, 256, 16)

    def body(chip_ref, own_ref, r_ref, w_ref, m_ref, v_ref, g_ref, d_ref, m2_ref, v2_ref):
        del chip_ref
        g = own_ref[0].astype(F32)
        for j in range(3):
            g = g + r_ref[j].astype(F32)
        g_ref[...] = g
        d_ref[...], m2_ref[...], v2_ref[...] = _adam_math(w_ref[...], g, m_ref[...], v_ref[...])

    spec = lambda: pl.BlockSpec((tr, C), lambda i, chip_ref: (i, 0))
    return pl.pallas_call(
        body, name=name,
        grid_spec=pltpu.PrefetchScalarGridSpec(
            num_scalar_prefetch=1, grid=(R // tr,),
            in_specs=[pl.BlockSpec((1, tr, C), lambda i, chip_ref: (chip_ref[0], i, 0)),
                      pl.BlockSpec((3, tr, C), lambda i, chip_ref: (0, i, 0)), spec(), spec(), spec()],
            out_specs=[spec(), spec(), spec(), spec()]),
        out_shape=[S((R, C), F32)] * 4, compiler_params=_cp(ARB),
    )(chip, chip_sums, recv, w, m, v)


def _adam_small(w, g, m, v):
    R, C = w.shape
    tr = _tile(R, 512, 8)

    def body(w_ref, g_ref, m_ref, v_ref, d_ref, m2_ref, v2_ref):
        d_ref[...], m2_ref[...], v2_ref[...] = _adam_math(w_ref[...], g_ref[...], m_ref[...], v_ref[...])

    spec = lambda: pl.BlockSpec((tr, C), lambda i: (i, 0))
    return pl.pallas_call(
        body, name="adam_small", grid=(R // tr,), in_specs=[spec()] * 4, out_specs=[spec()] * 3,
        out_shape=[S((R, C), F32)] * 3, compiler_params=_cp(ARB),
    )(w, g, m, v)


def _position():
    return lax.axis_index("x"), lax.axis_index("y"), lax.axis_index("c")


def _all_gather_weights(arrs):
    na = len(arrs)

    def body(*refs):
        ins, outs = refs[:na], refs[na:2 * na]
        send_sems, recv_sems, local_sems = refs[2 * na:]
        x, y, c = _position()
        me, sibling = (x, y, c), (x, y, 1 - c)
        chips = [(1 - x, y), (x, 1 - y), (1 - x, 1 - y)]

        def slot(a, p):
            return outs[a].at[4 * p[0] + 2 * p[1] + p[2]]

        def copy(a, kk, block, to, src=None):
            return pltpu.make_async_remote_copy(
                src_ref=slot(a, block) if src is None else src, dst_ref=slot(a, block),
                send_sem=send_sems.at[a, kk], recv_sem=recv_sems.at[a, kk], device_id=to, device_id_type=MESH)

        mine = [pltpu.make_async_copy(ins[a], slot(a, me), local_sems.at[a]) for a in range(na)]
        for cp in mine:
            cp.start()
        first = []
        for a in range(na):
            first.append(copy(a, 0, me, sibling, src=ins[a]))
            first += [copy(a, 1 + j, me, (*chip, c), src=ins[a]) for j, chip in enumerate(chips)]
        for cp in first:
            cp.start()
        passed = []
        for j, chip in enumerate(chips):
            for a in range(na):
                copy(a, 1 + j, (*chip, c), me).wait_recv()
                fw = copy(a, 4 + j, (*chip, c), sibling)
                fw.start()
                passed.append(fw)
        for a in range(na):
            copy(a, 0, sibling, me).wait_recv()
            for j, chip in enumerate(chips):
                copy(a, 4 + j, (*chip, 1 - c), me).wait_recv()
        for cp in first + passed:
            cp.wait_send()
        for cp in mine:
            cp.wait()

    any_spec = pl.BlockSpec(memory_space=pl.ANY)
    return pl.pallas_call(
        body, name="all_gather_weights",
        in_specs=[any_spec] * na, out_specs=[any_spec] * na,
        out_shape=[S((N_DEV,) + a.shape, a.dtype) for a in arrs],
        scratch_shapes=[pltpu.SemaphoreType.DMA((na, 7)), pltpu.SemaphoreType.DMA((na, 7)),
                        pltpu.SemaphoreType.DMA((na,))],
    )(*arrs)


def _exchange_sibling(arrs):
    na = len(arrs)
    K = arrs[0].shape[0]

    def body(*refs):
        ins, outs = refs[:na], refs[na:2 * na]
        send_sems, recv_sems = refs[2 * na:]
        x, y, c = _position()
        cps = [pltpu.make_async_remote_copy(src_ref=ins[a].at[k, 1 - c], dst_ref=outs[a].at[k],
                                            send_sem=send_sems.at[a, k], recv_sem=recv_sems.at[a, k],
                                            device_id=(x, y, 1 - c), device_id_type=MESH)
               for a in range(na) for k in range(K)]
        for cp in cps:
            cp.start()
        for cp in cps:
            cp.wait()

    any_spec = pl.BlockSpec(memory_space=pl.ANY)
    return pl.pallas_call(
        body, name="exchange_sibling", in_specs=[any_spec] * na, out_specs=[any_spec] * na,
        out_shape=[S((K,) + a.shape[2:], a.dtype) for a in arrs],
        scratch_shapes=[pltpu.SemaphoreType.DMA((na, K)), pltpu.SemaphoreType.DMA((na, K))],
    )(*arrs)


def _chip_exchange_copies(ins, outs, send_sems, recv_sems):
    x, y, c = _position()
    chips = [(1 - x, y), (x, 1 - y), (1 - x, 1 - y)]
    return [pltpu.make_async_remote_copy(
        src_ref=ins[a].at[2 * qx + qy], dst_ref=outs[a].at[j], send_sem=send_sems.at[a, j],
        recv_sem=recv_sems.at[a, j], device_id=(qx, qy, c), device_id_type=MESH)
        for a in range(len(ins)) for j, (qx, qy) in enumerate(chips)]


def _broadcast_copies(srcs, dsts, send_sems, recv_sems):
    x, y, c = _position()
    me = 4 * x + 2 * y + c
    cps = []
    for a in range(len(srcs)):
        for k in range(1, N_DEV):
            peer = (1 - x if k & 4 else x, 1 - y if k & 2 else y, 1 - c if k & 1 else c)
            cps.append(pltpu.make_async_remote_copy(
                src_ref=srcs[a], dst_ref=dsts[a].at[me], send_sem=send_sems.at[a, k - 1],
                recv_sem=recv_sems.at[a, k - 1], device_id=peer, device_id_type=MESH))
    return me, cps


def _all_reduce_small(part):
    R, C = part.shape

    def body(p_ref, out_ref, gath_ref, send_sems, recv_sems):
        me, cps = _broadcast_copies([p_ref], [gath_ref], send_sems, recv_sems)
        gath_ref[me] = p_ref[...]
        for cp in cps:
            cp.start()
        for cp in cps:
            cp.wait()
        acc = gath_ref[0]
        for d in range(1, N_DEV):
            acc = acc + gath_ref[d]
        out_ref[...] = acc

    vm = pl.BlockSpec(memory_space=pltpu.VMEM)
    return pl.pallas_call(
        body, name="all_reduce_small", in_specs=[vm], out_specs=vm, out_shape=S((R, C), F32),
        scratch_shapes=[pltpu.VMEM((N_DEV, R, C), F32), pltpu.SemaphoreType.DMA((1, N_DEV - 1)),
                        pltpu.SemaphoreType.DMA((1, N_DEV - 1))],
    )(part)


def _pack(parts):
    rows = []
    for p in parts:
        f = p.reshape(-1).astype(F32)
        pad = (-f.shape[0]) % (8 * LANES)
        rows.append(jnp.pad(f, (0, pad)).reshape(-1, LANES))
    return jnp.concatenate(rows, axis=0)


def _unpack(buf, shapes):
    out, r = [], 0
    for shp in shapes:
        n = 1
        for s in shp:
            n *= s
        nr = -(-n // (8 * LANES)) * 8
        out.append(buf[r:r + nr].reshape(-1)[:n].reshape(shp))
        r += nr
    return out


def kernel(x, norm_w, w_in, conv_w, a_log, dt_bias, head_norm_w, sgu_ln_w, sgu_ln_b, w_spatial, b_spatial, w_out, final_norm_w, loss_target, m_norm_w, m_w_in, m_conv_w, m_a_log, m_dt_bias, m_head_norm_w, m_sgu_ln_w, m_sgu_ln_b, m_w_spatial, m_b_spatial, m_w_out, m_final_norm_w, v_norm_w, v_w_in, v_conv_w, v_a_log, v_dt_bias, v_head_norm_w, v_sgu_ln_w, v_sgu_ln_b, v_w_spatial, v_b_spatial, v_w_out, v_final_norm_w):
    T, DM = x.shape[1], x.shape[2]
    H, D = a_log.shape[1], head_norm_w.shape[1]
    G, P = w_spatial.shape[1], w_spatial.shape[2]
    AW, BW = H * D, G * P
    MIX = AW + BW
    WD = w_in.shape[2]
    IN = N_DEV * WD
    RO = w_out.shape[1]
    CW = conv_w.shape[2]
    sizes = (3 * AW, AW, H, H, BW, BW, BW)
    assert sum(sizes) == IN and 2 * H <= LANES and 3 * H <= 32 and N_DEV * RO == MIX and N_DEV * CW == 3 * AW
    offs = [0]
    for s in sizes:
        offs.append(offs[-1] + s)
    px, py, pc = _position()
    dev = 4 * px + 2 * py + pc
    chip = 2 * px + py

    x2, tgt = x[0], loss_target[0]

    (g_win,) = _all_gather_weights([_cast_bf16(w_in[0], "cast_w_in")])
    w_main, w_ba = _relayout_w(g_win, offs[2], offs[4])
    alog_row = jnp.pad(a_log, ((0, 0), (H, LANES - 2 * H)))
    dtb_row = jnp.pad(dt_bias, ((0, 0), (H, LANES - 2 * H)))
    bs_t = b_spatial[0].T

    proj, ba, xn, (g_wout, g_conv) = _in_proj(x2, norm_w, w_main, w_ba, [_cast_bf16(w_out[0], "cast_w_out"), conv_w[0]])
    w_out_full = g_wout.reshape(MIX, DM)
    conv_full = g_conv.transpose(1, 0, 2).reshape(4, 3 * AW)
    q, k, v, c, gcol, grow = _prep_a_fwd(proj, ba, conv_full, alog_row, dtb_row, H, D)
    o, vnew, ssave, asave = _delta_fwd(q, k, v, gcol, grow, H, D)
    ocat = _mix_fwd(o, proj, head_norm_w, sgu_ln_w, sgu_ln_b, w_spatial[0], bs_t, H, D, G, P)
    dh, dh_bf, d_ocat, loss_acc, g_fnw = _out_proj_loss(ocat, w_out_full, x2, tgt, final_norm_w.reshape(1, DM))
    loss = lax.psum(loss_acc[0, 0], AXES)

    g_wout_part = _grad_w(ocat, dh_bf, "grad_w_out")
    d_o, dmain, g_hnw, g_ln, g_wsp, g_bs_t = _mix_bwd(
        d_ocat, o, proj, head_norm_w, sgu_ln_w, sgu_ln_b, w_spatial[0], bs_t, H, D, G, P)
    dq, dk, dv, dgate, dpar = _delta_bwd(q, k, v, gcol, grow, ba, vnew, ssave, asave, d_o, a_log, dt_bias, H, D)
    dc = _prep_a_bwd_pointwise(dq, dk, dv, c, H, D)
    dmain, g_conv_part = _conv_bwd(dc, proj, conv_full, dmain, 3 * AW)
    dba = dgate.astype(BF16)
    g_main, g_ba = _grad_w_in(xn, dmain, dba)

    g_win_blocks = _relayout_g(g_main, g_ba, WD, offs[2], offs[4]).reshape(4, 2, DM, WD)
    g_wout_blocks = g_wout_part.reshape(4, 2, RO, DM)
    core_idx = jnp.reshape(pc, (1,)).astype(jnp.int32)
    chip_idx = jnp.reshape(chip, (1,)).astype(jnp.int32)
    from_sib = _exchange_sibling([g_win_blocks, g_wout_blocks])
    chip_win = _pair_sum(g_win_blocks, from_sib[0], core_idx, "pair_sum_w_in")
    chip_wout = _pair_sum(g_wout_blocks, from_sib[1], core_idx, "pair_sum_w_out")
    small_shapes = [a_log.shape, dt_bias.shape, head_norm_w.shape, sgu_ln_w.shape, sgu_ln_b.shape,
                    w_spatial.shape, b_spatial.shape, final_norm_w.shape]
    parts = [dpar[0, :H], dpar[0, H:2 * H], g_hnw[0], g_ln[0], g_ln[1], g_wsp, g_bs_t[:, :G].T, g_fnw[0],
             g_conv_part[:4]]
    grad_x, g_nw, small_gath, (recv_win, recv_wout) = _dx(
        dmain, dba, w_main, w_ba, x2, dh, norm_w, [chip_win, chip_wout], _pack(parts))
    red = _sum_slots(small_gath)
    grad_w_in, delta_w_in, new_m_w_in, new_v_w_in = _sum_adam(
        chip_win, recv_win, w_in[0], m_w_in[0], v_w_in[0], chip_idx, "sum_adam_w_in")
    grad_w_out, delta_w_out, new_m_w_out, new_v_w_out = _sum_adam(
        chip_wout, recv_wout, w_out[0], m_w_out[0], v_w_out[0], chip_idx, "sum_adam_w_out")
    red_nw = _all_reduce_small(_pack([g_nw[0]]))
    grads_small = _unpack(red_nw, [norm_w.shape]) + _unpack(red, small_shapes + [(4, 3 * AW)])
    g_conv_full = grads_small.pop()
    grad_conv = lax.dynamic_slice_in_dim(g_conv_full, dev * CW, CW, axis=1)[None]
    small_w = [norm_w, a_log, dt_bias, head_norm_w, sgu_ln_w, sgu_ln_b, w_spatial, b_spatial, final_norm_w, conv_w]
    small_m = [m_norm_w, m_a_log, m_dt_bias, m_head_norm_w, m_sgu_ln_w, m_sgu_ln_b, m_w_spatial, m_b_spatial,
               m_final_norm_w, m_conv_w]
    small_v = [v_norm_w, v_a_log, v_dt_bias, v_head_norm_w, v_sgu_ln_w, v_sgu_ln_b, v_w_spatial, v_b_spatial,
               v_final_norm_w, v_conv_w]
    small_g = grads_small + [grad_conv]
    shapes10 = [w.shape for w in small_w]
    d_p, m_p, v_p = _adam_small(_pack(small_w), _pack(small_g), _pack(small_m), _pack(small_v))
    d_s, m_s, v_s = _unpack(d_p, shapes10), _unpack(m_p, shapes10), _unpack(v_p, shapes10)

    def order(small, win, wout):
        return [small[0], win[None], small[9], small[1], small[2], small[3], small[4], small[5], small[6], small[7],
                wout[None], small[8]]

    grads = order(small_g, grad_w_in, grad_w_out)
    deltas = order(d_s, delta_w_in, delta_w_out)
    new_m = order(m_s, new_m_w_in, new_m_w_out)
    new_v = order(v_s, new_v_w_in, new_v_w_out)
    return (loss, grad_x[None], *grads, *deltas, *new_m, *new_v)
```

```python
import functools

import jax
import jax.numpy as jnp
from jax import lax
from jax.experimental import pallas as pl
from jax.experimental.pallas import tpu as pltpu

F32 = jnp.float32
BF16 = jnp.bfloat16
MXU = jnp.bfloat16
HI = lax.Precision.HIGHEST
EPS = 1e-6
CHUNK_A = 64
LANES = 128
MESH = pl.DeviceIdType.MESH
AXES = ("x", "y", "c")
N_DEV = 8

ADAM_LR = 0.001
ADAM_B1 = 0.9
ADAM_B2 = 0.999
ADAM_EPS = 1e-08
ADAM_WD = 0.01
ADAM_STEP = 10

S = jax.ShapeDtypeStruct
ARB = "arbitrary"


def _cp(*sem):
    return pltpu.CompilerParams(dimension_semantics=tuple(sem), vmem_limit_bytes=56 * 1024 * 1024)


def _tile(n, cap, mult):
    best = None
    t = mult
    while t <= min(n, cap):
        if n % t == 0:
            best = t
        t += mult
    return best if best is not None else n


def _mm(a, b):
    return jnp.dot(a.astype(MXU), b.astype(MXU), preferred_element_type=F32)


def _mm_nt(a, b):
    return lax.dot_general(a.astype(MXU), b.astype(MXU), (((1,), (1,)), ((), ())), preferred_element_type=F32)


def _mm_tn(a, b):
    return lax.dot_general(a.astype(MXU), b.astype(MXU), (((0,), (0,)), ((), ())), preferred_element_type=F32)


def _mmh(a, b):
    return jnp.dot(a, b, precision=HI, preferred_element_type=F32)


def _mmh_tn(a, b):
    return lax.dot_general(a, b, (((0,), (0,)), ((), ())), precision=HI, preferred_element_type=F32)


def _sigmoid(x):
    return 1.0 / (1.0 + jnp.exp(-x))


def _silu(x):
    return x * _sigmoid(x)


def _dsilu(x):
    s = _sigmoid(x)
    return s * (1.0 + x * (1.0 - s))


def _softplus(x):
    return jnp.maximum(x, 0.0) + jnp.log(1.0 + jnp.exp(-jnp.abs(x)))


def _pieces(wd, gate_lo, gate_hi, total):
    out = []
    for d in range(N_DEV):
        lo, hi = d * wd, (d + 1) * wd
        for dest, a, b, shift in (("main", 0, gate_lo, 0), ("gate", gate_lo, gate_hi, -gate_lo),
                                  ("main", gate_hi, total, gate_lo - gate_hi)):
            s0, s1 = max(lo, a), min(hi, b)
            if s0 < s1:
                out.append((d, s0 - lo, s1 - lo, dest, s0 + shift))
    return out


def _cast_bf16(a, name):
    R, C = a.shape
    tr = _tile(R, 256, 16)

    def body(a_ref, o_ref):
        o_ref[...] = a_ref[...].astype(BF16)

    spec = pl.BlockSpec((tr, C), lambda i: (i, 0))
    return pl.pallas_call(body, name=name, grid=(R // tr,), in_specs=[spec], out_specs=spec,
                          out_shape=S((R, C), BF16), compiler_params=_cp(ARB))(a)


def _relayout_w(g_win, gate_lo, gate_hi):
    _, DM, WD = g_win.shape
    total = N_DEV * WD
    NM = total - (gate_hi - gate_lo)
    tr = _tile(DM, 256, 16)
    plan = _pieces(WD, gate_lo, gate_hi, total)

    def body(g_ref, main_ref, gate_ref):
        gate_ref[...] = jnp.zeros_like(gate_ref)
        for d, s0, s1, dest, c0 in plan:
            dst = main_ref if dest == "main" else gate_ref
            dst[:, c0:c0 + (s1 - s0)] = g_ref[d, :, s0:s1]

    return pl.pallas_call(
        body, name="relayout_w", grid=(DM // tr,),
        in_specs=[pl.BlockSpec((N_DEV, tr, WD), lambda i: (0, i, 0))],
        out_specs=[pl.BlockSpec((tr, NM), lambda i: (i, 0)), pl.BlockSpec((tr, LANES), lambda i: (i, 0))],
        out_shape=[S((DM, NM), g_win.dtype), S((DM, LANES), g_win.dtype)],
        compiler_params=_cp(ARB),
    )(g_win)


def _relayout_g(g_main, g_gate, WD, gate_lo, gate_hi):
    DM = g_main.shape[0]
    total = N_DEV * WD
    tr = _tile(DM, 256, 16)
    plan = _pieces(WD, gate_lo, gate_hi, total)

    def body(m_ref, gate_ref, out_ref):
        for d, s0, s1, dest, c0 in plan:
            src = m_ref if dest == "main" else gate_ref
            out_ref[d, :, s0:s1] = src[:, c0:c0 + (s1 - s0)].astype(BF16)

    return pl.pallas_call(
        body, name="relayout_g", grid=(DM // tr,),
        in_specs=[pl.BlockSpec((tr, g_main.shape[1]), lambda i: (i, 0)), pl.BlockSpec((tr, LANES), lambda i: (i, 0))],
        out_specs=pl.BlockSpec((N_DEV, tr, WD), lambda i: (0, i, 0)),
        out_shape=S((N_DEV, DM, WD), BF16),
        compiler_params=_cp(ARB),
    )(g_main, g_gate)


def _in_proj(x, norm_w, w_main, w_ba, shards):
    T, DM = x.shape
    NM = w_main.shape[1]
    tm = _tile(T, 1024, 8)
    tn = _tile(NM, 1024, LANES)
    ni, nj = T // tm, NM // tn
    ns = len(shards)

    def body(x_ref, nw_ref, w_ref, wba_ref, *rest):
        srcs = rest[:ns]
        proj_ref, ba_ref, xn_ref = rest[ns:ns + 3]
        gath = rest[ns + 3:2 * ns + 3]
        send_sems, recv_sems, local_sems = rest[2 * ns + 3:]
        i = pl.program_id(0)
        me, cps = _broadcast_copies(srcs, gath, send_sems, recv_sems)
        cps = cps + [pltpu.make_async_copy(srcs[a], gath[a].at[me], local_sems.at[a]) for a in range(ns)]

        @pl.when((i == 0) & (pl.program_id(1) == 0))
        def _():
            for cp in cps:
                cp.start()

        @pl.when((i == ni - 1) & (pl.program_id(1) == nj - 1))
        def _():
            for cp in cps:
                cp.wait()

        @pl.when(pl.program_id(1) == 0)
        def _():
            xv = x_ref[...]
            r = lax.rsqrt(jnp.mean(xv * xv, axis=-1, keepdims=True) + EPS)
            xn = (xv * r * nw_ref[...]).astype(BF16)
            xn_ref[...] = xn
            ba_ref[...] = jnp.dot(xn.astype(MXU), wba_ref[...].astype(MXU), preferred_element_type=F32)

        proj_ref[...] = jnp.dot(xn_ref[...].astype(MXU), w_ref[...].astype(MXU), preferred_element_type=F32)

    any_spec = pl.BlockSpec(memory_space=pl.ANY)
    res = pl.pallas_call(
        body, name="in_proj", grid=(ni, nj),
        in_specs=[pl.BlockSpec((tm, DM), lambda i, j: (i, 0)),
                  pl.BlockSpec((1, DM), lambda i, j: (0, 0)),
                  pl.BlockSpec((DM, tn), lambda i, j: (0, j)),
                  pl.BlockSpec((DM, LANES), lambda i, j: (0, 0))] + [any_spec] * ns,
        out_specs=[pl.BlockSpec((tm, tn), lambda i, j: (i, j)),
                   pl.BlockSpec((tm, LANES), lambda i, j: (i, 0)),
                   pl.BlockSpec((tm, DM), lambda i, j: (i, 0))] + [any_spec] * ns,
        out_shape=[S((T, NM), F32), S((T, LANES), F32), S((T, DM), BF16)]
        + [S((N_DEV,) + a.shape, a.dtype) for a in shards],
        scratch_shapes=[pltpu.SemaphoreType.DMA((ns, N_DEV - 1)), pltpu.SemaphoreType.DMA((ns, N_DEV - 1)),
                        pltpu.SemaphoreType.DMA((ns,))],
        compiler_params=_cp(ARB, ARB),
    )(x, norm_w, w_main, w_ba, *shards)
    return res[0], res[1], res[2], res[3:]


def _prep_a_fwd(proj, ba, conv_w, alog_row, dtb_row, H, D):
    T = proj.shape[0]
    AW = H * D
    C3 = 3 * AW
    tb = _tile(T, 256, CHUNK_A)
    nch = tb // CHUNK_A
    nblk = T // tb
    scale = float(D) ** -0.5

    def body(x_ref, halo_ref, ba_ref, cw_ref, al_ref, dt_ref, q_ref, k_ref, v_ref, c_ref, gcol_ref, grow_ref):
        i = pl.program_id(0)
        xv = x_ref[...]
        halo = halo_ref[...] * (i > 0).astype(F32)
        xp = jnp.concatenate([halo, xv], axis=0)
        cw = cw_ref[...]
        c = cw[0:1, :] * xp[5:5 + tb]
        for j in range(1, 4):
            c = c + cw[j:j + 1, :] * xp[5 + j:5 + j + tb]
        c_ref[...] = c
        a = _silu(c)
        for h in range(H):
            qh = a[:, h * D:(h + 1) * D]
            kh = a[:, AW + h * D:AW + (h + 1) * D]
            qr = lax.rsqrt(jnp.sum(qh * qh, axis=-1, keepdims=True) + EPS)
            kr = lax.rsqrt(jnp.sum(kh * kh, axis=-1, keepdims=True) + EPS)
            q_ref[:, h * D:(h + 1) * D] = qh * (qr * scale)
            k_ref[:, h * D:(h + 1) * D] = kh * kr
        v_ref[...] = a[:, 2 * AW:]

        bav = ba_ref[...]
        lane = lax.broadcasted_iota(jnp.int32, (tb, LANES), 1)
        beta = _sigmoid(bav)
        g = -jnp.exp(al_ref[...]) * _softplus(bav + dt_ref[...])
        gates = jnp.where(lane < H, beta, jnp.where(lane < 2 * H, g, 0.0))
        ri = lax.broadcasted_iota(jnp.int32, (CHUNK_A, CHUNK_A), 0)
        ci = lax.broadcasted_iota(jnp.int32, (CHUNK_A, CHUNK_A), 1)
        tri = (ri >= ci).astype(F32)
        lane_c = lax.broadcasted_iota(jnp.int32, (CHUNK_A, LANES), 1)
        for cc in range(nch):
            gch = gates[cc * CHUNK_A:(cc + 1) * CHUNK_A]
            gc = pltpu.roll(_mmh(tri, gch), H, 1)
            full = jnp.where(lane_c < 2 * H, gch, jnp.where(lane_c < 3 * H, gc, 0.0))
            gcol_ref[cc * CHUNK_A:(cc + 1) * CHUNK_A, :] = full
            grow_ref[cc] = full.T[0:32, :]

    return pl.pallas_call(
        body, name="prep_a_fwd", grid=(nblk,),
        in_specs=[pl.BlockSpec((tb, C3), lambda i: (i, 0)),
                  pl.BlockSpec((8, C3), lambda i: (jnp.maximum(i * (tb // 8) - 1, 0), 0)),
                  pl.BlockSpec((tb, LANES), lambda i: (i, 0)),
                  pl.BlockSpec((4, C3), lambda i: (0, 0)),
                  pl.BlockSpec((1, LANES), lambda i: (0, 0)),
                  pl.BlockSpec((1, LANES), lambda i: (0, 0))],
        out_specs=[pl.BlockSpec((tb, AW), lambda i: (i, 0)),
                   pl.BlockSpec((tb, AW), lambda i: (i, 0)),
                   pl.BlockSpec((tb, AW), lambda i: (i, 0)),
                   pl.BlockSpec((tb, C3), lambda i: (i, 0)),
                   pl.BlockSpec((tb, LANES), lambda i: (i, 0)),
                   pl.BlockSpec((nch, 32, CHUNK_A), lambda i: (i, 0, 0))],
        out_shape=[S((T, AW), F32), S((T, AW), F32), S((T, AW), F32), S((T, C3), F32),
                   S((T, LANES), F32), S((T // CHUNK_A, 32, CHUNK_A), F32)],
        compiler_params=_cp(ARB),
    )(proj, proj, ba, conv_w, alog_row, dtb_row)


_NN = (((1,), (0,)), ((), ()))
_TN = (((0,), (0,)), ((), ()))


def _split(a):
    hi = a.astype(BF16)
    return hi, (a - hi.astype(F32)).astype(BF16)


def _mm3(a, b, dims=_NN):
    ah, al = a if isinstance(a, tuple) else _split(a)
    bh, bl = b if isinstance(b, tuple) else _split(b)
    dg = lambda p, r: lax.dot_general(p, r, dims, preferred_element_type=F32)
    return dg(ah, bh) + (dg(ah, bl) + dg(al, bh))


def _interleave(gens):
    gens = list(gens)
    while gens:
        alive = []
        for g in gens:
            try:
                next(g)
                alive.append(g)
            except StopIteration:
                pass
        gens = alive


def _chunk_terms(q, k, v, gcolv, growv, h, H):
    C = CHUNK_A
    beta_c = gcolv[:, h:h + 1]
    g_c = gcolv[:, H + h:H + h + 1]
    gc_c = gcolv[:, 2 * H + h:2 * H + h + 1]
    gc_r = growv[2 * H + h:2 * H + h + 1, :]
    ri = lax.broadcasted_iota(jnp.int32, (C, C), 0)
    ci = lax.broadcasted_iota(jnp.int32, (C, C), 1)
    incl = ri >= ci
    strict = ri > ci
    kb = k * beta_c
    vb = v * beta_c
    p_raw = _mm_nt(kb, k)
    qk_raw = _mm_nt(q, k)
    gam = jnp.where(incl, jnp.exp(jnp.where(incl, gc_c - gc_r, 0.0)), 0.0)
    e_c = jnp.exp(gc_c)
    gl = gc_r[:, C - 1:C]
    edec = jnp.exp(gl - gc_c)
    yield
    lmat = jnp.where(strict, p_raw * gam, 0.0)
    attn = jnp.where(incl, qk_raw * gam, 0.0)
    return dict(beta_c=beta_c, g_c=g_c, gc_c=gc_c, gc_r=gc_r, incl=incl, strict=strict, gam=gam, e_c=e_c,
                kb=kb, vb=vb, lmat=lmat, attn=attn, gl=gl, edec=edec, ri=ri, ci=ci)


def _inv_unit_lower(lmat):
    C = lmat.shape[0]
    ri = lax.broadcasted_iota(jnp.int32, (C, C), 0)
    ci = lax.broadcasted_iota(jnp.int32, (C, C), 1)
    eye = (ri == ci).astype(F32)
    x = -lmat
    a = eye + x
    n = 1
    while 2 * n < C:
        xs = _split(x)
        x = _mm3(xs, xs)
        yield
        a = a + _mm3(a, x)
        n *= 2
    yield
    return a


def _delta_fwd(q, k, v, gcol, grow, H, D):
    T = q.shape[0]
    C = CHUNK_A
    N = T // C
    AW = H * D

    def body(q_ref, k_ref, v_ref, gcol_ref, grow_ref, o_ref, vn_ref, ssave_ref, asave_ref, s_ref):
        @pl.when(pl.program_id(0) == 0)
        def _():
            s_ref[...] = jnp.zeros_like(s_ref)

        gcolv = gcol_ref[...]
        growv = grow_ref[0]

        def head(h):
            sl = slice(h * D, (h + 1) * D)
            st = s_ref[h]
            ssave_ref[0, h] = st
            qv, kv, vv = q_ref[:, sl], k_ref[:, sl], v_ref[:, sl]
            t = yield from _chunk_terms(qv, kv, vv, gcolv, growv, h, H)
            ks = _mm(t["kb"] * t["e_c"], st)
            o_inter = _mm(qv * t["e_c"], st)
            a = yield from _inv_unit_lower(t["lmat"])
            asave_ref[0, h] = a
            v_new = _mm3(a, t["vb"] - ks)
            yield
            vn_ref[:, sl] = v_new
            o_intra = _mm(t["attn"], v_new)
            s_upd = _mm_tn(kv * t["edec"], v_new)
            yield
            o_ref[:, sl] = o_inter + o_intra
            s_ref[h] = st * jnp.exp(t["gl"]) + s_upd

        _interleave(head(h) for h in range(H))

    blk = lambda: pl.BlockSpec((C, AW), lambda n: (n, 0))
    return pl.pallas_call(
        body, name="delta_fwd", grid=(N,),
        in_specs=[blk(), blk(), blk(),
                  pl.BlockSpec((C, LANES), lambda n: (n, 0)),
                  pl.BlockSpec((1, 32, C), lambda n: (n, 0, 0))],
        out_specs=[blk(), blk(),
                   pl.BlockSpec((1, H, D, D), lambda n: (n, 0, 0, 0)),
                   pl.BlockSpec((1, H, C, C), lambda n: (n, 0, 0, 0))],
        out_shape=[S((T, AW), F32), S((T, AW), F32), S((N, H, D, D), F32), S((N, H, C, C), F32)],
        scratch_shapes=[pltpu.VMEM((H, D, D), F32)],
        compiler_params=_cp(ARB),
    )(q, k, v, gcol, grow)


def _delta_bwd(q, k, v, gcol, grow, ba, vnew, ssave, asave, d_o, a_log, dt_bias, H, D, carry):
    T = q.shape[0]
    C = CHUNK_A
    N = T // C
    AW = H * D
    nc = len(carry)

    def body(al_ref, dt_ref, q_ref, k_ref, v_ref, gcol_ref, grow_ref, ba_ref, vn_ref, ss_ref, as_ref, do_ref, *rest):
        cins = rest[:nc]
        dq_ref, dk_ref, dv_ref, dgate_ref, dpar_ref = rest[nc:nc + 5]
        couts = rest[nc + 5:2 * nc + 5]
        ds_ref, csend, crecv = rest[2 * nc + 5:]
        ccps = _chip_exchange_copies(cins, couts, csend, crecv)

        @pl.when(pl.program_id(0) == 0)
        def _():
            ds_ref[...] = jnp.zeros_like(ds_ref)
            dpar_ref[...] = jnp.zeros_like(dpar_ref)
            for cp in ccps:
                cp.start()

        gcolv = gcol_ref[...]
        growv = grow_ref[0]
        bav = ba_ref[...]
        lane = lax.broadcasted_iota(jnp.int32, (C, LANES), 1)
        lane1 = lax.broadcasted_iota(jnp.int32, (1, LANES), 1)
        rowi = lax.broadcasted_iota(jnp.int32, (C, 1), 0)
        acc = {"dgate": jnp.zeros((C, LANES), F32), "dpar": jnp.zeros((1, LANES), F32)}

        def head(h):
            sl = slice(h * D, (h + 1) * D)
            ds_next = ds_ref[h]
            st = ss_ref[0, h]
            a = as_ref[0, h]
            qv, kv, vv, dov, v_new = q_ref[:, sl], k_ref[:, sl], v_ref[:, sl], do_ref[:, sl], vn_ref[:, sl]
            t = yield from _chunk_terms(qv, kv, vv, gcolv, growv, h, H)
            beta_c, e_c, gam, kb = t["beta_c"], t["e_c"], t["gam"], t["kb"]
            incl, strict, attn, lmat, edec = t["incl"], t["strict"], t["attn"], t["lmat"], t["edec"]
            kdec = kv * edec
            egl = jnp.exp(t["gl"])
            qe = qv * e_c
            ekb = kb * e_c

            dkdec = _mm_nt(v_new, ds_next)
            dv_new_s = _mm(kdec, ds_next)
            t1 = _mm_nt(dov, st)
            ds_o = _mm_tn(qe, dov)
            dattn_raw = _mm_nt(dov, v_new)
            dv_new_o = _mm_tn(attn, dov)
            yield
            dgl = egl * jnp.sum(jnp.sum(st * ds_next, axis=1, keepdims=True), axis=0, keepdims=True)
            dk = edec * dkdec
            r = jnp.sum(dkdec * kdec, axis=1, keepdims=True)
            dgc = -r
            dgl = dgl + jnp.sum(r, axis=0, keepdims=True)
            dq = e_c * t1
            dgc = dgc + jnp.sum(t1 * qe, axis=1, keepdims=True)
            dattn = jnp.where(incl, dattn_raw, 0.0)
            dv_new = dv_new_s + dv_new_o
            dqm = dattn * gam
            z = dattn * attn
            dvb = _mm3(a, dv_new, _TN)
            dq_a = _mm(dqm, kv)
            dk_a = _mm_tn(dqm, qv)
            yield
            dq_ref[:, sl] = dq + dq_a
            dv_ref[:, sl] = beta_c * dvb
            ds_kb = _mm_tn(ekb, dvb)
            dekb_neg = _mm_nt(dvb, st)
            dl_neg = _mm_nt(dvb, v_new)
            yield
            ds_ref[h] = egl * ds_next + ds_o - ds_kb
            dekb = -dekb_neg
            dl = jnp.where(strict, -dl_neg, 0.0)
            dp = dl * gam
            z = z + dl * lmat
            dkb_p = _mm(dp, kv)
            dk_p = _mm_tn(dp, kb)
            dgc = dgc + jnp.sum(dekb * ekb, axis=1, keepdims=True)
            dgc = dgc + jnp.sum(z, axis=1, keepdims=True) - jnp.sum(z.T, axis=1, keepdims=True)
            dgc = dgc + jnp.where(rowi == C - 1, dgl, 0.0)
            upper = (t["ri"] <= t["ci"]).astype(F32)
            dg_b = _mm3(upper, jnp.broadcast_to(dgc, (C, LANES)))
            yield
            dkb = dkb_p + e_c * dekb
            dk_ref[:, sl] = dk + dk_a + dk_p + beta_c * dkb
            dbeta = jnp.sum(dkb * kv, axis=1, keepdims=True) + jnp.sum(dvb * vv, axis=1, keepdims=True)
            dg = dg_b[:, 0:1]
            a_raw = bav[:, H + h:H + h + 1]
            d_braw = dbeta * beta_c * (1.0 - beta_c)
            d_araw = dg * (-jnp.exp(al_ref[0, h])) * _sigmoid(a_raw + dt_ref[0, h])
            acc["dgate"] = acc["dgate"] + jnp.where(lane == h, d_braw, 0.0) + jnp.where(lane == H + h, d_araw, 0.0)
            dal = jnp.sum(dg * t["g_c"], axis=0, keepdims=True)
            ddt = jnp.sum(d_araw, axis=0, keepdims=True)
            acc["dpar"] = acc["dpar"] + jnp.where(lane1 == h, dal, 0.0) + jnp.where(lane1 == H + h, ddt, 0.0)

        _interleave(head(h) for h in range(H))
        dgate_ref[...] = acc["dgate"]
        dpar_ref[0:1, :] += acc["dpar"]

        @pl.when(pl.program_id(0) == N - 1)
        def _():
            for cp in ccps:
                cp.wait()

    rev = lambda s: N - 1 - s
    blk = lambda: pl.BlockSpec((C, AW), lambda s: (rev(s), 0))
    smem = pl.BlockSpec(memory_space=pltpu.SMEM)
    any_spec = pl.BlockSpec(memory_space=pl.ANY)
    res = pl.pallas_call(
        body, name="delta_bwd", grid=(N,),
        in_specs=[smem, smem, blk(), blk(), blk(),
                  pl.BlockSpec((C, LANES), lambda s: (rev(s), 0)),
                  pl.BlockSpec((1, 32, C), lambda s: (rev(s), 0, 0)),
                  pl.BlockSpec((C, LANES), lambda s: (rev(s), 0)),
                  blk(),
                  pl.BlockSpec((1, H, D, D), lambda s: (rev(s), 0, 0, 0)),
                  pl.BlockSpec((1, H, C, C), lambda s: (rev(s), 0, 0, 0)),
                  blk()] + [any_spec] * nc,
        out_specs=[blk(), blk(), blk(),
                   pl.BlockSpec((C, LANES), lambda s: (rev(s), 0)),
                   pl.BlockSpec((8, LANES), lambda s: (0, 0))] + [any_spec] * nc,
        out_shape=[S((T, AW), F32), S((T, AW), F32), S((T, AW), F32),
                   S((T, LANES), F32), S((8, LANES), F32)] + [S((3,) + a.shape[1:], a.dtype) for a in carry],
        scratch_shapes=[pltpu.VMEM((H, D, D), F32),
                        pltpu.SemaphoreType.DMA((max(nc, 1), 3)), pltpu.SemaphoreType.DMA((max(nc, 1), 3))],
        compiler_params=_cp(ARB),
    )(a_log, dt_bias, q, k, v, gcol, grow, ba, vnew, ssave, asave, d_o, *carry)
    return res[:5], res[5:]


def _ln_stats(xv):
    mu = jnp.mean(xv, axis=-1, keepdims=True)
    xc = xv - mu
    var = jnp.mean(xc * xc, axis=-1, keepdims=True)
    rstd = lax.rsqrt(var + EPS)
    return xc * rstd, rstd


def _mix_fwd(o, proj, head_norm_w, ln_w, ln_b, w_sp, bs_t, H, D, G, P):
    T = o.shape[0]
    AW, BW = H * D, G * P
    MIX = AW + BW
    nb = AW // BW if AW % BW == 0 else None
    assert nb == 1, "group widths must match the projection column blocks"
    cb = 3

    def body(o_ref, za_ref, ub_ref, vb_ref, zb_ref, hw_ref, lw_ref, lb_ref, w_ref, bs_ref, out_ref):
        hw = hw_ref[...]
        for h in range(H):
            sl = slice(h * D, (h + 1) * D)
            oh = o_ref[:, sl]
            rs = lax.rsqrt(jnp.mean(oh * oh, axis=-1, keepdims=True) + EPS)
            out_ref[:, sl] = (oh * rs * hw * _silu(za_ref[:, sl])).astype(BF16)
        xhat, _ = _ln_stats(vb_ref[...])
        vn = xhat * lw_ref[...] + lb_ref[...]
        ri = lax.broadcasted_iota(jnp.int32, (P, P), 0)
        ci = lax.broadcasted_iota(jnp.int32, (P, P), 1)
        bsv = bs_ref[...]
        for g in range(G):
            sl = slice(g * P, (g + 1) * P)
            wm = jnp.where(ri >= ci, w_ref[g], 0.0)
            s = _mm(wm, vn[:, sl]) + bsv[:, g:g + 1]
            out_ref[:, AW + g * P:AW + (g + 1) * P] = (ub_ref[:, sl] * s * _silu(zb_ref[:, sl])).astype(BF16)

    row = lambda w: pl.BlockSpec((1, w), lambda i: (0, 0))
    return pl.pallas_call(
        body, name="mix_fwd", grid=(T // P,),
        in_specs=[pl.BlockSpec((P, AW), lambda i: (i, 0)),
                  pl.BlockSpec((P, AW), lambda i: (i, cb)),
                  pl.BlockSpec((P, BW), lambda i: (i, cb + 1)),
                  pl.BlockSpec((P, BW), lambda i: (i, cb + 2)),
                  pl.BlockSpec((P, BW), lambda i: (i, cb + 3)),
                  row(D), row(BW), row(BW),
                  pl.BlockSpec((G, P, P), lambda i: (0, 0, 0)),
                  pl.BlockSpec((P, G), lambda i: (0, 0))],
        out_specs=pl.BlockSpec((P, MIX), lambda i: (i, 0)),
        out_shape=S((T, MIX), BF16),
        compiler_params=_cp(ARB),
    )(o, proj, proj, proj, proj, head_norm_w, ln_w, ln_b, w_sp, bs_t)


def _mix_bwd(d_ocat, o, proj, head_norm_w, ln_w, ln_b, w_sp, bs_t, H, D, G, P, carry):
    T = o.shape[0]
    AW, BW = H * D, G * P
    MIX = AW + BW
    cb = 3
    nc = len(carry)

    def body(dc_ref, o_ref, za_ref, ub_ref, vb_ref, zb_ref, hw_ref, lw_ref, lb_ref, w_ref, bs_ref, *rest):
        cins = rest[:nc]
        do_ref, dmain_ref, dhw_ref, dln_ref, dw_ref, dbs_ref = rest[nc:nc + 6]
        couts = rest[nc + 6:2 * nc + 6]
        dvn_ref, drest_ref, out_sems, csend, crecv = rest[2 * nc + 6:]
        i = pl.program_id(0)
        slot = lax.rem(i, 2)
        ccps = _sibling_copies(cins, couts, csend, crecv)

        def out_copy(step, s):
            return pltpu.make_async_copy(
                drest_ref.at[s], dmain_ref.at[pl.ds(step * P, P), pl.ds(cb * AW, AW + 3 * BW)], out_sems.at[s])

        @pl.when(i == 0)
        def _():
            dhw_ref[...] = jnp.zeros_like(dhw_ref)
            dln_ref[...] = jnp.zeros_like(dln_ref)
            dw_ref[...] = jnp.zeros_like(dw_ref)
            dbs_ref[...] = jnp.zeros_like(dbs_ref)
            for cp in ccps:
                cp.start()

        @pl.when(i >= 2)
        def _():
            out_copy(i - 2, slot).wait()

        hw = hw_ref[...]
        dhw = jnp.zeros((1, D), F32)
        for h in range(H):
            sl = slice(h * D, (h + 1) * D)
            oh = o_ref[:, sl]
            za = za_ref[:, sl]
            doa = dc_ref[:, sl]
            rs = lax.rsqrt(jnp.mean(oh * oh, axis=-1, keepdims=True) + EPS)
            xh = oh * rs
            d_on = doa * _silu(za)
            drest_ref[slot, :, sl] = (doa * (xh * hw) * _dsilu(za)).astype(BF16)
            dhw = dhw + jnp.sum(d_on * xh, axis=0, keepdims=True)
            dxh = d_on * hw
            do_ref[:, sl] = rs * (dxh - xh * jnp.mean(dxh * xh, axis=-1, keepdims=True))
        dhw_ref[0:1, :] += dhw

        xhat, rstd = _ln_stats(vb_ref[...])
        lw = lw_ref[...]
        vn = xhat * lw + lb_ref[...]
        ri = lax.broadcasted_iota(jnp.int32, (P, P), 0)
        ci = lax.broadcasted_iota(jnp.int32, (P, P), 1)
        lane = lax.broadcasted_iota(jnp.int32, (P, LANES), 1)
        bsv = bs_ref[...]
        dbs = jnp.zeros((P, LANES), F32)
        for g in range(G):
            sl = slice(g * P, (g + 1) * P)
            wm = jnp.where(ri >= ci, w_ref[g], 0.0)
            vng = vn[:, sl]
            s = _mm(wm, vng) + bsv[:, g:g + 1]
            dob = dc_ref[:, AW + g * P:AW + (g + 1) * P]
            ub = ub_ref[:, sl]
            zb = zb_ref[:, sl]
            szb = _silu(zb)
            drest_ref[slot, :, AW + g * P:AW + (g + 1) * P] = (dob * s * szb).astype(BF16)
            drest_ref[slot, :, AW + 2 * BW + g * P:AW + 2 * BW + (g + 1) * P] = (
                dob * ub * s * _dsilu(zb)).astype(BF16)
            ds = dob * ub * szb
            dvn_ref[:, sl] = _mm_tn(wm, ds)
            dw_ref[g] += jnp.where(ri >= ci, _mm_nt(ds, vng), 0.0)
            dbs = dbs + jnp.where(lane == g, jnp.sum(ds, axis=1, keepdims=True), 0.0)
        dbs_ref[...] += dbs
        dvn = dvn_ref[...]
        dln_ref[0:1, :] += jnp.sum(dvn * xhat, axis=0, keepdims=True)
        dln_ref[1:2, :] += jnp.sum(dvn, axis=0, keepdims=True)
        dxh = dvn * lw
        dvb = rstd * (dxh - jnp.mean(dxh, axis=-1, keepdims=True) - xhat * jnp.mean(dxh * xhat, axis=-1, keepdims=True))
        drest_ref[slot, :, AW + BW:AW + 2 * BW] = dvb.astype(BF16)

        out_copy(i, slot).start()

        @pl.when(i == nstep - 1)
        def _():
            out_copy(i, slot).wait()
            if nstep > 1:
                out_copy(i - 1, 1 - slot).wait()
            for cp in ccps:
                cp.wait()

    nstep = T // P
    row = lambda w: pl.BlockSpec((1, w), lambda i: (0, 0))
    any_spec = pl.BlockSpec(memory_space=pl.ANY)
    res = pl.pallas_call(
        body, name="mix_bwd", grid=(nstep,),
        in_specs=[pl.BlockSpec((P, MIX), lambda i: (i, 0)),
                  pl.BlockSpec((P, AW), lambda i: (i, 0)),
                  pl.BlockSpec((P, AW), lambda i: (i, cb)),
                  pl.BlockSpec((P, BW), lambda i: (i, cb + 1)),
                  pl.BlockSpec((P, BW), lambda i: (i, cb + 2)),
                  pl.BlockSpec((P, BW), lambda i: (i, cb + 3)),
                  row(D), row(BW), row(BW),
                  pl.BlockSpec((G, P, P), lambda i: (0, 0, 0)),
                  pl.BlockSpec((P, G), lambda i: (0, 0))] + [any_spec] * nc,
        out_specs=[pl.BlockSpec((P, AW), lambda i: (i, 0)),
                   any_spec,
                   pl.BlockSpec((8, D), lambda i: (0, 0)),
                   pl.BlockSpec((8, BW), lambda i: (0, 0)),
                   pl.BlockSpec((G, P, P), lambda i: (0, 0, 0)),
                   pl.BlockSpec((P, LANES), lambda i: (0, 0))] + [any_spec] * nc,
        out_shape=[S((T, AW), F32), S((T, cb * AW + AW + 3 * BW), BF16), S((8, D), F32), S((8, BW), F32),
                   S((G, P, P), F32), S((P, LANES), F32)] + [S(a.shape[:1] + a.shape[2:], a.dtype) for a in carry],
        scratch_shapes=[pltpu.VMEM((P, BW), F32), pltpu.VMEM((2, P, AW + 3 * BW), BF16),
                        pltpu.SemaphoreType.DMA((2,))] + _sibling_sems(carry),
        compiler_params=_cp(ARB),
    )(d_ocat, o, proj, proj, proj, proj, head_norm_w, ln_w, ln_b, w_sp, bs_t, *carry)
    return res[:6], res[6:]


def _out_proj_loss(ocat, w_out, x, target, fnw):
    T, MIX = ocat.shape
    DM = x.shape[1]
    tm = _tile(T, 256, 8)

    def body(oc_ref, w_ref, x_ref, t_ref, fw_ref, dh_ref, dhb_ref, doc_ref, loss_ref, gfw_ref):
        @pl.when(pl.program_id(0) == 0)
        def _():
            loss_ref[...] = jnp.zeros_like(loss_ref)
            gfw_ref[...] = jnp.zeros_like(gfw_ref)

        wv = w_ref[...]
        hh = x_ref[...] + jnp.dot(oc_ref[...].astype(MXU), wv.astype(MXU), preferred_element_type=F32)
        rs = lax.rsqrt(jnp.mean(hh * hh, axis=-1, keepdims=True) + EPS)
        hn = hh * rs
        fw = fw_ref[...]
        e = hn * fw - t_ref[...]
        row_loss = 0.5 * jnp.mean(e * e, axis=-1, keepdims=True)
        loss_ref[...] += jnp.sum(row_loss, axis=0, keepdims=True)
        dy = e * (1.0 / DM)
        gfw_ref[0:1, :] += jnp.sum(dy * hn, axis=0, keepdims=True)
        dhn = dy * fw
        dh = rs * (dhn - hn * jnp.mean(dhn * hn, axis=-1, keepdims=True))
        dh_ref[...] = dh
        dhb = dh.astype(BF16)
        dhb_ref[...] = dhb
        doc_ref[...] = _mm_nt(dhb, wv)

    return pl.pallas_call(
        body, name="out_proj_loss", grid=(T // tm,),
        in_specs=[pl.BlockSpec((tm, MIX), lambda i: (i, 0)),
                  pl.BlockSpec((MIX, DM), lambda i: (0, 0)),
                  pl.BlockSpec((tm, DM), lambda i: (i, 0)),
                  pl.BlockSpec((tm, DM), lambda i: (i, 0)),
                  pl.BlockSpec((1, DM), lambda i: (0, 0))],
        out_specs=[pl.BlockSpec((tm, DM), lambda i: (i, 0)),
                   pl.BlockSpec((tm, DM), lambda i: (i, 0)),
                   pl.BlockSpec((tm, MIX), lambda i: (i, 0)),
                   pl.BlockSpec((8, LANES), lambda i: (0, 0)),
                   pl.BlockSpec((8, DM), lambda i: (0, 0))],
        out_shape=[S((T, DM), F32), S((T, DM), BF16), S((T, MIX), F32), S((8, LANES), F32), S((8, DM), F32)],
        compiler_params=_cp(ARB),
    )(ocat, w_out, x, target, fnw)


def _grad_w(lhs, rhs, name):
    T, A = lhs.shape
    B = rhs.shape[1]
    ta = _tile(A, 512, LANES)
    tk = _tile(T, 1024, 16)
    nk = T // tk

    def body(l_ref, r_ref, out_ref, acc_ref):
        k = pl.program_id(1)
        part = _mm_tn(l_ref[...], r_ref[...])

        @pl.when(k == 0)
        def _():
            acc_ref[...] = part

        @pl.when(k > 0)
        def _():
            acc_ref[...] += part

        @pl.when(k == nk - 1)
        def _():
            out_ref[...] = acc_ref[...].astype(BF16)

    return pl.pallas_call(
        body, name=name, grid=(A // ta, nk),
        in_specs=[pl.BlockSpec((tk, ta), lambda i, k: (k, i)),
                  pl.BlockSpec((tk, B), lambda i, k: (k, 0))],
        out_specs=pl.BlockSpec((ta, B), lambda i, k: (i, 0)),
        out_shape=S((A, B), BF16),
        scratch_shapes=[pltpu.VMEM((ta, B), F32)],
        compiler_params=_cp(ARB, ARB),
    )(lhs, rhs)


def _grad_w_in(xn, dmain, dba):
    T, DM = xn.shape
    NM = dmain.shape[1]
    tn = _tile(NM, 1024, LANES)
    tk = _tile(T, 2048, 16)

    def body(xn_ref, dm_ref, dba_ref, gm_ref, gba_ref):
        j = pl.program_id(0)
        k = pl.program_id(1)

        @pl.when(k == 0)
        def _():
            gm_ref[...] = jnp.zeros_like(gm_ref)

        @pl.when((k == 0) & (j == 0))
        def _():
            gba_ref[...] = jnp.zeros_like(gba_ref)

        xv = xn_ref[...]
        gm_ref[...] += _mm_tn(xv, dm_ref[...])

        @pl.when(j == 0)
        def _():
            gba_ref[...] += _mm_tn(xv, dba_ref[...])

    return pl.pallas_call(
        body, name="grad_w_in", grid=(NM // tn, T // tk),
        in_specs=[pl.BlockSpec((tk, DM), lambda j, k: (k, 0)),
                  pl.BlockSpec((tk, tn), lambda j, k: (k, j)),
                  pl.BlockSpec((tk, LANES), lambda j, k: (k, 0))],
        out_specs=[pl.BlockSpec((DM, tn), lambda j, k: (0, j)),
                   pl.BlockSpec((DM, LANES), lambda j, k: (0, 0))],
        out_shape=[S((DM, NM), F32), S((DM, LANES), F32)],
        compiler_params=_cp(ARB, ARB),
    )(xn, dmain, dba)


def _dx(dmain, dba, w_main, w_ba, x, dh, norm_w, chip_sums, small):
    T, NM = dmain.shape
    DM = x.shape[1]
    tm = _tile(T, 512, 8)
    tk = _tile(NM, 1024, LANES)
    nk = NM // tk
    ni = T // tm
    na = len(chip_sums)

    def body(dm_ref, dba_ref, w_ref, wba_ref, x_ref, dh_ref, nw_ref, small_ref, *rest):
        ins = rest[:na]
        gx_ref, gnw_ref, gath_ref = rest[na:na + 3]
        outs = rest[na + 3:2 * na + 3]
        acc_ref, csend, crecv, ssend, srecv, lsem = rest[2 * na + 3:]
        i = pl.program_id(0)
        k = pl.program_id(1)
        me, small_cps = _broadcast_copies([small_ref], [gath_ref], ssend, srecv)
        cps = (_chip_exchange_copies(ins, outs, csend, crecv) + small_cps
               + [pltpu.make_async_copy(small_ref, gath_ref.at[me], lsem.at[0])])

        @pl.when((i == 0) & (k == 0))
        def _():
            gnw_ref[...] = jnp.zeros_like(gnw_ref)
            for cp in cps:
                cp.start()

        @pl.when(k == 0)
        def _():
            acc_ref[...] = _mm_nt(dba_ref[...], wba_ref[...])

        acc_ref[...] += _mm_nt(dm_ref[...], w_ref[...])

        @pl.when(k == nk - 1)
        def _():
            xv = x_ref[...]
            rs = lax.rsqrt(jnp.mean(xv * xv, axis=-1, keepdims=True) + EPS)
            xh = xv * rs
            dxn = acc_ref[...]
            gnw_ref[0:1, :] += jnp.sum(dxn * xh, axis=0, keepdims=True)
            dxh = dxn * nw_ref[...]
            gx_ref[...] = dh_ref[...] + rs * (dxh - xh * jnp.mean(dxh * xh, axis=-1, keepdims=True))

        @pl.when((i == ni - 1) & (k == nk - 1))
        def _():
            for cp in cps:
                cp.wait()

    any_spec = pl.BlockSpec(memory_space=pl.ANY)
    res = pl.pallas_call(
        body, name="dx", grid=(ni, nk),
        in_specs=[pl.BlockSpec((tm, tk), lambda i, k: (i, k)),
                  pl.BlockSpec((tm, LANES), lambda i, k: (i, 0)),
                  pl.BlockSpec((DM, tk), lambda i, k: (0, k)),
                  pl.BlockSpec((DM, LANES), lambda i, k: (0, 0)),
                  pl.BlockSpec((tm, DM), lambda i, k: (i, 0)),
                  pl.BlockSpec((tm, DM), lambda i, k: (i, 0)),
                  pl.BlockSpec((1, DM), lambda i, k: (0, 0)),
                  any_spec] + [any_spec] * na,
        out_specs=[pl.BlockSpec((tm, DM), lambda i, k: (i, 0)),
                   pl.BlockSpec((8, DM), lambda i, k: (0, 0)),
                   any_spec] + [any_spec] * na,
        out_shape=[S((T, DM), F32), S((8, DM), F32), S((N_DEV,) + small.shape, F32)]
        + [S((3,) + a.shape[1:], a.dtype) for a in chip_sums],
        scratch_shapes=[pltpu.VMEM((tm, DM), F32),
                        pltpu.SemaphoreType.DMA((na, 3)), pltpu.SemaphoreType.DMA((na, 3)),
                        pltpu.SemaphoreType.DMA((1, N_DEV - 1)), pltpu.SemaphoreType.DMA((1, N_DEV - 1)),
                        pltpu.SemaphoreType.DMA((1,))],
        compiler_params=_cp(ARB, ARB),
    )(dmain, dba, w_main, w_ba, x, dh, norm_w, small, *chip_sums)
    return res[0], res[1], res[2], res[3:]


def _sum_slots(gath):
    _, R, C = gath.shape
    tr = _tile(R, 512, 8)

    def body(g_ref, o_ref):
        tot = g_ref[0]
        for d in range(1, N_DEV):
            tot = tot + g_ref[d]
        o_ref[...] = tot

    return pl.pallas_call(
        body, name="sum_slots", grid=(R // tr,),
        in_specs=[pl.BlockSpec((N_DEV, tr, C), lambda i: (0, i, 0))],
        out_specs=pl.BlockSpec((tr, C), lambda i: (i, 0)),
        out_shape=S((R, C), F32), compiler_params=_cp(ARB),
    )(gath)


def _prep_a_bwd_pointwise(dq, dk, dv, c, H, D):
    T = c.shape[0]
    AW = H * D
    C3 = 3 * AW
    tb = _tile(T, 256, 8)
    scale = float(D) ** -0.5

    def body(dq_ref, dk_ref, dv_ref, c_ref, dc_ref):
        for h in range(H):
            for part, d_ref, sc in ((0, dq_ref, scale), (1, dk_ref, 1.0)):
                sl = slice(part * AW + h * D, part * AW + (h + 1) * D)
                cv = c_ref[:, sl]
                raw = _silu(cv)
                rs = lax.rsqrt(jnp.sum(raw * raw, axis=-1, keepdims=True) + EPS)
                nrm = raw * rs
                dn = d_ref[:, h * D:(h + 1) * D] * sc
                draw = rs * (dn - nrm * jnp.sum(dn * nrm, axis=-1, keepdims=True))
                dc_ref[:, sl] = draw * _dsilu(cv)
        dc_ref[:, 2 * AW:] = dv_ref[...] * _dsilu(c_ref[:, 2 * AW:])

    return pl.pallas_call(
        body, name="prep_a_bwd_pointwise", grid=(T // tb,),
        in_specs=[pl.BlockSpec((tb, AW), lambda i: (i, 0)),
                  pl.BlockSpec((tb, AW), lambda i: (i, 0)),
                  pl.BlockSpec((tb, AW), lambda i: (i, 0)),
                  pl.BlockSpec((tb, C3), lambda i: (i, 0))],
        out_specs=pl.BlockSpec((tb, C3), lambda i: (i, 0)),
        out_shape=S((T, C3), F32),
        compiler_params=_cp(ARB),
    )(dq, dk, dv, c)


def _conv_bwd(dc, proj, conv_w, dmain, C3):
    T = dc.shape[0]
    tb = _tile(T, 256, 8)
    nblk = T // tb
    r8 = tb // 8

    def body(dc_ref, dnext_ref, x_ref, halo_ref, cw_ref, dmain_in_ref, dx_ref, gcw_ref):
        del dmain_in_ref
        i = pl.program_id(0)

        @pl.when(i == 0)
        def _():
            gcw_ref[...] = jnp.zeros_like(gcw_ref)

        dcv = dc_ref[...]
        dnext = dnext_ref[...] * (i < nblk - 1).astype(F32)
        dcp = jnp.concatenate([dcv, dnext], axis=0)
        cw = cw_ref[...]
        dx = cw[3:4, :] * dcv
        for j in range(3):
            dx = dx + cw[j:j + 1, :] * dcp[3 - j:3 - j + tb]
        dx_ref[...] = dx.astype(BF16)
        halo = halo_ref[...] * (i > 0).astype(F32)
        xp = jnp.concatenate([halo, x_ref[...]], axis=0)
        for j in range(4):
            gcw_ref[j:j + 1, :] += jnp.sum(dcv * xp[5 + j:5 + j + tb], axis=0, keepdims=True)

    return pl.pallas_call(
        body, name="conv_bwd", grid=(nblk,),
        in_specs=[pl.BlockSpec((tb, C3), lambda i: (i, 0)),
                  pl.BlockSpec((8, C3), lambda i: (jnp.minimum((i + 1) * r8, T // 8 - 1), 0)),
                  pl.BlockSpec((tb, C3), lambda i: (i, 0)),
                  pl.BlockSpec((8, C3), lambda i: (jnp.maximum(i * r8 - 1, 0), 0)),
                  pl.BlockSpec((4, C3), lambda i: (0, 0)),
                  pl.BlockSpec(memory_space=pl.ANY)],
        out_specs=[pl.BlockSpec((tb, C3), lambda i: (i, 0)),
                   pl.BlockSpec((8, C3), lambda i: (0, 0))],
        out_shape=[S(dmain.shape, dmain.dtype), S((8, C3), F32)],
        input_output_aliases={5: 0},
        compiler_params=_cp(ARB),
    )(dc, dc, proj, proj, conv_w, dmain)


def _adam_math(w, g, m, v):
    m2 = ADAM_B1 * m + (1.0 - ADAM_B1) * g
    v2 = ADAM_B2 * v + (1.0 - ADAM_B2) * (g * g)
    m_hat = m2 / (1.0 - ADAM_B1 ** ADAM_STEP)
    v_hat = v2 / (1.0 - ADAM_B2 ** ADAM_STEP)
    delta = -ADAM_LR * (m_hat / (jnp.sqrt(v_hat) + ADAM_EPS) + ADAM_WD * w)
    return delta, m2, v2


def _pair_sum(blocks, recv, core, name):
    K, _, R, C = blocks.shape
    tr = _tile(R, 256, 16)

    def body(core_ref, a_ref, b_ref, o_ref):
        del core_ref
        o_ref[0] = (a_ref[0, 0].astype(F32) + b_ref[0].astype(F32)).astype(BF16)

    spec = lambda: pl.BlockSpec((1, tr, C), lambda k, i, core_ref: (k, i, 0))
    return pl.pallas_call(
        body, name=name,
        grid_spec=pltpu.PrefetchScalarGridSpec(
            num_scalar_prefetch=1, grid=(K, R // tr),
            in_specs=[pl.BlockSpec((1, 1, tr, C), lambda k, i, core_ref: (k, core_ref[0], i, 0)), spec()],
            out_specs=spec()),
        out_shape=S((K, R, C), BF16), compiler_params=_cp(ARB, ARB),
    )(core, blocks, recv)


def _sum_adam(chip_sums, recv, w, m, v, chip, name):
    R, C = w.shape
    tr = _tile(R, 256, 16)

    def body(chip_ref, own_ref, r_ref, w_ref, m_ref, v_ref, g_ref, d_ref, m2_ref, v2_ref):
        del chip_ref
        g = own_ref[0].astype(F32)
        for j in range(3):
            g = g + r_ref[j].astype(F32)
        g_ref[...] = g
        d_ref[...], m2_ref[...], v2_ref[...] = _adam_math(w_ref[...], g, m_ref[...], v_ref[...])

    spec = lambda: pl.BlockSpec((tr, C), lambda i, chip_ref: (i, 0))
    return pl.pallas_call(
        body, name=name,
        grid_spec=pltpu.PrefetchScalarGridSpec(
            num_scalar_prefetch=1, grid=(R // tr,),
            in_specs=[pl.BlockSpec((1, tr, C), lambda i, chip_ref: (chip_ref[0], i, 0)),
                      pl.BlockSpec((3, tr, C), lambda i, chip_ref: (0, i, 0)), spec(), spec(), spec()],
            out_specs=[spec(), spec(), spec(), spec()]),
        out_shape=[S((R, C), F32)] * 4, compiler_params=_cp(ARB),
    )(chip, chip_sums, recv, w, m, v)


def _adam_small(w, g, m, v):
    R, C = w.shape
    tr = _tile(R, 512, 8)

    def body(w_ref, g_ref, m_ref, v_ref, d_ref, m2_ref, v2_ref):
        d_ref[...], m2_ref[...], v2_ref[...] = _adam_math(w_ref[...], g_ref[...], m_ref[...], v_ref[...])

    spec = lambda: pl.BlockSpec((tr, C), lambda i: (i, 0))
    return pl.pallas_call(
        body, name="adam_small", grid=(R // tr,), in_specs=[spec()] * 4, out_specs=[spec()] * 3,
        out_shape=[S((R, C), F32)] * 3, compiler_params=_cp(ARB),
    )(w, g, m, v)


def _position():
    return lax.axis_index("x"), lax.axis_index("y"), lax.axis_index("c")


def _all_gather_weights(arr):
    R = arr.shape[0]
    half = R // 2
    assert half % 16 == 0

    def body(in_ref, out_ref, send_sems, recv_sems, local_sem):
        x, y, c = _position()
        me, sibling = (x, y, c), (x, y, 1 - c)
        xn, yn, diag = (1 - x, y), (x, 1 - y), (1 - x, 1 - y)
        upper, lower = pl.ds(0, half), pl.ds(half, half)

        def slot(p, rows=None):
            ref = out_ref.at[4 * p[0] + 2 * p[1] + p[2]]
            return ref if rows is None else ref.at[rows]

        def copy(kk, block, to, rows=None, src=None):
            return pltpu.make_async_remote_copy(
                src_ref=slot(block, rows) if src is None else src, dst_ref=slot(block, rows),
                send_sem=send_sems.at[kk], recv_sem=recv_sems.at[kk], device_id=to, device_id_type=MESH)

        mine = pltpu.make_async_copy(in_ref, slot(me), local_sem)
        mine.start()
        sent = [copy(0, me, sibling, src=in_ref), copy(1, me, (*xn, c), src=in_ref), copy(2, me, (*yn, c), src=in_ref)]
        for cp in sent:
            cp.start()

        def then(cps):
            for cp in cps:
                cp.start()
            sent.extend(cps)

        copy(1, (*xn, c), me).wait_recv()
        then([copy(5, (*xn, c), (*yn, c), rows=upper), copy(3, (*xn, c), sibling)])
        copy(2, (*yn, c), me).wait_recv()
        then([copy(6, (*yn, c), (*xn, c), rows=lower), copy(4, (*yn, c), sibling)])
        copy(5, (*diag, c), me, rows=upper).wait_recv()
        then([copy(7, (*diag, c), sibling, rows=upper)])
        copy(6, (*diag, c), me, rows=lower).wait_recv()
        then([copy(8, (*diag, c), sibling, rows=lower)])
        copy(0, sibling, me).wait_recv()
        copy(3, (*xn, 1 - c), me).wait_recv()
        copy(4, (*yn, 1 - c), me).wait_recv()
        copy(7, (*diag, 1 - c), me, rows=upper).wait_recv()
        copy(8, (*diag, 1 - c), me, rows=lower).wait_recv()
        for cp in sent:
            cp.wait_send()
        mine.wait()

    any_spec = pl.BlockSpec(memory_space=pl.ANY)
    return pl.pallas_call(
        body, name="all_gather_weights", in_specs=[any_spec], out_specs=any_spec,
        out_shape=S((N_DEV,) + arr.shape, arr.dtype),
        scratch_shapes=[pltpu.SemaphoreType.DMA((9,)), pltpu.SemaphoreType.DMA((9,)), pltpu.SemaphoreType.DMA],
    )(arr)


def _sibling_copies(ins, outs, send_sems, recv_sems):
    x, y, c = _position()
    return [pltpu.make_async_remote_copy(src_ref=ins[a].at[k, 1 - c], dst_ref=outs[a].at[k],
                                         send_sem=send_sems.at[a, k], recv_sem=recv_sems.at[a, k],
                                         device_id=(x, y, 1 - c), device_id_type=MESH)
            for a in range(len(ins)) for k in range(ins[a].shape[0])]


def _sibling_sems(arrs):
    shape = (max(len(arrs), 1), arrs[0].shape[0] if arrs else 1)
    return [pltpu.SemaphoreType.DMA(shape), pltpu.SemaphoreType.DMA(shape)]


def _exchange_sibling(arrs):
    na = len(arrs)

    def body(*refs):
        ins, outs = refs[:na], refs[na:2 * na]
        send_sems, recv_sems = refs[2 * na:]
        cps = _sibling_copies(ins, outs, send_sems, recv_sems)
        for cp in cps:
            cp.start()
        for cp in cps:
            cp.wait()

    any_spec = pl.BlockSpec(memory_space=pl.ANY)
    return pl.pallas_call(
        body, name="exchange_sibling", in_specs=[any_spec] * na, out_specs=[any_spec] * na,
        out_shape=[S(a.shape[:1] + a.shape[2:], a.dtype) for a in arrs],
        scratch_shapes=_sibling_sems(arrs),
    )(*arrs)


def _chip_exchange_copies(ins, outs, send_sems, recv_sems):
    x, y, c = _position()
    chips = [(1 - x, y), (x, 1 - y), (1 - x, 1 - y)]
    return [pltpu.make_async_remote_copy(
        src_ref=ins[a].at[2 * qx + qy], dst_ref=outs[a].at[j], send_sem=send_sems.at[a, j],
        recv_sem=recv_sems.at[a, j], device_id=(qx, qy, c), device_id_type=MESH)
        for a in range(len(ins)) for j, (qx, qy) in enumerate(chips)]


def _broadcast_copies(srcs, dsts, send_sems, recv_sems):
    x, y, c = _position()
    me = 4 * x + 2 * y + c
    cps = []
    for a in range(len(srcs)):
        for k in range(1, N_DEV):
            peer = (1 - x if k & 4 else x, 1 - y if k & 2 else y, 1 - c if k & 1 else c)
            cps.append(pltpu.make_async_remote_copy(
                src_ref=srcs[a], dst_ref=dsts[a].at[me], send_sem=send_sems.at[a, k - 1],
                recv_sem=recv_sems.at[a, k - 1], device_id=peer, device_id_type=MESH))
    return me, cps


def _all_reduce_small(part):
    R, C = part.shape

    def body(p_ref, out_ref, gath_ref, send_sems, recv_sems):
        me, cps = _broadcast_copies([p_ref], [gath_ref], send_sems, recv_sems)
        gath_ref[me] = p_ref[...]
        for cp in cps:
            cp.start()
        for cp in cps:
            cp.wait()
        acc = gath_ref[0]
        for d in range(1, N_DEV):
            acc = acc + gath_ref[d]
        out_ref[...] = acc

    vm = pl.BlockSpec(memory_space=pltpu.VMEM)
    return pl.pallas_call(
        body, name="all_reduce_small", in_specs=[vm], out_specs=vm, out_shape=S((R, C), F32),
        scratch_shapes=[pltpu.VMEM((N_DEV, R, C), F32), pltpu.SemaphoreType.DMA((1, N_DEV - 1)),
                        pltpu.SemaphoreType.DMA((1, N_DEV - 1))],
    )(part)


def _pack(parts):
    rows = []
    for p in parts:
        f = p.reshape(-1).astype(F32)
        pad = (-f.shape[0]) % (8 * LANES)
        rows.append(jnp.pad(f, (0, pad)).reshape(-1, LANES))
    return jnp.concatenate(rows, axis=0)


def _unpack(buf, shapes):
    out, r = [], 0
    for shp in shapes:
        n = 1
        for s in shp:
            n *= s
        nr = -(-n // (8 * LANES)) * 8
        out.append(buf[r:r + nr].reshape(-1)[:n].reshape(shp))
        r += nr
    return out


def kernel(x, norm_w, w_in, conv_w, a_log, dt_bias, head_norm_w, sgu_ln_w, sgu_ln_b, w_spatial, b_spatial, w_out, final_norm_w, loss_target, m_norm_w, m_w_in, m_conv_w, m_a_log, m_dt_bias, m_head_norm_w, m_sgu_ln_w, m_sgu_ln_b, m_w_spatial, m_b_spatial, m_w_out, m_final_norm_w, v_norm_w, v_w_in, v_conv_w, v_a_log, v_dt_bias, v_head_norm_w, v_sgu_ln_w, v_sgu_ln_b, v_w_spatial, v_b_spatial, v_w_out, v_final_norm_w):
    T, DM = x.shape[1], x.shape[2]
    H, D = a_log.shape[1], head_norm_w.shape[1]
    G, P = w_spatial.shape[1], w_spatial.shape[2]
    AW, BW = H * D, G * P
    MIX = AW + BW
    WD = w_in.shape[2]
    IN = N_DEV * WD
    RO = w_out.shape[1]
    CW = conv_w.shape[2]
    sizes = (3 * AW, AW, H, H, BW, BW, BW)
    assert sum(sizes) == IN and 2 * H <= LANES and 3 * H <= 32 and N_DEV * RO == MIX and N_DEV * CW == 3 * AW
    offs = [0]
    for s in sizes:
        offs.append(offs[-1] + s)
    px, py, pc = _position()
    dev = 4 * px + 2 * py + pc
    chip = 2 * px + py

    x2, tgt = x[0], loss_target[0]

    g_win = _all_gather_weights(_cast_bf16(w_in[0], "cast_w_in"))
    w_main, w_ba = _relayout_w(g_win, offs[2], offs[4])
    alog_row = jnp.pad(a_log, ((0, 0), (H, LANES - 2 * H)))
    dtb_row = jnp.pad(dt_bias, ((0, 0), (H, LANES - 2 * H)))
    bs_t = b_spatial[0].T

    proj, ba, xn, (g_wout, g_conv) = _in_proj(x2, norm_w, w_main, w_ba, [_cast_bf16(w_out[0], "cast_w_out"), conv_w[0]])
    w_out_full = g_wout.reshape(MIX, DM)
    conv_full = g_conv.transpose(1, 0, 2).reshape(4, 3 * AW)
    q, k, v, c, gcol, grow = _prep_a_fwd(proj, ba, conv_full, alog_row, dtb_row, H, D)
    o, vnew, ssave, asave = _delta_fwd(q, k, v, gcol, grow, H, D)
    ocat = _mix_fwd(o, proj, head_norm_w, sgu_ln_w, sgu_ln_b, w_spatial[0], bs_t, H, D, G, P)
    dh, dh_bf, d_ocat, loss_acc, g_fnw = _out_proj_loss(ocat, w_out_full, x2, tgt, final_norm_w.reshape(1, DM))
    loss = lax.psum(loss_acc[0, 0], AXES)

    core_idx = jnp.reshape(pc, (1,)).astype(jnp.int32)
    chip_idx = jnp.reshape(chip, (1,)).astype(jnp.int32)
    g_wout_blocks = _grad_w(ocat, dh_bf, "grad_w_out").reshape(4, 2, RO, DM)
    (d_o, dmain, g_hnw, g_ln, g_wsp, g_bs_t), (sib_wout,) = _mix_bwd(
        d_ocat, o, proj, head_norm_w, sgu_ln_w, sgu_ln_b, w_spatial[0], bs_t, H, D, G, P, [g_wout_blocks])
    chip_wout = _pair_sum(g_wout_blocks, sib_wout, core_idx, "pair_sum_w_out")
    (dq, dk, dv, dgate, dpar), (recv_wout,) = _delta_bwd(
        q, k, v, gcol, grow, ba, vnew, ssave, asave, d_o, a_log, dt_bias, H, D, [chip_wout])
    dc = _prep_a_bwd_pointwise(dq, dk, dv, c, H, D)
    dmain, g_conv_part = _conv_bwd(dc, proj, conv_full, dmain, 3 * AW)
    dba = dgate.astype(BF16)
    g_main, g_ba = _grad_w_in(xn, dmain, dba)
    g_win_blocks = _relayout_g(g_main, g_ba, WD, offs[2], offs[4]).reshape(4, 2, DM, WD)
    (sib_win,) = _exchange_sibling([g_win_blocks])
    chip_win = _pair_sum(g_win_blocks, sib_win, core_idx, "pair_sum_w_in")
    small_shapes = [a_log.shape, dt_bias.shape, head_norm_w.shape, sgu_ln_w.shape, sgu_ln_b.shape,
                    w_spatial.shape, b_spatial.shape, final_norm_w.shape]
    parts = [dpar[0, :H], dpar[0, H:2 * H], g_hnw[0], g_ln[0], g_ln[1], g_wsp, g_bs_t[:, :G].T, g_fnw[0],
             g_conv_part[:4]]
    grad_x, g_nw, small_gath, (recv_win,) = _dx(
        dmain, dba, w_main, w_ba, x2, dh, norm_w, [chip_win], _pack(parts))
    red = _sum_slots(small_gath)
    grad_w_in, delta_w_in, new_m_w_in, new_v_w_in = _sum_adam(
        chip_win, recv_win, w_in[0], m_w_in[0], v_w_in[0], chip_idx, "sum_adam_w_in")
    grad_w_out, delta_w_out, new_m_w_out, new_v_w_out = _sum_adam(
        chip_wout, recv_wout, w_out[0], m_w_out[0], v_w_out[0], chip_idx, "sum_adam_w_out")
    red_nw = _all_reduce_small(_pack([g_nw[0]]))
    grads_small = _unpack(red_nw, [norm_w.shape]) + _unpack(red, small_shapes + [(4, 3 * AW)])
    g_conv_full = grads_small.pop()
    grad_conv = lax.dynamic_slice_in_dim(g_conv_full, dev * CW, CW, axis=1)[None]
    small_w = [norm_w, a_log, dt_bias, head_norm_w, sgu_ln_w, sgu_ln_b, w_spatial, b_spatial, final_norm_w, conv_w]
    small_m = [m_norm_w, m_a_log, m_dt_bias, m_head_norm_w, m_sgu_ln_w, m_sgu_ln_b, m_w_spatial, m_b_spatial,
               m_final_norm_w, m_conv_w]
    small_v = [v_norm_w, v_a_log, v_dt_bias, v_head_norm_w, v_sgu_ln_w, v_sgu_ln_b, v_w_spatial, v_b_spatial,
               v_final_norm_w, v_conv_w]
    small_g = grads_small + [grad_conv]
    shapes10 = [w.shape for w in small_w]
    d_p, m_p, v_p = _adam_small(_pack(small_w), _pack(small_g), _pack(small_m), _pack(small_v))
    d_s, m_s, v_s = _unpack(d_p, shapes10), _unpack(m_p, shapes10), _unpack(v_p, shapes10)

    def order(small, win, wout):
        return [small[0], win[None], small[9], small[1], small[2], small[3], small[4], small[5], small[6], small[7],
                wout[None], small[8]]

    grads = order(small_g, grad_w_in, grad_w_out)
    deltas = order(d_s, delta_w_in, delta_w_out)
    new_m = order(m_s, new_m_w_in, new_m_w_out)
    new_v = order(v_s, new_v_w_in, new_v_w_out)
    return (loss, grad_x[None], *grads, *deltas, *new_m, *new_v)
```

```python
import functools

import jax
import jax.numpy as jnp
from jax import lax
from jax.experimental import pallas as pl
from jax.experimental.pallas import tpu as pltpu

F32 = jnp.float32
BF16 = jnp.bfloat16
MXU = jnp.bfloat16
HI = lax.Precision.HIGHEST
EPS = 1e-6
CHUNK_A = 64
LANES = 128
MESH = pl.DeviceIdType.MESH
AXES = ("x", "y", "c")
N_DEV = 8

ADAM_LR = 0.001
ADAM_B1 = 0.9
ADAM_B2 = 0.999
ADAM_EPS = 1e-08
ADAM_WD = 0.01
ADAM_STEP = 10

S = jax.ShapeDtypeStruct
ARB = "arbitrary"


def _cp(*sem):
    return pltpu.CompilerParams(dimension_semantics=tuple(sem), vmem_limit_bytes=56 * 1024 * 1024)


def _tile(n, cap, mult):
    best = None
    t = mult
    while t <= min(n, cap):
        if n % t == 0:
            best = t
        t += mult
    return best if best is not None else n


def _mm(a, b):
    return jnp.dot(a.astype(MXU), b.astype(MXU), preferred_element_type=F32)


def _mm_nt(a, b):
    return lax.dot_general(a.astype(MXU), b.astype(MXU), (((1,), (1,)), ((), ())), preferred_element_type=F32)


def _mm_tn(a, b):
    return lax.dot_general(a.astype(MXU), b.astype(MXU), (((0,), (0,)), ((), ())), preferred_element_type=F32)


def _mmh(a, b):
    return jnp.dot(a, b, precision=HI, preferred_element_type=F32)


def _mmh_tn(a, b):
    return lax.dot_general(a, b, (((0,), (0,)), ((), ())), precision=HI, preferred_element_type=F32)


def _sigmoid(x):
    return 1.0 / (1.0 + jnp.exp(-x))


def _silu(x):
    return x * _sigmoid(x)


def _dsilu(x):
    s = _sigmoid(x)
    return s * (1.0 + x * (1.0 - s))


def _softplus(x):
    return jnp.maximum(x, 0.0) + jnp.log(1.0 + jnp.exp(-jnp.abs(x)))


def _pieces(wd, gate_lo, gate_hi, total):
    out = []
    for d in range(N_DEV):
        lo, hi = d * wd, (d + 1) * wd
        for dest, a, b, shift in (("main", 0, gate_lo, 0), ("gate", gate_lo, gate_hi, -gate_lo),
                                  ("main", gate_hi, total, gate_lo - gate_hi)):
            s0, s1 = max(lo, a), min(hi, b)
            if s0 < s1:
                out.append((d, s0 - lo, s1 - lo, dest, s0 + shift))
    return out


def _cast_bf16(a, name):
    R, C = a.shape
    tr = _tile(R, 256, 16)

    def body(a_ref, o_ref):
        o_ref[...] = a_ref[...].astype(BF16)

    spec = pl.BlockSpec((tr, C), lambda i: (i, 0))
    return pl.pallas_call(body, name=name, grid=(R // tr,), in_specs=[spec], out_specs=spec,
                          out_shape=S((R, C), BF16), compiler_params=_cp(ARB))(a)


def _cast_bf16_t(a_t, name):
    C, R = a_t.shape
    tr = _tile(R, 256, LANES)

    def body(a_ref, o_ref):
        o_ref[...] = a_ref[...].T.astype(BF16)

    return pl.pallas_call(body, name=name, grid=(R // tr,), in_specs=[pl.BlockSpec((C, tr), lambda i: (0, i))],
                          out_specs=pl.BlockSpec((tr, C), lambda i: (i, 0)),
                          out_shape=S((R, C), BF16), compiler_params=_cp(ARB))(a_t)


def _relayout_w(g_win, gate_lo, gate_hi):
    _, DM, WD = g_win.shape
    total = N_DEV * WD
    NM = total - (gate_hi - gate_lo)
    tr = _tile(DM, 256, 16)
    plan = _pieces(WD, gate_lo, gate_hi, total)

    def body(g_ref, main_ref, gate_ref):
        gate_ref[...] = jnp.zeros_like(gate_ref)
        for d, s0, s1, dest, c0 in plan:
            dst = main_ref if dest == "main" else gate_ref
            dst[:, c0:c0 + (s1 - s0)] = g_ref[d, :, s0:s1]

    return pl.pallas_call(
        body, name="relayout_w", grid=(DM // tr,),
        in_specs=[pl.BlockSpec((N_DEV, tr, WD), lambda i: (0, i, 0))],
        out_specs=[pl.BlockSpec((tr, NM), lambda i: (i, 0)), pl.BlockSpec((tr, LANES), lambda i: (i, 0))],
        out_shape=[S((DM, NM), g_win.dtype), S((DM, LANES), g_win.dtype)],
        compiler_params=_cp(ARB),
    )(g_win)


def _relayout_g(g_main, g_gate, WD, gate_lo, gate_hi):
    DM = g_main.shape[0]
    total = N_DEV * WD
    tr = _tile(DM, 256, 16)
    plan = _pieces(WD, gate_lo, gate_hi, total)

    def body(m_ref, gate_ref, out_ref):
        for d, s0, s1, dest, c0 in plan:
            src = m_ref if dest == "main" else gate_ref
            out_ref[d, :, s0:s1] = src[:, c0:c0 + (s1 - s0)].astype(BF16)

    return pl.pallas_call(
        body, name="relayout_g", grid=(DM // tr,),
        in_specs=[pl.BlockSpec((tr, g_main.shape[1]), lambda i: (i, 0)), pl.BlockSpec((tr, LANES), lambda i: (i, 0))],
        out_specs=pl.BlockSpec((N_DEV, tr, WD), lambda i: (0, i, 0)),
        out_shape=S((N_DEV, DM, WD), BF16),
        compiler_params=_cp(ARB),
    )(g_main, g_gate)


def _in_proj(x, norm_w, w_main, w_ba, shards):
    T, DM = x.shape
    NM = w_main.shape[1]
    tm = _tile(T, 1024, 8)
    tn = _tile(NM, 1024, LANES)
    ni, nj = T // tm, NM // tn
    ns = len(shards)

    def body(x_ref, nw_ref, w_ref, wba_ref, *rest):
        srcs = rest[:ns]
        proj_ref, ba_ref, xn_ref = rest[ns:ns + 3]
        gath = rest[ns + 3:2 * ns + 3]
        send_sems, recv_sems, local_sems = rest[2 * ns + 3:]
        i = pl.program_id(0)
        me, cps = _broadcast_copies(srcs, gath, send_sems, recv_sems)
        cps = cps + [pltpu.make_async_copy(srcs[a], gath[a].at[me], local_sems.at[a]) for a in range(ns)]

        @pl.when((i == 0) & (pl.program_id(1) == 0))
        def _():
            for cp in cps:
                cp.start()

        @pl.when((i == ni - 1) & (pl.program_id(1) == nj - 1))
        def _():
            for cp in cps:
                cp.wait()

        @pl.when(pl.program_id(1) == 0)
        def _():
            xv = x_ref[...]
            r = lax.rsqrt(jnp.mean(xv * xv, axis=-1, keepdims=True) + EPS)
            xn = (xv * r * nw_ref[...]).astype(BF16)
            xn_ref[...] = xn
            ba_ref[...] = jnp.dot(xn.astype(MXU), wba_ref[...].astype(MXU), preferred_element_type=F32)

        proj_ref[...] = jnp.dot(xn_ref[...].astype(MXU), w_ref[...].astype(MXU), preferred_element_type=F32)

    any_spec = pl.BlockSpec(memory_space=pl.ANY)
    res = pl.pallas_call(
        body, name="in_proj", grid=(ni, nj),
        in_specs=[pl.BlockSpec((tm, DM), lambda i, j: (i, 0)),
                  pl.BlockSpec((1, DM), lambda i, j: (0, 0)),
                  pl.BlockSpec((DM, tn), lambda i, j: (0, j)),
                  pl.BlockSpec((DM, LANES), lambda i, j: (0, 0))] + [any_spec] * ns,
        out_specs=[pl.BlockSpec((tm, tn), lambda i, j: (i, j)),
                   pl.BlockSpec((tm, LANES), lambda i, j: (i, 0)),
                   pl.BlockSpec((tm, DM), lambda i, j: (i, 0))] + [any_spec] * ns,
        out_shape=[S((T, NM), F32), S((T, LANES), F32), S((T, DM), BF16)]
        + [S((N_DEV,) + a.shape, a.dtype) for a in shards],
        scratch_shapes=[pltpu.SemaphoreType.DMA((ns, N_DEV - 1)), pltpu.SemaphoreType.DMA((ns, N_DEV - 1)),
                        pltpu.SemaphoreType.DMA((ns,))],
        compiler_params=_cp(ARB, ARB),
    )(x, norm_w, w_main, w_ba, *shards)
    return res[0], res[1], res[2], res[3:]


def _prep_a_fwd(proj, ba, conv_w, alog_row, dtb_row, H, D):
    T = proj.shape[0]
    AW = H * D
    C3 = 3 * AW
    tb = _tile(T, 256, CHUNK_A)
    nch = tb // CHUNK_A
    nblk = T // tb
    scale = float(D) ** -0.5

    def body(x_ref, halo_ref, ba_ref, cw_ref, al_ref, dt_ref, q_ref, k_ref, v_ref, c_ref, gcol_ref, grow_ref):
        i = pl.program_id(0)
        xv = x_ref[...]
        halo = halo_ref[...] * (i > 0).astype(F32)
        xp = jnp.concatenate([halo, xv], axis=0)
        cw = cw_ref[...]
        c = cw[0:1, :] * xp[5:5 + tb]
        for j in range(1, 4):
            c = c + cw[j:j + 1, :] * xp[5 + j:5 + j + tb]
        c_ref[...] = c
        a = _silu(c)
        for h in range(H):
            qh = a[:, h * D:(h + 1) * D]
            kh = a[:, AW + h * D:AW + (h + 1) * D]
            qr = lax.rsqrt(jnp.sum(qh * qh, axis=-1, keepdims=True) + EPS)
            kr = lax.rsqrt(jnp.sum(kh * kh, axis=-1, keepdims=True) + EPS)
            q_ref[:, h * D:(h + 1) * D] = qh * (qr * scale)
            k_ref[:, h * D:(h + 1) * D] = kh * kr
        v_ref[...] = a[:, 2 * AW:]

        bav = ba_ref[...]
        lane = lax.broadcasted_iota(jnp.int32, (tb, LANES), 1)
        beta = _sigmoid(bav)
        g = -jnp.exp(al_ref[...]) * _softplus(bav + dt_ref[...])
        gates = jnp.where(lane < H, beta, jnp.where(lane < 2 * H, g, 0.0))
        ri = lax.broadcasted_iota(jnp.int32, (CHUNK_A, CHUNK_A), 0)
        ci = lax.broadcasted_iota(jnp.int32, (CHUNK_A, CHUNK_A), 1)
        tri = (ri >= ci).astype(F32)
        lane_c = lax.broadcasted_iota(jnp.int32, (CHUNK_A, LANES), 1)
        for cc in range(nch):
            gch = gates[cc * CHUNK_A:(cc + 1) * CHUNK_A]
            gc = pltpu.roll(_mmh(tri, gch), H, 1)
            full = jnp.where(lane_c < 2 * H, gch, jnp.where(lane_c < 3 * H, gc, 0.0))
            gcol_ref[cc * CHUNK_A:(cc + 1) * CHUNK_A, :] = full
            grow_ref[cc] = full.T[0:32, :]

    return pl.pallas_call(
        body, name="prep_a_fwd", grid=(nblk,),
        in_specs=[pl.BlockSpec((tb, C3), lambda i: (i, 0)),
                  pl.BlockSpec((8, C3), lambda i: (jnp.maximum(i * (tb // 8) - 1, 0), 0)),
                  pl.BlockSpec((tb, LANES), lambda i: (i, 0)),
                  pl.BlockSpec((4, C3), lambda i: (0, 0)),
                  pl.BlockSpec((1, LANES), lambda i: (0, 0)),
                  pl.BlockSpec((1, LANES), lambda i: (0, 0))],
        out_specs=[pl.BlockSpec((tb, AW), lambda i: (i, 0)),
                   pl.BlockSpec((tb, AW), lambda i: (i, 0)),
                   pl.BlockSpec((tb, AW), lambda i: (i, 0)),
                   pl.BlockSpec((tb, C3), lambda i: (i, 0)),
                   pl.BlockSpec((tb, LANES), lambda i: (i, 0)),
                   pl.BlockSpec((nch, 32, CHUNK_A), lambda i: (i, 0, 0))],
        out_shape=[S((T, AW), F32), S((T, AW), F32), S((T, AW), F32), S((T, C3), F32),
                   S((T, LANES), F32), S((T // CHUNK_A, 32, CHUNK_A), F32)],
        compiler_params=_cp(ARB),
    )(proj, proj, ba, conv_w, alog_row, dtb_row)


_NN = (((1,), (0,)), ((), ()))
_TN = (((0,), (0,)), ((), ()))


def _split(a):
    hi = a.astype(BF16)
    return hi, (a - hi.astype(F32)).astype(BF16)


def _mm3(a, b, dims=_NN):
    ah, al = a if isinstance(a, tuple) else _split(a)
    bh, bl = b if isinstance(b, tuple) else _split(b)
    dg = lambda p, r: lax.dot_general(p, r, dims, preferred_element_type=F32)
    return dg(ah, bh) + (dg(ah, bl) + dg(al, bh))


def _interleave(gens):
    gens = list(gens)
    while gens:
        alive = []
        for g in gens:
            try:
                next(g)
                alive.append(g)
            except StopIteration:
                pass
        gens = alive


def _chunk_terms(q, k, v, gcolv, growv, h, H):
    C = CHUNK_A
    beta_c = gcolv[:, h:h + 1]
    g_c = gcolv[:, H + h:H + h + 1]
    gc_c = gcolv[:, 2 * H + h:2 * H + h + 1]
    gc_r = growv[2 * H + h:2 * H + h + 1, :]
    ri = lax.broadcasted_iota(jnp.int32, (C, C), 0)
    ci = lax.broadcasted_iota(jnp.int32, (C, C), 1)
    incl = ri >= ci
    strict = ri > ci
    kb = k * beta_c
    vb = v * beta_c
    p_raw = _mm_nt(kb, k)
    qk_raw = _mm_nt(q, k)
    gam = jnp.where(incl, jnp.exp(jnp.where(incl, gc_c - gc_r, 0.0)), 0.0)
    e_c = jnp.exp(gc_c)
    gl = gc_r[:, C - 1:C]
    edec = jnp.exp(gl - gc_c)
    yield
    lmat = jnp.where(strict, p_raw * gam, 0.0)
    attn = jnp.where(incl, qk_raw * gam, 0.0)
    return dict(beta_c=beta_c, g_c=g_c, gc_c=gc_c, gc_r=gc_r, incl=incl, strict=strict, gam=gam, e_c=e_c,
                kb=kb, vb=vb, lmat=lmat, attn=attn, gl=gl, edec=edec, ri=ri, ci=ci)


def _inv_unit_lower(lmat):
    C = lmat.shape[0]
    ri = lax.broadcasted_iota(jnp.int32, (C, C), 0)
    ci = lax.broadcasted_iota(jnp.int32, (C, C), 1)
    eye = (ri == ci).astype(F32)
    x = -lmat
    a = eye + x
    n = 1
    while 2 * n < C:
        xs = _split(x)
        x = _mm3(xs, xs)
        yield
        a = a + _mm3(a, x)
        n *= 2
    yield
    return a


def _delta_fwd(q, k, v, gcol, grow, H, D):
    T = q.shape[0]
    C = CHUNK_A
    N = T // C
    AW = H * D

    def body(q_ref, k_ref, v_ref, gcol_ref, grow_ref, o_ref, vn_ref, ssave_ref, asave_ref, s_ref):
        @pl.when(pl.program_id(0) == 0)
        def _():
            s_ref[...] = jnp.zeros_like(s_ref)

        gcolv = gcol_ref[...]
        growv = grow_ref[0]

        def head(h):
            sl = slice(h * D, (h + 1) * D)
            st = s_ref[h]
            ssave_ref[0, h] = st
            qv, kv, vv = q_ref[:, sl], k_ref[:, sl], v_ref[:, sl]
            t = yield from _chunk_terms(qv, kv, vv, gcolv, growv, h, H)
            ks = _mm(t["kb"] * t["e_c"], st)
            o_inter = _mm(qv * t["e_c"], st)
            a = yield from _inv_unit_lower(t["lmat"])
            asave_ref[0, h] = a
            v_new = _mm3(a, t["vb"] - ks)
            yield
            vn_ref[:, sl] = v_new
            o_intra = _mm(t["attn"], v_new)
            s_upd = _mm_tn(kv * t["edec"], v_new)
            yield
            o_ref[:, sl] = o_inter + o_intra
            s_ref[h] = st * jnp.exp(t["gl"]) + s_upd

        _interleave(head(h) for h in range(H))

    blk = lambda: pl.BlockSpec((C, AW), lambda n: (n, 0))
    return pl.pallas_call(
        body, name="delta_fwd", grid=(N,),
        in_specs=[blk(), blk(), blk(),
                  pl.BlockSpec((C, LANES), lambda n: (n, 0)),
                  pl.BlockSpec((1, 32, C), lambda n: (n, 0, 0))],
        out_specs=[blk(), blk(),
                   pl.BlockSpec((1, H, D, D), lambda n: (n, 0, 0, 0)),
                   pl.BlockSpec((1, H, C, C), lambda n: (n, 0, 0, 0))],
        out_shape=[S((T, AW), F32), S((T, AW), F32), S((N, H, D, D), F32), S((N, H, C, C), F32)],
        scratch_shapes=[pltpu.VMEM((H, D, D), F32)],
        compiler_params=_cp(ARB),
    )(q, k, v, gcol, grow)


def _delta_bwd(q, k, v, gcol, grow, ba, vnew, ssave, asave, d_o, a_log, dt_bias, H, D, carry):
    T = q.shape[0]
    C = CHUNK_A
    N = T // C
    AW = H * D
    nc = len(carry)

    def body(al_ref, dt_ref, q_ref, k_ref, v_ref, gcol_ref, grow_ref, ba_ref, vn_ref, ss_ref, as_ref, do_ref, *rest):
        cins = rest[:nc]
        dq_ref, dk_ref, dv_ref, dgate_ref, dpar_ref = rest[nc:nc + 5]
        couts = rest[nc + 5:2 * nc + 5]
        ds_ref, csend, crecv = rest[2 * nc + 5:]
        ccps = _chip_exchange_copies(cins, couts, csend, crecv)

        @pl.when(pl.program_id(0) == 0)
        def _():
            ds_ref[...] = jnp.zeros_like(ds_ref)
            dpar_ref[...] = jnp.zeros_like(dpar_ref)
            for cp in ccps:
                cp.start()

        gcolv = gcol_ref[...]
        growv = grow_ref[0]
        bav = ba_ref[...]
        lane = lax.broadcasted_iota(jnp.int32, (C, LANES), 1)
        lane1 = lax.broadcasted_iota(jnp.int32, (1, LANES), 1)
        rowi = lax.broadcasted_iota(jnp.int32, (C, 1), 0)
        acc = {"dgate": jnp.zeros((C, LANES), F32), "dpar": jnp.zeros((1, LANES), F32)}

        def head(h):
            sl = slice(h * D, (h + 1) * D)
            ds_next = ds_ref[h]
            st = ss_ref[0, h]
            a = as_ref[0, h]
            qv, kv, vv, dov, v_new = q_ref[:, sl], k_ref[:, sl], v_ref[:, sl], do_ref[:, sl], vn_ref[:, sl]
            t = yield from _chunk_terms(qv, kv, vv, gcolv, growv, h, H)
            beta_c, e_c, gam, kb = t["beta_c"], t["e_c"], t["gam"], t["kb"]
            incl, strict, attn, lmat, edec = t["incl"], t["strict"], t["attn"], t["lmat"], t["edec"]
            kdec = kv * edec
            egl = jnp.exp(t["gl"])
            qe = qv * e_c
            ekb = kb * e_c

            dkdec = _mm_nt(v_new, ds_next)
            dv_new_s = _mm(kdec, ds_next)
            t1 = _mm_nt(dov, st)
            ds_o = _mm_tn(qe, dov)
            dattn_raw = _mm_nt(dov, v_new)
            dv_new_o = _mm_tn(attn, dov)
            yield
            dgl = egl * jnp.sum(jnp.sum(st * ds_next, axis=1, keepdims=True), axis=0, keepdims=True)
            dk = edec * dkdec
            r = jnp.sum(dkdec * kdec, axis=1, keepdims=True)
            dgc = -r
            dgl = dgl + jnp.sum(r, axis=0, keepdims=True)
            dq = e_c * t1
            dgc = dgc + jnp.sum(t1 * qe, axis=1, keepdims=True)
            dattn = jnp.where(incl, dattn_raw, 0.0)
            dv_new = dv_new_s + dv_new_o
            dqm = dattn * gam
            z = dattn * attn
            dvb = _mm3(a, dv_new, _TN)
            dq_a = _mm(dqm, kv)
            dk_a = _mm_tn(dqm, qv)
            yield
            dq_ref[:, sl] = dq + dq_a
            dv_ref[:, sl] = beta_c * dvb
            ds_kb = _mm_tn(ekb, dvb)
            dekb_neg = _mm_nt(dvb, st)
            dl_neg = _mm_nt(dvb, v_new)
            yield
            ds_ref[h] = egl * ds_next + ds_o - ds_kb
            dekb = -dekb_neg
            dl = jnp.where(strict, -dl_neg, 0.0)
            dp = dl * gam
            z = z + dl * lmat
            dkb_p = _mm(dp, kv)
            dk_p = _mm_tn(dp, kb)
            dgc = dgc + jnp.sum(dekb * ekb, axis=1, keepdims=True)
            dgc = dgc + jnp.sum(z, axis=1, keepdims=True) - jnp.sum(z.T, axis=1, keepdims=True)
            dgc = dgc + jnp.where(rowi == C - 1, dgl, 0.0)
            upper = (t["ri"] <= t["ci"]).astype(F32)
            dg_b = _mm3(upper, jnp.broadcast_to(dgc, (C, LANES)))
            yield
            dkb = dkb_p + e_c * dekb
            dk_ref[:, sl] = dk + dk_a + dk_p + beta_c * dkb
            dbeta = jnp.sum(dkb * kv, axis=1, keepdims=True) + jnp.sum(dvb * vv, axis=1, keepdims=True)
            dg = dg_b[:, 0:1]
            a_raw = bav[:, H + h:H + h + 1]
            d_braw = dbeta * beta_c * (1.0 - beta_c)
            d_araw = dg * (-jnp.exp(al_ref[0, h])) * _sigmoid(a_raw + dt_ref[0, h])
            acc["dgate"] = acc["dgate"] + jnp.where(lane == h, d_braw, 0.0) + jnp.where(lane == H + h, d_araw, 0.0)
            dal = jnp.sum(dg * t["g_c"], axis=0, keepdims=True)
            ddt = jnp.sum(d_araw, axis=0, keepdims=True)
            acc["dpar"] = acc["dpar"] + jnp.where(lane1 == h, dal, 0.0) + jnp.where(lane1 == H + h, ddt, 0.0)

        _interleave(head(h) for h in range(H))
        dgate_ref[...] = acc["dgate"]
        dpar_ref[0:1, :] += acc["dpar"]

        @pl.when(pl.program_id(0) == N - 1)
        def _():
            for cp in ccps:
                cp.wait()

    rev = lambda s: N - 1 - s
    blk = lambda: pl.BlockSpec((C, AW), lambda s: (rev(s), 0))
    smem = pl.BlockSpec(memory_space=pltpu.SMEM)
    any_spec = pl.BlockSpec(memory_space=pl.ANY)
    res = pl.pallas_call(
        body, name="delta_bwd", grid=(N,),
        in_specs=[smem, smem, blk(), blk(), blk(),
                  pl.BlockSpec((C, LANES), lambda s: (rev(s), 0)),
                  pl.BlockSpec((1, 32, C), lambda s: (rev(s), 0, 0)),
                  pl.BlockSpec((C, LANES), lambda s: (rev(s), 0)),
                  blk(),
                  pl.BlockSpec((1, H, D, D), lambda s: (rev(s), 0, 0, 0)),
                  pl.BlockSpec((1, H, C, C), lambda s: (rev(s), 0, 0, 0)),
                  blk()] + [any_spec] * nc,
        out_specs=[blk(), blk(), blk(),
                   pl.BlockSpec((C, LANES), lambda s: (rev(s), 0)),
                   pl.BlockSpec((8, LANES), lambda s: (0, 0))] + [any_spec] * nc,
        out_shape=[S((T, AW), F32), S((T, AW), F32), S((T, AW), F32),
                   S((T, LANES), F32), S((8, LANES), F32)] + [S((3,) + a.shape[1:], a.dtype) for a in carry],
        scratch_shapes=[pltpu.VMEM((H, D, D), F32),
                        pltpu.SemaphoreType.DMA((max(nc, 1), 3)), pltpu.SemaphoreType.DMA((max(nc, 1), 3))],
        compiler_params=_cp(ARB),
    )(a_log, dt_bias, q, k, v, gcol, grow, ba, vnew, ssave, asave, d_o, *carry)
    return res[:5], res[5:]


def _ln_stats(xv):
    mu = jnp.mean(xv, axis=-1, keepdims=True)
    xc = xv - mu
    var = jnp.mean(xc * xc, axis=-1, keepdims=True)
    rstd = lax.rsqrt(var + EPS)
    return xc * rstd, rstd


def _mix_fwd(o, proj, head_norm_w, ln_w, ln_b, w_sp, bs_t, H, D, G, P):
    T = o.shape[0]
    AW, BW = H * D, G * P
    MIX = AW + BW
    nb = AW // BW if AW % BW == 0 else None
    assert nb == 1, "group widths must match the projection column blocks"
    cb = 3

    def body(o_ref, za_ref, ub_ref, vb_ref, zb_ref, hw_ref, lw_ref, lb_ref, w_ref, bs_ref, out_ref):
        hw = hw_ref[...]
        for h in range(H):
            sl = slice(h * D, (h + 1) * D)
            oh = o_ref[:, sl]
            rs = lax.rsqrt(jnp.mean(oh * oh, axis=-1, keepdims=True) + EPS)
            out_ref[:, sl] = (oh * rs * hw * _silu(za_ref[:, sl])).astype(BF16)
        xhat, _ = _ln_stats(vb_ref[...])
        vn = xhat * lw_ref[...] + lb_ref[...]
        ri = lax.broadcasted_iota(jnp.int32, (P, P), 0)
        ci = lax.broadcasted_iota(jnp.int32, (P, P), 1)
        bsv = bs_ref[...]
        for g in range(G):
            sl = slice(g * P, (g + 1) * P)
            wm = jnp.where(ri >= ci, w_ref[g], 0.0)
            s = _mm(wm, vn[:, sl]) + bsv[:, g:g + 1]
            out_ref[:, AW + g * P:AW + (g + 1) * P] = (ub_ref[:, sl] * s * _silu(zb_ref[:, sl])).astype(BF16)

    row = lambda w: pl.BlockSpec((1, w), lambda i: (0, 0))
    return pl.pallas_call(
        body, name="mix_fwd", grid=(T // P,),
        in_specs=[pl.BlockSpec((P, AW), lambda i: (i, 0)),
                  pl.BlockSpec((P, AW), lambda i: (i, cb)),
                  pl.BlockSpec((P, BW), lambda i: (i, cb + 1)),
                  pl.BlockSpec((P, BW), lambda i: (i, cb + 2)),
                  pl.BlockSpec((P, BW), lambda i: (i, cb + 3)),
                  row(D), row(BW), row(BW),
                  pl.BlockSpec((G, P, P), lambda i: (0, 0, 0)),
                  pl.BlockSpec((P, G), lambda i: (0, 0))],
        out_specs=pl.BlockSpec((P, MIX), lambda i: (i, 0)),
        out_shape=S((T, MIX), BF16),
        compiler_params=_cp(ARB),
    )(o, proj, proj, proj, proj, head_norm_w, ln_w, ln_b, w_sp, bs_t)


def _mix_bwd(d_ocat, o, proj, head_norm_w, ln_w, ln_b, w_sp, bs_t, H, D, G, P, carry):
    T = o.shape[0]
    AW, BW = H * D, G * P
    MIX = AW + BW
    cb = 3
    nc = len(carry)

    def body(dc_ref, o_ref, za_ref, ub_ref, vb_ref, zb_ref, hw_ref, lw_ref, lb_ref, w_ref, bs_ref, *rest):
        cins = rest[:nc]
        do_ref, dmain_ref, dhw_ref, dln_ref, dw_ref, dbs_ref = rest[nc:nc + 6]
        couts = rest[nc + 6:2 * nc + 6]
        dvn_ref, drest_ref, out_sems, csend, crecv = rest[2 * nc + 6:]
        i = pl.program_id(0)
        slot = lax.rem(i, 2)
        ccps = _sibling_copies(cins, couts, csend, crecv)

        def out_copy(step, s):
            return pltpu.make_async_copy(
                drest_ref.at[s], dmain_ref.at[pl.ds(step * P, P), pl.ds(cb * AW, AW + 3 * BW)], out_sems.at[s])

        @pl.when(i == 0)
        def _():
            dhw_ref[...] = jnp.zeros_like(dhw_ref)
            dln_ref[...] = jnp.zeros_like(dln_ref)
            dw_ref[...] = jnp.zeros_like(dw_ref)
            dbs_ref[...] = jnp.zeros_like(dbs_ref)
            for cp in ccps:
                cp.start()

        @pl.when(i >= 2)
        def _():
            out_copy(i - 2, slot).wait()

        hw = hw_ref[...]
        dhw = jnp.zeros((1, D), F32)
        for h in range(H):
            sl = slice(h * D, (h + 1) * D)
            oh = o_ref[:, sl]
            za = za_ref[:, sl]
            doa = dc_ref[:, sl]
            rs = lax.rsqrt(jnp.mean(oh * oh, axis=-1, keepdims=True) + EPS)
            xh = oh * rs
            d_on = doa * _silu(za)
            drest_ref[slot, :, sl] = (doa * (xh * hw) * _dsilu(za)).astype(BF16)
            dhw = dhw + jnp.sum(d_on * xh, axis=0, keepdims=True)
            dxh = d_on * hw
            do_ref[:, sl] = rs * (dxh - xh * jnp.mean(dxh * xh, axis=-1, keepdims=True))
        dhw_ref[0:1, :] += dhw

        xhat, rstd = _ln_stats(vb_ref[...])
        lw = lw_ref[...]
        vn = xhat * lw + lb_ref[...]
        ri = lax.broadcasted_iota(jnp.int32, (P, P), 0)
        ci = lax.broadcasted_iota(jnp.int32, (P, P), 1)
        lane = lax.broadcasted_iota(jnp.int32, (P, LANES), 1)
        bsv = bs_ref[...]
        dbs = jnp.zeros((P, LANES), F32)
        for g in range(G):
            sl = slice(g * P, (g + 1) * P)
            wm = jnp.where(ri >= ci, w_ref[g], 0.0)
            vng = vn[:, sl]
            s = _mm(wm, vng) + bsv[:, g:g + 1]
            dob = dc_ref[:, AW + g * P:AW + (g + 1) * P]
            ub = ub_ref[:, sl]
            zb = zb_ref[:, sl]
            szb = _silu(zb)
            drest_ref[slot, :, AW + g * P:AW + (g + 1) * P] = (dob * s * szb).astype(BF16)
            drest_ref[slot, :, AW + 2 * BW + g * P:AW + 2 * BW + (g + 1) * P] = (
                dob * ub * s * _dsilu(zb)).astype(BF16)
            ds = dob * ub * szb
            dvn_ref[:, sl] = _mm_tn(wm, ds)
            dw_ref[g] += jnp.where(ri >= ci, _mm_nt(ds, vng), 0.0)
            dbs = dbs + jnp.where(lane == g, jnp.sum(ds, axis=1, keepdims=True), 0.0)
        dbs_ref[...] += dbs
        dvn = dvn_ref[...]
        dln_ref[0:1, :] += jnp.sum(dvn * xhat, axis=0, keepdims=True)
        dln_ref[1:2, :] += jnp.sum(dvn, axis=0, keepdims=True)
        dxh = dvn * lw
        dvb = rstd * (dxh - jnp.mean(dxh, axis=-1, keepdims=True) - xhat * jnp.mean(dxh * xhat, axis=-1, keepdims=True))
        drest_ref[slot, :, AW + BW:AW + 2 * BW] = dvb.astype(BF16)

        out_copy(i, slot).start()

        @pl.when(i == nstep - 1)
        def _():
            out_copy(i, slot).wait()
            if nstep > 1:
                out_copy(i - 1, 1 - slot).wait()
            for cp in ccps:
                cp.wait()

    nstep = T // P
    row = lambda w: pl.BlockSpec((1, w), lambda i: (0, 0))
    any_spec = pl.BlockSpec(memory_space=pl.ANY)
    res = pl.pallas_call(
        body, name="mix_bwd", grid=(nstep,),
        in_specs=[pl.BlockSpec((P, MIX), lambda i: (i, 0)),
                  pl.BlockSpec((P, AW), lambda i: (i, 0)),
                  pl.BlockSpec((P, AW), lambda i: (i, cb)),
                  pl.BlockSpec((P, BW), lambda i: (i, cb + 1)),
                  pl.BlockSpec((P, BW), lambda i: (i, cb + 2)),
                  pl.BlockSpec((P, BW), lambda i: (i, cb + 3)),
                  row(D), row(BW), row(BW),
                  pl.BlockSpec((G, P, P), lambda i: (0, 0, 0)),
                  pl.BlockSpec((P, G), lambda i: (0, 0))] + [any_spec] * nc,
        out_specs=[pl.BlockSpec((P, AW), lambda i: (i, 0)),
                   any_spec,
                   pl.BlockSpec((8, D), lambda i: (0, 0)),
                   pl.BlockSpec((8, BW), lambda i: (0, 0)),
                   pl.BlockSpec((G, P, P), lambda i: (0, 0, 0)),
                   pl.BlockSpec((P, LANES), lambda i: (0, 0))] + [any_spec] * nc,
        out_shape=[S((T, AW), F32), S((T, cb * AW + AW + 3 * BW), BF16), S((8, D), F32), S((8, BW), F32),
                   S((G, P, P), F32), S((P, LANES), F32)] + [S(a.shape[:1] + a.shape[2:], a.dtype) for a in carry],
        scratch_shapes=[pltpu.VMEM((P, BW), F32), pltpu.VMEM((2, P, AW + 3 * BW), BF16),
                        pltpu.SemaphoreType.DMA((2,))] + _sibling_sems(carry),
        compiler_params=_cp(ARB),
    )(d_ocat, o, proj, proj, proj, proj, head_norm_w, ln_w, ln_b, w_sp, bs_t, *carry)
    return res[:6], res[6:]


def _out_proj_loss(ocat, w_out, x, target, fnw):
    T, MIX = ocat.shape
    DM = x.shape[1]
    tm = _tile(T, 256, 8)

    def body(oc_ref, w_ref, x_ref, t_ref, fw_ref, dh_ref, dhb_ref, doc_ref, loss_ref, gfw_ref):
        @pl.when(pl.program_id(0) == 0)
        def _():
            loss_ref[...] = jnp.zeros_like(loss_ref)
            gfw_ref[...] = jnp.zeros_like(gfw_ref)

        wv = w_ref[...]
        hh = x_ref[...] + jnp.dot(oc_ref[...].astype(MXU), wv.astype(MXU), preferred_element_type=F32)
        rs = lax.rsqrt(jnp.mean(hh * hh, axis=-1, keepdims=True) + EPS)
        hn = hh * rs
        fw = fw_ref[...]
        e = hn * fw - t_ref[...]
        row_loss = 0.5 * jnp.mean(e * e, axis=-1, keepdims=True)
        loss_ref[...] += jnp.sum(row_loss, axis=0, keepdims=True)
        dy = e * (1.0 / DM)
        gfw_ref[0:1, :] += jnp.sum(dy * hn, axis=0, keepdims=True)
        dhn = dy * fw
        dh = rs * (dhn - hn * jnp.mean(dhn * hn, axis=-1, keepdims=True))
        dh_ref[...] = dh
        dhb = dh.astype(BF16)
        dhb_ref[...] = dhb
        doc_ref[...] = _mm_nt(dhb, wv)

    return pl.pallas_call(
        body, name="out_proj_loss", grid=(T // tm,),
        in_specs=[pl.BlockSpec((tm, MIX), lambda i: (i, 0)),
                  pl.BlockSpec((MIX, DM), lambda i: (0, 0)),
                  pl.BlockSpec((tm, DM), lambda i: (i, 0)),
                  pl.BlockSpec((tm, DM), lambda i: (i, 0)),
                  pl.BlockSpec((1, DM), lambda i: (0, 0))],
        out_specs=[pl.BlockSpec((tm, DM), lambda i: (i, 0)),
                   pl.BlockSpec((tm, DM), lambda i: (i, 0)),
                   pl.BlockSpec((tm, MIX), lambda i: (i, 0)),
                   pl.BlockSpec((8, LANES), lambda i: (0, 0)),
                   pl.BlockSpec((8, DM), lambda i: (0, 0))],
        out_shape=[S((T, DM), F32), S((T, DM), BF16), S((T, MIX), F32), S((8, LANES), F32), S((8, DM), F32)],
        compiler_params=_cp(ARB),
    )(ocat, w_out, x, target, fnw)


def _grad_w(lhs, rhs, name):
    T, A = lhs.shape
    B = rhs.shape[1]
    ta = _tile(A, 512, LANES)
    tk = _tile(T, 1024, 16)
    nk = T // tk

    def body(l_ref, r_ref, out_ref, acc_ref):
        k = pl.program_id(1)
        part = _mm_tn(l_ref[...], r_ref[...])

        @pl.when(k == 0)
        def _():
            acc_ref[...] = part

        @pl.when(k > 0)
        def _():
            acc_ref[...] += part

        @pl.when(k == nk - 1)
        def _():
            out_ref[...] = acc_ref[...].astype(BF16)

    return pl.pallas_call(
        body, name=name, grid=(A // ta, nk),
        in_specs=[pl.BlockSpec((tk, ta), lambda i, k: (k, i)),
                  pl.BlockSpec((tk, B), lambda i, k: (k, 0))],
        out_specs=pl.BlockSpec((ta, B), lambda i, k: (i, 0)),
        out_shape=S((A, B), BF16),
        scratch_shapes=[pltpu.VMEM((ta, B), F32)],
        compiler_params=_cp(ARB, ARB),
    )(lhs, rhs)


def _grad_w_in(xn, dmain, dba):
    T, DM = xn.shape
    NM = dmain.shape[1]
    tn = _tile(NM, 1024, LANES)
    tk = _tile(T, 2048, 16)

    def body(xn_ref, dm_ref, dba_ref, gm_ref, gba_ref):
        j = pl.program_id(0)
        k = pl.program_id(1)

        @pl.when(k == 0)
        def _():
            gm_ref[...] = jnp.zeros_like(gm_ref)

        @pl.when((k == 0) & (j == 0))
        def _():
            gba_ref[...] = jnp.zeros_like(gba_ref)

        xv = xn_ref[...]
        gm_ref[...] += _mm_tn(xv, dm_ref[...])

        @pl.when(j == 0)
        def _():
            gba_ref[...] += _mm_tn(xv, dba_ref[...])

    return pl.pallas_call(
        body, name="grad_w_in", grid=(NM // tn, T // tk),
        in_specs=[pl.BlockSpec((tk, DM), lambda j, k: (k, 0)),
                  pl.BlockSpec((tk, tn), lambda j, k: (k, j)),
                  pl.BlockSpec((tk, LANES), lambda j, k: (k, 0))],
        out_specs=[pl.BlockSpec((DM, tn), lambda j, k: (0, j)),
                   pl.BlockSpec((DM, LANES), lambda j, k: (0, 0))],
        out_shape=[S((DM, NM), F32), S((DM, LANES), F32)],
        compiler_params=_cp(ARB, ARB),
    )(xn, dmain, dba)


def _dx(dmain, dba, w_main, w_ba, x, dh, norm_w, chip_sum, small):
    T, NM = dmain.shape
    DM = x.shape[1]
    tm = _tile(T, 512, 8)
    tk = _tile(NM, 1024, LANES)
    nk = NM // tk
    ni = T // tm
    half = chip_sum.shape[1] // 2
    assert half % 16 == 0
    last_step = ni * nk - 1
    relay_step = min(2 * nk, last_step)

    def body(dm_ref, dba_ref, w_ref, wba_ref, x_ref, dh_ref, nw_ref, small_ref, cs_ref,
             gx_ref, gnw_ref, gath_ref, recv_ref, stage_ref, acc_ref, csend, crecv, ssend, srecv, lsem):
        i = pl.program_id(0)
        k = pl.program_id(1)
        step = i * nk + k
        px, py, pc = _position()
        xn, yn = (1 - px, py, pc), (px, 1 - py, pc)
        upper, lower = pl.ds(0, half), pl.ds(half, half)

        def rcopy(kk, src, dst, to):
            return pltpu.make_async_remote_copy(src_ref=src, dst_ref=dst, send_sem=csend.at[kk], recv_sem=crecv.at[kk],
                                                device_id=to, device_id_type=MESH)

        diag_blk = cs_ref.at[2 * (1 - px) + (1 - py)]
        to_stage = [rcopy(2, diag_blk.at[upper], stage_ref.at[0], xn), rcopy(3, diag_blk.at[lower], stage_ref.at[1], yn)]
        direct = [rcopy(0, cs_ref.at[2 * (1 - px) + py], recv_ref.at[0], xn),
                  rcopy(1, cs_ref.at[2 * px + (1 - py)], recv_ref.at[1], yn)]
        onward = [rcopy(4, stage_ref.at[0], recv_ref.at[2].at[upper], yn),
                  rcopy(5, stage_ref.at[1], recv_ref.at[2].at[lower], xn)]
        me, small_cps = _broadcast_copies([small_ref], [gath_ref], ssend, srecv)
        small_cps = small_cps + [pltpu.make_async_copy(small_ref, gath_ref.at[me], lsem.at[0])]

        @pl.when(step == 0)
        def _():
            gnw_ref[...] = jnp.zeros_like(gnw_ref)
            for cp in to_stage + direct + small_cps:
                cp.start()

        @pl.when(step == relay_step)
        def _():
            for cp in to_stage:
                cp.wait_recv()
            for cp in onward:
                cp.start()

        @pl.when(k == 0)
        def _():
            acc_ref[...] = _mm_nt(dba_ref[...], wba_ref[...])

        acc_ref[...] += _mm_nt(dm_ref[...], w_ref[...])

        @pl.when(k == nk - 1)
        def _():
            xv = x_ref[...]
            rs = lax.rsqrt(jnp.mean(xv * xv, axis=-1, keepdims=True) + EPS)
            xh = xv * rs
            dxn = acc_ref[...]
            gnw_ref[0:1, :] += jnp.sum(dxn * xh, axis=0, keepdims=True)
            dxh = dxn * nw_ref[...]
            gx_ref[...] = dh_ref[...] + rs * (dxh - xh * jnp.mean(dxh * xh, axis=-1, keepdims=True))

        @pl.when(step == last_step)
        def _():
            for cp in to_stage:
                cp.wait_send()
            for cp in direct + onward + small_cps:
                cp.wait()

    any_spec = pl.BlockSpec(memory_space=pl.ANY)
    res = pl.pallas_call(
        body, name="dx", grid=(ni, nk),
        in_specs=[pl.BlockSpec((tm, tk), lambda i, k: (i, k)),
                  pl.BlockSpec((tm, LANES), lambda i, k: (i, 0)),
                  pl.BlockSpec((DM, tk), lambda i, k: (0, k)),
                  pl.BlockSpec((DM, LANES), lambda i, k: (0, 0)),
                  pl.BlockSpec((tm, DM), lambda i, k: (i, 0)),
                  pl.BlockSpec((tm, DM), lambda i, k: (i, 0)),
                  pl.BlockSpec((1, DM), lambda i, k: (0, 0)),
                  any_spec, any_spec],
        out_specs=[pl.BlockSpec((tm, DM), lambda i, k: (i, 0)),
                   pl.BlockSpec((8, DM), lambda i, k: (0, 0)),
                   any_spec, any_spec, any_spec],
        out_shape=[S((T, DM), F32), S((8, DM), F32), S((N_DEV,) + small.shape, F32),
                   S((3,) + chip_sum.shape[1:], chip_sum.dtype), S((2, half) + chip_sum.shape[2:], chip_sum.dtype)],
        scratch_shapes=[pltpu.VMEM((tm, DM), F32),
                        pltpu.SemaphoreType.DMA((6,)), pltpu.SemaphoreType.DMA((6,)),
                        pltpu.SemaphoreType.DMA((1, N_DEV - 1)), pltpu.SemaphoreType.DMA((1, N_DEV - 1)),
                        pltpu.SemaphoreType.DMA((1,))],
        compiler_params=_cp(ARB, ARB),
    )(dmain, dba, w_main, w_ba, x, dh, norm_w, small, chip_sum)
    return res[0], res[1], res[2], res[3]


def _sum_slots(gath):
    _, R, C = gath.shape
    tr = _tile(R, 512, 8)

    def body(g_ref, o_ref):
        tot = g_ref[0]
        for d in range(1, N_DEV):
            tot = tot + g_ref[d]
        o_ref[...] = tot

    return pl.pallas_call(
        body, name="sum_slots", grid=(R // tr,),
        in_specs=[pl.BlockSpec((N_DEV, tr, C), lambda i: (0, i, 0))],
        out_specs=pl.BlockSpec((tr, C), lambda i: (i, 0)),
        out_shape=S((R, C), F32), compiler_params=_cp(ARB),
    )(gath)


def _prep_a_bwd_pointwise(dq, dk, dv, c, H, D):
    T = c.shape[0]
    AW = H * D
    C3 = 3 * AW
    tb = _tile(T, 256, 8)
    scale = float(D) ** -0.5

    def body(dq_ref, dk_ref, dv_ref, c_ref, dc_ref):
        for h in range(H):
            for part, d_ref, sc in ((0, dq_ref, scale), (1, dk_ref, 1.0)):
                sl = slice(part * AW + h * D, part * AW + (h + 1) * D)
                cv = c_ref[:, sl]
                raw = _silu(cv)
                rs = lax.rsqrt(jnp.sum(raw * raw, axis=-1, keepdims=True) + EPS)
                nrm = raw * rs
                dn = d_ref[:, h * D:(h + 1) * D] * sc
                draw = rs * (dn - nrm * jnp.sum(dn * nrm, axis=-1, keepdims=True))
                dc_ref[:, sl] = draw * _dsilu(cv)
        dc_ref[:, 2 * AW:] = dv_ref[...] * _dsilu(c_ref[:, 2 * AW:])

    return pl.pallas_call(
        body, name="prep_a_bwd_pointwise", grid=(T // tb,),
        in_specs=[pl.BlockSpec((tb, AW), lambda i: (i, 0)),
                  pl.BlockSpec((tb, AW), lambda i: (i, 0)),
                  pl.BlockSpec((tb, AW), lambda i: (i, 0)),
                  pl.BlockSpec((tb, C3), lambda i: (i, 0))],
        out_specs=pl.BlockSpec((tb, C3), lambda i: (i, 0)),
        out_shape=S((T, C3), F32),
        compiler_params=_cp(ARB),
    )(dq, dk, dv, c)


def _conv_bwd(dc, proj, conv_w, dmain, C3):
    T = dc.shape[0]
    tb = _tile(T, 256, 8)
    nblk = T // tb
    r8 = tb // 8

    def body(dc_ref, dnext_ref, x_ref, halo_ref, cw_ref, dmain_in_ref, dx_ref, gcw_ref):
        del dmain_in_ref
        i = pl.program_id(0)

        @pl.when(i == 0)
        def _():
            gcw_ref[...] = jnp.zeros_like(gcw_ref)

        dcv = dc_ref[...]
        dnext = dnext_ref[...] * (i < nblk - 1).astype(F32)
        dcp = jnp.concatenate([dcv, dnext], axis=0)
        cw = cw_ref[...]
        dx = cw[3:4, :] * dcv
        for j in range(3):
            dx = dx + cw[j:j + 1, :] * dcp[3 - j:3 - j + tb]
        dx_ref[...] = dx.astype(BF16)
        halo = halo_ref[...] * (i > 0).astype(F32)
        xp = jnp.concatenate([halo, x_ref[...]], axis=0)
        for j in range(4):
            gcw_ref[j:j + 1, :] += jnp.sum(dcv * xp[5 + j:5 + j + tb], axis=0, keepdims=True)

    return pl.pallas_call(
        body, name="conv_bwd", grid=(nblk,),
        in_specs=[pl.BlockSpec((tb, C3), lambda i: (i, 0)),
                  pl.BlockSpec((8, C3), lambda i: (jnp.minimum((i + 1) * r8, T // 8 - 1), 0)),
                  pl.BlockSpec((tb, C3), lambda i: (i, 0)),
                  pl.BlockSpec((8, C3), lambda i: (jnp.maximum(i * r8 - 1, 0), 0)),
                  pl.BlockSpec((4, C3), lambda i: (0, 0)),
                  pl.BlockSpec(memory_space=pl.ANY)],
        out_specs=[pl.BlockSpec((tb, C3), lambda i: (i, 0)),
                   pl.BlockSpec((8, C3), lambda i: (0, 0))],
        out_shape=[S(dmain.shape, dmain.dtype), S((8, C3), F32)],
        input_output_aliases={5: 0},
        compiler_params=_cp(ARB),
    )(dc, dc, proj, proj, conv_w, dmain)


def _adam_math(w, g, m, v):
    m2 = ADAM_B1 * m + (1.0 - ADAM_B1) * g
    v2 = ADAM_B2 * v + (1.0 - ADAM_B2) * (g * g)
    m_hat = m2 / (1.0 - ADAM_B1 ** ADAM_STEP)
    v_hat = v2 / (1.0 - ADAM_B2 ** ADAM_STEP)
    delta = -ADAM_LR * (m_hat / (jnp.sqrt(v_hat) + ADAM_EPS) + ADAM_WD * w)
    return delta, m2, v2


def _pair_sum(blocks, recv, core, name):
    K, _, R, C = blocks.shape
    tr = _tile(R, 256, 16)

    def body(core_ref, a_ref, b_ref, o_ref):
        del core_ref
        o_ref[0] = (a_ref[0, 0].astype(F32) + b_ref[0].astype(F32)).astype(BF16)

    spec = lambda: pl.BlockSpec((1, tr, C), lambda k, i, core_ref: (k, i, 0))
    return pl.pallas_call(
        body, name=name,
        grid_spec=pltpu.PrefetchScalarGridSpec(
            num_scalar_prefetch=1, grid=(K, R // tr),
            in_specs=[pl.BlockSpec((1, 1, tr, C), lambda k, i, core_ref: (k, core_ref[0], i, 0)), spec()],
            out_specs=spec()),
        out_shape=S((K, R, C), BF16), compiler_params=_cp(ARB, ARB),
    )(core, blocks, recv)


def _sum_adam(chip_sums, recv, w, m, v, chip, name, transposed=False):
    R, C = chip_sums.shape[1:]
    tr = _tile(R, 256, 16)

    def body(chip_ref, own_ref, r_ref, w_ref, m_ref, v_ref, g_ref, d_ref, m2_ref, v2_ref):
        del chip_ref
        g = own_ref[0].astype(F32)
        for j in range(3):
            g = g + r_ref[j].astype(F32)
        if transposed:
            g = g.T
        g_ref[...] = g
        d_ref[...], m2_ref[...], v2_ref[...] = _adam_math(w_ref[...], g, m_ref[...], v_ref[...])

    if transposed:
        spec = lambda: pl.BlockSpec((C, tr), lambda i, chip_ref: (0, i))
        shape = (C, R)
    else:
        spec = lambda: pl.BlockSpec((tr, C), lambda i, chip_ref: (i, 0))
        shape = (R, C)
    assert w.shape == shape
    return pl.pallas_call(
        body, name=name,
        grid_spec=pltpu.PrefetchScalarGridSpec(
            num_scalar_prefetch=1, grid=(R // tr,),
            in_specs=[pl.BlockSpec((1, tr, C), lambda i, chip_ref: (chip_ref[0], i, 0)),
                      pl.BlockSpec((3, tr, C), lambda i, chip_ref: (0, i, 0)), spec(), spec(), spec()],
            out_specs=[spec(), spec(), spec(), spec()]),
        out_shape=[S(shape, F32)] * 4, compiler_params=_cp(ARB),
    )(chip, chip_sums, recv, w, m, v)


def _adam_small(w, g, m, v):
    R, C = w.shape
    tr = _tile(R, 512, 8)

    def body(w_ref, g_ref, m_ref, v_ref, d_ref, m2_ref, v2_ref):
        d_ref[...], m2_ref[...], v2_ref[...] = _adam_math(w_ref[...], g_ref[...], m_ref[...], v_ref[...])

    spec = lambda: pl.BlockSpec((tr, C), lambda i: (i, 0))
    return pl.pallas_call(
        body, name="adam_small", grid=(R // tr,), in_specs=[spec()] * 4, out_specs=[spec()] * 3,
        out_shape=[S((R, C), F32)] * 3, compiler_params=_cp(ARB),
    )(w, g, m, v)


def _position():
    return lax.axis_index("x"), lax.axis_index("y"), lax.axis_index("c")


def _all_gather_weights(arr):
    R = arr.shape[0]
    half = R // 2
    assert half % 16 == 0

    def body(in_ref, out_ref, send_sems, recv_sems, local_sem):
        x, y, c = _position()
        me, sibling = (x, y, c), (x, y, 1 - c)
        xn, yn, diag = (1 - x, y), (x, 1 - y), (1 - x, 1 - y)
        upper, lower = pl.ds(0, half), pl.ds(half, half)

        def slot(p, rows=None):
            ref = out_ref.at[4 * p[0] + 2 * p[1] + p[2]]
            return ref if rows is None else ref.at[rows]

        def copy(kk, block, to, rows=None, src=None):
            return pltpu.make_async_remote_copy(
                src_ref=slot(block, rows) if src is None else src, dst_ref=slot(block, rows),
                send_sem=send_sems.at[kk], recv_sem=recv_sems.at[kk], device_id=to, device_id_type=MESH)

        mine = pltpu.make_async_copy(in_ref, slot(me), local_sem)
        mine.start()
        sent = [copy(0, me, sibling, src=in_ref), copy(1, me, (*xn, c), src=in_ref), copy(2, me, (*yn, c), src=in_ref)]
        for cp in sent:
            cp.start()

        def then(cps):
            for cp in cps:
                cp.start()
            sent.extend(cps)

        copy(1, (*xn, c), me).wait_recv()
        then([copy(5, (*xn, c), (*yn, c), rows=upper), copy(3, (*xn, c), sibling)])
        copy(2, (*yn, c), me).wait_recv()
        then([copy(6, (*yn, c), (*xn, c), rows=lower), copy(4, (*yn, c), sibling)])
        copy(5, (*diag, c), me, rows=upper).wait_recv()
        then([copy(7, (*diag, c), sibling, rows=upper)])
        copy(6, (*diag, c), me, rows=lower).wait_recv()
        then([copy(8, (*diag, c), sibling, rows=lower)])
        copy(0, sibling, me).wait_recv()
        copy(3, (*xn, 1 - c), me).wait_recv()
        copy(4, (*yn, 1 - c), me).wait_recv()
        copy(7, (*diag, 1 - c), me, rows=upper).wait_recv()
        copy(8, (*diag, 1 - c), me, rows=lower).wait_recv()
        for cp in sent:
            cp.wait_send()
        mine.wait()

    any_spec = pl.BlockSpec(memory_space=pl.ANY)
    return pl.pallas_call(
        body, name="all_gather_weights", in_specs=[any_spec], out_specs=any_spec,
        out_shape=S((N_DEV,) + arr.shape, arr.dtype),
        scratch_shapes=[pltpu.SemaphoreType.DMA((9,)), pltpu.SemaphoreType.DMA((9,)), pltpu.SemaphoreType.DMA],
    )(arr)


def _sibling_copies(ins, outs, send_sems, recv_sems):
    x, y, c = _position()
    return [pltpu.make_async_remote_copy(src_ref=ins[a].at[k, 1 - c], dst_ref=outs[a].at[k],
                                         send_sem=send_sems.at[a, k], recv_sem=recv_sems.at[a, k],
                                         device_id=(x, y, 1 - c), device_id_type=MESH)
            for a in range(len(ins)) for k in range(ins[a].shape[0])]


def _sibling_sems(arrs):
    shape = (max(len(arrs), 1), arrs[0].shape[0] if arrs else 1)
    return [pltpu.SemaphoreType.DMA(shape), pltpu.SemaphoreType.DMA(shape)]


def _exchange_sibling(arrs):
    na = len(arrs)

    def body(*refs):
        ins, outs = refs[:na], refs[na:2 * na]
        send_sems, recv_sems = refs[2 * na:]
        cps = _sibling_copies(ins, outs, send_sems, recv_sems)
        for cp in cps:
            cp.start()
        for cp in cps:
            cp.wait()

    any_spec = pl.BlockSpec(memory_space=pl.ANY)
    return pl.pallas_call(
        body, name="exchange_sibling", in_specs=[any_spec] * na, out_specs=[any_spec] * na,
        out_shape=[S(a.shape[:1] + a.shape[2:], a.dtype) for a in arrs],
        scratch_shapes=_sibling_sems(arrs),
    )(*arrs)


def _chip_exchange_copies(ins, outs, send_sems, recv_sems):
    x, y, c = _position()
    chips = [(1 - x, y), (x, 1 - y), (1 - x, 1 - y)]
    return [pltpu.make_async_remote_copy(
        src_ref=ins[a].at[2 * qx + qy], dst_ref=outs[a].at[j], send_sem=send_sems.at[a, j],
        recv_sem=recv_sems.at[a, j], device_id=(qx, qy, c), device_id_type=MESH)
        for a in range(len(ins)) for j, (qx, qy) in enumerate(chips)]


def _broadcast_copies(srcs, dsts, send_sems, recv_sems):
    x, y, c = _position()
    me = 4 * x + 2 * y + c
    cps = []
    for a in range(len(srcs)):
        for k in range(1, N_DEV):
            peer = (1 - x if k & 4 else x, 1 - y if k & 2 else y, 1 - c if k & 1 else c)
            cps.append(pltpu.make_async_remote_copy(
                src_ref=srcs[a], dst_ref=dsts[a].at[me], send_sem=send_sems.at[a, k - 1],
                recv_sem=recv_sems.at[a, k - 1], device_id=peer, device_id_type=MESH))
    return me, cps


def _all_reduce_small(part):
    R, C = part.shape

    def body(p_ref, out_ref, gath_ref, send_sems, recv_sems):
        me, cps = _broadcast_copies([p_ref], [gath_ref], send_sems, recv_sems)
        gath_ref[me] = p_ref[...]
        for cp in cps:
            cp.start()
        for cp in cps:
            cp.wait()
        acc = gath_ref[0]
        for d in range(1, N_DEV):
            acc = acc + gath_ref[d]
        out_ref[...] = acc

    vm = pl.BlockSpec(memory_space=pltpu.VMEM)
    return pl.pallas_call(
        body, name="all_reduce_small", in_specs=[vm], out_specs=vm, out_shape=S((R, C), F32),
        scratch_shapes=[pltpu.VMEM((N_DEV, R, C), F32), pltpu.SemaphoreType.DMA((1, N_DEV - 1)),
                        pltpu.SemaphoreType.DMA((1, N_DEV - 1))],
    )(part)


def _pack(parts):
    rows = []
    for p in parts:
        f = p.reshape(-1).astype(F32)
        pad = (-f.shape[0]) % (8 * LANES)
        rows.append(jnp.pad(f, (0, pad)).reshape(-1, LANES))
    return jnp.concatenate(rows, axis=0)


def _unpack(buf, shapes):
    out, r = [], 0
    for shp in shapes:
        n = 1
        for s in shp:
            n *= s
        nr = -(-n // (8 * LANES)) * 8
        out.append(buf[r:r + nr].reshape(-1)[:n].reshape(shp))
        r += nr
    return out


def kernel(x, norm_w, w_in, conv_w, a_log, dt_bias, head_norm_w, sgu_ln_w, sgu_ln_b, w_spatial, b_spatial, w_out, final_norm_w, loss_target, m_norm_w, m_w_in, m_conv_w, m_a_log, m_dt_bias, m_head_norm_w, m_sgu_ln_w, m_sgu_ln_b, m_w_spatial, m_b_spatial, m_w_out, m_final_norm_w, v_norm_w, v_w_in, v_conv_w, v_a_log, v_dt_bias, v_head_norm_w, v_sgu_ln_w, v_sgu_ln_b, v_w_spatial, v_b_spatial, v_w_out, v_final_norm_w):
    T, DM = x.shape[1], x.shape[2]
    H, D = a_log.shape[1], head_norm_w.shape[1]
    G, P = w_spatial.shape[1], w_spatial.shape[2]
    AW, BW = H * D, G * P
    MIX = AW + BW
    WD = w_in.shape[2]
    IN = N_DEV * WD
    RO = w_out.shape[1]
    CW = conv_w.shape[2]
    sizes = (3 * AW, AW, H, H, BW, BW, BW)
    assert sum(sizes) == IN and 2 * H <= LANES and 3 * H <= 32 and N_DEV * RO == MIX and N_DEV * CW == 3 * AW
    offs = [0]
    for s in sizes:
        offs.append(offs[-1] + s)
    px, py, pc = _position()
    dev = 4 * px + 2 * py + pc
    chip = 2 * px + py

    x2, tgt = x[0], loss_target[0]

    g_win = _all_gather_weights(_cast_bf16_t(w_in[0].T, "cast_w_in"))
    w_main, w_ba = _relayout_w(g_win, offs[2], offs[4])
    alog_row = jnp.pad(a_log, ((0, 0), (H, LANES - 2 * H)))
    dtb_row = jnp.pad(dt_bias, ((0, 0), (H, LANES - 2 * H)))
    bs_t = b_spatial[0].T

    proj, ba, xn, (g_wout, g_conv) = _in_proj(x2, norm_w, w_main, w_ba, [_cast_bf16(w_out[0], "cast_w_out"), conv_w[0]])
    w_out_full = g_wout.reshape(MIX, DM)
    conv_full = g_conv.transpose(1, 0, 2).reshape(4, 3 * AW)
    q, k, v, c, gcol, grow = _prep_a_fwd(proj, ba, conv_full, alog_row, dtb_row, H, D)
    o, vnew, ssave, asave = _delta_fwd(q, k, v, gcol, grow, H, D)
    ocat = _mix_fwd(o, proj, head_norm_w, sgu_ln_w, sgu_ln_b, w_spatial[0], bs_t, H, D, G, P)
    dh, dh_bf, d_ocat, loss_acc, g_fnw = _out_proj_loss(ocat, w_out_full, x2, tgt, final_norm_w.reshape(1, DM))
    loss = lax.psum(loss_acc[0, 0], AXES)

    core_idx = jnp.reshape(pc, (1,)).astype(jnp.int32)
    chip_idx = jnp.reshape(chip, (1,)).astype(jnp.int32)
    g_wout_blocks = _grad_w(ocat, dh_bf, "grad_w_out").reshape(4, 2, RO, DM)
    (d_o, dmain, g_hnw, g_ln, g_wsp, g_bs_t), (sib_wout,) = _mix_bwd(
        d_ocat, o, proj, head_norm_w, sgu_ln_w, sgu_ln_b, w_spatial[0], bs_t, H, D, G, P, [g_wout_blocks])
    chip_wout = _pair_sum(g_wout_blocks, sib_wout, core_idx, "pair_sum_w_out")
    (dq, dk, dv, dgate, dpar), (recv_wout,) = _delta_bwd(
        q, k, v, gcol, grow, ba, vnew, ssave, asave, d_o, a_log, dt_bias, H, D, [chip_wout])
    dc = _prep_a_bwd_pointwise(dq, dk, dv, c, H, D)
    dmain, g_conv_part = _conv_bwd(dc, proj, conv_full, dmain, 3 * AW)
    dba = dgate.astype(BF16)
    g_main, g_ba = _grad_w_in(xn, dmain, dba)
    g_win_blocks = _relayout_g(g_main, g_ba, WD, offs[2], offs[4]).reshape(4, 2, DM, WD)
    (sib_win,) = _exchange_sibling([g_win_blocks])
    chip_win = _pair_sum(g_win_blocks, sib_win, core_idx, "pair_sum_w_in")
    small_shapes = [a_log.shape, dt_bias.shape, head_norm_w.shape, sgu_ln_w.shape, sgu_ln_b.shape,
                    w_spatial.shape, b_spatial.shape, final_norm_w.shape]
    parts = [dpar[0, :H], dpar[0, H:2 * H], g_hnw[0], g_ln[0], g_ln[1], g_wsp, g_bs_t[:, :G].T, g_fnw[0],
             g_conv_part[:4]]
    grad_x, g_nw, small_gath, recv_win = _dx(dmain, dba, w_main, w_ba, x2, dh, norm_w, chip_win, _pack(parts))
    red = _sum_slots(small_gath)
    grad_w_in, delta_w_in, new_m_w_in, new_v_w_in = _sum_adam(
        chip_win, recv_win, w_in[0].T, m_w_in[0].T, v_w_in[0].T, chip_idx, "sum_adam_w_in", transposed=True)
    grad_w_out, delta_w_out, new_m_w_out, new_v_w_out = _sum_adam(
        chip_wout, recv_wout, w_out[0], m_w_out[0], v_w_out[0], chip_idx, "sum_adam_w_out")
    red_nw = _all_reduce_small(_pack([g_nw[0]]))
    grads_small = _unpack(red_nw, [norm_w.shape]) + _unpack(red, small_shapes + [(4, 3 * AW)])
    g_conv_full = grads_small.pop()
    grad_conv = lax.dynamic_slice_in_dim(g_conv_full, dev * CW, CW, axis=1)[None]
    small_w = [norm_w, a_log, dt_bias, head_norm_w, sgu_ln_w, sgu_ln_b, w_spatial, b_spatial, final_norm_w, conv_w]
    small_m = [m_norm_w, m_a_log, m_dt_bias, m_head_norm_w, m_sgu_ln_w, m_sgu_ln_b, m_w_spatial, m_b_spatial,
               m_final_norm_w, m_conv_w]
    small_v = [v_norm_w, v_a_log, v_dt_bias, v_head_norm_w, v_sgu_ln_w, v_sgu_ln_b, v_w_spatial, v_b_spatial,
               v_final_norm_w, v_conv_w]
    small_g = grads_small + [grad_conv]
    shapes10 = [w.shape for w in small_w]
    d_p, m_p, v_p = _adam_small(_pack(small_w), _pack(small_g), _pack(small_m), _pack(small_v))
    d_s, m_s, v_s = _unpack(d_p, shapes10), _unpack(m_p, shapes10), _unpack(v_p, shapes10)

    def order(small, win, wout):
        return [small[0], win.T[None], small[9], small[1], small[2], small[3], small[4], small[5], small[6], small[7],
                wout[None], small[8]]

    grads = order(small_g, grad_w_in, grad_w_out)
    deltas = order(d_s, delta_w_in, delta_w_out)
    new_m = order(m_s, new_m_w_in, new_m_w_out)
    new_v = order(v_s, new_v_w_in, new_v_w_out)
    return (loss, grad_x[None], *grads, *deltas, *new_m, *new_v)
```

```python
import functools

import jax
import jax.numpy as jnp
from jax import lax
from jax.experimental import pallas as pl
from jax.experimental.pallas import tpu as pltpu

F32 = jnp.float32
BF16 = jnp.bfloat16
MXU = jnp.bfloat16
HI = lax.Precision.HIGHEST
EPS = 1e-6
CHUNK_A = 64
LANES = 128
MESH = pl.DeviceIdType.MESH
AXES = ("x", "y", "c")
N_DEV = 8

ADAM_LR = 0.001
ADAM_B1 = 0.9
ADAM_B2 = 0.999
ADAM_EPS = 1e-08
ADAM_WD = 0.01
ADAM_STEP = 10

S = jax.ShapeDtypeStruct
ARB = "arbitrary"


def _cp(*sem):
    return pltpu.CompilerParams(dimension_semantics=tuple(sem), vmem_limit_bytes=56 * 1024 * 1024)


def _tile(n, cap, mult):
    best = None
    t = mult
    while t <= min(n, cap):
        if n % t == 0:
            best = t
        t += mult
    return best if best is not None else n


def _mm(a, b):
    return jnp.dot(a.astype(MXU), b.astype(MXU), preferred_element_type=F32)


def _mm_nt(a, b):
    return lax.dot_general(a.astype(MXU), b.astype(MXU), (((1,), (1,)), ((), ())), preferred_element_type=F32)


def _mm_tn(a, b):
    return lax.dot_general(a.astype(MXU), b.astype(MXU), (((0,), (0,)), ((), ())), preferred_element_type=F32)


def _mmh(a, b):
    return jnp.dot(a, b, precision=HI, preferred_element_type=F32)


def _mmh_tn(a, b):
    return lax.dot_general(a, b, (((0,), (0,)), ((), ())), precision=HI, preferred_element_type=F32)


def _sigmoid(x):
    return 1.0 / (1.0 + jnp.exp(-x))


def _silu(x):
    return x * _sigmoid(x)


def _dsilu(x):
    s = _sigmoid(x)
    return s * (1.0 + x * (1.0 - s))


def _softplus(x):
    return jnp.maximum(x, 0.0) + jnp.log(1.0 + jnp.exp(-jnp.abs(x)))


def _pieces(wd, gate_lo, gate_hi, total):
    out = []
    for d in range(N_DEV):
        lo, hi = d * wd, (d + 1) * wd
        for dest, a, b, shift in (("main", 0, gate_lo, 0), ("gate", gate_lo, gate_hi, -gate_lo),
                                  ("main", gate_hi, total, gate_lo - gate_hi)):
            s0, s1 = max(lo, a), min(hi, b)
            if s0 < s1:
                out.append((d, s0 - lo, s1 - lo, dest, s0 + shift))
    return out


def _cast_bf16(a, name):
    R, C = a.shape
    tr = _tile(R, 256, 16)

    def body(a_ref, o_ref):
        o_ref[...] = a_ref[...].astype(BF16)

    spec = pl.BlockSpec((tr, C), lambda i: (i, 0))
    return pl.pallas_call(body, name=name, grid=(R // tr,), in_specs=[spec], out_specs=spec,
                          out_shape=S((R, C), BF16), compiler_params=_cp(ARB))(a)


def _cast_bf16_t(a_t, name):
    C, R = a_t.shape
    tr = _tile(R, 256, LANES)

    def body(a_ref, o_ref):
        o_ref[...] = a_ref[...].T.astype(BF16)

    return pl.pallas_call(body, name=name, grid=(R // tr,), in_specs=[pl.BlockSpec((C, tr), lambda i: (0, i))],
                          out_specs=pl.BlockSpec((tr, C), lambda i: (i, 0)),
                          out_shape=S((R, C), BF16), compiler_params=_cp(ARB))(a_t)


def _relayout_w(g_win, gate_lo, gate_hi):
    _, DM, WD = g_win.shape
    total = N_DEV * WD
    NM = total - (gate_hi - gate_lo)
    tr = _tile(DM, 256, 16)
    plan = _pieces(WD, gate_lo, gate_hi, total)

    def body(g_ref, main_ref, gate_ref):
        gate_ref[...] = jnp.zeros_like(gate_ref)
        for d, s0, s1, dest, c0 in plan:
            dst = main_ref if dest == "main" else gate_ref
            dst[:, c0:c0 + (s1 - s0)] = g_ref[d, :, s0:s1]

    return pl.pallas_call(
        body, name="relayout_w", grid=(DM // tr,),
        in_specs=[pl.BlockSpec((N_DEV, tr, WD), lambda i: (0, i, 0))],
        out_specs=[pl.BlockSpec((tr, NM), lambda i: (i, 0)), pl.BlockSpec((tr, LANES), lambda i: (i, 0))],
        out_shape=[S((DM, NM), g_win.dtype), S((DM, LANES), g_win.dtype)],
        compiler_params=_cp(ARB),
    )(g_win)


def _relayout_g(g_main, g_gate, WD, gate_lo, gate_hi):
    DM = g_main.shape[0]
    total = N_DEV * WD
    tr = _tile(DM, 256, 16)
    plan = _pieces(WD, gate_lo, gate_hi, total)

    def body(m_ref, gate_ref, out_ref):
        for d, s0, s1, dest, c0 in plan:
            src = m_ref if dest == "main" else gate_ref
            out_ref[d, :, s0:s1] = src[:, c0:c0 + (s1 - s0)].astype(BF16)

    return pl.pallas_call(
        body, name="relayout_g", grid=(DM // tr,),
        in_specs=[pl.BlockSpec((tr, g_main.shape[1]), lambda i: (i, 0)), pl.BlockSpec((tr, LANES), lambda i: (i, 0))],
        out_specs=pl.BlockSpec((N_DEV, tr, WD), lambda i: (0, i, 0)),
        out_shape=S((N_DEV, DM, WD), BF16),
        compiler_params=_cp(ARB),
    )(g_main, g_gate)


def _in_proj(x, norm_w, w_main, w_ba, shards):
    T, DM = x.shape
    NM = w_main.shape[1]
    tm = _tile(T, 1024, 8)
    tn = _tile(NM, 1024, LANES)
    ni, nj = T // tm, NM // tn
    ns = len(shards)

    def body(x_ref, nw_ref, w_ref, wba_ref, *rest):
        srcs = rest[:ns]
        proj_ref, ba_ref, xn_ref = rest[ns:ns + 3]
        gath = rest[ns + 3:2 * ns + 3]
        send_sems, recv_sems, local_sems = rest[2 * ns + 3:]
        i = pl.program_id(0)
        me, cps = _broadcast_copies(srcs, gath, send_sems, recv_sems)
        cps = cps + [pltpu.make_async_copy(srcs[a], gath[a].at[me], local_sems.at[a]) for a in range(ns)]

        @pl.when((i == 0) & (pl.program_id(1) == 0))
        def _():
            for cp in cps:
                cp.start()

        @pl.when((i == ni - 1) & (pl.program_id(1) == nj - 1))
        def _():
            for cp in cps:
                cp.wait()

        @pl.when(pl.program_id(1) == 0)
        def _():
            xv = x_ref[...]
            r = lax.rsqrt(jnp.mean(xv * xv, axis=-1, keepdims=True) + EPS)
            xn = (xv * r * nw_ref[...]).astype(BF16)
            xn_ref[...] = xn
            ba_ref[...] = jnp.dot(xn.astype(MXU), wba_ref[...].astype(MXU), preferred_element_type=F32)

        proj_ref[...] = jnp.dot(xn_ref[...].astype(MXU), w_ref[...].astype(MXU), preferred_element_type=F32)

    any_spec = pl.BlockSpec(memory_space=pl.ANY)
    res = pl.pallas_call(
        body, name="in_proj", grid=(ni, nj),
        in_specs=[pl.BlockSpec((tm, DM), lambda i, j: (i, 0)),
                  pl.BlockSpec((1, DM), lambda i, j: (0, 0)),
                  pl.BlockSpec((DM, tn), lambda i, j: (0, j)),
                  pl.BlockSpec((DM, LANES), lambda i, j: (0, 0))] + [any_spec] * ns,
        out_specs=[pl.BlockSpec((tm, tn), lambda i, j: (i, j)),
                   pl.BlockSpec((tm, LANES), lambda i, j: (i, 0)),
                   pl.BlockSpec((tm, DM), lambda i, j: (i, 0))] + [any_spec] * ns,
        out_shape=[S((T, NM), F32), S((T, LANES), F32), S((T, DM), BF16)]
        + [S((N_DEV,) + a.shape, a.dtype) for a in shards],
        scratch_shapes=[pltpu.SemaphoreType.DMA((ns, N_DEV - 1)), pltpu.SemaphoreType.DMA((ns, N_DEV - 1)),
                        pltpu.SemaphoreType.DMA((ns,))],
        compiler_params=_cp(ARB, ARB),
    )(x, norm_w, w_main, w_ba, *shards)
    return res[0], res[1], res[2], res[3:]


def _prep_a_fwd(proj, ba, conv_w, alog_row, dtb_row, H, D):
    T = proj.shape[0]
    AW = H * D
    C3 = 3 * AW
    tb = _tile(T, 256, CHUNK_A)
    nch = tb // CHUNK_A
    nblk = T // tb
    scale = float(D) ** -0.5

    def body(x_ref, halo_ref, ba_ref, cw_ref, al_ref, dt_ref, q_ref, k_ref, v_ref, c_ref, gcol_ref, grow_ref):
        i = pl.program_id(0)
        xv = x_ref[...]
        halo = halo_ref[...] * (i > 0).astype(F32)
        xp = jnp.concatenate([halo, xv], axis=0)
        cw = cw_ref[...]
        c = cw[0:1, :] * xp[5:5 + tb]
        for j in range(1, 4):
            c = c + cw[j:j + 1, :] * xp[5 + j:5 + j + tb]
        c_ref[...] = c
        a = _silu(c)
        for h in range(H):
            qh = a[:, h * D:(h + 1) * D]
            kh = a[:, AW + h * D:AW + (h + 1) * D]
            qr = lax.rsqrt(jnp.sum(qh * qh, axis=-1, keepdims=True) + EPS)
            kr = lax.rsqrt(jnp.sum(kh * kh, axis=-1, keepdims=True) + EPS)
            q_ref[:, h * D:(h + 1) * D] = qh * (qr * scale)
            k_ref[:, h * D:(h + 1) * D] = kh * kr
        v_ref[...] = a[:, 2 * AW:]

        bav = ba_ref[...]
        lane = lax.broadcasted_iota(jnp.int32, (tb, LANES), 1)
        beta = _sigmoid(bav)
        g = -jnp.exp(al_ref[...]) * _softplus(bav + dt_ref[...])
        gates = jnp.where(lane < H, beta, jnp.where(lane < 2 * H, g, 0.0))
        ri = lax.broadcasted_iota(jnp.int32, (CHUNK_A, CHUNK_A), 0)
        ci = lax.broadcasted_iota(jnp.int32, (CHUNK_A, CHUNK_A), 1)
        tri = (ri >= ci).astype(F32)
        lane_c = lax.broadcasted_iota(jnp.int32, (CHUNK_A, LANES), 1)
        for cc in range(nch):
            gch = gates[cc * CHUNK_A:(cc + 1) * CHUNK_A]
            gc = pltpu.roll(_mmh(tri, gch), H, 1)
            full = jnp.where(lane_c < 2 * H, gch, jnp.where(lane_c < 3 * H, gc, 0.0))
            gcol_ref[cc * CHUNK_A:(cc + 1) * CHUNK_A, :] = full
            grow_ref[cc] = full.T[0:32, :]

    return pl.pallas_call(
        body, name="prep_a_fwd", grid=(nblk,),
        in_specs=[pl.BlockSpec((tb, C3), lambda i: (i, 0)),
                  pl.BlockSpec((8, C3), lambda i: (jnp.maximum(i * (tb // 8) - 1, 0), 0)),
                  pl.BlockSpec((tb, LANES), lambda i: (i, 0)),
                  pl.BlockSpec((4, C3), lambda i: (0, 0)),
                  pl.BlockSpec((1, LANES), lambda i: (0, 0)),
                  pl.BlockSpec((1, LANES), lambda i: (0, 0))],
        out_specs=[pl.BlockSpec((tb, AW), lambda i: (i, 0)),
                   pl.BlockSpec((tb, AW), lambda i: (i, 0)),
                   pl.BlockSpec((tb, AW), lambda i: (i, 0)),
                   pl.BlockSpec((tb, C3), lambda i: (i, 0)),
                   pl.BlockSpec((tb, LANES), lambda i: (i, 0)),
                   pl.BlockSpec((nch, 32, CHUNK_A), lambda i: (i, 0, 0))],
        out_shape=[S((T, AW), F32), S((T, AW), F32), S((T, AW), F32), S((T, C3), F32),
                   S((T, LANES), F32), S((T // CHUNK_A, 32, CHUNK_A), F32)],
        compiler_params=_cp(ARB),
    )(proj, proj, ba, conv_w, alog_row, dtb_row)


_NN = (((1,), (0,)), ((), ()))
_TN = (((0,), (0,)), ((), ()))


def _split(a):
    hi = a.astype(BF16)
    return hi, (a - hi.astype(F32)).astype(BF16)


def _mm3(a, b, dims=_NN):
    ah, al = a if isinstance(a, tuple) else _split(a)
    bh, bl = b if isinstance(b, tuple) else _split(b)
    dg = lambda p, r: lax.dot_general(p, r, dims, preferred_element_type=F32)
    return dg(ah, bh) + (dg(ah, bl) + dg(al, bh))


def _interleave(gens):
    gens = list(gens)
    while gens:
        alive = []
        for g in gens:
            try:
                next(g)
                alive.append(g)
            except StopIteration:
                pass
        gens = alive


def _chunk_terms(q, k, v, gcolv, growv, h, H):
    C = CHUNK_A
    beta_c = gcolv[:, h:h + 1]
    g_c = gcolv[:, H + h:H + h + 1]
    gc_c = gcolv[:, 2 * H + h:2 * H + h + 1]
    gc_r = growv[2 * H + h:2 * H + h + 1, :]
    ri = lax.broadcasted_iota(jnp.int32, (C, C), 0)
    ci = lax.broadcasted_iota(jnp.int32, (C, C), 1)
    incl = ri >= ci
    strict = ri > ci
    kb = k * beta_c
    vb = v * beta_c
    p_raw = _mm_nt(kb, k)
    qk_raw = _mm_nt(q, k)
    gam = jnp.where(incl, jnp.exp(jnp.where(incl, gc_c - gc_r, 0.0)), 0.0)
    e_c = jnp.exp(gc_c)
    gl = gc_r[:, C - 1:C]
    edec = jnp.exp(gl - gc_c)
    yield
    lmat = jnp.where(strict, p_raw * gam, 0.0)
    attn = jnp.where(incl, qk_raw * gam, 0.0)
    return dict(beta_c=beta_c, g_c=g_c, gc_c=gc_c, gc_r=gc_r, incl=incl, strict=strict, gam=gam, e_c=e_c,
                kb=kb, vb=vb, lmat=lmat, attn=attn, gl=gl, edec=edec, ri=ri, ci=ci)


def _inv_unit_lower(lmat):
    C = lmat.shape[0]
    ri = lax.broadcasted_iota(jnp.int32, (C, C), 0)
    ci = lax.broadcasted_iota(jnp.int32, (C, C), 1)
    eye = (ri == ci).astype(F32)
    x = -lmat
    a = eye + x
    n = 1
    while 2 * n < C:
        xs = _split(x)
        x = _mm3(xs, xs)
        yield
        a = a + _mm3(a, x)
        n *= 2
    yield
    return a


def _delta_fwd(q, k, v, gcol, grow, H, D):
    T = q.shape[0]
    C = CHUNK_A
    N = T // C
    AW = H * D

    def body(q_ref, k_ref, v_ref, gcol_ref, grow_ref, o_ref, vn_ref, ssave_ref, asave_ref, s_ref):
        @pl.when(pl.program_id(0) == 0)
        def _():
            s_ref[...] = jnp.zeros_like(s_ref)

        gcolv = gcol_ref[...]
        growv = grow_ref[0]

        def head(h):
            sl = slice(h * D, (h + 1) * D)
            st = s_ref[h]
            ssave_ref[0, h] = st
            qv, kv, vv = q_ref[:, sl], k_ref[:, sl], v_ref[:, sl]
            t = yield from _chunk_terms(qv, kv, vv, gcolv, growv, h, H)
            ks = _mm(t["kb"] * t["e_c"], st)
            o_inter = _mm(qv * t["e_c"], st)
            a = yield from _inv_unit_lower(t["lmat"])
            asave_ref[0, h] = a
            v_new = _mm3(a, t["vb"] - ks)
            yield
            vn_ref[:, sl] = v_new
            o_intra = _mm(t["attn"], v_new)
            s_upd = _mm_tn(kv * t["edec"], v_new)
            yield
            o_ref[:, sl] = o_inter + o_intra
            s_ref[h] = st * jnp.exp(t["gl"]) + s_upd

        _interleave(head(h) for h in range(H))

    blk = lambda: pl.BlockSpec((C, AW), lambda n: (n, 0))
    return pl.pallas_call(
        body, name="delta_fwd", grid=(N,),
        in_specs=[blk(), blk(), blk(),
                  pl.BlockSpec((C, LANES), lambda n: (n, 0)),
                  pl.BlockSpec((1, 32, C), lambda n: (n, 0, 0))],
        out_specs=[blk(), blk(),
                   pl.BlockSpec((1, H, D, D), lambda n: (n, 0, 0, 0)),
                   pl.BlockSpec((1, H, C, C), lambda n: (n, 0, 0, 0))],
        out_shape=[S((T, AW), F32), S((T, AW), F32), S((N, H, D, D), F32), S((N, H, C, C), F32)],
        scratch_shapes=[pltpu.VMEM((H, D, D), F32)],
        compiler_params=_cp(ARB),
    )(q, k, v, gcol, grow)


def _delta_bwd(q, k, v, gcol, grow, ba, vnew, ssave, asave, d_o, a_log, dt_bias, H, D, carry):
    T = q.shape[0]
    C = CHUNK_A
    N = T // C
    AW = H * D
    nc = len(carry)

    def body(al_ref, dt_ref, q_ref, k_ref, v_ref, gcol_ref, grow_ref, ba_ref, vn_ref, ss_ref, as_ref, do_ref, *rest):
        cins = rest[:nc]
        dq_ref, dk_ref, dv_ref, dgate_ref, dpar_ref = rest[nc:nc + 5]
        couts = rest[nc + 5:2 * nc + 5]
        ds_ref, csend, crecv = rest[2 * nc + 5:]
        ccps = _chip_exchange_copies(cins, couts, csend, crecv)

        @pl.when(pl.program_id(0) == 0)
        def _():
            ds_ref[...] = jnp.zeros_like(ds_ref)
            dpar_ref[...] = jnp.zeros_like(dpar_ref)
            for cp in ccps:
                cp.start()

        gcolv = gcol_ref[...]
        growv = grow_ref[0]
        bav = ba_ref[...]
        lane = lax.broadcasted_iota(jnp.int32, (C, LANES), 1)
        lane1 = lax.broadcasted_iota(jnp.int32, (1, LANES), 1)
        rowi = lax.broadcasted_iota(jnp.int32, (C, 1), 0)
        acc = {"dgate": jnp.zeros((C, LANES), F32), "dpar": jnp.zeros((1, LANES), F32)}

        def head(h):
            sl = slice(h * D, (h + 1) * D)
            ds_next = ds_ref[h]
            st = ss_ref[0, h]
            a = as_ref[0, h]
            qv, kv, vv, dov, v_new = q_ref[:, sl], k_ref[:, sl], v_ref[:, sl], do_ref[:, sl], vn_ref[:, sl]
            t = yield from _chunk_terms(qv, kv, vv, gcolv, growv, h, H)
            beta_c, e_c, gam, kb = t["beta_c"], t["e_c"], t["gam"], t["kb"]
            incl, strict, attn, lmat, edec = t["incl"], t["strict"], t["attn"], t["lmat"], t["edec"]
            kdec = kv * edec
            egl = jnp.exp(t["gl"])
            qe = qv * e_c
            ekb = kb * e_c

            dkdec = _mm_nt(v_new, ds_next)
            dv_new_s = _mm(kdec, ds_next)
            t1 = _mm_nt(dov, st)
            ds_o = _mm_tn(qe, dov)
            dattn_raw = _mm_nt(dov, v_new)
            dv_new_o = _mm_tn(attn, dov)
            yield
            dgl = egl * jnp.sum(jnp.sum(st * ds_next, axis=1, keepdims=True), axis=0, keepdims=True)
            dk = edec * dkdec
            r = jnp.sum(dkdec * kdec, axis=1, keepdims=True)
            dgc = -r
            dgl = dgl + jnp.sum(r, axis=0, keepdims=True)
            dq = e_c * t1
            dgc = dgc + jnp.sum(t1 * qe, axis=1, keepdims=True)
            dattn = jnp.where(incl, dattn_raw, 0.0)
            dv_new = dv_new_s + dv_new_o
            dqm = dattn * gam
            z = dattn * attn
            dvb = _mm3(a, dv_new, _TN)
            dq_a = _mm(dqm, kv)
            dk_a = _mm_tn(dqm, qv)
            yield
            dq_ref[:, sl] = dq + dq_a
            dv_ref[:, sl] = beta_c * dvb
            ds_kb = _mm_tn(ekb, dvb)
            dekb_neg = _mm_nt(dvb, st)
            dl_neg = _mm_nt(dvb, v_new)
            yield
            ds_ref[h] = egl * ds_next + ds_o - ds_kb
            dekb = -dekb_neg
            dl = jnp.where(strict, -dl_neg, 0.0)
            dp = dl * gam
            z = z + dl * lmat
            dkb_p = _mm(dp, kv)
            dk_p = _mm_tn(dp, kb)
            dgc = dgc + jnp.sum(dekb * ekb, axis=1, keepdims=True)
            dgc = dgc + jnp.sum(z, axis=1, keepdims=True) - jnp.sum(z.T, axis=1, keepdims=True)
            dgc = dgc + jnp.where(rowi == C - 1, dgl, 0.0)
            upper = (t["ri"] <= t["ci"]).astype(F32)
            dg_b = _mm3(upper, jnp.broadcast_to(dgc, (C, LANES)))
            yield
            dkb = dkb_p + e_c * dekb
            dk_ref[:, sl] = dk + dk_a + dk_p + beta_c * dkb
            dbeta = jnp.sum(dkb * kv, axis=1, keepdims=True) + jnp.sum(dvb * vv, axis=1, keepdims=True)
            dg = dg_b[:, 0:1]
            a_raw = bav[:, H + h:H + h + 1]
            d_braw = dbeta * beta_c * (1.0 - beta_c)
            d_araw = dg * (-jnp.exp(al_ref[0, h])) * _sigmoid(a_raw + dt_ref[0, h])
            acc["dgate"] = acc["dgate"] + jnp.where(lane == h, d_braw, 0.0) + jnp.where(lane == H + h, d_araw, 0.0)
            dal = jnp.sum(dg * t["g_c"], axis=0, keepdims=True)
            ddt = jnp.sum(d_araw, axis=0, keepdims=True)
            acc["dpar"] = acc["dpar"] + jnp.where(lane1 == h, dal, 0.0) + jnp.where(lane1 == H + h, ddt, 0.0)

        _interleave(head(h) for h in range(H))
        dgate_ref[...] = acc["dgate"]
        dpar_ref[0:1, :] += acc["dpar"]

        @pl.when(pl.program_id(0) == N - 1)
        def _():
            for cp in ccps:
                cp.wait()

    rev = lambda s: N - 1 - s
    blk = lambda: pl.BlockSpec((C, AW), lambda s: (rev(s), 0))
    smem = pl.BlockSpec(memory_space=pltpu.SMEM)
    any_spec = pl.BlockSpec(memory_space=pl.ANY)
    res = pl.pallas_call(
        body, name="delta_bwd", grid=(N,),
        in_specs=[smem, smem, blk(), blk(), blk(),
                  pl.BlockSpec((C, LANES), lambda s: (rev(s), 0)),
                  pl.BlockSpec((1, 32, C), lambda s: (rev(s), 0, 0)),
                  pl.BlockSpec((C, LANES), lambda s: (rev(s), 0)),
                  blk(),
                  pl.BlockSpec((1, H, D, D), lambda s: (rev(s), 0, 0, 0)),
                  pl.BlockSpec((1, H, C, C), lambda s: (rev(s), 0, 0, 0)),
                  blk()] + [any_spec] * nc,
        out_specs=[blk(), blk(), blk(),
                   pl.BlockSpec((C, LANES), lambda s: (rev(s), 0)),
                   pl.BlockSpec((8, LANES), lambda s: (0, 0))] + [any_spec] * nc,
        out_shape=[S((T, AW), F32), S((T, AW), F32), S((T, AW), F32),
                   S((T, LANES), F32), S((8, LANES), F32)] + [S((3,) + a.shape[1:], a.dtype) for a in carry],
        scratch_shapes=[pltpu.VMEM((H, D, D), F32),
                        pltpu.SemaphoreType.DMA((max(nc, 1), 3)), pltpu.SemaphoreType.DMA((max(nc, 1), 3))],
        compiler_params=_cp(ARB),
    )(a_log, dt_bias, q, k, v, gcol, grow, ba, vnew, ssave, asave, d_o, *carry)
    return res[:5], res[5:]


def _ln_stats(xv):
    mu = jnp.mean(xv, axis=-1, keepdims=True)
    xc = xv - mu
    var = jnp.mean(xc * xc, axis=-1, keepdims=True)
    rstd = lax.rsqrt(var + EPS)
    return xc * rstd, rstd


def _mix_fwd(o, proj, head_norm_w, ln_w, ln_b, w_sp, bs_t, H, D, G, P):
    T = o.shape[0]
    AW, BW = H * D, G * P
    MIX = AW + BW
    nb = AW // BW if AW % BW == 0 else None
    assert nb == 1, "group widths must match the projection column blocks"
    cb = 3

    def body(o_ref, za_ref, ub_ref, vb_ref, zb_ref, hw_ref, lw_ref, lb_ref, w_ref, bs_ref, out_ref):
        hw = hw_ref[...]
        for h in range(H):
            sl = slice(h * D, (h + 1) * D)
            oh = o_ref[:, sl]
            rs = lax.rsqrt(jnp.mean(oh * oh, axis=-1, keepdims=True) + EPS)
            out_ref[:, sl] = (oh * rs * hw * _silu(za_ref[:, sl])).astype(BF16)
        xhat, _ = _ln_stats(vb_ref[...])
        vn = xhat * lw_ref[...] + lb_ref[...]
        ri = lax.broadcasted_iota(jnp.int32, (P, P), 0)
        ci = lax.broadcasted_iota(jnp.int32, (P, P), 1)
        bsv = bs_ref[...]
        for g in range(G):
            sl = slice(g * P, (g + 1) * P)
            wm = jnp.where(ri >= ci, w_ref[g], 0.0)
            s = _mm(wm, vn[:, sl]) + bsv[:, g:g + 1]
            out_ref[:, AW + g * P:AW + (g + 1) * P] = (ub_ref[:, sl] * s * _silu(zb_ref[:, sl])).astype(BF16)

    row = lambda w: pl.BlockSpec((1, w), lambda i: (0, 0))
    return pl.pallas_call(
        body, name="mix_fwd", grid=(T // P,),
        in_specs=[pl.BlockSpec((P, AW), lambda i: (i, 0)),
                  pl.BlockSpec((P, AW), lambda i: (i, cb)),
                  pl.BlockSpec((P, BW), lambda i: (i, cb + 1)),
                  pl.BlockSpec((P, BW), lambda i: (i, cb + 2)),
                  pl.BlockSpec((P, BW), lambda i: (i, cb + 3)),
                  row(D), row(BW), row(BW),
                  pl.BlockSpec((G, P, P), lambda i: (0, 0, 0)),
                  pl.BlockSpec((P, G), lambda i: (0, 0))],
        out_specs=pl.BlockSpec((P, MIX), lambda i: (i, 0)),
        out_shape=S((T, MIX), BF16),
        compiler_params=_cp(ARB),
    )(o, proj, proj, proj, proj, head_norm_w, ln_w, ln_b, w_sp, bs_t)


def _mix_bwd(d_ocat, o, proj, head_norm_w, ln_w, ln_b, w_sp, bs_t, H, D, G, P, carry):
    T = o.shape[0]
    AW, BW = H * D, G * P
    MIX = AW + BW
    cb = 3
    nc = len(carry)

    def body(dc_ref, o_ref, za_ref, ub_ref, vb_ref, zb_ref, hw_ref, lw_ref, lb_ref, w_ref, bs_ref, *rest):
        cins = rest[:nc]
        do_ref, dmain_ref, dhw_ref, dln_ref, dw_ref, dbs_ref = rest[nc:nc + 6]
        couts = rest[nc + 6:2 * nc + 6]
        dvn_ref, drest_ref, out_sems, csend, crecv = rest[2 * nc + 6:]
        i = pl.program_id(0)
        slot = lax.rem(i, 2)
        ccps = _sibling_copies(cins, couts, csend, crecv)

        def out_copy(step, s):
            return pltpu.make_async_copy(
                drest_ref.at[s], dmain_ref.at[pl.ds(step * P, P), pl.ds(cb * AW, AW + 3 * BW)], out_sems.at[s])

        @pl.when(i == 0)
        def _():
            dhw_ref[...] = jnp.zeros_like(dhw_ref)
            dln_ref[...] = jnp.zeros_like(dln_ref)
            dw_ref[...] = jnp.zeros_like(dw_ref)
            dbs_ref[...] = jnp.zeros_like(dbs_ref)
            for cp in ccps:
                cp.start()

        @pl.when(i >= 2)
        def _():
            out_copy(i - 2, slot).wait()

        hw = hw_ref[...]
        dhw = jnp.zeros((1, D), F32)
        for h in range(H):
            sl = slice(h * D, (h + 1) * D)
            oh = o_ref[:, sl]
            za = za_ref[:, sl]
            doa = dc_ref[:, sl]
            rs = lax.rsqrt(jnp.mean(oh * oh, axis=-1, keepdims=True) + EPS)
            xh = oh * rs
            d_on = doa * _silu(za)
            drest_ref[slot, :, sl] = (doa * (xh * hw) * _dsilu(za)).astype(BF16)
            dhw = dhw + jnp.sum(d_on * xh, axis=0, keepdims=True)
            dxh = d_on * hw
            do_ref[:, sl] = rs * (dxh - xh * jnp.mean(dxh * xh, axis=-1, keepdims=True))
        dhw_ref[0:1, :] += dhw

        xhat, rstd = _ln_stats(vb_ref[...])
        lw = lw_ref[...]
        vn = xhat * lw + lb_ref[...]
        ri = lax.broadcasted_iota(jnp.int32, (P, P), 0)
        ci = lax.broadcasted_iota(jnp.int32, (P, P), 1)
        lane = lax.broadcasted_iota(jnp.int32, (P, LANES), 1)
        bsv = bs_ref[...]
        dbs = jnp.zeros((P, LANES), F32)
        for g in range(G):
            sl = slice(g * P, (g + 1) * P)
            wm = jnp.where(ri >= ci, w_ref[g], 0.0)
            vng = vn[:, sl]
            s = _mm(wm, vng) + bsv[:, g:g + 1]
            dob = dc_ref[:, AW + g * P:AW + (g + 1) * P]
            ub = ub_ref[:, sl]
            zb = zb_ref[:, sl]
            szb = _silu(zb)
            drest_ref[slot, :, AW + g * P:AW + (g + 1) * P] = (dob * s * szb).astype(BF16)
            drest_ref[slot, :, AW + 2 * BW + g * P:AW + 2 * BW + (g + 1) * P] = (
                dob * ub * s * _dsilu(zb)).astype(BF16)
            ds = dob * ub * szb
            dvn_ref[:, sl] = _mm_tn(wm, ds)
            dw_ref[g] += jnp.where(ri >= ci, _mm_nt(ds, vng), 0.0)
            dbs = dbs + jnp.where(lane == g, jnp.sum(ds, axis=1, keepdims=True), 0.0)
        dbs_ref[...] += dbs
        dvn = dvn_ref[...]
        dln_ref[0:1, :] += jnp.sum(dvn * xhat, axis=0, keepdims=True)
        dln_ref[1:2, :] += jnp.sum(dvn, axis=0, keepdims=True)
        dxh = dvn * lw
        dvb = rstd * (dxh - jnp.mean(dxh, axis=-1, keepdims=True) - xhat * jnp.mean(dxh * xhat, axis=-1, keepdims=True))
        drest_ref[slot, :, AW + BW:AW + 2 * BW] = dvb.astype(BF16)

        out_copy(i, slot).start()

        @pl.when(i == nstep - 1)
        def _():
            out_copy(i, slot).wait()
            if nstep > 1:
                out_copy(i - 1, 1 - slot).wait()
            for cp in ccps:
                cp.wait()

    nstep = T // P
    row = lambda w: pl.BlockSpec((1, w), lambda i: (0, 0))
    any_spec = pl.BlockSpec(memory_space=pl.ANY)
    res = pl.pallas_call(
        body, name="mix_bwd", grid=(nstep,),
        in_specs=[pl.BlockSpec((P, MIX), lambda i: (i, 0)),
                  pl.BlockSpec((P, AW), lambda i: (i, 0)),
                  pl.BlockSpec((P, AW), lambda i: (i, cb)),
                  pl.BlockSpec((P, BW), lambda i: (i, cb + 1)),
                  pl.BlockSpec((P, BW), lambda i: (i, cb + 2)),
                  pl.BlockSpec((P, BW), lambda i: (i, cb + 3)),
                  row(D), row(BW), row(BW),
                  pl.BlockSpec((G, P, P), lambda i: (0, 0, 0)),
                  pl.BlockSpec((P, G), lambda i: (0, 0))] + [any_spec] * nc,
        out_specs=[pl.BlockSpec((P, AW), lambda i: (i, 0)),
                   any_spec,
                   pl.BlockSpec((8, D), lambda i: (0, 0)),
                   pl.BlockSpec((8, BW), lambda i: (0, 0)),
                   pl.BlockSpec((G, P, P), lambda i: (0, 0, 0)),
                   pl.BlockSpec((P, LANES), lambda i: (0, 0))] + [any_spec] * nc,
        out_shape=[S((T, AW), F32), S((T, cb * AW + AW + 3 * BW), BF16), S((8, D), F32), S((8, BW), F32),
                   S((G, P, P), F32), S((P, LANES), F32)] + [S(a.shape[:1] + a.shape[2:], a.dtype) for a in carry],
        scratch_shapes=[pltpu.VMEM((P, BW), F32), pltpu.VMEM((2, P, AW + 3 * BW), BF16),
                        pltpu.SemaphoreType.DMA((2,))] + _sibling_sems(carry),
        compiler_params=_cp(ARB),
    )(d_ocat, o, proj, proj, proj, proj, head_norm_w, ln_w, ln_b, w_sp, bs_t, *carry)
    return res[:6], res[6:]


def _out_proj_loss(ocat, w_out, x, target, fnw):
    T, MIX = ocat.shape
    DM = x.shape[1]
    tm = _tile(T, 256, 8)

    def body(oc_ref, w_ref, x_ref, t_ref, fw_ref, dh_ref, dhb_ref, doc_ref, loss_ref, gfw_ref):
        @pl.when(pl.program_id(0) == 0)
        def _():
            loss_ref[...] = jnp.zeros_like(loss_ref)
            gfw_ref[...] = jnp.zeros_like(gfw_ref)

        wv = w_ref[...]
        hh = x_ref[...] + jnp.dot(oc_ref[...].astype(MXU), wv.astype(MXU), preferred_element_type=F32)
        rs = lax.rsqrt(jnp.mean(hh * hh, axis=-1, keepdims=True) + EPS)
        hn = hh * rs
        fw = fw_ref[...]
        e = hn * fw - t_ref[...]
        row_loss = 0.5 * jnp.mean(e * e, axis=-1, keepdims=True)
        loss_ref[...] += jnp.sum(row_loss, axis=0, keepdims=True)
        dy = e * (1.0 / DM)
        gfw_ref[0:1, :] += jnp.sum(dy * hn, axis=0, keepdims=True)
        dhn = dy * fw
        dh = rs * (dhn - hn * jnp.mean(dhn * hn, axis=-1, keepdims=True))
        dh_ref[...] = dh
        dhb = dh.astype(BF16)
        dhb_ref[...] = dhb
        doc_ref[...] = _mm_nt(dhb, wv)

    return pl.pallas_call(
        body, name="out_proj_loss", grid=(T // tm,),
        in_specs=[pl.BlockSpec((tm, MIX), lambda i: (i, 0)),
                  pl.BlockSpec((MIX, DM), lambda i: (0, 0)),
                  pl.BlockSpec((tm, DM), lambda i: (i, 0)),
                  pl.BlockSpec((tm, DM), lambda i: (i, 0)),
                  pl.BlockSpec((1, DM), lambda i: (0, 0))],
        out_specs=[pl.BlockSpec((tm, DM), lambda i: (i, 0)),
                   pl.BlockSpec((tm, DM), lambda i: (i, 0)),
                   pl.BlockSpec((tm, MIX), lambda i: (i, 0)),
                   pl.BlockSpec((8, LANES), lambda i: (0, 0)),
                   pl.BlockSpec((8, DM), lambda i: (0, 0))],
        out_shape=[S((T, DM), F32), S((T, DM), BF16), S((T, MIX), F32), S((8, LANES), F32), S((8, DM), F32)],
        compiler_params=_cp(ARB),
    )(ocat, w_out, x, target, fnw)


def _grad_w(lhs, rhs, name):
    T, A = lhs.shape
    B = rhs.shape[1]
    ta = _tile(A, 512, LANES)
    tk = _tile(T, 1024, 16)
    nk = T // tk

    def body(l_ref, r_ref, out_ref, acc_ref):
        k = pl.program_id(1)
        part = _mm_tn(l_ref[...], r_ref[...])

        @pl.when(k == 0)
        def _():
            acc_ref[...] = part

        @pl.when(k > 0)
        def _():
            acc_ref[...] += part

        @pl.when(k == nk - 1)
        def _():
            out_ref[...] = acc_ref[...].astype(BF16)

    return pl.pallas_call(
        body, name=name, grid=(A // ta, nk),
        in_specs=[pl.BlockSpec((tk, ta), lambda i, k: (k, i)),
                  pl.BlockSpec((tk, B), lambda i, k: (k, 0))],
        out_specs=pl.BlockSpec((ta, B), lambda i, k: (i, 0)),
        out_shape=S((A, B), BF16),
        scratch_shapes=[pltpu.VMEM((ta, B), F32)],
        compiler_params=_cp(ARB, ARB),
    )(lhs, rhs)


def _grad_w_in(xn, dmain, dba):
    T, DM = xn.shape
    NM = dmain.shape[1]
    tn = _tile(NM, 1024, LANES)
    tk = _tile(T, 2048, 16)

    def body(xn_ref, dm_ref, dba_ref, gm_ref, gba_ref):
        j = pl.program_id(0)
        k = pl.program_id(1)

        @pl.when(k == 0)
        def _():
            gm_ref[...] = jnp.zeros_like(gm_ref)

        @pl.when((k == 0) & (j == 0))
        def _():
            gba_ref[...] = jnp.zeros_like(gba_ref)

        xv = xn_ref[...]
        gm_ref[...] += _mm_tn(xv, dm_ref[...])

        @pl.when(j == 0)
        def _():
            gba_ref[...] += _mm_tn(xv, dba_ref[...])

    return pl.pallas_call(
        body, name="grad_w_in", grid=(NM // tn, T // tk),
        in_specs=[pl.BlockSpec((tk, DM), lambda j, k: (k, 0)),
                  pl.BlockSpec((tk, tn), lambda j, k: (k, j)),
                  pl.BlockSpec((tk, LANES), lambda j, k: (k, 0))],
        out_specs=[pl.BlockSpec((DM, tn), lambda j, k: (0, j)),
                   pl.BlockSpec((DM, LANES), lambda j, k: (0, 0))],
        out_shape=[S((DM, NM), F32), S((DM, LANES), F32)],
        compiler_params=_cp(ARB, ARB),
    )(xn, dmain, dba)


def _dx(dmain, dba, w_main, w_ba, x, dh, norm_w, chip_sum, small):
    T, NM = dmain.shape
    DM = x.shape[1]
    tm = _tile(T, 512, 8)
    tk = _tile(NM, 1024, LANES)
    nk = NM // tk
    ni = T // tm
    half = chip_sum.shape[1] // 2
    assert half % 16 == 0
    last_step = ni * nk - 1
    relay_step = min(2 * nk, last_step)

    def body(dm_ref, dba_ref, w_ref, wba_ref, x_ref, dh_ref, nw_ref, small_ref, cs_ref,
             gx_ref, gnw_ref, gath_ref, recv_ref, stage_ref, acc_ref, csend, crecv, ssend, srecv, lsem):
        i = pl.program_id(0)
        k = pl.program_id(1)
        step = i * nk + k
        px, py, pc = _position()
        xn, yn = (1 - px, py, pc), (px, 1 - py, pc)
        upper, lower = pl.ds(0, half), pl.ds(half, half)

        def rcopy(kk, src, dst, to):
            return pltpu.make_async_remote_copy(src_ref=src, dst_ref=dst, send_sem=csend.at[kk], recv_sem=crecv.at[kk],
                                                device_id=to, device_id_type=MESH)

        diag_blk = cs_ref.at[2 * (1 - px) + (1 - py)]
        to_stage = [rcopy(2, diag_blk.at[upper], stage_ref.at[0], xn), rcopy(3, diag_blk.at[lower], stage_ref.at[1], yn)]
        direct = [rcopy(0, cs_ref.at[2 * (1 - px) + py], recv_ref.at[0], xn),
                  rcopy(1, cs_ref.at[2 * px + (1 - py)], recv_ref.at[1], yn)]
        onward = [rcopy(4, stage_ref.at[0], recv_ref.at[2].at[upper], yn),
                  rcopy(5, stage_ref.at[1], recv_ref.at[2].at[lower], xn)]
        me, small_cps = _broadcast_copies([small_ref], [gath_ref], ssend, srecv)
        small_cps = small_cps + [pltpu.make_async_copy(small_ref, gath_ref.at[me], lsem.at[0])]

        @pl.when(step == 0)
        def _():
            gnw_ref[...] = jnp.zeros_like(gnw_ref)
            for cp in to_stage + direct + small_cps:
                cp.start()

        @pl.when(step == relay_step)
        def _():
            for cp in to_stage:
                cp.wait_recv()
            for cp in onward:
                cp.start()

        @pl.when(k == 0)
        def _():
            acc_ref[...] = _mm_nt(dba_ref[...], wba_ref[...])

        acc_ref[...] += _mm_nt(dm_ref[...], w_ref[...])

        @pl.when(k == nk - 1)
        def _():
            xv = x_ref[...]
            rs = lax.rsqrt(jnp.mean(xv * xv, axis=-1, keepdims=True) + EPS)
            xh = xv * rs
            dxn = acc_ref[...]
            gnw_ref[0:1, :] += jnp.sum(dxn * xh, axis=0, keepdims=True)
            dxh = dxn * nw_ref[...]
            gx_ref[...] = dh_ref[...] + rs * (dxh - xh * jnp.mean(dxh * xh, axis=-1, keepdims=True))

        @pl.when(step == last_step)
        def _():
            for cp in to_stage:
                cp.wait_send()
            for cp in direct + onward + small_cps:
                cp.wait()

    any_spec = pl.BlockSpec(memory_space=pl.ANY)
    res = pl.pallas_call(
        body, name="dx", grid=(ni, nk),
        in_specs=[pl.BlockSpec((tm, tk), lambda i, k: (i, k)),
                  pl.BlockSpec((tm, LANES), lambda i, k: (i, 0)),
                  pl.BlockSpec((DM, tk), lambda i, k: (0, k)),
                  pl.BlockSpec((DM, LANES), lambda i, k: (0, 0)),
                  pl.BlockSpec((tm, DM), lambda i, k: (i, 0)),
                  pl.BlockSpec((tm, DM), lambda i, k: (i, 0)),
                  pl.BlockSpec((1, DM), lambda i, k: (0, 0)),
                  any_spec, any_spec],
        out_specs=[pl.BlockSpec((tm, DM), lambda i, k: (i, 0)),
                   pl.BlockSpec((8, DM), lambda i, k: (0, 0)),
                   any_spec, any_spec, any_spec],
        out_shape=[S((T, DM), F32), S((8, DM), F32), S((N_DEV,) + small.shape, F32),
                   S((3,) + chip_sum.shape[1:], chip_sum.dtype), S((2, half) + chip_sum.shape[2:], chip_sum.dtype)],
        scratch_shapes=[pltpu.VMEM((tm, DM), F32),
                        pltpu.SemaphoreType.DMA((6,)), pltpu.SemaphoreType.DMA((6,)),
                        pltpu.SemaphoreType.DMA((1, N_DEV - 1)), pltpu.SemaphoreType.DMA((1, N_DEV - 1)),
                        pltpu.SemaphoreType.DMA((1,))],
        compiler_params=_cp(ARB, ARB),
    )(dmain, dba, w_main, w_ba, x, dh, norm_w, small, chip_sum)
    return res[0], res[1], res[2], res[3]


def _sum_slots(gath):
    _, R, C = gath.shape
    tr = _tile(R, 512, 8)

    def body(g_ref, o_ref):
        tot = g_ref[0]
        for d in range(1, N_DEV):
            tot = tot + g_ref[d]
        o_ref[...] = tot

    return pl.pallas_call(
        body, name="sum_slots", grid=(R // tr,),
        in_specs=[pl.BlockSpec((N_DEV, tr, C), lambda i: (0, i, 0))],
        out_specs=pl.BlockSpec((tr, C), lambda i: (i, 0)),
        out_shape=S((R, C), F32), compiler_params=_cp(ARB),
    )(gath)


def _prep_a_bwd(dq, dk, dv, c, proj, conv_w, dmain, H, D):
    T = c.shape[0]
    AW = H * D
    C3 = 3 * AW
    tb = _tile(T, 256, 8)
    nblk = T // tb
    r8 = tb // 8
    scale = float(D) ** -0.5

    def body(dq_ref, dk_ref, dv_ref, c_ref, dqn_ref, dkn_ref, dvn_ref, cn_ref, x_ref, halo_ref, cw_ref, dmain_in_ref,
             dx_ref, gcw_ref, dc_ref):
        del dmain_in_ref
        i = pl.program_id(0)

        @pl.when(i == 0)
        def _():
            gcw_ref[...] = jnp.zeros_like(gcw_ref)

        def pointwise(rows, dq_r, dk_r, dv_r, c_r, keep):
            for h in range(H):
                for part, d_r, sc in ((0, dq_r, scale), (1, dk_r, 1.0)):
                    sl = slice(part * AW + h * D, part * AW + (h + 1) * D)
                    cv = c_r[:, sl]
                    raw = _silu(cv)
                    rs = lax.rsqrt(jnp.sum(raw * raw, axis=-1, keepdims=True) + EPS)
                    nrm = raw * rs
                    dn = d_r[:, h * D:(h + 1) * D] * sc
                    draw = rs * (dn - nrm * jnp.sum(dn * nrm, axis=-1, keepdims=True))
                    dc_ref[rows, sl] = draw * _dsilu(cv) * keep
            dc_ref[rows, 2 * AW:] = dv_r[...] * _dsilu(c_r[:, 2 * AW:]) * keep

        pointwise(slice(0, tb), dq_ref, dk_ref, dv_ref, c_ref, 1.0)
        pointwise(slice(tb, tb + 8), dqn_ref, dkn_ref, dvn_ref, cn_ref, (i < nblk - 1).astype(F32))

        cw = cw_ref[...]
        dcv = dc_ref[0:tb, :]
        dx = cw[3:4, :] * dcv
        for j in range(3):
            dx = dx + cw[j:j + 1, :] * dc_ref[3 - j:3 - j + tb, :]
        dx_ref[...] = dx.astype(BF16)
        halo = halo_ref[...] * (i > 0).astype(F32)
        xp = jnp.concatenate([halo, x_ref[...]], axis=0)
        for j in range(4):
            gcw_ref[j:j + 1, :] += jnp.sum(dcv * xp[5 + j:5 + j + tb], axis=0, keepdims=True)

    nxt = lambda i: (jnp.minimum((i + 1) * r8, T // 8 - 1), 0)
    return pl.pallas_call(
        body, name="prep_a_bwd", grid=(nblk,),
        in_specs=[pl.BlockSpec((tb, AW), lambda i: (i, 0)),
                  pl.BlockSpec((tb, AW), lambda i: (i, 0)),
                  pl.BlockSpec((tb, AW), lambda i: (i, 0)),
                  pl.BlockSpec((tb, C3), lambda i: (i, 0)),
                  pl.BlockSpec((8, AW), nxt), pl.BlockSpec((8, AW), nxt), pl.BlockSpec((8, AW), nxt),
                  pl.BlockSpec((8, C3), nxt),
                  pl.BlockSpec((tb, C3), lambda i: (i, 0)),
                  pl.BlockSpec((8, C3), lambda i: (jnp.maximum(i * r8 - 1, 0), 0)),
                  pl.BlockSpec((4, C3), lambda i: (0, 0)),
                  pl.BlockSpec(memory_space=pl.ANY)],
        out_specs=[pl.BlockSpec((tb, C3), lambda i: (i, 0)),
                   pl.BlockSpec((8, C3), lambda i: (0, 0))],
        out_shape=[S(dmain.shape, dmain.dtype), S((8, C3), F32)],
        scratch_shapes=[pltpu.VMEM((tb + 8, C3), F32)],
        input_output_aliases={11: 0},
        compiler_params=_cp(ARB),
    )(dq, dk, dv, c, dq, dk, dv, c, proj, proj, conv_w, dmain)


def _adam_math(w, g, m, v):
    m2 = ADAM_B1 * m + (1.0 - ADAM_B1) * g
    v2 = ADAM_B2 * v + (1.0 - ADAM_B2) * (g * g)
    m_hat = m2 / (1.0 - ADAM_B1 ** ADAM_STEP)
    v_hat = v2 / (1.0 - ADAM_B2 ** ADAM_STEP)
    delta = -ADAM_LR * (m_hat / (jnp.sqrt(v_hat) + ADAM_EPS) + ADAM_WD * w)
    return delta, m2, v2


def _pair_sum(blocks, recv, core, name):
    K, _, R, C = blocks.shape
    tr = _tile(R, 256, 16)

    def body(core_ref, a_ref, b_ref, o_ref):
        del core_ref
        o_ref[0] = (a_ref[0, 0].astype(F32) + b_ref[0].astype(F32)).astype(BF16)

    spec = lambda: pl.BlockSpec((1, tr, C), lambda k, i, core_ref: (k, i, 0))
    return pl.pallas_call(
        body, name=name,
        grid_spec=pltpu.PrefetchScalarGridSpec(
            num_scalar_prefetch=1, grid=(K, R // tr),
            in_specs=[pl.BlockSpec((1, 1, tr, C), lambda k, i, core_ref: (k, core_ref[0], i, 0)), spec()],
            out_specs=spec()),
        out_shape=S((K, R, C), BF16), compiler_params=_cp(ARB, ARB),
    )(core, blocks, recv)


def _sum_adam(chip_sums, recv, w, m, v, chip, name, transposed=False):
    R, C = chip_sums.shape[1:]
    tr = _tile(R, 256, 16)

    def body(chip_ref, own_ref, r_ref, w_ref, m_ref, v_ref, g_ref, d_ref, m2_ref, v2_ref):
        del chip_ref
        g = own_ref[0].astype(F32)
        for j in range(3):
            g = g + r_ref[j].astype(F32)
        if transposed:
            g = g.T
        g_ref[...] = g
        d_ref[...], m2_ref[...], v2_ref[...] = _adam_math(w_ref[...], g, m_ref[...], v_ref[...])

    if transposed:
        spec = lambda: pl.BlockSpec((C, tr), lambda i, chip_ref: (0, i))
        shape = (C, R)
    else:
        spec = lambda: pl.BlockSpec((tr, C), lambda i, chip_ref: (i, 0))
        shape = (R, C)
    assert w.shape == shape
    return pl.pallas_call(
        body, name=name,
        grid_spec=pltpu.PrefetchScalarGridSpec(
            num_scalar_prefetch=1, grid=(R // tr,),
            in_specs=[pl.BlockSpec((1, tr, C), lambda i, chip_ref: (chip_ref[0], i, 0)),
                      pl.BlockSpec((3, tr, C), lambda i, chip_ref: (0, i, 0)), spec(), spec(), spec()],
            out_specs=[spec(), spec(), spec(), spec()]),
        out_shape=[S(shape, F32)] * 4, compiler_params=_cp(ARB),
    )(chip, chip_sums, recv, w, m, v)


def _adam_small(w, g, m, v):
    R, C = w.shape
    tr = _tile(R, 512, 8)

    def body(w_ref, g_ref, m_ref, v_ref, d_ref, m2_ref, v2_ref):
        d_ref[...], m2_ref[...], v2_ref[...] = _adam_math(w_ref[...], g_ref[...], m_ref[...], v_ref[...])

    spec = lambda: pl.BlockSpec((tr, C), lambda i: (i, 0))
    return pl.pallas_call(
        body, name="adam_small", grid=(R // tr,), in_specs=[spec()] * 4, out_specs=[spec()] * 3,
        out_shape=[S((R, C), F32)] * 3, compiler_params=_cp(ARB),
    )(w, g, m, v)


def _position():
    return lax.axis_index("x"), lax.axis_index("y"), lax.axis_index("c")


def _all_gather_weights(arr):
    R = arr.shape[0]
    half = R // 2
    assert half % 16 == 0

    def body(in_ref, out_ref, send_sems, recv_sems, local_sem):
        x, y, c = _position()
        me, sibling = (x, y, c), (x, y, 1 - c)
        xn, yn, diag = (1 - x, y), (x, 1 - y), (1 - x, 1 - y)
        upper, lower = pl.ds(0, half), pl.ds(half, half)

        def slot(p, rows=None):
            ref = out_ref.at[4 * p[0] + 2 * p[1] + p[2]]
            return ref if rows is None else ref.at[rows]

        def copy(kk, block, to, rows=None, src=None):
            return pltpu.make_async_remote_copy(
                src_ref=slot(block, rows) if src is None else src, dst_ref=slot(block, rows),
                send_sem=send_sems.at[kk], recv_sem=recv_sems.at[kk], device_id=to, device_id_type=MESH)

        mine = pltpu.make_async_copy(in_ref, slot(me), local_sem)
        mine.start()
        sent = [copy(0, me, sibling, src=in_ref), copy(1, me, (*xn, c), src=in_ref), copy(2, me, (*yn, c), src=in_ref)]
        for cp in sent:
            cp.start()

        def then(cps):
            for cp in cps:
                cp.start()
            sent.extend(cps)

        copy(1, (*xn, c), me).wait_recv()
        then([copy(5, (*xn, c), (*yn, c), rows=upper), copy(3, (*xn, c), sibling)])
        copy(2, (*yn, c), me).wait_recv()
        then([copy(6, (*yn, c), (*xn, c), rows=lower), copy(4, (*yn, c), sibling)])
        copy(5, (*diag, c), me, rows=upper).wait_recv()
        then([copy(7, (*diag, c), sibling, rows=upper)])
        copy(6, (*diag, c), me, rows=lower).wait_recv()
        then([copy(8, (*diag, c), sibling, rows=lower)])
        copy(0, sibling, me).wait_recv()
        copy(3, (*xn, 1 - c), me).wait_recv()
        copy(4, (*yn, 1 - c), me).wait_recv()
        copy(7, (*diag, 1 - c), me, rows=upper).wait_recv()
        copy(8, (*diag, 1 - c), me, rows=lower).wait_recv()
        for cp in sent:
            cp.wait_send()
        mine.wait()

    any_spec = pl.BlockSpec(memory_space=pl.ANY)
    return pl.pallas_call(
        body, name="all_gather_weights", in_specs=[any_spec], out_specs=any_spec,
        out_shape=S((N_DEV,) + arr.shape, arr.dtype),
        scratch_shapes=[pltpu.SemaphoreType.DMA((9,)), pltpu.SemaphoreType.DMA((9,)), pltpu.SemaphoreType.DMA],
    )(arr)


def _sibling_copies(ins, outs, send_sems, recv_sems):
    x, y, c = _position()
    return [pltpu.make_async_remote_copy(src_ref=ins[a].at[k, 1 - c], dst_ref=outs[a].at[k],
                                         send_sem=send_sems.at[a, k], recv_sem=recv_sems.at[a, k],
                                         device_id=(x, y, 1 - c), device_id_type=MESH)
            for a in range(len(ins)) for k in range(ins[a].shape[0])]


def _sibling_sems(arrs):
    shape = (max(len(arrs), 1), arrs[0].shape[0] if arrs else 1)
    return [pltpu.SemaphoreType.DMA(shape), pltpu.SemaphoreType.DMA(shape)]


def _exchange_sibling(arrs):
    na = len(arrs)

    def body(*refs):
        ins, outs = refs[:na], refs[na:2 * na]
        send_sems, recv_sems = refs[2 * na:]
        cps = _sibling_copies(ins, outs, send_sems, recv_sems)
        for cp in cps:
            cp.start()
        for cp in cps:
            cp.wait()

    any_spec = pl.BlockSpec(memory_space=pl.ANY)
    return pl.pallas_call(
        body, name="exchange_sibling", in_specs=[any_spec] * na, out_specs=[any_spec] * na,
        out_shape=[S(a.shape[:1] + a.shape[2:], a.dtype) for a in arrs],
        scratch_shapes=_sibling_sems(arrs),
    )(*arrs)


def _chip_exchange_copies(ins, outs, send_sems, recv_sems):
    x, y, c = _position()
    chips = [(1 - x, y), (x, 1 - y), (1 - x, 1 - y)]
    return [pltpu.make_async_remote_copy(
        src_ref=ins[a].at[2 * qx + qy], dst_ref=outs[a].at[j], send_sem=send_sems.at[a, j],
        recv_sem=recv_sems.at[a, j], device_id=(qx, qy, c), device_id_type=MESH)
        for a in range(len(ins)) for j, (qx, qy) in enumerate(chips)]


def _broadcast_copies(srcs, dsts, send_sems, recv_sems):
    x, y, c = _position()
    me = 4 * x + 2 * y + c
    cps = []
    for a in range(len(srcs)):
        for k in range(1, N_DEV):
            peer = (1 - x if k & 4 else x, 1 - y if k & 2 else y, 1 - c if k & 1 else c)
            cps.append(pltpu.make_async_remote_copy(
                src_ref=srcs[a], dst_ref=dsts[a].at[me], send_sem=send_sems.at[a, k - 1],
                recv_sem=recv_sems.at[a, k - 1], device_id=peer, device_id_type=MESH))
    return me, cps


def _all_reduce_small(part):
    R, C = part.shape

    def body(p_ref, out_ref, gath_ref, send_sems, recv_sems):
        me, cps = _broadcast_copies([p_ref], [gath_ref], send_sems, recv_sems)
        gath_ref[me] = p_ref[...]
        for cp in cps:
            cp.start()
        for cp in cps:
            cp.wait()
        acc = gath_ref[0]
        for d in range(1, N_DEV):
            acc = acc + gath_ref[d]
        out_ref[...] = acc

    vm = pl.BlockSpec(memory_space=pltpu.VMEM)
    return pl.pallas_call(
        body, name="all_reduce_small", in_specs=[vm], out_specs=vm, out_shape=S((R, C), F32),
        scratch_shapes=[pltpu.VMEM((N_DEV, R, C), F32), pltpu.SemaphoreType.DMA((1, N_DEV - 1)),
                        pltpu.SemaphoreType.DMA((1, N_DEV - 1))],
    )(part)


def _pack(parts):
    rows = []
    for p in parts:
        f = p.reshape(-1).astype(F32)
        pad = (-f.shape[0]) % (8 * LANES)
        rows.append(jnp.pad(f, (0, pad)).reshape(-1, LANES))
    return jnp.concatenate(rows, axis=0)


def _unpack(buf, shapes):
    out, r = [], 0
    for shp in shapes:
        n = 1
        for s in shp:
            n *= s
        nr = -(-n // (8 * LANES)) * 8
        out.append(buf[r:r + nr].reshape(-1)[:n].reshape(shp))
        r += nr
    return out


def kernel(x, norm_w, w_in, conv_w, a_log, dt_bias, head_norm_w, sgu_ln_w, sgu_ln_b, w_spatial, b_spatial, w_out, final_norm_w, loss_target, m_norm_w, m_w_in, m_conv_w, m_a_log, m_dt_bias, m_head_norm_w, m_sgu_ln_w, m_sgu_ln_b, m_w_spatial, m_b_spatial, m_w_out, m_final_norm_w, v_norm_w, v_w_in, v_conv_w, v_a_log, v_dt_bias, v_head_norm_w, v_sgu_ln_w, v_sgu_ln_b, v_w_spatial, v_b_spatial, v_w_out, v_final_norm_w):
    T, DM = x.shape[1], x.shape[2]
    H, D = a_log.shape[1], head_norm_w.shape[1]
    G, P = w_spatial.shape[1], w_spatial.shape[2]
    AW, BW = H * D, G * P
    MIX = AW + BW
    WD = w_in.shape[2]
    IN = N_DEV * WD
    RO = w_out.shape[1]
    CW = conv_w.shape[2]
    sizes = (3 * AW, AW, H, H, BW, BW, BW)
    assert sum(sizes) == IN and 2 * H <= LANES and 3 * H <= 32 and N_DEV * RO == MIX and N_DEV * CW == 3 * AW
    offs = [0]
    for s in sizes:
        offs.append(offs[-1] + s)
    px, py, pc = _position()
    dev = 4 * px + 2 * py + pc
    chip = 2 * px + py

    x2, tgt = x[0], loss_target[0]

    g_win = _all_gather_weights(_cast_bf16_t(w_in[0].T, "cast_w_in"))
    w_main, w_ba = _relayout_w(g_win, offs[2], offs[4])
    alog_row = jnp.pad(a_log, ((0, 0), (H, LANES - 2 * H)))
    dtb_row = jnp.pad(dt_bias, ((0, 0), (H, LANES - 2 * H)))
    bs_t = b_spatial[0].T

    proj, ba, xn, (g_wout, g_conv) = _in_proj(x2, norm_w, w_main, w_ba, [_cast_bf16(w_out[0], "cast_w_out"), conv_w[0]])
    w_out_full = g_wout.reshape(MIX, DM)
    conv_full = g_conv.transpose(1, 0, 2).reshape(4, 3 * AW)
    q, k, v, c, gcol, grow = _prep_a_fwd(proj, ba, conv_full, alog_row, dtb_row, H, D)
    o, vnew, ssave, asave = _delta_fwd(q, k, v, gcol, grow, H, D)
    ocat = _mix_fwd(o, proj, head_norm_w, sgu_ln_w, sgu_ln_b, w_spatial[0], bs_t, H, D, G, P)
    dh, dh_bf, d_ocat, loss_acc, g_fnw = _out_proj_loss(ocat, w_out_full, x2, tgt, final_norm_w.reshape(1, DM))
    loss = lax.psum(loss_acc[0, 0], AXES)

    core_idx = jnp.reshape(pc, (1,)).astype(jnp.int32)
    chip_idx = jnp.reshape(chip, (1,)).astype(jnp.int32)
    g_wout_blocks = _grad_w(ocat, dh_bf, "grad_w_out").reshape(4, 2, RO, DM)
    (d_o, dmain, g_hnw, g_ln, g_wsp, g_bs_t), (sib_wout,) = _mix_bwd(
        d_ocat, o, proj, head_norm_w, sgu_ln_w, sgu_ln_b, w_spatial[0], bs_t, H, D, G, P, [g_wout_blocks])
    chip_wout = _pair_sum(g_wout_blocks, sib_wout, core_idx, "pair_sum_w_out")
    (dq, dk, dv, dgate, dpar), (recv_wout,) = _delta_bwd(
        q, k, v, gcol, grow, ba, vnew, ssave, asave, d_o, a_log, dt_bias, H, D, [chip_wout])
    dmain, g_conv_part = _prep_a_bwd(dq, dk, dv, c, proj, conv_full, dmain, H, D)
    dba = dgate.astype(BF16)
    g_main, g_ba = _grad_w_in(xn, dmain, dba)
    g_win_blocks = _relayout_g(g_main, g_ba, WD, offs[2], offs[4]).reshape(4, 2, DM, WD)
    (sib_win,) = _exchange_sibling([g_win_blocks])
    chip_win = _pair_sum(g_win_blocks, sib_win, core_idx, "pair_sum_w_in")
    small_shapes = [a_log.shape, dt_bias.shape, head_norm_w.shape, sgu_ln_w.shape, sgu_ln_b.shape,
                    w_spatial.shape, b_spatial.shape, final_norm_w.shape]
    parts = [dpar[0, :H], dpar[0, H:2 * H], g_hnw[0], g_ln[0], g_ln[1], g_wsp, g_bs_t[:, :G].T, g_fnw[0],
             g_conv_part[:4]]
    grad_x, g_nw, small_gath, recv_win = _dx(dmain, dba, w_main, w_ba, x2, dh, norm_w, chip_win, _pack(parts))
    red = _sum_slots(small_gath)
    grad_w_in, delta_w_in, new_m_w_in, new_v_w_in = _sum_adam(
        chip_win, recv_win, w_in[0].T, m_w_in[0].T, v_w_in[0].T, chip_idx, "sum_adam_w_in", transposed=True)
    grad_w_out, delta_w_out, new_m_w_out, new_v_w_out = _sum_adam(
        chip_wout, recv_wout, w_out[0], m_w_out[0], v_w_out[0], chip_idx, "sum_adam_w_out")
    red_nw = _all_reduce_small(_pack([g_nw[0]]))
    grads_small = _unpack(red_nw, [norm_w.shape]) + _unpack(red, small_shapes + [(4, 3 * AW)])
    g_conv_full = grads_small.pop()
    grad_conv = lax.dynamic_slice_in_dim(g_conv_full, dev * CW, CW, axis=1)[None]
    small_w = [norm_w, a_log, dt_bias, head_norm_w, sgu_ln_w, sgu_ln_b, w_spatial, b_spatial, final_norm_w, conv_w]
    small_m = [m_norm_w, m_a_log, m_dt_bias, m_head_norm_w, m_sgu_ln_w, m_sgu_ln_b, m_w_spatial, m_b_spatial,
               m_final_norm_w, m_conv_w]
    small_v = [v_norm_w, v_a_log, v_dt_bias, v_head_norm_w, v_sgu_ln_w, v_sgu_ln_b, v_w_spatial, v_b_spatial,
               v_final_norm_w, v_conv_w]
    small_g = grads_small + [grad_conv]
    shapes10 = [w.shape for w in small_w]
    d_p, m_p, v_p = _adam_small(_pack(small_w), _pack(small_g), _pack(small_m), _pack(small_v))
    d_s, m_s, v_s = _unpack(d_p, shapes10), _unpack(m_p, shapes10), _unpack(v_p, shapes10)

    def order(small, win, wout):
        return [small[0], win.T[None], small[9], small[1], small[2], small[3], small[4], small[5], small[6], small[7],
                wout[None], small[8]]

    grads = order(small_g, grad_w_in, grad_w_out)
    deltas = order(d_s, delta_w_in, delta_w_out)
    new_m = order(m_s, new_m_w_in, new_m_w_out)
    new_v = order(v_s, new_v_w_in, new_v_w_out)
    return (loss, grad_x[None], *grads, *deltas, *new_m, *new_v)
```

```python
import functools

import jax
import jax.numpy as jnp
from jax import lax
from jax.experimental import pallas as pl
from jax.experimental.pallas import tpu as pltpu

F32 = jnp.float32
BF16 = jnp.bfloat16
MXU = jnp.bfloat16
HI = lax.Precision.HIGHEST
EPS = 1e-6
CHUNK_A = 64
LANES = 128
MESH = pl.DeviceIdType.MESH
AXES = ("x", "y", "c")
N_DEV = 8

ADAM_LR = 0.001
ADAM_B1 = 0.9
ADAM_B2 = 0.999
ADAM_EPS = 1e-08
ADAM_WD = 0.01
ADAM_STEP = 10

S = jax.ShapeDtypeStruct
ARB = "arbitrary"


def _cp(*sem):
    return pltpu.CompilerParams(dimension_semantics=tuple(sem), vmem_limit_bytes=56 * 1024 * 1024)


def _tile(n, cap, mult):
    best = None
    t = mult
    while t <= min(n, cap):
        if n % t == 0:
            best = t
        t += mult
    return best if best is not None else n


def _mm(a, b):
    return jnp.dot(a.astype(MXU), b.astype(MXU), preferred_element_type=F32)


def _mm_nt(a, b):
    return lax.dot_general(a.astype(MXU), b.astype(MXU), (((1,), (1,)), ((), ())), preferred_element_type=F32)


def _mm_tn(a, b):
    return lax.dot_general(a.astype(MXU), b.astype(MXU), (((0,), (0,)), ((), ())), preferred_element_type=F32)


def _mmh(a, b):
    return jnp.dot(a, b, precision=HI, preferred_element_type=F32)


def _mmh_tn(a, b):
    return lax.dot_general(a, b, (((0,), (0,)), ((), ())), precision=HI, preferred_element_type=F32)


def _sigmoid(x):
    return 1.0 / (1.0 + jnp.exp(-x))


def _silu(x):
    return x * _sigmoid(x)


def _dsilu(x):
    s = _sigmoid(x)
    return s * (1.0 + x * (1.0 - s))


def _softplus(x):
    return jnp.maximum(x, 0.0) + jnp.log(1.0 + jnp.exp(-jnp.abs(x)))


def _pieces(wd, gate_lo, gate_hi, total):
    out = []
    for d in range(N_DEV):
        lo, hi = d * wd, (d + 1) * wd
        for dest, a, b, shift in (("main", 0, gate_lo, 0), ("gate", gate_lo, gate_hi, -gate_lo),
                                  ("main", gate_hi, total, gate_lo - gate_hi)):
            s0, s1 = max(lo, a), min(hi, b)
            if s0 < s1:
                out.append((d, s0 - lo, s1 - lo, dest, s0 + shift))
    return out


def _cast_bf16(a, name):
    R, C = a.shape
    tr = _tile(R, 256, 16)

    def body(a_ref, o_ref):
        o_ref[...] = a_ref[...].astype(BF16)

    spec = pl.BlockSpec((tr, C), lambda i: (i, 0))
    return pl.pallas_call(body, name=name, grid=(R // tr,), in_specs=[spec], out_specs=spec,
                          out_shape=S((R, C), BF16), compiler_params=_cp(ARB))(a)


def _cast_bf16_t(a_t, name):
    C, R = a_t.shape
    tr = _tile(R, 256, LANES)

    def body(a_ref, o_ref):
        o_ref[...] = a_ref[...].T.astype(BF16)

    return pl.pallas_call(body, name=name, grid=(R // tr,), in_specs=[pl.BlockSpec((C, tr), lambda i: (0, i))],
                          out_specs=pl.BlockSpec((tr, C), lambda i: (i, 0)),
                          out_shape=S((R, C), BF16), compiler_params=_cp(ARB))(a_t)


def _relayout_w(g_win, gate_lo, gate_hi):
    _, DM, WD = g_win.shape
    total = N_DEV * WD
    NM = total - (gate_hi - gate_lo)
    tr = _tile(DM, 256, 16)
    plan = _pieces(WD, gate_lo, gate_hi, total)

    def body(g_ref, main_ref, gate_ref):
        gate_ref[...] = jnp.zeros_like(gate_ref)
        for d, s0, s1, dest, c0 in plan:
            dst = main_ref if dest == "main" else gate_ref
            dst[:, c0:c0 + (s1 - s0)] = g_ref[d, :, s0:s1]

    return pl.pallas_call(
        body, name="relayout_w", grid=(DM // tr,),
        in_specs=[pl.BlockSpec((N_DEV, tr, WD), lambda i: (0, i, 0))],
        out_specs=[pl.BlockSpec((tr, NM), lambda i: (i, 0)), pl.BlockSpec((tr, LANES), lambda i: (i, 0))],
        out_shape=[S((DM, NM), g_win.dtype), S((DM, LANES), g_win.dtype)],
        compiler_params=_cp(ARB),
    )(g_win)


def _relayout_g(g_main, g_gate, WD, gate_lo, gate_hi):
    DM = g_main.shape[0]
    total = N_DEV * WD
    tr = _tile(DM, 256, 16)
    plan = _pieces(WD, gate_lo, gate_hi, total)

    def body(m_ref, gate_ref, out_ref):
        for d, s0, s1, dest, c0 in plan:
            src = m_ref if dest == "main" else gate_ref
            out_ref[d, :, s0:s1] = src[:, c0:c0 + (s1 - s0)].astype(BF16)

    return pl.pallas_call(
        body, name="relayout_g", grid=(DM // tr,),
        in_specs=[pl.BlockSpec((tr, g_main.shape[1]), lambda i: (i, 0)), pl.BlockSpec((tr, LANES), lambda i: (i, 0))],
        out_specs=pl.BlockSpec((N_DEV, tr, WD), lambda i: (0, i, 0)),
        out_shape=S((N_DEV, DM, WD), BF16),
        compiler_params=_cp(ARB),
    )(g_main, g_gate)


def _in_proj(x, norm_w, w_main, w_ba, shards):
    T, DM = x.shape
    NM = w_main.shape[1]
    tm = _tile(T, 1024, 8)
    tn = _tile(NM, 1024, LANES)
    ni, nj = T // tm, NM // tn
    ns = len(shards)

    def body(x_ref, nw_ref, w_ref, wba_ref, *rest):
        srcs = rest[:ns]
        proj_ref, ba_ref, xn_ref = rest[ns:ns + 3]
        gath = rest[ns + 3:2 * ns + 3]
        send_sems, recv_sems, local_sems = rest[2 * ns + 3:]
        i = pl.program_id(0)
        me, cps = _broadcast_copies(srcs, gath, send_sems, recv_sems)
        cps = cps + [pltpu.make_async_copy(srcs[a], gath[a].at[me], local_sems.at[a]) for a in range(ns)]

        @pl.when((i == 0) & (pl.program_id(1) == 0))
        def _():
            for cp in cps:
                cp.start()

        @pl.when((i == ni - 1) & (pl.program_id(1) == nj - 1))
        def _():
            for cp in cps:
                cp.wait()

        @pl.when(pl.program_id(1) == 0)
        def _():
            xv = x_ref[...]
            r = lax.rsqrt(jnp.mean(xv * xv, axis=-1, keepdims=True) + EPS)
            xn = (xv * r * nw_ref[...]).astype(BF16)
            xn_ref[...] = xn
            ba_ref[...] = jnp.dot(xn.astype(MXU), wba_ref[...].astype(MXU), preferred_element_type=F32)

        proj_ref[...] = jnp.dot(xn_ref[...].astype(MXU), w_ref[...].astype(MXU), preferred_element_type=F32)

    any_spec = pl.BlockSpec(memory_space=pl.ANY)
    res = pl.pallas_call(
        body, name="in_proj", grid=(ni, nj),
        in_specs=[pl.BlockSpec((tm, DM), lambda i, j: (i, 0)),
                  pl.BlockSpec((1, DM), lambda i, j: (0, 0)),
                  pl.BlockSpec((DM, tn), lambda i, j: (0, j)),
                  pl.BlockSpec((DM, LANES), lambda i, j: (0, 0))] + [any_spec] * ns,
        out_specs=[pl.BlockSpec((tm, tn), lambda i, j: (i, j)),
                   pl.BlockSpec((tm, LANES), lambda i, j: (i, 0)),
                   pl.BlockSpec((tm, DM), lambda i, j: (i, 0))] + [any_spec] * ns,
        out_shape=[S((T, NM), F32), S((T, LANES), F32), S((T, DM), BF16)]
        + [S((N_DEV,) + a.shape, a.dtype) for a in shards],
        scratch_shapes=[pltpu.SemaphoreType.DMA((ns, N_DEV - 1)), pltpu.SemaphoreType.DMA((ns, N_DEV - 1)),
                        pltpu.SemaphoreType.DMA((ns,))],
        compiler_params=_cp(ARB, ARB),
    )(x, norm_w, w_main, w_ba, *shards)
    return res[0], res[1], res[2], res[3:]


def _prep_a_fwd(proj, ba, conv_w, alog_row, dtb_row, H, D):
    T = proj.shape[0]
    AW = H * D
    C3 = 3 * AW
    tb = _tile(T, 256, CHUNK_A)
    nch = tb // CHUNK_A
    nblk = T // tb
    scale = float(D) ** -0.5

    def body(x_ref, halo_ref, ba_ref, cw_ref, al_ref, dt_ref, q_ref, k_ref, v_ref, c_ref, gcol_ref, grow_ref):
        i = pl.program_id(0)
        xv = x_ref[...]
        halo = halo_ref[...] * (i > 0).astype(F32)
        xp = jnp.concatenate([halo, xv], axis=0)
        cw = cw_ref[...]
        c = cw[0:1, :] * xp[5:5 + tb]
        for j in range(1, 4):
            c = c + cw[j:j + 1, :] * xp[5 + j:5 + j + tb]
        c_ref[...] = c
        a = _silu(c)
        for h in range(H):
            qh = a[:, h * D:(h + 1) * D]
            kh = a[:, AW + h * D:AW + (h + 1) * D]
            qr = lax.rsqrt(jnp.sum(qh * qh, axis=-1, keepdims=True) + EPS)
            kr = lax.rsqrt(jnp.sum(kh * kh, axis=-1, keepdims=True) + EPS)
            q_ref[:, h * D:(h + 1) * D] = qh * (qr * scale)
            k_ref[:, h * D:(h + 1) * D] = kh * kr
        v_ref[...] = a[:, 2 * AW:]

        bav = ba_ref[...]
        lane = lax.broadcasted_iota(jnp.int32, (tb, LANES), 1)
        beta = _sigmoid(bav)
        g = -jnp.exp(al_ref[...]) * _softplus(bav + dt_ref[...])
        gates = jnp.where(lane < H, beta, jnp.where(lane < 2 * H, g, 0.0))
        ri = lax.broadcasted_iota(jnp.int32, (CHUNK_A, CHUNK_A), 0)
        ci = lax.broadcasted_iota(jnp.int32, (CHUNK_A, CHUNK_A), 1)
        tri = (ri >= ci).astype(F32)
        lane_c = lax.broadcasted_iota(jnp.int32, (CHUNK_A, LANES), 1)
        for cc in range(nch):
            gch = gates[cc * CHUNK_A:(cc + 1) * CHUNK_A]
            gc = pltpu.roll(_mmh(tri, gch), H, 1)
            full = jnp.where(lane_c < 2 * H, gch, jnp.where(lane_c < 3 * H, gc, 0.0))
            gcol_ref[cc * CHUNK_A:(cc + 1) * CHUNK_A, :] = full
            grow_ref[cc] = full.T[0:32, :]

    return pl.pallas_call(
        body, name="prep_a_fwd", grid=(nblk,),
        in_specs=[pl.BlockSpec((tb, C3), lambda i: (i, 0)),
                  pl.BlockSpec((8, C3), lambda i: (jnp.maximum(i * (tb // 8) - 1, 0), 0)),
                  pl.BlockSpec((tb, LANES), lambda i: (i, 0)),
                  pl.BlockSpec((4, C3), lambda i: (0, 0)),
                  pl.BlockSpec((1, LANES), lambda i: (0, 0)),
                  pl.BlockSpec((1, LANES), lambda i: (0, 0))],
        out_specs=[pl.BlockSpec((tb, AW), lambda i: (i, 0)),
                   pl.BlockSpec((tb, AW), lambda i: (i, 0)),
                   pl.BlockSpec((tb, AW), lambda i: (i, 0)),
                   pl.BlockSpec((tb, C3), lambda i: (i, 0)),
                   pl.BlockSpec((tb, LANES), lambda i: (i, 0)),
                   pl.BlockSpec((nch, 32, CHUNK_A), lambda i: (i, 0, 0))],
        out_shape=[S((T, AW), F32), S((T, AW), F32), S((T, AW), F32), S((T, C3), F32),
                   S((T, LANES), F32), S((T // CHUNK_A, 32, CHUNK_A), F32)],
        compiler_params=_cp(ARB),
    )(proj, proj, ba, conv_w, alog_row, dtb_row)


_NN = (((1,), (0,)), ((), ()))
_TN = (((0,), (0,)), ((), ()))


def _split(a):
    hi = a.astype(BF16)
    return hi, (a - hi.astype(F32)).astype(BF16)


def _mm3(a, b, dims=_NN):
    ah, al = a if isinstance(a, tuple) else _split(a)
    bh, bl = b if isinstance(b, tuple) else _split(b)
    dg = lambda p, r: lax.dot_general(p, r, dims, preferred_element_type=F32)
    return dg(ah, bh) + (dg(ah, bl) + dg(al, bh))


def _interleave(gens):
    gens = list(gens)
    while gens:
        alive = []
        for g in gens:
            try:
                next(g)
                alive.append(g)
            except StopIteration:
                pass
        gens = alive


def _chunk_terms(q, k, v, gcolv, growv, h, H):
    C = CHUNK_A
    beta_c = gcolv[:, h:h + 1]
    g_c = gcolv[:, H + h:H + h + 1]
    gc_c = gcolv[:, 2 * H + h:2 * H + h + 1]
    gc_r = growv[2 * H + h:2 * H + h + 1, :]
    ri = lax.broadcasted_iota(jnp.int32, (C, C), 0)
    ci = lax.broadcasted_iota(jnp.int32, (C, C), 1)
    incl = ri >= ci
    strict = ri > ci
    kb = k * beta_c
    vb = v * beta_c
    p_raw = _mm_nt(kb, k)
    qk_raw = _mm_nt(q, k)
    gam = jnp.where(incl, jnp.exp(jnp.where(incl, gc_c - gc_r, 0.0)), 0.0)
    e_c = jnp.exp(gc_c)
    gl = gc_r[:, C - 1:C]
    edec = jnp.exp(gl - gc_c)
    yield
    lmat = jnp.where(strict, p_raw * gam, 0.0)
    attn = jnp.where(incl, qk_raw * gam, 0.0)
    return dict(beta_c=beta_c, g_c=g_c, gc_c=gc_c, gc_r=gc_r, incl=incl, strict=strict, gam=gam, e_c=e_c,
                kb=kb, vb=vb, lmat=lmat, attn=attn, gl=gl, edec=edec, ri=ri, ci=ci)


def _inv_unit_lower(lmat):
    C = lmat.shape[0]
    ri = lax.broadcasted_iota(jnp.int32, (C, C), 0)
    ci = lax.broadcasted_iota(jnp.int32, (C, C), 1)
    eye = (ri == ci).astype(F32)
    x = -lmat
    a = eye + x
    n = 1
    while 2 * n < C:
        xs = _split(x)
        x = _mm3(xs, xs)
        yield
        a = a + _mm3(a, x)
        n *= 2
    yield
    return a


def _delta_fwd(q, k, v, gcol, grow, H, D):
    T = q.shape[0]
    C = CHUNK_A
    N = T // C
    AW = H * D
    CPS = 2 if N % 2 == 0 else 1

    def body(q_ref, k_ref, v_ref, gcol_ref, grow_ref, o_ref, vn_ref, ssave_ref, asave_ref, s_ref):
        @pl.when(pl.program_id(0) == 0)
        def _():
            s_ref[...] = jnp.zeros_like(s_ref)

        state = {(0, h): s_ref[h] for h in range(H)}

        def head(cc, h):
            rows = slice(cc * C, (cc + 1) * C)
            sl = slice(h * D, (h + 1) * D)
            qv, kv, vv = q_ref[rows, sl], k_ref[rows, sl], v_ref[rows, sl]
            t = yield from _chunk_terms(qv, kv, vv, gcol_ref[rows, :], grow_ref[cc], h, H)
            a = yield from _inv_unit_lower(t["lmat"])
            asave_ref[cc, h] = a
            while (cc, h) not in state:
                yield
            st = state[(cc, h)]
            ssave_ref[cc, h] = st
            ks = _mm(t["kb"] * t["e_c"], st)
            o_inter = _mm(qv * t["e_c"], st)
            yield
            v_new = _mm3(a, t["vb"] - ks)
            yield
            vn_ref[rows, sl] = v_new
            o_intra = _mm(t["attn"], v_new)
            s_upd = _mm_tn(kv * t["edec"], v_new)
            yield
            o_ref[rows, sl] = o_inter + o_intra
            state[(cc + 1, h)] = st * jnp.exp(t["gl"]) + s_upd

        _interleave(head(cc, h) for cc in range(CPS) for h in range(H))
        for h in range(H):
            s_ref[h] = state[(CPS, h)]

    blk = lambda: pl.BlockSpec((CPS * C, AW), lambda n: (n, 0))
    return pl.pallas_call(
        body, name="delta_fwd", grid=(N // CPS,),
        in_specs=[blk(), blk(), blk(),
                  pl.BlockSpec((CPS * C, LANES), lambda n: (n, 0)),
                  pl.BlockSpec((CPS, 32, C), lambda n: (n, 0, 0))],
        out_specs=[blk(), blk(),
                   pl.BlockSpec((CPS, H, D, D), lambda n: (n, 0, 0, 0)),
                   pl.BlockSpec((CPS, H, C, C), lambda n: (n, 0, 0, 0))],
        out_shape=[S((T, AW), F32), S((T, AW), F32), S((N, H, D, D), F32), S((N, H, C, C), F32)],
        scratch_shapes=[pltpu.VMEM((H, D, D), F32)],
        compiler_params=_cp(ARB),
    )(q, k, v, gcol, grow)


def _delta_bwd(q, k, v, gcol, grow, ba, vnew, ssave, asave, d_o, a_log, dt_bias, H, D, carry):
    T = q.shape[0]
    C = CHUNK_A
    N = T // C
    AW = H * D
    nc = len(carry)
    CPS = 2 if N % 2 == 0 else 1
    NS = N // CPS

    def body(al_ref, dt_ref, q_ref, k_ref, v_ref, gcol_ref, grow_ref, ba_ref, vn_ref, ss_ref, as_ref, do_ref, *rest):
        cins = rest[:nc]
        dq_ref, dk_ref, dv_ref, dgate_ref, dpar_ref = rest[nc:nc + 5]
        couts = rest[nc + 5:2 * nc + 5]
        ds_ref, csend, crecv = rest[2 * nc + 5:]
        ccps = _chip_exchange_copies(cins, couts, csend, crecv)

        @pl.when(pl.program_id(0) == 0)
        def _():
            ds_ref[...] = jnp.zeros_like(ds_ref)
            dpar_ref[...] = jnp.zeros_like(dpar_ref)
            for cp in ccps:
                cp.start()

        lane = lax.broadcasted_iota(jnp.int32, (C, LANES), 1)
        lane1 = lax.broadcasted_iota(jnp.int32, (1, LANES), 1)
        rowi = lax.broadcasted_iota(jnp.int32, (C, 1), 0)
        acc = {"dpar": jnp.zeros((1, LANES), F32)}
        for cc in range(CPS):
            acc[cc] = jnp.zeros((C, LANES), F32)
        state = {(0, h): ds_ref[h] for h in range(H)}

        def head(oi, h):
            cc = CPS - 1 - oi
            rows = slice(cc * C, (cc + 1) * C)
            sl = slice(h * D, (h + 1) * D)
            st = ss_ref[cc, h]
            a = as_ref[cc, h]
            qv, kv, vv, dov, v_new = q_ref[rows, sl], k_ref[rows, sl], v_ref[rows, sl], do_ref[rows, sl], vn_ref[rows, sl]
            bav = ba_ref[rows, :]
            t = yield from _chunk_terms(qv, kv, vv, gcol_ref[rows, :], grow_ref[cc], h, H)
            beta_c, e_c, gam, kb = t["beta_c"], t["e_c"], t["gam"], t["kb"]
            incl, strict, attn, lmat, edec = t["incl"], t["strict"], t["attn"], t["lmat"], t["edec"]
            kdec = kv * edec
            egl = jnp.exp(t["gl"])
            qe = qv * e_c
            ekb = kb * e_c

            t1 = _mm_nt(dov, st)
            ds_o = _mm_tn(qe, dov)
            dattn_raw = _mm_nt(dov, v_new)
            dv_new_o = _mm_tn(attn, dov)
            yield
            while (oi, h) not in state:
                yield
            ds_next = state[(oi, h)]
            dkdec = _mm_nt(v_new, ds_next)
            dv_new_s = _mm(kdec, ds_next)
            yield
            dgl = egl * jnp.sum(jnp.sum(st * ds_next, axis=1, keepdims=True), axis=0, keepdims=True)
            dk = edec * dkdec
            r = jnp.sum(dkdec * kdec, axis=1, keepdims=True)
            dgc = -r
            dgl = dgl + jnp.sum(r, axis=0, keepdims=True)
            dq = e_c * t1
            dgc = dgc + jnp.sum(t1 * qe, axis=1, keepdims=True)
            dattn = jnp.where(incl, dattn_raw, 0.0)
            dv_new = dv_new_s + dv_new_o
            dqm = dattn * gam
            z = dattn * attn
            dvb = _mm3(a, dv_new, _TN)
            dq_a = _mm(dqm, kv)
            dk_a = _mm_tn(dqm, qv)
            yield
            dq_ref[rows, sl] = dq + dq_a
            dv_ref[rows, sl] = beta_c * dvb
            ds_kb = _mm_tn(ekb, dvb)
            dekb_neg = _mm_nt(dvb, st)
            dl_neg = _mm_nt(dvb, v_new)
            yield
            state[(oi + 1, h)] = egl * ds_next + ds_o - ds_kb
            dekb = -dekb_neg
            dl = jnp.where(strict, -dl_neg, 0.0)
            dp = dl * gam
            z = z + dl * lmat
            dkb_p = _mm(dp, kv)
            dk_p = _mm_tn(dp, kb)
            dgc = dgc + jnp.sum(dekb * ekb, axis=1, keepdims=True)
            dgc = dgc + jnp.sum(z, axis=1, keepdims=True) - jnp.sum(z.T, axis=1, keepdims=True)
            dgc = dgc + jnp.where(rowi == C - 1, dgl, 0.0)
            upper = (t["ri"] <= t["ci"]).astype(F32)
            dg_b = _mm3(upper, jnp.broadcast_to(dgc, (C, LANES)))
            yield
            dkb = dkb_p + e_c * dekb
            dk_ref[rows, sl] = dk + dk_a + dk_p + beta_c * dkb
            dbeta = jnp.sum(dkb * kv, axis=1, keepdims=True) + jnp.sum(dvb * vv, axis=1, keepdims=True)
            dg = dg_b[:, 0:1]
            a_raw = bav[:, H + h:H + h + 1]
            d_braw = dbeta * beta_c * (1.0 - beta_c)
            d_araw = dg * (-jnp.exp(al_ref[0, h])) * _sigmoid(a_raw + dt_ref[0, h])
            acc[cc] = acc[cc] + jnp.where(lane == h, d_braw, 0.0) + jnp.where(lane == H + h, d_araw, 0.0)
            dal = jnp.sum(dg * t["g_c"], axis=0, keepdims=True)
            ddt = jnp.sum(d_araw, axis=0, keepdims=True)
            acc["dpar"] = acc["dpar"] + jnp.where(lane1 == h, dal, 0.0) + jnp.where(lane1 == H + h, ddt, 0.0)

        _interleave(head(oi, h) for oi in range(CPS) for h in range(H))
        for h in range(H):
            ds_ref[h] = state[(CPS, h)]
        for cc in range(CPS):
            dgate_ref[cc * C:(cc + 1) * C, :] = acc[cc]
        dpar_ref[0:1, :] += acc["dpar"]

        @pl.when(pl.program_id(0) == NS - 1)
        def _():
            for cp in ccps:
                cp.wait()

    rev = lambda s: NS - 1 - s
    blk = lambda: pl.BlockSpec((CPS * C, AW), lambda s: (rev(s), 0))
    smem = pl.BlockSpec(memory_space=pltpu.SMEM)
    any_spec = pl.BlockSpec(memory_space=pl.ANY)
    res = pl.pallas_call(
        body, name="delta_bwd", grid=(NS,),
        in_specs=[smem, smem, blk(), blk(), blk(),
                  pl.BlockSpec((CPS * C, LANES), lambda s: (rev(s), 0)),
                  pl.BlockSpec((CPS, 32, C), lambda s: (rev(s), 0, 0)),
                  pl.BlockSpec((CPS * C, LANES), lambda s: (rev(s), 0)),
                  blk(),
                  pl.BlockSpec((CPS, H, D, D), lambda s: (rev(s), 0, 0, 0)),
                  pl.BlockSpec((CPS, H, C, C), lambda s: (rev(s), 0, 0, 0)),
                  blk()] + [any_spec] * nc,
        out_specs=[blk(), blk(), blk(),
                   pl.BlockSpec((CPS * C, LANES), lambda s: (rev(s), 0)),
                   pl.BlockSpec((8, LANES), lambda s: (0, 0))] + [any_spec] * nc,
        out_shape=[S((T, AW), F32), S((T, AW), F32), S((T, AW), F32),
                   S((T, LANES), F32), S((8, LANES), F32)] + [S((3,) + a.shape[1:], a.dtype) for a in carry],
        scratch_shapes=[pltpu.VMEM((H, D, D), F32),
                        pltpu.SemaphoreType.DMA((max(nc, 1), 3)), pltpu.SemaphoreType.DMA((max(nc, 1), 3))],
        compiler_params=_cp(ARB),
    )(a_log, dt_bias, q, k, v, gcol, grow, ba, vnew, ssave, asave, d_o, *carry)
    return res[:5], res[5:]


def _ln_stats(xv):
    mu = jnp.mean(xv, axis=-1, keepdims=True)
    xc = xv - mu
    var = jnp.mean(xc * xc, axis=-1, keepdims=True)
    rstd = lax.rsqrt(var + EPS)
    return xc * rstd, rstd


def _mix_fwd(o, proj, head_norm_w, ln_w, ln_b, w_sp, bs_t, H, D, G, P):
    T = o.shape[0]
    AW, BW = H * D, G * P
    MIX = AW + BW
    nb = AW // BW if AW % BW == 0 else None
    assert nb == 1, "group widths must match the projection column blocks"
    cb = 3

    def body(o_ref, za_ref, ub_ref, vb_ref, zb_ref, hw_ref, lw_ref, lb_ref, w_ref, bs_ref, out_ref):
        hw = hw_ref[...]
        for h in range(H):
            sl = slice(h * D, (h + 1) * D)
            oh = o_ref[:, sl]
            rs = lax.rsqrt(jnp.mean(oh * oh, axis=-1, keepdims=True) + EPS)
            out_ref[:, sl] = (oh * rs * hw * _silu(za_ref[:, sl])).astype(BF16)
        xhat, _ = _ln_stats(vb_ref[...])
        vn = xhat * lw_ref[...] + lb_ref[...]
        ri = lax.broadcasted_iota(jnp.int32, (P, P), 0)
        ci = lax.broadcasted_iota(jnp.int32, (P, P), 1)
        bsv = bs_ref[...]
        for g in range(G):
            sl = slice(g * P, (g + 1) * P)
            wm = jnp.where(ri >= ci, w_ref[g], 0.0)
            s = _mm(wm, vn[:, sl]) + bsv[:, g:g + 1]
            out_ref[:, AW + g * P:AW + (g + 1) * P] = (ub_ref[:, sl] * s * _silu(zb_ref[:, sl])).astype(BF16)

    row = lambda w: pl.BlockSpec((1, w), lambda i: (0, 0))
    return pl.pallas_call(
        body, name="mix_fwd", grid=(T // P,),
        in_specs=[pl.BlockSpec((P, AW), lambda i: (i, 0)),
                  pl.BlockSpec((P, AW), lambda i: (i, cb)),
                  pl.BlockSpec((P, BW), lambda i: (i, cb + 1)),
                  pl.BlockSpec((P, BW), lambda i: (i, cb + 2)),
                  pl.BlockSpec((P, BW), lambda i: (i, cb + 3)),
                  row(D), row(BW), row(BW),
                  pl.BlockSpec((G, P, P), lambda i: (0, 0, 0)),
                  pl.BlockSpec((P, G), lambda i: (0, 0))],
        out_specs=pl.BlockSpec((P, MIX), lambda i: (i, 0)),
        out_shape=S((T, MIX), BF16),
        compiler_params=_cp(ARB),
    )(o, proj, proj, proj, proj, head_norm_w, ln_w, ln_b, w_sp, bs_t)


def _mix_bwd(d_ocat, o, proj, head_norm_w, ln_w, ln_b, w_sp, bs_t, H, D, G, P, carry):
    T = o.shape[0]
    AW, BW = H * D, G * P
    MIX = AW + BW
    cb = 3
    nc = len(carry)

    def body(dc_ref, o_ref, za_ref, ub_ref, vb_ref, zb_ref, hw_ref, lw_ref, lb_ref, w_ref, bs_ref, *rest):
        cins = rest[:nc]
        do_ref, dmain_ref, dhw_ref, dln_ref, dw_ref, dbs_ref = rest[nc:nc + 6]
        couts = rest[nc + 6:2 * nc + 6]
        dvn_ref, drest_ref, out_sems, csend, crecv = rest[2 * nc + 6:]
        i = pl.program_id(0)
        slot = lax.rem(i, 2)
        ccps = _sibling_copies(cins, couts, csend, crecv)

        def out_copy(step, s):
            return pltpu.make_async_copy(
                drest_ref.at[s], dmain_ref.at[pl.ds(step * P, P), pl.ds(cb * AW, AW + 3 * BW)], out_sems.at[s])

        @pl.when(i == 0)
        def _():
            dhw_ref[...] = jnp.zeros_like(dhw_ref)
            dln_ref[...] = jnp.zeros_like(dln_ref)
            dw_ref[...] = jnp.zeros_like(dw_ref)
            dbs_ref[...] = jnp.zeros_like(dbs_ref)
            for cp in ccps:
                cp.start()

        @pl.when(i >= 2)
        def _():
            out_copy(i - 2, slot).wait()

        hw = hw_ref[...]
        dhw = jnp.zeros((1, D), F32)
        for h in range(H):
            sl = slice(h * D, (h + 1) * D)
            oh = o_ref[:, sl]
            za = za_ref[:, sl]
            doa = dc_ref[:, sl]
            rs = lax.rsqrt(jnp.mean(oh * oh, axis=-1, keepdims=True) + EPS)
            xh = oh * rs
            d_on = doa * _silu(za)
            drest_ref[slot, :, sl] = (doa * (xh * hw) * _dsilu(za)).astype(BF16)
            dhw = dhw + jnp.sum(d_on * xh, axis=0, keepdims=True)
            dxh = d_on * hw
            do_ref[:, sl] = rs * (dxh - xh * jnp.mean(dxh * xh, axis=-1, keepdims=True))
        dhw_ref[0:1, :] += dhw

        xhat, rstd = _ln_stats(vb_ref[...])
        lw = lw_ref[...]
        vn = xhat * lw + lb_ref[...]
        ri = lax.broadcasted_iota(jnp.int32, (P, P), 0)
        ci = lax.broadcasted_iota(jnp.int32, (P, P), 1)
        lane = lax.broadcasted_iota(jnp.int32, (P, LANES), 1)
        bsv = bs_ref[...]
        dbs = jnp.zeros((P, LANES), F32)
        for g in range(G):
            sl = slice(g * P, (g + 1) * P)
            wm = jnp.where(ri >= ci, w_ref[g], 0.0)
            vng = vn[:, sl]
            s = _mm(wm, vng) + bsv[:, g:g + 1]
            dob = dc_ref[:, AW + g * P:AW + (g + 1) * P]
            ub = ub_ref[:, sl]
            zb = zb_ref[:, sl]
            szb = _silu(zb)
            drest_ref[slot, :, AW + g * P:AW + (g + 1) * P] = (dob * s * szb).astype(BF16)
            drest_ref[slot, :, AW + 2 * BW + g * P:AW + 2 * BW + (g + 1) * P] = (
                dob * ub * s * _dsilu(zb)).astype(BF16)
            ds = dob * ub * szb
            dvn_ref[:, sl] = _mm_tn(wm, ds)
            dw_ref[g] += jnp.where(ri >= ci, _mm_nt(ds, vng), 0.0)
            dbs = dbs + jnp.where(lane == g, jnp.sum(ds, axis=1, keepdims=True), 0.0)
        dbs_ref[...] += dbs
        dvn = dvn_ref[...]
        dln_ref[0:1, :] += jnp.sum(dvn * xhat, axis=0, keepdims=True)
        dln_ref[1:2, :] += jnp.sum(dvn, axis=0, keepdims=True)
        dxh = dvn * lw
        dvb = rstd * (dxh - jnp.mean(dxh, axis=-1, keepdims=True) - xhat * jnp.mean(dxh * xhat, axis=-1, keepdims=True))
        drest_ref[slot, :, AW + BW:AW + 2 * BW] = dvb.astype(BF16)

        out_copy(i, slot).start()

        @pl.when(i == nstep - 1)
        def _():
            out_copy(i, slot).wait()
            if nstep > 1:
                out_copy(i - 1, 1 - slot).wait()
            for cp in ccps:
                cp.wait()

    nstep = T // P
    row = lambda w: pl.BlockSpec((1, w), lambda i: (0, 0))
    any_spec = pl.BlockSpec(memory_space=pl.ANY)
    res = pl.pallas_call(
        body, name="mix_bwd", grid=(nstep,),
        in_specs=[pl.BlockSpec((P, MIX), lambda i: (i, 0)),
                  pl.BlockSpec((P, AW), lambda i: (i, 0)),
                  pl.BlockSpec((P, AW), lambda i: (i, cb)),
                  pl.BlockSpec((P, BW), lambda i: (i, cb + 1)),
                  pl.BlockSpec((P, BW), lambda i: (i, cb + 2)),
                  pl.BlockSpec((P, BW), lambda i: (i, cb + 3)),
                  row(D), row(BW), row(BW),
                  pl.BlockSpec((G, P, P), lambda i: (0, 0, 0)),
                  pl.BlockSpec((P, G), lambda i: (0, 0))] + [any_spec] * nc,
        out_specs=[pl.BlockSpec((P, AW), lambda i: (i, 0)),
                   any_spec,
                   pl.BlockSpec((8, D), lambda i: (0, 0)),
                   pl.BlockSpec((8, BW), lambda i: (0, 0)),
                   pl.BlockSpec((G, P, P), lambda i: (0, 0, 0)),
                   pl.BlockSpec((P, LANES), lambda i: (0, 0))] + [any_spec] * nc,
        out_shape=[S((T, AW), F32), S((T, cb * AW + AW + 3 * BW), BF16), S((8, D), F32), S((8, BW), F32),
                   S((G, P, P), F32), S((P, LANES), F32)] + [S(a.shape[:1] + a.shape[2:], a.dtype) for a in carry],
        scratch_shapes=[pltpu.VMEM((P, BW), F32), pltpu.VMEM((2, P, AW + 3 * BW), BF16),
                        pltpu.SemaphoreType.DMA((2,))] + _sibling_sems(carry),
        compiler_params=_cp(ARB),
    )(d_ocat, o, proj, proj, proj, proj, head_norm_w, ln_w, ln_b, w_sp, bs_t, *carry)
    return res[:6], res[6:]


def _out_proj_loss(ocat, w_out, x, target, fnw):
    T, MIX = ocat.shape
    DM = x.shape[1]
    tm = _tile(T, 256, 8)

    def body(oc_ref, w_ref, x_ref, t_ref, fw_ref, dh_ref, dhb_ref, doc_ref, loss_ref, gfw_ref):
        @pl.when(pl.program_id(0) == 0)
        def _():
            loss_ref[...] = jnp.zeros_like(loss_ref)
            gfw_ref[...] = jnp.zeros_like(gfw_ref)

        wv = w_ref[...]
        hh = x_ref[...] + jnp.dot(oc_ref[...].astype(MXU), wv.astype(MXU), preferred_element_type=F32)
        rs = lax.rsqrt(jnp.mean(hh * hh, axis=-1, keepdims=True) + EPS)
        hn = hh * rs
        fw = fw_ref[...]
        e = hn * fw - t_ref[...]
        row_loss = 0.5 * jnp.mean(e * e, axis=-1, keepdims=True)
        loss_ref[...] += jnp.sum(row_loss, axis=0, keepdims=True)
        dy = e * (1.0 / DM)
        gfw_ref[0:1, :] += jnp.sum(dy * hn, axis=0, keepdims=True)
        dhn = dy * fw
        dh = rs * (dhn - hn * jnp.mean(dhn * hn, axis=-1, keepdims=True))
        dh_ref[...] = dh
        dhb = dh.astype(BF16)
        dhb_ref[...] = dhb
        doc_ref[...] = _mm_nt(dhb, wv)

    return pl.pallas_call(
        body, name="out_proj_loss", grid=(T // tm,),
        in_specs=[pl.BlockSpec((tm, MIX), lambda i: (i, 0)),
                  pl.BlockSpec((MIX, DM), lambda i: (0, 0)),
                  pl.BlockSpec((tm, DM), lambda i: (i, 0)),
                  pl.BlockSpec((tm, DM), lambda i: (i, 0)),
                  pl.BlockSpec((1, DM), lambda i: (0, 0))],
        out_specs=[pl.BlockSpec((tm, DM), lambda i: (i, 0)),
                   pl.BlockSpec((tm, DM), lambda i: (i, 0)),
                   pl.BlockSpec((tm, MIX), lambda i: (i, 0)),
                   pl.BlockSpec((8, LANES), lambda i: (0, 0)),
                   pl.BlockSpec((8, DM), lambda i: (0, 0))],
        out_shape=[S((T, DM), F32), S((T, DM), BF16), S((T, MIX), F32), S((8, LANES), F32), S((8, DM), F32)],
        compiler_params=_cp(ARB),
    )(ocat, w_out, x, target, fnw)


def _grad_w(lhs, rhs, name):
    T, A = lhs.shape
    B = rhs.shape[1]
    ta = _tile(A, 512, LANES)
    tk = _tile(T, 1024, 16)
    nk = T // tk

    def body(l_ref, r_ref, out_ref, acc_ref):
        k = pl.program_id(1)
        part = _mm_tn(l_ref[...], r_ref[...])

        @pl.when(k == 0)
        def _():
            acc_ref[...] = part

        @pl.when(k > 0)
        def _():
            acc_ref[...] += part

        @pl.when(k == nk - 1)
        def _():
            out_ref[...] = acc_ref[...].astype(BF16)

    return pl.pallas_call(
        body, name=name, grid=(A // ta, nk),
        in_specs=[pl.BlockSpec((tk, ta), lambda i, k: (k, i)),
                  pl.BlockSpec((tk, B), lambda i, k: (k, 0))],
        out_specs=pl.BlockSpec((ta, B), lambda i, k: (i, 0)),
        out_shape=S((A, B), BF16),
        scratch_shapes=[pltpu.VMEM((ta, B), F32)],
        compiler_params=_cp(ARB, ARB),
    )(lhs, rhs)


def _grad_w_in(xn, dmain, dba):
    T, DM = xn.shape
    NM = dmain.shape[1]
    tn = _tile(NM, 1024, LANES)
    tk = _tile(T, 2048, 16)

    def body(xn_ref, dm_ref, dba_ref, gm_ref, gba_ref):
        j = pl.program_id(0)
        k = pl.program_id(1)

        @pl.when(k == 0)
        def _():
            gm_ref[...] = jnp.zeros_like(gm_ref)

        @pl.when((k == 0) & (j == 0))
        def _():
            gba_ref[...] = jnp.zeros_like(gba_ref)

        xv = xn_ref[...]
        gm_ref[...] += _mm_tn(xv, dm_ref[...])

        @pl.when(j == 0)
        def _():
            gba_ref[...] += _mm_tn(xv, dba_ref[...])

    return pl.pallas_call(
        body, name="grad_w_in", grid=(NM // tn, T // tk),
        in_specs=[pl.BlockSpec((tk, DM), lambda j, k: (k, 0)),
                  pl.BlockSpec((tk, tn), lambda j, k: (k, j)),
                  pl.BlockSpec((tk, LANES), lambda j, k: (k, 0))],
        out_specs=[pl.BlockSpec((DM, tn), lambda j, k: (0, j)),
                   pl.BlockSpec((DM, LANES), lambda j, k: (0, 0))],
        out_shape=[S((DM, NM), F32), S((DM, LANES), F32)],
        compiler_params=_cp(ARB, ARB),
    )(xn, dmain, dba)


def _dx(dmain, dba, w_main, w_ba, x, dh, norm_w, chip_sum, small):
    T, NM = dmain.shape
    DM = x.shape[1]
    tm = _tile(T, 512, 8)
    tk = _tile(NM, 1024, LANES)
    nk = NM // tk
    ni = T // tm
    half = chip_sum.shape[1] // 2
    assert half % 16 == 0
    last_step = ni * nk - 1
    relay_step = min(2 * nk, last_step)

    def body(dm_ref, dba_ref, w_ref, wba_ref, x_ref, dh_ref, nw_ref, small_ref, cs_ref,
             gx_ref, gnw_ref, gath_ref, recv_ref, stage_ref, acc_ref, csend, crecv, ssend, srecv, lsem):
        i = pl.program_id(0)
        k = pl.program_id(1)
        step = i * nk + k
        px, py, pc = _position()
        xn, yn = (1 - px, py, pc), (px, 1 - py, pc)
        upper, lower = pl.ds(0, half), pl.ds(half, half)

        def rcopy(kk, src, dst, to):
            return pltpu.make_async_remote_copy(src_ref=src, dst_ref=dst, send_sem=csend.at[kk], recv_sem=crecv.at[kk],
                                                device_id=to, device_id_type=MESH)

        diag_blk = cs_ref.at[2 * (1 - px) + (1 - py)]
        to_stage = [rcopy(2, diag_blk.at[upper], stage_ref.at[0], xn), rcopy(3, diag_blk.at[lower], stage_ref.at[1], yn)]
        direct = [rcopy(0, cs_ref.at[2 * (1 - px) + py], recv_ref.at[0], xn),
                  rcopy(1, cs_ref.at[2 * px + (1 - py)], recv_ref.at[1], yn)]
        onward = [rcopy(4, stage_ref.at[0], recv_ref.at[2].at[upper], yn),
                  rcopy(5, stage_ref.at[1], recv_ref.at[2].at[lower], xn)]
        me, small_cps = _broadcast_copies([small_ref], [gath_ref], ssend, srecv)
        small_cps = small_cps + [pltpu.make_async_copy(small_ref, gath_ref.at[me], lsem.at[0])]

        @pl.when(step == 0)
        def _():
            gnw_ref[...] = jnp.zeros_like(gnw_ref)
            for cp in to_stage + direct + small_cps:
                cp.start()

        @pl.when(step == relay_step)
        def _():
            for cp in to_stage:
                cp.wait_recv()
            for cp in onward:
                cp.start()

        @pl.when(k == 0)
        def _():
            acc_ref[...] = _mm_nt(dba_ref[...], wba_ref[...])

        acc_ref[...] += _mm_nt(dm_ref[...], w_ref[...])

        @pl.when(k == nk - 1)
        def _():
            xv = x_ref[...]
            rs = lax.rsqrt(jnp.mean(xv * xv, axis=-1, keepdims=True) + EPS)
            xh = xv * rs
            dxn = acc_ref[...]
            gnw_ref[0:1, :] += jnp.sum(dxn * xh, axis=0, keepdims=True)
            dxh = dxn * nw_ref[...]
            gx_ref[...] = dh_ref[...] + rs * (dxh - xh * jnp.mean(dxh * xh, axis=-1, keepdims=True))

        @pl.when(step == last_step)
        def _():
            for cp in to_stage:
                cp.wait_send()
            for cp in direct + onward + small_cps:
                cp.wait()

    any_spec = pl.BlockSpec(memory_space=pl.ANY)
    res = pl.pallas_call(
        body, name="dx", grid=(ni, nk),
        in_specs=[pl.BlockSpec((tm, tk), lambda i, k: (i, k)),
                  pl.BlockSpec((tm, LANES), lambda i, k: (i, 0)),
                  pl.BlockSpec((DM, tk), lambda i, k: (0, k)),
                  pl.BlockSpec((DM, LANES), lambda i, k: (0, 0)),
                  pl.BlockSpec((tm, DM), lambda i, k: (i, 0)),
                  pl.BlockSpec((tm, DM), lambda i, k: (i, 0)),
                  pl.BlockSpec((1, DM), lambda i, k: (0, 0)),
                  any_spec, any_spec],
        out_specs=[pl.BlockSpec((tm, DM), lambda i, k: (i, 0)),
                   pl.BlockSpec((8, DM), lambda i, k: (0, 0)),
                   any_spec, any_spec, any_spec],
        out_shape=[S((T, DM), F32), S((8, DM), F32), S((N_DEV,) + small.shape, F32),
                   S((3,) + chip_sum.shape[1:], chip_sum.dtype), S((2, half) + chip_sum.shape[2:], chip_sum.dtype)],
        scratch_shapes=[pltpu.VMEM((tm, DM), F32),
                        pltpu.SemaphoreType.DMA((6,)), pltpu.SemaphoreType.DMA((6,)),
                        pltpu.SemaphoreType.DMA((1, N_DEV - 1)), pltpu.SemaphoreType.DMA((1, N_DEV - 1)),
                        pltpu.SemaphoreType.DMA((1,))],
        compiler_params=_cp(ARB, ARB),
    )(dmain, dba, w_main, w_ba, x, dh, norm_w, small, chip_sum)
    return res[0], res[1], res[2], res[3]


def _sum_slots(gath):
    _, R, C = gath.shape
    tr = _tile(R, 512, 8)

    def body(g_ref, o_ref):
        tot = g_ref[0]
        for d in range(1, N_DEV):
            tot = tot + g_ref[d]
        o_ref[...] = tot

    return pl.pallas_call(
        body, name="sum_slots", grid=(R // tr,),
        in_specs=[pl.BlockSpec((N_DEV, tr, C), lambda i: (0, i, 0))],
        out_specs=pl.BlockSpec((tr, C), lambda i: (i, 0)),
        out_shape=S((R, C), F32), compiler_params=_cp(ARB),
    )(gath)


def _prep_a_bwd(dq, dk, dv, c, proj, conv_w, dmain, H, D):
    T = c.shape[0]
    AW = H * D
    C3 = 3 * AW
    tb = _tile(T, 256, 8)
    nblk = T // tb
    r8 = tb // 8
    scale = float(D) ** -0.5

    def body(dq_ref, dk_ref, dv_ref, c_ref, dqn_ref, dkn_ref, dvn_ref, cn_ref, x_ref, halo_ref, cw_ref, dmain_in_ref,
             dx_ref, gcw_ref, dc_ref):
        del dmain_in_ref
        i = pl.program_id(0)

        @pl.when(i == 0)
        def _():
            gcw_ref[...] = jnp.zeros_like(gcw_ref)

        def pointwise(rows, dq_r, dk_r, dv_r, c_r, keep):
            for h in range(H):
                for part, d_r, sc in ((0, dq_r, scale), (1, dk_r, 1.0)):
                    sl = slice(part * AW + h * D, part * AW + (h + 1) * D)
                    cv = c_r[:, sl]
                    raw = _silu(cv)
                    rs = lax.rsqrt(jnp.sum(raw * raw, axis=-1, keepdims=True) + EPS)
                    nrm = raw * rs
                    dn = d_r[:, h * D:(h + 1) * D] * sc
                    draw = rs * (dn - nrm * jnp.sum(dn * nrm, axis=-1, keepdims=True))
                    dc_ref[rows, sl] = draw * _dsilu(cv) * keep
            dc_ref[rows, 2 * AW:] = dv_r[...] * _dsilu(c_r[:, 2 * AW:]) * keep

        pointwise(slice(0, tb), dq_ref, dk_ref, dv_ref, c_ref, 1.0)
        pointwise(slice(tb, tb + 8), dqn_ref, dkn_ref, dvn_ref, cn_ref, (i < nblk - 1).astype(F32))

        cw = cw_ref[...]
        dcv = dc_ref[0:tb, :]
        dx = cw[3:4, :] * dcv
        for j in range(3):
            dx = dx + cw[j:j + 1, :] * dc_ref[3 - j:3 - j + tb, :]
        dx_ref[...] = dx.astype(BF16)
        halo = halo_ref[...] * (i > 0).astype(F32)
        xp = jnp.concatenate([halo, x_ref[...]], axis=0)
        for j in range(4):
            gcw_ref[j:j + 1, :] += jnp.sum(dcv * xp[5 + j:5 + j + tb], axis=0, keepdims=True)

    nxt = lambda i: (jnp.minimum((i + 1) * r8, T // 8 - 1), 0)
    return pl.pallas_call(
        body, name="prep_a_bwd", grid=(nblk,),
        in_specs=[pl.BlockSpec((tb, AW), lambda i: (i, 0)),
                  pl.BlockSpec((tb, AW), lambda i: (i, 0)),
                  pl.BlockSpec((tb, AW), lambda i: (i, 0)),
                  pl.BlockSpec((tb, C3), lambda i: (i, 0)),
                  pl.BlockSpec((8, AW), nxt), pl.BlockSpec((8, AW), nxt), pl.BlockSpec((8, AW), nxt),
                  pl.BlockSpec((8, C3), nxt),
                  pl.BlockSpec((tb, C3), lambda i: (i, 0)),
                  pl.BlockSpec((8, C3), lambda i: (jnp.maximum(i * r8 - 1, 0), 0)),
                  pl.BlockSpec((4, C3), lambda i: (0, 0)),
                  pl.BlockSpec(memory_space=pl.ANY)],
        out_specs=[pl.BlockSpec((tb, C3), lambda i: (i, 0)),
                   pl.BlockSpec((8, C3), lambda i: (0, 0))],
        out_shape=[S(dmain.shape, dmain.dtype), S((8, C3), F32)],
        scratch_shapes=[pltpu.VMEM((tb + 8, C3), F32)],
        input_output_aliases={11: 0},
        compiler_params=_cp(ARB),
    )(dq, dk, dv, c, dq, dk, dv, c, proj, proj, conv_w, dmain)


def _adam_math(w, g, m, v):
    m2 = ADAM_B1 * m + (1.0 - ADAM_B1) * g
    v2 = ADAM_B2 * v + (1.0 - ADAM_B2) * (g * g)
    m_hat = m2 / (1.0 - ADAM_B1 ** ADAM_STEP)
    v_hat = v2 / (1.0 - ADAM_B2 ** ADAM_STEP)
    delta = -ADAM_LR * (m_hat / (jnp.sqrt(v_hat) + ADAM_EPS) + ADAM_WD * w)
    return delta, m2, v2


def _pair_sum(blocks, recv, core, name):
    K, _, R, C = blocks.shape
    tr = _tile(R, 256, 16)

    def body(core_ref, a_ref, b_ref, o_ref):
        del core_ref
        o_ref[0] = (a_ref[0, 0].astype(F32) + b_ref[0].astype(F32)).astype(BF16)

    spec = lambda: pl.BlockSpec((1, tr, C), lambda k, i, core_ref: (k, i, 0))
    return pl.pallas_call(
        body, name=name,
        grid_spec=pltpu.PrefetchScalarGridSpec(
            num_scalar_prefetch=1, grid=(K, R // tr),
            in_specs=[pl.BlockSpec((1, 1, tr, C), lambda k, i, core_ref: (k, core_ref[0], i, 0)), spec()],
            out_specs=spec()),
        out_shape=S((K, R, C), BF16), compiler_params=_cp(ARB, ARB),
    )(core, blocks, recv)


def _sum_adam(chip_sums, recv, w, m, v, chip, name, transposed=False):
    R, C = chip_sums.shape[1:]
    tr = _tile(R, 256, 16)

    def body(chip_ref, own_ref, r_ref, w_ref, m_ref, v_ref, g_ref, d_ref, m2_ref, v2_ref):
        del chip_ref
        g = own_ref[0].astype(F32)
        for j in range(3):
            g = g + r_ref[j].astype(F32)
        if transposed:
            g = g.T
        g_ref[...] = g
        d_ref[...], m2_ref[...], v2_ref[...] = _adam_math(w_ref[...], g, m_ref[...], v_ref[...])

    if transposed:
        spec = lambda: pl.BlockSpec((C, tr), lambda i, chip_ref: (0, i))
        shape = (C, R)
    else:
        spec = lambda: pl.BlockSpec((tr, C), lambda i, chip_ref: (i, 0))
        shape = (R, C)
    assert w.shape == shape
    return pl.pallas_call(
        body, name=name,
        grid_spec=pltpu.PrefetchScalarGridSpec(
            num_scalar_prefetch=1, grid=(R // tr,),
            in_specs=[pl.BlockSpec((1, tr, C), lambda i, chip_ref: (chip_ref[0], i, 0)),
                      pl.BlockSpec((3, tr, C), lambda i, chip_ref: (0, i, 0)), spec(), spec(), spec()],
            out_specs=[spec(), spec(), spec(), spec()]),
        out_shape=[S(shape, F32)] * 4, compiler_params=_cp(ARB),
    )(chip, chip_sums, recv, w, m, v)


def _adam_small(w, g, m, v):
    R, C = w.shape
    tr = _tile(R, 512, 8)

    def body(w_ref, g_ref, m_ref, v_ref, d_ref, m2_ref, v2_ref):
        d_ref[...], m2_ref[...], v2_ref[...] = _adam_math(w_ref[...], g_ref[...], m_ref[...], v_ref[...])

    spec = lambda: pl.BlockSpec((tr, C), lambda i: (i, 0))
    return pl.pallas_call(
        body, name="adam_small", grid=(R // tr,), in_specs=[spec()] * 4, out_specs=[spec()] * 3,
        out_shape=[S((R, C), F32)] * 3, compiler_params=_cp(ARB),
    )(w, g, m, v)


def _position():
    return lax.axis_index("x"), lax.axis_index("y"), lax.axis_index("c")


def _all_gather_weights(arr):
    R = arr.shape[0]
    half = R // 2
    assert half % 16 == 0

    def body(in_ref, out_ref, send_sems, recv_sems, local_sem):
        x, y, c = _position()
        me, sibling = (x, y, c), (x, y, 1 - c)
        xn, yn, diag = (1 - x, y), (x, 1 - y), (1 - x, 1 - y)
        upper, lower = pl.ds(0, half), pl.ds(half, half)

        def slot(p, rows=None):
            ref = out_ref.at[4 * p[0] + 2 * p[1] + p[2]]
            return ref if rows is None else ref.at[rows]

        def copy(kk, block, to, rows=None, src=None):
            return pltpu.make_async_remote_copy(
                src_ref=slot(block, rows) if src is None else src, dst_ref=slot(block, rows),
                send_sem=send_sems.at[kk], recv_sem=recv_sems.at[kk], device_id=to, device_id_type=MESH)

        mine = pltpu.make_async_copy(in_ref, slot(me), local_sem)
        mine.start()
        sent = [copy(0, me, sibling, src=in_ref), copy(1, me, (*xn, c), src=in_ref), copy(2, me, (*yn, c), src=in_ref)]
        for cp in sent:
            cp.start()

        def then(cps):
            for cp in cps:
                cp.start()
            sent.extend(cps)

        copy(1, (*xn, c), me).wait_recv()
        then([copy(5, (*xn, c), (*yn, c), rows=upper), copy(3, (*xn, c), sibling)])
        copy(2, (*yn, c), me).wait_recv()
        then([copy(6, (*yn, c), (*xn, c), rows=lower), copy(4, (*yn, c), sibling)])
        copy(5, (*diag, c), me, rows=upper).wait_recv()
        then([copy(7, (*diag, c), sibling, rows=upper)])
        copy(6, (*diag, c), me, rows=lower).wait_recv()
        then([copy(8, (*diag, c), sibling, rows=lower)])
        copy(0, sibling, me).wait_recv()
        copy(3, (*xn, 1 - c), me).wait_recv()
        copy(4, (*yn, 1 - c), me).wait_recv()
        copy(7, (*diag, 1 - c), me, rows=upper).wait_recv()
        copy(8, (*diag, 1 - c), me, rows=lower).wait_recv()
        for cp in sent:
            cp.wait_send()
        mine.wait()

    any_spec = pl.BlockSpec(memory_space=pl.ANY)
    return pl.pallas_call(
        body, name="all_gather_weights", in_specs=[any_spec], out_specs=any_spec,
        out_shape=S((N_DEV,) + arr.shape, arr.dtype),
        scratch_shapes=[pltpu.SemaphoreType.DMA((9,)), pltpu.SemaphoreType.DMA((9,)), pltpu.SemaphoreType.DMA],
    )(arr)


def _sibling_copies(ins, outs, send_sems, recv_sems):
    x, y, c = _position()
    return [pltpu.make_async_remote_copy(src_ref=ins[a].at[k, 1 - c], dst_ref=outs[a].at[k],
                                         send_sem=send_sems.at[a, k], recv_sem=recv_sems.at[a, k],
                                         device_id=(x, y, 1 - c), device_id_type=MESH)
            for a in range(len(ins)) for k in range(ins[a].shape[0])]


def _sibling_sems(arrs):
    shape = (max(len(arrs), 1), arrs[0].shape[0] if arrs else 1)
    return [pltpu.SemaphoreType.DMA(shape), pltpu.SemaphoreType.DMA(shape)]


def _exchange_sibling(arrs):
    na = len(arrs)

    def body(*refs):
        ins, outs = refs[:na], refs[na:2 * na]
        send_sems, recv_sems = refs[2 * na:]
        cps = _sibling_copies(ins, outs, send_sems, recv_sems)
        for cp in cps:
            cp.start()
        for cp in cps:
            cp.wait()

    any_spec = pl.BlockSpec(memory_space=pl.ANY)
    return pl.pallas_call(
        body, name="exchange_sibling", in_specs=[any_spec] * na, out_specs=[any_spec] * na,
        out_shape=[S(a.shape[:1] + a.shape[2:], a.dtype) for a in arrs],
        scratch_shapes=_sibling_sems(arrs),
    )(*arrs)


def _chip_exchange_copies(ins, outs, send_sems, recv_sems):
    x, y, c = _position()
    chips = [(1 - x, y), (x, 1 - y), (1 - x, 1 - y)]
    return [pltpu.make_async_remote_copy(
        src_ref=ins[a].at[2 * qx + qy], dst_ref=outs[a].at[j], send_sem=send_sems.at[a, j],
        recv_sem=recv_sems.at[a, j], device_id=(qx, qy, c), device_id_type=MESH)
        for a in range(len(ins)) for j, (qx, qy) in enumerate(chips)]


def _broadcast_copies(srcs, dsts, send_sems, recv_sems):
    x, y, c = _position()
    me = 4 * x + 2 * y + c
    cps = []
    for a in range(len(srcs)):
        for k in range(1, N_DEV):
            peer = (1 - x if k & 4 else x, 1 - y if k & 2 else y, 1 - c if k & 1 else c)
            cps.append(pltpu.make_async_remote_copy(
                src_ref=srcs[a], dst_ref=dsts[a].at[me], send_sem=send_sems.at[a, k - 1],
                recv_sem=recv_sems.at[a, k - 1], device_id=peer, device_id_type=MESH))
    return me, cps


def _all_reduce_small(part):
    R, C = part.shape

    def body(p_ref, out_ref, gath_ref, send_sems, recv_sems):
        me, cps = _broadcast_copies([p_ref], [gath_ref], send_sems, recv_sems)
        gath_ref[me] = p_ref[...]
        for cp in cps:
            cp.start()
        for cp in cps:
            cp.wait()
        acc = gath_ref[0]
        for d in range(1, N_DEV):
            acc = acc + gath_ref[d]
        out_ref[...] = acc

    vm = pl.BlockSpec(memory_space=pltpu.VMEM)
    return pl.pallas_call(
        body, name="all_reduce_small", in_specs=[vm], out_specs=vm, out_shape=S((R, C), F32),
        scratch_shapes=[pltpu.VMEM((N_DEV, R, C), F32), pltpu.SemaphoreType.DMA((1, N_DEV - 1)),
                        pltpu.SemaphoreType.DMA((1, N_DEV - 1))],
    )(part)


def _pack(parts):
    rows = []
    for p in parts:
        f = p.reshape(-1).astype(F32)
        pad = (-f.shape[0]) % (8 * LANES)
        rows.append(jnp.pad(f, (0, pad)).reshape(-1, LANES))
    return jnp.concatenate(rows, axis=0)


def _unpack(buf, shapes):
    out, r = [], 0
    for shp in shapes:
        n = 1
        for s in shp:
            n *= s
        nr = -(-n // (8 * LANES)) * 8
        out.append(buf[r:r + nr].reshape(-1)[:n].reshape(shp))
        r += nr
    return out


def kernel(x, norm_w, w_in, conv_w, a_log, dt_bias, head_norm_w, sgu_ln_w, sgu_ln_b, w_spatial, b_spatial, w_out, final_norm_w, loss_target, m_norm_w, m_w_in, m_conv_w, m_a_log, m_dt_bias, m_head_norm_w, m_sgu_ln_w, m_sgu_ln_b, m_w_spatial, m_b_spatial, m_w_out, m_final_norm_w, v_norm_w, v_w_in, v_conv_w, v_a_log, v_dt_bias, v_head_norm_w, v_sgu_ln_w, v_sgu_ln_b, v_w_spatial, v_b_spatial, v_w_out, v_final_norm_w):
    T, DM = x.shape[1], x.shape[2]
    H, D = a_log.shape[1], head_norm_w.shape[1]
    G, P = w_spatial.shape[1], w_spatial.shape[2]
    AW, BW = H * D, G * P
    MIX = AW + BW
    WD = w_in.shape[2]
    IN = N_DEV * WD
    RO = w_out.shape[1]
    CW = conv_w.shape[2]
    sizes = (3 * AW, AW, H, H, BW, BW, BW)
    assert sum(sizes) == IN and 2 * H <= LANES and 3 * H <= 32 and N_DEV * RO == MIX and N_DEV * CW == 3 * AW
    offs = [0]
    for s in sizes:
        offs.append(offs[-1] + s)
    px, py, pc = _position()
    dev = 4 * px + 2 * py + pc
    chip = 2 * px + py

    x2, tgt = x[0], loss_target[0]

    g_win = _all_gather_weights(_cast_bf16_t(w_in[0].T, "cast_w_in"))
    w_main, w_ba = _relayout_w(g_win, offs[2], offs[4])
    alog_row = jnp.pad(a_log, ((0, 0), (H, LANES - 2 * H)))
    dtb_row = jnp.pad(dt_bias, ((0, 0), (H, LANES - 2 * H)))
    bs_t = b_spatial[0].T

    proj, ba, xn, (g_wout, g_conv) = _in_proj(x2, norm_w, w_main, w_ba, [_cast_bf16(w_out[0], "cast_w_out"), conv_w[0]])
    w_out_full = g_wout.reshape(MIX, DM)
    conv_full = g_conv.transpose(1, 0, 2).reshape(4, 3 * AW)
    q, k, v, c, gcol, grow = _prep_a_fwd(proj, ba, conv_full, alog_row, dtb_row, H, D)
    o, vnew, ssave, asave = _delta_fwd(q, k, v, gcol, grow, H, D)
    ocat = _mix_fwd(o, proj, head_norm_w, sgu_ln_w, sgu_ln_b, w_spatial[0], bs_t, H, D, G, P)
    dh, dh_bf, d_ocat, loss_acc, g_fnw = _out_proj_loss(ocat, w_out_full, x2, tgt, final_norm_w.reshape(1, DM))
    loss = lax.psum(loss_acc[0, 0], AXES)

    core_idx = jnp.reshape(pc, (1,)).astype(jnp.int32)
    chip_idx = jnp.reshape(chip, (1,)).astype(jnp.int32)
    g_wout_blocks = _grad_w(ocat, dh_bf, "grad_w_out").reshape(4, 2, RO, DM)
    (d_o, dmain, g_hnw, g_ln, g_wsp, g_bs_t), (sib_wout,) = _mix_bwd(
        d_ocat, o, proj, head_norm_w, sgu_ln_w, sgu_ln_b, w_spatial[0], bs_t, H, D, G, P, [g_wout_blocks])
    chip_wout = _pair_sum(g_wout_blocks, sib_wout, core_idx, "pair_sum_w_out")
    (dq, dk, dv, dgate, dpar), (recv_wout,) = _delta_bwd(
        q, k, v, gcol, grow, ba, vnew, ssave, asave, d_o, a_log, dt_bias, H, D, [chip_wout])
    dmain, g_conv_part = _prep_a_bwd(dq, dk, dv, c, proj, conv_full, dmain, H, D)
    dba = dgate.astype(BF16)
    g_main, g_ba = _grad_w_in(xn, dmain, dba)
    g_win_blocks = _relayout_g(g_main, g_ba, WD, offs[2], offs[4]).reshape(4, 2, DM, WD)
    (sib_win,) = _exchange_sibling([g_win_blocks])
    chip_win = _pair_sum(g_win_blocks, sib_win, core_idx, "pair_sum_w_in")
    small_shapes = [a_log.shape, dt_bias.shape, head_norm_w.shape, sgu_ln_w.shape, sgu_ln_b.shape,
                    w_spatial.shape, b_spatial.shape, final_norm_w.shape]
    parts = [dpar[0, :H], dpar[0, H:2 * H], g_hnw[0], g_ln[0], g_ln[1], g_wsp, g_bs_t[:, :G].T, g_fnw[0],
             g_conv_part[:4]]
    grad_x, g_nw, small_gath, recv_win = _dx(dmain, dba, w_main, w_ba, x2, dh, norm_w, chip_win, _pack(parts))
    red = _sum_slots(small_gath)
    grad_w_in, delta_w_in, new_m_w_in, new_v_w_in = _sum_adam(
        chip_win, recv_win, w_in[0].T, m_w_in[0].T, v_w_in[0].T, chip_idx, "sum_adam_w_in", transposed=True)
    grad_w_out, delta_w_out, new_m_w_out, new_v_w_out = _sum_adam(
        chip_wout, recv_wout, w_out[0], m_w_out[0], v_w_out[0], chip_idx, "sum_adam_w_out")
    red_nw = _all_reduce_small(_pack([g_nw[0]]))
    grads_small = _unpack(red_nw, [norm_w.shape]) + _unpack(red, small_shapes + [(4, 3 * AW)])
    g_conv_full = grads_small.pop()
    grad_conv = lax.dynamic_slice_in_dim(g_conv_full, dev * CW, CW, axis=1)[None]
    small_w = [norm_w, a_log, dt_bias, head_norm_w, sgu_ln_w, sgu_ln_b, w_spatial, b_spatial, final_norm_w, conv_w]
    small_m = [m_norm_w, m_a_log, m_dt_bias, m_head_norm_w, m_sgu_ln_w, m_sgu_ln_b, m_w_spatial, m_b_spatial,
               m_final_norm_w, m_conv_w]
    small_v = [v_norm_w, v_a_log, v_dt_bias, v_head_norm_w, v_sgu_ln_w, v_sgu_ln_b, v_w_spatial, v_b_spatial,
               v_final_norm_w, v_conv_w]
    small_g = grads_small + [grad_conv]
    shapes10 = [w.shape for w in small_w]
    d_p, m_p, v_p = _adam_small(_pack(small_w), _pack(small_g), _pack(small_m), _pack(small_v))
    d_s, m_s, v_s = _unpack(d_p, shapes10), _unpack(m_p, shapes10), _unpack(v_p, shapes10)

    def order(small, win, wout):
        return [small[0], win.T[None], small[9], small[1], small[2], small[3], small[4], small[5], small[6], small[7],
                wout[None], small[8]]

    grads = order(small_g, grad_w_in, grad_w_out)
    deltas = order(d_s, delta_w_in, delta_w_out)
    new_m = order(m_s, new_m_w_in, new_m_w_out)
    new_v = order(v_s, new_v_w_in, new_v_w_out)
    return (loss, grad_x[None], *grads, *deltas, *new_m, *new_v)
```

```python
import functools

import jax
import jax.numpy as jnp
from jax import lax
from jax.experimental import pallas as pl
from jax.experimental.pallas import tpu as pltpu

F32 = jnp.float32
BF16 = jnp.bfloat16
MXU = jnp.bfloat16
HI = lax.Precision.HIGHEST
EPS = 1e-6
CHUNK_A = 64
LANES = 128
MESH = pl.DeviceIdType.MESH
AXES = ("x", "y", "c")
N_DEV = 8

ADAM_LR = 0.001
ADAM_B1 = 0.9
ADAM_B2 = 0.999
ADAM_EPS = 1e-08
ADAM_WD = 0.01
ADAM_STEP = 10

S = jax.ShapeDtypeStruct
ARB = "arbitrary"


def _cp(*sem):
    return pltpu.CompilerParams(dimension_semantics=tuple(sem), vmem_limit_bytes=56 * 1024 * 1024)


def _tile(n, cap, mult):
    best = None
    t = mult
    while t <= min(n, cap):
        if n % t == 0:
            best = t
        t += mult
    return best if best is not None else n


def _mm(a, b):
    return jnp.dot(a.astype(MXU), b.astype(MXU), preferred_element_type=F32)


def _mm_nt(a, b):
    return lax.dot_general(a.astype(MXU), b.astype(MXU), (((1,), (1,)), ((), ())), preferred_element_type=F32)


def _mm_tn(a, b):
    return lax.dot_general(a.astype(MXU), b.astype(MXU), (((0,), (0,)), ((), ())), preferred_element_type=F32)


def _mmh(a, b):
    return jnp.dot(a, b, precision=HI, preferred_element_type=F32)


def _mmh_tn(a, b):
    return lax.dot_general(a, b, (((0,), (0,)), ((), ())), precision=HI, preferred_element_type=F32)


def _sigmoid(x):
    return 1.0 / (1.0 + jnp.exp(-x))


def _silu(x):
    return x * _sigmoid(x)


def _dsilu(x):
    s = _sigmoid(x)
    return s * (1.0 + x * (1.0 - s))


def _softplus(x):
    return jnp.maximum(x, 0.0) + jnp.log(1.0 + jnp.exp(-jnp.abs(x)))


def _pieces(wd, gate_lo, gate_hi, total):
    out = []
    for d in range(N_DEV):
        lo, hi = d * wd, (d + 1) * wd
        for dest, a, b, shift in (("main", 0, gate_lo, 0), ("gate", gate_lo, gate_hi, -gate_lo),
                                  ("main", gate_hi, total, gate_lo - gate_hi)):
            s0, s1 = max(lo, a), min(hi, b)
            if s0 < s1:
                out.append((d, s0 - lo, s1 - lo, dest, s0 + shift))
    return out


def _cast_bf16(a, name):
    R, C = a.shape
    tr = _tile(R, 256, 16)

    def body(a_ref, o_ref):
        o_ref[...] = a_ref[...].astype(BF16)

    spec = pl.BlockSpec((tr, C), lambda i: (i, 0))
    return pl.pallas_call(body, name=name, grid=(R // tr,), in_specs=[spec], out_specs=spec,
                          out_shape=S((R, C), BF16), compiler_params=_cp(ARB))(a)


def _cast_bf16_t(a_t, name):
    C, R = a_t.shape
    tr = _tile(R, 256, LANES)

    def body(a_ref, o_ref):
        o_ref[...] = a_ref[...].T.astype(BF16)

    return pl.pallas_call(body, name=name, grid=(R // tr,), in_specs=[pl.BlockSpec((C, tr), lambda i: (0, i))],
                          out_specs=pl.BlockSpec((tr, C), lambda i: (i, 0)),
                          out_shape=S((R, C), BF16), compiler_params=_cp(ARB))(a_t)


def _relayout_w(g_win, gate_lo, gate_hi):
    _, DM, WD = g_win.shape
    total = N_DEV * WD
    NM = total - (gate_hi - gate_lo)
    tr = _tile(DM, 256, 16)
    plan = _pieces(WD, gate_lo, gate_hi, total)

    def body(g_ref, main_ref, gate_ref):
        gate_ref[...] = jnp.zeros_like(gate_ref)
        for d, s0, s1, dest, c0 in plan:
            dst = main_ref if dest == "main" else gate_ref
            dst[:, c0:c0 + (s1 - s0)] = g_ref[d, :, s0:s1]

    return pl.pallas_call(
        body, name="relayout_w", grid=(DM // tr,),
        in_specs=[pl.BlockSpec((N_DEV, tr, WD), lambda i: (0, i, 0))],
        out_specs=[pl.BlockSpec((tr, NM), lambda i: (i, 0)), pl.BlockSpec((tr, LANES), lambda i: (i, 0))],
        out_shape=[S((DM, NM), g_win.dtype), S((DM, LANES), g_win.dtype)],
        compiler_params=_cp(ARB),
    )(g_win)


def _in_proj(x, norm_w, w_main, w_ba, shards):
    T, DM = x.shape
    NM = w_main.shape[1]
    tm = _tile(T, 1024, 8)
    tn = _tile(NM, 1024, LANES)
    ni, nj = T // tm, NM // tn
    ns = len(shards)

    def body(x_ref, nw_ref, w_ref, wba_ref, *rest):
        srcs = rest[:ns]
        proj_ref, ba_ref, xn_ref = rest[ns:ns + 3]
        gath = rest[ns + 3:2 * ns + 3]
        send_sems, recv_sems, local_sems = rest[2 * ns + 3:]
        i = pl.program_id(0)
        me, cps = _broadcast_copies(srcs, gath, send_sems, recv_sems)
        cps = cps + [pltpu.make_async_copy(srcs[a], gath[a].at[me], local_sems.at[a]) for a in range(ns)]

        @pl.when((i == 0) & (pl.program_id(1) == 0))
        def _():
            for cp in cps:
                cp.start()

        @pl.when((i == ni - 1) & (pl.program_id(1) == nj - 1))
        def _():
            for cp in cps:
                cp.wait()

        @pl.when(pl.program_id(1) == 0)
        def _():
            xv = x_ref[...]
            r = lax.rsqrt(jnp.mean(xv * xv, axis=-1, keepdims=True) + EPS)
            xn = (xv * r * nw_ref[...]).astype(BF16)
            xn_ref[...] = xn
            ba_ref[...] = jnp.dot(xn.astype(MXU), wba_ref[...].astype(MXU), preferred_element_type=F32)

        proj_ref[...] = jnp.dot(xn_ref[...].astype(MXU), w_ref[...].astype(MXU), preferred_element_type=F32)

    any_spec = pl.BlockSpec(memory_space=pl.ANY)
    res = pl.pallas_call(
        body, name="in_proj", grid=(ni, nj),
        in_specs=[pl.BlockSpec((tm, DM), lambda i, j: (i, 0)),
                  pl.BlockSpec((1, DM), lambda i, j: (0, 0)),
                  pl.BlockSpec((DM, tn), lambda i, j: (0, j)),
                  pl.BlockSpec((DM, LANES), lambda i, j: (0, 0))] + [any_spec] * ns,
        out_specs=[pl.BlockSpec((tm, tn), lambda i, j: (i, j)),
                   pl.BlockSpec((tm, LANES), lambda i, j: (i, 0)),
                   pl.BlockSpec((tm, DM), lambda i, j: (i, 0))] + [any_spec] * ns,
        out_shape=[S((T, NM), F32), S((T, LANES), F32), S((T, DM), BF16)]
        + [S((N_DEV,) + a.shape, a.dtype) for a in shards],
        scratch_shapes=[pltpu.SemaphoreType.DMA((ns, N_DEV - 1)), pltpu.SemaphoreType.DMA((ns, N_DEV - 1)),
                        pltpu.SemaphoreType.DMA((ns,))],
        compiler_params=_cp(ARB, ARB),
    )(x, norm_w, w_main, w_ba, *shards)
    return res[0], res[1], res[2], res[3:]


def _prep_a_fwd(proj, ba, conv_w, alog_row, dtb_row, H, D):
    T = proj.shape[0]
    AW = H * D
    C3 = 3 * AW
    tb = _tile(T, 256, CHUNK_A)
    nch = tb // CHUNK_A
    nblk = T // tb
    scale = float(D) ** -0.5

    def body(x_ref, halo_ref, ba_ref, cw_ref, al_ref, dt_ref, q_ref, k_ref, v_ref, c_ref, gcol_ref, grow_ref):
        i = pl.program_id(0)
        xv = x_ref[...]
        halo = halo_ref[...] * (i > 0).astype(F32)
        xp = jnp.concatenate([halo, xv], axis=0)
        cw = cw_ref[...]
        c = cw[0:1, :] * xp[5:5 + tb]
        for j in range(1, 4):
            c = c + cw[j:j + 1, :] * xp[5 + j:5 + j + tb]
        c_ref[...] = c
        a = _silu(c)
        for h in range(H):
            qh = a[:, h * D:(h + 1) * D]
            kh = a[:, AW + h * D:AW + (h + 1) * D]
            qr = lax.rsqrt(jnp.sum(qh * qh, axis=-1, keepdims=True) + EPS)
            kr = lax.rsqrt(jnp.sum(kh * kh, axis=-1, keepdims=True) + EPS)
            q_ref[:, h * D:(h + 1) * D] = qh * (qr * scale)
            k_ref[:, h * D:(h + 1) * D] = kh * kr
        v_ref[...] = a[:, 2 * AW:]

        bav = ba_ref[...]
        lane = lax.broadcasted_iota(jnp.int32, (tb, LANES), 1)
        beta = _sigmoid(bav)
        g = -jnp.exp(al_ref[...]) * _softplus(bav + dt_ref[...])
        gates = jnp.where(lane < H, beta, jnp.where(lane < 2 * H, g, 0.0))
        ri = lax.broadcasted_iota(jnp.int32, (CHUNK_A, CHUNK_A), 0)
        ci = lax.broadcasted_iota(jnp.int32, (CHUNK_A, CHUNK_A), 1)
        tri = (ri >= ci).astype(F32)
        lane_c = lax.broadcasted_iota(jnp.int32, (CHUNK_A, LANES), 1)
        for cc in range(nch):
            gch = gates[cc * CHUNK_A:(cc + 1) * CHUNK_A]
            gc = pltpu.roll(_mmh(tri, gch), H, 1)
            full = jnp.where(lane_c < 2 * H, gch, jnp.where(lane_c < 3 * H, gc, 0.0))
            gcol_ref[cc * CHUNK_A:(cc + 1) * CHUNK_A, :] = full
            grow_ref[cc] = full.T[0:32, :]

    return pl.pallas_call(
        body, name="prep_a_fwd", grid=(nblk,),
        in_specs=[pl.BlockSpec((tb, C3), lambda i: (i, 0)),
                  pl.BlockSpec((8, C3), lambda i: (jnp.maximum(i * (tb // 8) - 1, 0), 0)),
                  pl.BlockSpec((tb, LANES), lambda i: (i, 0)),
                  pl.BlockSpec((4, C3), lambda i: (0, 0)),
                  pl.BlockSpec((1, LANES), lambda i: (0, 0)),
                  pl.BlockSpec((1, LANES), lambda i: (0, 0))],
        out_specs=[pl.BlockSpec((tb, AW), lambda i: (i, 0)),
                   pl.BlockSpec((tb, AW), lambda i: (i, 0)),
                   pl.BlockSpec((tb, AW), lambda i: (i, 0)),
                   pl.BlockSpec((tb, C3), lambda i: (i, 0)),
                   pl.BlockSpec((tb, LANES), lambda i: (i, 0)),
                   pl.BlockSpec((nch, 32, CHUNK_A), lambda i: (i, 0, 0))],
        out_shape=[S((T, AW), F32), S((T, AW), F32), S((T, AW), F32), S((T, C3), F32),
                   S((T, LANES), F32), S((T // CHUNK_A, 32, CHUNK_A), F32)],
        compiler_params=_cp(ARB),
    )(proj, proj, ba, conv_w, alog_row, dtb_row)


_NN = (((1,), (0,)), ((), ()))
_TN = (((0,), (0,)), ((), ()))


def _split(a):
    hi = a.astype(BF16)
    return hi, (a - hi.astype(F32)).astype(BF16)


def _mm3(a, b, dims=_NN):
    ah, al = a if isinstance(a, tuple) else _split(a)
    bh, bl = b if isinstance(b, tuple) else _split(b)
    dg = lambda p, r: lax.dot_general(p, r, dims, preferred_element_type=F32)
    return dg(ah, bh) + (dg(ah, bl) + dg(al, bh))


def _interleave(gens):
    gens = list(gens)
    while gens:
        alive = []
        for g in gens:
            try:
                next(g)
                alive.append(g)
            except StopIteration:
                pass
        gens = alive


def _chunk_terms(q, k, v, gcolv, growv, h, H):
    C = CHUNK_A
    beta_c = gcolv[:, h:h + 1]
    g_c = gcolv[:, H + h:H + h + 1]
    gc_c = gcolv[:, 2 * H + h:2 * H + h + 1]
    gc_r = growv[2 * H + h:2 * H + h + 1, :]
    ri = lax.broadcasted_iota(jnp.int32, (C, C), 0)
    ci = lax.broadcasted_iota(jnp.int32, (C, C), 1)
    incl = ri >= ci
    strict = ri > ci
    kb = k * beta_c
    vb = v * beta_c
    p_raw = _mm_nt(kb, k)
    qk_raw = _mm_nt(q, k)
    gam = jnp.where(incl, jnp.exp(jnp.where(incl, gc_c - gc_r, 0.0)), 0.0)
    e_c = jnp.exp(gc_c)
    gl = gc_r[:, C - 1:C]
    edec = jnp.exp(gl - gc_c)
    yield
    lmat = jnp.where(strict, p_raw * gam, 0.0)
    attn = jnp.where(incl, qk_raw * gam, 0.0)
    return dict(beta_c=beta_c, g_c=g_c, gc_c=gc_c, gc_r=gc_r, incl=incl, strict=strict, gam=gam, e_c=e_c,
                kb=kb, vb=vb, lmat=lmat, attn=attn, gl=gl, edec=edec, ri=ri, ci=ci)


def _inv_unit_lower(lmat):
    C = lmat.shape[0]
    ri = lax.broadcasted_iota(jnp.int32, (C, C), 0)
    ci = lax.broadcasted_iota(jnp.int32, (C, C), 1)
    eye = (ri == ci).astype(F32)
    x = -lmat
    a = eye + x
    n = 1
    while 2 * n < C:
        xs = _split(x)
        x = _mm3(xs, xs)
        yield
        a = a + _mm3(a, x)
        n *= 2
    yield
    return a


def _delta_fwd(q, k, v, gcol, grow, H, D):
    T = q.shape[0]
    C = CHUNK_A
    N = T // C
    AW = H * D
    CPS = 2 if N % 2 == 0 else 1

    def body(q_ref, k_ref, v_ref, gcol_ref, grow_ref, o_ref, vn_ref, ssave_ref, asave_ref, s_ref):
        @pl.when(pl.program_id(0) == 0)
        def _():
            s_ref[...] = jnp.zeros_like(s_ref)

        state = {(0, h): s_ref[h] for h in range(H)}

        def head(cc, h):
            rows = slice(cc * C, (cc + 1) * C)
            sl = slice(h * D, (h + 1) * D)
            qv, kv, vv = q_ref[rows, sl], k_ref[rows, sl], v_ref[rows, sl]
            t = yield from _chunk_terms(qv, kv, vv, gcol_ref[rows, :], grow_ref[cc], h, H)
            a = yield from _inv_unit_lower(t["lmat"])
            asave_ref[cc, h] = a
            while (cc, h) not in state:
                yield
            st = state[(cc, h)]
            ssave_ref[cc, h] = st
            ks = _mm(t["kb"] * t["e_c"], st)
            o_inter = _mm(qv * t["e_c"], st)
            yield
            v_new = _mm3(a, t["vb"] - ks)
            yield
            vn_ref[rows, sl] = v_new
            o_intra = _mm(t["attn"], v_new)
            s_upd = _mm_tn(kv * t["edec"], v_new)
            yield
            o_ref[rows, sl] = o_inter + o_intra
            state[(cc + 1, h)] = st * jnp.exp(t["gl"]) + s_upd

        _interleave(head(cc, h) for cc in range(CPS) for h in range(H))
        for h in range(H):
            s_ref[h] = state[(CPS, h)]

    blk = lambda: pl.BlockSpec((CPS * C, AW), lambda n: (n, 0))
    return pl.pallas_call(
        body, name="delta_fwd", grid=(N // CPS,),
        in_specs=[blk(), blk(), blk(),
                  pl.BlockSpec((CPS * C, LANES), lambda n: (n, 0)),
                  pl.BlockSpec((CPS, 32, C), lambda n: (n, 0, 0))],
        out_specs=[blk(), blk(),
                   pl.BlockSpec((CPS, H, D, D), lambda n: (n, 0, 0, 0)),
                   pl.BlockSpec((CPS, H, C, C), lambda n: (n, 0, 0, 0))],
        out_shape=[S((T, AW), F32), S((T, AW), F32), S((N, H, D, D), F32), S((N, H, C, C), F32)],
        scratch_shapes=[pltpu.VMEM((H, D, D), F32)],
        compiler_params=_cp(ARB),
    )(q, k, v, gcol, grow)


def _delta_bwd(q, k, v, gcol, grow, ba, vnew, ssave, asave, d_o, a_log, dt_bias, H, D, carry):
    T = q.shape[0]
    C = CHUNK_A
    N = T // C
    AW = H * D
    nc = len(carry)
    CPS = 2 if N % 2 == 0 else 1
    NS = N // CPS

    def body(al_ref, dt_ref, q_ref, k_ref, v_ref, gcol_ref, grow_ref, ba_ref, vn_ref, ss_ref, as_ref, do_ref, *rest):
        cins = rest[:nc]
        dq_ref, dk_ref, dv_ref, dgate_ref, dpar_ref = rest[nc:nc + 5]
        couts = rest[nc + 5:2 * nc + 5]
        ds_ref, csend, crecv = rest[2 * nc + 5:]
        ccps = _chip_exchange_copies(cins, couts, csend, crecv)

        @pl.when(pl.program_id(0) == 0)
        def _():
            ds_ref[...] = jnp.zeros_like(ds_ref)
            dpar_ref[...] = jnp.zeros_like(dpar_ref)
            for cp in ccps:
                cp.start()

        lane = lax.broadcasted_iota(jnp.int32, (C, LANES), 1)
        lane1 = lax.broadcasted_iota(jnp.int32, (1, LANES), 1)
        rowi = lax.broadcasted_iota(jnp.int32, (C, 1), 0)
        acc = {"dpar": jnp.zeros((1, LANES), F32)}
        for cc in range(CPS):
            acc[cc] = jnp.zeros((C, LANES), F32)
        state = {(0, h): ds_ref[h] for h in range(H)}

        def head(oi, h):
            cc = CPS - 1 - oi
            rows = slice(cc * C, (cc + 1) * C)
            sl = slice(h * D, (h + 1) * D)
            st = ss_ref[cc, h]
            a = as_ref[cc, h]
            qv, kv, vv, dov, v_new = q_ref[rows, sl], k_ref[rows, sl], v_ref[rows, sl], do_ref[rows, sl], vn_ref[rows, sl]
            bav = ba_ref[rows, :]
            t = yield from _chunk_terms(qv, kv, vv, gcol_ref[rows, :], grow_ref[cc], h, H)
            beta_c, e_c, gam, kb = t["beta_c"], t["e_c"], t["gam"], t["kb"]
            incl, strict, attn, lmat, edec = t["incl"], t["strict"], t["attn"], t["lmat"], t["edec"]
            kdec = kv * edec
            egl = jnp.exp(t["gl"])
            qe = qv * e_c
            ekb = kb * e_c

            t1 = _mm_nt(dov, st)
            ds_o = _mm_tn(qe, dov)
            dattn_raw = _mm_nt(dov, v_new)
            dv_new_o = _mm_tn(attn, dov)
            yield
            while (oi, h) not in state:
                yield
            ds_next = state[(oi, h)]
            dkdec = _mm_nt(v_new, ds_next)
            dv_new_s = _mm(kdec, ds_next)
            yield
            dgl = egl * jnp.sum(jnp.sum(st * ds_next, axis=1, keepdims=True), axis=0, keepdims=True)
            dk = edec * dkdec
            r = jnp.sum(dkdec * kdec, axis=1, keepdims=True)
            dgc = -r
            dgl = dgl + jnp.sum(r, axis=0, keepdims=True)
            dq = e_c * t1
            dgc = dgc + jnp.sum(t1 * qe, axis=1, keepdims=True)
            dattn = jnp.where(incl, dattn_raw, 0.0)
            dv_new = dv_new_s + dv_new_o
            dqm = dattn * gam
            z = dattn * attn
            dvb = _mm3(a, dv_new, _TN)
            dq_a = _mm(dqm, kv)
            dk_a = _mm_tn(dqm, qv)
            yield
            dq_ref[rows, sl] = dq + dq_a
            dv_ref[rows, sl] = beta_c * dvb
            ds_kb = _mm_tn(ekb, dvb)
            dekb_neg = _mm_nt(dvb, st)
            dl_neg = _mm_nt(dvb, v_new)
            yield
            state[(oi + 1, h)] = egl * ds_next + ds_o - ds_kb
            dekb = -dekb_neg
            dl = jnp.where(strict, -dl_neg, 0.0)
            dp = dl * gam
            z = z + dl * lmat
            dkb_p = _mm(dp, kv)
            dk_p = _mm_tn(dp, kb)
            dgc = dgc + jnp.sum(dekb * ekb, axis=1, keepdims=True)
            dgc = dgc + jnp.sum(z, axis=1, keepdims=True) - jnp.sum(z.T, axis=1, keepdims=True)
            dgc = dgc + jnp.where(rowi == C - 1, dgl, 0.0)
            upper = (t["ri"] <= t["ci"]).astype(F32)
            dg_b = _mm3(upper, jnp.broadcast_to(dgc, (C, LANES)))
            yield
            dkb = dkb_p + e_c * dekb
            dk_ref[rows, sl] = dk + dk_a + dk_p + beta_c * dkb
            dbeta = jnp.sum(dkb * kv, axis=1, keepdims=True) + jnp.sum(dvb * vv, axis=1, keepdims=True)
            dg = dg_b[:, 0:1]
            a_raw = bav[:, H + h:H + h + 1]
            d_braw = dbeta * beta_c * (1.0 - beta_c)
            d_araw = dg * (-jnp.exp(al_ref[0, h])) * _sigmoid(a_raw + dt_ref[0, h])
            acc[cc] = acc[cc] + jnp.where(lane == h, d_braw, 0.0) + jnp.where(lane == H + h, d_araw, 0.0)
            dal = jnp.sum(dg * t["g_c"], axis=0, keepdims=True)
            ddt = jnp.sum(d_araw, axis=0, keepdims=True)
            acc["dpar"] = acc["dpar"] + jnp.where(lane1 == h, dal, 0.0) + jnp.where(lane1 == H + h, ddt, 0.0)

        _interleave(head(oi, h) for oi in range(CPS) for h in range(H))
        for h in range(H):
            ds_ref[h] = state[(CPS, h)]
        for cc in range(CPS):
            dgate_ref[cc * C:(cc + 1) * C, :] = acc[cc]
        dpar_ref[0:1, :] += acc["dpar"]

        @pl.when(pl.program_id(0) == NS - 1)
        def _():
            for cp in ccps:
                cp.wait()

    rev = lambda s: NS - 1 - s
    blk = lambda: pl.BlockSpec((CPS * C, AW), lambda s: (rev(s), 0))
    smem = pl.BlockSpec(memory_space=pltpu.SMEM)
    any_spec = pl.BlockSpec(memory_space=pl.ANY)
    res = pl.pallas_call(
        body, name="delta_bwd", grid=(NS,),
        in_specs=[smem, smem, blk(), blk(), blk(),
                  pl.BlockSpec((CPS * C, LANES), lambda s: (rev(s), 0)),
                  pl.BlockSpec((CPS, 32, C), lambda s: (rev(s), 0, 0)),
                  pl.BlockSpec((CPS * C, LANES), lambda s: (rev(s), 0)),
                  blk(),
                  pl.BlockSpec((CPS, H, D, D), lambda s: (rev(s), 0, 0, 0)),
                  pl.BlockSpec((CPS, H, C, C), lambda s: (rev(s), 0, 0, 0)),
                  blk()] + [any_spec] * nc,
        out_specs=[blk(), blk(), blk(),
                   pl.BlockSpec((CPS * C, LANES), lambda s: (rev(s), 0)),
                   pl.BlockSpec((8, LANES), lambda s: (0, 0))] + [any_spec] * nc,
        out_shape=[S((T, AW), F32), S((T, AW), F32), S((T, AW), F32),
                   S((T, LANES), F32), S((8, LANES), F32)] + [S((3,) + a.shape[1:], a.dtype) for a in carry],
        scratch_shapes=[pltpu.VMEM((H, D, D), F32),
                        pltpu.SemaphoreType.DMA((max(nc, 1), 3)), pltpu.SemaphoreType.DMA((max(nc, 1), 3))],
        compiler_params=_cp(ARB),
    )(a_log, dt_bias, q, k, v, gcol, grow, ba, vnew, ssave, asave, d_o, *carry)
    return res[:5], res[5:]


def _ln_stats(xv):
    mu = jnp.mean(xv, axis=-1, keepdims=True)
    xc = xv - mu
    var = jnp.mean(xc * xc, axis=-1, keepdims=True)
    rstd = lax.rsqrt(var + EPS)
    return xc * rstd, rstd


def _mix_fwd(o, proj, head_norm_w, ln_w, ln_b, w_sp, bs_t, H, D, G, P):
    T = o.shape[0]
    AW, BW = H * D, G * P
    MIX = AW + BW
    nb = AW // BW if AW % BW == 0 else None
    assert nb == 1, "group widths must match the projection column blocks"
    cb = 3

    def body(o_ref, za_ref, ub_ref, vb_ref, zb_ref, hw_ref, lw_ref, lb_ref, w_ref, bs_ref, out_ref):
        hw = hw_ref[...]
        for h in range(H):
            sl = slice(h * D, (h + 1) * D)
            oh = o_ref[:, sl]
            rs = lax.rsqrt(jnp.mean(oh * oh, axis=-1, keepdims=True) + EPS)
            out_ref[:, sl] = (oh * rs * hw * _silu(za_ref[:, sl])).astype(BF16)
        xhat, _ = _ln_stats(vb_ref[...])
        vn = xhat * lw_ref[...] + lb_ref[...]
        ri = lax.broadcasted_iota(jnp.int32, (P, P), 0)
        ci = lax.broadcasted_iota(jnp.int32, (P, P), 1)
        bsv = bs_ref[...]
        for g in range(G):
            sl = slice(g * P, (g + 1) * P)
            wm = jnp.where(ri >= ci, w_ref[g], 0.0)
            s = _mm(wm, vn[:, sl]) + bsv[:, g:g + 1]
            out_ref[:, AW + g * P:AW + (g + 1) * P] = (ub_ref[:, sl] * s * _silu(zb_ref[:, sl])).astype(BF16)

    row = lambda w: pl.BlockSpec((1, w), lambda i: (0, 0))
    return pl.pallas_call(
        body, name="mix_fwd", grid=(T // P,),
        in_specs=[pl.BlockSpec((P, AW), lambda i: (i, 0)),
                  pl.BlockSpec((P, AW), lambda i: (i, cb)),
                  pl.BlockSpec((P, BW), lambda i: (i, cb + 1)),
                  pl.BlockSpec((P, BW), lambda i: (i, cb + 2)),
                  pl.BlockSpec((P, BW), lambda i: (i, cb + 3)),
                  row(D), row(BW), row(BW),
                  pl.BlockSpec((G, P, P), lambda i: (0, 0, 0)),
                  pl.BlockSpec((P, G), lambda i: (0, 0))],
        out_specs=pl.BlockSpec((P, MIX), lambda i: (i, 0)),
        out_shape=S((T, MIX), BF16),
        compiler_params=_cp(ARB),
    )(o, proj, proj, proj, proj, head_norm_w, ln_w, ln_b, w_sp, bs_t)


def _mix_bwd(d_ocat, o, proj, head_norm_w, ln_w, ln_b, w_sp, bs_t, H, D, G, P, carry):
    T = o.shape[0]
    AW, BW = H * D, G * P
    MIX = AW + BW
    cb = 3
    nc = len(carry)

    def body(dc_ref, o_ref, za_ref, ub_ref, vb_ref, zb_ref, hw_ref, lw_ref, lb_ref, w_ref, bs_ref, *rest):
        cins = rest[:nc]
        do_ref, dmain_ref, dhw_ref, dln_ref, dw_ref, dbs_ref = rest[nc:nc + 6]
        couts = rest[nc + 6:2 * nc + 6]
        dvn_ref, drest_ref, out_sems, csend, crecv = rest[2 * nc + 6:]
        i = pl.program_id(0)
        slot = lax.rem(i, 2)
        ccps = _sibling_copies(cins, couts, csend, crecv)

        def out_copy(step, s):
            return pltpu.make_async_copy(
                drest_ref.at[s], dmain_ref.at[pl.ds(step * P, P), pl.ds(cb * AW, AW + 3 * BW)], out_sems.at[s])

        @pl.when(i == 0)
        def _():
            dhw_ref[...] = jnp.zeros_like(dhw_ref)
            dln_ref[...] = jnp.zeros_like(dln_ref)
            dw_ref[...] = jnp.zeros_like(dw_ref)
            dbs_ref[...] = jnp.zeros_like(dbs_ref)
            for cp in ccps:
                cp.start()

        @pl.when(i >= 2)
        def _():
            out_copy(i - 2, slot).wait()

        hw = hw_ref[...]
        dhw = jnp.zeros((1, D), F32)
        for h in range(H):
            sl = slice(h * D, (h + 1) * D)
            oh = o_ref[:, sl]
            za = za_ref[:, sl]
            doa = dc_ref[:, sl]
            rs = lax.rsqrt(jnp.mean(oh * oh, axis=-1, keepdims=True) + EPS)
            xh = oh * rs
            d_on = doa * _silu(za)
            drest_ref[slot, :, sl] = (doa * (xh * hw) * _dsilu(za)).astype(BF16)
            dhw = dhw + jnp.sum(d_on * xh, axis=0, keepdims=True)
            dxh = d_on * hw
            do_ref[:, sl] = rs * (dxh - xh * jnp.mean(dxh * xh, axis=-1, keepdims=True))
        dhw_ref[0:1, :] += dhw

        xhat, rstd = _ln_stats(vb_ref[...])
        lw = lw_ref[...]
        vn = xhat * lw + lb_ref[...]
        ri = lax.broadcasted_iota(jnp.int32, (P, P), 0)
        ci = lax.broadcasted_iota(jnp.int32, (P, P), 1)
        lane = lax.broadcasted_iota(jnp.int32, (P, LANES), 1)
        bsv = bs_ref[...]
        dbs = jnp.zeros((P, LANES), F32)
        for g in range(G):
            sl = slice(g * P, (g + 1) * P)
            wm = jnp.where(ri >= ci, w_ref[g], 0.0)
            vng = vn[:, sl]
            s = _mm(wm, vng) + bsv[:, g:g + 1]
            dob = dc_ref[:, AW + g * P:AW + (g + 1) * P]
            ub = ub_ref[:, sl]
            zb = zb_ref[:, sl]
            szb = _silu(zb)
            drest_ref[slot, :, AW + g * P:AW + (g + 1) * P] = (dob * s * szb).astype(BF16)
            drest_ref[slot, :, AW + 2 * BW + g * P:AW + 2 * BW + (g + 1) * P] = (
                dob * ub * s * _dsilu(zb)).astype(BF16)
            ds = dob * ub * szb
            dvn_ref[:, sl] = _mm_tn(wm, ds)
            dw_ref[g] += jnp.where(ri >= ci, _mm_nt(ds, vng), 0.0)
            dbs = dbs + jnp.where(lane == g, jnp.sum(ds, axis=1, keepdims=True), 0.0)
        dbs_ref[...] += dbs
        dvn = dvn_ref[...]
        dln_ref[0:1, :] += jnp.sum(dvn * xhat, axis=0, keepdims=True)
        dln_ref[1:2, :] += jnp.sum(dvn, axis=0, keepdims=True)
        dxh = dvn * lw
        dvb = rstd * (dxh - jnp.mean(dxh, axis=-1, keepdims=True) - xhat * jnp.mean(dxh * xhat, axis=-1, keepdims=True))
        drest_ref[slot, :, AW + BW:AW + 2 * BW] = dvb.astype(BF16)

        out_copy(i, slot).start()

        @pl.when(i == nstep - 1)
        def _():
            out_copy(i, slot).wait()
            if nstep > 1:
                out_copy(i - 1, 1 - slot).wait()
            for cp in ccps:
                cp.wait()

    nstep = T // P
    row = lambda w: pl.BlockSpec((1, w), lambda i: (0, 0))
    any_spec = pl.BlockSpec(memory_space=pl.ANY)
    res = pl.pallas_call(
        body, name="mix_bwd", grid=(nstep,),
        in_specs=[pl.BlockSpec((P, MIX), lambda i: (i, 0)),
                  pl.BlockSpec((P, AW), lambda i: (i, 0)),
                  pl.BlockSpec((P, AW), lambda i: (i, cb)),
                  pl.BlockSpec((P, BW), lambda i: (i, cb + 1)),
                  pl.BlockSpec((P, BW), lambda i: (i, cb + 2)),
                  pl.BlockSpec((P, BW), lambda i: (i, cb + 3)),
                  row(D), row(BW), row(BW),
                  pl.BlockSpec((G, P, P), lambda i: (0, 0, 0)),
                  pl.BlockSpec((P, G), lambda i: (0, 0))] + [any_spec] * nc,
        out_specs=[pl.BlockSpec((P, AW), lambda i: (i, 0)),
                   any_spec,
                   pl.BlockSpec((8, D), lambda i: (0, 0)),
                   pl.BlockSpec((8, BW), lambda i: (0, 0)),
                   pl.BlockSpec((G, P, P), lambda i: (0, 0, 0)),
                   pl.BlockSpec((P, LANES), lambda i: (0, 0))] + [any_spec] * nc,
        out_shape=[S((T, AW), F32), S((T, cb * AW + AW + 3 * BW), BF16), S((8, D), F32), S((8, BW), F32),
                   S((G, P, P), F32), S((P, LANES), F32)] + [S(a.shape[:1] + a.shape[2:], a.dtype) for a in carry],
        scratch_shapes=[pltpu.VMEM((P, BW), F32), pltpu.VMEM((2, P, AW + 3 * BW), BF16),
                        pltpu.SemaphoreType.DMA((2,))] + _sibling_sems(carry),
        compiler_params=_cp(ARB),
    )(d_ocat, o, proj, proj, proj, proj, head_norm_w, ln_w, ln_b, w_sp, bs_t, *carry)
    return res[:6], res[6:]


def _out_proj_loss(ocat, w_out, x, target, fnw):
    T, MIX = ocat.shape
    DM = x.shape[1]
    tm = _tile(T, 256, 8)

    def body(oc_ref, w_ref, x_ref, t_ref, fw_ref, dh_ref, dhb_ref, doc_ref, loss_ref, gfw_ref):
        @pl.when(pl.program_id(0) == 0)
        def _():
            loss_ref[...] = jnp.zeros_like(loss_ref)
            gfw_ref[...] = jnp.zeros_like(gfw_ref)

        wv = w_ref[...]
        hh = x_ref[...] + jnp.dot(oc_ref[...].astype(MXU), wv.astype(MXU), preferred_element_type=F32)
        rs = lax.rsqrt(jnp.mean(hh * hh, axis=-1, keepdims=True) + EPS)
        hn = hh * rs
        fw = fw_ref[...]
        e = hn * fw - t_ref[...]
        row_loss = 0.5 * jnp.mean(e * e, axis=-1, keepdims=True)
        loss_ref[...] += jnp.sum(row_loss, axis=0, keepdims=True)
        dy = e * (1.0 / DM)
        gfw_ref[0:1, :] += jnp.sum(dy * hn, axis=0, keepdims=True)
        dhn = dy * fw
        dh = rs * (dhn - hn * jnp.mean(dhn * hn, axis=-1, keepdims=True))
        dh_ref[...] = dh
        dhb = dh.astype(BF16)
        dhb_ref[...] = dhb
        doc_ref[...] = _mm_nt(dhb, wv)

    return pl.pallas_call(
        body, name="out_proj_loss", grid=(T // tm,),
        in_specs=[pl.BlockSpec((tm, MIX), lambda i: (i, 0)),
                  pl.BlockSpec((MIX, DM), lambda i: (0, 0)),
                  pl.BlockSpec((tm, DM), lambda i: (i, 0)),
                  pl.BlockSpec((tm, DM), lambda i: (i, 0)),
                  pl.BlockSpec((1, DM), lambda i: (0, 0))],
        out_specs=[pl.BlockSpec((tm, DM), lambda i: (i, 0)),
                   pl.BlockSpec((tm, DM), lambda i: (i, 0)),
                   pl.BlockSpec((tm, MIX), lambda i: (i, 0)),
                   pl.BlockSpec((8, LANES), lambda i: (0, 0)),
                   pl.BlockSpec((8, DM), lambda i: (0, 0))],
        out_shape=[S((T, DM), F32), S((T, DM), BF16), S((T, MIX), F32), S((8, LANES), F32), S((8, DM), F32)],
        compiler_params=_cp(ARB),
    )(ocat, w_out, x, target, fnw)


def _grad_w(lhs, rhs, name):
    T, A = lhs.shape
    B = rhs.shape[1]
    ta = _tile(A, 512, LANES)
    tk = _tile(T, 1024, 16)
    nk = T // tk

    def body(l_ref, r_ref, out_ref, acc_ref):
        k = pl.program_id(1)
        part = _mm_tn(l_ref[...], r_ref[...])

        @pl.when(k == 0)
        def _():
            acc_ref[...] = part

        @pl.when(k > 0)
        def _():
            acc_ref[...] += part

        @pl.when(k == nk - 1)
        def _():
            out_ref[...] = acc_ref[...].astype(BF16)

    return pl.pallas_call(
        body, name=name, grid=(A // ta, nk),
        in_specs=[pl.BlockSpec((tk, ta), lambda i, k: (k, i)),
                  pl.BlockSpec((tk, B), lambda i, k: (k, 0))],
        out_specs=pl.BlockSpec((ta, B), lambda i, k: (i, 0)),
        out_shape=S((A, B), BF16),
        scratch_shapes=[pltpu.VMEM((ta, B), F32)],
        compiler_params=_cp(ARB, ARB),
    )(lhs, rhs)


def _grad_w_in(xn, dmain, dba, WD, gate_lo, gate_hi):
    T, DM = xn.shape
    NM = dmain.shape[1]
    tn = _tile(NM, 1024, LANES)
    tk = _tile(T, 2048, 16)
    nj, nk = NM // tn, T // tk
    ND = N_DEV
    tiles = [[] for _ in range(nj)]
    first_tile, last_tile = {}, {}
    for d, s0, s1, dest, c0 in _pieces(WD, gate_lo, gate_hi, ND * WD):
        if dest != "main":
            continue
        while s0 < s1:
            jj = c0 // tn
            w = min(s1 - s0, (jj + 1) * tn - c0)
            tiles[jj].append((d, s0, w, "main", c0 - jj * tn))
            first_tile.setdefault(d, jj)
            last_tile[d] = jj
            s0, c0 = s0 + w, c0 + w
    for d, s0, s1, dest, c0 in _pieces(WD, gate_lo, gate_hi, ND * WD):
        if dest == "gate":
            tiles[first_tile[d]].append((d, s0, s1 - s0, "gate", c0))
    assert sorted(first_tile) == list(range(ND)) and all(last_tile[d] <= first_tile[d + 2] for d in range(ND - 2))

    def body(xn_ref, dm_ref, dba_ref, keep_ref, recv_ref, acc_ref, gate_ref, buf_ref, lsem, ssem, rsem):
        j = pl.program_id(0)
        k = pl.program_id(1)
        px, py, pc = _position()

        @pl.when(k == 0)
        def _():
            acc_ref[...] = jnp.zeros_like(acc_ref)

        @pl.when((j == 0) & (k == 0))
        def _():
            gate_ref[...] = jnp.zeros_like(gate_ref)

        xv = xn_ref[...]
        acc_ref[...] += _mm_tn(xv, dm_ref[...])

        @pl.when(j == 0)
        def _():
            gate_ref[...] += _mm_tn(xv, dba_ref[...])

        def local(d):
            return pltpu.make_async_copy(buf_ref.at[d % 2], keep_ref.at[d // 2], lsem.at[d // 2])

        def remote(d):
            return pltpu.make_async_remote_copy(
                src_ref=buf_ref.at[d % 2], dst_ref=recv_ref.at[d // 2], send_sem=ssem.at[d // 2],
                recv_sem=rsem.at[d // 2], device_id=(px, py, 1 - pc), device_id_type=MESH)

        def leave(d, start):
            @pl.when(pc == d % 2)
            def _():
                local(d).start() if start else local(d).wait()

            @pl.when(pc != d % 2)
            def _():
                remote(d).start() if start else remote(d).wait_send()

        def emit(jj):
            shards = sorted({p[0] for p in tiles[jj]})
            for d in shards:
                if first_tile[d] == jj and d >= 2:
                    leave(d - 2, False)
                for dd, s0, w, src, c0 in tiles[jj]:
                    if dd == d:
                        ref = acc_ref if src == "main" else gate_ref
                        buf_ref[d % 2, :, s0:s0 + w] = ref[:, c0:c0 + w].astype(BF16)
                if last_tile[d] == jj:
                    leave(d, True)
            if jj == nj - 1:
                for d in (ND - 2, ND - 1):
                    leave(d, False)
                for q in range(ND // 2):
                    remote(2 * q).wait_recv()

        for jj in range(nj):
            @pl.when((j == jj) & (k == nk - 1))
            def _(jj=jj):
                emit(jj)

    any_spec = pl.BlockSpec(memory_space=pl.ANY)
    return pl.pallas_call(
        body, name="grad_w_in", grid=(nj, nk),
        in_specs=[pl.BlockSpec((tk, DM), lambda j, k: (k, 0)),
                  pl.BlockSpec((tk, tn), lambda j, k: (k, j)),
                  pl.BlockSpec((tk, LANES), lambda j, k: (k, 0))],
        out_specs=[any_spec, any_spec],
        out_shape=[S((ND // 2, DM, WD), BF16), S((ND // 2, DM, WD), BF16)],
        scratch_shapes=[pltpu.VMEM((DM, tn), F32), pltpu.VMEM((DM, LANES), F32), pltpu.VMEM((2, DM, WD), BF16),
                        pltpu.SemaphoreType.DMA((ND // 2,)), pltpu.SemaphoreType.DMA((ND // 2,)),
                        pltpu.SemaphoreType.DMA((ND // 2,))],
        compiler_params=_cp(ARB, ARB),
    )(xn, dmain, dba)


def _pair_sum_plain(a, b, name):
    K, R, C = a.shape
    tr = _tile(R, 256, 16)

    def body(a_ref, b_ref, o_ref):
        o_ref[...] = (a_ref[...].astype(F32) + b_ref[...].astype(F32)).astype(BF16)

    spec = lambda: pl.BlockSpec((1, tr, C), lambda q, i: (q, i, 0))
    return pl.pallas_call(body, name=name, grid=(K, R // tr), in_specs=[spec(), spec()], out_specs=spec(),
                          out_shape=S((K, R, C), BF16), compiler_params=_cp(ARB, ARB))(a, b)


def _dx(dmain, dba, w_main, w_ba, x, dh, norm_w, chip_sum, small):
    T, NM = dmain.shape
    DM = x.shape[1]
    tm = _tile(T, 512, 8)
    tk = _tile(NM, 1024, LANES)
    nk = NM // tk
    ni = T // tm
    half = chip_sum.shape[1] // 2
    assert half % 16 == 0
    last_step = ni * nk - 1
    relay_step = min(2 * nk, last_step)

    def body(dm_ref, dba_ref, w_ref, wba_ref, x_ref, dh_ref, nw_ref, small_ref, cs_ref,
             gx_ref, gnw_ref, gath_ref, recv_ref, stage_ref, acc_ref, csend, crecv, ssend, srecv, lsem):
        i = pl.program_id(0)
        k = pl.program_id(1)
        step = i * nk + k
        px, py, pc = _position()
        xn, yn = (1 - px, py, pc), (px, 1 - py, pc)
        upper, lower = pl.ds(0, half), pl.ds(half, half)

        def rcopy(kk, src, dst, to):
            return pltpu.make_async_remote_copy(src_ref=src, dst_ref=dst, send_sem=csend.at[kk], recv_sem=crecv.at[kk],
                                                device_id=to, device_id_type=MESH)

        diag_blk = cs_ref.at[2 * (1 - px) + (1 - py)]
        to_stage = [rcopy(2, diag_blk.at[upper], stage_ref.at[0], xn), rcopy(3, diag_blk.at[lower], stage_ref.at[1], yn)]
        direct = [rcopy(0, cs_ref.at[2 * (1 - px) + py], recv_ref.at[0], xn),
                  rcopy(1, cs_ref.at[2 * px + (1 - py)], recv_ref.at[1], yn)]
        onward = [rcopy(4, stage_ref.at[0], recv_ref.at[2].at[upper], yn),
                  rcopy(5, stage_ref.at[1], recv_ref.at[2].at[lower], xn)]
        me, small_cps = _broadcast_copies([small_ref], [gath_ref], ssend, srecv)
        small_cps = small_cps + [pltpu.make_async_copy(small_ref, gath_ref.at[me], lsem.at[0])]

        @pl.when(step == 0)
        def _():
            gnw_ref[...] = jnp.zeros_like(gnw_ref)
            for cp in to_stage + direct + small_cps:
                cp.start()

        @pl.when(step == relay_step)
        def _():
            for cp in to_stage:
                cp.wait_recv()
            for cp in onward:
                cp.start()

        @pl.when(k == 0)
        def _():
            acc_ref[...] = _mm_nt(dba_ref[...], wba_ref[...])

        acc_ref[...] += _mm_nt(dm_ref[...], w_ref[...])

        @pl.when(k == nk - 1)
        def _():
            xv = x_ref[...]
            rs = lax.rsqrt(jnp.mean(xv * xv, axis=-1, keepdims=True) + EPS)
            xh = xv * rs
            dxn = acc_ref[...]
            gnw_ref[0:1, :] += jnp.sum(dxn * xh, axis=0, keepdims=True)
            dxh = dxn * nw_ref[...]
            gx_ref[...] = dh_ref[...] + rs * (dxh - xh * jnp.mean(dxh * xh, axis=-1, keepdims=True))

        @pl.when(step == last_step)
        def _():
            for cp in to_stage:
                cp.wait_send()
            for cp in direct + onward + small_cps:
                cp.wait()

    any_spec = pl.BlockSpec(memory_space=pl.ANY)
    res = pl.pallas_call(
        body, name="dx", grid=(ni, nk),
        in_specs=[pl.BlockSpec((tm, tk), lambda i, k: (i, k)),
                  pl.BlockSpec((tm, LANES), lambda i, k: (i, 0)),
                  pl.BlockSpec((DM, tk), lambda i, k: (0, k)),
                  pl.BlockSpec((DM, LANES), lambda i, k: (0, 0)),
                  pl.BlockSpec((tm, DM), lambda i, k: (i, 0)),
                  pl.BlockSpec((tm, DM), lambda i, k: (i, 0)),
                  pl.BlockSpec((1, DM), lambda i, k: (0, 0)),
                  any_spec, any_spec],
        out_specs=[pl.BlockSpec((tm, DM), lambda i, k: (i, 0)),
                   pl.BlockSpec((8, DM), lambda i, k: (0, 0)),
                   any_spec, any_spec, any_spec],
        out_shape=[S((T, DM), F32), S((8, DM), F32), S((N_DEV,) + small.shape, F32),
                   S((3,) + chip_sum.shape[1:], chip_sum.dtype), S((2, half) + chip_sum.shape[2:], chip_sum.dtype)],
        scratch_shapes=[pltpu.VMEM((tm, DM), F32),
                        pltpu.SemaphoreType.DMA((6,)), pltpu.SemaphoreType.DMA((6,)),
                        pltpu.SemaphoreType.DMA((1, N_DEV - 1)), pltpu.SemaphoreType.DMA((1, N_DEV - 1)),
                        pltpu.SemaphoreType.DMA((1,))],
        compiler_params=_cp(ARB, ARB),
    )(dmain, dba, w_main, w_ba, x, dh, norm_w, small, chip_sum)
    return res[0], res[1], res[2], res[3]


def _sum_slots(gath):
    _, R, C = gath.shape
    tr = _tile(R, 512, 8)

    def body(g_ref, o_ref):
        tot = g_ref[0]
        for d in range(1, N_DEV):
            tot = tot + g_ref[d]
        o_ref[...] = tot

    return pl.pallas_call(
        body, name="sum_slots", grid=(R // tr,),
        in_specs=[pl.BlockSpec((N_DEV, tr, C), lambda i: (0, i, 0))],
        out_specs=pl.BlockSpec((tr, C), lambda i: (i, 0)),
        out_shape=S((R, C), F32), compiler_params=_cp(ARB),
    )(gath)


def _prep_a_bwd(dq, dk, dv, c, proj, conv_w, dmain, H, D):
    T = c.shape[0]
    AW = H * D
    C3 = 3 * AW
    tb = _tile(T, 256, 8)
    nblk = T // tb
    r8 = tb // 8
    scale = float(D) ** -0.5

    def body(dq_ref, dk_ref, dv_ref, c_ref, dqn_ref, dkn_ref, dvn_ref, cn_ref, x_ref, halo_ref, cw_ref, dmain_in_ref,
             dx_ref, gcw_ref, dc_ref):
        del dmain_in_ref
        i = pl.program_id(0)

        @pl.when(i == 0)
        def _():
            gcw_ref[...] = jnp.zeros_like(gcw_ref)

        def pointwise(rows, dq_r, dk_r, dv_r, c_r, keep):
            for h in range(H):
                for part, d_r, sc in ((0, dq_r, scale), (1, dk_r, 1.0)):
                    sl = slice(part * AW + h * D, part * AW + (h + 1) * D)
                    cv = c_r[:, sl]
                    raw = _silu(cv)
                    rs = lax.rsqrt(jnp.sum(raw * raw, axis=-1, keepdims=True) + EPS)
                    nrm = raw * rs
                    dn = d_r[:, h * D:(h + 1) * D] * sc
                    draw = rs * (dn - nrm * jnp.sum(dn * nrm, axis=-1, keepdims=True))
                    dc_ref[rows, sl] = draw * _dsilu(cv) * keep
            dc_ref[rows, 2 * AW:] = dv_r[...] * _dsilu(c_r[:, 2 * AW:]) * keep

        pointwise(slice(0, tb), dq_ref, dk_ref, dv_ref, c_ref, 1.0)
        pointwise(slice(tb, tb + 8), dqn_ref, dkn_ref, dvn_ref, cn_ref, (i < nblk - 1).astype(F32))

        cw = cw_ref[...]
        dcv = dc_ref[0:tb, :]
        dx = cw[3:4, :] * dcv
        for j in range(3):
            dx = dx + cw[j:j + 1, :] * dc_ref[3 - j:3 - j + tb, :]
        dx_ref[...] = dx.astype(BF16)
        halo = halo_ref[...] * (i > 0).astype(F32)
        xp = jnp.concatenate([halo, x_ref[...]], axis=0)
        for j in range(4):
            gcw_ref[j:j + 1, :] += jnp.sum(dcv * xp[5 + j:5 + j + tb], axis=0, keepdims=True)

    nxt = lambda i: (jnp.minimum((i + 1) * r8, T // 8 - 1), 0)
    return pl.pallas_call(
        body, name="prep_a_bwd", grid=(nblk,),
        in_specs=[pl.BlockSpec((tb, AW), lambda i: (i, 0)),
                  pl.BlockSpec((tb, AW), lambda i: (i, 0)),
                  pl.BlockSpec((tb, AW), lambda i: (i, 0)),
                  pl.BlockSpec((tb, C3), lambda i: (i, 0)),
                  pl.BlockSpec((8, AW), nxt), pl.BlockSpec((8, AW), nxt), pl.BlockSpec((8, AW), nxt),
                  pl.BlockSpec((8, C3), nxt),
                  pl.BlockSpec((tb, C3), lambda i: (i, 0)),
                  pl.BlockSpec((8, C3), lambda i: (jnp.maximum(i * r8 - 1, 0), 0)),
                  pl.BlockSpec((4, C3), lambda i: (0, 0)),
                  pl.BlockSpec(memory_space=pl.ANY)],
        out_specs=[pl.BlockSpec((tb, C3), lambda i: (i, 0)),
                   pl.BlockSpec((8, C3), lambda i: (0, 0))],
        out_shape=[S(dmain.shape, dmain.dtype), S((8, C3), F32)],
        scratch_shapes=[pltpu.VMEM((tb + 8, C3), F32)],
        input_output_aliases={11: 0},
        compiler_params=_cp(ARB),
    )(dq, dk, dv, c, dq, dk, dv, c, proj, proj, conv_w, dmain)


def _adam_math(w, g, m, v):
    m2 = ADAM_B1 * m + (1.0 - ADAM_B1) * g
    v2 = ADAM_B2 * v + (1.0 - ADAM_B2) * (g * g)
    m_hat = m2 / (1.0 - ADAM_B1 ** ADAM_STEP)
    v_hat = v2 / (1.0 - ADAM_B2 ** ADAM_STEP)
    delta = -ADAM_LR * (m_hat / (jnp.sqrt(v_hat) + ADAM_EPS) + ADAM_WD * w)
    return delta, m2, v2


def _pair_sum(blocks, recv, core, name):
    K, _, R, C = blocks.shape
    tr = _tile(R, 256, 16)

    def body(core_ref, a_ref, b_ref, o_ref):
        del core_ref
        o_ref[0] = (a_ref[0, 0].astype(F32) + b_ref[0].astype(F32)).astype(BF16)

    spec = lambda: pl.BlockSpec((1, tr, C), lambda k, i, core_ref: (k, i, 0))
    return pl.pallas_call(
        body, name=name,
        grid_spec=pltpu.PrefetchScalarGridSpec(
            num_scalar_prefetch=1, grid=(K, R // tr),
            in_specs=[pl.BlockSpec((1, 1, tr, C), lambda k, i, core_ref: (k, core_ref[0], i, 0)), spec()],
            out_specs=spec()),
        out_shape=S((K, R, C), BF16), compiler_params=_cp(ARB, ARB),
    )(core, blocks, recv)


def _sum_adam(chip_sums, recv, w, m, v, chip, name, transposed=False):
    R, C = chip_sums.shape[1:]
    tr = _tile(R, 256, 16)

    def body(chip_ref, own_ref, r_ref, w_ref, m_ref, v_ref, g_ref, d_ref, m2_ref, v2_ref):
        del chip_ref
        g = own_ref[0].astype(F32)
        for j in range(3):
            g = g + r_ref[j].astype(F32)
        if transposed:
            g = g.T
        g_ref[...] = g
        d_ref[...], m2_ref[...], v2_ref[...] = _adam_math(w_ref[...], g, m_ref[...], v_ref[...])

    if transposed:
        spec = lambda: pl.BlockSpec((C, tr), lambda i, chip_ref: (0, i))
        shape = (C, R)
    else:
        spec = lambda: pl.BlockSpec((tr, C), lambda i, chip_ref: (i, 0))
        shape = (R, C)
    assert w.shape == shape
    return pl.pallas_call(
        body, name=name,
        grid_spec=pltpu.PrefetchScalarGridSpec(
            num_scalar_prefetch=1, grid=(R // tr,),
            in_specs=[pl.BlockSpec((1, tr, C), lambda i, chip_ref: (chip_ref[0], i, 0)),
                      pl.BlockSpec((3, tr, C), lambda i, chip_ref: (0, i, 0)), spec(), spec(), spec()],
            out_specs=[spec(), spec(), spec(), spec()]),
        out_shape=[S(shape, F32)] * 4, compiler_params=_cp(ARB),
    )(chip, chip_sums, recv, w, m, v)


def _adam_small(w, g, m, v):
    R, C = w.shape
    tr = _tile(R, 512, 8)

    def body(w_ref, g_ref, m_ref, v_ref, d_ref, m2_ref, v2_ref):
        d_ref[...], m2_ref[...], v2_ref[...] = _adam_math(w_ref[...], g_ref[...], m_ref[...], v_ref[...])

    spec = lambda: pl.BlockSpec((tr, C), lambda i: (i, 0))
    return pl.pallas_call(
        body, name="adam_small", grid=(R // tr,), in_specs=[spec()] * 4, out_specs=[spec()] * 3,
        out_shape=[S((R, C), F32)] * 3, compiler_params=_cp(ARB),
    )(w, g, m, v)


def _position():
    return lax.axis_index("x"), lax.axis_index("y"), lax.axis_index("c")


def _all_gather_weights(arr):
    R = arr.shape[0]
    half = R // 2
    assert half % 16 == 0

    def body(in_ref, out_ref, send_sems, recv_sems, local_sem):
        x, y, c = _position()
        me, sibling = (x, y, c), (x, y, 1 - c)
        xn, yn, diag = (1 - x, y), (x, 1 - y), (1 - x, 1 - y)
        upper, lower = pl.ds(0, half), pl.ds(half, half)

        def slot(p, rows=None):
            ref = out_ref.at[4 * p[0] + 2 * p[1] + p[2]]
            return ref if rows is None else ref.at[rows]

        def copy(kk, block, to, rows=None, src=None):
            return pltpu.make_async_remote_copy(
                src_ref=slot(block, rows) if src is None else src, dst_ref=slot(block, rows),
                send_sem=send_sems.at[kk], recv_sem=recv_sems.at[kk], device_id=to, device_id_type=MESH)

        mine = pltpu.make_async_copy(in_ref, slot(me), local_sem)
        mine.start()
        sent = [copy(0, me, sibling, src=in_ref), copy(1, me, (*xn, c), src=in_ref), copy(2, me, (*yn, c), src=in_ref)]
        for cp in sent:
            cp.start()

        def then(cps):
            for cp in cps:
                cp.start()
            sent.extend(cps)

        copy(1, (*xn, c), me).wait_recv()
        then([copy(5, (*xn, c), (*yn, c), rows=upper), copy(3, (*xn, c), sibling)])
        copy(2, (*yn, c), me).wait_recv()
        then([copy(6, (*yn, c), (*xn, c), rows=lower), copy(4, (*yn, c), sibling)])
        copy(5, (*diag, c), me, rows=upper).wait_recv()
        then([copy(7, (*diag, c), sibling, rows=upper)])
        copy(6, (*diag, c), me, rows=lower).wait_recv()
        then([copy(8, (*diag, c), sibling, rows=lower)])
        copy(0, sibling, me).wait_recv()
        copy(3, (*xn, 1 - c), me).wait_recv()
        copy(4, (*yn, 1 - c), me).wait_recv()
        copy(7, (*diag, 1 - c), me, rows=upper).wait_recv()
        copy(8, (*diag, 1 - c), me, rows=lower).wait_recv()
        for cp in sent:
            cp.wait_send()
        mine.wait()

    any_spec = pl.BlockSpec(memory_space=pl.ANY)
    return pl.pallas_call(
        body, name="all_gather_weights", in_specs=[any_spec], out_specs=any_spec,
        out_shape=S((N_DEV,) + arr.shape, arr.dtype),
        scratch_shapes=[pltpu.SemaphoreType.DMA((9,)), pltpu.SemaphoreType.DMA((9,)), pltpu.SemaphoreType.DMA],
    )(arr)


def _sibling_copies(ins, outs, send_sems, recv_sems):
    x, y, c = _position()
    return [pltpu.make_async_remote_copy(src_ref=ins[a].at[k, 1 - c], dst_ref=outs[a].at[k],
                                         send_sem=send_sems.at[a, k], recv_sem=recv_sems.at[a, k],
                                         device_id=(x, y, 1 - c), device_id_type=MESH)
            for a in range(len(ins)) for k in range(ins[a].shape[0])]


def _sibling_sems(arrs):
    shape = (max(len(arrs), 1), arrs[0].shape[0] if arrs else 1)
    return [pltpu.SemaphoreType.DMA(shape), pltpu.SemaphoreType.DMA(shape)]


def _chip_exchange_copies(ins, outs, send_sems, recv_sems):
    x, y, c = _position()
    chips = [(1 - x, y), (x, 1 - y), (1 - x, 1 - y)]
    return [pltpu.make_async_remote_copy(
        src_ref=ins[a].at[2 * qx + qy], dst_ref=outs[a].at[j], send_sem=send_sems.at[a, j],
        recv_sem=recv_sems.at[a, j], device_id=(qx, qy, c), device_id_type=MESH)
        for a in range(len(ins)) for j, (qx, qy) in enumerate(chips)]


def _broadcast_copies(srcs, dsts, send_sems, recv_sems):
    x, y, c = _position()
    me = 4 * x + 2 * y + c
    cps = []
    for a in range(len(srcs)):
        for k in range(1, N_DEV):
            peer = (1 - x if k & 4 else x, 1 - y if k & 2 else y, 1 - c if k & 1 else c)
            cps.append(pltpu.make_async_remote_copy(
                src_ref=srcs[a], dst_ref=dsts[a].at[me], send_sem=send_sems.at[a, k - 1],
                recv_sem=recv_sems.at[a, k - 1], device_id=peer, device_id_type=MESH))
    return me, cps


def _all_reduce_small(part):
    R, C = part.shape

    def body(p_ref, out_ref, gath_ref, send_sems, recv_sems):
        me, cps = _broadcast_copies([p_ref], [gath_ref], send_sems, recv_sems)
        gath_ref[me] = p_ref[...]
        for cp in cps:
            cp.start()
        for cp in cps:
            cp.wait()
        acc = gath_ref[0]
        for d in range(1, N_DEV):
            acc = acc + gath_ref[d]
        out_ref[...] = acc

    vm = pl.BlockSpec(memory_space=pltpu.VMEM)
    return pl.pallas_call(
        body, name="all_reduce_small", in_specs=[vm], out_specs=vm, out_shape=S((R, C), F32),
        scratch_shapes=[pltpu.VMEM((N_DEV, R, C), F32), pltpu.SemaphoreType.DMA((1, N_DEV - 1)),
                        pltpu.SemaphoreType.DMA((1, N_DEV - 1))],
    )(part)


def _pack(parts):
    rows = []
    for p in parts:
        f = p.reshape(-1).astype(F32)
        pad = (-f.shape[0]) % (8 * LANES)
        rows.append(jnp.pad(f, (0, pad)).reshape(-1, LANES))
    return jnp.concatenate(rows, axis=0)


def _unpack(buf, shapes):
    out, r = [], 0
    for shp in shapes:
        n = 1
        for s in shp:
            n *= s
        nr = -(-n // (8 * LANES)) * 8
        out.append(buf[r:r + nr].reshape(-1)[:n].reshape(shp))
        r += nr
    return out


def kernel(x, norm_w, w_in, conv_w, a_log, dt_bias, head_norm_w, sgu_ln_w, sgu_ln_b, w_spatial, b_spatial, w_out, final_norm_w, loss_target, m_norm_w, m_w_in, m_conv_w, m_a_log, m_dt_bias, m_head_norm_w, m_sgu_ln_w, m_sgu_ln_b, m_w_spatial, m_b_spatial, m_w_out, m_final_norm_w, v_norm_w, v_w_in, v_conv_w, v_a_log, v_dt_bias, v_head_norm_w, v_sgu_ln_w, v_sgu_ln_b, v_w_spatial, v_b_spatial, v_w_out, v_final_norm_w):
    T, DM = x.shape[1], x.shape[2]
    H, D = a_log.shape[1], head_norm_w.shape[1]
    G, P = w_spatial.shape[1], w_spatial.shape[2]
    AW, BW = H * D, G * P
    MIX = AW + BW
    WD = w_in.shape[2]
    IN = N_DEV * WD
    RO = w_out.shape[1]
    CW = conv_w.shape[2]
    sizes = (3 * AW, AW, H, H, BW, BW, BW)
    assert sum(sizes) == IN and 2 * H <= LANES and 3 * H <= 32 and N_DEV * RO == MIX and N_DEV * CW == 3 * AW
    offs = [0]
    for s in sizes:
        offs.append(offs[-1] + s)
    px, py, pc = _position()
    dev = 4 * px + 2 * py + pc
    chip = 2 * px + py

    x2, tgt = x[0], loss_target[0]

    g_win = _all_gather_weights(_cast_bf16_t(w_in[0].T, "cast_w_in"))
    w_main, w_ba = _relayout_w(g_win, offs[2], offs[4])
    alog_row = jnp.pad(a_log, ((0, 0), (H, LANES - 2 * H)))
    dtb_row = jnp.pad(dt_bias, ((0, 0), (H, LANES - 2 * H)))
    bs_t = b_spatial[0].T

    proj, ba, xn, (g_wout, g_conv) = _in_proj(x2, norm_w, w_main, w_ba, [_cast_bf16(w_out[0], "cast_w_out"), conv_w[0]])
    w_out_full = g_wout.reshape(MIX, DM)
    conv_full = g_conv.transpose(1, 0, 2).reshape(4, 3 * AW)
    q, k, v, c, gcol, grow = _prep_a_fwd(proj, ba, conv_full, alog_row, dtb_row, H, D)
    o, vnew, ssave, asave = _delta_fwd(q, k, v, gcol, grow, H, D)
    ocat = _mix_fwd(o, proj, head_norm_w, sgu_ln_w, sgu_ln_b, w_spatial[0], bs_t, H, D, G, P)
    dh, dh_bf, d_ocat, loss_acc, g_fnw = _out_proj_loss(ocat, w_out_full, x2, tgt, final_norm_w.reshape(1, DM))
    loss = lax.psum(loss_acc[0, 0], AXES)

    core_idx = jnp.reshape(pc, (1,)).astype(jnp.int32)
    chip_idx = jnp.reshape(chip, (1,)).astype(jnp.int32)
    g_wout_blocks = _grad_w(ocat, dh_bf, "grad_w_out").reshape(4, 2, RO, DM)
    (d_o, dmain, g_hnw, g_ln, g_wsp, g_bs_t), (sib_wout,) = _mix_bwd(
        d_ocat, o, proj, head_norm_w, sgu_ln_w, sgu_ln_b, w_spatial[0], bs_t, H, D, G, P, [g_wout_blocks])
    chip_wout = _pair_sum(g_wout_blocks, sib_wout, core_idx, "pair_sum_w_out")
    (dq, dk, dv, dgate, dpar), (recv_wout,) = _delta_bwd(
        q, k, v, gcol, grow, ba, vnew, ssave, asave, d_o, a_log, dt_bias, H, D, [chip_wout])
    dmain, g_conv_part = _prep_a_bwd(dq, dk, dv, c, proj, conv_full, dmain, H, D)
    dba = dgate.astype(BF16)
    keep_win, sib_win = _grad_w_in(xn, dmain, dba, WD, offs[2], offs[4])
    chip_win = _pair_sum_plain(keep_win, sib_win, "pair_sum_w_in")
    small_shapes = [a_log.shape, dt_bias.shape, head_norm_w.shape, sgu_ln_w.shape, sgu_ln_b.shape,
                    w_spatial.shape, b_spatial.shape, final_norm_w.shape]
    parts = [dpar[0, :H], dpar[0, H:2 * H], g_hnw[0], g_ln[0], g_ln[1], g_wsp, g_bs_t[:, :G].T, g_fnw[0],
             g_conv_part[:4]]
    grad_x, g_nw, small_gath, recv_win = _dx(dmain, dba, w_main, w_ba, x2, dh, norm_w, chip_win, _pack(parts))
    red = _sum_slots(small_gath)
    grad_w_in, delta_w_in, new_m_w_in, new_v_w_in = _sum_adam(
        chip_win, recv_win, w_in[0].T, m_w_in[0].T, v_w_in[0].T, chip_idx, "sum_adam_w_in", transposed=True)
    grad_w_out, delta_w_out, new_m_w_out, new_v_w_out = _sum_adam(
        chip_wout, recv_wout, w_out[0], m_w_out[0], v_w_out[0], chip_idx, "sum_adam_w_out")
    red_nw = _all_reduce_small(_pack([g_nw[0]]))
    grads_small = _unpack(red_nw, [norm_w.shape]) + _unpack(red, small_shapes + [(4, 3 * AW)])
    g_conv_full = grads_small.pop()
    grad_conv = lax.dynamic_slice_in_dim(g_conv_full, dev * CW, CW, axis=1)[None]
    small_w = [norm_w, a_log, dt_bias, head_norm_w, sgu_ln_w, sgu_ln_b, w_spatial, b_spatial, final_norm_w, conv_w]
    small_m = [m_norm_w, m_a_log, m_dt_bias, m_head_norm_w, m_sgu_ln_w, m_sgu_ln_b, m_w_spatial, m_b_spatial,
               m_final_norm_w, m_conv_w]
    small_v = [v_norm_w, v_a_log, v_dt_bias, v_head_norm_w, v_sgu_ln_w, v_sgu_ln_b, v_w_spatial, v_b_spatial,
               v_final_norm_w, v_conv_w]
    small_g = grads_small + [grad_conv]
    shapes10 = [w.shape for w in small_w]
    d_p, m_p, v_p = _adam_small(_pack(small_w), _pack(small_g), _pack(small_m), _pack(small_v))
    d_s, m_s, v_s = _unpack(d_p, shapes10), _unpack(m_p, shapes10), _unpack(v_p, shapes10)

    def order(small, win, wout):
        return [small[0], win.T[None], small[9], small[1], small[2], small[3], small[4], small[5], small[6], small[7],
                wout[None], small[8]]

    grads = order(small_g, grad_w_in, grad_w_out)
    deltas = order(d_s, delta_w_in, delta_w_out)
    new_m = order(m_s, new_m_w_in, new_m_w_out)
    new_v = order(v_s, new_v_w_in, new_v_w_out)
    return (loss, grad_x[None], *grads, *deltas, *new_m, *new_v)
```

```python
import functools

import jax
import jax.numpy as jnp
from jax import lax
from jax.experimental import pallas as pl
from jax.experimental.pallas import tpu as pltpu

F32 = jnp.float32
BF16 = jnp.bfloat16
MXU = jnp.bfloat16
HI = lax.Precision.HIGHEST
EPS = 1e-6
CHUNK_A = 64
LANES = 128
MESH = pl.DeviceIdType.MESH
AXES = ("x", "y", "c")
N_DEV = 8

ADAM_LR = 0.001
ADAM_B1 = 0.9
ADAM_B2 = 0.999
ADAM_EPS = 1e-08
ADAM_WD = 0.01
ADAM_STEP = 10

S = jax.ShapeDtypeStruct
ARB = "arbitrary"


def _cp(*sem, vmem_mib=56):
    return pltpu.CompilerParams(dimension_semantics=tuple(sem), vmem_limit_bytes=vmem_mib * 1024 * 1024)


def _tile(n, cap, mult):
    best = None
    t = mult
    while t <= min(n, cap):
        if n % t == 0:
            best = t
        t += mult
    return best if best is not None else n


def _mm(a, b):
    return jnp.dot(a.astype(MXU), b.astype(MXU), preferred_element_type=F32)


def _mm_nt(a, b):
    return lax.dot_general(a.astype(MXU), b.astype(MXU), (((1,), (1,)), ((), ())), preferred_element_type=F32)


def _mm_tn(a, b):
    return lax.dot_general(a.astype(MXU), b.astype(MXU), (((0,), (0,)), ((), ())), preferred_element_type=F32)


def _mmh(a, b):
    return jnp.dot(a, b, precision=HI, preferred_element_type=F32)


def _mmh_tn(a, b):
    return lax.dot_general(a, b, (((0,), (0,)), ((), ())), precision=HI, preferred_element_type=F32)


def _sigmoid(x):
    return 1.0 / (1.0 + jnp.exp(-x))


def _silu(x):
    return x * _sigmoid(x)


def _dsilu(x):
    s = _sigmoid(x)
    return s * (1.0 + x * (1.0 - s))


def _softplus(x):
    return jnp.maximum(x, 0.0) + jnp.log(1.0 + jnp.exp(-jnp.abs(x)))


def _pieces(wd, gate_lo, gate_hi, total):
    out = []
    for d in range(N_DEV):
        lo, hi = d * wd, (d + 1) * wd
        for dest, a, b, shift in (("main", 0, gate_lo, 0), ("gate", gate_lo, gate_hi, -gate_lo),
                                  ("main", gate_hi, total, gate_lo - gate_hi)):
            s0, s1 = max(lo, a), min(hi, b)
            if s0 < s1:
                out.append((d, s0 - lo, s1 - lo, dest, s0 + shift))
    return out


def _cast_bf16(a, name):
    R, C = a.shape
    tr = _tile(R, 256, 16)

    def body(a_ref, o_ref):
        o_ref[...] = a_ref[...].astype(BF16)

    spec = pl.BlockSpec((tr, C), lambda i: (i, 0))
    return pl.pallas_call(body, name=name, grid=(R // tr,), in_specs=[spec], out_specs=spec,
                          out_shape=S((R, C), BF16), compiler_params=_cp(ARB))(a)


def _cast_bf16_t(a_t, name):
    C, R = a_t.shape
    tr = _tile(R, 256, LANES)

    def body(a_ref, o_ref):
        o_ref[...] = a_ref[...].T.astype(BF16)

    return pl.pallas_call(body, name=name, grid=(R // tr,), in_specs=[pl.BlockSpec((C, tr), lambda i: (0, i))],
                          out_specs=pl.BlockSpec((tr, C), lambda i: (i, 0)),
                          out_shape=S((R, C), BF16), compiler_params=_cp(ARB))(a_t)


def _relayout_w(g_win, gate_lo, gate_hi):
    _, DM, WD = g_win.shape
    total = N_DEV * WD
    NM = total - (gate_hi - gate_lo)
    tr = _tile(DM, 256, 16)
    plan = _pieces(WD, gate_lo, gate_hi, total)

    def body(g_ref, main_ref, gate_ref):
        gate_ref[...] = jnp.zeros_like(gate_ref)
        for d, s0, s1, dest, c0 in plan:
            dst = main_ref if dest == "main" else gate_ref
            dst[:, c0:c0 + (s1 - s0)] = g_ref[d, :, s0:s1]

    return pl.pallas_call(
        body, name="relayout_w", grid=(DM // tr,),
        in_specs=[pl.BlockSpec((N_DEV, tr, WD), lambda i: (0, i, 0))],
        out_specs=[pl.BlockSpec((tr, NM), lambda i: (i, 0)), pl.BlockSpec((tr, LANES), lambda i: (i, 0))],
        out_shape=[S((DM, NM), g_win.dtype), S((DM, LANES), g_win.dtype)],
        compiler_params=_cp(ARB),
    )(g_win)


def _rms_xn(x, norm_w):
    T, DM = x.shape
    tm = _tile(T, 512, 16)

    def body(x_ref, nw_ref, o_ref):
        xv = x_ref[...]
        r = lax.rsqrt(jnp.mean(xv * xv, axis=-1, keepdims=True) + EPS)
        o_ref[...] = (xv * r * nw_ref[...]).astype(BF16)

    return pl.pallas_call(
        body, name="rms_xn", grid=(T // tm,),
        in_specs=[pl.BlockSpec((tm, DM), lambda i: (i, 0)), pl.BlockSpec((1, DM), lambda i: (0, 0))],
        out_specs=pl.BlockSpec((tm, DM), lambda i: (i, 0)),
        out_shape=S((T, DM), BF16), compiler_params=_cp(ARB),
    )(x, norm_w)


def _in_proj(xn, w_main, w_ba, shards):
    T, DM = xn.shape
    NM = w_main.shape[1]
    tm = _tile(T, 2048, 16)
    tn = _tile(NM, 1024, LANES)
    ni, nj = T // tm, NM // tn
    ns = len(shards)

    def body(xn_ref, w_ref, wba_ref, *rest):
        srcs = rest[:ns]
        proj_ref, ba_ref = rest[ns:ns + 2]
        gath = rest[ns + 2:2 * ns + 2]
        send_sems, recv_sems, local_sems = rest[2 * ns + 2:]
        i = pl.program_id(0)
        j = pl.program_id(1)
        me, cps = _broadcast_copies(srcs, gath, send_sems, recv_sems)
        cps = cps + [pltpu.make_async_copy(srcs[a], gath[a].at[me], local_sems.at[a]) for a in range(ns)]

        @pl.when((i == 0) & (j == 0))
        def _():
            for cp in cps:
                cp.start()

        @pl.when(j == 0)
        def _():
            ba_ref[...] = jnp.dot(xn_ref[...].astype(MXU), wba_ref[...].astype(MXU), preferred_element_type=F32)

        proj_ref[...] = jnp.dot(xn_ref[...].astype(MXU), w_ref[...].astype(MXU), preferred_element_type=F32)

        @pl.when((i == ni - 1) & (j == nj - 1))
        def _():
            for cp in cps:
                cp.wait()

    any_spec = pl.BlockSpec(memory_space=pl.ANY)
    res = pl.pallas_call(
        body, name="in_proj", grid=(ni, nj),
        in_specs=[pl.BlockSpec((tm, DM), lambda i, j: (i, 0)),
                  pl.BlockSpec((DM, tn), lambda i, j: (0, j)),
                  pl.BlockSpec((DM, LANES), lambda i, j: (0, 0))] + [any_spec] * ns,
        out_specs=[pl.BlockSpec((tm, tn), lambda i, j: (i, j)),
                   pl.BlockSpec((tm, LANES), lambda i, j: (i, 0))] + [any_spec] * ns,
        out_shape=[S((T, NM), F32), S((T, LANES), F32)] + [S((N_DEV,) + a.shape, a.dtype) for a in shards],
        scratch_shapes=[pltpu.SemaphoreType.DMA((ns, N_DEV - 1)), pltpu.SemaphoreType.DMA((ns, N_DEV - 1)),
                        pltpu.SemaphoreType.DMA((ns,))],
        compiler_params=_cp(ARB, ARB, vmem_mib=58),
    )(xn, w_main, w_ba, *shards)
    return res[0], res[1], res[2:]


def _prep_a_fwd(proj, ba, conv_w, alog_row, dtb_row, H, D):
    T = proj.shape[0]
    AW = H * D
    C3 = 3 * AW
    tb = _tile(T, 256, CHUNK_A)
    nch = tb // CHUNK_A
    nblk = T // tb
    scale = float(D) ** -0.5

    def body(x_ref, halo_ref, ba_ref, cw_ref, al_ref, dt_ref, q_ref, k_ref, v_ref, c_ref, gcol_ref, grow_ref):
        i = pl.program_id(0)
        xv = x_ref[...]
        halo = halo_ref[...] * (i > 0).astype(F32)
        xp = jnp.concatenate([halo, xv], axis=0)
        cw = cw_ref[...]
        c = cw[0:1, :] * xp[5:5 + tb]
        for j in range(1, 4):
            c = c + cw[j:j + 1, :] * xp[5 + j:5 + j + tb]
        c_ref[...] = c
        a = _silu(c)
        for h in range(H):
            qh = a[:, h * D:(h + 1) * D]
            kh = a[:, AW + h * D:AW + (h + 1) * D]
            qr = lax.rsqrt(jnp.sum(qh * qh, axis=-1, keepdims=True) + EPS)
            kr = lax.rsqrt(jnp.sum(kh * kh, axis=-1, keepdims=True) + EPS)
            q_ref[:, h * D:(h + 1) * D] = qh * (qr * scale)
            k_ref[:, h * D:(h + 1) * D] = kh * kr
        v_ref[...] = a[:, 2 * AW:]

        bav = ba_ref[...]
        lane = lax.broadcasted_iota(jnp.int32, (tb, LANES), 1)
        beta = _sigmoid(bav)
        g = -jnp.exp(al_ref[...]) * _softplus(bav + dt_ref[...])
        gates = jnp.where(lane < H, beta, jnp.where(lane < 2 * H, g, 0.0))
        ri = lax.broadcasted_iota(jnp.int32, (CHUNK_A, CHUNK_A), 0)
        ci = lax.broadcasted_iota(jnp.int32, (CHUNK_A, CHUNK_A), 1)
        tri = (ri >= ci).astype(F32)
        lane_c = lax.broadcasted_iota(jnp.int32, (CHUNK_A, LANES), 1)
        for cc in range(nch):
            gch = gates[cc * CHUNK_A:(cc + 1) * CHUNK_A]
            gc = pltpu.roll(_mmh(tri, gch), H, 1)
            full = jnp.where(lane_c < 2 * H, gch, jnp.where(lane_c < 3 * H, gc, 0.0))
            gcol_ref[cc * CHUNK_A:(cc + 1) * CHUNK_A, :] = full
            grow_ref[cc] = full.T[0:32, :]

    return pl.pallas_call(
        body, name="prep_a_fwd", grid=(nblk,),
        in_specs=[pl.BlockSpec((tb, C3), lambda i: (i, 0)),
                  pl.BlockSpec((8, C3), lambda i: (jnp.maximum(i * (tb // 8) - 1, 0), 0)),
                  pl.BlockSpec((tb, LANES), lambda i: (i, 0)),
                  pl.BlockSpec((4, C3), lambda i: (0, 0)),
                  pl.BlockSpec((1, LANES), lambda i: (0, 0)),
                  pl.BlockSpec((1, LANES), lambda i: (0, 0))],
        out_specs=[pl.BlockSpec((tb, AW), lambda i: (i, 0)),
                   pl.BlockSpec((tb, AW), lambda i: (i, 0)),
                   pl.BlockSpec((tb, AW), lambda i: (i, 0)),
                   pl.BlockSpec((tb, C3), lambda i: (i, 0)),
                   pl.BlockSpec((tb, LANES), lambda i: (i, 0)),
                   pl.BlockSpec((nch, 32, CHUNK_A), lambda i: (i, 0, 0))],
        out_shape=[S((T, AW), F32), S((T, AW), F32), S((T, AW), F32), S((T, C3), F32),
                   S((T, LANES), F32), S((T // CHUNK_A, 32, CHUNK_A), F32)],
        compiler_params=_cp(ARB),
    )(proj, proj, ba, conv_w, alog_row, dtb_row)


_NN = (((1,), (0,)), ((), ()))
_TN = (((0,), (0,)), ((), ()))


def _split(a):
    hi = a.astype(BF16)
    return hi, (a - hi.astype(F32)).astype(BF16)


def _mm3(a, b, dims=_NN):
    ah, al = a if isinstance(a, tuple) else _split(a)
    bh, bl = b if isinstance(b, tuple) else _split(b)
    dg = lambda p, r: lax.dot_general(p, r, dims, preferred_element_type=F32)
    return dg(ah, bh) + (dg(ah, bl) + dg(al, bh))


def _interleave(gens):
    gens = list(gens)
    while gens:
        alive = []
        for g in gens:
            try:
                next(g)
                alive.append(g)
            except StopIteration:
                pass
        gens = alive


def _chunk_terms(q, k, v, gcolv, growv, h, H):
    C = CHUNK_A
    beta_c = gcolv[:, h:h + 1]
    g_c = gcolv[:, H + h:H + h + 1]
    gc_c = gcolv[:, 2 * H + h:2 * H + h + 1]
    gc_r = growv[2 * H + h:2 * H + h + 1, :]
    ri = lax.broadcasted_iota(jnp.int32, (C, C), 0)
    ci = lax.broadcasted_iota(jnp.int32, (C, C), 1)
    incl = ri >= ci
    strict = ri > ci
    kb = k * beta_c
    vb = v * beta_c
    p_raw = _mm_nt(kb, k)
    qk_raw = _mm_nt(q, k)
    gam = jnp.where(incl, jnp.exp(jnp.where(incl, gc_c - gc_r, 0.0)), 0.0)
    e_c = jnp.exp(gc_c)
    gl = gc_r[:, C - 1:C]
    edec = jnp.exp(gl - gc_c)
    yield
    lmat = jnp.where(strict, p_raw * gam, 0.0)
    attn = jnp.where(incl, qk_raw * gam, 0.0)
    return dict(beta_c=beta_c, g_c=g_c, gc_c=gc_c, gc_r=gc_r, incl=incl, strict=strict, gam=gam, e_c=e_c,
                kb=kb, vb=vb, lmat=lmat, attn=attn, gl=gl, edec=edec, ri=ri, ci=ci)


def _inv_unit_lower(lmat):
    C = lmat.shape[0]
    ri = lax.broadcasted_iota(jnp.int32, (C, C), 0)
    ci = lax.broadcasted_iota(jnp.int32, (C, C), 1)
    eye = (ri == ci).astype(F32)
    x = -lmat
    a = eye + x
    n = 1
    while 2 * n < C:
        xs = _split(x)
        x = _mm3(xs, xs)
        yield
        a = a + _mm3(a, x)
        n *= 2
    yield
    return a


def _delta_fwd(q, k, v, gcol, grow, H, D):
    T = q.shape[0]
    C = CHUNK_A
    N = T // C
    AW = H * D
    CPS = 2 if N % 2 == 0 else 1

    def body(q_ref, k_ref, v_ref, gcol_ref, grow_ref, o_ref, vn_ref, ssave_ref, asave_ref, s_ref):
        @pl.when(pl.program_id(0) == 0)
        def _():
            s_ref[...] = jnp.zeros_like(s_ref)

        state = {(0, h): s_ref[h] for h in range(H)}

        def head(cc, h):
            rows = slice(cc * C, (cc + 1) * C)
            sl = slice(h * D, (h + 1) * D)
            qv, kv, vv = q_ref[rows, sl], k_ref[rows, sl], v_ref[rows, sl]
            t = yield from _chunk_terms(qv, kv, vv, gcol_ref[rows, :], grow_ref[cc], h, H)
            a = yield from _inv_unit_lower(t["lmat"])
            asave_ref[cc, h] = a
            while (cc, h) not in state:
                yield
            st = state[(cc, h)]
            ssave_ref[cc, h] = st
            ks = _mm(t["kb"] * t["e_c"], st)
            o_inter = _mm(qv * t["e_c"], st)
            yield
            v_new = _mm3(a, t["vb"] - ks)
            yield
            vn_ref[rows, sl] = v_new
            o_intra = _mm(t["attn"], v_new)
            s_upd = _mm_tn(kv * t["edec"], v_new)
            yield
            o_ref[rows, sl] = o_inter + o_intra
            state[(cc + 1, h)] = st * jnp.exp(t["gl"]) + s_upd

        _interleave(head(cc, h) for cc in range(CPS) for h in range(H))
        for h in range(H):
            s_ref[h] = state[(CPS, h)]

    blk = lambda: pl.BlockSpec((CPS * C, AW), lambda n: (n, 0))
    return pl.pallas_call(
        body, name="delta_fwd", grid=(N // CPS,),
        in_specs=[blk(), blk(), blk(),
                  pl.BlockSpec((CPS * C, LANES), lambda n: (n, 0)),
                  pl.BlockSpec((CPS, 32, C), lambda n: (n, 0, 0))],
        out_specs=[blk(), blk(),
                   pl.BlockSpec((CPS, H, D, D), lambda n: (n, 0, 0, 0)),
                   pl.BlockSpec((CPS, H, C, C), lambda n: (n, 0, 0, 0))],
        out_shape=[S((T, AW), F32), S((T, AW), F32), S((N, H, D, D), F32), S((N, H, C, C), F32)],
        scratch_shapes=[pltpu.VMEM((H, D, D), F32)],
        compiler_params=_cp(ARB),
    )(q, k, v, gcol, grow)


def _delta_bwd(q, k, v, gcol, grow, ba, vnew, ssave, asave, d_o, a_log, dt_bias, H, D, carry):
    T = q.shape[0]
    C = CHUNK_A
    N = T // C
    AW = H * D
    nc = len(carry)
    CPS = 2 if N % 2 == 0 else 1
    NS = N // CPS

    def body(al_ref, dt_ref, q_ref, k_ref, v_ref, gcol_ref, grow_ref, ba_ref, vn_ref, ss_ref, as_ref, do_ref, *rest):
        cins = rest[:nc]
        dq_ref, dk_ref, dv_ref, dgate_ref, dpar_ref = rest[nc:nc + 5]
        couts = rest[nc + 5:2 * nc + 5]
        ds_ref, csend, crecv = rest[2 * nc + 5:]
        ccps = _chip_exchange_copies(cins, couts, csend, crecv)

        @pl.when(pl.program_id(0) == 0)
        def _():
            ds_ref[...] = jnp.zeros_like(ds_ref)
            dpar_ref[...] = jnp.zeros_like(dpar_ref)
            for cp in ccps:
                cp.start()

        lane = lax.broadcasted_iota(jnp.int32, (C, LANES), 1)
        rowi = lax.broadcasted_iota(jnp.int32, (C, 1), 0)
        acc = {cc: jnp.zeros((C, LANES), F32) for cc in range(CPS)}
        state = {(0, h): ds_ref[h] for h in range(H)}

        def head(oi, h):
            cc = CPS - 1 - oi
            rows = slice(cc * C, (cc + 1) * C)
            sl = slice(h * D, (h + 1) * D)
            st = ss_ref[cc, h]
            a = as_ref[cc, h]
            qv, kv, vv, dov, v_new = q_ref[rows, sl], k_ref[rows, sl], v_ref[rows, sl], do_ref[rows, sl], vn_ref[rows, sl]
            t = yield from _chunk_terms(qv, kv, vv, gcol_ref[rows, :], grow_ref[cc], h, H)
            beta_c, e_c, gam, kb = t["beta_c"], t["e_c"], t["gam"], t["kb"]
            incl, strict, attn, lmat, edec = t["incl"], t["strict"], t["attn"], t["lmat"], t["edec"]
            kdec = kv * edec
            egl = jnp.exp(t["gl"])
            qe = qv * e_c
            ekb = kb * e_c

            t1 = _mm_nt(dov, st)
            ds_o = _mm_tn(qe, dov)
            dattn_raw = _mm_nt(dov, v_new)
            dv_new_o = _mm_tn(attn, dov)
            yield
            while (oi, h) not in state:
                yield
            ds_next = state[(oi, h)]
            dkdec = _mm_nt(v_new, ds_next)
            dv_new_s = _mm(kdec, ds_next)
            yield
            dgl = egl * jnp.sum(jnp.sum(st * ds_next, axis=1, keepdims=True), axis=0, keepdims=True)
            dk = edec * dkdec
            r = jnp.sum(dkdec * kdec, axis=1, keepdims=True)
            dgc = -r
            dgl = dgl + jnp.sum(r, axis=0, keepdims=True)
            dq = e_c * t1
            dgc = dgc + jnp.sum(t1 * qe, axis=1, keepdims=True)
            dattn = jnp.where(incl, dattn_raw, 0.0)
            dv_new = dv_new_s + dv_new_o
            dqm = dattn * gam
            z = dattn * attn
            dvb = _mm3(a, dv_new, _TN)
            dq_a = _mm(dqm, kv)
            dk_a = _mm_tn(dqm, qv)
            yield
            dq_ref[rows, sl] = dq + dq_a
            dv_ref[rows, sl] = beta_c * dvb
            ds_kb = _mm_tn(ekb, dvb)
            dekb_neg = _mm_nt(dvb, st)
            dl_neg = _mm_nt(dvb, v_new)
            yield
            state[(oi + 1, h)] = egl * ds_next + ds_o - ds_kb
            dekb = -dekb_neg
            dl = jnp.where(strict, -dl_neg, 0.0)
            dp = dl * gam
            z = z + dl * lmat
            dkb_p = _mm(dp, kv)
            dk_p = _mm_tn(dp, kb)
            dgc = dgc + jnp.sum(dekb * ekb, axis=1, keepdims=True)
            dgc = dgc + jnp.sum(z, axis=1, keepdims=True) - jnp.sum(z.T, axis=1, keepdims=True)
            dgc = dgc + jnp.where(rowi == C - 1, dgl, 0.0)
            yield
            dkb = dkb_p + e_c * dekb
            dk_ref[rows, sl] = dk + dk_a + dk_p + beta_c * dkb
            dbeta = jnp.sum(dkb * kv, axis=1, keepdims=True) + jnp.sum(dvb * vv, axis=1, keepdims=True)
            acc[cc] = acc[cc] + jnp.where(lane == h, dbeta, 0.0) + jnp.where(lane == H + h, dgc, 0.0)

        _interleave(head(oi, h) for oi in range(CPS) for h in range(H))
        for h in range(H):
            ds_ref[h] = state[(CPS, h)]
        ri = lax.broadcasted_iota(jnp.int32, (C, C), 0)
        ci = lax.broadcasted_iota(jnp.int32, (C, C), 1)
        upper = (ri <= ci).astype(F32)
        dal = jnp.zeros((1, LANES), F32)
        ddt = jnp.zeros((1, LANES), F32)
        for cc in range(CPS):
            rows = slice(cc * C, (cc + 1) * C)
            gates = gcol_ref[rows, :]
            dg_all = _mm3(upper, acc[cc])
            d_braw = acc[cc] * gates * (1.0 - gates)
            d_araw = dg_all * (-jnp.exp(al_ref[...])) * _sigmoid(ba_ref[rows, :] + dt_ref[...])
            dgate_ref[rows, :] = jnp.where(lane < H, d_braw, jnp.where(lane < 2 * H, d_araw, 0.0))
            dal = dal + jnp.sum(dg_all * gates, axis=0, keepdims=True)
            ddt = ddt + jnp.sum(d_araw, axis=0, keepdims=True)
        dpar_ref[0:1, :] += dal
        dpar_ref[1:2, :] += ddt

        @pl.when(pl.program_id(0) == NS - 1)
        def _():
            for cp in ccps:
                cp.wait()

    rev = lambda s: NS - 1 - s
    blk = lambda: pl.BlockSpec((CPS * C, AW), lambda s: (rev(s), 0))
    row = pl.BlockSpec((1, LANES), lambda s: (0, 0))
    any_spec = pl.BlockSpec(memory_space=pl.ANY)
    res = pl.pallas_call(
        body, name="delta_bwd", grid=(NS,),
        in_specs=[row, row, blk(), blk(), blk(),
                  pl.BlockSpec((CPS * C, LANES), lambda s: (rev(s), 0)),
                  pl.BlockSpec((CPS, 32, C), lambda s: (rev(s), 0, 0)),
                  pl.BlockSpec((CPS * C, LANES), lambda s: (rev(s), 0)),
                  blk(),
                  pl.BlockSpec((CPS, H, D, D), lambda s: (rev(s), 0, 0, 0)),
                  pl.BlockSpec((CPS, H, C, C), lambda s: (rev(s), 0, 0, 0)),
                  blk()] + [any_spec] * nc,
        out_specs=[blk(), blk(), blk(),
                   pl.BlockSpec((CPS * C, LANES), lambda s: (rev(s), 0)),
                   pl.BlockSpec((8, LANES), lambda s: (0, 0))] + [any_spec] * nc,
        out_shape=[S((T, AW), F32), S((T, AW), F32), S((T, AW), F32),
                   S((T, LANES), F32), S((8, LANES), F32)] + [S((3,) + a.shape[1:], a.dtype) for a in carry],
        scratch_shapes=[pltpu.VMEM((H, D, D), F32),
                        pltpu.SemaphoreType.DMA((max(nc, 1), 3)), pltpu.SemaphoreType.DMA((max(nc, 1), 3))],
        compiler_params=_cp(ARB),
    )(a_log, dt_bias, q, k, v, gcol, grow, ba, vnew, ssave, asave, d_o, *carry)
    return res[:5], res[5:]


def _ln_stats(xv):
    mu = jnp.mean(xv, axis=-1, keepdims=True)
    xc = xv - mu
    var = jnp.mean(xc * xc, axis=-1, keepdims=True)
    rstd = lax.rsqrt(var + EPS)
    return xc * rstd, rstd


def _mix_fwd(o, proj, head_norm_w, ln_w, ln_b, w_sp, bs_t, H, D, G, P):
    T = o.shape[0]
    AW, BW = H * D, G * P
    MIX = AW + BW
    nb = AW // BW if AW % BW == 0 else None
    assert nb == 1, "group widths must match the projection column blocks"
    cb = 3

    def body(o_ref, za_ref, ub_ref, vb_ref, zb_ref, hw_ref, lw_ref, lb_ref, w_ref, bs_ref, out_ref):
        hw = hw_ref[...]
        for h in range(H):
            sl = slice(h * D, (h + 1) * D)
            oh = o_ref[:, sl]
            rs = lax.rsqrt(jnp.mean(oh * oh, axis=-1, keepdims=True) + EPS)
            out_ref[:, sl] = (oh * rs * hw * _silu(za_ref[:, sl])).astype(BF16)
        xhat, _ = _ln_stats(vb_ref[...])
        vn = xhat * lw_ref[...] + lb_ref[...]
        ri = lax.broadcasted_iota(jnp.int32, (P, P), 0)
        ci = lax.broadcasted_iota(jnp.int32, (P, P), 1)
        bsv = bs_ref[...]
        for g in range(G):
            sl = slice(g * P, (g + 1) * P)
            wm = jnp.where(ri >= ci, w_ref[g], 0.0)
            s = _mm(wm, vn[:, sl]) + bsv[:, g:g + 1]
            out_ref[:, AW + g * P:AW + (g + 1) * P] = (ub_ref[:, sl] * s * _silu(zb_ref[:, sl])).astype(BF16)

    row = lambda w: pl.BlockSpec((1, w), lambda i: (0, 0))
    return pl.pallas_call(
        body, name="mix_fwd", grid=(T // P,),
        in_specs=[pl.BlockSpec((P, AW), lambda i: (i, 0)),
                  pl.BlockSpec((P, AW), lambda i: (i, cb)),
                  pl.BlockSpec((P, BW), lambda i: (i, cb + 1)),
                  pl.BlockSpec((P, BW), lambda i: (i, cb + 2)),
                  pl.BlockSpec((P, BW), lambda i: (i, cb + 3)),
                  row(D), row(BW), row(BW),
                  pl.BlockSpec((G, P, P), lambda i: (0, 0, 0)),
                  pl.BlockSpec((P, G), lambda i: (0, 0))],
        out_specs=pl.BlockSpec((P, MIX), lambda i: (i, 0)),
        out_shape=S((T, MIX), BF16),
        compiler_params=_cp(ARB),
    )(o, proj, proj, proj, proj, head_norm_w, ln_w, ln_b, w_sp, bs_t)


def _mix_bwd(d_ocat, o, proj, head_norm_w, ln_w, ln_b, w_sp, bs_t, H, D, G, P, carry):
    T = o.shape[0]
    AW, BW = H * D, G * P
    MIX = AW + BW
    cb = 3
    nc = len(carry)

    def body(dc_ref, o_ref, za_ref, ub_ref, vb_ref, zb_ref, hw_ref, lw_ref, lb_ref, w_ref, bs_ref, *rest):
        cins = rest[:nc]
        do_ref, dmain_ref, dhw_ref, dln_ref, dw_ref, dbs_ref = rest[nc:nc + 6]
        couts = rest[nc + 6:2 * nc + 6]
        dvn_ref, drest_ref, out_sems, csend, crecv = rest[2 * nc + 6:]
        i = pl.program_id(0)
        slot = lax.rem(i, 2)
        ccps = _sibling_copies(cins, couts, csend, crecv)

        def out_copy(step, s):
            return pltpu.make_async_copy(
                drest_ref.at[s], dmain_ref.at[pl.ds(step * P, P), pl.ds(cb * AW, AW + 3 * BW)], out_sems.at[s])

        @pl.when(i == 0)
        def _():
            dhw_ref[...] = jnp.zeros_like(dhw_ref)
            dln_ref[...] = jnp.zeros_like(dln_ref)
            dw_ref[...] = jnp.zeros_like(dw_ref)
            dbs_ref[...] = jnp.zeros_like(dbs_ref)
            for cp in ccps:
                cp.start()

        @pl.when(i >= 2)
        def _():
            out_copy(i - 2, slot).wait()

        hw = hw_ref[...]
        dhw = jnp.zeros((1, D), F32)
        for h in range(H):
            sl = slice(h * D, (h + 1) * D)
            oh = o_ref[:, sl]
            za = za_ref[:, sl]
            doa = dc_ref[:, sl]
            rs = lax.rsqrt(jnp.mean(oh * oh, axis=-1, keepdims=True) + EPS)
            xh = oh * rs
            d_on = doa * _silu(za)
            drest_ref[slot, :, sl] = (doa * (xh * hw) * _dsilu(za)).astype(BF16)
            dhw = dhw + jnp.sum(d_on * xh, axis=0, keepdims=True)
            dxh = d_on * hw
            do_ref[:, sl] = rs * (dxh - xh * jnp.mean(dxh * xh, axis=-1, keepdims=True))
        dhw_ref[0:1, :] += dhw

        xhat, rstd = _ln_stats(vb_ref[...])
        lw = lw_ref[...]
        vn = xhat * lw + lb_ref[...]
        ri = lax.broadcasted_iota(jnp.int32, (P, P), 0)
        ci = lax.broadcasted_iota(jnp.int32, (P, P), 1)
        lane = lax.broadcasted_iota(jnp.int32, (P, LANES), 1)
        bsv = bs_ref[...]
        dbs = jnp.zeros((P, LANES), F32)
        for g in range(G):
            sl = slice(g * P, (g + 1) * P)
            wm = jnp.where(ri >= ci, w_ref[g], 0.0)
            vng = vn[:, sl]
            s = _mm(wm, vng) + bsv[:, g:g + 1]
            dob = dc_ref[:, AW + g * P:AW + (g + 1) * P]
            ub = ub_ref[:, sl]
            zb = zb_ref[:, sl]
            szb = _silu(zb)
            drest_ref[slot, :, AW + g * P:AW + (g + 1) * P] = (dob * s * szb).astype(BF16)
            drest_ref[slot, :, AW + 2 * BW + g * P:AW + 2 * BW + (g + 1) * P] = (
                dob * ub * s * _dsilu(zb)).astype(BF16)
            ds = dob * ub * szb
            dvn_ref[:, sl] = _mm_tn(wm, ds)
            dw_ref[g] += jnp.where(ri >= ci, _mm_nt(ds, vng), 0.0)
            dbs = dbs + jnp.where(lane == g, jnp.sum(ds, axis=1, keepdims=True), 0.0)
        dbs_ref[...] += dbs
        dvn = dvn_ref[...]
        dln_ref[0:1, :] += jnp.sum(dvn * xhat, axis=0, keepdims=True)
        dln_ref[1:2, :] += jnp.sum(dvn, axis=0, keepdims=True)
        dxh = dvn * lw
        dvb = rstd * (dxh - jnp.mean(dxh, axis=-1, keepdims=True) - xhat * jnp.mean(dxh * xhat, axis=-1, keepdims=True))
        drest_ref[slot, :, AW + BW:AW + 2 * BW] = dvb.astype(BF16)

        out_copy(i, slot).start()

        @pl.when(i == nstep - 1)
        def _():
            out_copy(i, slot).wait()
            if nstep > 1:
                out_copy(i - 1, 1 - slot).wait()
            for cp in ccps:
                cp.wait()

    nstep = T // P
    row = lambda w: pl.BlockSpec((1, w), lambda i: (0, 0))
    any_spec = pl.BlockSpec(memory_space=pl.ANY)
    res = pl.pallas_call(
        body, name="mix_bwd", grid=(nstep,),
        in_specs=[pl.BlockSpec((P, MIX), lambda i: (i, 0)),
                  pl.BlockSpec((P, AW), lambda i: (i, 0)),
                  pl.BlockSpec((P, AW), lambda i: (i, cb)),
                  pl.BlockSpec((P, BW), lambda i: (i, cb + 1)),
                  pl.BlockSpec((P, BW), lambda i: (i, cb + 2)),
                  pl.BlockSpec((P, BW), lambda i: (i, cb + 3)),
                  row(D), row(BW), row(BW),
                  pl.BlockSpec((G, P, P), lambda i: (0, 0, 0)),
                  pl.BlockSpec((P, G), lambda i: (0, 0))] + [any_spec] * nc,
        out_specs=[pl.BlockSpec((P, AW), lambda i: (i, 0)),
                   any_spec,
                   pl.BlockSpec((8, D), lambda i: (0, 0)),
                   pl.BlockSpec((8, BW), lambda i: (0, 0)),
                   pl.BlockSpec((G, P, P), lambda i: (0, 0, 0)),
                   pl.BlockSpec((P, LANES), lambda i: (0, 0))] + [any_spec] * nc,
        out_shape=[S((T, AW), F32), S((T, cb * AW + AW + 3 * BW), BF16), S((8, D), F32), S((8, BW), F32),
                   S((G, P, P), F32), S((P, LANES), F32)] + [S(a.shape[:1] + a.shape[2:], a.dtype) for a in carry],
        scratch_shapes=[pltpu.VMEM((P, BW), F32), pltpu.VMEM((2, P, AW + 3 * BW), BF16),
                        pltpu.SemaphoreType.DMA((2,))] + _sibling_sems(carry),
        compiler_params=_cp(ARB),
    )(d_ocat, o, proj, proj, proj, proj, head_norm_w, ln_w, ln_b, w_sp, bs_t, *carry)
    return res[:6], res[6:]


def _out_proj_loss(ocat, w_out, x, target, fnw):
    T, MIX = ocat.shape
    DM = x.shape[1]
    tm = _tile(T, 256, 8)

    def body(oc_ref, w_ref, x_ref, t_ref, fw_ref, dh_ref, dhb_ref, doc_ref, loss_ref, gfw_ref):
        @pl.when(pl.program_id(0) == 0)
        def _():
            loss_ref[...] = jnp.zeros_like(loss_ref)
            gfw_ref[...] = jnp.zeros_like(gfw_ref)

        wv = w_ref[...]
        hh = x_ref[...] + jnp.dot(oc_ref[...].astype(MXU), wv.astype(MXU), preferred_element_type=F32)
        rs = lax.rsqrt(jnp.mean(hh * hh, axis=-1, keepdims=True) + EPS)
        hn = hh * rs
        fw = fw_ref[...]
        e = hn * fw - t_ref[...]
        row_loss = 0.5 * jnp.mean(e * e, axis=-1, keepdims=True)
        loss_ref[...] += jnp.sum(row_loss, axis=0, keepdims=True)
        dy = e * (1.0 / DM)
        gfw_ref[0:1, :] += jnp.sum(dy * hn, axis=0, keepdims=True)
        dhn = dy * fw
        dh = rs * (dhn - hn * jnp.mean(dhn * hn, axis=-1, keepdims=True))
        dh_ref[...] = dh
        dhb = dh.astype(BF16)
        dhb_ref[...] = dhb
        doc_ref[...] = _mm_nt(dhb, wv)

    return pl.pallas_call(
        body, name="out_proj_loss", grid=(T // tm,),
        in_specs=[pl.BlockSpec((tm, MIX), lambda i: (i, 0)),
                  pl.BlockSpec((MIX, DM), lambda i: (0, 0)),
                  pl.BlockSpec((tm, DM), lambda i: (i, 0)),
                  pl.BlockSpec((tm, DM), lambda i: (i, 0)),
                  pl.BlockSpec((1, DM), lambda i: (0, 0))],
        out_specs=[pl.BlockSpec((tm, DM), lambda i: (i, 0)),
                   pl.BlockSpec((tm, DM), lambda i: (i, 0)),
                   pl.BlockSpec((tm, MIX), lambda i: (i, 0)),
                   pl.BlockSpec((8, LANES), lambda i: (0, 0)),
                   pl.BlockSpec((8, DM), lambda i: (0, 0))],
        out_shape=[S((T, DM), F32), S((T, DM), BF16), S((T, MIX), F32), S((8, LANES), F32), S((8, DM), F32)],
        compiler_params=_cp(ARB),
    )(ocat, w_out, x, target, fnw)


def _grad_w(lhs, rhs, name):
    T, A = lhs.shape
    B = rhs.shape[1]
    ta = _tile(A, 512, LANES)
    tk = _tile(T, 1024, 16)
    nk = T // tk

    def body(l_ref, r_ref, out_ref, acc_ref):
        k = pl.program_id(1)
        part = _mm_tn(l_ref[...], r_ref[...])

        @pl.when(k == 0)
        def _():
            acc_ref[...] = part

        @pl.when(k > 0)
        def _():
            acc_ref[...] += part

        @pl.when(k == nk - 1)
        def _():
            out_ref[...] = acc_ref[...].astype(BF16)

    return pl.pallas_call(
        body, name=name, grid=(A // ta, nk),
        in_specs=[pl.BlockSpec((tk, ta), lambda i, k: (k, i)),
                  pl.BlockSpec((tk, B), lambda i, k: (k, 0))],
        out_specs=pl.BlockSpec((ta, B), lambda i, k: (i, 0)),
        out_shape=S((A, B), BF16),
        scratch_shapes=[pltpu.VMEM((ta, B), F32)],
        compiler_params=_cp(ARB, ARB),
    )(lhs, rhs)


def _grad_w_in(xn, dmain, dba, WD, gate_lo, gate_hi):
    T, DM = xn.shape
    NM = dmain.shape[1]
    tn = _tile(NM, 1024, LANES)
    tk = _tile(T, 2048, 16)
    nj, nk = NM // tn, T // tk
    ND = N_DEV
    tiles = [[] for _ in range(nj)]
    first_tile, last_tile = {}, {}
    for d, s0, s1, dest, c0 in _pieces(WD, gate_lo, gate_hi, ND * WD):
        if dest != "main":
            continue
        while s0 < s1:
            jj = c0 // tn
            w = min(s1 - s0, (jj + 1) * tn - c0)
            tiles[jj].append((d, s0, w, "main", c0 - jj * tn))
            first_tile.setdefault(d, jj)
            last_tile[d] = jj
            s0, c0 = s0 + w, c0 + w
    for d, s0, s1, dest, c0 in _pieces(WD, gate_lo, gate_hi, ND * WD):
        if dest == "gate":
            tiles[first_tile[d]].append((d, s0, s1 - s0, "gate", c0))
    assert sorted(first_tile) == list(range(ND)) and all(last_tile[d] <= first_tile[d + 2] for d in range(ND - 2))

    def body(xn_ref, dm_ref, dba_ref, keep_ref, recv_ref, acc_ref, gate_ref, buf_ref, lsem, ssem, rsem):
        j = pl.program_id(0)
        k = pl.program_id(1)
        px, py, pc = _position()

        @pl.when(k == 0)
        def _():
            acc_ref[...] = jnp.zeros_like(acc_ref)

        @pl.when((j == 0) & (k == 0))
        def _():
            gate_ref[...] = jnp.zeros_like(gate_ref)

        xv = xn_ref[...]
        acc_ref[...] += _mm_tn(xv, dm_ref[...])

        @pl.when(j == 0)
        def _():
            gate_ref[...] += _mm_tn(xv, dba_ref[...])

        def local(d):
            return pltpu.make_async_copy(buf_ref.at[d % 2], keep_ref.at[d // 2], lsem.at[d // 2])

        def remote(d):
            return pltpu.make_async_remote_copy(
                src_ref=buf_ref.at[d % 2], dst_ref=recv_ref.at[d // 2], send_sem=ssem.at[d // 2],
                recv_sem=rsem.at[d // 2], device_id=(px, py, 1 - pc), device_id_type=MESH)

        def leave(d, start):
            @pl.when(pc == d % 2)
            def _():
                local(d).start() if start else local(d).wait()

            @pl.when(pc != d % 2)
            def _():
                remote(d).start() if start else remote(d).wait_send()

        def emit(jj):
            shards = sorted({p[0] for p in tiles[jj]})
            for d in shards:
                if first_tile[d] == jj and d >= 2:
                    leave(d - 2, False)
                for dd, s0, w, src, c0 in tiles[jj]:
                    if dd == d:
                        ref = acc_ref if src == "main" else gate_ref
                        buf_ref[d % 2, :, s0:s0 + w] = ref[:, c0:c0 + w].astype(BF16)
                if last_tile[d] == jj:
                    leave(d, True)
            if jj == nj - 1:
                for d in (ND - 2, ND - 1):
                    leave(d, False)
                for q in range(ND // 2):
                    remote(2 * q).wait_recv()

        for jj in range(nj):
            @pl.when((j == jj) & (k == nk - 1))
            def _(jj=jj):
                emit(jj)

    any_spec = pl.BlockSpec(memory_space=pl.ANY)
    return pl.pallas_call(
        body, name="grad_w_in", grid=(nj, nk),
        in_specs=[pl.BlockSpec((tk, DM), lambda j, k: (k, 0)),
                  pl.BlockSpec((tk, tn), lambda j, k: (k, j)),
                  pl.BlockSpec((tk, LANES), lambda j, k: (k, 0))],
        out_specs=[any_spec, any_spec],
        out_shape=[S((ND // 2, DM, WD), BF16), S((ND // 2, DM, WD), BF16)],
        scratch_shapes=[pltpu.VMEM((DM, tn), F32), pltpu.VMEM((DM, LANES), F32), pltpu.VMEM((2, DM, WD), BF16),
                        pltpu.SemaphoreType.DMA((ND // 2,)), pltpu.SemaphoreType.DMA((ND // 2,)),
                        pltpu.SemaphoreType.DMA((ND // 2,))],
        compiler_params=_cp(ARB, ARB),
    )(xn, dmain, dba)


def _pair_sum_plain(a, b, name):
    K, R, C = a.shape
    tr = _tile(R, 256, 16)

    def body(a_ref, b_ref, o_ref):
        o_ref[...] = (a_ref[...].astype(F32) + b_ref[...].astype(F32)).astype(BF16)

    spec = lambda: pl.BlockSpec((1, tr, C), lambda q, i: (q, i, 0))
    return pl.pallas_call(body, name=name, grid=(K, R // tr), in_specs=[spec(), spec()], out_specs=spec(),
                          out_shape=S((K, R, C), BF16), compiler_params=_cp(ARB, ARB))(a, b)


def _dx(dmain, dba, w_main, w_ba, x, dh, norm_w, chip_sum, small):
    T, NM = dmain.shape
    DM = x.shape[1]
    tm = _tile(T, 512, 8)
    tk = _tile(NM, 1024, LANES)
    nk = NM // tk
    ni = T // tm
    half = chip_sum.shape[1] // 2
    assert half % 16 == 0
    last_step = ni * nk - 1
    relay_step = min(2 * nk, last_step)

    def body(dm_ref, dba_ref, w_ref, wba_ref, x_ref, dh_ref, nw_ref, small_ref, cs_ref,
             gx_ref, gnw_ref, gath_ref, recv_ref, stage_ref, acc_ref, csend, crecv, ssend, srecv, lsem):
        i = pl.program_id(0)
        k = pl.program_id(1)
        step = i * nk + k
        px, py, pc = _position()
        xn, yn = (1 - px, py, pc), (px, 1 - py, pc)
        upper, lower = pl.ds(0, half), pl.ds(half, half)

        def rcopy(kk, src, dst, to):
            return pltpu.make_async_remote_copy(src_ref=src, dst_ref=dst, send_sem=csend.at[kk], recv_sem=crecv.at[kk],
                                                device_id=to, device_id_type=MESH)

        diag_blk = cs_ref.at[2 * (1 - px) + (1 - py)]
        to_stage = [rcopy(2, diag_blk.at[upper], stage_ref.at[0], xn), rcopy(3, diag_blk.at[lower], stage_ref.at[1], yn)]
        direct = [rcopy(0, cs_ref.at[2 * (1 - px) + py], recv_ref.at[0], xn),
                  rcopy(1, cs_ref.at[2 * px + (1 - py)], recv_ref.at[1], yn)]
        onward = [rcopy(4, stage_ref.at[0], recv_ref.at[2].at[upper], yn),
                  rcopy(5, stage_ref.at[1], recv_ref.at[2].at[lower], xn)]
        me, small_cps = _broadcast_copies([small_ref], [gath_ref], ssend, srecv)
        small_cps = small_cps + [pltpu.make_async_copy(small_ref, gath_ref.at[me], lsem.at[0])]

        @pl.when(step == 0)
        def _():
            gnw_ref[...] = jnp.zeros_like(gnw_ref)
            for cp in to_stage + direct + small_cps:
                cp.start()

        @pl.when(step == relay_step)
        def _():
            for cp in to_stage:
                cp.wait_recv()
            for cp in onward:
                cp.start()

        @pl.when(k == 0)
        def _():
            acc_ref[...] = _mm_nt(dba_ref[...], wba_ref[...])

        acc_ref[...] += _mm_nt(dm_ref[...], w_ref[...])

        @pl.when(k == nk - 1)
        def _():
            xv = x_ref[...]
            rs = lax.rsqrt(jnp.mean(xv * xv, axis=-1, keepdims=True) + EPS)
            xh = xv * rs
            dxn = acc_ref[...]
            gnw_ref[0:1, :] += jnp.sum(dxn * xh, axis=0, keepdims=True)
            dxh = dxn * nw_ref[...]
            gx_ref[...] = dh_ref[...] + rs * (dxh - xh * jnp.mean(dxh * xh, axis=-1, keepdims=True))

        @pl.when(step == last_step)
        def _():
            for cp in to_stage:
                cp.wait_send()
            for cp in direct + onward + small_cps:
                cp.wait()

    any_spec = pl.BlockSpec(memory_space=pl.ANY)
    res = pl.pallas_call(
        body, name="dx", grid=(ni, nk),
        in_specs=[pl.BlockSpec((tm, tk), lambda i, k: (i, k)),
                  pl.BlockSpec((tm, LANES), lambda i, k: (i, 0)),
                  pl.BlockSpec((DM, tk), lambda i, k: (0, k)),
                  pl.BlockSpec((DM, LANES), lambda i, k: (0, 0)),
                  pl.BlockSpec((tm, DM), lambda i, k: (i, 0)),
                  pl.BlockSpec((tm, DM), lambda i, k: (i, 0)),
                  pl.BlockSpec((1, DM), lambda i, k: (0, 0)),
                  any_spec, any_spec],
        out_specs=[pl.BlockSpec((tm, DM), lambda i, k: (i, 0)),
                   pl.BlockSpec((8, DM), lambda i, k: (0, 0)),
                   any_spec, any_spec, any_spec],
        out_shape=[S((T, DM), F32), S((8, DM), F32), S((N_DEV,) + small.shape, F32),
                   S((3,) + chip_sum.shape[1:], chip_sum.dtype), S((2, half) + chip_sum.shape[2:], chip_sum.dtype)],
        scratch_shapes=[pltpu.VMEM((tm, DM), F32),
                        pltpu.SemaphoreType.DMA((6,)), pltpu.SemaphoreType.DMA((6,)),
                        pltpu.SemaphoreType.DMA((1, N_DEV - 1)), pltpu.SemaphoreType.DMA((1, N_DEV - 1)),
                        pltpu.SemaphoreType.DMA((1,))],
        compiler_params=_cp(ARB, ARB),
    )(dmain, dba, w_main, w_ba, x, dh, norm_w, small, chip_sum)
    return res[0], res[1], res[2], res[3]


def _sum_slots(gath):
    _, R, C = gath.shape
    tr = _tile(R, 512, 8)

    def body(g_ref, o_ref):
        tot = g_ref[0]
        for d in range(1, N_DEV):
            tot = tot + g_ref[d]
        o_ref[...] = tot

    return pl.pallas_call(
        body, name="sum_slots", grid=(R // tr,),
        in_specs=[pl.BlockSpec((N_DEV, tr, C), lambda i: (0, i, 0))],
        out_specs=pl.BlockSpec((tr, C), lambda i: (i, 0)),
        out_shape=S((R, C), F32), compiler_params=_cp(ARB),
    )(gath)


def _prep_a_bwd(dq, dk, dv, c, proj, conv_w, dmain, H, D):
    T = c.shape[0]
    AW = H * D
    C3 = 3 * AW
    tb = _tile(T, 256, 8)
    nblk = T // tb
    r8 = tb // 8
    scale = float(D) ** -0.5

    def body(dq_ref, dk_ref, dv_ref, c_ref, dqn_ref, dkn_ref, dvn_ref, cn_ref, x_ref, halo_ref, cw_ref, dmain_in_ref,
             dx_ref, gcw_ref, dc_ref):
        del dmain_in_ref
        i = pl.program_id(0)

        @pl.when(i == 0)
        def _():
            gcw_ref[...] = jnp.zeros_like(gcw_ref)

        def pointwise(rows, dq_r, dk_r, dv_r, c_r, keep):
            for h in range(H):
                for part, d_r, sc in ((0, dq_r, scale), (1, dk_r, 1.0)):
                    sl = slice(part * AW + h * D, part * AW + (h + 1) * D)
                    cv = c_r[:, sl]
                    raw = _silu(cv)
                    rs = lax.rsqrt(jnp.sum(raw * raw, axis=-1, keepdims=True) + EPS)
                    nrm = raw * rs
                    dn = d_r[:, h * D:(h + 1) * D] * sc
                    draw = rs * (dn - nrm * jnp.sum(dn * nrm, axis=-1, keepdims=True))
                    dc_ref[rows, sl] = draw * _dsilu(cv) * keep
            dc_ref[rows, 2 * AW:] = dv_r[...] * _dsilu(c_r[:, 2 * AW:]) * keep

        pointwise(slice(0, tb), dq_ref, dk_ref, dv_ref, c_ref, 1.0)
        pointwise(slice(tb, tb + 8), dqn_ref, dkn_ref, dvn_ref, cn_ref, (i < nblk - 1).astype(F32))

        cw = cw_ref[...]
        dcv = dc_ref[0:tb, :]
        dx = cw[3:4, :] * dcv
        for j in range(3):
            dx = dx + cw[j:j + 1, :] * dc_ref[3 - j:3 - j + tb, :]
        dx_ref[...] = dx.astype(BF16)
        halo = halo_ref[...] * (i > 0).astype(F32)
        xp = jnp.concatenate([halo, x_ref[...]], axis=0)
        for j in range(4):
            gcw_ref[j:j + 1, :] += jnp.sum(dcv * xp[5 + j:5 + j + tb], axis=0, keepdims=True)

    nxt = lambda i: (jnp.minimum((i + 1) * r8, T // 8 - 1), 0)
    return pl.pallas_call(
        body, name="prep_a_bwd", grid=(nblk,),
        in_specs=[pl.BlockSpec((tb, AW), lambda i: (i, 0)),
                  pl.BlockSpec((tb, AW), lambda i: (i, 0)),
                  pl.BlockSpec((tb, AW), lambda i: (i, 0)),
                  pl.BlockSpec((tb, C3), lambda i: (i, 0)),
                  pl.BlockSpec((8, AW), nxt), pl.BlockSpec((8, AW), nxt), pl.BlockSpec((8, AW), nxt),
                  pl.BlockSpec((8, C3), nxt),
                  pl.BlockSpec((tb, C3), lambda i: (i, 0)),
                  pl.BlockSpec((8, C3), lambda i: (jnp.maximum(i * r8 - 1, 0), 0)),
                  pl.BlockSpec((4, C3), lambda i: (0, 0)),
                  pl.BlockSpec(memory_space=pl.ANY)],
        out_specs=[pl.BlockSpec((tb, C3), lambda i: (i, 0)),
                   pl.BlockSpec((8, C3), lambda i: (0, 0))],
        out_shape=[S(dmain.shape, dmain.dtype), S((8, C3), F32)],
        scratch_shapes=[pltpu.VMEM((tb + 8, C3), F32)],
        input_output_aliases={11: 0},
        compiler_params=_cp(ARB),
    )(dq, dk, dv, c, dq, dk, dv, c, proj, proj, conv_w, dmain)


def _adam_math(w, g, m, v):
    m2 = ADAM_B1 * m + (1.0 - ADAM_B1) * g
    v2 = ADAM_B2 * v + (1.0 - ADAM_B2) * (g * g)
    m_hat = m2 / (1.0 - ADAM_B1 ** ADAM_STEP)
    v_hat = v2 / (1.0 - ADAM_B2 ** ADAM_STEP)
    delta = -ADAM_LR * (m_hat / (jnp.sqrt(v_hat) + ADAM_EPS) + ADAM_WD * w)
    return delta, m2, v2


def _pair_sum(blocks, recv, core, name):
    K, _, R, C = blocks.shape
    tr = _tile(R, 256, 16)

    def body(core_ref, a_ref, b_ref, o_ref):
        del core_ref
        o_ref[0] = (a_ref[0, 0].astype(F32) + b_ref[0].astype(F32)).astype(BF16)

    spec = lambda: pl.BlockSpec((1, tr, C), lambda k, i, core_ref: (k, i, 0))
    return pl.pallas_call(
        body, name=name,
        grid_spec=pltpu.PrefetchScalarGridSpec(
            num_scalar_prefetch=1, grid=(K, R // tr),
            in_specs=[pl.BlockSpec((1, 1, tr, C), lambda k, i, core_ref: (k, core_ref[0], i, 0)), spec()],
            out_specs=spec()),
        out_shape=S((K, R, C), BF16), compiler_params=_cp(ARB, ARB),
    )(core, blocks, recv)


def _sum_adam(chip_sums, recv, w, m, v, chip, name, transposed=False):
    R, C = chip_sums.shape[1:]
    tr = _tile(R, 256, 16)

    def body(chip_ref, own_ref, r_ref, w_ref, m_ref, v_ref, g_ref, d_ref, m2_ref, v2_ref):
        del chip_ref
        g = own_ref[0].astype(F32)
        for j in range(3):
            g = g + r_ref[j].astype(F32)
        if transposed:
            g = g.T
        g_ref[...] = g
        d_ref[...], m2_ref[...], v2_ref[...] = _adam_math(w_ref[...], g, m_ref[...], v_ref[...])

    if transposed:
        spec = lambda: pl.BlockSpec((C, tr), lambda i, chip_ref: (0, i))
        shape = (C, R)
    else:
        spec = lambda: pl.BlockSpec((tr, C), lambda i, chip_ref: (i, 0))
        shape = (R, C)
    assert w.shape == shape
    return pl.pallas_call(
        body, name=name,
        grid_spec=pltpu.PrefetchScalarGridSpec(
            num_scalar_prefetch=1, grid=(R // tr,),
            in_specs=[pl.BlockSpec((1, tr, C), lambda i, chip_ref: (chip_ref[0], i, 0)),
                      pl.BlockSpec((3, tr, C), lambda i, chip_ref: (0, i, 0)), spec(), spec(), spec()],
            out_specs=[spec(), spec(), spec(), spec()]),
        out_shape=[S(shape, F32)] * 4, compiler_params=_cp(ARB),
    )(chip, chip_sums, recv, w, m, v)


def _adam_small(w, g, m, v):
    R, C = w.shape
    tr = _tile(R, 512, 8)

    def body(w_ref, g_ref, m_ref, v_ref, d_ref, m2_ref, v2_ref):
        d_ref[...], m2_ref[...], v2_ref[...] = _adam_math(w_ref[...], g_ref[...], m_ref[...], v_ref[...])

    spec = lambda: pl.BlockSpec((tr, C), lambda i: (i, 0))
    return pl.pallas_call(
        body, name="adam_small", grid=(R // tr,), in_specs=[spec()] * 4, out_specs=[spec()] * 3,
        out_shape=[S((R, C), F32)] * 3, compiler_params=_cp(ARB),
    )(w, g, m, v)


def _position():
    return lax.axis_index("x"), lax.axis_index("y"), lax.axis_index("c")


def _all_gather_weights(arr):
    R = arr.shape[0]
    half = R // 2
    assert half % 16 == 0

    def body(in_ref, out_ref, send_sems, recv_sems, local_sem):
        x, y, c = _position()
        me, sibling = (x, y, c), (x, y, 1 - c)
        xn, yn, diag = (1 - x, y), (x, 1 - y), (1 - x, 1 - y)
        upper, lower = pl.ds(0, half), pl.ds(half, half)

        def slot(p, rows=None):
            ref = out_ref.at[4 * p[0] + 2 * p[1] + p[2]]
            return ref if rows is None else ref.at[rows]

        def copy(kk, block, to, rows=None, src=None):
            return pltpu.make_async_remote_copy(
                src_ref=slot(block, rows) if src is None else src, dst_ref=slot(block, rows),
                send_sem=send_sems.at[kk], recv_sem=recv_sems.at[kk], device_id=to, device_id_type=MESH)

        mine = pltpu.make_async_copy(in_ref, slot(me), local_sem)
        mine.start()
        sent = [copy(0, me, sibling, src=in_ref), copy(1, me, (*xn, c), src=in_ref), copy(2, me, (*yn, c), src=in_ref)]
        for cp in sent:
            cp.start()

        def then(cps):
            for cp in cps:
                cp.start()
            sent.extend(cps)

        copy(1, (*xn, c), me).wait_recv()
        then([copy(5, (*xn, c), (*yn, c), rows=upper), copy(3, (*xn, c), sibling)])
        copy(2, (*yn, c), me).wait_recv()
        then([copy(6, (*yn, c), (*xn, c), rows=lower), copy(4, (*yn, c), sibling)])
        copy(5, (*diag, c), me, rows=upper).wait_recv()
        then([copy(7, (*diag, c), sibling, rows=upper)])
        copy(6, (*diag, c), me, rows=lower).wait_recv()
        then([copy(8, (*diag, c), sibling, rows=lower)])
        copy(0, sibling, me).wait_recv()
        copy(3, (*xn, 1 - c), me).wait_recv()
        copy(4, (*yn, 1 - c), me).wait_recv()
        copy(7, (*diag, 1 - c), me, rows=upper).wait_recv()
        copy(8, (*diag, 1 - c), me, rows=lower).wait_recv()
        for cp in sent:
            cp.wait_send()
        mine.wait()

    any_spec = pl.BlockSpec(memory_space=pl.ANY)
    return pl.pallas_call(
        body, name="all_gather_weights", in_specs=[any_spec], out_specs=any_spec,
        out_shape=S((N_DEV,) + arr.shape, arr.dtype),
        scratch_shapes=[pltpu.SemaphoreType.DMA((9,)), pltpu.SemaphoreType.DMA((9,)), pltpu.SemaphoreType.DMA],
    )(arr)


def _sibling_copies(ins, outs, send_sems, recv_sems):
    x, y, c = _position()
    return [pltpu.make_async_remote_copy(src_ref=ins[a].at[k, 1 - c], dst_ref=outs[a].at[k],
                                         send_sem=send_sems.at[a, k], recv_sem=recv_sems.at[a, k],
                                         device_id=(x, y, 1 - c), device_id_type=MESH)
            for a in range(len(ins)) for k in range(ins[a].shape[0])]


def _sibling_sems(arrs):
    shape = (max(len(arrs), 1), arrs[0].shape[0] if arrs else 1)
    return [pltpu.SemaphoreType.DMA(shape), pltpu.SemaphoreType.DMA(shape)]


def _chip_exchange_copies(ins, outs, send_sems, recv_sems):
    x, y, c = _position()
    chips = [(1 - x, y), (x, 1 - y), (1 - x, 1 - y)]
    return [pltpu.make_async_remote_copy(
        src_ref=ins[a].at[2 * qx + qy], dst_ref=outs[a].at[j], send_sem=send_sems.at[a, j],
        recv_sem=recv_sems.at[a, j], device_id=(qx, qy, c), device_id_type=MESH)
        for a in range(len(ins)) for j, (qx, qy) in enumerate(chips)]


def _broadcast_copies(srcs, dsts, send_sems, recv_sems):
    x, y, c = _position()
    me = 4 * x + 2 * y + c
    cps = []
    for a in range(len(srcs)):
        for k in range(1, N_DEV):
            peer = (1 - x if k & 4 else x, 1 - y if k & 2 else y, 1 - c if k & 1 else c)
            cps.append(pltpu.make_async_remote_copy(
                src_ref=srcs[a], dst_ref=dsts[a].at[me], send_sem=send_sems.at[a, k - 1],
                recv_sem=recv_sems.at[a, k - 1], device_id=peer, device_id_type=MESH))
    return me, cps


def _all_reduce_small(part):
    R, C = part.shape

    def body(p_ref, out_ref, gath_ref, send_sems, recv_sems):
        me, cps = _broadcast_copies([p_ref], [gath_ref], send_sems, recv_sems)
        gath_ref[me] = p_ref[...]
        for cp in cps:
            cp.start()
        for cp in cps:
            cp.wait()
        acc = gath_ref[0]
        for d in range(1, N_DEV):
            acc = acc + gath_ref[d]
        out_ref[...] = acc

    vm = pl.BlockSpec(memory_space=pltpu.VMEM)
    return pl.pallas_call(
        body, name="all_reduce_small", in_specs=[vm], out_specs=vm, out_shape=S((R, C), F32),
        scratch_shapes=[pltpu.VMEM((N_DEV, R, C), F32), pltpu.SemaphoreType.DMA((1, N_DEV - 1)),
                        pltpu.SemaphoreType.DMA((1, N_DEV - 1))],
    )(part)


def _pack(parts):
    rows = []
    for p in parts:
        f = p.reshape(-1).astype(F32)
        pad = (-f.shape[0]) % (8 * LANES)
        rows.append(jnp.pad(f, (0, pad)).reshape(-1, LANES))
    return jnp.concatenate(rows, axis=0)


def _unpack(buf, shapes):
    out, r = [], 0
    for shp in shapes:
        n = 1
        for s in shp:
            n *= s
        nr = -(-n // (8 * LANES)) * 8
        out.append(buf[r:r + nr].reshape(-1)[:n].reshape(shp))
        r += nr
    return out


def kernel(x, norm_w, w_in, conv_w, a_log, dt_bias, head_norm_w, sgu_ln_w, sgu_ln_b, w_spatial, b_spatial, w_out, final_norm_w, loss_target, m_norm_w, m_w_in, m_conv_w, m_a_log, m_dt_bias, m_head_norm_w, m_sgu_ln_w, m_sgu_ln_b, m_w_spatial, m_b_spatial, m_w_out, m_final_norm_w, v_norm_w, v_w_in, v_conv_w, v_a_log, v_dt_bias, v_head_norm_w, v_sgu_ln_w, v_sgu_ln_b, v_w_spatial, v_b_spatial, v_w_out, v_final_norm_w):
    T, DM = x.shape[1], x.shape[2]
    H, D = a_log.shape[1], head_norm_w.shape[1]
    G, P = w_spatial.shape[1], w_spatial.shape[2]
    AW, BW = H * D, G * P
    MIX = AW + BW
    WD = w_in.shape[2]
    IN = N_DEV * WD
    RO = w_out.shape[1]
    CW = conv_w.shape[2]
    sizes = (3 * AW, AW, H, H, BW, BW, BW)
    assert sum(sizes) == IN and 2 * H <= LANES and 3 * H <= 32 and N_DEV * RO == MIX and N_DEV * CW == 3 * AW
    offs = [0]
    for s in sizes:
        offs.append(offs[-1] + s)
    px, py, pc = _position()
    dev = 4 * px + 2 * py + pc
    chip = 2 * px + py

    x2, tgt = x[0], loss_target[0]

    g_win = _all_gather_weights(_cast_bf16_t(w_in[0].T, "cast_w_in"))
    w_main, w_ba = _relayout_w(g_win, offs[2], offs[4])
    alog_row = jnp.pad(a_log, ((0, 0), (H, LANES - 2 * H)))
    dtb_row = jnp.pad(dt_bias, ((0, 0), (H, LANES - 2 * H)))
    bs_t = b_spatial[0].T

    xn = _rms_xn(x2, norm_w)
    proj, ba, (g_wout, g_conv) = _in_proj(xn, w_main, w_ba, [_cast_bf16(w_out[0], "cast_w_out"), conv_w[0]])
    w_out_full = g_wout.reshape(MIX, DM)
    conv_full = g_conv.transpose(1, 0, 2).reshape(4, 3 * AW)
    q, k, v, c, gcol, grow = _prep_a_fwd(proj, ba, conv_full, alog_row, dtb_row, H, D)
    o, vnew, ssave, asave = _delta_fwd(q, k, v, gcol, grow, H, D)
    ocat = _mix_fwd(o, proj, head_norm_w, sgu_ln_w, sgu_ln_b, w_spatial[0], bs_t, H, D, G, P)
    dh, dh_bf, d_ocat, loss_acc, g_fnw = _out_proj_loss(ocat, w_out_full, x2, tgt, final_norm_w.reshape(1, DM))
    loss = lax.psum(loss_acc[0, 0], AXES)

    core_idx = jnp.reshape(pc, (1,)).astype(jnp.int32)
    chip_idx = jnp.reshape(chip, (1,)).astype(jnp.int32)
    g_wout_blocks = _grad_w(ocat, dh_bf, "grad_w_out").reshape(4, 2, RO, DM)
    (d_o, dmain, g_hnw, g_ln, g_wsp, g_bs_t), (sib_wout,) = _mix_bwd(
        d_ocat, o, proj, head_norm_w, sgu_ln_w, sgu_ln_b, w_spatial[0], bs_t, H, D, G, P, [g_wout_blocks])
    chip_wout = _pair_sum(g_wout_blocks, sib_wout, core_idx, "pair_sum_w_out")
    (dq, dk, dv, dgate, dpar), (recv_wout,) = _delta_bwd(
        q, k, v, gcol, grow, ba, vnew, ssave, asave, d_o, alog_row, dtb_row, H, D, [chip_wout])
    dmain, g_conv_part = _prep_a_bwd(dq, dk, dv, c, proj, conv_full, dmain, H, D)
    dba = dgate.astype(BF16)
    keep_win, sib_win = _grad_w_in(xn, dmain, dba, WD, offs[2], offs[4])
    chip_win = _pair_sum_plain(keep_win, sib_win, "pair_sum_w_in")
    small_shapes = [a_log.shape, dt_bias.shape, head_norm_w.shape, sgu_ln_w.shape, sgu_ln_b.shape,
                    w_spatial.shape, b_spatial.shape, final_norm_w.shape]
    parts = [dpar[0, H:2 * H], dpar[1, H:2 * H], g_hnw[0], g_ln[0], g_ln[1], g_wsp, g_bs_t[:, :G].T, g_fnw[0],
             g_conv_part[:4]]
    grad_x, g_nw, small_gath, recv_win = _dx(dmain, dba, w_main, w_ba, x2, dh, norm_w, chip_win, _pack(parts))
    red = _sum_slots(small_gath)
    grad_w_in, delta_w_in, new_m_w_in, new_v_w_in = _sum_adam(
        chip_win, recv_win, w_in[0].T, m_w_in[0].T, v_w_in[0].T, chip_idx, "sum_adam_w_in", transposed=True)
    grad_w_out, delta_w_out, new_m_w_out, new_v_w_out = _sum_adam(
        chip_wout, recv_wout, w_out[0], m_w_out[0], v_w_out[0], chip_idx, "sum_adam_w_out")
    red_nw = _all_reduce_small(_pack([g_nw[0]]))
    grads_small = _unpack(red_nw, [norm_w.shape]) + _unpack(red, small_shapes + [(4, 3 * AW)])
    g_conv_full = grads_small.pop()
    grad_conv = lax.dynamic_slice_in_dim(g_conv_full, dev * CW, CW, axis=1)[None]
    small_w = [norm_w, a_log, dt_bias, head_norm_w, sgu_ln_w, sgu_ln_b, w_spatial, b_spatial, final_norm_w, conv_w]
    small_m = [m_norm_w, m_a_log, m_dt_bias, m_head_norm_w, m_sgu_ln_w, m_sgu_ln_b, m_w_spatial, m_b_spatial,
               m_final_norm_w, m_conv_w]
    small_v = [v_norm_w, v_a_log, v_dt_bias, v_head_norm_w, v_sgu_ln_w, v_sgu_ln_b, v_w_spatial, v_b_spatial,
               v_final_norm_w, v_conv_w]
    small_g = grads_small + [grad_conv]
    shapes10 = [w.shape for w in small_w]
    d_p, m_p, v_p = _adam_small(_pack(small_w), _pack(small_g), _pack(small_m), _pack(small_v))
    d_s, m_s, v_s = _unpack(d_p, shapes10), _unpack(m_p, shapes10), _unpack(v_p, shapes10)

    def order(small, win, wout):
        return [small[0], win.T[None], small[9], small[1], small[2], small[3], small[4], small[5], small[6], small[7],
                wout[None], small[8]]

    grads = order(small_g, grad_w_in, grad_w_out)
    deltas = order(d_s, delta_w_in, delta_w_out)
    new_m = order(m_s, new_m_w_in, new_m_w_out)
    new_v = order(v_s, new_v_w_in, new_v_w_out)
    return (loss, grad_x[None], *grads, *deltas, *new_m, *new_v)
```

```python
import functools

import jax
import jax.numpy as jnp
from jax import lax
from jax.experimental import pallas as pl
from jax.experimental.pallas import tpu as pltpu

F32 = jnp.float32
BF16 = jnp.bfloat16
MXU = jnp.bfloat16
HI = lax.Precision.HIGHEST
EPS = 1e-6
CHUNK_A = 64
LANES = 128
MESH = pl.DeviceIdType.MESH
AXES = ("x", "y", "c")
N_DEV = 8

ADAM_LR = 0.001
ADAM_B1 = 0.9
ADAM_B2 = 0.999
ADAM_EPS = 1e-08
ADAM_WD = 0.01
ADAM_STEP = 10

S = jax.ShapeDtypeStruct
ARB = "arbitrary"


def _cp(*sem, vmem_mib=56):
    return pltpu.CompilerParams(dimension_semantics=tuple(sem), vmem_limit_bytes=vmem_mib * 1024 * 1024)


def _tile(n, cap, mult):
    best = None
    t = mult
    while t <= min(n, cap):
        if n % t == 0:
            best = t
        t += mult
    return best if best is not None else n


def _mm(a, b):
    return jnp.dot(a.astype(MXU), b.astype(MXU), preferred_element_type=F32)


def _mm_nt(a, b):
    return lax.dot_general(a.astype(MXU), b.astype(MXU), (((1,), (1,)), ((), ())), preferred_element_type=F32)


def _mm_tn(a, b):
    return lax.dot_general(a.astype(MXU), b.astype(MXU), (((0,), (0,)), ((), ())), preferred_element_type=F32)


def _mmh(a, b):
    return jnp.dot(a, b, precision=HI, preferred_element_type=F32)


def _mmh_tn(a, b):
    return lax.dot_general(a, b, (((0,), (0,)), ((), ())), precision=HI, preferred_element_type=F32)


def _sigmoid(x):
    return 1.0 / (1.0 + jnp.exp(-x))


def _silu(x):
    return x * _sigmoid(x)


def _dsilu(x):
    s = _sigmoid(x)
    return s * (1.0 + x * (1.0 - s))


def _softplus(x):
    return jnp.maximum(x, 0.0) + jnp.log(1.0 + jnp.exp(-jnp.abs(x)))


def _pieces(wd, gate_lo, gate_hi, total):
    out = []
    for d in range(N_DEV):
        lo, hi = d * wd, (d + 1) * wd
        for dest, a, b, shift in (("main", 0, gate_lo, 0), ("gate", gate_lo, gate_hi, -gate_lo),
                                  ("main", gate_hi, total, gate_lo - gate_hi)):
            s0, s1 = max(lo, a), min(hi, b)
            if s0 < s1:
                out.append((d, s0 - lo, s1 - lo, dest, s0 + shift))
    return out


def _cast_bf16(a, name):
    R, C = a.shape
    tr = _tile(R, 256, 16)

    def body(a_ref, o_ref):
        o_ref[...] = a_ref[...].astype(BF16)

    spec = pl.BlockSpec((tr, C), lambda i: (i, 0))
    return pl.pallas_call(body, name=name, grid=(R // tr,), in_specs=[spec], out_specs=spec,
                          out_shape=S((R, C), BF16), compiler_params=_cp(ARB))(a)


def _cast_bf16_t(a_t, name):
    C, R = a_t.shape
    tr = _tile(R, 256, LANES)

    def body(a_ref, o_ref):
        o_ref[...] = a_ref[...].T.astype(BF16)

    return pl.pallas_call(body, name=name, grid=(R // tr,), in_specs=[pl.BlockSpec((C, tr), lambda i: (0, i))],
                          out_specs=pl.BlockSpec((tr, C), lambda i: (i, 0)),
                          out_shape=S((R, C), BF16), compiler_params=_cp(ARB))(a_t)


def _relayout_w(g_win, gate_lo, gate_hi):
    _, DM, WD = g_win.shape
    total = N_DEV * WD
    NM = total - (gate_hi - gate_lo)
    tr = _tile(DM, 256, 16)
    plan = _pieces(WD, gate_lo, gate_hi, total)

    def body(g_ref, main_ref, gate_ref):
        gate_ref[...] = jnp.zeros_like(gate_ref)
        for d, s0, s1, dest, c0 in plan:
            dst = main_ref if dest == "main" else gate_ref
            dst[:, c0:c0 + (s1 - s0)] = g_ref[d, :, s0:s1]

    return pl.pallas_call(
        body, name="relayout_w", grid=(DM // tr,),
        in_specs=[pl.BlockSpec((N_DEV, tr, WD), lambda i: (0, i, 0))],
        out_specs=[pl.BlockSpec((tr, NM), lambda i: (i, 0)), pl.BlockSpec((tr, LANES), lambda i: (i, 0))],
        out_shape=[S((DM, NM), g_win.dtype), S((DM, LANES), g_win.dtype)],
        compiler_params=_cp(ARB),
    )(g_win)


def _rms_xn(x, norm_w):
    T, DM = x.shape
    tm = _tile(T, 512, 16)

    def body(x_ref, nw_ref, o_ref):
        xv = x_ref[...]
        r = lax.rsqrt(jnp.mean(xv * xv, axis=-1, keepdims=True) + EPS)
        o_ref[...] = (xv * r * nw_ref[...]).astype(BF16)

    return pl.pallas_call(
        body, name="rms_xn", grid=(T // tm,),
        in_specs=[pl.BlockSpec((tm, DM), lambda i: (i, 0)), pl.BlockSpec((1, DM), lambda i: (0, 0))],
        out_specs=pl.BlockSpec((tm, DM), lambda i: (i, 0)),
        out_shape=S((T, DM), BF16), compiler_params=_cp(ARB),
    )(x, norm_w)


def _in_proj(xn, w_main, w_ba, shards):
    T, DM = xn.shape
    NM = w_main.shape[1]
    tm = _tile(T, 2048, 16)
    tn = _tile(NM, 1024, LANES)
    ni, nj = T // tm, NM // tn
    ns = len(shards)

    def body(xn_ref, w_ref, wba_ref, *rest):
        srcs = rest[:ns]
        proj_ref, ba_ref = rest[ns:ns + 2]
        gath = rest[ns + 2:2 * ns + 2]
        send_sems, recv_sems, local_sems = rest[2 * ns + 2:]
        i = pl.program_id(0)
        j = pl.program_id(1)
        me, cps = _broadcast_copies(srcs, gath, send_sems, recv_sems)
        cps = cps + [pltpu.make_async_copy(srcs[a], gath[a].at[me], local_sems.at[a]) for a in range(ns)]

        @pl.when((i == 0) & (j == 0))
        def _():
            for cp in cps:
                cp.start()

        @pl.when(j == 0)
        def _():
            ba_ref[...] = jnp.dot(xn_ref[...].astype(MXU), wba_ref[...].astype(MXU), preferred_element_type=F32)

        proj_ref[...] = jnp.dot(xn_ref[...].astype(MXU), w_ref[...].astype(MXU), preferred_element_type=F32)

        @pl.when((i == ni - 1) & (j == nj - 1))
        def _():
            for cp in cps:
                cp.wait()

    any_spec = pl.BlockSpec(memory_space=pl.ANY)
    res = pl.pallas_call(
        body, name="in_proj", grid=(ni, nj),
        in_specs=[pl.BlockSpec((tm, DM), lambda i, j: (i, 0)),
                  pl.BlockSpec((DM, tn), lambda i, j: (0, j)),
                  pl.BlockSpec((DM, LANES), lambda i, j: (0, 0))] + [any_spec] * ns,
        out_specs=[pl.BlockSpec((tm, tn), lambda i, j: (i, j)),
                   pl.BlockSpec((tm, LANES), lambda i, j: (i, 0))] + [any_spec] * ns,
        out_shape=[S((T, NM), F32), S((T, LANES), F32)] + [S((N_DEV,) + a.shape, a.dtype) for a in shards],
        scratch_shapes=[pltpu.SemaphoreType.DMA((ns, N_DEV - 1)), pltpu.SemaphoreType.DMA((ns, N_DEV - 1)),
                        pltpu.SemaphoreType.DMA((ns,))],
        compiler_params=_cp(ARB, ARB, vmem_mib=58),
    )(xn, w_main, w_ba, *shards)
    return res[0], res[1], res[2:]


def _prep_a_fwd(proj, ba, conv_w, alog_row, dtb_row, H, D):
    T = proj.shape[0]
    AW = H * D
    C3 = 3 * AW
    tb = _tile(T, 256, CHUNK_A)
    nch = tb // CHUNK_A
    nblk = T // tb
    scale = float(D) ** -0.5

    def body(x_ref, halo_ref, ba_ref, cw_ref, al_ref, dt_ref, q_ref, k_ref, v_ref, c_ref, gcol_ref, grow_ref):
        i = pl.program_id(0)
        xv = x_ref[...]
        halo = halo_ref[...] * (i > 0).astype(F32)
        xp = jnp.concatenate([halo, xv], axis=0)
        cw = cw_ref[...]
        c = cw[0:1, :] * xp[5:5 + tb]
        for j in range(1, 4):
            c = c + cw[j:j + 1, :] * xp[5 + j:5 + j + tb]
        c_ref[...] = c
        a = _silu(c)
        for h in range(H):
            qh = a[:, h * D:(h + 1) * D]
            kh = a[:, AW + h * D:AW + (h + 1) * D]
            qr = lax.rsqrt(jnp.sum(qh * qh, axis=-1, keepdims=True) + EPS)
            kr = lax.rsqrt(jnp.sum(kh * kh, axis=-1, keepdims=True) + EPS)
            q_ref[:, h * D:(h + 1) * D] = qh * (qr * scale)
            k_ref[:, h * D:(h + 1) * D] = kh * kr
        v_ref[...] = a[:, 2 * AW:]

        bav = ba_ref[...]
        lane = lax.broadcasted_iota(jnp.int32, (tb, LANES), 1)
        beta = _sigmoid(bav)
        g = -jnp.exp(al_ref[...]) * _softplus(bav + dt_ref[...])
        gates = jnp.where(lane < H, beta, jnp.where(lane < 2 * H, g, 0.0))
        ri = lax.broadcasted_iota(jnp.int32, (CHUNK_A, CHUNK_A), 0)
        ci = lax.broadcasted_iota(jnp.int32, (CHUNK_A, CHUNK_A), 1)
        tri = (ri >= ci).astype(F32)
        lane_c = lax.broadcasted_iota(jnp.int32, (CHUNK_A, LANES), 1)
        for cc in range(nch):
            gch = gates[cc * CHUNK_A:(cc + 1) * CHUNK_A]
            gc = pltpu.roll(_mmh(tri, gch), H, 1)
            full = jnp.where(lane_c < 2 * H, gch, jnp.where(lane_c < 3 * H, gc, 0.0))
            gcol_ref[cc * CHUNK_A:(cc + 1) * CHUNK_A, :] = full
            grow_ref[cc] = full.T[0:32, :]

    return pl.pallas_call(
        body, name="prep_a_fwd", grid=(nblk,),
        in_specs=[pl.BlockSpec((tb, C3), lambda i: (i, 0)),
                  pl.BlockSpec((8, C3), lambda i: (jnp.maximum(i * (tb // 8) - 1, 0), 0)),
                  pl.BlockSpec((tb, LANES), lambda i: (i, 0)),
                  pl.BlockSpec((4, C3), lambda i: (0, 0)),
                  pl.BlockSpec((1, LANES), lambda i: (0, 0)),
                  pl.BlockSpec((1, LANES), lambda i: (0, 0))],
        out_specs=[pl.BlockSpec((tb, AW), lambda i: (i, 0)),
                   pl.BlockSpec((tb, AW), lambda i: (i, 0)),
                   pl.BlockSpec((tb, AW), lambda i: (i, 0)),
                   pl.BlockSpec((tb, C3), lambda i: (i, 0)),
                   pl.BlockSpec((tb, LANES), lambda i: (i, 0)),
                   pl.BlockSpec((nch, 32, CHUNK_A), lambda i: (i, 0, 0))],
        out_shape=[S((T, AW), F32), S((T, AW), F32), S((T, AW), F32), S((T, C3), F32),
                   S((T, LANES), F32), S((T // CHUNK_A, 32, CHUNK_A), F32)],
        compiler_params=_cp(ARB),
    )(proj, proj, ba, conv_w, alog_row, dtb_row)


_NN = (((1,), (0,)), ((), ()))
_TN = (((0,), (0,)), ((), ()))


def _split(a):
    hi = a.astype(BF16)
    return hi, (a - hi.astype(F32)).astype(BF16)


def _mm3(a, b, dims=_NN):
    ah, al = a if isinstance(a, tuple) else _split(a)
    bh, bl = b if isinstance(b, tuple) else _split(b)
    dg = lambda p, r: lax.dot_general(p, r, dims, preferred_element_type=F32)
    return dg(ah, bh) + (dg(ah, bl) + dg(al, bh))


def _interleave(gens):
    gens = list(gens)
    while gens:
        alive = []
        for g in gens:
            try:
                next(g)
                alive.append(g)
            except StopIteration:
                pass
        gens = alive


def _chunk_terms(q, k, v, gcolv, growv, h, H):
    C = CHUNK_A
    beta_c = gcolv[:, h:h + 1]
    g_c = gcolv[:, H + h:H + h + 1]
    gc_c = gcolv[:, 2 * H + h:2 * H + h + 1]
    gc_r = growv[2 * H + h:2 * H + h + 1, :]
    ri = lax.broadcasted_iota(jnp.int32, (C, C), 0)
    ci = lax.broadcasted_iota(jnp.int32, (C, C), 1)
    incl = ri >= ci
    strict = ri > ci
    kb = k * beta_c
    vb = v * beta_c
    p_raw = _mm_nt(kb, k)
    qk_raw = _mm_nt(q, k)
    gam = jnp.where(incl, jnp.exp(jnp.where(incl, gc_c - gc_r, 0.0)), 0.0)
    e_c = jnp.exp(gc_c)
    gl = gc_r[:, C - 1:C]
    edec = jnp.exp(gl - gc_c)
    yield
    lmat = jnp.where(strict, p_raw * gam, 0.0)
    attn = jnp.where(incl, qk_raw * gam, 0.0)
    return dict(beta_c=beta_c, g_c=g_c, gc_c=gc_c, gc_r=gc_r, incl=incl, strict=strict, gam=gam, e_c=e_c,
                kb=kb, vb=vb, lmat=lmat, attn=attn, gl=gl, edec=edec, ri=ri, ci=ci)


def _inv_unit_lower(lmat):
    C = lmat.shape[0]
    ri = lax.broadcasted_iota(jnp.int32, (C, C), 0)
    ci = lax.broadcasted_iota(jnp.int32, (C, C), 1)
    eye = (ri == ci).astype(F32)
    x = -lmat
    a = eye + x
    n = 1
    while 2 * n < C:
        xs = _split(x)
        x = _mm3(xs, xs)
        yield
        a = a + _mm3(a, x)
        n *= 2
    yield
    return a


def _delta_fwd(q, k, v, gcol, grow, H, D):
    T = q.shape[0]
    C = CHUNK_A
    N = T // C
    AW = H * D
    CPS = 2 if N % 2 == 0 else 1

    def body(q_ref, k_ref, v_ref, gcol_ref, grow_ref, o_ref, vn_ref, ssave_ref, asave_ref, s_ref):
        @pl.when(pl.program_id(0) == 0)
        def _():
            s_ref[...] = jnp.zeros_like(s_ref)

        state = {(0, h): s_ref[h] for h in range(H)}

        def head(cc, h):
            rows = slice(cc * C, (cc + 1) * C)
            sl = slice(h * D, (h + 1) * D)
            qv, kv, vv = q_ref[rows, sl], k_ref[rows, sl], v_ref[rows, sl]
            t = yield from _chunk_terms(qv, kv, vv, gcol_ref[rows, :], grow_ref[cc], h, H)
            a = yield from _inv_unit_lower(t["lmat"])
            asave_ref[cc, h] = a
            while (cc, h) not in state:
                yield
            st = state[(cc, h)]
            ssave_ref[cc, h] = st
            ks = _mm(t["kb"] * t["e_c"], st)
            o_inter = _mm(qv * t["e_c"], st)
            yield
            v_new = _mm3(a, t["vb"] - ks)
            yield
            vn_ref[rows, sl] = v_new
            o_intra = _mm(t["attn"], v_new)
            s_upd = _mm_tn(kv * t["edec"], v_new)
            yield
            o_ref[rows, sl] = o_inter + o_intra
            state[(cc + 1, h)] = st * jnp.exp(t["gl"]) + s_upd

        _interleave(head(cc, h) for cc in range(CPS) for h in range(H))
        for h in range(H):
            s_ref[h] = state[(CPS, h)]

    blk = lambda: pl.BlockSpec((CPS * C, AW), lambda n: (n, 0))
    return pl.pallas_call(
        body, name="delta_fwd", grid=(N // CPS,),
        in_specs=[blk(), blk(), blk(),
                  pl.BlockSpec((CPS * C, LANES), lambda n: (n, 0)),
                  pl.BlockSpec((CPS, 32, C), lambda n: (n, 0, 0))],
        out_specs=[blk(), blk(),
                   pl.BlockSpec((CPS, H, D, D), lambda n: (n, 0, 0, 0)),
                   pl.BlockSpec((CPS, H, C, C), lambda n: (n, 0, 0, 0))],
        out_shape=[S((T, AW), F32), S((T, AW), F32), S((N, H, D, D), F32), S((N, H, C, C), F32)],
        scratch_shapes=[pltpu.VMEM((H, D, D), F32)],
        compiler_params=_cp(ARB),
    )(q, k, v, gcol, grow)


def _delta_bwd(q, k, v, gcol, grow, ba, vnew, ssave, asave, d_o, a_log, dt_bias, H, D, carry):
    T = q.shape[0]
    C = CHUNK_A
    N = T // C
    AW = H * D
    nc = len(carry)
    CPS = 2 if N % 2 == 0 else 1
    NS = N // CPS

    def body(al_ref, dt_ref, q_ref, k_ref, v_ref, gcol_ref, grow_ref, ba_ref, vn_ref, ss_ref, as_ref, do_ref, *rest):
        cins = rest[:nc]
        dq_ref, dk_ref, dv_ref, dgate_ref, dpar_ref = rest[nc:nc + 5]
        couts = rest[nc + 5:2 * nc + 5]
        ds_ref, csend, crecv = rest[2 * nc + 5:]
        ccps = _chip_exchange_copies(cins, couts, csend, crecv)

        @pl.when(pl.program_id(0) == 0)
        def _():
            ds_ref[...] = jnp.zeros_like(ds_ref)
            dpar_ref[...] = jnp.zeros_like(dpar_ref)
            for cp in ccps:
                cp.start()

        lane = lax.broadcasted_iota(jnp.int32, (C, LANES), 1)
        rowi = lax.broadcasted_iota(jnp.int32, (C, 1), 0)
        acc = {cc: jnp.zeros((C, LANES), F32) for cc in range(CPS)}
        state = {(0, h): ds_ref[h] for h in range(H)}

        def head(oi, h):
            cc = CPS - 1 - oi
            rows = slice(cc * C, (cc + 1) * C)
            sl = slice(h * D, (h + 1) * D)
            st = ss_ref[cc, h]
            a = as_ref[cc, h]
            qv, kv, vv, dov, v_new = q_ref[rows, sl], k_ref[rows, sl], v_ref[rows, sl], do_ref[rows, sl], vn_ref[rows, sl]
            t = yield from _chunk_terms(qv, kv, vv, gcol_ref[rows, :], grow_ref[cc], h, H)
            beta_c, e_c, gam, kb = t["beta_c"], t["e_c"], t["gam"], t["kb"]
            incl, strict, attn, lmat, edec = t["incl"], t["strict"], t["attn"], t["lmat"], t["edec"]
            kdec = kv * edec
            egl = jnp.exp(t["gl"])
            qe = qv * e_c
            ekb = kb * e_c

            t1 = _mm_nt(dov, st)
            ds_o = _mm_tn(qe, dov)
            dattn_raw = _mm_nt(dov, v_new)
            dv_new_o = _mm_tn(attn, dov)
            yield
            while (oi, h) not in state:
                yield
            ds_next = state[(oi, h)]
            dkdec = _mm_nt(v_new, ds_next)
            dv_new_s = _mm(kdec, ds_next)
            yield
            dgl = egl * jnp.sum(jnp.sum(st * ds_next, axis=1, keepdims=True), axis=0, keepdims=True)
            dk = edec * dkdec
            r = jnp.sum(dkdec * kdec, axis=1, keepdims=True)
            dgc = -r
            dgl = dgl + jnp.sum(r, axis=0, keepdims=True)
            dq = e_c * t1
            dgc = dgc + jnp.sum(t1 * qe, axis=1, keepdims=True)
            dattn = jnp.where(incl, dattn_raw, 0.0)
            dv_new = dv_new_s + dv_new_o
            dqm = dattn * gam
            z = dattn * attn
            dvb = _mm3(a, dv_new, _TN)
            dq_a = _mm(dqm, kv)
            dk_a = _mm_tn(dqm, qv)
            yield
            dq_ref[rows, sl] = dq + dq_a
            dv_ref[rows, sl] = beta_c * dvb
            ds_kb = _mm_tn(ekb, dvb)
            dekb_neg = _mm_nt(dvb, st)
            dl_neg = _mm_nt(dvb, v_new)
            yield
            state[(oi + 1, h)] = egl * ds_next + ds_o - ds_kb
            dekb = -dekb_neg
            dl = jnp.where(strict, -dl_neg, 0.0)
            dp = dl * gam
            z = z + dl * lmat
            dkb_p = _mm(dp, kv)
            dk_p = _mm_tn(dp, kb)
            dgc = dgc + jnp.sum(dekb * ekb, axis=1, keepdims=True)
            dgc = dgc + jnp.sum(z, axis=1, keepdims=True) - jnp.sum(z.T, axis=1, keepdims=True)
            dgc = dgc + jnp.where(rowi == C - 1, dgl, 0.0)
            yield
            dkb = dkb_p + e_c * dekb
            dk_ref[rows, sl] = dk + dk_a + dk_p + beta_c * dkb
            dbeta = jnp.sum(dkb * kv, axis=1, keepdims=True) + jnp.sum(dvb * vv, axis=1, keepdims=True)
            acc[cc] = acc[cc] + jnp.where(lane == h, dbeta, 0.0) + jnp.where(lane == H + h, dgc, 0.0)

        _interleave(head(oi, h) for oi in range(CPS) for h in range(H))
        for h in range(H):
            ds_ref[h] = state[(CPS, h)]
        ri = lax.broadcasted_iota(jnp.int32, (C, C), 0)
        ci = lax.broadcasted_iota(jnp.int32, (C, C), 1)
        upper = (ri <= ci).astype(F32)
        dal = jnp.zeros((1, LANES), F32)
        ddt = jnp.zeros((1, LANES), F32)
        for cc in range(CPS):
            rows = slice(cc * C, (cc + 1) * C)
            gates = gcol_ref[rows, :]
            dg_all = _mm3(upper, acc[cc])
            d_braw = acc[cc] * gates * (1.0 - gates)
            d_araw = dg_all * (-jnp.exp(al_ref[...])) * _sigmoid(ba_ref[rows, :] + dt_ref[...])
            dgate_ref[rows, :] = jnp.where(lane < H, d_braw, jnp.where(lane < 2 * H, d_araw, 0.0))
            dal = dal + jnp.sum(dg_all * gates, axis=0, keepdims=True)
            ddt = ddt + jnp.sum(d_araw, axis=0, keepdims=True)
        dpar_ref[0:1, :] += dal
        dpar_ref[1:2, :] += ddt

        @pl.when(pl.program_id(0) == NS - 1)
        def _():
            for cp in ccps:
                cp.wait()

    rev = lambda s: NS - 1 - s
    blk = lambda: pl.BlockSpec((CPS * C, AW), lambda s: (rev(s), 0))
    row = pl.BlockSpec((1, LANES), lambda s: (0, 0))
    any_spec = pl.BlockSpec(memory_space=pl.ANY)
    res = pl.pallas_call(
        body, name="delta_bwd", grid=(NS,),
        in_specs=[row, row, blk(), blk(), blk(),
                  pl.BlockSpec((CPS * C, LANES), lambda s: (rev(s), 0)),
                  pl.BlockSpec((CPS, 32, C), lambda s: (rev(s), 0, 0)),
                  pl.BlockSpec((CPS * C, LANES), lambda s: (rev(s), 0)),
                  blk(),
                  pl.BlockSpec((CPS, H, D, D), lambda s: (rev(s), 0, 0, 0)),
                  pl.BlockSpec((CPS, H, C, C), lambda s: (rev(s), 0, 0, 0)),
                  blk()] + [any_spec] * nc,
        out_specs=[blk(), blk(), blk(),
                   pl.BlockSpec((CPS * C, LANES), lambda s: (rev(s), 0)),
                   pl.BlockSpec((8, LANES), lambda s: (0, 0))] + [any_spec] * nc,
        out_shape=[S((T, AW), F32), S((T, AW), F32), S((T, AW), F32),
                   S((T, LANES), F32), S((8, LANES), F32)] + [S((3,) + a.shape[1:], a.dtype) for a in carry],
        scratch_shapes=[pltpu.VMEM((H, D, D), F32),
                        pltpu.SemaphoreType.DMA((max(nc, 1), 3)), pltpu.SemaphoreType.DMA((max(nc, 1), 3))],
        compiler_params=_cp(ARB),
    )(a_log, dt_bias, q, k, v, gcol, grow, ba, vnew, ssave, asave, d_o, *carry)
    return res[:5], res[5:]


def _ln_stats(xv):
    mu = jnp.mean(xv, axis=-1, keepdims=True)
    xc = xv - mu
    var = jnp.mean(xc * xc, axis=-1, keepdims=True)
    rstd = lax.rsqrt(var + EPS)
    return xc * rstd, rstd


def _mix_fwd(o, proj, head_norm_w, ln_w, ln_b, w_sp, bs_t, H, D, G, P):
    T = o.shape[0]
    AW, BW = H * D, G * P
    MIX = AW + BW
    nb = AW // BW if AW % BW == 0 else None
    assert nb == 1, "group widths must match the projection column blocks"
    cb = 3

    def body(o_ref, za_ref, ub_ref, vb_ref, zb_ref, hw_ref, lw_ref, lb_ref, w_ref, bs_ref, out_ref):
        hw = hw_ref[...]
        for h in range(H):
            sl = slice(h * D, (h + 1) * D)
            oh = o_ref[:, sl]
            rs = lax.rsqrt(jnp.mean(oh * oh, axis=-1, keepdims=True) + EPS)
            out_ref[:, sl] = (oh * rs * hw * _silu(za_ref[:, sl])).astype(BF16)
        xhat, _ = _ln_stats(vb_ref[...])
        vn = xhat * lw_ref[...] + lb_ref[...]
        ri = lax.broadcasted_iota(jnp.int32, (P, P), 0)
        ci = lax.broadcasted_iota(jnp.int32, (P, P), 1)
        bsv = bs_ref[...]
        for g in range(G):
            sl = slice(g * P, (g + 1) * P)
            wm = jnp.where(ri >= ci, w_ref[g], 0.0)
            s = _mm(wm, vn[:, sl]) + bsv[:, g:g + 1]
            out_ref[:, AW + g * P:AW + (g + 1) * P] = (ub_ref[:, sl] * s * _silu(zb_ref[:, sl])).astype(BF16)

    row = lambda w: pl.BlockSpec((1, w), lambda i: (0, 0))
    return pl.pallas_call(
        body, name="mix_fwd", grid=(T // P,),
        in_specs=[pl.BlockSpec((P, AW), lambda i: (i, 0)),
                  pl.BlockSpec((P, AW), lambda i: (i, cb)),
                  pl.BlockSpec((P, BW), lambda i: (i, cb + 1)),
                  pl.BlockSpec((P, BW), lambda i: (i, cb + 2)),
                  pl.BlockSpec((P, BW), lambda i: (i, cb + 3)),
                  row(D), row(BW), row(BW),
                  pl.BlockSpec((G, P, P), lambda i: (0, 0, 0)),
                  pl.BlockSpec((P, G), lambda i: (0, 0))],
        out_specs=pl.BlockSpec((P, MIX), lambda i: (i, 0)),
        out_shape=S((T, MIX), BF16),
        compiler_params=_cp(ARB),
    )(o, proj, proj, proj, proj, head_norm_w, ln_w, ln_b, w_sp, bs_t)


def _mix_bwd(d_ocat, o, proj, head_norm_w, ln_w, ln_b, w_sp, bs_t, H, D, G, P, carry):
    T = o.shape[0]
    AW, BW = H * D, G * P
    MIX = AW + BW
    cb = 3
    nc = len(carry)

    def body(dc_ref, o_ref, za_ref, ub_ref, vb_ref, zb_ref, hw_ref, lw_ref, lb_ref, w_ref, bs_ref, *rest):
        cins = rest[:nc]
        do_ref, dmain_ref, dhw_ref, dln_ref, dw_ref, dbs_ref = rest[nc:nc + 6]
        couts = rest[nc + 6:2 * nc + 6]
        dvn_ref, drest_ref, out_sems, csend, crecv = rest[2 * nc + 6:]
        i = pl.program_id(0)
        slot = lax.rem(i, 2)
        ccps = _sibling_copies(cins, couts, csend, crecv)

        def out_copy(step, s):
            return pltpu.make_async_copy(
                drest_ref.at[s], dmain_ref.at[pl.ds(step * P, P), pl.ds(cb * AW, AW + 3 * BW)], out_sems.at[s])

        @pl.when(i == 0)
        def _():
            dhw_ref[...] = jnp.zeros_like(dhw_ref)
            dln_ref[...] = jnp.zeros_like(dln_ref)
            dw_ref[...] = jnp.zeros_like(dw_ref)
            dbs_ref[...] = jnp.zeros_like(dbs_ref)
            for cp in ccps:
                cp.start()

        @pl.when(i >= 2)
        def _():
            out_copy(i - 2, slot).wait()

        hw = hw_ref[...]
        dhw = jnp.zeros((1, D), F32)
        for h in range(H):
            sl = slice(h * D, (h + 1) * D)
            oh = o_ref[:, sl]
            za = za_ref[:, sl]
            doa = dc_ref[:, sl]
            rs = lax.rsqrt(jnp.mean(oh * oh, axis=-1, keepdims=True) + EPS)
            xh = oh * rs
            d_on = doa * _silu(za)
            drest_ref[slot, :, sl] = (doa * (xh * hw) * _dsilu(za)).astype(BF16)
            dhw = dhw + jnp.sum(d_on * xh, axis=0, keepdims=True)
            dxh = d_on * hw
            do_ref[:, sl] = rs * (dxh - xh * jnp.mean(dxh * xh, axis=-1, keepdims=True))
        dhw_ref[0:1, :] += dhw

        xhat, rstd = _ln_stats(vb_ref[...])
        lw = lw_ref[...]
        vn = xhat * lw + lb_ref[...]
        ri = lax.broadcasted_iota(jnp.int32, (P, P), 0)
        ci = lax.broadcasted_iota(jnp.int32, (P, P), 1)
        lane = lax.broadcasted_iota(jnp.int32, (P, LANES), 1)
        bsv = bs_ref[...]
        dbs = jnp.zeros((P, LANES), F32)
        for g in range(G):
            sl = slice(g * P, (g + 1) * P)
            wm = jnp.where(ri >= ci, w_ref[g], 0.0)
            vng = vn[:, sl]
            s = _mm(wm, vng) + bsv[:, g:g + 1]
            dob = dc_ref[:, AW + g * P:AW + (g + 1) * P]
            ub = ub_ref[:, sl]
            zb = zb_ref[:, sl]
            szb = _silu(zb)
            drest_ref[slot, :, AW + g * P:AW + (g + 1) * P] = (dob * s * szb).astype(BF16)
            drest_ref[slot, :, AW + 2 * BW + g * P:AW + 2 * BW + (g + 1) * P] = (
                dob * ub * s * _dsilu(zb)).astype(BF16)
            ds = dob * ub * szb
            dvn_ref[:, sl] = _mm_tn(wm, ds)
            dw_ref[g] += jnp.where(ri >= ci, _mm_nt(ds, vng), 0.0)
            dbs = dbs + jnp.where(lane == g, jnp.sum(ds, axis=1, keepdims=True), 0.0)
        dbs_ref[...] += dbs
        dvn = dvn_ref[...]
        dln_ref[0:1, :] += jnp.sum(dvn * xhat, axis=0, keepdims=True)
        dln_ref[1:2, :] += jnp.sum(dvn, axis=0, keepdims=True)
        dxh = dvn * lw
        dvb = rstd * (dxh - jnp.mean(dxh, axis=-1, keepdims=True) - xhat * jnp.mean(dxh * xhat, axis=-1, keepdims=True))
        drest_ref[slot, :, AW + BW:AW + 2 * BW] = dvb.astype(BF16)

        out_copy(i, slot).start()

        @pl.when(i == nstep - 1)
        def _():
            out_copy(i, slot).wait()
            if nstep > 1:
                out_copy(i - 1, 1 - slot).wait()
            for cp in ccps:
                cp.wait()

    nstep = T // P
    row = lambda w: pl.BlockSpec((1, w), lambda i: (0, 0))
    any_spec = pl.BlockSpec(memory_space=pl.ANY)
    res = pl.pallas_call(
        body, name="mix_bwd", grid=(nstep,),
        in_specs=[pl.BlockSpec((P, MIX), lambda i: (i, 0)),
                  pl.BlockSpec((P, AW), lambda i: (i, 0)),
                  pl.BlockSpec((P, AW), lambda i: (i, cb)),
                  pl.BlockSpec((P, BW), lambda i: (i, cb + 1)),
                  pl.BlockSpec((P, BW), lambda i: (i, cb + 2)),
                  pl.BlockSpec((P, BW), lambda i: (i, cb + 3)),
                  row(D), row(BW), row(BW),
                  pl.BlockSpec((G, P, P), lambda i: (0, 0, 0)),
                  pl.BlockSpec((P, G), lambda i: (0, 0))] + [any_spec] * nc,
        out_specs=[pl.BlockSpec((P, AW), lambda i: (i, 0)),
                   any_spec,
                   pl.BlockSpec((8, D), lambda i: (0, 0)),
                   pl.BlockSpec((8, BW), lambda i: (0, 0)),
                   pl.BlockSpec((G, P, P), lambda i: (0, 0, 0)),
                   pl.BlockSpec((P, LANES), lambda i: (0, 0))] + [any_spec] * nc,
        out_shape=[S((T, AW), F32), S((T, cb * AW + AW + 3 * BW), BF16), S((8, D), F32), S((8, BW), F32),
                   S((G, P, P), F32), S((P, LANES), F32)] + [S(a.shape[:1] + a.shape[2:], a.dtype) for a in carry],
        scratch_shapes=[pltpu.VMEM((P, BW), F32), pltpu.VMEM((2, P, AW + 3 * BW), BF16),
                        pltpu.SemaphoreType.DMA((2,))] + _sibling_sems(carry),
        compiler_params=_cp(ARB),
    )(d_ocat, o, proj, proj, proj, proj, head_norm_w, ln_w, ln_b, w_sp, bs_t, *carry)
    return res[:6], res[6:]


def _out_proj_loss(ocat, w_out, x, target, fnw):
    T, MIX = ocat.shape
    DM = x.shape[1]
    tm = _tile(T, 256, 8)

    def body(oc_ref, w_ref, x_ref, t_ref, fw_ref, dh_ref, dhb_ref, doc_ref, loss_ref, gfw_ref):
        @pl.when(pl.program_id(0) == 0)
        def _():
            loss_ref[...] = jnp.zeros_like(loss_ref)
            gfw_ref[...] = jnp.zeros_like(gfw_ref)

        wv = w_ref[...]
        hh = x_ref[...] + jnp.dot(oc_ref[...].astype(MXU), wv.astype(MXU), preferred_element_type=F32)
        rs = lax.rsqrt(jnp.mean(hh * hh, axis=-1, keepdims=True) + EPS)
        hn = hh * rs
        fw = fw_ref[...]
        e = hn * fw - t_ref[...]
        row_loss = 0.5 * jnp.mean(e * e, axis=-1, keepdims=True)
        loss_ref[...] += jnp.sum(row_loss, axis=0, keepdims=True)
        dy = e * (1.0 / DM)
        gfw_ref[0:1, :] += jnp.sum(dy * hn, axis=0, keepdims=True)
        dhn = dy * fw
        dh = rs * (dhn - hn * jnp.mean(dhn * hn, axis=-1, keepdims=True))
        dh_ref[...] = dh
        dhb = dh.astype(BF16)
        dhb_ref[...] = dhb
        doc_ref[...] = _mm_nt(dhb, wv)

    return pl.pallas_call(
        body, name="out_proj_loss", grid=(T // tm,),
        in_specs=[pl.BlockSpec((tm, MIX), lambda i: (i, 0)),
                  pl.BlockSpec((MIX, DM), lambda i: (0, 0)),
                  pl.BlockSpec((tm, DM), lambda i: (i, 0)),
                  pl.BlockSpec((tm, DM), lambda i: (i, 0)),
                  pl.BlockSpec((1, DM), lambda i: (0, 0))],
        out_specs=[pl.BlockSpec((tm, DM), lambda i: (i, 0)),
                   pl.BlockSpec((tm, DM), lambda i: (i, 0)),
                   pl.BlockSpec((tm, MIX), lambda i: (i, 0)),
                   pl.BlockSpec((8, LANES), lambda i: (0, 0)),
                   pl.BlockSpec((8, DM), lambda i: (0, 0))],
        out_shape=[S((T, DM), F32), S((T, DM), BF16), S((T, MIX), F32), S((8, LANES), F32), S((8, DM), F32)],
        compiler_params=_cp(ARB),
    )(ocat, w_out, x, target, fnw)


def _grad_w(lhs, rhs, name):
    T, A = lhs.shape
    B = rhs.shape[1]
    ta = _tile(A, 512, LANES)
    tk = _tile(T, 1024, 16)
    nk = T // tk

    def body(l_ref, r_ref, out_ref, acc_ref):
        k = pl.program_id(1)
        part = _mm_tn(l_ref[...], r_ref[...])

        @pl.when(k == 0)
        def _():
            acc_ref[...] = part

        @pl.when(k > 0)
        def _():
            acc_ref[...] += part

        @pl.when(k == nk - 1)
        def _():
            out_ref[...] = acc_ref[...].astype(BF16)

    return pl.pallas_call(
        body, name=name, grid=(A // ta, nk),
        in_specs=[pl.BlockSpec((tk, ta), lambda i, k: (k, i)),
                  pl.BlockSpec((tk, B), lambda i, k: (k, 0))],
        out_specs=pl.BlockSpec((ta, B), lambda i, k: (i, 0)),
        out_shape=S((A, B), BF16),
        scratch_shapes=[pltpu.VMEM((ta, B), F32)],
        compiler_params=_cp(ARB, ARB),
    )(lhs, rhs)


def _grad_w_in(xn, dmain, dba, WD, gate_lo, gate_hi):
    T, DM = xn.shape
    NM = dmain.shape[1]
    tn = _tile(NM, 1024, LANES)
    tk = _tile(T, 2048, 16)
    nj, nk = NM // tn, T // tk
    ND = N_DEV
    tiles = [[] for _ in range(nj)]
    first_tile, last_tile = {}, {}
    for d, s0, s1, dest, c0 in _pieces(WD, gate_lo, gate_hi, ND * WD):
        if dest != "main":
            continue
        while s0 < s1:
            jj = c0 // tn
            w = min(s1 - s0, (jj + 1) * tn - c0)
            tiles[jj].append((d, s0, w, "main", c0 - jj * tn))
            first_tile.setdefault(d, jj)
            last_tile[d] = jj
            s0, c0 = s0 + w, c0 + w
    for d, s0, s1, dest, c0 in _pieces(WD, gate_lo, gate_hi, ND * WD):
        if dest == "gate":
            tiles[first_tile[d]].append((d, s0, s1 - s0, "gate", c0))
    assert sorted(first_tile) == list(range(ND)) and all(last_tile[d] <= first_tile[d + 2] for d in range(ND - 2))

    def body(xn_ref, dm_ref, dba_ref, keep_ref, recv_ref, acc_ref, gate_ref, buf_ref, lsem, ssem, rsem):
        j = pl.program_id(0)
        k = pl.program_id(1)
        px, py, pc = _position()

        @pl.when(k == 0)
        def _():
            acc_ref[...] = jnp.zeros_like(acc_ref)

        @pl.when((j == 0) & (k == 0))
        def _():
            gate_ref[...] = jnp.zeros_like(gate_ref)

        xv = xn_ref[...]
        acc_ref[...] += _mm_tn(xv, dm_ref[...])

        @pl.when(j == 0)
        def _():
            gate_ref[...] += _mm_tn(xv, dba_ref[...])

        def local(d):
            return pltpu.make_async_copy(buf_ref.at[d % 2], keep_ref.at[d // 2], lsem.at[d // 2])

        def remote(d):
            return pltpu.make_async_remote_copy(
                src_ref=buf_ref.at[d % 2], dst_ref=recv_ref.at[d // 2], send_sem=ssem.at[d // 2],
                recv_sem=rsem.at[d // 2], device_id=(px, py, 1 - pc), device_id_type=MESH)

        def leave(d, start):
            @pl.when(pc == d % 2)
            def _():
                local(d).start() if start else local(d).wait()

            @pl.when(pc != d % 2)
            def _():
                remote(d).start() if start else remote(d).wait_send()

        def emit(jj):
            shards = sorted({p[0] for p in tiles[jj]})
            for d in shards:
                if first_tile[d] == jj and d >= 2:
                    leave(d - 2, False)
                for dd, s0, w, src, c0 in tiles[jj]:
                    if dd == d:
                        ref = acc_ref if src == "main" else gate_ref
                        buf_ref[d % 2, :, s0:s0 + w] = ref[:, c0:c0 + w].astype(BF16)
                if last_tile[d] == jj:
                    leave(d, True)
            if jj == nj - 1:
                for d in (ND - 2, ND - 1):
                    leave(d, False)
                for q in range(ND // 2):
                    remote(2 * q).wait_recv()

        for jj in range(nj):
            @pl.when((j == jj) & (k == nk - 1))
            def _(jj=jj):
                emit(jj)

    any_spec = pl.BlockSpec(memory_space=pl.ANY)
    return pl.pallas_call(
        body, name="grad_w_in", grid=(nj, nk),
        in_specs=[pl.BlockSpec((tk, DM), lambda j, k: (k, 0)),
                  pl.BlockSpec((tk, tn), lambda j, k: (k, j)),
                  pl.BlockSpec((tk, LANES), lambda j, k: (k, 0))],
        out_specs=[any_spec, any_spec],
        out_shape=[S((ND // 2, DM, WD), BF16), S((ND // 2, DM, WD), BF16)],
        scratch_shapes=[pltpu.VMEM((DM, tn), F32), pltpu.VMEM((DM, LANES), F32), pltpu.VMEM((2, DM, WD), BF16),
                        pltpu.SemaphoreType.DMA((ND // 2,)), pltpu.SemaphoreType.DMA((ND // 2,)),
                        pltpu.SemaphoreType.DMA((ND // 2,))],
        compiler_params=_cp(ARB, ARB),
    )(xn, dmain, dba)


def _pair_sum_plain(a, b, name):
    K, R, C = a.shape
    tr = _tile(R, 256, 16)

    def body(a_ref, b_ref, o_ref):
        o_ref[...] = (a_ref[...].astype(F32) + b_ref[...].astype(F32)).astype(BF16)

    spec = lambda: pl.BlockSpec((1, tr, C), lambda q, i: (q, i, 0))
    return pl.pallas_call(body, name=name, grid=(K, R // tr), in_specs=[spec(), spec()], out_specs=spec(),
                          out_shape=S((K, R, C), BF16), compiler_params=_cp(ARB, ARB))(a, b)


def _dx_rows(T):
    tm = _tile(T, 512, 8)
    return tm if T // tm >= 2 else T // 2


def _dx_part(name, dmain, dba, w_main, w_ba, x, dh, norm_w, blk0, nblk, prev, hbm_in, hbm_alias, hbm_new, make_copies):
    T, NM = dmain.shape
    DM = x.shape[1]
    tm = _dx_rows(T)
    tk = _tile(NM, 1024, LANES)
    nk = NM // tk
    n_in, n_al, n_new = len(hbm_in), len(hbm_alias), len(hbm_new)
    n_prev = 0 if prev is None else 2
    last_step = nblk * nk - 1

    def body(dm_ref, dba_ref, w_ref, wba_ref, x_ref, dh_ref, nw_ref, *rest):
        r = list(rest)
        gnw_prev_ref = r.pop(0) if n_prev else None
        if n_prev:
            r.pop(0)
        in_refs = [r.pop(0) for _ in range(n_in)]
        del r[:n_al]
        gx_ref, gnw_ref = r.pop(0), r.pop(0)
        alias_refs = [r.pop(0) for _ in range(n_al)]
        new_refs = [r.pop(0) for _ in range(n_new)]
        acc_ref, send_sems, recv_sems = r
        i = pl.program_id(0)
        k = pl.program_id(1)
        step = i * nk + k
        cps = make_copies(in_refs, alias_refs, new_refs, send_sems, recv_sems)

        @pl.when(step == 0)
        def _():
            gnw_ref[...] = gnw_prev_ref[...] if n_prev else jnp.zeros_like(gnw_ref)
            for cp in cps:
                cp.start()

        @pl.when(k == 0)
        def _():
            acc_ref[...] = _mm_nt(dba_ref[...], wba_ref[...])

        acc_ref[...] += _mm_nt(dm_ref[...], w_ref[...])

        @pl.when(k == nk - 1)
        def _():
            xv = x_ref[...]
            rs = lax.rsqrt(jnp.mean(xv * xv, axis=-1, keepdims=True) + EPS)
            xh = xv * rs
            dxn = acc_ref[...]
            gnw_ref[0:1, :] += jnp.sum(dxn * xh, axis=0, keepdims=True)
            dxh = dxn * nw_ref[...]
            gx_ref[...] = dh_ref[...] + rs * (dxh - xh * jnp.mean(dxh * xh, axis=-1, keepdims=True))

        @pl.when(step == last_step)
        def _():
            for cp in cps:
                cp.wait()

    any_spec = pl.BlockSpec(memory_space=pl.ANY)
    prev_specs = [pl.BlockSpec((8, DM), lambda i, k: (0, 0)), any_spec] if n_prev else []
    prev_args = [prev[1], prev[0]] if n_prev else []
    aliases = {8: 0} if n_prev else {}
    for q in range(n_al):
        aliases[7 + n_prev + n_in + q] = 2 + q
    res = pl.pallas_call(
        body, name=name, grid=(nblk, nk),
        in_specs=[pl.BlockSpec((tm, tk), lambda i, k: (blk0 + i, k)),
                  pl.BlockSpec((tm, LANES), lambda i, k: (blk0 + i, 0)),
                  pl.BlockSpec((DM, tk), lambda i, k: (0, k)),
                  pl.BlockSpec((DM, LANES), lambda i, k: (0, 0)),
                  pl.BlockSpec((tm, DM), lambda i, k: (blk0 + i, 0)),
                  pl.BlockSpec((tm, DM), lambda i, k: (blk0 + i, 0)),
                  pl.BlockSpec((1, DM), lambda i, k: (0, 0))] + prev_specs + [any_spec] * (n_in + n_al),
        out_specs=[pl.BlockSpec((tm, DM), lambda i, k: (blk0 + i, 0)),
                   pl.BlockSpec((8, DM), lambda i, k: (0, 0))] + [any_spec] * (n_al + n_new),
        out_shape=[S((T, DM), F32), S((8, DM), F32)] + [S(a.shape, a.dtype) for a in hbm_alias] + list(hbm_new),
        scratch_shapes=[pltpu.VMEM((tm, DM), F32), pltpu.SemaphoreType.DMA((10,)), pltpu.SemaphoreType.DMA((10,))],
        input_output_aliases=aliases,
        compiler_params=_cp(ARB, ARB),
    )(dmain, dba, w_main, w_ba, x, dh, norm_w, *prev_args, *hbm_in, *hbm_alias)
    return (res[0], res[1]), res[2:2 + n_al], res[2 + n_al:]


def _remote(kk, src, dst, to, send_sems, recv_sems):
    return pltpu.make_async_remote_copy(src_ref=src, dst_ref=dst, send_sem=send_sems.at[kk], recv_sem=recv_sems.at[kk],
                                        device_id=to, device_id_type=MESH)


def _dx(dmain, dba, w_main, w_ba, x, dh, norm_w, chip_sum, small, cut):
    R, C = chip_sum.shape[1:]
    half = R // 2
    assert half % 16 == 0
    T = x.shape[0]
    ni = T // _dx_rows(T)
    cut = max(1, min(cut, ni - 1))
    upper, lower = pl.ds(0, half), pl.ds(half, half)

    def nbrs():
        px, py, pc = _position()
        return (px, py), (1 - px, py, pc), (px, 1 - py, pc)

    def phase1(ins, als, news, ss, rs):
        (px, py), xn, yn = nbrs()
        cs = ins[0]
        recv, stage = news
        bx, by, bd = cs.at[2 * (1 - px) + py], cs.at[2 * px + (1 - py)], cs.at[2 * (1 - px) + (1 - py)]
        return [_remote(0, bx.at[upper], recv.at[0].at[upper], xn, ss, rs),
                _remote(1, by.at[lower], recv.at[1].at[lower], yn, ss, rs),
                _remote(2, bd.at[upper], stage.at[0], xn, ss, rs),
                _remote(3, bd.at[lower], stage.at[1], yn, ss, rs)]

    def phase2(ins, als, news, ss, rs):
        (px, py), xn, yn = nbrs()
        comb, small_ref = ins
        recv, gath = als[0], news[0]
        me, small_cps = _broadcast_copies([small_ref], [gath], _Sem2(ss, 2), _Sem2(rs, 2))
        return ([_remote(0, comb.at[0], recv.at[1].at[upper], yn, ss, rs),
                 _remote(1, comb.at[1], recv.at[0].at[lower], xn, ss, rs)] + small_cps
                + [pltpu.make_async_copy(small_ref, gath.at[me], ss.at[9])])

    (gx, gnw), _, (recv, stage) = _dx_part(
        "dx_a", dmain, dba, w_main, w_ba, x, dh, norm_w, 0, cut, None, [chip_sum], [],
        [S((2, R, C), chip_sum.dtype), S((2, half, C), chip_sum.dtype)], phase1)
    comb = _relay_add(chip_sum, stage)
    (gx, gnw), (recv,), (gath,) = _dx_part(
        "dx_b", dmain, dba, w_main, w_ba, x, dh, norm_w, cut, ni - cut, (gx, gnw), [comb, small], [recv],
        [S((N_DEV,) + small.shape, F32)], phase2)
    return gx, gnw, gath, recv


class _Sem2:
    def __init__(self, sems, lo):
        self.sems, self.lo = sems, lo

    @property
    def at(self):
        outer = self

        class _At:
            def __getitem__(self, idx):
                a, k = idx
                return outer.sems.at[outer.lo + k]
        return _At()


def _relay_add(chip_sum, stage):
    _, R, C = chip_sum.shape
    half = R // 2
    tr = _tile(half, 256, 16)
    nt = half // tr
    px, py, _ = _position()
    idx = jnp.stack([2 * px + (1 - py), 2 * (1 - px) + py]).astype(jnp.int32)

    def body(idx_ref, p_ref, s_ref, o_ref):
        del idx_ref
        o_ref[0] = (p_ref[0].astype(F32) + s_ref[0].astype(F32)).astype(BF16)

    return pl.pallas_call(
        body, name="relay_add",
        grid_spec=pltpu.PrefetchScalarGridSpec(
            num_scalar_prefetch=1, grid=(2, nt),
            in_specs=[pl.BlockSpec((1, tr, C), lambda s, i, idx_ref: (idx_ref[s], s * nt + i, 0)),
                      pl.BlockSpec((1, tr, C), lambda s, i, idx_ref: (s, i, 0))],
            out_specs=pl.BlockSpec((1, tr, C), lambda s, i, idx_ref: (s, i, 0))),
        out_shape=S((2, half, C), BF16), compiler_params=_cp(ARB, ARB),
    )(idx, chip_sum, stage)


def _sum_slots(gath):
    _, R, C = gath.shape
    tr = _tile(R, 512, 8)

    def body(g_ref, o_ref):
        tot = g_ref[0]
        for d in range(1, N_DEV):
            tot = tot + g_ref[d]
        o_ref[...] = tot

    return pl.pallas_call(
        body, name="sum_slots", grid=(R // tr,),
        in_specs=[pl.BlockSpec((N_DEV, tr, C), lambda i: (0, i, 0))],
        out_specs=pl.BlockSpec((tr, C), lambda i: (i, 0)),
        out_shape=S((R, C), F32), compiler_params=_cp(ARB),
    )(gath)


def _prep_a_bwd(dq, dk, dv, c, proj, conv_w, dmain, H, D):
    T = c.shape[0]
    AW = H * D
    C3 = 3 * AW
    tb = _tile(T, 256, 8)
    nblk = T // tb
    r8 = tb // 8
    scale = float(D) ** -0.5

    def body(dq_ref, dk_ref, dv_ref, c_ref, dqn_ref, dkn_ref, dvn_ref, cn_ref, x_ref, halo_ref, cw_ref, dmain_in_ref,
             dx_ref, gcw_ref, dc_ref):
        del dmain_in_ref
        i = pl.program_id(0)

        @pl.when(i == 0)
        def _():
            gcw_ref[...] = jnp.zeros_like(gcw_ref)

        def pointwise(rows, dq_r, dk_r, dv_r, c_r, keep):
            for h in range(H):
                for part, d_r, sc in ((0, dq_r, scale), (1, dk_r, 1.0)):
                    sl = slice(part * AW + h * D, part * AW + (h + 1) * D)
                    cv = c_r[:, sl]
                    raw = _silu(cv)
                    rs = lax.rsqrt(jnp.sum(raw * raw, axis=-1, keepdims=True) + EPS)
                    nrm = raw * rs
                    dn = d_r[:, h * D:(h + 1) * D] * sc
                    draw = rs * (dn - nrm * jnp.sum(dn * nrm, axis=-1, keepdims=True))
                    dc_ref[rows, sl] = draw * _dsilu(cv) * keep
            dc_ref[rows, 2 * AW:] = dv_r[...] * _dsilu(c_r[:, 2 * AW:]) * keep

        pointwise(slice(0, tb), dq_ref, dk_ref, dv_ref, c_ref, 1.0)
        pointwise(slice(tb, tb + 8), dqn_ref, dkn_ref, dvn_ref, cn_ref, (i < nblk - 1).astype(F32))

        cw = cw_ref[...]
        dcv = dc_ref[0:tb, :]
        dx = cw[3:4, :] * dcv
        for j in range(3):
            dx = dx + cw[j:j + 1, :] * dc_ref[3 - j:3 - j + tb, :]
        dx_ref[...] = dx.astype(BF16)
        halo = halo_ref[...] * (i > 0).astype(F32)
        xp = jnp.concatenate([halo, x_ref[...]], axis=0)
        for j in range(4):
            gcw_ref[j:j + 1, :] += jnp.sum(dcv * xp[5 + j:5 + j + tb], axis=0, keepdims=True)

    nxt = lambda i: (jnp.minimum((i + 1) * r8, T // 8 - 1), 0)
    return pl.pallas_call(
        body, name="prep_a_bwd", grid=(nblk,),
        in_specs=[pl.BlockSpec((tb, AW), lambda i: (i, 0)),
                  pl.BlockSpec((tb, AW), lambda i: (i, 0)),
                  pl.BlockSpec((tb, AW), lambda i: (i, 0)),
                  pl.BlockSpec((tb, C3), lambda i: (i, 0)),
                  pl.BlockSpec((8, AW), nxt), pl.BlockSpec((8, AW), nxt), pl.BlockSpec((8, AW), nxt),
                  pl.BlockSpec((8, C3), nxt),
                  pl.BlockSpec((tb, C3), lambda i: (i, 0)),
                  pl.BlockSpec((8, C3), lambda i: (jnp.maximum(i * r8 - 1, 0), 0)),
                  pl.BlockSpec((4, C3), lambda i: (0, 0)),
                  pl.BlockSpec(memory_space=pl.ANY)],
        out_specs=[pl.BlockSpec((tb, C3), lambda i: (i, 0)),
                   pl.BlockSpec((8, C3), lambda i: (0, 0))],
        out_shape=[S(dmain.shape, dmain.dtype), S((8, C3), F32)],
        scratch_shapes=[pltpu.VMEM((tb + 8, C3), F32)],
        input_output_aliases={11: 0},
        compiler_params=_cp(ARB),
    )(dq, dk, dv, c, dq, dk, dv, c, proj, proj, conv_w, dmain)


def _adam_math(w, g, m, v):
    m2 = ADAM_B1 * m + (1.0 - ADAM_B1) * g
    v2 = ADAM_B2 * v + (1.0 - ADAM_B2) * (g * g)
    m_hat = m2 / (1.0 - ADAM_B1 ** ADAM_STEP)
    v_hat = v2 / (1.0 - ADAM_B2 ** ADAM_STEP)
    delta = -ADAM_LR * (m_hat / (jnp.sqrt(v_hat) + ADAM_EPS) + ADAM_WD * w)
    return delta, m2, v2


def _pair_sum(blocks, recv, core, name):
    K, _, R, C = blocks.shape
    tr = _tile(R, 256, 16)

    def body(core_ref, a_ref, b_ref, o_ref):
        del core_ref
        o_ref[0] = (a_ref[0, 0].astype(F32) + b_ref[0].astype(F32)).astype(BF16)

    spec = lambda: pl.BlockSpec((1, tr, C), lambda k, i, core_ref: (k, i, 0))
    return pl.pallas_call(
        body, name=name,
        grid_spec=pltpu.PrefetchScalarGridSpec(
            num_scalar_prefetch=1, grid=(K, R // tr),
            in_specs=[pl.BlockSpec((1, 1, tr, C), lambda k, i, core_ref: (k, core_ref[0], i, 0)), spec()],
            out_specs=spec()),
        out_shape=S((K, R, C), BF16), compiler_params=_cp(ARB, ARB),
    )(core, blocks, recv)


def _sum_adam(chip_sums, recv, w, m, v, chip, name, transposed=False):
    R, C = chip_sums.shape[1:]
    NR = recv.shape[0]
    tr = _tile(R, 256, 16)

    def body(chip_ref, own_ref, r_ref, w_ref, m_ref, v_ref, g_ref, d_ref, m2_ref, v2_ref):
        del chip_ref
        g = own_ref[0].astype(F32)
        for j in range(NR):
            g = g + r_ref[j].astype(F32)
        if transposed:
            g = g.T
        g_ref[...] = g
        d_ref[...], m2_ref[...], v2_ref[...] = _adam_math(w_ref[...], g, m_ref[...], v_ref[...])

    if transposed:
        spec = lambda: pl.BlockSpec((C, tr), lambda i, chip_ref: (0, i))
        shape = (C, R)
    else:
        spec = lambda: pl.BlockSpec((tr, C), lambda i, chip_ref: (i, 0))
        shape = (R, C)
    assert w.shape == shape
    return pl.pallas_call(
        body, name=name,
        grid_spec=pltpu.PrefetchScalarGridSpec(
            num_scalar_prefetch=1, grid=(R // tr,),
            in_specs=[pl.BlockSpec((1, tr, C), lambda i, chip_ref: (chip_ref[0], i, 0)),
                      pl.BlockSpec((NR, tr, C), lambda i, chip_ref: (0, i, 0)), spec(), spec(), spec()],
            out_specs=[spec(), spec(), spec(), spec()]),
        out_shape=[S(shape, F32)] * 4, compiler_params=_cp(ARB),
    )(chip, chip_sums, recv, w, m, v)


def _adam_small(w, g, m, v):
    R, C = w.shape
    tr = _tile(R, 512, 8)

    def body(w_ref, g_ref, m_ref, v_ref, d_ref, m2_ref, v2_ref):
        d_ref[...], m2_ref[...], v2_ref[...] = _adam_math(w_ref[...], g_ref[...], m_ref[...], v_ref[...])

    spec = lambda: pl.BlockSpec((tr, C), lambda i: (i, 0))
    return pl.pallas_call(
        body, name="adam_small", grid=(R // tr,), in_specs=[spec()] * 4, out_specs=[spec()] * 3,
        out_shape=[S((R, C), F32)] * 3, compiler_params=_cp(ARB),
    )(w, g, m, v)


def _position():
    return lax.axis_index("x"), lax.axis_index("y"), lax.axis_index("c")


def _all_gather_weights(arr):
    R = arr.shape[0]
    half = R // 2
    assert half % 16 == 0

    def body(in_ref, out_ref, send_sems, recv_sems, local_sem):
        x, y, c = _position()
        me, sibling = (x, y, c), (x, y, 1 - c)
        xn, yn, diag = (1 - x, y), (x, 1 - y), (1 - x, 1 - y)
        upper, lower = pl.ds(0, half), pl.ds(half, half)

        def slot(p, rows=None):
            ref = out_ref.at[4 * p[0] + 2 * p[1] + p[2]]
            return ref if rows is None else ref.at[rows]

        def copy(kk, block, to, rows=None, src=None):
            return pltpu.make_async_remote_copy(
                src_ref=slot(block, rows) if src is None else src, dst_ref=slot(block, rows),
                send_sem=send_sems.at[kk], recv_sem=recv_sems.at[kk], device_id=to, device_id_type=MESH)

        mine = pltpu.make_async_copy(in_ref, slot(me), local_sem)
        mine.start()
        sent = [copy(0, me, sibling, src=in_ref), copy(1, me, (*xn, c), src=in_ref), copy(2, me, (*yn, c), src=in_ref)]
        for cp in sent:
            cp.start()

        def then(cps):
            for cp in cps:
                cp.start()
            sent.extend(cps)

        copy(1, (*xn, c), me).wait_recv()
        then([copy(5, (*xn, c), (*yn, c), rows=upper), copy(3, (*xn, c), sibling)])
        copy(2, (*yn, c), me).wait_recv()
        then([copy(6, (*yn, c), (*xn, c), rows=lower), copy(4, (*yn, c), sibling)])
        copy(5, (*diag, c), me, rows=upper).wait_recv()
        then([copy(7, (*diag, c), sibling, rows=upper)])
        copy(6, (*diag, c), me, rows=lower).wait_recv()
        then([copy(8, (*diag, c), sibling, rows=lower)])
        copy(0, sibling, me).wait_recv()
        copy(3, (*xn, 1 - c), me).wait_recv()
        copy(4, (*yn, 1 - c), me).wait_recv()
        copy(7, (*diag, 1 - c), me, rows=upper).wait_recv()
        copy(8, (*diag, 1 - c), me, rows=lower).wait_recv()
        for cp in sent:
            cp.wait_send()
        mine.wait()

    any_spec = pl.BlockSpec(memory_space=pl.ANY)
    return pl.pallas_call(
        body, name="all_gather_weights", in_specs=[any_spec], out_specs=any_spec,
        out_shape=S((N_DEV,) + arr.shape, arr.dtype),
        scratch_shapes=[pltpu.SemaphoreType.DMA((9,)), pltpu.SemaphoreType.DMA((9,)), pltpu.SemaphoreType.DMA],
    )(arr)


def _sibling_copies(ins, outs, send_sems, recv_sems):
    x, y, c = _position()
    return [pltpu.make_async_remote_copy(src_ref=ins[a].at[k, 1 - c], dst_ref=outs[a].at[k],
                                         send_sem=send_sems.at[a, k], recv_sem=recv_sems.at[a, k],
                                         device_id=(x, y, 1 - c), device_id_type=MESH)
            for a in range(len(ins)) for k in range(ins[a].shape[0])]


def _sibling_sems(arrs):
    shape = (max(len(arrs), 1), arrs[0].shape[0] if arrs else 1)
    return [pltpu.SemaphoreType.DMA(shape), pltpu.SemaphoreType.DMA(shape)]


def _chip_exchange_copies(ins, outs, send_sems, recv_sems):
    x, y, c = _position()
    chips = [(1 - x, y), (x, 1 - y), (1 - x, 1 - y)]
    return [pltpu.make_async_remote_copy(
        src_ref=ins[a].at[2 * qx + qy], dst_ref=outs[a].at[j], send_sem=send_sems.at[a, j],
        recv_sem=recv_sems.at[a, j], device_id=(qx, qy, c), device_id_type=MESH)
        for a in range(len(ins)) for j, (qx, qy) in enumerate(chips)]


def _broadcast_copies(srcs, dsts, send_sems, recv_sems):
    x, y, c = _position()
    me = 4 * x + 2 * y + c
    cps = []
    for a in range(len(srcs)):
        for k in range(1, N_DEV):
            peer = (1 - x if k & 4 else x, 1 - y if k & 2 else y, 1 - c if k & 1 else c)
            cps.append(pltpu.make_async_remote_copy(
                src_ref=srcs[a], dst_ref=dsts[a].at[me], send_sem=send_sems.at[a, k - 1],
                recv_sem=recv_sems.at[a, k - 1], device_id=peer, device_id_type=MESH))
    return me, cps


def _all_reduce_small(part):
    R, C = part.shape

    def body(p_ref, out_ref, gath_ref, send_sems, recv_sems):
        me, cps = _broadcast_copies([p_ref], [gath_ref], send_sems, recv_sems)
        gath_ref[me] = p_ref[...]
        for cp in cps:
            cp.start()
        for cp in cps:
            cp.wait()
        acc = gath_ref[0]
        for d in range(1, N_DEV):
            acc = acc + gath_ref[d]
        out_ref[...] = acc

    vm = pl.BlockSpec(memory_space=pltpu.VMEM)
    return pl.pallas_call(
        body, name="all_reduce_small", in_specs=[vm], out_specs=vm, out_shape=S((R, C), F32),
        scratch_shapes=[pltpu.VMEM((N_DEV, R, C), F32), pltpu.SemaphoreType.DMA((1, N_DEV - 1)),
                        pltpu.SemaphoreType.DMA((1, N_DEV - 1))],
    )(part)


def _pack(parts):
    rows = []
    for p in parts:
        f = p.reshape(-1).astype(F32)
        pad = (-f.shape[0]) % (8 * LANES)
        rows.append(jnp.pad(f, (0, pad)).reshape(-1, LANES))
    return jnp.concatenate(rows, axis=0)


def _unpack(buf, shapes):
    out, r = [], 0
    for shp in shapes:
        n = 1
        for s in shp:
            n *= s
        nr = -(-n // (8 * LANES)) * 8
        out.append(buf[r:r + nr].reshape(-1)[:n].reshape(shp))
        r += nr
    return out


def kernel(x, norm_w, w_in, conv_w, a_log, dt_bias, head_norm_w, sgu_ln_w, sgu_ln_b, w_spatial, b_spatial, w_out, final_norm_w, loss_target, m_norm_w, m_w_in, m_conv_w, m_a_log, m_dt_bias, m_head_norm_w, m_sgu_ln_w, m_sgu_ln_b, m_w_spatial, m_b_spatial, m_w_out, m_final_norm_w, v_norm_w, v_w_in, v_conv_w, v_a_log, v_dt_bias, v_head_norm_w, v_sgu_ln_w, v_sgu_ln_b, v_w_spatial, v_b_spatial, v_w_out, v_final_norm_w):
    T, DM = x.shape[1], x.shape[2]
    H, D = a_log.shape[1], head_norm_w.shape[1]
    G, P = w_spatial.shape[1], w_spatial.shape[2]
    AW, BW = H * D, G * P
    MIX = AW + BW
    WD = w_in.shape[2]
    IN = N_DEV * WD
    RO = w_out.shape[1]
    CW = conv_w.shape[2]
    sizes = (3 * AW, AW, H, H, BW, BW, BW)
    assert sum(sizes) == IN and 2 * H <= LANES and 3 * H <= 32 and N_DEV * RO == MIX and N_DEV * CW == 3 * AW
    offs = [0]
    for s in sizes:
        offs.append(offs[-1] + s)
    px, py, pc = _position()
    dev = 4 * px + 2 * py + pc
    chip = 2 * px + py

    x2, tgt = x[0], loss_target[0]

    g_win = _all_gather_weights(_cast_bf16_t(w_in[0].T, "cast_w_in"))
    w_main, w_ba = _relayout_w(g_win, offs[2], offs[4])
    alog_row = jnp.pad(a_log, ((0, 0), (H, LANES - 2 * H)))
    dtb_row = jnp.pad(dt_bias, ((0, 0), (H, LANES - 2 * H)))
    bs_t = b_spatial[0].T

    xn = _rms_xn(x2, norm_w)
    proj, ba, (g_wout, g_conv) = _in_proj(xn, w_main, w_ba, [_cast_bf16(w_out[0], "cast_w_out"), conv_w[0]])
    w_out_full = g_wout.reshape(MIX, DM)
    conv_full = g_conv.transpose(1, 0, 2).reshape(4, 3 * AW)
    q, k, v, c, gcol, grow = _prep_a_fwd(proj, ba, conv_full, alog_row, dtb_row, H, D)
    o, vnew, ssave, asave = _delta_fwd(q, k, v, gcol, grow, H, D)
    ocat = _mix_fwd(o, proj, head_norm_w, sgu_ln_w, sgu_ln_b, w_spatial[0], bs_t, H, D, G, P)
    dh, dh_bf, d_ocat, loss_acc, g_fnw = _out_proj_loss(ocat, w_out_full, x2, tgt, final_norm_w.reshape(1, DM))
    loss = lax.psum(loss_acc[0, 0], AXES)

    core_idx = jnp.reshape(pc, (1,)).astype(jnp.int32)
    chip_idx = jnp.reshape(chip, (1,)).astype(jnp.int32)
    g_wout_blocks = _grad_w(ocat, dh_bf, "grad_w_out").reshape(4, 2, RO, DM)
    (d_o, dmain, g_hnw, g_ln, g_wsp, g_bs_t), (sib_wout,) = _mix_bwd(
        d_ocat, o, proj, head_norm_w, sgu_ln_w, sgu_ln_b, w_spatial[0], bs_t, H, D, G, P, [g_wout_blocks])
    chip_wout = _pair_sum(g_wout_blocks, sib_wout, core_idx, "pair_sum_w_out")
    (dq, dk, dv, dgate, dpar), (recv_wout,) = _delta_bwd(
        q, k, v, gcol, grow, ba, vnew, ssave, asave, d_o, alog_row, dtb_row, H, D, [chip_wout])
    dmain, g_conv_part = _prep_a_bwd(dq, dk, dv, c, proj, conv_full, dmain, H, D)
    dba = dgate.astype(BF16)
    keep_win, sib_win = _grad_w_in(xn, dmain, dba, WD, offs[2], offs[4])
    chip_win = _pair_sum_plain(keep_win, sib_win, "pair_sum_w_in")
    small_shapes = [a_log.shape, dt_bias.shape, head_norm_w.shape, sgu_ln_w.shape, sgu_ln_b.shape,
                    w_spatial.shape, b_spatial.shape, final_norm_w.shape]
    parts = [dpar[0, H:2 * H], dpar[1, H:2 * H], g_hnw[0], g_ln[0], g_ln[1], g_wsp, g_bs_t[:, :G].T, g_fnw[0],
             g_conv_part[:4]]
    grad_x, g_nw, small_gath, recv_win = _dx(dmain, dba, w_main, w_ba, x2, dh, norm_w, chip_win, _pack(parts), 3)
    red = _sum_slots(small_gath)
    grad_w_in, delta_w_in, new_m_w_in, new_v_w_in = _sum_adam(
        chip_win, recv_win, w_in[0].T, m_w_in[0].T, v_w_in[0].T, chip_idx, "sum_adam_w_in", transposed=True)
    grad_w_out, delta_w_out, new_m_w_out, new_v_w_out = _sum_adam(
        chip_wout, recv_wout, w_out[0], m_w_out[0], v_w_out[0], chip_idx, "sum_adam_w_out")
    red_nw = _all_reduce_small(_pack([g_nw[0]]))
    grads_small = _unpack(red_nw, [norm_w.shape]) + _unpack(red, small_shapes + [(4, 3 * AW)])
    g_conv_full = grads_small.pop()
    grad_conv = lax.dynamic_slice_in_dim(g_conv_full, dev * CW, CW, axis=1)[None]
    small_w = [norm_w, a_log, dt_bias, head_norm_w, sgu_ln_w, sgu_ln_b, w_spatial, b_spatial, final_norm_w, conv_w]
    small_m = [m_norm_w, m_a_log, m_dt_bias, m_head_norm_w, m_sgu_ln_w, m_sgu_ln_b, m_w_spatial, m_b_spatial,
               m_final_norm_w, m_conv_w]
    small_v = [v_norm_w, v_a_log, v_dt_bias, v_head_norm_w, v_sgu_ln_w, v_sgu_ln_b, v_w_spatial, v_b_spatial,
               v_final_norm_w, v_conv_w]
    small_g = grads_small + [grad_conv]
    shapes10 = [w.shape for w in small_w]
    d_p, m_p, v_p = _adam_small(_pack(small_w), _pack(small_g), _pack(small_m), _pack(small_v))
    d_s, m_s, v_s = _unpack(d_p, shapes10), _unpack(m_p, shapes10), _unpack(v_p, shapes10)

    def order(small, win, wout):
        return [small[0], win.T[None], small[9], small[1], small[2], small[3], small[4], small[5], small[6], small[7],
                wout[None], small[8]]

    grads = order(small_g, grad_w_in, grad_w_out)
    deltas = order(d_s, delta_w_in, delta_w_out)
    new_m = order(m_s, new_m_w_in, new_m_w_out)
    new_v = order(v_s, new_v_w_in, new_v_w_out)
    return (loss, grad_x[None], *grads, *deltas, *new_m, *new_v)
```

```python
import functools

import jax
import jax.numpy as jnp
from jax import lax
from jax.experimental import pallas as pl
from jax.experimental.pallas import tpu as pltpu

F32 = jnp.float32
BF16 = jnp.bfloat16
MXU = jnp.bfloat16
HI = lax.Precision.HIGHEST
EPS = 1e-6
CHUNK_A = 64
LANES = 128
MESH = pl.DeviceIdType.MESH
AXES = ("x", "y", "c")
N_DEV = 8

ADAM_LR = 0.001
ADAM_B1 = 0.9
ADAM_B2 = 0.999
ADAM_EPS = 1e-08
ADAM_WD = 0.01
ADAM_STEP = 10

S = jax.ShapeDtypeStruct
ARB = "arbitrary"


def _cp(*sem, vmem_mib=56):
    return pltpu.CompilerParams(dimension_semantics=tuple(sem), vmem_limit_bytes=vmem_mib * 1024 * 1024)


def _tile(n, cap, mult):
    best = None
    t = mult
    while t <= min(n, cap):
        if n % t == 0:
            best = t
        t += mult
    return best if best is not None else n


def _mm(a, b):
    return jnp.dot(a.astype(MXU), b.astype(MXU), preferred_element_type=F32)


def _mm_nt(a, b):
    return lax.dot_general(a.astype(MXU), b.astype(MXU), (((1,), (1,)), ((), ())), preferred_element_type=F32)


def _mm_tn(a, b):
    return lax.dot_general(a.astype(MXU), b.astype(MXU), (((0,), (0,)), ((), ())), preferred_element_type=F32)


def _mmh(a, b):
    return jnp.dot(a, b, precision=HI, preferred_element_type=F32)


def _mmh_tn(a, b):
    return lax.dot_general(a, b, (((0,), (0,)), ((), ())), precision=HI, preferred_element_type=F32)


def _sigmoid(x):
    return 1.0 / (1.0 + jnp.exp(-x))


def _silu(x):
    return x * _sigmoid(x)


def _dsilu(x):
    s = _sigmoid(x)
    return s * (1.0 + x * (1.0 - s))


def _softplus(x):
    return jnp.maximum(x, 0.0) + jnp.log(1.0 + jnp.exp(-jnp.abs(x)))


def _pieces(wd, gate_lo, gate_hi, total):
    out = []
    for d in range(N_DEV):
        lo, hi = d * wd, (d + 1) * wd
        for dest, a, b, shift in (("main", 0, gate_lo, 0), ("gate", gate_lo, gate_hi, -gate_lo),
                                  ("main", gate_hi, total, gate_lo - gate_hi)):
            s0, s1 = max(lo, a), min(hi, b)
            if s0 < s1:
                out.append((d, s0 - lo, s1 - lo, dest, s0 + shift))
    return out


def _cast_bf16(a, name):
    R, C = a.shape
    tr = _tile(R, 256, 16)

    def body(a_ref, o_ref):
        o_ref[...] = a_ref[...].astype(BF16)

    spec = pl.BlockSpec((tr, C), lambda i: (i, 0))
    return pl.pallas_call(body, name=name, grid=(R // tr,), in_specs=[spec], out_specs=spec,
                          out_shape=S((R, C), BF16), compiler_params=_cp(ARB))(a)


def _cast_bf16_t(a_t, name):
    C, R = a_t.shape
    tr = _tile(R, 256, LANES)

    def body(a_ref, o_ref):
        o_ref[...] = a_ref[...].T.astype(BF16)

    return pl.pallas_call(body, name=name, grid=(R // tr,), in_specs=[pl.BlockSpec((C, tr), lambda i: (0, i))],
                          out_specs=pl.BlockSpec((tr, C), lambda i: (i, 0)),
                          out_shape=S((R, C), BF16), compiler_params=_cp(ARB))(a_t)


def _relayout_w(g_win, gate_lo, gate_hi):
    _, DM, WD = g_win.shape
    total = N_DEV * WD
    NM = total - (gate_hi - gate_lo)
    tr = _tile(DM, 256, 16)
    plan = _pieces(WD, gate_lo, gate_hi, total)

    def body(g_ref, main_ref, gate_ref):
        gate_ref[...] = jnp.zeros_like(gate_ref)
        for d, s0, s1, dest, c0 in plan:
            dst = main_ref if dest == "main" else gate_ref
            dst[:, c0:c0 + (s1 - s0)] = g_ref[d, :, s0:s1]

    return pl.pallas_call(
        body, name="relayout_w", grid=(DM // tr,),
        in_specs=[pl.BlockSpec((N_DEV, tr, WD), lambda i: (0, i, 0))],
        out_specs=[pl.BlockSpec((tr, NM), lambda i: (i, 0)), pl.BlockSpec((tr, LANES), lambda i: (i, 0))],
        out_shape=[S((DM, NM), g_win.dtype), S((DM, LANES), g_win.dtype)],
        compiler_params=_cp(ARB),
    )(g_win)


def _rms_xn(x, norm_w):
    T, DM = x.shape
    tm = _tile(T, 512, 16)

    def body(x_ref, nw_ref, o_ref):
        xv = x_ref[...]
        r = lax.rsqrt(jnp.mean(xv * xv, axis=-1, keepdims=True) + EPS)
        o_ref[...] = (xv * r * nw_ref[...]).astype(BF16)

    return pl.pallas_call(
        body, name="rms_xn", grid=(T // tm,),
        in_specs=[pl.BlockSpec((tm, DM), lambda i: (i, 0)), pl.BlockSpec((1, DM), lambda i: (0, 0))],
        out_specs=pl.BlockSpec((tm, DM), lambda i: (i, 0)),
        out_shape=S((T, DM), BF16), compiler_params=_cp(ARB),
    )(x, norm_w)


def _in_proj(xn, w_main, w_ba, shards):
    T, DM = xn.shape
    NM = w_main.shape[1]
    tm = _tile(T, 2048, 16)
    tn = _tile(NM, 1024, LANES)
    ni, nj = T // tm, NM // tn
    ns = len(shards)

    def body(xn_ref, w_ref, wba_ref, *rest):
        srcs = rest[:ns]
        proj_ref, ba_ref = rest[ns:ns + 2]
        gath = rest[ns + 2:2 * ns + 2]
        send_sems, recv_sems, local_sems = rest[2 * ns + 2:]
        i = pl.program_id(0)
        j = pl.program_id(1)
        me, cps = _broadcast_copies(srcs, gath, send_sems, recv_sems)
        cps = cps + [pltpu.make_async_copy(srcs[a], gath[a].at[me], local_sems.at[a]) for a in range(ns)]

        @pl.when((i == 0) & (j == 0))
        def _():
            for cp in cps:
                cp.start()

        @pl.when(j == 0)
        def _():
            ba_ref[...] = jnp.dot(xn_ref[...].astype(MXU), wba_ref[...].astype(MXU), preferred_element_type=F32)

        proj_ref[...] = jnp.dot(xn_ref[...].astype(MXU), w_ref[...].astype(MXU), preferred_element_type=F32)

        @pl.when((i == ni - 1) & (j == nj - 1))
        def _():
            for cp in cps:
                cp.wait()

    any_spec = pl.BlockSpec(memory_space=pl.ANY)
    res = pl.pallas_call(
        body, name="in_proj", grid=(ni, nj),
        in_specs=[pl.BlockSpec((tm, DM), lambda i, j: (i, 0)),
                  pl.BlockSpec((DM, tn), lambda i, j: (0, j)),
                  pl.BlockSpec((DM, LANES), lambda i, j: (0, 0))] + [any_spec] * ns,
        out_specs=[pl.BlockSpec((tm, tn), lambda i, j: (i, j)),
                   pl.BlockSpec((tm, LANES), lambda i, j: (i, 0))] + [any_spec] * ns,
        out_shape=[S((T, NM), F32), S((T, LANES), F32)] + [S((N_DEV,) + a.shape, a.dtype) for a in shards],
        scratch_shapes=[pltpu.SemaphoreType.DMA((ns, N_DEV - 1)), pltpu.SemaphoreType.DMA((ns, N_DEV - 1)),
                        pltpu.SemaphoreType.DMA((ns,))],
        compiler_params=_cp(ARB, ARB, vmem_mib=58),
    )(xn, w_main, w_ba, *shards)
    return res[0], res[1], res[2:]


def _prep_a_fwd(proj, ba, conv_w, alog_row, dtb_row, H, D):
    T = proj.shape[0]
    AW = H * D
    C3 = 3 * AW
    tb = _tile(T, 256, CHUNK_A)
    nch = tb // CHUNK_A
    nblk = T // tb
    scale = float(D) ** -0.5

    def body(x_ref, halo_ref, ba_ref, cw_ref, al_ref, dt_ref, q_ref, k_ref, v_ref, c_ref, gcol_ref, grow_ref):
        i = pl.program_id(0)
        xv = x_ref[...]
        halo = halo_ref[...] * (i > 0).astype(F32)
        xp = jnp.concatenate([halo, xv], axis=0)
        cw = cw_ref[...]
        c = cw[0:1, :] * xp[5:5 + tb]
        for j in range(1, 4):
            c = c + cw[j:j + 1, :] * xp[5 + j:5 + j + tb]
        c_ref[...] = c
        a = _silu(c)
        for h in range(H):
            qh = a[:, h * D:(h + 1) * D]
            kh = a[:, AW + h * D:AW + (h + 1) * D]
            qr = lax.rsqrt(jnp.sum(qh * qh, axis=-1, keepdims=True) + EPS)
            kr = lax.rsqrt(jnp.sum(kh * kh, axis=-1, keepdims=True) + EPS)
            q_ref[:, h * D:(h + 1) * D] = qh * (qr * scale)
            k_ref[:, h * D:(h + 1) * D] = kh * kr
        v_ref[...] = a[:, 2 * AW:]

        bav = ba_ref[...]
        lane = lax.broadcasted_iota(jnp.int32, (tb, LANES), 1)
        beta = _sigmoid(bav)
        g = -jnp.exp(al_ref[...]) * _softplus(bav + dt_ref[...])
        gates = jnp.where(lane < H, beta, jnp.where(lane < 2 * H, g, 0.0))
        ri = lax.broadcasted_iota(jnp.int32, (CHUNK_A, CHUNK_A), 0)
        ci = lax.broadcasted_iota(jnp.int32, (CHUNK_A, CHUNK_A), 1)
        tri = (ri >= ci).astype(F32)
        lane_c = lax.broadcasted_iota(jnp.int32, (CHUNK_A, LANES), 1)
        for cc in range(nch):
            gch = gates[cc * CHUNK_A:(cc + 1) * CHUNK_A]
            gc = pltpu.roll(_mmh(tri, gch), H, 1)
            full = jnp.where(lane_c < 2 * H, gch, jnp.where(lane_c < 3 * H, gc, 0.0))
            gcol_ref[cc * CHUNK_A:(cc + 1) * CHUNK_A, :] = full
            grow_ref[cc] = full.T[0:32, :]

    return pl.pallas_call(
        body, name="prep_a_fwd", grid=(nblk,),
        in_specs=[pl.BlockSpec((tb, C3), lambda i: (i, 0)),
                  pl.BlockSpec((8, C3), lambda i: (jnp.maximum(i * (tb // 8) - 1, 0), 0)),
                  pl.BlockSpec((tb, LANES), lambda i: (i, 0)),
                  pl.BlockSpec((4, C3), lambda i: (0, 0)),
                  pl.BlockSpec((1, LANES), lambda i: (0, 0)),
                  pl.BlockSpec((1, LANES), lambda i: (0, 0))],
        out_specs=[pl.BlockSpec((tb, AW), lambda i: (i, 0)),
                   pl.BlockSpec((tb, AW), lambda i: (i, 0)),
                   pl.BlockSpec((tb, AW), lambda i: (i, 0)),
                   pl.BlockSpec((tb, C3), lambda i: (i, 0)),
                   pl.BlockSpec((tb, LANES), lambda i: (i, 0)),
                   pl.BlockSpec((nch, 32, CHUNK_A), lambda i: (i, 0, 0))],
        out_shape=[S((T, AW), F32), S((T, AW), F32), S((T, AW), F32), S((T, C3), F32),
                   S((T, LANES), F32), S((T // CHUNK_A, 32, CHUNK_A), F32)],
        compiler_params=_cp(ARB),
    )(proj, proj, ba, conv_w, alog_row, dtb_row)


_NN = (((1,), (0,)), ((), ()))
_TN = (((0,), (0,)), ((), ()))


def _split(a):
    hi = a.astype(BF16)
    return hi, (a - hi.astype(F32)).astype(BF16)


def _mm3(a, b, dims=_NN):
    ah, al = a if isinstance(a, tuple) else _split(a)
    bh, bl = b if isinstance(b, tuple) else _split(b)
    dg = lambda p, r: lax.dot_general(p, r, dims, preferred_element_type=F32)
    return dg(ah, bh) + (dg(ah, bl) + dg(al, bh))


def _interleave(gens):
    gens = list(gens)
    while gens:
        alive = []
        for g in gens:
            try:
                next(g)
                alive.append(g)
            except StopIteration:
                pass
        gens = alive


def _chunk_terms(q, k, v, gcolv, growv, h, H):
    C = CHUNK_A
    beta_c = gcolv[:, h:h + 1]
    g_c = gcolv[:, H + h:H + h + 1]
    gc_c = gcolv[:, 2 * H + h:2 * H + h + 1]
    gc_r = growv[2 * H + h:2 * H + h + 1, :]
    ri = lax.broadcasted_iota(jnp.int32, (C, C), 0)
    ci = lax.broadcasted_iota(jnp.int32, (C, C), 1)
    incl = ri >= ci
    strict = ri > ci
    kb = k * beta_c
    vb = v * beta_c
    p_raw = _mm_nt(kb, k)
    qk_raw = _mm_nt(q, k)
    gam = jnp.where(incl, jnp.exp(jnp.where(incl, gc_c - gc_r, 0.0)), 0.0)
    e_c = jnp.exp(gc_c)
    gl = gc_r[:, C - 1:C]
    edec = jnp.exp(gl - gc_c)
    yield
    lmat = jnp.where(strict, p_raw * gam, 0.0)
    attn = jnp.where(incl, qk_raw * gam, 0.0)
    return dict(beta_c=beta_c, g_c=g_c, gc_c=gc_c, gc_r=gc_r, incl=incl, strict=strict, gam=gam, e_c=e_c,
                kb=kb, vb=vb, lmat=lmat, attn=attn, gl=gl, edec=edec, ri=ri, ci=ci)


def _inv_unit_lower(lmat):
    C = lmat.shape[0]
    ri = lax.broadcasted_iota(jnp.int32, (C, C), 0)
    ci = lax.broadcasted_iota(jnp.int32, (C, C), 1)
    eye = (ri == ci).astype(F32)
    x = -lmat
    a = eye + x
    n = 1
    while 2 * n < C:
        xs = _split(x)
        x = _mm3(xs, xs)
        yield
        a = a + _mm3(a, x)
        n *= 2
    yield
    return a


def _delta_fwd(q, k, v, gcol, grow, H, D):
    T = q.shape[0]
    C = CHUNK_A
    N = T // C
    AW = H * D
    CPS = 2 if N % 2 == 0 else 1

    def body(q_ref, k_ref, v_ref, gcol_ref, grow_ref, o_ref, vn_ref, ssave_ref, asave_ref, s_ref):
        @pl.when(pl.program_id(0) == 0)
        def _():
            s_ref[...] = jnp.zeros_like(s_ref)

        state = {(0, h): s_ref[h] for h in range(H)}

        def head(cc, h):
            rows = slice(cc * C, (cc + 1) * C)
            sl = slice(h * D, (h + 1) * D)
            qv, kv, vv = q_ref[rows, sl], k_ref[rows, sl], v_ref[rows, sl]
            t = yield from _chunk_terms(qv, kv, vv, gcol_ref[rows, :], grow_ref[cc], h, H)
            a = yield from _inv_unit_lower(t["lmat"])
            asave_ref[cc, h] = a
            while (cc, h) not in state:
                yield
            st = state[(cc, h)]
            ssave_ref[cc, h] = st
            ks = _mm(t["kb"] * t["e_c"], st)
            o_inter = _mm(qv * t["e_c"], st)
            yield
            v_new = _mm3(a, t["vb"] - ks)
            yield
            vn_ref[rows, sl] = v_new
            o_intra = _mm(t["attn"], v_new)
            s_upd = _mm_tn(kv * t["edec"], v_new)
            yield
            o_ref[rows, sl] = o_inter + o_intra
            state[(cc + 1, h)] = st * jnp.exp(t["gl"]) + s_upd

        _interleave(head(cc, h) for cc in range(CPS) for h in range(H))
        for h in range(H):
            s_ref[h] = state[(CPS, h)]

    blk = lambda: pl.BlockSpec((CPS * C, AW), lambda n: (n, 0))
    return pl.pallas_call(
        body, name="delta_fwd", grid=(N // CPS,),
        in_specs=[blk(), blk(), blk(),
                  pl.BlockSpec((CPS * C, LANES), lambda n: (n, 0)),
                  pl.BlockSpec((CPS, 32, C), lambda n: (n, 0, 0))],
        out_specs=[blk(), blk(),
                   pl.BlockSpec((CPS, H, D, D), lambda n: (n, 0, 0, 0)),
                   pl.BlockSpec((CPS, H, C, C), lambda n: (n, 0, 0, 0))],
        out_shape=[S((T, AW), F32), S((T, AW), F32), S((N, H, D, D), F32), S((N, H, C, C), F32)],
        scratch_shapes=[pltpu.VMEM((H, D, D), F32)],
        compiler_params=_cp(ARB),
    )(q, k, v, gcol, grow)


def _delta_bwd(q, k, v, gcol, grow, ba, vnew, ssave, asave, d_o, a_log, dt_bias, H, D, carry):
    T = q.shape[0]
    C = CHUNK_A
    N = T // C
    AW = H * D
    nc = len(carry)
    CPS = 2 if N % 2 == 0 else 1
    NS = N // CPS

    def body(al_ref, dt_ref, q_ref, k_ref, v_ref, gcol_ref, grow_ref, ba_ref, vn_ref, ss_ref, as_ref, do_ref, *rest):
        cins = rest[:nc]
        dq_ref, dk_ref, dv_ref, dgate_ref, dpar_ref = rest[nc:nc + 5]
        couts = rest[nc + 5:2 * nc + 5]
        ds_ref, csend, crecv = rest[2 * nc + 5:]
        ccps = _chip_exchange_copies(cins, couts, csend, crecv)

        @pl.when(pl.program_id(0) == 0)
        def _():
            ds_ref[...] = jnp.zeros_like(ds_ref)
            dpar_ref[...] = jnp.zeros_like(dpar_ref)
            for cp in ccps:
                cp.start()

        lane = lax.broadcasted_iota(jnp.int32, (C, LANES), 1)
        rowi = lax.broadcasted_iota(jnp.int32, (C, 1), 0)
        acc = {cc: jnp.zeros((C, LANES), F32) for cc in range(CPS)}
        state = {(0, h): ds_ref[h] for h in range(H)}

        def head(oi, h):
            cc = CPS - 1 - oi
            rows = slice(cc * C, (cc + 1) * C)
            sl = slice(h * D, (h + 1) * D)
            st = ss_ref[cc, h]
            a = as_ref[cc, h]
            qv, kv, vv, dov, v_new = q_ref[rows, sl], k_ref[rows, sl], v_ref[rows, sl], do_ref[rows, sl], vn_ref[rows, sl]
            t = yield from _chunk_terms(qv, kv, vv, gcol_ref[rows, :], grow_ref[cc], h, H)
            beta_c, e_c, gam, kb = t["beta_c"], t["e_c"], t["gam"], t["kb"]
            incl, strict, attn, lmat, edec = t["incl"], t["strict"], t["attn"], t["lmat"], t["edec"]
            kdec = kv * edec
            egl = jnp.exp(t["gl"])
            qe = qv * e_c
            ekb = kb * e_c

            t1 = _mm_nt(dov, st)
            ds_o = _mm_tn(qe, dov)
            dattn_raw = _mm_nt(dov, v_new)
            dv_new_o = _mm_tn(attn, dov)
            yield
            while (oi, h) not in state:
                yield
            ds_next = state[(oi, h)]
            dkdec = _mm_nt(v_new, ds_next)
            dv_new_s = _mm(kdec, ds_next)
            yield
            dgl = egl * jnp.sum(jnp.sum(st * ds_next, axis=1, keepdims=True), axis=0, keepdims=True)
            dk = edec * dkdec
            r = jnp.sum(dkdec * kdec, axis=1, keepdims=True)
            dgc = -r
            dgl = dgl + jnp.sum(r, axis=0, keepdims=True)
            dq = e_c * t1
            dgc = dgc + jnp.sum(t1 * qe, axis=1, keepdims=True)
            dattn = jnp.where(incl, dattn_raw, 0.0)
            dv_new = dv_new_s + dv_new_o
            dqm = dattn * gam
            z = dattn * attn
            dvb = _mm3(a, dv_new, _TN)
            dq_a = _mm(dqm, kv)
            dk_a = _mm_tn(dqm, qv)
            yield
            dq_ref[rows, sl] = dq + dq_a
            dv_ref[rows, sl] = beta_c * dvb
            ds_kb = _mm_tn(ekb, dvb)
            dekb_neg = _mm_nt(dvb, st)
            dl_neg = _mm_nt(dvb, v_new)
            yield
            state[(oi + 1, h)] = egl * ds_next + ds_o - ds_kb
            dekb = -dekb_neg
            dl = jnp.where(strict, -dl_neg, 0.0)
            dp = dl * gam
            z = z + dl * lmat
            dkb_p = _mm(dp, kv)
            dk_p = _mm_tn(dp, kb)
            dgc = dgc + jnp.sum(dekb * ekb, axis=1, keepdims=True)
            dgc = dgc + jnp.sum(z, axis=1, keepdims=True) - jnp.sum(z.T, axis=1, keepdims=True)
            dgc = dgc + jnp.where(rowi == C - 1, dgl, 0.0)
            yield
            dkb = dkb_p + e_c * dekb
            dk_ref[rows, sl] = dk + dk_a + dk_p + beta_c * dkb
            dbeta = jnp.sum(dkb * kv, axis=1, keepdims=True) + jnp.sum(dvb * vv, axis=1, keepdims=True)
            acc[cc] = acc[cc] + jnp.where(lane == h, dbeta, 0.0) + jnp.where(lane == H + h, dgc, 0.0)

        _interleave(head(oi, h) for oi in range(CPS) for h in range(H))
        for h in range(H):
            ds_ref[h] = state[(CPS, h)]
        ri = lax.broadcasted_iota(jnp.int32, (C, C), 0)
        ci = lax.broadcasted_iota(jnp.int32, (C, C), 1)
        upper = (ri <= ci).astype(F32)
        dal = jnp.zeros((1, LANES), F32)
        ddt = jnp.zeros((1, LANES), F32)
        for cc in range(CPS):
            rows = slice(cc * C, (cc + 1) * C)
            gates = gcol_ref[rows, :]
            dg_all = _mm3(upper, acc[cc])
            d_braw = acc[cc] * gates * (1.0 - gates)
            d_araw = dg_all * (-jnp.exp(al_ref[...])) * _sigmoid(ba_ref[rows, :] + dt_ref[...])
            dgate_ref[rows, :] = jnp.where(lane < H, d_braw, jnp.where(lane < 2 * H, d_araw, 0.0))
            dal = dal + jnp.sum(dg_all * gates, axis=0, keepdims=True)
            ddt = ddt + jnp.sum(d_araw, axis=0, keepdims=True)
        dpar_ref[0:1, :] += dal
        dpar_ref[1:2, :] += ddt

        @pl.when(pl.program_id(0) == NS - 1)
        def _():
            for cp in ccps:
                cp.wait()

    rev = lambda s: NS - 1 - s
    blk = lambda: pl.BlockSpec((CPS * C, AW), lambda s: (rev(s), 0))
    row = pl.BlockSpec((1, LANES), lambda s: (0, 0))
    any_spec = pl.BlockSpec(memory_space=pl.ANY)
    res = pl.pallas_call(
        body, name="delta_bwd", grid=(NS,),
        in_specs=[row, row, blk(), blk(), blk(),
                  pl.BlockSpec((CPS * C, LANES), lambda s: (rev(s), 0)),
                  pl.BlockSpec((CPS, 32, C), lambda s: (rev(s), 0, 0)),
                  pl.BlockSpec((CPS * C, LANES), lambda s: (rev(s), 0)),
                  blk(),
                  pl.BlockSpec((CPS, H, D, D), lambda s: (rev(s), 0, 0, 0)),
                  pl.BlockSpec((CPS, H, C, C), lambda s: (rev(s), 0, 0, 0)),
                  blk()] + [any_spec] * nc,
        out_specs=[blk(), blk(), blk(),
                   pl.BlockSpec((CPS * C, LANES), lambda s: (rev(s), 0)),
                   pl.BlockSpec((8, LANES), lambda s: (0, 0))] + [any_spec] * nc,
        out_shape=[S((T, AW), F32), S((T, AW), F32), S((T, AW), F32),
                   S((T, LANES), F32), S((8, LANES), F32)] + [S((3,) + a.shape[1:], a.dtype) for a in carry],
        scratch_shapes=[pltpu.VMEM((H, D, D), F32),
                        pltpu.SemaphoreType.DMA((max(nc, 1), 3)), pltpu.SemaphoreType.DMA((max(nc, 1), 3))],
        compiler_params=_cp(ARB),
    )(a_log, dt_bias, q, k, v, gcol, grow, ba, vnew, ssave, asave, d_o, *carry)
    return res[:5], res[5:]


def _ln_stats(xv):
    mu = jnp.mean(xv, axis=-1, keepdims=True)
    xc = xv - mu
    var = jnp.mean(xc * xc, axis=-1, keepdims=True)
    rstd = lax.rsqrt(var + EPS)
    return xc * rstd, rstd


def _mix_fwd(o, proj, head_norm_w, ln_w, ln_b, w_sp, bs_t, H, D, G, P):
    T = o.shape[0]
    AW, BW = H * D, G * P
    MIX = AW + BW
    nb = AW // BW if AW % BW == 0 else None
    assert nb == 1, "group widths must match the projection column blocks"
    cb = 3

    def body(o_ref, za_ref, ub_ref, vb_ref, zb_ref, hw_ref, lw_ref, lb_ref, w_ref, bs_ref, out_ref):
        hw = hw_ref[...]
        for h in range(H):
            sl = slice(h * D, (h + 1) * D)
            oh = o_ref[:, sl]
            rs = lax.rsqrt(jnp.mean(oh * oh, axis=-1, keepdims=True) + EPS)
            out_ref[:, sl] = (oh * rs * hw * _silu(za_ref[:, sl])).astype(BF16)
        xhat, _ = _ln_stats(vb_ref[...])
        vn = xhat * lw_ref[...] + lb_ref[...]
        ri = lax.broadcasted_iota(jnp.int32, (P, P), 0)
        ci = lax.broadcasted_iota(jnp.int32, (P, P), 1)
        bsv = bs_ref[...]
        for g in range(G):
            sl = slice(g * P, (g + 1) * P)
            wm = jnp.where(ri >= ci, w_ref[g], 0.0)
            s = _mm(wm, vn[:, sl]) + bsv[:, g:g + 1]
            out_ref[:, AW + g * P:AW + (g + 1) * P] = (ub_ref[:, sl] * s * _silu(zb_ref[:, sl])).astype(BF16)

    row = lambda w: pl.BlockSpec((1, w), lambda i: (0, 0))
    return pl.pallas_call(
        body, name="mix_fwd", grid=(T // P,),
        in_specs=[pl.BlockSpec((P, AW), lambda i: (i, 0)),
                  pl.BlockSpec((P, AW), lambda i: (i, cb)),
                  pl.BlockSpec((P, BW), lambda i: (i, cb + 1)),
                  pl.BlockSpec((P, BW), lambda i: (i, cb + 2)),
                  pl.BlockSpec((P, BW), lambda i: (i, cb + 3)),
                  row(D), row(BW), row(BW),
                  pl.BlockSpec((G, P, P), lambda i: (0, 0, 0)),
                  pl.BlockSpec((P, G), lambda i: (0, 0))],
        out_specs=pl.BlockSpec((P, MIX), lambda i: (i, 0)),
        out_shape=S((T, MIX), BF16),
        compiler_params=_cp(ARB),
    )(o, proj, proj, proj, proj, head_norm_w, ln_w, ln_b, w_sp, bs_t)


def _mix_bwd(d_ocat, o, proj, head_norm_w, ln_w, ln_b, w_sp, bs_t, H, D, G, P, carry):
    T = o.shape[0]
    AW, BW = H * D, G * P
    MIX = AW + BW
    cb = 3
    nc = len(carry)

    def body(dc_ref, o_ref, za_ref, ub_ref, vb_ref, zb_ref, hw_ref, lw_ref, lb_ref, w_ref, bs_ref, *rest):
        cins = rest[:nc]
        do_ref, dmain_ref, dhw_ref, dln_ref, dw_ref, dbs_ref = rest[nc:nc + 6]
        couts = rest[nc + 6:2 * nc + 6]
        dvn_ref, drest_ref, out_sems, csend, crecv = rest[2 * nc + 6:]
        i = pl.program_id(0)
        slot = lax.rem(i, 2)
        ccps = _sibling_copies(cins, couts, csend, crecv)

        def out_copy(step, s):
            return pltpu.make_async_copy(
                drest_ref.at[s], dmain_ref.at[pl.ds(step * P, P), pl.ds(cb * AW, AW + 3 * BW)], out_sems.at[s])

        @pl.when(i == 0)
        def _():
            dhw_ref[...] = jnp.zeros_like(dhw_ref)
            dln_ref[...] = jnp.zeros_like(dln_ref)
            dw_ref[...] = jnp.zeros_like(dw_ref)
            dbs_ref[...] = jnp.zeros_like(dbs_ref)
            for cp in ccps:
                cp.start()

        @pl.when(i >= 2)
        def _():
            out_copy(i - 2, slot).wait()

        hw = hw_ref[...]
        dhw = jnp.zeros((1, D), F32)
        for h in range(H):
            sl = slice(h * D, (h + 1) * D)
            oh = o_ref[:, sl]
            za = za_ref[:, sl]
            doa = dc_ref[:, sl]
            rs = lax.rsqrt(jnp.mean(oh * oh, axis=-1, keepdims=True) + EPS)
            xh = oh * rs
            d_on = doa * _silu(za)
            drest_ref[slot, :, sl] = (doa * (xh * hw) * _dsilu(za)).astype(BF16)
            dhw = dhw + jnp.sum(d_on * xh, axis=0, keepdims=True)
            dxh = d_on * hw
            do_ref[:, sl] = rs * (dxh - xh * jnp.mean(dxh * xh, axis=-1, keepdims=True))
        dhw_ref[0:1, :] += dhw

        xhat, rstd = _ln_stats(vb_ref[...])
        lw = lw_ref[...]
        vn = xhat * lw + lb_ref[...]
        ri = lax.broadcasted_iota(jnp.int32, (P, P), 0)
        ci = lax.broadcasted_iota(jnp.int32, (P, P), 1)
        lane = lax.broadcasted_iota(jnp.int32, (P, LANES), 1)
        bsv = bs_ref[...]
        dbs = jnp.zeros((P, LANES), F32)
        for g in range(G):
            sl = slice(g * P, (g + 1) * P)
            wm = jnp.where(ri >= ci, w_ref[g], 0.0)
            vng = vn[:, sl]
            s = _mm(wm, vng) + bsv[:, g:g + 1]
            dob = dc_ref[:, AW + g * P:AW + (g + 1) * P]
            ub = ub_ref[:, sl]
            zb = zb_ref[:, sl]
            szb = _silu(zb)
            drest_ref[slot, :, AW + g * P:AW + (g + 1) * P] = (dob * s * szb).astype(BF16)
            drest_ref[slot, :, AW + 2 * BW + g * P:AW + 2 * BW + (g + 1) * P] = (
                dob * ub * s * _dsilu(zb)).astype(BF16)
            ds = dob * ub * szb
            dvn_ref[:, sl] = _mm_tn(wm, ds)
            dw_ref[g] += jnp.where(ri >= ci, _mm_nt(ds, vng), 0.0)
            dbs = dbs + jnp.where(lane == g, jnp.sum(ds, axis=1, keepdims=True), 0.0)
        dbs_ref[...] += dbs
        dvn = dvn_ref[...]
        dln_ref[0:1, :] += jnp.sum(dvn * xhat, axis=0, keepdims=True)
        dln_ref[1:2, :] += jnp.sum(dvn, axis=0, keepdims=True)
        dxh = dvn * lw
        dvb = rstd * (dxh - jnp.mean(dxh, axis=-1, keepdims=True) - xhat * jnp.mean(dxh * xhat, axis=-1, keepdims=True))
        drest_ref[slot, :, AW + BW:AW + 2 * BW] = dvb.astype(BF16)

        out_copy(i, slot).start()

        @pl.when(i == nstep - 1)
        def _():
            out_copy(i, slot).wait()
            if nstep > 1:
                out_copy(i - 1, 1 - slot).wait()
            for cp in ccps:
                cp.wait()

    nstep = T // P
    row = lambda w: pl.BlockSpec((1, w), lambda i: (0, 0))
    any_spec = pl.BlockSpec(memory_space=pl.ANY)
    res = pl.pallas_call(
        body, name="mix_bwd", grid=(nstep,),
        in_specs=[pl.BlockSpec((P, MIX), lambda i: (i, 0)),
                  pl.BlockSpec((P, AW), lambda i: (i, 0)),
                  pl.BlockSpec((P, AW), lambda i: (i, cb)),
                  pl.BlockSpec((P, BW), lambda i: (i, cb + 1)),
                  pl.BlockSpec((P, BW), lambda i: (i, cb + 2)),
                  pl.BlockSpec((P, BW), lambda i: (i, cb + 3)),
                  row(D), row(BW), row(BW),
                  pl.BlockSpec((G, P, P), lambda i: (0, 0, 0)),
                  pl.BlockSpec((P, G), lambda i: (0, 0))] + [any_spec] * nc,
        out_specs=[pl.BlockSpec((P, AW), lambda i: (i, 0)),
                   any_spec,
                   pl.BlockSpec((8, D), lambda i: (0, 0)),
                   pl.BlockSpec((8, BW), lambda i: (0, 0)),
                   pl.BlockSpec((G, P, P), lambda i: (0, 0, 0)),
                   pl.BlockSpec((P, LANES), lambda i: (0, 0))] + [any_spec] * nc,
        out_shape=[S((T, AW), F32), S((T, cb * AW + AW + 3 * BW), BF16), S((8, D), F32), S((8, BW), F32),
                   S((G, P, P), F32), S((P, LANES), F32)] + [S(a.shape[:1] + a.shape[2:], a.dtype) for a in carry],
        scratch_shapes=[pltpu.VMEM((P, BW), F32), pltpu.VMEM((2, P, AW + 3 * BW), BF16),
                        pltpu.SemaphoreType.DMA((2,))] + _sibling_sems(carry),
        compiler_params=_cp(ARB),
    )(d_ocat, o, proj, proj, proj, proj, head_norm_w, ln_w, ln_b, w_sp, bs_t, *carry)
    return res[:6], res[6:]


def _out_proj_loss(ocat, w_out, x, target, fnw):
    T, MIX = ocat.shape
    DM = x.shape[1]
    tm = _tile(T, 256, 8)

    def body(oc_ref, w_ref, x_ref, t_ref, fw_ref, dh_ref, dhb_ref, doc_ref, loss_ref, gfw_ref):
        @pl.when(pl.program_id(0) == 0)
        def _():
            loss_ref[...] = jnp.zeros_like(loss_ref)
            gfw_ref[...] = jnp.zeros_like(gfw_ref)

        wv = w_ref[...]
        hh = x_ref[...] + jnp.dot(oc_ref[...].astype(MXU), wv.astype(MXU), preferred_element_type=F32)
        rs = lax.rsqrt(jnp.mean(hh * hh, axis=-1, keepdims=True) + EPS)
        hn = hh * rs
        fw = fw_ref[...]
        e = hn * fw - t_ref[...]
        row_loss = 0.5 * jnp.mean(e * e, axis=-1, keepdims=True)
        loss_ref[...] += jnp.sum(row_loss, axis=0, keepdims=True)
        dy = e * (1.0 / DM)
        gfw_ref[0:1, :] += jnp.sum(dy * hn, axis=0, keepdims=True)
        dhn = dy * fw
        dh = rs * (dhn - hn * jnp.mean(dhn * hn, axis=-1, keepdims=True))
        dh_ref[...] = dh
        dhb = dh.astype(BF16)
        dhb_ref[...] = dhb
        doc_ref[...] = _mm_nt(dhb, wv)

    return pl.pallas_call(
        body, name="out_proj_loss", grid=(T // tm,),
        in_specs=[pl.BlockSpec((tm, MIX), lambda i: (i, 0)),
                  pl.BlockSpec((MIX, DM), lambda i: (0, 0)),
                  pl.BlockSpec((tm, DM), lambda i: (i, 0)),
                  pl.BlockSpec((tm, DM), lambda i: (i, 0)),
                  pl.BlockSpec((1, DM), lambda i: (0, 0))],
        out_specs=[pl.BlockSpec((tm, DM), lambda i: (i, 0)),
                   pl.BlockSpec((tm, DM), lambda i: (i, 0)),
                   pl.BlockSpec((tm, MIX), lambda i: (i, 0)),
                   pl.BlockSpec((8, LANES), lambda i: (0, 0)),
                   pl.BlockSpec((8, DM), lambda i: (0, 0))],
        out_shape=[S((T, DM), F32), S((T, DM), BF16), S((T, MIX), F32), S((8, LANES), F32), S((8, DM), F32)],
        compiler_params=_cp(ARB),
    )(ocat, w_out, x, target, fnw)


def _grad_w(lhs, rhs, name):
    T, A = lhs.shape
    B = rhs.shape[1]
    ta = _tile(A, 512, LANES)
    tk = _tile(T, 1024, 16)
    nk = T // tk

    def body(l_ref, r_ref, out_ref, acc_ref):
        k = pl.program_id(1)
        part = _mm_tn(l_ref[...], r_ref[...])

        @pl.when(k == 0)
        def _():
            acc_ref[...] = part

        @pl.when(k > 0)
        def _():
            acc_ref[...] += part

        @pl.when(k == nk - 1)
        def _():
            out_ref[...] = acc_ref[...].astype(BF16)

    return pl.pallas_call(
        body, name=name, grid=(A // ta, nk),
        in_specs=[pl.BlockSpec((tk, ta), lambda i, k: (k, i)),
                  pl.BlockSpec((tk, B), lambda i, k: (k, 0))],
        out_specs=pl.BlockSpec((ta, B), lambda i, k: (i, 0)),
        out_shape=S((A, B), BF16),
        scratch_shapes=[pltpu.VMEM((ta, B), F32)],
        compiler_params=_cp(ARB, ARB),
    )(lhs, rhs)


def _grad_w_in(xn, dmain, dba, WD, gate_lo, gate_hi):
    T, DM = xn.shape
    NM = dmain.shape[1]
    tn = _tile(NM, 1024, LANES)
    tk = _tile(T, 2048, 16)
    nj, nk = NM // tn, T // tk
    ND = N_DEV
    tiles = [[] for _ in range(nj)]
    first_tile, last_tile = {}, {}
    for d, s0, s1, dest, c0 in _pieces(WD, gate_lo, gate_hi, ND * WD):
        if dest != "main":
            continue
        while s0 < s1:
            jj = c0 // tn
            w = min(s1 - s0, (jj + 1) * tn - c0)
            tiles[jj].append((d, s0, w, "main", c0 - jj * tn))
            first_tile.setdefault(d, jj)
            last_tile[d] = jj
            s0, c0 = s0 + w, c0 + w
    for d, s0, s1, dest, c0 in _pieces(WD, gate_lo, gate_hi, ND * WD):
        if dest == "gate":
            tiles[first_tile[d]].append((d, s0, s1 - s0, "gate", c0))
    assert sorted(first_tile) == list(range(ND)) and all(last_tile[d] <= first_tile[d + 2] for d in range(ND - 2))

    def body(xn_ref, dm_ref, dba_ref, keep_ref, recv_ref, acc_ref, gate_ref, buf_ref, lsem, ssem, rsem):
        j = pl.program_id(0)
        k = pl.program_id(1)
        px, py, pc = _position()

        @pl.when(k == 0)
        def _():
            acc_ref[...] = jnp.zeros_like(acc_ref)

        @pl.when((j == 0) & (k == 0))
        def _():
            gate_ref[...] = jnp.zeros_like(gate_ref)

        xv = xn_ref[...]
        acc_ref[...] += _mm_tn(xv, dm_ref[...])

        @pl.when(j == 0)
        def _():
            gate_ref[...] += _mm_tn(xv, dba_ref[...])

        def local(d):
            return pltpu.make_async_copy(buf_ref.at[d % 2], keep_ref.at[d // 2], lsem.at[d // 2])

        def remote(d):
            return pltpu.make_async_remote_copy(
                src_ref=buf_ref.at[d % 2], dst_ref=recv_ref.at[d // 2], send_sem=ssem.at[d // 2],
                recv_sem=rsem.at[d // 2], device_id=(px, py, 1 - pc), device_id_type=MESH)

        def leave(d, start):
            @pl.when(pc == d % 2)
            def _():
                local(d).start() if start else local(d).wait()

            @pl.when(pc != d % 2)
            def _():
                remote(d).start() if start else remote(d).wait_send()

        def emit(jj):
            shards = sorted({p[0] for p in tiles[jj]})
            for d in shards:
                if first_tile[d] == jj and d >= 2:
                    leave(d - 2, False)
                for dd, s0, w, src, c0 in tiles[jj]:
                    if dd == d:
                        ref = acc_ref if src == "main" else gate_ref
                        buf_ref[d % 2, :, s0:s0 + w] = ref[:, c0:c0 + w].astype(BF16)
                if last_tile[d] == jj:
                    leave(d, True)
            if jj == nj - 1:
                for d in (ND - 2, ND - 1):
                    leave(d, False)
                for q in range(ND // 2):
                    remote(2 * q).wait_recv()

        for jj in range(nj):
            @pl.when((j == jj) & (k == nk - 1))
            def _(jj=jj):
                emit(jj)

    any_spec = pl.BlockSpec(memory_space=pl.ANY)
    return pl.pallas_call(
        body, name="grad_w_in", grid=(nj, nk),
        in_specs=[pl.BlockSpec((tk, DM), lambda j, k: (k, 0)),
                  pl.BlockSpec((tk, tn), lambda j, k: (k, j)),
                  pl.BlockSpec((tk, LANES), lambda j, k: (k, 0))],
        out_specs=[any_spec, any_spec],
        out_shape=[S((ND // 2, DM, WD), BF16), S((ND // 2, DM, WD), BF16)],
        scratch_shapes=[pltpu.VMEM((DM, tn), F32), pltpu.VMEM((DM, LANES), F32), pltpu.VMEM((2, DM, WD), BF16),
                        pltpu.SemaphoreType.DMA((ND // 2,)), pltpu.SemaphoreType.DMA((ND // 2,)),
                        pltpu.SemaphoreType.DMA((ND // 2,))],
        compiler_params=_cp(ARB, ARB),
    )(xn, dmain, dba)


def _pair_sum_plain(a, b, name):
    K, R, C = a.shape
    tr = _tile(R, 256, 16)

    def body(a_ref, b_ref, o_ref):
        o_ref[...] = (a_ref[...].astype(F32) + b_ref[...].astype(F32)).astype(BF16)

    spec = lambda: pl.BlockSpec((1, tr, C), lambda q, i: (q, i, 0))
    return pl.pallas_call(body, name=name, grid=(K, R // tr), in_specs=[spec(), spec()], out_specs=spec(),
                          out_shape=S((K, R, C), BF16), compiler_params=_cp(ARB, ARB))(a, b)


def _dx_rows(T):
    tm = _tile(T, 512, 8)
    return tm if T // tm >= 2 else T // 2


def _dx_part(name, dmain, dba, w_main, w_ba, x, dh, norm_w, blk0, nblk, prev, hbm_in, hbm_alias, hbm_new, make_copies):
    T, NM = dmain.shape
    DM = x.shape[1]
    tm = _dx_rows(T)
    tk = _tile(NM, 1024, LANES)
    nk = NM // tk
    n_in, n_al, n_new = len(hbm_in), len(hbm_alias), len(hbm_new)
    n_prev = 0 if prev is None else 2
    last_step = nblk * nk - 1

    def body(dm_ref, dba_ref, w_ref, wba_ref, x_ref, dh_ref, nw_ref, *rest):
        r = list(rest)
        gnw_prev_ref = r.pop(0) if n_prev else None
        if n_prev:
            r.pop(0)
        in_refs = [r.pop(0) for _ in range(n_in)]
        del r[:n_al]
        gx_ref, gnw_ref = r.pop(0), r.pop(0)
        alias_refs = [r.pop(0) for _ in range(n_al)]
        new_refs = [r.pop(0) for _ in range(n_new)]
        acc_ref, send_sems, recv_sems = r
        i = pl.program_id(0)
        k = pl.program_id(1)
        step = i * nk + k
        cps = make_copies(in_refs, alias_refs, new_refs, send_sems, recv_sems)

        @pl.when(step == 0)
        def _():
            gnw_ref[...] = gnw_prev_ref[...] if n_prev else jnp.zeros_like(gnw_ref)
            for cp in cps:
                cp.start()

        @pl.when(k == 0)
        def _():
            acc_ref[...] = _mm_nt(dba_ref[...], wba_ref[...])

        acc_ref[...] += _mm_nt(dm_ref[...], w_ref[...])

        @pl.when(k == nk - 1)
        def _():
            xv = x_ref[...]
            rs = lax.rsqrt(jnp.mean(xv * xv, axis=-1, keepdims=True) + EPS)
            xh = xv * rs
            dxn = acc_ref[...]
            gnw_ref[0:1, :] += jnp.sum(dxn * xh, axis=0, keepdims=True)
            dxh = dxn * nw_ref[...]
            gx_ref[...] = dh_ref[...] + rs * (dxh - xh * jnp.mean(dxh * xh, axis=-1, keepdims=True))

        @pl.when(step == last_step)
        def _():
            for cp in cps:
                cp.wait()

    any_spec = pl.BlockSpec(memory_space=pl.ANY)
    prev_specs = [pl.BlockSpec((8, DM), lambda i, k: (0, 0)), any_spec] if n_prev else []
    prev_args = [prev[1], prev[0]] if n_prev else []
    aliases = {8: 0} if n_prev else {}
    for q in range(n_al):
        aliases[7 + n_prev + n_in + q] = 2 + q
    res = pl.pallas_call(
        body, name=name, grid=(nblk, nk),
        in_specs=[pl.BlockSpec((tm, tk), lambda i, k: (blk0 + i, k)),
                  pl.BlockSpec((tm, LANES), lambda i, k: (blk0 + i, 0)),
                  pl.BlockSpec((DM, tk), lambda i, k: (0, k)),
                  pl.BlockSpec((DM, LANES), lambda i, k: (0, 0)),
                  pl.BlockSpec((tm, DM), lambda i, k: (blk0 + i, 0)),
                  pl.BlockSpec((tm, DM), lambda i, k: (blk0 + i, 0)),
                  pl.BlockSpec((1, DM), lambda i, k: (0, 0))] + prev_specs + [any_spec] * (n_in + n_al),
        out_specs=[pl.BlockSpec((tm, DM), lambda i, k: (blk0 + i, 0)),
                   pl.BlockSpec((8, DM), lambda i, k: (0, 0))] + [any_spec] * (n_al + n_new),
        out_shape=[S((T, DM), F32), S((8, DM), F32)] + [S(a.shape, a.dtype) for a in hbm_alias] + list(hbm_new),
        scratch_shapes=[pltpu.VMEM((tm, DM), F32), pltpu.SemaphoreType.DMA((10,)), pltpu.SemaphoreType.DMA((10,))],
        input_output_aliases=aliases,
        compiler_params=_cp(ARB, ARB),
    )(dmain, dba, w_main, w_ba, x, dh, norm_w, *prev_args, *hbm_in, *hbm_alias)
    return (res[0], res[1]), res[2:2 + n_al], res[2 + n_al:]


def _remote(kk, src, dst, to, send_sems, recv_sems):
    return pltpu.make_async_remote_copy(src_ref=src, dst_ref=dst, send_sem=send_sems.at[kk], recv_sem=recv_sems.at[kk],
                                        device_id=to, device_id_type=MESH)


def _dx(dmain, dba, w_main, w_ba, x, dh, norm_w, chip_sum, small, cut):
    R, C = chip_sum.shape[1:]
    half = R // 2
    assert half % 16 == 0
    T = x.shape[0]
    ni = T // _dx_rows(T)
    cut = max(1, min(cut, ni - 1))
    upper, lower = pl.ds(0, half), pl.ds(half, half)

    def nbrs():
        px, py, pc = _position()
        return (px, py), (1 - px, py, pc), (px, 1 - py, pc)

    def phase1(ins, als, news, ss, rs):
        (px, py), xn, yn = nbrs()
        cs = ins[0]
        recv, stage = news
        bx, by, bd = cs.at[2 * (1 - px) + py], cs.at[2 * px + (1 - py)], cs.at[2 * (1 - px) + (1 - py)]
        return [_remote(0, bx.at[upper], recv.at[0].at[upper], xn, ss, rs),
                _remote(1, by.at[lower], recv.at[1].at[lower], yn, ss, rs),
                _remote(2, bd.at[upper], stage.at[0], xn, ss, rs),
                _remote(3, bd.at[lower], stage.at[1], yn, ss, rs)]

    def phase2(ins, als, news, ss, rs):
        (px, py), xn, yn = nbrs()
        comb, small_ref = ins
        recv, gath = als[0], news[0]
        me, small_cps = _broadcast_copies([small_ref], [gath], _Sem2(ss, 2), _Sem2(rs, 2))
        return ([_remote(0, comb.at[0], recv.at[1].at[upper], yn, ss, rs),
                 _remote(1, comb.at[1], recv.at[0].at[lower], xn, ss, rs)] + small_cps
                + [pltpu.make_async_copy(small_ref, gath.at[me], ss.at[9])])

    (gx, gnw), _, (recv, stage) = _dx_part(
        "dx_a", dmain, dba, w_main, w_ba, x, dh, norm_w, 0, cut, None, [chip_sum], [],
        [S((2, R, C), chip_sum.dtype), S((2, half, C), chip_sum.dtype)], phase1)
    comb = _relay_add(chip_sum, stage)
    (gx, gnw), (recv,), (gath,) = _dx_part(
        "dx_b", dmain, dba, w_main, w_ba, x, dh, norm_w, cut, ni - cut, (gx, gnw), [comb, small], [recv],
        [S((N_DEV,) + small.shape, F32)], phase2)
    return gx, gnw, gath, recv


class _Sem2:
    def __init__(self, sems, lo):
        self.sems, self.lo = sems, lo

    @property
    def at(self):
        outer = self

        class _At:
            def __getitem__(self, idx):
                a, k = idx
                return outer.sems.at[outer.lo + k]
        return _At()


def _relay_add(chip_sum, stage):
    _, R, C = chip_sum.shape
    half = R // 2
    tr = _tile(half, 256, 16)
    nt = half // tr
    px, py, _ = _position()
    idx = jnp.stack([2 * px + (1 - py), 2 * (1 - px) + py]).astype(jnp.int32)

    def body(idx_ref, p_ref, s_ref, o_ref):
        del idx_ref
        o_ref[0] = (p_ref[0].astype(F32) + s_ref[0].astype(F32)).astype(BF16)

    return pl.pallas_call(
        body, name="relay_add",
        grid_spec=pltpu.PrefetchScalarGridSpec(
            num_scalar_prefetch=1, grid=(2, nt),
            in_specs=[pl.BlockSpec((1, tr, C), lambda s, i, idx_ref: (idx_ref[s], s * nt + i, 0)),
                      pl.BlockSpec((1, tr, C), lambda s, i, idx_ref: (s, i, 0))],
            out_specs=pl.BlockSpec((1, tr, C), lambda s, i, idx_ref: (s, i, 0))),
        out_shape=S((2, half, C), BF16), compiler_params=_cp(ARB, ARB),
    )(idx, chip_sum, stage)


def _sum_slots(gath):
    _, R, C = gath.shape
    tr = _tile(R, 512, 8)

    def body(g_ref, o_ref):
        tot = g_ref[0]
        for d in range(1, N_DEV):
            tot = tot + g_ref[d]
        o_ref[...] = tot

    return pl.pallas_call(
        body, name="sum_slots", grid=(R // tr,),
        in_specs=[pl.BlockSpec((N_DEV, tr, C), lambda i: (0, i, 0))],
        out_specs=pl.BlockSpec((tr, C), lambda i: (i, 0)),
        out_shape=S((R, C), F32), compiler_params=_cp(ARB),
    )(gath)


def _prep_a_bwd(dq, dk, dv, c, proj, conv_w, dmain, H, D):
    T = c.shape[0]
    AW = H * D
    C3 = 3 * AW
    tb = _tile(T, 256, 8)
    nblk = T // tb
    r8 = tb // 8
    scale = float(D) ** -0.5

    def body(dq_ref, dk_ref, dv_ref, c_ref, dqn_ref, dkn_ref, dvn_ref, cn_ref, x_ref, halo_ref, cw_ref, dmain_in_ref,
             dx_ref, gcw_ref, dc_ref):
        del dmain_in_ref
        i = pl.program_id(0)

        @pl.when(i == 0)
        def _():
            gcw_ref[...] = jnp.zeros_like(gcw_ref)

        def pointwise(rows, dq_r, dk_r, dv_r, c_r, keep):
            for h in range(H):
                for part, d_r, sc in ((0, dq_r, scale), (1, dk_r, 1.0)):
                    sl = slice(part * AW + h * D, part * AW + (h + 1) * D)
                    cv = c_r[:, sl]
                    raw = _silu(cv)
                    rs = lax.rsqrt(jnp.sum(raw * raw, axis=-1, keepdims=True) + EPS)
                    nrm = raw * rs
                    dn = d_r[:, h * D:(h + 1) * D] * sc
                    draw = rs * (dn - nrm * jnp.sum(dn * nrm, axis=-1, keepdims=True))
                    dc_ref[rows, sl] = draw * _dsilu(cv) * keep
            dc_ref[rows, 2 * AW:] = dv_r[...] * _dsilu(c_r[:, 2 * AW:]) * keep

        pointwise(slice(0, tb), dq_ref, dk_ref, dv_ref, c_ref, 1.0)
        pointwise(slice(tb, tb + 8), dqn_ref, dkn_ref, dvn_ref, cn_ref, (i < nblk - 1).astype(F32))

        cw = cw_ref[...]
        dcv = dc_ref[0:tb, :]
        dx = cw[3:4, :] * dcv
        for j in range(3):
            dx = dx + cw[j:j + 1, :] * dc_ref[3 - j:3 - j + tb, :]
        dx_ref[...] = dx.astype(BF16)
        halo = halo_ref[...] * (i > 0).astype(F32)
        xp = jnp.concatenate([halo, x_ref[...]], axis=0)
        for j in range(4):
            gcw_ref[j:j + 1, :] += jnp.sum(dcv * xp[5 + j:5 + j + tb], axis=0, keepdims=True)

    nxt = lambda i: (jnp.minimum((i + 1) * r8, T // 8 - 1), 0)
    return pl.pallas_call(
        body, name="prep_a_bwd", grid=(nblk,),
        in_specs=[pl.BlockSpec((tb, AW), lambda i: (i, 0)),
                  pl.BlockSpec((tb, AW), lambda i: (i, 0)),
                  pl.BlockSpec((tb, AW), lambda i: (i, 0)),
                  pl.BlockSpec((tb, C3), lambda i: (i, 0)),
                  pl.BlockSpec((8, AW), nxt), pl.BlockSpec((8, AW), nxt), pl.BlockSpec((8, AW), nxt),
                  pl.BlockSpec((8, C3), nxt),
                  pl.BlockSpec((tb, C3), lambda i: (i, 0)),
                  pl.BlockSpec((8, C3), lambda i: (jnp.maximum(i * r8 - 1, 0), 0)),
                  pl.BlockSpec((4, C3), lambda i: (0, 0)),
                  pl.BlockSpec(memory_space=pl.ANY)],
        out_specs=[pl.BlockSpec((tb, C3), lambda i: (i, 0)),
                   pl.BlockSpec((8, C3), lambda i: (0, 0))],
        out_shape=[S(dmain.shape, dmain.dtype), S((8, C3), F32)],
        scratch_shapes=[pltpu.VMEM((tb + 8, C3), F32)],
        input_output_aliases={11: 0},
        compiler_params=_cp(ARB),
    )(dq, dk, dv, c, dq, dk, dv, c, proj, proj, conv_w, dmain)


def _adam_math(w, g, m, v):
    m2 = ADAM_B1 * m + (1.0 - ADAM_B1) * g
    v2 = ADAM_B2 * v + (1.0 - ADAM_B2) * (g * g)
    m_hat = m2 / (1.0 - ADAM_B1 ** ADAM_STEP)
    v_hat = v2 / (1.0 - ADAM_B2 ** ADAM_STEP)
    delta = -ADAM_LR * (m_hat / (jnp.sqrt(v_hat) + ADAM_EPS) + ADAM_WD * w)
    return delta, m2, v2


def _pair_sum(blocks, recv, core, name):
    K, _, R, C = blocks.shape
    tr = _tile(R, 256, 16)

    def body(core_ref, a_ref, b_ref, o_ref):
        del core_ref
        o_ref[0] = (a_ref[0, 0].astype(F32) + b_ref[0].astype(F32)).astype(BF16)

    spec = lambda: pl.BlockSpec((1, tr, C), lambda k, i, core_ref: (k, i, 0))
    return pl.pallas_call(
        body, name=name,
        grid_spec=pltpu.PrefetchScalarGridSpec(
            num_scalar_prefetch=1, grid=(K, R // tr),
            in_specs=[pl.BlockSpec((1, 1, tr, C), lambda k, i, core_ref: (k, core_ref[0], i, 0)), spec()],
            out_specs=spec()),
        out_shape=S((K, R, C), BF16), compiler_params=_cp(ARB, ARB),
    )(core, blocks, recv)


def _sum_adam(chip_sums, recv, w, m, v, chip, name, transposed=False):
    R, C = chip_sums.shape[1:]
    NR = recv.shape[0]
    tr = _tile(R, 256, 16)

    def body(chip_ref, own_ref, r_ref, w_ref, m_ref, v_ref, g_ref, d_ref, m2_ref, v2_ref):
        del chip_ref
        g = own_ref[0].astype(F32)
        for j in range(NR):
            g = g + r_ref[j].astype(F32)
        if transposed:
            g = g.T
        g_ref[...] = g
        d_ref[...], m2_ref[...], v2_ref[...] = _adam_math(w_ref[...], g, m_ref[...], v_ref[...])

    if transposed:
        spec = lambda: pl.BlockSpec((C, tr), lambda i, chip_ref: (0, i))
        shape = (C, R)
    else:
        spec = lambda: pl.BlockSpec((tr, C), lambda i, chip_ref: (i, 0))
        shape = (R, C)
    assert w.shape == shape
    return pl.pallas_call(
        body, name=name,
        grid_spec=pltpu.PrefetchScalarGridSpec(
            num_scalar_prefetch=1, grid=(R // tr,),
            in_specs=[pl.BlockSpec((1, tr, C), lambda i, chip_ref: (chip_ref[0], i, 0)),
                      pl.BlockSpec((NR, tr, C), lambda i, chip_ref: (0, i, 0)), spec(), spec(), spec()],
            out_specs=[spec(), spec(), spec(), spec()]),
        out_shape=[S(shape, F32)] * 4, compiler_params=_cp(ARB),
    )(chip, chip_sums, recv, w, m, v)


def _adam_small(w, g, m, v):
    R, C = w.shape
    tr = _tile(R, 512, 8)

    def body(w_ref, g_ref, m_ref, v_ref, d_ref, m2_ref, v2_ref):
        d_ref[...], m2_ref[...], v2_ref[...] = _adam_math(w_ref[...], g_ref[...], m_ref[...], v_ref[...])

    spec = lambda: pl.BlockSpec((tr, C), lambda i: (i, 0))
    return pl.pallas_call(
        body, name="adam_small", grid=(R // tr,), in_specs=[spec()] * 4, out_specs=[spec()] * 3,
        out_shape=[S((R, C), F32)] * 3, compiler_params=_cp(ARB),
    )(w, g, m, v)


def _position():
    return lax.axis_index("x"), lax.axis_index("y"), lax.axis_index("c")


def _all_gather_weights(arr):
    R = arr.shape[0]
    half = R // 2
    assert half % 16 == 0

    def body(in_ref, out_ref, send_sems, recv_sems, local_sem):
        x, y, c = _position()
        me, sibling = (x, y, c), (x, y, 1 - c)
        xn, yn, diag = (1 - x, y), (x, 1 - y), (1 - x, 1 - y)
        upper, lower = pl.ds(0, half), pl.ds(half, half)

        def slot(p, rows=None):
            ref = out_ref.at[4 * p[0] + 2 * p[1] + p[2]]
            return ref if rows is None else ref.at[rows]

        def copy(kk, block, to, rows=None, src=None):
            return pltpu.make_async_remote_copy(
                src_ref=slot(block, rows) if src is None else src, dst_ref=slot(block, rows),
                send_sem=send_sems.at[kk], recv_sem=recv_sems.at[kk], device_id=to, device_id_type=MESH)

        mine = pltpu.make_async_copy(in_ref, slot(me), local_sem)
        mine.start()
        sent = [copy(0, me, sibling, src=in_ref), copy(1, me, (*xn, c), src=in_ref), copy(2, me, (*yn, c), src=in_ref)]
        for cp in sent:
            cp.start()

        def then(cps):
            for cp in cps:
                cp.start()
            sent.extend(cps)

        copy(1, (*xn, c), me).wait_recv()
        then([copy(5, (*xn, c), (*yn, c), rows=upper), copy(3, (*xn, c), sibling)])
        copy(2, (*yn, c), me).wait_recv()
        then([copy(6, (*yn, c), (*xn, c), rows=lower), copy(4, (*yn, c), sibling)])
        copy(5, (*diag, c), me, rows=upper).wait_recv()
        then([copy(7, (*diag, c), sibling, rows=upper)])
        copy(6, (*diag, c), me, rows=lower).wait_recv()
        then([copy(8, (*diag, c), sibling, rows=lower)])
        copy(0, sibling, me).wait_recv()
        copy(3, (*xn, 1 - c), me).wait_recv()
        copy(4, (*yn, 1 - c), me).wait_recv()
        copy(7, (*diag, 1 - c), me, rows=upper).wait_recv()
        copy(8, (*diag, 1 - c), me, rows=lower).wait_recv()
        for cp in sent:
            cp.wait_send()
        mine.wait()

    any_spec = pl.BlockSpec(memory_space=pl.ANY)
    return pl.pallas_call(
        body, name="all_gather_weights", in_specs=[any_spec], out_specs=any_spec,
        out_shape=S((N_DEV,) + arr.shape, arr.dtype),
        scratch_shapes=[pltpu.SemaphoreType.DMA((9,)), pltpu.SemaphoreType.DMA((9,)), pltpu.SemaphoreType.DMA],
    )(arr)


def _sibling_copies(ins, outs, send_sems, recv_sems):
    x, y, c = _position()
    return [pltpu.make_async_remote_copy(src_ref=ins[a].at[k, 1 - c], dst_ref=outs[a].at[k],
                                         send_sem=send_sems.at[a, k], recv_sem=recv_sems.at[a, k],
                                         device_id=(x, y, 1 - c), device_id_type=MESH)
            for a in range(len(ins)) for k in range(ins[a].shape[0])]


def _sibling_sems(arrs):
    shape = (max(len(arrs), 1), arrs[0].shape[0] if arrs else 1)
    return [pltpu.SemaphoreType.DMA(shape), pltpu.SemaphoreType.DMA(shape)]


def _chip_exchange_copies(ins, outs, send_sems, recv_sems):
    x, y, c = _position()
    chips = [(1 - x, y), (x, 1 - y), (1 - x, 1 - y)]
    return [pltpu.make_async_remote_copy(
        src_ref=ins[a].at[2 * qx + qy], dst_ref=outs[a].at[j], send_sem=send_sems.at[a, j],
        recv_sem=recv_sems.at[a, j], device_id=(qx, qy, c), device_id_type=MESH)
        for a in range(len(ins)) for j, (qx, qy) in enumerate(chips)]


def _broadcast_copies(srcs, dsts, send_sems, recv_sems):
    x, y, c = _position()
    me = 4 * x + 2 * y + c
    cps = []
    for a in range(len(srcs)):
        for k in range(1, N_DEV):
            peer = (1 - x if k & 4 else x, 1 - y if k & 2 else y, 1 - c if k & 1 else c)
            cps.append(pltpu.make_async_remote_copy(
                src_ref=srcs[a], dst_ref=dsts[a].at[me], send_sem=send_sems.at[a, k - 1],
                recv_sem=recv_sems.at[a, k - 1], device_id=peer, device_id_type=MESH))
    return me, cps


def _all_reduce_small(part):
    R, C = part.shape

    def body(p_ref, out_ref, gath_ref, send_sems, recv_sems):
        me, cps = _broadcast_copies([p_ref], [gath_ref], send_sems, recv_sems)
        gath_ref[me] = p_ref[...]
        for cp in cps:
            cp.start()
        for cp in cps:
            cp.wait()
        acc = gath_ref[0]
        for d in range(1, N_DEV):
            acc = acc + gath_ref[d]
        out_ref[...] = acc

    vm = pl.BlockSpec(memory_space=pltpu.VMEM)
    return pl.pallas_call(
        body, name="all_reduce_small", in_specs=[vm], out_specs=vm, out_shape=S((R, C), F32),
        scratch_shapes=[pltpu.VMEM((N_DEV, R, C), F32), pltpu.SemaphoreType.DMA((1, N_DEV - 1)),
                        pltpu.SemaphoreType.DMA((1, N_DEV - 1))],
    )(part)


def _pack(parts):
    rows = []
    for p in parts:
        f = p.reshape(-1).astype(F32)
        pad = (-f.shape[0]) % (8 * LANES)
        rows.append(jnp.pad(f, (0, pad)).reshape(-1, LANES))
    return jnp.concatenate(rows, axis=0)


def _unpack(buf, shapes):
    out, r = [], 0
    for shp in shapes:
        n = 1
        for s in shp:
            n *= s
        nr = -(-n // (8 * LANES)) * 8
        out.append(buf[r:r + nr].reshape(-1)[:n].reshape(shp))
        r += nr
    return out


def kernel(x, norm_w, w_in, conv_w, a_log, dt_bias, head_norm_w, sgu_ln_w, sgu_ln_b, w_spatial, b_spatial, w_out, final_norm_w, loss_target, m_norm_w, m_w_in, m_conv_w, m_a_log, m_dt_bias, m_head_norm_w, m_sgu_ln_w, m_sgu_ln_b, m_w_spatial, m_b_spatial, m_w_out, m_final_norm_w, v_norm_w, v_w_in, v_conv_w, v_a_log, v_dt_bias, v_head_norm_w, v_sgu_ln_w, v_sgu_ln_b, v_w_spatial, v_b_spatial, v_w_out, v_final_norm_w):
    T, DM = x.shape[1], x.shape[2]
    H, D = a_log.shape[1], head_norm_w.shape[1]
    G, P = w_spatial.shape[1], w_spatial.shape[2]
    AW, BW = H * D, G * P
    MIX = AW + BW
    WD = w_in.shape[2]
    IN = N_DEV * WD
    RO = w_out.shape[1]
    CW = conv_w.shape[2]
    sizes = (3 * AW, AW, H, H, BW, BW, BW)
    assert sum(sizes) == IN and 2 * H <= LANES and 3 * H <= 32 and N_DEV * RO == MIX and N_DEV * CW == 3 * AW
    offs = [0]
    for s in sizes:
        offs.append(offs[-1] + s)
    px, py, pc = _position()
    dev = 4 * px + 2 * py + pc
    chip = 2 * px + py

    x2, tgt = x[0], loss_target[0]

    g_win = _all_gather_weights(_cast_bf16_t(w_in[0].T, "cast_w_in"))
    w_main, w_ba = _relayout_w(g_win, offs[2], offs[4])
    alog_row = jnp.pad(a_log, ((0, 0), (H, LANES - 2 * H)))
    dtb_row = jnp.pad(dt_bias, ((0, 0), (H, LANES - 2 * H)))
    bs_t = b_spatial[0].T

    xn = _rms_xn(x2, norm_w)
    proj, ba, (g_wout, g_conv) = _in_proj(xn, w_main, w_ba, [_cast_bf16(w_out[0], "cast_w_out"), conv_w[0]])
    w_out_full = g_wout.reshape(MIX, DM)
    conv_full = g_conv.transpose(1, 0, 2).reshape(4, 3 * AW)
    q, k, v, c, gcol, grow = _prep_a_fwd(proj, ba, conv_full, alog_row, dtb_row, H, D)
    o, vnew, ssave, asave = _delta_fwd(q, k, v, gcol, grow, H, D)
    ocat = _mix_fwd(o, proj, head_norm_w, sgu_ln_w, sgu_ln_b, w_spatial[0], bs_t, H, D, G, P)
    dh, dh_bf, d_ocat, loss_acc, g_fnw = _out_proj_loss(ocat, w_out_full, x2, tgt, final_norm_w.reshape(1, DM))
    loss = lax.psum(loss_acc[0, 0], AXES)

    core_idx = jnp.reshape(pc, (1,)).astype(jnp.int32)
    chip_idx = jnp.reshape(chip, (1,)).astype(jnp.int32)
    g_wout_blocks = _grad_w(ocat, dh_bf, "grad_w_out").reshape(4, 2, RO, DM)
    (d_o, dmain, g_hnw, g_ln, g_wsp, g_bs_t), (sib_wout,) = _mix_bwd(
        d_ocat, o, proj, head_norm_w, sgu_ln_w, sgu_ln_b, w_spatial[0], bs_t, H, D, G, P, [g_wout_blocks])
    chip_wout = _pair_sum(g_wout_blocks, sib_wout, core_idx, "pair_sum_w_out")
    (dq, dk, dv, dgate, dpar), (recv_wout,) = _delta_bwd(
        q, k, v, gcol, grow, ba, vnew, ssave, asave, d_o, alog_row, dtb_row, H, D, [chip_wout])
    dmain, g_conv_part = _prep_a_bwd(dq, dk, dv, c, proj, conv_full, dmain, H, D)
    dba = dgate.astype(BF16)
    keep_win, sib_win = _grad_w_in(xn, dmain, dba, WD, offs[2], offs[4])
    chip_win = _pair_sum_plain(keep_win, sib_win, "pair_sum_w_in")
    small_shapes = [a_log.shape, dt_bias.shape, head_norm_w.shape, sgu_ln_w.shape, sgu_ln_b.shape,
                    w_spatial.shape, b_spatial.shape, final_norm_w.shape]
    parts = [dpar[0, H:2 * H], dpar[1, H:2 * H], g_hnw[0], g_ln[0], g_ln[1], g_wsp, g_bs_t[:, :G].T, g_fnw[0],
             g_conv_part[:4]]
    grad_x, g_nw, small_gath, recv_win = _dx(dmain, dba, w_main, w_ba, x2, dh, norm_w, chip_win, _pack(parts), 4)
    red = _sum_slots(small_gath)
    grad_w_in, delta_w_in, new_m_w_in, new_v_w_in = _sum_adam(
        chip_win, recv_win, w_in[0].T, m_w_in[0].T, v_w_in[0].T, chip_idx, "sum_adam_w_in", transposed=True)
    grad_w_out, delta_w_out, new_m_w_out, new_v_w_out = _sum_adam(
        chip_wout, recv_wout, w_out[0], m_w_out[0], v_w_out[0], chip_idx, "sum_adam_w_out")
    red_nw = _all_reduce_small(_pack([g_nw[0]]))
    grads_small = _unpack(red_nw, [norm_w.shape]) + _unpack(red, small_shapes + [(4, 3 * AW)])
    g_conv_full = grads_small.pop()
    grad_conv = lax.dynamic_slice_in_dim(g_conv_full, dev * CW, CW, axis=1)[None]
    small_w = [norm_w, a_log, dt_bias, head_norm_w, sgu_ln_w, sgu_ln_b, w_spatial, b_spatial, final_norm_w, conv_w]
    small_m = [m_norm_w, m_a_log, m_dt_bias, m_head_norm_w, m_sgu_ln_w, m_sgu_ln_b, m_w_spatial, m_b_spatial,
               m_final_norm_w, m_conv_w]
    small_v = [v_norm_w, v_a_log, v_dt_bias, v_head_norm_w, v_sgu_ln_w, v_sgu_ln_b, v_w_spatial, v_b_spatial,
               v_final_norm_w, v_conv_w]
    small_g = grads_small + [grad_conv]
    shapes10 = [w.shape for w in small_w]
    d_p, m_p, v_p = _adam_small(_pack(small_w), _pack(small_g), _pack(small_m), _pack(small_v))
    d_s, m_s, v_s = _unpack(d_p, shapes10), _unpack(m_p, shapes10), _unpack(v_p, shapes10)

    def order(small, win, wout):
        return [small[0], win.T[None], small[9], small[1], small[2], small[3], small[4], small[5], small[6], small[7],
                wout[None], small[8]]

    grads = order(small_g, grad_w_in, grad_w_out)
    deltas = order(d_s, delta_w_in, delta_w_out)
    new_m = order(m_s, new_m_w_in, new_m_w_out)
    new_v = order(v_s, new_v_w_in, new_v_w_out)
    return (loss, grad_x[None], *grads, *deltas, *new_m, *new_v)
```

```python
import jax
import jax.numpy as jnp
from jax import lax
from jax.experimental import pallas as pl
from jax.experimental.pallas import tpu as pltpu

F32 = jnp.float32
BF16 = jnp.bfloat16
MXU = jnp.bfloat16
HI = lax.Precision.HIGHEST
EPS = 1e-6
CHUNK_A = 64
LANES = 128
MESH = pl.DeviceIdType.MESH
N_DEV = 8

ADAM_LR = 0.001
ADAM_B1 = 0.9
ADAM_B2 = 0.999
ADAM_EPS = 1e-08
ADAM_WD = 0.01
ADAM_STEP = 10

S = jax.ShapeDtypeStruct
ARB = "arbitrary"


def _cp(*sem, vmem_mib=56):
    return pltpu.CompilerParams(dimension_semantics=tuple(sem), vmem_limit_bytes=vmem_mib * 1024 * 1024)


def _tile(n, cap, mult):
    best = None
    t = mult
    while t <= min(n, cap):
        if n % t == 0:
            best = t
        t += mult
    return best if best is not None else n


def _mm(a, b):
    return jnp.dot(a.astype(MXU), b.astype(MXU), preferred_element_type=F32)


def _mm_nt(a, b):
    return lax.dot_general(a.astype(MXU), b.astype(MXU), (((1,), (1,)), ((), ())), preferred_element_type=F32)


def _mm_tn(a, b):
    return lax.dot_general(a.astype(MXU), b.astype(MXU), (((0,), (0,)), ((), ())), preferred_element_type=F32)


def _mmh(a, b):
    return jnp.dot(a, b, precision=HI, preferred_element_type=F32)


def _sigmoid(x):
    return 1.0 / (1.0 + jnp.exp(-x))


def _silu(x):
    return x * _sigmoid(x)


def _dsilu(x):
    s = _sigmoid(x)
    return s * (1.0 + x * (1.0 - s))


def _softplus(x):
    return jnp.maximum(x, 0.0) + jnp.log(1.0 + jnp.exp(-jnp.abs(x)))


def _pieces(wd, gate_lo, gate_hi, total):
    out = []
    for d in range(N_DEV):
        lo, hi = d * wd, (d + 1) * wd
        for dest, a, b, shift in (("main", 0, gate_lo, 0), ("gate", gate_lo, gate_hi, -gate_lo),
                                  ("main", gate_hi, total, gate_lo - gate_hi)):
            s0, s1 = max(lo, a), min(hi, b)
            if s0 < s1:
                out.append((d, s0 - lo, s1 - lo, dest, s0 + shift))
    return out


def _cast_bf16(a, name):
    R, C = a.shape
    tr = _tile(R, 256, 16)

    def body(a_ref, o_ref):
        o_ref[...] = a_ref[...].astype(BF16)

    spec = pl.BlockSpec((tr, C), lambda i: (i, 0))
    return pl.pallas_call(body, name=name, grid=(R // tr,), in_specs=[spec], out_specs=spec,
                          out_shape=S((R, C), BF16), compiler_params=_cp(ARB))(a)


def _cast_bf16_t(a_t, name):
    C, R = a_t.shape
    tr = _tile(R, 256, LANES)

    def body(a_ref, o_ref):
        o_ref[...] = a_ref[...].T.astype(BF16)

    return pl.pallas_call(body, name=name, grid=(R // tr,), in_specs=[pl.BlockSpec((C, tr), lambda i: (0, i))],
                          out_specs=pl.BlockSpec((tr, C), lambda i: (i, 0)),
                          out_shape=S((R, C), BF16), compiler_params=_cp(ARB))(a_t)


def _relayout_w(g_win, gate_lo, gate_hi):
    _, DM, WD = g_win.shape
    total = N_DEV * WD
    NM = total - (gate_hi - gate_lo)
    tr = _tile(DM, 256, 16)
    plan = _pieces(WD, gate_lo, gate_hi, total)

    def body(g_ref, main_ref, gate_ref):
        gate_ref[...] = jnp.zeros_like(gate_ref)
        for d, s0, s1, dest, c0 in plan:
            dst = main_ref if dest == "main" else gate_ref
            dst[:, c0:c0 + (s1 - s0)] = g_ref[d, :, s0:s1]

    return pl.pallas_call(
        body, name="relayout_w", grid=(DM // tr,),
        in_specs=[pl.BlockSpec((N_DEV, tr, WD), lambda i: (0, i, 0))],
        out_specs=[pl.BlockSpec((tr, NM), lambda i: (i, 0)), pl.BlockSpec((tr, LANES), lambda i: (i, 0))],
        out_shape=[S((DM, NM), g_win.dtype), S((DM, LANES), g_win.dtype)],
        compiler_params=_cp(ARB),
    )(g_win)


def _rms_xn(x, norm_w):
    T, DM = x.shape
    tm = _tile(T, 512, 16)

    def body(x_ref, nw_ref, o_ref):
        xv = x_ref[...]
        r = lax.rsqrt(jnp.mean(xv * xv, axis=-1, keepdims=True) + EPS)
        o_ref[...] = (xv * r * nw_ref[...]).astype(BF16)

    return pl.pallas_call(
        body, name="rms_xn", grid=(T // tm,),
        in_specs=[pl.BlockSpec((tm, DM), lambda i: (i, 0)), pl.BlockSpec((1, DM), lambda i: (0, 0))],
        out_specs=pl.BlockSpec((tm, DM), lambda i: (i, 0)),
        out_shape=S((T, DM), BF16), compiler_params=_cp(ARB),
    )(x, norm_w)


def _in_proj(xn, w_main, w_ba, shards):
    T, DM = xn.shape
    NM = w_main.shape[1]
    tm = _tile(T, 2048, 16)
    tn = _tile(NM, 1024, LANES)
    ni, nj = T // tm, NM // tn
    ns = len(shards)

    def body(xn_ref, w_ref, wba_ref, *rest):
        srcs = rest[:ns]
        proj_ref, ba_ref = rest[ns:ns + 2]
        gath = rest[ns + 2:2 * ns + 2]
        send_sems, recv_sems, local_sems = rest[2 * ns + 2:]
        i = pl.program_id(0)
        j = pl.program_id(1)
        me, cps = _broadcast_copies(srcs, gath, send_sems, recv_sems)
        cps = cps + [pltpu.make_async_copy(srcs[a], gath[a].at[me], local_sems.at[a]) for a in range(ns)]

        @pl.when((i == 0) & (j == 0))
        def _():
            for cp in cps:
                cp.start()

        @pl.when(j == 0)
        def _():
            ba_ref[...] = jnp.dot(xn_ref[...].astype(MXU), wba_ref[...].astype(MXU), preferred_element_type=F32)

        proj_ref[...] = jnp.dot(xn_ref[...].astype(MXU), w_ref[...].astype(MXU), preferred_element_type=F32)

        @pl.when((i == ni - 1) & (j == nj - 1))
        def _():
            for cp in cps:
                cp.wait()

    any_spec = pl.BlockSpec(memory_space=pl.ANY)
    res = pl.pallas_call(
        body, name="in_proj", grid=(ni, nj),
        in_specs=[pl.BlockSpec((tm, DM), lambda i, j: (i, 0)),
                  pl.BlockSpec((DM, tn), lambda i, j: (0, j)),
                  pl.BlockSpec((DM, LANES), lambda i, j: (0, 0))] + [any_spec] * ns,
        out_specs=[pl.BlockSpec((tm, tn), lambda i, j: (i, j)),
                   pl.BlockSpec((tm, LANES), lambda i, j: (i, 0))] + [any_spec] * ns,
        out_shape=[S((T, NM), F32), S((T, LANES), F32)] + [S((N_DEV,) + a.shape, a.dtype) for a in shards],
        scratch_shapes=[pltpu.SemaphoreType.DMA((ns, N_DEV - 1)), pltpu.SemaphoreType.DMA((ns, N_DEV - 1)),
                        pltpu.SemaphoreType.DMA((ns,))],
        compiler_params=_cp(ARB, ARB, vmem_mib=58),
    )(xn, w_main, w_ba, *shards)
    return res[0], res[1], res[2:]


def _prep_a_fwd(proj, ba, conv_w, alog_row, dtb_row, H, D):
    T = proj.shape[0]
    AW = H * D
    C3 = 3 * AW
    tb = _tile(T, 256, CHUNK_A)
    nch = tb // CHUNK_A
    nblk = T // tb
    scale = float(D) ** -0.5

    def body(x_ref, halo_ref, ba_ref, cw_ref, al_ref, dt_ref, q_ref, k_ref, v_ref, c_ref, gcol_ref, grow_ref):
        i = pl.program_id(0)
        xv = x_ref[...]
        halo = halo_ref[...] * (i > 0).astype(F32)
        xp = jnp.concatenate([halo, xv], axis=0)
        cw = cw_ref[...]
        c = cw[0:1, :] * xp[5:5 + tb]
        for j in range(1, 4):
            c = c + cw[j:j + 1, :] * xp[5 + j:5 + j + tb]
        c_ref[...] = c
        a = _silu(c)
        for h in range(H):
            qh = a[:, h * D:(h + 1) * D]
            kh = a[:, AW + h * D:AW + (h + 1) * D]
            qr = lax.rsqrt(jnp.sum(qh * qh, axis=-1, keepdims=True) + EPS)
            kr = lax.rsqrt(jnp.sum(kh * kh, axis=-1, keepdims=True) + EPS)
            q_ref[:, h * D:(h + 1) * D] = qh * (qr * scale)
            k_ref[:, h * D:(h + 1) * D] = kh * kr
        v_ref[...] = a[:, 2 * AW:]

        bav = ba_ref[...]
        lane = lax.broadcasted_iota(jnp.int32, (tb, LANES), 1)
        beta = _sigmoid(bav)
        g = -jnp.exp(al_ref[...]) * _softplus(bav + dt_ref[...])
        gates = jnp.where(lane < H, beta, jnp.where(lane < 2 * H, g, 0.0))
        ri = lax.broadcasted_iota(jnp.int32, (CHUNK_A, CHUNK_A), 0)
        ci = lax.broadcasted_iota(jnp.int32, (CHUNK_A, CHUNK_A), 1)
        tri = (ri >= ci).astype(F32)
        lane_c = lax.broadcasted_iota(jnp.int32, (CHUNK_A, LANES), 1)
        for cc in range(nch):
            gch = gates[cc * CHUNK_A:(cc + 1) * CHUNK_A]
            gc = pltpu.roll(_mmh(tri, gch), H, 1)
            full = jnp.where(lane_c < 2 * H, gch, jnp.where(lane_c < 3 * H, gc, 0.0))
            gcol_ref[cc * CHUNK_A:(cc + 1) * CHUNK_A, :] = full
            grow_ref[cc] = full.T[0:32, :]

    return pl.pallas_call(
        body, name="prep_a_fwd", grid=(nblk,),
        in_specs=[pl.BlockSpec((tb, C3), lambda i: (i, 0)),
                  pl.BlockSpec((8, C3), lambda i: (jnp.maximum(i * (tb // 8) - 1, 0), 0)),
                  pl.BlockSpec((tb, LANES), lambda i: (i, 0)),
                  pl.BlockSpec((4, C3), lambda i: (0, 0)),
                  pl.BlockSpec((1, LANES), lambda i: (0, 0)),
                  pl.BlockSpec((1, LANES), lambda i: (0, 0))],
        out_specs=[pl.BlockSpec((tb, AW), lambda i: (i, 0)),
                   pl.BlockSpec((tb, AW), lambda i: (i, 0)),
                   pl.BlockSpec((tb, AW), lambda i: (i, 0)),
                   pl.BlockSpec((tb, C3), lambda i: (i, 0)),
                   pl.BlockSpec((tb, LANES), lambda i: (i, 0)),
                   pl.BlockSpec((nch, 32, CHUNK_A), lambda i: (i, 0, 0))],
        out_shape=[S((T, AW), F32), S((T, AW), F32), S((T, AW), F32), S((T, C3), F32),
                   S((T, LANES), F32), S((T // CHUNK_A, 32, CHUNK_A), F32)],
        compiler_params=_cp(ARB),
    )(proj, proj, ba, conv_w, alog_row, dtb_row)


_NN = (((1,), (0,)), ((), ()))
_TN = (((0,), (0,)), ((), ()))


def _split(a):
    hi = a.astype(BF16)
    return hi, (a - hi.astype(F32)).astype(BF16)


def _mm3(a, b, dims=_NN):
    ah, al = a if isinstance(a, tuple) else _split(a)
    bh, bl = b if isinstance(b, tuple) else _split(b)
    dg = lambda p, r: lax.dot_general(p, r, dims, preferred_element_type=F32)
    return dg(ah, bh) + (dg(ah, bl) + dg(al, bh))


def _interleave(gens):
    gens = list(gens)
    while gens:
        alive = []
        for g in gens:
            try:
                next(g)
                alive.append(g)
            except StopIteration:
                pass
        gens = alive


def _chunk_terms(q, k, v, gcolv, growv, h, H):
    C = CHUNK_A
    beta_c = gcolv[:, h:h + 1]
    g_c = gcolv[:, H + h:H + h + 1]
    gc_c = gcolv[:, 2 * H + h:2 * H + h + 1]
    gc_r = growv[2 * H + h:2 * H + h + 1, :]
    ri = lax.broadcasted_iota(jnp.int32, (C, C), 0)
    ci = lax.broadcasted_iota(jnp.int32, (C, C), 1)
    incl = ri >= ci
    strict = ri > ci
    kb = k * beta_c
    vb = v * beta_c
    p_raw = _mm_nt(kb, k)
    qk_raw = _mm_nt(q, k)
    gam = jnp.where(incl, jnp.exp(jnp.where(incl, gc_c - gc_r, 0.0)), 0.0)
    e_c = jnp.exp(gc_c)
    gl = gc_r[:, C - 1:C]
    edec = jnp.exp(gl - gc_c)
    yield
    lmat = jnp.where(strict, p_raw * gam, 0.0)
    attn = jnp.where(incl, qk_raw * gam, 0.0)
    return dict(beta_c=beta_c, g_c=g_c, gc_c=gc_c, gc_r=gc_r, incl=incl, strict=strict, gam=gam, e_c=e_c,
                kb=kb, vb=vb, lmat=lmat, attn=attn, gl=gl, edec=edec, ri=ri, ci=ci)


def _inv_unit_lower(lmat):
    C = lmat.shape[0]
    ri = lax.broadcasted_iota(jnp.int32, (C, C), 0)
    ci = lax.broadcasted_iota(jnp.int32, (C, C), 1)
    eye = (ri == ci).astype(F32)
    x = -lmat
    a = eye + x
    n = 1
    while 2 * n < C:
        xs = _split(x)
        x = _mm3(xs, xs)
        yield
        a = a + _mm3(a, x)
        n *= 2
    yield
    return a


def _delta_fwd(q, k, v, gcol, grow, H, D):
    T = q.shape[0]
    C = CHUNK_A
    N = T // C
    AW = H * D
    CPS = 2 if N % 2 == 0 else 1

    def body(q_ref, k_ref, v_ref, gcol_ref, grow_ref, o_ref, vn_ref, ssave_ref, asave_ref, s_ref):
        @pl.when(pl.program_id(0) == 0)
        def _():
            s_ref[...] = jnp.zeros_like(s_ref)

        state = {(0, h): s_ref[h] for h in range(H)}

        def head(cc, h):
            rows = slice(cc * C, (cc + 1) * C)
            sl = slice(h * D, (h + 1) * D)
            qv, kv, vv = q_ref[rows, sl], k_ref[rows, sl], v_ref[rows, sl]
            t = yield from _chunk_terms(qv, kv, vv, gcol_ref[rows, :], grow_ref[cc], h, H)
            a = yield from _inv_unit_lower(t["lmat"])
            asave_ref[cc, h] = a
            while (cc, h) not in state:
                yield
            st = state[(cc, h)]
            ssave_ref[cc, h] = st
            ks = _mm(t["kb"] * t["e_c"], st)
            o_inter = _mm(qv * t["e_c"], st)
            yield
            v_new = _mm3(a, t["vb"] - ks)
            yield
            vn_ref[rows, sl] = v_new
            o_intra = _mm(t["attn"], v_new)
            s_upd = _mm_tn(kv * t["edec"], v_new)
            yield
            o_ref[rows, sl] = o_inter + o_intra
            state[(cc + 1, h)] = st * jnp.exp(t["gl"]) + s_upd

        _interleave(head(cc, h) for cc in range(CPS) for h in range(H))
        for h in range(H):
            s_ref[h] = state[(CPS, h)]

    blk = lambda: pl.BlockSpec((CPS * C, AW), lambda n: (n, 0))
    return pl.pallas_call(
        body, name="delta_fwd", grid=(N // CPS,),
        in_specs=[blk(), blk(), blk(),
                  pl.BlockSpec((CPS * C, LANES), lambda n: (n, 0)),
                  pl.BlockSpec((CPS, 32, C), lambda n: (n, 0, 0))],
        out_specs=[blk(), blk(),
                   pl.BlockSpec((CPS, H, D, D), lambda n: (n, 0, 0, 0)),
                   pl.BlockSpec((CPS, H, C, C), lambda n: (n, 0, 0, 0))],
        out_shape=[S((T, AW), F32), S((T, AW), F32), S((N, H, D, D), F32), S((N, H, C, C), F32)],
        scratch_shapes=[pltpu.VMEM((H, D, D), F32)],
        compiler_params=_cp(ARB),
    )(q, k, v, gcol, grow)


def _delta_bwd(q, k, v, gcol, grow, ba, vnew, ssave, asave, d_o, a_log, dt_bias, H, D, carry):
    T = q.shape[0]
    C = CHUNK_A
    N = T // C
    AW = H * D
    nc = len(carry)
    CPS = 2 if N % 2 == 0 else 1
    NS = N // CPS

    def body(al_ref, dt_ref, q_ref, k_ref, v_ref, gcol_ref, grow_ref, ba_ref, vn_ref, ss_ref, as_ref, do_ref, *rest):
        cins = rest[:nc]
        dq_ref, dk_ref, dv_ref, dgate_ref, dpar_ref = rest[nc:nc + 5]
        couts = rest[nc + 5:2 * nc + 5]
        ds_ref, csend, crecv = rest[2 * nc + 5:]
        ccps = _chip_exchange_copies(cins, couts, csend, crecv)

        @pl.when(pl.program_id(0) == 0)
        def _():
            ds_ref[...] = jnp.zeros_like(ds_ref)
            dpar_ref[...] = jnp.zeros_like(dpar_ref)
            for cp in ccps:
                cp.start()

        lane = lax.broadcasted_iota(jnp.int32, (C, LANES), 1)
        rowi = lax.broadcasted_iota(jnp.int32, (C, 1), 0)
        acc = {cc: jnp.zeros((C, LANES), F32) for cc in range(CPS)}
        state = {(0, h): ds_ref[h] for h in range(H)}

        def head(oi, h):
            cc = CPS - 1 - oi
            rows = slice(cc * C, (cc + 1) * C)
            sl = slice(h * D, (h + 1) * D)
            st = ss_ref[cc, h]
            a = as_ref[cc, h]
            qv, kv, vv, dov, v_new = q_ref[rows, sl], k_ref[rows, sl], v_ref[rows, sl], do_ref[rows, sl], vn_ref[rows, sl]
            t = yield from _chunk_terms(qv, kv, vv, gcol_ref[rows, :], grow_ref[cc], h, H)
            beta_c, e_c, gam, kb = t["beta_c"], t["e_c"], t["gam"], t["kb"]
            incl, strict, attn, lmat, edec = t["incl"], t["strict"], t["attn"], t["lmat"], t["edec"]
            kdec = kv * edec
            egl = jnp.exp(t["gl"])
            qe = qv * e_c
            ekb = kb * e_c

            t1 = _mm_nt(dov, st)
            ds_o = _mm_tn(qe, dov)
            dattn_raw = _mm_nt(dov, v_new)
            dv_new_o = _mm_tn(attn, dov)
            yield
            while (oi, h) not in state:
                yield
            ds_next = state[(oi, h)]
            dkdec = _mm_nt(v_new, ds_next)
            dv_new_s = _mm(kdec, ds_next)
            yield
            dgl = egl * jnp.sum(jnp.sum(st * ds_next, axis=1, keepdims=True), axis=0, keepdims=True)
            dk = edec * dkdec
            r = jnp.sum(dkdec * kdec, axis=1, keepdims=True)
            dgc = -r
            dgl = dgl + jnp.sum(r, axis=0, keepdims=True)
            dq = e_c * t1
            dgc = dgc + jnp.sum(t1 * qe, axis=1, keepdims=True)
            dattn = jnp.where(incl, dattn_raw, 0.0)
            dv_new = dv_new_s + dv_new_o
            dqm = dattn * gam
            z = dattn * attn
            dvb = _mm3(a, dv_new, _TN)
            dq_a = _mm(dqm, kv)
            dk_a = _mm_tn(dqm, qv)
            yield
            dq_ref[rows, sl] = dq + dq_a
            dv_ref[rows, sl] = beta_c * dvb
            ds_kb = _mm_tn(ekb, dvb)
            dekb_neg = _mm_nt(dvb, st)
            dl_neg = _mm_nt(dvb, v_new)
            yield
            state[(oi + 1, h)] = egl * ds_next + ds_o - ds_kb
            dekb = -dekb_neg
            dl = jnp.where(strict, -dl_neg, 0.0)
            dp = dl * gam
            z = z + dl * lmat
            dkb_p = _mm(dp, kv)
            dk_p = _mm_tn(dp, kb)
            dgc = dgc + jnp.sum(dekb * ekb, axis=1, keepdims=True)
            dgc = dgc + jnp.sum(z, axis=1, keepdims=True) - jnp.sum(z.T, axis=1, keepdims=True)
            dgc = dgc + jnp.where(rowi == C - 1, dgl, 0.0)
            yield
            dkb = dkb_p + e_c * dekb
            dk_ref[rows, sl] = dk + dk_a + dk_p + beta_c * dkb
            dbeta = jnp.sum(dkb * kv + dvb * vv, axis=1, keepdims=True)
            acc[cc] = acc[cc] + jnp.where(lane == h, dbeta, 0.0) + jnp.where(lane == H + h, dgc, 0.0)

        _interleave(head(oi, h) for oi in range(CPS) for h in range(H))
        for h in range(H):
            ds_ref[h] = state[(CPS, h)]
        ri = lax.broadcasted_iota(jnp.int32, (C, C), 0)
        ci = lax.broadcasted_iota(jnp.int32, (C, C), 1)
        upper = (ri <= ci).astype(F32)
        dal = jnp.zeros((1, LANES), F32)
        ddt = jnp.zeros((1, LANES), F32)
        for cc in range(CPS):
            rows = slice(cc * C, (cc + 1) * C)
            gates = gcol_ref[rows, :]
            dg_all = _mm3(upper, acc[cc])
            d_braw = acc[cc] * gates * (1.0 - gates)
            d_araw = dg_all * (-jnp.exp(al_ref[...])) * _sigmoid(ba_ref[rows, :] + dt_ref[...])
            dgate_ref[rows, :] = jnp.where(lane < H, d_braw, jnp.where(lane < 2 * H, d_araw, 0.0))
            dal = dal + jnp.sum(dg_all * gates, axis=0, keepdims=True)
            ddt = ddt + jnp.sum(d_araw, axis=0, keepdims=True)
        dpar_ref[0:1, :] += dal
        dpar_ref[1:2, :] += ddt

        @pl.when(pl.program_id(0) == NS - 1)
        def _():
            for cp in ccps:
                cp.wait()

    rev = lambda s: NS - 1 - s
    blk = lambda: pl.BlockSpec((CPS * C, AW), lambda s: (rev(s), 0))
    row = pl.BlockSpec((1, LANES), lambda s: (0, 0))
    any_spec = pl.BlockSpec(memory_space=pl.ANY)
    res = pl.pallas_call(
        body, name="delta_bwd", grid=(NS,),
        in_specs=[row, row, blk(), blk(), blk(),
                  pl.BlockSpec((CPS * C, LANES), lambda s: (rev(s), 0)),
                  pl.BlockSpec((CPS, 32, C), lambda s: (rev(s), 0, 0)),
                  pl.BlockSpec((CPS * C, LANES), lambda s: (rev(s), 0)),
                  blk(),
                  pl.BlockSpec((CPS, H, D, D), lambda s: (rev(s), 0, 0, 0)),
                  pl.BlockSpec((CPS, H, C, C), lambda s: (rev(s), 0, 0, 0)),
                  blk()] + [any_spec] * nc,
        out_specs=[blk(), blk(), blk(),
                   pl.BlockSpec((CPS * C, LANES), lambda s: (rev(s), 0)),
                   pl.BlockSpec((8, LANES), lambda s: (0, 0))] + [any_spec] * nc,
        out_shape=[S((T, AW), F32), S((T, AW), F32), S((T, AW), F32),
                   S((T, LANES), F32), S((8, LANES), F32)] + [S((3,) + a.shape[1:], a.dtype) for a in carry],
        scratch_shapes=[pltpu.VMEM((H, D, D), F32),
                        pltpu.SemaphoreType.DMA((max(nc, 1), 3)), pltpu.SemaphoreType.DMA((max(nc, 1), 3))],
        compiler_params=_cp(ARB),
    )(a_log, dt_bias, q, k, v, gcol, grow, ba, vnew, ssave, asave, d_o, *carry)
    return res[:5], res[5:]


def _ln_stats(xv):
    mu = jnp.mean(xv, axis=-1, keepdims=True)
    xc = xv - mu
    var = jnp.mean(xc * xc, axis=-1, keepdims=True)
    rstd = lax.rsqrt(var + EPS)
    return xc * rstd, rstd


def _mix_fwd(o, proj, head_norm_w, ln_w, ln_b, w_sp, bs_t, H, D, G, P):
    T = o.shape[0]
    AW, BW = H * D, G * P
    MIX = AW + BW
    nb = AW // BW if AW % BW == 0 else None
    assert nb == 1, "group widths must match the projection column blocks"
    cb = 3

    def body(o_ref, za_ref, ub_ref, vb_ref, zb_ref, hw_ref, lw_ref, lb_ref, w_ref, bs_ref, out_ref):
        hw = hw_ref[...]
        for h in range(H):
            sl = slice(h * D, (h + 1) * D)
            oh = o_ref[:, sl]
            rs = lax.rsqrt(jnp.mean(oh * oh, axis=-1, keepdims=True) + EPS)
            out_ref[:, sl] = (oh * rs * hw * _silu(za_ref[:, sl])).astype(BF16)
        xhat, _ = _ln_stats(vb_ref[...])
        vn = xhat * lw_ref[...] + lb_ref[...]
        ri = lax.broadcasted_iota(jnp.int32, (P, P), 0)
        ci = lax.broadcasted_iota(jnp.int32, (P, P), 1)
        bsv = bs_ref[...]
        for g in range(G):
            sl = slice(g * P, (g + 1) * P)
            wm = jnp.where(ri >= ci, w_ref[g], 0.0)
            s = _mm(wm, vn[:, sl]) + bsv[:, g:g + 1]
            out_ref[:, AW + g * P:AW + (g + 1) * P] = (ub_ref[:, sl] * s * _silu(zb_ref[:, sl])).astype(BF16)

    row = lambda w: pl.BlockSpec((1, w), lambda i: (0, 0))
    return pl.pallas_call(
        body, name="mix_fwd", grid=(T // P,),
        in_specs=[pl.BlockSpec((P, AW), lambda i: (i, 0)),
                  pl.BlockSpec((P, AW), lambda i: (i, cb)),
                  pl.BlockSpec((P, BW), lambda i: (i, cb + 1)),
                  pl.BlockSpec((P, BW), lambda i: (i, cb + 2)),
                  pl.BlockSpec((P, BW), lambda i: (i, cb + 3)),
                  row(D), row(BW), row(BW),
                  pl.BlockSpec((G, P, P), lambda i: (0, 0, 0)),
                  pl.BlockSpec((P, G), lambda i: (0, 0))],
        out_specs=pl.BlockSpec((P, MIX), lambda i: (i, 0)),
        out_shape=S((T, MIX), BF16),
        compiler_params=_cp(ARB),
    )(o, proj, proj, proj, proj, head_norm_w, ln_w, ln_b, w_sp, bs_t)


def _mix_bwd(d_ocat, o, proj, head_norm_w, ln_w, ln_b, w_sp, bs_t, H, D, G, P, carry):
    T = o.shape[0]
    AW, BW = H * D, G * P
    MIX = AW + BW
    cb = 3
    nc = len(carry)

    def body(dc_ref, o_ref, za_ref, ub_ref, vb_ref, zb_ref, hw_ref, lw_ref, lb_ref, w_ref, bs_ref, *rest):
        cins = rest[:nc]
        do_ref, dmain_ref, dhw_ref, dln_ref, dw_ref, dbs_ref = rest[nc:nc + 6]
        couts = rest[nc + 6:2 * nc + 6]
        dvn_ref, drest_ref, out_sems, csend, crecv = rest[2 * nc + 6:]
        i = pl.program_id(0)
        slot = lax.rem(i, 2)
        ccps = _sibling_copies(cins, couts, csend, crecv)

        def out_copy(step, s):
            return pltpu.make_async_copy(
                drest_ref.at[s], dmain_ref.at[pl.ds(step * P, P), pl.ds(cb * AW, AW + 3 * BW)], out_sems.at[s])

        @pl.when(i == 0)
        def _():
            dhw_ref[...] = jnp.zeros_like(dhw_ref)
            dln_ref[...] = jnp.zeros_like(dln_ref)
            dw_ref[...] = jnp.zeros_like(dw_ref)
            dbs_ref[...] = jnp.zeros_like(dbs_ref)
            for cp in ccps:
                cp.start()

        @pl.when(i >= 2)
        def _():
            out_copy(i - 2, slot).wait()

        hw = hw_ref[...]
        dhw = jnp.zeros((1, D), F32)
        for h in range(H):
            sl = slice(h * D, (h + 1) * D)
            oh = o_ref[:, sl]
            za = za_ref[:, sl]
            doa = dc_ref[:, sl]
            rs = lax.rsqrt(jnp.mean(oh * oh, axis=-1, keepdims=True) + EPS)
            xh = oh * rs
            d_on = doa * _silu(za)
            drest_ref[slot, :, sl] = (doa * (xh * hw) * _dsilu(za)).astype(BF16)
            dhw = dhw + jnp.sum(d_on * xh, axis=0, keepdims=True)
            dxh = d_on * hw
            do_ref[:, sl] = rs * (dxh - xh * jnp.mean(dxh * xh, axis=-1, keepdims=True))
        dhw_ref[0:1, :] += dhw

        xhat, rstd = _ln_stats(vb_ref[...])
        lw = lw_ref[...]
        vn = xhat * lw + lb_ref[...]
        ri = lax.broadcasted_iota(jnp.int32, (P, P), 0)
        ci = lax.broadcasted_iota(jnp.int32, (P, P), 1)
        lane = lax.broadcasted_iota(jnp.int32, (P, LANES), 1)
        bsv = bs_ref[...]
        dbs = jnp.zeros((P, LANES), F32)
        for g in range(G):
            sl = slice(g * P, (g + 1) * P)
            wm = jnp.where(ri >= ci, w_ref[g], 0.0)
            vng = vn[:, sl]
            s = _mm(wm, vng) + bsv[:, g:g + 1]
            dob = dc_ref[:, AW + g * P:AW + (g + 1) * P]
            ub = ub_ref[:, sl]
            zb = zb_ref[:, sl]
            szb = _silu(zb)
            drest_ref[slot, :, AW + g * P:AW + (g + 1) * P] = (dob * s * szb).astype(BF16)
            drest_ref[slot, :, AW + 2 * BW + g * P:AW + 2 * BW + (g + 1) * P] = (
                dob * ub * s * _dsilu(zb)).astype(BF16)
            ds = dob * ub * szb
            dvn_ref[:, sl] = _mm_tn(wm, ds)
            dw_ref[g] += jnp.where(ri >= ci, _mm_nt(ds, vng), 0.0)
            dbs = dbs + jnp.where(lane == g, jnp.sum(ds, axis=1, keepdims=True), 0.0)
        dbs_ref[...] += dbs
        dvn = dvn_ref[...]
        dln_ref[0:1, :] += jnp.sum(dvn * xhat, axis=0, keepdims=True)
        dln_ref[1:2, :] += jnp.sum(dvn, axis=0, keepdims=True)
        dxh = dvn * lw
        dvb = rstd * (dxh - jnp.mean(dxh, axis=-1, keepdims=True) - xhat * jnp.mean(dxh * xhat, axis=-1, keepdims=True))
        drest_ref[slot, :, AW + BW:AW + 2 * BW] = dvb.astype(BF16)

        out_copy(i, slot).start()

        @pl.when(i == nstep - 1)
        def _():
            out_copy(i, slot).wait()
            if nstep > 1:
                out_copy(i - 1, 1 - slot).wait()
            for cp in ccps:
                cp.wait()

    nstep = T // P
    row = lambda w: pl.BlockSpec((1, w), lambda i: (0, 0))
    any_spec = pl.BlockSpec(memory_space=pl.ANY)
    res = pl.pallas_call(
        body, name="mix_bwd", grid=(nstep,),
        in_specs=[pl.BlockSpec((P, MIX), lambda i: (i, 0)),
                  pl.BlockSpec((P, AW), lambda i: (i, 0)),
                  pl.BlockSpec((P, AW), lambda i: (i, cb)),
                  pl.BlockSpec((P, BW), lambda i: (i, cb + 1)),
                  pl.BlockSpec((P, BW), lambda i: (i, cb + 2)),
                  pl.BlockSpec((P, BW), lambda i: (i, cb + 3)),
                  row(D), row(BW), row(BW),
                  pl.BlockSpec((G, P, P), lambda i: (0, 0, 0)),
                  pl.BlockSpec((P, G), lambda i: (0, 0))] + [any_spec] * nc,
        out_specs=[pl.BlockSpec((P, AW), lambda i: (i, 0)),
                   any_spec,
                   pl.BlockSpec((8, D), lambda i: (0, 0)),
                   pl.BlockSpec((8, BW), lambda i: (0, 0)),
                   pl.BlockSpec((G, P, P), lambda i: (0, 0, 0)),
                   pl.BlockSpec((P, LANES), lambda i: (0, 0))] + [any_spec] * nc,
        out_shape=[S((T, AW), F32), S((T, cb * AW + AW + 3 * BW), BF16), S((8, D), F32), S((8, BW), F32),
                   S((G, P, P), F32), S((P, LANES), F32)] + [S(a.shape[:1] + a.shape[2:], a.dtype) for a in carry],
        scratch_shapes=[pltpu.VMEM((P, BW), F32), pltpu.VMEM((2, P, AW + 3 * BW), BF16),
                        pltpu.SemaphoreType.DMA((2,))] + _sibling_sems(carry),
        compiler_params=_cp(ARB),
    )(d_ocat, o, proj, proj, proj, proj, head_norm_w, ln_w, ln_b, w_sp, bs_t, *carry)
    return res[:6], res[6:]


def _out_proj_loss(ocat, w_out, x, target, fnw):
    T, MIX = ocat.shape
    DM = x.shape[1]
    tm = _tile(T, 256, 8)

    def body(oc_ref, w_ref, x_ref, t_ref, fw_ref, dh_ref, dhb_ref, doc_ref, loss_ref, gfw_ref):
        @pl.when(pl.program_id(0) == 0)
        def _():
            loss_ref[...] = jnp.zeros_like(loss_ref)
            gfw_ref[...] = jnp.zeros_like(gfw_ref)

        wv = w_ref[...]
        hh = x_ref[...] + jnp.dot(oc_ref[...].astype(MXU), wv.astype(MXU), preferred_element_type=F32)
        rs = lax.rsqrt(jnp.mean(hh * hh, axis=-1, keepdims=True) + EPS)
        hn = hh * rs
        fw = fw_ref[...]
        e = hn * fw - t_ref[...]
        row_loss = 0.5 * jnp.mean(e * e, axis=-1, keepdims=True)
        loss_ref[...] += jnp.sum(row_loss, axis=0, keepdims=True)
        dy = e * (1.0 / DM)
        gfw_ref[0:1, :] += jnp.sum(dy * hn, axis=0, keepdims=True)
        dhn = dy * fw
        dh = rs * (dhn - hn * jnp.mean(dhn * hn, axis=-1, keepdims=True))
        dh_ref[...] = dh
        dhb = dh.astype(BF16)
        dhb_ref[...] = dhb
        doc_ref[...] = _mm_nt(dhb, wv)

    return pl.pallas_call(
        body, name="out_proj_loss", grid=(T // tm,),
        in_specs=[pl.BlockSpec((tm, MIX), lambda i: (i, 0)),
                  pl.BlockSpec((MIX, DM), lambda i: (0, 0)),
                  pl.BlockSpec((tm, DM), lambda i: (i, 0)),
                  pl.BlockSpec((tm, DM), lambda i: (i, 0)),
                  pl.BlockSpec((1, DM), lambda i: (0, 0))],
        out_specs=[pl.BlockSpec((tm, DM), lambda i: (i, 0)),
                   pl.BlockSpec((tm, DM), lambda i: (i, 0)),
                   pl.BlockSpec((tm, MIX), lambda i: (i, 0)),
                   pl.BlockSpec((8, LANES), lambda i: (0, 0)),
                   pl.BlockSpec((8, DM), lambda i: (0, 0))],
        out_shape=[S((T, DM), F32), S((T, DM), BF16), S((T, MIX), F32), S((8, LANES), F32), S((8, DM), F32)],
        compiler_params=_cp(ARB),
    )(ocat, w_out, x, target, fnw)


def _grad_w(lhs, rhs, name):
    T, A = lhs.shape
    B = rhs.shape[1]
    ta = _tile(A, 512, LANES)
    tk = _tile(T, 1024, 16)
    nk = T // tk

    def body(l_ref, r_ref, out_ref, acc_ref):
        k = pl.program_id(1)
        part = _mm_tn(l_ref[...], r_ref[...])

        @pl.when(k == 0)
        def _():
            acc_ref[...] = part

        @pl.when(k > 0)
        def _():
            acc_ref[...] += part

        @pl.when(k == nk - 1)
        def _():
            out_ref[...] = acc_ref[...].astype(BF16)

    return pl.pallas_call(
        body, name=name, grid=(A // ta, nk),
        in_specs=[pl.BlockSpec((tk, ta), lambda i, k: (k, i)),
                  pl.BlockSpec((tk, B), lambda i, k: (k, 0))],
        out_specs=pl.BlockSpec((ta, B), lambda i, k: (i, 0)),
        out_shape=S((A, B), BF16),
        scratch_shapes=[pltpu.VMEM((ta, B), F32)],
        compiler_params=_cp(ARB, ARB),
    )(lhs, rhs)


def _grad_w_in(xn, dmain, dba, WD, gate_lo, gate_hi):
    T, DM = xn.shape
    NM = dmain.shape[1]
    tn = _tile(NM, 1024, LANES)
    tk = _tile(T, 2048, 16)
    nj, nk = NM // tn, T // tk
    ND = N_DEV
    tiles = [[] for _ in range(nj)]
    first_tile, last_tile = {}, {}
    for d, s0, s1, dest, c0 in _pieces(WD, gate_lo, gate_hi, ND * WD):
        if dest != "main":
            continue
        while s0 < s1:
            jj = c0 // tn
            w = min(s1 - s0, (jj + 1) * tn - c0)
            tiles[jj].append((d, s0, w, "main", c0 - jj * tn))
            first_tile.setdefault(d, jj)
            last_tile[d] = jj
            s0, c0 = s0 + w, c0 + w
    for d, s0, s1, dest, c0 in _pieces(WD, gate_lo, gate_hi, ND * WD):
        if dest == "gate":
            tiles[first_tile[d]].append((d, s0, s1 - s0, "gate", c0))
    assert sorted(first_tile) == list(range(ND)) and all(last_tile[d] <= first_tile[d + 2] for d in range(ND - 2))

    def body(xn_ref, dm_ref, dba_ref, keep_ref, recv_ref, acc_ref, gate_ref, buf_ref, lsem, ssem, rsem):
        j = pl.program_id(0)
        k = pl.program_id(1)
        px, py, pc = _position()

        @pl.when(k == 0)
        def _():
            acc_ref[...] = jnp.zeros_like(acc_ref)

        @pl.when((j == 0) & (k == 0))
        def _():
            gate_ref[...] = jnp.zeros_like(gate_ref)

        xv = xn_ref[...]
        acc_ref[...] += _mm_tn(xv, dm_ref[...])

        @pl.when(j == 0)
        def _():
            gate_ref[...] += _mm_tn(xv, dba_ref[...])

        def local(d):
            return pltpu.make_async_copy(buf_ref.at[d % 2], keep_ref.at[d // 2], lsem.at[d // 2])

        def remote(d):
            return pltpu.make_async_remote_copy(
                src_ref=buf_ref.at[d % 2], dst_ref=recv_ref.at[d // 2], send_sem=ssem.at[d // 2],
                recv_sem=rsem.at[d // 2], device_id=(px, py, 1 - pc), device_id_type=MESH)

        def leave(d, start):
            @pl.when(pc == d % 2)
            def _():
                local(d).start() if start else local(d).wait()

            @pl.when(pc != d % 2)
            def _():
                remote(d).start() if start else remote(d).wait_send()

        def emit(jj):
            shards = sorted({p[0] for p in tiles[jj]})
            for d in shards:
                if first_tile[d] == jj and d >= 2:
                    leave(d - 2, False)
                for dd, s0, w, src, c0 in tiles[jj]:
                    if dd == d:
                        ref = acc_ref if src == "main" else gate_ref
                        buf_ref[d % 2, :, s0:s0 + w] = ref[:, c0:c0 + w].astype(BF16)
                if last_tile[d] == jj:
                    leave(d, True)
            if jj == nj - 1:
                for d in (ND - 2, ND - 1):
                    leave(d, False)
                for q in range(ND // 2):
                    remote(2 * q).wait_recv()

        for jj in range(nj):
            @pl.when((j == jj) & (k == nk - 1))
            def _(jj=jj):
                emit(jj)

    any_spec = pl.BlockSpec(memory_space=pl.ANY)
    return pl.pallas_call(
        body, name="grad_w_in", grid=(nj, nk),
        in_specs=[pl.BlockSpec((tk, DM), lambda j, k: (k, 0)),
                  pl.BlockSpec((tk, tn), lambda j, k: (k, j)),
                  pl.BlockSpec((tk, LANES), lambda j, k: (k, 0))],
        out_specs=[any_spec, any_spec],
        out_shape=[S((ND // 2, DM, WD), BF16), S((ND // 2, DM, WD), BF16)],
        scratch_shapes=[pltpu.VMEM((DM, tn), F32), pltpu.VMEM((DM, LANES), F32), pltpu.VMEM((2, DM, WD), BF16),
                        pltpu.SemaphoreType.DMA((ND // 2,)), pltpu.SemaphoreType.DMA((ND // 2,)),
                        pltpu.SemaphoreType.DMA((ND // 2,))],
        compiler_params=_cp(ARB, ARB),
    )(xn, dmain, dba)


def _pair_sum_plain(a, b, name):
    K, R, C = a.shape
    tr = _tile(R, 256, 16)

    def body(a_ref, b_ref, o_ref):
        o_ref[...] = (a_ref[...].astype(F32) + b_ref[...].astype(F32)).astype(BF16)

    spec = lambda: pl.BlockSpec((1, tr, C), lambda q, i: (q, i, 0))
    return pl.pallas_call(body, name=name, grid=(K, R // tr), in_specs=[spec(), spec()], out_specs=spec(),
                          out_shape=S((K, R, C), BF16), compiler_params=_cp(ARB, ARB))(a, b)


def _dx_rows(T):
    tm = _tile(T, 512, 8)
    return tm if T // tm >= 2 else T // 2


def _dx_part(name, dmain, dba, w_main, w_ba, x, dh, norm_w, blk0, nblk, prev, hbm_in, hbm_alias, hbm_new, make_copies):
    T, NM = dmain.shape
    DM = x.shape[1]
    tm = _dx_rows(T)
    tk = _tile(NM, 1024, LANES)
    nk = NM // tk
    n_in, n_al, n_new = len(hbm_in), len(hbm_alias), len(hbm_new)
    n_prev = 0 if prev is None else 2
    last_step = nblk * nk - 1

    def body(dm_ref, dba_ref, w_ref, wba_ref, x_ref, dh_ref, nw_ref, *rest):
        r = list(rest)
        gnw_prev_ref = r.pop(0) if n_prev else None
        if n_prev:
            r.pop(0)
        in_refs = [r.pop(0) for _ in range(n_in)]
        del r[:n_al]
        gx_ref, gnw_ref = r.pop(0), r.pop(0)
        alias_refs = [r.pop(0) for _ in range(n_al)]
        new_refs = [r.pop(0) for _ in range(n_new)]
        acc_ref, send_sems, recv_sems = r
        i = pl.program_id(0)
        k = pl.program_id(1)
        step = i * nk + k
        cps = make_copies(in_refs, alias_refs, new_refs, send_sems, recv_sems)

        @pl.when(step == 0)
        def _():
            gnw_ref[...] = gnw_prev_ref[...] if n_prev else jnp.zeros_like(gnw_ref)
            for cp in cps:
                cp.start()

        @pl.when(k == 0)
        def _():
            acc_ref[...] = _mm_nt(dba_ref[...], wba_ref[...])

        acc_ref[...] += _mm_nt(dm_ref[...], w_ref[...])

        @pl.when(k == nk - 1)
        def _():
            xv = x_ref[...]
            rs = lax.rsqrt(jnp.mean(xv * xv, axis=-1, keepdims=True) + EPS)
            xh = xv * rs
            dxn = acc_ref[...]
            gnw_ref[0:1, :] += jnp.sum(dxn * xh, axis=0, keepdims=True)
            dxh = dxn * nw_ref[...]
            gx_ref[...] = dh_ref[...] + rs * (dxh - xh * jnp.mean(dxh * xh, axis=-1, keepdims=True))

        @pl.when(step == last_step)
        def _():
            for cp in cps:
                cp.wait()

    any_spec = pl.BlockSpec(memory_space=pl.ANY)
    prev_specs = [pl.BlockSpec((8, DM), lambda i, k: (0, 0)), any_spec] if n_prev else []
    prev_args = [prev[1], prev[0]] if n_prev else []
    aliases = {8: 0} if n_prev else {}
    for q in range(n_al):
        aliases[7 + n_prev + n_in + q] = 2 + q
    res = pl.pallas_call(
        body, name=name, grid=(nblk, nk),
        in_specs=[pl.BlockSpec((tm, tk), lambda i, k: (blk0 + i, k)),
                  pl.BlockSpec((tm, LANES), lambda i, k: (blk0 + i, 0)),
                  pl.BlockSpec((DM, tk), lambda i, k: (0, k)),
                  pl.BlockSpec((DM, LANES), lambda i, k: (0, 0)),
                  pl.BlockSpec((tm, DM), lambda i, k: (blk0 + i, 0)),
                  pl.BlockSpec((tm, DM), lambda i, k: (blk0 + i, 0)),
                  pl.BlockSpec((1, DM), lambda i, k: (0, 0))] + prev_specs + [any_spec] * (n_in + n_al),
        out_specs=[pl.BlockSpec((tm, DM), lambda i, k: (blk0 + i, 0)),
                   pl.BlockSpec((8, DM), lambda i, k: (0, 0))] + [any_spec] * (n_al + n_new),
        out_shape=[S((T, DM), F32), S((8, DM), F32)] + [S(a.shape, a.dtype) for a in hbm_alias] + list(hbm_new),
        scratch_shapes=[pltpu.VMEM((tm, DM), F32), pltpu.SemaphoreType.DMA((10,)), pltpu.SemaphoreType.DMA((10,))],
        input_output_aliases=aliases,
        compiler_params=_cp(ARB, ARB),
    )(dmain, dba, w_main, w_ba, x, dh, norm_w, *prev_args, *hbm_in, *hbm_alias)
    return (res[0], res[1]), res[2:2 + n_al], res[2 + n_al:]


def _remote(kk, src, dst, to, send_sems, recv_sems):
    return pltpu.make_async_remote_copy(src_ref=src, dst_ref=dst, send_sem=send_sems.at[kk], recv_sem=recv_sems.at[kk],
                                        device_id=to, device_id_type=MESH)


def _dx(dmain, dba, w_main, w_ba, x, dh, norm_w, chip_sum, small, cut):
    R, C = chip_sum.shape[1:]
    half = R // 2
    assert half % 16 == 0
    T = x.shape[0]
    ni = T // _dx_rows(T)
    cut = max(1, min(cut, ni - 1))
    upper, lower = pl.ds(0, half), pl.ds(half, half)

    def nbrs():
        px, py, pc = _position()
        return (px, py), (1 - px, py, pc), (px, 1 - py, pc)

    def phase1(ins, als, news, ss, rs):
        (px, py), xn, yn = nbrs()
        cs = ins[0]
        recv, stage = news
        bx, by, bd = cs.at[2 * (1 - px) + py], cs.at[2 * px + (1 - py)], cs.at[2 * (1 - px) + (1 - py)]
        return [_remote(0, bx.at[upper], recv.at[0].at[upper], xn, ss, rs),
                _remote(1, by.at[lower], recv.at[1].at[lower], yn, ss, rs),
                _remote(2, bd.at[upper], stage.at[0], xn, ss, rs),
                _remote(3, bd.at[lower], stage.at[1], yn, ss, rs)]

    def phase2(ins, als, news, ss, rs):
        (px, py), xn, yn = nbrs()
        comb, small_ref = ins
        recv, gath = als[0], news[0]
        me, small_cps = _broadcast_copies([small_ref], [gath], _Sem2(ss, 2), _Sem2(rs, 2))
        return ([_remote(0, comb.at[0], recv.at[1].at[upper], yn, ss, rs),
                 _remote(1, comb.at[1], recv.at[0].at[lower], xn, ss, rs)] + small_cps
                + [pltpu.make_async_copy(small_ref, gath.at[me], ss.at[9])])

    (gx, gnw), _, (recv, stage) = _dx_part(
        "dx_a", dmain, dba, w_main, w_ba, x, dh, norm_w, 0, cut, None, [chip_sum], [],
        [S((2, R, C), chip_sum.dtype), S((2, half, C), chip_sum.dtype)], phase1)
    comb = _relay_add(chip_sum, stage)
    (gx, gnw), (recv,), (gath,) = _dx_part(
        "dx_b", dmain, dba, w_main, w_ba, x, dh, norm_w, cut, ni - cut, (gx, gnw), [comb, small], [recv],
        [S((N_DEV,) + small.shape, F32)], phase2)
    return gx, gnw, gath, recv


class _Sem2:
    def __init__(self, sems, lo):
        self.sems, self.lo = sems, lo

    @property
    def at(self):
        outer = self

        class _At:
            def __getitem__(self, idx):
                a, k = idx
                return outer.sems.at[outer.lo + k]
        return _At()


def _relay_add(chip_sum, stage):
    _, R, C = chip_sum.shape
    half = R // 2
    tr = _tile(half, 256, 16)
    nt = half // tr
    px, py, _ = _position()
    idx = jnp.stack([2 * px + (1 - py), 2 * (1 - px) + py]).astype(jnp.int32)

    def body(idx_ref, p_ref, s_ref, o_ref):
        del idx_ref
        o_ref[0] = (p_ref[0].astype(F32) + s_ref[0].astype(F32)).astype(BF16)

    return pl.pallas_call(
        body, name="relay_add",
        grid_spec=pltpu.PrefetchScalarGridSpec(
            num_scalar_prefetch=1, grid=(2, nt),
            in_specs=[pl.BlockSpec((1, tr, C), lambda s, i, idx_ref: (idx_ref[s], s * nt + i, 0)),
                      pl.BlockSpec((1, tr, C), lambda s, i, idx_ref: (s, i, 0))],
            out_specs=pl.BlockSpec((1, tr, C), lambda s, i, idx_ref: (s, i, 0))),
        out_shape=S((2, half, C), BF16), compiler_params=_cp(ARB, ARB),
    )(idx, chip_sum, stage)


def _sum_slots(gath):
    _, R, C = gath.shape
    tr = _tile(R, 512, 8)

    def body(g_ref, o_ref):
        tot = g_ref[0]
        for d in range(1, N_DEV):
            tot = tot + g_ref[d]
        o_ref[...] = tot

    return pl.pallas_call(
        body, name="sum_slots", grid=(R // tr,),
        in_specs=[pl.BlockSpec((N_DEV, tr, C), lambda i: (0, i, 0))],
        out_specs=pl.BlockSpec((tr, C), lambda i: (i, 0)),
        out_shape=S((R, C), F32), compiler_params=_cp(ARB),
    )(gath)


def _prep_a_bwd(dq, dk, dv, c, proj, conv_w, dmain, H, D):
    T = c.shape[0]
    AW = H * D
    C3 = 3 * AW
    tb = _tile(T, 256, 8)
    nblk = T // tb
    r8 = tb // 8
    scale = float(D) ** -0.5

    def body(dq_ref, dk_ref, dv_ref, c_ref, dqn_ref, dkn_ref, dvn_ref, cn_ref, x_ref, halo_ref, cw_ref, dmain_in_ref,
             dx_ref, gcw_ref, dc_ref):
        del dmain_in_ref
        i = pl.program_id(0)

        @pl.when(i == 0)
        def _():
            gcw_ref[...] = jnp.zeros_like(gcw_ref)

        def pointwise(rows, dq_r, dk_r, dv_r, c_r, keep):
            for h in range(H):
                for part, d_r, sc in ((0, dq_r, scale), (1, dk_r, 1.0)):
                    sl = slice(part * AW + h * D, part * AW + (h + 1) * D)
                    cv = c_r[:, sl]
                    raw = _silu(cv)
                    rs = lax.rsqrt(jnp.sum(raw * raw, axis=-1, keepdims=True) + EPS)
                    nrm = raw * rs
                    dn = d_r[:, h * D:(h + 1) * D] * sc
                    draw = rs * (dn - nrm * jnp.sum(dn * nrm, axis=-1, keepdims=True))
                    dc_ref[rows, sl] = draw * _dsilu(cv) * keep
            dc_ref[rows, 2 * AW:] = dv_r[...] * _dsilu(c_r[:, 2 * AW:]) * keep

        pointwise(slice(0, tb), dq_ref, dk_ref, dv_ref, c_ref, 1.0)
        pointwise(slice(tb, tb + 8), dqn_ref, dkn_ref, dvn_ref, cn_ref, (i < nblk - 1).astype(F32))

        cw = cw_ref[...]
        dcv = dc_ref[0:tb, :]
        dx = cw[3:4, :] * dcv
        for j in range(3):
            dx = dx + cw[j:j + 1, :] * dc_ref[3 - j:3 - j + tb, :]
        dx_ref[...] = dx.astype(BF16)
        halo = halo_ref[...] * (i > 0).astype(F32)
        xp = jnp.concatenate([halo, x_ref[...]], axis=0)
        for j in range(4):
            gcw_ref[j:j + 1, :] += jnp.sum(dcv * xp[5 + j:5 + j + tb], axis=0, keepdims=True)

    nxt = lambda i: (jnp.minimum((i + 1) * r8, T // 8 - 1), 0)
    return pl.pallas_call(
        body, name="prep_a_bwd", grid=(nblk,),
        in_specs=[pl.BlockSpec((tb, AW), lambda i: (i, 0)),
                  pl.BlockSpec((tb, AW), lambda i: (i, 0)),
                  pl.BlockSpec((tb, AW), lambda i: (i, 0)),
                  pl.BlockSpec((tb, C3), lambda i: (i, 0)),
                  pl.BlockSpec((8, AW), nxt), pl.BlockSpec((8, AW), nxt), pl.BlockSpec((8, AW), nxt),
                  pl.BlockSpec((8, C3), nxt),
                  pl.BlockSpec((tb, C3), lambda i: (i, 0)),
                  pl.BlockSpec((8, C3), lambda i: (jnp.maximum(i * r8 - 1, 0), 0)),
                  pl.BlockSpec((4, C3), lambda i: (0, 0)),
                  pl.BlockSpec(memory_space=pl.ANY)],
        out_specs=[pl.BlockSpec((tb, C3), lambda i: (i, 0)),
                   pl.BlockSpec((8, C3), lambda i: (0, 0))],
        out_shape=[S(dmain.shape, dmain.dtype), S((8, C3), F32)],
        scratch_shapes=[pltpu.VMEM((tb + 8, C3), F32)],
        input_output_aliases={11: 0},
        compiler_params=_cp(ARB),
    )(dq, dk, dv, c, dq, dk, dv, c, proj, proj, conv_w, dmain)


def _adam_math(w, g, m, v):
    m2 = ADAM_B1 * m + (1.0 - ADAM_B1) * g
    v2 = ADAM_B2 * v + (1.0 - ADAM_B2) * (g * g)
    m_hat = m2 / (1.0 - ADAM_B1 ** ADAM_STEP)
    v_hat = v2 / (1.0 - ADAM_B2 ** ADAM_STEP)
    delta = -ADAM_LR * (m_hat / (jnp.sqrt(v_hat) + ADAM_EPS) + ADAM_WD * w)
    return delta, m2, v2


def _pair_sum(blocks, recv, core, name):
    K, _, R, C = blocks.shape
    tr = _tile(R, 256, 16)

    def body(core_ref, a_ref, b_ref, o_ref):
        del core_ref
        o_ref[0] = (a_ref[0, 0].astype(F32) + b_ref[0].astype(F32)).astype(BF16)

    spec = lambda: pl.BlockSpec((1, tr, C), lambda k, i, core_ref: (k, i, 0))
    return pl.pallas_call(
        body, name=name,
        grid_spec=pltpu.PrefetchScalarGridSpec(
            num_scalar_prefetch=1, grid=(K, R // tr),
            in_specs=[pl.BlockSpec((1, 1, tr, C), lambda k, i, core_ref: (k, core_ref[0], i, 0)), spec()],
            out_specs=spec()),
        out_shape=S((K, R, C), BF16), compiler_params=_cp(ARB, ARB),
    )(core, blocks, recv)


def _sum_adam(chip_sums, recv, w, m, v, chip, name, transposed=False):
    R, C = chip_sums.shape[1:]
    NR = recv.shape[0]
    tr = _tile(R, 256, 16)

    def body(chip_ref, own_ref, r_ref, w_ref, m_ref, v_ref, g_ref, d_ref, m2_ref, v2_ref):
        del chip_ref
        g = own_ref[0].astype(F32)
        for j in range(NR):
            g = g + r_ref[j].astype(F32)
        if transposed:
            g = g.T
        g_ref[...] = g
        d_ref[...], m2_ref[...], v2_ref[...] = _adam_math(w_ref[...], g, m_ref[...], v_ref[...])

    if transposed:
        spec = lambda: pl.BlockSpec((C, tr), lambda i, chip_ref: (0, i))
        shape = (C, R)
    else:
        spec = lambda: pl.BlockSpec((tr, C), lambda i, chip_ref: (i, 0))
        shape = (R, C)
    assert w.shape == shape
    return pl.pallas_call(
        body, name=name,
        grid_spec=pltpu.PrefetchScalarGridSpec(
            num_scalar_prefetch=1, grid=(R // tr,),
            in_specs=[pl.BlockSpec((1, tr, C), lambda i, chip_ref: (chip_ref[0], i, 0)),
                      pl.BlockSpec((NR, tr, C), lambda i, chip_ref: (0, i, 0)), spec(), spec(), spec()],
            out_specs=[spec(), spec(), spec(), spec()]),
        out_shape=[S(shape, F32)] * 4, compiler_params=_cp(ARB),
    )(chip, chip_sums, recv, w, m, v)


def _adam_small(w, g, m, v):
    R, C = w.shape
    tr = _tile(R, 512, 8)

    def body(w_ref, g_ref, m_ref, v_ref, d_ref, m2_ref, v2_ref):
        d_ref[...], m2_ref[...], v2_ref[...] = _adam_math(w_ref[...], g_ref[...], m_ref[...], v_ref[...])

    spec = lambda: pl.BlockSpec((tr, C), lambda i: (i, 0))
    return pl.pallas_call(
        body, name="adam_small", grid=(R // tr,), in_specs=[spec()] * 4, out_specs=[spec()] * 3,
        out_shape=[S((R, C), F32)] * 3, compiler_params=_cp(ARB),
    )(w, g, m, v)


def _position():
    return lax.axis_index("x"), lax.axis_index("y"), lax.axis_index("c")


def _all_gather_weights(arr):
    R = arr.shape[0]
    half = R // 2
    assert half % 16 == 0

    def body(in_ref, out_ref, send_sems, recv_sems, local_sem):
        x, y, c = _position()
        me, sibling = (x, y, c), (x, y, 1 - c)
        xn, yn, diag = (1 - x, y), (x, 1 - y), (1 - x, 1 - y)
        upper, lower = pl.ds(0, half), pl.ds(half, half)

        def slot(p, rows=None):
            ref = out_ref.at[4 * p[0] + 2 * p[1] + p[2]]
            return ref if rows is None else ref.at[rows]

        def copy(kk, block, to, rows=None, src=None):
            return pltpu.make_async_remote_copy(
                src_ref=slot(block, rows) if src is None else src, dst_ref=slot(block, rows),
                send_sem=send_sems.at[kk], recv_sem=recv_sems.at[kk], device_id=to, device_id_type=MESH)

        mine = pltpu.make_async_copy(in_ref, slot(me), local_sem)
        mine.start()
        sent = [copy(0, me, sibling, src=in_ref), copy(1, me, (*xn, c), src=in_ref), copy(2, me, (*yn, c), src=in_ref)]
        for cp in sent:
            cp.start()

        def then(cps):
            for cp in cps:
                cp.start()
            sent.extend(cps)

        copy(1, (*xn, c), me).wait_recv()
        then([copy(5, (*xn, c), (*yn, c), rows=upper), copy(3, (*xn, c), sibling)])
        copy(2, (*yn, c), me).wait_recv()
        then([copy(6, (*yn, c), (*xn, c), rows=lower), copy(4, (*yn, c), sibling)])
        copy(5, (*diag, c), me, rows=upper).wait_recv()
        then([copy(7, (*diag, c), sibling, rows=upper)])
        copy(6, (*diag, c), me, rows=lower).wait_recv()
        then([copy(8, (*diag, c), sibling, rows=lower)])
        copy(0, sibling, me).wait_recv()
        copy(3, (*xn, 1 - c), me).wait_recv()
        copy(4, (*yn, 1 - c), me).wait_recv()
        copy(7, (*diag, 1 - c), me, rows=upper).wait_recv()
        copy(8, (*diag, 1 - c), me, rows=lower).wait_recv()
        for cp in sent:
            cp.wait_send()
        mine.wait()

    any_spec = pl.BlockSpec(memory_space=pl.ANY)
    return pl.pallas_call(
        body, name="all_gather_weights", in_specs=[any_spec], out_specs=any_spec,
        out_shape=S((N_DEV,) + arr.shape, arr.dtype),
        scratch_shapes=[pltpu.SemaphoreType.DMA((9,)), pltpu.SemaphoreType.DMA((9,)), pltpu.SemaphoreType.DMA],
    )(arr)


def _sibling_copies(ins, outs, send_sems, recv_sems):
    x, y, c = _position()
    return [pltpu.make_async_remote_copy(src_ref=ins[a].at[k, 1 - c], dst_ref=outs[a].at[k],
                                         send_sem=send_sems.at[a, k], recv_sem=recv_sems.at[a, k],
                                         device_id=(x, y, 1 - c), device_id_type=MESH)
            for a in range(len(ins)) for k in range(ins[a].shape[0])]


def _sibling_sems(arrs):
    shape = (max(len(arrs), 1), arrs[0].shape[0] if arrs else 1)
    return [pltpu.SemaphoreType.DMA(shape), pltpu.SemaphoreType.DMA(shape)]


def _chip_exchange_copies(ins, outs, send_sems, recv_sems):
    x, y, c = _position()
    chips = [(1 - x, y), (x, 1 - y), (1 - x, 1 - y)]
    return [pltpu.make_async_remote_copy(
        src_ref=ins[a].at[2 * qx + qy], dst_ref=outs[a].at[j], send_sem=send_sems.at[a, j],
        recv_sem=recv_sems.at[a, j], device_id=(qx, qy, c), device_id_type=MESH)
        for a in range(len(ins)) for j, (qx, qy) in enumerate(chips)]


def _broadcast_copies(srcs, dsts, send_sems, recv_sems):
    x, y, c = _position()
    me = 4 * x + 2 * y + c
    cps = []
    for a in range(len(srcs)):
        for k in range(1, N_DEV):
            peer = (1 - x if k & 4 else x, 1 - y if k & 2 else y, 1 - c if k & 1 else c)
            cps.append(pltpu.make_async_remote_copy(
                src_ref=srcs[a], dst_ref=dsts[a].at[me], send_sem=send_sems.at[a, k - 1],
                recv_sem=recv_sems.at[a, k - 1], device_id=peer, device_id_type=MESH))
    return me, cps


def _all_reduce_small(part):
    R, C = part.shape

    def body(p_ref, out_ref, gath_ref, send_sems, recv_sems):
        me, cps = _broadcast_copies([p_ref], [gath_ref], send_sems, recv_sems)
        gath_ref[me] = p_ref[...]
        for cp in cps:
            cp.start()
        for cp in cps:
            cp.wait()
        acc = gath_ref[0]
        for d in range(1, N_DEV):
            acc = acc + gath_ref[d]
        out_ref[...] = acc

    vm = pl.BlockSpec(memory_space=pltpu.VMEM)
    return pl.pallas_call(
        body, name="all_reduce_small", in_specs=[vm], out_specs=vm, out_shape=S((R, C), F32),
        scratch_shapes=[pltpu.VMEM((N_DEV, R, C), F32), pltpu.SemaphoreType.DMA((1, N_DEV - 1)),
                        pltpu.SemaphoreType.DMA((1, N_DEV - 1))],
    )(part)


def _pack(parts):
    rows = []
    for p in parts:
        f = p.reshape(-1).astype(F32)
        pad = (-f.shape[0]) % (8 * LANES)
        rows.append(jnp.pad(f, (0, pad)).reshape(-1, LANES))
    return jnp.concatenate(rows, axis=0)


def _unpack(buf, shapes):
    out, r = [], 0
    for shp in shapes:
        n = 1
        for s in shp:
            n *= s
        nr = -(-n // (8 * LANES)) * 8
        out.append(buf[r:r + nr].reshape(-1)[:n].reshape(shp))
        r += nr
    return out


def kernel(x, norm_w, w_in, conv_w, a_log, dt_bias, head_norm_w, sgu_ln_w, sgu_ln_b, w_spatial, b_spatial, w_out, final_norm_w, loss_target, m_norm_w, m_w_in, m_conv_w, m_a_log, m_dt_bias, m_head_norm_w, m_sgu_ln_w, m_sgu_ln_b, m_w_spatial, m_b_spatial, m_w_out, m_final_norm_w, v_norm_w, v_w_in, v_conv_w, v_a_log, v_dt_bias, v_head_norm_w, v_sgu_ln_w, v_sgu_ln_b, v_w_spatial, v_b_spatial, v_w_out, v_final_norm_w):
    T, DM = x.shape[1], x.shape[2]
    H, D = a_log.shape[1], head_norm_w.shape[1]
    G, P = w_spatial.shape[1], w_spatial.shape[2]
    AW, BW = H * D, G * P
    MIX = AW + BW
    WD = w_in.shape[2]
    IN = N_DEV * WD
    RO = w_out.shape[1]
    CW = conv_w.shape[2]
    sizes = (3 * AW, AW, H, H, BW, BW, BW)
    assert sum(sizes) == IN and 2 * H <= LANES and 3 * H <= 32 and N_DEV * RO == MIX and N_DEV * CW == 3 * AW
    offs = [0]
    for s in sizes:
        offs.append(offs[-1] + s)
    px, py, pc = _position()
    dev = 4 * px + 2 * py + pc
    chip = 2 * px + py

    x2, tgt = x[0], loss_target[0]

    g_win = _all_gather_weights(_cast_bf16_t(w_in[0].T, "cast_w_in"))
    w_main, w_ba = _relayout_w(g_win, offs[2], offs[4])
    alog_row = jnp.pad(a_log, ((0, 0), (H, LANES - 2 * H)))
    dtb_row = jnp.pad(dt_bias, ((0, 0), (H, LANES - 2 * H)))
    bs_t = b_spatial[0].T

    xn = _rms_xn(x2, norm_w)
    proj, ba, (g_wout, g_conv) = _in_proj(xn, w_main, w_ba, [_cast_bf16(w_out[0], "cast_w_out"), conv_w[0]])
    w_out_full = g_wout.reshape(MIX, DM)
    conv_full = g_conv.transpose(1, 0, 2).reshape(4, 3 * AW)
    q, k, v, c, gcol, grow = _prep_a_fwd(proj, ba, conv_full, alog_row, dtb_row, H, D)
    o, vnew, ssave, asave = _delta_fwd(q, k, v, gcol, grow, H, D)
    ocat = _mix_fwd(o, proj, head_norm_w, sgu_ln_w, sgu_ln_b, w_spatial[0], bs_t, H, D, G, P)
    dh, dh_bf, d_ocat, loss_acc, g_fnw = _out_proj_loss(ocat, w_out_full, x2, tgt, final_norm_w.reshape(1, DM))

    core_idx = jnp.reshape(pc, (1,)).astype(jnp.int32)
    chip_idx = jnp.reshape(chip, (1,)).astype(jnp.int32)
    g_wout_blocks = _grad_w(ocat, dh_bf, "grad_w_out").reshape(4, 2, RO, DM)
    (d_o, dmain, g_hnw, g_ln, g_wsp, g_bs_t), (sib_wout,) = _mix_bwd(
        d_ocat, o, proj, head_norm_w, sgu_ln_w, sgu_ln_b, w_spatial[0], bs_t, H, D, G, P, [g_wout_blocks])
    chip_wout = _pair_sum(g_wout_blocks, sib_wout, core_idx, "pair_sum_w_out")
    (dq, dk, dv, dgate, dpar), (recv_wout,) = _delta_bwd(
        q, k, v, gcol, grow, ba, vnew, ssave, asave, d_o, alog_row, dtb_row, H, D, [chip_wout])
    dmain, g_conv_part = _prep_a_bwd(dq, dk, dv, c, proj, conv_full, dmain, H, D)
    dba = dgate.astype(BF16)
    keep_win, sib_win = _grad_w_in(xn, dmain, dba, WD, offs[2], offs[4])
    chip_win = _pair_sum_plain(keep_win, sib_win, "pair_sum_w_in")
    small_shapes = [a_log.shape, dt_bias.shape, head_norm_w.shape, sgu_ln_w.shape, sgu_ln_b.shape,
                    w_spatial.shape, b_spatial.shape, final_norm_w.shape]
    parts = [dpar[0, H:2 * H], dpar[1, H:2 * H], g_hnw[0], g_ln[0], g_ln[1], g_wsp, g_bs_t[:, :G].T, g_fnw[0],
             g_conv_part[:4], loss_acc[0, :1]]
    grad_x, g_nw, small_gath, recv_win = _dx(dmain, dba, w_main, w_ba, x2, dh, norm_w, chip_win, _pack(parts), 4)
    red = _sum_slots(small_gath)
    grad_w_in, delta_w_in, new_m_w_in, new_v_w_in = _sum_adam(
        chip_win, recv_win, w_in[0].T, m_w_in[0].T, v_w_in[0].T, chip_idx, "sum_adam_w_in", transposed=True)
    grad_w_out, delta_w_out, new_m_w_out, new_v_w_out = _sum_adam(
        chip_wout, recv_wout, w_out[0], m_w_out[0], v_w_out[0], chip_idx, "sum_adam_w_out")
    red_nw = _all_reduce_small(_pack([g_nw[0]]))
    grads_small = _unpack(red_nw, [norm_w.shape]) + _unpack(red, small_shapes + [(4, 3 * AW), (1,)])
    loss = grads_small.pop()[0]
    g_conv_full = grads_small.pop()
    grad_conv = lax.dynamic_slice_in_dim(g_conv_full, dev * CW, CW, axis=1)[None]
    small_w = [norm_w, a_log, dt_bias, head_norm_w, sgu_ln_w, sgu_ln_b, w_spatial, b_spatial, final_norm_w, conv_w]
    small_m = [m_norm_w, m_a_log, m_dt_bias, m_head_norm_w, m_sgu_ln_w, m_sgu_ln_b, m_w_spatial, m_b_spatial,
               m_final_norm_w, m_conv_w]
    small_v = [v_norm_w, v_a_log, v_dt_bias, v_head_norm_w, v_sgu_ln_w, v_sgu_ln_b, v_w_spatial, v_b_spatial,
               v_final_norm_w, v_conv_w]
    small_g = grads_small + [grad_conv]
    shapes10 = [w.shape for w in small_w]
    d_p, m_p, v_p = _adam_small(_pack(small_w), _pack(small_g), _pack(small_m), _pack(small_v))
    d_s, m_s, v_s = _unpack(d_p, shapes10), _unpack(m_p, shapes10), _unpack(v_p, shapes10)

    def order(small, win, wout):
        return [small[0], win.T[None], small[9], small[1], small[2], small[3], small[4], small[5], small[6], small[7],
                wout[None], small[8]]

    grads = order(small_g, grad_w_in, grad_w_out)
    deltas = order(d_s, delta_w_in, delta_w_out)
    new_m = order(m_s, new_m_w_in, new_m_w_out)
    new_v = order(v_s, new_v_w_in, new_v_w_out)
    return (loss, grad_x[None], *grads, *deltas, *new_m, *new_v)
```

```python
import jax
import jax.numpy as jnp
from jax import lax
from jax.experimental import pallas as pl
from jax.experimental.pallas import tpu as pltpu

F32 = jnp.float32
BF16 = jnp.bfloat16
MXU = jnp.bfloat16
HI = lax.Precision.HIGHEST
EPS = 1e-6
CHUNK_A = 64
LANES = 128
MESH = pl.DeviceIdType.MESH
N_DEV = 8

ADAM_LR = 0.001
ADAM_B1 = 0.9
ADAM_B2 = 0.999
ADAM_EPS = 1e-08
ADAM_WD = 0.01
ADAM_STEP = 10

S = jax.ShapeDtypeStruct
ARB = "arbitrary"


def _cp(*sem, vmem_mib=56):
    return pltpu.CompilerParams(dimension_semantics=tuple(sem), vmem_limit_bytes=vmem_mib * 1024 * 1024)


def _tile(n, cap, mult):
    best = None
    t = mult
    while t <= min(n, cap):
        if n % t == 0:
            best = t
        t += mult
    return best if best is not None else n


def _mm(a, b):
    return jnp.dot(a.astype(MXU), b.astype(MXU), preferred_element_type=F32)


def _mm_nt(a, b):
    return lax.dot_general(a.astype(MXU), b.astype(MXU), (((1,), (1,)), ((), ())), preferred_element_type=F32)


def _mm_tn(a, b):
    return lax.dot_general(a.astype(MXU), b.astype(MXU), (((0,), (0,)), ((), ())), preferred_element_type=F32)


def _mmh(a, b):
    return jnp.dot(a, b, precision=HI, preferred_element_type=F32)


def _sigmoid(x):
    return 1.0 / (1.0 + jnp.exp(-x))


def _silu(x):
    return x * _sigmoid(x)


def _dsilu(x):
    s = _sigmoid(x)
    return s * (1.0 + x * (1.0 - s))


def _softplus(x):
    return jnp.maximum(x, 0.0) + jnp.log(1.0 + jnp.exp(-jnp.abs(x)))


def _pieces(wd, gate_lo, gate_hi, total):
    out = []
    for d in range(N_DEV):
        lo, hi = d * wd, (d + 1) * wd
        for dest, a, b, shift in (("main", 0, gate_lo, 0), ("gate", gate_lo, gate_hi, -gate_lo),
                                  ("main", gate_hi, total, gate_lo - gate_hi)):
            s0, s1 = max(lo, a), min(hi, b)
            if s0 < s1:
                out.append((d, s0 - lo, s1 - lo, dest, s0 + shift))
    return out


def _cast_bf16(a, name):
    R, C = a.shape
    tr = _tile(R, 256, 16)

    def body(a_ref, o_ref):
        o_ref[...] = a_ref[...].astype(BF16)

    spec = pl.BlockSpec((tr, C), lambda i: (i, 0))
    return pl.pallas_call(body, name=name, grid=(R // tr,), in_specs=[spec], out_specs=spec,
                          out_shape=S((R, C), BF16), compiler_params=_cp(ARB))(a)


def _cast_bf16_t(a_t, name):
    C, R = a_t.shape
    tr = _tile(R, 256, LANES)

    def body(a_ref, o_ref):
        o_ref[...] = a_ref[...].T.astype(BF16)

    return pl.pallas_call(body, name=name, grid=(R // tr,), in_specs=[pl.BlockSpec((C, tr), lambda i: (0, i))],
                          out_specs=pl.BlockSpec((tr, C), lambda i: (i, 0)),
                          out_shape=S((R, C), BF16), compiler_params=_cp(ARB))(a_t)


def _relayout_w(g_win, gate_lo, gate_hi):
    _, DM, WD = g_win.shape
    total = N_DEV * WD
    NM = total - (gate_hi - gate_lo)
    tr = _tile(DM, 256, 16)
    plan = _pieces(WD, gate_lo, gate_hi, total)

    def body(g_ref, main_ref, gate_ref):
        gate_ref[...] = jnp.zeros_like(gate_ref)
        for d, s0, s1, dest, c0 in plan:
            dst = main_ref if dest == "main" else gate_ref
            dst[:, c0:c0 + (s1 - s0)] = g_ref[d, :, s0:s1]

    return pl.pallas_call(
        body, name="relayout_w", grid=(DM // tr,),
        in_specs=[pl.BlockSpec((N_DEV, tr, WD), lambda i: (0, i, 0))],
        out_specs=[pl.BlockSpec((tr, NM), lambda i: (i, 0)), pl.BlockSpec((tr, LANES), lambda i: (i, 0))],
        out_shape=[S((DM, NM), g_win.dtype), S((DM, LANES), g_win.dtype)],
        compiler_params=_cp(ARB),
    )(g_win)


def _rms_xn(x, norm_w):
    T, DM = x.shape
    tm = _tile(T, 512, 16)

    def body(x_ref, nw_ref, o_ref):
        xv = x_ref[...]
        r = lax.rsqrt(jnp.mean(xv * xv, axis=-1, keepdims=True) + EPS)
        o_ref[...] = (xv * r * nw_ref[...]).astype(BF16)

    return pl.pallas_call(
        body, name="rms_xn", grid=(T // tm,),
        in_specs=[pl.BlockSpec((tm, DM), lambda i: (i, 0)), pl.BlockSpec((1, DM), lambda i: (0, 0))],
        out_specs=pl.BlockSpec((tm, DM), lambda i: (i, 0)),
        out_shape=S((T, DM), BF16), compiler_params=_cp(ARB),
    )(x, norm_w)


def _in_proj(xn, w_main, w_ba, shards):
    T, DM = xn.shape
    NM = w_main.shape[1]
    tm = _tile(T, 2048, 16)
    tn = _tile(NM, 1024, LANES)
    ni, nj = T // tm, NM // tn
    ns = len(shards)

    def body(xn_ref, w_ref, wba_ref, *rest):
        srcs = rest[:ns]
        proj_ref, ba_ref = rest[ns:ns + 2]
        gath = rest[ns + 2:2 * ns + 2]
        send_sems, recv_sems, local_sems = rest[2 * ns + 2:]
        i = pl.program_id(0)
        j = pl.program_id(1)
        me, cps = _broadcast_copies(srcs, gath, send_sems, recv_sems)
        cps = cps + [pltpu.make_async_copy(srcs[a], gath[a].at[me], local_sems.at[a]) for a in range(ns)]

        @pl.when((i == 0) & (j == 0))
        def _():
            for cp in cps:
                cp.start()

        @pl.when(j == 0)
        def _():
            ba_ref[...] = jnp.dot(xn_ref[...].astype(MXU), wba_ref[...].astype(MXU), preferred_element_type=F32)

        proj_ref[...] = jnp.dot(xn_ref[...].astype(MXU), w_ref[...].astype(MXU), preferred_element_type=F32)

        @pl.when((i == ni - 1) & (j == nj - 1))
        def _():
            for cp in cps:
                cp.wait()

    any_spec = pl.BlockSpec(memory_space=pl.ANY)
    res = pl.pallas_call(
        body, name="in_proj", grid=(ni, nj),
        in_specs=[pl.BlockSpec((tm, DM), lambda i, j: (i, 0)),
                  pl.BlockSpec((DM, tn), lambda i, j: (0, j)),
                  pl.BlockSpec((DM, LANES), lambda i, j: (0, 0))] + [any_spec] * ns,
        out_specs=[pl.BlockSpec((tm, tn), lambda i, j: (i, j)),
                   pl.BlockSpec((tm, LANES), lambda i, j: (i, 0))] + [any_spec] * ns,
        out_shape=[S((T, NM), F32), S((T, LANES), F32)] + [S((N_DEV,) + a.shape, a.dtype) for a in shards],
        scratch_shapes=[pltpu.SemaphoreType.DMA((ns, N_DEV - 1)), pltpu.SemaphoreType.DMA((ns, N_DEV - 1)),
                        pltpu.SemaphoreType.DMA((ns,))],
        compiler_params=_cp(ARB, ARB, vmem_mib=58),
    )(xn, w_main, w_ba, *shards)
    return res[0], res[1], res[2:]


def _prep_a_fwd(proj, ba, conv_w, alog_row, dtb_row, H, D):
    T = proj.shape[0]
    AW = H * D
    C3 = 3 * AW
    tb = _tile(T, 256, CHUNK_A)
    nch = tb // CHUNK_A
    nblk = T // tb
    scale = float(D) ** -0.5

    def body(x_ref, halo_ref, ba_ref, cw_ref, al_ref, dt_ref, q_ref, k_ref, v_ref, c_ref, gcol_ref, grow_ref):
        i = pl.program_id(0)
        xv = x_ref[...]
        halo = halo_ref[...] * (i > 0).astype(F32)
        xp = jnp.concatenate([halo, xv], axis=0)
        cw = cw_ref[...]
        c = cw[0:1, :] * xp[5:5 + tb]
        for j in range(1, 4):
            c = c + cw[j:j + 1, :] * xp[5 + j:5 + j + tb]
        c_ref[...] = c
        a = _silu(c)
        for h in range(H):
            qh = a[:, h * D:(h + 1) * D]
            kh = a[:, AW + h * D:AW + (h + 1) * D]
            qr = lax.rsqrt(jnp.sum(qh * qh, axis=-1, keepdims=True) + EPS)
            kr = lax.rsqrt(jnp.sum(kh * kh, axis=-1, keepdims=True) + EPS)
            q_ref[:, h * D:(h + 1) * D] = qh * (qr * scale)
            k_ref[:, h * D:(h + 1) * D] = kh * kr
        v_ref[...] = a[:, 2 * AW:]

        bav = ba_ref[...]
        lane = lax.broadcasted_iota(jnp.int32, (tb, LANES), 1)
        beta = _sigmoid(bav)
        g = -jnp.exp(al_ref[...]) * _softplus(bav + dt_ref[...])
        gates = jnp.where(lane < H, beta, jnp.where(lane < 2 * H, g, 0.0))
        ri = lax.broadcasted_iota(jnp.int32, (CHUNK_A, CHUNK_A), 0)
        ci = lax.broadcasted_iota(jnp.int32, (CHUNK_A, CHUNK_A), 1)
        tri = (ri >= ci).astype(F32)
        lane_c = lax.broadcasted_iota(jnp.int32, (CHUNK_A, LANES), 1)
        for cc in range(nch):
            gch = gates[cc * CHUNK_A:(cc + 1) * CHUNK_A]
            gc = pltpu.roll(_mmh(tri, gch), H, 1)
            full = jnp.where(lane_c < 2 * H, gch, jnp.where(lane_c < 3 * H, gc, 0.0))
            gcol_ref[cc * CHUNK_A:(cc + 1) * CHUNK_A, :] = full
            grow_ref[cc] = full.T[0:32, :]

    return pl.pallas_call(
        body, name="prep_a_fwd", grid=(nblk,),
        in_specs=[pl.BlockSpec((tb, C3), lambda i: (i, 0)),
                  pl.BlockSpec((8, C3), lambda i: (jnp.maximum(i * (tb // 8) - 1, 0), 0)),
                  pl.BlockSpec((tb, LANES), lambda i: (i, 0)),
                  pl.BlockSpec((4, C3), lambda i: (0, 0)),
                  pl.BlockSpec((1, LANES), lambda i: (0, 0)),
                  pl.BlockSpec((1, LANES), lambda i: (0, 0))],
        out_specs=[pl.BlockSpec((tb, AW), lambda i: (i, 0)),
                   pl.BlockSpec((tb, AW), lambda i: (i, 0)),
                   pl.BlockSpec((tb, AW), lambda i: (i, 0)),
                   pl.BlockSpec((tb, C3), lambda i: (i, 0)),
                   pl.BlockSpec((tb, LANES), lambda i: (i, 0)),
                   pl.BlockSpec((nch, 32, CHUNK_A), lambda i: (i, 0, 0))],
        out_shape=[S((T, AW), F32), S((T, AW), F32), S((T, AW), F32), S((T, C3), F32),
                   S((T, LANES), F32), S((T // CHUNK_A, 32, CHUNK_A), F32)],
        compiler_params=_cp(ARB),
    )(proj, proj, ba, conv_w, alog_row, dtb_row)


_NN = (((1,), (0,)), ((), ()))
_TN = (((0,), (0,)), ((), ()))


def _split(a):
    hi = a.astype(BF16)
    return hi, (a - hi.astype(F32)).astype(BF16)


def _mm3(a, b, dims=_NN):
    ah, al = a if isinstance(a, tuple) else _split(a)
    bh, bl = b if isinstance(b, tuple) else _split(b)
    dg = lambda p, r: lax.dot_general(p, r, dims, preferred_element_type=F32)
    return dg(ah, bh) + (dg(ah, bl) + dg(al, bh))


def _interleave(gens):
    gens = list(gens)
    while gens:
        alive = []
        for g in gens:
            try:
                next(g)
                alive.append(g)
            except StopIteration:
                pass
        gens = alive


def _chunk_terms(q, k, v, gcolv, growv, h, H):
    C = CHUNK_A
    beta_c = gcolv[:, h:h + 1]
    g_c = gcolv[:, H + h:H + h + 1]
    gc_c = gcolv[:, 2 * H + h:2 * H + h + 1]
    gc_r = growv[2 * H + h:2 * H + h + 1, :]
    ri = lax.broadcasted_iota(jnp.int32, (C, C), 0)
    ci = lax.broadcasted_iota(jnp.int32, (C, C), 1)
    incl = ri >= ci
    strict = ri > ci
    kb = k * beta_c
    vb = v * beta_c
    p_raw = _mm_nt(kb, k)
    qk_raw = _mm_nt(q, k)
    gam = jnp.where(incl, jnp.exp(jnp.where(incl, gc_c - gc_r, 0.0)), 0.0)
    e_c = jnp.exp(gc_c)
    gl = gc_r[:, C - 1:C]
    edec = jnp.exp(gl - gc_c)
    yield
    lmat = jnp.where(strict, p_raw * gam, 0.0)
    attn = jnp.where(incl, qk_raw * gam, 0.0)
    return dict(beta_c=beta_c, g_c=g_c, gc_c=gc_c, gc_r=gc_r, incl=incl, strict=strict, gam=gam, e_c=e_c,
                kb=kb, vb=vb, lmat=lmat, attn=attn, gl=gl, edec=edec, ri=ri, ci=ci)


def _inv_unit_lower(lmat):
    C = lmat.shape[0]
    ri = lax.broadcasted_iota(jnp.int32, (C, C), 0)
    ci = lax.broadcasted_iota(jnp.int32, (C, C), 1)
    eye = (ri == ci).astype(F32)
    x = -lmat
    a = eye + x
    n = 1
    while 2 * n < C:
        xs = _split(x)
        x = _mm3(xs, xs)
        yield
        a = a + _mm3(a, x)
        n *= 2
    yield
    return a


def _delta_fwd(q, k, v, gcol, grow, H, D):
    T = q.shape[0]
    C = CHUNK_A
    N = T // C
    AW = H * D
    CPS = 2 if N % 2 == 0 else 1

    def body(q_ref, k_ref, v_ref, gcol_ref, grow_ref, o_ref, vn_ref, ssave_ref, asave_ref, s_ref):
        @pl.when(pl.program_id(0) == 0)
        def _():
            s_ref[...] = jnp.zeros_like(s_ref)

        state = {(0, h): s_ref[h] for h in range(H)}

        def head(cc, h):
            rows = slice(cc * C, (cc + 1) * C)
            sl = slice(h * D, (h + 1) * D)
            qv, kv, vv = q_ref[rows, sl], k_ref[rows, sl], v_ref[rows, sl]
            t = yield from _chunk_terms(qv, kv, vv, gcol_ref[rows, :], grow_ref[cc], h, H)
            a = yield from _inv_unit_lower(t["lmat"])
            asave_ref[cc, h] = a
            while (cc, h) not in state:
                yield
            st = state[(cc, h)]
            ssave_ref[cc, h] = st
            ks = _mm(t["kb"] * t["e_c"], st)
            o_inter = _mm(qv * t["e_c"], st)
            yield
            v_new = _mm3(a, t["vb"] - ks)
            yield
            vn_ref[rows, sl] = v_new
            o_intra = _mm(t["attn"], v_new)
            s_upd = _mm_tn(kv * t["edec"], v_new)
            yield
            o_ref[rows, sl] = o_inter + o_intra
            state[(cc + 1, h)] = st * jnp.exp(t["gl"]) + s_upd

        _interleave(head(cc, h) for cc in range(CPS) for h in range(H))
        for h in range(H):
            s_ref[h] = state[(CPS, h)]

    blk = lambda: pl.BlockSpec((CPS * C, AW), lambda n: (n, 0))
    return pl.pallas_call(
        body, name="delta_fwd", grid=(N // CPS,),
        in_specs=[blk(), blk(), blk(),
                  pl.BlockSpec((CPS * C, LANES), lambda n: (n, 0)),
                  pl.BlockSpec((CPS, 32, C), lambda n: (n, 0, 0))],
        out_specs=[blk(), blk(),
                   pl.BlockSpec((CPS, H, D, D), lambda n: (n, 0, 0, 0)),
                   pl.BlockSpec((CPS, H, C, C), lambda n: (n, 0, 0, 0))],
        out_shape=[S((T, AW), F32), S((T, AW), F32), S((N, H, D, D), F32), S((N, H, C, C), F32)],
        scratch_shapes=[pltpu.VMEM((H, D, D), F32)],
        compiler_params=_cp(ARB),
    )(q, k, v, gcol, grow)


def _delta_bwd(q, k, v, gcol, grow, ba, vnew, ssave, asave, d_o, a_log, dt_bias, H, D, carry):
    T = q.shape[0]
    C = CHUNK_A
    N = T // C
    AW = H * D
    nc = len(carry)
    CPS = 2 if N % 2 == 0 else 1
    NS = N // CPS

    def body(al_ref, dt_ref, q_ref, k_ref, v_ref, gcol_ref, grow_ref, ba_ref, vn_ref, ss_ref, as_ref, do_ref, *rest):
        cins = rest[:nc]
        dq_ref, dk_ref, dv_ref, dgate_ref, dpar_ref = rest[nc:nc + 5]
        couts = rest[nc + 5:2 * nc + 5]
        ds_ref, csend, crecv = rest[2 * nc + 5:]
        ccps = _chip_exchange_copies(cins, couts, csend, crecv)

        @pl.when(pl.program_id(0) == 0)
        def _():
            ds_ref[...] = jnp.zeros_like(ds_ref)
            dpar_ref[...] = jnp.zeros_like(dpar_ref)
            for cp in ccps:
                cp.start()

        lane = lax.broadcasted_iota(jnp.int32, (C, LANES), 1)
        rowi = lax.broadcasted_iota(jnp.int32, (C, 1), 0)
        acc = {cc: jnp.zeros((C, LANES), F32) for cc in range(CPS)}
        state = {(0, h): ds_ref[h] for h in range(H)}

        def head(oi, h):
            cc = CPS - 1 - oi
            rows = slice(cc * C, (cc + 1) * C)
            sl = slice(h * D, (h + 1) * D)
            st = ss_ref[cc, h]
            a = as_ref[cc, h]
            qv, kv, vv, dov, v_new = q_ref[rows, sl], k_ref[rows, sl], v_ref[rows, sl], do_ref[rows, sl], vn_ref[rows, sl]
            t = yield from _chunk_terms(qv, kv, vv, gcol_ref[rows, :], grow_ref[cc], h, H)
            beta_c, e_c, gam, kb = t["beta_c"], t["e_c"], t["gam"], t["kb"]
            incl, strict, attn, lmat, edec = t["incl"], t["strict"], t["attn"], t["lmat"], t["edec"]
            kdec = kv * edec
            egl = jnp.exp(t["gl"])
            qe = qv * e_c
            ekb = kb * e_c

            t1 = _mm_nt(dov, st)
            ds_o = _mm_tn(qe, dov)
            dattn_raw = _mm_nt(dov, v_new)
            dv_new_o = _mm_tn(attn, dov)
            yield
            while (oi, h) not in state:
                yield
            ds_next = state[(oi, h)]
            dkdec = _mm_nt(v_new, ds_next)
            dv_new_s = _mm(kdec, ds_next)
            yield
            dgl = egl * jnp.sum(jnp.sum(st * ds_next, axis=1, keepdims=True), axis=0, keepdims=True)
            dk = edec * dkdec
            r = jnp.sum(dkdec * kdec, axis=1, keepdims=True)
            dgc = -r
            dgl = dgl + jnp.sum(r, axis=0, keepdims=True)
            dq = e_c * t1
            dgc = dgc + jnp.sum(t1 * qe, axis=1, keepdims=True)
            dattn = jnp.where(incl, dattn_raw, 0.0)
            dv_new = dv_new_s + dv_new_o
            dqm = dattn * gam
            z = dattn * attn
            dvb = _mm3(a, dv_new, _TN)
            dq_a = _mm(dqm, kv)
            dk_a = _mm_tn(dqm, qv)
            yield
            dq_ref[rows, sl] = dq + dq_a
            dv_ref[rows, sl] = beta_c * dvb
            ds_kb = _mm_tn(ekb, dvb)
            dekb_neg = _mm_nt(dvb, st)
            dl_neg = _mm_nt(dvb, v_new)
            yield
            state[(oi + 1, h)] = egl * ds_next + ds_o - ds_kb
            dekb = -dekb_neg
            dl = jnp.where(strict, -dl_neg, 0.0)
            dp = dl * gam
            z = z + dl * lmat
            dkb_p = _mm(dp, kv)
            dk_p = _mm_tn(dp, kb)
            dgc = dgc + jnp.sum(dekb * ekb, axis=1, keepdims=True)
            dgc = dgc + jnp.sum(z, axis=1, keepdims=True) - jnp.sum(z.T, axis=1, keepdims=True)
            dgc = dgc + jnp.where(rowi == C - 1, dgl, 0.0)
            yield
            dkb = dkb_p + e_c * dekb
            dk_ref[rows, sl] = dk + dk_a + dk_p + beta_c * dkb
            dbeta = jnp.sum(dkb * kv + dvb * vv, axis=1, keepdims=True)
            acc[cc] = acc[cc] + jnp.where(lane == h, dbeta, 0.0) + jnp.where(lane == H + h, dgc, 0.0)

        _interleave(head(oi, h) for oi in range(CPS) for h in range(H))
        for h in range(H):
            ds_ref[h] = state[(CPS, h)]
        ri = lax.broadcasted_iota(jnp.int32, (C, C), 0)
        ci = lax.broadcasted_iota(jnp.int32, (C, C), 1)
        upper = (ri <= ci).astype(F32)
        dal = jnp.zeros((1, LANES), F32)
        ddt = jnp.zeros((1, LANES), F32)
        for cc in range(CPS):
            rows = slice(cc * C, (cc + 1) * C)
            gates = gcol_ref[rows, :]
            dg_all = _mm3(upper, acc[cc])
            d_braw = acc[cc] * gates * (1.0 - gates)
            d_araw = dg_all * (-jnp.exp(al_ref[...])) * _sigmoid(ba_ref[rows, :] + dt_ref[...])
            dgate_ref[rows, :] = jnp.where(lane < H, d_braw, jnp.where(lane < 2 * H, d_araw, 0.0))
            dal = dal + jnp.sum(dg_all * gates, axis=0, keepdims=True)
            ddt = ddt + jnp.sum(d_araw, axis=0, keepdims=True)
        dpar_ref[0:1, :] += dal
        dpar_ref[1:2, :] += ddt

        @pl.when(pl.program_id(0) == NS - 1)
        def _():
            for cp in ccps:
                cp.wait()

    rev = lambda s: NS - 1 - s
    blk = lambda: pl.BlockSpec((CPS * C, AW), lambda s: (rev(s), 0))
    row = pl.BlockSpec((1, LANES), lambda s: (0, 0))
    any_spec = pl.BlockSpec(memory_space=pl.ANY)
    res = pl.pallas_call(
        body, name="delta_bwd", grid=(NS,),
        in_specs=[row, row, blk(), blk(), blk(),
                  pl.BlockSpec((CPS * C, LANES), lambda s: (rev(s), 0)),
                  pl.BlockSpec((CPS, 32, C), lambda s: (rev(s), 0, 0)),
                  pl.BlockSpec((CPS * C, LANES), lambda s: (rev(s), 0)),
                  blk(),
                  pl.BlockSpec((CPS, H, D, D), lambda s: (rev(s), 0, 0, 0)),
                  pl.BlockSpec((CPS, H, C, C), lambda s: (rev(s), 0, 0, 0)),
                  blk()] + [any_spec] * nc,
        out_specs=[blk(), blk(), blk(),
                   pl.BlockSpec((CPS * C, LANES), lambda s: (rev(s), 0)),
                   pl.BlockSpec((8, LANES), lambda s: (0, 0))] + [any_spec] * nc,
        out_shape=[S((T, AW), F32), S((T, AW), F32), S((T, AW), F32),
                   S((T, LANES), F32), S((8, LANES), F32)] + [S((3,) + a.shape[1:], a.dtype) for a in carry],
        scratch_shapes=[pltpu.VMEM((H, D, D), F32),
                        pltpu.SemaphoreType.DMA((max(nc, 1), 3)), pltpu.SemaphoreType.DMA((max(nc, 1), 3))],
        compiler_params=_cp(ARB),
    )(a_log, dt_bias, q, k, v, gcol, grow, ba, vnew, ssave, asave, d_o, *carry)
    return res[:5], res[5:]


def _ln_stats(xv):
    mu = jnp.mean(xv, axis=-1, keepdims=True)
    xc = xv - mu
    var = jnp.mean(xc * xc, axis=-1, keepdims=True)
    rstd = lax.rsqrt(var + EPS)
    return xc * rstd, rstd


def _mix_fwd(o, proj, head_norm_w, ln_w, ln_b, w_sp, bs_t, H, D, G, P):
    T = o.shape[0]
    AW, BW = H * D, G * P
    MIX = AW + BW
    nb = AW // BW if AW % BW == 0 else None
    assert nb == 1, "group widths must match the projection column blocks"
    cb = 3

    def body(o_ref, za_ref, ub_ref, vb_ref, zb_ref, hw_ref, lw_ref, lb_ref, w_ref, bs_ref, out_ref):
        hw = hw_ref[...]
        for h in range(H):
            sl = slice(h * D, (h + 1) * D)
            oh = o_ref[:, sl]
            rs = lax.rsqrt(jnp.mean(oh * oh, axis=-1, keepdims=True) + EPS)
            out_ref[:, sl] = (oh * rs * hw * _silu(za_ref[:, sl])).astype(BF16)
        xhat, _ = _ln_stats(vb_ref[...])
        vn = xhat * lw_ref[...] + lb_ref[...]
        ri = lax.broadcasted_iota(jnp.int32, (P, P), 0)
        ci = lax.broadcasted_iota(jnp.int32, (P, P), 1)
        bsv = bs_ref[...]
        for g in range(G):
            sl = slice(g * P, (g + 1) * P)
            wm = jnp.where(ri >= ci, w_ref[g], 0.0)
            s = _mm(wm, vn[:, sl]) + bsv[:, g:g + 1]
            out_ref[:, AW + g * P:AW + (g + 1) * P] = (ub_ref[:, sl] * s * _silu(zb_ref[:, sl])).astype(BF16)

    row = lambda w: pl.BlockSpec((1, w), lambda i: (0, 0))
    return pl.pallas_call(
        body, name="mix_fwd", grid=(T // P,),
        in_specs=[pl.BlockSpec((P, AW), lambda i: (i, 0)),
                  pl.BlockSpec((P, AW), lambda i: (i, cb)),
                  pl.BlockSpec((P, BW), lambda i: (i, cb + 1)),
                  pl.BlockSpec((P, BW), lambda i: (i, cb + 2)),
                  pl.BlockSpec((P, BW), lambda i: (i, cb + 3)),
                  row(D), row(BW), row(BW),
                  pl.BlockSpec((G, P, P), lambda i: (0, 0, 0)),
                  pl.BlockSpec((P, G), lambda i: (0, 0))],
        out_specs=pl.BlockSpec((P, MIX), lambda i: (i, 0)),
        out_shape=S((T, MIX), BF16),
        compiler_params=_cp(ARB),
    )(o, proj, proj, proj, proj, head_norm_w, ln_w, ln_b, w_sp, bs_t)


def _mix_bwd(d_ocat, o, proj, head_norm_w, ln_w, ln_b, w_sp, bs_t, H, D, G, P, carry):
    T = o.shape[0]
    AW, BW = H * D, G * P
    MIX = AW + BW
    cb = 3
    nc = len(carry)

    def body(dc_ref, o_ref, za_ref, ub_ref, vb_ref, zb_ref, hw_ref, lw_ref, lb_ref, w_ref, bs_ref, *rest):
        cins = rest[:nc]
        do_ref, dmain_ref, dhw_ref, dln_ref, dw_ref, dbs_ref = rest[nc:nc + 6]
        couts = rest[nc + 6:2 * nc + 6]
        dvn_ref, drest_ref, out_sems, csend, crecv = rest[2 * nc + 6:]
        i = pl.program_id(0)
        slot = lax.rem(i, 2)
        ccps = _sibling_copies(cins, couts, csend, crecv)

        def out_copy(step, s):
            return pltpu.make_async_copy(
                drest_ref.at[s], dmain_ref.at[pl.ds(step * P, P), pl.ds(cb * AW, AW + 3 * BW)], out_sems.at[s])

        @pl.when(i == 0)
        def _():
            dhw_ref[...] = jnp.zeros_like(dhw_ref)
            dln_ref[...] = jnp.zeros_like(dln_ref)
            dw_ref[...] = jnp.zeros_like(dw_ref)
            dbs_ref[...] = jnp.zeros_like(dbs_ref)
            for cp in ccps:
                cp.start()

        @pl.when(i >= 2)
        def _():
            out_copy(i - 2, slot).wait()

        hw = hw_ref[...]
        dhw = jnp.zeros((1, D), F32)
        for h in range(H):
            sl = slice(h * D, (h + 1) * D)
            oh = o_ref[:, sl]
            za = za_ref[:, sl]
            doa = dc_ref[:, sl]
            rs = lax.rsqrt(jnp.mean(oh * oh, axis=-1, keepdims=True) + EPS)
            xh = oh * rs
            d_on = doa * _silu(za)
            drest_ref[slot, :, sl] = (doa * (xh * hw) * _dsilu(za)).astype(BF16)
            dhw = dhw + jnp.sum(d_on * xh, axis=0, keepdims=True)
            dxh = d_on * hw
            do_ref[:, sl] = rs * (dxh - xh * jnp.mean(dxh * xh, axis=-1, keepdims=True))
        dhw_ref[0:1, :] += dhw

        xhat, rstd = _ln_stats(vb_ref[...])
        lw = lw_ref[...]
        vn = xhat * lw + lb_ref[...]
        ri = lax.broadcasted_iota(jnp.int32, (P, P), 0)
        ci = lax.broadcasted_iota(jnp.int32, (P, P), 1)
        lane = lax.broadcasted_iota(jnp.int32, (P, LANES), 1)
        bsv = bs_ref[...]
        dbs = jnp.zeros((P, LANES), F32)
        for g in range(G):
            sl = slice(g * P, (g + 1) * P)
            wm = jnp.where(ri >= ci, w_ref[g], 0.0)
            vng = vn[:, sl]
            s = _mm(wm, vng) + bsv[:, g:g + 1]
            dob = dc_ref[:, AW + g * P:AW + (g + 1) * P]
            ub = ub_ref[:, sl]
            zb = zb_ref[:, sl]
            szb = _silu(zb)
            drest_ref[slot, :, AW + g * P:AW + (g + 1) * P] = (dob * s * szb).astype(BF16)
            drest_ref[slot, :, AW + 2 * BW + g * P:AW + 2 * BW + (g + 1) * P] = (
                dob * ub * s * _dsilu(zb)).astype(BF16)
            ds = dob * ub * szb
            dvn_ref[:, sl] = _mm_tn(wm, ds)
            dw_ref[g] += jnp.where(ri >= ci, _mm_nt(ds, vng), 0.0)
            dbs = dbs + jnp.where(lane == g, jnp.sum(ds, axis=1, keepdims=True), 0.0)
        dbs_ref[...] += dbs
        dvn = dvn_ref[...]
        dln_ref[0:1, :] += jnp.sum(dvn * xhat, axis=0, keepdims=True)
        dln_ref[1:2, :] += jnp.sum(dvn, axis=0, keepdims=True)
        dxh = dvn * lw
        dvb = rstd * (dxh - jnp.mean(dxh, axis=-1, keepdims=True) - xhat * jnp.mean(dxh * xhat, axis=-1, keepdims=True))
        drest_ref[slot, :, AW + BW:AW + 2 * BW] = dvb.astype(BF16)

        out_copy(i, slot).start()

        @pl.when(i == nstep - 1)
        def _():
            out_copy(i, slot).wait()
            if nstep > 1:
                out_copy(i - 1, 1 - slot).wait()
            for cp in ccps:
                cp.wait()

    nstep = T // P
    row = lambda w: pl.BlockSpec((1, w), lambda i: (0, 0))
    any_spec = pl.BlockSpec(memory_space=pl.ANY)
    res = pl.pallas_call(
        body, name="mix_bwd", grid=(nstep,),
        in_specs=[pl.BlockSpec((P, MIX), lambda i: (i, 0)),
                  pl.BlockSpec((P, AW), lambda i: (i, 0)),
                  pl.BlockSpec((P, AW), lambda i: (i, cb)),
                  pl.BlockSpec((P, BW), lambda i: (i, cb + 1)),
                  pl.BlockSpec((P, BW), lambda i: (i, cb + 2)),
                  pl.BlockSpec((P, BW), lambda i: (i, cb + 3)),
                  row(D), row(BW), row(BW),
                  pl.BlockSpec((G, P, P), lambda i: (0, 0, 0)),
                  pl.BlockSpec((P, G), lambda i: (0, 0))] + [any_spec] * nc,
        out_specs=[pl.BlockSpec((P, AW), lambda i: (i, 0)),
                   any_spec,
                   pl.BlockSpec((8, D), lambda i: (0, 0)),
                   pl.BlockSpec((8, BW), lambda i: (0, 0)),
                   pl.BlockSpec((G, P, P), lambda i: (0, 0, 0)),
                   pl.BlockSpec((P, LANES), lambda i: (0, 0))] + [any_spec] * nc,
        out_shape=[S((T, AW), F32), S((T, cb * AW + AW + 3 * BW), BF16), S((8, D), F32), S((8, BW), F32),
                   S((G, P, P), F32), S((P, LANES), F32)] + [S(a.shape[:1] + a.shape[2:], a.dtype) for a in carry],
        scratch_shapes=[pltpu.VMEM((P, BW), F32), pltpu.VMEM((2, P, AW + 3 * BW), BF16),
                        pltpu.SemaphoreType.DMA((2,))] + _sibling_sems(carry),
        compiler_params=_cp(ARB),
    )(d_ocat, o, proj, proj, proj, proj, head_norm_w, ln_w, ln_b, w_sp, bs_t, *carry)
    return res[:6], res[6:]


def _out_proj_loss(ocat, w_out, x, target, fnw):
    T, MIX = ocat.shape
    DM = x.shape[1]
    tm = _tile(T, 256, 8)

    def body(oc_ref, w_ref, x_ref, t_ref, fw_ref, dh_ref, dhb_ref, doc_ref, loss_ref, gfw_ref):
        @pl.when(pl.program_id(0) == 0)
        def _():
            loss_ref[...] = jnp.zeros_like(loss_ref)
            gfw_ref[...] = jnp.zeros_like(gfw_ref)

        wv = w_ref[...]
        hh = x_ref[...] + jnp.dot(oc_ref[...].astype(MXU), wv.astype(MXU), preferred_element_type=F32)
        rs = lax.rsqrt(jnp.mean(hh * hh, axis=-1, keepdims=True) + EPS)
        hn = hh * rs
        fw = fw_ref[...]
        e = hn * fw - t_ref[...]
        row_loss = 0.5 * jnp.mean(e * e, axis=-1, keepdims=True)
        loss_ref[...] += jnp.sum(row_loss, axis=0, keepdims=True)
        dy = e * (1.0 / DM)
        gfw_ref[0:1, :] += jnp.sum(dy * hn, axis=0, keepdims=True)
        dhn = dy * fw
        dh = rs * (dhn - hn * jnp.mean(dhn * hn, axis=-1, keepdims=True))
        dh_ref[...] = dh
        dhb = dh.astype(BF16)
        dhb_ref[...] = dhb
        doc_ref[...] = _mm_nt(dhb, wv)

    return pl.pallas_call(
        body, name="out_proj_loss", grid=(T // tm,),
        in_specs=[pl.BlockSpec((tm, MIX), lambda i: (i, 0)),
                  pl.BlockSpec((MIX, DM), lambda i: (0, 0)),
                  pl.BlockSpec((tm, DM), lambda i: (i, 0)),
                  pl.BlockSpec((tm, DM), lambda i: (i, 0)),
                  pl.BlockSpec((1, DM), lambda i: (0, 0))],
        out_specs=[pl.BlockSpec((tm, DM), lambda i: (i, 0)),
                   pl.BlockSpec((tm, DM), lambda i: (i, 0)),
                   pl.BlockSpec((tm, MIX), lambda i: (i, 0)),
                   pl.BlockSpec((8, LANES), lambda i: (0, 0)),
                   pl.BlockSpec((8, DM), lambda i: (0, 0))],
        out_shape=[S((T, DM), F32), S((T, DM), BF16), S((T, MIX), F32), S((8, LANES), F32), S((8, DM), F32)],
        compiler_params=_cp(ARB),
    )(ocat, w_out, x, target, fnw)


def _grad_w(lhs, rhs, name):
    T, A = lhs.shape
    B = rhs.shape[1]
    ta = _tile(A, 512, LANES)
    tk = _tile(T, 1024, 16)
    nk = T // tk

    def body(l_ref, r_ref, out_ref, acc_ref):
        k = pl.program_id(1)
        part = _mm_tn(l_ref[...], r_ref[...])

        @pl.when(k == 0)
        def _():
            acc_ref[...] = part

        @pl.when(k > 0)
        def _():
            acc_ref[...] += part

        @pl.when(k == nk - 1)
        def _():
            out_ref[...] = acc_ref[...].astype(BF16)

    return pl.pallas_call(
        body, name=name, grid=(A // ta, nk),
        in_specs=[pl.BlockSpec((tk, ta), lambda i, k: (k, i)),
                  pl.BlockSpec((tk, B), lambda i, k: (k, 0))],
        out_specs=pl.BlockSpec((ta, B), lambda i, k: (i, 0)),
        out_shape=S((A, B), BF16),
        scratch_shapes=[pltpu.VMEM((ta, B), F32)],
        compiler_params=_cp(ARB, ARB),
    )(lhs, rhs)


def _grad_w_in(xn, dmain, dba, WD, gate_lo, gate_hi):
    T, DM = xn.shape
    NM = dmain.shape[1]
    tn = _tile(NM, 1024, LANES)
    tk = _tile(T, 2048, 16)
    nj, nk = NM // tn, T // tk
    ND = N_DEV
    tiles = [[] for _ in range(nj)]
    first_tile, last_tile = {}, {}
    for d, s0, s1, dest, c0 in _pieces(WD, gate_lo, gate_hi, ND * WD):
        if dest != "main":
            continue
        while s0 < s1:
            jj = c0 // tn
            w = min(s1 - s0, (jj + 1) * tn - c0)
            tiles[jj].append((d, s0, w, "main", c0 - jj * tn))
            first_tile.setdefault(d, jj)
            last_tile[d] = jj
            s0, c0 = s0 + w, c0 + w
    for d, s0, s1, dest, c0 in _pieces(WD, gate_lo, gate_hi, ND * WD):
        if dest == "gate":
            tiles[first_tile[d]].append((d, s0, s1 - s0, "gate", c0))
    assert sorted(first_tile) == list(range(ND)) and all(last_tile[d] <= first_tile[d + 2] for d in range(ND - 2))

    def body(xn_ref, dm_ref, dba_ref, keep_ref, recv_ref, acc_ref, gate_ref, buf_ref, lsem, ssem, rsem):
        j = pl.program_id(0)
        k = pl.program_id(1)
        px, py, pc = _position()

        @pl.when(k == 0)
        def _():
            acc_ref[...] = jnp.zeros_like(acc_ref)

        @pl.when((j == 0) & (k == 0))
        def _():
            gate_ref[...] = jnp.zeros_like(gate_ref)

        xv = xn_ref[...]
        acc_ref[...] += _mm_tn(xv, dm_ref[...])

        @pl.when(j == 0)
        def _():
            gate_ref[...] += _mm_tn(xv, dba_ref[...])

        def local(d):
            return pltpu.make_async_copy(buf_ref.at[d % 2], keep_ref.at[d // 2], lsem.at[d // 2])

        def remote(d):
            return pltpu.make_async_remote_copy(
                src_ref=buf_ref.at[d % 2], dst_ref=recv_ref.at[d // 2], send_sem=ssem.at[d // 2],
                recv_sem=rsem.at[d // 2], device_id=(px, py, 1 - pc), device_id_type=MESH)

        def leave(d, start):
            @pl.when(pc == d % 2)
            def _():
                local(d).start() if start else local(d).wait()

            @pl.when(pc != d % 2)
            def _():
                remote(d).start() if start else remote(d).wait_send()

        def emit(jj):
            shards = sorted({p[0] for p in tiles[jj]})
            for d in shards:
                if first_tile[d] == jj and d >= 2:
                    leave(d - 2, False)
                for dd, s0, w, src, c0 in tiles[jj]:
                    if dd == d:
                        ref = acc_ref if src == "main" else gate_ref
                        buf_ref[d % 2, :, s0:s0 + w] = ref[:, c0:c0 + w].astype(BF16)
                if last_tile[d] == jj:
                    leave(d, True)
            if jj == nj - 1:
                for d in (ND - 2, ND - 1):
                    leave(d, False)
                for q in range(ND // 2):
                    remote(2 * q).wait_recv()

        for jj in range(nj):
            @pl.when((j == jj) & (k == nk - 1))
            def _(jj=jj):
                emit(jj)

    any_spec = pl.BlockSpec(memory_space=pl.ANY)
    return pl.pallas_call(
        body, name="grad_w_in", grid=(nj, nk),
        in_specs=[pl.BlockSpec((tk, DM), lambda j, k: (k, 0)),
                  pl.BlockSpec((tk, tn), lambda j, k: (k, j)),
                  pl.BlockSpec((tk, LANES), lambda j, k: (k, 0))],
        out_specs=[any_spec, any_spec],
        out_shape=[S((ND // 2, DM, WD), BF16), S((ND // 2, DM, WD), BF16)],
        scratch_shapes=[pltpu.VMEM((DM, tn), F32), pltpu.VMEM((DM, LANES), F32), pltpu.VMEM((2, DM, WD), BF16),
                        pltpu.SemaphoreType.DMA((ND // 2,)), pltpu.SemaphoreType.DMA((ND // 2,)),
                        pltpu.SemaphoreType.DMA((ND // 2,))],
        compiler_params=_cp(ARB, ARB),
    )(xn, dmain, dba)


def _pair_sum_plain(a, b, name):
    K, R, C = a.shape
    tr = _tile(R, 256, 16)

    def body(a_ref, b_ref, o_ref):
        o_ref[...] = (a_ref[...].astype(F32) + b_ref[...].astype(F32)).astype(BF16)

    spec = lambda: pl.BlockSpec((1, tr, C), lambda q, i: (q, i, 0))
    return pl.pallas_call(body, name=name, grid=(K, R // tr), in_specs=[spec(), spec()], out_specs=spec(),
                          out_shape=S((K, R, C), BF16), compiler_params=_cp(ARB, ARB))(a, b)


def _dx_rows(T):
    tm = _tile(T, 512, 8)
    return tm if T // tm >= 2 else T // 2


def _dx_part(name, dmain, dba, w_main, w_ba, x, dh, norm_w, blk0, nblk, prev, hbm_in, hbm_alias, hbm_new, make_copies):
    T, NM = dmain.shape
    DM = x.shape[1]
    tm = _dx_rows(T)
    tk = _tile(NM, 1024, LANES)
    nk = NM // tk
    n_in, n_al, n_new = len(hbm_in), len(hbm_alias), len(hbm_new)
    n_prev = 0 if prev is None else 2
    last_step = nblk * nk - 1

    def body(dm_ref, dba_ref, w_ref, wba_ref, x_ref, dh_ref, nw_ref, *rest):
        r = list(rest)
        gnw_prev_ref = r.pop(0) if n_prev else None
        if n_prev:
            r.pop(0)
        in_refs = [r.pop(0) for _ in range(n_in)]
        del r[:n_al]
        gx_ref, gnw_ref = r.pop(0), r.pop(0)
        alias_refs = [r.pop(0) for _ in range(n_al)]
        new_refs = [r.pop(0) for _ in range(n_new)]
        acc_ref, send_sems, recv_sems = r
        i = pl.program_id(0)
        k = pl.program_id(1)
        step = i * nk + k
        cps = make_copies(in_refs, alias_refs, new_refs, send_sems, recv_sems)

        @pl.when(step == 0)
        def _():
            gnw_ref[...] = gnw_prev_ref[...] if n_prev else jnp.zeros_like(gnw_ref)
            for cp in cps:
                cp.start()

        @pl.when(k == 0)
        def _():
            acc_ref[...] = _mm_nt(dba_ref[...], wba_ref[...])

        acc_ref[...] += _mm_nt(dm_ref[...], w_ref[...])

        @pl.when(k == nk - 1)
        def _():
            xv = x_ref[...]
            rs = lax.rsqrt(jnp.mean(xv * xv, axis=-1, keepdims=True) + EPS)
            xh = xv * rs
            dxn = acc_ref[...]
            gnw_ref[0:1, :] += jnp.sum(dxn * xh, axis=0, keepdims=True)
            dxh = dxn * nw_ref[...]
            gx_ref[...] = dh_ref[...] + rs * (dxh - xh * jnp.mean(dxh * xh, axis=-1, keepdims=True))

        @pl.when(step == last_step)
        def _():
            for cp in cps:
                cp.wait()

    any_spec = pl.BlockSpec(memory_space=pl.ANY)
    prev_specs = [pl.BlockSpec((8, DM), lambda i, k: (0, 0)), any_spec] if n_prev else []
    prev_args = [prev[1], prev[0]] if n_prev else []
    aliases = {8: 0} if n_prev else {}
    for q in range(n_al):
        aliases[7 + n_prev + n_in + q] = 2 + q
    res = pl.pallas_call(
        body, name=name, grid=(nblk, nk),
        in_specs=[pl.BlockSpec((tm, tk), lambda i, k: (blk0 + i, k)),
                  pl.BlockSpec((tm, LANES), lambda i, k: (blk0 + i, 0)),
                  pl.BlockSpec((DM, tk), lambda i, k: (0, k)),
                  pl.BlockSpec((DM, LANES), lambda i, k: (0, 0)),
                  pl.BlockSpec((tm, DM), lambda i, k: (blk0 + i, 0)),
                  pl.BlockSpec((tm, DM), lambda i, k: (blk0 + i, 0)),
                  pl.BlockSpec((1, DM), lambda i, k: (0, 0))] + prev_specs + [any_spec] * (n_in + n_al),
        out_specs=[pl.BlockSpec((tm, DM), lambda i, k: (blk0 + i, 0)),
                   pl.BlockSpec((8, DM), lambda i, k: (0, 0))] + [any_spec] * (n_al + n_new),
        out_shape=[S((T, DM), F32), S((8, DM), F32)] + [S(a.shape, a.dtype) for a in hbm_alias] + list(hbm_new),
        scratch_shapes=[pltpu.VMEM((tm, DM), F32), pltpu.SemaphoreType.DMA((10,)), pltpu.SemaphoreType.DMA((10,))],
        input_output_aliases=aliases,
        compiler_params=_cp(ARB, ARB),
    )(dmain, dba, w_main, w_ba, x, dh, norm_w, *prev_args, *hbm_in, *hbm_alias)
    return (res[0], res[1]), res[2:2 + n_al], res[2 + n_al:]


def _remote(kk, src, dst, to, send_sems, recv_sems):
    return pltpu.make_async_remote_copy(src_ref=src, dst_ref=dst, send_sem=send_sems.at[kk], recv_sem=recv_sems.at[kk],
                                        device_id=to, device_id_type=MESH)


def _dx(dmain, dba, w_main, w_ba, x, dh, norm_w, chip_sum, small, cut):
    R, C = chip_sum.shape[1:]
    half = R // 2
    assert half % 16 == 0
    T = x.shape[0]
    ni = T // _dx_rows(T)
    cut = max(1, min(cut, ni - 1))
    upper, lower = pl.ds(0, half), pl.ds(half, half)

    def nbrs():
        px, py, pc = _position()
        return (px, py), (1 - px, py, pc), (px, 1 - py, pc)

    def phase1(ins, als, news, ss, rs):
        (px, py), xn, yn = nbrs()
        cs = ins[0]
        recv, stage = news
        bx, by, bd = cs.at[2 * (1 - px) + py], cs.at[2 * px + (1 - py)], cs.at[2 * (1 - px) + (1 - py)]
        return [_remote(0, bx.at[upper], recv.at[0].at[upper], xn, ss, rs),
                _remote(1, by.at[lower], recv.at[1].at[lower], yn, ss, rs),
                _remote(2, bd.at[upper], stage.at[0], xn, ss, rs),
                _remote(3, bd.at[lower], stage.at[1], yn, ss, rs)]

    def phase2(ins, als, news, ss, rs):
        (px, py), xn, yn = nbrs()
        comb, small_ref = ins
        recv, gath = als[0], news[0]
        me, small_cps = _broadcast_copies([small_ref], [gath], _Sem2(ss, 2), _Sem2(rs, 2))
        return ([_remote(0, comb.at[0], recv.at[1].at[upper], yn, ss, rs),
                 _remote(1, comb.at[1], recv.at[0].at[lower], xn, ss, rs)] + small_cps
                + [pltpu.make_async_copy(small_ref, gath.at[me], ss.at[9])])

    (gx, gnw), _, (recv, stage) = _dx_part(
        "dx_a", dmain, dba, w_main, w_ba, x, dh, norm_w, 0, cut, None, [chip_sum], [],
        [S((2, R, C), chip_sum.dtype), S((2, half, C), chip_sum.dtype)], phase1)
    comb = _relay_add(chip_sum, stage)
    (gx, gnw), (recv,), (gath,) = _dx_part(
        "dx_b", dmain, dba, w_main, w_ba, x, dh, norm_w, cut, ni - cut, (gx, gnw), [comb, small], [recv],
        [S((N_DEV,) + small.shape, F32)], phase2)
    return gx, gnw, gath, recv


class _Sem2:
    def __init__(self, sems, lo):
        self.sems, self.lo = sems, lo

    @property
    def at(self):
        outer = self

        class _At:
            def __getitem__(self, idx):
                a, k = idx
                return outer.sems.at[outer.lo + k]
        return _At()


def _relay_add(chip_sum, stage):
    _, R, C = chip_sum.shape
    half = R // 2
    tr = _tile(half, 256, 16)
    nt = half // tr
    px, py, _ = _position()
    idx = jnp.stack([2 * px + (1 - py), 2 * (1 - px) + py]).astype(jnp.int32)

    def body(idx_ref, p_ref, s_ref, o_ref):
        del idx_ref
        o_ref[0] = (p_ref[0].astype(F32) + s_ref[0].astype(F32)).astype(BF16)

    return pl.pallas_call(
        body, name="relay_add",
        grid_spec=pltpu.PrefetchScalarGridSpec(
            num_scalar_prefetch=1, grid=(2, nt),
            in_specs=[pl.BlockSpec((1, tr, C), lambda s, i, idx_ref: (idx_ref[s], s * nt + i, 0)),
                      pl.BlockSpec((1, tr, C), lambda s, i, idx_ref: (s, i, 0))],
            out_specs=pl.BlockSpec((1, tr, C), lambda s, i, idx_ref: (s, i, 0))),
        out_shape=S((2, half, C), BF16), compiler_params=_cp(ARB, ARB),
    )(idx, chip_sum, stage)


def _sum_slots(gath):
    _, R, C = gath.shape
    tr = R if R <= 2048 else _tile(R, 512, 8)

    def body(g_ref, o_ref):
        tot = g_ref[0]
        for d in range(1, N_DEV):
            tot = tot + g_ref[d]
        o_ref[...] = tot

    return pl.pallas_call(
        body, name="sum_slots", grid=(R // tr,),
        in_specs=[pl.BlockSpec((N_DEV, tr, C), lambda i: (0, i, 0))],
        out_specs=pl.BlockSpec((tr, C), lambda i: (i, 0)),
        out_shape=S((R, C), F32), compiler_params=_cp(ARB),
    )(gath)


def _prep_a_bwd(dq, dk, dv, c, proj, conv_w, dmain, H, D):
    T = c.shape[0]
    AW = H * D
    C3 = 3 * AW
    tb = _tile(T, 256, 8)
    nblk = T // tb
    r8 = tb // 8
    scale = float(D) ** -0.5

    def body(dq_ref, dk_ref, dv_ref, c_ref, dqn_ref, dkn_ref, dvn_ref, cn_ref, x_ref, halo_ref, cw_ref, dmain_in_ref,
             dx_ref, gcw_ref, dc_ref):
        del dmain_in_ref
        i = pl.program_id(0)

        @pl.when(i == 0)
        def _():
            gcw_ref[...] = jnp.zeros_like(gcw_ref)

        def pointwise(rows, dq_r, dk_r, dv_r, c_r, keep):
            for h in range(H):
                for part, d_r, sc in ((0, dq_r, scale), (1, dk_r, 1.0)):
                    sl = slice(part * AW + h * D, part * AW + (h + 1) * D)
                    cv = c_r[:, sl]
                    raw = _silu(cv)
                    rs = lax.rsqrt(jnp.sum(raw * raw, axis=-1, keepdims=True) + EPS)
                    nrm = raw * rs
                    dn = d_r[:, h * D:(h + 1) * D] * sc
                    draw = rs * (dn - nrm * jnp.sum(dn * nrm, axis=-1, keepdims=True))
                    dc_ref[rows, sl] = draw * _dsilu(cv) * keep
            dc_ref[rows, 2 * AW:] = dv_r[...] * _dsilu(c_r[:, 2 * AW:]) * keep

        pointwise(slice(0, tb), dq_ref, dk_ref, dv_ref, c_ref, 1.0)
        pointwise(slice(tb, tb + 8), dqn_ref, dkn_ref, dvn_ref, cn_ref, (i < nblk - 1).astype(F32))

        cw = cw_ref[...]
        dcv = dc_ref[0:tb, :]
        dx = cw[3:4, :] * dcv
        for j in range(3):
            dx = dx + cw[j:j + 1, :] * dc_ref[3 - j:3 - j + tb, :]
        dx_ref[...] = dx.astype(BF16)
        halo = halo_ref[...] * (i > 0).astype(F32)
        xp = jnp.concatenate([halo, x_ref[...]], axis=0)
        for j in range(4):
            gcw_ref[j:j + 1, :] += jnp.sum(dcv * xp[5 + j:5 + j + tb], axis=0, keepdims=True)

    nxt = lambda i: (jnp.minimum((i + 1) * r8, T // 8 - 1), 0)
    return pl.pallas_call(
        body, name="prep_a_bwd", grid=(nblk,),
        in_specs=[pl.BlockSpec((tb, AW), lambda i: (i, 0)),
                  pl.BlockSpec((tb, AW), lambda i: (i, 0)),
                  pl.BlockSpec((tb, AW), lambda i: (i, 0)),
                  pl.BlockSpec((tb, C3), lambda i: (i, 0)),
                  pl.BlockSpec((8, AW), nxt), pl.BlockSpec((8, AW), nxt), pl.BlockSpec((8, AW), nxt),
                  pl.BlockSpec((8, C3), nxt),
                  pl.BlockSpec((tb, C3), lambda i: (i, 0)),
                  pl.BlockSpec((8, C3), lambda i: (jnp.maximum(i * r8 - 1, 0), 0)),
                  pl.BlockSpec((4, C3), lambda i: (0, 0)),
                  pl.BlockSpec(memory_space=pl.ANY)],
        out_specs=[pl.BlockSpec((tb, C3), lambda i: (i, 0)),
                   pl.BlockSpec((8, C3), lambda i: (0, 0))],
        out_shape=[S(dmain.shape, dmain.dtype), S((8, C3), F32)],
        scratch_shapes=[pltpu.VMEM((tb + 8, C3), F32)],
        input_output_aliases={11: 0},
        compiler_params=_cp(ARB),
    )(dq, dk, dv, c, dq, dk, dv, c, proj, proj, conv_w, dmain)


def _adam_math(w, g, m, v):
    m2 = ADAM_B1 * m + (1.0 - ADAM_B1) * g
    v2 = ADAM_B2 * v + (1.0 - ADAM_B2) * (g * g)
    m_hat = m2 / (1.0 - ADAM_B1 ** ADAM_STEP)
    v_hat = v2 / (1.0 - ADAM_B2 ** ADAM_STEP)
    delta = -ADAM_LR * (m_hat / (jnp.sqrt(v_hat) + ADAM_EPS) + ADAM_WD * w)
    return delta, m2, v2


def _pair_sum(blocks, recv, core, name):
    K, _, R, C = blocks.shape
    tr = _tile(R, 256, 16)

    def body(core_ref, a_ref, b_ref, o_ref):
        del core_ref
        o_ref[0] = (a_ref[0, 0].astype(F32) + b_ref[0].astype(F32)).astype(BF16)

    spec = lambda: pl.BlockSpec((1, tr, C), lambda k, i, core_ref: (k, i, 0))
    return pl.pallas_call(
        body, name=name,
        grid_spec=pltpu.PrefetchScalarGridSpec(
            num_scalar_prefetch=1, grid=(K, R // tr),
            in_specs=[pl.BlockSpec((1, 1, tr, C), lambda k, i, core_ref: (k, core_ref[0], i, 0)), spec()],
            out_specs=spec()),
        out_shape=S((K, R, C), BF16), compiler_params=_cp(ARB, ARB),
    )(core, blocks, recv)


def _sum_adam(chip_sums, recv, w, m, v, chip, name, transposed=False):
    R, C = chip_sums.shape[1:]
    NR = recv.shape[0]
    tr = _tile(R, 256, 16)

    def body(chip_ref, own_ref, r_ref, w_ref, m_ref, v_ref, g_ref, d_ref, m2_ref, v2_ref):
        del chip_ref
        g = own_ref[0].astype(F32)
        for j in range(NR):
            g = g + r_ref[j].astype(F32)
        if transposed:
            g = g.T
        g_ref[...] = g
        d_ref[...], m2_ref[...], v2_ref[...] = _adam_math(w_ref[...], g, m_ref[...], v_ref[...])

    if transposed:
        spec = lambda: pl.BlockSpec((C, tr), lambda i, chip_ref: (0, i))
        shape = (C, R)
    else:
        spec = lambda: pl.BlockSpec((tr, C), lambda i, chip_ref: (i, 0))
        shape = (R, C)
    assert w.shape == shape
    return pl.pallas_call(
        body, name=name,
        grid_spec=pltpu.PrefetchScalarGridSpec(
            num_scalar_prefetch=1, grid=(R // tr,),
            in_specs=[pl.BlockSpec((1, tr, C), lambda i, chip_ref: (chip_ref[0], i, 0)),
                      pl.BlockSpec((NR, tr, C), lambda i, chip_ref: (0, i, 0)), spec(), spec(), spec()],
            out_specs=[spec(), spec(), spec(), spec()]),
        out_shape=[S(shape, F32)] * 4, compiler_params=_cp(ARB),
    )(chip, chip_sums, recv, w, m, v)


def _adam_small(w, g, m, v):
    R, C = w.shape
    tr = _tile(R, 512, 8)

    def body(w_ref, g_ref, m_ref, v_ref, d_ref, m2_ref, v2_ref):
        d_ref[...], m2_ref[...], v2_ref[...] = _adam_math(w_ref[...], g_ref[...], m_ref[...], v_ref[...])

    spec = lambda: pl.BlockSpec((tr, C), lambda i: (i, 0))
    return pl.pallas_call(
        body, name="adam_small", grid=(R // tr,), in_specs=[spec()] * 4, out_specs=[spec()] * 3,
        out_shape=[S((R, C), F32)] * 3, compiler_params=_cp(ARB),
    )(w, g, m, v)


def _position():
    return lax.axis_index("x"), lax.axis_index("y"), lax.axis_index("c")


def _all_gather_weights(arr):
    R = arr.shape[0]
    half = R // 2
    assert half % 16 == 0

    def body(in_ref, out_ref, send_sems, recv_sems, local_sem):
        x, y, c = _position()
        me, sibling = (x, y, c), (x, y, 1 - c)
        xn, yn, diag = (1 - x, y), (x, 1 - y), (1 - x, 1 - y)
        upper, lower = pl.ds(0, half), pl.ds(half, half)

        def slot(p, rows=None):
            ref = out_ref.at[4 * p[0] + 2 * p[1] + p[2]]
            return ref if rows is None else ref.at[rows]

        def copy(kk, block, to, rows=None, src=None):
            return pltpu.make_async_remote_copy(
                src_ref=slot(block, rows) if src is None else src, dst_ref=slot(block, rows),
                send_sem=send_sems.at[kk], recv_sem=recv_sems.at[kk], device_id=to, device_id_type=MESH)

        mine = pltpu.make_async_copy(in_ref, slot(me), local_sem)
        mine.start()
        sent = [copy(0, me, sibling, src=in_ref), copy(1, me, (*xn, c), src=in_ref), copy(2, me, (*yn, c), src=in_ref)]
        for cp in sent:
            cp.start()

        def then(cps):
            for cp in cps:
                cp.start()
            sent.extend(cps)

        copy(1, (*xn, c), me).wait_recv()
        then([copy(5, (*xn, c), (*yn, c), rows=upper), copy(3, (*xn, c), sibling)])
        copy(2, (*yn, c), me).wait_recv()
        then([copy(6, (*yn, c), (*xn, c), rows=lower), copy(4, (*yn, c), sibling)])
        copy(5, (*diag, c), me, rows=upper).wait_recv()
        then([copy(7, (*diag, c), sibling, rows=upper)])
        copy(6, (*diag, c), me, rows=lower).wait_recv()
        then([copy(8, (*diag, c), sibling, rows=lower)])
        copy(0, sibling, me).wait_recv()
        copy(3, (*xn, 1 - c), me).wait_recv()
        copy(4, (*yn, 1 - c), me).wait_recv()
        copy(7, (*diag, 1 - c), me, rows=upper).wait_recv()
        copy(8, (*diag, 1 - c), me, rows=lower).wait_recv()
        for cp in sent:
            cp.wait_send()
        mine.wait()

    any_spec = pl.BlockSpec(memory_space=pl.ANY)
    return pl.pallas_call(
        body, name="all_gather_weights", in_specs=[any_spec], out_specs=any_spec,
        out_shape=S((N_DEV,) + arr.shape, arr.dtype),
        scratch_shapes=[pltpu.SemaphoreType.DMA((9,)), pltpu.SemaphoreType.DMA((9,)), pltpu.SemaphoreType.DMA],
    )(arr)


def _sibling_copies(ins, outs, send_sems, recv_sems):
    x, y, c = _position()
    return [pltpu.make_async_remote_copy(src_ref=ins[a].at[k, 1 - c], dst_ref=outs[a].at[k],
                                         send_sem=send_sems.at[a, k], recv_sem=recv_sems.at[a, k],
                                         device_id=(x, y, 1 - c), device_id_type=MESH)
            for a in range(len(ins)) for k in range(ins[a].shape[0])]


def _sibling_sems(arrs):
    shape = (max(len(arrs), 1), arrs[0].shape[0] if arrs else 1)
    return [pltpu.SemaphoreType.DMA(shape), pltpu.SemaphoreType.DMA(shape)]


def _chip_exchange_copies(ins, outs, send_sems, recv_sems):
    x, y, c = _position()
    chips = [(1 - x, y), (x, 1 - y), (1 - x, 1 - y)]
    return [pltpu.make_async_remote_copy(
        src_ref=ins[a].at[2 * qx + qy], dst_ref=outs[a].at[j], send_sem=send_sems.at[a, j],
        recv_sem=recv_sems.at[a, j], device_id=(qx, qy, c), device_id_type=MESH)
        for a in range(len(ins)) for j, (qx, qy) in enumerate(chips)]


def _broadcast_copies(srcs, dsts, send_sems, recv_sems):
    x, y, c = _position()
    me = 4 * x + 2 * y + c
    cps = []
    for a in range(len(srcs)):
        for k in range(1, N_DEV):
            peer = (1 - x if k & 4 else x, 1 - y if k & 2 else y, 1 - c if k & 1 else c)
            cps.append(pltpu.make_async_remote_copy(
                src_ref=srcs[a], dst_ref=dsts[a].at[me], send_sem=send_sems.at[a, k - 1],
                recv_sem=recv_sems.at[a, k - 1], device_id=peer, device_id_type=MESH))
    return me, cps


def _all_reduce_small(part):
    R, C = part.shape

    def body(p_ref, out_ref, gath_ref, send_sems, recv_sems):
        me, cps = _broadcast_copies([p_ref], [gath_ref], send_sems, recv_sems)
        gath_ref[me] = p_ref[...]
        for cp in cps:
            cp.start()
        for cp in cps:
            cp.wait()
        acc = gath_ref[0]
        for d in range(1, N_DEV):
            acc = acc + gath_ref[d]
        out_ref[...] = acc

    vm = pl.BlockSpec(memory_space=pltpu.VMEM)
    return pl.pallas_call(
        body, name="all_reduce_small", in_specs=[vm], out_specs=vm, out_shape=S((R, C), F32),
        scratch_shapes=[pltpu.VMEM((N_DEV, R, C), F32), pltpu.SemaphoreType.DMA((1, N_DEV - 1)),
                        pltpu.SemaphoreType.DMA((1, N_DEV - 1))],
    )(part)


def _pack(parts):
    rows = []
    for p in parts:
        f = p.reshape(-1).astype(F32)
        pad = (-f.shape[0]) % (8 * LANES)
        rows.append(jnp.pad(f, (0, pad)).reshape(-1, LANES))
    return jnp.concatenate(rows, axis=0)


def _unpack(buf, shapes):
    out, r = [], 0
    for shp in shapes:
        n = 1
        for s in shp:
            n *= s
        nr = -(-n // (8 * LANES)) * 8
        out.append(buf[r:r + nr].reshape(-1)[:n].reshape(shp))
        r += nr
    return out


def kernel(x, norm_w, w_in, conv_w, a_log, dt_bias, head_norm_w, sgu_ln_w, sgu_ln_b, w_spatial, b_spatial, w_out, final_norm_w, loss_target, m_norm_w, m_w_in, m_conv_w, m_a_log, m_dt_bias, m_head_norm_w, m_sgu_ln_w, m_sgu_ln_b, m_w_spatial, m_b_spatial, m_w_out, m_final_norm_w, v_norm_w, v_w_in, v_conv_w, v_a_log, v_dt_bias, v_head_norm_w, v_sgu_ln_w, v_sgu_ln_b, v_w_spatial, v_b_spatial, v_w_out, v_final_norm_w):
    T, DM = x.shape[1], x.shape[2]
    H, D = a_log.shape[1], head_norm_w.shape[1]
    G, P = w_spatial.shape[1], w_spatial.shape[2]
    AW, BW = H * D, G * P
    MIX = AW + BW
    WD = w_in.shape[2]
    IN = N_DEV * WD
    RO = w_out.shape[1]
    CW = conv_w.shape[2]
    sizes = (3 * AW, AW, H, H, BW, BW, BW)
    assert sum(sizes) == IN and 2 * H <= LANES and 3 * H <= 32 and N_DEV * RO == MIX and N_DEV * CW == 3 * AW
    offs = [0]
    for s in sizes:
        offs.append(offs[-1] + s)
    px, py, pc = _position()
    dev = 4 * px + 2 * py + pc
    chip = 2 * px + py

    x2, tgt = x[0], loss_target[0]

    g_win = _all_gather_weights(_cast_bf16_t(w_in[0].T, "cast_w_in"))
    w_main, w_ba = _relayout_w(g_win, offs[2], offs[4])
    alog_row = jnp.pad(a_log, ((0, 0), (H, LANES - 2 * H)))
    dtb_row = jnp.pad(dt_bias, ((0, 0), (H, LANES - 2 * H)))
    bs_t = b_spatial[0].T

    xn = _rms_xn(x2, norm_w)
    proj, ba, (g_wout, g_conv) = _in_proj(xn, w_main, w_ba, [_cast_bf16(w_out[0], "cast_w_out"), conv_w[0]])
    w_out_full = g_wout.reshape(MIX, DM)
    conv_full = g_conv.transpose(1, 0, 2).reshape(4, 3 * AW)
    q, k, v, c, gcol, grow = _prep_a_fwd(proj, ba, conv_full, alog_row, dtb_row, H, D)
    o, vnew, ssave, asave = _delta_fwd(q, k, v, gcol, grow, H, D)
    ocat = _mix_fwd(o, proj, head_norm_w, sgu_ln_w, sgu_ln_b, w_spatial[0], bs_t, H, D, G, P)
    dh, dh_bf, d_ocat, loss_acc, g_fnw = _out_proj_loss(ocat, w_out_full, x2, tgt, final_norm_w.reshape(1, DM))

    core_idx = jnp.reshape(pc, (1,)).astype(jnp.int32)
    chip_idx = jnp.reshape(chip, (1,)).astype(jnp.int32)
    g_wout_blocks = _grad_w(ocat, dh_bf, "grad_w_out").reshape(4, 2, RO, DM)
    (d_o, dmain, g_hnw, g_ln, g_wsp, g_bs_t), (sib_wout,) = _mix_bwd(
        d_ocat, o, proj, head_norm_w, sgu_ln_w, sgu_ln_b, w_spatial[0], bs_t, H, D, G, P, [g_wout_blocks])
    chip_wout = _pair_sum(g_wout_blocks, sib_wout, core_idx, "pair_sum_w_out")
    (dq, dk, dv, dgate, dpar), (recv_wout,) = _delta_bwd(
        q, k, v, gcol, grow, ba, vnew, ssave, asave, d_o, alog_row, dtb_row, H, D, [chip_wout])
    dmain, g_conv_part = _prep_a_bwd(dq, dk, dv, c, proj, conv_full, dmain, H, D)
    dba = dgate.astype(BF16)
    keep_win, sib_win = _grad_w_in(xn, dmain, dba, WD, offs[2], offs[4])
    chip_win = _pair_sum_plain(keep_win, sib_win, "pair_sum_w_in")
    small_shapes = [a_log.shape, dt_bias.shape, head_norm_w.shape, sgu_ln_w.shape, sgu_ln_b.shape,
                    w_spatial.shape, b_spatial.shape, final_norm_w.shape]
    parts = [dpar[0, H:2 * H], dpar[1, H:2 * H], g_hnw[0], g_ln[0], g_ln[1], g_wsp, g_bs_t[:, :G].T, g_fnw[0],
             g_conv_part[:4], loss_acc[0, :1]]
    grad_x, g_nw, small_gath, recv_win = _dx(dmain, dba, w_main, w_ba, x2, dh, norm_w, chip_win, _pack(parts), 4)
    red = _sum_slots(small_gath)
    grad_w_in, delta_w_in, new_m_w_in, new_v_w_in = _sum_adam(
        chip_win, recv_win, w_in[0].T, m_w_in[0].T, v_w_in[0].T, chip_idx, "sum_adam_w_in", transposed=True)
    grad_w_out, delta_w_out, new_m_w_out, new_v_w_out = _sum_adam(
        chip_wout, recv_wout, w_out[0], m_w_out[0], v_w_out[0], chip_idx, "sum_adam_w_out")
    red_nw = _all_reduce_small(_pack([g_nw[0]]))
    grads_small = _unpack(red_nw, [norm_w.shape]) + _unpack(red, small_shapes + [(4, 3 * AW), (1,)])
    loss = grads_small.pop()[0]
    g_conv_full = grads_small.pop()
    grad_conv = lax.dynamic_slice_in_dim(g_conv_full, dev * CW, CW, axis=1)[None]
    small_w = [norm_w, a_log, dt_bias, head_norm_w, sgu_ln_w, sgu_ln_b, w_spatial, b_spatial, final_norm_w, conv_w]
    small_m = [m_norm_w, m_a_log, m_dt_bias, m_head_norm_w, m_sgu_ln_w, m_sgu_ln_b, m_w_spatial, m_b_spatial,
               m_final_norm_w, m_conv_w]
    small_v = [v_norm_w, v_a_log, v_dt_bias, v_head_norm_w, v_sgu_ln_w, v_sgu_ln_b, v_w_spatial, v_b_spatial,
               v_final_norm_w, v_conv_w]
    small_g = grads_small + [grad_conv]
    shapes10 = [w.shape for w in small_w]
    d_p, m_p, v_p = _adam_small(_pack(small_w), _pack(small_g), _pack(small_m), _pack(small_v))
    d_s, m_s, v_s = _unpack(d_p, shapes10), _unpack(m_p, shapes10), _unpack(v_p, shapes10)

    def order(small, win, wout):
        return [small[0], win.T[None], small[9], small[1], small[2], small[3], small[4], small[5], small[6], small[7],
                wout[None], small[8]]

    grads = order(small_g, grad_w_in, grad_w_out)
    deltas = order(d_s, delta_w_in, delta_w_out)
    new_m = order(m_s, new_m_w_in, new_m_w_out)
    new_v = order(v_s, new_v_w_in, new_v_w_out)
    return (loss, grad_x[None], *grads, *deltas, *new_m, *new_v)
```

```python
import jax
import jax.numpy as jnp
from jax import lax
from jax.experimental import pallas as pl
from jax.experimental.pallas import tpu as pltpu

F32 = jnp.float32
BF16 = jnp.bfloat16
MXU = jnp.bfloat16
HI = lax.Precision.HIGHEST
EPS = 1e-6
CHUNK_A = 64
LANES = 128
MESH = pl.DeviceIdType.MESH
N_DEV = 8

ADAM_LR = 0.001
ADAM_B1 = 0.9
ADAM_B2 = 0.999
ADAM_EPS = 1e-08
ADAM_WD = 0.01
ADAM_STEP = 10

S = jax.ShapeDtypeStruct
ARB = "arbitrary"


def _cp(*sem, vmem_mib=56):
    return pltpu.CompilerParams(dimension_semantics=tuple(sem), vmem_limit_bytes=vmem_mib * 1024 * 1024)


def _tile(n, cap, mult):
    best = None
    t = mult
    while t <= min(n, cap):
        if n % t == 0:
            best = t
        t += mult
    return best if best is not None else n


def _mm(a, b):
    return jnp.dot(a.astype(MXU), b.astype(MXU), preferred_element_type=F32)


def _mm_nt(a, b):
    return lax.dot_general(a.astype(MXU), b.astype(MXU), (((1,), (1,)), ((), ())), preferred_element_type=F32)


def _mm_tn(a, b):
    return lax.dot_general(a.astype(MXU), b.astype(MXU), (((0,), (0,)), ((), ())), preferred_element_type=F32)


def _mmh(a, b):
    return jnp.dot(a, b, precision=HI, preferred_element_type=F32)


def _sigmoid(x):
    return 1.0 / (1.0 + jnp.exp(-x))


def _silu(x):
    return x * _sigmoid(x)


def _dsilu(x):
    s = _sigmoid(x)
    return s * (1.0 + x * (1.0 - s))


def _softplus(x):
    return jnp.maximum(x, 0.0) + jnp.log(1.0 + jnp.exp(-jnp.abs(x)))


def _pieces(wd, gate_lo, gate_hi, total):
    out = []
    for d in range(N_DEV):
        lo, hi = d * wd, (d + 1) * wd
        for dest, a, b, shift in (("main", 0, gate_lo, 0), ("gate", gate_lo, gate_hi, -gate_lo),
                                  ("main", gate_hi, total, gate_lo - gate_hi)):
            s0, s1 = max(lo, a), min(hi, b)
            if s0 < s1:
                out.append((d, s0 - lo, s1 - lo, dest, s0 + shift))
    return out


def _cast_bf16(a, name):
    R, C = a.shape
    tr = _tile(R, 256, 16)

    def body(a_ref, o_ref):
        o_ref[...] = a_ref[...].astype(BF16)

    spec = pl.BlockSpec((tr, C), lambda i: (i, 0))
    return pl.pallas_call(body, name=name, grid=(R // tr,), in_specs=[spec], out_specs=spec,
                          out_shape=S((R, C), BF16), compiler_params=_cp(ARB))(a)


def _cast_bf16_t(a_t, name):
    C, R = a_t.shape
    tr = _tile(R, 256, LANES)

    def body(a_ref, o_ref):
        o_ref[...] = a_ref[...].T.astype(BF16)

    return pl.pallas_call(body, name=name, grid=(R // tr,), in_specs=[pl.BlockSpec((C, tr), lambda i: (0, i))],
                          out_specs=pl.BlockSpec((tr, C), lambda i: (i, 0)),
                          out_shape=S((R, C), BF16), compiler_params=_cp(ARB))(a_t)


def _relayout_w(g_win, gate_lo, gate_hi):
    _, DM, WD = g_win.shape
    total = N_DEV * WD
    NM = total - (gate_hi - gate_lo)
    tr = _tile(DM, 256, 16)
    plan = _pieces(WD, gate_lo, gate_hi, total)

    def body(g_ref, main_ref, gate_ref):
        gate_ref[...] = jnp.zeros_like(gate_ref)
        for d, s0, s1, dest, c0 in plan:
            dst = main_ref if dest == "main" else gate_ref
            dst[:, c0:c0 + (s1 - s0)] = g_ref[d, :, s0:s1]

    return pl.pallas_call(
        body, name="relayout_w", grid=(DM // tr,),
        in_specs=[pl.BlockSpec((N_DEV, tr, WD), lambda i: (0, i, 0))],
        out_specs=[pl.BlockSpec((tr, NM), lambda i: (i, 0)), pl.BlockSpec((tr, LANES), lambda i: (i, 0))],
        out_shape=[S((DM, NM), g_win.dtype), S((DM, LANES), g_win.dtype)],
        compiler_params=_cp(ARB),
    )(g_win)


def _rms_xn(x, norm_w):
    T, DM = x.shape
    tm = _tile(T, 512, 16)

    def body(x_ref, nw_ref, o_ref):
        xv = x_ref[...]
        r = lax.rsqrt(jnp.mean(xv * xv, axis=-1, keepdims=True) + EPS)
        o_ref[...] = (xv * r * nw_ref[...]).astype(BF16)

    return pl.pallas_call(
        body, name="rms_xn", grid=(T // tm,),
        in_specs=[pl.BlockSpec((tm, DM), lambda i: (i, 0)), pl.BlockSpec((1, DM), lambda i: (0, 0))],
        out_specs=pl.BlockSpec((tm, DM), lambda i: (i, 0)),
        out_shape=S((T, DM), BF16), compiler_params=_cp(ARB),
    )(x, norm_w)


def _in_proj(xn, w_main, w_ba, shards):
    T, DM = xn.shape
    NM = w_main.shape[1]
    tm = _tile(T, 2048, 16)
    tn = _tile(NM, 1024, LANES)
    ni, nj = T // tm, NM // tn
    ns = len(shards)

    def body(xn_ref, w_ref, wba_ref, *rest):
        srcs = rest[:ns]
        proj_ref, ba_ref = rest[ns:ns + 2]
        gath = rest[ns + 2:2 * ns + 2]
        send_sems, recv_sems, local_sems = rest[2 * ns + 2:]
        i = pl.program_id(0)
        j = pl.program_id(1)
        me, cps = _broadcast_copies(srcs, gath, send_sems, recv_sems)
        cps = cps + [pltpu.make_async_copy(srcs[a], gath[a].at[me], local_sems.at[a]) for a in range(ns)]

        @pl.when((i == 0) & (j == 0))
        def _():
            for cp in cps:
                cp.start()

        @pl.when(j == 0)
        def _():
            ba_ref[...] = jnp.dot(xn_ref[...].astype(MXU), wba_ref[...].astype(MXU), preferred_element_type=F32)

        proj_ref[...] = jnp.dot(xn_ref[...].astype(MXU), w_ref[...].astype(MXU), preferred_element_type=F32)

        @pl.when((i == ni - 1) & (j == nj - 1))
        def _():
            for cp in cps:
                cp.wait()

    any_spec = pl.BlockSpec(memory_space=pl.ANY)
    res = pl.pallas_call(
        body, name="in_proj", grid=(ni, nj),
        in_specs=[pl.BlockSpec((tm, DM), lambda i, j: (i, 0)),
                  pl.BlockSpec((DM, tn), lambda i, j: (0, j)),
                  pl.BlockSpec((DM, LANES), lambda i, j: (0, 0))] + [any_spec] * ns,
        out_specs=[pl.BlockSpec((tm, tn), lambda i, j: (i, j)),
                   pl.BlockSpec((tm, LANES), lambda i, j: (i, 0))] + [any_spec] * ns,
        out_shape=[S((T, NM), F32), S((T, LANES), F32)] + [S((N_DEV,) + a.shape, a.dtype) for a in shards],
        scratch_shapes=[pltpu.SemaphoreType.DMA((ns, N_DEV - 1)), pltpu.SemaphoreType.DMA((ns, N_DEV - 1)),
                        pltpu.SemaphoreType.DMA((ns,))],
        compiler_params=_cp(ARB, ARB, vmem_mib=58),
    )(xn, w_main, w_ba, *shards)
    return res[0], res[1], res[2:]


def _prep_a_fwd(proj, ba, conv_w, alog_row, dtb_row, H, D):
    T = proj.shape[0]
    AW = H * D
    C3 = 3 * AW
    tb = _tile(T, 256, CHUNK_A)
    nch = tb // CHUNK_A
    nblk = T // tb
    scale = float(D) ** -0.5

    def body(x_ref, halo_ref, ba_ref, cw_ref, al_ref, dt_ref, q_ref, k_ref, v_ref, c_ref, gcol_ref, grow_ref):
        i = pl.program_id(0)
        xv = x_ref[...]
        halo = halo_ref[...] * (i > 0).astype(F32)
        xp = jnp.concatenate([halo, xv], axis=0)
        cw = cw_ref[...]
        c = cw[0:1, :] * xp[5:5 + tb]
        for j in range(1, 4):
            c = c + cw[j:j + 1, :] * xp[5 + j:5 + j + tb]
        c_ref[...] = c
        a = _silu(c)
        for h in range(H):
            qh = a[:, h * D:(h + 1) * D]
            kh = a[:, AW + h * D:AW + (h + 1) * D]
            qr = lax.rsqrt(jnp.sum(qh * qh, axis=-1, keepdims=True) + EPS)
            kr = lax.rsqrt(jnp.sum(kh * kh, axis=-1, keepdims=True) + EPS)
            q_ref[:, h * D:(h + 1) * D] = qh * (qr * scale)
            k_ref[:, h * D:(h + 1) * D] = kh * kr
        v_ref[...] = a[:, 2 * AW:]

        bav = ba_ref[...]
        lane = lax.broadcasted_iota(jnp.int32, (tb, LANES), 1)
        beta = _sigmoid(bav)
        g = -jnp.exp(al_ref[...]) * _softplus(bav + dt_ref[...])
        gates = jnp.where(lane < H, beta, jnp.where(lane < 2 * H, g, 0.0))
        ri = lax.broadcasted_iota(jnp.int32, (CHUNK_A, CHUNK_A), 0)
        ci = lax.broadcasted_iota(jnp.int32, (CHUNK_A, CHUNK_A), 1)
        tri = (ri >= ci).astype(F32)
        lane_c = lax.broadcasted_iota(jnp.int32, (CHUNK_A, LANES), 1)
        for cc in range(nch):
            gch = gates[cc * CHUNK_A:(cc + 1) * CHUNK_A]
            gc = pltpu.roll(_mmh(tri, gch), H, 1)
            full = jnp.where(lane_c < 2 * H, gch, jnp.where(lane_c < 3 * H, gc, 0.0))
            gcol_ref[cc * CHUNK_A:(cc + 1) * CHUNK_A, :] = full
            grow_ref[cc] = full.T[0:32, :]

    return pl.pallas_call(
        body, name="prep_a_fwd", grid=(nblk,),
        in_specs=[pl.BlockSpec((tb, C3), lambda i: (i, 0)),
                  pl.BlockSpec((8, C3), lambda i: (jnp.maximum(i * (tb // 8) - 1, 0), 0)),
                  pl.BlockSpec((tb, LANES), lambda i: (i, 0)),
                  pl.BlockSpec((4, C3), lambda i: (0, 0)),
                  pl.BlockSpec((1, LANES), lambda i: (0, 0)),
                  pl.BlockSpec((1, LANES), lambda i: (0, 0))],
        out_specs=[pl.BlockSpec((tb, AW), lambda i: (i, 0)),
                   pl.BlockSpec((tb, AW), lambda i: (i, 0)),
                   pl.BlockSpec((tb, AW), lambda i: (i, 0)),
                   pl.BlockSpec((tb, C3), lambda i: (i, 0)),
                   pl.BlockSpec((tb, LANES), lambda i: (i, 0)),
                   pl.BlockSpec((nch, 32, CHUNK_A), lambda i: (i, 0, 0))],
        out_shape=[S((T, AW), F32), S((T, AW), F32), S((T, AW), F32), S((T, C3), F32),
                   S((T, LANES), F32), S((T // CHUNK_A, 32, CHUNK_A), F32)],
        compiler_params=_cp(ARB),
    )(proj, proj, ba, conv_w, alog_row, dtb_row)


_NN = (((1,), (0,)), ((), ()))
_TN = (((0,), (0,)), ((), ()))


def _split(a):
    hi = a.astype(BF16)
    return hi, (a - hi.astype(F32)).astype(BF16)


def _mm3(a, b, dims=_NN):
    ah, al = a if isinstance(a, tuple) else _split(a)
    bh, bl = b if isinstance(b, tuple) else _split(b)
    dg = lambda p, r: lax.dot_general(p, r, dims, preferred_element_type=F32)
    return dg(ah, bh) + (dg(ah, bl) + dg(al, bh))


def _interleave(gens):
    gens = list(gens)
    while gens:
        alive = []
        for g in gens:
            try:
                next(g)
                alive.append(g)
            except StopIteration:
                pass
        gens = alive


def _chunk_terms(q, k, v, gcolv, growv, h, H):
    C = CHUNK_A
    beta_c = gcolv[:, h:h + 1]
    g_c = gcolv[:, H + h:H + h + 1]
    gc_c = gcolv[:, 2 * H + h:2 * H + h + 1]
    gc_r = growv[2 * H + h:2 * H + h + 1, :]
    ri = lax.broadcasted_iota(jnp.int32, (C, C), 0)
    ci = lax.broadcasted_iota(jnp.int32, (C, C), 1)
    incl = ri >= ci
    strict = ri > ci
    kb = k * beta_c
    vb = v * beta_c
    p_raw = _mm_nt(kb, k)
    qk_raw = _mm_nt(q, k)
    gam = jnp.where(incl, jnp.exp(jnp.where(incl, gc_c - gc_r, 0.0)), 0.0)
    e_c = jnp.exp(gc_c)
    gl = gc_r[:, C - 1:C]
    edec = jnp.exp(gl - gc_c)
    yield
    lmat = jnp.where(strict, p_raw * gam, 0.0)
    attn = jnp.where(incl, qk_raw * gam, 0.0)
    return dict(beta_c=beta_c, g_c=g_c, gc_c=gc_c, gc_r=gc_r, incl=incl, strict=strict, gam=gam, e_c=e_c,
                kb=kb, vb=vb, lmat=lmat, attn=attn, gl=gl, edec=edec, ri=ri, ci=ci)


def _inv_unit_lower(lmat):
    C = lmat.shape[0]
    ri = lax.broadcasted_iota(jnp.int32, (C, C), 0)
    ci = lax.broadcasted_iota(jnp.int32, (C, C), 1)
    eye = (ri == ci).astype(F32)
    x = -lmat
    a = eye + x
    n = 1
    while 2 * n < C:
        xs = _split(x)
        x = _mm3(xs, xs)
        yield
        a = a + _mm3(a, x)
        n *= 2
    yield
    return a


def _delta_fwd(q, k, v, gcol, grow, H, D):
    T = q.shape[0]
    C = CHUNK_A
    N = T // C
    AW = H * D
    CPS = 2 if N % 2 == 0 else 1

    def body(q_ref, k_ref, v_ref, gcol_ref, grow_ref, o_ref, vn_ref, ssave_ref, asave_ref, s_ref):
        @pl.when(pl.program_id(0) == 0)
        def _():
            s_ref[...] = jnp.zeros_like(s_ref)

        state = {(0, h): s_ref[h] for h in range(H)}

        def head(cc, h):
            rows = slice(cc * C, (cc + 1) * C)
            sl = slice(h * D, (h + 1) * D)
            qv, kv, vv = q_ref[rows, sl], k_ref[rows, sl], v_ref[rows, sl]
            t = yield from _chunk_terms(qv, kv, vv, gcol_ref[rows, :], grow_ref[cc], h, H)
            a = yield from _inv_unit_lower(t["lmat"])
            asave_ref[cc, h] = a
            while (cc, h) not in state:
                yield
            st = state[(cc, h)]
            ssave_ref[cc, h] = st
            ks = _mm(t["kb"] * t["e_c"], st)
            o_inter = _mm(qv * t["e_c"], st)
            yield
            v_new = _mm3(a, t["vb"] - ks)
            yield
            vn_ref[rows, sl] = v_new
            o_intra = _mm(t["attn"], v_new)
            s_upd = _mm_tn(kv * t["edec"], v_new)
            yield
            o_ref[rows, sl] = o_inter + o_intra
            state[(cc + 1, h)] = st * jnp.exp(t["gl"]) + s_upd

        _interleave(head(cc, h) for cc in range(CPS) for h in range(H))
        for h in range(H):
            s_ref[h] = state[(CPS, h)]

    blk = lambda: pl.BlockSpec((CPS * C, AW), lambda n: (n, 0))
    return pl.pallas_call(
        body, name="delta_fwd", grid=(N // CPS,),
        in_specs=[blk(), blk(), blk(),
                  pl.BlockSpec((CPS * C, LANES), lambda n: (n, 0)),
                  pl.BlockSpec((CPS, 32, C), lambda n: (n, 0, 0))],
        out_specs=[blk(), blk(),
                   pl.BlockSpec((CPS, H, D, D), lambda n: (n, 0, 0, 0)),
                   pl.BlockSpec((CPS, H, C, C), lambda n: (n, 0, 0, 0))],
        out_shape=[S((T, AW), F32), S((T, AW), F32), S((N, H, D, D), F32), S((N, H, C, C), F32)],
        scratch_shapes=[pltpu.VMEM((H, D, D), F32)],
        compiler_params=_cp(ARB),
    )(q, k, v, gcol, grow)


def _delta_bwd(q, k, v, gcol, grow, ba, vnew, ssave, asave, d_o, a_log, dt_bias, H, D, carry):
    T = q.shape[0]
    C = CHUNK_A
    N = T // C
    AW = H * D
    nc = len(carry)
    CPS = 2 if N % 2 == 0 else 1
    NS = N // CPS

    def body(al_ref, dt_ref, q_ref, k_ref, v_ref, gcol_ref, grow_ref, ba_ref, vn_ref, ss_ref, as_ref, do_ref, *rest):
        cins = rest[:nc]
        dq_ref, dk_ref, dv_ref, dgate_ref, dpar_ref = rest[nc:nc + 5]
        couts = rest[nc + 5:2 * nc + 5]
        ds_ref, csend, crecv = rest[2 * nc + 5:]
        ccps = _chip_exchange_copies(cins, couts, csend, crecv)

        @pl.when(pl.program_id(0) == 0)
        def _():
            ds_ref[...] = jnp.zeros_like(ds_ref)
            dpar_ref[...] = jnp.zeros_like(dpar_ref)
            for cp in ccps:
                cp.start()

        lane = lax.broadcasted_iota(jnp.int32, (C, LANES), 1)
        rowi = lax.broadcasted_iota(jnp.int32, (C, 1), 0)
        acc = {cc: jnp.zeros((C, LANES), F32) for cc in range(CPS)}
        state = {(0, h): ds_ref[h] for h in range(H)}

        def head(oi, h):
            cc = CPS - 1 - oi
            rows = slice(cc * C, (cc + 1) * C)
            sl = slice(h * D, (h + 1) * D)
            st = ss_ref[cc, h]
            a = as_ref[cc, h]
            qv, kv, vv, dov, v_new = q_ref[rows, sl], k_ref[rows, sl], v_ref[rows, sl], do_ref[rows, sl], vn_ref[rows, sl]
            t = yield from _chunk_terms(qv, kv, vv, gcol_ref[rows, :], grow_ref[cc], h, H)
            beta_c, e_c, gam, kb = t["beta_c"], t["e_c"], t["gam"], t["kb"]
            incl, strict, attn, lmat, edec = t["incl"], t["strict"], t["attn"], t["lmat"], t["edec"]
            kdec = kv * edec
            egl = jnp.exp(t["gl"])
            qe = qv * e_c
            ekb = kb * e_c

            t1 = _mm_nt(dov, st)
            ds_o = _mm_tn(qe, dov)
            dattn_raw = _mm_nt(dov, v_new)
            dv_new_o = _mm_tn(attn, dov)
            yield
            while (oi, h) not in state:
                yield
            ds_next = state[(oi, h)]
            dkdec = _mm_nt(v_new, ds_next)
            dv_new_s = _mm(kdec, ds_next)
            yield
            dgl = egl * jnp.sum(jnp.sum(st * ds_next, axis=1, keepdims=True), axis=0, keepdims=True)
            dk = edec * dkdec
            r = jnp.sum(dkdec * kdec, axis=1, keepdims=True)
            dgc = -r
            dgl = dgl + jnp.sum(r, axis=0, keepdims=True)
            dq = e_c * t1
            dgc = dgc + jnp.sum(t1 * qe, axis=1, keepdims=True)
            dattn = jnp.where(incl, dattn_raw, 0.0)
            dv_new = dv_new_s + dv_new_o
            dqm = dattn * gam
            z = dattn * attn
            dvb = _mm3(a, dv_new, _TN)
            dq_a = _mm(dqm, kv)
            dk_a = _mm_tn(dqm, qv)
            yield
            dq_ref[rows, sl] = dq + dq_a
            dv_ref[rows, sl] = beta_c * dvb
            ds_kb = _mm_tn(ekb, dvb)
            dekb_neg = _mm_nt(dvb, st)
            dl_neg = _mm_nt(dvb, v_new)
            yield
            state[(oi + 1, h)] = egl * ds_next + ds_o - ds_kb
            dekb = -dekb_neg
            dl = jnp.where(strict, -dl_neg, 0.0)
            dp = dl * gam
            z = z + dl * lmat
            dkb_p = _mm(dp, kv)
            dk_p = _mm_tn(dp, kb)
            dgc = dgc + jnp.sum(dekb * ekb, axis=1, keepdims=True)
            dgc = dgc + jnp.sum(z, axis=1, keepdims=True) - jnp.sum(z.T, axis=1, keepdims=True)
            dgc = dgc + jnp.where(rowi == C - 1, dgl, 0.0)
            yield
            dkb = dkb_p + e_c * dekb
            dk_ref[rows, sl] = dk + dk_a + dk_p + beta_c * dkb
            dbeta = jnp.sum(dkb * kv, axis=1, keepdims=True) + jnp.sum(dvb * vv, axis=1, keepdims=True)
            acc[cc] = acc[cc] + jnp.where(lane == h, dbeta, 0.0) + jnp.where(lane == H + h, dgc, 0.0)

        _interleave(head(oi, h) for oi in range(CPS) for h in range(H))
        for h in range(H):
            ds_ref[h] = state[(CPS, h)]
        ri = lax.broadcasted_iota(jnp.int32, (C, C), 0)
        ci = lax.broadcasted_iota(jnp.int32, (C, C), 1)
        upper = (ri <= ci).astype(F32)
        dal = jnp.zeros((1, LANES), F32)
        ddt = jnp.zeros((1, LANES), F32)
        for cc in range(CPS):
            rows = slice(cc * C, (cc + 1) * C)
            gates = gcol_ref[rows, :]
            dg_all = _mm3(upper, acc[cc])
            d_braw = acc[cc] * gates * (1.0 - gates)
            d_araw = dg_all * (-jnp.exp(al_ref[...])) * _sigmoid(ba_ref[rows, :] + dt_ref[...])
            dgate_ref[rows, :] = jnp.where(lane < H, d_braw, jnp.where(lane < 2 * H, d_araw, 0.0))
            dal = dal + jnp.sum(dg_all * gates, axis=0, keepdims=True)
            ddt = ddt + jnp.sum(d_araw, axis=0, keepdims=True)
        dpar_ref[0:1, :] += dal
        dpar_ref[1:2, :] += ddt

        @pl.when(pl.program_id(0) == NS - 1)
        def _():
            for cp in ccps:
                cp.wait()

    rev = lambda s: NS - 1 - s
    blk = lambda: pl.BlockSpec((CPS * C, AW), lambda s: (rev(s), 0))
    row = pl.BlockSpec((1, LANES), lambda s: (0, 0))
    any_spec = pl.BlockSpec(memory_space=pl.ANY)
    res = pl.pallas_call(
        body, name="delta_bwd", grid=(NS,),
        in_specs=[row, row, blk(), blk(), blk(),
                  pl.BlockSpec((CPS * C, LANES), lambda s: (rev(s), 0)),
                  pl.BlockSpec((CPS, 32, C), lambda s: (rev(s), 0, 0)),
                  pl.BlockSpec((CPS * C, LANES), lambda s: (rev(s), 0)),
                  blk(),
                  pl.BlockSpec((CPS, H, D, D), lambda s: (rev(s), 0, 0, 0)),
                  pl.BlockSpec((CPS, H, C, C), lambda s: (rev(s), 0, 0, 0)),
                  blk()] + [any_spec] * nc,
        out_specs=[blk(), blk(), blk(),
                   pl.BlockSpec((CPS * C, LANES), lambda s: (rev(s), 0)),
                   pl.BlockSpec((8, LANES), lambda s: (0, 0))] + [any_spec] * nc,
        out_shape=[S((T, AW), F32), S((T, AW), F32), S((T, AW), F32),
                   S((T, LANES), F32), S((8, LANES), F32)] + [S((3,) + a.shape[1:], a.dtype) for a in carry],
        scratch_shapes=[pltpu.VMEM((H, D, D), F32),
                        pltpu.SemaphoreType.DMA((max(nc, 1), 3)), pltpu.SemaphoreType.DMA((max(nc, 1), 3))],
        compiler_params=_cp(ARB),
    )(a_log, dt_bias, q, k, v, gcol, grow, ba, vnew, ssave, asave, d_o, *carry)
    return res[:5], res[5:]


def _ln_stats(xv):
    mu = jnp.mean(xv, axis=-1, keepdims=True)
    xc = xv - mu
    var = jnp.mean(xc * xc, axis=-1, keepdims=True)
    rstd = lax.rsqrt(var + EPS)
    return xc * rstd, rstd


def _mix_fwd(o, proj, head_norm_w, ln_w, ln_b, w_sp, bs_t, H, D, G, P):
    T = o.shape[0]
    AW, BW = H * D, G * P
    MIX = AW + BW
    nb = AW // BW if AW % BW == 0 else None
    assert nb == 1, "group widths must match the projection column blocks"
    cb = 3

    def body(o_ref, za_ref, ub_ref, vb_ref, zb_ref, hw_ref, lw_ref, lb_ref, w_ref, bs_ref, out_ref):
        hw = hw_ref[...]
        for h in range(H):
            sl = slice(h * D, (h + 1) * D)
            oh = o_ref[:, sl]
            rs = lax.rsqrt(jnp.mean(oh * oh, axis=-1, keepdims=True) + EPS)
            out_ref[:, sl] = (oh * rs * hw * _silu(za_ref[:, sl])).astype(BF16)
        xhat, _ = _ln_stats(vb_ref[...])
        vn = xhat * lw_ref[...] + lb_ref[...]
        ri = lax.broadcasted_iota(jnp.int32, (P, P), 0)
        ci = lax.broadcasted_iota(jnp.int32, (P, P), 1)
        bsv = bs_ref[...]
        for g in range(G):
            sl = slice(g * P, (g + 1) * P)
            wm = jnp.where(ri >= ci, w_ref[g], 0.0)
            s = _mm(wm, vn[:, sl]) + bsv[:, g:g + 1]
            out_ref[:, AW + g * P:AW + (g + 1) * P] = (ub_ref[:, sl] * s * _silu(zb_ref[:, sl])).astype(BF16)

    row = lambda w: pl.BlockSpec((1, w), lambda i: (0, 0))
    return pl.pallas_call(
        body, name="mix_fwd", grid=(T // P,),
        in_specs=[pl.BlockSpec((P, AW), lambda i: (i, 0)),
                  pl.BlockSpec((P, AW), lambda i: (i, cb)),
                  pl.BlockSpec((P, BW), lambda i: (i, cb + 1)),
                  pl.BlockSpec((P, BW), lambda i: (i, cb + 2)),
                  pl.BlockSpec((P, BW), lambda i: (i, cb + 3)),
                  row(D), row(BW), row(BW),
                  pl.BlockSpec((G, P, P), lambda i: (0, 0, 0)),
                  pl.BlockSpec((P, G), lambda i: (0, 0))],
        out_specs=pl.BlockSpec((P, MIX), lambda i: (i, 0)),
        out_shape=S((T, MIX), BF16),
        compiler_params=_cp(ARB),
    )(o, proj, proj, proj, proj, head_norm_w, ln_w, ln_b, w_sp, bs_t)


def _mix_bwd(d_ocat, o, proj, head_norm_w, ln_w, ln_b, w_sp, bs_t, H, D, G, P, carry):
    T = o.shape[0]
    AW, BW = H * D, G * P
    MIX = AW + BW
    cb = 3
    nc = len(carry)

    def body(dc_ref, o_ref, za_ref, ub_ref, vb_ref, zb_ref, hw_ref, lw_ref, lb_ref, w_ref, bs_ref, *rest):
        cins = rest[:nc]
        do_ref, dmain_ref, dhw_ref, dln_ref, dw_ref, dbs_ref = rest[nc:nc + 6]
        couts = rest[nc + 6:2 * nc + 6]
        dvn_ref, drest_ref, out_sems, csend, crecv = rest[2 * nc + 6:]
        i = pl.program_id(0)
        slot = lax.rem(i, 2)
        ccps = _sibling_copies(cins, couts, csend, crecv)

        def out_copy(step, s):
            return pltpu.make_async_copy(
                drest_ref.at[s], dmain_ref.at[pl.ds(step * P, P), pl.ds(cb * AW, AW + 3 * BW)], out_sems.at[s])

        @pl.when(i == 0)
        def _():
            dhw_ref[...] = jnp.zeros_like(dhw_ref)
            dln_ref[...] = jnp.zeros_like(dln_ref)
            dw_ref[...] = jnp.zeros_like(dw_ref)
            dbs_ref[...] = jnp.zeros_like(dbs_ref)
            for cp in ccps:
                cp.start()

        @pl.when(i >= 2)
        def _():
            out_copy(i - 2, slot).wait()

        hw = hw_ref[...]
        dhw = jnp.zeros((1, D), F32)
        for h in range(H):
            sl = slice(h * D, (h + 1) * D)
            oh = o_ref[:, sl]
            za = za_ref[:, sl]
            doa = dc_ref[:, sl]
            rs = lax.rsqrt(jnp.mean(oh * oh, axis=-1, keepdims=True) + EPS)
            xh = oh * rs
            d_on = doa * _silu(za)
            drest_ref[slot, :, sl] = (doa * (xh * hw) * _dsilu(za)).astype(BF16)
            dhw = dhw + jnp.sum(d_on * xh, axis=0, keepdims=True)
            dxh = d_on * hw
            do_ref[:, sl] = rs * (dxh - xh * jnp.mean(dxh * xh, axis=-1, keepdims=True))
        dhw_ref[0:1, :] += dhw

        xhat, rstd = _ln_stats(vb_ref[...])
        lw = lw_ref[...]
        vn = xhat * lw + lb_ref[...]
        ri = lax.broadcasted_iota(jnp.int32, (P, P), 0)
        ci = lax.broadcasted_iota(jnp.int32, (P, P), 1)
        lane = lax.broadcasted_iota(jnp.int32, (P, LANES), 1)
        bsv = bs_ref[...]
        dbs = jnp.zeros((P, LANES), F32)
        for g in range(G):
            sl = slice(g * P, (g + 1) * P)
            wm = jnp.where(ri >= ci, w_ref[g], 0.0)
            vng = vn[:, sl]
            s = _mm(wm, vng) + bsv[:, g:g + 1]
            dob = dc_ref[:, AW + g * P:AW + (g + 1) * P]
            ub = ub_ref[:, sl]
            zb = zb_ref[:, sl]
            szb = _silu(zb)
            drest_ref[slot, :, AW + g * P:AW + (g + 1) * P] = (dob * s * szb).astype(BF16)
            drest_ref[slot, :, AW + 2 * BW + g * P:AW + 2 * BW + (g + 1) * P] = (
                dob * ub * s * _dsilu(zb)).astype(BF16)
            ds = dob * ub * szb
            dvn_ref[:, sl] = _mm_tn(wm, ds)
            dw_ref[g] += jnp.where(ri >= ci, _mm_nt(ds, vng), 0.0)
            dbs = dbs + jnp.where(lane == g, jnp.sum(ds, axis=1, keepdims=True), 0.0)
        dbs_ref[...] += dbs
        dvn = dvn_ref[...]
        dln_ref[0:1, :] += jnp.sum(dvn * xhat, axis=0, keepdims=True)
        dln_ref[1:2, :] += jnp.sum(dvn, axis=0, keepdims=True)
        dxh = dvn * lw
        dvb = rstd * (dxh - jnp.mean(dxh, axis=-1, keepdims=True) - xhat * jnp.mean(dxh * xhat, axis=-1, keepdims=True))
        drest_ref[slot, :, AW + BW:AW + 2 * BW] = dvb.astype(BF16)

        out_copy(i, slot).start()

        @pl.when(i == nstep - 1)
        def _():
            out_copy(i, slot).wait()
            if nstep > 1:
                out_copy(i - 1, 1 - slot).wait()
            for cp in ccps:
                cp.wait()

    nstep = T // P
    row = lambda w: pl.BlockSpec((1, w), lambda i: (0, 0))
    any_spec = pl.BlockSpec(memory_space=pl.ANY)
    res = pl.pallas_call(
        body, name="mix_bwd", grid=(nstep,),
        in_specs=[pl.BlockSpec((P, MIX), lambda i: (i, 0)),
                  pl.BlockSpec((P, AW), lambda i: (i, 0)),
                  pl.BlockSpec((P, AW), lambda i: (i, cb)),
                  pl.BlockSpec((P, BW), lambda i: (i, cb + 1)),
                  pl.BlockSpec((P, BW), lambda i: (i, cb + 2)),
                  pl.BlockSpec((P, BW), lambda i: (i, cb + 3)),
                  row(D), row(BW), row(BW),
                  pl.BlockSpec((G, P, P), lambda i: (0, 0, 0)),
                  pl.BlockSpec((P, G), lambda i: (0, 0))] + [any_spec] * nc,
        out_specs=[pl.BlockSpec((P, AW), lambda i: (i, 0)),
                   any_spec,
                   pl.BlockSpec((8, D), lambda i: (0, 0)),
                   pl.BlockSpec((8, BW), lambda i: (0, 0)),
                   pl.BlockSpec((G, P, P), lambda i: (0, 0, 0)),
                   pl.BlockSpec((P, LANES), lambda i: (0, 0))] + [any_spec] * nc,
        out_shape=[S((T, AW), F32), S((T, cb * AW + AW + 3 * BW), BF16), S((8, D), F32), S((8, BW), F32),
                   S((G, P, P), F32), S((P, LANES), F32)] + [S(a.shape[:1] + a.shape[2:], a.dtype) for a in carry],
        scratch_shapes=[pltpu.VMEM((P, BW), F32), pltpu.VMEM((2, P, AW + 3 * BW), BF16),
                        pltpu.SemaphoreType.DMA((2,))] + _sibling_sems(carry),
        compiler_params=_cp(ARB),
    )(d_ocat, o, proj, proj, proj, proj, head_norm_w, ln_w, ln_b, w_sp, bs_t, *carry)
    return res[:6], res[6:]


def _out_proj_loss(ocat, w_out, x, target, fnw):
    T, MIX = ocat.shape
    DM = x.shape[1]
    tm = _tile(T, 256, 8)

    def body(oc_ref, w_ref, x_ref, t_ref, fw_ref, dh_ref, dhb_ref, doc_ref, loss_ref, gfw_ref):
        @pl.when(pl.program_id(0) == 0)
        def _():
            loss_ref[...] = jnp.zeros_like(loss_ref)
            gfw_ref[...] = jnp.zeros_like(gfw_ref)

        wv = w_ref[...]
        hh = x_ref[...] + jnp.dot(oc_ref[...].astype(MXU), wv.astype(MXU), preferred_element_type=F32)
        rs = lax.rsqrt(jnp.mean(hh * hh, axis=-1, keepdims=True) + EPS)
        hn = hh * rs
        fw = fw_ref[...]
        e = hn * fw - t_ref[...]
        row_loss = 0.5 * jnp.mean(e * e, axis=-1, keepdims=True)
        loss_ref[...] += jnp.sum(row_loss, axis=0, keepdims=True)
        dy = e * (1.0 / DM)
        gfw_ref[0:1, :] += jnp.sum(dy * hn, axis=0, keepdims=True)
        dhn = dy * fw
        dh = rs * (dhn - hn * jnp.mean(dhn * hn, axis=-1, keepdims=True))
        dh_ref[...] = dh
        dhb = dh.astype(BF16)
        dhb_ref[...] = dhb
        doc_ref[...] = _mm_nt(dhb, wv)

    return pl.pallas_call(
        body, name="out_proj_loss", grid=(T // tm,),
        in_specs=[pl.BlockSpec((tm, MIX), lambda i: (i, 0)),
                  pl.BlockSpec((MIX, DM), lambda i: (0, 0)),
                  pl.BlockSpec((tm, DM), lambda i: (i, 0)),
                  pl.BlockSpec((tm, DM), lambda i: (i, 0)),
                  pl.BlockSpec((1, DM), lambda i: (0, 0))],
        out_specs=[pl.BlockSpec((tm, DM), lambda i: (i, 0)),
                   pl.BlockSpec((tm, DM), lambda i: (i, 0)),
                   pl.BlockSpec((tm, MIX), lambda i: (i, 0)),
                   pl.BlockSpec((8, LANES), lambda i: (0, 0)),
                   pl.BlockSpec((8, DM), lambda i: (0, 0))],
        out_shape=[S((T, DM), F32), S((T, DM), BF16), S((T, MIX), F32), S((8, LANES), F32), S((8, DM), F32)],
        compiler_params=_cp(ARB),
    )(ocat, w_out, x, target, fnw)


def _grad_w(lhs, rhs, name):
    T, A = lhs.shape
    B = rhs.shape[1]
    ta = _tile(A, 512, LANES)
    tk = _tile(T, 1024, 16)
    nk = T // tk

    def body(l_ref, r_ref, out_ref, acc_ref):
        k = pl.program_id(1)
        part = _mm_tn(l_ref[...], r_ref[...])

        @pl.when(k == 0)
        def _():
            acc_ref[...] = part

        @pl.when(k > 0)
        def _():
            acc_ref[...] += part

        @pl.when(k == nk - 1)
        def _():
            out_ref[...] = acc_ref[...].astype(BF16)

    return pl.pallas_call(
        body, name=name, grid=(A // ta, nk),
        in_specs=[pl.BlockSpec((tk, ta), lambda i, k: (k, i)),
                  pl.BlockSpec((tk, B), lambda i, k: (k, 0))],
        out_specs=pl.BlockSpec((ta, B), lambda i, k: (i, 0)),
        out_shape=S((A, B), BF16),
        scratch_shapes=[pltpu.VMEM((ta, B), F32)],
        compiler_params=_cp(ARB, ARB),
    )(lhs, rhs)


def _grad_w_in(xn, dmain, dba, WD, gate_lo, gate_hi):
    T, DM = xn.shape
    NM = dmain.shape[1]
    tn = _tile(NM, 1024, LANES)
    tk = _tile(T, 2048, 16)
    nj, nk = NM // tn, T // tk
    ND = N_DEV
    tiles = [[] for _ in range(nj)]
    first_tile, last_tile = {}, {}
    for d, s0, s1, dest, c0 in _pieces(WD, gate_lo, gate_hi, ND * WD):
        if dest != "main":
            continue
        while s0 < s1:
            jj = c0 // tn
            w = min(s1 - s0, (jj + 1) * tn - c0)
            tiles[jj].append((d, s0, w, "main", c0 - jj * tn))
            first_tile.setdefault(d, jj)
            last_tile[d] = jj
            s0, c0 = s0 + w, c0 + w
    for d, s0, s1, dest, c0 in _pieces(WD, gate_lo, gate_hi, ND * WD):
        if dest == "gate":
            tiles[first_tile[d]].append((d, s0, s1 - s0, "gate", c0))
    assert sorted(first_tile) == list(range(ND)) and all(last_tile[d] <= first_tile[d + 2] for d in range(ND - 2))

    def body(xn_ref, dm_ref, dba_ref, keep_ref, recv_ref, acc_ref, gate_ref, buf_ref, lsem, ssem, rsem):
        j = pl.program_id(0)
        k = pl.program_id(1)
        px, py, pc = _position()

        @pl.when(k == 0)
        def _():
            acc_ref[...] = jnp.zeros_like(acc_ref)

        @pl.when((j == 0) & (k == 0))
        def _():
            gate_ref[...] = jnp.zeros_like(gate_ref)

        xv = xn_ref[...]
        acc_ref[...] += _mm_tn(xv, dm_ref[...])

        @pl.when(j == 0)
        def _():
            gate_ref[...] += _mm_tn(xv, dba_ref[...])

        def local(d):
            return pltpu.make_async_copy(buf_ref.at[d % 2], keep_ref.at[d // 2], lsem.at[d // 2])

        def remote(d):
            return pltpu.make_async_remote_copy(
                src_ref=buf_ref.at[d % 2], dst_ref=recv_ref.at[d // 2], send_sem=ssem.at[d // 2],
                recv_sem=rsem.at[d // 2], device_id=(px, py, 1 - pc), device_id_type=MESH)

        def leave(d, start):
            @pl.when(pc == d % 2)
            def _():
                local(d).start() if start else local(d).wait()

            @pl.when(pc != d % 2)
            def _():
                remote(d).start() if start else remote(d).wait_send()

        def emit(jj):
            shards = sorted({p[0] for p in tiles[jj]})
            for d in shards:
                if first_tile[d] == jj and d >= 2:
                    leave(d - 2, False)
                for dd, s0, w, src, c0 in tiles[jj]:
                    if dd == d:
                        ref = acc_ref if src == "main" else gate_ref
                        buf_ref[d % 2, :, s0:s0 + w] = ref[:, c0:c0 + w].astype(BF16)
                if last_tile[d] == jj:
                    leave(d, True)
            if jj == nj - 1:
                for d in (ND - 2, ND - 1):
                    leave(d, False)
                for q in range(ND // 2):
                    remote(2 * q).wait_recv()

        for jj in range(nj):
            @pl.when((j == jj) & (k == nk - 1))
            def _(jj=jj):
                emit(jj)

    any_spec = pl.BlockSpec(memory_space=pl.ANY)
    return pl.pallas_call(
        body, name="grad_w_in", grid=(nj, nk),
        in_specs=[pl.BlockSpec((tk, DM), lambda j, k: (k, 0)),
                  pl.BlockSpec((tk, tn), lambda j, k: (k, j)),
                  pl.BlockSpec((tk, LANES), lambda j, k: (k, 0))],
        out_specs=[any_spec, any_spec],
        out_shape=[S((ND // 2, DM, WD), BF16), S((ND // 2, DM, WD), BF16)],
        scratch_shapes=[pltpu.VMEM((DM, tn), F32), pltpu.VMEM((DM, LANES), F32), pltpu.VMEM((2, DM, WD), BF16),
                        pltpu.SemaphoreType.DMA((ND // 2,)), pltpu.SemaphoreType.DMA((ND // 2,)),
                        pltpu.SemaphoreType.DMA((ND // 2,))],
        compiler_params=_cp(ARB, ARB),
    )(xn, dmain, dba)


def _pair_sum_plain(a, b, name):
    K, R, C = a.shape
    tr = _tile(R, 256, 16)

    def body(a_ref, b_ref, o_ref):
        o_ref[...] = (a_ref[...].astype(F32) + b_ref[...].astype(F32)).astype(BF16)

    spec = lambda: pl.BlockSpec((1, tr, C), lambda q, i: (q, i, 0))
    return pl.pallas_call(body, name=name, grid=(K, R // tr), in_specs=[spec(), spec()], out_specs=spec(),
                          out_shape=S((K, R, C), BF16), compiler_params=_cp(ARB, ARB))(a, b)


def _dx_rows(T):
    tm = _tile(T, 512, 8)
    return tm if T // tm >= 2 else T // 2


def _dx_part(name, dmain, dba, w_main, w_ba, x, dh, norm_w, blk0, nblk, prev, hbm_in, hbm_alias, hbm_new, make_copies):
    T, NM = dmain.shape
    DM = x.shape[1]
    tm = _dx_rows(T)
    tk = _tile(NM, 1024, LANES)
    nk = NM // tk
    n_in, n_al, n_new = len(hbm_in), len(hbm_alias), len(hbm_new)
    n_prev = 0 if prev is None else 2
    last_step = nblk * nk - 1

    def body(dm_ref, dba_ref, w_ref, wba_ref, x_ref, dh_ref, nw_ref, *rest):
        r = list(rest)
        gnw_prev_ref = r.pop(0) if n_prev else None
        if n_prev:
            r.pop(0)
        in_refs = [r.pop(0) for _ in range(n_in)]
        del r[:n_al]
        gx_ref, gnw_ref = r.pop(0), r.pop(0)
        alias_refs = [r.pop(0) for _ in range(n_al)]
        new_refs = [r.pop(0) for _ in range(n_new)]
        acc_ref, send_sems, recv_sems = r
        i = pl.program_id(0)
        k = pl.program_id(1)
        step = i * nk + k
        cps = make_copies(in_refs, alias_refs, new_refs, send_sems, recv_sems)

        @pl.when(step == 0)
        def _():
            gnw_ref[...] = gnw_prev_ref[...] if n_prev else jnp.zeros_like(gnw_ref)
            for cp in cps:
                cp.start()

        @pl.when(k == 0)
        def _():
            acc_ref[...] = _mm_nt(dba_ref[...], wba_ref[...])

        acc_ref[...] += _mm_nt(dm_ref[...], w_ref[...])

        @pl.when(k == nk - 1)
        def _():
            xv = x_ref[...]
            rs = lax.rsqrt(jnp.mean(xv * xv, axis=-1, keepdims=True) + EPS)
            xh = xv * rs
            dxn = acc_ref[...]
            gnw_ref[0:1, :] += jnp.sum(dxn * xh, axis=0, keepdims=True)
            dxh = dxn * nw_ref[...]
            gx_ref[...] = dh_ref[...] + rs * (dxh - xh * jnp.mean(dxh * xh, axis=-1, keepdims=True))

        @pl.when(step == last_step)
        def _():
            for cp in cps:
                cp.wait()

    any_spec = pl.BlockSpec(memory_space=pl.ANY)
    prev_specs = [pl.BlockSpec((8, DM), lambda i, k: (0, 0)), any_spec] if n_prev else []
    prev_args = [prev[1], prev[0]] if n_prev else []
    aliases = {8: 0} if n_prev else {}
    for q in range(n_al):
        aliases[7 + n_prev + n_in + q] = 2 + q
    res = pl.pallas_call(
        body, name=name, grid=(nblk, nk),
        in_specs=[pl.BlockSpec((tm, tk), lambda i, k: (blk0 + i, k)),
                  pl.BlockSpec((tm, LANES), lambda i, k: (blk0 + i, 0)),
                  pl.BlockSpec((DM, tk), lambda i, k: (0, k)),
                  pl.BlockSpec((DM, LANES), lambda i, k: (0, 0)),
                  pl.BlockSpec((tm, DM), lambda i, k: (blk0 + i, 0)),
                  pl.BlockSpec((tm, DM), lambda i, k: (blk0 + i, 0)),
                  pl.BlockSpec((1, DM), lambda i, k: (0, 0))] + prev_specs + [any_spec] * (n_in + n_al),
        out_specs=[pl.BlockSpec((tm, DM), lambda i, k: (blk0 + i, 0)),
                   pl.BlockSpec((8, DM), lambda i, k: (0, 0))] + [any_spec] * (n_al + n_new),
        out_shape=[S((T, DM), F32), S((8, DM), F32)] + [S(a.shape, a.dtype) for a in hbm_alias] + list(hbm_new),
        scratch_shapes=[pltpu.VMEM((tm, DM), F32), pltpu.SemaphoreType.DMA((10,)), pltpu.SemaphoreType.DMA((10,))],
        input_output_aliases=aliases,
        compiler_params=_cp(ARB, ARB),
    )(dmain, dba, w_main, w_ba, x, dh, norm_w, *prev_args, *hbm_in, *hbm_alias)
    return (res[0], res[1]), res[2:2 + n_al], res[2 + n_al:]


def _remote(kk, src, dst, to, send_sems, recv_sems):
    return pltpu.make_async_remote_copy(src_ref=src, dst_ref=dst, send_sem=send_sems.at[kk], recv_sem=recv_sems.at[kk],
                                        device_id=to, device_id_type=MESH)


def _dx(dmain, dba, w_main, w_ba, x, dh, norm_w, chip_sum, small, cut):
    R, C = chip_sum.shape[1:]
    half = R // 2
    assert half % 16 == 0
    T = x.shape[0]
    ni = T // _dx_rows(T)
    cut = max(1, min(cut, ni - 1))
    upper, lower = pl.ds(0, half), pl.ds(half, half)

    def nbrs():
        px, py, pc = _position()
        return (px, py), (1 - px, py, pc), (px, 1 - py, pc)

    def phase1(ins, als, news, ss, rs):
        (px, py), xn, yn = nbrs()
        cs = ins[0]
        recv, stage = news
        bx, by, bd = cs.at[2 * (1 - px) + py], cs.at[2 * px + (1 - py)], cs.at[2 * (1 - px) + (1 - py)]
        return [_remote(0, bx.at[upper], recv.at[0].at[upper], xn, ss, rs),
                _remote(1, by.at[lower], recv.at[1].at[lower], yn, ss, rs),
                _remote(2, bd.at[upper], stage.at[0], xn, ss, rs),
                _remote(3, bd.at[lower], stage.at[1], yn, ss, rs)]

    def phase2(ins, als, news, ss, rs):
        (px, py), xn, yn = nbrs()
        comb, small_ref = ins
        recv, gath = als[0], news[0]
        me, small_cps = _broadcast_copies([small_ref], [gath], _Sem2(ss, 2), _Sem2(rs, 2))
        return ([_remote(0, comb.at[0], recv.at[1].at[upper], yn, ss, rs),
                 _remote(1, comb.at[1], recv.at[0].at[lower], xn, ss, rs)] + small_cps
                + [pltpu.make_async_copy(small_ref, gath.at[me], ss.at[9])])

    (gx, gnw), _, (recv, stage) = _dx_part(
        "dx_a", dmain, dba, w_main, w_ba, x, dh, norm_w, 0, cut, None, [chip_sum], [],
        [S((2, R, C), chip_sum.dtype), S((2, half, C), chip_sum.dtype)], phase1)
    comb = _relay_add(chip_sum, stage)
    (gx, gnw), (recv,), (gath,) = _dx_part(
        "dx_b", dmain, dba, w_main, w_ba, x, dh, norm_w, cut, ni - cut, (gx, gnw), [comb, small], [recv],
        [S((N_DEV,) + small.shape, F32)], phase2)
    return gx, gnw, gath, recv


class _Sem2:
    def __init__(self, sems, lo):
        self.sems, self.lo = sems, lo

    @property
    def at(self):
        outer = self

        class _At:
            def __getitem__(self, idx):
                a, k = idx
                return outer.sems.at[outer.lo + k]
        return _At()


def _relay_add(chip_sum, stage):
    _, R, C = chip_sum.shape
    half = R // 2
    tr = _tile(half, 256, 16)
    nt = half // tr
    px, py, _ = _position()
    idx = jnp.stack([2 * px + (1 - py), 2 * (1 - px) + py]).astype(jnp.int32)

    def body(idx_ref, p_ref, s_ref, o_ref):
        del idx_ref
        o_ref[0] = (p_ref[0].astype(F32) + s_ref[0].astype(F32)).astype(BF16)

    return pl.pallas_call(
        body, name="relay_add",
        grid_spec=pltpu.PrefetchScalarGridSpec(
            num_scalar_prefetch=1, grid=(2, nt),
            in_specs=[pl.BlockSpec((1, tr, C), lambda s, i, idx_ref: (idx_ref[s], s * nt + i, 0)),
                      pl.BlockSpec((1, tr, C), lambda s, i, idx_ref: (s, i, 0))],
            out_specs=pl.BlockSpec((1, tr, C), lambda s, i, idx_ref: (s, i, 0))),
        out_shape=S((2, half, C), BF16), compiler_params=_cp(ARB, ARB),
    )(idx, chip_sum, stage)


def _sum_slots(gath):
    _, R, C = gath.shape
    tr = R if R <= 2048 else _tile(R, 512, 8)

    def body(g_ref, o_ref):
        tot = g_ref[0]
        for d in range(1, N_DEV):
            tot = tot + g_ref[d]
        o_ref[...] = tot

    return pl.pallas_call(
        body, name="sum_slots", grid=(R // tr,),
        in_specs=[pl.BlockSpec((N_DEV, tr, C), lambda i: (0, i, 0))],
        out_specs=pl.BlockSpec((tr, C), lambda i: (i, 0)),
        out_shape=S((R, C), F32), compiler_params=_cp(ARB),
    )(gath)


def _prep_a_bwd(dq, dk, dv, c, proj, conv_w, dmain, H, D):
    T = c.shape[0]
    AW = H * D
    C3 = 3 * AW
    tb = _tile(T, 256, 8)
    nblk = T // tb
    r8 = tb // 8
    scale = float(D) ** -0.5

    def body(dq_ref, dk_ref, dv_ref, c_ref, dqn_ref, dkn_ref, dvn_ref, cn_ref, x_ref, halo_ref, cw_ref, dmain_in_ref,
             dx_ref, gcw_ref, dc_ref):
        del dmain_in_ref
        i = pl.program_id(0)

        @pl.when(i == 0)
        def _():
            gcw_ref[...] = jnp.zeros_like(gcw_ref)

        def pointwise(rows, dq_r, dk_r, dv_r, c_r, keep):
            for h in range(H):
                for part, d_r, sc in ((0, dq_r, scale), (1, dk_r, 1.0)):
                    sl = slice(part * AW + h * D, part * AW + (h + 1) * D)
                    cv = c_r[:, sl]
                    raw = _silu(cv)
                    rs = lax.rsqrt(jnp.sum(raw * raw, axis=-1, keepdims=True) + EPS)
                    nrm = raw * rs
                    dn = d_r[:, h * D:(h + 1) * D] * sc
                    draw = rs * (dn - nrm * jnp.sum(dn * nrm, axis=-1, keepdims=True))
                    dc_ref[rows, sl] = draw * _dsilu(cv) * keep
            dc_ref[rows, 2 * AW:] = dv_r[...] * _dsilu(c_r[:, 2 * AW:]) * keep

        pointwise(slice(0, tb), dq_ref, dk_ref, dv_ref, c_ref, 1.0)
        pointwise(slice(tb, tb + 8), dqn_ref, dkn_ref, dvn_ref, cn_ref, (i < nblk - 1).astype(F32))

        cw = cw_ref[...]
        dcv = dc_ref[0:tb, :]
        dx = cw[3:4, :] * dcv
        for j in range(3):
            dx = dx + cw[j:j + 1, :] * dc_ref[3 - j:3 - j + tb, :]
        dx_ref[...] = dx.astype(BF16)
        halo = halo_ref[...] * (i > 0).astype(F32)
        xp = jnp.concatenate([halo, x_ref[...]], axis=0)
        for j in range(4):
            gcw_ref[j:j + 1, :] += jnp.sum(dcv * xp[5 + j:5 + j + tb], axis=0, keepdims=True)

    nxt = lambda i: (jnp.minimum((i + 1) * r8, T // 8 - 1), 0)
    return pl.pallas_call(
        body, name="prep_a_bwd", grid=(nblk,),
        in_specs=[pl.BlockSpec((tb, AW), lambda i: (i, 0)),
                  pl.BlockSpec((tb, AW), lambda i: (i, 0)),
                  pl.BlockSpec((tb, AW), lambda i: (i, 0)),
                  pl.BlockSpec((tb, C3), lambda i: (i, 0)),
                  pl.BlockSpec((8, AW), nxt), pl.BlockSpec((8, AW), nxt), pl.BlockSpec((8, AW), nxt),
                  pl.BlockSpec((8, C3), nxt),
                  pl.BlockSpec((tb, C3), lambda i: (i, 0)),
                  pl.BlockSpec((8, C3), lambda i: (jnp.maximum(i * r8 - 1, 0), 0)),
                  pl.BlockSpec((4, C3), lambda i: (0, 0)),
                  pl.BlockSpec(memory_space=pl.ANY)],
        out_specs=[pl.BlockSpec((tb, C3), lambda i: (i, 0)),
                   pl.BlockSpec((8, C3), lambda i: (0, 0))],
        out_shape=[S(dmain.shape, dmain.dtype), S((8, C3), F32)],
        scratch_shapes=[pltpu.VMEM((tb + 8, C3), F32)],
        input_output_aliases={11: 0},
        compiler_params=_cp(ARB),
    )(dq, dk, dv, c, dq, dk, dv, c, proj, proj, conv_w, dmain)


def _adam_math(w, g, m, v):
    m2 = ADAM_B1 * m + (1.0 - ADAM_B1) * g
    v2 = ADAM_B2 * v + (1.0 - ADAM_B2) * (g * g)
    m_hat = m2 / (1.0 - ADAM_B1 ** ADAM_STEP)
    v_hat = v2 / (1.0 - ADAM_B2 ** ADAM_STEP)
    delta = -ADAM_LR * (m_hat / (jnp.sqrt(v_hat) + ADAM_EPS) + ADAM_WD * w)
    return delta, m2, v2


def _pair_sum(blocks, recv, core, name):
    K, _, R, C = blocks.shape
    tr = _tile(R, 256, 16)

    def body(core_ref, a_ref, b_ref, o_ref):
        del core_ref
        o_ref[0] = (a_ref[0, 0].astype(F32) + b_ref[0].astype(F32)).astype(BF16)

    spec = lambda: pl.BlockSpec((1, tr, C), lambda k, i, core_ref: (k, i, 0))
    return pl.pallas_call(
        body, name=name,
        grid_spec=pltpu.PrefetchScalarGridSpec(
            num_scalar_prefetch=1, grid=(K, R // tr),
            in_specs=[pl.BlockSpec((1, 1, tr, C), lambda k, i, core_ref: (k, core_ref[0], i, 0)), spec()],
            out_specs=spec()),
        out_shape=S((K, R, C), BF16), compiler_params=_cp(ARB, ARB),
    )(core, blocks, recv)


def _sum_adam(chip_sums, recv, w, m, v, chip, name, transposed=False):
    R, C = chip_sums.shape[1:]
    NR = recv.shape[0]
    tr = _tile(R, 256, 16)

    def body(chip_ref, own_ref, r_ref, w_ref, m_ref, v_ref, g_ref, d_ref, m2_ref, v2_ref):
        del chip_ref
        g = own_ref[0].astype(F32)
        for j in range(NR):
            g = g + r_ref[j].astype(F32)
        if transposed:
            g = g.T
        g_ref[...] = g
        d_ref[...], m2_ref[...], v2_ref[...] = _adam_math(w_ref[...], g, m_ref[...], v_ref[...])

    if transposed:
        spec = lambda: pl.BlockSpec((C, tr), lambda i, chip_ref: (0, i))
        shape = (C, R)
    else:
        spec = lambda: pl.BlockSpec((tr, C), lambda i, chip_ref: (i, 0))
        shape = (R, C)
    assert w.shape == shape
    return pl.pallas_call(
        body, name=name,
        grid_spec=pltpu.PrefetchScalarGridSpec(
            num_scalar_prefetch=1, grid=(R // tr,),
            in_specs=[pl.BlockSpec((1, tr, C), lambda i, chip_ref: (chip_ref[0], i, 0)),
                      pl.BlockSpec((NR, tr, C), lambda i, chip_ref: (0, i, 0)), spec(), spec(), spec()],
            out_specs=[spec(), spec(), spec(), spec()]),
        out_shape=[S(shape, F32)] * 4, compiler_params=_cp(ARB),
    )(chip, chip_sums, recv, w, m, v)


def _adam_small(w, g, m, v):
    R, C = w.shape
    tr = _tile(R, 512, 8)

    def body(w_ref, g_ref, m_ref, v_ref, d_ref, m2_ref, v2_ref):
        d_ref[...], m2_ref[...], v2_ref[...] = _adam_math(w_ref[...], g_ref[...], m_ref[...], v_ref[...])

    spec = lambda: pl.BlockSpec((tr, C), lambda i: (i, 0))
    return pl.pallas_call(
        body, name="adam_small", grid=(R // tr,), in_specs=[spec()] * 4, out_specs=[spec()] * 3,
        out_shape=[S((R, C), F32)] * 3, compiler_params=_cp(ARB),
    )(w, g, m, v)


def _position():
    return lax.axis_index("x"), lax.axis_index("y"), lax.axis_index("c")


def _all_gather_weights(arr):
    R = arr.shape[0]
    half = R // 2
    assert half % 16 == 0

    def body(in_ref, out_ref, send_sems, recv_sems, local_sem):
        x, y, c = _position()
        me, sibling = (x, y, c), (x, y, 1 - c)
        xn, yn, diag = (1 - x, y), (x, 1 - y), (1 - x, 1 - y)
        upper, lower = pl.ds(0, half), pl.ds(half, half)

        def slot(p, rows=None):
            ref = out_ref.at[4 * p[0] + 2 * p[1] + p[2]]
            return ref if rows is None else ref.at[rows]

        def copy(kk, block, to, rows=None, src=None):
            return pltpu.make_async_remote_copy(
                src_ref=slot(block, rows) if src is None else src, dst_ref=slot(block, rows),
                send_sem=send_sems.at[kk], recv_sem=recv_sems.at[kk], device_id=to, device_id_type=MESH)

        mine = pltpu.make_async_copy(in_ref, slot(me), local_sem)
        mine.start()
        sent = [copy(0, me, sibling, src=in_ref), copy(1, me, (*xn, c), src=in_ref), copy(2, me, (*yn, c), src=in_ref)]
        for cp in sent:
            cp.start()

        def then(cps):
            for cp in cps:
                cp.start()
            sent.extend(cps)

        copy(1, (*xn, c), me).wait_recv()
        then([copy(5, (*xn, c), (*yn, c), rows=upper), copy(3, (*xn, c), sibling)])
        copy(2, (*yn, c), me).wait_recv()
        then([copy(6, (*yn, c), (*xn, c), rows=lower), copy(4, (*yn, c), sibling)])
        copy(5, (*diag, c), me, rows=upper).wait_recv()
        then([copy(7, (*diag, c), sibling, rows=upper)])
        copy(6, (*diag, c), me, rows=lower).wait_recv()
        then([copy(8, (*diag, c), sibling, rows=lower)])
        copy(0, sibling, me).wait_recv()
        copy(3, (*xn, 1 - c), me).wait_recv()
        copy(4, (*yn, 1 - c), me).wait_recv()
        copy(7, (*diag, 1 - c), me, rows=upper).wait_recv()
        copy(8, (*diag, 1 - c), me, rows=lower).wait_recv()
        for cp in sent:
            cp.wait_send()
        mine.wait()

    any_spec = pl.BlockSpec(memory_space=pl.ANY)
    return pl.pallas_call(
        body, name="all_gather_weights", in_specs=[any_spec], out_specs=any_spec,
        out_shape=S((N_DEV,) + arr.shape, arr.dtype),
        scratch_shapes=[pltpu.SemaphoreType.DMA((9,)), pltpu.SemaphoreType.DMA((9,)), pltpu.SemaphoreType.DMA],
    )(arr)


def _sibling_copies(ins, outs, send_sems, recv_sems):
    x, y, c = _position()
    return [pltpu.make_async_remote_copy(src_ref=ins[a].at[k, 1 - c], dst_ref=outs[a].at[k],
                                         send_sem=send_sems.at[a, k], recv_sem=recv_sems.at[a, k],
                                         device_id=(x, y, 1 - c), device_id_type=MESH)
            for a in range(len(ins)) for k in range(ins[a].shape[0])]


def _sibling_sems(arrs):
    shape = (max(len(arrs), 1), arrs[0].shape[0] if arrs else 1)
    return [pltpu.SemaphoreType.DMA(shape), pltpu.SemaphoreType.DMA(shape)]


def _chip_exchange_copies(ins, outs, send_sems, recv_sems):
    x, y, c = _position()
    chips = [(1 - x, y), (x, 1 - y), (1 - x, 1 - y)]
    return [pltpu.make_async_remote_copy(
        src_ref=ins[a].at[2 * qx + qy], dst_ref=outs[a].at[j], send_sem=send_sems.at[a, j],
        recv_sem=recv_sems.at[a, j], device_id=(qx, qy, c), device_id_type=MESH)
        for a in range(len(ins)) for j, (qx, qy) in enumerate(chips)]


def _broadcast_copies(srcs, dsts, send_sems, recv_sems):
    x, y, c = _position()
    me = 4 * x + 2 * y + c
    cps = []
    for a in range(len(srcs)):
        for k in range(1, N_DEV):
            peer = (1 - x if k & 4 else x, 1 - y if k & 2 else y, 1 - c if k & 1 else c)
            cps.append(pltpu.make_async_remote_copy(
                src_ref=srcs[a], dst_ref=dsts[a].at[me], send_sem=send_sems.at[a, k - 1],
                recv_sem=recv_sems.at[a, k - 1], device_id=peer, device_id_type=MESH))
    return me, cps


def _all_reduce_small(part):
    R, C = part.shape

    def body(p_ref, out_ref, gath_ref, send_sems, recv_sems):
        me, cps = _broadcast_copies([p_ref], [gath_ref], send_sems, recv_sems)
        gath_ref[me] = p_ref[...]
        for cp in cps:
            cp.start()
        for cp in cps:
            cp.wait()
        acc = gath_ref[0]
        for d in range(1, N_DEV):
            acc = acc + gath_ref[d]
        out_ref[...] = acc

    vm = pl.BlockSpec(memory_space=pltpu.VMEM)
    return pl.pallas_call(
        body, name="all_reduce_small", in_specs=[vm], out_specs=vm, out_shape=S((R, C), F32),
        scratch_shapes=[pltpu.VMEM((N_DEV, R, C), F32), pltpu.SemaphoreType.DMA((1, N_DEV - 1)),
                        pltpu.SemaphoreType.DMA((1, N_DEV - 1))],
    )(part)


def _pack(parts):
    rows = []
    for p in parts:
        f = p.reshape(-1).astype(F32)
        pad = (-f.shape[0]) % (8 * LANES)
        rows.append(jnp.pad(f, (0, pad)).reshape(-1, LANES))
    return jnp.concatenate(rows, axis=0)


def _unpack(buf, shapes):
    out, r = [], 0
    for shp in shapes:
        n = 1
        for s in shp:
            n *= s
        nr = -(-n // (8 * LANES)) * 8
        out.append(buf[r:r + nr].reshape(-1)[:n].reshape(shp))
        r += nr
    return out


def kernel(x, norm_w, w_in, conv_w, a_log, dt_bias, head_norm_w, sgu_ln_w, sgu_ln_b, w_spatial, b_spatial, w_out, final_norm_w, loss_target, m_norm_w, m_w_in, m_conv_w, m_a_log, m_dt_bias, m_head_norm_w, m_sgu_ln_w, m_sgu_ln_b, m_w_spatial, m_b_spatial, m_w_out, m_final_norm_w, v_norm_w, v_w_in, v_conv_w, v_a_log, v_dt_bias, v_head_norm_w, v_sgu_ln_w, v_sgu_ln_b, v_w_spatial, v_b_spatial, v_w_out, v_final_norm_w):
    T, DM = x.shape[1], x.shape[2]
    H, D = a_log.shape[1], head_norm_w.shape[1]
    G, P = w_spatial.shape[1], w_spatial.shape[2]
    AW, BW = H * D, G * P
    MIX = AW + BW
    WD = w_in.shape[2]
    IN = N_DEV * WD
    RO = w_out.shape[1]
    CW = conv_w.shape[2]
    sizes = (3 * AW, AW, H, H, BW, BW, BW)
    assert sum(sizes) == IN and 2 * H <= LANES and 3 * H <= 32 and N_DEV * RO == MIX and N_DEV * CW == 3 * AW
    offs = [0]
    for s in sizes:
        offs.append(offs[-1] + s)
    px, py, pc = _position()
    dev = 4 * px + 2 * py + pc
    chip = 2 * px + py

    x2, tgt = x[0], loss_target[0]

    g_win = _all_gather_weights(_cast_bf16_t(w_in[0].T, "cast_w_in"))
    w_main, w_ba = _relayout_w(g_win, offs[2], offs[4])
    alog_row = jnp.pad(a_log, ((0, 0), (H, LANES - 2 * H)))
    dtb_row = jnp.pad(dt_bias, ((0, 0), (H, LANES - 2 * H)))
    bs_t = b_spatial[0].T

    xn = _rms_xn(x2, norm_w)
    proj, ba, (g_wout, g_conv) = _in_proj(xn, w_main, w_ba, [_cast_bf16(w_out[0], "cast_w_out"), conv_w[0]])
    w_out_full = g_wout.reshape(MIX, DM)
    conv_full = g_conv.transpose(1, 0, 2).reshape(4, 3 * AW)
    q, k, v, c, gcol, grow = _prep_a_fwd(proj, ba, conv_full, alog_row, dtb_row, H, D)
    o, vnew, ssave, asave = _delta_fwd(q, k, v, gcol, grow, H, D)
    ocat = _mix_fwd(o, proj, head_norm_w, sgu_ln_w, sgu_ln_b, w_spatial[0], bs_t, H, D, G, P)
    dh, dh_bf, d_ocat, loss_acc, g_fnw = _out_proj_loss(ocat, w_out_full, x2, tgt, final_norm_w.reshape(1, DM))

    core_idx = jnp.reshape(pc, (1,)).astype(jnp.int32)
    chip_idx = jnp.reshape(chip, (1,)).astype(jnp.int32)
    g_wout_blocks = _grad_w(ocat, dh_bf, "grad_w_out").reshape(4, 2, RO, DM)
    (d_o, dmain, g_hnw, g_ln, g_wsp, g_bs_t), (sib_wout,) = _mix_bwd(
        d_ocat, o, proj, head_norm_w, sgu_ln_w, sgu_ln_b, w_spatial[0], bs_t, H, D, G, P, [g_wout_blocks])
    chip_wout = _pair_sum(g_wout_blocks, sib_wout, core_idx, "pair_sum_w_out")
    (dq, dk, dv, dgate, dpar), (recv_wout,) = _delta_bwd(
        q, k, v, gcol, grow, ba, vnew, ssave, asave, d_o, alog_row, dtb_row, H, D, [chip_wout])
    dmain, g_conv_part = _prep_a_bwd(dq, dk, dv, c, proj, conv_full, dmain, H, D)
    dba = dgate.astype(BF16)
    keep_win, sib_win = _grad_w_in(xn, dmain, dba, WD, offs[2], offs[4])
    chip_win = _pair_sum_plain(keep_win, sib_win, "pair_sum_w_in")
    small_shapes = [a_log.shape, dt_bias.shape, head_norm_w.shape, sgu_ln_w.shape, sgu_ln_b.shape,
                    w_spatial.shape, b_spatial.shape, final_norm_w.shape]
    parts = [dpar[0, H:2 * H], dpar[1, H:2 * H], g_hnw[0], g_ln[0], g_ln[1], g_wsp, g_bs_t[:, :G].T, g_fnw[0],
             g_conv_part[:4], loss_acc[0, :1]]
    grad_x, g_nw, small_gath, recv_win = _dx(dmain, dba, w_main, w_ba, x2, dh, norm_w, chip_win, _pack(parts), 4)
    red = _sum_slots(small_gath)
    grad_w_in, delta_w_in, new_m_w_in, new_v_w_in = _sum_adam(
        chip_win, recv_win, w_in[0].T, m_w_in[0].T, v_w_in[0].T, chip_idx, "sum_adam_w_in", transposed=True)
    grad_w_out, delta_w_out, new_m_w_out, new_v_w_out = _sum_adam(
        chip_wout, recv_wout, w_out[0], m_w_out[0], v_w_out[0], chip_idx, "sum_adam_w_out")
    red_nw = _all_reduce_small(_pack([g_nw[0]]))
    grads_small = _unpack(red_nw, [norm_w.shape]) + _unpack(red, small_shapes + [(4, 3 * AW), (1,)])
    loss = grads_small.pop()[0]
    g_conv_full = grads_small.pop()
    grad_conv = lax.dynamic_slice_in_dim(g_conv_full, dev * CW, CW, axis=1)[None]
    small_w = [norm_w, a_log, dt_bias, head_norm_w, sgu_ln_w, sgu_ln_b, w_spatial, b_spatial, final_norm_w, conv_w]
    small_m = [m_norm_w, m_a_log, m_dt_bias, m_head_norm_w, m_sgu_ln_w, m_sgu_ln_b, m_w_spatial, m_b_spatial,
               m_final_norm_w, m_conv_w]
    small_v = [v_norm_w, v_a_log, v_dt_bias, v_head_norm_w, v_sgu_ln_w, v_sgu_ln_b, v_w_spatial, v_b_spatial,
               v_final_norm_w, v_conv_w]
    small_g = grads_small + [grad_conv]
    shapes10 = [w.shape for w in small_w]
    d_p, m_p, v_p = _adam_small(_pack(small_w), _pack(small_g), _pack(small_m), _pack(small_v))
    d_s, m_s, v_s = _unpack(d_p, shapes10), _unpack(m_p, shapes10), _unpack(v_p, shapes10)

    def order(small, win, wout):
        return [small[0], win.T[None], small[9], small[1], small[2], small[3], small[4], small[5], small[6], small[7],
                wout[None], small[8]]

    grads = order(small_g, grad_w_in, grad_w_out)
    deltas = order(d_s, delta_w_in, delta_w_out)
    new_m = order(m_s, new_m_w_in, new_m_w_out)
    new_v = order(v_s, new_v_w_in, new_v_w_out)
    return (loss, grad_x[None], *grads, *deltas, *new_m, *new_v)
```

```python
import jax
import jax.numpy as jnp
from jax import lax
from jax.experimental import pallas as pl
from jax.experimental.pallas import tpu as pltpu

F32 = jnp.float32
BF16 = jnp.bfloat16
MXU = jnp.bfloat16
HI = lax.Precision.HIGHEST
EPS = 1e-6
CHUNK_A = 64
LANES = 128
MESH = pl.DeviceIdType.MESH
N_DEV = 8

ADAM_LR = 0.001
ADAM_B1 = 0.9
ADAM_B2 = 0.999
ADAM_EPS = 1e-08
ADAM_WD = 0.01
ADAM_STEP = 10

S = jax.ShapeDtypeStruct
ARB = "arbitrary"


def _cp(*sem, vmem_mib=56):
    return pltpu.CompilerParams(dimension_semantics=tuple(sem), vmem_limit_bytes=vmem_mib * 1024 * 1024)


def _tile(n, cap, mult):
    best = None
    t = mult
    while t <= min(n, cap):
        if n % t == 0:
            best = t
        t += mult
    return best if best is not None else n


def _mm(a, b):
    return jnp.dot(a.astype(MXU), b.astype(MXU), preferred_element_type=F32)


def _mm_nt(a, b):
    return lax.dot_general(a.astype(MXU), b.astype(MXU), (((1,), (1,)), ((), ())), preferred_element_type=F32)


def _mm_tn(a, b):
    return lax.dot_general(a.astype(MXU), b.astype(MXU), (((0,), (0,)), ((), ())), preferred_element_type=F32)


def _mmh(a, b):
    return jnp.dot(a, b, precision=HI, preferred_element_type=F32)


def _sigmoid(x):
    return 1.0 / (1.0 + jnp.exp(-x))


def _silu(x):
    return x * _sigmoid(x)


def _dsilu(x):
    s = _sigmoid(x)
    return s * (1.0 + x * (1.0 - s))


def _softplus(x):
    return jnp.maximum(x, 0.0) + jnp.log(1.0 + jnp.exp(-jnp.abs(x)))


def _pieces(wd, gate_lo, gate_hi, total):
    out = []
    for d in range(N_DEV):
        lo, hi = d * wd, (d + 1) * wd
        for dest, a, b, shift in (("main", 0, gate_lo, 0), ("gate", gate_lo, gate_hi, -gate_lo),
                                  ("main", gate_hi, total, gate_lo - gate_hi)):
            s0, s1 = max(lo, a), min(hi, b)
            if s0 < s1:
                out.append((d, s0 - lo, s1 - lo, dest, s0 + shift))
    return out


def _cast_bf16(a, name):
    R, C = a.shape
    tr = _tile(R, 256, 16)

    def body(a_ref, o_ref):
        o_ref[...] = a_ref[...].astype(BF16)

    spec = pl.BlockSpec((tr, C), lambda i: (i, 0))
    return pl.pallas_call(body, name=name, grid=(R // tr,), in_specs=[spec], out_specs=spec,
                          out_shape=S((R, C), BF16), compiler_params=_cp(ARB))(a)


def _cast_bf16_t(a_t, name):
    C, R = a_t.shape
    tr = _tile(R, 256, LANES)

    def body(a_ref, o_ref):
        o_ref[...] = a_ref[...].T.astype(BF16)

    return pl.pallas_call(body, name=name, grid=(R // tr,), in_specs=[pl.BlockSpec((C, tr), lambda i: (0, i))],
                          out_specs=pl.BlockSpec((tr, C), lambda i: (i, 0)),
                          out_shape=S((R, C), BF16), compiler_params=_cp(ARB))(a_t)


def _relayout_w(g_win, gate_lo, gate_hi):
    _, DM, WD = g_win.shape
    total = N_DEV * WD
    NM = total - (gate_hi - gate_lo)
    tr = _tile(DM, 256, 16)
    plan = _pieces(WD, gate_lo, gate_hi, total)

    def body(g_ref, main_ref, gate_ref):
        gate_ref[...] = jnp.zeros_like(gate_ref)
        for d, s0, s1, dest, c0 in plan:
            dst = main_ref if dest == "main" else gate_ref
            dst[:, c0:c0 + (s1 - s0)] = g_ref[d, :, s0:s1]

    return pl.pallas_call(
        body, name="relayout_w", grid=(DM // tr,),
        in_specs=[pl.BlockSpec((N_DEV, tr, WD), lambda i: (0, i, 0))],
        out_specs=[pl.BlockSpec((tr, NM), lambda i: (i, 0)), pl.BlockSpec((tr, LANES), lambda i: (i, 0))],
        out_shape=[S((DM, NM), g_win.dtype), S((DM, LANES), g_win.dtype)],
        compiler_params=_cp(ARB),
    )(g_win)


def _in_proj(xn, w_main, w_ba, shards):
    T, DM = xn.shape
    NM = w_main.shape[1]
    tm = _tile(T, 2048, 16)
    tn = _tile(NM, 1024, LANES)
    ni, nj = T // tm, NM // tn
    ns = len(shards)

    def body(xn_ref, w_ref, wba_ref, *rest):
        srcs = rest[:ns]
        proj_ref, ba_ref = rest[ns:ns + 2]
        gath = rest[ns + 2:2 * ns + 2]
        send_sems, recv_sems, local_sems = rest[2 * ns + 2:]
        i = pl.program_id(0)
        j = pl.program_id(1)
        me, cps = _broadcast_copies(srcs, gath, send_sems, recv_sems)
        cps = cps + [pltpu.make_async_copy(srcs[a], gath[a].at[me], local_sems.at[a]) for a in range(ns)]

        @pl.when((i == 0) & (j == 0))
        def _():
            for cp in cps:
                cp.start()

        @pl.when(j == 0)
        def _():
            ba_ref[...] = jnp.dot(xn_ref[...].astype(MXU), wba_ref[...].astype(MXU), preferred_element_type=F32)

        proj_ref[...] = jnp.dot(xn_ref[...].astype(MXU), w_ref[...].astype(MXU), preferred_element_type=F32)

        @pl.when((i == ni - 1) & (j == nj - 1))
        def _():
            for cp in cps:
                cp.wait()

    any_spec = pl.BlockSpec(memory_space=pl.ANY)
    res = pl.pallas_call(
        body, name="in_proj", grid=(ni, nj),
        in_specs=[pl.BlockSpec((tm, DM), lambda i, j: (i, 0)),
                  pl.BlockSpec((DM, tn), lambda i, j: (0, j)),
                  pl.BlockSpec((DM, LANES), lambda i, j: (0, 0))] + [any_spec] * ns,
        out_specs=[pl.BlockSpec((tm, tn), lambda i, j: (i, j)),
                   pl.BlockSpec((tm, LANES), lambda i, j: (i, 0))] + [any_spec] * ns,
        out_shape=[S((T, NM), F32), S((T, LANES), F32)] + [S((N_DEV,) + a.shape, a.dtype) for a in shards],
        scratch_shapes=[pltpu.SemaphoreType.DMA((ns, N_DEV - 1)), pltpu.SemaphoreType.DMA((ns, N_DEV - 1)),
                        pltpu.SemaphoreType.DMA((ns,))],
        compiler_params=_cp(ARB, ARB, vmem_mib=58),
    )(xn, w_main, w_ba, *shards)
    return res[0], res[1], res[2:]


def _prep_a_fwd(proj, ba, conv_w, alog_row, dtb_row, H, D):
    T = proj.shape[0]
    AW = H * D
    C3 = 3 * AW
    tb = _tile(T, 256, CHUNK_A)
    nch = tb // CHUNK_A
    nblk = T // tb
    scale = float(D) ** -0.5

    def body(x_ref, halo_ref, ba_ref, cw_ref, al_ref, dt_ref, q_ref, k_ref, v_ref, c_ref, gcol_ref, grow_ref):
        i = pl.program_id(0)
        xv = x_ref[...]
        halo = halo_ref[...] * (i > 0).astype(F32)
        xp = jnp.concatenate([halo, xv], axis=0)
        cw = cw_ref[...]
        c = cw[0:1, :] * xp[5:5 + tb]
        for j in range(1, 4):
            c = c + cw[j:j + 1, :] * xp[5 + j:5 + j + tb]
        c_ref[...] = c
        a = _silu(c)
        for h in range(H):
            qh = a[:, h * D:(h + 1) * D]
            kh = a[:, AW + h * D:AW + (h + 1) * D]
            qr = lax.rsqrt(jnp.sum(qh * qh, axis=-1, keepdims=True) + EPS)
            kr = lax.rsqrt(jnp.sum(kh * kh, axis=-1, keepdims=True) + EPS)
            q_ref[:, h * D:(h + 1) * D] = qh * (qr * scale)
            k_ref[:, h * D:(h + 1) * D] = kh * kr
        v_ref[...] = a[:, 2 * AW:]

        bav = ba_ref[...]
        lane = lax.broadcasted_iota(jnp.int32, (tb, LANES), 1)
        beta = _sigmoid(bav)
        g = -jnp.exp(al_ref[...]) * _softplus(bav + dt_ref[...])
        gates = jnp.where(lane < H, beta, jnp.where(lane < 2 * H, g, 0.0))
        ri = lax.broadcasted_iota(jnp.int32, (CHUNK_A, CHUNK_A), 0)
        ci = lax.broadcasted_iota(jnp.int32, (CHUNK_A, CHUNK_A), 1)
        tri = (ri >= ci).astype(F32)
        lane_c = lax.broadcasted_iota(jnp.int32, (CHUNK_A, LANES), 1)
        for cc in range(nch):
            gch = gates[cc * CHUNK_A:(cc + 1) * CHUNK_A]
            gc = pltpu.roll(_mmh(tri, gch), H, 1)
            full = jnp.where(lane_c < 2 * H, gch, jnp.where(lane_c < 3 * H, gc, 0.0))
            gcol_ref[cc * CHUNK_A:(cc + 1) * CHUNK_A, :] = full
            grow_ref[cc] = full.T[0:32, :]

    return pl.pallas_call(
        body, name="prep_a_fwd", grid=(nblk,),
        in_specs=[pl.BlockSpec((tb, C3), lambda i: (i, 0)),
                  pl.BlockSpec((8, C3), lambda i: (jnp.maximum(i * (tb // 8) - 1, 0), 0)),
                  pl.BlockSpec((tb, LANES), lambda i: (i, 0)),
                  pl.BlockSpec((4, C3), lambda i: (0, 0)),
                  pl.BlockSpec((1, LANES), lambda i: (0, 0)),
                  pl.BlockSpec((1, LANES), lambda i: (0, 0))],
        out_specs=[pl.BlockSpec((tb, AW), lambda i: (i, 0)),
                   pl.BlockSpec((tb, AW), lambda i: (i, 0)),
                   pl.BlockSpec((tb, AW), lambda i: (i, 0)),
                   pl.BlockSpec((tb, C3), lambda i: (i, 0)),
                   pl.BlockSpec((tb, LANES), lambda i: (i, 0)),
                   pl.BlockSpec((nch, 32, CHUNK_A), lambda i: (i, 0, 0))],
        out_shape=[S((T, AW), F32), S((T, AW), F32), S((T, AW), F32), S((T, C3), F32),
                   S((T, LANES), F32), S((T // CHUNK_A, 32, CHUNK_A), F32)],
        compiler_params=_cp(ARB),
    )(proj, proj, ba, conv_w, alog_row, dtb_row)


_NN = (((1,), (0,)), ((), ()))
_TN = (((0,), (0,)), ((), ()))


def _split(a):
    hi = a.astype(BF16)
    return hi, (a - hi.astype(F32)).astype(BF16)


def _mm3(a, b, dims=_NN):
    ah, al = a if isinstance(a, tuple) else _split(a)
    bh, bl = b if isinstance(b, tuple) else _split(b)
    dg = lambda p, r: lax.dot_general(p, r, dims, preferred_element_type=F32)
    return dg(ah, bh) + (dg(ah, bl) + dg(al, bh))


def _interleave(gens):
    gens = list(gens)
    while gens:
        alive = []
        for g in gens:
            try:
                next(g)
                alive.append(g)
            except StopIteration:
                pass
        gens = alive


def _chunk_terms(q, k, v, gcolv, growv, h, H):
    C = CHUNK_A
    beta_c = gcolv[:, h:h + 1]
    g_c = gcolv[:, H + h:H + h + 1]
    gc_c = gcolv[:, 2 * H + h:2 * H + h + 1]
    gc_r = growv[2 * H + h:2 * H + h + 1, :]
    ri = lax.broadcasted_iota(jnp.int32, (C, C), 0)
    ci = lax.broadcasted_iota(jnp.int32, (C, C), 1)
    incl = ri >= ci
    strict = ri > ci
    kb = k * beta_c
    vb = v * beta_c
    p_raw = _mm_nt(kb, k)
    qk_raw = _mm_nt(q, k)
    gam = jnp.where(incl, jnp.exp(jnp.where(incl, gc_c - gc_r, 0.0)), 0.0)
    e_c = jnp.exp(gc_c)
    gl = gc_r[:, C - 1:C]
    edec = jnp.exp(gl - gc_c)
    yield
    lmat = jnp.where(strict, p_raw * gam, 0.0)
    attn = jnp.where(incl, qk_raw * gam, 0.0)
    return dict(beta_c=beta_c, g_c=g_c, gc_c=gc_c, gc_r=gc_r, incl=incl, strict=strict, gam=gam, e_c=e_c,
                kb=kb, vb=vb, lmat=lmat, attn=attn, gl=gl, edec=edec, ri=ri, ci=ci)


def _inv_unit_lower(lmat):
    C = lmat.shape[0]
    ri = lax.broadcasted_iota(jnp.int32, (C, C), 0)
    ci = lax.broadcasted_iota(jnp.int32, (C, C), 1)
    eye = (ri == ci).astype(F32)
    x = -lmat
    a = eye + x
    n = 1
    while 2 * n < C:
        xs = _split(x)
        x = _mm3(xs, xs)
        yield
        a = a + _mm3(a, x)
        n *= 2
    yield
    return a


def _delta_fwd(q, k, v, gcol, grow, H, D):
    T = q.shape[0]
    C = CHUNK_A
    N = T // C
    AW = H * D
    CPS = 2 if N % 2 == 0 else 1

    def body(q_ref, k_ref, v_ref, gcol_ref, grow_ref, o_ref, vn_ref, ssave_ref, asave_ref, s_ref):
        @pl.when(pl.program_id(0) == 0)
        def _():
            s_ref[...] = jnp.zeros_like(s_ref)

        state = {(0, h): s_ref[h] for h in range(H)}

        def head(cc, h):
            rows = slice(cc * C, (cc + 1) * C)
            sl = slice(h * D, (h + 1) * D)
            qv, kv, vv = q_ref[rows, sl], k_ref[rows, sl], v_ref[rows, sl]
            t = yield from _chunk_terms(qv, kv, vv, gcol_ref[rows, :], grow_ref[cc], h, H)
            a = yield from _inv_unit_lower(t["lmat"])
            asave_ref[cc, h] = a
            while (cc, h) not in state:
                yield
            st = state[(cc, h)]
            ssave_ref[cc, h] = st
            ks = _mm(t["kb"] * t["e_c"], st)
            o_inter = _mm(qv * t["e_c"], st)
            yield
            v_new = _mm3(a, t["vb"] - ks)
            yield
            vn_ref[rows, sl] = v_new
            o_intra = _mm(t["attn"], v_new)
            s_upd = _mm_tn(kv * t["edec"], v_new)
            yield
            o_ref[rows, sl] = o_inter + o_intra
            state[(cc + 1, h)] = st * jnp.exp(t["gl"]) + s_upd

        _interleave(head(cc, h) for cc in range(CPS) for h in range(H))
        for h in range(H):
            s_ref[h] = state[(CPS, h)]

    blk = lambda: pl.BlockSpec((CPS * C, AW), lambda n: (n, 0))
    return pl.pallas_call(
        body, name="delta_fwd", grid=(N // CPS,),
        in_specs=[blk(), blk(), blk(),
                  pl.BlockSpec((CPS * C, LANES), lambda n: (n, 0)),
                  pl.BlockSpec((CPS, 32, C), lambda n: (n, 0, 0))],
        out_specs=[blk(), blk(),
                   pl.BlockSpec((CPS, H, D, D), lambda n: (n, 0, 0, 0)),
                   pl.BlockSpec((CPS, H, C, C), lambda n: (n, 0, 0, 0))],
        out_shape=[S((T, AW), F32), S((T, AW), F32), S((N, H, D, D), F32), S((N, H, C, C), F32)],
        scratch_shapes=[pltpu.VMEM((H, D, D), F32)],
        compiler_params=_cp(ARB),
    )(q, k, v, gcol, grow)


def _delta_bwd(q, k, v, gcol, grow, ba, vnew, ssave, asave, d_o, a_log, dt_bias, H, D, carry):
    T = q.shape[0]
    C = CHUNK_A
    N = T // C
    AW = H * D
    nc = len(carry)
    CPS = 2 if N % 2 == 0 else 1
    NS = N // CPS

    def body(al_ref, dt_ref, q_ref, k_ref, v_ref, gcol_ref, grow_ref, ba_ref, vn_ref, ss_ref, as_ref, do_ref, *rest):
        cins = rest[:nc]
        dq_ref, dk_ref, dv_ref, dgate_ref, dpar_ref = rest[nc:nc + 5]
        couts = rest[nc + 5:2 * nc + 5]
        ds_ref, csend, crecv = rest[2 * nc + 5:]
        ccps = _chip_exchange_copies(cins, couts, csend, crecv)

        @pl.when(pl.program_id(0) == 0)
        def _():
            ds_ref[...] = jnp.zeros_like(ds_ref)
            dpar_ref[...] = jnp.zeros_like(dpar_ref)
            for cp in ccps:
                cp.start()

        lane = lax.broadcasted_iota(jnp.int32, (C, LANES), 1)
        rowi = lax.broadcasted_iota(jnp.int32, (C, 1), 0)
        acc = {cc: jnp.zeros((C, LANES), F32) for cc in range(CPS)}
        state = {(0, h): ds_ref[h] for h in range(H)}

        def head(oi, h):
            cc = CPS - 1 - oi
            rows = slice(cc * C, (cc + 1) * C)
            sl = slice(h * D, (h + 1) * D)
            st = ss_ref[cc, h]
            a = as_ref[cc, h]
            qv, kv, vv, dov, v_new = q_ref[rows, sl], k_ref[rows, sl], v_ref[rows, sl], do_ref[rows, sl], vn_ref[rows, sl]
            t = yield from _chunk_terms(qv, kv, vv, gcol_ref[rows, :], grow_ref[cc], h, H)
            beta_c, e_c, gam, kb = t["beta_c"], t["e_c"], t["gam"], t["kb"]
            incl, strict, attn, lmat, edec = t["incl"], t["strict"], t["attn"], t["lmat"], t["edec"]
            kdec = kv * edec
            egl = jnp.exp(t["gl"])
            qe = qv * e_c
            ekb = kb * e_c

            t1 = _mm_nt(dov, st)
            ds_o = _mm_tn(qe, dov)
            dattn_raw = _mm_nt(dov, v_new)
            dv_new_o = _mm_tn(attn, dov)
            yield
            while (oi, h) not in state:
                yield
            ds_next = state[(oi, h)]
            dkdec = _mm_nt(v_new, ds_next)
            dv_new_s = _mm(kdec, ds_next)
            yield
            dgl = egl * jnp.sum(jnp.sum(st * ds_next, axis=1, keepdims=True), axis=0, keepdims=True)
            dk = edec * dkdec
            r = jnp.sum(dkdec * kdec, axis=1, keepdims=True)
            dgc = -r
            dgl = dgl + jnp.sum(r, axis=0, keepdims=True)
            dq = e_c * t1
            dgc = dgc + jnp.sum(t1 * qe, axis=1, keepdims=True)
            dattn = jnp.where(incl, dattn_raw, 0.0)
            dv_new = dv_new_s + dv_new_o
            dqm = dattn * gam
            z = dattn * attn
            dvb = _mm3(a, dv_new, _TN)
            dq_a = _mm(dqm, kv)
            dk_a = _mm_tn(dqm, qv)
            yield
            dq_ref[rows, sl] = dq + dq_a
            dv_ref[rows, sl] = beta_c * dvb
            ds_kb = _mm_tn(ekb, dvb)
            dekb_neg = _mm_nt(dvb, st)
            dl_neg = _mm_nt(dvb, v_new)
            yield
            state[(oi + 1, h)] = egl * ds_next + ds_o - ds_kb
            dekb = -dekb_neg
            dl = jnp.where(strict, -dl_neg, 0.0)
            dp = dl * gam
            z = z + dl * lmat
            dkb_p = _mm(dp, kv)
            dk_p = _mm_tn(dp, kb)
            dgc = dgc + jnp.sum(dekb * ekb, axis=1, keepdims=True)
            dgc = dgc + jnp.sum(z, axis=1, keepdims=True) - jnp.sum(z.T, axis=1, keepdims=True)
            dgc = dgc + jnp.where(rowi == C - 1, dgl, 0.0)
            yield
            dkb = dkb_p + e_c * dekb
            dk_ref[rows, sl] = dk + dk_a + dk_p + beta_c * dkb
            dbeta = jnp.sum(dkb * kv, axis=1, keepdims=True) + jnp.sum(dvb * vv, axis=1, keepdims=True)
            acc[cc] = acc[cc] + jnp.where(lane == h, dbeta, 0.0) + jnp.where(lane == H + h, dgc, 0.0)

        _interleave(head(oi, h) for oi in range(CPS) for h in range(H))
        for h in range(H):
            ds_ref[h] = state[(CPS, h)]
        ri = lax.broadcasted_iota(jnp.int32, (C, C), 0)
        ci = lax.broadcasted_iota(jnp.int32, (C, C), 1)
        upper = (ri <= ci).astype(F32)
        dal = jnp.zeros((1, LANES), F32)
        ddt = jnp.zeros((1, LANES), F32)
        for cc in range(CPS):
            rows = slice(cc * C, (cc + 1) * C)
            gates = gcol_ref[rows, :]
            dg_all = _mm3(upper, acc[cc])
            d_braw = acc[cc] * gates * (1.0 - gates)
            d_araw = dg_all * (-jnp.exp(al_ref[...])) * _sigmoid(ba_ref[rows, :] + dt_ref[...])
            dgate_ref[rows, :] = jnp.where(lane < H, d_braw, jnp.where(lane < 2 * H, d_araw, 0.0))
            dal = dal + jnp.sum(dg_all * gates, axis=0, keepdims=True)
            ddt = ddt + jnp.sum(d_araw, axis=0, keepdims=True)
        dpar_ref[0:1, :] += dal
        dpar_ref[1:2, :] += ddt

        @pl.when(pl.program_id(0) == NS - 1)
        def _():
            for cp in ccps:
                cp.wait()

    rev = lambda s: NS - 1 - s
    blk = lambda: pl.BlockSpec((CPS * C, AW), lambda s: (rev(s), 0))
    row = pl.BlockSpec((1, LANES), lambda s: (0, 0))
    any_spec = pl.BlockSpec(memory_space=pl.ANY)
    res = pl.pallas_call(
        body, name="delta_bwd", grid=(NS,),
        in_specs=[row, row, blk(), blk(), blk(),
                  pl.BlockSpec((CPS * C, LANES), lambda s: (rev(s), 0)),
                  pl.BlockSpec((CPS, 32, C), lambda s: (rev(s), 0, 0)),
                  pl.BlockSpec((CPS * C, LANES), lambda s: (rev(s), 0)),
                  blk(),
                  pl.BlockSpec((CPS, H, D, D), lambda s: (rev(s), 0, 0, 0)),
                  pl.BlockSpec((CPS, H, C, C), lambda s: (rev(s), 0, 0, 0)),
                  blk()] + [any_spec] * nc,
        out_specs=[blk(), blk(), blk(),
                   pl.BlockSpec((CPS * C, LANES), lambda s: (rev(s), 0)),
                   pl.BlockSpec((8, LANES), lambda s: (0, 0))] + [any_spec] * nc,
        out_shape=[S((T, AW), F32), S((T, AW), F32), S((T, AW), F32),
                   S((T, LANES), F32), S((8, LANES), F32)] + [S((3,) + a.shape[1:], a.dtype) for a in carry],
        scratch_shapes=[pltpu.VMEM((H, D, D), F32),
                        pltpu.SemaphoreType.DMA((max(nc, 1), 3)), pltpu.SemaphoreType.DMA((max(nc, 1), 3))],
        compiler_params=_cp(ARB),
    )(a_log, dt_bias, q, k, v, gcol, grow, ba, vnew, ssave, asave, d_o, *carry)
    return res[:5], res[5:]


def _ln_stats(xv):
    mu = jnp.mean(xv, axis=-1, keepdims=True)
    xc = xv - mu
    var = jnp.mean(xc * xc, axis=-1, keepdims=True)
    rstd = lax.rsqrt(var + EPS)
    return xc * rstd, rstd


def _mix_fwd(o, proj, head_norm_w, ln_w, ln_b, w_sp, bs_t, H, D, G, P):
    T = o.shape[0]
    AW, BW = H * D, G * P
    MIX = AW + BW
    nb = AW // BW if AW % BW == 0 else None
    assert nb == 1, "group widths must match the projection column blocks"
    cb = 3

    def body(o_ref, za_ref, ub_ref, vb_ref, zb_ref, hw_ref, lw_ref, lb_ref, w_ref, bs_ref, out_ref):
        hw = hw_ref[...]
        for h in range(H):
            sl = slice(h * D, (h + 1) * D)
            oh = o_ref[:, sl]
            rs = lax.rsqrt(jnp.mean(oh * oh, axis=-1, keepdims=True) + EPS)
            out_ref[:, sl] = (oh * rs * hw * _silu(za_ref[:, sl])).astype(BF16)
        xhat, _ = _ln_stats(vb_ref[...])
        vn = xhat * lw_ref[...] + lb_ref[...]
        ri = lax.broadcasted_iota(jnp.int32, (P, P), 0)
        ci = lax.broadcasted_iota(jnp.int32, (P, P), 1)
        bsv = bs_ref[...]
        for g in range(G):
            sl = slice(g * P, (g + 1) * P)
            wm = jnp.where(ri >= ci, w_ref[g], 0.0)
            s = _mm(wm, vn[:, sl]) + bsv[:, g:g + 1]
            out_ref[:, AW + g * P:AW + (g + 1) * P] = (ub_ref[:, sl] * s * _silu(zb_ref[:, sl])).astype(BF16)

    row = lambda w: pl.BlockSpec((1, w), lambda i: (0, 0))
    return pl.pallas_call(
        body, name="mix_fwd", grid=(T // P,),
        in_specs=[pl.BlockSpec((P, AW), lambda i: (i, 0)),
                  pl.BlockSpec((P, AW), lambda i: (i, cb)),
                  pl.BlockSpec((P, BW), lambda i: (i, cb + 1)),
                  pl.BlockSpec((P, BW), lambda i: (i, cb + 2)),
                  pl.BlockSpec((P, BW), lambda i: (i, cb + 3)),
                  row(D), row(BW), row(BW),
                  pl.BlockSpec((G, P, P), lambda i: (0, 0, 0)),
                  pl.BlockSpec((P, G), lambda i: (0, 0))],
        out_specs=pl.BlockSpec((P, MIX), lambda i: (i, 0)),
        out_shape=S((T, MIX), BF16),
        compiler_params=_cp(ARB),
    )(o, proj, proj, proj, proj, head_norm_w, ln_w, ln_b, w_sp, bs_t)


def _mix_bwd(d_ocat, o, proj, head_norm_w, ln_w, ln_b, w_sp, bs_t, H, D, G, P, carry):
    T = o.shape[0]
    AW, BW = H * D, G * P
    MIX = AW + BW
    cb = 3
    nc = len(carry)

    def body(dc_ref, o_ref, za_ref, ub_ref, vb_ref, zb_ref, hw_ref, lw_ref, lb_ref, w_ref, bs_ref, *rest):
        cins = rest[:nc]
        do_ref, dmain_ref, dhw_ref, dln_ref, dw_ref, dbs_ref = rest[nc:nc + 6]
        couts = rest[nc + 6:2 * nc + 6]
        dvn_ref, drest_ref, out_sems, csend, crecv = rest[2 * nc + 6:]
        i = pl.program_id(0)
        slot = lax.rem(i, 2)
        ccps = _sibling_copies(cins, couts, csend, crecv)

        def out_copy(step, s):
            return pltpu.make_async_copy(
                drest_ref.at[s], dmain_ref.at[pl.ds(step * P, P), pl.ds(cb * AW, AW + 3 * BW)], out_sems.at[s])

        @pl.when(i == 0)
        def _():
            dhw_ref[...] = jnp.zeros_like(dhw_ref)
            dln_ref[...] = jnp.zeros_like(dln_ref)
            dw_ref[...] = jnp.zeros_like(dw_ref)
            dbs_ref[...] = jnp.zeros_like(dbs_ref)
            for cp in ccps:
                cp.start()

        @pl.when(i >= 2)
        def _():
            out_copy(i - 2, slot).wait()

        hw = hw_ref[...]
        dhw = jnp.zeros((1, D), F32)
        for h in range(H):
            sl = slice(h * D, (h + 1) * D)
            oh = o_ref[:, sl]
            za = za_ref[:, sl]
            doa = dc_ref[:, sl]
            rs = lax.rsqrt(jnp.mean(oh * oh, axis=-1, keepdims=True) + EPS)
            xh = oh * rs
            d_on = doa * _silu(za)
            drest_ref[slot, :, sl] = (doa * (xh * hw) * _dsilu(za)).astype(BF16)
            dhw = dhw + jnp.sum(d_on * xh, axis=0, keepdims=True)
            dxh = d_on * hw
            do_ref[:, sl] = rs * (dxh - xh * jnp.mean(dxh * xh, axis=-1, keepdims=True))
        dhw_ref[0:1, :] += dhw

        xhat, rstd = _ln_stats(vb_ref[...])
        lw = lw_ref[...]
        vn = xhat * lw + lb_ref[...]
        ri = lax.broadcasted_iota(jnp.int32, (P, P), 0)
        ci = lax.broadcasted_iota(jnp.int32, (P, P), 1)
        lane = lax.broadcasted_iota(jnp.int32, (P, LANES), 1)
        bsv = bs_ref[...]
        dbs = jnp.zeros((P, LANES), F32)
        for g in range(G):
            sl = slice(g * P, (g + 1) * P)
            wm = jnp.where(ri >= ci, w_ref[g], 0.0)
            vng = vn[:, sl]
            s = _mm(wm, vng) + bsv[:, g:g + 1]
            dob = dc_ref[:, AW + g * P:AW + (g + 1) * P]
            ub = ub_ref[:, sl]
            zb = zb_ref[:, sl]
            szb = _silu(zb)
            drest_ref[slot, :, AW + g * P:AW + (g + 1) * P] = (dob * s * szb).astype(BF16)
            drest_ref[slot, :, AW + 2 * BW + g * P:AW + 2 * BW + (g + 1) * P] = (
                dob * ub * s * _dsilu(zb)).astype(BF16)
            ds = dob * ub * szb
            dvn_ref[:, sl] = _mm_tn(wm, ds)
            dw_ref[g] += jnp.where(ri >= ci, _mm_nt(ds, vng), 0.0)
            dbs = dbs + jnp.where(lane == g, jnp.sum(ds, axis=1, keepdims=True), 0.0)
        dbs_ref[...] += dbs
        dvn = dvn_ref[...]
        dln_ref[0:1, :] += jnp.sum(dvn * xhat, axis=0, keepdims=True)
        dln_ref[1:2, :] += jnp.sum(dvn, axis=0, keepdims=True)
        dxh = dvn * lw
        dvb = rstd * (dxh - jnp.mean(dxh, axis=-1, keepdims=True) - xhat * jnp.mean(dxh * xhat, axis=-1, keepdims=True))
        drest_ref[slot, :, AW + BW:AW + 2 * BW] = dvb.astype(BF16)

        out_copy(i, slot).start()

        @pl.when(i == nstep - 1)
        def _():
            out_copy(i, slot).wait()
            if nstep > 1:
                out_copy(i - 1, 1 - slot).wait()
            for cp in ccps:
                cp.wait()

    nstep = T // P
    row = lambda w: pl.BlockSpec((1, w), lambda i: (0, 0))
    any_spec = pl.BlockSpec(memory_space=pl.ANY)
    res = pl.pallas_call(
        body, name="mix_bwd", grid=(nstep,),
        in_specs=[pl.BlockSpec((P, MIX), lambda i: (i, 0)),
                  pl.BlockSpec((P, AW), lambda i: (i, 0)),
                  pl.BlockSpec((P, AW), lambda i: (i, cb)),
                  pl.BlockSpec((P, BW), lambda i: (i, cb + 1)),
                  pl.BlockSpec((P, BW), lambda i: (i, cb + 2)),
                  pl.BlockSpec((P, BW), lambda i: (i, cb + 3)),
                  row(D), row(BW), row(BW),
                  pl.BlockSpec((G, P, P), lambda i: (0, 0, 0)),
                  pl.BlockSpec((P, G), lambda i: (0, 0))] + [any_spec] * nc,
        out_specs=[pl.BlockSpec((P, AW), lambda i: (i, 0)),
                   any_spec,
                   pl.BlockSpec((8, D), lambda i: (0, 0)),
                   pl.BlockSpec((8, BW), lambda i: (0, 0)),
                   pl.BlockSpec((G, P, P), lambda i: (0, 0, 0)),
                   pl.BlockSpec((P, LANES), lambda i: (0, 0))] + [any_spec] * nc,
        out_shape=[S((T, AW), F32), S((T, cb * AW + AW + 3 * BW), BF16), S((8, D), F32), S((8, BW), F32),
                   S((G, P, P), F32), S((P, LANES), F32)] + [S(a.shape[:1] + a.shape[2:], a.dtype) for a in carry],
        scratch_shapes=[pltpu.VMEM((P, BW), F32), pltpu.VMEM((2, P, AW + 3 * BW), BF16),
                        pltpu.SemaphoreType.DMA((2,))] + _sibling_sems(carry),
        compiler_params=_cp(ARB),
    )(d_ocat, o, proj, proj, proj, proj, head_norm_w, ln_w, ln_b, w_sp, bs_t, *carry)
    return res[:6], res[6:]


def _out_proj_loss(ocat, w_out, x, target, fnw):
    T, MIX = ocat.shape
    DM = x.shape[1]
    tm = _tile(T, 256, 8)

    def body(oc_ref, w_ref, x_ref, t_ref, fw_ref, dh_ref, dhb_ref, doc_ref, loss_ref, gfw_ref):
        @pl.when(pl.program_id(0) == 0)
        def _():
            loss_ref[...] = jnp.zeros_like(loss_ref)
            gfw_ref[...] = jnp.zeros_like(gfw_ref)

        wv = w_ref[...]
        hh = x_ref[...] + jnp.dot(oc_ref[...].astype(MXU), wv.astype(MXU), preferred_element_type=F32)
        rs = lax.rsqrt(jnp.mean(hh * hh, axis=-1, keepdims=True) + EPS)
        hn = hh * rs
        fw = fw_ref[...]
        e = hn * fw - t_ref[...]
        row_loss = 0.5 * jnp.mean(e * e, axis=-1, keepdims=True)
        loss_ref[...] += jnp.sum(row_loss, axis=0, keepdims=True)
        dy = e * (1.0 / DM)
        gfw_ref[0:1, :] += jnp.sum(dy * hn, axis=0, keepdims=True)
        dhn = dy * fw
        dh = rs * (dhn - hn * jnp.mean(dhn * hn, axis=-1, keepdims=True))
        dh_ref[...] = dh
        dhb = dh.astype(BF16)
        dhb_ref[...] = dhb
        doc_ref[...] = _mm_nt(dhb, wv)

    return pl.pallas_call(
        body, name="out_proj_loss", grid=(T // tm,),
        in_specs=[pl.BlockSpec((tm, MIX), lambda i: (i, 0)),
                  pl.BlockSpec((MIX, DM), lambda i: (0, 0)),
                  pl.BlockSpec((tm, DM), lambda i: (i, 0)),
                  pl.BlockSpec((tm, DM), lambda i: (i, 0)),
                  pl.BlockSpec((1, DM), lambda i: (0, 0))],
        out_specs=[pl.BlockSpec((tm, DM), lambda i: (i, 0)),
                   pl.BlockSpec((tm, DM), lambda i: (i, 0)),
                   pl.BlockSpec((tm, MIX), lambda i: (i, 0)),
                   pl.BlockSpec((8, LANES), lambda i: (0, 0)),
                   pl.BlockSpec((8, DM), lambda i: (0, 0))],
        out_shape=[S((T, DM), F32), S((T, DM), BF16), S((T, MIX), F32), S((8, LANES), F32), S((8, DM), F32)],
        compiler_params=_cp(ARB),
    )(ocat, w_out, x, target, fnw)


def _grad_w(lhs, rhs, name):
    T, A = lhs.shape
    B = rhs.shape[1]
    ta = _tile(A, 512, LANES)
    tk = _tile(T, 1024, 16)
    nk = T // tk

    def body(l_ref, r_ref, out_ref, acc_ref):
        k = pl.program_id(1)
        part = _mm_tn(l_ref[...], r_ref[...])

        @pl.when(k == 0)
        def _():
            acc_ref[...] = part

        @pl.when(k > 0)
        def _():
            acc_ref[...] += part

        @pl.when(k == nk - 1)
        def _():
            out_ref[...] = acc_ref[...].astype(BF16)

    return pl.pallas_call(
        body, name=name, grid=(A // ta, nk),
        in_specs=[pl.BlockSpec((tk, ta), lambda i, k: (k, i)),
                  pl.BlockSpec((tk, B), lambda i, k: (k, 0))],
        out_specs=pl.BlockSpec((ta, B), lambda i, k: (i, 0)),
        out_shape=S((A, B), BF16),
        scratch_shapes=[pltpu.VMEM((ta, B), F32)],
        compiler_params=_cp(ARB, ARB),
    )(lhs, rhs)


def _grad_w_in(xn, dmain, dba, WD, gate_lo, gate_hi):
    T, DM = xn.shape
    NM = dmain.shape[1]
    tn = _tile(NM, 1024, LANES)
    tk = _tile(T, 2048, 16)
    nj, nk = NM // tn, T // tk
    ND = N_DEV
    tiles = [[] for _ in range(nj)]
    first_tile, last_tile = {}, {}
    for d, s0, s1, dest, c0 in _pieces(WD, gate_lo, gate_hi, ND * WD):
        if dest != "main":
            continue
        while s0 < s1:
            jj = c0 // tn
            w = min(s1 - s0, (jj + 1) * tn - c0)
            tiles[jj].append((d, s0, w, "main", c0 - jj * tn))
            first_tile.setdefault(d, jj)
            last_tile[d] = jj
            s0, c0 = s0 + w, c0 + w
    for d, s0, s1, dest, c0 in _pieces(WD, gate_lo, gate_hi, ND * WD):
        if dest == "gate":
            tiles[first_tile[d]].append((d, s0, s1 - s0, "gate", c0))
    assert sorted(first_tile) == list(range(ND)) and all(last_tile[d] <= first_tile[d + 2] for d in range(ND - 2))

    def body(xn_ref, dm_ref, dba_ref, keep_ref, recv_ref, acc_ref, gate_ref, buf_ref, lsem, ssem, rsem):
        j = pl.program_id(0)
        k = pl.program_id(1)
        px, py, pc = _position()

        @pl.when(k == 0)
        def _():
            acc_ref[...] = jnp.zeros_like(acc_ref)

        @pl.when((j == 0) & (k == 0))
        def _():
            gate_ref[...] = jnp.zeros_like(gate_ref)

        xv = xn_ref[...]
        acc_ref[...] += _mm_tn(xv, dm_ref[...])

        @pl.when(j == 0)
        def _():
            gate_ref[...] += _mm_tn(xv, dba_ref[...])

        def local(d):
            return pltpu.make_async_copy(buf_ref.at[d % 2], keep_ref.at[d // 2], lsem.at[d // 2])

        def remote(d):
            return pltpu.make_async_remote_copy(
                src_ref=buf_ref.at[d % 2], dst_ref=recv_ref.at[d // 2], send_sem=ssem.at[d // 2],
                recv_sem=rsem.at[d // 2], device_id=(px, py, 1 - pc), device_id_type=MESH)

        def leave(d, start):
            @pl.when(pc == d % 2)
            def _():
                local(d).start() if start else local(d).wait()

            @pl.when(pc != d % 2)
            def _():
                remote(d).start() if start else remote(d).wait_send()

        def emit(jj):
            shards = sorted({p[0] for p in tiles[jj]})
            for d in shards:
                if first_tile[d] == jj and d >= 2:
                    leave(d - 2, False)
                for dd, s0, w, src, c0 in tiles[jj]:
                    if dd == d:
                        ref = acc_ref if src == "main" else gate_ref
                        buf_ref[d % 2, :, s0:s0 + w] = ref[:, c0:c0 + w].astype(BF16)
                if last_tile[d] == jj:
                    leave(d, True)
            if jj == nj - 1:
                for d in (ND - 2, ND - 1):
                    leave(d, False)
                for q in range(ND // 2):
                    remote(2 * q).wait_recv()

        for jj in range(nj):
            @pl.when((j == jj) & (k == nk - 1))
            def _(jj=jj):
                emit(jj)

    any_spec = pl.BlockSpec(memory_space=pl.ANY)
    return pl.pallas_call(
        body, name="grad_w_in", grid=(nj, nk),
        in_specs=[pl.BlockSpec((tk, DM), lambda j, k: (k, 0)),
                  pl.BlockSpec((tk, tn), lambda j, k: (k, j)),
                  pl.BlockSpec((tk, LANES), lambda j, k: (k, 0))],
        out_specs=[any_spec, any_spec],
        out_shape=[S((ND // 2, DM, WD), BF16), S((ND // 2, DM, WD), BF16)],
        scratch_shapes=[pltpu.VMEM((DM, tn), F32), pltpu.VMEM((DM, LANES), F32), pltpu.VMEM((2, DM, WD), BF16),
                        pltpu.SemaphoreType.DMA((ND // 2,)), pltpu.SemaphoreType.DMA((ND // 2,)),
                        pltpu.SemaphoreType.DMA((ND // 2,))],
        compiler_params=_cp(ARB, ARB),
    )(xn, dmain, dba)


def _pair_sum_plain(a, b, name):
    K, R, C = a.shape
    tr = _tile(R, 256, 16)

    def body(a_ref, b_ref, o_ref):
        o_ref[...] = (a_ref[...].astype(F32) + b_ref[...].astype(F32)).astype(BF16)

    spec = lambda: pl.BlockSpec((1, tr, C), lambda q, i: (q, i, 0))
    return pl.pallas_call(body, name=name, grid=(K, R // tr), in_specs=[spec(), spec()], out_specs=spec(),
                          out_shape=S((K, R, C), BF16), compiler_params=_cp(ARB, ARB))(a, b)


def _dx_rows(T):
    tm = _tile(T, 512, 8)
    return tm if T // tm >= 2 else T // 2


def _dx_part(name, dmain, dba, w_main, w_ba, x, dh, norm_w, blk0, nblk, prev, hbm_in, hbm_alias, hbm_new, make_copies):
    T, NM = dmain.shape
    DM = x.shape[1]
    tm = _dx_rows(T)
    tk = _tile(NM, 1024, LANES)
    nk = NM // tk
    n_in, n_al, n_new = len(hbm_in), len(hbm_alias), len(hbm_new)
    n_prev = 0 if prev is None else 2
    last_step = nblk * nk - 1

    def body(dm_ref, dba_ref, w_ref, wba_ref, x_ref, dh_ref, nw_ref, *rest):
        r = list(rest)
        gnw_prev_ref = r.pop(0) if n_prev else None
        if n_prev:
            r.pop(0)
        in_refs = [r.pop(0) for _ in range(n_in)]
        del r[:n_al]
        gx_ref, gnw_ref = r.pop(0), r.pop(0)
        alias_refs = [r.pop(0) for _ in range(n_al)]
        new_refs = [r.pop(0) for _ in range(n_new)]
        acc_ref, send_sems, recv_sems = r
        i = pl.program_id(0)
        k = pl.program_id(1)
        step = i * nk + k
        cps = make_copies(in_refs, alias_refs, new_refs, send_sems, recv_sems)

        @pl.when(step == 0)
        def _():
            gnw_ref[...] = gnw_prev_ref[...] if n_prev else jnp.zeros_like(gnw_ref)
            for cp in cps:
                cp.start()

        @pl.when(k == 0)
        def _():
            acc_ref[...] = _mm_nt(dba_ref[...], wba_ref[...])

        acc_ref[...] += _mm_nt(dm_ref[...], w_ref[...])

        @pl.when(k == nk - 1)
        def _():
            xv = x_ref[...]
            rs = lax.rsqrt(jnp.mean(xv * xv, axis=-1, keepdims=True) + EPS)
            xh = xv * rs
            dxn = acc_ref[...]
            gnw_ref[0:1, :] += jnp.sum(dxn * xh, axis=0, keepdims=True)
            dxh = dxn * nw_ref[...]
            gx_ref[...] = dh_ref[...] + rs * (dxh - xh * jnp.mean(dxh * xh, axis=-1, keepdims=True))

        @pl.when(step == last_step)
        def _():
            for cp in cps:
                cp.wait()

    any_spec = pl.BlockSpec(memory_space=pl.ANY)
    prev_specs = [pl.BlockSpec((8, DM), lambda i, k: (0, 0)), any_spec] if n_prev else []
    prev_args = [prev[1], prev[0]] if n_prev else []
    aliases = {8: 0} if n_prev else {}
    for q in range(n_al):
        aliases[7 + n_prev + n_in + q] = 2 + q
    res = pl.pallas_call(
        body, name=name, grid=(nblk, nk),
        in_specs=[pl.BlockSpec((tm, tk), lambda i, k: (blk0 + i, k)),
                  pl.BlockSpec((tm, LANES), lambda i, k: (blk0 + i, 0)),
                  pl.BlockSpec((DM, tk), lambda i, k: (0, k)),
                  pl.BlockSpec((DM, LANES), lambda i, k: (0, 0)),
                  pl.BlockSpec((tm, DM), lambda i, k: (blk0 + i, 0)),
                  pl.BlockSpec((tm, DM), lambda i, k: (blk0 + i, 0)),
                  pl.BlockSpec((1, DM), lambda i, k: (0, 0))] + prev_specs + [any_spec] * (n_in + n_al),
        out_specs=[pl.BlockSpec((tm, DM), lambda i, k: (blk0 + i, 0)),
                   pl.BlockSpec((8, DM), lambda i, k: (0, 0))] + [any_spec] * (n_al + n_new),
        out_shape=[S((T, DM), F32), S((8, DM), F32)] + [S(a.shape, a.dtype) for a in hbm_alias] + list(hbm_new),
        scratch_shapes=[pltpu.VMEM((tm, DM), F32), pltpu.SemaphoreType.DMA((10,)), pltpu.SemaphoreType.DMA((10,))],
        input_output_aliases=aliases,
        compiler_params=_cp(ARB, ARB),
    )(dmain, dba, w_main, w_ba, x, dh, norm_w, *prev_args, *hbm_in, *hbm_alias)
    return (res[0], res[1]), res[2:2 + n_al], res[2 + n_al:]


def _remote(kk, src, dst, to, send_sems, recv_sems):
    return pltpu.make_async_remote_copy(src_ref=src, dst_ref=dst, send_sem=send_sems.at[kk], recv_sem=recv_sems.at[kk],
                                        device_id=to, device_id_type=MESH)


def _dx(dmain, dba, w_main, w_ba, x, dh, norm_w, chip_sum, small, cut):
    R, C = chip_sum.shape[1:]
    half = R // 2
    assert half % 16 == 0
    T = x.shape[0]
    ni = T // _dx_rows(T)
    cut = max(1, min(cut, ni - 1))
    upper, lower = pl.ds(0, half), pl.ds(half, half)

    def nbrs():
        px, py, pc = _position()
        return (px, py), (1 - px, py, pc), (px, 1 - py, pc)

    def phase1(ins, als, news, ss, rs):
        (px, py), xn, yn = nbrs()
        cs = ins[0]
        recv, stage = news
        bx, by, bd = cs.at[2 * (1 - px) + py], cs.at[2 * px + (1 - py)], cs.at[2 * (1 - px) + (1 - py)]
        return [_remote(0, bx.at[upper], recv.at[0].at[upper], xn, ss, rs),
                _remote(1, by.at[lower], recv.at[1].at[lower], yn, ss, rs),
                _remote(2, bd.at[upper], stage.at[0], xn, ss, rs),
                _remote(3, bd.at[lower], stage.at[1], yn, ss, rs)]

    def phase2(ins, als, news, ss, rs):
        (px, py), xn, yn = nbrs()
        comb, small_ref = ins
        recv, gath = als[0], news[0]
        me, small_cps = _broadcast_copies([small_ref], [gath], _Sem2(ss, 2), _Sem2(rs, 2))
        return ([_remote(0, comb.at[0], recv.at[1].at[upper], yn, ss, rs),
                 _remote(1, comb.at[1], recv.at[0].at[lower], xn, ss, rs)] + small_cps
                + [pltpu.make_async_copy(small_ref, gath.at[me], ss.at[9])])

    (gx, gnw), _, (recv, stage) = _dx_part(
        "dx_a", dmain, dba, w_main, w_ba, x, dh, norm_w, 0, cut, None, [chip_sum], [],
        [S((2, R, C), chip_sum.dtype), S((2, half, C), chip_sum.dtype)], phase1)
    comb = _relay_add(chip_sum, stage)
    (gx, gnw), (recv,), (gath,) = _dx_part(
        "dx_b", dmain, dba, w_main, w_ba, x, dh, norm_w, cut, ni - cut, (gx, gnw), [comb, small], [recv],
        [S((N_DEV,) + small.shape, F32)], phase2)
    return gx, gnw, gath, recv


class _Sem2:
    def __init__(self, sems, lo):
        self.sems, self.lo = sems, lo

    @property
    def at(self):
        outer = self

        class _At:
            def __getitem__(self, idx):
                a, k = idx
                return outer.sems.at[outer.lo + k]
        return _At()


def _relay_add(chip_sum, stage):
    _, R, C = chip_sum.shape
    half = R // 2
    tr = _tile(half, 256, 16)
    nt = half // tr
    px, py, _ = _position()
    idx = jnp.stack([2 * px + (1 - py), 2 * (1 - px) + py]).astype(jnp.int32)

    def body(idx_ref, p_ref, s_ref, o_ref):
        del idx_ref
        o_ref[0] = (p_ref[0].astype(F32) + s_ref[0].astype(F32)).astype(BF16)

    return pl.pallas_call(
        body, name="relay_add",
        grid_spec=pltpu.PrefetchScalarGridSpec(
            num_scalar_prefetch=1, grid=(2, nt),
            in_specs=[pl.BlockSpec((1, tr, C), lambda s, i, idx_ref: (idx_ref[s], s * nt + i, 0)),
                      pl.BlockSpec((1, tr, C), lambda s, i, idx_ref: (s, i, 0))],
            out_specs=pl.BlockSpec((1, tr, C), lambda s, i, idx_ref: (s, i, 0))),
        out_shape=S((2, half, C), BF16), compiler_params=_cp(ARB, ARB),
    )(idx, chip_sum, stage)


def _sum_slots(gath):
    _, R, C = gath.shape
    tr = R if R <= 2048 else _tile(R, 512, 8)

    def body(g_ref, o_ref):
        tot = g_ref[0]
        for d in range(1, N_DEV):
            tot = tot + g_ref[d]
        o_ref[...] = tot

    return pl.pallas_call(
        body, name="sum_slots", grid=(R // tr,),
        in_specs=[pl.BlockSpec((N_DEV, tr, C), lambda i: (0, i, 0))],
        out_specs=pl.BlockSpec((tr, C), lambda i: (i, 0)),
        out_shape=S((R, C), F32), compiler_params=_cp(ARB),
    )(gath)


def _prep_a_bwd(dq, dk, dv, c, proj, conv_w, dmain, H, D):
    T = c.shape[0]
    AW = H * D
    C3 = 3 * AW
    tb = _tile(T, 256, 8)
    nblk = T // tb
    r8 = tb // 8
    scale = float(D) ** -0.5

    def body(dq_ref, dk_ref, dv_ref, c_ref, dqn_ref, dkn_ref, dvn_ref, cn_ref, x_ref, halo_ref, cw_ref, dmain_in_ref,
             dx_ref, gcw_ref, dc_ref):
        del dmain_in_ref
        i = pl.program_id(0)

        @pl.when(i == 0)
        def _():
            gcw_ref[...] = jnp.zeros_like(gcw_ref)

        def pointwise(rows, dq_r, dk_r, dv_r, c_r, keep):
            for h in range(H):
                for part, d_r, sc in ((0, dq_r, scale), (1, dk_r, 1.0)):
                    sl = slice(part * AW + h * D, part * AW + (h + 1) * D)
                    cv = c_r[:, sl]
                    raw = _silu(cv)
                    rs = lax.rsqrt(jnp.sum(raw * raw, axis=-1, keepdims=True) + EPS)
                    nrm = raw * rs
                    dn = d_r[:, h * D:(h + 1) * D] * sc
                    draw = rs * (dn - nrm * jnp.sum(dn * nrm, axis=-1, keepdims=True))
                    dc_ref[rows, sl] = draw * _dsilu(cv) * keep
            dc_ref[rows, 2 * AW:] = dv_r[...] * _dsilu(c_r[:, 2 * AW:]) * keep

        pointwise(slice(0, tb), dq_ref, dk_ref, dv_ref, c_ref, 1.0)
        pointwise(slice(tb, tb + 8), dqn_ref, dkn_ref, dvn_ref, cn_ref, (i < nblk - 1).astype(F32))

        cw = cw_ref[...]
        dcv = dc_ref[0:tb, :]
        dx = cw[3:4, :] * dcv
        for j in range(3):
            dx = dx + cw[j:j + 1, :] * dc_ref[3 - j:3 - j + tb, :]
        dx_ref[...] = dx.astype(BF16)
        halo = halo_ref[...] * (i > 0).astype(F32)
        xp = jnp.concatenate([halo, x_ref[...]], axis=0)
        for j in range(4):
            gcw_ref[j:j + 1, :] += jnp.sum(dcv * xp[5 + j:5 + j + tb], axis=0, keepdims=True)

    nxt = lambda i: (jnp.minimum((i + 1) * r8, T // 8 - 1), 0)
    return pl.pallas_call(
        body, name="prep_a_bwd", grid=(nblk,),
        in_specs=[pl.BlockSpec((tb, AW), lambda i: (i, 0)),
                  pl.BlockSpec((tb, AW), lambda i: (i, 0)),
                  pl.BlockSpec((tb, AW), lambda i: (i, 0)),
                  pl.BlockSpec((tb, C3), lambda i: (i, 0)),
                  pl.BlockSpec((8, AW), nxt), pl.BlockSpec((8, AW), nxt), pl.BlockSpec((8, AW), nxt),
                  pl.BlockSpec((8, C3), nxt),
                  pl.BlockSpec((tb, C3), lambda i: (i, 0)),
                  pl.BlockSpec((8, C3), lambda i: (jnp.maximum(i * r8 - 1, 0), 0)),
                  pl.BlockSpec((4, C3), lambda i: (0, 0)),
                  pl.BlockSpec(memory_space=pl.ANY)],
        out_specs=[pl.BlockSpec((tb, C3), lambda i: (i, 0)),
                   pl.BlockSpec((8, C3), lambda i: (0, 0))],
        out_shape=[S(dmain.shape, dmain.dtype), S((8, C3), F32)],
        scratch_shapes=[pltpu.VMEM((tb + 8, C3), F32)],
        input_output_aliases={11: 0},
        compiler_params=_cp(ARB),
    )(dq, dk, dv, c, dq, dk, dv, c, proj, proj, conv_w, dmain)


def _adam_math(w, g, m, v):
    m2 = ADAM_B1 * m + (1.0 - ADAM_B1) * g
    v2 = ADAM_B2 * v + (1.0 - ADAM_B2) * (g * g)
    m_hat = m2 / (1.0 - ADAM_B1 ** ADAM_STEP)
    v_hat = v2 / (1.0 - ADAM_B2 ** ADAM_STEP)
    delta = -ADAM_LR * (m_hat / (jnp.sqrt(v_hat) + ADAM_EPS) + ADAM_WD * w)
    return delta, m2, v2


def _pair_sum(blocks, recv, core, name):
    K, _, R, C = blocks.shape
    tr = _tile(R, 256, 16)

    def body(core_ref, a_ref, b_ref, o_ref):
        del core_ref
        o_ref[0] = (a_ref[0, 0].astype(F32) + b_ref[0].astype(F32)).astype(BF16)

    spec = lambda: pl.BlockSpec((1, tr, C), lambda k, i, core_ref: (k, i, 0))
    return pl.pallas_call(
        body, name=name,
        grid_spec=pltpu.PrefetchScalarGridSpec(
            num_scalar_prefetch=1, grid=(K, R // tr),
            in_specs=[pl.BlockSpec((1, 1, tr, C), lambda k, i, core_ref: (k, core_ref[0], i, 0)), spec()],
            out_specs=spec()),
        out_shape=S((K, R, C), BF16), compiler_params=_cp(ARB, ARB),
    )(core, blocks, recv)


def _sum_adam(chip_sums, recv, w, m, v, chip, name, transposed=False):
    R, C = chip_sums.shape[1:]
    NR = recv.shape[0]
    tr = _tile(R, 256, 16)

    def body(chip_ref, own_ref, r_ref, w_ref, m_ref, v_ref, g_ref, d_ref, m2_ref, v2_ref):
        del chip_ref
        g = own_ref[0].astype(F32)
        for j in range(NR):
            g = g + r_ref[j].astype(F32)
        if transposed:
            g = g.T
        g_ref[...] = g
        d_ref[...], m2_ref[...], v2_ref[...] = _adam_math(w_ref[...], g, m_ref[...], v_ref[...])

    if transposed:
        spec = lambda: pl.BlockSpec((C, tr), lambda i, chip_ref: (0, i))
        shape = (C, R)
    else:
        spec = lambda: pl.BlockSpec((tr, C), lambda i, chip_ref: (i, 0))
        shape = (R, C)
    assert w.shape == shape
    return pl.pallas_call(
        body, name=name,
        grid_spec=pltpu.PrefetchScalarGridSpec(
            num_scalar_prefetch=1, grid=(R // tr,),
            in_specs=[pl.BlockSpec((1, tr, C), lambda i, chip_ref: (chip_ref[0], i, 0)),
                      pl.BlockSpec((NR, tr, C), lambda i, chip_ref: (0, i, 0)), spec(), spec(), spec()],
            out_specs=[spec(), spec(), spec(), spec()]),
        out_shape=[S(shape, F32)] * 4, compiler_params=_cp(ARB),
    )(chip, chip_sums, recv, w, m, v)


def _adam_small(w, g, m, v):
    R, C = w.shape
    tr = _tile(R, 512, 8)

    def body(w_ref, g_ref, m_ref, v_ref, d_ref, m2_ref, v2_ref):
        d_ref[...], m2_ref[...], v2_ref[...] = _adam_math(w_ref[...], g_ref[...], m_ref[...], v_ref[...])

    spec = lambda: pl.BlockSpec((tr, C), lambda i: (i, 0))
    return pl.pallas_call(
        body, name="adam_small", grid=(R // tr,), in_specs=[spec()] * 4, out_specs=[spec()] * 3,
        out_shape=[S((R, C), F32)] * 3, compiler_params=_cp(ARB),
    )(w, g, m, v)


def _position():
    return lax.axis_index("x"), lax.axis_index("y"), lax.axis_index("c")


def _all_gather_weights(arr, x_in, norm_w):
    R = arr.shape[0]
    half = R // 2
    assert half % 16 == 0
    T, DM = x_in.shape
    tm = _tile(T, 512, 16)
    nstep = T // tm

    def body(x_ref, nw_ref, in_ref, xn_ref, out_ref, send_sems, recv_sems, local_sem):
        i = pl.program_id(0)
        x, y, c = _position()
        me, sibling = (x, y, c), (x, y, 1 - c)
        xn, yn, diag = (1 - x, y), (x, 1 - y), (1 - x, 1 - y)
        upper, lower = pl.ds(0, half), pl.ds(half, half)

        def slot(p, rows=None):
            ref = out_ref.at[4 * p[0] + 2 * p[1] + p[2]]
            return ref if rows is None else ref.at[rows]

        def copy(kk, block, to, rows=None, src=None):
            return pltpu.make_async_remote_copy(
                src_ref=slot(block, rows) if src is None else src, dst_ref=slot(block, rows),
                send_sem=send_sems.at[kk], recv_sem=recv_sems.at[kk], device_id=to, device_id_type=MESH)

        mine = pltpu.make_async_copy(in_ref, slot(me), local_sem)
        first = [copy(0, me, sibling, src=in_ref), copy(1, me, (*xn, c), src=in_ref), copy(2, me, (*yn, c), src=in_ref)]

        @pl.when(i == 0)
        def _():
            mine.start()
            for cp in first:
                cp.start()

        xv = x_ref[...]
        r = lax.rsqrt(jnp.mean(xv * xv, axis=-1, keepdims=True) + EPS)
        xn_ref[...] = (xv * r * nw_ref[...]).astype(BF16)

        @pl.when(i == nstep - 1)
        def _():
            sent = list(first)

            def then(cps):
                for cp in cps:
                    cp.start()
                sent.extend(cps)

            copy(1, (*xn, c), me).wait_recv()
            then([copy(5, (*xn, c), (*yn, c), rows=upper), copy(3, (*xn, c), sibling)])
            copy(2, (*yn, c), me).wait_recv()
            then([copy(6, (*yn, c), (*xn, c), rows=lower), copy(4, (*yn, c), sibling)])
            copy(5, (*diag, c), me, rows=upper).wait_recv()
            then([copy(7, (*diag, c), sibling, rows=upper)])
            copy(6, (*diag, c), me, rows=lower).wait_recv()
            then([copy(8, (*diag, c), sibling, rows=lower)])
            copy(0, sibling, me).wait_recv()
            copy(3, (*xn, 1 - c), me).wait_recv()
            copy(4, (*yn, 1 - c), me).wait_recv()
            copy(7, (*diag, 1 - c), me, rows=upper).wait_recv()
            copy(8, (*diag, 1 - c), me, rows=lower).wait_recv()
            for cp in sent:
                cp.wait_send()
            mine.wait()

    any_spec = pl.BlockSpec(memory_space=pl.ANY)
    return pl.pallas_call(
        body, name="all_gather_weights", grid=(nstep,),
        in_specs=[pl.BlockSpec((tm, DM), lambda i: (i, 0)), pl.BlockSpec((1, DM), lambda i: (0, 0)), any_spec],
        out_specs=[pl.BlockSpec((tm, DM), lambda i: (i, 0)), any_spec],
        out_shape=[S((T, DM), BF16), S((N_DEV,) + arr.shape, arr.dtype)],
        scratch_shapes=[pltpu.SemaphoreType.DMA((9,)), pltpu.SemaphoreType.DMA((9,)), pltpu.SemaphoreType.DMA],
        compiler_params=_cp(ARB),
    )(x_in, norm_w, arr)


def _sibling_copies(ins, outs, send_sems, recv_sems):
    x, y, c = _position()
    return [pltpu.make_async_remote_copy(src_ref=ins[a].at[k, 1 - c], dst_ref=outs[a].at[k],
                                         send_sem=send_sems.at[a, k], recv_sem=recv_sems.at[a, k],
                                         device_id=(x, y, 1 - c), device_id_type=MESH)
            for a in range(len(ins)) for k in range(ins[a].shape[0])]


def _sibling_sems(arrs):
    shape = (max(len(arrs), 1), arrs[0].shape[0] if arrs else 1)
    return [pltpu.SemaphoreType.DMA(shape), pltpu.SemaphoreType.DMA(shape)]


def _chip_exchange_copies(ins, outs, send_sems, recv_sems):
    x, y, c = _position()
    chips = [(1 - x, y), (x, 1 - y), (1 - x, 1 - y)]
    return [pltpu.make_async_remote_copy(
        src_ref=ins[a].at[2 * qx + qy], dst_ref=outs[a].at[j], send_sem=send_sems.at[a, j],
        recv_sem=recv_sems.at[a, j], device_id=(qx, qy, c), device_id_type=MESH)
        for a in range(len(ins)) for j, (qx, qy) in enumerate(chips)]


def _broadcast_copies(srcs, dsts, send_sems, recv_sems):
    x, y, c = _position()
    me = 4 * x + 2 * y + c
    cps = []
    for a in range(len(srcs)):
        for k in range(1, N_DEV):
            peer = (1 - x if k & 4 else x, 1 - y if k & 2 else y, 1 - c if k & 1 else c)
            cps.append(pltpu.make_async_remote_copy(
                src_ref=srcs[a], dst_ref=dsts[a].at[me], send_sem=send_sems.at[a, k - 1],
                recv_sem=recv_sems.at[a, k - 1], device_id=peer, device_id_type=MESH))
    return me, cps


def _all_reduce_small(part):
    R, C = part.shape

    def body(p_ref, out_ref, gath_ref, send_sems, recv_sems):
        me, cps = _broadcast_copies([p_ref], [gath_ref], send_sems, recv_sems)
        gath_ref[me] = p_ref[...]
        for cp in cps:
            cp.start()
        for cp in cps:
            cp.wait()
        acc = gath_ref[0]
        for d in range(1, N_DEV):
            acc = acc + gath_ref[d]
        out_ref[...] = acc

    vm = pl.BlockSpec(memory_space=pltpu.VMEM)
    return pl.pallas_call(
        body, name="all_reduce_small", in_specs=[vm], out_specs=vm, out_shape=S((R, C), F32),
        scratch_shapes=[pltpu.VMEM((N_DEV, R, C), F32), pltpu.SemaphoreType.DMA((1, N_DEV - 1)),
                        pltpu.SemaphoreType.DMA((1, N_DEV - 1))],
    )(part)


def _pack(parts):
    rows = []
    for p in parts:
        f = p.reshape(-1).astype(F32)
        pad = (-f.shape[0]) % (8 * LANES)
        rows.append(jnp.pad(f, (0, pad)).reshape(-1, LANES))
    return jnp.concatenate(rows, axis=0)


def _unpack(buf, shapes):
    out, r = [], 0
    for shp in shapes:
        n = 1
        for s in shp:
            n *= s
        nr = -(-n // (8 * LANES)) * 8
        out.append(buf[r:r + nr].reshape(-1)[:n].reshape(shp))
        r += nr
    return out


def kernel(x, norm_w, w_in, conv_w, a_log, dt_bias, head_norm_w, sgu_ln_w, sgu_ln_b, w_spatial, b_spatial, w_out, final_norm_w, loss_target, m_norm_w, m_w_in, m_conv_w, m_a_log, m_dt_bias, m_head_norm_w, m_sgu_ln_w, m_sgu_ln_b, m_w_spatial, m_b_spatial, m_w_out, m_final_norm_w, v_norm_w, v_w_in, v_conv_w, v_a_log, v_dt_bias, v_head_norm_w, v_sgu_ln_w, v_sgu_ln_b, v_w_spatial, v_b_spatial, v_w_out, v_final_norm_w):
    T, DM = x.shape[1], x.shape[2]
    H, D = a_log.shape[1], head_norm_w.shape[1]
    G, P = w_spatial.shape[1], w_spatial.shape[2]
    AW, BW = H * D, G * P
    MIX = AW + BW
    WD = w_in.shape[2]
    IN = N_DEV * WD
    RO = w_out.shape[1]
    CW = conv_w.shape[2]
    sizes = (3 * AW, AW, H, H, BW, BW, BW)
    assert sum(sizes) == IN and 2 * H <= LANES and 3 * H <= 32 and N_DEV * RO == MIX and N_DEV * CW == 3 * AW
    offs = [0]
    for s in sizes:
        offs.append(offs[-1] + s)
    px, py, pc = _position()
    dev = 4 * px + 2 * py + pc
    chip = 2 * px + py

    x2, tgt = x[0], loss_target[0]

    xn, g_win = _all_gather_weights(_cast_bf16_t(w_in[0].T, "cast_w_in"), x2, norm_w)
    w_main, w_ba = _relayout_w(g_win, offs[2], offs[4])
    alog_row = jnp.pad(a_log, ((0, 0), (H, LANES - 2 * H)))
    dtb_row = jnp.pad(dt_bias, ((0, 0), (H, LANES - 2 * H)))
    bs_t = b_spatial[0].T

    proj, ba, (g_wout, g_conv) = _in_proj(xn, w_main, w_ba, [_cast_bf16(w_out[0], "cast_w_out"), conv_w[0]])
    w_out_full = g_wout.reshape(MIX, DM)
    conv_full = g_conv.transpose(1, 0, 2).reshape(4, 3 * AW)
    q, k, v, c, gcol, grow = _prep_a_fwd(proj, ba, conv_full, alog_row, dtb_row, H, D)
    o, vnew, ssave, asave = _delta_fwd(q, k, v, gcol, grow, H, D)
    ocat = _mix_fwd(o, proj, head_norm_w, sgu_ln_w, sgu_ln_b, w_spatial[0], bs_t, H, D, G, P)
    dh, dh_bf, d_ocat, loss_acc, g_fnw = _out_proj_loss(ocat, w_out_full, x2, tgt, final_norm_w.reshape(1, DM))

    core_idx = jnp.reshape(pc, (1,)).astype(jnp.int32)
    chip_idx = jnp.reshape(chip, (1,)).astype(jnp.int32)
    g_wout_blocks = _grad_w(ocat, dh_bf, "grad_w_out").reshape(4, 2, RO, DM)
    (d_o, dmain, g_hnw, g_ln, g_wsp, g_bs_t), (sib_wout,) = _mix_bwd(
        d_ocat, o, proj, head_norm_w, sgu_ln_w, sgu_ln_b, w_spatial[0], bs_t, H, D, G, P, [g_wout_blocks])
    chip_wout = _pair_sum(g_wout_blocks, sib_wout, core_idx, "pair_sum_w_out")
    (dq, dk, dv, dgate, dpar), (recv_wout,) = _delta_bwd(
        q, k, v, gcol, grow, ba, vnew, ssave, asave, d_o, alog_row, dtb_row, H, D, [chip_wout])
    dmain, g_conv_part = _prep_a_bwd(dq, dk, dv, c, proj, conv_full, dmain, H, D)
    dba = dgate.astype(BF16)
    keep_win, sib_win = _grad_w_in(xn, dmain, dba, WD, offs[2], offs[4])
    chip_win = _pair_sum_plain(keep_win, sib_win, "pair_sum_w_in")
    small_shapes = [a_log.shape, dt_bias.shape, head_norm_w.shape, sgu_ln_w.shape, sgu_ln_b.shape,
                    w_spatial.shape, b_spatial.shape, final_norm_w.shape]
    parts = [dpar[0, H:2 * H], dpar[1, H:2 * H], g_hnw[0], g_ln[0], g_ln[1], g_wsp, g_bs_t[:, :G].T, g_fnw[0],
             g_conv_part[:4], loss_acc[0, :1]]
    grad_x, g_nw, small_gath, recv_win = _dx(dmain, dba, w_main, w_ba, x2, dh, norm_w, chip_win, _pack(parts), 4)
    red = _sum_slots(small_gath)
    grad_w_in, delta_w_in, new_m_w_in, new_v_w_in = _sum_adam(
        chip_win, recv_win, w_in[0].T, m_w_in[0].T, v_w_in[0].T, chip_idx, "sum_adam_w_in", transposed=True)
    grad_w_out, delta_w_out, new_m_w_out, new_v_w_out = _sum_adam(
        chip_wout, recv_wout, w_out[0], m_w_out[0], v_w_out[0], chip_idx, "sum_adam_w_out")
    red_nw = _all_reduce_small(_pack([g_nw[0]]))
    grads_small = _unpack(red_nw, [norm_w.shape]) + _unpack(red, small_shapes + [(4, 3 * AW), (1,)])
    loss = grads_small.pop()[0]
    g_conv_full = grads_small.pop()
    grad_conv = lax.dynamic_slice_in_dim(g_conv_full, dev * CW, CW, axis=1)[None]
    small_w = [norm_w, a_log, dt_bias, head_norm_w, sgu_ln_w, sgu_ln_b, w_spatial, b_spatial, final_norm_w, conv_w]
    small_m = [m_norm_w, m_a_log, m_dt_bias, m_head_norm_w, m_sgu_ln_w, m_sgu_ln_b, m_w_spatial, m_b_spatial,
               m_final_norm_w, m_conv_w]
    small_v = [v_norm_w, v_a_log, v_dt_bias, v_head_norm_w, v_sgu_ln_w, v_sgu_ln_b, v_w_spatial, v_b_spatial,
               v_final_norm_w, v_conv_w]
    small_g = grads_small + [grad_conv]
    shapes10 = [w.shape for w in small_w]
    d_p, m_p, v_p = _adam_small(_pack(small_w), _pack(small_g), _pack(small_m), _pack(small_v))
    d_s, m_s, v_s = _unpack(d_p, shapes10), _unpack(m_p, shapes10), _unpack(v_p, shapes10)

    def order(small, win, wout):
        return [small[0], win.T[None], small[9], small[1], small[2], small[3], small[4], small[5], small[6], small[7],
                wout[None], small[8]]

    grads = order(small_g, grad_w_in, grad_w_out)
    deltas = order(d_s, delta_w_in, delta_w_out)
    new_m = order(m_s, new_m_w_in, new_m_w_out)
    new_v = order(v_s, new_v_w_in, new_v_w_out)
    return (loss, grad_x[None], *grads, *deltas, *new_m, *new_v)
```

```python
import jax
import jax.numpy as jnp
from jax import lax
from jax.experimental import pallas as pl
from jax.experimental.pallas import tpu as pltpu

F32 = jnp.float32
BF16 = jnp.bfloat16
MXU = jnp.bfloat16
HI = lax.Precision.HIGHEST
EPS = 1e-6
CHUNK_A = 64
LANES = 128
MESH = pl.DeviceIdType.MESH
N_DEV = 8

ADAM_LR = 0.001
ADAM_B1 = 0.9
ADAM_B2 = 0.999
ADAM_EPS = 1e-08
ADAM_WD = 0.01
ADAM_STEP = 10

S = jax.ShapeDtypeStruct
ARB = "arbitrary"


def _cp(*sem, vmem_mib=56):
    return pltpu.CompilerParams(dimension_semantics=tuple(sem), vmem_limit_bytes=vmem_mib * 1024 * 1024)


def _tile(n, cap, mult):
    best = None
    t = mult
    while t <= min(n, cap):
        if n % t == 0:
            best = t
        t += mult
    return best if best is not None else n


def _mm(a, b):
    return jnp.dot(a.astype(MXU), b.astype(MXU), preferred_element_type=F32)


def _mm_nt(a, b):
    return lax.dot_general(a.astype(MXU), b.astype(MXU), (((1,), (1,)), ((), ())), preferred_element_type=F32)


def _mm_tn(a, b):
    return lax.dot_general(a.astype(MXU), b.astype(MXU), (((0,), (0,)), ((), ())), preferred_element_type=F32)


def _mmh(a, b):
    return jnp.dot(a, b, precision=HI, preferred_element_type=F32)


def _sigmoid(x):
    return 1.0 / (1.0 + jnp.exp(-x))


def _silu(x):
    return x * _sigmoid(x)


def _dsilu(x):
    s = _sigmoid(x)
    return s * (1.0 + x * (1.0 - s))


def _softplus(x):
    return jnp.maximum(x, 0.0) + jnp.log(1.0 + jnp.exp(-jnp.abs(x)))


def _pieces(wd, gate_lo, gate_hi, total):
    out = []
    for d in range(N_DEV):
        lo, hi = d * wd, (d + 1) * wd
        for dest, a, b, shift in (("main", 0, gate_lo, 0), ("gate", gate_lo, gate_hi, -gate_lo),
                                  ("main", gate_hi, total, gate_lo - gate_hi)):
            s0, s1 = max(lo, a), min(hi, b)
            if s0 < s1:
                out.append((d, s0 - lo, s1 - lo, dest, s0 + shift))
    return out


def _cast_bf16(a, name):
    R, C = a.shape
    tr = _tile(R, 256, 16)

    def body(a_ref, o_ref):
        o_ref[...] = a_ref[...].astype(BF16)

    spec = pl.BlockSpec((tr, C), lambda i: (i, 0))
    return pl.pallas_call(body, name=name, grid=(R // tr,), in_specs=[spec], out_specs=spec,
                          out_shape=S((R, C), BF16), compiler_params=_cp(ARB))(a)


def _cast_bf16_t(a_t, name):
    C, R = a_t.shape
    tr = _tile(R, 256, LANES)

    def body(a_ref, o_ref):
        o_ref[...] = a_ref[...].T.astype(BF16)

    return pl.pallas_call(body, name=name, grid=(R // tr,), in_specs=[pl.BlockSpec((C, tr), lambda i: (0, i))],
                          out_specs=pl.BlockSpec((tr, C), lambda i: (i, 0)),
                          out_shape=S((R, C), BF16), compiler_params=_cp(ARB))(a_t)


def _relayout_w(g_win, gate_lo, gate_hi):
    _, DM, WD = g_win.shape
    total = N_DEV * WD
    NM = total - (gate_hi - gate_lo)
    tr = _tile(DM, 256, 16)
    plan = _pieces(WD, gate_lo, gate_hi, total)

    def body(g_ref, main_ref, gate_ref):
        gate_ref[...] = jnp.zeros_like(gate_ref)
        for d, s0, s1, dest, c0 in plan:
            dst = main_ref if dest == "main" else gate_ref
            dst[:, c0:c0 + (s1 - s0)] = g_ref[d, :, s0:s1]

    return pl.pallas_call(
        body, name="relayout_w", grid=(DM // tr,),
        in_specs=[pl.BlockSpec((N_DEV, tr, WD), lambda i: (0, i, 0))],
        out_specs=[pl.BlockSpec((tr, NM), lambda i: (i, 0)), pl.BlockSpec((tr, LANES), lambda i: (i, 0))],
        out_shape=[S((DM, NM), g_win.dtype), S((DM, LANES), g_win.dtype)],
        compiler_params=_cp(ARB),
    )(g_win)


def _in_proj(xn, w_main, w_ba, shards):
    T, DM = xn.shape
    NM = w_main.shape[1]
    tm = _tile(T, 2048, 16)
    tn = _tile(NM, 1024, LANES)
    ni, nj = T // tm, NM // tn
    ns = len(shards)

    def body(xn_ref, w_ref, wba_ref, *rest):
        srcs = rest[:ns]
        proj_ref, ba_ref = rest[ns:ns + 2]
        gath = rest[ns + 2:2 * ns + 2]
        send_sems, recv_sems, local_sems = rest[2 * ns + 2:]
        i = pl.program_id(0)
        j = pl.program_id(1)
        me, cps = _broadcast_copies(srcs, gath, send_sems, recv_sems)
        cps = cps + [pltpu.make_async_copy(srcs[a], gath[a].at[me], local_sems.at[a]) for a in range(ns)]

        @pl.when((i == 0) & (j == 0))
        def _():
            for cp in cps:
                cp.start()

        @pl.when(j == 0)
        def _():
            ba_ref[...] = jnp.dot(xn_ref[...].astype(MXU), wba_ref[...].astype(MXU), preferred_element_type=F32)

        proj_ref[...] = jnp.dot(xn_ref[...].astype(MXU), w_ref[...].astype(MXU), preferred_element_type=F32)

        @pl.when((i == ni - 1) & (j == nj - 1))
        def _():
            for cp in cps:
                cp.wait()

    any_spec = pl.BlockSpec(memory_space=pl.ANY)
    res = pl.pallas_call(
        body, name="in_proj", grid=(ni, nj),
        in_specs=[pl.BlockSpec((tm, DM), lambda i, j: (i, 0)),
                  pl.BlockSpec((DM, tn), lambda i, j: (0, j)),
                  pl.BlockSpec((DM, LANES), lambda i, j: (0, 0))] + [any_spec] * ns,
        out_specs=[pl.BlockSpec((tm, tn), lambda i, j: (i, j)),
                   pl.BlockSpec((tm, LANES), lambda i, j: (i, 0))] + [any_spec] * ns,
        out_shape=[S((T, NM), F32), S((T, LANES), F32)] + [S((N_DEV,) + a.shape, a.dtype) for a in shards],
        scratch_shapes=[pltpu.SemaphoreType.DMA((ns, N_DEV - 1)), pltpu.SemaphoreType.DMA((ns, N_DEV - 1)),
                        pltpu.SemaphoreType.DMA((ns,))],
        compiler_params=_cp(ARB, ARB, vmem_mib=58),
    )(xn, w_main, w_ba, *shards)
    return res[0], res[1], res[2:]


def _prep_a_fwd(proj, ba, conv_w, alog_row, dtb_row, H, D):
    T = proj.shape[0]
    AW = H * D
    C3 = 3 * AW
    tb = _tile(T, 256, CHUNK_A)
    nch = tb // CHUNK_A
    nblk = T // tb
    scale = float(D) ** -0.5

    def body(x_ref, halo_ref, ba_ref, cw_ref, al_ref, dt_ref, q_ref, k_ref, v_ref, c_ref, gcol_ref, grow_ref):
        i = pl.program_id(0)
        xv = x_ref[...]
        halo = halo_ref[...] * (i > 0).astype(F32)
        xp = jnp.concatenate([halo, xv], axis=0)
        cw = cw_ref[...]
        c = cw[0:1, :] * xp[5:5 + tb]
        for j in range(1, 4):
            c = c + cw[j:j + 1, :] * xp[5 + j:5 + j + tb]
        c_ref[...] = c
        a = _silu(c)
        for h in range(H):
            qh = a[:, h * D:(h + 1) * D]
            kh = a[:, AW + h * D:AW + (h + 1) * D]
            qr = lax.rsqrt(jnp.sum(qh * qh, axis=-1, keepdims=True) + EPS)
            kr = lax.rsqrt(jnp.sum(kh * kh, axis=-1, keepdims=True) + EPS)
            q_ref[:, h * D:(h + 1) * D] = qh * (qr * scale)
            k_ref[:, h * D:(h + 1) * D] = kh * kr
        v_ref[...] = a[:, 2 * AW:]

        bav = ba_ref[...]
        lane = lax.broadcasted_iota(jnp.int32, (tb, LANES), 1)
        beta = _sigmoid(bav)
        g = -jnp.exp(al_ref[...]) * _softplus(bav + dt_ref[...])
        gates = jnp.where(lane < H, beta, jnp.where(lane < 2 * H, g, 0.0))
        ri = lax.broadcasted_iota(jnp.int32, (CHUNK_A, CHUNK_A), 0)
        ci = lax.broadcasted_iota(jnp.int32, (CHUNK_A, CHUNK_A), 1)
        tri = (ri >= ci).astype(F32)
        lane_c = lax.broadcasted_iota(jnp.int32, (CHUNK_A, LANES), 1)
        for cc in range(nch):
            gch = gates[cc * CHUNK_A:(cc + 1) * CHUNK_A]
            gc = pltpu.roll(_mmh(tri, gch), H, 1)
            full = jnp.where(lane_c < 2 * H, gch, jnp.where(lane_c < 3 * H, gc, 0.0))
            gcol_ref[cc * CHUNK_A:(cc + 1) * CHUNK_A, :] = full
            grow_ref[cc] = full.T[0:32, :]

    return pl.pallas_call(
        body, name="prep_a_fwd", grid=(nblk,),
        in_specs=[pl.BlockSpec((tb, C3), lambda i: (i, 0)),
                  pl.BlockSpec((8, C3), lambda i: (jnp.maximum(i * (tb // 8) - 1, 0), 0)),
                  pl.BlockSpec((tb, LANES), lambda i: (i, 0)),
                  pl.BlockSpec((4, C3), lambda i: (0, 0)),
                  pl.BlockSpec((1, LANES), lambda i: (0, 0)),
                  pl.BlockSpec((1, LANES), lambda i: (0, 0))],
        out_specs=[pl.BlockSpec((tb, AW), lambda i: (i, 0)),
                   pl.BlockSpec((tb, AW), lambda i: (i, 0)),
                   pl.BlockSpec((tb, AW), lambda i: (i, 0)),
                   pl.BlockSpec((tb, C3), lambda i: (i, 0)),
                   pl.BlockSpec((tb, LANES), lambda i: (i, 0)),
                   pl.BlockSpec((nch, 32, CHUNK_A), lambda i: (i, 0, 0))],
        out_shape=[S((T, AW), F32), S((T, AW), F32), S((T, AW), F32), S((T, C3), F32),
                   S((T, LANES), F32), S((T // CHUNK_A, 32, CHUNK_A), F32)],
        compiler_params=_cp(ARB),
    )(proj, proj, ba, conv_w, alog_row, dtb_row)


_NN = (((1,), (0,)), ((), ()))
_TN = (((0,), (0,)), ((), ()))


def _split(a):
    hi = a.astype(BF16)
    return hi, (a - hi.astype(F32)).astype(BF16)


def _mm3(a, b, dims=_NN):
    ah, al = a if isinstance(a, tuple) else _split(a)
    bh, bl = b if isinstance(b, tuple) else _split(b)
    dg = lambda p, r: lax.dot_general(p, r, dims, preferred_element_type=F32)
    return dg(ah, bh) + (dg(ah, bl) + dg(al, bh))


def _interleave(gens):
    gens = list(gens)
    while gens:
        alive = []
        for g in gens:
            try:
                next(g)
                alive.append(g)
            except StopIteration:
                pass
        gens = alive


def _chunk_terms(q, k, v, gcolv, growv, h, H):
    C = CHUNK_A
    beta_c = gcolv[:, h:h + 1]
    g_c = gcolv[:, H + h:H + h + 1]
    gc_c = gcolv[:, 2 * H + h:2 * H + h + 1]
    gc_r = growv[2 * H + h:2 * H + h + 1, :]
    ri = lax.broadcasted_iota(jnp.int32, (C, C), 0)
    ci = lax.broadcasted_iota(jnp.int32, (C, C), 1)
    incl = ri >= ci
    strict = ri > ci
    kb = k * beta_c
    vb = v * beta_c
    p_raw = _mm_nt(kb, k)
    qk_raw = _mm_nt(q, k)
    gam = jnp.where(incl, jnp.exp(jnp.where(incl, gc_c - gc_r, 0.0)), 0.0)
    e_c = jnp.exp(gc_c)
    gl = gc_r[:, C - 1:C]
    edec = jnp.exp(gl - gc_c)
    yield
    lmat = jnp.where(strict, p_raw * gam, 0.0)
    attn = jnp.where(incl, qk_raw * gam, 0.0)
    return dict(beta_c=beta_c, g_c=g_c, gc_c=gc_c, gc_r=gc_r, incl=incl, strict=strict, gam=gam, e_c=e_c,
                kb=kb, vb=vb, lmat=lmat, attn=attn, gl=gl, edec=edec, ri=ri, ci=ci)


INV_BLOCK = 16


def _inv_unit_lower(lmat):
    C = lmat.shape[0]
    ri = lax.broadcasted_iota(jnp.int32, (C, C), 0)
    ci = lax.broadcasted_iota(jnp.int32, (C, C), 1)
    eye = (ri == ci).astype(F32)
    same = (ri // INV_BLOCK) == (ci // INV_BLOCK)

    def neumann(x, order):
        a = eye + x
        n = 1
        while 2 * n < order:
            xs = _split(x)
            x = _mm3(xs, xs)
            yield
            a = a + _mm3(a, x)
            n *= 2
        yield
        return a

    inv_d = yield from neumann(-jnp.where(same, lmat, 0.0), INV_BLOCK)
    m = _mm3(inv_d, jnp.where(same, 0.0, lmat))
    yield
    inv_m = yield from neumann(-m, C // INV_BLOCK)
    a = _mm3(inv_m, inv_d)
    yield
    return a


def _delta_fwd(q, k, v, gcol, grow, H, D):
    T = q.shape[0]
    C = CHUNK_A
    N = T // C
    AW = H * D
    CPS = 2 if N % 2 == 0 else 1

    def body(q_ref, k_ref, v_ref, gcol_ref, grow_ref, o_ref, vn_ref, ssave_ref, asave_ref, s_ref):
        @pl.when(pl.program_id(0) == 0)
        def _():
            s_ref[...] = jnp.zeros_like(s_ref)

        state = {(0, h): s_ref[h] for h in range(H)}

        def head(cc, h):
            rows = slice(cc * C, (cc + 1) * C)
            sl = slice(h * D, (h + 1) * D)
            qv, kv, vv = q_ref[rows, sl], k_ref[rows, sl], v_ref[rows, sl]
            t = yield from _chunk_terms(qv, kv, vv, gcol_ref[rows, :], grow_ref[cc], h, H)
            a = yield from _inv_unit_lower(t["lmat"])
            asave_ref[cc, h] = a
            while (cc, h) not in state:
                yield
            st = state[(cc, h)]
            ssave_ref[cc, h] = st
            ks = _mm(t["kb"] * t["e_c"], st)
            o_inter = _mm(qv * t["e_c"], st)
            yield
            v_new = _mm3(a, t["vb"] - ks)
            yield
            vn_ref[rows, sl] = v_new
            o_intra = _mm(t["attn"], v_new)
            s_upd = _mm_tn(kv * t["edec"], v_new)
            yield
            o_ref[rows, sl] = o_inter + o_intra
            state[(cc + 1, h)] = st * jnp.exp(t["gl"]) + s_upd

        _interleave(head(cc, h) for cc in range(CPS) for h in range(H))
        for h in range(H):
            s_ref[h] = state[(CPS, h)]

    blk = lambda: pl.BlockSpec((CPS * C, AW), lambda n: (n, 0))
    return pl.pallas_call(
        body, name="delta_fwd", grid=(N // CPS,),
        in_specs=[blk(), blk(), blk(),
                  pl.BlockSpec((CPS * C, LANES), lambda n: (n, 0)),
                  pl.BlockSpec((CPS, 32, C), lambda n: (n, 0, 0))],
        out_specs=[blk(), blk(),
                   pl.BlockSpec((CPS, H, D, D), lambda n: (n, 0, 0, 0)),
                   pl.BlockSpec((CPS, H, C, C), lambda n: (n, 0, 0, 0))],
        out_shape=[S((T, AW), F32), S((T, AW), F32), S((N, H, D, D), F32), S((N, H, C, C), F32)],
        scratch_shapes=[pltpu.VMEM((H, D, D), F32)],
        compiler_params=_cp(ARB),
    )(q, k, v, gcol, grow)


def _delta_bwd(q, k, v, gcol, grow, ba, vnew, ssave, asave, d_o, a_log, dt_bias, H, D, carry):
    T = q.shape[0]
    C = CHUNK_A
    N = T // C
    AW = H * D
    nc = len(carry)
    CPS = 2 if N % 2 == 0 else 1
    NS = N // CPS

    def body(al_ref, dt_ref, q_ref, k_ref, v_ref, gcol_ref, grow_ref, ba_ref, vn_ref, ss_ref, as_ref, do_ref, *rest):
        cins = rest[:nc]
        dq_ref, dk_ref, dv_ref, dgate_ref, dpar_ref = rest[nc:nc + 5]
        couts = rest[nc + 5:2 * nc + 5]
        ds_ref, csend, crecv = rest[2 * nc + 5:]
        ccps = _chip_exchange_copies(cins, couts, csend, crecv)

        @pl.when(pl.program_id(0) == 0)
        def _():
            ds_ref[...] = jnp.zeros_like(ds_ref)
            dpar_ref[...] = jnp.zeros_like(dpar_ref)
            for cp in ccps:
                cp.start()

        lane = lax.broadcasted_iota(jnp.int32, (C, LANES), 1)
        rowi = lax.broadcasted_iota(jnp.int32, (C, 1), 0)
        acc = {cc: jnp.zeros((C, LANES), F32) for cc in range(CPS)}
        state = {(0, h): ds_ref[h] for h in range(H)}

        def head(oi, h):
            cc = CPS - 1 - oi
            rows = slice(cc * C, (cc + 1) * C)
            sl = slice(h * D, (h + 1) * D)
            st = ss_ref[cc, h]
            a = as_ref[cc, h]
            qv, kv, vv, dov, v_new = q_ref[rows, sl], k_ref[rows, sl], v_ref[rows, sl], do_ref[rows, sl], vn_ref[rows, sl]
            t = yield from _chunk_terms(qv, kv, vv, gcol_ref[rows, :], grow_ref[cc], h, H)
            beta_c, e_c, gam, kb = t["beta_c"], t["e_c"], t["gam"], t["kb"]
            incl, strict, attn, lmat, edec = t["incl"], t["strict"], t["attn"], t["lmat"], t["edec"]
            kdec = kv * edec
            egl = jnp.exp(t["gl"])
            qe = qv * e_c
            ekb = kb * e_c

            t1 = _mm_nt(dov, st)
            ds_o = _mm_tn(qe, dov)
            dattn_raw = _mm_nt(dov, v_new)
            dv_new_o = _mm_tn(attn, dov)
            yield
            while (oi, h) not in state:
                yield
            ds_next = state[(oi, h)]
            dkdec = _mm_nt(v_new, ds_next)
            dv_new_s = _mm(kdec, ds_next)
            yield
            dgl = egl * jnp.sum(jnp.sum(st * ds_next, axis=1, keepdims=True), axis=0, keepdims=True)
            dk = edec * dkdec
            r = jnp.sum(dkdec * kdec, axis=1, keepdims=True)
            dgc = -r
            dgl = dgl + jnp.sum(r, axis=0, keepdims=True)
            dq = e_c * t1
            dgc = dgc + jnp.sum(t1 * qe, axis=1, keepdims=True)
            dattn = jnp.where(incl, dattn_raw, 0.0)
            dv_new = dv_new_s + dv_new_o
            dqm = dattn * gam
            z = dattn * attn
            dvb = _mm3(a, dv_new, _TN)
            dq_a = _mm(dqm, kv)
            dk_a = _mm_tn(dqm, qv)
            yield
            dq_ref[rows, sl] = dq + dq_a
            dv_ref[rows, sl] = beta_c * dvb
            ds_kb = _mm_tn(ekb, dvb)
            dekb_neg = _mm_nt(dvb, st)
            dl_neg = _mm_nt(dvb, v_new)
            yield
            state[(oi + 1, h)] = egl * ds_next + ds_o - ds_kb
            dekb = -dekb_neg
            dl = jnp.where(strict, -dl_neg, 0.0)
            dp = dl * gam
            z = z + dl * lmat
            dkb_p = _mm(dp, kv)
            dk_p = _mm_tn(dp, kb)
            dgc = dgc + jnp.sum(dekb * ekb, axis=1, keepdims=True)
            dgc = dgc + jnp.sum(z, axis=1, keepdims=True) - jnp.sum(z.T, axis=1, keepdims=True)
            dgc = dgc + jnp.where(rowi == C - 1, dgl, 0.0)
            yield
            dkb = dkb_p + e_c * dekb
            dk_ref[rows, sl] = dk + dk_a + dk_p + beta_c * dkb
            dbeta = jnp.sum(dkb * kv, axis=1, keepdims=True) + jnp.sum(dvb * vv, axis=1, keepdims=True)
            acc[cc] = acc[cc] + jnp.where(lane == h, dbeta, 0.0) + jnp.where(lane == H + h, dgc, 0.0)

        _interleave(head(oi, h) for oi in range(CPS) for h in range(H))
        for h in range(H):
            ds_ref[h] = state[(CPS, h)]
        ri = lax.broadcasted_iota(jnp.int32, (C, C), 0)
        ci = lax.broadcasted_iota(jnp.int32, (C, C), 1)
        upper = (ri <= ci).astype(F32)
        dal = jnp.zeros((1, LANES), F32)
        ddt = jnp.zeros((1, LANES), F32)
        for cc in range(CPS):
            rows = slice(cc * C, (cc + 1) * C)
            gates = gcol_ref[rows, :]
            dg_all = _mm3(upper, acc[cc])
            d_braw = acc[cc] * gates * (1.0 - gates)
            d_araw = dg_all * (-jnp.exp(al_ref[...])) * _sigmoid(ba_ref[rows, :] + dt_ref[...])
            dgate_ref[rows, :] = jnp.where(lane < H, d_braw, jnp.where(lane < 2 * H, d_araw, 0.0))
            dal = dal + jnp.sum(dg_all * gates, axis=0, keepdims=True)
            ddt = ddt + jnp.sum(d_araw, axis=0, keepdims=True)
        dpar_ref[0:1, :] += dal
        dpar_ref[1:2, :] += ddt

        @pl.when(pl.program_id(0) == NS - 1)
        def _():
            for cp in ccps:
                cp.wait()

    rev = lambda s: NS - 1 - s
    blk = lambda: pl.BlockSpec((CPS * C, AW), lambda s: (rev(s), 0))
    row = pl.BlockSpec((1, LANES), lambda s: (0, 0))
    any_spec = pl.BlockSpec(memory_space=pl.ANY)
    res = pl.pallas_call(
        body, name="delta_bwd", grid=(NS,),
        in_specs=[row, row, blk(), blk(), blk(),
                  pl.BlockSpec((CPS * C, LANES), lambda s: (rev(s), 0)),
                  pl.BlockSpec((CPS, 32, C), lambda s: (rev(s), 0, 0)),
                  pl.BlockSpec((CPS * C, LANES), lambda s: (rev(s), 0)),
                  blk(),
                  pl.BlockSpec((CPS, H, D, D), lambda s: (rev(s), 0, 0, 0)),
                  pl.BlockSpec((CPS, H, C, C), lambda s: (rev(s), 0, 0, 0)),
                  blk()] + [any_spec] * nc,
        out_specs=[blk(), blk(), blk(),
                   pl.BlockSpec((CPS * C, LANES), lambda s: (rev(s), 0)),
                   pl.BlockSpec((8, LANES), lambda s: (0, 0))] + [any_spec] * nc,
        out_shape=[S((T, AW), F32), S((T, AW), F32), S((T, AW), F32),
                   S((T, LANES), F32), S((8, LANES), F32)] + [S((3,) + a.shape[1:], a.dtype) for a in carry],
        scratch_shapes=[pltpu.VMEM((H, D, D), F32),
                        pltpu.SemaphoreType.DMA((max(nc, 1), 3)), pltpu.SemaphoreType.DMA((max(nc, 1), 3))],
        compiler_params=_cp(ARB),
    )(a_log, dt_bias, q, k, v, gcol, grow, ba, vnew, ssave, asave, d_o, *carry)
    return res[:5], res[5:]


def _ln_stats(xv):
    mu = jnp.mean(xv, axis=-1, keepdims=True)
    xc = xv - mu
    var = jnp.mean(xc * xc, axis=-1, keepdims=True)
    rstd = lax.rsqrt(var + EPS)
    return xc * rstd, rstd


def _mix_fwd(o, proj, head_norm_w, ln_w, ln_b, w_sp, bs_t, H, D, G, P):
    T = o.shape[0]
    AW, BW = H * D, G * P
    MIX = AW + BW
    nb = AW // BW if AW % BW == 0 else None
    assert nb == 1, "group widths must match the projection column blocks"
    cb = 3

    def body(o_ref, za_ref, ub_ref, vb_ref, zb_ref, hw_ref, lw_ref, lb_ref, w_ref, bs_ref, out_ref):
        hw = hw_ref[...]
        for h in range(H):
            sl = slice(h * D, (h + 1) * D)
            oh = o_ref[:, sl]
            rs = lax.rsqrt(jnp.mean(oh * oh, axis=-1, keepdims=True) + EPS)
            out_ref[:, sl] = (oh * rs * hw * _silu(za_ref[:, sl])).astype(BF16)
        xhat, _ = _ln_stats(vb_ref[...])
        vn = xhat * lw_ref[...] + lb_ref[...]
        ri = lax.broadcasted_iota(jnp.int32, (P, P), 0)
        ci = lax.broadcasted_iota(jnp.int32, (P, P), 1)
        bsv = bs_ref[...]
        for g in range(G):
            sl = slice(g * P, (g + 1) * P)
            wm = jnp.where(ri >= ci, w_ref[g], 0.0)
            s = _mm(wm, vn[:, sl]) + bsv[:, g:g + 1]
            out_ref[:, AW + g * P:AW + (g + 1) * P] = (ub_ref[:, sl] * s * _silu(zb_ref[:, sl])).astype(BF16)

    row = lambda w: pl.BlockSpec((1, w), lambda i: (0, 0))
    return pl.pallas_call(
        body, name="mix_fwd", grid=(T // P,),
        in_specs=[pl.BlockSpec((P, AW), lambda i: (i, 0)),
                  pl.BlockSpec((P, AW), lambda i: (i, cb)),
                  pl.BlockSpec((P, BW), lambda i: (i, cb + 1)),
                  pl.BlockSpec((P, BW), lambda i: (i, cb + 2)),
                  pl.BlockSpec((P, BW), lambda i: (i, cb + 3)),
                  row(D), row(BW), row(BW),
                  pl.BlockSpec((G, P, P), lambda i: (0, 0, 0)),
                  pl.BlockSpec((P, G), lambda i: (0, 0))],
        out_specs=pl.BlockSpec((P, MIX), lambda i: (i, 0)),
        out_shape=S((T, MIX), BF16),
        compiler_params=_cp(ARB),
    )(o, proj, proj, proj, proj, head_norm_w, ln_w, ln_b, w_sp, bs_t)


def _mix_bwd(d_ocat, o, proj, head_norm_w, ln_w, ln_b, w_sp, bs_t, H, D, G, P, carry):
    T = o.shape[0]
    AW, BW = H * D, G * P
    MIX = AW + BW
    cb = 3
    nc = len(carry)

    def body(dc_ref, o_ref, za_ref, ub_ref, vb_ref, zb_ref, hw_ref, lw_ref, lb_ref, w_ref, bs_ref, *rest):
        cins = rest[:nc]
        do_ref, dmain_ref, dhw_ref, dln_ref, dw_ref, dbs_ref = rest[nc:nc + 6]
        couts = rest[nc + 6:2 * nc + 6]
        dvn_ref, drest_ref, out_sems, csend, crecv = rest[2 * nc + 6:]
        i = pl.program_id(0)
        slot = lax.rem(i, 2)
        ccps = _sibling_copies(cins, couts, csend, crecv)

        def out_copy(step, s):
            return pltpu.make_async_copy(
                drest_ref.at[s], dmain_ref.at[pl.ds(step * P, P), pl.ds(cb * AW, AW + 3 * BW)], out_sems.at[s])

        @pl.when(i == 0)
        def _():
            dhw_ref[...] = jnp.zeros_like(dhw_ref)
            dln_ref[...] = jnp.zeros_like(dln_ref)
            dw_ref[...] = jnp.zeros_like(dw_ref)
            dbs_ref[...] = jnp.zeros_like(dbs_ref)
            for cp in ccps:
                cp.start()

        @pl.when(i >= 2)
        def _():
            out_copy(i - 2, slot).wait()

        hw = hw_ref[...]
        dhw = jnp.zeros((1, D), F32)
        for h in range(H):
            sl = slice(h * D, (h + 1) * D)
            oh = o_ref[:, sl]
            za = za_ref[:, sl]
            doa = dc_ref[:, sl]
            rs = lax.rsqrt(jnp.mean(oh * oh, axis=-1, keepdims=True) + EPS)
            xh = oh * rs
            d_on = doa * _silu(za)
            drest_ref[slot, :, sl] = (doa * (xh * hw) * _dsilu(za)).astype(BF16)
            dhw = dhw + jnp.sum(d_on * xh, axis=0, keepdims=True)
            dxh = d_on * hw
            do_ref[:, sl] = rs * (dxh - xh * jnp.mean(dxh * xh, axis=-1, keepdims=True))
        dhw_ref[0:1, :] += dhw

        xhat, rstd = _ln_stats(vb_ref[...])
        lw = lw_ref[...]
        vn = xhat * lw + lb_ref[...]
        ri = lax.broadcasted_iota(jnp.int32, (P, P), 0)
        ci = lax.broadcasted_iota(jnp.int32, (P, P), 1)
        lane = lax.broadcasted_iota(jnp.int32, (P, LANES), 1)
        bsv = bs_ref[...]
        dbs = jnp.zeros((P, LANES), F32)
        for g in range(G):
            sl = slice(g * P, (g + 1) * P)
            wm = jnp.where(ri >= ci, w_ref[g], 0.0)
            vng = vn[:, sl]
            s = _mm(wm, vng) + bsv[:, g:g + 1]
            dob = dc_ref[:, AW + g * P:AW + (g + 1) * P]
            ub = ub_ref[:, sl]
            zb = zb_ref[:, sl]
            szb = _silu(zb)
            drest_ref[slot, :, AW + g * P:AW + (g + 1) * P] = (dob * s * szb).astype(BF16)
            drest_ref[slot, :, AW + 2 * BW + g * P:AW + 2 * BW + (g + 1) * P] = (
                dob * ub * s * _dsilu(zb)).astype(BF16)
            ds = dob * ub * szb
            dvn_ref[:, sl] = _mm_tn(wm, ds)
            dw_ref[g] += jnp.where(ri >= ci, _mm_nt(ds, vng), 0.0)
            dbs = dbs + jnp.where(lane == g, jnp.sum(ds, axis=1, keepdims=True), 0.0)
        dbs_ref[...] += dbs
        dvn = dvn_ref[...]
        dln_ref[0:1, :] += jnp.sum(dvn * xhat, axis=0, keepdims=True)
        dln_ref[1:2, :] += jnp.sum(dvn, axis=0, keepdims=True)
        dxh = dvn * lw
        dvb = rstd * (dxh - jnp.mean(dxh, axis=-1, keepdims=True) - xhat * jnp.mean(dxh * xhat, axis=-1, keepdims=True))
        drest_ref[slot, :, AW + BW:AW + 2 * BW] = dvb.astype(BF16)

        out_copy(i, slot).start()

        @pl.when(i == nstep - 1)
        def _():
            out_copy(i, slot).wait()
            if nstep > 1:
                out_copy(i - 1, 1 - slot).wait()
            for cp in ccps:
                cp.wait()

    nstep = T // P
    row = lambda w: pl.BlockSpec((1, w), lambda i: (0, 0))
    any_spec = pl.BlockSpec(memory_space=pl.ANY)
    res = pl.pallas_call(
        body, name="mix_bwd", grid=(nstep,),
        in_specs=[pl.BlockSpec((P, MIX), lambda i: (i, 0)),
                  pl.BlockSpec((P, AW), lambda i: (i, 0)),
                  pl.BlockSpec((P, AW), lambda i: (i, cb)),
                  pl.BlockSpec((P, BW), lambda i: (i, cb + 1)),
                  pl.BlockSpec((P, BW), lambda i: (i, cb + 2)),
                  pl.BlockSpec((P, BW), lambda i: (i, cb + 3)),
                  row(D), row(BW), row(BW),
                  pl.BlockSpec((G, P, P), lambda i: (0, 0, 0)),
                  pl.BlockSpec((P, G), lambda i: (0, 0))] + [any_spec] * nc,
        out_specs=[pl.BlockSpec((P, AW), lambda i: (i, 0)),
                   any_spec,
                   pl.BlockSpec((8, D), lambda i: (0, 0)),
                   pl.BlockSpec((8, BW), lambda i: (0, 0)),
                   pl.BlockSpec((G, P, P), lambda i: (0, 0, 0)),
                   pl.BlockSpec((P, LANES), lambda i: (0, 0))] + [any_spec] * nc,
        out_shape=[S((T, AW), F32), S((T, cb * AW + AW + 3 * BW), BF16), S((8, D), F32), S((8, BW), F32),
                   S((G, P, P), F32), S((P, LANES), F32)] + [S(a.shape[:1] + a.shape[2:], a.dtype) for a in carry],
        scratch_shapes=[pltpu.VMEM((P, BW), F32), pltpu.VMEM((2, P, AW + 3 * BW), BF16),
                        pltpu.SemaphoreType.DMA((2,))] + _sibling_sems(carry),
        compiler_params=_cp(ARB),
    )(d_ocat, o, proj, proj, proj, proj, head_norm_w, ln_w, ln_b, w_sp, bs_t, *carry)
    return res[:6], res[6:]


def _out_proj_loss(ocat, w_out, x, target, fnw):
    T, MIX = ocat.shape
    DM = x.shape[1]
    tm = _tile(T, 256, 8)

    def body(oc_ref, w_ref, x_ref, t_ref, fw_ref, dh_ref, dhb_ref, doc_ref, loss_ref, gfw_ref):
        @pl.when(pl.program_id(0) == 0)
        def _():
            loss_ref[...] = jnp.zeros_like(loss_ref)
            gfw_ref[...] = jnp.zeros_like(gfw_ref)

        wv = w_ref[...]
        hh = x_ref[...] + jnp.dot(oc_ref[...].astype(MXU), wv.astype(MXU), preferred_element_type=F32)
        rs = lax.rsqrt(jnp.mean(hh * hh, axis=-1, keepdims=True) + EPS)
        hn = hh * rs
        fw = fw_ref[...]
        e = hn * fw - t_ref[...]
        row_loss = 0.5 * jnp.mean(e * e, axis=-1, keepdims=True)
        loss_ref[...] += jnp.sum(row_loss, axis=0, keepdims=True)
        dy = e * (1.0 / DM)
        gfw_ref[0:1, :] += jnp.sum(dy * hn, axis=0, keepdims=True)
        dhn = dy * fw
        dh = rs * (dhn - hn * jnp.mean(dhn * hn, axis=-1, keepdims=True))
        dh_ref[...] = dh
        dhb = dh.astype(BF16)
        dhb_ref[...] = dhb
        doc_ref[...] = _mm_nt(dhb, wv)

    return pl.pallas_call(
        body, name="out_proj_loss", grid=(T // tm,),
        in_specs=[pl.BlockSpec((tm, MIX), lambda i: (i, 0)),
                  pl.BlockSpec((MIX, DM), lambda i: (0, 0)),
                  pl.BlockSpec((tm, DM), lambda i: (i, 0)),
                  pl.BlockSpec((tm, DM), lambda i: (i, 0)),
                  pl.BlockSpec((1, DM), lambda i: (0, 0))],
        out_specs=[pl.BlockSpec((tm, DM), lambda i: (i, 0)),
                   pl.BlockSpec((tm, DM), lambda i: (i, 0)),
                   pl.BlockSpec((tm, MIX), lambda i: (i, 0)),
                   pl.BlockSpec((8, LANES), lambda i: (0, 0)),
                   pl.BlockSpec((8, DM), lambda i: (0, 0))],
        out_shape=[S((T, DM), F32), S((T, DM), BF16), S((T, MIX), F32), S((8, LANES), F32), S((8, DM), F32)],
        compiler_params=_cp(ARB),
    )(ocat, w_out, x, target, fnw)


def _grad_w(lhs, rhs, name):
    T, A = lhs.shape
    B = rhs.shape[1]
    ta = _tile(A, 512, LANES)
    tk = _tile(T, 1024, 16)
    nk = T // tk

    def body(l_ref, r_ref, out_ref, acc_ref):
        k = pl.program_id(1)
        part = _mm_tn(l_ref[...], r_ref[...])

        @pl.when(k == 0)
        def _():
            acc_ref[...] = part

        @pl.when(k > 0)
        def _():
            acc_ref[...] += part

        @pl.when(k == nk - 1)
        def _():
            out_ref[...] = acc_ref[...].astype(BF16)

    return pl.pallas_call(
        body, name=name, grid=(A // ta, nk),
        in_specs=[pl.BlockSpec((tk, ta), lambda i, k: (k, i)),
                  pl.BlockSpec((tk, B), lambda i, k: (k, 0))],
        out_specs=pl.BlockSpec((ta, B), lambda i, k: (i, 0)),
        out_shape=S((A, B), BF16),
        scratch_shapes=[pltpu.VMEM((ta, B), F32)],
        compiler_params=_cp(ARB, ARB),
    )(lhs, rhs)


def _grad_w_in(xn, dmain, dba, WD, gate_lo, gate_hi):
    T, DM = xn.shape
    NM = dmain.shape[1]
    tn = _tile(NM, 1024, LANES)
    tk = _tile(T, 2048, 16)
    nj, nk = NM // tn, T // tk
    ND = N_DEV
    tiles = [[] for _ in range(nj)]
    first_tile, last_tile = {}, {}
    for d, s0, s1, dest, c0 in _pieces(WD, gate_lo, gate_hi, ND * WD):
        if dest != "main":
            continue
        while s0 < s1:
            jj = c0 // tn
            w = min(s1 - s0, (jj + 1) * tn - c0)
            tiles[jj].append((d, s0, w, "main", c0 - jj * tn))
            first_tile.setdefault(d, jj)
            last_tile[d] = jj
            s0, c0 = s0 + w, c0 + w
    for d, s0, s1, dest, c0 in _pieces(WD, gate_lo, gate_hi, ND * WD):
        if dest == "gate":
            tiles[first_tile[d]].append((d, s0, s1 - s0, "gate", c0))
    assert sorted(first_tile) == list(range(ND)) and all(last_tile[d] <= first_tile[d + 2] for d in range(ND - 2))

    def body(xn_ref, dm_ref, dba_ref, keep_ref, recv_ref, acc_ref, gate_ref, buf_ref, lsem, ssem, rsem):
        j = pl.program_id(0)
        k = pl.program_id(1)
        px, py, pc = _position()

        @pl.when(k == 0)
        def _():
            acc_ref[...] = jnp.zeros_like(acc_ref)

        @pl.when((j == 0) & (k == 0))
        def _():
            gate_ref[...] = jnp.zeros_like(gate_ref)

        xv = xn_ref[...]
        acc_ref[...] += _mm_tn(xv, dm_ref[...])

        @pl.when(j == 0)
        def _():
            gate_ref[...] += _mm_tn(xv, dba_ref[...])

        def local(d):
            return pltpu.make_async_copy(buf_ref.at[d % 2], keep_ref.at[d // 2], lsem.at[d // 2])

        def remote(d):
            return pltpu.make_async_remote_copy(
                src_ref=buf_ref.at[d % 2], dst_ref=recv_ref.at[d // 2], send_sem=ssem.at[d // 2],
                recv_sem=rsem.at[d // 2], device_id=(px, py, 1 - pc), device_id_type=MESH)

        def leave(d, start):
            @pl.when(pc == d % 2)
            def _():
                local(d).start() if start else local(d).wait()

            @pl.when(pc != d % 2)
            def _():
                remote(d).start() if start else remote(d).wait_send()

        def emit(jj):
            shards = sorted({p[0] for p in tiles[jj]})
            for d in shards:
                if first_tile[d] == jj and d >= 2:
                    leave(d - 2, False)
                for dd, s0, w, src, c0 in tiles[jj]:
                    if dd == d:
                        ref = acc_ref if src == "main" else gate_ref
                        buf_ref[d % 2, :, s0:s0 + w] = ref[:, c0:c0 + w].astype(BF16)
                if last_tile[d] == jj:
                    leave(d, True)
            if jj == nj - 1:
                for d in (ND - 2, ND - 1):
                    leave(d, False)
                for q in range(ND // 2):
                    remote(2 * q).wait_recv()

        for jj in range(nj):
            @pl.when((j == jj) & (k == nk - 1))
            def _(jj=jj):
                emit(jj)

    any_spec = pl.BlockSpec(memory_space=pl.ANY)
    return pl.pallas_call(
        body, name="grad_w_in", grid=(nj, nk),
        in_specs=[pl.BlockSpec((tk, DM), lambda j, k: (k, 0)),
                  pl.BlockSpec((tk, tn), lambda j, k: (k, j)),
                  pl.BlockSpec((tk, LANES), lambda j, k: (k, 0))],
        out_specs=[any_spec, any_spec],
        out_shape=[S((ND // 2, DM, WD), BF16), S((ND // 2, DM, WD), BF16)],
        scratch_shapes=[pltpu.VMEM((DM, tn), F32), pltpu.VMEM((DM, LANES), F32), pltpu.VMEM((2, DM, WD), BF16),
                        pltpu.SemaphoreType.DMA((ND // 2,)), pltpu.SemaphoreType.DMA((ND // 2,)),
                        pltpu.SemaphoreType.DMA((ND // 2,))],
        compiler_params=_cp(ARB, ARB),
    )(xn, dmain, dba)


def _pair_sum_plain(a, b, name):
    K, R, C = a.shape
    tr = _tile(R, 256, 16)

    def body(a_ref, b_ref, o_ref):
        o_ref[...] = (a_ref[...].astype(F32) + b_ref[...].astype(F32)).astype(BF16)

    spec = lambda: pl.BlockSpec((1, tr, C), lambda q, i: (q, i, 0))
    return pl.pallas_call(body, name=name, grid=(K, R // tr), in_specs=[spec(), spec()], out_specs=spec(),
                          out_shape=S((K, R, C), BF16), compiler_params=_cp(ARB, ARB))(a, b)


def _dx_rows(T):
    tm = _tile(T, 512, 8)
    return tm if T // tm >= 2 else T // 2


def _dx_part(name, dmain, dba, w_main, w_ba, x, dh, norm_w, blk0, nblk, prev, hbm_in, hbm_alias, hbm_new, make_copies):
    T, NM = dmain.shape
    DM = x.shape[1]
    tm = _dx_rows(T)
    tk = _tile(NM, 1024, LANES)
    nk = NM // tk
    n_in, n_al, n_new = len(hbm_in), len(hbm_alias), len(hbm_new)
    n_prev = 0 if prev is None else 2
    last_step = nblk * nk - 1

    def body(dm_ref, dba_ref, w_ref, wba_ref, x_ref, dh_ref, nw_ref, *rest):
        r = list(rest)
        gnw_prev_ref = r.pop(0) if n_prev else None
        if n_prev:
            r.pop(0)
        in_refs = [r.pop(0) for _ in range(n_in)]
        del r[:n_al]
        gx_ref, gnw_ref = r.pop(0), r.pop(0)
        alias_refs = [r.pop(0) for _ in range(n_al)]
        new_refs = [r.pop(0) for _ in range(n_new)]
        acc_ref, send_sems, recv_sems = r
        i = pl.program_id(0)
        k = pl.program_id(1)
        step = i * nk + k
        cps = make_copies(in_refs, alias_refs, new_refs, send_sems, recv_sems)

        @pl.when(step == 0)
        def _():
            gnw_ref[...] = gnw_prev_ref[...] if n_prev else jnp.zeros_like(gnw_ref)
            for cp in cps:
                cp.start()

        @pl.when(k == 0)
        def _():
            acc_ref[...] = _mm_nt(dba_ref[...], wba_ref[...])

        acc_ref[...] += _mm_nt(dm_ref[...], w_ref[...])

        @pl.when(k == nk - 1)
        def _():
            xv = x_ref[...]
            rs = lax.rsqrt(jnp.mean(xv * xv, axis=-1, keepdims=True) + EPS)
            xh = xv * rs
            dxn = acc_ref[...]
            gnw_ref[0:1, :] += jnp.sum(dxn * xh, axis=0, keepdims=True)
            dxh = dxn * nw_ref[...]
            gx_ref[...] = dh_ref[...] + rs * (dxh - xh * jnp.mean(dxh * xh, axis=-1, keepdims=True))

        @pl.when(step == last_step)
        def _():
            for cp in cps:
                cp.wait()

    any_spec = pl.BlockSpec(memory_space=pl.ANY)
    prev_specs = [pl.BlockSpec((8, DM), lambda i, k: (0, 0)), any_spec] if n_prev else []
    prev_args = [prev[1], prev[0]] if n_prev else []
    aliases = {8: 0} if n_prev else {}
    for q in range(n_al):
        aliases[7 + n_prev + n_in + q] = 2 + q
    res = pl.pallas_call(
        body, name=name, grid=(nblk, nk),
        in_specs=[pl.BlockSpec((tm, tk), lambda i, k: (blk0 + i, k)),
                  pl.BlockSpec((tm, LANES), lambda i, k: (blk0 + i, 0)),
                  pl.BlockSpec((DM, tk), lambda i, k: (0, k)),
                  pl.BlockSpec((DM, LANES), lambda i, k: (0, 0)),
                  pl.BlockSpec((tm, DM), lambda i, k: (blk0 + i, 0)),
                  pl.BlockSpec((tm, DM), lambda i, k: (blk0 + i, 0)),
                  pl.BlockSpec((1, DM), lambda i, k: (0, 0))] + prev_specs + [any_spec] * (n_in + n_al),
        out_specs=[pl.BlockSpec((tm, DM), lambda i, k: (blk0 + i, 0)),
                   pl.BlockSpec((8, DM), lambda i, k: (0, 0))] + [any_spec] * (n_al + n_new),
        out_shape=[S((T, DM), F32), S((8, DM), F32)] + [S(a.shape, a.dtype) for a in hbm_alias] + list(hbm_new),
        scratch_shapes=[pltpu.VMEM((tm, DM), F32), pltpu.SemaphoreType.DMA((10,)), pltpu.SemaphoreType.DMA((10,))],
        input_output_aliases=aliases,
        compiler_params=_cp(ARB, ARB),
    )(dmain, dba, w_main, w_ba, x, dh, norm_w, *prev_args, *hbm_in, *hbm_alias)
    return (res[0], res[1]), res[2:2 + n_al], res[2 + n_al:]


def _remote(kk, src, dst, to, send_sems, recv_sems):
    return pltpu.make_async_remote_copy(src_ref=src, dst_ref=dst, send_sem=send_sems.at[kk], recv_sem=recv_sems.at[kk],
                                        device_id=to, device_id_type=MESH)


def _dx(dmain, dba, w_main, w_ba, x, dh, norm_w, chip_sum, small, cut):
    R, C = chip_sum.shape[1:]
    half = R // 2
    assert half % 16 == 0
    T = x.shape[0]
    ni = T // _dx_rows(T)
    cut = max(1, min(cut, ni - 1))
    upper, lower = pl.ds(0, half), pl.ds(half, half)

    def nbrs():
        px, py, pc = _position()
        return (px, py), (1 - px, py, pc), (px, 1 - py, pc)

    def phase1(ins, als, news, ss, rs):
        (px, py), xn, yn = nbrs()
        cs = ins[0]
        recv, stage = news
        bx, by, bd = cs.at[2 * (1 - px) + py], cs.at[2 * px + (1 - py)], cs.at[2 * (1 - px) + (1 - py)]
        return [_remote(0, bx.at[upper], recv.at[0].at[upper], xn, ss, rs),
                _remote(1, by.at[lower], recv.at[1].at[lower], yn, ss, rs),
                _remote(2, bd.at[upper], stage.at[0], xn, ss, rs),
                _remote(3, bd.at[lower], stage.at[1], yn, ss, rs)]

    def phase2(ins, als, news, ss, rs):
        (px, py), xn, yn = nbrs()
        comb, small_ref = ins
        recv, gath = als[0], news[0]
        me, small_cps = _broadcast_copies([small_ref], [gath], _Sem2(ss, 2), _Sem2(rs, 2))
        return ([_remote(0, comb.at[0], recv.at[1].at[upper], yn, ss, rs),
                 _remote(1, comb.at[1], recv.at[0].at[lower], xn, ss, rs)] + small_cps
                + [pltpu.make_async_copy(small_ref, gath.at[me], ss.at[9])])

    (gx, gnw), _, (recv, stage) = _dx_part(
        "dx_a", dmain, dba, w_main, w_ba, x, dh, norm_w, 0, cut, None, [chip_sum], [],
        [S((2, R, C), chip_sum.dtype), S((2, half, C), chip_sum.dtype)], phase1)
    comb = _relay_add(chip_sum, stage)
    (gx, gnw), (recv,), (gath,) = _dx_part(
        "dx_b", dmain, dba, w_main, w_ba, x, dh, norm_w, cut, ni - cut, (gx, gnw), [comb, small], [recv],
        [S((N_DEV,) + small.shape, F32)], phase2)
    return gx, gnw, gath, recv


class _Sem2:
    def __init__(self, sems, lo):
        self.sems, self.lo = sems, lo

    @property
    def at(self):
        outer = self

        class _At:
            def __getitem__(self, idx):
                a, k = idx
                return outer.sems.at[outer.lo + k]
        return _At()


def _relay_add(chip_sum, stage):
    _, R, C = chip_sum.shape
    half = R // 2
    tr = _tile(half, 256, 16)
    nt = half // tr
    px, py, _ = _position()
    idx = jnp.stack([2 * px + (1 - py), 2 * (1 - px) + py]).astype(jnp.int32)

    def body(idx_ref, p_ref, s_ref, o_ref):
        del idx_ref
        o_ref[0] = (p_ref[0].astype(F32) + s_ref[0].astype(F32)).astype(BF16)

    return pl.pallas_call(
        body, name="relay_add",
        grid_spec=pltpu.PrefetchScalarGridSpec(
            num_scalar_prefetch=1, grid=(2, nt),
            in_specs=[pl.BlockSpec((1, tr, C), lambda s, i, idx_ref: (idx_ref[s], s * nt + i, 0)),
                      pl.BlockSpec((1, tr, C), lambda s, i, idx_ref: (s, i, 0))],
            out_specs=pl.BlockSpec((1, tr, C), lambda s, i, idx_ref: (s, i, 0))),
        out_shape=S((2, half, C), BF16), compiler_params=_cp(ARB, ARB),
    )(idx, chip_sum, stage)


def _sum_slots(gath):
    _, R, C = gath.shape
    tr = R if R <= 2048 else _tile(R, 512, 8)

    def body(g_ref, o_ref):
        tot = g_ref[0]
        for d in range(1, N_DEV):
            tot = tot + g_ref[d]
        o_ref[...] = tot

    return pl.pallas_call(
        body, name="sum_slots", grid=(R // tr,),
        in_specs=[pl.BlockSpec((N_DEV, tr, C), lambda i: (0, i, 0))],
        out_specs=pl.BlockSpec((tr, C), lambda i: (i, 0)),
        out_shape=S((R, C), F32), compiler_params=_cp(ARB),
    )(gath)


def _prep_a_bwd(dq, dk, dv, c, proj, conv_w, dmain, H, D):
    T = c.shape[0]
    AW = H * D
    C3 = 3 * AW
    tb = _tile(T, 256, 8)
    nblk = T // tb
    r8 = tb // 8
    scale = float(D) ** -0.5

    def body(dq_ref, dk_ref, dv_ref, c_ref, dqn_ref, dkn_ref, dvn_ref, cn_ref, x_ref, halo_ref, cw_ref, dmain_in_ref,
             dx_ref, gcw_ref, dc_ref):
        del dmain_in_ref
        i = pl.program_id(0)

        @pl.when(i == 0)
        def _():
            gcw_ref[...] = jnp.zeros_like(gcw_ref)

        def pointwise(rows, dq_r, dk_r, dv_r, c_r, keep):
            for h in range(H):
                for part, d_r, sc in ((0, dq_r, scale), (1, dk_r, 1.0)):
                    sl = slice(part * AW + h * D, part * AW + (h + 1) * D)
                    cv = c_r[:, sl]
                    raw = _silu(cv)
                    rs = lax.rsqrt(jnp.sum(raw * raw, axis=-1, keepdims=True) + EPS)
                    nrm = raw * rs
                    dn = d_r[:, h * D:(h + 1) * D] * sc
                    draw = rs * (dn - nrm * jnp.sum(dn * nrm, axis=-1, keepdims=True))
                    dc_ref[rows, sl] = draw * _dsilu(cv) * keep
            dc_ref[rows, 2 * AW:] = dv_r[...] * _dsilu(c_r[:, 2 * AW:]) * keep

        pointwise(slice(0, tb), dq_ref, dk_ref, dv_ref, c_ref, 1.0)
        pointwise(slice(tb, tb + 8), dqn_ref, dkn_ref, dvn_ref, cn_ref, (i < nblk - 1).astype(F32))

        cw = cw_ref[...]
        dcv = dc_ref[0:tb, :]
        dx = cw[3:4, :] * dcv
        for j in range(3):
            dx = dx + cw[j:j + 1, :] * dc_ref[3 - j:3 - j + tb, :]
        dx_ref[...] = dx.astype(BF16)
        halo = halo_ref[...] * (i > 0).astype(F32)
        xp = jnp.concatenate([halo, x_ref[...]], axis=0)
        for j in range(4):
            gcw_ref[j:j + 1, :] += jnp.sum(dcv * xp[5 + j:5 + j + tb], axis=0, keepdims=True)

    nxt = lambda i: (jnp.minimum((i + 1) * r8, T // 8 - 1), 0)
    return pl.pallas_call(
        body, name="prep_a_bwd", grid=(nblk,),
        in_specs=[pl.BlockSpec((tb, AW), lambda i: (i, 0)),
                  pl.BlockSpec((tb, AW), lambda i: (i, 0)),
                  pl.BlockSpec((tb, AW), lambda i: (i, 0)),
                  pl.BlockSpec((tb, C3), lambda i: (i, 0)),
                  pl.BlockSpec((8, AW), nxt), pl.BlockSpec((8, AW), nxt), pl.BlockSpec((8, AW), nxt),
                  pl.BlockSpec((8, C3), nxt),
                  pl.BlockSpec((tb, C3), lambda i: (i, 0)),
                  pl.BlockSpec((8, C3), lambda i: (jnp.maximum(i * r8 - 1, 0), 0)),
                  pl.BlockSpec((4, C3), lambda i: (0, 0)),
                  pl.BlockSpec(memory_space=pl.ANY)],
        out_specs=[pl.BlockSpec((tb, C3), lambda i: (i, 0)),
                   pl.BlockSpec((8, C3), lambda i: (0, 0))],
        out_shape=[S(dmain.shape, dmain.dtype), S((8, C3), F32)],
        scratch_shapes=[pltpu.VMEM((tb + 8, C3), F32)],
        input_output_aliases={11: 0},
        compiler_params=_cp(ARB),
    )(dq, dk, dv, c, dq, dk, dv, c, proj, proj, conv_w, dmain)


def _adam_math(w, g, m, v):
    m2 = ADAM_B1 * m + (1.0 - ADAM_B1) * g
    v2 = ADAM_B2 * v + (1.0 - ADAM_B2) * (g * g)
    m_hat = m2 / (1.0 - ADAM_B1 ** ADAM_STEP)
    v_hat = v2 / (1.0 - ADAM_B2 ** ADAM_STEP)
    delta = -ADAM_LR * (m_hat / (jnp.sqrt(v_hat) + ADAM_EPS) + ADAM_WD * w)
    return delta, m2, v2


def _pair_sum(blocks, recv, core, name):
    K, _, R, C = blocks.shape
    tr = _tile(R, 256, 16)

    def body(core_ref, a_ref, b_ref, o_ref):
        del core_ref
        o_ref[0] = (a_ref[0, 0].astype(F32) + b_ref[0].astype(F32)).astype(BF16)

    spec = lambda: pl.BlockSpec((1, tr, C), lambda k, i, core_ref: (k, i, 0))
    return pl.pallas_call(
        body, name=name,
        grid_spec=pltpu.PrefetchScalarGridSpec(
            num_scalar_prefetch=1, grid=(K, R // tr),
            in_specs=[pl.BlockSpec((1, 1, tr, C), lambda k, i, core_ref: (k, core_ref[0], i, 0)), spec()],
            out_specs=spec()),
        out_shape=S((K, R, C), BF16), compiler_params=_cp(ARB, ARB),
    )(core, blocks, recv)


def _sum_adam(chip_sums, recv, w, m, v, chip, name, transposed=False):
    R, C = chip_sums.shape[1:]
    NR = recv.shape[0]
    tr = _tile(R, 256, 16)

    def body(chip_ref, own_ref, r_ref, w_ref, m_ref, v_ref, g_ref, d_ref, m2_ref, v2_ref):
        del chip_ref
        g = own_ref[0].astype(F32)
        for j in range(NR):
            g = g + r_ref[j].astype(F32)
        if transposed:
            g = g.T
        g_ref[...] = g
        d_ref[...], m2_ref[...], v2_ref[...] = _adam_math(w_ref[...], g, m_ref[...], v_ref[...])

    if transposed:
        spec = lambda: pl.BlockSpec((C, tr), lambda i, chip_ref: (0, i))
        shape = (C, R)
    else:
        spec = lambda: pl.BlockSpec((tr, C), lambda i, chip_ref: (i, 0))
        shape = (R, C)
    assert w.shape == shape
    return pl.pallas_call(
        body, name=name,
        grid_spec=pltpu.PrefetchScalarGridSpec(
            num_scalar_prefetch=1, grid=(R // tr,),
            in_specs=[pl.BlockSpec((1, tr, C), lambda i, chip_ref: (chip_ref[0], i, 0)),
                      pl.BlockSpec((NR, tr, C), lambda i, chip_ref: (0, i, 0)), spec(), spec(), spec()],
            out_specs=[spec(), spec(), spec(), spec()]),
        out_shape=[S(shape, F32)] * 4, compiler_params=_cp(ARB),
    )(chip, chip_sums, recv, w, m, v)


def _adam_small(w, g, m, v):
    R, C = w.shape
    tr = _tile(R, 512, 8)

    def body(w_ref, g_ref, m_ref, v_ref, d_ref, m2_ref, v2_ref):
        d_ref[...], m2_ref[...], v2_ref[...] = _adam_math(w_ref[...], g_ref[...], m_ref[...], v_ref[...])

    spec = lambda: pl.BlockSpec((tr, C), lambda i: (i, 0))
    return pl.pallas_call(
        body, name="adam_small", grid=(R // tr,), in_specs=[spec()] * 4, out_specs=[spec()] * 3,
        out_shape=[S((R, C), F32)] * 3, compiler_params=_cp(ARB),
    )(w, g, m, v)


def _position():
    return lax.axis_index("x"), lax.axis_index("y"), lax.axis_index("c")


def _all_gather_weights(arr, x_in, norm_w):
    R = arr.shape[0]
    half = R // 2
    assert half % 16 == 0
    T, DM = x_in.shape
    tm = _tile(T, 512, 16)
    nstep = T // tm

    def body(x_ref, nw_ref, in_ref, xn_ref, out_ref, send_sems, recv_sems, local_sem):
        i = pl.program_id(0)
        x, y, c = _position()
        me, sibling = (x, y, c), (x, y, 1 - c)
        xn, yn, diag = (1 - x, y), (x, 1 - y), (1 - x, 1 - y)
        upper, lower = pl.ds(0, half), pl.ds(half, half)

        def slot(p, rows=None):
            ref = out_ref.at[4 * p[0] + 2 * p[1] + p[2]]
            return ref if rows is None else ref.at[rows]

        def copy(kk, block, to, rows=None, src=None):
            return pltpu.make_async_remote_copy(
                src_ref=slot(block, rows) if src is None else src, dst_ref=slot(block, rows),
                send_sem=send_sems.at[kk], recv_sem=recv_sems.at[kk], device_id=to, device_id_type=MESH)

        mine = pltpu.make_async_copy(in_ref, slot(me), local_sem)
        first = [copy(0, me, sibling, src=in_ref), copy(1, me, (*xn, c), src=in_ref), copy(2, me, (*yn, c), src=in_ref)]

        @pl.when(i == 0)
        def _():
            mine.start()
            for cp in first:
                cp.start()

        xv = x_ref[...]
        r = lax.rsqrt(jnp.mean(xv * xv, axis=-1, keepdims=True) + EPS)
        xn_ref[...] = (xv * r * nw_ref[...]).astype(BF16)

        @pl.when(i == nstep - 1)
        def _():
            sent = list(first)

            def then(cps):
                for cp in cps:
                    cp.start()
                sent.extend(cps)

            copy(1, (*xn, c), me).wait_recv()
            then([copy(5, (*xn, c), (*yn, c), rows=upper), copy(3, (*xn, c), sibling)])
            copy(2, (*yn, c), me).wait_recv()
            then([copy(6, (*yn, c), (*xn, c), rows=lower), copy(4, (*yn, c), sibling)])
            copy(5, (*diag, c), me, rows=upper).wait_recv()
            then([copy(7, (*diag, c), sibling, rows=upper)])
            copy(6, (*diag, c), me, rows=lower).wait_recv()
            then([copy(8, (*diag, c), sibling, rows=lower)])
            copy(0, sibling, me).wait_recv()
            copy(3, (*xn, 1 - c), me).wait_recv()
            copy(4, (*yn, 1 - c), me).wait_recv()
            copy(7, (*diag, 1 - c), me, rows=upper).wait_recv()
            copy(8, (*diag, 1 - c), me, rows=lower).wait_recv()
            for cp in sent:
                cp.wait_send()
            mine.wait()

    any_spec = pl.BlockSpec(memory_space=pl.ANY)
    return pl.pallas_call(
        body, name="all_gather_weights", grid=(nstep,),
        in_specs=[pl.BlockSpec((tm, DM), lambda i: (i, 0)), pl.BlockSpec((1, DM), lambda i: (0, 0)), any_spec],
        out_specs=[pl.BlockSpec((tm, DM), lambda i: (i, 0)), any_spec],
        out_shape=[S((T, DM), BF16), S((N_DEV,) + arr.shape, arr.dtype)],
        scratch_shapes=[pltpu.SemaphoreType.DMA((9,)), pltpu.SemaphoreType.DMA((9,)), pltpu.SemaphoreType.DMA],
        compiler_params=_cp(ARB),
    )(x_in, norm_w, arr)


def _sibling_copies(ins, outs, send_sems, recv_sems):
    x, y, c = _position()
    return [pltpu.make_async_remote_copy(src_ref=ins[a].at[k, 1 - c], dst_ref=outs[a].at[k],
                                         send_sem=send_sems.at[a, k], recv_sem=recv_sems.at[a, k],
                                         device_id=(x, y, 1 - c), device_id_type=MESH)
            for a in range(len(ins)) for k in range(ins[a].shape[0])]


def _sibling_sems(arrs):
    shape = (max(len(arrs), 1), arrs[0].shape[0] if arrs else 1)
    return [pltpu.SemaphoreType.DMA(shape), pltpu.SemaphoreType.DMA(shape)]


def _chip_exchange_copies(ins, outs, send_sems, recv_sems):
    x, y, c = _position()
    chips = [(1 - x, y), (x, 1 - y), (1 - x, 1 - y)]
    return [pltpu.make_async_remote_copy(
        src_ref=ins[a].at[2 * qx + qy], dst_ref=outs[a].at[j], send_sem=send_sems.at[a, j],
        recv_sem=recv_sems.at[a, j], device_id=(qx, qy, c), device_id_type=MESH)
        for a in range(len(ins)) for j, (qx, qy) in enumerate(chips)]


def _broadcast_copies(srcs, dsts, send_sems, recv_sems):
    x, y, c = _position()
    me = 4 * x + 2 * y + c
    cps = []
    for a in range(len(srcs)):
        for k in range(1, N_DEV):
            peer = (1 - x if k & 4 else x, 1 - y if k & 2 else y, 1 - c if k & 1 else c)
            cps.append(pltpu.make_async_remote_copy(
                src_ref=srcs[a], dst_ref=dsts[a].at[me], send_sem=send_sems.at[a, k - 1],
                recv_sem=recv_sems.at[a, k - 1], device_id=peer, device_id_type=MESH))
    return me, cps


def _all_reduce_small(part):
    R, C = part.shape

    def body(p_ref, out_ref, gath_ref, send_sems, recv_sems):
        me, cps = _broadcast_copies([p_ref], [gath_ref], send_sems, recv_sems)
        gath_ref[me] = p_ref[...]
        for cp in cps:
            cp.start()
        for cp in cps:
            cp.wait()
        acc = gath_ref[0]
        for d in range(1, N_DEV):
            acc = acc + gath_ref[d]
        out_ref[...] = acc

    vm = pl.BlockSpec(memory_space=pltpu.VMEM)
    return pl.pallas_call(
        body, name="all_reduce_small", in_specs=[vm], out_specs=vm, out_shape=S((R, C), F32),
        scratch_shapes=[pltpu.VMEM((N_DEV, R, C), F32), pltpu.SemaphoreType.DMA((1, N_DEV - 1)),
                        pltpu.SemaphoreType.DMA((1, N_DEV - 1))],
    )(part)


def _pack(parts):
    rows = []
    for p in parts:
        f = p.reshape(-1).astype(F32)
        pad = (-f.shape[0]) % (8 * LANES)
        rows.append(jnp.pad(f, (0, pad)).reshape(-1, LANES))
    return jnp.concatenate(rows, axis=0)


def _unpack(buf, shapes):
    out, r = [], 0
    for shp in shapes:
        n = 1
        for s in shp:
            n *= s
        nr = -(-n // (8 * LANES)) * 8
        out.append(buf[r:r + nr].reshape(-1)[:n].reshape(shp))
        r += nr
    return out


def kernel(x, norm_w, w_in, conv_w, a_log, dt_bias, head_norm_w, sgu_ln_w, sgu_ln_b, w_spatial, b_spatial, w_out, final_norm_w, loss_target, m_norm_w, m_w_in, m_conv_w, m_a_log, m_dt_bias, m_head_norm_w, m_sgu_ln_w, m_sgu_ln_b, m_w_spatial, m_b_spatial, m_w_out, m_final_norm_w, v_norm_w, v_w_in, v_conv_w, v_a_log, v_dt_bias, v_head_norm_w, v_sgu_ln_w, v_sgu_ln_b, v_w_spatial, v_b_spatial, v_w_out, v_final_norm_w):
    T, DM = x.shape[1], x.shape[2]
    H, D = a_log.shape[1], head_norm_w.shape[1]
    G, P = w_spatial.shape[1], w_spatial.shape[2]
    AW, BW = H * D, G * P
    MIX = AW + BW
    WD = w_in.shape[2]
    IN = N_DEV * WD
    RO = w_out.shape[1]
    CW = conv_w.shape[2]
    sizes = (3 * AW, AW, H, H, BW, BW, BW)
    assert sum(sizes) == IN and 2 * H <= LANES and 3 * H <= 32 and N_DEV * RO == MIX and N_DEV * CW == 3 * AW
    offs = [0]
    for s in sizes:
        offs.append(offs[-1] + s)
    px, py, pc = _position()
    dev = 4 * px + 2 * py + pc
    chip = 2 * px + py

    x2, tgt = x[0], loss_target[0]

    xn, g_win = _all_gather_weights(_cast_bf16_t(w_in[0].T, "cast_w_in"), x2, norm_w)
    w_main, w_ba = _relayout_w(g_win, offs[2], offs[4])
    alog_row = jnp.pad(a_log, ((0, 0), (H, LANES - 2 * H)))
    dtb_row = jnp.pad(dt_bias, ((0, 0), (H, LANES - 2 * H)))
    bs_t = b_spatial[0].T

    proj, ba, (g_wout, g_conv) = _in_proj(xn, w_main, w_ba, [_cast_bf16(w_out[0], "cast_w_out"), conv_w[0]])
    w_out_full = g_wout.reshape(MIX, DM)
    conv_full = g_conv.transpose(1, 0, 2).reshape(4, 3 * AW)
    q, k, v, c, gcol, grow = _prep_a_fwd(proj, ba, conv_full, alog_row, dtb_row, H, D)
    o, vnew, ssave, asave = _delta_fwd(q, k, v, gcol, grow, H, D)
    ocat = _mix_fwd(o, proj, head_norm_w, sgu_ln_w, sgu_ln_b, w_spatial[0], bs_t, H, D, G, P)
    dh, dh_bf, d_ocat, loss_acc, g_fnw = _out_proj_loss(ocat, w_out_full, x2, tgt, final_norm_w.reshape(1, DM))

    core_idx = jnp.reshape(pc, (1,)).astype(jnp.int32)
    chip_idx = jnp.reshape(chip, (1,)).astype(jnp.int32)
    g_wout_blocks = _grad_w(ocat, dh_bf, "grad_w_out").reshape(4, 2, RO, DM)
    (d_o, dmain, g_hnw, g_ln, g_wsp, g_bs_t), (sib_wout,) = _mix_bwd(
        d_ocat, o, proj, head_norm_w, sgu_ln_w, sgu_ln_b, w_spatial[0], bs_t, H, D, G, P, [g_wout_blocks])
    chip_wout = _pair_sum(g_wout_blocks, sib_wout, core_idx, "pair_sum_w_out")
    (dq, dk, dv, dgate, dpar), (recv_wout,) = _delta_bwd(
        q, k, v, gcol, grow, ba, vnew, ssave, asave, d_o, alog_row, dtb_row, H, D, [chip_wout])
    dmain, g_conv_part = _prep_a_bwd(dq, dk, dv, c, proj, conv_full, dmain, H, D)
    dba = dgate.astype(BF16)
    keep_win, sib_win = _grad_w_in(xn, dmain, dba, WD, offs[2], offs[4])
    chip_win = _pair_sum_plain(keep_win, sib_win, "pair_sum_w_in")
    small_shapes = [a_log.shape, dt_bias.shape, head_norm_w.shape, sgu_ln_w.shape, sgu_ln_b.shape,
                    w_spatial.shape, b_spatial.shape, final_norm_w.shape]
    parts = [dpar[0, H:2 * H], dpar[1, H:2 * H], g_hnw[0], g_ln[0], g_ln[1], g_wsp, g_bs_t[:, :G].T, g_fnw[0],
             g_conv_part[:4], loss_acc[0, :1]]
    grad_x, g_nw, small_gath, recv_win = _dx(dmain, dba, w_main, w_ba, x2, dh, norm_w, chip_win, _pack(parts), 4)
    red = _sum_slots(small_gath)
    grad_w_in, delta_w_in, new_m_w_in, new_v_w_in = _sum_adam(
        chip_win, recv_win, w_in[0].T, m_w_in[0].T, v_w_in[0].T, chip_idx, "sum_adam_w_in", transposed=True)
    grad_w_out, delta_w_out, new_m_w_out, new_v_w_out = _sum_adam(
        chip_wout, recv_wout, w_out[0], m_w_out[0], v_w_out[0], chip_idx, "sum_adam_w_out")
    red_nw = _all_reduce_small(_pack([g_nw[0]]))
    grads_small = _unpack(red_nw, [norm_w.shape]) + _unpack(red, small_shapes + [(4, 3 * AW), (1,)])
    loss = grads_small.pop()[0]
    g_conv_full = grads_small.pop()
    grad_conv = lax.dynamic_slice_in_dim(g_conv_full, dev * CW, CW, axis=1)[None]
    small_w = [norm_w, a_log, dt_bias, head_norm_w, sgu_ln_w, sgu_ln_b, w_spatial, b_spatial, final_norm_w, conv_w]
    small_m = [m_norm_w, m_a_log, m_dt_bias, m_head_norm_w, m_sgu_ln_w, m_sgu_ln_b, m_w_spatial, m_b_spatial,
               m_final_norm_w, m_conv_w]
    small_v = [v_norm_w, v_a_log, v_dt_bias, v_head_norm_w, v_sgu_ln_w, v_sgu_ln_b, v_w_spatial, v_b_spatial,
               v_final_norm_w, v_conv_w]
    small_g = grads_small + [grad_conv]
    shapes10 = [w.shape for w in small_w]
    d_p, m_p, v_p = _adam_small(_pack(small_w), _pack(small_g), _pack(small_m), _pack(small_v))
    d_s, m_s, v_s = _unpack(d_p, shapes10), _unpack(m_p, shapes10), _unpack(v_p, shapes10)

    def order(small, win, wout):
        return [small[0], win.T[None], small[9], small[1], small[2], small[3], small[4], small[5], small[6], small[7],
                wout[None], small[8]]

    grads = order(small_g, grad_w_in, grad_w_out)
    deltas = order(d_s, delta_w_in, delta_w_out)
    new_m = order(m_s, new_m_w_in, new_m_w_out)
    new_v = order(v_s, new_v_w_in, new_v_w_out)
    return (loss, grad_x[None], *grads, *deltas, *new_m, *new_v)
```

```python
import jax
import jax.numpy as jnp
from jax import lax
from jax.experimental import pallas as pl
from jax.experimental.pallas import tpu as pltpu

F32 = jnp.float32
BF16 = jnp.bfloat16
MXU = jnp.bfloat16
HI = lax.Precision.HIGHEST
EPS = 1e-6
CHUNK_A = 64
LANES = 128
MESH = pl.DeviceIdType.MESH
N_DEV = 8

ADAM_LR = 0.001
ADAM_B1 = 0.9
ADAM_B2 = 0.999
ADAM_EPS = 1e-08
ADAM_WD = 0.01
ADAM_STEP = 10

S = jax.ShapeDtypeStruct
ARB = "arbitrary"


def _cp(*sem, vmem_mib=56):
    return pltpu.CompilerParams(dimension_semantics=tuple(sem), vmem_limit_bytes=vmem_mib * 1024 * 1024)


def _tile(n, cap, mult):
    best = None
    t = mult
    while t <= min(n, cap):
        if n % t == 0:
            best = t
        t += mult
    return best if best is not None else n


def _mm(a, b):
    return jnp.dot(a.astype(MXU), b.astype(MXU), preferred_element_type=F32)


def _mm_nt(a, b):
    return lax.dot_general(a.astype(MXU), b.astype(MXU), (((1,), (1,)), ((), ())), preferred_element_type=F32)


def _mm_tn(a, b):
    return lax.dot_general(a.astype(MXU), b.astype(MXU), (((0,), (0,)), ((), ())), preferred_element_type=F32)


def _mmh(a, b):
    return jnp.dot(a, b, precision=HI, preferred_element_type=F32)


def _sigmoid(x):
    return 1.0 / (1.0 + jnp.exp(-x))


def _silu(x):
    return x * _sigmoid(x)


def _dsilu(x):
    s = _sigmoid(x)
    return s * (1.0 + x * (1.0 - s))


def _softplus(x):
    return jnp.maximum(x, 0.0) + jnp.log(1.0 + jnp.exp(-jnp.abs(x)))


def _pieces(wd, gate_lo, gate_hi, total):
    out = []
    for d in range(N_DEV):
        lo, hi = d * wd, (d + 1) * wd
        for dest, a, b, shift in (("main", 0, gate_lo, 0), ("gate", gate_lo, gate_hi, -gate_lo),
                                  ("main", gate_hi, total, gate_lo - gate_hi)):
            s0, s1 = max(lo, a), min(hi, b)
            if s0 < s1:
                out.append((d, s0 - lo, s1 - lo, dest, s0 + shift))
    return out


def _local_tiles(wd, gate_lo, gate_hi, nm):
    n_tiles = N_DEV - 1
    assert nm % (n_tiles * LANES) == 0
    tn = nm // n_tiles
    plans = []
    for m in range(N_DEV // 2):
        lo, hi = 2 * m * tn, (2 * m + 1) * tn
        plan = []
        for d, s0, s1, dest, c0 in _pieces(wd, gate_lo, gate_hi, N_DEV * wd):
            if dest != "main":
                continue
            a, b = max(c0, lo), min(c0 + (s1 - s0), hi)
            if a < b:
                assert d // 2 == m, "tile 2m must come from chip m's own shards"
                plan.append((d, s0 + (a - c0), b - a, a - lo))
        assert sum(p[2] for p in plan) == tn
        plans.append(plan)
    return tn, plans


def _cast_bf16(a, name):
    R, C = a.shape
    tr = _tile(R, 256, 16)

    def body(a_ref, o_ref):
        o_ref[...] = a_ref[...].astype(BF16)

    spec = pl.BlockSpec((tr, C), lambda i: (i, 0))
    return pl.pallas_call(body, name=name, grid=(R // tr,), in_specs=[spec], out_specs=spec,
                          out_shape=S((R, C), BF16), compiler_params=_cp(ARB))(a)


def _cast_bf16_t(a_t, name):
    C, R = a_t.shape
    tr = _tile(R, 256, LANES)

    def body(a_ref, o_ref):
        o_ref[...] = a_ref[...].T.astype(BF16)

    return pl.pallas_call(body, name=name, grid=(R // tr,), in_specs=[pl.BlockSpec((C, tr), lambda i: (0, i))],
                          out_specs=pl.BlockSpec((tr, C), lambda i: (i, 0)),
                          out_shape=S((R, C), BF16), compiler_params=_cp(ARB))(a_t)


def _relayout_w(g_win, gate_lo, gate_hi):
    _, DM, WD = g_win.shape
    total = N_DEV * WD
    NM = total - (gate_hi - gate_lo)
    tr = _tile(DM, 256, 16)
    plan = _pieces(WD, gate_lo, gate_hi, total)

    def body(g_ref, main_ref, gate_ref):
        gate_ref[...] = jnp.zeros_like(gate_ref)
        for d, s0, s1, dest, c0 in plan:
            dst = main_ref if dest == "main" else gate_ref
            dst[:, c0:c0 + (s1 - s0)] = g_ref[d, :, s0:s1]

    return pl.pallas_call(
        body, name="relayout_w", grid=(DM // tr,),
        in_specs=[pl.BlockSpec((N_DEV, tr, WD), lambda i: (0, i, 0))],
        out_specs=[pl.BlockSpec((tr, NM), lambda i: (i, 0)), pl.BlockSpec((tr, LANES), lambda i: (i, 0))],
        out_shape=[S((DM, NM), g_win.dtype), S((DM, LANES), g_win.dtype)],
        compiler_params=_cp(ARB),
    )(g_win)


def _in_proj(xn, w_main, w_ba, proj_part, tiles, tn, shards):
    T, DM = xn.shape
    NM = w_main.shape[1]
    tm = _tile(T, 2048, 16)
    ni, nj = T // tm, tiles.shape[0]
    ns = len(shards)

    def body(tiles_ref, xn_ref, w_ref, wba_ref, part_ref, *rest):
        del tiles_ref, part_ref
        srcs = rest[:ns]
        proj_ref, ba_ref = rest[ns:ns + 2]
        gath = rest[ns + 2:2 * ns + 2]
        send_sems, recv_sems, local_sems = rest[2 * ns + 2:]
        i = pl.program_id(0)
        j = pl.program_id(1)
        me, cps = _broadcast_copies(srcs, gath, send_sems, recv_sems)
        cps = cps + [pltpu.make_async_copy(srcs[a], gath[a].at[me], local_sems.at[a]) for a in range(ns)]

        @pl.when((i == 0) & (j == 0))
        def _():
            for cp in cps:
                cp.start()

        @pl.when(j == 0)
        def _():
            ba_ref[...] = jnp.dot(xn_ref[...].astype(MXU), wba_ref[...].astype(MXU), preferred_element_type=F32)

        proj_ref[...] = jnp.dot(xn_ref[...].astype(MXU), w_ref[...].astype(MXU), preferred_element_type=F32)

        @pl.when((i == ni - 1) & (j == nj - 1))
        def _():
            for cp in cps:
                cp.wait()

    any_spec = pl.BlockSpec(memory_space=pl.ANY)
    res = pl.pallas_call(
        body, name="in_proj",
        grid_spec=pltpu.PrefetchScalarGridSpec(
            num_scalar_prefetch=1, grid=(ni, nj),
            in_specs=[pl.BlockSpec((tm, DM), lambda i, j, t: (i, 0)),
                      pl.BlockSpec((DM, tn), lambda i, j, t: (0, t[j])),
                      pl.BlockSpec((DM, LANES), lambda i, j, t: (0, 0)), any_spec] + [any_spec] * ns,
            out_specs=[pl.BlockSpec((tm, tn), lambda i, j, t: (i, t[j])),
                       pl.BlockSpec((tm, LANES), lambda i, j, t: (i, 0))] + [any_spec] * ns,
            scratch_shapes=[pltpu.SemaphoreType.DMA((ns, N_DEV - 1)), pltpu.SemaphoreType.DMA((ns, N_DEV - 1)),
                            pltpu.SemaphoreType.DMA((ns,))]),
        out_shape=[S((T, NM), F32), S((T, LANES), F32)] + [S((N_DEV,) + a.shape, a.dtype) for a in shards],
        input_output_aliases={4: 0},
        compiler_params=_cp(ARB, ARB, vmem_mib=58),
    )(tiles, xn, w_main, w_ba, proj_part, *shards)
    return res[0], res[1], res[2:]


def _prep_a_fwd(proj, ba, conv_w, alog_row, dtb_row, H, D):
    T = proj.shape[0]
    AW = H * D
    C3 = 3 * AW
    tb = _tile(T, 256, CHUNK_A)
    nch = tb // CHUNK_A
    nblk = T // tb
    scale = float(D) ** -0.5

    def body(x_ref, halo_ref, ba_ref, cw_ref, al_ref, dt_ref, q_ref, k_ref, v_ref, c_ref, gcol_ref, grow_ref):
        i = pl.program_id(0)
        xv = x_ref[...]
        halo = halo_ref[...] * (i > 0).astype(F32)
        xp = jnp.concatenate([halo, xv], axis=0)
        cw = cw_ref[...]
        c = cw[0:1, :] * xp[5:5 + tb]
        for j in range(1, 4):
            c = c + cw[j:j + 1, :] * xp[5 + j:5 + j + tb]
        c_ref[...] = c
        a = _silu(c)
        for h in range(H):
            qh = a[:, h * D:(h + 1) * D]
            kh = a[:, AW + h * D:AW + (h + 1) * D]
            qr = lax.rsqrt(jnp.sum(qh * qh, axis=-1, keepdims=True) + EPS)
            kr = lax.rsqrt(jnp.sum(kh * kh, axis=-1, keepdims=True) + EPS)
            q_ref[:, h * D:(h + 1) * D] = qh * (qr * scale)
            k_ref[:, h * D:(h + 1) * D] = kh * kr
        v_ref[...] = a[:, 2 * AW:]

        bav = ba_ref[...]
        lane = lax.broadcasted_iota(jnp.int32, (tb, LANES), 1)
        beta = _sigmoid(bav)
        g = -jnp.exp(al_ref[...]) * _softplus(bav + dt_ref[...])
        gates = jnp.where(lane < H, beta, jnp.where(lane < 2 * H, g, 0.0))
        ri = lax.broadcasted_iota(jnp.int32, (CHUNK_A, CHUNK_A), 0)
        ci = lax.broadcasted_iota(jnp.int32, (CHUNK_A, CHUNK_A), 1)
        tri = (ri >= ci).astype(F32)
        lane_c = lax.broadcasted_iota(jnp.int32, (CHUNK_A, LANES), 1)
        for cc in range(nch):
            gch = gates[cc * CHUNK_A:(cc + 1) * CHUNK_A]
            gc = pltpu.roll(_mmh(tri, gch), H, 1)
            full = jnp.where(lane_c < 2 * H, gch, jnp.where(lane_c < 3 * H, gc, 0.0))
            gcol_ref[cc * CHUNK_A:(cc + 1) * CHUNK_A, :] = full
            grow_ref[cc] = full.T[0:32, :]

    return pl.pallas_call(
        body, name="prep_a_fwd", grid=(nblk,),
        in_specs=[pl.BlockSpec((tb, C3), lambda i: (i, 0)),
                  pl.BlockSpec((8, C3), lambda i: (jnp.maximum(i * (tb // 8) - 1, 0), 0)),
                  pl.BlockSpec((tb, LANES), lambda i: (i, 0)),
                  pl.BlockSpec((4, C3), lambda i: (0, 0)),
                  pl.BlockSpec((1, LANES), lambda i: (0, 0)),
                  pl.BlockSpec((1, LANES), lambda i: (0, 0))],
        out_specs=[pl.BlockSpec((tb, AW), lambda i: (i, 0)),
                   pl.BlockSpec((tb, AW), lambda i: (i, 0)),
                   pl.BlockSpec((tb, AW), lambda i: (i, 0)),
                   pl.BlockSpec((tb, C3), lambda i: (i, 0)),
                   pl.BlockSpec((tb, LANES), lambda i: (i, 0)),
                   pl.BlockSpec((nch, 32, CHUNK_A), lambda i: (i, 0, 0))],
        out_shape=[S((T, AW), F32), S((T, AW), F32), S((T, AW), F32), S((T, C3), F32),
                   S((T, LANES), F32), S((T // CHUNK_A, 32, CHUNK_A), F32)],
        compiler_params=_cp(ARB),
    )(proj, proj, ba, conv_w, alog_row, dtb_row)


_NN = (((1,), (0,)), ((), ()))
_TN = (((0,), (0,)), ((), ()))


def _split(a):
    hi = a.astype(BF16)
    return hi, (a - hi.astype(F32)).astype(BF16)


def _mm3(a, b, dims=_NN):
    ah, al = a if isinstance(a, tuple) else _split(a)
    bh, bl = b if isinstance(b, tuple) else _split(b)
    dg = lambda p, r: lax.dot_general(p, r, dims, preferred_element_type=F32)
    return dg(ah, bh) + (dg(ah, bl) + dg(al, bh))


def _interleave(gens):
    gens = list(gens)
    while gens:
        alive = []
        for g in gens:
            try:
                next(g)
                alive.append(g)
            except StopIteration:
                pass
        gens = alive


def _chunk_terms(q, k, v, gcolv, growv, h, H):
    C = CHUNK_A
    beta_c = gcolv[:, h:h + 1]
    g_c = gcolv[:, H + h:H + h + 1]
    gc_c = gcolv[:, 2 * H + h:2 * H + h + 1]
    gc_r = growv[2 * H + h:2 * H + h + 1, :]
    ri = lax.broadcasted_iota(jnp.int32, (C, C), 0)
    ci = lax.broadcasted_iota(jnp.int32, (C, C), 1)
    incl = ri >= ci
    strict = ri > ci
    kb = k * beta_c
    vb = v * beta_c
    p_raw = _mm_nt(kb, k)
    qk_raw = _mm_nt(q, k)
    gam = jnp.where(incl, jnp.exp(jnp.where(incl, gc_c - gc_r, 0.0)), 0.0)
    e_c = jnp.exp(gc_c)
    gl = gc_r[:, C - 1:C]
    edec = jnp.exp(gl - gc_c)
    yield
    lmat = jnp.where(strict, p_raw * gam, 0.0)
    attn = jnp.where(incl, qk_raw * gam, 0.0)
    return dict(beta_c=beta_c, g_c=g_c, gc_c=gc_c, gc_r=gc_r, incl=incl, strict=strict, gam=gam, e_c=e_c,
                kb=kb, vb=vb, lmat=lmat, attn=attn, gl=gl, edec=edec, ri=ri, ci=ci)


INV_BLOCK = 16


def _inv_unit_lower(lmat):
    C = lmat.shape[0]
    ri = lax.broadcasted_iota(jnp.int32, (C, C), 0)
    ci = lax.broadcasted_iota(jnp.int32, (C, C), 1)
    eye = (ri == ci).astype(F32)
    same = (ri // INV_BLOCK) == (ci // INV_BLOCK)

    def neumann(x, order):
        a = eye + x
        n = 1
        while 2 * n < order:
            xs = _split(x)
            x = _mm3(xs, xs)
            yield
            a = a + _mm3(a, x)
            n *= 2
        yield
        return a

    inv_d = yield from neumann(-jnp.where(same, lmat, 0.0), INV_BLOCK)
    m = _mm3(inv_d, jnp.where(same, 0.0, lmat))
    yield
    inv_m = yield from neumann(-m, C // INV_BLOCK)
    a = _mm3(inv_m, inv_d)
    yield
    return a


def _delta_fwd(q, k, v, gcol, grow, H, D):
    T = q.shape[0]
    C = CHUNK_A
    N = T // C
    AW = H * D
    CPS = 2 if N % 2 == 0 else 1

    def body(q_ref, k_ref, v_ref, gcol_ref, grow_ref, o_ref, vn_ref, ssave_ref, asave_ref, s_ref):
        @pl.when(pl.program_id(0) == 0)
        def _():
            s_ref[...] = jnp.zeros_like(s_ref)

        state = {(0, h): s_ref[h] for h in range(H)}

        def head(cc, h):
            rows = slice(cc * C, (cc + 1) * C)
            sl = slice(h * D, (h + 1) * D)
            qv, kv, vv = q_ref[rows, sl], k_ref[rows, sl], v_ref[rows, sl]
            t = yield from _chunk_terms(qv, kv, vv, gcol_ref[rows, :], grow_ref[cc], h, H)
            a = yield from _inv_unit_lower(t["lmat"])
            asave_ref[cc, h] = a
            while (cc, h) not in state:
                yield
            st = state[(cc, h)]
            ssave_ref[cc, h] = st
            ks = _mm(t["kb"] * t["e_c"], st)
            o_inter = _mm(qv * t["e_c"], st)
            yield
            v_new = _mm3(a, t["vb"] - ks)
            yield
            vn_ref[rows, sl] = v_new
            o_intra = _mm(t["attn"], v_new)
            s_upd = _mm_tn(kv * t["edec"], v_new)
            yield
            o_ref[rows, sl] = o_inter + o_intra
            state[(cc + 1, h)] = st * jnp.exp(t["gl"]) + s_upd

        _interleave(head(cc, h) for cc in range(CPS) for h in range(H))
        for h in range(H):
            s_ref[h] = state[(CPS, h)]

    blk = lambda: pl.BlockSpec((CPS * C, AW), lambda n: (n, 0))
    return pl.pallas_call(
        body, name="delta_fwd", grid=(N // CPS,),
        in_specs=[blk(), blk(), blk(),
                  pl.BlockSpec((CPS * C, LANES), lambda n: (n, 0)),
                  pl.BlockSpec((CPS, 32, C), lambda n: (n, 0, 0))],
        out_specs=[blk(), blk(),
                   pl.BlockSpec((CPS, H, D, D), lambda n: (n, 0, 0, 0)),
                   pl.BlockSpec((CPS, H, C, C), lambda n: (n, 0, 0, 0))],
        out_shape=[S((T, AW), F32), S((T, AW), F32), S((N, H, D, D), F32), S((N, H, C, C), F32)],
        scratch_shapes=[pltpu.VMEM((H, D, D), F32)],
        compiler_params=_cp(ARB),
    )(q, k, v, gcol, grow)


def _delta_bwd(q, k, v, gcol, grow, ba, vnew, ssave, asave, d_o, a_log, dt_bias, H, D, carry):
    T = q.shape[0]
    C = CHUNK_A
    N = T // C
    AW = H * D
    nc = len(carry)
    CPS = 2 if N % 2 == 0 else 1
    NS = N // CPS

    def body(al_ref, dt_ref, q_ref, k_ref, v_ref, gcol_ref, grow_ref, ba_ref, vn_ref, ss_ref, as_ref, do_ref, *rest):
        cins = rest[:nc]
        dq_ref, dk_ref, dv_ref, dgate_ref, dpar_ref = rest[nc:nc + 5]
        couts = rest[nc + 5:2 * nc + 5]
        ds_ref, csend, crecv = rest[2 * nc + 5:]
        ccps = _chip_exchange_copies(cins, couts, csend, crecv)

        @pl.when(pl.program_id(0) == 0)
        def _():
            ds_ref[...] = jnp.zeros_like(ds_ref)
            dpar_ref[...] = jnp.zeros_like(dpar_ref)
            for cp in ccps:
                cp.start()

        lane = lax.broadcasted_iota(jnp.int32, (C, LANES), 1)
        rowi = lax.broadcasted_iota(jnp.int32, (C, 1), 0)
        acc = {cc: jnp.zeros((C, LANES), F32) for cc in range(CPS)}
        state = {(0, h): ds_ref[h] for h in range(H)}

        def head(oi, h):
            cc = CPS - 1 - oi
            rows = slice(cc * C, (cc + 1) * C)
            sl = slice(h * D, (h + 1) * D)
            st = ss_ref[cc, h]
            a = as_ref[cc, h]
            qv, kv, vv, dov, v_new = q_ref[rows, sl], k_ref[rows, sl], v_ref[rows, sl], do_ref[rows, sl], vn_ref[rows, sl]
            t = yield from _chunk_terms(qv, kv, vv, gcol_ref[rows, :], grow_ref[cc], h, H)
            beta_c, e_c, gam, kb = t["beta_c"], t["e_c"], t["gam"], t["kb"]
            incl, strict, attn, lmat, edec = t["incl"], t["strict"], t["attn"], t["lmat"], t["edec"]
            kdec = kv * edec
            egl = jnp.exp(t["gl"])
            qe = qv * e_c
            ekb = kb * e_c

            t1 = _mm_nt(dov, st)
            ds_o = _mm_tn(qe, dov)
            dattn_raw = _mm_nt(dov, v_new)
            dv_new_o = _mm_tn(attn, dov)
            yield
            while (oi, h) not in state:
                yield
            ds_next = state[(oi, h)]
            dkdec = _mm_nt(v_new, ds_next)
            dv_new_s = _mm(kdec, ds_next)
            yield
            dgl = egl * jnp.sum(jnp.sum(st * ds_next, axis=1, keepdims=True), axis=0, keepdims=True)
            dk = edec * dkdec
            r = jnp.sum(dkdec * kdec, axis=1, keepdims=True)
            dgc = -r
            dgl = dgl + jnp.sum(r, axis=0, keepdims=True)
            dq = e_c * t1
            dgc = dgc + jnp.sum(t1 * qe, axis=1, keepdims=True)
            dattn = jnp.where(incl, dattn_raw, 0.0)
            dv_new = dv_new_s + dv_new_o
            dqm = dattn * gam
            z = dattn * attn
            dvb = _mm3(a, dv_new, _TN)
            dq_a = _mm(dqm, kv)
            dk_a = _mm_tn(dqm, qv)
            yield
            dq_ref[rows, sl] = dq + dq_a
            dv_ref[rows, sl] = beta_c * dvb
            ds_kb = _mm_tn(ekb, dvb)
            dekb_neg = _mm_nt(dvb, st)
            dl_neg = _mm_nt(dvb, v_new)
            yield
            state[(oi + 1, h)] = egl * ds_next + ds_o - ds_kb
            dekb = -dekb_neg
            dl = jnp.where(strict, -dl_neg, 0.0)
            dp = dl * gam
            z = z + dl * lmat
            dkb_p = _mm(dp, kv)
            dk_p = _mm_tn(dp, kb)
            dgc = dgc + jnp.sum(dekb * ekb, axis=1, keepdims=True)
            dgc = dgc + jnp.sum(z, axis=1, keepdims=True) - jnp.sum(z.T, axis=1, keepdims=True)
            dgc = dgc + jnp.where(rowi == C - 1, dgl, 0.0)
            yield
            dkb = dkb_p + e_c * dekb
            dk_ref[rows, sl] = dk + dk_a + dk_p + beta_c * dkb
            dbeta = jnp.sum(dkb * kv, axis=1, keepdims=True) + jnp.sum(dvb * vv, axis=1, keepdims=True)
            acc[cc] = acc[cc] + jnp.where(lane == h, dbeta, 0.0) + jnp.where(lane == H + h, dgc, 0.0)

        _interleave(head(oi, h) for oi in range(CPS) for h in range(H))
        for h in range(H):
            ds_ref[h] = state[(CPS, h)]
        ri = lax.broadcasted_iota(jnp.int32, (C, C), 0)
        ci = lax.broadcasted_iota(jnp.int32, (C, C), 1)
        upper = (ri <= ci).astype(F32)
        dal = jnp.zeros((1, LANES), F32)
        ddt = jnp.zeros((1, LANES), F32)
        for cc in range(CPS):
            rows = slice(cc * C, (cc + 1) * C)
            gates = gcol_ref[rows, :]
            dg_all = _mm3(upper, acc[cc])
            d_braw = acc[cc] * gates * (1.0 - gates)
            d_araw = dg_all * (-jnp.exp(al_ref[...])) * _sigmoid(ba_ref[rows, :] + dt_ref[...])
            dgate_ref[rows, :] = jnp.where(lane < H, d_braw, jnp.where(lane < 2 * H, d_araw, 0.0))
            dal = dal + jnp.sum(dg_all * gates, axis=0, keepdims=True)
            ddt = ddt + jnp.sum(d_araw, axis=0, keepdims=True)
        dpar_ref[0:1, :] += dal
        dpar_ref[1:2, :] += ddt

        @pl.when(pl.program_id(0) == NS - 1)
        def _():
            for cp in ccps:
                cp.wait()

    rev = lambda s: NS - 1 - s
    blk = lambda: pl.BlockSpec((CPS * C, AW), lambda s: (rev(s), 0))
    row = pl.BlockSpec((1, LANES), lambda s: (0, 0))
    any_spec = pl.BlockSpec(memory_space=pl.ANY)
    res = pl.pallas_call(
        body, name="delta_bwd", grid=(NS,),
        in_specs=[row, row, blk(), blk(), blk(),
                  pl.BlockSpec((CPS * C, LANES), lambda s: (rev(s), 0)),
                  pl.BlockSpec((CPS, 32, C), lambda s: (rev(s), 0, 0)),
                  pl.BlockSpec((CPS * C, LANES), lambda s: (rev(s), 0)),
                  blk(),
                  pl.BlockSpec((CPS, H, D, D), lambda s: (rev(s), 0, 0, 0)),
                  pl.BlockSpec((CPS, H, C, C), lambda s: (rev(s), 0, 0, 0)),
                  blk()] + [any_spec] * nc,
        out_specs=[blk(), blk(), blk(),
                   pl.BlockSpec((CPS * C, LANES), lambda s: (rev(s), 0)),
                   pl.BlockSpec((8, LANES), lambda s: (0, 0))] + [any_spec] * nc,
        out_shape=[S((T, AW), F32), S((T, AW), F32), S((T, AW), F32),
                   S((T, LANES), F32), S((8, LANES), F32)] + [S((3,) + a.shape[1:], a.dtype) for a in carry],
        scratch_shapes=[pltpu.VMEM((H, D, D), F32),
                        pltpu.SemaphoreType.DMA((max(nc, 1), 3)), pltpu.SemaphoreType.DMA((max(nc, 1), 3))],
        compiler_params=_cp(ARB),
    )(a_log, dt_bias, q, k, v, gcol, grow, ba, vnew, ssave, asave, d_o, *carry)
    return res[:5], res[5:]


def _ln_stats(xv):
    mu = jnp.mean(xv, axis=-1, keepdims=True)
    xc = xv - mu
    var = jnp.mean(xc * xc, axis=-1, keepdims=True)
    rstd = lax.rsqrt(var + EPS)
    return xc * rstd, rstd


def _mix_fwd(o, proj, head_norm_w, ln_w, ln_b, w_sp, bs_t, H, D, G, P):
    T = o.shape[0]
    AW, BW = H * D, G * P
    MIX = AW + BW
    nb = AW // BW if AW % BW == 0 else None
    assert nb == 1, "group widths must match the projection column blocks"
    cb = 3

    def body(o_ref, za_ref, ub_ref, vb_ref, zb_ref, hw_ref, lw_ref, lb_ref, w_ref, bs_ref, out_ref):
        hw = hw_ref[...]
        for h in range(H):
            sl = slice(h * D, (h + 1) * D)
            oh = o_ref[:, sl]
            rs = lax.rsqrt(jnp.mean(oh * oh, axis=-1, keepdims=True) + EPS)
            out_ref[:, sl] = (oh * rs * hw * _silu(za_ref[:, sl])).astype(BF16)
        xhat, _ = _ln_stats(vb_ref[...])
        vn = xhat * lw_ref[...] + lb_ref[...]
        ri = lax.broadcasted_iota(jnp.int32, (P, P), 0)
        ci = lax.broadcasted_iota(jnp.int32, (P, P), 1)
        bsv = bs_ref[...]
        for g in range(G):
            sl = slice(g * P, (g + 1) * P)
            wm = jnp.where(ri >= ci, w_ref[g], 0.0)
            s = _mm(wm, vn[:, sl]) + bsv[:, g:g + 1]
            out_ref[:, AW + g * P:AW + (g + 1) * P] = (ub_ref[:, sl] * s * _silu(zb_ref[:, sl])).astype(BF16)

    row = lambda w: pl.BlockSpec((1, w), lambda i: (0, 0))
    return pl.pallas_call(
        body, name="mix_fwd", grid=(T // P,),
        in_specs=[pl.BlockSpec((P, AW), lambda i: (i, 0)),
                  pl.BlockSpec((P, AW), lambda i: (i, cb)),
                  pl.BlockSpec((P, BW), lambda i: (i, cb + 1)),
                  pl.BlockSpec((P, BW), lambda i: (i, cb + 2)),
                  pl.BlockSpec((P, BW), lambda i: (i, cb + 3)),
                  row(D), row(BW), row(BW),
                  pl.BlockSpec((G, P, P), lambda i: (0, 0, 0)),
                  pl.BlockSpec((P, G), lambda i: (0, 0))],
        out_specs=pl.BlockSpec((P, MIX), lambda i: (i, 0)),
        out_shape=S((T, MIX), BF16),
        compiler_params=_cp(ARB),
    )(o, proj, proj, proj, proj, head_norm_w, ln_w, ln_b, w_sp, bs_t)


def _mix_bwd(d_ocat, o, proj, head_norm_w, ln_w, ln_b, w_sp, bs_t, H, D, G, P, carry):
    T = o.shape[0]
    AW, BW = H * D, G * P
    MIX = AW + BW
    cb = 3
    nc = len(carry)

    def body(dc_ref, o_ref, za_ref, ub_ref, vb_ref, zb_ref, hw_ref, lw_ref, lb_ref, w_ref, bs_ref, *rest):
        cins = rest[:nc]
        do_ref, dmain_ref, dhw_ref, dln_ref, dw_ref, dbs_ref = rest[nc:nc + 6]
        couts = rest[nc + 6:2 * nc + 6]
        dvn_ref, drest_ref, out_sems, csend, crecv = rest[2 * nc + 6:]
        i = pl.program_id(0)
        slot = lax.rem(i, 2)
        ccps = _sibling_copies(cins, couts, csend, crecv)

        def out_copy(step, s):
            return pltpu.make_async_copy(
                drest_ref.at[s], dmain_ref.at[pl.ds(step * P, P), pl.ds(cb * AW, AW + 3 * BW)], out_sems.at[s])

        @pl.when(i == 0)
        def _():
            dhw_ref[...] = jnp.zeros_like(dhw_ref)
            dln_ref[...] = jnp.zeros_like(dln_ref)
            dw_ref[...] = jnp.zeros_like(dw_ref)
            dbs_ref[...] = jnp.zeros_like(dbs_ref)
            for cp in ccps:
                cp.start()

        @pl.when(i >= 2)
        def _():
            out_copy(i - 2, slot).wait()

        hw = hw_ref[...]
        dhw = jnp.zeros((1, D), F32)
        for h in range(H):
            sl = slice(h * D, (h + 1) * D)
            oh = o_ref[:, sl]
            za = za_ref[:, sl]
            doa = dc_ref[:, sl]
            rs = lax.rsqrt(jnp.mean(oh * oh, axis=-1, keepdims=True) + EPS)
            xh = oh * rs
            d_on = doa * _silu(za)
            drest_ref[slot, :, sl] = (doa * (xh * hw) * _dsilu(za)).astype(BF16)
            dhw = dhw + jnp.sum(d_on * xh, axis=0, keepdims=True)
            dxh = d_on * hw
            do_ref[:, sl] = rs * (dxh - xh * jnp.mean(dxh * xh, axis=-1, keepdims=True))
        dhw_ref[0:1, :] += dhw

        xhat, rstd = _ln_stats(vb_ref[...])
        lw = lw_ref[...]
        vn = xhat * lw + lb_ref[...]
        ri = lax.broadcasted_iota(jnp.int32, (P, P), 0)
        ci = lax.broadcasted_iota(jnp.int32, (P, P), 1)
        lane = lax.broadcasted_iota(jnp.int32, (P, LANES), 1)
        bsv = bs_ref[...]
        dbs = jnp.zeros((P, LANES), F32)
        for g in range(G):
            sl = slice(g * P, (g + 1) * P)
            wm = jnp.where(ri >= ci, w_ref[g], 0.0)
            vng = vn[:, sl]
            s = _mm(wm, vng) + bsv[:, g:g + 1]
            dob = dc_ref[:, AW + g * P:AW + (g + 1) * P]
            ub = ub_ref[:, sl]
            zb = zb_ref[:, sl]
            szb = _silu(zb)
            drest_ref[slot, :, AW + g * P:AW + (g + 1) * P] = (dob * s * szb).astype(BF16)
            drest_ref[slot, :, AW + 2 * BW + g * P:AW + 2 * BW + (g + 1) * P] = (
                dob * ub * s * _dsilu(zb)).astype(BF16)
            ds = dob * ub * szb
            dvn_ref[:, sl] = _mm_tn(wm, ds)
            dw_ref[g] += jnp.where(ri >= ci, _mm_nt(ds, vng), 0.0)
            dbs = dbs + jnp.where(lane == g, jnp.sum(ds, axis=1, keepdims=True), 0.0)
        dbs_ref[...] += dbs
        dvn = dvn_ref[...]
        dln_ref[0:1, :] += jnp.sum(dvn * xhat, axis=0, keepdims=True)
        dln_ref[1:2, :] += jnp.sum(dvn, axis=0, keepdims=True)
        dxh = dvn * lw
        dvb = rstd * (dxh - jnp.mean(dxh, axis=-1, keepdims=True) - xhat * jnp.mean(dxh * xhat, axis=-1, keepdims=True))
        drest_ref[slot, :, AW + BW:AW + 2 * BW] = dvb.astype(BF16)

        out_copy(i, slot).start()

        @pl.when(i == nstep - 1)
        def _():
            out_copy(i, slot).wait()
            if nstep > 1:
                out_copy(i - 1, 1 - slot).wait()
            for cp in ccps:
                cp.wait()

    nstep = T // P
    row = lambda w: pl.BlockSpec((1, w), lambda i: (0, 0))
    any_spec = pl.BlockSpec(memory_space=pl.ANY)
    res = pl.pallas_call(
        body, name="mix_bwd", grid=(nstep,),
        in_specs=[pl.BlockSpec((P, MIX), lambda i: (i, 0)),
                  pl.BlockSpec((P, AW), lambda i: (i, 0)),
                  pl.BlockSpec((P, AW), lambda i: (i, cb)),
                  pl.BlockSpec((P, BW), lambda i: (i, cb + 1)),
                  pl.BlockSpec((P, BW), lambda i: (i, cb + 2)),
                  pl.BlockSpec((P, BW), lambda i: (i, cb + 3)),
                  row(D), row(BW), row(BW),
                  pl.BlockSpec((G, P, P), lambda i: (0, 0, 0)),
                  pl.BlockSpec((P, G), lambda i: (0, 0))] + [any_spec] * nc,
        out_specs=[pl.BlockSpec((P, AW), lambda i: (i, 0)),
                   any_spec,
                   pl.BlockSpec((8, D), lambda i: (0, 0)),
                   pl.BlockSpec((8, BW), lambda i: (0, 0)),
                   pl.BlockSpec((G, P, P), lambda i: (0, 0, 0)),
                   pl.BlockSpec((P, LANES), lambda i: (0, 0))] + [any_spec] * nc,
        out_shape=[S((T, AW), F32), S((T, cb * AW + AW + 3 * BW), BF16), S((8, D), F32), S((8, BW), F32),
                   S((G, P, P), F32), S((P, LANES), F32)] + [S(a.shape[:1] + a.shape[2:], a.dtype) for a in carry],
        scratch_shapes=[pltpu.VMEM((P, BW), F32), pltpu.VMEM((2, P, AW + 3 * BW), BF16),
                        pltpu.SemaphoreType.DMA((2,))] + _sibling_sems(carry),
        compiler_params=_cp(ARB),
    )(d_ocat, o, proj, proj, proj, proj, head_norm_w, ln_w, ln_b, w_sp, bs_t, *carry)
    return res[:6], res[6:]


def _out_proj_loss(ocat, w_out, x, target, fnw):
    T, MIX = ocat.shape
    DM = x.shape[1]
    tm = _tile(T, 256, 8)

    def body(oc_ref, w_ref, x_ref, t_ref, fw_ref, dh_ref, dhb_ref, doc_ref, loss_ref, gfw_ref):
        @pl.when(pl.program_id(0) == 0)
        def _():
            loss_ref[...] = jnp.zeros_like(loss_ref)
            gfw_ref[...] = jnp.zeros_like(gfw_ref)

        wv = w_ref[...]
        hh = x_ref[...] + jnp.dot(oc_ref[...].astype(MXU), wv.astype(MXU), preferred_element_type=F32)
        rs = lax.rsqrt(jnp.mean(hh * hh, axis=-1, keepdims=True) + EPS)
        hn = hh * rs
        fw = fw_ref[...]
        e = hn * fw - t_ref[...]
        row_loss = 0.5 * jnp.mean(e * e, axis=-1, keepdims=True)
        loss_ref[...] += jnp.sum(row_loss, axis=0, keepdims=True)
        dy = e * (1.0 / DM)
        gfw_ref[0:1, :] += jnp.sum(dy * hn, axis=0, keepdims=True)
        dhn = dy * fw
        dh = rs * (dhn - hn * jnp.mean(dhn * hn, axis=-1, keepdims=True))
        dh_ref[...] = dh
        dhb = dh.astype(BF16)
        dhb_ref[...] = dhb
        doc_ref[...] = _mm_nt(dhb, wv)

    return pl.pallas_call(
        body, name="out_proj_loss", grid=(T // tm,),
        in_specs=[pl.BlockSpec((tm, MIX), lambda i: (i, 0)),
                  pl.BlockSpec((MIX, DM), lambda i: (0, 0)),
                  pl.BlockSpec((tm, DM), lambda i: (i, 0)),
                  pl.BlockSpec((tm, DM), lambda i: (i, 0)),
                  pl.BlockSpec((1, DM), lambda i: (0, 0))],
        out_specs=[pl.BlockSpec((tm, DM), lambda i: (i, 0)),
                   pl.BlockSpec((tm, DM), lambda i: (i, 0)),
                   pl.BlockSpec((tm, MIX), lambda i: (i, 0)),
                   pl.BlockSpec((8, LANES), lambda i: (0, 0)),
                   pl.BlockSpec((8, DM), lambda i: (0, 0))],
        out_shape=[S((T, DM), F32), S((T, DM), BF16), S((T, MIX), F32), S((8, LANES), F32), S((8, DM), F32)],
        compiler_params=_cp(ARB),
    )(ocat, w_out, x, target, fnw)


def _grad_w(lhs, rhs, name):
    T, A = lhs.shape
    B = rhs.shape[1]
    ta = _tile(A, 512, LANES)
    tk = _tile(T, 1024, 16)
    nk = T // tk

    def body(l_ref, r_ref, out_ref, acc_ref):
        k = pl.program_id(1)
        part = _mm_tn(l_ref[...], r_ref[...])

        @pl.when(k == 0)
        def _():
            acc_ref[...] = part

        @pl.when(k > 0)
        def _():
            acc_ref[...] += part

        @pl.when(k == nk - 1)
        def _():
            out_ref[...] = acc_ref[...].astype(BF16)

    return pl.pallas_call(
        body, name=name, grid=(A // ta, nk),
        in_specs=[pl.BlockSpec((tk, ta), lambda i, k: (k, i)),
                  pl.BlockSpec((tk, B), lambda i, k: (k, 0))],
        out_specs=pl.BlockSpec((ta, B), lambda i, k: (i, 0)),
        out_shape=S((A, B), BF16),
        scratch_shapes=[pltpu.VMEM((ta, B), F32)],
        compiler_params=_cp(ARB, ARB),
    )(lhs, rhs)


def _grad_w_in(xn, dmain, dba, WD, gate_lo, gate_hi):
    T, DM = xn.shape
    NM = dmain.shape[1]
    tn = _tile(NM, 1024, LANES)
    tk = _tile(T, 2048, 16)
    nj, nk = NM // tn, T // tk
    ND = N_DEV
    tiles = [[] for _ in range(nj)]
    first_tile, last_tile = {}, {}
    for d, s0, s1, dest, c0 in _pieces(WD, gate_lo, gate_hi, ND * WD):
        if dest != "main":
            continue
        while s0 < s1:
            jj = c0 // tn
            w = min(s1 - s0, (jj + 1) * tn - c0)
            tiles[jj].append((d, s0, w, "main", c0 - jj * tn))
            first_tile.setdefault(d, jj)
            last_tile[d] = jj
            s0, c0 = s0 + w, c0 + w
    for d, s0, s1, dest, c0 in _pieces(WD, gate_lo, gate_hi, ND * WD):
        if dest == "gate":
            tiles[first_tile[d]].append((d, s0, s1 - s0, "gate", c0))
    assert sorted(first_tile) == list(range(ND)) and all(last_tile[d] <= first_tile[d + 2] for d in range(ND - 2))

    def body(xn_ref, dm_ref, dba_ref, keep_ref, recv_ref, acc_ref, gate_ref, buf_ref, lsem, ssem, rsem):
        j = pl.program_id(0)
        k = pl.program_id(1)
        px, py, pc = _position()

        @pl.when(k == 0)
        def _():
            acc_ref[...] = jnp.zeros_like(acc_ref)

        @pl.when((j == 0) & (k == 0))
        def _():
            gate_ref[...] = jnp.zeros_like(gate_ref)

        xv = xn_ref[...]
        acc_ref[...] += _mm_tn(xv, dm_ref[...])

        @pl.when(j == 0)
        def _():
            gate_ref[...] += _mm_tn(xv, dba_ref[...])

        def local(d):
            return pltpu.make_async_copy(buf_ref.at[d % 2], keep_ref.at[d // 2], lsem.at[d // 2])

        def remote(d):
            return pltpu.make_async_remote_copy(
                src_ref=buf_ref.at[d % 2], dst_ref=recv_ref.at[d // 2], send_sem=ssem.at[d // 2],
                recv_sem=rsem.at[d // 2], device_id=(px, py, 1 - pc), device_id_type=MESH)

        def leave(d, start):
            @pl.when(pc == d % 2)
            def _():
                local(d).start() if start else local(d).wait()

            @pl.when(pc != d % 2)
            def _():
                remote(d).start() if start else remote(d).wait_send()

        def emit(jj):
            shards = sorted({p[0] for p in tiles[jj]})
            for d in shards:
                if first_tile[d] == jj and d >= 2:
                    leave(d - 2, False)
                for dd, s0, w, src, c0 in tiles[jj]:
                    if dd == d:
                        ref = acc_ref if src == "main" else gate_ref
                        buf_ref[d % 2, :, s0:s0 + w] = ref[:, c0:c0 + w].astype(BF16)
                if last_tile[d] == jj:
                    leave(d, True)
            if jj == nj - 1:
                for d in (ND - 2, ND - 1):
                    leave(d, False)
                for q in range(ND // 2):
                    remote(2 * q).wait_recv()

        for jj in range(nj):
            @pl.when((j == jj) & (k == nk - 1))
            def _(jj=jj):
                emit(jj)

    any_spec = pl.BlockSpec(memory_space=pl.ANY)
    return pl.pallas_call(
        body, name="grad_w_in", grid=(nj, nk),
        in_specs=[pl.BlockSpec((tk, DM), lambda j, k: (k, 0)),
                  pl.BlockSpec((tk, tn), lambda j, k: (k, j)),
                  pl.BlockSpec((tk, LANES), lambda j, k: (k, 0))],
        out_specs=[any_spec, any_spec],
        out_shape=[S((ND // 2, DM, WD), BF16), S((ND // 2, DM, WD), BF16)],
        scratch_shapes=[pltpu.VMEM((DM, tn), F32), pltpu.VMEM((DM, LANES), F32), pltpu.VMEM((2, DM, WD), BF16),
                        pltpu.SemaphoreType.DMA((ND // 2,)), pltpu.SemaphoreType.DMA((ND // 2,)),
                        pltpu.SemaphoreType.DMA((ND // 2,))],
        compiler_params=_cp(ARB, ARB),
    )(xn, dmain, dba)


def _pair_sum_plain(a, b, name):
    K, R, C = a.shape
    tr = _tile(R, 256, 16)

    def body(a_ref, b_ref, o_ref):
        o_ref[...] = (a_ref[...].astype(F32) + b_ref[...].astype(F32)).astype(BF16)

    spec = lambda: pl.BlockSpec((1, tr, C), lambda q, i: (q, i, 0))
    return pl.pallas_call(body, name=name, grid=(K, R // tr), in_specs=[spec(), spec()], out_specs=spec(),
                          out_shape=S((K, R, C), BF16), compiler_params=_cp(ARB, ARB))(a, b)


def _dx_rows(T):
    tm = _tile(T, 512, 8)
    return tm if T // tm >= 2 else T // 2


def _dx_part(name, dmain, dba, w_main, w_ba, x, dh, norm_w, blk0, nblk, prev, hbm_in, hbm_alias, hbm_new, make_copies):
    T, NM = dmain.shape
    DM = x.shape[1]
    tm = _dx_rows(T)
    tk = _tile(NM, 1024, LANES)
    nk = NM // tk
    n_in, n_al, n_new = len(hbm_in), len(hbm_alias), len(hbm_new)
    n_prev = 0 if prev is None else 2
    last_step = nblk * nk - 1

    def body(dm_ref, dba_ref, w_ref, wba_ref, x_ref, dh_ref, nw_ref, *rest):
        r = list(rest)
        gnw_prev_ref = r.pop(0) if n_prev else None
        if n_prev:
            r.pop(0)
        in_refs = [r.pop(0) for _ in range(n_in)]
        del r[:n_al]
        gx_ref, gnw_ref = r.pop(0), r.pop(0)
        alias_refs = [r.pop(0) for _ in range(n_al)]
        new_refs = [r.pop(0) for _ in range(n_new)]
        acc_ref, send_sems, recv_sems = r
        i = pl.program_id(0)
        k = pl.program_id(1)
        step = i * nk + k
        cps = make_copies(in_refs, alias_refs, new_refs, send_sems, recv_sems)

        @pl.when(step == 0)
        def _():
            gnw_ref[...] = gnw_prev_ref[...] if n_prev else jnp.zeros_like(gnw_ref)
            for cp in cps:
                cp.start()

        @pl.when(k == 0)
        def _():
            acc_ref[...] = _mm_nt(dba_ref[...], wba_ref[...])

        acc_ref[...] += _mm_nt(dm_ref[...], w_ref[...])

        @pl.when(k == nk - 1)
        def _():
            xv = x_ref[...]
            rs = lax.rsqrt(jnp.mean(xv * xv, axis=-1, keepdims=True) + EPS)
            xh = xv * rs
            dxn = acc_ref[...]
            gnw_ref[0:1, :] += jnp.sum(dxn * xh, axis=0, keepdims=True)
            dxh = dxn * nw_ref[...]
            gx_ref[...] = dh_ref[...] + rs * (dxh - xh * jnp.mean(dxh * xh, axis=-1, keepdims=True))

        @pl.when(step == last_step)
        def _():
            for cp in cps:
                cp.wait()

    any_spec = pl.BlockSpec(memory_space=pl.ANY)
    prev_specs = [pl.BlockSpec((8, DM), lambda i, k: (0, 0)), any_spec] if n_prev else []
    prev_args = [prev[1], prev[0]] if n_prev else []
    aliases = {8: 0} if n_prev else {}
    for q in range(n_al):
        aliases[7 + n_prev + n_in + q] = 2 + q
    res = pl.pallas_call(
        body, name=name, grid=(nblk, nk),
        in_specs=[pl.BlockSpec((tm, tk), lambda i, k: (blk0 + i, k)),
                  pl.BlockSpec((tm, LANES), lambda i, k: (blk0 + i, 0)),
                  pl.BlockSpec((DM, tk), lambda i, k: (0, k)),
                  pl.BlockSpec((DM, LANES), lambda i, k: (0, 0)),
                  pl.BlockSpec((tm, DM), lambda i, k: (blk0 + i, 0)),
                  pl.BlockSpec((tm, DM), lambda i, k: (blk0 + i, 0)),
                  pl.BlockSpec((1, DM), lambda i, k: (0, 0))] + prev_specs + [any_spec] * (n_in + n_al),
        out_specs=[pl.BlockSpec((tm, DM), lambda i, k: (blk0 + i, 0)),
                   pl.BlockSpec((8, DM), lambda i, k: (0, 0))] + [any_spec] * (n_al + n_new),
        out_shape=[S((T, DM), F32), S((8, DM), F32)] + [S(a.shape, a.dtype) for a in hbm_alias] + list(hbm_new),
        scratch_shapes=[pltpu.VMEM((tm, DM), F32), pltpu.SemaphoreType.DMA((10,)), pltpu.SemaphoreType.DMA((10,))],
        input_output_aliases=aliases,
        compiler_params=_cp(ARB, ARB),
    )(dmain, dba, w_main, w_ba, x, dh, norm_w, *prev_args, *hbm_in, *hbm_alias)
    return (res[0], res[1]), res[2:2 + n_al], res[2 + n_al:]


def _remote(kk, src, dst, to, send_sems, recv_sems):
    return pltpu.make_async_remote_copy(src_ref=src, dst_ref=dst, send_sem=send_sems.at[kk], recv_sem=recv_sems.at[kk],
                                        device_id=to, device_id_type=MESH)


def _dx(dmain, dba, w_main, w_ba, x, dh, norm_w, chip_sum, small, cut):
    R, C = chip_sum.shape[1:]
    half = R // 2
    assert half % 16 == 0
    T = x.shape[0]
    ni = T // _dx_rows(T)
    cut = max(1, min(cut, ni - 1))
    upper, lower = pl.ds(0, half), pl.ds(half, half)

    def nbrs():
        px, py, pc = _position()
        return (px, py), (1 - px, py, pc), (px, 1 - py, pc)

    def phase1(ins, als, news, ss, rs):
        (px, py), xn, yn = nbrs()
        cs = ins[0]
        recv, stage = news
        bx, by, bd = cs.at[2 * (1 - px) + py], cs.at[2 * px + (1 - py)], cs.at[2 * (1 - px) + (1 - py)]
        return [_remote(0, bx.at[upper], recv.at[0].at[upper], xn, ss, rs),
                _remote(1, by.at[lower], recv.at[1].at[lower], yn, ss, rs),
                _remote(2, bd.at[upper], stage.at[0], xn, ss, rs),
                _remote(3, bd.at[lower], stage.at[1], yn, ss, rs)]

    def phase2(ins, als, news, ss, rs):
        (px, py), xn, yn = nbrs()
        comb, small_ref = ins
        recv, gath = als[0], news[0]
        me, small_cps = _broadcast_copies([small_ref], [gath], _Sem2(ss, 2), _Sem2(rs, 2))
        return ([_remote(0, comb.at[0], recv.at[1].at[upper], yn, ss, rs),
                 _remote(1, comb.at[1], recv.at[0].at[lower], xn, ss, rs)] + small_cps
                + [pltpu.make_async_copy(small_ref, gath.at[me], ss.at[9])])

    (gx, gnw), _, (recv, stage) = _dx_part(
        "dx_a", dmain, dba, w_main, w_ba, x, dh, norm_w, 0, cut, None, [chip_sum], [],
        [S((2, R, C), chip_sum.dtype), S((2, half, C), chip_sum.dtype)], phase1)
    comb = _relay_add(chip_sum, stage)
    (gx, gnw), (recv,), (gath,) = _dx_part(
        "dx_b", dmain, dba, w_main, w_ba, x, dh, norm_w, cut, ni - cut, (gx, gnw), [comb, small], [recv],
        [S((N_DEV,) + small.shape, F32)], phase2)
    return gx, gnw, gath, recv


class _Sem2:
    def __init__(self, sems, lo):
        self.sems, self.lo = sems, lo

    @property
    def at(self):
        outer = self

        class _At:
            def __getitem__(self, idx):
                a, k = idx
                return outer.sems.at[outer.lo + k]
        return _At()


def _relay_add(chip_sum, stage):
    _, R, C = chip_sum.shape
    half = R // 2
    tr = _tile(half, 256, 16)
    nt = half // tr
    px, py, _ = _position()
    idx = jnp.stack([2 * px + (1 - py), 2 * (1 - px) + py]).astype(jnp.int32)

    def body(idx_ref, p_ref, s_ref, o_ref):
        del idx_ref
        o_ref[0] = (p_ref[0].astype(F32) + s_ref[0].astype(F32)).astype(BF16)

    return pl.pallas_call(
        body, name="relay_add",
        grid_spec=pltpu.PrefetchScalarGridSpec(
            num_scalar_prefetch=1, grid=(2, nt),
            in_specs=[pl.BlockSpec((1, tr, C), lambda s, i, idx_ref: (idx_ref[s], s * nt + i, 0)),
                      pl.BlockSpec((1, tr, C), lambda s, i, idx_ref: (s, i, 0))],
            out_specs=pl.BlockSpec((1, tr, C), lambda s, i, idx_ref: (s, i, 0))),
        out_shape=S((2, half, C), BF16), compiler_params=_cp(ARB, ARB),
    )(idx, chip_sum, stage)


def _sum_slots(gath):
    _, R, C = gath.shape
    tr = R if R <= 2048 else _tile(R, 512, 8)

    def body(g_ref, o_ref):
        tot = g_ref[0]
        for d in range(1, N_DEV):
            tot = tot + g_ref[d]
        o_ref[...] = tot

    return pl.pallas_call(
        body, name="sum_slots", grid=(R // tr,),
        in_specs=[pl.BlockSpec((N_DEV, tr, C), lambda i: (0, i, 0))],
        out_specs=pl.BlockSpec((tr, C), lambda i: (i, 0)),
        out_shape=S((R, C), F32), compiler_params=_cp(ARB),
    )(gath)


def _prep_a_bwd(dq, dk, dv, c, proj, conv_w, dmain, H, D):
    T = c.shape[0]
    AW = H * D
    C3 = 3 * AW
    tb = _tile(T, 256, 8)
    nblk = T // tb
    r8 = tb // 8
    scale = float(D) ** -0.5

    def body(dq_ref, dk_ref, dv_ref, c_ref, dqn_ref, dkn_ref, dvn_ref, cn_ref, x_ref, halo_ref, cw_ref, dmain_in_ref,
             dx_ref, gcw_ref, dc_ref):
        del dmain_in_ref
        i = pl.program_id(0)

        @pl.when(i == 0)
        def _():
            gcw_ref[...] = jnp.zeros_like(gcw_ref)

        def pointwise(rows, dq_r, dk_r, dv_r, c_r, keep):
            for h in range(H):
                for part, d_r, sc in ((0, dq_r, scale), (1, dk_r, 1.0)):
                    sl = slice(part * AW + h * D, part * AW + (h + 1) * D)
                    cv = c_r[:, sl]
                    raw = _silu(cv)
                    rs = lax.rsqrt(jnp.sum(raw * raw, axis=-1, keepdims=True) + EPS)
                    nrm = raw * rs
                    dn = d_r[:, h * D:(h + 1) * D] * sc
                    draw = rs * (dn - nrm * jnp.sum(dn * nrm, axis=-1, keepdims=True))
                    dc_ref[rows, sl] = draw * _dsilu(cv) * keep
            dc_ref[rows, 2 * AW:] = dv_r[...] * _dsilu(c_r[:, 2 * AW:]) * keep

        pointwise(slice(0, tb), dq_ref, dk_ref, dv_ref, c_ref, 1.0)
        pointwise(slice(tb, tb + 8), dqn_ref, dkn_ref, dvn_ref, cn_ref, (i < nblk - 1).astype(F32))

        cw = cw_ref[...]
        dcv = dc_ref[0:tb, :]
        dx = cw[3:4, :] * dcv
        for j in range(3):
            dx = dx + cw[j:j + 1, :] * dc_ref[3 - j:3 - j + tb, :]
        dx_ref[...] = dx.astype(BF16)
        halo = halo_ref[...] * (i > 0).astype(F32)
        xp = jnp.concatenate([halo, x_ref[...]], axis=0)
        for j in range(4):
            gcw_ref[j:j + 1, :] += jnp.sum(dcv * xp[5 + j:5 + j + tb], axis=0, keepdims=True)

    nxt = lambda i: (jnp.minimum((i + 1) * r8, T // 8 - 1), 0)
    return pl.pallas_call(
        body, name="prep_a_bwd", grid=(nblk,),
        in_specs=[pl.BlockSpec((tb, AW), lambda i: (i, 0)),
                  pl.BlockSpec((tb, AW), lambda i: (i, 0)),
                  pl.BlockSpec((tb, AW), lambda i: (i, 0)),
                  pl.BlockSpec((tb, C3), lambda i: (i, 0)),
                  pl.BlockSpec((8, AW), nxt), pl.BlockSpec((8, AW), nxt), pl.BlockSpec((8, AW), nxt),
                  pl.BlockSpec((8, C3), nxt),
                  pl.BlockSpec((tb, C3), lambda i: (i, 0)),
                  pl.BlockSpec((8, C3), lambda i: (jnp.maximum(i * r8 - 1, 0), 0)),
                  pl.BlockSpec((4, C3), lambda i: (0, 0)),
                  pl.BlockSpec(memory_space=pl.ANY)],
        out_specs=[pl.BlockSpec((tb, C3), lambda i: (i, 0)),
                   pl.BlockSpec((8, C3), lambda i: (0, 0))],
        out_shape=[S(dmain.shape, dmain.dtype), S((8, C3), F32)],
        scratch_shapes=[pltpu.VMEM((tb + 8, C3), F32)],
        input_output_aliases={11: 0},
        compiler_params=_cp(ARB),
    )(dq, dk, dv, c, dq, dk, dv, c, proj, proj, conv_w, dmain)


def _adam_math(w, g, m, v):
    m2 = ADAM_B1 * m + (1.0 - ADAM_B1) * g
    v2 = ADAM_B2 * v + (1.0 - ADAM_B2) * (g * g)
    m_hat = m2 / (1.0 - ADAM_B1 ** ADAM_STEP)
    v_hat = v2 / (1.0 - ADAM_B2 ** ADAM_STEP)
    delta = -ADAM_LR * (m_hat / (jnp.sqrt(v_hat) + ADAM_EPS) + ADAM_WD * w)
    return delta, m2, v2


def _pair_sum(blocks, recv, core, name):
    K, _, R, C = blocks.shape
    tr = _tile(R, 256, 16)

    def body(core_ref, a_ref, b_ref, o_ref):
        del core_ref
        o_ref[0] = (a_ref[0, 0].astype(F32) + b_ref[0].astype(F32)).astype(BF16)

    spec = lambda: pl.BlockSpec((1, tr, C), lambda k, i, core_ref: (k, i, 0))
    return pl.pallas_call(
        body, name=name,
        grid_spec=pltpu.PrefetchScalarGridSpec(
            num_scalar_prefetch=1, grid=(K, R // tr),
            in_specs=[pl.BlockSpec((1, 1, tr, C), lambda k, i, core_ref: (k, core_ref[0], i, 0)), spec()],
            out_specs=spec()),
        out_shape=S((K, R, C), BF16), compiler_params=_cp(ARB, ARB),
    )(core, blocks, recv)


def _sum_adam(chip_sums, recv, w, m, v, chip, name, transposed=False):
    R, C = chip_sums.shape[1:]
    NR = recv.shape[0]
    tr = _tile(R, 256, 16)

    def body(chip_ref, own_ref, r_ref, w_ref, m_ref, v_ref, g_ref, d_ref, m2_ref, v2_ref):
        del chip_ref
        g = own_ref[0].astype(F32)
        for j in range(NR):
            g = g + r_ref[j].astype(F32)
        if transposed:
            g = g.T
        g_ref[...] = g
        d_ref[...], m2_ref[...], v2_ref[...] = _adam_math(w_ref[...], g, m_ref[...], v_ref[...])

    if transposed:
        spec = lambda: pl.BlockSpec((C, tr), lambda i, chip_ref: (0, i))
        shape = (C, R)
    else:
        spec = lambda: pl.BlockSpec((tr, C), lambda i, chip_ref: (i, 0))
        shape = (R, C)
    assert w.shape == shape
    return pl.pallas_call(
        body, name=name,
        grid_spec=pltpu.PrefetchScalarGridSpec(
            num_scalar_prefetch=1, grid=(R // tr,),
            in_specs=[pl.BlockSpec((1, tr, C), lambda i, chip_ref: (chip_ref[0], i, 0)),
                      pl.BlockSpec((NR, tr, C), lambda i, chip_ref: (0, i, 0)), spec(), spec(), spec()],
            out_specs=[spec(), spec(), spec(), spec()]),
        out_shape=[S(shape, F32)] * 4, compiler_params=_cp(ARB),
    )(chip, chip_sums, recv, w, m, v)


def _adam_small(w, g, m, v):
    R, C = w.shape
    tr = _tile(R, 512, 8)

    def body(w_ref, g_ref, m_ref, v_ref, d_ref, m2_ref, v2_ref):
        d_ref[...], m2_ref[...], v2_ref[...] = _adam_math(w_ref[...], g_ref[...], m_ref[...], v_ref[...])

    spec = lambda: pl.BlockSpec((tr, C), lambda i: (i, 0))
    return pl.pallas_call(
        body, name="adam_small", grid=(R // tr,), in_specs=[spec()] * 4, out_specs=[spec()] * 3,
        out_shape=[S((R, C), F32)] * 3, compiler_params=_cp(ARB),
    )(w, g, m, v)


def _position():
    return lax.axis_index("x"), lax.axis_index("y"), lax.axis_index("c")


def _all_gather_weights(arr, x_in, norm_w, chip, tn, plans, nm):
    R = arr.shape[0]
    half = R // 2
    assert half % 16 == 0
    T, DM = x_in.shape
    tm = _tile(T, 512, 16)
    nstep = T // tm

    def body(chip_ref, x_ref, nw_ref, in_ref, xn_ref, out_ref, proj_ref, wtile_ref, stage_ref,
             send_sems, recv_sems, local_sem, stage_sems):
        i = pl.program_id(0)
        x, y, c = _position()
        me, sibling = (x, y, c), (x, y, 1 - c)
        xn, yn, diag = (1 - x, y), (x, 1 - y), (1 - x, 1 - y)
        upper, lower = pl.ds(0, half), pl.ds(half, half)

        def slot(p, rows=None):
            ref = out_ref.at[4 * p[0] + 2 * p[1] + p[2]]
            return ref if rows is None else ref.at[rows]

        def copy(kk, block, to, rows=None, src=None):
            return pltpu.make_async_remote_copy(
                src_ref=slot(block, rows) if src is None else src, dst_ref=slot(block, rows),
                send_sem=send_sems.at[kk], recv_sem=recv_sems.at[kk], device_id=to, device_id_type=MESH)

        mine = pltpu.make_async_copy(in_ref, slot(me), local_sem)
        first = [copy(0, me, sibling, src=in_ref), copy(1, me, (*xn, c), src=in_ref), copy(2, me, (*yn, c), src=in_ref)]

        @pl.when(i == 0)
        def _():
            mine.start()
            for cp in first:
                cp.start()
            copy(0, sibling, me).wait_recv()
            loads = [pltpu.make_async_copy(in_ref, stage_ref.at[c], stage_sems.at[0]),
                     pltpu.make_async_copy(slot(sibling), stage_ref.at[1 - c], stage_sems.at[1])]
            for cp in loads:
                cp.start()
            for cp in loads:
                cp.wait()
            for m, plan in enumerate(plans):
                @pl.when(chip_ref[0] == m)
                def _(plan=plan):
                    for d, s0, w, c0 in plan:
                        wtile_ref[:, c0:c0 + w] = stage_ref[d % 2, :, s0:s0 + w]

        xv = x_ref[...]
        r = lax.rsqrt(jnp.mean(xv * xv, axis=-1, keepdims=True) + EPS)
        xnv = (xv * r * nw_ref[...]).astype(BF16)
        xn_ref[...] = xnv
        proj_ref[...] = jnp.dot(xnv.astype(MXU), wtile_ref[...].astype(MXU), preferred_element_type=F32)

        @pl.when(i == nstep - 1)
        def _():
            sent = list(first)

            def then(cps):
                for cp in cps:
                    cp.start()
                sent.extend(cps)

            copy(1, (*xn, c), me).wait_recv()
            then([copy(5, (*xn, c), (*yn, c), rows=upper), copy(3, (*xn, c), sibling)])
            copy(2, (*yn, c), me).wait_recv()
            then([copy(6, (*yn, c), (*xn, c), rows=lower), copy(4, (*yn, c), sibling)])
            copy(5, (*diag, c), me, rows=upper).wait_recv()
            then([copy(7, (*diag, c), sibling, rows=upper)])
            copy(6, (*diag, c), me, rows=lower).wait_recv()
            then([copy(8, (*diag, c), sibling, rows=lower)])
            copy(3, (*xn, 1 - c), me).wait_recv()
            copy(4, (*yn, 1 - c), me).wait_recv()
            copy(7, (*diag, 1 - c), me, rows=upper).wait_recv()
            copy(8, (*diag, 1 - c), me, rows=lower).wait_recv()
            for cp in sent:
                cp.wait_send()
            mine.wait()

    any_spec = pl.BlockSpec(memory_space=pl.ANY)
    return pl.pallas_call(
        body, name="all_gather_weights",
        grid_spec=pltpu.PrefetchScalarGridSpec(
            num_scalar_prefetch=1, grid=(nstep,),
            in_specs=[pl.BlockSpec((tm, DM), lambda i, chip_ref: (i, 0)),
                      pl.BlockSpec((1, DM), lambda i, chip_ref: (0, 0)), any_spec],
            out_specs=[pl.BlockSpec((tm, DM), lambda i, chip_ref: (i, 0)), any_spec,
                       pl.BlockSpec((tm, tn), lambda i, chip_ref: (i, 2 * chip_ref[0]))],
            scratch_shapes=[pltpu.VMEM((DM, tn), arr.dtype), pltpu.VMEM((2,) + arr.shape, arr.dtype),
                            pltpu.SemaphoreType.DMA((9,)), pltpu.SemaphoreType.DMA((9,)), pltpu.SemaphoreType.DMA,
                            pltpu.SemaphoreType.DMA((2,))]),
        out_shape=[S((T, DM), BF16), S((N_DEV,) + arr.shape, arr.dtype), S((T, nm), F32)],
        compiler_params=_cp(ARB),
    )(chip, x_in, norm_w, arr)


def _sibling_copies(ins, outs, send_sems, recv_sems):
    x, y, c = _position()
    return [pltpu.make_async_remote_copy(src_ref=ins[a].at[k, 1 - c], dst_ref=outs[a].at[k],
                                         send_sem=send_sems.at[a, k], recv_sem=recv_sems.at[a, k],
                                         device_id=(x, y, 1 - c), device_id_type=MESH)
            for a in range(len(ins)) for k in range(ins[a].shape[0])]


def _sibling_sems(arrs):
    shape = (max(len(arrs), 1), arrs[0].shape[0] if arrs else 1)
    return [pltpu.SemaphoreType.DMA(shape), pltpu.SemaphoreType.DMA(shape)]


def _chip_exchange_copies(ins, outs, send_sems, recv_sems):
    x, y, c = _position()
    chips = [(1 - x, y), (x, 1 - y), (1 - x, 1 - y)]
    return [pltpu.make_async_remote_copy(
        src_ref=ins[a].at[2 * qx + qy], dst_ref=outs[a].at[j], send_sem=send_sems.at[a, j],
        recv_sem=recv_sems.at[a, j], device_id=(qx, qy, c), device_id_type=MESH)
        for a in range(len(ins)) for j, (qx, qy) in enumerate(chips)]


def _broadcast_copies(srcs, dsts, send_sems, recv_sems):
    x, y, c = _position()
    me = 4 * x + 2 * y + c
    cps = []
    for a in range(len(srcs)):
        for k in range(1, N_DEV):
            peer = (1 - x if k & 4 else x, 1 - y if k & 2 else y, 1 - c if k & 1 else c)
            cps.append(pltpu.make_async_remote_copy(
                src_ref=srcs[a], dst_ref=dsts[a].at[me], send_sem=send_sems.at[a, k - 1],
                recv_sem=recv_sems.at[a, k - 1], device_id=peer, device_id_type=MESH))
    return me, cps


def _all_reduce_small(part):
    R, C = part.shape

    def body(p_ref, out_ref, gath_ref, send_sems, recv_sems):
        me, cps = _broadcast_copies([p_ref], [gath_ref], send_sems, recv_sems)
        gath_ref[me] = p_ref[...]
        for cp in cps:
            cp.start()
        for cp in cps:
            cp.wait()
        acc = gath_ref[0]
        for d in range(1, N_DEV):
            acc = acc + gath_ref[d]
        out_ref[...] = acc

    vm = pl.BlockSpec(memory_space=pltpu.VMEM)
    return pl.pallas_call(
        body, name="all_reduce_small", in_specs=[vm], out_specs=vm, out_shape=S((R, C), F32),
        scratch_shapes=[pltpu.VMEM((N_DEV, R, C), F32), pltpu.SemaphoreType.DMA((1, N_DEV - 1)),
                        pltpu.SemaphoreType.DMA((1, N_DEV - 1))],
    )(part)


def _pack(parts):
    rows = []
    for p in parts:
        f = p.reshape(-1).astype(F32)
        pad = (-f.shape[0]) % (8 * LANES)
        rows.append(jnp.pad(f, (0, pad)).reshape(-1, LANES))
    return jnp.concatenate(rows, axis=0)


def _unpack(buf, shapes):
    out, r = [], 0
    for shp in shapes:
        n = 1
        for s in shp:
            n *= s
        nr = -(-n // (8 * LANES)) * 8
        out.append(buf[r:r + nr].reshape(-1)[:n].reshape(shp))
        r += nr
    return out


def kernel(x, norm_w, w_in, conv_w, a_log, dt_bias, head_norm_w, sgu_ln_w, sgu_ln_b, w_spatial, b_spatial, w_out, final_norm_w, loss_target, m_norm_w, m_w_in, m_conv_w, m_a_log, m_dt_bias, m_head_norm_w, m_sgu_ln_w, m_sgu_ln_b, m_w_spatial, m_b_spatial, m_w_out, m_final_norm_w, v_norm_w, v_w_in, v_conv_w, v_a_log, v_dt_bias, v_head_norm_w, v_sgu_ln_w, v_sgu_ln_b, v_w_spatial, v_b_spatial, v_w_out, v_final_norm_w):
    T, DM = x.shape[1], x.shape[2]
    H, D = a_log.shape[1], head_norm_w.shape[1]
    G, P = w_spatial.shape[1], w_spatial.shape[2]
    AW, BW = H * D, G * P
    MIX = AW + BW
    WD = w_in.shape[2]
    IN = N_DEV * WD
    RO = w_out.shape[1]
    CW = conv_w.shape[2]
    sizes = (3 * AW, AW, H, H, BW, BW, BW)
    assert sum(sizes) == IN and 2 * H <= LANES and 3 * H <= 32 and N_DEV * RO == MIX and N_DEV * CW == 3 * AW
    offs = [0]
    for s in sizes:
        offs.append(offs[-1] + s)
    px, py, pc = _position()
    dev = 4 * px + 2 * py + pc
    chip = 2 * px + py

    x2, tgt = x[0], loss_target[0]

    core_idx = jnp.reshape(pc, (1,)).astype(jnp.int32)
    chip_idx = jnp.reshape(chip, (1,)).astype(jnp.int32)
    NM = IN - 2 * H
    tn_loc, tile_plans = _local_tiles(WD, offs[2], offs[4], NM)
    xn, g_win, proj_part = _all_gather_weights(
        _cast_bf16_t(w_in[0].T, "cast_w_in"), x2, norm_w, chip_idx, tn_loc, tile_plans, NM)
    w_main, w_ba = _relayout_w(g_win, offs[2], offs[4])
    alog_row = jnp.pad(a_log, ((0, 0), (H, LANES - 2 * H)))
    dtb_row = jnp.pad(dt_bias, ((0, 0), (H, LANES - 2 * H)))
    bs_t = b_spatial[0].T

    others = jnp.arange(N_DEV - 2, dtype=jnp.int32)
    others = others + (others >= 2 * chip).astype(jnp.int32)
    proj, ba, (g_wout, g_conv) = _in_proj(xn, w_main, w_ba, proj_part, others, tn_loc,
                                          [_cast_bf16(w_out[0], "cast_w_out"), conv_w[0]])
    w_out_full = g_wout.reshape(MIX, DM)
    conv_full = g_conv.transpose(1, 0, 2).reshape(4, 3 * AW)
    q, k, v, c, gcol, grow = _prep_a_fwd(proj, ba, conv_full, alog_row, dtb_row, H, D)
    o, vnew, ssave, asave = _delta_fwd(q, k, v, gcol, grow, H, D)
    ocat = _mix_fwd(o, proj, head_norm_w, sgu_ln_w, sgu_ln_b, w_spatial[0], bs_t, H, D, G, P)
    dh, dh_bf, d_ocat, loss_acc, g_fnw = _out_proj_loss(ocat, w_out_full, x2, tgt, final_norm_w.reshape(1, DM))

    g_wout_blocks = _grad_w(ocat, dh_bf, "grad_w_out").reshape(4, 2, RO, DM)
    (d_o, dmain, g_hnw, g_ln, g_wsp, g_bs_t), (sib_wout,) = _mix_bwd(
        d_ocat, o, proj, head_norm_w, sgu_ln_w, sgu_ln_b, w_spatial[0], bs_t, H, D, G, P, [g_wout_blocks])
    chip_wout = _pair_sum(g_wout_blocks, sib_wout, core_idx, "pair_sum_w_out")
    (dq, dk, dv, dgate, dpar), (recv_wout,) = _delta_bwd(
        q, k, v, gcol, grow, ba, vnew, ssave, asave, d_o, alog_row, dtb_row, H, D, [chip_wout])
    dmain, g_conv_part = _prep_a_bwd(dq, dk, dv, c, proj, conv_full, dmain, H, D)
    dba = dgate.astype(BF16)
    keep_win, sib_win = _grad_w_in(xn, dmain, dba, WD, offs[2], offs[4])
    chip_win = _pair_sum_plain(keep_win, sib_win, "pair_sum_w_in")
    small_shapes = [a_log.shape, dt_bias.shape, head_norm_w.shape, sgu_ln_w.shape, sgu_ln_b.shape,
                    w_spatial.shape, b_spatial.shape, final_norm_w.shape]
    parts = [dpar[0, H:2 * H], dpar[1, H:2 * H], g_hnw[0], g_ln[0], g_ln[1], g_wsp, g_bs_t[:, :G].T, g_fnw[0],
             g_conv_part[:4], loss_acc[0, :1]]
    grad_x, g_nw, small_gath, recv_win = _dx(dmain, dba, w_main, w_ba, x2, dh, norm_w, chip_win, _pack(parts), 4)
    red = _sum_slots(small_gath)
    grad_w_in, delta_w_in, new_m_w_in, new_v_w_in = _sum_adam(
        chip_win, recv_win, w_in[0].T, m_w_in[0].T, v_w_in[0].T, chip_idx, "sum_adam_w_in", transposed=True)
    grad_w_out, delta_w_out, new_m_w_out, new_v_w_out = _sum_adam(
        chip_wout, recv_wout, w_out[0], m_w_out[0], v_w_out[0], chip_idx, "sum_adam_w_out")
    red_nw = _all_reduce_small(_pack([g_nw[0]]))
    grads_small = _unpack(red_nw, [norm_w.shape]) + _unpack(red, small_shapes + [(4, 3 * AW), (1,)])
    loss = grads_small.pop()[0]
    g_conv_full = grads_small.pop()
    grad_conv = lax.dynamic_slice_in_dim(g_conv_full, dev * CW, CW, axis=1)[None]
    small_w = [norm_w, a_log, dt_bias, head_norm_w, sgu_ln_w, sgu_ln_b, w_spatial, b_spatial, final_norm_w, conv_w]
    small_m = [m_norm_w, m_a_log, m_dt_bias, m_head_norm_w, m_sgu_ln_w, m_sgu_ln_b, m_w_spatial, m_b_spatial,
               m_final_norm_w, m_conv_w]
    small_v = [v_norm_w, v_a_log, v_dt_bias, v_head_norm_w, v_sgu_ln_w, v_sgu_ln_b, v_w_spatial, v_b_spatial,
               v_final_norm_w, v_conv_w]
    small_g = grads_small + [grad_conv]
    shapes10 = [w.shape for w in small_w]
    d_p, m_p, v_p = _adam_small(_pack(small_w), _pack(small_g), _pack(small_m), _pack(small_v))
    d_s, m_s, v_s = _unpack(d_p, shapes10), _unpack(m_p, shapes10), _unpack(v_p, shapes10)

    def order(small, win, wout):
        return [small[0], win.T[None], small[9], small[1], small[2], small[3], small[4], small[5], small[6], small[7],
                wout[None], small[8]]

    grads = order(small_g, grad_w_in, grad_w_out)
    deltas = order(d_s, delta_w_in, delta_w_out)
    new_m = order(m_s, new_m_w_in, new_m_w_out)
    new_v = order(v_s, new_v_w_in, new_v_w_out)
    return (loss, grad_x[None], *grads, *deltas, *new_m, *new_v)
```

```python
import jax
import jax.numpy as jnp
from jax import lax
from jax.experimental import pallas as pl
from jax.experimental.pallas import tpu as pltpu

F32 = jnp.float32
BF16 = jnp.bfloat16
MXU = jnp.bfloat16
HI = lax.Precision.HIGHEST
EPS = 1e-6
CHUNK_A = 64
LANES = 128
MESH = pl.DeviceIdType.MESH
N_DEV = 8

ADAM_LR = 0.001
ADAM_B1 = 0.9
ADAM_B2 = 0.999
ADAM_EPS = 1e-08
ADAM_WD = 0.01
ADAM_STEP = 10

S = jax.ShapeDtypeStruct
ARB = "arbitrary"


def _cp(*sem, vmem_mib=56):
    return pltpu.CompilerParams(dimension_semantics=tuple(sem), vmem_limit_bytes=vmem_mib * 1024 * 1024)


def _tile(n, cap, mult):
    best = None
    t = mult
    while t <= min(n, cap):
        if n % t == 0:
            best = t
        t += mult
    return best if best is not None else n


def _mm(a, b):
    return jnp.dot(a.astype(MXU), b.astype(MXU), preferred_element_type=F32)


def _mm_nt(a, b):
    return lax.dot_general(a.astype(MXU), b.astype(MXU), (((1,), (1,)), ((), ())), preferred_element_type=F32)


def _mm_tn(a, b):
    return lax.dot_general(a.astype(MXU), b.astype(MXU), (((0,), (0,)), ((), ())), preferred_element_type=F32)


def _mmh(a, b):
    return jnp.dot(a, b, precision=HI, preferred_element_type=F32)


def _sigmoid(x):
    return 1.0 / (1.0 + jnp.exp(-x))


def _silu(x):
    return x * _sigmoid(x)


def _dsilu(x):
    s = _sigmoid(x)
    return s * (1.0 + x * (1.0 - s))


def _softplus(x):
    return jnp.maximum(x, 0.0) + jnp.log(1.0 + jnp.exp(-jnp.abs(x)))


def _pieces(wd, gate_lo, gate_hi, total):
    out = []
    for d in range(N_DEV):
        lo, hi = d * wd, (d + 1) * wd
        for dest, a, b, shift in (("main", 0, gate_lo, 0), ("gate", gate_lo, gate_hi, -gate_lo),
                                  ("main", gate_hi, total, gate_lo - gate_hi)):
            s0, s1 = max(lo, a), min(hi, b)
            if s0 < s1:
                out.append((d, s0 - lo, s1 - lo, dest, s0 + shift))
    return out


def _tile_plans(wd, gate_lo, gate_hi, nm):
    n_tiles = N_DEV - 1
    assert nm % (n_tiles * LANES) == 0
    tn = nm // n_tiles
    plans = [[] for _ in range(n_tiles)]
    for d, s0, s1, dest, c0 in _pieces(wd, gate_lo, gate_hi, N_DEV * wd):
        if dest != "main":
            continue
        for t in range(n_tiles):
            a, b = max(c0, t * tn), min(c0 + (s1 - s0), (t + 1) * tn)
            if a < b:
                plans[t].append((d, s0 + (a - c0), b - a, a - t * tn))
    assert all(sum(p[2] for p in plan) == tn for plan in plans)
    near = []
    for m in range(N_DEV // 2):
        assert all(p[0] // 2 == m for p in plans[2 * m]), "tile 2m must come from chip m's own shards"
        chips = (m, m ^ 1, m ^ 2)
        near.append([2 * m] + [t for t in range(n_tiles)
                               if t != 2 * m and all(p[0] // 2 in chips for p in plans[t])])
    count = min(len(r) for r in near)
    return tn, plans, [r[:count] for r in near]


def _cast_bf16(a, name):
    R, C = a.shape
    tr = _tile(R, 256, 16)

    def body(a_ref, o_ref):
        o_ref[...] = a_ref[...].astype(BF16)

    spec = pl.BlockSpec((tr, C), lambda i: (i, 0))
    return pl.pallas_call(body, name=name, grid=(R // tr,), in_specs=[spec], out_specs=spec,
                          out_shape=S((R, C), BF16), compiler_params=_cp(ARB))(a)


def _cast_bf16_t(a_t, name):
    C, R = a_t.shape
    tr = _tile(R, 256, LANES)

    def body(a_ref, o_ref):
        o_ref[...] = a_ref[...].T.astype(BF16)

    return pl.pallas_call(body, name=name, grid=(R // tr,), in_specs=[pl.BlockSpec((C, tr), lambda i: (0, i))],
                          out_specs=pl.BlockSpec((tr, C), lambda i: (i, 0)),
                          out_shape=S((R, C), BF16), compiler_params=_cp(ARB))(a_t)


def _relayout_w(g_win, gate_lo, gate_hi):
    _, DM, WD = g_win.shape
    total = N_DEV * WD
    NM = total - (gate_hi - gate_lo)
    tr = _tile(DM, 256, 16)
    plan = _pieces(WD, gate_lo, gate_hi, total)

    def body(g_ref, main_ref, gate_ref):
        gate_ref[...] = jnp.zeros_like(gate_ref)
        for d, s0, s1, dest, c0 in plan:
            dst = main_ref if dest == "main" else gate_ref
            dst[:, c0:c0 + (s1 - s0)] = g_ref[d, :, s0:s1]

    return pl.pallas_call(
        body, name="relayout_w", grid=(DM // tr,),
        in_specs=[pl.BlockSpec((N_DEV, tr, WD), lambda i: (0, i, 0))],
        out_specs=[pl.BlockSpec((tr, NM), lambda i: (i, 0)), pl.BlockSpec((tr, LANES), lambda i: (i, 0))],
        out_shape=[S((DM, NM), g_win.dtype), S((DM, LANES), g_win.dtype)],
        compiler_params=_cp(ARB),
    )(g_win)


def _in_proj(xn, w_main, w_ba, proj_part, tiles, tn, shards):
    T, DM = xn.shape
    NM = w_main.shape[1]
    tm = _tile(T, 2048, 16)
    ni, nj = T // tm, tiles.shape[0]
    ns = len(shards)

    def body(tiles_ref, xn_ref, w_ref, wba_ref, part_ref, *rest):
        del tiles_ref, part_ref
        srcs = rest[:ns]
        proj_ref, ba_ref = rest[ns:ns + 2]
        gath = rest[ns + 2:2 * ns + 2]
        send_sems, recv_sems, local_sems = rest[2 * ns + 2:]
        i = pl.program_id(0)
        j = pl.program_id(1)
        me, cps = _broadcast_copies(srcs, gath, send_sems, recv_sems)
        cps = cps + [pltpu.make_async_copy(srcs[a], gath[a].at[me], local_sems.at[a]) for a in range(ns)]

        @pl.when((i == 0) & (j == 0))
        def _():
            for cp in cps:
                cp.start()

        @pl.when(j == 0)
        def _():
            ba_ref[...] = jnp.dot(xn_ref[...].astype(MXU), wba_ref[...].astype(MXU), preferred_element_type=F32)

        proj_ref[...] = jnp.dot(xn_ref[...].astype(MXU), w_ref[...].astype(MXU), preferred_element_type=F32)

        @pl.when((i == ni - 1) & (j == nj - 1))
        def _():
            for cp in cps:
                cp.wait()

    any_spec = pl.BlockSpec(memory_space=pl.ANY)
    res = pl.pallas_call(
        body, name="in_proj",
        grid_spec=pltpu.PrefetchScalarGridSpec(
            num_scalar_prefetch=1, grid=(ni, nj),
            in_specs=[pl.BlockSpec((tm, DM), lambda i, j, t: (i, 0)),
                      pl.BlockSpec((DM, tn), lambda i, j, t: (0, t[j])),
                      pl.BlockSpec((DM, LANES), lambda i, j, t: (0, 0)), any_spec] + [any_spec] * ns,
            out_specs=[pl.BlockSpec((tm, tn), lambda i, j, t: (i, t[j])),
                       pl.BlockSpec((tm, LANES), lambda i, j, t: (i, 0))] + [any_spec] * ns,
            scratch_shapes=[pltpu.SemaphoreType.DMA((ns, N_DEV - 1)), pltpu.SemaphoreType.DMA((ns, N_DEV - 1)),
                            pltpu.SemaphoreType.DMA((ns,))]),
        out_shape=[S((T, NM), F32), S((T, LANES), F32)] + [S((N_DEV,) + a.shape, a.dtype) for a in shards],
        input_output_aliases={4: 0},
        compiler_params=_cp(ARB, ARB, vmem_mib=58),
    )(tiles, xn, w_main, w_ba, proj_part, *shards)
    return res[0], res[1], res[2:]


def _prep_a_fwd(proj, ba, conv_w, alog_row, dtb_row, H, D):
    T = proj.shape[0]
    AW = H * D
    C3 = 3 * AW
    tb = _tile(T, 256, CHUNK_A)
    nch = tb // CHUNK_A
    nblk = T // tb
    scale = float(D) ** -0.5

    def body(x_ref, halo_ref, ba_ref, cw_ref, al_ref, dt_ref, q_ref, k_ref, v_ref, c_ref, gcol_ref, grow_ref):
        i = pl.program_id(0)
        xv = x_ref[...]
        halo = halo_ref[...] * (i > 0).astype(F32)
        xp = jnp.concatenate([halo, xv], axis=0)
        cw = cw_ref[...]
        c = cw[0:1, :] * xp[5:5 + tb]
        for j in range(1, 4):
            c = c + cw[j:j + 1, :] * xp[5 + j:5 + j + tb]
        c_ref[...] = c
        a = _silu(c)
        for h in range(H):
            qh = a[:, h * D:(h + 1) * D]
            kh = a[:, AW + h * D:AW + (h + 1) * D]
            qr = lax.rsqrt(jnp.sum(qh * qh, axis=-1, keepdims=True) + EPS)
            kr = lax.rsqrt(jnp.sum(kh * kh, axis=-1, keepdims=True) + EPS)
            q_ref[:, h * D:(h + 1) * D] = qh * (qr * scale)
            k_ref[:, h * D:(h + 1) * D] = kh * kr
        v_ref[...] = a[:, 2 * AW:]

        bav = ba_ref[...]
        lane = lax.broadcasted_iota(jnp.int32, (tb, LANES), 1)
        beta = _sigmoid(bav)
        g = -jnp.exp(al_ref[...]) * _softplus(bav + dt_ref[...])
        gates = jnp.where(lane < H, beta, jnp.where(lane < 2 * H, g, 0.0))
        ri = lax.broadcasted_iota(jnp.int32, (CHUNK_A, CHUNK_A), 0)
        ci = lax.broadcasted_iota(jnp.int32, (CHUNK_A, CHUNK_A), 1)
        tri = (ri >= ci).astype(F32)
        lane_c = lax.broadcasted_iota(jnp.int32, (CHUNK_A, LANES), 1)
        for cc in range(nch):
            gch = gates[cc * CHUNK_A:(cc + 1) * CHUNK_A]
            gc = pltpu.roll(_mmh(tri, gch), H, 1)
            full = jnp.where(lane_c < 2 * H, gch, jnp.where(lane_c < 3 * H, gc, 0.0))
            gcol_ref[cc * CHUNK_A:(cc + 1) * CHUNK_A, :] = full
            grow_ref[cc] = full.T[0:32, :]

    return pl.pallas_call(
        body, name="prep_a_fwd", grid=(nblk,),
        in_specs=[pl.BlockSpec((tb, C3), lambda i: (i, 0)),
                  pl.BlockSpec((8, C3), lambda i: (jnp.maximum(i * (tb // 8) - 1, 0), 0)),
                  pl.BlockSpec((tb, LANES), lambda i: (i, 0)),
                  pl.BlockSpec((4, C3), lambda i: (0, 0)),
                  pl.BlockSpec((1, LANES), lambda i: (0, 0)),
                  pl.BlockSpec((1, LANES), lambda i: (0, 0))],
        out_specs=[pl.BlockSpec((tb, AW), lambda i: (i, 0)),
                   pl.BlockSpec((tb, AW), lambda i: (i, 0)),
                   pl.BlockSpec((tb, AW), lambda i: (i, 0)),
                   pl.BlockSpec((tb, C3), lambda i: (i, 0)),
                   pl.BlockSpec((tb, LANES), lambda i: (i, 0)),
                   pl.BlockSpec((nch, 32, CHUNK_A), lambda i: (i, 0, 0))],
        out_shape=[S((T, AW), F32), S((T, AW), F32), S((T, AW), F32), S((T, C3), F32),
                   S((T, LANES), F32), S((T // CHUNK_A, 32, CHUNK_A), F32)],
        compiler_params=_cp(ARB),
    )(proj, proj, ba, conv_w, alog_row, dtb_row)


_NN = (((1,), (0,)), ((), ()))
_TN = (((0,), (0,)), ((), ()))


def _split(a):
    hi = a.astype(BF16)
    return hi, (a - hi.astype(F32)).astype(BF16)


def _mm3(a, b, dims=_NN):
    ah, al = a if isinstance(a, tuple) else _split(a)
    bh, bl = b if isinstance(b, tuple) else _split(b)
    dg = lambda p, r: lax.dot_general(p, r, dims, preferred_element_type=F32)
    return dg(ah, bh) + (dg(ah, bl) + dg(al, bh))


def _interleave(gens):
    gens = list(gens)
    while gens:
        alive = []
        for g in gens:
            try:
                next(g)
                alive.append(g)
            except StopIteration:
                pass
        gens = alive


def _chunk_terms(q, k, v, gcolv, growv, h, H):
    C = CHUNK_A
    beta_c = gcolv[:, h:h + 1]
    g_c = gcolv[:, H + h:H + h + 1]
    gc_c = gcolv[:, 2 * H + h:2 * H + h + 1]
    gc_r = growv[2 * H + h:2 * H + h + 1, :]
    ri = lax.broadcasted_iota(jnp.int32, (C, C), 0)
    ci = lax.broadcasted_iota(jnp.int32, (C, C), 1)
    incl = ri >= ci
    strict = ri > ci
    kb = k * beta_c
    vb = v * beta_c
    p_raw = _mm_nt(kb, k)
    qk_raw = _mm_nt(q, k)
    gam = jnp.where(incl, jnp.exp(jnp.where(incl, gc_c - gc_r, 0.0)), 0.0)
    e_c = jnp.exp(gc_c)
    gl = gc_r[:, C - 1:C]
    edec = jnp.exp(gl - gc_c)
    yield
    lmat = jnp.where(strict, p_raw * gam, 0.0)
    attn = jnp.where(incl, qk_raw * gam, 0.0)
    return dict(beta_c=beta_c, g_c=g_c, gc_c=gc_c, gc_r=gc_r, incl=incl, strict=strict, gam=gam, e_c=e_c,
                kb=kb, vb=vb, lmat=lmat, attn=attn, gl=gl, edec=edec, ri=ri, ci=ci)


INV_BLOCK = 16


def _inv_unit_lower(lmat):
    C = lmat.shape[0]
    ri = lax.broadcasted_iota(jnp.int32, (C, C), 0)
    ci = lax.broadcasted_iota(jnp.int32, (C, C), 1)
    eye = (ri == ci).astype(F32)
    same = (ri // INV_BLOCK) == (ci // INV_BLOCK)

    def neumann(x, order):
        a = eye + x
        n = 1
        while 2 * n < order:
            xs = _split(x)
            x = _mm3(xs, xs)
            yield
            a = a + _mm3(a, x)
            n *= 2
        yield
        return a

    inv_d = yield from neumann(-jnp.where(same, lmat, 0.0), INV_BLOCK)
    m = _mm3(inv_d, jnp.where(same, 0.0, lmat))
    yield
    inv_m = yield from neumann(-m, C // INV_BLOCK)
    a = _mm3(inv_m, inv_d)
    yield
    return a


def _delta_fwd(q, k, v, gcol, grow, H, D):
    T = q.shape[0]
    C = CHUNK_A
    N = T // C
    AW = H * D
    CPS = 2 if N % 2 == 0 else 1

    def body(q_ref, k_ref, v_ref, gcol_ref, grow_ref, o_ref, vn_ref, ssave_ref, asave_ref, s_ref):
        @pl.when(pl.program_id(0) == 0)
        def _():
            s_ref[...] = jnp.zeros_like(s_ref)

        state = {(0, h): s_ref[h] for h in range(H)}

        def head(cc, h):
            rows = slice(cc * C, (cc + 1) * C)
            sl = slice(h * D, (h + 1) * D)
            qv, kv, vv = q_ref[rows, sl], k_ref[rows, sl], v_ref[rows, sl]
            t = yield from _chunk_terms(qv, kv, vv, gcol_ref[rows, :], grow_ref[cc], h, H)
            a = yield from _inv_unit_lower(t["lmat"])
            asave_ref[cc, h] = a
            while (cc, h) not in state:
                yield
            st = state[(cc, h)]
            ssave_ref[cc, h] = st
            ks = _mm(t["kb"] * t["e_c"], st)
            o_inter = _mm(qv * t["e_c"], st)
            yield
            v_new = _mm3(a, t["vb"] - ks)
            yield
            vn_ref[rows, sl] = v_new
            o_intra = _mm(t["attn"], v_new)
            s_upd = _mm_tn(kv * t["edec"], v_new)
            yield
            o_ref[rows, sl] = o_inter + o_intra
            state[(cc + 1, h)] = st * jnp.exp(t["gl"]) + s_upd

        _interleave(head(cc, h) for cc in range(CPS) for h in range(H))
        for h in range(H):
            s_ref[h] = state[(CPS, h)]

    blk = lambda: pl.BlockSpec((CPS * C, AW), lambda n: (n, 0))
    return pl.pallas_call(
        body, name="delta_fwd", grid=(N // CPS,),
        in_specs=[blk(), blk(), blk(),
                  pl.BlockSpec((CPS * C, LANES), lambda n: (n, 0)),
                  pl.BlockSpec((CPS, 32, C), lambda n: (n, 0, 0))],
        out_specs=[blk(), blk(),
                   pl.BlockSpec((CPS, H, D, D), lambda n: (n, 0, 0, 0)),
                   pl.BlockSpec((CPS, H, C, C), lambda n: (n, 0, 0, 0))],
        out_shape=[S((T, AW), F32), S((T, AW), F32), S((N, H, D, D), F32), S((N, H, C, C), F32)],
        scratch_shapes=[pltpu.VMEM((H, D, D), F32)],
        compiler_params=_cp(ARB),
    )(q, k, v, gcol, grow)


def _delta_bwd(q, k, v, gcol, grow, ba, vnew, ssave, asave, d_o, a_log, dt_bias, H, D, carry):
    T = q.shape[0]
    C = CHUNK_A
    N = T // C
    AW = H * D
    nc = len(carry)
    CPS = 2 if N % 2 == 0 else 1
    NS = N // CPS

    def body(al_ref, dt_ref, q_ref, k_ref, v_ref, gcol_ref, grow_ref, ba_ref, vn_ref, ss_ref, as_ref, do_ref, *rest):
        cins = rest[:nc]
        dq_ref, dk_ref, dv_ref, dgate_ref, dpar_ref = rest[nc:nc + 5]
        couts = rest[nc + 5:2 * nc + 5]
        ds_ref, csend, crecv = rest[2 * nc + 5:]
        ccps = _chip_exchange_copies(cins, couts, csend, crecv)

        @pl.when(pl.program_id(0) == 0)
        def _():
            ds_ref[...] = jnp.zeros_like(ds_ref)
            dpar_ref[...] = jnp.zeros_like(dpar_ref)
            for cp in ccps:
                cp.start()

        lane = lax.broadcasted_iota(jnp.int32, (C, LANES), 1)
        rowi = lax.broadcasted_iota(jnp.int32, (C, 1), 0)
        acc = {cc: jnp.zeros((C, LANES), F32) for cc in range(CPS)}
        state = {(0, h): ds_ref[h] for h in range(H)}

        def head(oi, h):
            cc = CPS - 1 - oi
            rows = slice(cc * C, (cc + 1) * C)
            sl = slice(h * D, (h + 1) * D)
            st = ss_ref[cc, h]
            a = as_ref[cc, h]
            qv, kv, vv, dov, v_new = q_ref[rows, sl], k_ref[rows, sl], v_ref[rows, sl], do_ref[rows, sl], vn_ref[rows, sl]
            t = yield from _chunk_terms(qv, kv, vv, gcol_ref[rows, :], grow_ref[cc], h, H)
            beta_c, e_c, gam, kb = t["beta_c"], t["e_c"], t["gam"], t["kb"]
            incl, strict, attn, lmat, edec = t["incl"], t["strict"], t["attn"], t["lmat"], t["edec"]
            kdec = kv * edec
            egl = jnp.exp(t["gl"])
            qe = qv * e_c
            ekb = kb * e_c

            t1 = _mm_nt(dov, st)
            ds_o = _mm_tn(qe, dov)
            dattn_raw = _mm_nt(dov, v_new)
            dv_new_o = _mm_tn(attn, dov)
            yield
            while (oi, h) not in state:
                yield
            ds_next = state[(oi, h)]
            dkdec = _mm_nt(v_new, ds_next)
            dv_new_s = _mm(kdec, ds_next)
            yield
            dgl = egl * jnp.sum(jnp.sum(st * ds_next, axis=1, keepdims=True), axis=0, keepdims=True)
            dk = edec * dkdec
            r = jnp.sum(dkdec * kdec, axis=1, keepdims=True)
            dgc = -r
            dgl = dgl + jnp.sum(r, axis=0, keepdims=True)
            dq = e_c * t1
            dgc = dgc + jnp.sum(t1 * qe, axis=1, keepdims=True)
            dattn = jnp.where(incl, dattn_raw, 0.0)
            dv_new = dv_new_s + dv_new_o
            dqm = dattn * gam
            z = dattn * attn
            dvb = _mm3(a, dv_new, _TN)
            dq_a = _mm(dqm, kv)
            dk_a = _mm_tn(dqm, qv)
            yield
            dq_ref[rows, sl] = dq + dq_a
            dv_ref[rows, sl] = beta_c * dvb
            ds_kb = _mm_tn(ekb, dvb)
            dekb_neg = _mm_nt(dvb, st)
            dl_neg = _mm_nt(dvb, v_new)
            yield
            state[(oi + 1, h)] = egl * ds_next + ds_o - ds_kb
            dekb = -dekb_neg
            dl = jnp.where(strict, -dl_neg, 0.0)
            dp = dl * gam
            z = z + dl * lmat
            dkb_p = _mm(dp, kv)
            dk_p = _mm_tn(dp, kb)
            dgc = dgc + jnp.sum(dekb * ekb, axis=1, keepdims=True)
            dgc = dgc + jnp.sum(z, axis=1, keepdims=True) - jnp.sum(z.T, axis=1, keepdims=True)
            dgc = dgc + jnp.where(rowi == C - 1, dgl, 0.0)
            yield
            dkb = dkb_p + e_c * dekb
            dk_ref[rows, sl] = dk + dk_a + dk_p + beta_c * dkb
            dbeta = jnp.sum(dkb * kv, axis=1, keepdims=True) + jnp.sum(dvb * vv, axis=1, keepdims=True)
            acc[cc] = acc[cc] + jnp.where(lane == h, dbeta, 0.0) + jnp.where(lane == H + h, dgc, 0.0)

        _interleave(head(oi, h) for oi in range(CPS) for h in range(H))
        for h in range(H):
            ds_ref[h] = state[(CPS, h)]
        ri = lax.broadcasted_iota(jnp.int32, (C, C), 0)
        ci = lax.broadcasted_iota(jnp.int32, (C, C), 1)
        upper = (ri <= ci).astype(F32)
        dal = jnp.zeros((1, LANES), F32)
        ddt = jnp.zeros((1, LANES), F32)
        for cc in range(CPS):
            rows = slice(cc * C, (cc + 1) * C)
            gates = gcol_ref[rows, :]
            dg_all = _mm3(upper, acc[cc])
            d_braw = acc[cc] * gates * (1.0 - gates)
            d_araw = dg_all * (-jnp.exp(al_ref[...])) * _sigmoid(ba_ref[rows, :] + dt_ref[...])
            dgate_ref[rows, :] = jnp.where(lane < H, d_braw, jnp.where(lane < 2 * H, d_araw, 0.0))
            dal = dal + jnp.sum(dg_all * gates, axis=0, keepdims=True)
            ddt = ddt + jnp.sum(d_araw, axis=0, keepdims=True)
        dpar_ref[0:1, :] += dal
        dpar_ref[1:2, :] += ddt

        @pl.when(pl.program_id(0) == NS - 1)
        def _():
            for cp in ccps:
                cp.wait()

    rev = lambda s: NS - 1 - s
    blk = lambda: pl.BlockSpec((CPS * C, AW), lambda s: (rev(s), 0))
    row = pl.BlockSpec((1, LANES), lambda s: (0, 0))
    any_spec = pl.BlockSpec(memory_space=pl.ANY)
    res = pl.pallas_call(
        body, name="delta_bwd", grid=(NS,),
        in_specs=[row, row, blk(), blk(), blk(),
                  pl.BlockSpec((CPS * C, LANES), lambda s: (rev(s), 0)),
                  pl.BlockSpec((CPS, 32, C), lambda s: (rev(s), 0, 0)),
                  pl.BlockSpec((CPS * C, LANES), lambda s: (rev(s), 0)),
                  blk(),
                  pl.BlockSpec((CPS, H, D, D), lambda s: (rev(s), 0, 0, 0)),
                  pl.BlockSpec((CPS, H, C, C), lambda s: (rev(s), 0, 0, 0)),
                  blk()] + [any_spec] * nc,
        out_specs=[blk(), blk(), blk(),
                   pl.BlockSpec((CPS * C, LANES), lambda s: (rev(s), 0)),
                   pl.BlockSpec((8, LANES), lambda s: (0, 0))] + [any_spec] * nc,
        out_shape=[S((T, AW), F32), S((T, AW), F32), S((T, AW), F32),
                   S((T, LANES), F32), S((8, LANES), F32)] + [S((3,) + a.shape[1:], a.dtype) for a in carry],
        scratch_shapes=[pltpu.VMEM((H, D, D), F32),
                        pltpu.SemaphoreType.DMA((max(nc, 1), 3)), pltpu.SemaphoreType.DMA((max(nc, 1), 3))],
        compiler_params=_cp(ARB),
    )(a_log, dt_bias, q, k, v, gcol, grow, ba, vnew, ssave, asave, d_o, *carry)
    return res[:5], res[5:]


def _ln_stats(xv):
    mu = jnp.mean(xv, axis=-1, keepdims=True)
    xc = xv - mu
    var = jnp.mean(xc * xc, axis=-1, keepdims=True)
    rstd = lax.rsqrt(var + EPS)
    return xc * rstd, rstd


def _mix_fwd(o, proj, head_norm_w, ln_w, ln_b, w_sp, bs_t, H, D, G, P):
    T = o.shape[0]
    AW, BW = H * D, G * P
    MIX = AW + BW
    nb = AW // BW if AW % BW == 0 else None
    assert nb == 1, "group widths must match the projection column blocks"
    cb = 3

    def body(o_ref, za_ref, ub_ref, vb_ref, zb_ref, hw_ref, lw_ref, lb_ref, w_ref, bs_ref, out_ref):
        hw = hw_ref[...]
        for h in range(H):
            sl = slice(h * D, (h + 1) * D)
            oh = o_ref[:, sl]
            rs = lax.rsqrt(jnp.mean(oh * oh, axis=-1, keepdims=True) + EPS)
            out_ref[:, sl] = (oh * rs * hw * _silu(za_ref[:, sl])).astype(BF16)
        xhat, _ = _ln_stats(vb_ref[...])
        vn = xhat * lw_ref[...] + lb_ref[...]
        ri = lax.broadcasted_iota(jnp.int32, (P, P), 0)
        ci = lax.broadcasted_iota(jnp.int32, (P, P), 1)
        bsv = bs_ref[...]
        for g in range(G):
            sl = slice(g * P, (g + 1) * P)
            wm = jnp.where(ri >= ci, w_ref[g], 0.0)
            s = _mm(wm, vn[:, sl]) + bsv[:, g:g + 1]
            out_ref[:, AW + g * P:AW + (g + 1) * P] = (ub_ref[:, sl] * s * _silu(zb_ref[:, sl])).astype(BF16)

    row = lambda w: pl.BlockSpec((1, w), lambda i: (0, 0))
    return pl.pallas_call(
        body, name="mix_fwd", grid=(T // P,),
        in_specs=[pl.BlockSpec((P, AW), lambda i: (i, 0)),
                  pl.BlockSpec((P, AW), lambda i: (i, cb)),
                  pl.BlockSpec((P, BW), lambda i: (i, cb + 1)),
                  pl.BlockSpec((P, BW), lambda i: (i, cb + 2)),
                  pl.BlockSpec((P, BW), lambda i: (i, cb + 3)),
                  row(D), row(BW), row(BW),
                  pl.BlockSpec((G, P, P), lambda i: (0, 0, 0)),
                  pl.BlockSpec((P, G), lambda i: (0, 0))],
        out_specs=pl.BlockSpec((P, MIX), lambda i: (i, 0)),
        out_shape=S((T, MIX), BF16),
        compiler_params=_cp(ARB),
    )(o, proj, proj, proj, proj, head_norm_w, ln_w, ln_b, w_sp, bs_t)


def _mix_bwd(d_ocat, o, proj, head_norm_w, ln_w, ln_b, w_sp, bs_t, H, D, G, P, carry):
    T = o.shape[0]
    AW, BW = H * D, G * P
    MIX = AW + BW
    cb = 3
    nc = len(carry)

    def body(dc_ref, o_ref, za_ref, ub_ref, vb_ref, zb_ref, hw_ref, lw_ref, lb_ref, w_ref, bs_ref, *rest):
        cins = rest[:nc]
        do_ref, dmain_ref, dhw_ref, dln_ref, dw_ref, dbs_ref = rest[nc:nc + 6]
        couts = rest[nc + 6:2 * nc + 6]
        dvn_ref, drest_ref, out_sems, csend, crecv = rest[2 * nc + 6:]
        i = pl.program_id(0)
        slot = lax.rem(i, 2)
        ccps = _sibling_copies(cins, couts, csend, crecv)

        def out_copy(step, s):
            return pltpu.make_async_copy(
                drest_ref.at[s], dmain_ref.at[pl.ds(step * P, P), pl.ds(cb * AW, AW + 3 * BW)], out_sems.at[s])

        @pl.when(i == 0)
        def _():
            dhw_ref[...] = jnp.zeros_like(dhw_ref)
            dln_ref[...] = jnp.zeros_like(dln_ref)
            dw_ref[...] = jnp.zeros_like(dw_ref)
            dbs_ref[...] = jnp.zeros_like(dbs_ref)
            for cp in ccps:
                cp.start()

        @pl.when(i >= 2)
        def _():
            out_copy(i - 2, slot).wait()

        hw = hw_ref[...]
        dhw = jnp.zeros((1, D), F32)
        for h in range(H):
            sl = slice(h * D, (h + 1) * D)
            oh = o_ref[:, sl]
            za = za_ref[:, sl]
            doa = dc_ref[:, sl]
            rs = lax.rsqrt(jnp.mean(oh * oh, axis=-1, keepdims=True) + EPS)
            xh = oh * rs
            d_on = doa * _silu(za)
            drest_ref[slot, :, sl] = (doa * (xh * hw) * _dsilu(za)).astype(BF16)
            dhw = dhw + jnp.sum(d_on * xh, axis=0, keepdims=True)
            dxh = d_on * hw
            do_ref[:, sl] = rs * (dxh - xh * jnp.mean(dxh * xh, axis=-1, keepdims=True))
        dhw_ref[0:1, :] += dhw

        xhat, rstd = _ln_stats(vb_ref[...])
        lw = lw_ref[...]
        vn = xhat * lw + lb_ref[...]
        ri = lax.broadcasted_iota(jnp.int32, (P, P), 0)
        ci = lax.broadcasted_iota(jnp.int32, (P, P), 1)
        lane = lax.broadcasted_iota(jnp.int32, (P, LANES), 1)
        bsv = bs_ref[...]
        dbs = jnp.zeros((P, LANES), F32)
        for g in range(G):
            sl = slice(g * P, (g + 1) * P)
            wm = jnp.where(ri >= ci, w_ref[g], 0.0)
            vng = vn[:, sl]
            s = _mm(wm, vng) + bsv[:, g:g + 1]
            dob = dc_ref[:, AW + g * P:AW + (g + 1) * P]
            ub = ub_ref[:, sl]
            zb = zb_ref[:, sl]
            szb = _silu(zb)
            drest_ref[slot, :, AW + g * P:AW + (g + 1) * P] = (dob * s * szb).astype(BF16)
            drest_ref[slot, :, AW + 2 * BW + g * P:AW + 2 * BW + (g + 1) * P] = (
                dob * ub * s * _dsilu(zb)).astype(BF16)
            ds = dob * ub * szb
            dvn_ref[:, sl] = _mm_tn(wm, ds)
            dw_ref[g] += jnp.where(ri >= ci, _mm_nt(ds, vng), 0.0)
            dbs = dbs + jnp.where(lane == g, jnp.sum(ds, axis=1, keepdims=True), 0.0)
        dbs_ref[...] += dbs
        dvn = dvn_ref[...]
        dln_ref[0:1, :] += jnp.sum(dvn * xhat, axis=0, keepdims=True)
        dln_ref[1:2, :] += jnp.sum(dvn, axis=0, keepdims=True)
        dxh = dvn * lw
        dvb = rstd * (dxh - jnp.mean(dxh, axis=-1, keepdims=True) - xhat * jnp.mean(dxh * xhat, axis=-1, keepdims=True))
        drest_ref[slot, :, AW + BW:AW + 2 * BW] = dvb.astype(BF16)

        out_copy(i, slot).start()

        @pl.when(i == nstep - 1)
        def _():
            out_copy(i, slot).wait()
            if nstep > 1:
                out_copy(i - 1, 1 - slot).wait()
            for cp in ccps:
                cp.wait()

    nstep = T // P
    row = lambda w: pl.BlockSpec((1, w), lambda i: (0, 0))
    any_spec = pl.BlockSpec(memory_space=pl.ANY)
    res = pl.pallas_call(
        body, name="mix_bwd", grid=(nstep,),
        in_specs=[pl.BlockSpec((P, MIX), lambda i: (i, 0)),
                  pl.BlockSpec((P, AW), lambda i: (i, 0)),
                  pl.BlockSpec((P, AW), lambda i: (i, cb)),
                  pl.BlockSpec((P, BW), lambda i: (i, cb + 1)),
                  pl.BlockSpec((P, BW), lambda i: (i, cb + 2)),
                  pl.BlockSpec((P, BW), lambda i: (i, cb + 3)),
                  row(D), row(BW), row(BW),
                  pl.BlockSpec((G, P, P), lambda i: (0, 0, 0)),
                  pl.BlockSpec((P, G), lambda i: (0, 0))] + [any_spec] * nc,
        out_specs=[pl.BlockSpec((P, AW), lambda i: (i, 0)),
                   any_spec,
                   pl.BlockSpec((8, D), lambda i: (0, 0)),
                   pl.BlockSpec((8, BW), lambda i: (0, 0)),
                   pl.BlockSpec((G, P, P), lambda i: (0, 0, 0)),
                   pl.BlockSpec((P, LANES), lambda i: (0, 0))] + [any_spec] * nc,
        out_shape=[S((T, AW), F32), S((T, cb * AW + AW + 3 * BW), BF16), S((8, D), F32), S((8, BW), F32),
                   S((G, P, P), F32), S((P, LANES), F32)] + [S(a.shape[:1] + a.shape[2:], a.dtype) for a in carry],
        scratch_shapes=[pltpu.VMEM((P, BW), F32), pltpu.VMEM((2, P, AW + 3 * BW), BF16),
                        pltpu.SemaphoreType.DMA((2,))] + _sibling_sems(carry),
        compiler_params=_cp(ARB),
    )(d_ocat, o, proj, proj, proj, proj, head_norm_w, ln_w, ln_b, w_sp, bs_t, *carry)
    return res[:6], res[6:]


def _out_proj_loss(ocat, w_out, x, target, fnw):
    T, MIX = ocat.shape
    DM = x.shape[1]
    tm = _tile(T, 256, 8)

    def body(oc_ref, w_ref, x_ref, t_ref, fw_ref, dh_ref, dhb_ref, doc_ref, loss_ref, gfw_ref):
        @pl.when(pl.program_id(0) == 0)
        def _():
            loss_ref[...] = jnp.zeros_like(loss_ref)
            gfw_ref[...] = jnp.zeros_like(gfw_ref)

        wv = w_ref[...]
        hh = x_ref[...] + jnp.dot(oc_ref[...].astype(MXU), wv.astype(MXU), preferred_element_type=F32)
        rs = lax.rsqrt(jnp.mean(hh * hh, axis=-1, keepdims=True) + EPS)
        hn = hh * rs
        fw = fw_ref[...]
        e = hn * fw - t_ref[...]
        row_loss = 0.5 * jnp.mean(e * e, axis=-1, keepdims=True)
        loss_ref[...] += jnp.sum(row_loss, axis=0, keepdims=True)
        dy = e * (1.0 / DM)
        gfw_ref[0:1, :] += jnp.sum(dy * hn, axis=0, keepdims=True)
        dhn = dy * fw
        dh = rs * (dhn - hn * jnp.mean(dhn * hn, axis=-1, keepdims=True))
        dh_ref[...] = dh
        dhb = dh.astype(BF16)
        dhb_ref[...] = dhb
        doc_ref[...] = _mm_nt(dhb, wv)

    return pl.pallas_call(
        body, name="out_proj_loss", grid=(T // tm,),
        in_specs=[pl.BlockSpec((tm, MIX), lambda i: (i, 0)),
                  pl.BlockSpec((MIX, DM), lambda i: (0, 0)),
                  pl.BlockSpec((tm, DM), lambda i: (i, 0)),
                  pl.BlockSpec((tm, DM), lambda i: (i, 0)),
                  pl.BlockSpec((1, DM), lambda i: (0, 0))],
        out_specs=[pl.BlockSpec((tm, DM), lambda i: (i, 0)),
                   pl.BlockSpec((tm, DM), lambda i: (i, 0)),
                   pl.BlockSpec((tm, MIX), lambda i: (i, 0)),
                   pl.BlockSpec((8, LANES), lambda i: (0, 0)),
                   pl.BlockSpec((8, DM), lambda i: (0, 0))],
        out_shape=[S((T, DM), F32), S((T, DM), BF16), S((T, MIX), F32), S((8, LANES), F32), S((8, DM), F32)],
        compiler_params=_cp(ARB),
    )(ocat, w_out, x, target, fnw)


def _grad_w(lhs, rhs, name):
    T, A = lhs.shape
    B = rhs.shape[1]
    ta = _tile(A, 512, LANES)
    tk = _tile(T, 1024, 16)
    nk = T // tk

    def body(l_ref, r_ref, out_ref, acc_ref):
        k = pl.program_id(1)
        part = _mm_tn(l_ref[...], r_ref[...])

        @pl.when(k == 0)
        def _():
            acc_ref[...] = part

        @pl.when(k > 0)
        def _():
            acc_ref[...] += part

        @pl.when(k == nk - 1)
        def _():
            out_ref[...] = acc_ref[...].astype(BF16)

    return pl.pallas_call(
        body, name=name, grid=(A // ta, nk),
        in_specs=[pl.BlockSpec((tk, ta), lambda i, k: (k, i)),
                  pl.BlockSpec((tk, B), lambda i, k: (k, 0))],
        out_specs=pl.BlockSpec((ta, B), lambda i, k: (i, 0)),
        out_shape=S((A, B), BF16),
        scratch_shapes=[pltpu.VMEM((ta, B), F32)],
        compiler_params=_cp(ARB, ARB),
    )(lhs, rhs)


def _grad_w_in(xn, dmain, dba, WD, gate_lo, gate_hi):
    T, DM = xn.shape
    NM = dmain.shape[1]
    tn = _tile(NM, 1024, LANES)
    tk = _tile(T, 2048, 16)
    nj, nk = NM // tn, T // tk
    ND = N_DEV
    tiles = [[] for _ in range(nj)]
    first_tile, last_tile = {}, {}
    for d, s0, s1, dest, c0 in _pieces(WD, gate_lo, gate_hi, ND * WD):
        if dest != "main":
            continue
        while s0 < s1:
            jj = c0 // tn
            w = min(s1 - s0, (jj + 1) * tn - c0)
            tiles[jj].append((d, s0, w, "main", c0 - jj * tn))
            first_tile.setdefault(d, jj)
            last_tile[d] = jj
            s0, c0 = s0 + w, c0 + w
    for d, s0, s1, dest, c0 in _pieces(WD, gate_lo, gate_hi, ND * WD):
        if dest == "gate":
            tiles[first_tile[d]].append((d, s0, s1 - s0, "gate", c0))
    assert sorted(first_tile) == list(range(ND)) and all(last_tile[d] <= first_tile[d + 2] for d in range(ND - 2))

    def body(xn_ref, dm_ref, dba_ref, keep_ref, recv_ref, acc_ref, gate_ref, buf_ref, lsem, ssem, rsem):
        j = pl.program_id(0)
        k = pl.program_id(1)
        px, py, pc = _position()

        @pl.when(k == 0)
        def _():
            acc_ref[...] = jnp.zeros_like(acc_ref)

        @pl.when((j == 0) & (k == 0))
        def _():
            gate_ref[...] = jnp.zeros_like(gate_ref)

        xv = xn_ref[...]
        acc_ref[...] += _mm_tn(xv, dm_ref[...])

        @pl.when(j == 0)
        def _():
            gate_ref[...] += _mm_tn(xv, dba_ref[...])

        def local(d):
            return pltpu.make_async_copy(buf_ref.at[d % 2], keep_ref.at[d // 2], lsem.at[d // 2])

        def remote(d):
            return pltpu.make_async_remote_copy(
                src_ref=buf_ref.at[d % 2], dst_ref=recv_ref.at[d // 2], send_sem=ssem.at[d // 2],
                recv_sem=rsem.at[d // 2], device_id=(px, py, 1 - pc), device_id_type=MESH)

        def leave(d, start):
            @pl.when(pc == d % 2)
            def _():
                local(d).start() if start else local(d).wait()

            @pl.when(pc != d % 2)
            def _():
                remote(d).start() if start else remote(d).wait_send()

        def emit(jj):
            shards = sorted({p[0] for p in tiles[jj]})
            for d in shards:
                if first_tile[d] == jj and d >= 2:
                    leave(d - 2, False)
                for dd, s0, w, src, c0 in tiles[jj]:
                    if dd == d:
                        ref = acc_ref if src == "main" else gate_ref
                        buf_ref[d % 2, :, s0:s0 + w] = ref[:, c0:c0 + w].astype(BF16)
                if last_tile[d] == jj:
                    leave(d, True)
            if jj == nj - 1:
                for d in (ND - 2, ND - 1):
                    leave(d, False)
                for q in range(ND // 2):
                    remote(2 * q).wait_recv()

        for jj in range(nj):
            @pl.when((j == jj) & (k == nk - 1))
            def _(jj=jj):
                emit(jj)

    any_spec = pl.BlockSpec(memory_space=pl.ANY)
    return pl.pallas_call(
        body, name="grad_w_in", grid=(nj, nk),
        in_specs=[pl.BlockSpec((tk, DM), lambda j, k: (k, 0)),
                  pl.BlockSpec((tk, tn), lambda j, k: (k, j)),
                  pl.BlockSpec((tk, LANES), lambda j, k: (k, 0))],
        out_specs=[any_spec, any_spec],
        out_shape=[S((ND // 2, DM, WD), BF16), S((ND // 2, DM, WD), BF16)],
        scratch_shapes=[pltpu.VMEM((DM, tn), F32), pltpu.VMEM((DM, LANES), F32), pltpu.VMEM((2, DM, WD), BF16),
                        pltpu.SemaphoreType.DMA((ND // 2,)), pltpu.SemaphoreType.DMA((ND // 2,)),
                        pltpu.SemaphoreType.DMA((ND // 2,))],
        compiler_params=_cp(ARB, ARB),
    )(xn, dmain, dba)


def _pair_sum_plain(a, b, name):
    K, R, C = a.shape
    tr = _tile(R, 256, 16)

    def body(a_ref, b_ref, o_ref):
        o_ref[...] = (a_ref[...].astype(F32) + b_ref[...].astype(F32)).astype(BF16)

    spec = lambda: pl.BlockSpec((1, tr, C), lambda q, i: (q, i, 0))
    return pl.pallas_call(body, name=name, grid=(K, R // tr), in_specs=[spec(), spec()], out_specs=spec(),
                          out_shape=S((K, R, C), BF16), compiler_params=_cp(ARB, ARB))(a, b)


def _dx_rows(T):
    tm = _tile(T, 512, 8)
    return tm if T // tm >= 2 else T // 2


def _dx_part(name, dmain, dba, w_main, w_ba, x, dh, norm_w, blk0, nblk, prev, hbm_in, hbm_alias, hbm_new, make_copies):
    T, NM = dmain.shape
    DM = x.shape[1]
    tm = _dx_rows(T)
    tk = _tile(NM, 1024, LANES)
    nk = NM // tk
    n_in, n_al, n_new = len(hbm_in), len(hbm_alias), len(hbm_new)
    n_prev = 0 if prev is None else 2
    last_step = nblk * nk - 1

    def body(dm_ref, dba_ref, w_ref, wba_ref, x_ref, dh_ref, nw_ref, *rest):
        r = list(rest)
        gnw_prev_ref = r.pop(0) if n_prev else None
        if n_prev:
            r.pop(0)
        in_refs = [r.pop(0) for _ in range(n_in)]
        del r[:n_al]
        gx_ref, gnw_ref = r.pop(0), r.pop(0)
        alias_refs = [r.pop(0) for _ in range(n_al)]
        new_refs = [r.pop(0) for _ in range(n_new)]
        acc_ref, send_sems, recv_sems = r
        i = pl.program_id(0)
        k = pl.program_id(1)
        step = i * nk + k
        cps = make_copies(in_refs, alias_refs, new_refs, send_sems, recv_sems)

        @pl.when(step == 0)
        def _():
            gnw_ref[...] = gnw_prev_ref[...] if n_prev else jnp.zeros_like(gnw_ref)
            for cp in cps:
                cp.start()

        @pl.when(k == 0)
        def _():
            acc_ref[...] = _mm_nt(dba_ref[...], wba_ref[...])

        acc_ref[...] += _mm_nt(dm_ref[...], w_ref[...])

        @pl.when(k == nk - 1)
        def _():
            xv = x_ref[...]
            rs = lax.rsqrt(jnp.mean(xv * xv, axis=-1, keepdims=True) + EPS)
            xh = xv * rs
            dxn = acc_ref[...]
            gnw_ref[0:1, :] += jnp.sum(dxn * xh, axis=0, keepdims=True)
            dxh = dxn * nw_ref[...]
            gx_ref[...] = dh_ref[...] + rs * (dxh - xh * jnp.mean(dxh * xh, axis=-1, keepdims=True))

        @pl.when(step == last_step)
        def _():
            for cp in cps:
                cp.wait()

    any_spec = pl.BlockSpec(memory_space=pl.ANY)
    prev_specs = [pl.BlockSpec((8, DM), lambda i, k: (0, 0)), any_spec] if n_prev else []
    prev_args = [prev[1], prev[0]] if n_prev else []
    aliases = {8: 0} if n_prev else {}
    for q in range(n_al):
        aliases[7 + n_prev + n_in + q] = 2 + q
    res = pl.pallas_call(
        body, name=name, grid=(nblk, nk),
        in_specs=[pl.BlockSpec((tm, tk), lambda i, k: (blk0 + i, k)),
                  pl.BlockSpec((tm, LANES), lambda i, k: (blk0 + i, 0)),
                  pl.BlockSpec((DM, tk), lambda i, k: (0, k)),
                  pl.BlockSpec((DM, LANES), lambda i, k: (0, 0)),
                  pl.BlockSpec((tm, DM), lambda i, k: (blk0 + i, 0)),
                  pl.BlockSpec((tm, DM), lambda i, k: (blk0 + i, 0)),
                  pl.BlockSpec((1, DM), lambda i, k: (0, 0))] + prev_specs + [any_spec] * (n_in + n_al),
        out_specs=[pl.BlockSpec((tm, DM), lambda i, k: (blk0 + i, 0)),
                   pl.BlockSpec((8, DM), lambda i, k: (0, 0))] + [any_spec] * (n_al + n_new),
        out_shape=[S((T, DM), F32), S((8, DM), F32)] + [S(a.shape, a.dtype) for a in hbm_alias] + list(hbm_new),
        scratch_shapes=[pltpu.VMEM((tm, DM), F32), pltpu.SemaphoreType.DMA((10,)), pltpu.SemaphoreType.DMA((10,))],
        input_output_aliases=aliases,
        compiler_params=_cp(ARB, ARB),
    )(dmain, dba, w_main, w_ba, x, dh, norm_w, *prev_args, *hbm_in, *hbm_alias)
    return (res[0], res[1]), res[2:2 + n_al], res[2 + n_al:]


def _remote(kk, src, dst, to, send_sems, recv_sems):
    return pltpu.make_async_remote_copy(src_ref=src, dst_ref=dst, send_sem=send_sems.at[kk], recv_sem=recv_sems.at[kk],
                                        device_id=to, device_id_type=MESH)


def _dx(dmain, dba, w_main, w_ba, x, dh, norm_w, chip_sum, small, cut):
    R, C = chip_sum.shape[1:]
    half = R // 2
    assert half % 16 == 0
    T = x.shape[0]
    ni = T // _dx_rows(T)
    cut = max(1, min(cut, ni - 1))
    upper, lower = pl.ds(0, half), pl.ds(half, half)

    def nbrs():
        px, py, pc = _position()
        return (px, py), (1 - px, py, pc), (px, 1 - py, pc)

    def phase1(ins, als, news, ss, rs):
        (px, py), xn, yn = nbrs()
        cs = ins[0]
        recv, stage = news
        bx, by, bd = cs.at[2 * (1 - px) + py], cs.at[2 * px + (1 - py)], cs.at[2 * (1 - px) + (1 - py)]
        return [_remote(0, bx.at[upper], recv.at[0].at[upper], xn, ss, rs),
                _remote(1, by.at[lower], recv.at[1].at[lower], yn, ss, rs),
                _remote(2, bd.at[upper], stage.at[0], xn, ss, rs),
                _remote(3, bd.at[lower], stage.at[1], yn, ss, rs)]

    def phase2(ins, als, news, ss, rs):
        (px, py), xn, yn = nbrs()
        comb, small_ref = ins
        recv, gath = als[0], news[0]
        me, small_cps = _broadcast_copies([small_ref], [gath], _Sem2(ss, 2), _Sem2(rs, 2))
        return ([_remote(0, comb.at[0], recv.at[1].at[upper], yn, ss, rs),
                 _remote(1, comb.at[1], recv.at[0].at[lower], xn, ss, rs)] + small_cps
                + [pltpu.make_async_copy(small_ref, gath.at[me], ss.at[9])])

    (gx, gnw), _, (recv, stage) = _dx_part(
        "dx_a", dmain, dba, w_main, w_ba, x, dh, norm_w, 0, cut, None, [chip_sum], [],
        [S((2, R, C), chip_sum.dtype), S((2, half, C), chip_sum.dtype)], phase1)
    comb = _relay_add(chip_sum, stage)
    (gx, gnw), (recv,), (gath,) = _dx_part(
        "dx_b", dmain, dba, w_main, w_ba, x, dh, norm_w, cut, ni - cut, (gx, gnw), [comb, small], [recv],
        [S((N_DEV,) + small.shape, F32)], phase2)
    return gx, gnw, gath, recv


class _Sem2:
    def __init__(self, sems, lo):
        self.sems, self.lo = sems, lo

    @property
    def at(self):
        outer = self

        class _At:
            def __getitem__(self, idx):
                a, k = idx
                return outer.sems.at[outer.lo + k]
        return _At()


def _relay_add(chip_sum, stage):
    _, R, C = chip_sum.shape
    half = R // 2
    tr = _tile(half, 256, 16)
    nt = half // tr
    px, py, _ = _position()
    idx = jnp.stack([2 * px + (1 - py), 2 * (1 - px) + py]).astype(jnp.int32)

    def body(idx_ref, p_ref, s_ref, o_ref):
        del idx_ref
        o_ref[0] = (p_ref[0].astype(F32) + s_ref[0].astype(F32)).astype(BF16)

    return pl.pallas_call(
        body, name="relay_add",
        grid_spec=pltpu.PrefetchScalarGridSpec(
            num_scalar_prefetch=1, grid=(2, nt),
            in_specs=[pl.BlockSpec((1, tr, C), lambda s, i, idx_ref: (idx_ref[s], s * nt + i, 0)),
                      pl.BlockSpec((1, tr, C), lambda s, i, idx_ref: (s, i, 0))],
            out_specs=pl.BlockSpec((1, tr, C), lambda s, i, idx_ref: (s, i, 0))),
        out_shape=S((2, half, C), BF16), compiler_params=_cp(ARB, ARB),
    )(idx, chip_sum, stage)


def _sum_slots(gath):
    _, R, C = gath.shape
    tr = R if R <= 2048 else _tile(R, 512, 8)

    def body(g_ref, o_ref):
        tot = g_ref[0]
        for d in range(1, N_DEV):
            tot = tot + g_ref[d]
        o_ref[...] = tot

    return pl.pallas_call(
        body, name="sum_slots", grid=(R // tr,),
        in_specs=[pl.BlockSpec((N_DEV, tr, C), lambda i: (0, i, 0))],
        out_specs=pl.BlockSpec((tr, C), lambda i: (i, 0)),
        out_shape=S((R, C), F32), compiler_params=_cp(ARB),
    )(gath)


def _prep_a_bwd(dq, dk, dv, c, proj, conv_w, dmain, H, D):
    T = c.shape[0]
    AW = H * D
    C3 = 3 * AW
    tb = _tile(T, 256, 8)
    nblk = T // tb
    r8 = tb // 8
    scale = float(D) ** -0.5

    def body(dq_ref, dk_ref, dv_ref, c_ref, dqn_ref, dkn_ref, dvn_ref, cn_ref, x_ref, halo_ref, cw_ref, dmain_in_ref,
             dx_ref, gcw_ref, dc_ref):
        del dmain_in_ref
        i = pl.program_id(0)

        @pl.when(i == 0)
        def _():
            gcw_ref[...] = jnp.zeros_like(gcw_ref)

        def pointwise(rows, dq_r, dk_r, dv_r, c_r, keep):
            for h in range(H):
                for part, d_r, sc in ((0, dq_r, scale), (1, dk_r, 1.0)):
                    sl = slice(part * AW + h * D, part * AW + (h + 1) * D)
                    cv = c_r[:, sl]
                    raw = _silu(cv)
                    rs = lax.rsqrt(jnp.sum(raw * raw, axis=-1, keepdims=True) + EPS)
                    nrm = raw * rs
                    dn = d_r[:, h * D:(h + 1) * D] * sc
                    draw = rs * (dn - nrm * jnp.sum(dn * nrm, axis=-1, keepdims=True))
                    dc_ref[rows, sl] = draw * _dsilu(cv) * keep
            dc_ref[rows, 2 * AW:] = dv_r[...] * _dsilu(c_r[:, 2 * AW:]) * keep

        pointwise(slice(0, tb), dq_ref, dk_ref, dv_ref, c_ref, 1.0)
        pointwise(slice(tb, tb + 8), dqn_ref, dkn_ref, dvn_ref, cn_ref, (i < nblk - 1).astype(F32))

        cw = cw_ref[...]
        dcv = dc_ref[0:tb, :]
        dx = cw[3:4, :] * dcv
        for j in range(3):
            dx = dx + cw[j:j + 1, :] * dc_ref[3 - j:3 - j + tb, :]
        dx_ref[...] = dx.astype(BF16)
        halo = halo_ref[...] * (i > 0).astype(F32)
        xp = jnp.concatenate([halo, x_ref[...]], axis=0)
        for j in range(4):
            gcw_ref[j:j + 1, :] += jnp.sum(dcv * xp[5 + j:5 + j + tb], axis=0, keepdims=True)

    nxt = lambda i: (jnp.minimum((i + 1) * r8, T // 8 - 1), 0)
    return pl.pallas_call(
        body, name="prep_a_bwd", grid=(nblk,),
        in_specs=[pl.BlockSpec((tb, AW), lambda i: (i, 0)),
                  pl.BlockSpec((tb, AW), lambda i: (i, 0)),
                  pl.BlockSpec((tb, AW), lambda i: (i, 0)),
                  pl.BlockSpec((tb, C3), lambda i: (i, 0)),
                  pl.BlockSpec((8, AW), nxt), pl.BlockSpec((8, AW), nxt), pl.BlockSpec((8, AW), nxt),
                  pl.BlockSpec((8, C3), nxt),
                  pl.BlockSpec((tb, C3), lambda i: (i, 0)),
                  pl.BlockSpec((8, C3), lambda i: (jnp.maximum(i * r8 - 1, 0), 0)),
                  pl.BlockSpec((4, C3), lambda i: (0, 0)),
                  pl.BlockSpec(memory_space=pl.ANY)],
        out_specs=[pl.BlockSpec((tb, C3), lambda i: (i, 0)),
                   pl.BlockSpec((8, C3), lambda i: (0, 0))],
        out_shape=[S(dmain.shape, dmain.dtype), S((8, C3), F32)],
        scratch_shapes=[pltpu.VMEM((tb + 8, C3), F32)],
        input_output_aliases={11: 0},
        compiler_params=_cp(ARB),
    )(dq, dk, dv, c, dq, dk, dv, c, proj, proj, conv_w, dmain)


def _adam_math(w, g, m, v):
    m2 = ADAM_B1 * m + (1.0 - ADAM_B1) * g
    v2 = ADAM_B2 * v + (1.0 - ADAM_B2) * (g * g)
    m_hat = m2 / (1.0 - ADAM_B1 ** ADAM_STEP)
    v_hat = v2 / (1.0 - ADAM_B2 ** ADAM_STEP)
    delta = -ADAM_LR * (m_hat / (jnp.sqrt(v_hat) + ADAM_EPS) + ADAM_WD * w)
    return delta, m2, v2


def _pair_sum(blocks, recv, core, name):
    K, _, R, C = blocks.shape
    tr = _tile(R, 256, 16)

    def body(core_ref, a_ref, b_ref, o_ref):
        del core_ref
        o_ref[0] = (a_ref[0, 0].astype(F32) + b_ref[0].astype(F32)).astype(BF16)

    spec = lambda: pl.BlockSpec((1, tr, C), lambda k, i, core_ref: (k, i, 0))
    return pl.pallas_call(
        body, name=name,
        grid_spec=pltpu.PrefetchScalarGridSpec(
            num_scalar_prefetch=1, grid=(K, R // tr),
            in_specs=[pl.BlockSpec((1, 1, tr, C), lambda k, i, core_ref: (k, core_ref[0], i, 0)), spec()],
            out_specs=spec()),
        out_shape=S((K, R, C), BF16), compiler_params=_cp(ARB, ARB),
    )(core, blocks, recv)


def _sum_adam(chip_sums, recv, w, m, v, chip, name, transposed=False):
    R, C = chip_sums.shape[1:]
    NR = recv.shape[0]
    tr = _tile(R, 256, 16)

    def body(chip_ref, own_ref, r_ref, w_ref, m_ref, v_ref, g_ref, d_ref, m2_ref, v2_ref):
        del chip_ref
        g = own_ref[0].astype(F32)
        for j in range(NR):
            g = g + r_ref[j].astype(F32)
        if transposed:
            g = g.T
        g_ref[...] = g
        d_ref[...], m2_ref[...], v2_ref[...] = _adam_math(w_ref[...], g, m_ref[...], v_ref[...])

    if transposed:
        spec = lambda: pl.BlockSpec((C, tr), lambda i, chip_ref: (0, i))
        shape = (C, R)
    else:
        spec = lambda: pl.BlockSpec((tr, C), lambda i, chip_ref: (i, 0))
        shape = (R, C)
    assert w.shape == shape
    return pl.pallas_call(
        body, name=name,
        grid_spec=pltpu.PrefetchScalarGridSpec(
            num_scalar_prefetch=1, grid=(R // tr,),
            in_specs=[pl.BlockSpec((1, tr, C), lambda i, chip_ref: (chip_ref[0], i, 0)),
                      pl.BlockSpec((NR, tr, C), lambda i, chip_ref: (0, i, 0)), spec(), spec(), spec()],
            out_specs=[spec(), spec(), spec(), spec()]),
        out_shape=[S(shape, F32)] * 4, compiler_params=_cp(ARB),
    )(chip, chip_sums, recv, w, m, v)


def _adam_small(w, g, m, v):
    R, C = w.shape
    tr = _tile(R, 512, 8)

    def body(w_ref, g_ref, m_ref, v_ref, d_ref, m2_ref, v2_ref):
        d_ref[...], m2_ref[...], v2_ref[...] = _adam_math(w_ref[...], g_ref[...], m_ref[...], v_ref[...])

    spec = lambda: pl.BlockSpec((tr, C), lambda i: (i, 0))
    return pl.pallas_call(
        body, name="adam_small", grid=(R // tr,), in_specs=[spec()] * 4, out_specs=[spec()] * 3,
        out_shape=[S((R, C), F32)] * 3, compiler_params=_cp(ARB),
    )(w, g, m, v)


def _position():
    return lax.axis_index("x"), lax.axis_index("y"), lax.axis_index("c")


def _all_gather_weights(arr, x_in, norm_w, sel, tn, plans, near, nm):
    R = arr.shape[0]
    half = R // 2
    assert half % 16 == 0
    T, DM = x_in.shape
    tm = _tile(T, 512, 16)
    nstep = T // tm
    nslot = len(near[0])
    assert nslot >= 3
    shards_of = [sorted({p[0] for p in plan}) for plan in plans]
    nstage = max(len(shards_of[t]) for r in near for t in r)

    def body(sel_ref, x_ref, nw_ref, in_ref, xn_ref, out_ref, proj_ref, wtile_ref, stage_ref, xn_all,
             send_sems, recv_sems, local_sem, stage_sems):
        i = pl.program_id(0)
        rb = i % nstep
        x, y, c = _position()
        me, sibling = (x, y, c), (x, y, 1 - c)
        xn, yn, diag = (1 - x, y), (x, 1 - y), (1 - x, 1 - y)
        upper, lower = pl.ds(0, half), pl.ds(half, half)

        def slot(p, rows=None):
            ref = out_ref.at[4 * p[0] + 2 * p[1] + p[2]]
            return ref if rows is None else ref.at[rows]

        def copy(kk, block, to, rows=None, src=None):
            return pltpu.make_async_remote_copy(
                src_ref=slot(block, rows) if src is None else src, dst_ref=slot(block, rows),
                send_sem=send_sems.at[kk], recv_sem=recv_sems.at[kk], device_id=to, device_id_type=MESH)

        mine = pltpu.make_async_copy(in_ref, slot(me), local_sem)
        first = [copy(0, me, sibling, src=in_ref), copy(1, me, (*xn, c), src=in_ref), copy(2, me, (*yn, c), src=in_ref)]
        from_x = [copy(5, (*xn, c), (*yn, c), rows=upper), copy(3, (*xn, c), sibling)]
        from_y = [copy(6, (*yn, c), (*xn, c), rows=lower), copy(4, (*yn, c), sibling)]
        from_diag = [copy(7, (*diag, c), sibling, rows=upper), copy(8, (*diag, c), sibling, rows=lower)]

        def lay_out(k, loads_of):
            for m in range(len(near)):
                @pl.when(sel_ref[0] == m)
                def _(m=m):
                    t = near[m][k]
                    loads = loads_of(m, shards_of[t])
                    for cp in loads:
                        cp.start()
                    for cp in loads:
                        cp.wait()
                    for d, s0, w, c0 in plans[t]:
                        wtile_ref[:, c0:c0 + w] = stage_ref[shards_of[t].index(d), :, s0:s0 + w]

        def load_landed(m, shards):
            return [pltpu.make_async_copy(out_ref.at[d], stage_ref.at[n], stage_sems.at[n]) for n, d in enumerate(shards)]

        @pl.when(i == 0)
        def _():
            mine.start()
            for cp in first:
                cp.start()
            copy(0, sibling, me).wait_recv()
            lay_out(0, lambda m, shards: [pltpu.make_async_copy(in_ref, stage_ref.at[c], stage_sems.at[0]),
                                          pltpu.make_async_copy(slot(sibling), stage_ref.at[1 - c], stage_sems.at[1])])

        @pl.when(i == nstep)
        def _():
            copy(1, (*xn, c), me).wait_recv()
            for cp in from_x:
                cp.start()
            copy(2, (*yn, c), me).wait_recv()
            for cp in from_y:
                cp.start()
            copy(3, (*xn, 1 - c), me).wait_recv()
            copy(4, (*yn, 1 - c), me).wait_recv()
            mine.wait()

        @pl.when(i == (nslot - 1) * nstep)
        def _():
            copy(5, (*diag, c), me, rows=upper).wait_recv()
            from_diag[0].start()
            copy(6, (*diag, c), me, rows=lower).wait_recv()
            from_diag[1].start()

        for k in range(1, nslot):
            @pl.when(i == k * nstep)
            def _(k=k):
                lay_out(k, load_landed)

        @pl.when(i < nstep)
        def _():
            xv = x_ref[...]
            r = lax.rsqrt(jnp.mean(xv * xv, axis=-1, keepdims=True) + EPS)
            xnv = (xv * r * nw_ref[...]).astype(BF16)
            xn_ref[...] = xnv
            xn_all[rb] = xnv

        proj_ref[...] = jnp.dot(xn_all[rb].astype(MXU), wtile_ref[...].astype(MXU), preferred_element_type=F32)

        @pl.when(i == nslot * nstep - 1)
        def _():
            copy(7, (*diag, 1 - c), me, rows=upper).wait_recv()
            copy(8, (*diag, 1 - c), me, rows=lower).wait_recv()
            for cp in first + from_x + from_y + from_diag:
                cp.wait_send()

    any_spec = pl.BlockSpec(memory_space=pl.ANY)
    last = nstep - 1
    return pl.pallas_call(
        body, name="all_gather_weights",
        grid_spec=pltpu.PrefetchScalarGridSpec(
            num_scalar_prefetch=1, grid=(nslot * nstep,),
            in_specs=[pl.BlockSpec((tm, DM), lambda i, sel_ref: (jnp.minimum(i, last), 0)),
                      pl.BlockSpec((1, DM), lambda i, sel_ref: (0, 0)), any_spec],
            out_specs=[pl.BlockSpec((tm, DM), lambda i, sel_ref: (jnp.minimum(i, last), 0)), any_spec,
                       pl.BlockSpec((tm, tn), lambda i, sel_ref: (i % nstep, sel_ref[1 + i // nstep]))],
            scratch_shapes=[pltpu.VMEM((DM, tn), arr.dtype), pltpu.VMEM((nstage,) + arr.shape, arr.dtype),
                            pltpu.VMEM((nstep, tm, DM), BF16),
                            pltpu.SemaphoreType.DMA((9,)), pltpu.SemaphoreType.DMA((9,)), pltpu.SemaphoreType.DMA,
                            pltpu.SemaphoreType.DMA((nstage,))]),
        out_shape=[S((T, DM), BF16), S((N_DEV,) + arr.shape, arr.dtype), S((T, nm), F32)],
        compiler_params=_cp(ARB),
    )(sel, x_in, norm_w, arr)


def _sibling_copies(ins, outs, send_sems, recv_sems):
    x, y, c = _position()
    return [pltpu.make_async_remote_copy(src_ref=ins[a].at[k, 1 - c], dst_ref=outs[a].at[k],
                                         send_sem=send_sems.at[a, k], recv_sem=recv_sems.at[a, k],
                                         device_id=(x, y, 1 - c), device_id_type=MESH)
            for a in range(len(ins)) for k in range(ins[a].shape[0])]


def _sibling_sems(arrs):
    shape = (max(len(arrs), 1), arrs[0].shape[0] if arrs else 1)
    return [pltpu.SemaphoreType.DMA(shape), pltpu.SemaphoreType.DMA(shape)]


def _chip_exchange_copies(ins, outs, send_sems, recv_sems):
    x, y, c = _position()
    chips = [(1 - x, y), (x, 1 - y), (1 - x, 1 - y)]
    return [pltpu.make_async_remote_copy(
        src_ref=ins[a].at[2 * qx + qy], dst_ref=outs[a].at[j], send_sem=send_sems.at[a, j],
        recv_sem=recv_sems.at[a, j], device_id=(qx, qy, c), device_id_type=MESH)
        for a in range(len(ins)) for j, (qx, qy) in enumerate(chips)]


def _broadcast_copies(srcs, dsts, send_sems, recv_sems):
    x, y, c = _position()
    me = 4 * x + 2 * y + c
    cps = []
    for a in range(len(srcs)):
        for k in range(1, N_DEV):
            peer = (1 - x if k & 4 else x, 1 - y if k & 2 else y, 1 - c if k & 1 else c)
            cps.append(pltpu.make_async_remote_copy(
                src_ref=srcs[a], dst_ref=dsts[a].at[me], send_sem=send_sems.at[a, k - 1],
                recv_sem=recv_sems.at[a, k - 1], device_id=peer, device_id_type=MESH))
    return me, cps


def _all_reduce_small(part):
    R, C = part.shape

    def body(p_ref, out_ref, gath_ref, send_sems, recv_sems):
        me, cps = _broadcast_copies([p_ref], [gath_ref], send_sems, recv_sems)
        gath_ref[me] = p_ref[...]
        for cp in cps:
            cp.start()
        for cp in cps:
            cp.wait()
        acc = gath_ref[0]
        for d in range(1, N_DEV):
            acc = acc + gath_ref[d]
        out_ref[...] = acc

    vm = pl.BlockSpec(memory_space=pltpu.VMEM)
    return pl.pallas_call(
        body, name="all_reduce_small", in_specs=[vm], out_specs=vm, out_shape=S((R, C), F32),
        scratch_shapes=[pltpu.VMEM((N_DEV, R, C), F32), pltpu.SemaphoreType.DMA((1, N_DEV - 1)),
                        pltpu.SemaphoreType.DMA((1, N_DEV - 1))],
    )(part)


def _pack(parts):
    rows = []
    for p in parts:
        f = p.reshape(-1).astype(F32)
        pad = (-f.shape[0]) % (8 * LANES)
        rows.append(jnp.pad(f, (0, pad)).reshape(-1, LANES))
    return jnp.concatenate(rows, axis=0)


def _unpack(buf, shapes):
    out, r = [], 0
    for shp in shapes:
        n = 1
        for s in shp:
            n *= s
        nr = -(-n // (8 * LANES)) * 8
        out.append(buf[r:r + nr].reshape(-1)[:n].reshape(shp))
        r += nr
    return out


def kernel(x, norm_w, w_in, conv_w, a_log, dt_bias, head_norm_w, sgu_ln_w, sgu_ln_b, w_spatial, b_spatial, w_out, final_norm_w, loss_target, m_norm_w, m_w_in, m_conv_w, m_a_log, m_dt_bias, m_head_norm_w, m_sgu_ln_w, m_sgu_ln_b, m_w_spatial, m_b_spatial, m_w_out, m_final_norm_w, v_norm_w, v_w_in, v_conv_w, v_a_log, v_dt_bias, v_head_norm_w, v_sgu_ln_w, v_sgu_ln_b, v_w_spatial, v_b_spatial, v_w_out, v_final_norm_w):
    T, DM = x.shape[1], x.shape[2]
    H, D = a_log.shape[1], head_norm_w.shape[1]
    G, P = w_spatial.shape[1], w_spatial.shape[2]
    AW, BW = H * D, G * P
    MIX = AW + BW
    WD = w_in.shape[2]
    IN = N_DEV * WD
    RO = w_out.shape[1]
    CW = conv_w.shape[2]
    sizes = (3 * AW, AW, H, H, BW, BW, BW)
    assert sum(sizes) == IN and 2 * H <= LANES and 3 * H <= 32 and N_DEV * RO == MIX and N_DEV * CW == 3 * AW
    offs = [0]
    for s in sizes:
        offs.append(offs[-1] + s)
    px, py, pc = _position()
    dev = 4 * px + 2 * py + pc
    chip = 2 * px + py

    x2, tgt = x[0], loss_target[0]

    core_idx = jnp.reshape(pc, (1,)).astype(jnp.int32)
    chip_idx = jnp.reshape(chip, (1,)).astype(jnp.int32)
    NM = IN - 2 * H
    tn_w, tile_plans, near = _tile_plans(WD, offs[2], offs[4], NM)
    far = [[t for t in range(len(tile_plans)) if t not in r] for r in near]
    sel = jnp.concatenate([chip_idx, jnp.asarray(near, jnp.int32)[chip]])
    xn, g_win, proj_part = _all_gather_weights(
        _cast_bf16_t(w_in[0].T, "cast_w_in"), x2, norm_w, sel, tn_w, tile_plans, near, NM)
    w_main, w_ba = _relayout_w(g_win, offs[2], offs[4])
    alog_row = jnp.pad(a_log, ((0, 0), (H, LANES - 2 * H)))
    dtb_row = jnp.pad(dt_bias, ((0, 0), (H, LANES - 2 * H)))
    bs_t = b_spatial[0].T

    proj, ba, (g_wout, g_conv) = _in_proj(xn, w_main, w_ba, proj_part, jnp.asarray(far, jnp.int32)[chip], tn_w,
                                          [_cast_bf16(w_out[0], "cast_w_out"), conv_w[0]])
    w_out_full = g_wout.reshape(MIX, DM)
    conv_full = g_conv.transpose(1, 0, 2).reshape(4, 3 * AW)
    q, k, v, c, gcol, grow = _prep_a_fwd(proj, ba, conv_full, alog_row, dtb_row, H, D)
    o, vnew, ssave, asave = _delta_fwd(q, k, v, gcol, grow, H, D)
    ocat = _mix_fwd(o, proj, head_norm_w, sgu_ln_w, sgu_ln_b, w_spatial[0], bs_t, H, D, G, P)
    dh, dh_bf, d_ocat, loss_acc, g_fnw = _out_proj_loss(ocat, w_out_full, x2, tgt, final_norm_w.reshape(1, DM))

    g_wout_blocks = _grad_w(ocat, dh_bf, "grad_w_out").reshape(4, 2, RO, DM)
    (d_o, dmain, g_hnw, g_ln, g_wsp, g_bs_t), (sib_wout,) = _mix_bwd(
        d_ocat, o, proj, head_norm_w, sgu_ln_w, sgu_ln_b, w_spatial[0], bs_t, H, D, G, P, [g_wout_blocks])
    chip_wout = _pair_sum(g_wout_blocks, sib_wout, core_idx, "pair_sum_w_out")
    (dq, dk, dv, dgate, dpar), (recv_wout,) = _delta_bwd(
        q, k, v, gcol, grow, ba, vnew, ssave, asave, d_o, alog_row, dtb_row, H, D, [chip_wout])
    dmain, g_conv_part = _prep_a_bwd(dq, dk, dv, c, proj, conv_full, dmain, H, D)
    dba = dgate.astype(BF16)
    keep_win, sib_win = _grad_w_in(xn, dmain, dba, WD, offs[2], offs[4])
    chip_win = _pair_sum_plain(keep_win, sib_win, "pair_sum_w_in")
    small_shapes = [a_log.shape, dt_bias.shape, head_norm_w.shape, sgu_ln_w.shape, sgu_ln_b.shape,
                    w_spatial.shape, b_spatial.shape, final_norm_w.shape]
    parts = [dpar[0, H:2 * H], dpar[1, H:2 * H], g_hnw[0], g_ln[0], g_ln[1], g_wsp, g_bs_t[:, :G].T, g_fnw[0],
             g_conv_part[:4], loss_acc[0, :1]]
    grad_x, g_nw, small_gath, recv_win = _dx(dmain, dba, w_main, w_ba, x2, dh, norm_w, chip_win, _pack(parts), 4)
    red = _sum_slots(small_gath)
    grad_w_in, delta_w_in, new_m_w_in, new_v_w_in = _sum_adam(
        chip_win, recv_win, w_in[0].T, m_w_in[0].T, v_w_in[0].T, chip_idx, "sum_adam_w_in", transposed=True)
    grad_w_out, delta_w_out, new_m_w_out, new_v_w_out = _sum_adam(
        chip_wout, recv_wout, w_out[0], m_w_out[0], v_w_out[0], chip_idx, "sum_adam_w_out")
    red_nw = _all_reduce_small(_pack([g_nw[0]]))
    grads_small = _unpack(red_nw, [norm_w.shape]) + _unpack(red, small_shapes + [(4, 3 * AW), (1,)])
    loss = grads_small.pop()[0]
    g_conv_full = grads_small.pop()
    grad_conv = lax.dynamic_slice_in_dim(g_conv_full, dev * CW, CW, axis=1)[None]
    small_w = [norm_w, a_log, dt_bias, head_norm_w, sgu_ln_w, sgu_ln_b, w_spatial, b_spatial, final_norm_w, conv_w]
    small_m = [m_norm_w, m_a_log, m_dt_bias, m_head_norm_w, m_sgu_ln_w, m_sgu_ln_b, m_w_spatial, m_b_spatial,
               m_final_norm_w, m_conv_w]
    small_v = [v_norm_w, v_a_log, v_dt_bias, v_head_norm_w, v_sgu_ln_w, v_sgu_ln_b, v_w_spatial, v_b_spatial,
               v_final_norm_w, v_conv_w]
    small_g = grads_small + [grad_conv]
    shapes10 = [w.shape for w in small_w]
    d_p, m_p, v_p = _adam_small(_pack(small_w), _pack(small_g), _pack(small_m), _pack(small_v))
    d_s, m_s, v_s = _unpack(d_p, shapes10), _unpack(m_p, shapes10), _unpack(v_p, shapes10)

    def order(small, win, wout):
        return [small[0], win.T[None], small[9], small[1], small[2], small[3], small[4], small[5], small[6], small[7],
                wout[None], small[8]]

    grads = order(small_g, grad_w_in, grad_w_out)
    deltas = order(d_s, delta_w_in, delta_w_out)
    new_m = order(m_s, new_m_w_in, new_m_w_out)
    new_v = order(v_s, new_v_w_in, new_v_w_out)
    return (loss, grad_x[None], *grads, *deltas, *new_m, *new_v)
```

```python
import jax
import jax.numpy as jnp
from jax import lax
from jax.experimental import pallas as pl
from jax.experimental.pallas import tpu as pltpu

F32 = jnp.float32
BF16 = jnp.bfloat16
MXU = jnp.bfloat16
HI = lax.Precision.HIGHEST
EPS = 1e-6
CHUNK_A = 64
LANES = 128
MESH = pl.DeviceIdType.MESH
N_DEV = 8

ADAM_LR = 0.001
ADAM_B1 = 0.9
ADAM_B2 = 0.999
ADAM_EPS = 1e-08
ADAM_WD = 0.01
ADAM_STEP = 10

S = jax.ShapeDtypeStruct
ARB = "arbitrary"


def _cp(*sem, vmem_mib=56):
    return pltpu.CompilerParams(dimension_semantics=tuple(sem), vmem_limit_bytes=vmem_mib * 1024 * 1024)


def _tile(n, cap, mult):
    best = None
    t = mult
    while t <= min(n, cap):
        if n % t == 0:
            best = t
        t += mult
    return best if best is not None else n


def _mm(a, b):
    return jnp.dot(a.astype(MXU), b.astype(MXU), preferred_element_type=F32)


def _mm_nt(a, b):
    return lax.dot_general(a.astype(MXU), b.astype(MXU), (((1,), (1,)), ((), ())), preferred_element_type=F32)


def _mm_tn(a, b):
    return lax.dot_general(a.astype(MXU), b.astype(MXU), (((0,), (0,)), ((), ())), preferred_element_type=F32)


def _mmh(a, b):
    return jnp.dot(a, b, precision=HI, preferred_element_type=F32)


def _sigmoid(x):
    return 1.0 / (1.0 + jnp.exp(-x))


def _silu(x):
    return x * _sigmoid(x)


def _dsilu(x):
    s = _sigmoid(x)
    return s * (1.0 + x * (1.0 - s))


def _softplus(x):
    return jnp.maximum(x, 0.0) + jnp.log(1.0 + jnp.exp(-jnp.abs(x)))


def _pieces(wd, gate_lo, gate_hi, total):
    out = []
    for d in range(N_DEV):
        lo, hi = d * wd, (d + 1) * wd
        for dest, a, b, shift in (("main", 0, gate_lo, 0), ("gate", gate_lo, gate_hi, -gate_lo),
                                  ("main", gate_hi, total, gate_lo - gate_hi)):
            s0, s1 = max(lo, a), min(hi, b)
            if s0 < s1:
                out.append((d, s0 - lo, s1 - lo, dest, s0 + shift))
    return out


def _local_tiles(wd, gate_lo, gate_hi, nm):
    n_tiles = N_DEV - 1
    assert nm % (n_tiles * LANES) == 0
    tn = nm // n_tiles
    plans = []
    for m in range(N_DEV // 2):
        lo, hi = 2 * m * tn, (2 * m + 1) * tn
        plan = []
        for d, s0, s1, dest, c0 in _pieces(wd, gate_lo, gate_hi, N_DEV * wd):
            if dest != "main":
                continue
            a, b = max(c0, lo), min(c0 + (s1 - s0), hi)
            if a < b:
                assert d // 2 == m, "tile 2m must come from chip m's own shards"
                plan.append((d, s0 + (a - c0), b - a, a - lo))
        assert sum(p[2] for p in plan) == tn
        plans.append(plan)
    return tn, plans


def _cast_bf16(a, name):
    R, C = a.shape
    tr = _tile(R, 256, 16)

    def body(a_ref, o_ref):
        o_ref[...] = a_ref[...].astype(BF16)

    spec = pl.BlockSpec((tr, C), lambda i: (i, 0))
    return pl.pallas_call(body, name=name, grid=(R // tr,), in_specs=[spec], out_specs=spec,
                          out_shape=S((R, C), BF16), compiler_params=_cp(ARB))(a)


def _cast_bf16_t(a_t, name):
    C, R = a_t.shape
    tr = _tile(R, 256, LANES)

    def body(a_ref, o_ref):
        o_ref[...] = a_ref[...].T.astype(BF16)

    return pl.pallas_call(body, name=name, grid=(R // tr,), in_specs=[pl.BlockSpec((C, tr), lambda i: (0, i))],
                          out_specs=pl.BlockSpec((tr, C), lambda i: (i, 0)),
                          out_shape=S((R, C), BF16), compiler_params=_cp(ARB))(a_t)


def _relayout_w(g_win, gate_lo, gate_hi):
    _, DM, WD = g_win.shape
    total = N_DEV * WD
    NM = total - (gate_hi - gate_lo)
    tr = _tile(DM, 256, 16)
    plan = _pieces(WD, gate_lo, gate_hi, total)

    def body(g_ref, main_ref, gate_ref):
        gate_ref[...] = jnp.zeros_like(gate_ref)
        for d, s0, s1, dest, c0 in plan:
            dst = main_ref if dest == "main" else gate_ref
            dst[:, c0:c0 + (s1 - s0)] = g_ref[d, :, s0:s1]

    return pl.pallas_call(
        body, name="relayout_w", grid=(DM // tr,),
        in_specs=[pl.BlockSpec((N_DEV, tr, WD), lambda i: (0, i, 0))],
        out_specs=[pl.BlockSpec((tr, NM), lambda i: (i, 0)), pl.BlockSpec((tr, LANES), lambda i: (i, 0))],
        out_shape=[S((DM, NM), g_win.dtype), S((DM, LANES), g_win.dtype)],
        compiler_params=_cp(ARB),
    )(g_win)


def _in_proj(xn, w_main, w_ba, proj_part, tiles, tn, shards):
    T, DM = xn.shape
    NM = w_main.shape[1]
    tm = _tile(T, 2048, 16)
    ni, nj = T // tm, tiles.shape[0]
    ns = len(shards)

    def body(tiles_ref, xn_ref, w_ref, wba_ref, part_ref, *rest):
        del tiles_ref, part_ref
        srcs = rest[:ns]
        proj_ref, ba_ref = rest[ns:ns + 2]
        gath = rest[ns + 2:2 * ns + 2]
        send_sems, recv_sems, local_sems = rest[2 * ns + 2:]
        i = pl.program_id(0)
        j = pl.program_id(1)
        me, cps = _broadcast_copies(srcs, gath, send_sems, recv_sems)
        cps = cps + [pltpu.make_async_copy(srcs[a], gath[a].at[me], local_sems.at[a]) for a in range(ns)]

        @pl.when((i == 0) & (j == 0))
        def _():
            for cp in cps:
                cp.start()

        @pl.when(j == 0)
        def _():
            ba_ref[...] = jnp.dot(xn_ref[...].astype(MXU), wba_ref[...].astype(MXU), preferred_element_type=F32)

        proj_ref[...] = jnp.dot(xn_ref[...].astype(MXU), w_ref[...].astype(MXU), preferred_element_type=F32)

        @pl.when((i == ni - 1) & (j == nj - 1))
        def _():
            for cp in cps:
                cp.wait()

    any_spec = pl.BlockSpec(memory_space=pl.ANY)
    res = pl.pallas_call(
        body, name="in_proj",
        grid_spec=pltpu.PrefetchScalarGridSpec(
            num_scalar_prefetch=1, grid=(ni, nj),
            in_specs=[pl.BlockSpec((tm, DM), lambda i, j, t: (i, 0)),
                      pl.BlockSpec((DM, tn), lambda i, j, t: (0, t[j])),
                      pl.BlockSpec((DM, LANES), lambda i, j, t: (0, 0)), any_spec] + [any_spec] * ns,
            out_specs=[pl.BlockSpec((tm, tn), lambda i, j, t: (i, t[j])),
                       pl.BlockSpec((tm, LANES), lambda i, j, t: (i, 0))] + [any_spec] * ns,
            scratch_shapes=[pltpu.SemaphoreType.DMA((ns, N_DEV - 1)), pltpu.SemaphoreType.DMA((ns, N_DEV - 1)),
                            pltpu.SemaphoreType.DMA((ns,))]),
        out_shape=[S((T, NM), F32), S((T, LANES), F32)] + [S((N_DEV,) + a.shape, a.dtype) for a in shards],
        input_output_aliases={4: 0},
        compiler_params=_cp(ARB, ARB, vmem_mib=58),
    )(tiles, xn, w_main, w_ba, proj_part, *shards)
    return res[0], res[1], res[2:]


def _prep_a_fwd(proj, ba, conv_w, alog_row, dtb_row, H, D):
    T = proj.shape[0]
    AW = H * D
    C3 = 3 * AW
    tb = _tile(T, 256, CHUNK_A)
    nch = tb // CHUNK_A
    nblk = T // tb
    scale = float(D) ** -0.5

    def body(x_ref, halo_ref, ba_ref, cw_ref, al_ref, dt_ref, q_ref, k_ref, v_ref, c_ref, gcol_ref, grow_ref):
        i = pl.program_id(0)
        xv = x_ref[...]
        halo = halo_ref[...] * (i > 0).astype(F32)
        xp = jnp.concatenate([halo, xv], axis=0)
        cw = cw_ref[...]
        c = cw[0:1, :] * xp[5:5 + tb]
        for j in range(1, 4):
            c = c + cw[j:j + 1, :] * xp[5 + j:5 + j + tb]
        c_ref[...] = c
        a = _silu(c)
        for h in range(H):
            qh = a[:, h * D:(h + 1) * D]
            kh = a[:, AW + h * D:AW + (h + 1) * D]
            qr = lax.rsqrt(jnp.sum(qh * qh, axis=-1, keepdims=True) + EPS)
            kr = lax.rsqrt(jnp.sum(kh * kh, axis=-1, keepdims=True) + EPS)
            q_ref[:, h * D:(h + 1) * D] = qh * (qr * scale)
            k_ref[:, h * D:(h + 1) * D] = kh * kr
        v_ref[...] = a[:, 2 * AW:]

        bav = ba_ref[...]
        lane = lax.broadcasted_iota(jnp.int32, (tb, LANES), 1)
        beta = _sigmoid(bav)
        g = -jnp.exp(al_ref[...]) * _softplus(bav + dt_ref[...])
        gates = jnp.where(lane < H, beta, jnp.where(lane < 2 * H, g, 0.0))
        ri = lax.broadcasted_iota(jnp.int32, (CHUNK_A, CHUNK_A), 0)
        ci = lax.broadcasted_iota(jnp.int32, (CHUNK_A, CHUNK_A), 1)
        tri = (ri >= ci).astype(F32)
        lane_c = lax.broadcasted_iota(jnp.int32, (CHUNK_A, LANES), 1)
        for cc in range(nch):
            gch = gates[cc * CHUNK_A:(cc + 1) * CHUNK_A]
            gc = pltpu.roll(_mmh(tri, gch), H, 1)
            full = jnp.where(lane_c < 2 * H, gch, jnp.where(lane_c < 3 * H, gc, 0.0))
            gcol_ref[cc * CHUNK_A:(cc + 1) * CHUNK_A, :] = full
            grow_ref[cc] = full.T[0:32, :]

    return pl.pallas_call(
        body, name="prep_a_fwd", grid=(nblk,),
        in_specs=[pl.BlockSpec((tb, C3), lambda i: (i, 0)),
                  pl.BlockSpec((8, C3), lambda i: (jnp.maximum(i * (tb // 8) - 1, 0), 0)),
                  pl.BlockSpec((tb, LANES), lambda i: (i, 0)),
                  pl.BlockSpec((4, C3), lambda i: (0, 0)),
                  pl.BlockSpec((1, LANES), lambda i: (0, 0)),
                  pl.BlockSpec((1, LANES), lambda i: (0, 0))],
        out_specs=[pl.BlockSpec((tb, AW), lambda i: (i, 0)),
                   pl.BlockSpec((tb, AW), lambda i: (i, 0)),
                   pl.BlockSpec((tb, AW), lambda i: (i, 0)),
                   pl.BlockSpec((tb, C3), lambda i: (i, 0)),
                   pl.BlockSpec((tb, LANES), lambda i: (i, 0)),
                   pl.BlockSpec((nch, 32, CHUNK_A), lambda i: (i, 0, 0))],
        out_shape=[S((T, AW), F32), S((T, AW), F32), S((T, AW), F32), S((T, C3), F32),
                   S((T, LANES), F32), S((T // CHUNK_A, 32, CHUNK_A), F32)],
        compiler_params=_cp(ARB),
    )(proj, proj, ba, conv_w, alog_row, dtb_row)


_NN = (((1,), (0,)), ((), ()))
_TN = (((0,), (0,)), ((), ()))


def _split(a):
    hi = a.astype(BF16)
    return hi, (a - hi.astype(F32)).astype(BF16)


def _mm3(a, b, dims=_NN):
    ah, al = a if isinstance(a, tuple) else _split(a)
    bh, bl = b if isinstance(b, tuple) else _split(b)
    dg = lambda p, r: lax.dot_general(p, r, dims, preferred_element_type=F32)
    return dg(ah, bh) + (dg(ah, bl) + dg(al, bh))


def _interleave(gens):
    gens = list(gens)
    while gens:
        alive = []
        for g in gens:
            try:
                next(g)
                alive.append(g)
            except StopIteration:
                pass
        gens = alive


def _chunk_terms(q, k, v, gcolv, growv, h, H):
    C = CHUNK_A
    beta_c = gcolv[:, h:h + 1]
    g_c = gcolv[:, H + h:H + h + 1]
    gc_c = gcolv[:, 2 * H + h:2 * H + h + 1]
    gc_r = growv[2 * H + h:2 * H + h + 1, :]
    ri = lax.broadcasted_iota(jnp.int32, (C, C), 0)
    ci = lax.broadcasted_iota(jnp.int32, (C, C), 1)
    incl = ri >= ci
    strict = ri > ci
    kb = k * beta_c
    vb = v * beta_c
    p_raw = _mm_nt(kb, k)
    qk_raw = _mm_nt(q, k)
    gam = jnp.where(incl, jnp.exp(jnp.where(incl, gc_c - gc_r, 0.0)), 0.0)
    e_c = jnp.exp(gc_c)
    gl = gc_r[:, C - 1:C]
    edec = jnp.exp(gl - gc_c)
    yield
    lmat = jnp.where(strict, p_raw * gam, 0.0)
    attn = jnp.where(incl, qk_raw * gam, 0.0)
    return dict(beta_c=beta_c, g_c=g_c, gc_c=gc_c, gc_r=gc_r, incl=incl, strict=strict, gam=gam, e_c=e_c,
                kb=kb, vb=vb, lmat=lmat, attn=attn, gl=gl, edec=edec, ri=ri, ci=ci)


INV_BLOCK = 16


def _inv_unit_lower(lmat):
    C = lmat.shape[0]
    ri = lax.broadcasted_iota(jnp.int32, (C, C), 0)
    ci = lax.broadcasted_iota(jnp.int32, (C, C), 1)
    eye = (ri == ci).astype(F32)
    same = (ri // INV_BLOCK) == (ci // INV_BLOCK)

    def neumann(x, order):
        a = eye + x
        n = 1
        while 2 * n < order:
            xs = _split(x)
            x = _mm3(xs, xs)
            yield
            a = a + _mm3(a, x)
            n *= 2
        yield
        return a

    inv_d = yield from neumann(-jnp.where(same, lmat, 0.0), INV_BLOCK)
    m = _mm3(inv_d, jnp.where(same, 0.0, lmat))
    yield
    inv_m = yield from neumann(-m, C // INV_BLOCK)
    a = _mm3(inv_m, inv_d)
    yield
    return a


def _delta_fwd(q, k, v, gcol, grow, H, D):
    T = q.shape[0]
    C = CHUNK_A
    N = T // C
    AW = H * D
    CPS = 2 if N % 2 == 0 else 1

    def body(q_ref, k_ref, v_ref, gcol_ref, grow_ref, o_ref, vn_ref, ssave_ref, asave_ref, s_ref):
        @pl.when(pl.program_id(0) == 0)
        def _():
            s_ref[...] = jnp.zeros_like(s_ref)

        state = {(0, h): s_ref[h] for h in range(H)}

        def head(cc, h):
            rows = slice(cc * C, (cc + 1) * C)
            sl = slice(h * D, (h + 1) * D)
            qv, kv, vv = q_ref[rows, sl], k_ref[rows, sl], v_ref[rows, sl]
            t = yield from _chunk_terms(qv, kv, vv, gcol_ref[rows, :], grow_ref[cc], h, H)
            a = yield from _inv_unit_lower(t["lmat"])
            asave_ref[cc, h] = a
            while (cc, h) not in state:
                yield
            st = state[(cc, h)]
            ssave_ref[cc, h] = st
            ks = _mm(t["kb"] * t["e_c"], st)
            o_inter = _mm(qv * t["e_c"], st)
            yield
            v_new = _mm3(a, t["vb"] - ks)
            yield
            vn_ref[rows, sl] = v_new
            o_intra = _mm(t["attn"], v_new)
            s_upd = _mm_tn(kv * t["edec"], v_new)
            yield
            o_ref[rows, sl] = o_inter + o_intra
            state[(cc + 1, h)] = st * jnp.exp(t["gl"]) + s_upd

        _interleave(head(cc, h) for cc in range(CPS) for h in range(H))
        for h in range(H):
            s_ref[h] = state[(CPS, h)]

    blk = lambda: pl.BlockSpec((CPS * C, AW), lambda n: (n, 0))
    return pl.pallas_call(
        body, name="delta_fwd", grid=(N // CPS,),
        in_specs=[blk(), blk(), blk(),
                  pl.BlockSpec((CPS * C, LANES), lambda n: (n, 0)),
                  pl.BlockSpec((CPS, 32, C), lambda n: (n, 0, 0))],
        out_specs=[blk(), blk(),
                   pl.BlockSpec((CPS, H, D, D), lambda n: (n, 0, 0, 0)),
                   pl.BlockSpec((CPS, H, C, C), lambda n: (n, 0, 0, 0))],
        out_shape=[S((T, AW), F32), S((T, AW), F32), S((N, H, D, D), F32), S((N, H, C, C), F32)],
        scratch_shapes=[pltpu.VMEM((H, D, D), F32)],
        compiler_params=_cp(ARB),
    )(q, k, v, gcol, grow)


def _delta_bwd(q, k, v, gcol, grow, ba, vnew, ssave, asave, d_o, a_log, dt_bias, H, D, carry):
    T = q.shape[0]
    C = CHUNK_A
    N = T // C
    AW = H * D
    nc = len(carry)
    CPS = 2 if N % 2 == 0 else 1
    NS = N // CPS

    def body(al_ref, dt_ref, q_ref, k_ref, v_ref, gcol_ref, grow_ref, ba_ref, vn_ref, ss_ref, as_ref, do_ref, *rest):
        cins = rest[:nc]
        dq_ref, dk_ref, dv_ref, dgate_ref, dpar_ref = rest[nc:nc + 5]
        couts = rest[nc + 5:2 * nc + 5]
        ds_ref, csend, crecv = rest[2 * nc + 5:]
        ccps = _chip_exchange_copies(cins, couts, csend, crecv)

        @pl.when(pl.program_id(0) == 0)
        def _():
            ds_ref[...] = jnp.zeros_like(ds_ref)
            dpar_ref[...] = jnp.zeros_like(dpar_ref)
            for cp in ccps:
                cp.start()

        lane = lax.broadcasted_iota(jnp.int32, (C, LANES), 1)
        rowi = lax.broadcasted_iota(jnp.int32, (C, 1), 0)
        acc = {cc: jnp.zeros((C, LANES), F32) for cc in range(CPS)}
        state = {(0, h): ds_ref[h] for h in range(H)}

        def head(oi, h):
            cc = CPS - 1 - oi
            rows = slice(cc * C, (cc + 1) * C)
            sl = slice(h * D, (h + 1) * D)
            st = ss_ref[cc, h]
            a = as_ref[cc, h]
            qv, kv, vv, dov, v_new = q_ref[rows, sl], k_ref[rows, sl], v_ref[rows, sl], do_ref[rows, sl], vn_ref[rows, sl]
            t = yield from _chunk_terms(qv, kv, vv, gcol_ref[rows, :], grow_ref[cc], h, H)
            beta_c, e_c, gam, kb = t["beta_c"], t["e_c"], t["gam"], t["kb"]
            incl, strict, attn, lmat, edec = t["incl"], t["strict"], t["attn"], t["lmat"], t["edec"]
            kdec = kv * edec
            egl = jnp.exp(t["gl"])
            qe = qv * e_c
            ekb = kb * e_c

            t1 = _mm_nt(dov, st)
            ds_o = _mm_tn(qe, dov)
            dattn_raw = _mm_nt(dov, v_new)
            dv_new_o = _mm_tn(attn, dov)
            yield
            while (oi, h) not in state:
                yield
            ds_next = state[(oi, h)]
            dkdec = _mm_nt(v_new, ds_next)
            dv_new_s = _mm(kdec, ds_next)
            yield
            dgl = egl * jnp.sum(jnp.sum(st * ds_next, axis=1, keepdims=True), axis=0, keepdims=True)
            dk = edec * dkdec
            r = jnp.sum(dkdec * kdec, axis=1, keepdims=True)
            dgc = -r
            dgl = dgl + jnp.sum(r, axis=0, keepdims=True)
            dq = e_c * t1
            dgc = dgc + jnp.sum(t1 * qe, axis=1, keepdims=True)
            dattn = jnp.where(incl, dattn_raw, 0.0)
            dv_new = dv_new_s + dv_new_o
            dqm = dattn * gam
            z = dattn * attn
            dvb = _mm3(a, dv_new, _TN)
            dq_a = _mm(dqm, kv)
            dk_a = _mm_tn(dqm, qv)
            yield
            dq_ref[rows, sl] = dq + dq_a
            dv_ref[rows, sl] = beta_c * dvb
            ds_kb = _mm_tn(ekb, dvb)
            dekb_neg = _mm_nt(dvb, st)
            dl_neg = _mm_nt(dvb, v_new)
            yield
            state[(oi + 1, h)] = egl * ds_next + ds_o - ds_kb
            dekb = -dekb_neg
            dl = jnp.where(strict, -dl_neg, 0.0)
            dp = dl * gam
            z = z + dl * lmat
            dkb_p = _mm(dp, kv)
            dk_p = _mm_tn(dp, kb)
            dgc = dgc + jnp.sum(dekb * ekb, axis=1, keepdims=True)
            dgc = dgc + jnp.sum(z, axis=1, keepdims=True) - jnp.sum(z.T, axis=1, keepdims=True)
            dgc = dgc + jnp.where(rowi == C - 1, dgl, 0.0)
            yield
            dkb = dkb_p + e_c * dekb
            dk_ref[rows, sl] = dk + dk_a + dk_p + beta_c * dkb
            dbeta = jnp.sum(dkb * kv, axis=1, keepdims=True) + jnp.sum(dvb * vv, axis=1, keepdims=True)
            acc[cc] = acc[cc] + jnp.where(lane == h, dbeta, 0.0) + jnp.where(lane == H + h, dgc, 0.0)

        _interleave(head(oi, h) for oi in range(CPS) for h in range(H))
        for h in range(H):
            ds_ref[h] = state[(CPS, h)]
        ri = lax.broadcasted_iota(jnp.int32, (C, C), 0)
        ci = lax.broadcasted_iota(jnp.int32, (C, C), 1)
        upper = (ri <= ci).astype(F32)
        dal = jnp.zeros((1, LANES), F32)
        ddt = jnp.zeros((1, LANES), F32)
        for cc in range(CPS):
            rows = slice(cc * C, (cc + 1) * C)
            gates = gcol_ref[rows, :]
            dg_all = _mm3(upper, acc[cc])
            d_braw = acc[cc] * gates * (1.0 - gates)
            d_araw = dg_all * (-jnp.exp(al_ref[...])) * _sigmoid(ba_ref[rows, :] + dt_ref[...])
            dgate_ref[rows, :] = jnp.where(lane < H, d_braw, jnp.where(lane < 2 * H, d_araw, 0.0))
            dal = dal + jnp.sum(dg_all * gates, axis=0, keepdims=True)
            ddt = ddt + jnp.sum(d_araw, axis=0, keepdims=True)
        dpar_ref[0:1, :] += dal
        dpar_ref[1:2, :] += ddt

        @pl.when(pl.program_id(0) == NS - 1)
        def _():
            for cp in ccps:
                cp.wait()

    rev = lambda s: NS - 1 - s
    blk = lambda: pl.BlockSpec((CPS * C, AW), lambda s: (rev(s), 0))
    row = pl.BlockSpec((1, LANES), lambda s: (0, 0))
    any_spec = pl.BlockSpec(memory_space=pl.ANY)
    res = pl.pallas_call(
        body, name="delta_bwd", grid=(NS,),
        in_specs=[row, row, blk(), blk(), blk(),
                  pl.BlockSpec((CPS * C, LANES), lambda s: (rev(s), 0)),
                  pl.BlockSpec((CPS, 32, C), lambda s: (rev(s), 0, 0)),
                  pl.BlockSpec((CPS * C, LANES), lambda s: (rev(s), 0)),
                  blk(),
                  pl.BlockSpec((CPS, H, D, D), lambda s: (rev(s), 0, 0, 0)),
                  pl.BlockSpec((CPS, H, C, C), lambda s: (rev(s), 0, 0, 0)),
                  blk()] + [any_spec] * nc,
        out_specs=[blk(), blk(), blk(),
                   pl.BlockSpec((CPS * C, LANES), lambda s: (rev(s), 0)),
                   pl.BlockSpec((8, LANES), lambda s: (0, 0))] + [any_spec] * nc,
        out_shape=[S((T, AW), F32), S((T, AW), F32), S((T, AW), F32),
                   S((T, LANES), F32), S((8, LANES), F32)] + [S((3,) + a.shape[1:], a.dtype) for a in carry],
        scratch_shapes=[pltpu.VMEM((H, D, D), F32),
                        pltpu.SemaphoreType.DMA((max(nc, 1), 3)), pltpu.SemaphoreType.DMA((max(nc, 1), 3))],
        compiler_params=_cp(ARB),
    )(a_log, dt_bias, q, k, v, gcol, grow, ba, vnew, ssave, asave, d_o, *carry)
    return res[:5], res[5:]


def _ln_stats(xv):
    mu = jnp.mean(xv, axis=-1, keepdims=True)
    xc = xv - mu
    var = jnp.mean(xc * xc, axis=-1, keepdims=True)
    rstd = lax.rsqrt(var + EPS)
    return xc * rstd, rstd


def _mix_fwd(o, proj, head_norm_w, ln_w, ln_b, w_sp, bs_t, H, D, G, P):
    T = o.shape[0]
    AW, BW = H * D, G * P
    MIX = AW + BW
    nb = AW // BW if AW % BW == 0 else None
    assert nb == 1, "group widths must match the projection column blocks"
    cb = 3

    def body(o_ref, za_ref, ub_ref, vb_ref, zb_ref, hw_ref, lw_ref, lb_ref, w_ref, bs_ref, out_ref):
        hw = hw_ref[...]
        for h in range(H):
            sl = slice(h * D, (h + 1) * D)
            oh = o_ref[:, sl]
            rs = lax.rsqrt(jnp.mean(oh * oh, axis=-1, keepdims=True) + EPS)
            out_ref[:, sl] = (oh * rs * hw * _silu(za_ref[:, sl])).astype(BF16)
        xhat, _ = _ln_stats(vb_ref[...])
        vn = xhat * lw_ref[...] + lb_ref[...]
        ri = lax.broadcasted_iota(jnp.int32, (P, P), 0)
        ci = lax.broadcasted_iota(jnp.int32, (P, P), 1)
        bsv = bs_ref[...]
        for g in range(G):
            sl = slice(g * P, (g + 1) * P)
            wm = jnp.where(ri >= ci, w_ref[g], 0.0)
            s = _mm(wm, vn[:, sl]) + bsv[:, g:g + 1]
            out_ref[:, AW + g * P:AW + (g + 1) * P] = (ub_ref[:, sl] * s * _silu(zb_ref[:, sl])).astype(BF16)

    row = lambda w: pl.BlockSpec((1, w), lambda i: (0, 0))
    return pl.pallas_call(
        body, name="mix_fwd", grid=(T // P,),
        in_specs=[pl.BlockSpec((P, AW), lambda i: (i, 0)),
                  pl.BlockSpec((P, AW), lambda i: (i, cb)),
                  pl.BlockSpec((P, BW), lambda i: (i, cb + 1)),
                  pl.BlockSpec((P, BW), lambda i: (i, cb + 2)),
                  pl.BlockSpec((P, BW), lambda i: (i, cb + 3)),
                  row(D), row(BW), row(BW),
                  pl.BlockSpec((G, P, P), lambda i: (0, 0, 0)),
                  pl.BlockSpec((P, G), lambda i: (0, 0))],
        out_specs=pl.BlockSpec((P, MIX), lambda i: (i, 0)),
        out_shape=S((T, MIX), BF16),
        compiler_params=_cp(ARB),
    )(o, proj, proj, proj, proj, head_norm_w, ln_w, ln_b, w_sp, bs_t)


def _mix_bwd(d_ocat, o, proj, head_norm_w, ln_w, ln_b, w_sp, bs_t, H, D, G, P, carry):
    T = o.shape[0]
    AW, BW = H * D, G * P
    MIX = AW + BW
    cb = 3
    nc = len(carry)

    def body(dc_ref, o_ref, za_ref, ub_ref, vb_ref, zb_ref, hw_ref, lw_ref, lb_ref, w_ref, bs_ref, *rest):
        cins = rest[:nc]
        do_ref, dmain_ref, dhw_ref, dln_ref, dw_ref, dbs_ref = rest[nc:nc + 6]
        couts = rest[nc + 6:2 * nc + 6]
        dvn_ref, drest_ref, out_sems, csend, crecv = rest[2 * nc + 6:]
        i = pl.program_id(0)
        slot = lax.rem(i, 2)
        ccps = _sibling_copies(cins, couts, csend, crecv)

        def out_copy(step, s):
            return pltpu.make_async_copy(
                drest_ref.at[s], dmain_ref.at[pl.ds(step * P, P), pl.ds(cb * AW, AW + 3 * BW)], out_sems.at[s])

        @pl.when(i == 0)
        def _():
            dhw_ref[...] = jnp.zeros_like(dhw_ref)
            dln_ref[...] = jnp.zeros_like(dln_ref)
            dw_ref[...] = jnp.zeros_like(dw_ref)
            dbs_ref[...] = jnp.zeros_like(dbs_ref)
            for cp in ccps:
                cp.start()

        @pl.when(i >= 2)
        def _():
            out_copy(i - 2, slot).wait()

        hw = hw_ref[...]
        dhw = jnp.zeros((1, D), F32)
        for h in range(H):
            sl = slice(h * D, (h + 1) * D)
            oh = o_ref[:, sl]
            za = za_ref[:, sl]
            doa = dc_ref[:, sl]
            rs = lax.rsqrt(jnp.mean(oh * oh, axis=-1, keepdims=True) + EPS)
            xh = oh * rs
            d_on = doa * _silu(za)
            drest_ref[slot, :, sl] = (doa * (xh * hw) * _dsilu(za)).astype(BF16)
            dhw = dhw + jnp.sum(d_on * xh, axis=0, keepdims=True)
            dxh = d_on * hw
            do_ref[:, sl] = rs * (dxh - xh * jnp.mean(dxh * xh, axis=-1, keepdims=True))
        dhw_ref[0:1, :] += dhw

        xhat, rstd = _ln_stats(vb_ref[...])
        lw = lw_ref[...]
        vn = xhat * lw + lb_ref[...]
        ri = lax.broadcasted_iota(jnp.int32, (P, P), 0)
        ci = lax.broadcasted_iota(jnp.int32, (P, P), 1)
        lane = lax.broadcasted_iota(jnp.int32, (P, LANES), 1)
        bsv = bs_ref[...]
        dbs = jnp.zeros((P, LANES), F32)
        for g in range(G):
            sl = slice(g * P, (g + 1) * P)
            wm = jnp.where(ri >= ci, w_ref[g], 0.0)
            vng = vn[:, sl]
            s = _mm(wm, vng) + bsv[:, g:g + 1]
            dob = dc_ref[:, AW + g * P:AW + (g + 1) * P]
            ub = ub_ref[:, sl]
            zb = zb_ref[:, sl]
            szb = _silu(zb)
            drest_ref[slot, :, AW + g * P:AW + (g + 1) * P] = (dob * s * szb).astype(BF16)
            drest_ref[slot, :, AW + 2 * BW + g * P:AW + 2 * BW + (g + 1) * P] = (
                dob * ub * s * _dsilu(zb)).astype(BF16)
            ds = dob * ub * szb
            dvn_ref[:, sl] = _mm_tn(wm, ds)
            dw_ref[g] += jnp.where(ri >= ci, _mm_nt(ds, vng), 0.0)
            dbs = dbs + jnp.where(lane == g, jnp.sum(ds, axis=1, keepdims=True), 0.0)
        dbs_ref[...] += dbs
        dvn = dvn_ref[...]
        dln_ref[0:1, :] += jnp.sum(dvn * xhat, axis=0, keepdims=True)
        dln_ref[1:2, :] += jnp.sum(dvn, axis=0, keepdims=True)
        dxh = dvn * lw
        dvb = rstd * (dxh - jnp.mean(dxh, axis=-1, keepdims=True) - xhat * jnp.mean(dxh * xhat, axis=-1, keepdims=True))
        drest_ref[slot, :, AW + BW:AW + 2 * BW] = dvb.astype(BF16)

        out_copy(i, slot).start()

        @pl.when(i == nstep - 1)
        def _():
            out_copy(i, slot).wait()
            if nstep > 1:
                out_copy(i - 1, 1 - slot).wait()
            for cp in ccps:
                cp.wait()

    nstep = T // P
    row = lambda w: pl.BlockSpec((1, w), lambda i: (0, 0))
    any_spec = pl.BlockSpec(memory_space=pl.ANY)
    res = pl.pallas_call(
        body, name="mix_bwd", grid=(nstep,),
        in_specs=[pl.BlockSpec((P, MIX), lambda i: (i, 0)),
                  pl.BlockSpec((P, AW), lambda i: (i, 0)),
                  pl.BlockSpec((P, AW), lambda i: (i, cb)),
                  pl.BlockSpec((P, BW), lambda i: (i, cb + 1)),
                  pl.BlockSpec((P, BW), lambda i: (i, cb + 2)),
                  pl.BlockSpec((P, BW), lambda i: (i, cb + 3)),
                  row(D), row(BW), row(BW),
                  pl.BlockSpec((G, P, P), lambda i: (0, 0, 0)),
                  pl.BlockSpec((P, G), lambda i: (0, 0))] + [any_spec] * nc,
        out_specs=[pl.BlockSpec((P, AW), lambda i: (i, 0)),
                   any_spec,
                   pl.BlockSpec((8, D), lambda i: (0, 0)),
                   pl.BlockSpec((8, BW), lambda i: (0, 0)),
                   pl.BlockSpec((G, P, P), lambda i: (0, 0, 0)),
                   pl.BlockSpec((P, LANES), lambda i: (0, 0))] + [any_spec] * nc,
        out_shape=[S((T, AW), F32), S((T, cb * AW + AW + 3 * BW), BF16), S((8, D), F32), S((8, BW), F32),
                   S((G, P, P), F32), S((P, LANES), F32)] + [S(a.shape[:1] + a.shape[2:], a.dtype) for a in carry],
        scratch_shapes=[pltpu.VMEM((P, BW), F32), pltpu.VMEM((2, P, AW + 3 * BW), BF16),
                        pltpu.SemaphoreType.DMA((2,))] + _sibling_sems(carry),
        compiler_params=_cp(ARB),
    )(d_ocat, o, proj, proj, proj, proj, head_norm_w, ln_w, ln_b, w_sp, bs_t, *carry)
    return res[:6], res[6:]


def _out_proj_loss(ocat, w_out, x, target, fnw):
    T, MIX = ocat.shape
    DM = x.shape[1]
    tm = _tile(T, 256, 8)

    def body(oc_ref, w_ref, x_ref, t_ref, fw_ref, dh_ref, dhb_ref, doc_ref, loss_ref, gfw_ref):
        @pl.when(pl.program_id(0) == 0)
        def _():
            loss_ref[...] = jnp.zeros_like(loss_ref)
            gfw_ref[...] = jnp.zeros_like(gfw_ref)

        wv = w_ref[...]
        hh = x_ref[...] + jnp.dot(oc_ref[...].astype(MXU), wv.astype(MXU), preferred_element_type=F32)
        rs = lax.rsqrt(jnp.mean(hh * hh, axis=-1, keepdims=True) + EPS)
        hn = hh * rs
        fw = fw_ref[...]
        e = hn * fw - t_ref[...]
        row_loss = 0.5 * jnp.mean(e * e, axis=-1, keepdims=True)
        loss_ref[...] += jnp.sum(row_loss, axis=0, keepdims=True)
        dy = e * (1.0 / DM)
        gfw_ref[0:1, :] += jnp.sum(dy * hn, axis=0, keepdims=True)
        dhn = dy * fw
        dh = rs * (dhn - hn * jnp.mean(dhn * hn, axis=-1, keepdims=True))
        dh_ref[...] = dh
        dhb = dh.astype(BF16)
        dhb_ref[...] = dhb
        doc_ref[...] = _mm_nt(dhb, wv)

    return pl.pallas_call(
        body, name="out_proj_loss", grid=(T // tm,),
        in_specs=[pl.BlockSpec((tm, MIX), lambda i: (i, 0)),
                  pl.BlockSpec((MIX, DM), lambda i: (0, 0)),
                  pl.BlockSpec((tm, DM), lambda i: (i, 0)),
                  pl.BlockSpec((tm, DM), lambda i: (i, 0)),
                  pl.BlockSpec((1, DM), lambda i: (0, 0))],
        out_specs=[pl.BlockSpec((tm, DM), lambda i: (i, 0)),
                   pl.BlockSpec((tm, DM), lambda i: (i, 0)),
                   pl.BlockSpec((tm, MIX), lambda i: (i, 0)),
                   pl.BlockSpec((8, LANES), lambda i: (0, 0)),
                   pl.BlockSpec((8, DM), lambda i: (0, 0))],
        out_shape=[S((T, DM), F32), S((T, DM), BF16), S((T, MIX), F32), S((8, LANES), F32), S((8, DM), F32)],
        compiler_params=_cp(ARB),
    )(ocat, w_out, x, target, fnw)


def _grad_w(lhs, rhs, name):
    T, A = lhs.shape
    B = rhs.shape[1]
    ta = _tile(A, 512, LANES)
    tk = _tile(T, 1024, 16)
    nk = T // tk

    def body(l_ref, r_ref, out_ref, acc_ref):
        k = pl.program_id(1)
        part = _mm_tn(l_ref[...], r_ref[...])

        @pl.when(k == 0)
        def _():
            acc_ref[...] = part

        @pl.when(k > 0)
        def _():
            acc_ref[...] += part

        @pl.when(k == nk - 1)
        def _():
            out_ref[...] = acc_ref[...].astype(BF16)

    return pl.pallas_call(
        body, name=name, grid=(A // ta, nk),
        in_specs=[pl.BlockSpec((tk, ta), lambda i, k: (k, i)),
                  pl.BlockSpec((tk, B), lambda i, k: (k, 0))],
        out_specs=pl.BlockSpec((ta, B), lambda i, k: (i, 0)),
        out_shape=S((A, B), BF16),
        scratch_shapes=[pltpu.VMEM((ta, B), F32)],
        compiler_params=_cp(ARB, ARB),
    )(lhs, rhs)


def _grad_w_in(xn, dmain, dba, WD, gate_lo, gate_hi):
    T, DM = xn.shape
    NM = dmain.shape[1]
    tn = _tile(NM, 1024, LANES)
    tk = _tile(T, 2048, 16)
    nj, nk = NM // tn, T // tk
    ND = N_DEV
    tiles = [[] for _ in range(nj)]
    first_tile, last_tile = {}, {}
    for d, s0, s1, dest, c0 in _pieces(WD, gate_lo, gate_hi, ND * WD):
        if dest != "main":
            continue
        while s0 < s1:
            jj = c0 // tn
            w = min(s1 - s0, (jj + 1) * tn - c0)
            tiles[jj].append((d, s0, w, "main", c0 - jj * tn))
            first_tile.setdefault(d, jj)
            last_tile[d] = jj
            s0, c0 = s0 + w, c0 + w
    for d, s0, s1, dest, c0 in _pieces(WD, gate_lo, gate_hi, ND * WD):
        if dest == "gate":
            tiles[first_tile[d]].append((d, s0, s1 - s0, "gate", c0))
    assert sorted(first_tile) == list(range(ND)) and all(last_tile[d] <= first_tile[d + 2] for d in range(ND - 2))

    def body(xn_ref, dm_ref, dba_ref, keep_ref, recv_ref, acc_ref, gate_ref, buf_ref, lsem, ssem, rsem):
        j = pl.program_id(0)
        k = pl.program_id(1)
        px, py, pc = _position()

        @pl.when(k == 0)
        def _():
            acc_ref[...] = jnp.zeros_like(acc_ref)

        @pl.when((j == 0) & (k == 0))
        def _():
            gate_ref[...] = jnp.zeros_like(gate_ref)

        xv = xn_ref[...]
        acc_ref[...] += _mm_tn(xv, dm_ref[...])

        @pl.when(j == 0)
        def _():
            gate_ref[...] += _mm_tn(xv, dba_ref[...])

        def local(d):
            return pltpu.make_async_copy(buf_ref.at[d % 2], keep_ref.at[d // 2], lsem.at[d // 2])

        def remote(d):
            return pltpu.make_async_remote_copy(
                src_ref=buf_ref.at[d % 2], dst_ref=recv_ref.at[d // 2], send_sem=ssem.at[d // 2],
                recv_sem=rsem.at[d // 2], device_id=(px, py, 1 - pc), device_id_type=MESH)

        def leave(d, start):
            @pl.when(pc == d % 2)
            def _():
                local(d).start() if start else local(d).wait()

            @pl.when(pc != d % 2)
            def _():
                remote(d).start() if start else remote(d).wait_send()

        def emit(jj):
            shards = sorted({p[0] for p in tiles[jj]})
            for d in shards:
                if first_tile[d] == jj and d >= 2:
                    leave(d - 2, False)
                for dd, s0, w, src, c0 in tiles[jj]:
                    if dd == d:
                        ref = acc_ref if src == "main" else gate_ref
                        buf_ref[d % 2, :, s0:s0 + w] = ref[:, c0:c0 + w].astype(BF16)
                if last_tile[d] == jj:
                    leave(d, True)
            if jj == nj - 1:
                for d in (ND - 2, ND - 1):
                    leave(d, False)
                for q in range(ND // 2):
                    remote(2 * q).wait_recv()

        for jj in range(nj):
            @pl.when((j == jj) & (k == nk - 1))
            def _(jj=jj):
                emit(jj)

    any_spec = pl.BlockSpec(memory_space=pl.ANY)
    return pl.pallas_call(
        body, name="grad_w_in", grid=(nj, nk),
        in_specs=[pl.BlockSpec((tk, DM), lambda j, k: (k, 0)),
                  pl.BlockSpec((tk, tn), lambda j, k: (k, j)),
                  pl.BlockSpec((tk, LANES), lambda j, k: (k, 0))],
        out_specs=[any_spec, any_spec],
        out_shape=[S((ND // 2, DM, WD), BF16), S((ND // 2, DM, WD), BF16)],
        scratch_shapes=[pltpu.VMEM((DM, tn), F32), pltpu.VMEM((DM, LANES), F32), pltpu.VMEM((2, DM, WD), BF16),
                        pltpu.SemaphoreType.DMA((ND // 2,)), pltpu.SemaphoreType.DMA((ND // 2,)),
                        pltpu.SemaphoreType.DMA((ND // 2,))],
        compiler_params=_cp(ARB, ARB),
    )(xn, dmain, dba)


def _pair_sum_plain(a, b, name):
    K, R, C = a.shape
    tr = _tile(R, 1024, 16)

    def body(a_ref, b_ref, o_ref):
        o_ref[...] = (a_ref[...].astype(F32) + b_ref[...].astype(F32)).astype(BF16)

    spec = lambda: pl.BlockSpec((1, tr, C), lambda q, i: (q, i, 0))
    return pl.pallas_call(body, name=name, grid=(K, R // tr), in_specs=[spec(), spec()], out_specs=spec(),
                          out_shape=S((K, R, C), BF16), compiler_params=_cp(ARB, ARB))(a, b)


def _dx_rows(T):
    tm = _tile(T, 512, 8)
    return tm if T // tm >= 2 else T // 2


def _dx_part(name, dmain, dba, w_main, w_ba, x, dh, norm_w, blk0, nblk, prev, hbm_in, hbm_alias, hbm_new, make_copies):
    T, NM = dmain.shape
    DM = x.shape[1]
    tm = _dx_rows(T)
    tk = _tile(NM, 1024, LANES)
    nk = NM // tk
    n_in, n_al, n_new = len(hbm_in), len(hbm_alias), len(hbm_new)
    n_prev = 0 if prev is None else 2
    last_step = nblk * nk - 1

    def body(dm_ref, dba_ref, w_ref, wba_ref, x_ref, dh_ref, nw_ref, *rest):
        r = list(rest)
        gnw_prev_ref = r.pop(0) if n_prev else None
        if n_prev:
            r.pop(0)
        in_refs = [r.pop(0) for _ in range(n_in)]
        del r[:n_al]
        gx_ref, gnw_ref = r.pop(0), r.pop(0)
        alias_refs = [r.pop(0) for _ in range(n_al)]
        new_refs = [r.pop(0) for _ in range(n_new)]
        acc_ref, send_sems, recv_sems = r
        i = pl.program_id(0)
        k = pl.program_id(1)
        step = i * nk + k
        cps = make_copies(in_refs, alias_refs, new_refs, send_sems, recv_sems)

        @pl.when(step == 0)
        def _():
            gnw_ref[...] = gnw_prev_ref[...] if n_prev else jnp.zeros_like(gnw_ref)
            for cp in cps:
                cp.start()

        @pl.when(k == 0)
        def _():
            acc_ref[...] = _mm_nt(dba_ref[...], wba_ref[...])

        acc_ref[...] += _mm_nt(dm_ref[...], w_ref[...])

        @pl.when(k == nk - 1)
        def _():
            xv = x_ref[...]
            rs = lax.rsqrt(jnp.mean(xv * xv, axis=-1, keepdims=True) + EPS)
            xh = xv * rs
            dxn = acc_ref[...]
            gnw_ref[0:1, :] += jnp.sum(dxn * xh, axis=0, keepdims=True)
            dxh = dxn * nw_ref[...]
            gx_ref[...] = dh_ref[...] + rs * (dxh - xh * jnp.mean(dxh * xh, axis=-1, keepdims=True))

        @pl.when(step == last_step)
        def _():
            for cp in cps:
                cp.wait()

    any_spec = pl.BlockSpec(memory_space=pl.ANY)
    prev_specs = [pl.BlockSpec((8, DM), lambda i, k: (0, 0)), any_spec] if n_prev else []
    prev_args = [prev[1], prev[0]] if n_prev else []
    aliases = {8: 0} if n_prev else {}
    for q in range(n_al):
        aliases[7 + n_prev + n_in + q] = 2 + q
    res = pl.pallas_call(
        body, name=name, grid=(nblk, nk),
        in_specs=[pl.BlockSpec((tm, tk), lambda i, k: (blk0 + i, k)),
                  pl.BlockSpec((tm, LANES), lambda i, k: (blk0 + i, 0)),
                  pl.BlockSpec((DM, tk), lambda i, k: (0, k)),
                  pl.BlockSpec((DM, LANES), lambda i, k: (0, 0)),
                  pl.BlockSpec((tm, DM), lambda i, k: (blk0 + i, 0)),
                  pl.BlockSpec((tm, DM), lambda i, k: (blk0 + i, 0)),
                  pl.BlockSpec((1, DM), lambda i, k: (0, 0))] + prev_specs + [any_spec] * (n_in + n_al),
        out_specs=[pl.BlockSpec((tm, DM), lambda i, k: (blk0 + i, 0)),
                   pl.BlockSpec((8, DM), lambda i, k: (0, 0))] + [any_spec] * (n_al + n_new),
        out_shape=[S((T, DM), F32), S((8, DM), F32)] + [S(a.shape, a.dtype) for a in hbm_alias] + list(hbm_new),
        scratch_shapes=[pltpu.VMEM((tm, DM), F32), pltpu.SemaphoreType.DMA((10,)), pltpu.SemaphoreType.DMA((10,))],
        input_output_aliases=aliases,
        compiler_params=_cp(ARB, ARB),
    )(dmain, dba, w_main, w_ba, x, dh, norm_w, *prev_args, *hbm_in, *hbm_alias)
    return (res[0], res[1]), res[2:2 + n_al], res[2 + n_al:]


def _remote(kk, src, dst, to, send_sems, recv_sems):
    return pltpu.make_async_remote_copy(src_ref=src, dst_ref=dst, send_sem=send_sems.at[kk], recv_sem=recv_sems.at[kk],
                                        device_id=to, device_id_type=MESH)


def _dx(dmain, dba, w_main, w_ba, x, dh, norm_w, chip_sum, small, cut):
    R, C = chip_sum.shape[1:]
    half = R // 2
    assert half % 16 == 0
    T = x.shape[0]
    ni = T // _dx_rows(T)
    cut = max(1, min(cut, ni - 1))
    upper, lower = pl.ds(0, half), pl.ds(half, half)

    def nbrs():
        px, py, pc = _position()
        return (px, py), (1 - px, py, pc), (px, 1 - py, pc)

    def phase1(ins, als, news, ss, rs):
        (px, py), xn, yn = nbrs()
        cs = ins[0]
        recv, stage = news
        bx, by, bd = cs.at[2 * (1 - px) + py], cs.at[2 * px + (1 - py)], cs.at[2 * (1 - px) + (1 - py)]
        return [_remote(0, bx.at[upper], recv.at[0].at[upper], xn, ss, rs),
                _remote(1, by.at[lower], recv.at[1].at[lower], yn, ss, rs),
                _remote(2, bd.at[upper], stage.at[0], xn, ss, rs),
                _remote(3, bd.at[lower], stage.at[1], yn, ss, rs)]

    def phase2(ins, als, news, ss, rs):
        (px, py), xn, yn = nbrs()
        comb, small_ref = ins
        recv, gath = als[0], news[0]
        me, small_cps = _broadcast_copies([small_ref], [gath], _Sem2(ss, 2), _Sem2(rs, 2))
        return ([_remote(0, comb.at[0], recv.at[1].at[upper], yn, ss, rs),
                 _remote(1, comb.at[1], recv.at[0].at[lower], xn, ss, rs)] + small_cps
                + [pltpu.make_async_copy(small_ref, gath.at[me], ss.at[9])])

    (gx, gnw), _, (recv, stage) = _dx_part(
        "dx_a", dmain, dba, w_main, w_ba, x, dh, norm_w, 0, cut, None, [chip_sum], [],
        [S((2, R, C), chip_sum.dtype), S((2, half, C), chip_sum.dtype)], phase1)
    comb = _relay_add(chip_sum, stage)
    (gx, gnw), (recv,), (gath,) = _dx_part(
        "dx_b", dmain, dba, w_main, w_ba, x, dh, norm_w, cut, ni - cut, (gx, gnw), [comb, small], [recv],
        [S((N_DEV,) + small.shape, F32)], phase2)
    return gx, gnw, gath, recv


class _Sem2:
    def __init__(self, sems, lo):
        self.sems, self.lo = sems, lo

    @property
    def at(self):
        outer = self

        class _At:
            def __getitem__(self, idx):
                a, k = idx
                return outer.sems.at[outer.lo + k]
        return _At()


def _relay_add(chip_sum, stage):
    _, R, C = chip_sum.shape
    half = R // 2
    tr = _tile(half, 256, 16)
    nt = half // tr
    px, py, _ = _position()
    idx = jnp.stack([2 * px + (1 - py), 2 * (1 - px) + py]).astype(jnp.int32)

    def body(idx_ref, p_ref, s_ref, o_ref):
        del idx_ref
        o_ref[0] = (p_ref[0].astype(F32) + s_ref[0].astype(F32)).astype(BF16)

    return pl.pallas_call(
        body, name="relay_add",
        grid_spec=pltpu.PrefetchScalarGridSpec(
            num_scalar_prefetch=1, grid=(2, nt),
            in_specs=[pl.BlockSpec((1, tr, C), lambda s, i, idx_ref: (idx_ref[s], s * nt + i, 0)),
                      pl.BlockSpec((1, tr, C), lambda s, i, idx_ref: (s, i, 0))],
            out_specs=pl.BlockSpec((1, tr, C), lambda s, i, idx_ref: (s, i, 0))),
        out_shape=S((2, half, C), BF16), compiler_params=_cp(ARB, ARB),
    )(idx, chip_sum, stage)


def _sum_slots(gath):
    _, R, C = gath.shape
    tr = R if R <= 2048 else _tile(R, 512, 8)

    def body(g_ref, o_ref):
        tot = g_ref[0]
        for d in range(1, N_DEV):
            tot = tot + g_ref[d]
        o_ref[...] = tot

    return pl.pallas_call(
        body, name="sum_slots", grid=(R // tr,),
        in_specs=[pl.BlockSpec((N_DEV, tr, C), lambda i: (0, i, 0))],
        out_specs=pl.BlockSpec((tr, C), lambda i: (i, 0)),
        out_shape=S((R, C), F32), compiler_params=_cp(ARB),
    )(gath)


def _prep_a_bwd(dq, dk, dv, c, proj, conv_w, dmain, H, D):
    T = c.shape[0]
    AW = H * D
    C3 = 3 * AW
    tb = _tile(T, 256, 8)
    nblk = T // tb
    r8 = tb // 8
    scale = float(D) ** -0.5

    def body(dq_ref, dk_ref, dv_ref, c_ref, dqn_ref, dkn_ref, dvn_ref, cn_ref, x_ref, halo_ref, cw_ref, dmain_in_ref,
             dx_ref, gcw_ref, dc_ref):
        del dmain_in_ref
        i = pl.program_id(0)

        @pl.when(i == 0)
        def _():
            gcw_ref[...] = jnp.zeros_like(gcw_ref)

        def pointwise(rows, dq_r, dk_r, dv_r, c_r, keep):
            for h in range(H):
                for part, d_r, sc in ((0, dq_r, scale), (1, dk_r, 1.0)):
                    sl = slice(part * AW + h * D, part * AW + (h + 1) * D)
                    cv = c_r[:, sl]
                    raw = _silu(cv)
                    rs = lax.rsqrt(jnp.sum(raw * raw, axis=-1, keepdims=True) + EPS)
                    nrm = raw * rs
                    dn = d_r[:, h * D:(h + 1) * D] * sc
                    draw = rs * (dn - nrm * jnp.sum(dn * nrm, axis=-1, keepdims=True))
                    dc_ref[rows, sl] = draw * _dsilu(cv) * keep
            dc_ref[rows, 2 * AW:] = dv_r[...] * _dsilu(c_r[:, 2 * AW:]) * keep

        pointwise(slice(0, tb), dq_ref, dk_ref, dv_ref, c_ref, 1.0)
        pointwise(slice(tb, tb + 8), dqn_ref, dkn_ref, dvn_ref, cn_ref, (i < nblk - 1).astype(F32))

        cw = cw_ref[...]
        dcv = dc_ref[0:tb, :]
        dx = cw[3:4, :] * dcv
        for j in range(3):
            dx = dx + cw[j:j + 1, :] * dc_ref[3 - j:3 - j + tb, :]
        dx_ref[...] = dx.astype(BF16)
        halo = halo_ref[...] * (i > 0).astype(F32)
        xp = jnp.concatenate([halo, x_ref[...]], axis=0)
        for j in range(4):
            gcw_ref[j:j + 1, :] += jnp.sum(dcv * xp[5 + j:5 + j + tb], axis=0, keepdims=True)

    nxt = lambda i: (jnp.minimum((i + 1) * r8, T // 8 - 1), 0)
    return pl.pallas_call(
        body, name="prep_a_bwd", grid=(nblk,),
        in_specs=[pl.BlockSpec((tb, AW), lambda i: (i, 0)),
                  pl.BlockSpec((tb, AW), lambda i: (i, 0)),
                  pl.BlockSpec((tb, AW), lambda i: (i, 0)),
                  pl.BlockSpec((tb, C3), lambda i: (i, 0)),
                  pl.BlockSpec((8, AW), nxt), pl.BlockSpec((8, AW), nxt), pl.BlockSpec((8, AW), nxt),
                  pl.BlockSpec((8, C3), nxt),
                  pl.BlockSpec((tb, C3), lambda i: (i, 0)),
                  pl.BlockSpec((8, C3), lambda i: (jnp.maximum(i * r8 - 1, 0), 0)),
                  pl.BlockSpec((4, C3), lambda i: (0, 0)),
                  pl.BlockSpec(memory_space=pl.ANY)],
        out_specs=[pl.BlockSpec((tb, C3), lambda i: (i, 0)),
                   pl.BlockSpec((8, C3), lambda i: (0, 0))],
        out_shape=[S(dmain.shape, dmain.dtype), S((8, C3), F32)],
        scratch_shapes=[pltpu.VMEM((tb + 8, C3), F32)],
        input_output_aliases={11: 0},
        compiler_params=_cp(ARB),
    )(dq, dk, dv, c, dq, dk, dv, c, proj, proj, conv_w, dmain)


def _adam_math(w, g, m, v):
    m2 = ADAM_B1 * m + (1.0 - ADAM_B1) * g
    v2 = ADAM_B2 * v + (1.0 - ADAM_B2) * (g * g)
    m_hat = m2 / (1.0 - ADAM_B1 ** ADAM_STEP)
    v_hat = v2 / (1.0 - ADAM_B2 ** ADAM_STEP)
    delta = -ADAM_LR * (m_hat / (jnp.sqrt(v_hat) + ADAM_EPS) + ADAM_WD * w)
    return delta, m2, v2


def _pair_sum(blocks, recv, core, name):
    K, _, R, C = blocks.shape
    tr = _tile(R, 256, 16)

    def body(core_ref, a_ref, b_ref, o_ref):
        del core_ref
        o_ref[0] = (a_ref[0, 0].astype(F32) + b_ref[0].astype(F32)).astype(BF16)

    spec = lambda: pl.BlockSpec((1, tr, C), lambda k, i, core_ref: (k, i, 0))
    return pl.pallas_call(
        body, name=name,
        grid_spec=pltpu.PrefetchScalarGridSpec(
            num_scalar_prefetch=1, grid=(K, R // tr),
            in_specs=[pl.BlockSpec((1, 1, tr, C), lambda k, i, core_ref: (k, core_ref[0], i, 0)), spec()],
            out_specs=spec()),
        out_shape=S((K, R, C), BF16), compiler_params=_cp(ARB, ARB),
    )(core, blocks, recv)


def _sum_adam(chip_sums, recv, w, m, v, chip, name, transposed=False):
    R, C = chip_sums.shape[1:]
    NR = recv.shape[0]
    tr = _tile(R, min(256, max(R // 4, 16)), 16)

    def body(chip_ref, own_ref, r_ref, w_ref, m_ref, v_ref, g_ref, d_ref, m2_ref, v2_ref):
        del chip_ref
        g = own_ref[0].astype(F32)
        for j in range(NR):
            g = g + r_ref[j].astype(F32)
        if transposed:
            g = g.T
        g_ref[...] = g
        d_ref[...], m2_ref[...], v2_ref[...] = _adam_math(w_ref[...], g, m_ref[...], v_ref[...])

    if transposed:
        spec = lambda: pl.BlockSpec((C, tr), lambda i, chip_ref: (0, i))
        shape = (C, R)
    else:
        spec = lambda: pl.BlockSpec((tr, C), lambda i, chip_ref: (i, 0))
        shape = (R, C)
    assert w.shape == shape
    return pl.pallas_call(
        body, name=name,
        grid_spec=pltpu.PrefetchScalarGridSpec(
            num_scalar_prefetch=1, grid=(R // tr,),
            in_specs=[pl.BlockSpec((1, tr, C), lambda i, chip_ref: (chip_ref[0], i, 0)),
                      pl.BlockSpec((NR, tr, C), lambda i, chip_ref: (0, i, 0)), spec(), spec(), spec()],
            out_specs=[spec(), spec(), spec(), spec()]),
        out_shape=[S(shape, F32)] * 4, compiler_params=_cp(ARB),
    )(chip, chip_sums, recv, w, m, v)


def _adam_small(w, g, m, v):
    R, C = w.shape
    tr = _tile(R, 512, 8)

    def body(w_ref, g_ref, m_ref, v_ref, d_ref, m2_ref, v2_ref):
        d_ref[...], m2_ref[...], v2_ref[...] = _adam_math(w_ref[...], g_ref[...], m_ref[...], v_ref[...])

    spec = lambda: pl.BlockSpec((tr, C), lambda i: (i, 0))
    return pl.pallas_call(
        body, name="adam_small", grid=(R // tr,), in_specs=[spec()] * 4, out_specs=[spec()] * 3,
        out_shape=[S((R, C), F32)] * 3, compiler_params=_cp(ARB),
    )(w, g, m, v)


def _position():
    return lax.axis_index("x"), lax.axis_index("y"), lax.axis_index("c")


def _all_gather_weights(arr, x_in, norm_w, chip, tn, plans, nm):
    R = arr.shape[0]
    half = R // 2
    assert half % 16 == 0
    T, DM = x_in.shape
    tm = _tile(T, 512, 16)
    nstep = T // tm

    def body(chip_ref, x_ref, nw_ref, in_ref, xn_ref, out_ref, proj_ref, wtile_ref, stage_ref,
             send_sems, recv_sems, local_sem, stage_sems):
        i = pl.program_id(0)
        x, y, c = _position()
        me, sibling = (x, y, c), (x, y, 1 - c)
        xn, yn, diag = (1 - x, y), (x, 1 - y), (1 - x, 1 - y)
        upper, lower = pl.ds(0, half), pl.ds(half, half)

        def slot(p, rows=None):
            ref = out_ref.at[4 * p[0] + 2 * p[1] + p[2]]
            return ref if rows is None else ref.at[rows]

        def copy(kk, block, to, rows=None, src=None):
            return pltpu.make_async_remote_copy(
                src_ref=slot(block, rows) if src is None else src, dst_ref=slot(block, rows),
                send_sem=send_sems.at[kk], recv_sem=recv_sems.at[kk], device_id=to, device_id_type=MESH)

        mine = pltpu.make_async_copy(in_ref, slot(me), local_sem)
        first = [copy(0, me, sibling, src=in_ref), copy(1, me, (*xn, c), src=in_ref), copy(2, me, (*yn, c), src=in_ref)]

        @pl.when(i == 0)
        def _():
            mine.start()
            for cp in first:
                cp.start()
            copy(0, sibling, me).wait_recv()
            loads = [pltpu.make_async_copy(in_ref, stage_ref.at[c], stage_sems.at[0]),
                     pltpu.make_async_copy(slot(sibling), stage_ref.at[1 - c], stage_sems.at[1])]
            for cp in loads:
                cp.start()
            for cp in loads:
                cp.wait()
            for m, plan in enumerate(plans):
                @pl.when(chip_ref[0] == m)
                def _(plan=plan):
                    for d, s0, w, c0 in plan:
                        wtile_ref[:, c0:c0 + w] = stage_ref[d % 2, :, s0:s0 + w]

        xv = x_ref[...]
        r = lax.rsqrt(jnp.mean(xv * xv, axis=-1, keepdims=True) + EPS)
        xnv = (xv * r * nw_ref[...]).astype(BF16)
        xn_ref[...] = xnv
        proj_ref[...] = jnp.dot(xnv.astype(MXU), wtile_ref[...].astype(MXU), preferred_element_type=F32)

        @pl.when(i == nstep - 1)
        def _():
            sent = list(first)

            def then(cps):
                for cp in cps:
                    cp.start()
                sent.extend(cps)

            copy(1, (*xn, c), me).wait_recv()
            then([copy(5, (*xn, c), (*yn, c), rows=upper), copy(3, (*xn, c), sibling)])
            copy(2, (*yn, c), me).wait_recv()
            then([copy(6, (*yn, c), (*xn, c), rows=lower), copy(4, (*yn, c), sibling)])
            copy(5, (*diag, c), me, rows=upper).wait_recv()
            then([copy(7, (*diag, c), sibling, rows=upper)])
            copy(6, (*diag, c), me, rows=lower).wait_recv()
            then([copy(8, (*diag, c), sibling, rows=lower)])
            copy(3, (*xn, 1 - c), me).wait_recv()
            copy(4, (*yn, 1 - c), me).wait_recv()
            copy(7, (*diag, 1 - c), me, rows=upper).wait_recv()
            copy(8, (*diag, 1 - c), me, rows=lower).wait_recv()
            for cp in sent:
                cp.wait_send()
            mine.wait()

    any_spec = pl.BlockSpec(memory_space=pl.ANY)
    return pl.pallas_call(
        body, name="all_gather_weights",
        grid_spec=pltpu.PrefetchScalarGridSpec(
            num_scalar_prefetch=1, grid=(nstep,),
            in_specs=[pl.BlockSpec((tm, DM), lambda i, chip_ref: (i, 0)),
                      pl.BlockSpec((1, DM), lambda i, chip_ref: (0, 0)), any_spec],
            out_specs=[pl.BlockSpec((tm, DM), lambda i, chip_ref: (i, 0)), any_spec,
                       pl.BlockSpec((tm, tn), lambda i, chip_ref: (i, 2 * chip_ref[0]))],
            scratch_shapes=[pltpu.VMEM((DM, tn), arr.dtype), pltpu.VMEM((2,) + arr.shape, arr.dtype),
                            pltpu.SemaphoreType.DMA((9,)), pltpu.SemaphoreType.DMA((9,)), pltpu.SemaphoreType.DMA,
                            pltpu.SemaphoreType.DMA((2,))]),
        out_shape=[S((T, DM), BF16), S((N_DEV,) + arr.shape, arr.dtype), S((T, nm), F32)],
        compiler_params=_cp(ARB),
    )(chip, x_in, norm_w, arr)


def _sibling_copies(ins, outs, send_sems, recv_sems):
    x, y, c = _position()
    return [pltpu.make_async_remote_copy(src_ref=ins[a].at[k, 1 - c], dst_ref=outs[a].at[k],
                                         send_sem=send_sems.at[a, k], recv_sem=recv_sems.at[a, k],
                                         device_id=(x, y, 1 - c), device_id_type=MESH)
            for a in range(len(ins)) for k in range(ins[a].shape[0])]


def _sibling_sems(arrs):
    shape = (max(len(arrs), 1), arrs[0].shape[0] if arrs else 1)
    return [pltpu.SemaphoreType.DMA(shape), pltpu.SemaphoreType.DMA(shape)]


def _chip_exchange_copies(ins, outs, send_sems, recv_sems):
    x, y, c = _position()
    chips = [(1 - x, y), (x, 1 - y), (1 - x, 1 - y)]
    return [pltpu.make_async_remote_copy(
        src_ref=ins[a].at[2 * qx + qy], dst_ref=outs[a].at[j], send_sem=send_sems.at[a, j],
        recv_sem=recv_sems.at[a, j], device_id=(qx, qy, c), device_id_type=MESH)
        for a in range(len(ins)) for j, (qx, qy) in enumerate(chips)]


def _broadcast_copies(srcs, dsts, send_sems, recv_sems):
    x, y, c = _position()
    me = 4 * x + 2 * y + c
    cps = []
    for a in range(len(srcs)):
        for k in range(1, N_DEV):
            peer = (1 - x if k & 4 else x, 1 - y if k & 2 else y, 1 - c if k & 1 else c)
            cps.append(pltpu.make_async_remote_copy(
                src_ref=srcs[a], dst_ref=dsts[a].at[me], send_sem=send_sems.at[a, k - 1],
                recv_sem=recv_sems.at[a, k - 1], device_id=peer, device_id_type=MESH))
    return me, cps


def _all_reduce_small(part):
    R, C = part.shape

    def body(p_ref, out_ref, gath_ref, send_sems, recv_sems):
        me, cps = _broadcast_copies([p_ref], [gath_ref], send_sems, recv_sems)
        gath_ref[me] = p_ref[...]
        for cp in cps:
            cp.start()
        for cp in cps:
            cp.wait()
        acc = gath_ref[0]
        for d in range(1, N_DEV):
            acc = acc + gath_ref[d]
        out_ref[...] = acc

    vm = pl.BlockSpec(memory_space=pltpu.VMEM)
    return pl.pallas_call(
        body, name="all_reduce_small", in_specs=[vm], out_specs=vm, out_shape=S((R, C), F32),
        scratch_shapes=[pltpu.VMEM((N_DEV, R, C), F32), pltpu.SemaphoreType.DMA((1, N_DEV - 1)),
                        pltpu.SemaphoreType.DMA((1, N_DEV - 1))],
    )(part)


def _pack(parts):
    rows = []
    for p in parts:
        f = p.reshape(-1).astype(F32)
        pad = (-f.shape[0]) % (8 * LANES)
        rows.append(jnp.pad(f, (0, pad)).reshape(-1, LANES))
    return jnp.concatenate(rows, axis=0)


def _unpack(buf, shapes):
    out, r = [], 0
    for shp in shapes:
        n = 1
        for s in shp:
            n *= s
        nr = -(-n // (8 * LANES)) * 8
        out.append(buf[r:r + nr].reshape(-1)[:n].reshape(shp))
        r += nr
    return out


def kernel(x, norm_w, w_in, conv_w, a_log, dt_bias, head_norm_w, sgu_ln_w, sgu_ln_b, w_spatial, b_spatial, w_out, final_norm_w, loss_target, m_norm_w, m_w_in, m_conv_w, m_a_log, m_dt_bias, m_head_norm_w, m_sgu_ln_w, m_sgu_ln_b, m_w_spatial, m_b_spatial, m_w_out, m_final_norm_w, v_norm_w, v_w_in, v_conv_w, v_a_log, v_dt_bias, v_head_norm_w, v_sgu_ln_w, v_sgu_ln_b, v_w_spatial, v_b_spatial, v_w_out, v_final_norm_w):
    T, DM = x.shape[1], x.shape[2]
    H, D = a_log.shape[1], head_norm_w.shape[1]
    G, P = w_spatial.shape[1], w_spatial.shape[2]
    AW, BW = H * D, G * P
    MIX = AW + BW
    WD = w_in.shape[2]
    IN = N_DEV * WD
    RO = w_out.shape[1]
    CW = conv_w.shape[2]
    sizes = (3 * AW, AW, H, H, BW, BW, BW)
    assert sum(sizes) == IN and 2 * H <= LANES and 3 * H <= 32 and N_DEV * RO == MIX and N_DEV * CW == 3 * AW
    offs = [0]
    for s in sizes:
        offs.append(offs[-1] + s)
    px, py, pc = _position()
    dev = 4 * px + 2 * py + pc
    chip = 2 * px + py

    x2, tgt = x[0], loss_target[0]

    core_idx = jnp.reshape(pc, (1,)).astype(jnp.int32)
    chip_idx = jnp.reshape(chip, (1,)).astype(jnp.int32)
    NM = IN - 2 * H
    tn_loc, tile_plans = _local_tiles(WD, offs[2], offs[4], NM)
    xn, g_win, proj_part = _all_gather_weights(
        _cast_bf16_t(w_in[0].T, "cast_w_in"), x2, norm_w, chip_idx, tn_loc, tile_plans, NM)
    w_main, w_ba = _relayout_w(g_win, offs[2], offs[4])
    alog_row = jnp.pad(a_log, ((0, 0), (H, LANES - 2 * H)))
    dtb_row = jnp.pad(dt_bias, ((0, 0), (H, LANES - 2 * H)))
    bs_t = b_spatial[0].T

    others = jnp.arange(N_DEV - 2, dtype=jnp.int32)
    others = others + (others >= 2 * chip).astype(jnp.int32)
    proj, ba, (g_wout, g_conv) = _in_proj(xn, w_main, w_ba, proj_part, others, tn_loc,
                                          [_cast_bf16(w_out[0], "cast_w_out"), conv_w[0]])
    w_out_full = g_wout.reshape(MIX, DM)
    conv_full = g_conv.transpose(1, 0, 2).reshape(4, 3 * AW)
    q, k, v, c, gcol, grow = _prep_a_fwd(proj, ba, conv_full, alog_row, dtb_row, H, D)
    o, vnew, ssave, asave = _delta_fwd(q, k, v, gcol, grow, H, D)
    ocat = _mix_fwd(o, proj, head_norm_w, sgu_ln_w, sgu_ln_b, w_spatial[0], bs_t, H, D, G, P)
    dh, dh_bf, d_ocat, loss_acc, g_fnw = _out_proj_loss(ocat, w_out_full, x2, tgt, final_norm_w.reshape(1, DM))

    g_wout_blocks = _grad_w(ocat, dh_bf, "grad_w_out").reshape(4, 2, RO, DM)
    (d_o, dmain, g_hnw, g_ln, g_wsp, g_bs_t), (sib_wout,) = _mix_bwd(
        d_ocat, o, proj, head_norm_w, sgu_ln_w, sgu_ln_b, w_spatial[0], bs_t, H, D, G, P, [g_wout_blocks])
    chip_wout = _pair_sum(g_wout_blocks, sib_wout, core_idx, "pair_sum_w_out")
    (dq, dk, dv, dgate, dpar), (recv_wout,) = _delta_bwd(
        q, k, v, gcol, grow, ba, vnew, ssave, asave, d_o, alog_row, dtb_row, H, D, [chip_wout])
    dmain, g_conv_part = _prep_a_bwd(dq, dk, dv, c, proj, conv_full, dmain, H, D)
    dba = dgate.astype(BF16)
    keep_win, sib_win = _grad_w_in(xn, dmain, dba, WD, offs[2], offs[4])
    chip_win = _pair_sum_plain(keep_win, sib_win, "pair_sum_w_in")
    small_shapes = [a_log.shape, dt_bias.shape, head_norm_w.shape, sgu_ln_w.shape, sgu_ln_b.shape,
                    w_spatial.shape, b_spatial.shape, final_norm_w.shape]
    parts = [dpar[0, H:2 * H], dpar[1, H:2 * H], g_hnw[0], g_ln[0], g_ln[1], g_wsp, g_bs_t[:, :G].T, g_fnw[0],
             g_conv_part[:4], loss_acc[0, :1]]
    grad_x, g_nw, small_gath, recv_win = _dx(dmain, dba, w_main, w_ba, x2, dh, norm_w, chip_win, _pack(parts), 4)
    red = _sum_slots(small_gath)
    grad_w_in, delta_w_in, new_m_w_in, new_v_w_in = _sum_adam(
        chip_win, recv_win, w_in[0].T, m_w_in[0].T, v_w_in[0].T, chip_idx, "sum_adam_w_in", transposed=True)
    grad_w_out, delta_w_out, new_m_w_out, new_v_w_out = _sum_adam(
        chip_wout, recv_wout, w_out[0], m_w_out[0], v_w_out[0], chip_idx, "sum_adam_w_out")
    red_nw = _all_reduce_small(_pack([g_nw[0]]))
    grads_small = _unpack(red_nw, [norm_w.shape]) + _unpack(red, small_shapes + [(4, 3 * AW), (1,)])
    loss = grads_small.pop()[0]
    g_conv_full = grads_small.pop()
    grad_conv = lax.dynamic_slice_in_dim(g_conv_full, dev * CW, CW, axis=1)[None]
    small_w = [norm_w, a_log, dt_bias, head_norm_w, sgu_ln_w, sgu_ln_b, w_spatial, b_spatial, final_norm_w, conv_w]
    small_m = [m_norm_w, m_a_log, m_dt_bias, m_head_norm_w, m_sgu_ln_w, m_sgu_ln_b, m_w_spatial, m_b_spatial,
               m_final_norm_w, m_conv_w]
    small_v = [v_norm_w, v_a_log, v_dt_bias, v_head_norm_w, v_sgu_ln_w, v_sgu_ln_b, v_w_spatial, v_b_spatial,
               v_final_norm_w, v_conv_w]
    small_g = grads_small + [grad_conv]
    shapes10 = [w.shape for w in small_w]
    d_p, m_p, v_p = _adam_small(_pack(small_w), _pack(small_g), _pack(small_m), _pack(small_v))
    d_s, m_s, v_s = _unpack(d_p, shapes10), _unpack(m_p, shapes10), _unpack(v_p, shapes10)

    def order(small, win, wout):
        return [small[0], win.T[None], small[9], small[1], small[2], small[3], small[4], small[5], small[6], small[7],
                wout[None], small[8]]

    grads = order(small_g, grad_w_in, grad_w_out)
    deltas = order(d_s, delta_w_in, delta_w_out)
    new_m = order(m_s, new_m_w_in, new_m_w_out)
    new_v = order(v_s, new_v_w_in, new_v_w_out)
    return (loss, grad_x[None], *grads, *deltas, *new_m, *new_v)
```

```python
import jax
import jax.numpy as jnp
from jax import lax
from jax.experimental import pallas as pl
from jax.experimental.pallas import tpu as pltpu

F32 = jnp.float32
BF16 = jnp.bfloat16
MXU = jnp.bfloat16
HI = lax.Precision.HIGHEST
EPS = 1e-6
CHUNK_A = 64
LANES = 128
MESH = pl.DeviceIdType.MESH
N_DEV = 8

ADAM_LR = 0.001
ADAM_B1 = 0.9
ADAM_B2 = 0.999
ADAM_EPS = 1e-08
ADAM_WD = 0.01
ADAM_STEP = 10

S = jax.ShapeDtypeStruct
ARB = "arbitrary"


def _cp(*sem, vmem_mib=56):
    return pltpu.CompilerParams(dimension_semantics=tuple(sem), vmem_limit_bytes=vmem_mib * 1024 * 1024)


def _tile(n, cap, mult):
    best = None
    t = mult
    while t <= min(n, cap):
        if n % t == 0:
            best = t
        t += mult
    return best if best is not None else n


def _mm(a, b):
    return jnp.dot(a.astype(MXU), b.astype(MXU), preferred_element_type=F32)


def _mm_nt(a, b):
    return lax.dot_general(a.astype(MXU), b.astype(MXU), (((1,), (1,)), ((), ())), preferred_element_type=F32)


def _mm_tn(a, b):
    return lax.dot_general(a.astype(MXU), b.astype(MXU), (((0,), (0,)), ((), ())), preferred_element_type=F32)


def _mmh(a, b):
    return jnp.dot(a, b, precision=HI, preferred_element_type=F32)


def _sigmoid(x):
    return 1.0 / (1.0 + jnp.exp(-x))


def _silu(x):
    return x * _sigmoid(x)


def _dsilu(x):
    s = _sigmoid(x)
    return s * (1.0 + x * (1.0 - s))


def _softplus(x):
    return jnp.maximum(x, 0.0) + jnp.log(1.0 + jnp.exp(-jnp.abs(x)))


def _pieces(wd, gate_lo, gate_hi, total):
    out = []
    for d in range(N_DEV):
        lo, hi = d * wd, (d + 1) * wd
        for dest, a, b, shift in (("main", 0, gate_lo, 0), ("gate", gate_lo, gate_hi, -gate_lo),
                                  ("main", gate_hi, total, gate_lo - gate_hi)):
            s0, s1 = max(lo, a), min(hi, b)
            if s0 < s1:
                out.append((d, s0 - lo, s1 - lo, dest, s0 + shift))
    return out


def _local_tiles(wd, gate_lo, gate_hi, nm):
    n_tiles = N_DEV - 1
    assert nm % (n_tiles * LANES) == 0
    tn = nm // n_tiles
    plans = []
    for m in range(N_DEV // 2):
        lo, hi = 2 * m * tn, (2 * m + 1) * tn
        plan = []
        for d, s0, s1, dest, c0 in _pieces(wd, gate_lo, gate_hi, N_DEV * wd):
            if dest != "main":
                continue
            a, b = max(c0, lo), min(c0 + (s1 - s0), hi)
            if a < b:
                assert d // 2 == m, "tile 2m must come from chip m's own shards"
                plan.append((d, s0 + (a - c0), b - a, a - lo))
        assert sum(p[2] for p in plan) == tn
        plans.append(plan)
    return tn, plans


def _cast_bf16(a, name):
    R, C = a.shape
    tr = _tile(R, 256, 16)

    def body(a_ref, o_ref):
        o_ref[...] = a_ref[...].astype(BF16)

    spec = pl.BlockSpec((tr, C), lambda i: (i, 0))
    return pl.pallas_call(body, name=name, grid=(R // tr,), in_specs=[spec], out_specs=spec,
                          out_shape=S((R, C), BF16), compiler_params=_cp(ARB))(a)


def _cast_bf16_t(a_t, name):
    C, R = a_t.shape
    tr = _tile(R, 256, LANES)

    def body(a_ref, o_ref):
        o_ref[...] = a_ref[...].T.astype(BF16)

    return pl.pallas_call(body, name=name, grid=(R // tr,), in_specs=[pl.BlockSpec((C, tr), lambda i: (0, i))],
                          out_specs=pl.BlockSpec((tr, C), lambda i: (i, 0)),
                          out_shape=S((R, C), BF16), compiler_params=_cp(ARB))(a_t)


def _relayout_w(g_win, gate_lo, gate_hi):
    _, DM, WD = g_win.shape
    total = N_DEV * WD
    NM = total - (gate_hi - gate_lo)
    tr = _tile(DM, 256, 16)
    plan = _pieces(WD, gate_lo, gate_hi, total)

    def body(g_ref, main_ref, gate_ref):
        gate_ref[...] = jnp.zeros_like(gate_ref)
        for d, s0, s1, dest, c0 in plan:
            dst = main_ref if dest == "main" else gate_ref
            dst[:, c0:c0 + (s1 - s0)] = g_ref[d, :, s0:s1]

    return pl.pallas_call(
        body, name="relayout_w", grid=(DM // tr,),
        in_specs=[pl.BlockSpec((N_DEV, tr, WD), lambda i: (0, i, 0))],
        out_specs=[pl.BlockSpec((tr, NM), lambda i: (i, 0)), pl.BlockSpec((tr, LANES), lambda i: (i, 0))],
        out_shape=[S((DM, NM), g_win.dtype), S((DM, LANES), g_win.dtype)],
        compiler_params=_cp(ARB),
    )(g_win)


def _in_proj(xn, w_main, w_ba, proj_part, tiles, tn, shards):
    T, DM = xn.shape
    NM = w_main.shape[1]
    tm = _tile(T, 2048, 16)
    ni, nj = T // tm, tiles.shape[0]
    ns = len(shards)

    def body(tiles_ref, xn_ref, w_ref, wba_ref, part_ref, *rest):
        del tiles_ref, part_ref
        srcs = rest[:ns]
        proj_ref, ba_ref = rest[ns:ns + 2]
        gath = rest[ns + 2:2 * ns + 2]
        send_sems, recv_sems, local_sems = rest[2 * ns + 2:]
        i = pl.program_id(0)
        j = pl.program_id(1)
        me, cps = _broadcast_copies(srcs, gath, send_sems, recv_sems)
        cps = cps + [pltpu.make_async_copy(srcs[a], gath[a].at[me], local_sems.at[a]) for a in range(ns)]

        @pl.when((i == 0) & (j == 0))
        def _():
            for cp in cps:
                cp.start()

        @pl.when(j == 0)
        def _():
            ba_ref[...] = jnp.dot(xn_ref[...].astype(MXU), wba_ref[...].astype(MXU), preferred_element_type=F32)

        proj_ref[...] = jnp.dot(xn_ref[...].astype(MXU), w_ref[...].astype(MXU), preferred_element_type=F32)

        @pl.when((i == ni - 1) & (j == nj - 1))
        def _():
            for cp in cps:
                cp.wait()

    any_spec = pl.BlockSpec(memory_space=pl.ANY)
    res = pl.pallas_call(
        body, name="in_proj",
        grid_spec=pltpu.PrefetchScalarGridSpec(
            num_scalar_prefetch=1, grid=(ni, nj),
            in_specs=[pl.BlockSpec((tm, DM), lambda i, j, t: (i, 0)),
                      pl.BlockSpec((DM, tn), lambda i, j, t: (0, t[j])),
                      pl.BlockSpec((DM, LANES), lambda i, j, t: (0, 0)), any_spec] + [any_spec] * ns,
            out_specs=[pl.BlockSpec((tm, tn), lambda i, j, t: (i, t[j])),
                       pl.BlockSpec((tm, LANES), lambda i, j, t: (i, 0))] + [any_spec] * ns,
            scratch_shapes=[pltpu.SemaphoreType.DMA((ns, N_DEV - 1)), pltpu.SemaphoreType.DMA((ns, N_DEV - 1)),
                            pltpu.SemaphoreType.DMA((ns,))]),
        out_shape=[S((T, NM), F32), S((T, LANES), F32)] + [S((N_DEV,) + a.shape, a.dtype) for a in shards],
        input_output_aliases={4: 0},
        compiler_params=_cp(ARB, ARB, vmem_mib=58),
    )(tiles, xn, w_main, w_ba, proj_part, *shards)
    return res[0], res[1], res[2:]


def _prep_a_fwd(proj, ba, conv_w, alog_row, dtb_row, H, D):
    T = proj.shape[0]
    AW = H * D
    C3 = 3 * AW
    tb = _tile(T, 256, CHUNK_A)
    nch = tb // CHUNK_A
    nblk = T // tb
    scale = float(D) ** -0.5

    def body(x_ref, halo_ref, ba_ref, cw_ref, al_ref, dt_ref, q_ref, k_ref, v_ref, c_ref, gcol_ref, grow_ref):
        i = pl.program_id(0)
        xv = x_ref[...]
        halo = halo_ref[...] * (i > 0).astype(F32)
        xp = jnp.concatenate([halo, xv], axis=0)
        cw = cw_ref[...]
        c = cw[0:1, :] * xp[5:5 + tb]
        for j in range(1, 4):
            c = c + cw[j:j + 1, :] * xp[5 + j:5 + j + tb]
        c_ref[...] = c
        a = _silu(c)
        for h in range(H):
            qh = a[:, h * D:(h + 1) * D]
            kh = a[:, AW + h * D:AW + (h + 1) * D]
            qr = lax.rsqrt(jnp.sum(qh * qh, axis=-1, keepdims=True) + EPS)
            kr = lax.rsqrt(jnp.sum(kh * kh, axis=-1, keepdims=True) + EPS)
            q_ref[:, h * D:(h + 1) * D] = qh * (qr * scale)
            k_ref[:, h * D:(h + 1) * D] = kh * kr
        v_ref[...] = a[:, 2 * AW:]

        bav = ba_ref[...]
        lane = lax.broadcasted_iota(jnp.int32, (tb, LANES), 1)
        beta = _sigmoid(bav)
        g = -jnp.exp(al_ref[...]) * _softplus(bav + dt_ref[...])
        gates = jnp.where(lane < H, beta, jnp.where(lane < 2 * H, g, 0.0))
        ri = lax.broadcasted_iota(jnp.int32, (CHUNK_A, CHUNK_A), 0)
        ci = lax.broadcasted_iota(jnp.int32, (CHUNK_A, CHUNK_A), 1)
        tri = (ri >= ci).astype(F32)
        lane_c = lax.broadcasted_iota(jnp.int32, (CHUNK_A, LANES), 1)
        for cc in range(nch):
            gch = gates[cc * CHUNK_A:(cc + 1) * CHUNK_A]
            gc = pltpu.roll(_mmh(tri, gch), H, 1)
            full = jnp.where(lane_c < 2 * H, gch, jnp.where(lane_c < 3 * H, gc, 0.0))
            gcol_ref[cc * CHUNK_A:(cc + 1) * CHUNK_A, :] = full
            grow_ref[cc] = full.T[0:32, :]

    return pl.pallas_call(
        body, name="prep_a_fwd", grid=(nblk,),
        in_specs=[pl.BlockSpec((tb, C3), lambda i: (i, 0)),
                  pl.BlockSpec((8, C3), lambda i: (jnp.maximum(i * (tb // 8) - 1, 0), 0)),
                  pl.BlockSpec((tb, LANES), lambda i: (i, 0)),
                  pl.BlockSpec((4, C3), lambda i: (0, 0)),
                  pl.BlockSpec((1, LANES), lambda i: (0, 0)),
                  pl.BlockSpec((1, LANES), lambda i: (0, 0))],
        out_specs=[pl.BlockSpec((tb, AW), lambda i: (i, 0)),
                   pl.BlockSpec((tb, AW), lambda i: (i, 0)),
                   pl.BlockSpec((tb, AW), lambda i: (i, 0)),
                   pl.BlockSpec((tb, C3), lambda i: (i, 0)),
                   pl.BlockSpec((tb, LANES), lambda i: (i, 0)),
                   pl.BlockSpec((nch, 32, CHUNK_A), lambda i: (i, 0, 0))],
        out_shape=[S((T, AW), F32), S((T, AW), F32), S((T, AW), F32), S((T, C3), F32),
                   S((T, LANES), F32), S((T // CHUNK_A, 32, CHUNK_A), F32)],
        compiler_params=_cp(ARB),
    )(proj, proj, ba, conv_w, alog_row, dtb_row)


_NN = (((1,), (0,)), ((), ()))
_TN = (((0,), (0,)), ((), ()))


def _split(a):
    hi = a.astype(BF16)
    return hi, (a - hi.astype(F32)).astype(BF16)


def _mm3(a, b, dims=_NN):
    ah, al = a if isinstance(a, tuple) else _split(a)
    bh, bl = b if isinstance(b, tuple) else _split(b)
    dg = lambda p, r: lax.dot_general(p, r, dims, preferred_element_type=F32)
    return dg(ah, bh) + (dg(ah, bl) + dg(al, bh))


def _interleave(gens):
    gens = list(gens)
    while gens:
        alive = []
        for g in gens:
            try:
                next(g)
                alive.append(g)
            except StopIteration:
                pass
        gens = alive


def _chunk_terms(q, k, v, gcolv, growv, h, H):
    C = CHUNK_A
    beta_c = gcolv[:, h:h + 1]
    g_c = gcolv[:, H + h:H + h + 1]
    gc_c = gcolv[:, 2 * H + h:2 * H + h + 1]
    gc_r = growv[2 * H + h:2 * H + h + 1, :]
    ri = lax.broadcasted_iota(jnp.int32, (C, C), 0)
    ci = lax.broadcasted_iota(jnp.int32, (C, C), 1)
    incl = ri >= ci
    strict = ri > ci
    kb = k * beta_c
    vb = v * beta_c
    p_raw = _mm_nt(kb, k)
    qk_raw = _mm_nt(q, k)
    gam = jnp.where(incl, jnp.exp(jnp.where(incl, gc_c - gc_r, 0.0)), 0.0)
    e_c = jnp.exp(gc_c)
    gl = gc_r[:, C - 1:C]
    edec = jnp.exp(gl - gc_c)
    yield
    lmat = jnp.where(strict, p_raw * gam, 0.0)
    attn = jnp.where(incl, qk_raw * gam, 0.0)
    return dict(beta_c=beta_c, g_c=g_c, gc_c=gc_c, gc_r=gc_r, incl=incl, strict=strict, gam=gam, e_c=e_c,
                kb=kb, vb=vb, lmat=lmat, attn=attn, gl=gl, edec=edec, ri=ri, ci=ci)


INV_BLOCK = 16


def _inv_unit_lower(lmat):
    C = lmat.shape[0]
    ri = lax.broadcasted_iota(jnp.int32, (C, C), 0)
    ci = lax.broadcasted_iota(jnp.int32, (C, C), 1)
    eye = (ri == ci).astype(F32)
    same = (ri // INV_BLOCK) == (ci // INV_BLOCK)

    def neumann(x, order):
        a = eye + x
        n = 1
        while 2 * n < order:
            xs = _split(x)
            x = _mm3(xs, xs)
            yield
            a = a + _mm3(a, x)
            n *= 2
        yield
        return a

    inv_d = yield from neumann(-jnp.where(same, lmat, 0.0), INV_BLOCK)
    m = _mm3(inv_d, jnp.where(same, 0.0, lmat))
    yield
    inv_m = yield from neumann(-m, C // INV_BLOCK)
    a = _mm3(inv_m, inv_d)
    yield
    return a


def _delta_fwd(q, k, v, gcol, grow, H, D):
    T = q.shape[0]
    C = CHUNK_A
    N = T // C
    AW = H * D
    CPS = 2 if N % 2 == 0 else 1

    def body(q_ref, k_ref, v_ref, gcol_ref, grow_ref, o_ref, vn_ref, ssave_ref, asave_ref, s_ref):
        @pl.when(pl.program_id(0) == 0)
        def _():
            s_ref[...] = jnp.zeros_like(s_ref)

        state = {(0, h): s_ref[h] for h in range(H)}

        def head(cc, h):
            rows = slice(cc * C, (cc + 1) * C)
            sl = slice(h * D, (h + 1) * D)
            qv, kv, vv = q_ref[rows, sl], k_ref[rows, sl], v_ref[rows, sl]
            t = yield from _chunk_terms(qv, kv, vv, gcol_ref[rows, :], grow_ref[cc], h, H)
            a = yield from _inv_unit_lower(t["lmat"])
            asave_ref[cc, h] = a
            while (cc, h) not in state:
                yield
            st = state[(cc, h)]
            ssave_ref[cc, h] = st
            ks = _mm(t["kb"] * t["e_c"], st)
            o_inter = _mm(qv * t["e_c"], st)
            yield
            v_new = _mm3(a, t["vb"] - ks)
            yield
            vn_ref[rows, sl] = v_new
            o_intra = _mm(t["attn"], v_new)
            s_upd = _mm_tn(kv * t["edec"], v_new)
            yield
            o_ref[rows, sl] = o_inter + o_intra
            state[(cc + 1, h)] = st * jnp.exp(t["gl"]) + s_upd

        _interleave(head(cc, h) for cc in range(CPS) for h in range(H))
        for h in range(H):
            s_ref[h] = state[(CPS, h)]

    blk = lambda: pl.BlockSpec((CPS * C, AW), lambda n: (n, 0))
    return pl.pallas_call(
        body, name="delta_fwd", grid=(N // CPS,),
        in_specs=[blk(), blk(), blk(),
                  pl.BlockSpec((CPS * C, LANES), lambda n: (n, 0)),
                  pl.BlockSpec((CPS, 32, C), lambda n: (n, 0, 0))],
        out_specs=[blk(), blk(),
                   pl.BlockSpec((CPS, H, D, D), lambda n: (n, 0, 0, 0)),
                   pl.BlockSpec((CPS, H, C, C), lambda n: (n, 0, 0, 0))],
        out_shape=[S((T, AW), F32), S((T, AW), F32), S((N, H, D, D), F32), S((N, H, C, C), F32)],
        scratch_shapes=[pltpu.VMEM((H, D, D), F32)],
        compiler_params=_cp(ARB),
    )(q, k, v, gcol, grow)


def _delta_bwd(q, k, v, gcol, grow, ba, vnew, ssave, asave, d_o, a_log, dt_bias, H, D, carry):
    T = q.shape[0]
    C = CHUNK_A
    N = T // C
    AW = H * D
    nc = len(carry)
    CPS = 2 if N % 2 == 0 else 1
    NS = N // CPS

    def body(al_ref, dt_ref, q_ref, k_ref, v_ref, gcol_ref, grow_ref, ba_ref, vn_ref, ss_ref, as_ref, do_ref, *rest):
        cins = rest[:nc]
        dq_ref, dk_ref, dv_ref, dgate_ref, dpar_ref = rest[nc:nc + 5]
        couts = rest[nc + 5:2 * nc + 5]
        ds_ref, csend, crecv = rest[2 * nc + 5:]
        ccps = _chip_exchange_copies(cins, couts, csend, crecv)

        @pl.when(pl.program_id(0) == 0)
        def _():
            ds_ref[...] = jnp.zeros_like(ds_ref)
            dpar_ref[...] = jnp.zeros_like(dpar_ref)
            for cp in ccps:
                cp.start()

        lane = lax.broadcasted_iota(jnp.int32, (C, LANES), 1)
        rowi = lax.broadcasted_iota(jnp.int32, (C, 1), 0)
        acc = {cc: jnp.zeros((C, LANES), F32) for cc in range(CPS)}
        state = {(0, h): ds_ref[h] for h in range(H)}

        def head(oi, h):
            cc = CPS - 1 - oi
            rows = slice(cc * C, (cc + 1) * C)
            sl = slice(h * D, (h + 1) * D)
            st = ss_ref[cc, h]
            a = as_ref[cc, h]
            qv, kv, vv, dov, v_new = q_ref[rows, sl], k_ref[rows, sl], v_ref[rows, sl], do_ref[rows, sl], vn_ref[rows, sl]
            t = yield from _chunk_terms(qv, kv, vv, gcol_ref[rows, :], grow_ref[cc], h, H)
            beta_c, e_c, gam, kb = t["beta_c"], t["e_c"], t["gam"], t["kb"]
            incl, strict, attn, lmat, edec = t["incl"], t["strict"], t["attn"], t["lmat"], t["edec"]
            kdec = kv * edec
            egl = jnp.exp(t["gl"])
            qe = qv * e_c
            ekb = kb * e_c

            t1 = _mm_nt(dov, st)
            ds_o = _mm_tn(qe, dov)
            dattn_raw = _mm_nt(dov, v_new)
            dv_new_o = _mm_tn(attn, dov)
            yield
            while (oi, h) not in state:
                yield
            ds_next = state[(oi, h)]
            dkdec = _mm_nt(v_new, ds_next)
            dv_new_s = _mm(kdec, ds_next)
            yield
            dgl = egl * jnp.sum(jnp.sum(st * ds_next, axis=1, keepdims=True), axis=0, keepdims=True)
            dk = edec * dkdec
            r = jnp.sum(dkdec * kdec, axis=1, keepdims=True)
            dgc = -r
            dgl = dgl + jnp.sum(r, axis=0, keepdims=True)
            dq = e_c * t1
            dgc = dgc + jnp.sum(t1 * qe, axis=1, keepdims=True)
            dattn = jnp.where(incl, dattn_raw, 0.0)
            dv_new = dv_new_s + dv_new_o
            dqm = dattn * gam
            z = dattn * attn
            dvb = _mm3(a, dv_new, _TN)
            dq_a = _mm(dqm, kv)
            dk_a = _mm_tn(dqm, qv)
            yield
            dq_ref[rows, sl] = dq + dq_a
            dv_ref[rows, sl] = beta_c * dvb
            ds_kb = _mm_tn(ekb, dvb)
            dekb_neg = _mm_nt(dvb, st)
            dl_neg = _mm_nt(dvb, v_new)
            yield
            state[(oi + 1, h)] = egl * ds_next + ds_o - ds_kb
            dekb = -dekb_neg
            dl = jnp.where(strict, -dl_neg, 0.0)
            dp = dl * gam
            z = z + dl * lmat
            dkb_p = _mm(dp, kv)
            dk_p = _mm_tn(dp, kb)
            dgc = dgc + jnp.sum(dekb * ekb, axis=1, keepdims=True)
            dgc = dgc + jnp.sum(z, axis=1, keepdims=True) - jnp.sum(z.T, axis=1, keepdims=True)
            dgc = dgc + jnp.where(rowi == C - 1, dgl, 0.0)
            yield
            dkb = dkb_p + e_c * dekb
            dk_ref[rows, sl] = dk + dk_a + dk_p + beta_c * dkb
            dbeta = jnp.sum(dkb * kv, axis=1, keepdims=True) + jnp.sum(dvb * vv, axis=1, keepdims=True)
            acc[cc] = acc[cc] + jnp.where(lane == h, dbeta, 0.0) + jnp.where(lane == H + h, dgc, 0.0)

        _interleave(head(oi, h) for oi in range(CPS) for h in range(H))
        for h in range(H):
            ds_ref[h] = state[(CPS, h)]
        ri = lax.broadcasted_iota(jnp.int32, (C, C), 0)
        ci = lax.broadcasted_iota(jnp.int32, (C, C), 1)
        upper = (ri <= ci).astype(F32)
        dal = jnp.zeros((1, LANES), F32)
        ddt = jnp.zeros((1, LANES), F32)
        for cc in range(CPS):
            rows = slice(cc * C, (cc + 1) * C)
            gates = gcol_ref[rows, :]
            dg_all = _mm3(upper, acc[cc])
            d_braw = acc[cc] * gates * (1.0 - gates)
            d_araw = dg_all * (-jnp.exp(al_ref[...])) * _sigmoid(ba_ref[rows, :] + dt_ref[...])
            dgate_ref[rows, :] = jnp.where(lane < H, d_braw, jnp.where(lane < 2 * H, d_araw, 0.0))
            dal = dal + jnp.sum(dg_all * gates, axis=0, keepdims=True)
            ddt = ddt + jnp.sum(d_araw, axis=0, keepdims=True)
        dpar_ref[0:1, :] += dal
        dpar_ref[1:2, :] += ddt

        @pl.when(pl.program_id(0) == NS - 1)
        def _():
            for cp in ccps:
                cp.wait()

    rev = lambda s: NS - 1 - s
    blk = lambda: pl.BlockSpec((CPS * C, AW), lambda s: (rev(s), 0))
    row = pl.BlockSpec((1, LANES), lambda s: (0, 0))
    any_spec = pl.BlockSpec(memory_space=pl.ANY)
    res = pl.pallas_call(
        body, name="delta_bwd", grid=(NS,),
        in_specs=[row, row, blk(), blk(), blk(),
                  pl.BlockSpec((CPS * C, LANES), lambda s: (rev(s), 0)),
                  pl.BlockSpec((CPS, 32, C), lambda s: (rev(s), 0, 0)),
                  pl.BlockSpec((CPS * C, LANES), lambda s: (rev(s), 0)),
                  blk(),
                  pl.BlockSpec((CPS, H, D, D), lambda s: (rev(s), 0, 0, 0)),
                  pl.BlockSpec((CPS, H, C, C), lambda s: (rev(s), 0, 0, 0)),
                  blk()] + [any_spec] * nc,
        out_specs=[blk(), blk(), blk(),
                   pl.BlockSpec((CPS * C, LANES), lambda s: (rev(s), 0)),
                   pl.BlockSpec((8, LANES), lambda s: (0, 0))] + [any_spec] * nc,
        out_shape=[S((T, AW), F32), S((T, AW), F32), S((T, AW), F32),
                   S((T, LANES), F32), S((8, LANES), F32)] + [S((3,) + a.shape[1:], a.dtype) for a in carry],
        scratch_shapes=[pltpu.VMEM((H, D, D), F32),
                        pltpu.SemaphoreType.DMA((max(nc, 1), 3)), pltpu.SemaphoreType.DMA((max(nc, 1), 3))],
        compiler_params=_cp(ARB),
    )(a_log, dt_bias, q, k, v, gcol, grow, ba, vnew, ssave, asave, d_o, *carry)
    return res[:5], res[5:]


def _ln_stats(xv):
    mu = jnp.mean(xv, axis=-1, keepdims=True)
    xc = xv - mu
    var = jnp.mean(xc * xc, axis=-1, keepdims=True)
    rstd = lax.rsqrt(var + EPS)
    return xc * rstd, rstd


def _mix_fwd(o, proj, head_norm_w, ln_w, ln_b, w_sp, bs_t, H, D, G, P):
    T = o.shape[0]
    AW, BW = H * D, G * P
    MIX = AW + BW
    nb = AW // BW if AW % BW == 0 else None
    assert nb == 1, "group widths must match the projection column blocks"
    cb = 3

    def body(o_ref, za_ref, ub_ref, vb_ref, zb_ref, hw_ref, lw_ref, lb_ref, w_ref, bs_ref, out_ref):
        hw = hw_ref[...]
        for h in range(H):
            sl = slice(h * D, (h + 1) * D)
            oh = o_ref[:, sl]
            rs = lax.rsqrt(jnp.mean(oh * oh, axis=-1, keepdims=True) + EPS)
            out_ref[:, sl] = (oh * rs * hw * _silu(za_ref[:, sl])).astype(BF16)
        xhat, _ = _ln_stats(vb_ref[...])
        vn = xhat * lw_ref[...] + lb_ref[...]
        ri = lax.broadcasted_iota(jnp.int32, (P, P), 0)
        ci = lax.broadcasted_iota(jnp.int32, (P, P), 1)
        bsv = bs_ref[...]
        for g in range(G):
            sl = slice(g * P, (g + 1) * P)
            wm = jnp.where(ri >= ci, w_ref[g], 0.0)
            s = _mm(wm, vn[:, sl]) + bsv[:, g:g + 1]
            out_ref[:, AW + g * P:AW + (g + 1) * P] = (ub_ref[:, sl] * s * _silu(zb_ref[:, sl])).astype(BF16)

    row = lambda w: pl.BlockSpec((1, w), lambda i: (0, 0))
    return pl.pallas_call(
        body, name="mix_fwd", grid=(T // P,),
        in_specs=[pl.BlockSpec((P, AW), lambda i: (i, 0)),
                  pl.BlockSpec((P, AW), lambda i: (i, cb)),
                  pl.BlockSpec((P, BW), lambda i: (i, cb + 1)),
                  pl.BlockSpec((P, BW), lambda i: (i, cb + 2)),
                  pl.BlockSpec((P, BW), lambda i: (i, cb + 3)),
                  row(D), row(BW), row(BW),
                  pl.BlockSpec((G, P, P), lambda i: (0, 0, 0)),
                  pl.BlockSpec((P, G), lambda i: (0, 0))],
        out_specs=pl.BlockSpec((P, MIX), lambda i: (i, 0)),
        out_shape=S((T, MIX), BF16),
        compiler_params=_cp(ARB),
    )(o, proj, proj, proj, proj, head_norm_w, ln_w, ln_b, w_sp, bs_t)


def _mix_bwd(d_ocat, o, proj, head_norm_w, ln_w, ln_b, w_sp, bs_t, H, D, G, P, carry):
    T = o.shape[0]
    AW, BW = H * D, G * P
    MIX = AW + BW
    cb = 3
    nc = len(carry)

    def body(dc_ref, o_ref, za_ref, ub_ref, vb_ref, zb_ref, hw_ref, lw_ref, lb_ref, w_ref, bs_ref, *rest):
        cins = rest[:nc]
        do_ref, dmain_ref, dhw_ref, dln_ref, dw_ref, dbs_ref = rest[nc:nc + 6]
        couts = rest[nc + 6:2 * nc + 6]
        dvn_ref, drest_ref, out_sems, csend, crecv = rest[2 * nc + 6:]
        i = pl.program_id(0)
        slot = lax.rem(i, 2)
        ccps = _sibling_copies(cins, couts, csend, crecv)

        def out_copy(step, s):
            return pltpu.make_async_copy(
                drest_ref.at[s], dmain_ref.at[pl.ds(step * P, P), pl.ds(cb * AW, AW + 3 * BW)], out_sems.at[s])

        @pl.when(i == 0)
        def _():
            dhw_ref[...] = jnp.zeros_like(dhw_ref)
            dln_ref[...] = jnp.zeros_like(dln_ref)
            dw_ref[...] = jnp.zeros_like(dw_ref)
            dbs_ref[...] = jnp.zeros_like(dbs_ref)
            for cp in ccps:
                cp.start()

        @pl.when(i >= 2)
        def _():
            out_copy(i - 2, slot).wait()

        hw = hw_ref[...]
        dhw = jnp.zeros((1, D), F32)
        for h in range(H):
            sl = slice(h * D, (h + 1) * D)
            oh = o_ref[:, sl]
            za = za_ref[:, sl]
            doa = dc_ref[:, sl]
            rs = lax.rsqrt(jnp.mean(oh * oh, axis=-1, keepdims=True) + EPS)
            xh = oh * rs
            d_on = doa * _silu(za)
            drest_ref[slot, :, sl] = (doa * (xh * hw) * _dsilu(za)).astype(BF16)
            dhw = dhw + jnp.sum(d_on * xh, axis=0, keepdims=True)
            dxh = d_on * hw
            do_ref[:, sl] = rs * (dxh - xh * jnp.mean(dxh * xh, axis=-1, keepdims=True))
        dhw_ref[0:1, :] += dhw

        xhat, rstd = _ln_stats(vb_ref[...])
        lw = lw_ref[...]
        vn = xhat * lw + lb_ref[...]
        ri = lax.broadcasted_iota(jnp.int32, (P, P), 0)
        ci = lax.broadcasted_iota(jnp.int32, (P, P), 1)
        lane = lax.broadcasted_iota(jnp.int32, (P, LANES), 1)
        bsv = bs_ref[...]
        dbs = jnp.zeros((P, LANES), F32)
        for g in range(G):
            sl = slice(g * P, (g + 1) * P)
            wm = jnp.where(ri >= ci, w_ref[g], 0.0)
            vng = vn[:, sl]
            s = _mm(wm, vng) + bsv[:, g:g + 1]
            dob = dc_ref[:, AW + g * P:AW + (g + 1) * P]
            ub = ub_ref[:, sl]
            zb = zb_ref[:, sl]
            szb = _silu(zb)
            drest_ref[slot, :, AW + g * P:AW + (g + 1) * P] = (dob * s * szb).astype(BF16)
            drest_ref[slot, :, AW + 2 * BW + g * P:AW + 2 * BW + (g + 1) * P] = (
                dob * ub * s * _dsilu(zb)).astype(BF16)
            ds = dob * ub * szb
            dvn_ref[:, sl] = _mm_tn(wm, ds)
            dw_ref[g] += jnp.where(ri >= ci, _mm_nt(ds, vng), 0.0)
            dbs = dbs + jnp.where(lane == g, jnp.sum(ds, axis=1, keepdims=True), 0.0)
        dbs_ref[...] += dbs
        dvn = dvn_ref[...]
        dln_ref[0:1, :] += jnp.sum(dvn * xhat, axis=0, keepdims=True)
        dln_ref[1:2, :] += jnp.sum(dvn, axis=0, keepdims=True)
        dxh = dvn * lw
        dvb = rstd * (dxh - jnp.mean(dxh, axis=-1, keepdims=True) - xhat * jnp.mean(dxh * xhat, axis=-1, keepdims=True))
        drest_ref[slot, :, AW + BW:AW + 2 * BW] = dvb.astype(BF16)

        out_copy(i, slot).start()

        @pl.when(i == nstep - 1)
        def _():
            out_copy(i, slot).wait()
            if nstep > 1:
                out_copy(i - 1, 1 - slot).wait()
            for cp in ccps:
                cp.wait()

    nstep = T // P
    row = lambda w: pl.BlockSpec((1, w), lambda i: (0, 0))
    any_spec = pl.BlockSpec(memory_space=pl.ANY)
    res = pl.pallas_call(
        body, name="mix_bwd", grid=(nstep,),
        in_specs=[pl.BlockSpec((P, MIX), lambda i: (i, 0)),
                  pl.BlockSpec((P, AW), lambda i: (i, 0)),
                  pl.BlockSpec((P, AW), lambda i: (i, cb)),
                  pl.BlockSpec((P, BW), lambda i: (i, cb + 1)),
                  pl.BlockSpec((P, BW), lambda i: (i, cb + 2)),
                  pl.BlockSpec((P, BW), lambda i: (i, cb + 3)),
                  row(D), row(BW), row(BW),
                  pl.BlockSpec((G, P, P), lambda i: (0, 0, 0)),
                  pl.BlockSpec((P, G), lambda i: (0, 0))] + [any_spec] * nc,
        out_specs=[pl.BlockSpec((P, AW), lambda i: (i, 0)),
                   any_spec,
                   pl.BlockSpec((8, D), lambda i: (0, 0)),
                   pl.BlockSpec((8, BW), lambda i: (0, 0)),
                   pl.BlockSpec((G, P, P), lambda i: (0, 0, 0)),
                   pl.BlockSpec((P, LANES), lambda i: (0, 0))] + [any_spec] * nc,
        out_shape=[S((T, AW), F32), S((T, cb * AW + AW + 3 * BW), BF16), S((8, D), F32), S((8, BW), F32),
                   S((G, P, P), F32), S((P, LANES), F32)] + [S(a.shape[:1] + a.shape[2:], a.dtype) for a in carry],
        scratch_shapes=[pltpu.VMEM((P, BW), F32), pltpu.VMEM((2, P, AW + 3 * BW), BF16),
                        pltpu.SemaphoreType.DMA((2,))] + _sibling_sems(carry),
        compiler_params=_cp(ARB),
    )(d_ocat, o, proj, proj, proj, proj, head_norm_w, ln_w, ln_b, w_sp, bs_t, *carry)
    return res[:6], res[6:]


def _out_proj_loss(ocat, w_out, x, target, fnw):
    T, MIX = ocat.shape
    DM = x.shape[1]
    tm = _tile(T, 256, 8)

    def body(oc_ref, w_ref, x_ref, t_ref, fw_ref, dh_ref, dhb_ref, doc_ref, loss_ref, gfw_ref):
        @pl.when(pl.program_id(0) == 0)
        def _():
            loss_ref[...] = jnp.zeros_like(loss_ref)
            gfw_ref[...] = jnp.zeros_like(gfw_ref)

        wv = w_ref[...]
        hh = x_ref[...] + jnp.dot(oc_ref[...].astype(MXU), wv.astype(MXU), preferred_element_type=F32)
        rs = lax.rsqrt(jnp.mean(hh * hh, axis=-1, keepdims=True) + EPS)
        hn = hh * rs
        fw = fw_ref[...]
        e = hn * fw - t_ref[...]
        row_loss = 0.5 * jnp.mean(e * e, axis=-1, keepdims=True)
        loss_ref[...] += jnp.sum(row_loss, axis=0, keepdims=True)
        dy = e * (1.0 / DM)
        gfw_ref[0:1, :] += jnp.sum(dy * hn, axis=0, keepdims=True)
        dhn = dy * fw
        dh = rs * (dhn - hn * jnp.mean(dhn * hn, axis=-1, keepdims=True))
        dh_ref[...] = dh
        dhb = dh.astype(BF16)
        dhb_ref[...] = dhb
        doc_ref[...] = _mm_nt(dhb, wv)

    return pl.pallas_call(
        body, name="out_proj_loss", grid=(T // tm,),
        in_specs=[pl.BlockSpec((tm, MIX), lambda i: (i, 0)),
                  pl.BlockSpec((MIX, DM), lambda i: (0, 0)),
                  pl.BlockSpec((tm, DM), lambda i: (i, 0)),
                  pl.BlockSpec((tm, DM), lambda i: (i, 0)),
                  pl.BlockSpec((1, DM), lambda i: (0, 0))],
        out_specs=[pl.BlockSpec((tm, DM), lambda i: (i, 0)),
                   pl.BlockSpec((tm, DM), lambda i: (i, 0)),
                   pl.BlockSpec((tm, MIX), lambda i: (i, 0)),
                   pl.BlockSpec((8, LANES), lambda i: (0, 0)),
                   pl.BlockSpec((8, DM), lambda i: (0, 0))],
        out_shape=[S((T, DM), F32), S((T, DM), BF16), S((T, MIX), F32), S((8, LANES), F32), S((8, DM), F32)],
        compiler_params=_cp(ARB),
    )(ocat, w_out, x, target, fnw)


def _grad_w(lhs, rhs, name):
    T, A = lhs.shape
    B = rhs.shape[1]
    ta = _tile(A, 512, LANES)
    tk = _tile(T, 1024, 16)
    nk = T // tk

    def body(l_ref, r_ref, out_ref, acc_ref):
        k = pl.program_id(1)
        part = _mm_tn(l_ref[...], r_ref[...])

        @pl.when(k == 0)
        def _():
            acc_ref[...] = part

        @pl.when(k > 0)
        def _():
            acc_ref[...] += part

        @pl.when(k == nk - 1)
        def _():
            out_ref[...] = acc_ref[...].astype(BF16)

    return pl.pallas_call(
        body, name=name, grid=(A // ta, nk),
        in_specs=[pl.BlockSpec((tk, ta), lambda i, k: (k, i)),
                  pl.BlockSpec((tk, B), lambda i, k: (k, 0))],
        out_specs=pl.BlockSpec((ta, B), lambda i, k: (i, 0)),
        out_shape=S((A, B), BF16),
        scratch_shapes=[pltpu.VMEM((ta, B), F32)],
        compiler_params=_cp(ARB, ARB),
    )(lhs, rhs)


def _grad_w_in(xn, dmain, dba, WD, gate_lo, gate_hi):
    T, DM = xn.shape
    NM = dmain.shape[1]
    tn = _tile(NM, 1024, LANES)
    tk = _tile(T, 2048, 16)
    nj, nk = NM // tn, T // tk
    ND = N_DEV
    tiles = [[] for _ in range(nj)]
    first_tile, last_tile = {}, {}
    for d, s0, s1, dest, c0 in _pieces(WD, gate_lo, gate_hi, ND * WD):
        if dest != "main":
            continue
        while s0 < s1:
            jj = c0 // tn
            w = min(s1 - s0, (jj + 1) * tn - c0)
            tiles[jj].append((d, s0, w, "main", c0 - jj * tn))
            first_tile.setdefault(d, jj)
            last_tile[d] = jj
            s0, c0 = s0 + w, c0 + w
    for d, s0, s1, dest, c0 in _pieces(WD, gate_lo, gate_hi, ND * WD):
        if dest == "gate":
            tiles[first_tile[d]].append((d, s0, s1 - s0, "gate", c0))
    assert sorted(first_tile) == list(range(ND)) and all(last_tile[d] <= first_tile[d + 2] for d in range(ND - 2))

    def body(xn_ref, dm_ref, dba_ref, keep_ref, recv_ref, acc_ref, gate_ref, buf_ref, lsem, ssem, rsem):
        j = pl.program_id(0)
        k = pl.program_id(1)
        px, py, pc = _position()

        @pl.when(k == 0)
        def _():
            acc_ref[...] = jnp.zeros_like(acc_ref)

        @pl.when((j == 0) & (k == 0))
        def _():
            gate_ref[...] = jnp.zeros_like(gate_ref)

        xv = xn_ref[...]
        acc_ref[...] += _mm_tn(xv, dm_ref[...])

        @pl.when(j == 0)
        def _():
            gate_ref[...] += _mm_tn(xv, dba_ref[...])

        def local(d):
            return pltpu.make_async_copy(buf_ref.at[d % 2], keep_ref.at[d // 2], lsem.at[d // 2])

        def remote(d):
            return pltpu.make_async_remote_copy(
                src_ref=buf_ref.at[d % 2], dst_ref=recv_ref.at[d // 2], send_sem=ssem.at[d // 2],
                recv_sem=rsem.at[d // 2], device_id=(px, py, 1 - pc), device_id_type=MESH)

        def leave(d, start):
            @pl.when(pc == d % 2)
            def _():
                local(d).start() if start else local(d).wait()

            @pl.when(pc != d % 2)
            def _():
                remote(d).start() if start else remote(d).wait_send()

        def emit(jj):
            shards = sorted({p[0] for p in tiles[jj]})
            for d in shards:
                if first_tile[d] == jj and d >= 2:
                    leave(d - 2, False)
                for dd, s0, w, src, c0 in tiles[jj]:
                    if dd == d:
                        ref = acc_ref if src == "main" else gate_ref
                        buf_ref[d % 2, :, s0:s0 + w] = ref[:, c0:c0 + w].astype(BF16)
                if last_tile[d] == jj:
                    leave(d, True)
            if jj == nj - 1:
                for d in (ND - 2, ND - 1):
                    leave(d, False)
                for q in range(ND // 2):
                    remote(2 * q).wait_recv()

        for jj in range(nj):
            @pl.when((j == jj) & (k == nk - 1))
            def _(jj=jj):
                emit(jj)

    any_spec = pl.BlockSpec(memory_space=pl.ANY)
    return pl.pallas_call(
        body, name="grad_w_in", grid=(nj, nk),
        in_specs=[pl.BlockSpec((tk, DM), lambda j, k: (k, 0)),
                  pl.BlockSpec((tk, tn), lambda j, k: (k, j)),
                  pl.BlockSpec((tk, LANES), lambda j, k: (k, 0))],
        out_specs=[any_spec, any_spec],
        out_shape=[S((ND // 2, DM, WD), BF16), S((ND // 2, DM, WD), BF16)],
        scratch_shapes=[pltpu.VMEM((DM, tn), F32), pltpu.VMEM((DM, LANES), F32), pltpu.VMEM((2, DM, WD), BF16),
                        pltpu.SemaphoreType.DMA((ND // 2,)), pltpu.SemaphoreType.DMA((ND // 2,)),
                        pltpu.SemaphoreType.DMA((ND // 2,))],
        compiler_params=_cp(ARB, ARB),
    )(xn, dmain, dba)


def _pair_sum_plain(a, b, name):
    K, R, C = a.shape
    tr = _tile(R, 1024, 16)

    def body(a_ref, b_ref, o_ref):
        o_ref[...] = (a_ref[...].astype(F32) + b_ref[...].astype(F32)).astype(BF16)

    spec = lambda: pl.BlockSpec((1, tr, C), lambda q, i: (q, i, 0))
    return pl.pallas_call(body, name=name, grid=(K, R // tr), in_specs=[spec(), spec()], out_specs=spec(),
                          out_shape=S((K, R, C), BF16), compiler_params=_cp(ARB, ARB))(a, b)


def _dx_rows(T):
    tm = _tile(T, 512, 8)
    return tm if T // tm >= 2 else T // 2


def _dx_part(name, dmain, dba, w_main, w_ba, x, dh, norm_w, blk0, nblk, prev, hbm_in, hbm_alias, hbm_new, make_copies):
    T, NM = dmain.shape
    DM = x.shape[1]
    tm = _dx_rows(T)
    tk = _tile(NM, 1024, LANES)
    nk = NM // tk
    n_in, n_al, n_new = len(hbm_in), len(hbm_alias), len(hbm_new)
    n_prev = 0 if prev is None else 2
    last_step = nblk * nk - 1

    def body(dm_ref, dba_ref, w_ref, wba_ref, x_ref, dh_ref, nw_ref, *rest):
        r = list(rest)
        gnw_prev_ref = r.pop(0) if n_prev else None
        if n_prev:
            r.pop(0)
        in_refs = [r.pop(0) for _ in range(n_in)]
        del r[:n_al]
        gx_ref, gnw_ref = r.pop(0), r.pop(0)
        alias_refs = [r.pop(0) for _ in range(n_al)]
        new_refs = [r.pop(0) for _ in range(n_new)]
        acc_ref, send_sems, recv_sems = r
        i = pl.program_id(0)
        k = pl.program_id(1)
        step = i * nk + k
        cps = make_copies(in_refs, alias_refs, new_refs, send_sems, recv_sems)

        @pl.when(step == 0)
        def _():
            gnw_ref[...] = gnw_prev_ref[...] if n_prev else jnp.zeros_like(gnw_ref)
            for cp in cps:
                cp.start()

        @pl.when(k == 0)
        def _():
            acc_ref[...] = _mm_nt(dba_ref[...], wba_ref[...])

        acc_ref[...] += _mm_nt(dm_ref[...], w_ref[...])

        @pl.when(k == nk - 1)
        def _():
            xv = x_ref[...]
            rs = lax.rsqrt(jnp.mean(xv * xv, axis=-1, keepdims=True) + EPS)
            xh = xv * rs
            dxn = acc_ref[...]
            gnw_ref[0:1, :] += jnp.sum(dxn * xh, axis=0, keepdims=True)
            dxh = dxn * nw_ref[...]
            gx_ref[...] = dh_ref[...] + rs * (dxh - xh * jnp.mean(dxh * xh, axis=-1, keepdims=True))

        @pl.when(step == last_step)
        def _():
            for cp in cps:
                cp.wait()

    any_spec = pl.BlockSpec(memory_space=pl.ANY)
    prev_specs = [pl.BlockSpec((8, DM), lambda i, k: (0, 0)), any_spec] if n_prev else []
    prev_args = [prev[1], prev[0]] if n_prev else []
    aliases = {8: 0} if n_prev else {}
    for q in range(n_al):
        aliases[7 + n_prev + n_in + q] = 2 + q
    res = pl.pallas_call(
        body, name=name, grid=(nblk, nk),
        in_specs=[pl.BlockSpec((tm, tk), lambda i, k: (blk0 + i, k)),
                  pl.BlockSpec((tm, LANES), lambda i, k: (blk0 + i, 0)),
                  pl.BlockSpec((DM, tk), lambda i, k: (0, k)),
                  pl.BlockSpec((DM, LANES), lambda i, k: (0, 0)),
                  pl.BlockSpec((tm, DM), lambda i, k: (blk0 + i, 0)),
                  pl.BlockSpec((tm, DM), lambda i, k: (blk0 + i, 0)),
                  pl.BlockSpec((1, DM), lambda i, k: (0, 0))] + prev_specs + [any_spec] * (n_in + n_al),
        out_specs=[pl.BlockSpec((tm, DM), lambda i, k: (blk0 + i, 0)),
                   pl.BlockSpec((8, DM), lambda i, k: (0, 0))] + [any_spec] * (n_al + n_new),
        out_shape=[S((T, DM), F32), S((8, DM), F32)] + [S(a.shape, a.dtype) for a in hbm_alias] + list(hbm_new),
        scratch_shapes=[pltpu.VMEM((tm, DM), F32), pltpu.SemaphoreType.DMA((10,)), pltpu.SemaphoreType.DMA((10,))],
        input_output_aliases=aliases,
        compiler_params=_cp(ARB, ARB),
    )(dmain, dba, w_main, w_ba, x, dh, norm_w, *prev_args, *hbm_in, *hbm_alias)
    return (res[0], res[1]), res[2:2 + n_al], res[2 + n_al:]


def _remote(kk, src, dst, to, send_sems, recv_sems):
    return pltpu.make_async_remote_copy(src_ref=src, dst_ref=dst, send_sem=send_sems.at[kk], recv_sem=recv_sems.at[kk],
                                        device_id=to, device_id_type=MESH)


def _dx(dmain, dba, w_main, w_ba, x, dh, norm_w, chip_sum, small, cut):
    R, C = chip_sum.shape[1:]
    half = R // 2
    assert half % 16 == 0
    T = x.shape[0]
    ni = T // _dx_rows(T)
    cut = max(1, min(cut, ni - 1))
    upper, lower = pl.ds(0, half), pl.ds(half, half)

    def nbrs():
        px, py, pc = _position()
        return (px, py), (1 - px, py, pc), (px, 1 - py, pc)

    def phase1(ins, als, news, ss, rs):
        (px, py), xn, yn = nbrs()
        cs = ins[0]
        recv, stage = news
        bx, by, bd = cs.at[2 * (1 - px) + py], cs.at[2 * px + (1 - py)], cs.at[2 * (1 - px) + (1 - py)]
        return [_remote(0, bx.at[upper], recv.at[0].at[upper], xn, ss, rs),
                _remote(1, by.at[lower], recv.at[1].at[lower], yn, ss, rs),
                _remote(2, bd.at[upper], stage.at[0], xn, ss, rs),
                _remote(3, bd.at[lower], stage.at[1], yn, ss, rs)]

    def phase2(ins, als, news, ss, rs):
        (px, py), xn, yn = nbrs()
        comb, small_ref = ins
        recv, gath = als[0], news[0]
        me, small_cps = _broadcast_copies([small_ref], [gath], _Sem2(ss, 2), _Sem2(rs, 2))
        return ([_remote(0, comb.at[0], recv.at[1].at[upper], yn, ss, rs),
                 _remote(1, comb.at[1], recv.at[0].at[lower], xn, ss, rs)] + small_cps
                + [pltpu.make_async_copy(small_ref, gath.at[me], ss.at[9])])

    (gx, gnw), _, (recv, stage) = _dx_part(
        "dx_a", dmain, dba, w_main, w_ba, x, dh, norm_w, 0, cut, None, [chip_sum], [],
        [S((2, R, C), chip_sum.dtype), S((2, half, C), chip_sum.dtype)], phase1)
    comb = _relay_add(chip_sum, stage)
    (gx, gnw), (recv,), (gath,) = _dx_part(
        "dx_b", dmain, dba, w_main, w_ba, x, dh, norm_w, cut, ni - cut, (gx, gnw), [comb, small], [recv],
        [S((N_DEV,) + small.shape, F32)], phase2)
    return gx, gnw, gath, recv


class _Sem2:
    def __init__(self, sems, lo):
        self.sems, self.lo = sems, lo

    @property
    def at(self):
        outer = self

        class _At:
            def __getitem__(self, idx):
                a, k = idx
                return outer.sems.at[outer.lo + k]
        return _At()


def _relay_add(chip_sum, stage):
    _, R, C = chip_sum.shape
    half = R // 2
    tr = _tile(half, 256, 16)
    nt = half // tr
    px, py, _ = _position()
    idx = jnp.stack([2 * px + (1 - py), 2 * (1 - px) + py]).astype(jnp.int32)

    def body(idx_ref, p_ref, s_ref, o_ref):
        del idx_ref
        o_ref[0] = (p_ref[0].astype(F32) + s_ref[0].astype(F32)).astype(BF16)

    return pl.pallas_call(
        body, name="relay_add",
        grid_spec=pltpu.PrefetchScalarGridSpec(
            num_scalar_prefetch=1, grid=(2, nt),
            in_specs=[pl.BlockSpec((1, tr, C), lambda s, i, idx_ref: (idx_ref[s], s * nt + i, 0)),
                      pl.BlockSpec((1, tr, C), lambda s, i, idx_ref: (s, i, 0))],
            out_specs=pl.BlockSpec((1, tr, C), lambda s, i, idx_ref: (s, i, 0))),
        out_shape=S((2, half, C), BF16), compiler_params=_cp(ARB, ARB),
    )(idx, chip_sum, stage)


def _sum_slots(gath):
    _, R, C = gath.shape
    tr = R if R <= 2048 else _tile(R, 512, 8)

    def body(g_ref, o_ref):
        tot = g_ref[0]
        for d in range(1, N_DEV):
            tot = tot + g_ref[d]
        o_ref[...] = tot

    return pl.pallas_call(
        body, name="sum_slots", grid=(R // tr,),
        in_specs=[pl.BlockSpec((N_DEV, tr, C), lambda i: (0, i, 0))],
        out_specs=pl.BlockSpec((tr, C), lambda i: (i, 0)),
        out_shape=S((R, C), F32), compiler_params=_cp(ARB),
    )(gath)


def _prep_a_bwd(dq, dk, dv, c, proj, conv_w, dmain, H, D):
    T = c.shape[0]
    AW = H * D
    C3 = 3 * AW
    tb = _tile(T, 256, 8)
    nblk = T // tb
    r8 = tb // 8
    scale = float(D) ** -0.5

    def body(dq_ref, dk_ref, dv_ref, c_ref, dqn_ref, dkn_ref, dvn_ref, cn_ref, x_ref, halo_ref, cw_ref, dmain_in_ref,
             dx_ref, gcw_ref, dc_ref):
        del dmain_in_ref
        i = pl.program_id(0)

        @pl.when(i == 0)
        def _():
            gcw_ref[...] = jnp.zeros_like(gcw_ref)

        def pointwise(rows, dq_r, dk_r, dv_r, c_r, keep):
            for h in range(H):
                for part, d_r, sc in ((0, dq_r, scale), (1, dk_r, 1.0)):
                    sl = slice(part * AW + h * D, part * AW + (h + 1) * D)
                    cv = c_r[:, sl]
                    raw = _silu(cv)
                    rs = lax.rsqrt(jnp.sum(raw * raw, axis=-1, keepdims=True) + EPS)
                    nrm = raw * rs
                    dn = d_r[:, h * D:(h + 1) * D] * sc
                    draw = rs * (dn - nrm * jnp.sum(dn * nrm, axis=-1, keepdims=True))
                    dc_ref[rows, sl] = draw * _dsilu(cv) * keep
            dc_ref[rows, 2 * AW:] = dv_r[...] * _dsilu(c_r[:, 2 * AW:]) * keep

        pointwise(slice(0, tb), dq_ref, dk_ref, dv_ref, c_ref, 1.0)
        pointwise(slice(tb, tb + 8), dqn_ref, dkn_ref, dvn_ref, cn_ref, (i < nblk - 1).astype(F32))

        cw = cw_ref[...]
        dcv = dc_ref[0:tb, :]
        dx = cw[3:4, :] * dcv
        for j in range(3):
            dx = dx + cw[j:j + 1, :] * dc_ref[3 - j:3 - j + tb, :]
        dx_ref[...] = dx.astype(BF16)
        halo = halo_ref[...] * (i > 0).astype(F32)
        xp = jnp.concatenate([halo, x_ref[...]], axis=0)
        for j in range(4):
            gcw_ref[j:j + 1, :] += jnp.sum(dcv * xp[5 + j:5 + j + tb], axis=0, keepdims=True)

    nxt = lambda i: (jnp.minimum((i + 1) * r8, T // 8 - 1), 0)
    return pl.pallas_call(
        body, name="prep_a_bwd", grid=(nblk,),
        in_specs=[pl.BlockSpec((tb, AW), lambda i: (i, 0)),
                  pl.BlockSpec((tb, AW), lambda i: (i, 0)),
                  pl.BlockSpec((tb, AW), lambda i: (i, 0)),
                  pl.BlockSpec((tb, C3), lambda i: (i, 0)),
                  pl.BlockSpec((8, AW), nxt), pl.BlockSpec((8, AW), nxt), pl.BlockSpec((8, AW), nxt),
                  pl.BlockSpec((8, C3), nxt),
                  pl.BlockSpec((tb, C3), lambda i: (i, 0)),
                  pl.BlockSpec((8, C3), lambda i: (jnp.maximum(i * r8 - 1, 0), 0)),
                  pl.BlockSpec((4, C3), lambda i: (0, 0)),
                  pl.BlockSpec(memory_space=pl.ANY)],
        out_specs=[pl.BlockSpec((tb, C3), lambda i: (i, 0)),
                   pl.BlockSpec((8, C3), lambda i: (0, 0))],
        out_shape=[S(dmain.shape, dmain.dtype), S((8, C3), F32)],
        scratch_shapes=[pltpu.VMEM((tb + 8, C3), F32)],
        input_output_aliases={11: 0},
        compiler_params=_cp(ARB),
    )(dq, dk, dv, c, dq, dk, dv, c, proj, proj, conv_w, dmain)


def _adam_math(w, g, m, v):
    m2 = ADAM_B1 * m + (1.0 - ADAM_B1) * g
    v2 = ADAM_B2 * v + (1.0 - ADAM_B2) * (g * g)
    m_hat = m2 / (1.0 - ADAM_B1 ** ADAM_STEP)
    v_hat = v2 / (1.0 - ADAM_B2 ** ADAM_STEP)
    delta = -ADAM_LR * (m_hat / (jnp.sqrt(v_hat) + ADAM_EPS) + ADAM_WD * w)
    return delta, m2, v2


def _pair_sum(blocks, recv, core, name):
    K, _, R, C = blocks.shape
    tr = _tile(R, 256, 16)

    def body(core_ref, a_ref, b_ref, o_ref):
        del core_ref
        o_ref[0] = (a_ref[0, 0].astype(F32) + b_ref[0].astype(F32)).astype(BF16)

    spec = lambda: pl.BlockSpec((1, tr, C), lambda k, i, core_ref: (k, i, 0))
    return pl.pallas_call(
        body, name=name,
        grid_spec=pltpu.PrefetchScalarGridSpec(
            num_scalar_prefetch=1, grid=(K, R // tr),
            in_specs=[pl.BlockSpec((1, 1, tr, C), lambda k, i, core_ref: (k, core_ref[0], i, 0)), spec()],
            out_specs=spec()),
        out_shape=S((K, R, C), BF16), compiler_params=_cp(ARB, ARB),
    )(core, blocks, recv)


def _sum_adam(chip_sums, recv, w, m, v, chip, name, transposed=False):
    R, C = chip_sums.shape[1:]
    NR = recv.shape[0]
    tr = _tile(R, min(256, max(R // 4, 16)), 16)

    def body(chip_ref, own_ref, r_ref, w_ref, m_ref, v_ref, g_ref, d_ref, m2_ref, v2_ref):
        del chip_ref
        g = own_ref[0].astype(F32)
        for j in range(NR):
            g = g + r_ref[j].astype(F32)
        if transposed:
            g = g.T
        g_ref[...] = g
        d_ref[...], m2_ref[...], v2_ref[...] = _adam_math(w_ref[...], g, m_ref[...], v_ref[...])

    if transposed:
        spec = lambda: pl.BlockSpec((C, tr), lambda i, chip_ref: (0, i))
        shape = (C, R)
    else:
        spec = lambda: pl.BlockSpec((tr, C), lambda i, chip_ref: (i, 0))
        shape = (R, C)
    assert w.shape == shape
    return pl.pallas_call(
        body, name=name,
        grid_spec=pltpu.PrefetchScalarGridSpec(
            num_scalar_prefetch=1, grid=(R // tr,),
            in_specs=[pl.BlockSpec((1, tr, C), lambda i, chip_ref: (chip_ref[0], i, 0)),
                      pl.BlockSpec((NR, tr, C), lambda i, chip_ref: (0, i, 0)), spec(), spec(), spec()],
            out_specs=[spec(), spec(), spec(), spec()]),
        out_shape=[S(shape, F32)] * 4, compiler_params=_cp(ARB),
    )(chip, chip_sums, recv, w, m, v)


def _adam_small(ws, gs, ms, vs):
    n = len(ws)

    def body(*refs):
        ins, outs = refs[:4 * n], refs[4 * n:]
        for p in range(n):
            w_ref, g_ref, m_ref, v_ref = (ins[a * n + p] for a in range(4))
            outs[p][...], outs[n + p][...], outs[2 * n + p][...] = _adam_math(
                w_ref[...], g_ref[...], m_ref[...], v_ref[...])

    vm = pl.BlockSpec(memory_space=pltpu.VMEM)
    res = pl.pallas_call(
        body, name="adam_small", in_specs=[vm] * (4 * n), out_specs=[vm] * (3 * n),
        out_shape=[S(w.shape, F32) for w in ws] * 3,
    )(*ws, *gs, *ms, *vs)
    return res[:n], res[n:2 * n], res[2 * n:]


def _position():
    return lax.axis_index("x"), lax.axis_index("y"), lax.axis_index("c")


def _all_gather_weights(arr, x_in, norm_w, chip, tn, plans, nm):
    R = arr.shape[0]
    half = R // 2
    assert half % 16 == 0
    T, DM = x_in.shape
    tm = _tile(T, 512, 16)
    nstep = T // tm

    def body(chip_ref, x_ref, nw_ref, in_ref, xn_ref, out_ref, proj_ref, wtile_ref, stage_ref,
             send_sems, recv_sems, local_sem, stage_sems):
        i = pl.program_id(0)
        x, y, c = _position()
        me, sibling = (x, y, c), (x, y, 1 - c)
        xn, yn, diag = (1 - x, y), (x, 1 - y), (1 - x, 1 - y)
        upper, lower = pl.ds(0, half), pl.ds(half, half)

        def slot(p, rows=None):
            ref = out_ref.at[4 * p[0] + 2 * p[1] + p[2]]
            return ref if rows is None else ref.at[rows]

        def copy(kk, block, to, rows=None, src=None):
            return pltpu.make_async_remote_copy(
                src_ref=slot(block, rows) if src is None else src, dst_ref=slot(block, rows),
                send_sem=send_sems.at[kk], recv_sem=recv_sems.at[kk], device_id=to, device_id_type=MESH)

        mine = pltpu.make_async_copy(in_ref, slot(me), local_sem)
        first = [copy(0, me, sibling, src=in_ref), copy(1, me, (*xn, c), src=in_ref), copy(2, me, (*yn, c), src=in_ref)]

        @pl.when(i == 0)
        def _():
            mine.start()
            for cp in first:
                cp.start()
            copy(0, sibling, me).wait_recv()
            loads = [pltpu.make_async_copy(in_ref, stage_ref.at[c], stage_sems.at[0]),
                     pltpu.make_async_copy(slot(sibling), stage_ref.at[1 - c], stage_sems.at[1])]
            for cp in loads:
                cp.start()
            for cp in loads:
                cp.wait()
            for m, plan in enumerate(plans):
                @pl.when(chip_ref[0] == m)
                def _(plan=plan):
                    for d, s0, w, c0 in plan:
                        wtile_ref[:, c0:c0 + w] = stage_ref[d % 2, :, s0:s0 + w]

        xv = x_ref[...]
        r = lax.rsqrt(jnp.mean(xv * xv, axis=-1, keepdims=True) + EPS)
        xnv = (xv * r * nw_ref[...]).astype(BF16)
        xn_ref[...] = xnv
        proj_ref[...] = jnp.dot(xnv.astype(MXU), wtile_ref[...].astype(MXU), preferred_element_type=F32)

        @pl.when(i == nstep - 1)
        def _():
            sent = list(first)

            def then(cps):
                for cp in cps:
                    cp.start()
                sent.extend(cps)

            copy(1, (*xn, c), me).wait_recv()
            then([copy(5, (*xn, c), (*yn, c), rows=upper), copy(3, (*xn, c), sibling)])
            copy(2, (*yn, c), me).wait_recv()
            then([copy(6, (*yn, c), (*xn, c), rows=lower), copy(4, (*yn, c), sibling)])
            copy(5, (*diag, c), me, rows=upper).wait_recv()
            then([copy(7, (*diag, c), sibling, rows=upper)])
            copy(6, (*diag, c), me, rows=lower).wait_recv()
            then([copy(8, (*diag, c), sibling, rows=lower)])
            copy(3, (*xn, 1 - c), me).wait_recv()
            copy(4, (*yn, 1 - c), me).wait_recv()
            copy(7, (*diag, 1 - c), me, rows=upper).wait_recv()
            copy(8, (*diag, 1 - c), me, rows=lower).wait_recv()
            for cp in sent:
                cp.wait_send()
            mine.wait()

    any_spec = pl.BlockSpec(memory_space=pl.ANY)
    return pl.pallas_call(
        body, name="all_gather_weights",
        grid_spec=pltpu.PrefetchScalarGridSpec(
            num_scalar_prefetch=1, grid=(nstep,),
            in_specs=[pl.BlockSpec((tm, DM), lambda i, chip_ref: (i, 0)),
                      pl.BlockSpec((1, DM), lambda i, chip_ref: (0, 0)), any_spec],
            out_specs=[pl.BlockSpec((tm, DM), lambda i, chip_ref: (i, 0)), any_spec,
                       pl.BlockSpec((tm, tn), lambda i, chip_ref: (i, 2 * chip_ref[0]))],
            scratch_shapes=[pltpu.VMEM((DM, tn), arr.dtype), pltpu.VMEM((2,) + arr.shape, arr.dtype),
                            pltpu.SemaphoreType.DMA((9,)), pltpu.SemaphoreType.DMA((9,)), pltpu.SemaphoreType.DMA,
                            pltpu.SemaphoreType.DMA((2,))]),
        out_shape=[S((T, DM), BF16), S((N_DEV,) + arr.shape, arr.dtype), S((T, nm), F32)],
        compiler_params=_cp(ARB),
    )(chip, x_in, norm_w, arr)


def _sibling_copies(ins, outs, send_sems, recv_sems):
    x, y, c = _position()
    return [pltpu.make_async_remote_copy(src_ref=ins[a].at[k, 1 - c], dst_ref=outs[a].at[k],
                                         send_sem=send_sems.at[a, k], recv_sem=recv_sems.at[a, k],
                                         device_id=(x, y, 1 - c), device_id_type=MESH)
            for a in range(len(ins)) for k in range(ins[a].shape[0])]


def _sibling_sems(arrs):
    shape = (max(len(arrs), 1), arrs[0].shape[0] if arrs else 1)
    return [pltpu.SemaphoreType.DMA(shape), pltpu.SemaphoreType.DMA(shape)]


def _chip_exchange_copies(ins, outs, send_sems, recv_sems):
    x, y, c = _position()
    chips = [(1 - x, y), (x, 1 - y), (1 - x, 1 - y)]
    return [pltpu.make_async_remote_copy(
        src_ref=ins[a].at[2 * qx + qy], dst_ref=outs[a].at[j], send_sem=send_sems.at[a, j],
        recv_sem=recv_sems.at[a, j], device_id=(qx, qy, c), device_id_type=MESH)
        for a in range(len(ins)) for j, (qx, qy) in enumerate(chips)]


def _broadcast_copies(srcs, dsts, send_sems, recv_sems):
    x, y, c = _position()
    me = 4 * x + 2 * y + c
    cps = []
    for a in range(len(srcs)):
        for k in range(1, N_DEV):
            peer = (1 - x if k & 4 else x, 1 - y if k & 2 else y, 1 - c if k & 1 else c)
            cps.append(pltpu.make_async_remote_copy(
                src_ref=srcs[a], dst_ref=dsts[a].at[me], send_sem=send_sems.at[a, k - 1],
                recv_sem=recv_sems.at[a, k - 1], device_id=peer, device_id_type=MESH))
    return me, cps


def _all_reduce_small(part):
    R, C = part.shape

    def body(p_ref, out_ref, gath_ref, send_sems, recv_sems):
        me, cps = _broadcast_copies([p_ref], [gath_ref], send_sems, recv_sems)
        gath_ref[me] = p_ref[...]
        for cp in cps:
            cp.start()
        for cp in cps:
            cp.wait()
        acc = gath_ref[0]
        for d in range(1, N_DEV):
            acc = acc + gath_ref[d]
        out_ref[...] = acc

    vm = pl.BlockSpec(memory_space=pltpu.VMEM)
    return pl.pallas_call(
        body, name="all_reduce_small", in_specs=[vm], out_specs=vm, out_shape=S((R, C), F32),
        scratch_shapes=[pltpu.VMEM((N_DEV, R, C), F32), pltpu.SemaphoreType.DMA((1, N_DEV - 1)),
                        pltpu.SemaphoreType.DMA((1, N_DEV - 1))],
    )(part)


def _pack(parts):
    rows = []
    for p in parts:
        f = p.reshape(-1).astype(F32)
        pad = (-f.shape[0]) % (8 * LANES)
        rows.append(jnp.pad(f, (0, pad)).reshape(-1, LANES))
    return jnp.concatenate(rows, axis=0)


def _unpack(buf, shapes):
    out, r = [], 0
    for shp in shapes:
        n = 1
        for s in shp:
            n *= s
        nr = -(-n // (8 * LANES)) * 8
        out.append(buf[r:r + nr].reshape(-1)[:n].reshape(shp))
        r += nr
    return out


def kernel(x, norm_w, w_in, conv_w, a_log, dt_bias, head_norm_w, sgu_ln_w, sgu_ln_b, w_spatial, b_spatial, w_out, final_norm_w, loss_target, m_norm_w, m_w_in, m_conv_w, m_a_log, m_dt_bias, m_head_norm_w, m_sgu_ln_w, m_sgu_ln_b, m_w_spatial, m_b_spatial, m_w_out, m_final_norm_w, v_norm_w, v_w_in, v_conv_w, v_a_log, v_dt_bias, v_head_norm_w, v_sgu_ln_w, v_sgu_ln_b, v_w_spatial, v_b_spatial, v_w_out, v_final_norm_w):
    T, DM = x.shape[1], x.shape[2]
    H, D = a_log.shape[1], head_norm_w.shape[1]
    G, P = w_spatial.shape[1], w_spatial.shape[2]
    AW, BW = H * D, G * P
    MIX = AW + BW
    WD = w_in.shape[2]
    IN = N_DEV * WD
    RO = w_out.shape[1]
    CW = conv_w.shape[2]
    sizes = (3 * AW, AW, H, H, BW, BW, BW)
    assert sum(sizes) == IN and 2 * H <= LANES and 3 * H <= 32 and N_DEV * RO == MIX and N_DEV * CW == 3 * AW
    offs = [0]
    for s in sizes:
        offs.append(offs[-1] + s)
    px, py, pc = _position()
    dev = 4 * px + 2 * py + pc
    chip = 2 * px + py

    x2, tgt = x[0], loss_target[0]

    core_idx = jnp.reshape(pc, (1,)).astype(jnp.int32)
    chip_idx = jnp.reshape(chip, (1,)).astype(jnp.int32)
    NM = IN - 2 * H
    tn_loc, tile_plans = _local_tiles(WD, offs[2], offs[4], NM)
    xn, g_win, proj_part = _all_gather_weights(
        _cast_bf16_t(w_in[0].T, "cast_w_in"), x2, norm_w, chip_idx, tn_loc, tile_plans, NM)
    w_main, w_ba = _relayout_w(g_win, offs[2], offs[4])
    alog_row = jnp.pad(a_log, ((0, 0), (H, LANES - 2 * H)))
    dtb_row = jnp.pad(dt_bias, ((0, 0), (H, LANES - 2 * H)))
    bs_t = b_spatial[0].T

    others = jnp.arange(N_DEV - 2, dtype=jnp.int32)
    others = others + (others >= 2 * chip).astype(jnp.int32)
    proj, ba, (g_wout, g_conv) = _in_proj(xn, w_main, w_ba, proj_part, others, tn_loc,
                                          [_cast_bf16(w_out[0], "cast_w_out"), conv_w[0]])
    w_out_full = g_wout.reshape(MIX, DM)
    conv_full = g_conv.transpose(1, 0, 2).reshape(4, 3 * AW)
    q, k, v, c, gcol, grow = _prep_a_fwd(proj, ba, conv_full, alog_row, dtb_row, H, D)
    o, vnew, ssave, asave = _delta_fwd(q, k, v, gcol, grow, H, D)
    ocat = _mix_fwd(o, proj, head_norm_w, sgu_ln_w, sgu_ln_b, w_spatial[0], bs_t, H, D, G, P)
    dh, dh_bf, d_ocat, loss_acc, g_fnw = _out_proj_loss(ocat, w_out_full, x2, tgt, final_norm_w.reshape(1, DM))

    g_wout_blocks = _grad_w(ocat, dh_bf, "grad_w_out").reshape(4, 2, RO, DM)
    (d_o, dmain, g_hnw, g_ln, g_wsp, g_bs_t), (sib_wout,) = _mix_bwd(
        d_ocat, o, proj, head_norm_w, sgu_ln_w, sgu_ln_b, w_spatial[0], bs_t, H, D, G, P, [g_wout_blocks])
    chip_wout = _pair_sum(g_wout_blocks, sib_wout, core_idx, "pair_sum_w_out")
    (dq, dk, dv, dgate, dpar), (recv_wout,) = _delta_bwd(
        q, k, v, gcol, grow, ba, vnew, ssave, asave, d_o, alog_row, dtb_row, H, D, [chip_wout])
    dmain, g_conv_part = _prep_a_bwd(dq, dk, dv, c, proj, conv_full, dmain, H, D)
    dba = dgate.astype(BF16)
    keep_win, sib_win = _grad_w_in(xn, dmain, dba, WD, offs[2], offs[4])
    chip_win = _pair_sum_plain(keep_win, sib_win, "pair_sum_w_in")
    small_shapes = [a_log.shape, dt_bias.shape, head_norm_w.shape, sgu_ln_w.shape, sgu_ln_b.shape,
                    w_spatial.shape, b_spatial.shape, final_norm_w.shape]
    parts = [dpar[0, H:2 * H], dpar[1, H:2 * H], g_hnw[0], g_ln[0], g_ln[1], g_wsp, g_bs_t[:, :G].T, g_fnw[0],
             g_conv_part[:4], loss_acc[0, :1]]
    grad_x, g_nw, small_gath, recv_win = _dx(dmain, dba, w_main, w_ba, x2, dh, norm_w, chip_win, _pack(parts), 4)
    red = _sum_slots(small_gath)
    grad_w_in, delta_w_in, new_m_w_in, new_v_w_in = _sum_adam(
        chip_win, recv_win, w_in[0].T, m_w_in[0].T, v_w_in[0].T, chip_idx, "sum_adam_w_in", transposed=True)
    grad_w_out, delta_w_out, new_m_w_out, new_v_w_out = _sum_adam(
        chip_wout, recv_wout, w_out[0], m_w_out[0], v_w_out[0], chip_idx, "sum_adam_w_out")
    red_nw = _all_reduce_small(_pack([g_nw[0]]))
    grads_small = _unpack(red_nw, [norm_w.shape]) + _unpack(red, small_shapes + [(4, 3 * AW), (1,)])
    loss = grads_small.pop()[0]
    g_conv_full = grads_small.pop()
    grad_conv = lax.dynamic_slice_in_dim(g_conv_full, dev * CW, CW, axis=1)[None]
    small_w = [norm_w, a_log, dt_bias, head_norm_w, sgu_ln_w, sgu_ln_b, w_spatial, b_spatial, final_norm_w, conv_w]
    small_m = [m_norm_w, m_a_log, m_dt_bias, m_head_norm_w, m_sgu_ln_w, m_sgu_ln_b, m_w_spatial, m_b_spatial,
               m_final_norm_w, m_conv_w]
    small_v = [v_norm_w, v_a_log, v_dt_bias, v_head_norm_w, v_sgu_ln_w, v_sgu_ln_b, v_w_spatial, v_b_spatial,
               v_final_norm_w, v_conv_w]
    small_g = grads_small + [grad_conv]
    d_s, m_s, v_s = _adam_small(small_w, small_g, small_m, small_v)

    def order(small, win, wout):
        return [small[0], win.T[None], small[9], small[1], small[2], small[3], small[4], small[5], small[6], small[7],
                wout[None], small[8]]

    grads = order(small_g, grad_w_in, grad_w_out)
    deltas = order(d_s, delta_w_in, delta_w_out)
    new_m = order(m_s, new_m_w_in, new_m_w_out)
    new_v = order(v_s, new_v_w_in, new_v_w_out)
    return (loss, grad_x[None], *grads, *deltas, *new_m, *new_v)
```

```python
import jax
import jax.numpy as jnp
from jax import lax
from jax.experimental import pallas as pl
from jax.experimental.pallas import tpu as pltpu

F32 = jnp.float32
BF16 = jnp.bfloat16
MXU = jnp.bfloat16
HI = lax.Precision.HIGHEST
EPS = 1e-6
CHUNK_A = 64
LANES = 128
MESH = pl.DeviceIdType.MESH
N_DEV = 8

ADAM_LR = 0.001
ADAM_B1 = 0.9
ADAM_B2 = 0.999
ADAM_EPS = 1e-08
ADAM_WD = 0.01
ADAM_STEP = 10

S = jax.ShapeDtypeStruct
ARB = "arbitrary"


def _cp(*sem, vmem_mib=56):
    return pltpu.CompilerParams(dimension_semantics=tuple(sem), vmem_limit_bytes=vmem_mib * 1024 * 1024)


def _tile(n, cap, mult):
    best = None
    t = mult
    while t <= min(n, cap):
        if n % t == 0:
            best = t
        t += mult
    return best if best is not None else n


def _mm(a, b):
    return jnp.dot(a.astype(MXU), b.astype(MXU), preferred_element_type=F32)


def _mm_nt(a, b):
    return lax.dot_general(a.astype(MXU), b.astype(MXU), (((1,), (1,)), ((), ())), preferred_element_type=F32)


def _mm_tn(a, b):
    return lax.dot_general(a.astype(MXU), b.astype(MXU), (((0,), (0,)), ((), ())), preferred_element_type=F32)


def _mmh(a, b):
    return jnp.dot(a, b, precision=HI, preferred_element_type=F32)


def _sigmoid(x):
    return 1.0 / (1.0 + jnp.exp(-x))


def _silu(x):
    return x * _sigmoid(x)


def _dsilu(x):
    s = _sigmoid(x)
    return s * (1.0 + x * (1.0 - s))


def _softplus(x):
    return jnp.maximum(x, 0.0) + jnp.log(1.0 + jnp.exp(-jnp.abs(x)))


def _pieces(wd, gate_lo, gate_hi, total):
    out = []
    for d in range(N_DEV):
        lo, hi = d * wd, (d + 1) * wd
        for dest, a, b, shift in (("main", 0, gate_lo, 0), ("gate", gate_lo, gate_hi, -gate_lo),
                                  ("main", gate_hi, total, gate_lo - gate_hi)):
            s0, s1 = max(lo, a), min(hi, b)
            if s0 < s1:
                out.append((d, s0 - lo, s1 - lo, dest, s0 + shift))
    return out


def _local_tiles(wd, gate_lo, gate_hi, nm):
    n_tiles = N_DEV - 1
    assert nm % (n_tiles * LANES) == 0
    tn = nm // n_tiles
    plans = []
    for m in range(N_DEV // 2):
        lo, hi = 2 * m * tn, (2 * m + 1) * tn
        plan = []
        for d, s0, s1, dest, c0 in _pieces(wd, gate_lo, gate_hi, N_DEV * wd):
            if dest != "main":
                continue
            a, b = max(c0, lo), min(c0 + (s1 - s0), hi)
            if a < b:
                assert d // 2 == m, "tile 2m must come from chip m's own shards"
                plan.append((d, s0 + (a - c0), b - a, a - lo))
        assert sum(p[2] for p in plan) == tn
        plans.append(plan)
    return tn, plans


def _cast_bf16(a, name):
    R, C = a.shape
    tr = _tile(R, 256, 16)

    def body(a_ref, o_ref):
        o_ref[...] = a_ref[...].astype(BF16)

    spec = pl.BlockSpec((tr, C), lambda i: (i, 0))
    return pl.pallas_call(body, name=name, grid=(R // tr,), in_specs=[spec], out_specs=spec,
                          out_shape=S((R, C), BF16), compiler_params=_cp(ARB))(a)


def _cast_bf16_t(a_t, name):
    C, R = a_t.shape
    tr = _tile(R, 256, LANES)

    def body(a_ref, o_ref):
        o_ref[...] = a_ref[...].T.astype(BF16)

    return pl.pallas_call(body, name=name, grid=(R // tr,), in_specs=[pl.BlockSpec((C, tr), lambda i: (0, i))],
                          out_specs=pl.BlockSpec((tr, C), lambda i: (i, 0)),
                          out_shape=S((R, C), BF16), compiler_params=_cp(ARB))(a_t)


def _relayout_w(g_win, gate_lo, gate_hi):
    _, DM, WD = g_win.shape
    total = N_DEV * WD
    NM = total - (gate_hi - gate_lo)
    tr = _tile(DM, 256, 16)
    plan = _pieces(WD, gate_lo, gate_hi, total)

    def body(g_ref, main_ref, gate_ref):
        gate_ref[...] = jnp.zeros_like(gate_ref)
        for d, s0, s1, dest, c0 in plan:
            dst = main_ref if dest == "main" else gate_ref
            dst[:, c0:c0 + (s1 - s0)] = g_ref[d, :, s0:s1]

    return pl.pallas_call(
        body, name="relayout_w", grid=(DM // tr,),
        in_specs=[pl.BlockSpec((N_DEV, tr, WD), lambda i: (0, i, 0))],
        out_specs=[pl.BlockSpec((tr, NM), lambda i: (i, 0)), pl.BlockSpec((tr, LANES), lambda i: (i, 0))],
        out_shape=[S((DM, NM), g_win.dtype), S((DM, LANES), g_win.dtype)],
        compiler_params=_cp(ARB),
    )(g_win)


def _in_proj(xn, w_main, w_ba, proj_part, tiles, tn, shards):
    T, DM = xn.shape
    NM = w_main.shape[1]
    tm = _tile(T, 2048, 16)
    ni, nj = T // tm, tiles.shape[0]
    ns = len(shards)

    def body(tiles_ref, xn_ref, w_ref, wba_ref, part_ref, *rest):
        del tiles_ref, part_ref
        srcs = rest[:ns]
        proj_ref, ba_ref = rest[ns:ns + 2]
        gath = rest[ns + 2:2 * ns + 2]
        send_sems, recv_sems, local_sems = rest[2 * ns + 2:]
        i = pl.program_id(0)
        j = pl.program_id(1)
        me, cps = _broadcast_copies(srcs, gath, send_sems, recv_sems)
        cps = cps + [pltpu.make_async_copy(srcs[a], gath[a].at[me], local_sems.at[a]) for a in range(ns)]

        @pl.when((i == 0) & (j == 0))
        def _():
            for cp in cps:
                cp.start()

        @pl.when(j == 0)
        def _():
            ba_ref[...] = jnp.dot(xn_ref[...].astype(MXU), wba_ref[...].astype(MXU), preferred_element_type=F32)

        proj_ref[...] = jnp.dot(xn_ref[...].astype(MXU), w_ref[...].astype(MXU), preferred_element_type=F32)

        @pl.when((i == ni - 1) & (j == nj - 1))
        def _():
            for cp in cps:
                cp.wait()

    any_spec = pl.BlockSpec(memory_space=pl.ANY)
    res = pl.pallas_call(
        body, name="in_proj",
        grid_spec=pltpu.PrefetchScalarGridSpec(
            num_scalar_prefetch=1, grid=(ni, nj),
            in_specs=[pl.BlockSpec((tm, DM), lambda i, j, t: (i, 0)),
                      pl.BlockSpec((DM, tn), lambda i, j, t: (0, t[j])),
                      pl.BlockSpec((DM, LANES), lambda i, j, t: (0, 0)), any_spec] + [any_spec] * ns,
            out_specs=[pl.BlockSpec((tm, tn), lambda i, j, t: (i, t[j])),
                       pl.BlockSpec((tm, LANES), lambda i, j, t: (i, 0))] + [any_spec] * ns,
            scratch_shapes=[pltpu.SemaphoreType.DMA((ns, N_DEV - 1)), pltpu.SemaphoreType.DMA((ns, N_DEV - 1)),
                            pltpu.SemaphoreType.DMA((ns,))]),
        out_shape=[S((T, NM), F32), S((T, LANES), F32)] + [S((N_DEV,) + a.shape, a.dtype) for a in shards],
        input_output_aliases={4: 0},
        compiler_params=_cp(ARB, ARB, vmem_mib=58),
    )(tiles, xn, w_main, w_ba, proj_part, *shards)
    return res[0], res[1], res[2:]


def _prep_a_fwd(proj, ba, conv_w, alog_row, dtb_row, H, D):
    T = proj.shape[0]
    AW = H * D
    C3 = 3 * AW
    tb = _tile(T, 256, CHUNK_A)
    nch = tb // CHUNK_A
    nblk = T // tb
    scale = float(D) ** -0.5

    def body(x_ref, halo_ref, ba_ref, cw_ref, al_ref, dt_ref, q_ref, k_ref, v_ref, c_ref, gcol_ref, grow_ref):
        i = pl.program_id(0)
        xv = x_ref[...]
        halo = halo_ref[...] * (i > 0).astype(F32)
        xp = jnp.concatenate([halo, xv], axis=0)
        cw = cw_ref[...]
        c = cw[0:1, :] * xp[5:5 + tb]
        for j in range(1, 4):
            c = c + cw[j:j + 1, :] * xp[5 + j:5 + j + tb]
        c_ref[...] = c
        a = _silu(c)
        for h in range(H):
            qh = a[:, h * D:(h + 1) * D]
            kh = a[:, AW + h * D:AW + (h + 1) * D]
            qr = lax.rsqrt(jnp.sum(qh * qh, axis=-1, keepdims=True) + EPS)
            kr = lax.rsqrt(jnp.sum(kh * kh, axis=-1, keepdims=True) + EPS)
            q_ref[:, h * D:(h + 1) * D] = qh * (qr * scale)
            k_ref[:, h * D:(h + 1) * D] = kh * kr
        v_ref[...] = a[:, 2 * AW:]

        bav = ba_ref[...]
        lane = lax.broadcasted_iota(jnp.int32, (tb, LANES), 1)
        beta = _sigmoid(bav)
        g = -jnp.exp(al_ref[...]) * _softplus(bav + dt_ref[...])
        gates = jnp.where(lane < H, beta, jnp.where(lane < 2 * H, g, 0.0))
        ri = lax.broadcasted_iota(jnp.int32, (CHUNK_A, CHUNK_A), 0)
        ci = lax.broadcasted_iota(jnp.int32, (CHUNK_A, CHUNK_A), 1)
        tri = (ri >= ci).astype(F32)
        lane_c = lax.broadcasted_iota(jnp.int32, (CHUNK_A, LANES), 1)
        for cc in range(nch):
            gch = gates[cc * CHUNK_A:(cc + 1) * CHUNK_A]
            gc = pltpu.roll(_mmh(tri, gch), H, 1)
            full = jnp.where(lane_c < 2 * H, gch, jnp.where(lane_c < 3 * H, gc, 0.0))
            gcol_ref[cc * CHUNK_A:(cc + 1) * CHUNK_A, :] = full
            grow_ref[cc] = full.T[0:32, :]

    return pl.pallas_call(
        body, name="prep_a_fwd", grid=(nblk,),
        in_specs=[pl.BlockSpec((tb, C3), lambda i: (i, 0)),
                  pl.BlockSpec((8, C3), lambda i: (jnp.maximum(i * (tb // 8) - 1, 0), 0)),
                  pl.BlockSpec((tb, LANES), lambda i: (i, 0)),
                  pl.BlockSpec((4, C3), lambda i: (0, 0)),
                  pl.BlockSpec((1, LANES), lambda i: (0, 0)),
                  pl.BlockSpec((1, LANES), lambda i: (0, 0))],
        out_specs=[pl.BlockSpec((tb, AW), lambda i: (i, 0)),
                   pl.BlockSpec((tb, AW), lambda i: (i, 0)),
                   pl.BlockSpec((tb, AW), lambda i: (i, 0)),
                   pl.BlockSpec((tb, C3), lambda i: (i, 0)),
                   pl.BlockSpec((tb, LANES), lambda i: (i, 0)),
                   pl.BlockSpec((nch, 32, CHUNK_A), lambda i: (i, 0, 0))],
        out_shape=[S((T, AW), F32), S((T, AW), F32), S((T, AW), F32), S((T, C3), F32),
                   S((T, LANES), F32), S((T // CHUNK_A, 32, CHUNK_A), F32)],
        compiler_params=_cp(ARB),
    )(proj, proj, ba, conv_w, alog_row, dtb_row)


_NN = (((1,), (0,)), ((), ()))
_TN = (((0,), (0,)), ((), ()))


def _split(a):
    hi = a.astype(BF16)
    return hi, (a - hi.astype(F32)).astype(BF16)


def _mm3(a, b, dims=_NN):
    ah, al = a if isinstance(a, tuple) else _split(a)
    bh, bl = b if isinstance(b, tuple) else _split(b)
    dg = lambda p, r: lax.dot_general(p, r, dims, preferred_element_type=F32)
    return dg(ah, bh) + (dg(ah, bl) + dg(al, bh))


def _interleave(gens):
    gens = list(gens)
    while gens:
        alive = []
        for g in gens:
            try:
                next(g)
                alive.append(g)
            except StopIteration:
                pass
        gens = alive


def _chunk_terms(q, k, v, gcolv, growv, h, H):
    C = CHUNK_A
    beta_c = gcolv[:, h:h + 1]
    g_c = gcolv[:, H + h:H + h + 1]
    gc_c = gcolv[:, 2 * H + h:2 * H + h + 1]
    gc_r = growv[2 * H + h:2 * H + h + 1, :]
    ri = lax.broadcasted_iota(jnp.int32, (C, C), 0)
    ci = lax.broadcasted_iota(jnp.int32, (C, C), 1)
    incl = ri >= ci
    strict = ri > ci
    kb = k * beta_c
    vb = v * beta_c
    p_raw = _mm_nt(kb, k)
    qk_raw = _mm_nt(q, k)
    gam = jnp.where(incl, jnp.exp(jnp.where(incl, gc_c - gc_r, 0.0)), 0.0)
    e_c = jnp.exp(gc_c)
    gl = gc_r[:, C - 1:C]
    edec = jnp.exp(gl - gc_c)
    yield
    lmat = jnp.where(strict, p_raw * gam, 0.0)
    attn = jnp.where(incl, qk_raw * gam, 0.0)
    return dict(beta_c=beta_c, g_c=g_c, gc_c=gc_c, gc_r=gc_r, incl=incl, strict=strict, gam=gam, e_c=e_c,
                kb=kb, vb=vb, lmat=lmat, attn=attn, gl=gl, edec=edec, ri=ri, ci=ci)


INV_BLOCK = 16


def _inv_unit_lower(lmat):
    C = lmat.shape[0]
    ri = lax.broadcasted_iota(jnp.int32, (C, C), 0)
    ci = lax.broadcasted_iota(jnp.int32, (C, C), 1)
    eye = (ri == ci).astype(F32)
    same = (ri // INV_BLOCK) == (ci // INV_BLOCK)

    def neumann(x, order):
        a = eye + x
        n = 1
        while 2 * n < order:
            xs = _split(x)
            x = _mm3(xs, xs)
            yield
            a = a + _mm3(a, x)
            n *= 2
        yield
        return a

    inv_d = yield from neumann(-jnp.where(same, lmat, 0.0), INV_BLOCK)
    m = _mm3(inv_d, jnp.where(same, 0.0, lmat))
    yield
    inv_m = yield from neumann(-m, C // INV_BLOCK)
    a = _mm3(inv_m, inv_d)
    yield
    return a


def _delta_fwd(q, k, v, gcol, grow, H, D):
    T = q.shape[0]
    C = CHUNK_A
    N = T // C
    AW = H * D
    CPS = 2 if N % 2 == 0 else 1

    def body(q_ref, k_ref, v_ref, gcol_ref, grow_ref, o_ref, vn_ref, ssave_ref, asave_ref, s_ref):
        @pl.when(pl.program_id(0) == 0)
        def _():
            s_ref[...] = jnp.zeros_like(s_ref)

        state = {(0, h): s_ref[h] for h in range(H)}

        def head(cc, h):
            rows = slice(cc * C, (cc + 1) * C)
            sl = slice(h * D, (h + 1) * D)
            qv, kv, vv = q_ref[rows, sl], k_ref[rows, sl], v_ref[rows, sl]
            t = yield from _chunk_terms(qv, kv, vv, gcol_ref[rows, :], grow_ref[cc], h, H)
            a = yield from _inv_unit_lower(t["lmat"])
            asave_ref[cc, h] = a
            while (cc, h) not in state:
                yield
            st = state[(cc, h)]
            ssave_ref[cc, h] = st
            ks = _mm(t["kb"] * t["e_c"], st)
            o_inter = _mm(qv * t["e_c"], st)
            yield
            v_new = _mm3(a, t["vb"] - ks)
            yield
            vn_ref[rows, sl] = v_new
            o_intra = _mm(t["attn"], v_new)
            s_upd = _mm_tn(kv * t["edec"], v_new)
            yield
            o_ref[rows, sl] = o_inter + o_intra
            state[(cc + 1, h)] = st * jnp.exp(t["gl"]) + s_upd

        _interleave(head(cc, h) for cc in range(CPS) for h in range(H))
        for h in range(H):
            s_ref[h] = state[(CPS, h)]

    blk = lambda: pl.BlockSpec((CPS * C, AW), lambda n: (n, 0))
    return pl.pallas_call(
        body, name="delta_fwd", grid=(N // CPS,),
        in_specs=[blk(), blk(), blk(),
                  pl.BlockSpec((CPS * C, LANES), lambda n: (n, 0)),
                  pl.BlockSpec((CPS, 32, C), lambda n: (n, 0, 0))],
        out_specs=[blk(), blk(),
                   pl.BlockSpec((CPS, H, D, D), lambda n: (n, 0, 0, 0)),
                   pl.BlockSpec((CPS, H, C, C), lambda n: (n, 0, 0, 0))],
        out_shape=[S((T, AW), F32), S((T, AW), F32), S((N, H, D, D), F32), S((N, H, C, C), F32)],
        scratch_shapes=[pltpu.VMEM((H, D, D), F32)],
        compiler_params=_cp(ARB),
    )(q, k, v, gcol, grow)


def _delta_bwd(q, k, v, gcol, grow, ba, vnew, ssave, asave, d_o, a_log, dt_bias, H, D, carry):
    T = q.shape[0]
    C = CHUNK_A
    N = T // C
    AW = H * D
    nc = len(carry)
    CPS = 2 if N % 2 == 0 else 1
    NS = N // CPS

    def body(al_ref, dt_ref, q_ref, k_ref, v_ref, gcol_ref, grow_ref, ba_ref, vn_ref, ss_ref, as_ref, do_ref, *rest):
        cins = rest[:nc]
        dq_ref, dk_ref, dv_ref, dgate_ref, dpar_ref = rest[nc:nc + 5]
        couts = rest[nc + 5:2 * nc + 5]
        ds_ref, csend, crecv = rest[2 * nc + 5:]
        ccps = _chip_exchange_copies(cins, couts, csend, crecv)

        @pl.when(pl.program_id(0) == 0)
        def _():
            ds_ref[...] = jnp.zeros_like(ds_ref)
            dpar_ref[...] = jnp.zeros_like(dpar_ref)
            for cp in ccps:
                cp.start()

        lane = lax.broadcasted_iota(jnp.int32, (C, LANES), 1)
        rowi = lax.broadcasted_iota(jnp.int32, (C, 1), 0)
        acc = {cc: jnp.zeros((C, LANES), F32) for cc in range(CPS)}
        state = {(0, h): ds_ref[h] for h in range(H)}

        def head(oi, h):
            cc = CPS - 1 - oi
            rows = slice(cc * C, (cc + 1) * C)
            sl = slice(h * D, (h + 1) * D)
            st = ss_ref[cc, h]
            a = as_ref[cc, h]
            qv, kv, vv, dov, v_new = q_ref[rows, sl], k_ref[rows, sl], v_ref[rows, sl], do_ref[rows, sl], vn_ref[rows, sl]
            t = yield from _chunk_terms(qv, kv, vv, gcol_ref[rows, :], grow_ref[cc], h, H)
            beta_c, e_c, gam, kb = t["beta_c"], t["e_c"], t["gam"], t["kb"]
            incl, strict, attn, lmat, edec = t["incl"], t["strict"], t["attn"], t["lmat"], t["edec"]
            kdec = kv * edec
            egl = jnp.exp(t["gl"])
            qe = qv * e_c
            ekb = kb * e_c

            t1 = _mm_nt(dov, st)
            ds_o = _mm_tn(qe, dov)
            dattn_raw = _mm_nt(dov, v_new)
            dv_new_o = _mm_tn(attn, dov)
            yield
            while (oi, h) not in state:
                yield
            ds_next = state[(oi, h)]
            dkdec = _mm_nt(v_new, ds_next)
            dv_new_s = _mm(kdec, ds_next)
            yield
            dgl = egl * jnp.sum(jnp.sum(st * ds_next, axis=1, keepdims=True), axis=0, keepdims=True)
            dk = edec * dkdec
            r = jnp.sum(dkdec * kdec, axis=1, keepdims=True)
            dgc = -r
            dgl = dgl + jnp.sum(r, axis=0, keepdims=True)
            dq = e_c * t1
            dgc = dgc + jnp.sum(t1 * qe, axis=1, keepdims=True)
            dattn = jnp.where(incl, dattn_raw, 0.0)
            dv_new = dv_new_s + dv_new_o
            dqm = dattn * gam
            z = dattn * attn
            dvb = _mm3(a, dv_new, _TN)
            dq_a = _mm(dqm, kv)
            dk_a = _mm_tn(dqm, qv)
            yield
            dq_ref[rows, sl] = dq + dq_a
            dv_ref[rows, sl] = beta_c * dvb
            ds_kb = _mm_tn(ekb, dvb)
            dekb_neg = _mm_nt(dvb, st)
            dl_neg = _mm_nt(dvb, v_new)
            yield
            state[(oi + 1, h)] = egl * ds_next + ds_o - ds_kb
            dekb = -dekb_neg
            dl = jnp.where(strict, -dl_neg, 0.0)
            dp = dl * gam
            z = z + dl * lmat
            dkb_p = _mm(dp, kv)
            dk_p = _mm_tn(dp, kb)
            dgc = dgc + jnp.sum(dekb * ekb, axis=1, keepdims=True)
            dgc = dgc + jnp.sum(z, axis=1, keepdims=True) - jnp.sum(z.T, axis=1, keepdims=True)
            dgc = dgc + jnp.where(rowi == C - 1, dgl, 0.0)
            yield
            dkb = dkb_p + e_c * dekb
            dk_ref[rows, sl] = dk + dk_a + dk_p + beta_c * dkb
            dbeta = jnp.sum(dkb * kv, axis=1, keepdims=True) + jnp.sum(dvb * vv, axis=1, keepdims=True)
            acc[cc] = acc[cc] + jnp.where(lane == h, dbeta, 0.0) + jnp.where(lane == H + h, dgc, 0.0)

        _interleave(head(oi, h) for oi in range(CPS) for h in range(H))
        for h in range(H):
            ds_ref[h] = state[(CPS, h)]
        ri = lax.broadcasted_iota(jnp.int32, (C, C), 0)
        ci = lax.broadcasted_iota(jnp.int32, (C, C), 1)
        upper = (ri <= ci).astype(F32)
        dal = jnp.zeros((1, LANES), F32)
        ddt = jnp.zeros((1, LANES), F32)
        for cc in range(CPS):
            rows = slice(cc * C, (cc + 1) * C)
            gates = gcol_ref[rows, :]
            dg_all = _mm3(upper, acc[cc])
            d_braw = acc[cc] * gates * (1.0 - gates)
            d_araw = dg_all * (-jnp.exp(al_ref[...])) * _sigmoid(ba_ref[rows, :] + dt_ref[...])
            dgate_ref[rows, :] = jnp.where(lane < H, d_braw, jnp.where(lane < 2 * H, d_araw, 0.0))
            dal = dal + jnp.sum(dg_all * gates, axis=0, keepdims=True)
            ddt = ddt + jnp.sum(d_araw, axis=0, keepdims=True)
        dpar_ref[0:1, :] += dal
        dpar_ref[1:2, :] += ddt

        @pl.when(pl.program_id(0) == NS - 1)
        def _():
            for cp in ccps:
                cp.wait()

    rev = lambda s: NS - 1 - s
    blk = lambda: pl.BlockSpec((CPS * C, AW), lambda s: (rev(s), 0))
    row = pl.BlockSpec((1, LANES), lambda s: (0, 0))
    any_spec = pl.BlockSpec(memory_space=pl.ANY)
    res = pl.pallas_call(
        body, name="delta_bwd", grid=(NS,),
        in_specs=[row, row, blk(), blk(), blk(),
                  pl.BlockSpec((CPS * C, LANES), lambda s: (rev(s), 0)),
                  pl.BlockSpec((CPS, 32, C), lambda s: (rev(s), 0, 0)),
                  pl.BlockSpec((CPS * C, LANES), lambda s: (rev(s), 0)),
                  blk(),
                  pl.BlockSpec((CPS, H, D, D), lambda s: (rev(s), 0, 0, 0)),
                  pl.BlockSpec((CPS, H, C, C), lambda s: (rev(s), 0, 0, 0)),
                  blk()] + [any_spec] * nc,
        out_specs=[blk(), blk(), blk(),
                   pl.BlockSpec((CPS * C, LANES), lambda s: (rev(s), 0)),
                   pl.BlockSpec((8, LANES), lambda s: (0, 0))] + [any_spec] * nc,
        out_shape=[S((T, AW), F32), S((T, AW), F32), S((T, AW), F32),
                   S((T, LANES), F32), S((8, LANES), F32)] + [S((3,) + a.shape[1:], a.dtype) for a in carry],
        scratch_shapes=[pltpu.VMEM((H, D, D), F32),
                        pltpu.SemaphoreType.DMA((max(nc, 1), 3)), pltpu.SemaphoreType.DMA((max(nc, 1), 3))],
        compiler_params=_cp(ARB),
    )(a_log, dt_bias, q, k, v, gcol, grow, ba, vnew, ssave, asave, d_o, *carry)
    return res[:5], res[5:]


def _ln_stats(xv):
    mu = jnp.mean(xv, axis=-1, keepdims=True)
    xc = xv - mu
    var = jnp.mean(xc * xc, axis=-1, keepdims=True)
    rstd = lax.rsqrt(var + EPS)
    return xc * rstd, rstd


def _mix_fwd(o, proj, head_norm_w, ln_w, ln_b, w_sp, bs_t, H, D, G, P):
    T = o.shape[0]
    AW, BW = H * D, G * P
    MIX = AW + BW
    nb = AW // BW if AW % BW == 0 else None
    assert nb == 1, "group widths must match the projection column blocks"
    cb = 3

    def body(o_ref, za_ref, ub_ref, vb_ref, zb_ref, hw_ref, lw_ref, lb_ref, w_ref, bs_ref, out_ref):
        hw = hw_ref[...]
        for h in range(H):
            sl = slice(h * D, (h + 1) * D)
            oh = o_ref[:, sl]
            rs = lax.rsqrt(jnp.mean(oh * oh, axis=-1, keepdims=True) + EPS)
            out_ref[:, sl] = (oh * rs * hw * _silu(za_ref[:, sl])).astype(BF16)
        xhat, _ = _ln_stats(vb_ref[...])
        vn = xhat * lw_ref[...] + lb_ref[...]
        ri = lax.broadcasted_iota(jnp.int32, (P, P), 0)
        ci = lax.broadcasted_iota(jnp.int32, (P, P), 1)
        bsv = bs_ref[...]
        for g in range(G):
            sl = slice(g * P, (g + 1) * P)
            wm = jnp.where(ri >= ci, w_ref[g], 0.0)
            s = _mm(wm, vn[:, sl]) + bsv[:, g:g + 1]
            out_ref[:, AW + g * P:AW + (g + 1) * P] = (ub_ref[:, sl] * s * _silu(zb_ref[:, sl])).astype(BF16)

    row = lambda w: pl.BlockSpec((1, w), lambda i: (0, 0))
    return pl.pallas_call(
        body, name="mix_fwd", grid=(T // P,),
        in_specs=[pl.BlockSpec((P, AW), lambda i: (i, 0)),
                  pl.BlockSpec((P, AW), lambda i: (i, cb)),
                  pl.BlockSpec((P, BW), lambda i: (i, cb + 1)),
                  pl.BlockSpec((P, BW), lambda i: (i, cb + 2)),
                  pl.BlockSpec((P, BW), lambda i: (i, cb + 3)),
                  row(D), row(BW), row(BW),
                  pl.BlockSpec((G, P, P), lambda i: (0, 0, 0)),
                  pl.BlockSpec((P, G), lambda i: (0, 0))],
        out_specs=pl.BlockSpec((P, MIX), lambda i: (i, 0)),
        out_shape=S((T, MIX), BF16),
        compiler_params=_cp(ARB),
    )(o, proj, proj, proj, proj, head_norm_w, ln_w, ln_b, w_sp, bs_t)


def _mix_bwd(d_ocat, o, proj, head_norm_w, ln_w, ln_b, w_sp, bs_t, H, D, G, P, carry):
    T = o.shape[0]
    AW, BW = H * D, G * P
    MIX = AW + BW
    cb = 3
    nc = len(carry)

    def body(dc_ref, o_ref, za_ref, ub_ref, vb_ref, zb_ref, hw_ref, lw_ref, lb_ref, w_ref, bs_ref, *rest):
        cins = rest[:nc]
        do_ref, dmain_ref, dhw_ref, dln_ref, dw_ref, dbs_ref = rest[nc:nc + 6]
        couts = rest[nc + 6:2 * nc + 6]
        dvn_ref, drest_ref, out_sems, csend, crecv = rest[2 * nc + 6:]
        i = pl.program_id(0)
        slot = lax.rem(i, 2)
        ccps = _sibling_copies(cins, couts, csend, crecv)

        def out_copy(step, s):
            return pltpu.make_async_copy(
                drest_ref.at[s], dmain_ref.at[pl.ds(step * P, P), pl.ds(cb * AW, AW + 3 * BW)], out_sems.at[s])

        @pl.when(i == 0)
        def _():
            dhw_ref[...] = jnp.zeros_like(dhw_ref)
            dln_ref[...] = jnp.zeros_like(dln_ref)
            dw_ref[...] = jnp.zeros_like(dw_ref)
            dbs_ref[...] = jnp.zeros_like(dbs_ref)
            for cp in ccps:
                cp.start()

        @pl.when(i >= 2)
        def _():
            out_copy(i - 2, slot).wait()

        hw = hw_ref[...]
        dhw = jnp.zeros((1, D), F32)
        for h in range(H):
            sl = slice(h * D, (h + 1) * D)
            oh = o_ref[:, sl]
            za = za_ref[:, sl]
            doa = dc_ref[:, sl]
            rs = lax.rsqrt(jnp.mean(oh * oh, axis=-1, keepdims=True) + EPS)
            xh = oh * rs
            d_on = doa * _silu(za)
            drest_ref[slot, :, sl] = (doa * (xh * hw) * _dsilu(za)).astype(BF16)
            dhw = dhw + jnp.sum(d_on * xh, axis=0, keepdims=True)
            dxh = d_on * hw
            do_ref[:, sl] = rs * (dxh - xh * jnp.mean(dxh * xh, axis=-1, keepdims=True))
        dhw_ref[0:1, :] += dhw

        xhat, rstd = _ln_stats(vb_ref[...])
        lw = lw_ref[...]
        vn = xhat * lw + lb_ref[...]
        ri = lax.broadcasted_iota(jnp.int32, (P, P), 0)
        ci = lax.broadcasted_iota(jnp.int32, (P, P), 1)
        lane = lax.broadcasted_iota(jnp.int32, (P, LANES), 1)
        bsv = bs_ref[...]
        dbs = jnp.zeros((P, LANES), F32)
        for g in range(G):
            sl = slice(g * P, (g + 1) * P)
            wm = jnp.where(ri >= ci, w_ref[g], 0.0)
            vng = vn[:, sl]
            s = _mm(wm, vng) + bsv[:, g:g + 1]
            dob = dc_ref[:, AW + g * P:AW + (g + 1) * P]
            ub = ub_ref[:, sl]
            zb = zb_ref[:, sl]
            szb = _silu(zb)
            drest_ref[slot, :, AW + g * P:AW + (g + 1) * P] = (dob * s * szb).astype(BF16)
            drest_ref[slot, :, AW + 2 * BW + g * P:AW + 2 * BW + (g + 1) * P] = (
                dob * ub * s * _dsilu(zb)).astype(BF16)
            ds = dob * ub * szb
            dvn_ref[:, sl] = _mm_tn(wm, ds)
            dw_ref[g] += jnp.where(ri >= ci, _mm_nt(ds, vng), 0.0)
            dbs = dbs + jnp.where(lane == g, jnp.sum(ds, axis=1, keepdims=True), 0.0)
        dbs_ref[...] += dbs
        dvn = dvn_ref[...]
        dln_ref[0:1, :] += jnp.sum(dvn * xhat, axis=0, keepdims=True)
        dln_ref[1:2, :] += jnp.sum(dvn, axis=0, keepdims=True)
        dxh = dvn * lw
        dvb = rstd * (dxh - jnp.mean(dxh, axis=-1, keepdims=True) - xhat * jnp.mean(dxh * xhat, axis=-1, keepdims=True))
        drest_ref[slot, :, AW + BW:AW + 2 * BW] = dvb.astype(BF16)

        out_copy(i, slot).start()

        @pl.when(i == nstep - 1)
        def _():
            out_copy(i, slot).wait()
            if nstep > 1:
                out_copy(i - 1, 1 - slot).wait()
            for cp in ccps:
                cp.wait()

    nstep = T // P
    row = lambda w: pl.BlockSpec((1, w), lambda i: (0, 0))
    any_spec = pl.BlockSpec(memory_space=pl.ANY)
    res = pl.pallas_call(
        body, name="mix_bwd", grid=(nstep,),
        in_specs=[pl.BlockSpec((P, MIX), lambda i: (i, 0)),
                  pl.BlockSpec((P, AW), lambda i: (i, 0)),
                  pl.BlockSpec((P, AW), lambda i: (i, cb)),
                  pl.BlockSpec((P, BW), lambda i: (i, cb + 1)),
                  pl.BlockSpec((P, BW), lambda i: (i, cb + 2)),
                  pl.BlockSpec((P, BW), lambda i: (i, cb + 3)),
                  row(D), row(BW), row(BW),
                  pl.BlockSpec((G, P, P), lambda i: (0, 0, 0)),
                  pl.BlockSpec((P, G), lambda i: (0, 0))] + [any_spec] * nc,
        out_specs=[pl.BlockSpec((P, AW), lambda i: (i, 0)),
                   any_spec,
                   pl.BlockSpec((8, D), lambda i: (0, 0)),
                   pl.BlockSpec((8, BW), lambda i: (0, 0)),
                   pl.BlockSpec((G, P, P), lambda i: (0, 0, 0)),
                   pl.BlockSpec((P, LANES), lambda i: (0, 0))] + [any_spec] * nc,
        out_shape=[S((T, AW), F32), S((T, cb * AW + AW + 3 * BW), BF16), S((8, D), F32), S((8, BW), F32),
                   S((G, P, P), F32), S((P, LANES), F32)] + [S(a.shape[:1] + a.shape[2:], a.dtype) for a in carry],
        scratch_shapes=[pltpu.VMEM((P, BW), F32), pltpu.VMEM((2, P, AW + 3 * BW), BF16),
                        pltpu.SemaphoreType.DMA((2,))] + _sibling_sems(carry),
        compiler_params=_cp(ARB),
    )(d_ocat, o, proj, proj, proj, proj, head_norm_w, ln_w, ln_b, w_sp, bs_t, *carry)
    return res[:6], res[6:]


def _out_proj_loss(ocat, w_out, x, target, fnw):
    T, MIX = ocat.shape
    DM = x.shape[1]
    tm = _tile(T, 256, 8)

    def body(oc_ref, w_ref, x_ref, t_ref, fw_ref, dh_ref, dhb_ref, doc_ref, loss_ref, gfw_ref):
        @pl.when(pl.program_id(0) == 0)
        def _():
            loss_ref[...] = jnp.zeros_like(loss_ref)
            gfw_ref[...] = jnp.zeros_like(gfw_ref)

        wv = w_ref[...]
        hh = x_ref[...] + jnp.dot(oc_ref[...].astype(MXU), wv.astype(MXU), preferred_element_type=F32)
        rs = lax.rsqrt(jnp.mean(hh * hh, axis=-1, keepdims=True) + EPS)
        hn = hh * rs
        fw = fw_ref[...]
        e = hn * fw - t_ref[...]
        row_loss = 0.5 * jnp.mean(e * e, axis=-1, keepdims=True)
        loss_ref[...] += jnp.sum(row_loss, axis=0, keepdims=True)
        dy = e * (1.0 / DM)
        gfw_ref[0:1, :] += jnp.sum(dy * hn, axis=0, keepdims=True)
        dhn = dy * fw
        dh = rs * (dhn - hn * jnp.mean(dhn * hn, axis=-1, keepdims=True))
        dh_ref[...] = dh
        dhb = dh.astype(BF16)
        dhb_ref[...] = dhb
        doc_ref[...] = _mm_nt(dhb, wv)

    return pl.pallas_call(
        body, name="out_proj_loss", grid=(T // tm,),
        in_specs=[pl.BlockSpec((tm, MIX), lambda i: (i, 0)),
                  pl.BlockSpec((MIX, DM), lambda i: (0, 0)),
                  pl.BlockSpec((tm, DM), lambda i: (i, 0)),
                  pl.BlockSpec((tm, DM), lambda i: (i, 0)),
                  pl.BlockSpec((1, DM), lambda i: (0, 0))],
        out_specs=[pl.BlockSpec((tm, DM), lambda i: (i, 0)),
                   pl.BlockSpec((tm, DM), lambda i: (i, 0)),
                   pl.BlockSpec((tm, MIX), lambda i: (i, 0)),
                   pl.BlockSpec((8, LANES), lambda i: (0, 0)),
                   pl.BlockSpec((8, DM), lambda i: (0, 0))],
        out_shape=[S((T, DM), F32), S((T, DM), BF16), S((T, MIX), F32), S((8, LANES), F32), S((8, DM), F32)],
        compiler_params=_cp(ARB),
    )(ocat, w_out, x, target, fnw)


def _grad_w(lhs, rhs, name):
    T, A = lhs.shape
    B = rhs.shape[1]
    ta = _tile(A, 512, LANES)
    tk = _tile(T, 1024, 16)
    nk = T // tk

    def body(l_ref, r_ref, out_ref, acc_ref):
        k = pl.program_id(1)
        part = _mm_tn(l_ref[...], r_ref[...])

        @pl.when(k == 0)
        def _():
            acc_ref[...] = part

        @pl.when(k > 0)
        def _():
            acc_ref[...] += part

        @pl.when(k == nk - 1)
        def _():
            out_ref[...] = acc_ref[...].astype(BF16)

    return pl.pallas_call(
        body, name=name, grid=(A // ta, nk),
        in_specs=[pl.BlockSpec((tk, ta), lambda i, k: (k, i)),
                  pl.BlockSpec((tk, B), lambda i, k: (k, 0))],
        out_specs=pl.BlockSpec((ta, B), lambda i, k: (i, 0)),
        out_shape=S((A, B), BF16),
        scratch_shapes=[pltpu.VMEM((ta, B), F32)],
        compiler_params=_cp(ARB, ARB),
    )(lhs, rhs)


def _grad_w_in(xn, dmain, dba, WD, gate_lo, gate_hi):
    T, DM = xn.shape
    NM = dmain.shape[1]
    tn = _tile(NM, 1024, LANES)
    tk = _tile(T, 2048, 16)
    nj, nk = NM // tn, T // tk
    ND = N_DEV
    tiles = [[] for _ in range(nj)]
    first_tile, last_tile = {}, {}
    for d, s0, s1, dest, c0 in _pieces(WD, gate_lo, gate_hi, ND * WD):
        if dest != "main":
            continue
        while s0 < s1:
            jj = c0 // tn
            w = min(s1 - s0, (jj + 1) * tn - c0)
            tiles[jj].append((d, s0, w, "main", c0 - jj * tn))
            first_tile.setdefault(d, jj)
            last_tile[d] = jj
            s0, c0 = s0 + w, c0 + w
    for d, s0, s1, dest, c0 in _pieces(WD, gate_lo, gate_hi, ND * WD):
        if dest == "gate":
            tiles[first_tile[d]].append((d, s0, s1 - s0, "gate", c0))
    assert sorted(first_tile) == list(range(ND)) and all(last_tile[d] <= first_tile[d + 2] for d in range(ND - 2))

    def body(xn_ref, dm_ref, dba_ref, keep_ref, recv_ref, acc_ref, gate_ref, buf_ref, lsem, ssem, rsem):
        j = pl.program_id(0)
        k = pl.program_id(1)
        px, py, pc = _position()

        @pl.when(k == 0)
        def _():
            acc_ref[...] = jnp.zeros_like(acc_ref)

        @pl.when((j == 0) & (k == 0))
        def _():
            gate_ref[...] = jnp.zeros_like(gate_ref)

        xv = xn_ref[...]
        acc_ref[...] += _mm_tn(xv, dm_ref[...])

        @pl.when(j == 0)
        def _():
            gate_ref[...] += _mm_tn(xv, dba_ref[...])

        def local(d):
            return pltpu.make_async_copy(buf_ref.at[d % 2], keep_ref.at[d // 2], lsem.at[d // 2])

        def remote(d):
            return pltpu.make_async_remote_copy(
                src_ref=buf_ref.at[d % 2], dst_ref=recv_ref.at[d // 2], send_sem=ssem.at[d // 2],
                recv_sem=rsem.at[d // 2], device_id=(px, py, 1 - pc), device_id_type=MESH)

        def leave(d, start):
            @pl.when(pc == d % 2)
            def _():
                local(d).start() if start else local(d).wait()

            @pl.when(pc != d % 2)
            def _():
                remote(d).start() if start else remote(d).wait_send()

        def emit(jj):
            shards = sorted({p[0] for p in tiles[jj]})
            for d in shards:
                if first_tile[d] == jj and d >= 2:
                    leave(d - 2, False)
                for dd, s0, w, src, c0 in tiles[jj]:
                    if dd == d:
                        ref = acc_ref if src == "main" else gate_ref
                        buf_ref[d % 2, :, s0:s0 + w] = ref[:, c0:c0 + w].astype(BF16)
                if last_tile[d] == jj:
                    leave(d, True)
            if jj == nj - 1:
                for d in (ND - 2, ND - 1):
                    leave(d, False)
                for q in range(ND // 2):
                    remote(2 * q).wait_recv()

        for jj in range(nj):
            @pl.when((j == jj) & (k == nk - 1))
            def _(jj=jj):
                emit(jj)

    any_spec = pl.BlockSpec(memory_space=pl.ANY)
    return pl.pallas_call(
        body, name="grad_w_in", grid=(nj, nk),
        in_specs=[pl.BlockSpec((tk, DM), lambda j, k: (k, 0)),
                  pl.BlockSpec((tk, tn), lambda j, k: (k, j)),
                  pl.BlockSpec((tk, LANES), lambda j, k: (k, 0))],
        out_specs=[any_spec, any_spec],
        out_shape=[S((ND // 2, DM, WD), BF16), S((ND // 2, DM, WD), BF16)],
        scratch_shapes=[pltpu.VMEM((DM, tn), F32), pltpu.VMEM((DM, LANES), F32), pltpu.VMEM((2, DM, WD), BF16),
                        pltpu.SemaphoreType.DMA((ND // 2,)), pltpu.SemaphoreType.DMA((ND // 2,)),
                        pltpu.SemaphoreType.DMA((ND // 2,))],
        compiler_params=_cp(ARB, ARB),
    )(xn, dmain, dba)


def _pair_sum_plain(a, b, name):
    K, R, C = a.shape
    tr = _tile(R, 1024, 16)

    def body(a_ref, b_ref, o_ref):
        o_ref[...] = (a_ref[...].astype(F32) + b_ref[...].astype(F32)).astype(BF16)

    spec = lambda: pl.BlockSpec((1, tr, C), lambda q, i: (q, i, 0))
    return pl.pallas_call(body, name=name, grid=(K, R // tr), in_specs=[spec(), spec()], out_specs=spec(),
                          out_shape=S((K, R, C), BF16), compiler_params=_cp(ARB, ARB))(a, b)


def _dx_rows(T):
    tm = _tile(T, 512, 8)
    return tm if T // tm >= 2 else T // 2


def _dx_part(name, dmain, dba, w_main, w_ba, x, dh, norm_w, blk0, nblk, prev, hbm_in, hbm_alias, hbm_new, make_copies):
    T, NM = dmain.shape
    DM = x.shape[1]
    tm = _dx_rows(T)
    tk = _tile(NM, 1024, LANES)
    nk = NM // tk
    n_in, n_al, n_new = len(hbm_in), len(hbm_alias), len(hbm_new)
    n_prev = 0 if prev is None else 2
    last_step = nblk * nk - 1

    def body(dm_ref, dba_ref, w_ref, wba_ref, x_ref, dh_ref, nw_ref, *rest):
        r = list(rest)
        gnw_prev_ref = r.pop(0) if n_prev else None
        if n_prev:
            r.pop(0)
        in_refs = [r.pop(0) for _ in range(n_in)]
        del r[:n_al]
        gx_ref, gnw_ref = r.pop(0), r.pop(0)
        alias_refs = [r.pop(0) for _ in range(n_al)]
        new_refs = [r.pop(0) for _ in range(n_new)]
        acc_ref, send_sems, recv_sems = r
        i = pl.program_id(0)
        k = pl.program_id(1)
        step = i * nk + k
        cps = make_copies(in_refs, alias_refs, new_refs, send_sems, recv_sems)

        @pl.when(step == 0)
        def _():
            gnw_ref[...] = gnw_prev_ref[...] if n_prev else jnp.zeros_like(gnw_ref)
            for cp in cps:
                cp.start()

        @pl.when(k == 0)
        def _():
            acc_ref[...] = _mm_nt(dba_ref[...], wba_ref[...])

        acc_ref[...] += _mm_nt(dm_ref[...], w_ref[...])

        @pl.when(k == nk - 1)
        def _():
            xv = x_ref[...]
            rs = lax.rsqrt(jnp.mean(xv * xv, axis=-1, keepdims=True) + EPS)
            xh = xv * rs
            dxn = acc_ref[...]
            gnw_ref[0:1, :] += jnp.sum(dxn * xh, axis=0, keepdims=True)
            dxh = dxn * nw_ref[...]
            gx_ref[...] = dh_ref[...] + rs * (dxh - xh * jnp.mean(dxh * xh, axis=-1, keepdims=True))

        @pl.when(step == last_step)
        def _():
            for cp in cps:
                cp.wait()

    any_spec = pl.BlockSpec(memory_space=pl.ANY)
    prev_specs = [pl.BlockSpec((8, DM), lambda i, k: (0, 0)), any_spec] if n_prev else []
    prev_args = [prev[1], prev[0]] if n_prev else []
    aliases = {8: 0} if n_prev else {}
    for q in range(n_al):
        aliases[7 + n_prev + n_in + q] = 2 + q
    res = pl.pallas_call(
        body, name=name, grid=(nblk, nk),
        in_specs=[pl.BlockSpec((tm, tk), lambda i, k: (blk0 + i, k)),
                  pl.BlockSpec((tm, LANES), lambda i, k: (blk0 + i, 0)),
                  pl.BlockSpec((DM, tk), lambda i, k: (0, k)),
                  pl.BlockSpec((DM, LANES), lambda i, k: (0, 0)),
                  pl.BlockSpec((tm, DM), lambda i, k: (blk0 + i, 0)),
                  pl.BlockSpec((tm, DM), lambda i, k: (blk0 + i, 0)),
                  pl.BlockSpec((1, DM), lambda i, k: (0, 0))] + prev_specs + [any_spec] * (n_in + n_al),
        out_specs=[pl.BlockSpec((tm, DM), lambda i, k: (blk0 + i, 0)),
                   pl.BlockSpec((8, DM), lambda i, k: (0, 0))] + [any_spec] * (n_al + n_new),
        out_shape=[S((T, DM), F32), S((8, DM), F32)] + [S(a.shape, a.dtype) for a in hbm_alias] + list(hbm_new),
        scratch_shapes=[pltpu.VMEM((tm, DM), F32), pltpu.SemaphoreType.DMA((10,)), pltpu.SemaphoreType.DMA((10,))],
        input_output_aliases=aliases,
        compiler_params=_cp(ARB, ARB),
    )(dmain, dba, w_main, w_ba, x, dh, norm_w, *prev_args, *hbm_in, *hbm_alias)
    return (res[0], res[1]), res[2:2 + n_al], res[2 + n_al:]


def _remote(kk, src, dst, to, send_sems, recv_sems):
    return pltpu.make_async_remote_copy(src_ref=src, dst_ref=dst, send_sem=send_sems.at[kk], recv_sem=recv_sems.at[kk],
                                        device_id=to, device_id_type=MESH)


def _dx(dmain, dba, w_main, w_ba, x, dh, norm_w, chip_sum, small, cut):
    R, C = chip_sum.shape[1:]
    half = R // 2
    assert half % 16 == 0
    T = x.shape[0]
    ni = T // _dx_rows(T)
    cut = max(1, min(cut, ni - 1))
    upper, lower = pl.ds(0, half), pl.ds(half, half)

    def nbrs():
        px, py, pc = _position()
        return (px, py), (1 - px, py, pc), (px, 1 - py, pc)

    def phase1(ins, als, news, ss, rs):
        (px, py), xn, yn = nbrs()
        cs = ins[0]
        recv, stage = news
        bx, by, bd = cs.at[2 * (1 - px) + py], cs.at[2 * px + (1 - py)], cs.at[2 * (1 - px) + (1 - py)]
        return [_remote(0, bx.at[upper], recv.at[0].at[upper], xn, ss, rs),
                _remote(1, by.at[lower], recv.at[1].at[lower], yn, ss, rs),
                _remote(2, bd.at[upper], stage.at[0], xn, ss, rs),
                _remote(3, bd.at[lower], stage.at[1], yn, ss, rs)]

    def phase2(ins, als, news, ss, rs):
        (px, py), xn, yn = nbrs()
        comb, small_ref = ins
        recv, gath = als[0], news[0]
        me, small_cps = _broadcast_copies([small_ref], [gath], _Sem2(ss, 2), _Sem2(rs, 2))
        return ([_remote(0, comb.at[0], recv.at[1].at[upper], yn, ss, rs),
                 _remote(1, comb.at[1], recv.at[0].at[lower], xn, ss, rs)] + small_cps
                + [pltpu.make_async_copy(small_ref, gath.at[me], ss.at[9])])

    (gx, gnw), _, (recv, stage) = _dx_part(
        "dx_a", dmain, dba, w_main, w_ba, x, dh, norm_w, 0, cut, None, [chip_sum], [],
        [S((2, R, C), chip_sum.dtype), S((2, half, C), chip_sum.dtype)], phase1)
    comb = _relay_add(chip_sum, stage)
    (gx, gnw), (recv,), (gath,) = _dx_part(
        "dx_b", dmain, dba, w_main, w_ba, x, dh, norm_w, cut, ni - cut, (gx, gnw), [comb, small], [recv],
        [S((N_DEV,) + small.shape, F32)], phase2)
    return gx, gnw, gath, recv


class _Sem2:
    def __init__(self, sems, lo):
        self.sems, self.lo = sems, lo

    @property
    def at(self):
        outer = self

        class _At:
            def __getitem__(self, idx):
                a, k = idx
                return outer.sems.at[outer.lo + k]
        return _At()


def _relay_add(chip_sum, stage):
    _, R, C = chip_sum.shape
    half = R // 2
    tr = _tile(half, 256, 16)
    nt = half // tr
    px, py, _ = _position()
    idx = jnp.stack([2 * px + (1 - py), 2 * (1 - px) + py]).astype(jnp.int32)

    def body(idx_ref, p_ref, s_ref, o_ref):
        del idx_ref
        o_ref[0] = (p_ref[0].astype(F32) + s_ref[0].astype(F32)).astype(BF16)

    return pl.pallas_call(
        body, name="relay_add",
        grid_spec=pltpu.PrefetchScalarGridSpec(
            num_scalar_prefetch=1, grid=(2, nt),
            in_specs=[pl.BlockSpec((1, tr, C), lambda s, i, idx_ref: (idx_ref[s], s * nt + i, 0)),
                      pl.BlockSpec((1, tr, C), lambda s, i, idx_ref: (s, i, 0))],
            out_specs=pl.BlockSpec((1, tr, C), lambda s, i, idx_ref: (s, i, 0))),
        out_shape=S((2, half, C), BF16), compiler_params=_cp(ARB, ARB),
    )(idx, chip_sum, stage)


def _sum_slots(gath, shapes):
    spans, outs, r = [], [], 0
    for shp in shapes:
        n = 1
        for s in shp:
            n *= s
        nr = -(-n // (8 * LANES)) * 8
        spans.append((r, nr, n))
        outs.append(S((1, n), F32) if n < LANES else S((nr, LANES), F32))
        r += nr
    assert r == gath.shape[1] and gath.shape[2] == LANES

    def body(g_ref, *o_refs):
        tot = g_ref[0]
        for d in range(1, N_DEV):
            tot = tot + g_ref[d]
        for (r0, nr, n), o_ref in zip(spans, o_refs):
            o_ref[...] = tot[r0:r0 + 1, :n] if n < LANES else tot[r0:r0 + nr]

    vm = pl.BlockSpec(memory_space=pltpu.VMEM)
    res = pl.pallas_call(body, name="sum_slots", in_specs=[vm], out_specs=[vm] * len(outs), out_shape=outs)(gath)
    return [a.reshape(-1)[:n].reshape(shp) for a, (_, _, n), shp in zip(res, spans, shapes)]


def _prep_a_bwd(dq, dk, dv, c, proj, conv_w, dmain, H, D):
    T = c.shape[0]
    AW = H * D
    C3 = 3 * AW
    tb = _tile(T, 256, 8)
    nblk = T // tb
    r8 = tb // 8
    scale = float(D) ** -0.5

    def body(dq_ref, dk_ref, dv_ref, c_ref, dqn_ref, dkn_ref, dvn_ref, cn_ref, x_ref, halo_ref, cw_ref, dmain_in_ref,
             dx_ref, gcw_ref, dc_ref):
        del dmain_in_ref
        i = pl.program_id(0)

        @pl.when(i == 0)
        def _():
            gcw_ref[...] = jnp.zeros_like(gcw_ref)

        def pointwise(rows, dq_r, dk_r, dv_r, c_r, keep):
            for h in range(H):
                for part, d_r, sc in ((0, dq_r, scale), (1, dk_r, 1.0)):
                    sl = slice(part * AW + h * D, part * AW + (h + 1) * D)
                    cv = c_r[:, sl]
                    raw = _silu(cv)
                    rs = lax.rsqrt(jnp.sum(raw * raw, axis=-1, keepdims=True) + EPS)
                    nrm = raw * rs
                    dn = d_r[:, h * D:(h + 1) * D] * sc
                    draw = rs * (dn - nrm * jnp.sum(dn * nrm, axis=-1, keepdims=True))
                    dc_ref[rows, sl] = draw * _dsilu(cv) * keep
            dc_ref[rows, 2 * AW:] = dv_r[...] * _dsilu(c_r[:, 2 * AW:]) * keep

        pointwise(slice(0, tb), dq_ref, dk_ref, dv_ref, c_ref, 1.0)
        pointwise(slice(tb, tb + 8), dqn_ref, dkn_ref, dvn_ref, cn_ref, (i < nblk - 1).astype(F32))

        cw = cw_ref[...]
        dcv = dc_ref[0:tb, :]
        dx = cw[3:4, :] * dcv
        for j in range(3):
            dx = dx + cw[j:j + 1, :] * dc_ref[3 - j:3 - j + tb, :]
        dx_ref[...] = dx.astype(BF16)
        halo = halo_ref[...] * (i > 0).astype(F32)
        xp = jnp.concatenate([halo, x_ref[...]], axis=0)
        for j in range(4):
            gcw_ref[j:j + 1, :] += jnp.sum(dcv * xp[5 + j:5 + j + tb], axis=0, keepdims=True)

    nxt = lambda i: (jnp.minimum((i + 1) * r8, T // 8 - 1), 0)
    return pl.pallas_call(
        body, name="prep_a_bwd", grid=(nblk,),
        in_specs=[pl.BlockSpec((tb, AW), lambda i: (i, 0)),
                  pl.BlockSpec((tb, AW), lambda i: (i, 0)),
                  pl.BlockSpec((tb, AW), lambda i: (i, 0)),
                  pl.BlockSpec((tb, C3), lambda i: (i, 0)),
                  pl.BlockSpec((8, AW), nxt), pl.BlockSpec((8, AW), nxt), pl.BlockSpec((8, AW), nxt),
                  pl.BlockSpec((8, C3), nxt),
                  pl.BlockSpec((tb, C3), lambda i: (i, 0)),
                  pl.BlockSpec((8, C3), lambda i: (jnp.maximum(i * r8 - 1, 0), 0)),
                  pl.BlockSpec((4, C3), lambda i: (0, 0)),
                  pl.BlockSpec(memory_space=pl.ANY)],
        out_specs=[pl.BlockSpec((tb, C3), lambda i: (i, 0)),
                   pl.BlockSpec((8, C3), lambda i: (0, 0))],
        out_shape=[S(dmain.shape, dmain.dtype), S((8, C3), F32)],
        scratch_shapes=[pltpu.VMEM((tb + 8, C3), F32)],
        input_output_aliases={11: 0},
        compiler_params=_cp(ARB),
    )(dq, dk, dv, c, dq, dk, dv, c, proj, proj, conv_w, dmain)


def _adam_math(w, g, m, v):
    m2 = ADAM_B1 * m + (1.0 - ADAM_B1) * g
    v2 = ADAM_B2 * v + (1.0 - ADAM_B2) * (g * g)
    m_hat = m2 / (1.0 - ADAM_B1 ** ADAM_STEP)
    v_hat = v2 / (1.0 - ADAM_B2 ** ADAM_STEP)
    delta = -ADAM_LR * (m_hat / (jnp.sqrt(v_hat) + ADAM_EPS) + ADAM_WD * w)
    return delta, m2, v2


def _pair_sum(blocks, recv, core, name):
    K, _, R, C = blocks.shape
    tr = _tile(R, 256, 16)

    def body(core_ref, a_ref, b_ref, o_ref):
        del core_ref
        o_ref[0] = (a_ref[0, 0].astype(F32) + b_ref[0].astype(F32)).astype(BF16)

    spec = lambda: pl.BlockSpec((1, tr, C), lambda k, i, core_ref: (k, i, 0))
    return pl.pallas_call(
        body, name=name,
        grid_spec=pltpu.PrefetchScalarGridSpec(
            num_scalar_prefetch=1, grid=(K, R // tr),
            in_specs=[pl.BlockSpec((1, 1, tr, C), lambda k, i, core_ref: (k, core_ref[0], i, 0)), spec()],
            out_specs=spec()),
        out_shape=S((K, R, C), BF16), compiler_params=_cp(ARB, ARB),
    )(core, blocks, recv)


def _sum_adam(chip_sums, recv, w, m, v, chip, name, transposed=False):
    R, C = chip_sums.shape[1:]
    NR = recv.shape[0]
    tr = _tile(R, min(256, max(R // 4, 16)), 16)

    def body(chip_ref, own_ref, r_ref, w_ref, m_ref, v_ref, g_ref, d_ref, m2_ref, v2_ref):
        del chip_ref
        g = own_ref[0].astype(F32)
        for j in range(NR):
            g = g + r_ref[j].astype(F32)
        if transposed:
            g = g.T
        g_ref[...] = g
        d_ref[...], m2_ref[...], v2_ref[...] = _adam_math(w_ref[...], g, m_ref[...], v_ref[...])

    if transposed:
        spec = lambda: pl.BlockSpec((C, tr), lambda i, chip_ref: (0, i))
        shape = (C, R)
    else:
        spec = lambda: pl.BlockSpec((tr, C), lambda i, chip_ref: (i, 0))
        shape = (R, C)
    assert w.shape == shape
    return pl.pallas_call(
        body, name=name,
        grid_spec=pltpu.PrefetchScalarGridSpec(
            num_scalar_prefetch=1, grid=(R // tr,),
            in_specs=[pl.BlockSpec((1, tr, C), lambda i, chip_ref: (chip_ref[0], i, 0)),
                      pl.BlockSpec((NR, tr, C), lambda i, chip_ref: (0, i, 0)), spec(), spec(), spec()],
            out_specs=[spec(), spec(), spec(), spec()]),
        out_shape=[S(shape, F32)] * 4, compiler_params=_cp(ARB),
    )(chip, chip_sums, recv, w, m, v)


def _adam_small(ws, gs, ms, vs):
    n = len(ws)

    def body(*refs):
        ins, outs = refs[:4 * n], refs[4 * n:]
        for p in range(n):
            w_ref, g_ref, m_ref, v_ref = (ins[a * n + p] for a in range(4))
            outs[p][...], outs[n + p][...], outs[2 * n + p][...] = _adam_math(
                w_ref[...], g_ref[...], m_ref[...], v_ref[...])

    vm = pl.BlockSpec(memory_space=pltpu.VMEM)
    res = pl.pallas_call(
        body, name="adam_small", in_specs=[vm] * (4 * n), out_specs=[vm] * (3 * n),
        out_shape=[S(w.shape, F32) for w in ws] * 3,
    )(*ws, *gs, *ms, *vs)
    return res[:n], res[n:2 * n], res[2 * n:]


def _position():
    return lax.axis_index("x"), lax.axis_index("y"), lax.axis_index("c")


def _all_gather_weights(arr, x_in, norm_w, chip, tn, plans, nm):
    R = arr.shape[0]
    half = R // 2
    assert half % 16 == 0
    T, DM = x_in.shape
    tm = _tile(T, 512, 16)
    nstep = T // tm

    def body(chip_ref, x_ref, nw_ref, in_ref, xn_ref, out_ref, proj_ref, wtile_ref, stage_ref,
             send_sems, recv_sems, local_sem, stage_sems):
        i = pl.program_id(0)
        x, y, c = _position()
        me, sibling = (x, y, c), (x, y, 1 - c)
        xn, yn, diag = (1 - x, y), (x, 1 - y), (1 - x, 1 - y)
        upper, lower = pl.ds(0, half), pl.ds(half, half)

        def slot(p, rows=None):
            ref = out_ref.at[4 * p[0] + 2 * p[1] + p[2]]
            return ref if rows is None else ref.at[rows]

        def copy(kk, block, to, rows=None, src=None):
            return pltpu.make_async_remote_copy(
                src_ref=slot(block, rows) if src is None else src, dst_ref=slot(block, rows),
                send_sem=send_sems.at[kk], recv_sem=recv_sems.at[kk], device_id=to, device_id_type=MESH)

        mine = pltpu.make_async_copy(in_ref, slot(me), local_sem)
        first = [copy(0, me, sibling, src=in_ref), copy(1, me, (*xn, c), src=in_ref), copy(2, me, (*yn, c), src=in_ref)]

        @pl.when(i == 0)
        def _():
            mine.start()
            for cp in first:
                cp.start()
            copy(0, sibling, me).wait_recv()
            loads = [pltpu.make_async_copy(in_ref, stage_ref.at[c], stage_sems.at[0]),
                     pltpu.make_async_copy(slot(sibling), stage_ref.at[1 - c], stage_sems.at[1])]
            for cp in loads:
                cp.start()
            for cp in loads:
                cp.wait()
            for m, plan in enumerate(plans):
                @pl.when(chip_ref[0] == m)
                def _(plan=plan):
                    for d, s0, w, c0 in plan:
                        wtile_ref[:, c0:c0 + w] = stage_ref[d % 2, :, s0:s0 + w]

        xv = x_ref[...]
        r = lax.rsqrt(jnp.mean(xv * xv, axis=-1, keepdims=True) + EPS)
        xnv = (xv * r * nw_ref[...]).astype(BF16)
        xn_ref[...] = xnv
        proj_ref[...] = jnp.dot(xnv.astype(MXU), wtile_ref[...].astype(MXU), preferred_element_type=F32)

        @pl.when(i == nstep - 1)
        def _():
            sent = list(first)

            def then(cps):
                for cp in cps:
                    cp.start()
                sent.extend(cps)

            copy(1, (*xn, c), me).wait_recv()
            then([copy(5, (*xn, c), (*yn, c), rows=upper), copy(3, (*xn, c), sibling)])
            copy(2, (*yn, c), me).wait_recv()
            then([copy(6, (*yn, c), (*xn, c), rows=lower), copy(4, (*yn, c), sibling)])
            copy(5, (*diag, c), me, rows=upper).wait_recv()
            then([copy(7, (*diag, c), sibling, rows=upper)])
            copy(6, (*diag, c), me, rows=lower).wait_recv()
            then([copy(8, (*diag, c), sibling, rows=lower)])
            copy(3, (*xn, 1 - c), me).wait_recv()
            copy(4, (*yn, 1 - c), me).wait_recv()
            copy(7, (*diag, 1 - c), me, rows=upper).wait_recv()
            copy(8, (*diag, 1 - c), me, rows=lower).wait_recv()
            for cp in sent:
                cp.wait_send()
            mine.wait()

    any_spec = pl.BlockSpec(memory_space=pl.ANY)
    return pl.pallas_call(
        body, name="all_gather_weights",
        grid_spec=pltpu.PrefetchScalarGridSpec(
            num_scalar_prefetch=1, grid=(nstep,),
            in_specs=[pl.BlockSpec((tm, DM), lambda i, chip_ref: (i, 0)),
                      pl.BlockSpec((1, DM), lambda i, chip_ref: (0, 0)), any_spec],
            out_specs=[pl.BlockSpec((tm, DM), lambda i, chip_ref: (i, 0)), any_spec,
                       pl.BlockSpec((tm, tn), lambda i, chip_ref: (i, 2 * chip_ref[0]))],
            scratch_shapes=[pltpu.VMEM((DM, tn), arr.dtype), pltpu.VMEM((2,) + arr.shape, arr.dtype),
                            pltpu.SemaphoreType.DMA((9,)), pltpu.SemaphoreType.DMA((9,)), pltpu.SemaphoreType.DMA,
                            pltpu.SemaphoreType.DMA((2,))]),
        out_shape=[S((T, DM), BF16), S((N_DEV,) + arr.shape, arr.dtype), S((T, nm), F32)],
        compiler_params=_cp(ARB),
    )(chip, x_in, norm_w, arr)


def _sibling_copies(ins, outs, send_sems, recv_sems):
    x, y, c = _position()
    return [pltpu.make_async_remote_copy(src_ref=ins[a].at[k, 1 - c], dst_ref=outs[a].at[k],
                                         send_sem=send_sems.at[a, k], recv_sem=recv_sems.at[a, k],
                                         device_id=(x, y, 1 - c), device_id_type=MESH)
            for a in range(len(ins)) for k in range(ins[a].shape[0])]


def _sibling_sems(arrs):
    shape = (max(len(arrs), 1), arrs[0].shape[0] if arrs else 1)
    return [pltpu.SemaphoreType.DMA(shape), pltpu.SemaphoreType.DMA(shape)]


def _chip_exchange_copies(ins, outs, send_sems, recv_sems):
    x, y, c = _position()
    chips = [(1 - x, y), (x, 1 - y), (1 - x, 1 - y)]
    return [pltpu.make_async_remote_copy(
        src_ref=ins[a].at[2 * qx + qy], dst_ref=outs[a].at[j], send_sem=send_sems.at[a, j],
        recv_sem=recv_sems.at[a, j], device_id=(qx, qy, c), device_id_type=MESH)
        for a in range(len(ins)) for j, (qx, qy) in enumerate(chips)]


def _broadcast_copies(srcs, dsts, send_sems, recv_sems):
    x, y, c = _position()
    me = 4 * x + 2 * y + c
    cps = []
    for a in range(len(srcs)):
        for k in range(1, N_DEV):
            peer = (1 - x if k & 4 else x, 1 - y if k & 2 else y, 1 - c if k & 1 else c)
            cps.append(pltpu.make_async_remote_copy(
                src_ref=srcs[a], dst_ref=dsts[a].at[me], send_sem=send_sems.at[a, k - 1],
                recv_sem=recv_sems.at[a, k - 1], device_id=peer, device_id_type=MESH))
    return me, cps


def _all_reduce_small(part):
    R, C = part.shape

    def body(p_ref, out_ref, gath_ref, send_sems, recv_sems):
        me, cps = _broadcast_copies([p_ref], [gath_ref], send_sems, recv_sems)
        gath_ref[me] = p_ref[...]
        for cp in cps:
            cp.start()
        for cp in cps:
            cp.wait()
        acc = gath_ref[0]
        for d in range(1, N_DEV):
            acc = acc + gath_ref[d]
        out_ref[...] = acc

    vm = pl.BlockSpec(memory_space=pltpu.VMEM)
    return pl.pallas_call(
        body, name="all_reduce_small", in_specs=[vm], out_specs=vm, out_shape=S((R, C), F32),
        scratch_shapes=[pltpu.VMEM((N_DEV, R, C), F32), pltpu.SemaphoreType.DMA((1, N_DEV - 1)),
                        pltpu.SemaphoreType.DMA((1, N_DEV - 1))],
    )(part)


def _pack(parts):
    rows = []
    for p in parts:
        f = p.reshape(-1).astype(F32)
        pad = (-f.shape[0]) % (8 * LANES)
        rows.append(jnp.pad(f, (0, pad)).reshape(-1, LANES))
    return jnp.concatenate(rows, axis=0)


def _unpack(buf, shapes):
    out, r = [], 0
    for shp in shapes:
        n = 1
        for s in shp:
            n *= s
        nr = -(-n // (8 * LANES)) * 8
        out.append(buf[r:r + nr].reshape(-1)[:n].reshape(shp))
        r += nr
    return out


def kernel(x, norm_w, w_in, conv_w, a_log, dt_bias, head_norm_w, sgu_ln_w, sgu_ln_b, w_spatial, b_spatial, w_out, final_norm_w, loss_target, m_norm_w, m_w_in, m_conv_w, m_a_log, m_dt_bias, m_head_norm_w, m_sgu_ln_w, m_sgu_ln_b, m_w_spatial, m_b_spatial, m_w_out, m_final_norm_w, v_norm_w, v_w_in, v_conv_w, v_a_log, v_dt_bias, v_head_norm_w, v_sgu_ln_w, v_sgu_ln_b, v_w_spatial, v_b_spatial, v_w_out, v_final_norm_w):
    T, DM = x.shape[1], x.shape[2]
    H, D = a_log.shape[1], head_norm_w.shape[1]
    G, P = w_spatial.shape[1], w_spatial.shape[2]
    AW, BW = H * D, G * P
    MIX = AW + BW
    WD = w_in.shape[2]
    IN = N_DEV * WD
    RO = w_out.shape[1]
    CW = conv_w.shape[2]
    sizes = (3 * AW, AW, H, H, BW, BW, BW)
    assert sum(sizes) == IN and 2 * H <= LANES and 3 * H <= 32 and N_DEV * RO == MIX and N_DEV * CW == 3 * AW
    offs = [0]
    for s in sizes:
        offs.append(offs[-1] + s)
    px, py, pc = _position()
    dev = 4 * px + 2 * py + pc
    chip = 2 * px + py

    x2, tgt = x[0], loss_target[0]

    core_idx = jnp.reshape(pc, (1,)).astype(jnp.int32)
    chip_idx = jnp.reshape(chip, (1,)).astype(jnp.int32)
    NM = IN - 2 * H
    tn_loc, tile_plans = _local_tiles(WD, offs[2], offs[4], NM)
    xn, g_win, proj_part = _all_gather_weights(
        _cast_bf16_t(w_in[0].T, "cast_w_in"), x2, norm_w, chip_idx, tn_loc, tile_plans, NM)
    w_main, w_ba = _relayout_w(g_win, offs[2], offs[4])
    alog_row = jnp.pad(a_log, ((0, 0), (H, LANES - 2 * H)))
    dtb_row = jnp.pad(dt_bias, ((0, 0), (H, LANES - 2 * H)))
    bs_t = b_spatial[0].T

    others = jnp.arange(N_DEV - 2, dtype=jnp.int32)
    others = others + (others >= 2 * chip).astype(jnp.int32)
    proj, ba, (g_wout, g_conv) = _in_proj(xn, w_main, w_ba, proj_part, others, tn_loc,
                                          [_cast_bf16(w_out[0], "cast_w_out"), conv_w[0]])
    w_out_full = g_wout.reshape(MIX, DM)
    conv_full = g_conv.transpose(1, 0, 2).reshape(4, 3 * AW)
    q, k, v, c, gcol, grow = _prep_a_fwd(proj, ba, conv_full, alog_row, dtb_row, H, D)
    o, vnew, ssave, asave = _delta_fwd(q, k, v, gcol, grow, H, D)
    ocat = _mix_fwd(o, proj, head_norm_w, sgu_ln_w, sgu_ln_b, w_spatial[0], bs_t, H, D, G, P)
    dh, dh_bf, d_ocat, loss_acc, g_fnw = _out_proj_loss(ocat, w_out_full, x2, tgt, final_norm_w.reshape(1, DM))

    g_wout_blocks = _grad_w(ocat, dh_bf, "grad_w_out").reshape(4, 2, RO, DM)
    (d_o, dmain, g_hnw, g_ln, g_wsp, g_bs_t), (sib_wout,) = _mix_bwd(
        d_ocat, o, proj, head_norm_w, sgu_ln_w, sgu_ln_b, w_spatial[0], bs_t, H, D, G, P, [g_wout_blocks])
    chip_wout = _pair_sum(g_wout_blocks, sib_wout, core_idx, "pair_sum_w_out")
    (dq, dk, dv, dgate, dpar), (recv_wout,) = _delta_bwd(
        q, k, v, gcol, grow, ba, vnew, ssave, asave, d_o, alog_row, dtb_row, H, D, [chip_wout])
    dmain, g_conv_part = _prep_a_bwd(dq, dk, dv, c, proj, conv_full, dmain, H, D)
    dba = dgate.astype(BF16)
    keep_win, sib_win = _grad_w_in(xn, dmain, dba, WD, offs[2], offs[4])
    chip_win = _pair_sum_plain(keep_win, sib_win, "pair_sum_w_in")
    small_shapes = [a_log.shape, dt_bias.shape, head_norm_w.shape, sgu_ln_w.shape, sgu_ln_b.shape,
                    w_spatial.shape, b_spatial.shape, final_norm_w.shape]
    parts = [dpar[0, H:2 * H], dpar[1, H:2 * H], g_hnw[0], g_ln[0], g_ln[1], g_wsp, g_bs_t[:, :G].T, g_fnw[0],
             g_conv_part[:4], loss_acc[0, :1]]
    grad_x, g_nw, small_gath, recv_win = _dx(dmain, dba, w_main, w_ba, x2, dh, norm_w, chip_win, _pack(parts), 4)
    red = _sum_slots(small_gath, small_shapes + [(4, 3 * AW), (1,)])
    grad_w_in, delta_w_in, new_m_w_in, new_v_w_in = _sum_adam(
        chip_win, recv_win, w_in[0].T, m_w_in[0].T, v_w_in[0].T, chip_idx, "sum_adam_w_in", transposed=True)
    grad_w_out, delta_w_out, new_m_w_out, new_v_w_out = _sum_adam(
        chip_wout, recv_wout, w_out[0], m_w_out[0], v_w_out[0], chip_idx, "sum_adam_w_out")
    red_nw = _all_reduce_small(_pack([g_nw[0]]))
    grads_small = _unpack(red_nw, [norm_w.shape]) + red
    loss = grads_small.pop()[0]
    g_conv_full = grads_small.pop()
    grad_conv = lax.dynamic_slice_in_dim(g_conv_full, dev * CW, CW, axis=1)[None]
    small_w = [norm_w, a_log, dt_bias, head_norm_w, sgu_ln_w, sgu_ln_b, w_spatial, b_spatial, final_norm_w, conv_w]
    small_m = [m_norm_w, m_a_log, m_dt_bias, m_head_norm_w, m_sgu_ln_w, m_sgu_ln_b, m_w_spatial, m_b_spatial,
               m_final_norm_w, m_conv_w]
    small_v = [v_norm_w, v_a_log, v_dt_bias, v_head_norm_w, v_sgu_ln_w, v_sgu_ln_b, v_w_spatial, v_b_spatial,
               v_final_norm_w, v_conv_w]
    small_g = grads_small + [grad_conv]
    d_s, m_s, v_s = _adam_small(small_w, small_g, small_m, small_v)

    def order(small, win, wout):
        return [small[0], win.T[None], small[9], small[1], small[2], small[3], small[4], small[5], small[6], small[7],
                wout[None], small[8]]

    grads = order(small_g, grad_w_in, grad_w_out)
    deltas = order(d_s, delta_w_in, delta_w_out)
    new_m = order(m_s, new_m_w_in, new_m_w_out)
    new_v = order(v_s, new_v_w_in, new_v_w_out)
    return (loss, grad_x[None], *grads, *deltas, *new_m, *new_v)
```

```python
import jax
import jax.numpy as jnp
from jax import lax
from jax.experimental import pallas as pl
from jax.experimental.pallas import tpu as pltpu

F32 = jnp.float32
BF16 = jnp.bfloat16
MXU = jnp.bfloat16
HI = lax.Precision.HIGHEST
EPS = 1e-6
CHUNK_A = 64
LANES = 128
MESH = pl.DeviceIdType.MESH
N_DEV = 8

ADAM_LR = 0.001
ADAM_B1 = 0.9
ADAM_B2 = 0.999
ADAM_EPS = 1e-08
ADAM_WD = 0.01
ADAM_STEP = 10

S = jax.ShapeDtypeStruct
ARB = "arbitrary"


def _cp(*sem, vmem_mib=56):
    return pltpu.CompilerParams(dimension_semantics=tuple(sem), vmem_limit_bytes=vmem_mib * 1024 * 1024)


def _tile(n, cap, mult):
    best = None
    t = mult
    while t <= min(n, cap):
        if n % t == 0:
            best = t
        t += mult
    return best if best is not None else n


def _mm(a, b):
    return jnp.dot(a.astype(MXU), b.astype(MXU), preferred_element_type=F32)


def _mm_nt(a, b):
    return lax.dot_general(a.astype(MXU), b.astype(MXU), (((1,), (1,)), ((), ())), preferred_element_type=F32)


def _mm_tn(a, b):
    return lax.dot_general(a.astype(MXU), b.astype(MXU), (((0,), (0,)), ((), ())), preferred_element_type=F32)


def _mmh(a, b):
    return jnp.dot(a, b, precision=HI, preferred_element_type=F32)


def _sigmoid(x):
    return 1.0 / (1.0 + jnp.exp(-x))


def _silu(x):
    return x * _sigmoid(x)


def _dsilu(x):
    s = _sigmoid(x)
    return s * (1.0 + x * (1.0 - s))


def _softplus(x):
    return jnp.maximum(x, 0.0) + jnp.log(1.0 + jnp.exp(-jnp.abs(x)))


def _pieces(wd, gate_lo, gate_hi, total):
    out = []
    for d in range(N_DEV):
        lo, hi = d * wd, (d + 1) * wd
        for dest, a, b, shift in (("main", 0, gate_lo, 0), ("gate", gate_lo, gate_hi, -gate_lo),
                                  ("main", gate_hi, total, gate_lo - gate_hi)):
            s0, s1 = max(lo, a), min(hi, b)
            if s0 < s1:
                out.append((d, s0 - lo, s1 - lo, dest, s0 + shift))
    return out


def _local_tiles(wd, gate_lo, gate_hi, nm):
    n_tiles = N_DEV - 1
    assert nm % (n_tiles * LANES) == 0
    tn = nm // n_tiles
    plans = []
    for m in range(N_DEV // 2):
        lo, hi = 2 * m * tn, (2 * m + 1) * tn
        plan = []
        for d, s0, s1, dest, c0 in _pieces(wd, gate_lo, gate_hi, N_DEV * wd):
            if dest != "main":
                continue
            a, b = max(c0, lo), min(c0 + (s1 - s0), hi)
            if a < b:
                assert d // 2 == m, "tile 2m must come from chip m's own shards"
                plan.append((d, s0 + (a - c0), b - a, a - lo))
        assert sum(p[2] for p in plan) == tn
        plans.append(plan)
    return tn, plans


def _cast_bf16(a, name):
    R, C = a.shape
    tr = _tile(R, 256, 16)

    def body(a_ref, o_ref):
        o_ref[...] = a_ref[...].astype(BF16)

    spec = pl.BlockSpec((tr, C), lambda i: (i, 0))
    return pl.pallas_call(body, name=name, grid=(R // tr,), in_specs=[spec], out_specs=spec,
                          out_shape=S((R, C), BF16), compiler_params=_cp(ARB))(a)


def _cast_bf16_t(a_t, name):
    C, R = a_t.shape
    tr = _tile(R, 256, LANES)

    def body(a_ref, o_ref):
        o_ref[...] = a_ref[...].T.astype(BF16)

    return pl.pallas_call(body, name=name, grid=(R // tr,), in_specs=[pl.BlockSpec((C, tr), lambda i: (0, i))],
                          out_specs=pl.BlockSpec((tr, C), lambda i: (i, 0)),
                          out_shape=S((R, C), BF16), compiler_params=_cp(ARB))(a_t)


def _relayout_w(g_win, gate_lo, gate_hi):
    _, DM, WD = g_win.shape
    total = N_DEV * WD
    NM = total - (gate_hi - gate_lo)
    tr = _tile(DM, 256, 16)
    plan = _pieces(WD, gate_lo, gate_hi, total)

    def body(g_ref, main_ref, gate_ref):
        gate_ref[...] = jnp.zeros_like(gate_ref)
        for d, s0, s1, dest, c0 in plan:
            dst = main_ref if dest == "main" else gate_ref
            dst[:, c0:c0 + (s1 - s0)] = g_ref[d, :, s0:s1]

    return pl.pallas_call(
        body, name="relayout_w", grid=(DM // tr,),
        in_specs=[pl.BlockSpec((N_DEV, tr, WD), lambda i: (0, i, 0))],
        out_specs=[pl.BlockSpec((tr, NM), lambda i: (i, 0)), pl.BlockSpec((tr, LANES), lambda i: (i, 0))],
        out_shape=[S((DM, NM), g_win.dtype), S((DM, LANES), g_win.dtype)],
        compiler_params=_cp(ARB),
    )(g_win)


def _in_proj(xn, w_main, w_ba, proj_part, tiles, tn, shards):
    T, DM = xn.shape
    NM = w_main.shape[1]
    tm = _tile(T, 2048, 16)
    ni, nj = T // tm, tiles.shape[0]
    ns = len(shards)

    def body(tiles_ref, xn_ref, w_ref, wba_ref, part_ref, *rest):
        del tiles_ref, part_ref
        srcs = rest[:ns]
        proj_ref, ba_ref = rest[ns:ns + 2]
        gath = rest[ns + 2:2 * ns + 2]
        send_sems, recv_sems, local_sems = rest[2 * ns + 2:]
        i = pl.program_id(0)
        j = pl.program_id(1)
        me, cps = _broadcast_copies(srcs, gath, send_sems, recv_sems)
        cps = cps + [pltpu.make_async_copy(srcs[a], gath[a].at[me], local_sems.at[a]) for a in range(ns)]

        @pl.when((i == 0) & (j == 0))
        def _():
            for cp in cps:
                cp.start()

        @pl.when(j == 0)
        def _():
            ba_ref[...] = jnp.dot(xn_ref[...].astype(MXU), wba_ref[...].astype(MXU), preferred_element_type=F32)

        proj_ref[...] = jnp.dot(xn_ref[...].astype(MXU), w_ref[...].astype(MXU), preferred_element_type=F32)

        @pl.when((i == ni - 1) & (j == nj - 1))
        def _():
            for cp in cps:
                cp.wait()

    any_spec = pl.BlockSpec(memory_space=pl.ANY)
    res = pl.pallas_call(
        body, name="in_proj",
        grid_spec=pltpu.PrefetchScalarGridSpec(
            num_scalar_prefetch=1, grid=(ni, nj),
            in_specs=[pl.BlockSpec((tm, DM), lambda i, j, t: (i, 0)),
                      pl.BlockSpec((DM, tn), lambda i, j, t: (0, t[j])),
                      pl.BlockSpec((DM, LANES), lambda i, j, t: (0, 0)), any_spec] + [any_spec] * ns,
            out_specs=[pl.BlockSpec((tm, tn), lambda i, j, t: (i, t[j])),
                       pl.BlockSpec((tm, LANES), lambda i, j, t: (i, 0))] + [any_spec] * ns,
            scratch_shapes=[pltpu.SemaphoreType.DMA((ns, N_DEV - 1)), pltpu.SemaphoreType.DMA((ns, N_DEV - 1)),
                            pltpu.SemaphoreType.DMA((ns,))]),
        out_shape=[S((T, NM), F32), S((T, LANES), F32)] + [S((N_DEV,) + a.shape, a.dtype) for a in shards],
        input_output_aliases={4: 0},
        compiler_params=_cp(ARB, ARB, vmem_mib=58),
    )(tiles, xn, w_main, w_ba, proj_part, *shards)
    return res[0], res[1], res[2:]


def _prep_a_fwd(proj, ba, conv_w, alog_row, dtb_row, H, D):
    T = proj.shape[0]
    AW = H * D
    C3 = 3 * AW
    tb = _tile(T, 256, CHUNK_A)
    nch = tb // CHUNK_A
    nblk = T // tb
    scale = float(D) ** -0.5

    def body(x_ref, halo_ref, ba_ref, cw_ref, al_ref, dt_ref, q_ref, k_ref, v_ref, c_ref, gcol_ref, grow_ref):
        i = pl.program_id(0)
        xv = x_ref[...]
        halo = halo_ref[...] * (i > 0).astype(F32)
        xp = jnp.concatenate([halo, xv], axis=0)
        cw = cw_ref[...]
        c = cw[0:1, :] * xp[5:5 + tb]
        for j in range(1, 4):
            c = c + cw[j:j + 1, :] * xp[5 + j:5 + j + tb]
        c_ref[...] = c
        a = _silu(c)
        for h in range(H):
            qh = a[:, h * D:(h + 1) * D]
            kh = a[:, AW + h * D:AW + (h + 1) * D]
            qr = lax.rsqrt(jnp.sum(qh * qh, axis=-1, keepdims=True) + EPS)
            kr = lax.rsqrt(jnp.sum(kh * kh, axis=-1, keepdims=True) + EPS)
            q_ref[:, h * D:(h + 1) * D] = qh * (qr * scale)
            k_ref[:, h * D:(h + 1) * D] = kh * kr
        v_ref[...] = a[:, 2 * AW:]

        bav = ba_ref[...]
        lane = lax.broadcasted_iota(jnp.int32, (tb, LANES), 1)
        beta = _sigmoid(bav)
        g = -jnp.exp(al_ref[...]) * _softplus(bav + dt_ref[...])
        gates = jnp.where(lane < H, beta, jnp.where(lane < 2 * H, g, 0.0))
        ri = lax.broadcasted_iota(jnp.int32, (CHUNK_A, CHUNK_A), 0)
        ci = lax.broadcasted_iota(jnp.int32, (CHUNK_A, CHUNK_A), 1)
        tri = (ri >= ci).astype(F32)
        lane_c = lax.broadcasted_iota(jnp.int32, (CHUNK_A, LANES), 1)
        for cc in range(nch):
            gch = gates[cc * CHUNK_A:(cc + 1) * CHUNK_A]
            gc = pltpu.roll(_mmh(tri, gch), H, 1)
            full = jnp.where(lane_c < 2 * H, gch, jnp.where(lane_c < 3 * H, gc, 0.0))
            gcol_ref[cc * CHUNK_A:(cc + 1) * CHUNK_A, :] = full
            grow_ref[cc] = full.T[0:32, :]

    return pl.pallas_call(
        body, name="prep_a_fwd", grid=(nblk,),
        in_specs=[pl.BlockSpec((tb, C3), lambda i: (i, 0)),
                  pl.BlockSpec((8, C3), lambda i: (jnp.maximum(i * (tb // 8) - 1, 0), 0)),
                  pl.BlockSpec((tb, LANES), lambda i: (i, 0)),
                  pl.BlockSpec((4, C3), lambda i: (0, 0)),
                  pl.BlockSpec((1, LANES), lambda i: (0, 0)),
                  pl.BlockSpec((1, LANES), lambda i: (0, 0))],
        out_specs=[pl.BlockSpec((tb, AW), lambda i: (i, 0)),
                   pl.BlockSpec((tb, AW), lambda i: (i, 0)),
                   pl.BlockSpec((tb, AW), lambda i: (i, 0)),
                   pl.BlockSpec((tb, C3), lambda i: (i, 0)),
                   pl.BlockSpec((tb, LANES), lambda i: (i, 0)),
                   pl.BlockSpec((nch, 32, CHUNK_A), lambda i: (i, 0, 0))],
        out_shape=[S((T, AW), F32), S((T, AW), F32), S((T, AW), F32), S((T, C3), F32),
                   S((T, LANES), F32), S((T // CHUNK_A, 32, CHUNK_A), F32)],
        compiler_params=_cp(ARB),
    )(proj, proj, ba, conv_w, alog_row, dtb_row)


_NN = (((1,), (0,)), ((), ()))
_TN = (((0,), (0,)), ((), ()))


def _split(a):
    hi = a.astype(BF16)
    return hi, (a - hi.astype(F32)).astype(BF16)


def _mm3(a, b, dims=_NN):
    ah, al = a if isinstance(a, tuple) else _split(a)
    bh, bl = b if isinstance(b, tuple) else _split(b)
    dg = lambda p, r: lax.dot_general(p, r, dims, preferred_element_type=F32)
    return dg(ah, bh) + (dg(ah, bl) + dg(al, bh))


def _interleave(gens):
    gens = list(gens)
    while gens:
        alive = []
        for g in gens:
            try:
                next(g)
                alive.append(g)
            except StopIteration:
                pass
        gens = alive


def _chunk_terms(q, k, v, gcolv, growv, h, H):
    C = CHUNK_A
    beta_c = gcolv[:, h:h + 1]
    g_c = gcolv[:, H + h:H + h + 1]
    gc_c = gcolv[:, 2 * H + h:2 * H + h + 1]
    gc_r = growv[2 * H + h:2 * H + h + 1, :]
    ri = lax.broadcasted_iota(jnp.int32, (C, C), 0)
    ci = lax.broadcasted_iota(jnp.int32, (C, C), 1)
    incl = ri >= ci
    strict = ri > ci
    kb = k * beta_c
    vb = v * beta_c
    p_raw = _mm_nt(kb, k)
    qk_raw = _mm_nt(q, k)
    gam = jnp.where(incl, jnp.exp(jnp.where(incl, gc_c - gc_r, 0.0)), 0.0)
    e_c = jnp.exp(gc_c)
    gl = gc_r[:, C - 1:C]
    edec = jnp.exp(gl - gc_c)
    yield
    lmat = jnp.where(strict, p_raw * gam, 0.0)
    attn = jnp.where(incl, qk_raw * gam, 0.0)
    return dict(beta_c=beta_c, g_c=g_c, gc_c=gc_c, gc_r=gc_r, incl=incl, strict=strict, gam=gam, e_c=e_c,
                kb=kb, vb=vb, lmat=lmat, attn=attn, gl=gl, edec=edec, ri=ri, ci=ci)


INV_BLOCK = 16


def _inv_unit_lower(lmat):
    C = lmat.shape[0]
    ri = lax.broadcasted_iota(jnp.int32, (C, C), 0)
    ci = lax.broadcasted_iota(jnp.int32, (C, C), 1)
    eye = (ri == ci).astype(F32)
    same = (ri // INV_BLOCK) == (ci // INV_BLOCK)

    def neumann(x, order):
        a = eye + x
        n = 1
        while 2 * n < order:
            xs = _split(x)
            x = _mm3(xs, xs)
            yield
            a = a + _mm3(a, x)
            n *= 2
        yield
        return a

    inv_d = yield from neumann(-jnp.where(same, lmat, 0.0), INV_BLOCK)
    m = _mm3(inv_d, jnp.where(same, 0.0, lmat))
    yield
    inv_m = yield from neumann(-m, C // INV_BLOCK)
    a = _mm3(inv_m, inv_d)
    yield
    return a


def _delta_fwd(q, k, v, gcol, grow, H, D):
    T = q.shape[0]
    C = CHUNK_A
    N = T // C
    AW = H * D
    CPS = 2 if N % 2 == 0 else 1

    def body(q_ref, k_ref, v_ref, gcol_ref, grow_ref, o_ref, vn_ref, ssave_ref, asave_ref, s_ref):
        @pl.when(pl.program_id(0) == 0)
        def _():
            s_ref[...] = jnp.zeros_like(s_ref)

        state = {(0, h): s_ref[h] for h in range(H)}

        def head(cc, h):
            rows = slice(cc * C, (cc + 1) * C)
            sl = slice(h * D, (h + 1) * D)
            qv, kv, vv = q_ref[rows, sl], k_ref[rows, sl], v_ref[rows, sl]
            t = yield from _chunk_terms(qv, kv, vv, gcol_ref[rows, :], grow_ref[cc], h, H)
            a = yield from _inv_unit_lower(t["lmat"])
            asave_ref[cc, h] = a
            while (cc, h) not in state:
                yield
            st = state[(cc, h)]
            ssave_ref[cc, h] = st
            ks = _mm(t["kb"] * t["e_c"], st)
            o_inter = _mm(qv * t["e_c"], st)
            yield
            v_new = _mm3(a, t["vb"] - ks)
            yield
            vn_ref[rows, sl] = v_new
            o_intra = _mm(t["attn"], v_new)
            s_upd = _mm_tn(kv * t["edec"], v_new)
            yield
            o_ref[rows, sl] = o_inter + o_intra
            state[(cc + 1, h)] = st * jnp.exp(t["gl"]) + s_upd

        _interleave(head(cc, h) for cc in range(CPS) for h in range(H))
        for h in range(H):
            s_ref[h] = state[(CPS, h)]

    blk = lambda: pl.BlockSpec((CPS * C, AW), lambda n: (n, 0))
    return pl.pallas_call(
        body, name="delta_fwd", grid=(N // CPS,),
        in_specs=[blk(), blk(), blk(),
                  pl.BlockSpec((CPS * C, LANES), lambda n: (n, 0)),
                  pl.BlockSpec((CPS, 32, C), lambda n: (n, 0, 0))],
        out_specs=[blk(), blk(),
                   pl.BlockSpec((CPS, H, D, D), lambda n: (n, 0, 0, 0)),
                   pl.BlockSpec((CPS, H, C, C), lambda n: (n, 0, 0, 0))],
        out_shape=[S((T, AW), F32), S((T, AW), F32), S((N, H, D, D), F32), S((N, H, C, C), F32)],
        scratch_shapes=[pltpu.VMEM((H, D, D), F32)],
        compiler_params=_cp(ARB),
    )(q, k, v, gcol, grow)


def _delta_bwd(q, k, v, gcol, grow, ba, vnew, ssave, asave, d_o, a_log, dt_bias, H, D, carry):
    T = q.shape[0]
    C = CHUNK_A
    N = T // C
    AW = H * D
    nc = len(carry)
    CPS = 2 if N % 2 == 0 else 1
    NS = N // CPS

    def body(al_ref, dt_ref, q_ref, k_ref, v_ref, gcol_ref, grow_ref, ba_ref, vn_ref, ss_ref, as_ref, do_ref, *rest):
        cins = rest[:nc]
        dq_ref, dk_ref, dv_ref, dgate_ref, dpar_ref = rest[nc:nc + 5]
        couts = rest[nc + 5:2 * nc + 5]
        ds_ref, csend, crecv = rest[2 * nc + 5:]
        ccps = _chip_exchange_copies(cins, couts, csend, crecv)

        @pl.when(pl.program_id(0) == 0)
        def _():
            ds_ref[...] = jnp.zeros_like(ds_ref)
            dpar_ref[...] = jnp.zeros_like(dpar_ref)
            for cp in ccps:
                cp.start()

        lane = lax.broadcasted_iota(jnp.int32, (C, LANES), 1)
        rowi = lax.broadcasted_iota(jnp.int32, (C, 1), 0)
        acc = {cc: jnp.zeros((C, LANES), F32) for cc in range(CPS)}
        state = {(0, h): ds_ref[h] for h in range(H)}

        def head(oi, h):
            cc = CPS - 1 - oi
            rows = slice(cc * C, (cc + 1) * C)
            sl = slice(h * D, (h + 1) * D)
            st = ss_ref[cc, h]
            a = as_ref[cc, h]
            qv, kv, vv, dov, v_new = q_ref[rows, sl], k_ref[rows, sl], v_ref[rows, sl], do_ref[rows, sl], vn_ref[rows, sl]
            t = yield from _chunk_terms(qv, kv, vv, gcol_ref[rows, :], grow_ref[cc], h, H)
            beta_c, e_c, gam, kb = t["beta_c"], t["e_c"], t["gam"], t["kb"]
            incl, strict, attn, lmat, edec = t["incl"], t["strict"], t["attn"], t["lmat"], t["edec"]
            kdec = kv * edec
            egl = jnp.exp(t["gl"])
            qe = qv * e_c
            ekb = kb * e_c

            t1 = _mm_nt(dov, st)
            ds_o = _mm_tn(qe, dov)
            dattn_raw = _mm_nt(dov, v_new)
            dv_new_o = _mm_tn(attn, dov)
            yield
            while (oi, h) not in state:
                yield
            ds_next = state[(oi, h)]
            dkdec = _mm_nt(v_new, ds_next)
            dv_new_s = _mm(kdec, ds_next)
            yield
            dgl = egl * jnp.sum(jnp.sum(st * ds_next, axis=1, keepdims=True), axis=0, keepdims=True)
            dk = edec * dkdec
            r = jnp.sum(dkdec * kdec, axis=1, keepdims=True)
            dgc = -r
            dgl = dgl + jnp.sum(r, axis=0, keepdims=True)
            dq = e_c * t1
            dgc = dgc + jnp.sum(t1 * qe, axis=1, keepdims=True)
            dattn = jnp.where(incl, dattn_raw, 0.0)
            dv_new = dv_new_s + dv_new_o
            dqm = dattn * gam
            z = dattn * attn
            dvb = _mm3(a, dv_new, _TN)
            dq_a = _mm(dqm, kv)
            dk_a = _mm_tn(dqm, qv)
            yield
            dq_ref[rows, sl] = dq + dq_a
            dv_ref[rows, sl] = beta_c * dvb
            ds_kb = _mm_tn(ekb, dvb)
            dekb_neg = _mm_nt(dvb, st)
            dl_neg = _mm_nt(dvb, v_new)
            yield
            state[(oi + 1, h)] = egl * ds_next + ds_o - ds_kb
            dekb = -dekb_neg
            dl = jnp.where(strict, -dl_neg, 0.0)
            dp = dl * gam
            z = z + dl * lmat
            dkb_p = _mm(dp, kv)
            dk_p = _mm_tn(dp, kb)
            dgc = dgc + jnp.sum(dekb * ekb, axis=1, keepdims=True)
            dgc = dgc + jnp.sum(z, axis=1, keepdims=True) - jnp.sum(z.T, axis=1, keepdims=True)
            dgc = dgc + jnp.where(rowi == C - 1, dgl, 0.0)
            yield
            dkb = dkb_p + e_c * dekb
            dk_ref[rows, sl] = dk + dk_a + dk_p + beta_c * dkb
            dbeta = jnp.sum(dkb * kv, axis=1, keepdims=True) + jnp.sum(dvb * vv, axis=1, keepdims=True)
            acc[cc] = acc[cc] + jnp.where(lane == h, dbeta, 0.0) + jnp.where(lane == H + h, dgc, 0.0)

        _interleave(head(oi, h) for oi in range(CPS) for h in range(H))
        for h in range(H):
            ds_ref[h] = state[(CPS, h)]
        ri = lax.broadcasted_iota(jnp.int32, (C, C), 0)
        ci = lax.broadcasted_iota(jnp.int32, (C, C), 1)
        upper = (ri <= ci).astype(F32)
        dal = jnp.zeros((1, LANES), F32)
        ddt = jnp.zeros((1, LANES), F32)
        for cc in range(CPS):
            rows = slice(cc * C, (cc + 1) * C)
            gates = gcol_ref[rows, :]
            dg_all = _mm3(upper, acc[cc])
            d_braw = acc[cc] * gates * (1.0 - gates)
            d_araw = dg_all * (-jnp.exp(al_ref[...])) * _sigmoid(ba_ref[rows, :] + dt_ref[...])
            dgate_ref[rows, :] = jnp.where(lane < H, d_braw, jnp.where(lane < 2 * H, d_araw, 0.0)).astype(BF16)
            dal = dal + jnp.sum(dg_all * gates, axis=0, keepdims=True)
            ddt = ddt + jnp.sum(d_araw, axis=0, keepdims=True)
        dpar_ref[0:1, :] += dal
        dpar_ref[1:2, :] += ddt

        @pl.when(pl.program_id(0) == NS - 1)
        def _():
            for cp in ccps:
                cp.wait()

    rev = lambda s: NS - 1 - s
    blk = lambda: pl.BlockSpec((CPS * C, AW), lambda s: (rev(s), 0))
    row = pl.BlockSpec((1, LANES), lambda s: (0, 0))
    any_spec = pl.BlockSpec(memory_space=pl.ANY)
    res = pl.pallas_call(
        body, name="delta_bwd", grid=(NS,),
        in_specs=[row, row, blk(), blk(), blk(),
                  pl.BlockSpec((CPS * C, LANES), lambda s: (rev(s), 0)),
                  pl.BlockSpec((CPS, 32, C), lambda s: (rev(s), 0, 0)),
                  pl.BlockSpec((CPS * C, LANES), lambda s: (rev(s), 0)),
                  blk(),
                  pl.BlockSpec((CPS, H, D, D), lambda s: (rev(s), 0, 0, 0)),
                  pl.BlockSpec((CPS, H, C, C), lambda s: (rev(s), 0, 0, 0)),
                  blk()] + [any_spec] * nc,
        out_specs=[blk(), blk(), blk(),
                   pl.BlockSpec((CPS * C, LANES), lambda s: (rev(s), 0)),
                   pl.BlockSpec((8, LANES), lambda s: (0, 0))] + [any_spec] * nc,
        out_shape=[S((T, AW), F32), S((T, AW), F32), S((T, AW), F32),
                   S((T, LANES), BF16), S((8, LANES), F32)] + [S((3,) + a.shape[1:], a.dtype) for a in carry],
        scratch_shapes=[pltpu.VMEM((H, D, D), F32),
                        pltpu.SemaphoreType.DMA((max(nc, 1), 3)), pltpu.SemaphoreType.DMA((max(nc, 1), 3))],
        compiler_params=_cp(ARB),
    )(a_log, dt_bias, q, k, v, gcol, grow, ba, vnew, ssave, asave, d_o, *carry)
    return res[:5], res[5:]


def _ln_stats(xv):
    mu = jnp.mean(xv, axis=-1, keepdims=True)
    xc = xv - mu
    var = jnp.mean(xc * xc, axis=-1, keepdims=True)
    rstd = lax.rsqrt(var + EPS)
    return xc * rstd, rstd


def _mix_fwd(o, proj, head_norm_w, ln_w, ln_b, w_sp, bs_t, H, D, G, P):
    T = o.shape[0]
    AW, BW = H * D, G * P
    MIX = AW + BW
    nb = AW // BW if AW % BW == 0 else None
    assert nb == 1, "group widths must match the projection column blocks"
    cb = 3

    def body(o_ref, za_ref, ub_ref, vb_ref, zb_ref, hw_ref, lw_ref, lb_ref, w_ref, bs_ref, out_ref):
        hw = hw_ref[...]
        for h in range(H):
            sl = slice(h * D, (h + 1) * D)
            oh = o_ref[:, sl]
            rs = lax.rsqrt(jnp.mean(oh * oh, axis=-1, keepdims=True) + EPS)
            out_ref[:, sl] = (oh * rs * hw * _silu(za_ref[:, sl])).astype(BF16)
        xhat, _ = _ln_stats(vb_ref[...])
        vn = xhat * lw_ref[...] + lb_ref[...]
        ri = lax.broadcasted_iota(jnp.int32, (P, P), 0)
        ci = lax.broadcasted_iota(jnp.int32, (P, P), 1)
        bsv = bs_ref[...]
        for g in range(G):
            sl = slice(g * P, (g + 1) * P)
            wm = jnp.where(ri >= ci, w_ref[g], 0.0)
            s = _mm(wm, vn[:, sl]) + bsv[:, g:g + 1]
            out_ref[:, AW + g * P:AW + (g + 1) * P] = (ub_ref[:, sl] * s * _silu(zb_ref[:, sl])).astype(BF16)

    row = lambda w: pl.BlockSpec((1, w), lambda i: (0, 0))
    return pl.pallas_call(
        body, name="mix_fwd", grid=(T // P,),
        in_specs=[pl.BlockSpec((P, AW), lambda i: (i, 0)),
                  pl.BlockSpec((P, AW), lambda i: (i, cb)),
                  pl.BlockSpec((P, BW), lambda i: (i, cb + 1)),
                  pl.BlockSpec((P, BW), lambda i: (i, cb + 2)),
                  pl.BlockSpec((P, BW), lambda i: (i, cb + 3)),
                  row(D), row(BW), row(BW),
                  pl.BlockSpec((G, P, P), lambda i: (0, 0, 0)),
                  pl.BlockSpec((P, G), lambda i: (0, 0))],
        out_specs=pl.BlockSpec((P, MIX), lambda i: (i, 0)),
        out_shape=S((T, MIX), BF16),
        compiler_params=_cp(ARB),
    )(o, proj, proj, proj, proj, head_norm_w, ln_w, ln_b, w_sp, bs_t)


def _mix_bwd(d_ocat, o, proj, head_norm_w, ln_w, ln_b, w_sp, bs_t, H, D, G, P, carry):
    T = o.shape[0]
    AW, BW = H * D, G * P
    MIX = AW + BW
    cb = 3
    nc = len(carry)

    def body(dc_ref, o_ref, za_ref, ub_ref, vb_ref, zb_ref, hw_ref, lw_ref, lb_ref, w_ref, bs_ref, *rest):
        cins = rest[:nc]
        do_ref, dmain_ref, dhw_ref, dln_ref, dw_ref, dbs_ref = rest[nc:nc + 6]
        couts = rest[nc + 6:2 * nc + 6]
        dvn_ref, drest_ref, out_sems, csend, crecv = rest[2 * nc + 6:]
        i = pl.program_id(0)
        slot = lax.rem(i, 2)
        ccps = _sibling_copies(cins, couts, csend, crecv)

        def out_copy(step, s):
            return pltpu.make_async_copy(
                drest_ref.at[s], dmain_ref.at[pl.ds(step * P, P), pl.ds(cb * AW, AW + 3 * BW)], out_sems.at[s])

        @pl.when(i == 0)
        def _():
            dhw_ref[...] = jnp.zeros_like(dhw_ref)
            dln_ref[...] = jnp.zeros_like(dln_ref)
            dw_ref[...] = jnp.zeros_like(dw_ref)
            dbs_ref[...] = jnp.zeros_like(dbs_ref)
            for cp in ccps:
                cp.start()

        @pl.when(i >= 2)
        def _():
            out_copy(i - 2, slot).wait()

        hw = hw_ref[...]
        dhw = jnp.zeros((1, D), F32)
        for h in range(H):
            sl = slice(h * D, (h + 1) * D)
            oh = o_ref[:, sl]
            za = za_ref[:, sl]
            doa = dc_ref[:, sl]
            rs = lax.rsqrt(jnp.mean(oh * oh, axis=-1, keepdims=True) + EPS)
            xh = oh * rs
            d_on = doa * _silu(za)
            drest_ref[slot, :, sl] = (doa * (xh * hw) * _dsilu(za)).astype(BF16)
            dhw = dhw + jnp.sum(d_on * xh, axis=0, keepdims=True)
            dxh = d_on * hw
            do_ref[:, sl] = rs * (dxh - xh * jnp.mean(dxh * xh, axis=-1, keepdims=True))
        dhw_ref[0:1, :] += dhw

        xhat, rstd = _ln_stats(vb_ref[...])
        lw = lw_ref[...]
        vn = xhat * lw + lb_ref[...]
        ri = lax.broadcasted_iota(jnp.int32, (P, P), 0)
        ci = lax.broadcasted_iota(jnp.int32, (P, P), 1)
        lane = lax.broadcasted_iota(jnp.int32, (P, LANES), 1)
        bsv = bs_ref[...]
        dbs = jnp.zeros((P, LANES), F32)
        for g in range(G):
            sl = slice(g * P, (g + 1) * P)
            wm = jnp.where(ri >= ci, w_ref[g], 0.0)
            vng = vn[:, sl]
            s = _mm(wm, vng) + bsv[:, g:g + 1]
            dob = dc_ref[:, AW + g * P:AW + (g + 1) * P]
            ub = ub_ref[:, sl]
            zb = zb_ref[:, sl]
            szb = _silu(zb)
            drest_ref[slot, :, AW + g * P:AW + (g + 1) * P] = (dob * s * szb).astype(BF16)
            drest_ref[slot, :, AW + 2 * BW + g * P:AW + 2 * BW + (g + 1) * P] = (
                dob * ub * s * _dsilu(zb)).astype(BF16)
            ds = dob * ub * szb
            dvn_ref[:, sl] = _mm_tn(wm, ds)
            dw_ref[g] += jnp.where(ri >= ci, _mm_nt(ds, vng), 0.0)
            dbs = dbs + jnp.where(lane == g, jnp.sum(ds, axis=1, keepdims=True), 0.0)
        dbs_ref[...] += dbs
        dvn = dvn_ref[...]
        dln_ref[0:1, :] += jnp.sum(dvn * xhat, axis=0, keepdims=True)
        dln_ref[1:2, :] += jnp.sum(dvn, axis=0, keepdims=True)
        dxh = dvn * lw
        dvb = rstd * (dxh - jnp.mean(dxh, axis=-1, keepdims=True) - xhat * jnp.mean(dxh * xhat, axis=-1, keepdims=True))
        drest_ref[slot, :, AW + BW:AW + 2 * BW] = dvb.astype(BF16)

        out_copy(i, slot).start()

        @pl.when(i == nstep - 1)
        def _():
            out_copy(i, slot).wait()
            if nstep > 1:
                out_copy(i - 1, 1 - slot).wait()
            for cp in ccps:
                cp.wait()

    nstep = T // P
    row = lambda w: pl.BlockSpec((1, w), lambda i: (0, 0))
    any_spec = pl.BlockSpec(memory_space=pl.ANY)
    res = pl.pallas_call(
        body, name="mix_bwd", grid=(nstep,),
        in_specs=[pl.BlockSpec((P, MIX), lambda i: (i, 0)),
                  pl.BlockSpec((P, AW), lambda i: (i, 0)),
                  pl.BlockSpec((P, AW), lambda i: (i, cb)),
                  pl.BlockSpec((P, BW), lambda i: (i, cb + 1)),
                  pl.BlockSpec((P, BW), lambda i: (i, cb + 2)),
                  pl.BlockSpec((P, BW), lambda i: (i, cb + 3)),
                  row(D), row(BW), row(BW),
                  pl.BlockSpec((G, P, P), lambda i: (0, 0, 0)),
                  pl.BlockSpec((P, G), lambda i: (0, 0))] + [any_spec] * nc,
        out_specs=[pl.BlockSpec((P, AW), lambda i: (i, 0)),
                   any_spec,
                   pl.BlockSpec((8, D), lambda i: (0, 0)),
                   pl.BlockSpec((8, BW), lambda i: (0, 0)),
                   pl.BlockSpec((G, P, P), lambda i: (0, 0, 0)),
                   pl.BlockSpec((P, LANES), lambda i: (0, 0))] + [any_spec] * nc,
        out_shape=[S((T, AW), F32), S((T, cb * AW + AW + 3 * BW), BF16), S((8, D), F32), S((8, BW), F32),
                   S((G, P, P), F32), S((P, LANES), F32)] + [S(a.shape[:1] + a.shape[2:], a.dtype) for a in carry],
        scratch_shapes=[pltpu.VMEM((P, BW), F32), pltpu.VMEM((2, P, AW + 3 * BW), BF16),
                        pltpu.SemaphoreType.DMA((2,))] + _sibling_sems(carry),
        compiler_params=_cp(ARB),
    )(d_ocat, o, proj, proj, proj, proj, head_norm_w, ln_w, ln_b, w_sp, bs_t, *carry)
    return res[:6], res[6:]


def _out_proj_loss(ocat, w_out, x, target, fnw):
    T, MIX = ocat.shape
    DM = x.shape[1]
    tm = _tile(T, 256, 8)

    def body(oc_ref, w_ref, x_ref, t_ref, fw_ref, dh_ref, dhb_ref, doc_ref, loss_ref, gfw_ref):
        @pl.when(pl.program_id(0) == 0)
        def _():
            loss_ref[...] = jnp.zeros_like(loss_ref)
            gfw_ref[...] = jnp.zeros_like(gfw_ref)

        wv = w_ref[...]
        hh = x_ref[...] + jnp.dot(oc_ref[...].astype(MXU), wv.astype(MXU), preferred_element_type=F32)
        rs = lax.rsqrt(jnp.mean(hh * hh, axis=-1, keepdims=True) + EPS)
        hn = hh * rs
        fw = fw_ref[...]
        e = hn * fw - t_ref[...]
        row_loss = 0.5 * jnp.mean(e * e, axis=-1, keepdims=True)
        loss_ref[...] += jnp.sum(row_loss, axis=0, keepdims=True)
        dy = e * (1.0 / DM)
        gfw_ref[0:1, :] += jnp.sum(dy * hn, axis=0, keepdims=True)
        dhn = dy * fw
        dh = rs * (dhn - hn * jnp.mean(dhn * hn, axis=-1, keepdims=True))
        dh_ref[...] = dh
        dhb = dh.astype(BF16)
        dhb_ref[...] = dhb
        doc_ref[...] = _mm_nt(dhb, wv)

    return pl.pallas_call(
        body, name="out_proj_loss", grid=(T // tm,),
        in_specs=[pl.BlockSpec((tm, MIX), lambda i: (i, 0)),
                  pl.BlockSpec((MIX, DM), lambda i: (0, 0)),
                  pl.BlockSpec((tm, DM), lambda i: (i, 0)),
                  pl.BlockSpec((tm, DM), lambda i: (i, 0)),
                  pl.BlockSpec((1, DM), lambda i: (0, 0))],
        out_specs=[pl.BlockSpec((tm, DM), lambda i: (i, 0)),
                   pl.BlockSpec((tm, DM), lambda i: (i, 0)),
                   pl.BlockSpec((tm, MIX), lambda i: (i, 0)),
                   pl.BlockSpec((8, LANES), lambda i: (0, 0)),
                   pl.BlockSpec((8, DM), lambda i: (0, 0))],
        out_shape=[S((T, DM), F32), S((T, DM), BF16), S((T, MIX), F32), S((8, LANES), F32), S((8, DM), F32)],
        compiler_params=_cp(ARB),
    )(ocat, w_out, x, target, fnw)


def _grad_w(lhs, rhs, name):
    T, A = lhs.shape
    B = rhs.shape[1]
    ta = _tile(A, 512, LANES)
    tk = _tile(T, 1024, 16)
    nk = T // tk

    def body(l_ref, r_ref, out_ref, acc_ref):
        k = pl.program_id(1)
        part = _mm_tn(l_ref[...], r_ref[...])

        @pl.when(k == 0)
        def _():
            acc_ref[...] = part

        @pl.when(k > 0)
        def _():
            acc_ref[...] += part

        @pl.when(k == nk - 1)
        def _():
            out_ref[...] = acc_ref[...].astype(BF16)

    return pl.pallas_call(
        body, name=name, grid=(A // ta, nk),
        in_specs=[pl.BlockSpec((tk, ta), lambda i, k: (k, i)),
                  pl.BlockSpec((tk, B), lambda i, k: (k, 0))],
        out_specs=pl.BlockSpec((ta, B), lambda i, k: (i, 0)),
        out_shape=S((A, B), BF16),
        scratch_shapes=[pltpu.VMEM((ta, B), F32)],
        compiler_params=_cp(ARB, ARB),
    )(lhs, rhs)


def _grad_w_in(xn, dmain, dba, WD, gate_lo, gate_hi):
    T, DM = xn.shape
    NM = dmain.shape[1]
    tn = _tile(NM, 1024, LANES)
    tk = _tile(T, 2048, 16)
    nj, nk = NM // tn, T // tk
    ND = N_DEV
    tiles = [[] for _ in range(nj)]
    first_tile, last_tile = {}, {}
    for d, s0, s1, dest, c0 in _pieces(WD, gate_lo, gate_hi, ND * WD):
        if dest != "main":
            continue
        while s0 < s1:
            jj = c0 // tn
            w = min(s1 - s0, (jj + 1) * tn - c0)
            tiles[jj].append((d, s0, w, "main", c0 - jj * tn))
            first_tile.setdefault(d, jj)
            last_tile[d] = jj
            s0, c0 = s0 + w, c0 + w
    for d, s0, s1, dest, c0 in _pieces(WD, gate_lo, gate_hi, ND * WD):
        if dest == "gate":
            tiles[first_tile[d]].append((d, s0, s1 - s0, "gate", c0))
    assert sorted(first_tile) == list(range(ND)) and all(last_tile[d] <= first_tile[d + 2] for d in range(ND - 2))

    def body(xn_ref, dm_ref, dba_ref, keep_ref, recv_ref, acc_ref, gate_ref, buf_ref, lsem, ssem, rsem):
        j = pl.program_id(0)
        k = pl.program_id(1)
        px, py, pc = _position()

        @pl.when(k == 0)
        def _():
            acc_ref[...] = jnp.zeros_like(acc_ref)

        @pl.when((j == 0) & (k == 0))
        def _():
            gate_ref[...] = jnp.zeros_like(gate_ref)

        xv = xn_ref[...]
        acc_ref[...] += _mm_tn(xv, dm_ref[...])

        @pl.when(j == 0)
        def _():
            gate_ref[...] += _mm_tn(xv, dba_ref[...])

        def local(d):
            return pltpu.make_async_copy(buf_ref.at[d % 2], keep_ref.at[d // 2], lsem.at[d // 2])

        def remote(d):
            return pltpu.make_async_remote_copy(
                src_ref=buf_ref.at[d % 2], dst_ref=recv_ref.at[d // 2], send_sem=ssem.at[d // 2],
                recv_sem=rsem.at[d // 2], device_id=(px, py, 1 - pc), device_id_type=MESH)

        def leave(d, start):
            @pl.when(pc == d % 2)
            def _():
                local(d).start() if start else local(d).wait()

            @pl.when(pc != d % 2)
            def _():
                remote(d).start() if start else remote(d).wait_send()

        def emit(jj):
            shards = sorted({p[0] for p in tiles[jj]})
            for d in shards:
                if first_tile[d] == jj and d >= 2:
                    leave(d - 2, False)
                for dd, s0, w, src, c0 in tiles[jj]:
                    if dd == d:
                        ref = acc_ref if src == "main" else gate_ref
                        buf_ref[d % 2, :, s0:s0 + w] = ref[:, c0:c0 + w].astype(BF16)
                if last_tile[d] == jj:
                    leave(d, True)
            if jj == nj - 1:
                for d in (ND - 2, ND - 1):
                    leave(d, False)
                for q in range(ND // 2):
                    remote(2 * q).wait_recv()

        for jj in range(nj):
            @pl.when((j == jj) & (k == nk - 1))
            def _(jj=jj):
                emit(jj)

    any_spec = pl.BlockSpec(memory_space=pl.ANY)
    return pl.pallas_call(
        body, name="grad_w_in", grid=(nj, nk),
        in_specs=[pl.BlockSpec((tk, DM), lambda j, k: (k, 0)),
                  pl.BlockSpec((tk, tn), lambda j, k: (k, j)),
                  pl.BlockSpec((tk, LANES), lambda j, k: (k, 0))],
        out_specs=[any_spec, any_spec],
        out_shape=[S((ND // 2, DM, WD), BF16), S((ND // 2, DM, WD), BF16)],
        scratch_shapes=[pltpu.VMEM((DM, tn), F32), pltpu.VMEM((DM, LANES), F32), pltpu.VMEM((2, DM, WD), BF16),
                        pltpu.SemaphoreType.DMA((ND // 2,)), pltpu.SemaphoreType.DMA((ND // 2,)),
                        pltpu.SemaphoreType.DMA((ND // 2,))],
        compiler_params=_cp(ARB, ARB),
    )(xn, dmain, dba)


def _pair_sum_plain(a, b, name):
    K, R, C = a.shape
    tr = _tile(R, 1024, 16)

    def body(a_ref, b_ref, o_ref):
        o_ref[...] = (a_ref[...].astype(F32) + b_ref[...].astype(F32)).astype(BF16)

    spec = lambda: pl.BlockSpec((1, tr, C), lambda q, i: (q, i, 0))
    return pl.pallas_call(body, name=name, grid=(K, R // tr), in_specs=[spec(), spec()], out_specs=spec(),
                          out_shape=S((K, R, C), BF16), compiler_params=_cp(ARB, ARB))(a, b)


def _dx_rows(T):
    tm = _tile(T, 512, 8)
    return tm if T // tm >= 2 else T // 2


def _dx_part(name, dmain, dba, w_main, w_ba, x, dh, norm_w, blk0, nblk, prev, hbm_in, hbm_alias, hbm_new, make_copies):
    T, NM = dmain.shape
    DM = x.shape[1]
    tm = _dx_rows(T)
    tk = _tile(NM, 1024, LANES)
    nk = NM // tk
    n_in, n_al, n_new = len(hbm_in), len(hbm_alias), len(hbm_new)
    n_prev = 0 if prev is None else 2
    last_step = nblk * nk - 1

    def body(dm_ref, dba_ref, w_ref, wba_ref, x_ref, dh_ref, nw_ref, *rest):
        r = list(rest)
        gnw_prev_ref = r.pop(0) if n_prev else None
        if n_prev:
            r.pop(0)
        in_refs = [r.pop(0) for _ in range(n_in)]
        del r[:n_al]
        gx_ref, gnw_ref = r.pop(0), r.pop(0)
        alias_refs = [r.pop(0) for _ in range(n_al)]
        new_refs = [r.pop(0) for _ in range(n_new)]
        acc_ref, send_sems, recv_sems = r
        i = pl.program_id(0)
        k = pl.program_id(1)
        step = i * nk + k
        cps = make_copies(in_refs, alias_refs, new_refs, send_sems, recv_sems)

        @pl.when(step == 0)
        def _():
            gnw_ref[...] = gnw_prev_ref[...] if n_prev else jnp.zeros_like(gnw_ref)
            for cp in cps:
                cp.start()

        @pl.when(k == 0)
        def _():
            acc_ref[...] = _mm_nt(dba_ref[...], wba_ref[...])

        acc_ref[...] += _mm_nt(dm_ref[...], w_ref[...])

        @pl.when(k == nk - 1)
        def _():
            xv = x_ref[...]
            rs = lax.rsqrt(jnp.mean(xv * xv, axis=-1, keepdims=True) + EPS)
            xh = xv * rs
            dxn = acc_ref[...]
            gnw_ref[0:1, :] += jnp.sum(dxn * xh, axis=0, keepdims=True)
            dxh = dxn * nw_ref[...]
            gx_ref[...] = dh_ref[...] + rs * (dxh - xh * jnp.mean(dxh * xh, axis=-1, keepdims=True))

        @pl.when(step == last_step)
        def _():
            for cp in cps:
                cp.wait()

    any_spec = pl.BlockSpec(memory_space=pl.ANY)
    prev_specs = [pl.BlockSpec((8, DM), lambda i, k: (0, 0)), any_spec] if n_prev else []
    prev_args = [prev[1], prev[0]] if n_prev else []
    aliases = {8: 0} if n_prev else {}
    for q in range(n_al):
        aliases[7 + n_prev + n_in + q] = 2 + q
    res = pl.pallas_call(
        body, name=name, grid=(nblk, nk),
        in_specs=[pl.BlockSpec((tm, tk), lambda i, k: (blk0 + i, k)),
                  pl.BlockSpec((tm, LANES), lambda i, k: (blk0 + i, 0)),
                  pl.BlockSpec((DM, tk), lambda i, k: (0, k)),
                  pl.BlockSpec((DM, LANES), lambda i, k: (0, 0)),
                  pl.BlockSpec((tm, DM), lambda i, k: (blk0 + i, 0)),
                  pl.BlockSpec((tm, DM), lambda i, k: (blk0 + i, 0)),
                  pl.BlockSpec((1, DM), lambda i, k: (0, 0))] + prev_specs + [any_spec] * (n_in + n_al),
        out_specs=[pl.BlockSpec((tm, DM), lambda i, k: (blk0 + i, 0)),
                   pl.BlockSpec((8, DM), lambda i, k: (0, 0))] + [any_spec] * (n_al + n_new),
        out_shape=[S((T, DM), F32), S((8, DM), F32)] + [S(a.shape, a.dtype) for a in hbm_alias] + list(hbm_new),
        scratch_shapes=[pltpu.VMEM((tm, DM), F32), pltpu.SemaphoreType.DMA((10,)), pltpu.SemaphoreType.DMA((10,))],
        input_output_aliases=aliases,
        compiler_params=_cp(ARB, ARB),
    )(dmain, dba, w_main, w_ba, x, dh, norm_w, *prev_args, *hbm_in, *hbm_alias)
    return (res[0], res[1]), res[2:2 + n_al], res[2 + n_al:]


def _remote(kk, src, dst, to, send_sems, recv_sems):
    return pltpu.make_async_remote_copy(src_ref=src, dst_ref=dst, send_sem=send_sems.at[kk], recv_sem=recv_sems.at[kk],
                                        device_id=to, device_id_type=MESH)


def _dx(dmain, dba, w_main, w_ba, x, dh, norm_w, chip_sum, small, cut):
    R, C = chip_sum.shape[1:]
    half = R // 2
    assert half % 16 == 0
    T = x.shape[0]
    ni = T // _dx_rows(T)
    cut = max(1, min(cut, ni - 1))
    upper, lower = pl.ds(0, half), pl.ds(half, half)

    def nbrs():
        px, py, pc = _position()
        return (px, py), (1 - px, py, pc), (px, 1 - py, pc)

    def phase1(ins, als, news, ss, rs):
        (px, py), xn, yn = nbrs()
        cs = ins[0]
        recv, stage = news
        bx, by, bd = cs.at[2 * (1 - px) + py], cs.at[2 * px + (1 - py)], cs.at[2 * (1 - px) + (1 - py)]
        return [_remote(0, bx.at[upper], recv.at[0].at[upper], xn, ss, rs),
                _remote(1, by.at[lower], recv.at[1].at[lower], yn, ss, rs),
                _remote(2, bd.at[upper], stage.at[0], xn, ss, rs),
                _remote(3, bd.at[lower], stage.at[1], yn, ss, rs)]

    def phase2(ins, als, news, ss, rs):
        (px, py), xn, yn = nbrs()
        comb, small_ref = ins
        recv, gath = als[0], news[0]
        me, small_cps = _broadcast_copies([small_ref], [gath], _Sem2(ss, 2), _Sem2(rs, 2))
        return ([_remote(0, comb.at[0], recv.at[1].at[upper], yn, ss, rs),
                 _remote(1, comb.at[1], recv.at[0].at[lower], xn, ss, rs)] + small_cps
                + [pltpu.make_async_copy(small_ref, gath.at[me], ss.at[9])])

    (gx, gnw), _, (recv, stage) = _dx_part(
        "dx_a", dmain, dba, w_main, w_ba, x, dh, norm_w, 0, cut, None, [chip_sum], [],
        [S((2, R, C), chip_sum.dtype), S((2, half, C), chip_sum.dtype)], phase1)
    comb = _relay_add(chip_sum, stage)
    (gx, gnw), (recv,), (gath,) = _dx_part(
        "dx_b", dmain, dba, w_main, w_ba, x, dh, norm_w, cut, ni - cut, (gx, gnw), [comb, small], [recv],
        [S((N_DEV,) + small.shape, F32)], phase2)
    return gx, gnw, gath, recv


class _Sem2:
    def __init__(self, sems, lo):
        self.sems, self.lo = sems, lo

    @property
    def at(self):
        outer = self

        class _At:
            def __getitem__(self, idx):
                a, k = idx
                return outer.sems.at[outer.lo + k]
        return _At()


def _relay_add(chip_sum, stage):
    _, R, C = chip_sum.shape
    half = R // 2
    tr = _tile(half, 256, 16)
    nt = half // tr
    px, py, _ = _position()
    idx = jnp.stack([2 * px + (1 - py), 2 * (1 - px) + py]).astype(jnp.int32)

    def body(idx_ref, p_ref, s_ref, o_ref):
        del idx_ref
        o_ref[0] = (p_ref[0].astype(F32) + s_ref[0].astype(F32)).astype(BF16)

    return pl.pallas_call(
        body, name="relay_add",
        grid_spec=pltpu.PrefetchScalarGridSpec(
            num_scalar_prefetch=1, grid=(2, nt),
            in_specs=[pl.BlockSpec((1, tr, C), lambda s, i, idx_ref: (idx_ref[s], s * nt + i, 0)),
                      pl.BlockSpec((1, tr, C), lambda s, i, idx_ref: (s, i, 0))],
            out_specs=pl.BlockSpec((1, tr, C), lambda s, i, idx_ref: (s, i, 0))),
        out_shape=S((2, half, C), BF16), compiler_params=_cp(ARB, ARB),
    )(idx, chip_sum, stage)


def _copy_rows(a, name, parts=8):
    R = a.shape[0]
    assert R % parts == 0
    rb = R // parts

    def body(a_ref, o_ref, sems):
        cps = [pltpu.make_async_copy(a_ref.at[pl.ds(q * rb, rb)], o_ref.at[pl.ds(q * rb, rb)], sems.at[q])
               for q in range(parts)]
        for cp in cps:
            cp.start()
        for cp in cps:
            cp.wait()

    any_spec = pl.BlockSpec(memory_space=pl.ANY)
    return pl.pallas_call(body, name=name, in_specs=[any_spec], out_specs=any_spec, out_shape=S(a.shape, a.dtype),
                          scratch_shapes=[pltpu.SemaphoreType.DMA((parts,))])(a)


def _sum_slots(gath, shapes):
    spans, outs, r = [], [], 0
    for shp in shapes:
        n = 1
        for s in shp:
            n *= s
        nr = -(-n // (8 * LANES)) * 8
        spans.append((r, nr, n))
        outs.append(S((1, n), F32) if n < LANES else S((nr, LANES), F32))
        r += nr
    assert r == gath.shape[1] and gath.shape[2] == LANES

    def body(g_ref, *o_refs):
        tot = g_ref[0]
        for d in range(1, N_DEV):
            tot = tot + g_ref[d]
        for (r0, nr, n), o_ref in zip(spans, o_refs):
            o_ref[...] = tot[r0:r0 + 1, :n] if n < LANES else tot[r0:r0 + nr]

    vm = pl.BlockSpec(memory_space=pltpu.VMEM)
    res = pl.pallas_call(body, name="sum_slots", in_specs=[vm], out_specs=[vm] * len(outs), out_shape=outs)(gath)
    return [a.reshape(-1)[:n].reshape(shp) for a, (_, _, n), shp in zip(res, spans, shapes)]


def _prep_a_bwd(dq, dk, dv, c, proj, conv_w, dmain, H, D):
    T = c.shape[0]
    AW = H * D
    C3 = 3 * AW
    tb = _tile(T, 256, 8)
    nblk = T // tb
    r8 = tb // 8
    scale = float(D) ** -0.5

    def body(dq_ref, dk_ref, dv_ref, c_ref, dqn_ref, dkn_ref, dvn_ref, cn_ref, x_ref, halo_ref, cw_ref, dmain_in_ref,
             dx_ref, gcw_ref, dc_ref):
        del dmain_in_ref
        i = pl.program_id(0)

        @pl.when(i == 0)
        def _():
            gcw_ref[...] = jnp.zeros_like(gcw_ref)

        def pointwise(rows, dq_r, dk_r, dv_r, c_r, keep):
            for h in range(H):
                for part, d_r, sc in ((0, dq_r, scale), (1, dk_r, 1.0)):
                    sl = slice(part * AW + h * D, part * AW + (h + 1) * D)
                    cv = c_r[:, sl]
                    raw = _silu(cv)
                    rs = lax.rsqrt(jnp.sum(raw * raw, axis=-1, keepdims=True) + EPS)
                    nrm = raw * rs
                    dn = d_r[:, h * D:(h + 1) * D] * sc
                    draw = rs * (dn - nrm * jnp.sum(dn * nrm, axis=-1, keepdims=True))
                    dc_ref[rows, sl] = draw * _dsilu(cv) * keep
            dc_ref[rows, 2 * AW:] = dv_r[...] * _dsilu(c_r[:, 2 * AW:]) * keep

        pointwise(slice(0, tb), dq_ref, dk_ref, dv_ref, c_ref, 1.0)
        pointwise(slice(tb, tb + 8), dqn_ref, dkn_ref, dvn_ref, cn_ref, (i < nblk - 1).astype(F32))

        cw = cw_ref[...]
        dcv = dc_ref[0:tb, :]
        dx = cw[3:4, :] * dcv
        for j in range(3):
            dx = dx + cw[j:j + 1, :] * dc_ref[3 - j:3 - j + tb, :]
        dx_ref[...] = dx.astype(BF16)
        halo = halo_ref[...] * (i > 0).astype(F32)
        xp = jnp.concatenate([halo, x_ref[...]], axis=0)
        for j in range(4):
            gcw_ref[j:j + 1, :] += jnp.sum(dcv * xp[5 + j:5 + j + tb], axis=0, keepdims=True)

    nxt = lambda i: (jnp.minimum((i + 1) * r8, T // 8 - 1), 0)
    return pl.pallas_call(
        body, name="prep_a_bwd", grid=(nblk,),
        in_specs=[pl.BlockSpec((tb, AW), lambda i: (i, 0)),
                  pl.BlockSpec((tb, AW), lambda i: (i, 0)),
                  pl.BlockSpec((tb, AW), lambda i: (i, 0)),
                  pl.BlockSpec((tb, C3), lambda i: (i, 0)),
                  pl.BlockSpec((8, AW), nxt), pl.BlockSpec((8, AW), nxt), pl.BlockSpec((8, AW), nxt),
                  pl.BlockSpec((8, C3), nxt),
                  pl.BlockSpec((tb, C3), lambda i: (i, 0)),
                  pl.BlockSpec((8, C3), lambda i: (jnp.maximum(i * r8 - 1, 0), 0)),
                  pl.BlockSpec((4, C3), lambda i: (0, 0)),
                  pl.BlockSpec(memory_space=pl.ANY)],
        out_specs=[pl.BlockSpec((tb, C3), lambda i: (i, 0)),
                   pl.BlockSpec((8, C3), lambda i: (0, 0))],
        out_shape=[S(dmain.shape, dmain.dtype), S((8, C3), F32)],
        scratch_shapes=[pltpu.VMEM((tb + 8, C3), F32)],
        input_output_aliases={11: 0},
        compiler_params=_cp(ARB),
    )(dq, dk, dv, c, dq, dk, dv, c, proj, proj, conv_w, dmain)


def _adam_math(w, g, m, v):
    m2 = ADAM_B1 * m + (1.0 - ADAM_B1) * g
    v2 = ADAM_B2 * v + (1.0 - ADAM_B2) * (g * g)
    m_hat = m2 / (1.0 - ADAM_B1 ** ADAM_STEP)
    v_hat = v2 / (1.0 - ADAM_B2 ** ADAM_STEP)
    delta = -ADAM_LR * (m_hat / (jnp.sqrt(v_hat) + ADAM_EPS) + ADAM_WD * w)
    return delta, m2, v2


def _pair_sum(blocks, recv, core, name):
    K, _, R, C = blocks.shape
    tr = _tile(R, 256, 16)

    def body(core_ref, a_ref, b_ref, o_ref):
        del core_ref
        o_ref[0] = (a_ref[0, 0].astype(F32) + b_ref[0].astype(F32)).astype(BF16)

    spec = lambda: pl.BlockSpec((1, tr, C), lambda k, i, core_ref: (k, i, 0))
    return pl.pallas_call(
        body, name=name,
        grid_spec=pltpu.PrefetchScalarGridSpec(
            num_scalar_prefetch=1, grid=(K, R // tr),
            in_specs=[pl.BlockSpec((1, 1, tr, C), lambda k, i, core_ref: (k, core_ref[0], i, 0)), spec()],
            out_specs=spec()),
        out_shape=S((K, R, C), BF16), compiler_params=_cp(ARB, ARB),
    )(core, blocks, recv)


def _sum_adam(chip_sums, recv, w, m, v, chip, name, transposed=False):
    R, C = chip_sums.shape[1:]
    NR = recv.shape[0]
    tr = _tile(R, min(256, max(R // 4, 16)), 16)

    def body(chip_ref, own_ref, r_ref, w_ref, m_ref, v_ref, g_ref, d_ref, m2_ref, v2_ref):
        del chip_ref
        g = own_ref[0].astype(F32)
        for j in range(NR):
            g = g + r_ref[j].astype(F32)
        if transposed:
            g = g.T
        g_ref[...] = g
        d_ref[...], m2_ref[...], v2_ref[...] = _adam_math(w_ref[...], g, m_ref[...], v_ref[...])

    if transposed:
        spec = lambda: pl.BlockSpec((C, tr), lambda i, chip_ref: (0, i))
        shape = (C, R)
    else:
        spec = lambda: pl.BlockSpec((tr, C), lambda i, chip_ref: (i, 0))
        shape = (R, C)
    assert w.shape == shape
    return pl.pallas_call(
        body, name=name,
        grid_spec=pltpu.PrefetchScalarGridSpec(
            num_scalar_prefetch=1, grid=(R // tr,),
            in_specs=[pl.BlockSpec((1, tr, C), lambda i, chip_ref: (chip_ref[0], i, 0)),
                      pl.BlockSpec((NR, tr, C), lambda i, chip_ref: (0, i, 0)), spec(), spec(), spec()],
            out_specs=[spec(), spec(), spec(), spec()]),
        out_shape=[S(shape, F32)] * 4, compiler_params=_cp(ARB),
    )(chip, chip_sums, recv, w, m, v)


def _adam_small(ws, gs, ms, vs):
    n = len(ws)

    def body(*refs):
        ins, outs = refs[:4 * n], refs[4 * n:]
        for p in range(n):
            w_ref, g_ref, m_ref, v_ref = (ins[a * n + p] for a in range(4))
            outs[p][...], outs[n + p][...], outs[2 * n + p][...] = _adam_math(
                w_ref[...], g_ref[...], m_ref[...], v_ref[...])

    vm = pl.BlockSpec(memory_space=pltpu.VMEM)
    res = pl.pallas_call(
        body, name="adam_small", in_specs=[vm] * (4 * n), out_specs=[vm] * (3 * n),
        out_shape=[S(w.shape, F32) for w in ws] * 3,
    )(*ws, *gs, *ms, *vs)
    return res[:n], res[n:2 * n], res[2 * n:]


def _position():
    return lax.axis_index("x"), lax.axis_index("y"), lax.axis_index("c")


def _all_gather_weights(arr, x_in, norm_w, chip, tn, plans, nm):
    R = arr.shape[0]
    half = R // 2
    assert half % 16 == 0
    T, DM = x_in.shape
    tm = _tile(T, 512, 16)
    nstep = T // tm

    def body(chip_ref, x_ref, nw_ref, in_ref, xn_ref, out_ref, proj_ref, wtile_ref, stage_ref,
             send_sems, recv_sems, local_sem, stage_sems):
        i = pl.program_id(0)
        x, y, c = _position()
        me, sibling = (x, y, c), (x, y, 1 - c)
        xn, yn, diag = (1 - x, y), (x, 1 - y), (1 - x, 1 - y)
        upper, lower = pl.ds(0, half), pl.ds(half, half)

        def slot(p, rows=None):
            ref = out_ref.at[4 * p[0] + 2 * p[1] + p[2]]
            return ref if rows is None else ref.at[rows]

        def copy(kk, block, to, rows=None, src=None):
            return pltpu.make_async_remote_copy(
                src_ref=slot(block, rows) if src is None else src, dst_ref=slot(block, rows),
                send_sem=send_sems.at[kk], recv_sem=recv_sems.at[kk], device_id=to, device_id_type=MESH)

        mine = pltpu.make_async_copy(in_ref, slot(me), local_sem)
        first = [copy(0, me, sibling, src=in_ref), copy(1, me, (*xn, c), src=in_ref), copy(2, me, (*yn, c), src=in_ref)]

        @pl.when(i == 0)
        def _():
            mine.start()
            for cp in first:
                cp.start()
            copy(0, sibling, me).wait_recv()
            loads = [pltpu.make_async_copy(in_ref, stage_ref.at[c], stage_sems.at[0]),
                     pltpu.make_async_copy(slot(sibling), stage_ref.at[1 - c], stage_sems.at[1])]
            for cp in loads:
                cp.start()
            for cp in loads:
                cp.wait()
            for m, plan in enumerate(plans):
                @pl.when(chip_ref[0] == m)
                def _(plan=plan):
                    for d, s0, w, c0 in plan:
                        wtile_ref[:, c0:c0 + w] = stage_ref[d % 2, :, s0:s0 + w]

        xv = x_ref[...]
        r = lax.rsqrt(jnp.mean(xv * xv, axis=-1, keepdims=True) + EPS)
        xnv = (xv * r * nw_ref[...]).astype(BF16)
        xn_ref[...] = xnv
        proj_ref[...] = jnp.dot(xnv.astype(MXU), wtile_ref[...].astype(MXU), preferred_element_type=F32)

        @pl.when(i == nstep - 1)
        def _():
            sent = list(first)

            def then(cps):
                for cp in cps:
                    cp.start()
                sent.extend(cps)

            copy(1, (*xn, c), me).wait_recv()
            then([copy(5, (*xn, c), (*yn, c), rows=upper), copy(3, (*xn, c), sibling)])
            copy(2, (*yn, c), me).wait_recv()
            then([copy(6, (*yn, c), (*xn, c), rows=lower), copy(4, (*yn, c), sibling)])
            copy(5, (*diag, c), me, rows=upper).wait_recv()
            then([copy(7, (*diag, c), sibling, rows=upper)])
            copy(6, (*diag, c), me, rows=lower).wait_recv()
            then([copy(8, (*diag, c), sibling, rows=lower)])
            copy(3, (*xn, 1 - c), me).wait_recv()
            copy(4, (*yn, 1 - c), me).wait_recv()
            copy(7, (*diag, 1 - c), me, rows=upper).wait_recv()
            copy(8, (*diag, 1 - c), me, rows=lower).wait_recv()
            for cp in sent:
                cp.wait_send()
            mine.wait()

    any_spec = pl.BlockSpec(memory_space=pl.ANY)
    return pl.pallas_call(
        body, name="all_gather_weights",
        grid_spec=pltpu.PrefetchScalarGridSpec(
            num_scalar_prefetch=1, grid=(nstep,),
            in_specs=[pl.BlockSpec((tm, DM), lambda i, chip_ref: (i, 0)),
                      pl.BlockSpec((1, DM), lambda i, chip_ref: (0, 0)), any_spec],
            out_specs=[pl.BlockSpec((tm, DM), lambda i, chip_ref: (i, 0)), any_spec,
                       pl.BlockSpec((tm, tn), lambda i, chip_ref: (i, 2 * chip_ref[0]))],
            scratch_shapes=[pltpu.VMEM((DM, tn), arr.dtype), pltpu.VMEM((2,) + arr.shape, arr.dtype),
                            pltpu.SemaphoreType.DMA((9,)), pltpu.SemaphoreType.DMA((9,)), pltpu.SemaphoreType.DMA,
                            pltpu.SemaphoreType.DMA((2,))]),
        out_shape=[S((T, DM), BF16), S((N_DEV,) + arr.shape, arr.dtype), S((T, nm), F32)],
        compiler_params=_cp(ARB),
    )(chip, x_in, norm_w, arr)


def _sibling_copies(ins, outs, send_sems, recv_sems):
    x, y, c = _position()
    return [pltpu.make_async_remote_copy(src_ref=ins[a].at[k, 1 - c], dst_ref=outs[a].at[k],
                                         send_sem=send_sems.at[a, k], recv_sem=recv_sems.at[a, k],
                                         device_id=(x, y, 1 - c), device_id_type=MESH)
            for a in range(len(ins)) for k in range(ins[a].shape[0])]


def _sibling_sems(arrs):
    shape = (max(len(arrs), 1), arrs[0].shape[0] if arrs else 1)
    return [pltpu.SemaphoreType.DMA(shape), pltpu.SemaphoreType.DMA(shape)]


def _chip_exchange_copies(ins, outs, send_sems, recv_sems):
    x, y, c = _position()
    chips = [(1 - x, y), (x, 1 - y), (1 - x, 1 - y)]
    return [pltpu.make_async_remote_copy(
        src_ref=ins[a].at[2 * qx + qy], dst_ref=outs[a].at[j], send_sem=send_sems.at[a, j],
        recv_sem=recv_sems.at[a, j], device_id=(qx, qy, c), device_id_type=MESH)
        for a in range(len(ins)) for j, (qx, qy) in enumerate(chips)]


def _broadcast_copies(srcs, dsts, send_sems, recv_sems):
    x, y, c = _position()
    me = 4 * x + 2 * y + c
    cps = []
    for a in range(len(srcs)):
        for k in range(1, N_DEV):
            peer = (1 - x if k & 4 else x, 1 - y if k & 2 else y, 1 - c if k & 1 else c)
            cps.append(pltpu.make_async_remote_copy(
                src_ref=srcs[a], dst_ref=dsts[a].at[me], send_sem=send_sems.at[a, k - 1],
                recv_sem=recv_sems.at[a, k - 1], device_id=peer, device_id_type=MESH))
    return me, cps


def _all_reduce_small(part):
    R, C = part.shape

    def body(p_ref, out_ref, gath_ref, send_sems, recv_sems):
        me, cps = _broadcast_copies([p_ref], [gath_ref], send_sems, recv_sems)
        gath_ref[me] = p_ref[...]
        for cp in cps:
            cp.start()
        for cp in cps:
            cp.wait()
        acc = gath_ref[0]
        for d in range(1, N_DEV):
            acc = acc + gath_ref[d]
        out_ref[...] = acc

    vm = pl.BlockSpec(memory_space=pltpu.VMEM)
    return pl.pallas_call(
        body, name="all_reduce_small", in_specs=[vm], out_specs=vm, out_shape=S((R, C), F32),
        scratch_shapes=[pltpu.VMEM((N_DEV, R, C), F32), pltpu.SemaphoreType.DMA((1, N_DEV - 1)),
                        pltpu.SemaphoreType.DMA((1, N_DEV - 1))],
    )(part)


def _pack(parts):
    rows = []
    for p in parts:
        f = p.reshape(-1).astype(F32)
        pad = (-f.shape[0]) % (8 * LANES)
        rows.append(jnp.pad(f, (0, pad)).reshape(-1, LANES))
    return jnp.concatenate(rows, axis=0)


def _unpack(buf, shapes):
    out, r = [], 0
    for shp in shapes:
        n = 1
        for s in shp:
            n *= s
        nr = -(-n // (8 * LANES)) * 8
        out.append(buf[r:r + nr].reshape(-1)[:n].reshape(shp))
        r += nr
    return out


def kernel(x, norm_w, w_in, conv_w, a_log, dt_bias, head_norm_w, sgu_ln_w, sgu_ln_b, w_spatial, b_spatial, w_out, final_norm_w, loss_target, m_norm_w, m_w_in, m_conv_w, m_a_log, m_dt_bias, m_head_norm_w, m_sgu_ln_w, m_sgu_ln_b, m_w_spatial, m_b_spatial, m_w_out, m_final_norm_w, v_norm_w, v_w_in, v_conv_w, v_a_log, v_dt_bias, v_head_norm_w, v_sgu_ln_w, v_sgu_ln_b, v_w_spatial, v_b_spatial, v_w_out, v_final_norm_w):
    T, DM = x.shape[1], x.shape[2]
    H, D = a_log.shape[1], head_norm_w.shape[1]
    G, P = w_spatial.shape[1], w_spatial.shape[2]
    AW, BW = H * D, G * P
    MIX = AW + BW
    WD = w_in.shape[2]
    IN = N_DEV * WD
    RO = w_out.shape[1]
    CW = conv_w.shape[2]
    sizes = (3 * AW, AW, H, H, BW, BW, BW)
    assert sum(sizes) == IN and 2 * H <= LANES and 3 * H <= 32 and N_DEV * RO == MIX and N_DEV * CW == 3 * AW
    offs = [0]
    for s in sizes:
        offs.append(offs[-1] + s)
    px, py, pc = _position()
    dev = 4 * px + 2 * py + pc
    chip = 2 * px + py

    x2, tgt = x[0], loss_target[0]

    core_idx = jnp.reshape(pc, (1,)).astype(jnp.int32)
    chip_idx = jnp.reshape(chip, (1,)).astype(jnp.int32)
    NM = IN - 2 * H
    tn_loc, tile_plans = _local_tiles(WD, offs[2], offs[4], NM)
    xn, g_win, proj_part = _all_gather_weights(
        _cast_bf16_t(w_in[0].T, "cast_w_in"), x2, norm_w, chip_idx, tn_loc, tile_plans, NM)
    w_main, w_ba = _relayout_w(g_win, offs[2], offs[4])
    alog_row = jnp.pad(a_log, ((0, 0), (H, LANES - 2 * H)))
    dtb_row = jnp.pad(dt_bias, ((0, 0), (H, LANES - 2 * H)))
    bs_t = b_spatial[0].T

    others = jnp.arange(N_DEV - 2, dtype=jnp.int32)
    others = others + (others >= 2 * chip).astype(jnp.int32)
    proj, ba, (g_wout, g_conv) = _in_proj(xn, w_main, w_ba, proj_part, others, tn_loc,
                                          [_cast_bf16(w_out[0], "cast_w_out"), conv_w[0]])
    w_out_full = g_wout.reshape(MIX, DM)
    conv_full = g_conv.transpose(1, 0, 2).reshape(4, 3 * AW)
    q, k, v, c, gcol, grow = _prep_a_fwd(proj, ba, conv_full, alog_row, dtb_row, H, D)
    o, vnew, ssave, asave = _delta_fwd(q, k, v, gcol, grow, H, D)
    ocat = _mix_fwd(o, proj, head_norm_w, sgu_ln_w, sgu_ln_b, w_spatial[0], bs_t, H, D, G, P)
    dh, dh_bf, d_ocat, loss_acc, g_fnw = _out_proj_loss(ocat, w_out_full, x2, tgt, final_norm_w.reshape(1, DM))

    g_wout_blocks = _grad_w(ocat, dh_bf, "grad_w_out").reshape(4, 2, RO, DM)
    (d_o, dmain, g_hnw, g_ln, g_wsp, g_bs_t), (sib_wout,) = _mix_bwd(
        d_ocat, o, proj, head_norm_w, sgu_ln_w, sgu_ln_b, w_spatial[0], bs_t, H, D, G, P, [g_wout_blocks])
    chip_wout = _pair_sum(g_wout_blocks, sib_wout, core_idx, "pair_sum_w_out")
    (dq, dk, dv, dba, dpar), (recv_wout,) = _delta_bwd(
        q, k, v, gcol, grow, ba, vnew, ssave, asave, d_o, alog_row, dtb_row, H, D, [chip_wout])
    dmain, g_conv_part = _prep_a_bwd(dq, dk, dv, c, proj, conv_full, dmain, H, D)
    keep_win, sib_win = _grad_w_in(xn, dmain, dba, WD, offs[2], offs[4])
    chip_win = _pair_sum_plain(keep_win, sib_win, "pair_sum_w_in")
    small_shapes = [a_log.shape, dt_bias.shape, head_norm_w.shape, sgu_ln_w.shape, sgu_ln_b.shape,
                    w_spatial.shape, b_spatial.shape, final_norm_w.shape]
    parts = [dpar[0, H:2 * H], dpar[1, H:2 * H], g_hnw[0], g_ln[0], g_ln[1], g_wsp, g_bs_t[:, :G].T, g_fnw[0],
             g_conv_part[:4], loss_acc[0, :1]]
    grad_x, g_nw, small_gath, recv_win = _dx(dmain, dba, w_main, w_ba, x2, dh, norm_w, chip_win, _pack(parts), 4)
    red = _sum_slots(small_gath, small_shapes + [(4, 3 * AW), (1,)])
    grad_w_in, delta_w_in, new_m_w_in, new_v_w_in = _sum_adam(
        chip_win, recv_win, w_in[0].T, m_w_in[0].T, v_w_in[0].T, chip_idx, "sum_adam_w_in", transposed=True)
    grad_x = _copy_rows(grad_x, "copy_grad_x")
    grad_w_out, delta_w_out, new_m_w_out, new_v_w_out = _sum_adam(
        chip_wout, recv_wout, w_out[0], m_w_out[0], v_w_out[0], chip_idx, "sum_adam_w_out")
    red_nw = _all_reduce_small(_pack([g_nw[0]]))
    grads_small = _unpack(red_nw, [norm_w.shape]) + red
    loss = grads_small.pop()[0]
    g_conv_full = grads_small.pop()
    grad_conv = lax.dynamic_slice_in_dim(g_conv_full, dev * CW, CW, axis=1)[None]
    small_w = [norm_w, a_log, dt_bias, head_norm_w, sgu_ln_w, sgu_ln_b, w_spatial, b_spatial, final_norm_w, conv_w]
    small_m = [m_norm_w, m_a_log, m_dt_bias, m_head_norm_w, m_sgu_ln_w, m_sgu_ln_b, m_w_spatial, m_b_spatial,
               m_final_norm_w, m_conv_w]
    small_v = [v_norm_w, v_a_log, v_dt_bias, v_head_norm_w, v_sgu_ln_w, v_sgu_ln_b, v_w_spatial, v_b_spatial,
               v_final_norm_w, v_conv_w]
    small_g = grads_small + [grad_conv]
    d_s, m_s, v_s = _adam_small(small_w, small_g, small_m, small_v)

    def order(small, win, wout):
        return [small[0], win.T[None], small[9], small[1], small[2], small[3], small[4], small[5], small[6], small[7],
                wout[None], small[8]]

    grads = order(small_g, grad_w_in, grad_w_out)
    deltas = order(d_s, delta_w_in, delta_w_out)
    new_m = order(m_s, new_m_w_in, new_m_w_out)
    new_v = order(v_s, new_v_w_in, new_v_w_out)
    return (loss, grad_x[None], *grads, *deltas, *new_m, *new_v)
```

```python
import jax
import jax.numpy as jnp
from jax import lax
from jax.experimental import pallas as pl
from jax.experimental.pallas import tpu as pltpu

F32 = jnp.float32
BF16 = jnp.bfloat16
MXU = jnp.bfloat16
HI = lax.Precision.HIGHEST
EPS = 1e-6
CHUNK_A = 64
LANES = 128
MESH = pl.DeviceIdType.MESH
N_DEV = 8

ADAM_LR = 0.001
ADAM_B1 = 0.9
ADAM_B2 = 0.999
ADAM_EPS = 1e-08
ADAM_WD = 0.01
ADAM_STEP = 10

S = jax.ShapeDtypeStruct
ARB = "arbitrary"


def _cp(*sem, vmem_mib=56):
    return pltpu.CompilerParams(dimension_semantics=tuple(sem), vmem_limit_bytes=vmem_mib * 1024 * 1024)


def _tile(n, cap, mult):
    best = None
    t = mult
    while t <= min(n, cap):
        if n % t == 0:
            best = t
        t += mult
    return best if best is not None else n


def _mm(a, b):
    return jnp.dot(a.astype(MXU), b.astype(MXU), preferred_element_type=F32)


def _mm_nt(a, b):
    return lax.dot_general(a.astype(MXU), b.astype(MXU), (((1,), (1,)), ((), ())), preferred_element_type=F32)


def _mm_tn(a, b):
    return lax.dot_general(a.astype(MXU), b.astype(MXU), (((0,), (0,)), ((), ())), preferred_element_type=F32)


def _mmh(a, b):
    return jnp.dot(a, b, precision=HI, preferred_element_type=F32)


def _sigmoid(x):
    return 1.0 / (1.0 + jnp.exp(-x))


def _silu(x):
    return x * _sigmoid(x)


def _dsilu(x):
    s = _sigmoid(x)
    return s * (1.0 + x * (1.0 - s))


def _softplus(x):
    return jnp.maximum(x, 0.0) + jnp.log(1.0 + jnp.exp(-jnp.abs(x)))


def _pieces(wd, gate_lo, gate_hi, total):
    out = []
    for d in range(N_DEV):
        lo, hi = d * wd, (d + 1) * wd
        for dest, a, b, shift in (("main", 0, gate_lo, 0), ("gate", gate_lo, gate_hi, -gate_lo),
                                  ("main", gate_hi, total, gate_lo - gate_hi)):
            s0, s1 = max(lo, a), min(hi, b)
            if s0 < s1:
                out.append((d, s0 - lo, s1 - lo, dest, s0 + shift))
    return out


def _local_tiles(wd, gate_lo, gate_hi, nm):
    n_tiles = N_DEV - 1
    assert nm % (n_tiles * LANES) == 0
    tn = nm // n_tiles
    plans = []
    for m in range(N_DEV // 2):
        lo, hi = 2 * m * tn, (2 * m + 1) * tn
        plan = []
        for d, s0, s1, dest, c0 in _pieces(wd, gate_lo, gate_hi, N_DEV * wd):
            if dest != "main":
                continue
            a, b = max(c0, lo), min(c0 + (s1 - s0), hi)
            if a < b:
                assert d // 2 == m, "tile 2m must come from chip m's own shards"
                plan.append((d, s0 + (a - c0), b - a, a - lo))
        assert sum(p[2] for p in plan) == tn
        plans.append(plan)
    return tn, plans


def _cast_bf16(a, name):
    R, C = a.shape
    tr = _tile(R, 256, 16)

    def body(a_ref, o_ref):
        o_ref[...] = a_ref[...].astype(BF16)

    spec = pl.BlockSpec((tr, C), lambda i: (i, 0))
    return pl.pallas_call(body, name=name, grid=(R // tr,), in_specs=[spec], out_specs=spec,
                          out_shape=S((R, C), BF16), compiler_params=_cp(ARB))(a)


def _cast_bf16_t(a_t, name):
    C, R = a_t.shape
    tr = _tile(R, 256, LANES)

    def body(a_ref, o_ref):
        o_ref[...] = a_ref[...].T.astype(BF16)

    return pl.pallas_call(body, name=name, grid=(R // tr,), in_specs=[pl.BlockSpec((C, tr), lambda i: (0, i))],
                          out_specs=pl.BlockSpec((tr, C), lambda i: (i, 0)),
                          out_shape=S((R, C), BF16), compiler_params=_cp(ARB))(a_t)


def _relayout_w(g_win, gate_lo, gate_hi):
    _, DM, WD = g_win.shape
    total = N_DEV * WD
    NM = total - (gate_hi - gate_lo)
    tr = _tile(DM, 256, 16)
    plan = _pieces(WD, gate_lo, gate_hi, total)

    def body(g_ref, main_ref, gate_ref):
        gate_ref[...] = jnp.zeros_like(gate_ref)
        for d, s0, s1, dest, c0 in plan:
            dst = main_ref if dest == "main" else gate_ref
            dst[:, c0:c0 + (s1 - s0)] = g_ref[d, :, s0:s1]

    return pl.pallas_call(
        body, name="relayout_w", grid=(DM // tr,),
        in_specs=[pl.BlockSpec((N_DEV, tr, WD), lambda i: (0, i, 0))],
        out_specs=[pl.BlockSpec((tr, NM), lambda i: (i, 0)), pl.BlockSpec((tr, LANES), lambda i: (i, 0))],
        out_shape=[S((DM, NM), g_win.dtype), S((DM, LANES), g_win.dtype)],
        compiler_params=_cp(ARB),
    )(g_win)


def _in_proj(xn, w_main, w_ba, proj_part, tiles, tn, shards):
    T, DM = xn.shape
    NM = w_main.shape[1]
    tm = _tile(T, 2048, 16)
    ni, nj = T // tm, tiles.shape[0]
    ns = len(shards)

    def body(tiles_ref, xn_ref, w_ref, wba_ref, part_ref, *rest):
        del tiles_ref, part_ref
        srcs = rest[:ns]
        proj_ref, ba_ref = rest[ns:ns + 2]
        gath = rest[ns + 2:2 * ns + 2]
        send_sems, recv_sems, local_sems = rest[2 * ns + 2:]
        i = pl.program_id(0)
        j = pl.program_id(1)
        me, cps = _broadcast_copies(srcs, gath, send_sems, recv_sems)
        cps = cps + [pltpu.make_async_copy(srcs[a], gath[a].at[me], local_sems.at[a]) for a in range(ns)]

        @pl.when((i == 0) & (j == 0))
        def _():
            for cp in cps:
                cp.start()

        @pl.when(j == 0)
        def _():
            ba_ref[...] = jnp.dot(xn_ref[...].astype(MXU), wba_ref[...].astype(MXU), preferred_element_type=F32)

        proj_ref[...] = jnp.dot(xn_ref[...].astype(MXU), w_ref[...].astype(MXU), preferred_element_type=F32)

        @pl.when((i == ni - 1) & (j == nj - 1))
        def _():
            for cp in cps:
                cp.wait()

    any_spec = pl.BlockSpec(memory_space=pl.ANY)
    res = pl.pallas_call(
        body, name="in_proj",
        grid_spec=pltpu.PrefetchScalarGridSpec(
            num_scalar_prefetch=1, grid=(ni, nj),
            in_specs=[pl.BlockSpec((tm, DM), lambda i, j, t: (i, 0)),
                      pl.BlockSpec((DM, tn), lambda i, j, t: (0, t[j])),
                      pl.BlockSpec((DM, LANES), lambda i, j, t: (0, 0)), any_spec] + [any_spec] * ns,
            out_specs=[pl.BlockSpec((tm, tn), lambda i, j, t: (i, t[j])),
                       pl.BlockSpec((tm, LANES), lambda i, j, t: (i, 0))] + [any_spec] * ns,
            scratch_shapes=[pltpu.SemaphoreType.DMA((ns, N_DEV - 1)), pltpu.SemaphoreType.DMA((ns, N_DEV - 1)),
                            pltpu.SemaphoreType.DMA((ns,))]),
        out_shape=[S((T, NM), F32), S((T, LANES), F32)] + [S((N_DEV,) + a.shape, a.dtype) for a in shards],
        input_output_aliases={4: 0},
        compiler_params=_cp(ARB, ARB, vmem_mib=58),
    )(tiles, xn, w_main, w_ba, proj_part, *shards)
    return res[0], res[1], res[2:]


def _prep_a_fwd(proj, ba, conv_w, alog_row, dtb_row, H, D):
    T = proj.shape[0]
    AW = H * D
    C3 = 3 * AW
    tb = _tile(T, 256, CHUNK_A)
    nch = tb // CHUNK_A
    nblk = T // tb
    scale = float(D) ** -0.5

    def body(x_ref, halo_ref, ba_ref, cw_ref, al_ref, dt_ref, q_ref, k_ref, v_ref, c_ref, gcol_ref, grow_ref):
        i = pl.program_id(0)
        xv = x_ref[...]
        halo = halo_ref[...] * (i > 0).astype(F32)
        xp = jnp.concatenate([halo, xv], axis=0)
        cw = cw_ref[...]
        c = cw[0:1, :] * xp[5:5 + tb]
        for j in range(1, 4):
            c = c + cw[j:j + 1, :] * xp[5 + j:5 + j + tb]
        c_ref[...] = c
        a = _silu(c)
        for h in range(H):
            qh = a[:, h * D:(h + 1) * D]
            kh = a[:, AW + h * D:AW + (h + 1) * D]
            qr = lax.rsqrt(jnp.sum(qh * qh, axis=-1, keepdims=True) + EPS)
            kr = lax.rsqrt(jnp.sum(kh * kh, axis=-1, keepdims=True) + EPS)
            q_ref[:, h * D:(h + 1) * D] = qh * (qr * scale)
            k_ref[:, h * D:(h + 1) * D] = kh * kr
        v_ref[...] = a[:, 2 * AW:]

        bav = ba_ref[...]
        lane = lax.broadcasted_iota(jnp.int32, (tb, LANES), 1)
        beta = _sigmoid(bav)
        g = -jnp.exp(al_ref[...]) * _softplus(bav + dt_ref[...])
        gates = jnp.where(lane < H, beta, jnp.where(lane < 2 * H, g, 0.0))
        ri = lax.broadcasted_iota(jnp.int32, (CHUNK_A, CHUNK_A), 0)
        ci = lax.broadcasted_iota(jnp.int32, (CHUNK_A, CHUNK_A), 1)
        tri = (ri >= ci).astype(F32)
        lane_c = lax.broadcasted_iota(jnp.int32, (CHUNK_A, LANES), 1)
        for cc in range(nch):
            gch = gates[cc * CHUNK_A:(cc + 1) * CHUNK_A]
            gc = pltpu.roll(_mmh(tri, gch), H, 1)
            full = jnp.where(lane_c < 2 * H, gch, jnp.where(lane_c < 3 * H, gc, 0.0))
            gcol_ref[cc * CHUNK_A:(cc + 1) * CHUNK_A, :] = full
            grow_ref[cc] = full.T[0:32, :]

    return pl.pallas_call(
        body, name="prep_a_fwd", grid=(nblk,),
        in_specs=[pl.BlockSpec((tb, C3), lambda i: (i, 0)),
                  pl.BlockSpec((8, C3), lambda i: (jnp.maximum(i * (tb // 8) - 1, 0), 0)),
                  pl.BlockSpec((tb, LANES), lambda i: (i, 0)),
                  pl.BlockSpec((4, C3), lambda i: (0, 0)),
                  pl.BlockSpec((1, LANES), lambda i: (0, 0)),
                  pl.BlockSpec((1, LANES), lambda i: (0, 0))],
        out_specs=[pl.BlockSpec((tb, AW), lambda i: (i, 0)),
                   pl.BlockSpec((tb, AW), lambda i: (i, 0)),
                   pl.BlockSpec((tb, AW), lambda i: (i, 0)),
                   pl.BlockSpec((tb, C3), lambda i: (i, 0)),
                   pl.BlockSpec((tb, LANES), lambda i: (i, 0)),
                   pl.BlockSpec((nch, 32, CHUNK_A), lambda i: (i, 0, 0))],
        out_shape=[S((T, AW), F32), S((T, AW), F32), S((T, AW), F32), S((T, C3), F32),
                   S((T, LANES), F32), S((T // CHUNK_A, 32, CHUNK_A), F32)],
        compiler_params=_cp(ARB),
    )(proj, proj, ba, conv_w, alog_row, dtb_row)


_NN = (((1,), (0,)), ((), ()))
_TN = (((0,), (0,)), ((), ()))


def _split(a):
    hi = a.astype(BF16)
    return hi, (a - hi.astype(F32)).astype(BF16)


def _mm3(a, b, dims=_NN):
    ah, al = a if isinstance(a, tuple) else _split(a)
    bh, bl = b if isinstance(b, tuple) else _split(b)
    dg = lambda p, r: lax.dot_general(p, r, dims, preferred_element_type=F32)
    return dg(ah, bh) + (dg(ah, bl) + dg(al, bh))


def _interleave(gens):
    gens = list(gens)
    while gens:
        alive = []
        for g in gens:
            try:
                next(g)
                alive.append(g)
            except StopIteration:
                pass
        gens = alive


def _chunk_terms(q, k, v, gcolv, growv, h, H):
    C = CHUNK_A
    beta_c = gcolv[:, h:h + 1]
    g_c = gcolv[:, H + h:H + h + 1]
    gc_c = gcolv[:, 2 * H + h:2 * H + h + 1]
    gc_r = growv[2 * H + h:2 * H + h + 1, :]
    ri = lax.broadcasted_iota(jnp.int32, (C, C), 0)
    ci = lax.broadcasted_iota(jnp.int32, (C, C), 1)
    incl = ri >= ci
    strict = ri > ci
    kb = k * beta_c
    vb = v * beta_c
    p_raw = _mm_nt(kb, k)
    qk_raw = _mm_nt(q, k)
    gam = jnp.where(incl, jnp.exp(jnp.where(incl, gc_c - gc_r, 0.0)), 0.0)
    e_c = jnp.exp(gc_c)
    gl = gc_r[:, C - 1:C]
    edec = jnp.exp(gl - gc_c)
    yield
    lmat = jnp.where(strict, p_raw * gam, 0.0)
    attn = jnp.where(incl, qk_raw * gam, 0.0)
    return dict(beta_c=beta_c, g_c=g_c, gc_c=gc_c, gc_r=gc_r, incl=incl, strict=strict, gam=gam, e_c=e_c,
                kb=kb, vb=vb, lmat=lmat, attn=attn, gl=gl, edec=edec, ri=ri, ci=ci)


INV_BLOCK = 16


def _inv_unit_lower(lmat):
    C = lmat.shape[0]
    ri = lax.broadcasted_iota(jnp.int32, (C, C), 0)
    ci = lax.broadcasted_iota(jnp.int32, (C, C), 1)
    eye = (ri == ci).astype(F32)
    same = (ri // INV_BLOCK) == (ci // INV_BLOCK)

    def neumann(x, order):
        a = eye + x
        n = 1
        while 2 * n < order:
            xs = _split(x)
            x = _mm3(xs, xs)
            yield
            a = a + _mm3(a, x)
            n *= 2
        yield
        return a

    inv_d = yield from neumann(-jnp.where(same, lmat, 0.0), INV_BLOCK)
    m = _mm3(inv_d, jnp.where(same, 0.0, lmat))
    yield
    inv_m = yield from neumann(-m, C // INV_BLOCK)
    a = _mm3(inv_m, inv_d)
    yield
    return a


def _delta_fwd(q, k, v, gcol, grow, H, D):
    T = q.shape[0]
    C = CHUNK_A
    N = T // C
    AW = H * D
    CPS = 2 if N % 2 == 0 else 1

    def body(q_ref, k_ref, v_ref, gcol_ref, grow_ref, o_ref, vn_ref, ssave_ref, asave_ref, s_ref):
        @pl.when(pl.program_id(0) == 0)
        def _():
            s_ref[...] = jnp.zeros_like(s_ref)

        state = {(0, h): s_ref[h] for h in range(H)}

        def head(cc, h):
            rows = slice(cc * C, (cc + 1) * C)
            sl = slice(h * D, (h + 1) * D)
            qv, kv, vv = q_ref[rows, sl], k_ref[rows, sl], v_ref[rows, sl]
            t = yield from _chunk_terms(qv, kv, vv, gcol_ref[rows, :], grow_ref[cc], h, H)
            a = yield from _inv_unit_lower(t["lmat"])
            asave_ref[cc, h] = a
            while (cc, h) not in state:
                yield
            st = state[(cc, h)]
            ssave_ref[cc, h] = st
            ks = _mm(t["kb"] * t["e_c"], st)
            o_inter = _mm(qv * t["e_c"], st)
            yield
            v_new = _mm3(a, t["vb"] - ks)
            yield
            vn_ref[rows, sl] = v_new
            o_intra = _mm(t["attn"], v_new)
            s_upd = _mm_tn(kv * t["edec"], v_new)
            yield
            o_ref[rows, sl] = o_inter + o_intra
            state[(cc + 1, h)] = st * jnp.exp(t["gl"]) + s_upd

        _interleave(head(cc, h) for cc in range(CPS) for h in range(H))
        for h in range(H):
            s_ref[h] = state[(CPS, h)]

    blk = lambda: pl.BlockSpec((CPS * C, AW), lambda n: (n, 0))
    return pl.pallas_call(
        body, name="delta_fwd", grid=(N // CPS,),
        in_specs=[blk(), blk(), blk(),
                  pl.BlockSpec((CPS * C, LANES), lambda n: (n, 0)),
                  pl.BlockSpec((CPS, 32, C), lambda n: (n, 0, 0))],
        out_specs=[blk(), blk(),
                   pl.BlockSpec((CPS, H, D, D), lambda n: (n, 0, 0, 0)),
                   pl.BlockSpec((CPS, H, C, C), lambda n: (n, 0, 0, 0))],
        out_shape=[S((T, AW), F32), S((T, AW), F32), S((N, H, D, D), F32), S((N, H, C, C), F32)],
        scratch_shapes=[pltpu.VMEM((H, D, D), F32)],
        compiler_params=_cp(ARB),
    )(q, k, v, gcol, grow)


def _delta_bwd(q, k, v, gcol, grow, ba, vnew, ssave, asave, d_o, a_log, dt_bias, H, D, carry):
    T = q.shape[0]
    C = CHUNK_A
    N = T // C
    AW = H * D
    nc = len(carry)
    CPS = 2 if N % 2 == 0 else 1
    NS = N // CPS

    def body(al_ref, dt_ref, q_ref, k_ref, v_ref, gcol_ref, grow_ref, ba_ref, vn_ref, ss_ref, as_ref, do_ref, *rest):
        cins = rest[:nc]
        dq_ref, dk_ref, dv_ref, dgate_ref, dpar_ref = rest[nc:nc + 5]
        couts = rest[nc + 5:2 * nc + 5]
        ds_ref, csend, crecv = rest[2 * nc + 5:]
        ccps = _chip_exchange_copies(cins, couts, csend, crecv)

        @pl.when(pl.program_id(0) == 0)
        def _():
            ds_ref[...] = jnp.zeros_like(ds_ref)
            dpar_ref[...] = jnp.zeros_like(dpar_ref)
            for cp in ccps:
                cp.start()

        lane = lax.broadcasted_iota(jnp.int32, (C, LANES), 1)
        rowi = lax.broadcasted_iota(jnp.int32, (C, 1), 0)
        acc = {cc: jnp.zeros((C, LANES), F32) for cc in range(CPS)}
        state = {(0, h): ds_ref[h] for h in range(H)}

        def head(oi, h):
            cc = CPS - 1 - oi
            rows = slice(cc * C, (cc + 1) * C)
            sl = slice(h * D, (h + 1) * D)
            st = ss_ref[cc, h]
            a = as_ref[cc, h]
            qv, kv, vv, dov, v_new = q_ref[rows, sl], k_ref[rows, sl], v_ref[rows, sl], do_ref[rows, sl], vn_ref[rows, sl]
            t = yield from _chunk_terms(qv, kv, vv, gcol_ref[rows, :], grow_ref[cc], h, H)
            beta_c, e_c, gam, kb = t["beta_c"], t["e_c"], t["gam"], t["kb"]
            incl, strict, attn, lmat, edec = t["incl"], t["strict"], t["attn"], t["lmat"], t["edec"]
            kdec = kv * edec
            egl = jnp.exp(t["gl"])
            qe = qv * e_c
            ekb = kb * e_c

            t1 = _mm_nt(dov, st)
            ds_o = _mm_tn(qe, dov)
            dattn_raw = _mm_nt(dov, v_new)
            dv_new_o = _mm_tn(attn, dov)
            yield
            while (oi, h) not in state:
                yield
            ds_next = state[(oi, h)]
            dkdec = _mm_nt(v_new, ds_next)
            dv_new_s = _mm(kdec, ds_next)
            yield
            dgl = egl * jnp.sum(jnp.sum(st * ds_next, axis=1, keepdims=True), axis=0, keepdims=True)
            dk = edec * dkdec
            r = jnp.sum(dkdec * kdec, axis=1, keepdims=True)
            dgc = -r
            dgl = dgl + jnp.sum(r, axis=0, keepdims=True)
            dq = e_c * t1
            dgc = dgc + jnp.sum(t1 * qe, axis=1, keepdims=True)
            dattn = jnp.where(incl, dattn_raw, 0.0)
            dv_new = dv_new_s + dv_new_o
            dqm = dattn * gam
            z = dattn * attn
            dvb = _mm3(a, dv_new, _TN)
            dq_a = _mm(dqm, kv)
            dk_a = _mm_tn(dqm, qv)
            yield
            dq_ref[rows, sl] = dq + dq_a
            dv_ref[rows, sl] = beta_c * dvb
            ds_kb = _mm_tn(ekb, dvb)
            dekb_neg = _mm_nt(dvb, st)
            dl_neg = _mm_nt(dvb, v_new)
            yield
            state[(oi + 1, h)] = egl * ds_next + ds_o - ds_kb
            dekb = -dekb_neg
            dl = jnp.where(strict, -dl_neg, 0.0)
            dp = dl * gam
            z = z + dl * lmat
            dkb_p = _mm(dp, kv)
            dk_p = _mm_tn(dp, kb)
            dgc = dgc + jnp.sum(dekb * ekb, axis=1, keepdims=True)
            dgc = dgc + jnp.sum(z, axis=1, keepdims=True) - jnp.sum(z.T, axis=1, keepdims=True)
            dgc = dgc + jnp.where(rowi == C - 1, dgl, 0.0)
            yield
            dkb = dkb_p + e_c * dekb
            dk_ref[rows, sl] = dk + dk_a + dk_p + beta_c * dkb
            dbeta = jnp.sum(dkb * kv, axis=1, keepdims=True) + jnp.sum(dvb * vv, axis=1, keepdims=True)
            acc[cc] = acc[cc] + jnp.where(lane == h, dbeta, 0.0) + jnp.where(lane == H + h, dgc, 0.0)

        _interleave(head(oi, h) for oi in range(CPS) for h in range(H))
        for h in range(H):
            ds_ref[h] = state[(CPS, h)]
        ri = lax.broadcasted_iota(jnp.int32, (C, C), 0)
        ci = lax.broadcasted_iota(jnp.int32, (C, C), 1)
        upper = (ri <= ci).astype(F32)
        dal = jnp.zeros((1, LANES), F32)
        ddt = jnp.zeros((1, LANES), F32)
        for cc in range(CPS):
            rows = slice(cc * C, (cc + 1) * C)
            gates = gcol_ref[rows, :]
            dg_all = _mm3(upper, acc[cc])
            d_braw = acc[cc] * gates * (1.0 - gates)
            d_araw = dg_all * (-jnp.exp(al_ref[...])) * _sigmoid(ba_ref[rows, :] + dt_ref[...])
            dgate_ref[rows, :] = jnp.where(lane < H, d_braw, jnp.where(lane < 2 * H, d_araw, 0.0)).astype(BF16)
            dal = dal + jnp.sum(dg_all * gates, axis=0, keepdims=True)
            ddt = ddt + jnp.sum(d_araw, axis=0, keepdims=True)
        dpar_ref[0:1, :] += dal
        dpar_ref[1:2, :] += ddt

        @pl.when(pl.program_id(0) == NS - 1)
        def _():
            for cp in ccps:
                cp.wait()

    rev = lambda s: NS - 1 - s
    blk = lambda: pl.BlockSpec((CPS * C, AW), lambda s: (rev(s), 0))
    row = pl.BlockSpec((1, LANES), lambda s: (0, 0))
    any_spec = pl.BlockSpec(memory_space=pl.ANY)
    res = pl.pallas_call(
        body, name="delta_bwd", grid=(NS,),
        in_specs=[row, row, blk(), blk(), blk(),
                  pl.BlockSpec((CPS * C, LANES), lambda s: (rev(s), 0)),
                  pl.BlockSpec((CPS, 32, C), lambda s: (rev(s), 0, 0)),
                  pl.BlockSpec((CPS * C, LANES), lambda s: (rev(s), 0)),
                  blk(),
                  pl.BlockSpec((CPS, H, D, D), lambda s: (rev(s), 0, 0, 0)),
                  pl.BlockSpec((CPS, H, C, C), lambda s: (rev(s), 0, 0, 0)),
                  blk()] + [any_spec] * nc,
        out_specs=[blk(), blk(), blk(),
                   pl.BlockSpec((CPS * C, LANES), lambda s: (rev(s), 0)),
                   pl.BlockSpec((8, LANES), lambda s: (0, 0))] + [any_spec] * nc,
        out_shape=[S((T, AW), F32), S((T, AW), F32), S((T, AW), F32),
                   S((T, LANES), BF16), S((8, LANES), F32)] + [S((3,) + a.shape[1:], a.dtype) for a in carry],
        scratch_shapes=[pltpu.VMEM((H, D, D), F32),
                        pltpu.SemaphoreType.DMA((max(nc, 1), 3)), pltpu.SemaphoreType.DMA((max(nc, 1), 3))],
        compiler_params=_cp(ARB),
    )(a_log, dt_bias, q, k, v, gcol, grow, ba, vnew, ssave, asave, d_o, *carry)
    return res[:5], res[5:]


def _ln_stats(xv):
    mu = jnp.mean(xv, axis=-1, keepdims=True)
    xc = xv - mu
    var = jnp.mean(xc * xc, axis=-1, keepdims=True)
    rstd = lax.rsqrt(var + EPS)
    return xc * rstd, rstd


def _mix_fwd(o, proj, head_norm_w, ln_w, ln_b, w_sp, bs_t, H, D, G, P):
    T = o.shape[0]
    AW, BW = H * D, G * P
    MIX = AW + BW
    nb = AW // BW if AW % BW == 0 else None
    assert nb == 1, "group widths must match the projection column blocks"
    cb = 3

    def body(o_ref, za_ref, ub_ref, vb_ref, zb_ref, hw_ref, lw_ref, lb_ref, w_ref, bs_ref, out_ref):
        hw = hw_ref[...]
        for h in range(H):
            sl = slice(h * D, (h + 1) * D)
            oh = o_ref[:, sl]
            rs = lax.rsqrt(jnp.mean(oh * oh, axis=-1, keepdims=True) + EPS)
            out_ref[:, sl] = (oh * rs * hw * _silu(za_ref[:, sl])).astype(BF16)
        xhat, _ = _ln_stats(vb_ref[...])
        vn = xhat * lw_ref[...] + lb_ref[...]
        ri = lax.broadcasted_iota(jnp.int32, (P, P), 0)
        ci = lax.broadcasted_iota(jnp.int32, (P, P), 1)
        bsv = bs_ref[...]
        for g in range(G):
            sl = slice(g * P, (g + 1) * P)
            wm = jnp.where(ri >= ci, w_ref[g], 0.0)
            s = _mm(wm, vn[:, sl]) + bsv[:, g:g + 1]
            out_ref[:, AW + g * P:AW + (g + 1) * P] = (ub_ref[:, sl] * s * _silu(zb_ref[:, sl])).astype(BF16)

    row = lambda w: pl.BlockSpec((1, w), lambda i: (0, 0))
    return pl.pallas_call(
        body, name="mix_fwd", grid=(T // P,),
        in_specs=[pl.BlockSpec((P, AW), lambda i: (i, 0)),
                  pl.BlockSpec((P, AW), lambda i: (i, cb)),
                  pl.BlockSpec((P, BW), lambda i: (i, cb + 1)),
                  pl.BlockSpec((P, BW), lambda i: (i, cb + 2)),
                  pl.BlockSpec((P, BW), lambda i: (i, cb + 3)),
                  row(D), row(BW), row(BW),
                  pl.BlockSpec((G, P, P), lambda i: (0, 0, 0)),
                  pl.BlockSpec((P, G), lambda i: (0, 0))],
        out_specs=pl.BlockSpec((P, MIX), lambda i: (i, 0)),
        out_shape=S((T, MIX), BF16),
        compiler_params=_cp(ARB),
    )(o, proj, proj, proj, proj, head_norm_w, ln_w, ln_b, w_sp, bs_t)


def _mix_bwd(d_ocat, o, proj, head_norm_w, ln_w, ln_b, w_sp, bs_t, H, D, G, P, carry):
    T = o.shape[0]
    AW, BW = H * D, G * P
    MIX = AW + BW
    cb = 3
    nc = len(carry)

    def body(dc_ref, o_ref, za_ref, ub_ref, vb_ref, zb_ref, hw_ref, lw_ref, lb_ref, w_ref, bs_ref, *rest):
        cins = rest[:nc]
        do_ref, dmain_ref, dhw_ref, dln_ref, dw_ref, dbs_ref = rest[nc:nc + 6]
        couts = rest[nc + 6:2 * nc + 6]
        dvn_ref, drest_ref, out_sems, csend, crecv = rest[2 * nc + 6:]
        i = pl.program_id(0)
        slot = lax.rem(i, 2)
        ccps = _sibling_copies(cins, couts, csend, crecv)

        def out_copy(step, s):
            return pltpu.make_async_copy(
                drest_ref.at[s], dmain_ref.at[pl.ds(step * P, P), pl.ds(cb * AW, AW + 3 * BW)], out_sems.at[s])

        @pl.when(i == 0)
        def _():
            dhw_ref[...] = jnp.zeros_like(dhw_ref)
            dln_ref[...] = jnp.zeros_like(dln_ref)
            dw_ref[...] = jnp.zeros_like(dw_ref)
            dbs_ref[...] = jnp.zeros_like(dbs_ref)
            for cp in ccps:
                cp.start()

        @pl.when(i >= 2)
        def _():
            out_copy(i - 2, slot).wait()

        hw = hw_ref[...]
        dhw = jnp.zeros((1, D), F32)
        for h in range(H):
            sl = slice(h * D, (h + 1) * D)
            oh = o_ref[:, sl]
            za = za_ref[:, sl]
            doa = dc_ref[:, sl]
            rs = lax.rsqrt(jnp.mean(oh * oh, axis=-1, keepdims=True) + EPS)
            xh = oh * rs
            d_on = doa * _silu(za)
            drest_ref[slot, :, sl] = (doa * (xh * hw) * _dsilu(za)).astype(BF16)
            dhw = dhw + jnp.sum(d_on * xh, axis=0, keepdims=True)
            dxh = d_on * hw
            do_ref[:, sl] = rs * (dxh - xh * jnp.mean(dxh * xh, axis=-1, keepdims=True))
        dhw_ref[0:1, :] += dhw

        xhat, rstd = _ln_stats(vb_ref[...])
        lw = lw_ref[...]
        vn = xhat * lw + lb_ref[...]
        ri = lax.broadcasted_iota(jnp.int32, (P, P), 0)
        ci = lax.broadcasted_iota(jnp.int32, (P, P), 1)
        lane = lax.broadcasted_iota(jnp.int32, (P, LANES), 1)
        bsv = bs_ref[...]
        dbs = jnp.zeros((P, LANES), F32)
        for g in range(G):
            sl = slice(g * P, (g + 1) * P)
            wm = jnp.where(ri >= ci, w_ref[g], 0.0)
            vng = vn[:, sl]
            s = _mm(wm, vng) + bsv[:, g:g + 1]
            dob = dc_ref[:, AW + g * P:AW + (g + 1) * P]
            ub = ub_ref[:, sl]
            zb = zb_ref[:, sl]
            szb = _silu(zb)
            drest_ref[slot, :, AW + g * P:AW + (g + 1) * P] = (dob * s * szb).astype(BF16)
            drest_ref[slot, :, AW + 2 * BW + g * P:AW + 2 * BW + (g + 1) * P] = (
                dob * ub * s * _dsilu(zb)).astype(BF16)
            ds = dob * ub * szb
            dvn_ref[:, sl] = _mm_tn(wm, ds)
            dw_ref[g] += jnp.where(ri >= ci, _mm_nt(ds, vng), 0.0)
            dbs = dbs + jnp.where(lane == g, jnp.sum(ds, axis=1, keepdims=True), 0.0)
        dbs_ref[...] += dbs
        dvn = dvn_ref[...]
        dln_ref[0:1, :] += jnp.sum(dvn * xhat, axis=0, keepdims=True)
        dln_ref[1:2, :] += jnp.sum(dvn, axis=0, keepdims=True)
        dxh = dvn * lw
        dvb = rstd * (dxh - jnp.mean(dxh, axis=-1, keepdims=True) - xhat * jnp.mean(dxh * xhat, axis=-1, keepdims=True))
        drest_ref[slot, :, AW + BW:AW + 2 * BW] = dvb.astype(BF16)

        out_copy(i, slot).start()

        @pl.when(i == nstep - 1)
        def _():
            out_copy(i, slot).wait()
            if nstep > 1:
                out_copy(i - 1, 1 - slot).wait()
            for cp in ccps:
                cp.wait()

    nstep = T // P
    row = lambda w: pl.BlockSpec((1, w), lambda i: (0, 0))
    any_spec = pl.BlockSpec(memory_space=pl.ANY)
    res = pl.pallas_call(
        body, name="mix_bwd", grid=(nstep,),
        in_specs=[pl.BlockSpec((P, MIX), lambda i: (i, 0)),
                  pl.BlockSpec((P, AW), lambda i: (i, 0)),
                  pl.BlockSpec((P, AW), lambda i: (i, cb)),
                  pl.BlockSpec((P, BW), lambda i: (i, cb + 1)),
                  pl.BlockSpec((P, BW), lambda i: (i, cb + 2)),
                  pl.BlockSpec((P, BW), lambda i: (i, cb + 3)),
                  row(D), row(BW), row(BW),
                  pl.BlockSpec((G, P, P), lambda i: (0, 0, 0)),
                  pl.BlockSpec((P, G), lambda i: (0, 0))] + [any_spec] * nc,
        out_specs=[pl.BlockSpec((P, AW), lambda i: (i, 0)),
                   any_spec,
                   pl.BlockSpec((8, D), lambda i: (0, 0)),
                   pl.BlockSpec((8, BW), lambda i: (0, 0)),
                   pl.BlockSpec((G, P, P), lambda i: (0, 0, 0)),
                   pl.BlockSpec((P, LANES), lambda i: (0, 0))] + [any_spec] * nc,
        out_shape=[S((T, AW), F32), S((T, cb * AW + AW + 3 * BW), BF16), S((8, D), F32), S((8, BW), F32),
                   S((G, P, P), F32), S((P, LANES), F32)] + [S(a.shape[:1] + a.shape[2:], a.dtype) for a in carry],
        scratch_shapes=[pltpu.VMEM((P, BW), F32), pltpu.VMEM((2, P, AW + 3 * BW), BF16),
                        pltpu.SemaphoreType.DMA((2,))] + _sibling_sems(carry),
        compiler_params=_cp(ARB),
    )(d_ocat, o, proj, proj, proj, proj, head_norm_w, ln_w, ln_b, w_sp, bs_t, *carry)
    return res[:6], res[6:]


def _out_proj_loss(ocat, w_out, x, target, fnw):
    T, MIX = ocat.shape
    DM = x.shape[1]
    tm = _tile(T, 256, 8)

    def body(oc_ref, w_ref, x_ref, t_ref, fw_ref, dh_ref, dhb_ref, doc_ref, loss_ref, gfw_ref):
        @pl.when(pl.program_id(0) == 0)
        def _():
            loss_ref[...] = jnp.zeros_like(loss_ref)
            gfw_ref[...] = jnp.zeros_like(gfw_ref)

        wv = w_ref[...]
        hh = x_ref[...] + jnp.dot(oc_ref[...].astype(MXU), wv.astype(MXU), preferred_element_type=F32)
        rs = lax.rsqrt(jnp.mean(hh * hh, axis=-1, keepdims=True) + EPS)
        hn = hh * rs
        fw = fw_ref[...]
        e = hn * fw - t_ref[...]
        row_loss = 0.5 * jnp.mean(e * e, axis=-1, keepdims=True)
        loss_ref[...] += jnp.sum(row_loss, axis=0, keepdims=True)
        dy = e * (1.0 / DM)
        gfw_ref[0:1, :] += jnp.sum(dy * hn, axis=0, keepdims=True)
        dhn = dy * fw
        dh = rs * (dhn - hn * jnp.mean(dhn * hn, axis=-1, keepdims=True))
        dh_ref[...] = dh
        dhb = dh.astype(BF16)
        dhb_ref[...] = dhb
        doc_ref[...] = _mm_nt(dhb, wv)

    return pl.pallas_call(
        body, name="out_proj_loss", grid=(T // tm,),
        in_specs=[pl.BlockSpec((tm, MIX), lambda i: (i, 0)),
                  pl.BlockSpec((MIX, DM), lambda i: (0, 0)),
                  pl.BlockSpec((tm, DM), lambda i: (i, 0)),
                  pl.BlockSpec((tm, DM), lambda i: (i, 0)),
                  pl.BlockSpec((1, DM), lambda i: (0, 0))],
        out_specs=[pl.BlockSpec((tm, DM), lambda i: (i, 0)),
                   pl.BlockSpec((tm, DM), lambda i: (i, 0)),
                   pl.BlockSpec((tm, MIX), lambda i: (i, 0)),
                   pl.BlockSpec((8, LANES), lambda i: (0, 0)),
                   pl.BlockSpec((8, DM), lambda i: (0, 0))],
        out_shape=[S((T, DM), F32), S((T, DM), BF16), S((T, MIX), F32), S((8, LANES), F32), S((8, DM), F32)],
        compiler_params=_cp(ARB),
    )(ocat, w_out, x, target, fnw)


def _grad_w(lhs, rhs, name):
    T, A = lhs.shape
    B = rhs.shape[1]
    ta = _tile(A, 512, LANES)
    tk = _tile(T, 1024, 16)
    nk = T // tk

    def body(l_ref, r_ref, out_ref, acc_ref):
        k = pl.program_id(1)
        part = _mm_tn(l_ref[...], r_ref[...])

        @pl.when(k == 0)
        def _():
            acc_ref[...] = part

        @pl.when(k > 0)
        def _():
            acc_ref[...] += part

        @pl.when(k == nk - 1)
        def _():
            out_ref[...] = acc_ref[...].astype(BF16)

    return pl.pallas_call(
        body, name=name, grid=(A // ta, nk),
        in_specs=[pl.BlockSpec((tk, ta), lambda i, k: (k, i)),
                  pl.BlockSpec((tk, B), lambda i, k: (k, 0))],
        out_specs=pl.BlockSpec((ta, B), lambda i, k: (i, 0)),
        out_shape=S((A, B), BF16),
        scratch_shapes=[pltpu.VMEM((ta, B), F32)],
        compiler_params=_cp(ARB, ARB),
    )(lhs, rhs)


def _grad_w_in(xn, dmain, dba, WD, gate_lo, gate_hi):
    T, DM = xn.shape
    NM = dmain.shape[1]
    tn = _tile(NM, 1024, LANES)
    tk = _tile(T, 2048, 16)
    nj, nk = NM // tn, T // tk
    ND = N_DEV
    tiles = [[] for _ in range(nj)]
    first_tile, last_tile = {}, {}
    for d, s0, s1, dest, c0 in _pieces(WD, gate_lo, gate_hi, ND * WD):
        if dest != "main":
            continue
        while s0 < s1:
            jj = c0 // tn
            w = min(s1 - s0, (jj + 1) * tn - c0)
            tiles[jj].append((d, s0, w, "main", c0 - jj * tn))
            first_tile.setdefault(d, jj)
            last_tile[d] = jj
            s0, c0 = s0 + w, c0 + w
    for d, s0, s1, dest, c0 in _pieces(WD, gate_lo, gate_hi, ND * WD):
        if dest == "gate":
            tiles[first_tile[d]].append((d, s0, s1 - s0, "gate", c0))
    assert sorted(first_tile) == list(range(ND)) and all(last_tile[d] <= first_tile[d + 2] for d in range(ND - 2))

    def body(xn_ref, dm_ref, dba_ref, keep_ref, recv_ref, acc_ref, gate_ref, buf_ref, lsem, ssem, rsem):
        j = pl.program_id(0)
        k = pl.program_id(1)
        px, py, pc = _position()

        @pl.when(k == 0)
        def _():
            acc_ref[...] = jnp.zeros_like(acc_ref)

        @pl.when((j == 0) & (k == 0))
        def _():
            gate_ref[...] = jnp.zeros_like(gate_ref)

        xv = xn_ref[...]
        acc_ref[...] += _mm_tn(xv, dm_ref[...])

        @pl.when(j == 0)
        def _():
            gate_ref[...] += _mm_tn(xv, dba_ref[...])

        def local(d):
            return pltpu.make_async_copy(buf_ref.at[d % 2], keep_ref.at[d // 2], lsem.at[d // 2])

        def remote(d):
            return pltpu.make_async_remote_copy(
                src_ref=buf_ref.at[d % 2], dst_ref=recv_ref.at[d // 2], send_sem=ssem.at[d // 2],
                recv_sem=rsem.at[d // 2], device_id=(px, py, 1 - pc), device_id_type=MESH)

        def leave(d, start):
            @pl.when(pc == d % 2)
            def _():
                local(d).start() if start else local(d).wait()

            @pl.when(pc != d % 2)
            def _():
                remote(d).start() if start else remote(d).wait_send()

        def emit(jj):
            shards = sorted({p[0] for p in tiles[jj]})
            for d in shards:
                if first_tile[d] == jj and d >= 2:
                    leave(d - 2, False)
                for dd, s0, w, src, c0 in tiles[jj]:
                    if dd == d:
                        ref = acc_ref if src == "main" else gate_ref
                        buf_ref[d % 2, :, s0:s0 + w] = ref[:, c0:c0 + w].astype(BF16)
                if last_tile[d] == jj:
                    leave(d, True)
            if jj == nj - 1:
                for d in (ND - 2, ND - 1):
                    leave(d, False)
                for q in range(ND // 2):
                    remote(2 * q).wait_recv()

        for jj in range(nj):
            @pl.when((j == jj) & (k == nk - 1))
            def _(jj=jj):
                emit(jj)

    any_spec = pl.BlockSpec(memory_space=pl.ANY)
    return pl.pallas_call(
        body, name="grad_w_in", grid=(nj, nk),
        in_specs=[pl.BlockSpec((tk, DM), lambda j, k: (k, 0)),
                  pl.BlockSpec((tk, tn), lambda j, k: (k, j)),
                  pl.BlockSpec((tk, LANES), lambda j, k: (k, 0))],
        out_specs=[any_spec, any_spec],
        out_shape=[S((ND // 2, DM, WD), BF16), S((ND // 2, DM, WD), BF16)],
        scratch_shapes=[pltpu.VMEM((DM, tn), F32), pltpu.VMEM((DM, LANES), F32), pltpu.VMEM((2, DM, WD), BF16),
                        pltpu.SemaphoreType.DMA((ND // 2,)), pltpu.SemaphoreType.DMA((ND // 2,)),
                        pltpu.SemaphoreType.DMA((ND // 2,))],
        compiler_params=_cp(ARB, ARB),
    )(xn, dmain, dba)


def _pair_sum_plain(a, b, name):
    K, R, C = a.shape
    tr = _tile(R, 1024, 16)

    def body(a_ref, b_ref, o_ref):
        o_ref[...] = (a_ref[...].astype(F32) + b_ref[...].astype(F32)).astype(BF16)

    spec = lambda: pl.BlockSpec((1, tr, C), lambda q, i: (q, i, 0))
    return pl.pallas_call(body, name=name, grid=(K, R // tr), in_specs=[spec(), spec()], out_specs=spec(),
                          out_shape=S((K, R, C), BF16), compiler_params=_cp(ARB, ARB))(a, b)


def _dx_rows(T):
    tm = _tile(T, 512, 8)
    return tm if T // tm >= 2 else T // 2


def _dx_part(name, dmain, dba, w_main, w_ba, x, dh, norm_w, blk0, nblk, prev, hbm_in, hbm_alias, hbm_new, make_copies):
    T, NM = dmain.shape
    DM = x.shape[1]
    tm = _dx_rows(T)
    tk = _tile(NM, 1024, LANES)
    nk = NM // tk
    n_in, n_al, n_new = len(hbm_in), len(hbm_alias), len(hbm_new)
    n_prev = 0 if prev is None else 2
    last_step = nblk * nk - 1

    def body(dm_ref, dba_ref, w_ref, wba_ref, x_ref, dh_ref, nw_ref, *rest):
        r = list(rest)
        gnw_prev_ref = r.pop(0) if n_prev else None
        if n_prev:
            r.pop(0)
        in_refs = [r.pop(0) for _ in range(n_in)]
        del r[:n_al]
        gx_ref, gnw_ref = r.pop(0), r.pop(0)
        alias_refs = [r.pop(0) for _ in range(n_al)]
        new_refs = [r.pop(0) for _ in range(n_new)]
        acc_ref, send_sems, recv_sems = r
        i = pl.program_id(0)
        k = pl.program_id(1)
        step = i * nk + k
        cps = make_copies(in_refs, alias_refs, new_refs, send_sems, recv_sems)

        @pl.when(step == 0)
        def _():
            gnw_ref[...] = gnw_prev_ref[...] if n_prev else jnp.zeros_like(gnw_ref)
            for cp in cps:
                cp.start()

        @pl.when(k == 0)
        def _():
            acc_ref[...] = _mm_nt(dba_ref[...], wba_ref[...])

        acc_ref[...] += _mm_nt(dm_ref[...], w_ref[...])

        @pl.when(k == nk - 1)
        def _():
            xv = x_ref[...]
            rs = lax.rsqrt(jnp.mean(xv * xv, axis=-1, keepdims=True) + EPS)
            xh = xv * rs
            dxn = acc_ref[...]
            gnw_ref[0:1, :] += jnp.sum(dxn * xh, axis=0, keepdims=True)
            dxh = dxn * nw_ref[...]
            gx_ref[...] = dh_ref[...] + rs * (dxh - xh * jnp.mean(dxh * xh, axis=-1, keepdims=True))

        @pl.when(step == last_step)
        def _():
            for cp in cps:
                cp.wait()

    any_spec = pl.BlockSpec(memory_space=pl.ANY)
    prev_specs = [pl.BlockSpec((8, DM), lambda i, k: (0, 0)), any_spec] if n_prev else []
    prev_args = [prev[1], prev[0]] if n_prev else []
    aliases = {8: 0} if n_prev else {}
    for q in range(n_al):
        aliases[7 + n_prev + n_in + q] = 2 + q
    res = pl.pallas_call(
        body, name=name, grid=(nblk, nk),
        in_specs=[pl.BlockSpec((tm, tk), lambda i, k: (blk0 + i, k)),
                  pl.BlockSpec((tm, LANES), lambda i, k: (blk0 + i, 0)),
                  pl.BlockSpec((DM, tk), lambda i, k: (0, k)),
                  pl.BlockSpec((DM, LANES), lambda i, k: (0, 0)),
                  pl.BlockSpec((tm, DM), lambda i, k: (blk0 + i, 0)),
                  pl.BlockSpec((tm, DM), lambda i, k: (blk0 + i, 0)),
                  pl.BlockSpec((1, DM), lambda i, k: (0, 0))] + prev_specs + [any_spec] * (n_in + n_al),
        out_specs=[pl.BlockSpec((tm, DM), lambda i, k: (blk0 + i, 0)),
                   pl.BlockSpec((8, DM), lambda i, k: (0, 0))] + [any_spec] * (n_al + n_new),
        out_shape=[S((T, DM), F32), S((8, DM), F32)] + [S(a.shape, a.dtype) for a in hbm_alias] + list(hbm_new),
        scratch_shapes=[pltpu.VMEM((tm, DM), F32), pltpu.SemaphoreType.DMA((10,)), pltpu.SemaphoreType.DMA((10,))],
        input_output_aliases=aliases,
        compiler_params=_cp(ARB, ARB),
    )(dmain, dba, w_main, w_ba, x, dh, norm_w, *prev_args, *hbm_in, *hbm_alias)
    return (res[0], res[1]), res[2:2 + n_al], res[2 + n_al:]


def _remote(kk, src, dst, to, send_sems, recv_sems):
    return pltpu.make_async_remote_copy(src_ref=src, dst_ref=dst, send_sem=send_sems.at[kk], recv_sem=recv_sems.at[kk],
                                        device_id=to, device_id_type=MESH)


def _dx(dmain, dba, w_main, w_ba, x, dh, norm_w, chip_sum, small, cut):
    R, C = chip_sum.shape[1:]
    half = R // 2
    assert half % 16 == 0
    T = x.shape[0]
    ni = T // _dx_rows(T)
    cut = max(1, min(cut, ni - 1))
    upper, lower = pl.ds(0, half), pl.ds(half, half)

    def nbrs():
        px, py, pc = _position()
        return (px, py), (1 - px, py, pc), (px, 1 - py, pc)

    def phase1(ins, als, news, ss, rs):
        (px, py), xn, yn = nbrs()
        cs = ins[0]
        recv, stage = news
        bx, by, bd = cs.at[2 * (1 - px) + py], cs.at[2 * px + (1 - py)], cs.at[2 * (1 - px) + (1 - py)]
        return [_remote(0, bx.at[upper], recv.at[0].at[upper], xn, ss, rs),
                _remote(1, by.at[lower], recv.at[1].at[lower], yn, ss, rs),
                _remote(2, bd.at[upper], stage.at[0], xn, ss, rs),
                _remote(3, bd.at[lower], stage.at[1], yn, ss, rs)]

    def phase2(ins, als, news, ss, rs):
        (px, py), xn, yn = nbrs()
        comb, small_ref = ins
        recv, gath = als[0], news[0]
        me, small_cps = _broadcast_copies([small_ref], [gath], _Sem2(ss, 2), _Sem2(rs, 2))
        return ([_remote(0, comb.at[0], recv.at[1].at[upper], yn, ss, rs),
                 _remote(1, comb.at[1], recv.at[0].at[lower], xn, ss, rs)] + small_cps
                + [pltpu.make_async_copy(small_ref, gath.at[me], ss.at[9])])

    (gx, gnw), _, (recv, stage) = _dx_part(
        "dx_a", dmain, dba, w_main, w_ba, x, dh, norm_w, 0, cut, None, [chip_sum], [],
        [S((2, R, C), chip_sum.dtype), S((2, half, C), chip_sum.dtype)], phase1)
    comb = _relay_add(chip_sum, stage)
    (gx, gnw), (recv,), (gath,) = _dx_part(
        "dx_b", dmain, dba, w_main, w_ba, x, dh, norm_w, cut, ni - cut, (gx, gnw), [comb, small], [recv],
        [S((N_DEV,) + small.shape, F32)], phase2)
    return gx, gnw, gath, recv


class _Sem2:
    def __init__(self, sems, lo):
        self.sems, self.lo = sems, lo

    @property
    def at(self):
        outer = self

        class _At:
            def __getitem__(self, idx):
                a, k = idx
                return outer.sems.at[outer.lo + k]
        return _At()


def _relay_add(chip_sum, stage):
    _, R, C = chip_sum.shape
    half = R // 2
    tr = _tile(half, 256, 16)
    nt = half // tr
    px, py, _ = _position()
    idx = jnp.stack([2 * px + (1 - py), 2 * (1 - px) + py]).astype(jnp.int32)

    def body(idx_ref, p_ref, s_ref, o_ref):
        del idx_ref
        o_ref[0] = (p_ref[0].astype(F32) + s_ref[0].astype(F32)).astype(BF16)

    return pl.pallas_call(
        body, name="relay_add",
        grid_spec=pltpu.PrefetchScalarGridSpec(
            num_scalar_prefetch=1, grid=(2, nt),
            in_specs=[pl.BlockSpec((1, tr, C), lambda s, i, idx_ref: (idx_ref[s], s * nt + i, 0)),
                      pl.BlockSpec((1, tr, C), lambda s, i, idx_ref: (s, i, 0))],
            out_specs=pl.BlockSpec((1, tr, C), lambda s, i, idx_ref: (s, i, 0))),
        out_shape=S((2, half, C), BF16), compiler_params=_cp(ARB, ARB),
    )(idx, chip_sum, stage)


def _copy_rows(a, name):
    R, C = a.shape
    tr = _tile(R, 512, 8)

    def body(a_ref, o_ref):
        o_ref[...] = a_ref[...]

    spec = pl.BlockSpec((tr, C), lambda i: (i, 0))
    return pl.pallas_call(body, name=name, grid=(R // tr,), in_specs=[spec], out_specs=spec,
                          out_shape=S(a.shape, a.dtype), compiler_params=_cp(ARB))(a)


def _sum_slots(gath, shapes):
    spans, outs, r = [], [], 0
    for shp in shapes:
        n = 1
        for s in shp:
            n *= s
        nr = -(-n // (8 * LANES)) * 8
        spans.append((r, nr, n))
        outs.append(S((1, n), F32) if n < LANES else S((nr, LANES), F32))
        r += nr
    assert r == gath.shape[1] and gath.shape[2] == LANES

    def body(g_ref, *o_refs):
        tot = g_ref[0]
        for d in range(1, N_DEV):
            tot = tot + g_ref[d]
        for (r0, nr, n), o_ref in zip(spans, o_refs):
            o_ref[...] = tot[r0:r0 + 1, :n] if n < LANES else tot[r0:r0 + nr]

    vm = pl.BlockSpec(memory_space=pltpu.VMEM)
    res = pl.pallas_call(body, name="sum_slots", in_specs=[vm], out_specs=[vm] * len(outs), out_shape=outs)(gath)
    return [a.reshape(-1)[:n].reshape(shp) for a, (_, _, n), shp in zip(res, spans, shapes)]


def _prep_a_bwd(dq, dk, dv, c, proj, conv_w, dmain, H, D):
    T = c.shape[0]
    AW = H * D
    C3 = 3 * AW
    tb = _tile(T, 256, 8)
    nblk = T // tb
    r8 = tb // 8
    scale = float(D) ** -0.5

    def body(dq_ref, dk_ref, dv_ref, c_ref, dqn_ref, dkn_ref, dvn_ref, cn_ref, x_ref, halo_ref, cw_ref, dmain_in_ref,
             dx_ref, gcw_ref, dc_ref):
        del dmain_in_ref
        i = pl.program_id(0)

        @pl.when(i == 0)
        def _():
            gcw_ref[...] = jnp.zeros_like(gcw_ref)

        def pointwise(rows, dq_r, dk_r, dv_r, c_r, keep):
            for h in range(H):
                for part, d_r, sc in ((0, dq_r, scale), (1, dk_r, 1.0)):
                    sl = slice(part * AW + h * D, part * AW + (h + 1) * D)
                    cv = c_r[:, sl]
                    raw = _silu(cv)
                    rs = lax.rsqrt(jnp.sum(raw * raw, axis=-1, keepdims=True) + EPS)
                    nrm = raw * rs
                    dn = d_r[:, h * D:(h + 1) * D] * sc
                    draw = rs * (dn - nrm * jnp.sum(dn * nrm, axis=-1, keepdims=True))
                    dc_ref[rows, sl] = draw * _dsilu(cv) * keep
            dc_ref[rows, 2 * AW:] = dv_r[...] * _dsilu(c_r[:, 2 * AW:]) * keep

        pointwise(slice(0, tb), dq_ref, dk_ref, dv_ref, c_ref, 1.0)
        pointwise(slice(tb, tb + 8), dqn_ref, dkn_ref, dvn_ref, cn_ref, (i < nblk - 1).astype(F32))

        cw = cw_ref[...]
        dcv = dc_ref[0:tb, :]
        dx = cw[3:4, :] * dcv
        for j in range(3):
            dx = dx + cw[j:j + 1, :] * dc_ref[3 - j:3 - j + tb, :]
        dx_ref[...] = dx.astype(BF16)
        halo = halo_ref[...] * (i > 0).astype(F32)
        xp = jnp.concatenate([halo, x_ref[...]], axis=0)
        for j in range(4):
            gcw_ref[j:j + 1, :] += jnp.sum(dcv * xp[5 + j:5 + j + tb], axis=0, keepdims=True)

    nxt = lambda i: (jnp.minimum((i + 1) * r8, T // 8 - 1), 0)
    return pl.pallas_call(
        body, name="prep_a_bwd", grid=(nblk,),
        in_specs=[pl.BlockSpec((tb, AW), lambda i: (i, 0)),
                  pl.BlockSpec((tb, AW), lambda i: (i, 0)),
                  pl.BlockSpec((tb, AW), lambda i: (i, 0)),
                  pl.BlockSpec((tb, C3), lambda i: (i, 0)),
                  pl.BlockSpec((8, AW), nxt), pl.BlockSpec((8, AW), nxt), pl.BlockSpec((8, AW), nxt),
                  pl.BlockSpec((8, C3), nxt),
                  pl.BlockSpec((tb, C3), lambda i: (i, 0)),
                  pl.BlockSpec((8, C3), lambda i: (jnp.maximum(i * r8 - 1, 0), 0)),
                  pl.BlockSpec((4, C3), lambda i: (0, 0)),
                  pl.BlockSpec(memory_space=pl.ANY)],
        out_specs=[pl.BlockSpec((tb, C3), lambda i: (i, 0)),
                   pl.BlockSpec((8, C3), lambda i: (0, 0))],
        out_shape=[S(dmain.shape, dmain.dtype), S((8, C3), F32)],
        scratch_shapes=[pltpu.VMEM((tb + 8, C3), F32)],
        input_output_aliases={11: 0},
        compiler_params=_cp(ARB),
    )(dq, dk, dv, c, dq, dk, dv, c, proj, proj, conv_w, dmain)


def _adam_math(w, g, m, v):
    m2 = ADAM_B1 * m + (1.0 - ADAM_B1) * g
    v2 = ADAM_B2 * v + (1.0 - ADAM_B2) * (g * g)
    m_hat = m2 / (1.0 - ADAM_B1 ** ADAM_STEP)
    v_hat = v2 / (1.0 - ADAM_B2 ** ADAM_STEP)
    delta = -ADAM_LR * (m_hat / (jnp.sqrt(v_hat) + ADAM_EPS) + ADAM_WD * w)
    return delta, m2, v2


def _pair_sum(blocks, recv, core, name):
    K, _, R, C = blocks.shape
    tr = _tile(R, 256, 16)

    def body(core_ref, a_ref, b_ref, o_ref):
        del core_ref
        o_ref[0] = (a_ref[0, 0].astype(F32) + b_ref[0].astype(F32)).astype(BF16)

    spec = lambda: pl.BlockSpec((1, tr, C), lambda k, i, core_ref: (k, i, 0))
    return pl.pallas_call(
        body, name=name,
        grid_spec=pltpu.PrefetchScalarGridSpec(
            num_scalar_prefetch=1, grid=(K, R // tr),
            in_specs=[pl.BlockSpec((1, 1, tr, C), lambda k, i, core_ref: (k, core_ref[0], i, 0)), spec()],
            out_specs=spec()),
        out_shape=S((K, R, C), BF16), compiler_params=_cp(ARB, ARB),
    )(core, blocks, recv)


def _sum_adam(chip_sums, recv, w, m, v, chip, name, transposed=False):
    R, C = chip_sums.shape[1:]
    NR = recv.shape[0]
    tr = _tile(R, min(256, max(R // 4, 16)), 16)

    def body(chip_ref, own_ref, r_ref, w_ref, m_ref, v_ref, g_ref, d_ref, m2_ref, v2_ref):
        del chip_ref
        g = own_ref[0].astype(F32)
        for j in range(NR):
            g = g + r_ref[j].astype(F32)
        if transposed:
            g = g.T
        g_ref[...] = g
        d_ref[...], m2_ref[...], v2_ref[...] = _adam_math(w_ref[...], g, m_ref[...], v_ref[...])

    if transposed:
        spec = lambda: pl.BlockSpec((C, tr), lambda i, chip_ref: (0, i))
        shape = (C, R)
    else:
        spec = lambda: pl.BlockSpec((tr, C), lambda i, chip_ref: (i, 0))
        shape = (R, C)
    assert w.shape == shape
    return pl.pallas_call(
        body, name=name,
        grid_spec=pltpu.PrefetchScalarGridSpec(
            num_scalar_prefetch=1, grid=(R // tr,),
            in_specs=[pl.BlockSpec((1, tr, C), lambda i, chip_ref: (chip_ref[0], i, 0)),
                      pl.BlockSpec((NR, tr, C), lambda i, chip_ref: (0, i, 0)), spec(), spec(), spec()],
            out_specs=[spec(), spec(), spec(), spec()]),
        out_shape=[S(shape, F32)] * 4, compiler_params=_cp(ARB),
    )(chip, chip_sums, recv, w, m, v)


def _adam_small(ws, gs, ms, vs):
    n = len(ws)

    def body(*refs):
        ins, outs = refs[:4 * n], refs[4 * n:]
        for p in range(n):
            w_ref, g_ref, m_ref, v_ref = (ins[a * n + p] for a in range(4))
            outs[p][...], outs[n + p][...], outs[2 * n + p][...] = _adam_math(
                w_ref[...], g_ref[...], m_ref[...], v_ref[...])

    vm = pl.BlockSpec(memory_space=pltpu.VMEM)
    res = pl.pallas_call(
        body, name="adam_small", in_specs=[vm] * (4 * n), out_specs=[vm] * (3 * n),
        out_shape=[S(w.shape, F32) for w in ws] * 3,
    )(*ws, *gs, *ms, *vs)
    return res[:n], res[n:2 * n], res[2 * n:]


def _position():
    return lax.axis_index("x"), lax.axis_index("y"), lax.axis_index("c")


def _all_gather_weights(arr, x_in, norm_w, chip, tn, plans, nm):
    R = arr.shape[0]
    half = R // 2
    assert half % 16 == 0
    T, DM = x_in.shape
    tm = _tile(T, 512, 16)
    nstep = T // tm

    def body(chip_ref, x_ref, nw_ref, in_ref, xn_ref, out_ref, proj_ref, wtile_ref, stage_ref,
             send_sems, recv_sems, local_sem, stage_sems):
        i = pl.program_id(0)
        x, y, c = _position()
        me, sibling = (x, y, c), (x, y, 1 - c)
        xn, yn, diag = (1 - x, y), (x, 1 - y), (1 - x, 1 - y)
        upper, lower = pl.ds(0, half), pl.ds(half, half)

        def slot(p, rows=None):
            ref = out_ref.at[4 * p[0] + 2 * p[1] + p[2]]
            return ref if rows is None else ref.at[rows]

        def copy(kk, block, to, rows=None, src=None):
            return pltpu.make_async_remote_copy(
                src_ref=slot(block, rows) if src is None else src, dst_ref=slot(block, rows),
                send_sem=send_sems.at[kk], recv_sem=recv_sems.at[kk], device_id=to, device_id_type=MESH)

        mine = pltpu.make_async_copy(in_ref, slot(me), local_sem)
        first = [copy(0, me, sibling, src=in_ref), copy(1, me, (*xn, c), src=in_ref), copy(2, me, (*yn, c), src=in_ref)]

        @pl.when(i == 0)
        def _():
            mine.start()
            for cp in first:
                cp.start()
            copy(0, sibling, me).wait_recv()
            loads = [pltpu.make_async_copy(in_ref, stage_ref.at[c], stage_sems.at[0]),
                     pltpu.make_async_copy(slot(sibling), stage_ref.at[1 - c], stage_sems.at[1])]
            for cp in loads:
                cp.start()
            for cp in loads:
                cp.wait()
            for m, plan in enumerate(plans):
                @pl.when(chip_ref[0] == m)
                def _(plan=plan):
                    for d, s0, w, c0 in plan:
                        wtile_ref[:, c0:c0 + w] = stage_ref[d % 2, :, s0:s0 + w]

        xv = x_ref[...]
        r = lax.rsqrt(jnp.mean(xv * xv, axis=-1, keepdims=True) + EPS)
        xnv = (xv * r * nw_ref[...]).astype(BF16)
        xn_ref[...] = xnv
        proj_ref[...] = jnp.dot(xnv.astype(MXU), wtile_ref[...].astype(MXU), preferred_element_type=F32)

        @pl.when(i == nstep - 1)
        def _():
            sent = list(first)

            def then(cps):
                for cp in cps:
                    cp.start()
                sent.extend(cps)

            copy(1, (*xn, c), me).wait_recv()
            then([copy(5, (*xn, c), (*yn, c), rows=upper), copy(3, (*xn, c), sibling)])
            copy(2, (*yn, c), me).wait_recv()
            then([copy(6, (*yn, c), (*xn, c), rows=lower), copy(4, (*yn, c), sibling)])
            copy(5, (*diag, c), me, rows=upper).wait_recv()
            then([copy(7, (*diag, c), sibling, rows=upper)])
            copy(6, (*diag, c), me, rows=lower).wait_recv()
            then([copy(8, (*diag, c), sibling, rows=lower)])
            copy(3, (*xn, 1 - c), me).wait_recv()
            copy(4, (*yn, 1 - c), me).wait_recv()
            copy(7, (*diag, 1 - c), me, rows=upper).wait_recv()
            copy(8, (*diag, 1 - c), me, rows=lower).wait_recv()
            for cp in sent:
                cp.wait_send()
            mine.wait()

    any_spec = pl.BlockSpec(memory_space=pl.ANY)
    return pl.pallas_call(
        body, name="all_gather_weights",
        grid_spec=pltpu.PrefetchScalarGridSpec(
            num_scalar_prefetch=1, grid=(nstep,),
            in_specs=[pl.BlockSpec((tm, DM), lambda i, chip_ref: (i, 0)),
                      pl.BlockSpec((1, DM), lambda i, chip_ref: (0, 0)), any_spec],
            out_specs=[pl.BlockSpec((tm, DM), lambda i, chip_ref: (i, 0)), any_spec,
                       pl.BlockSpec((tm, tn), lambda i, chip_ref: (i, 2 * chip_ref[0]))],
            scratch_shapes=[pltpu.VMEM((DM, tn), arr.dtype), pltpu.VMEM((2,) + arr.shape, arr.dtype),
                            pltpu.SemaphoreType.DMA((9,)), pltpu.SemaphoreType.DMA((9,)), pltpu.SemaphoreType.DMA,
                            pltpu.SemaphoreType.DMA((2,))]),
        out_shape=[S((T, DM), BF16), S((N_DEV,) + arr.shape, arr.dtype), S((T, nm), F32)],
        compiler_params=_cp(ARB),
    )(chip, x_in, norm_w, arr)


def _sibling_copies(ins, outs, send_sems, recv_sems):
    x, y, c = _position()
    return [pltpu.make_async_remote_copy(src_ref=ins[a].at[k, 1 - c], dst_ref=outs[a].at[k],
                                         send_sem=send_sems.at[a, k], recv_sem=recv_sems.at[a, k],
                                         device_id=(x, y, 1 - c), device_id_type=MESH)
            for a in range(len(ins)) for k in range(ins[a].shape[0])]


def _sibling_sems(arrs):
    shape = (max(len(arrs), 1), arrs[0].shape[0] if arrs else 1)
    return [pltpu.SemaphoreType.DMA(shape), pltpu.SemaphoreType.DMA(shape)]


def _chip_exchange_copies(ins, outs, send_sems, recv_sems):
    x, y, c = _position()
    chips = [(1 - x, y), (x, 1 - y), (1 - x, 1 - y)]
    return [pltpu.make_async_remote_copy(
        src_ref=ins[a].at[2 * qx + qy], dst_ref=outs[a].at[j], send_sem=send_sems.at[a, j],
        recv_sem=recv_sems.at[a, j], device_id=(qx, qy, c), device_id_type=MESH)
        for a in range(len(ins)) for j, (qx, qy) in enumerate(chips)]


def _broadcast_copies(srcs, dsts, send_sems, recv_sems):
    x, y, c = _position()
    me = 4 * x + 2 * y + c
    cps = []
    for a in range(len(srcs)):
        for k in range(1, N_DEV):
            peer = (1 - x if k & 4 else x, 1 - y if k & 2 else y, 1 - c if k & 1 else c)
            cps.append(pltpu.make_async_remote_copy(
                src_ref=srcs[a], dst_ref=dsts[a].at[me], send_sem=send_sems.at[a, k - 1],
                recv_sem=recv_sems.at[a, k - 1], device_id=peer, device_id_type=MESH))
    return me, cps


def _all_reduce_small(part):
    R, C = part.shape

    def body(p_ref, out_ref, gath_ref, send_sems, recv_sems):
        me, cps = _broadcast_copies([p_ref], [gath_ref], send_sems, recv_sems)
        gath_ref[me] = p_ref[...]
        for cp in cps:
            cp.start()
        for cp in cps:
            cp.wait()
        acc = gath_ref[0]
        for d in range(1, N_DEV):
            acc = acc + gath_ref[d]
        out_ref[...] = acc

    vm = pl.BlockSpec(memory_space=pltpu.VMEM)
    return pl.pallas_call(
        body, name="all_reduce_small", in_specs=[vm], out_specs=vm, out_shape=S((R, C), F32),
        scratch_shapes=[pltpu.VMEM((N_DEV, R, C), F32), pltpu.SemaphoreType.DMA((1, N_DEV - 1)),
                        pltpu.SemaphoreType.DMA((1, N_DEV - 1))],
    )(part)


def _pack(parts):
    rows = []
    for p in parts:
        f = p.reshape(-1).astype(F32)
        pad = (-f.shape[0]) % (8 * LANES)
        rows.append(jnp.pad(f, (0, pad)).reshape(-1, LANES))
    return jnp.concatenate(rows, axis=0)


def _unpack(buf, shapes):
    out, r = [], 0
    for shp in shapes:
        n = 1
        for s in shp:
            n *= s
        nr = -(-n // (8 * LANES)) * 8
        out.append(buf[r:r + nr].reshape(-1)[:n].reshape(shp))
        r += nr
    return out


def kernel(x, norm_w, w_in, conv_w, a_log, dt_bias, head_norm_w, sgu_ln_w, sgu_ln_b, w_spatial, b_spatial, w_out, final_norm_w, loss_target, m_norm_w, m_w_in, m_conv_w, m_a_log, m_dt_bias, m_head_norm_w, m_sgu_ln_w, m_sgu_ln_b, m_w_spatial, m_b_spatial, m_w_out, m_final_norm_w, v_norm_w, v_w_in, v_conv_w, v_a_log, v_dt_bias, v_head_norm_w, v_sgu_ln_w, v_sgu_ln_b, v_w_spatial, v_b_spatial, v_w_out, v_final_norm_w):
    T, DM = x.shape[1], x.shape[2]
    H, D = a_log.shape[1], head_norm_w.shape[1]
    G, P = w_spatial.shape[1], w_spatial.shape[2]
    AW, BW = H * D, G * P
    MIX = AW + BW
    WD = w_in.shape[2]
    IN = N_DEV * WD
    RO = w_out.shape[1]
    CW = conv_w.shape[2]
    sizes = (3 * AW, AW, H, H, BW, BW, BW)
    assert sum(sizes) == IN and 2 * H <= LANES and 3 * H <= 32 and N_DEV * RO == MIX and N_DEV * CW == 3 * AW
    offs = [0]
    for s in sizes:
        offs.append(offs[-1] + s)
    px, py, pc = _position()
    dev = 4 * px + 2 * py + pc
    chip = 2 * px + py

    x2, tgt = x[0], loss_target[0]

    core_idx = jnp.reshape(pc, (1,)).astype(jnp.int32)
    chip_idx = jnp.reshape(chip, (1,)).astype(jnp.int32)
    NM = IN - 2 * H
    tn_loc, tile_plans = _local_tiles(WD, offs[2], offs[4], NM)
    xn, g_win, proj_part = _all_gather_weights(
        _cast_bf16_t(w_in[0].T, "cast_w_in"), x2, norm_w, chip_idx, tn_loc, tile_plans, NM)
    w_main, w_ba = _relayout_w(g_win, offs[2], offs[4])
    alog_row = jnp.pad(a_log, ((0, 0), (H, LANES - 2 * H)))
    dtb_row = jnp.pad(dt_bias, ((0, 0), (H, LANES - 2 * H)))
    bs_t = b_spatial[0].T

    others = jnp.arange(N_DEV - 2, dtype=jnp.int32)
    others = others + (others >= 2 * chip).astype(jnp.int32)
    proj, ba, (g_wout, g_conv) = _in_proj(xn, w_main, w_ba, proj_part, others, tn_loc,
                                          [_cast_bf16(w_out[0], "cast_w_out"), conv_w[0]])
    w_out_full = g_wout.reshape(MIX, DM)
    conv_full = g_conv.transpose(1, 0, 2).reshape(4, 3 * AW)
    q, k, v, c, gcol, grow = _prep_a_fwd(proj, ba, conv_full, alog_row, dtb_row, H, D)
    o, vnew, ssave, asave = _delta_fwd(q, k, v, gcol, grow, H, D)
    ocat = _mix_fwd(o, proj, head_norm_w, sgu_ln_w, sgu_ln_b, w_spatial[0], bs_t, H, D, G, P)
    dh, dh_bf, d_ocat, loss_acc, g_fnw = _out_proj_loss(ocat, w_out_full, x2, tgt, final_norm_w.reshape(1, DM))

    g_wout_blocks = _grad_w(ocat, dh_bf, "grad_w_out").reshape(4, 2, RO, DM)
    (d_o, dmain, g_hnw, g_ln, g_wsp, g_bs_t), (sib_wout,) = _mix_bwd(
        d_ocat, o, proj, head_norm_w, sgu_ln_w, sgu_ln_b, w_spatial[0], bs_t, H, D, G, P, [g_wout_blocks])
    chip_wout = _pair_sum(g_wout_blocks, sib_wout, core_idx, "pair_sum_w_out")
    (dq, dk, dv, dba, dpar), (recv_wout,) = _delta_bwd(
        q, k, v, gcol, grow, ba, vnew, ssave, asave, d_o, alog_row, dtb_row, H, D, [chip_wout])
    dmain, g_conv_part = _prep_a_bwd(dq, dk, dv, c, proj, conv_full, dmain, H, D)
    keep_win, sib_win = _grad_w_in(xn, dmain, dba, WD, offs[2], offs[4])
    chip_win = _pair_sum_plain(keep_win, sib_win, "pair_sum_w_in")
    small_shapes = [a_log.shape, dt_bias.shape, head_norm_w.shape, sgu_ln_w.shape, sgu_ln_b.shape,
                    w_spatial.shape, b_spatial.shape, final_norm_w.shape]
    parts = [dpar[0, H:2 * H], dpar[1, H:2 * H], g_hnw[0], g_ln[0], g_ln[1], g_wsp, g_bs_t[:, :G].T, g_fnw[0],
             g_conv_part[:4], loss_acc[0, :1]]
    grad_x, g_nw, small_gath, recv_win = _dx(dmain, dba, w_main, w_ba, x2, dh, norm_w, chip_win, _pack(parts), 4)
    red = _sum_slots(small_gath, small_shapes + [(4, 3 * AW), (1,)])
    grad_w_in, delta_w_in, new_m_w_in, new_v_w_in = _sum_adam(
        chip_win, recv_win, w_in[0].T, m_w_in[0].T, v_w_in[0].T, chip_idx, "sum_adam_w_in", transposed=True)
    grad_x = _copy_rows(grad_x, "copy_grad_x")
    grad_w_out, delta_w_out, new_m_w_out, new_v_w_out = _sum_adam(
        chip_wout, recv_wout, w_out[0], m_w_out[0], v_w_out[0], chip_idx, "sum_adam_w_out")
    red_nw = _all_reduce_small(_pack([g_nw[0]]))
    grads_small = _unpack(red_nw, [norm_w.shape]) + red
    loss = grads_small.pop()[0]
    g_conv_full = grads_small.pop()
    grad_conv = lax.dynamic_slice_in_dim(g_conv_full, dev * CW, CW, axis=1)[None]
    small_w = [norm_w, a_log, dt_bias, head_norm_w, sgu_ln_w, sgu_ln_b, w_spatial, b_spatial, final_norm_w, conv_w]
    small_m = [m_norm_w, m_a_log, m_dt_bias, m_head_norm_w, m_sgu_ln_w, m_sgu_ln_b, m_w_spatial, m_b_spatial,
               m_final_norm_w, m_conv_w]
    small_v = [v_norm_w, v_a_log, v_dt_bias, v_head_norm_w, v_sgu_ln_w, v_sgu_ln_b, v_w_spatial, v_b_spatial,
               v_final_norm_w, v_conv_w]
    small_g = grads_small + [grad_conv]
    d_s, m_s, v_s = _adam_small(small_w, small_g, small_m, small_v)

    def order(small, win, wout):
        return [small[0], win.T[None], small[9], small[1], small[2], small[3], small[4], small[5], small[6], small[7],
                wout[None], small[8]]

    grads = order(small_g, grad_w_in, grad_w_out)
    deltas = order(d_s, delta_w_in, delta_w_out)
    new_m = order(m_s, new_m_w_in, new_m_w_out)
    new_v = order(v_s, new_v_w_in, new_v_w_out)
    return (loss, grad_x[None], *grads, *deltas, *new_m, *new_v)
```

```python
import jax
import jax.numpy as jnp
from jax import lax
from jax.experimental import pallas as pl
from jax.experimental.pallas import tpu as pltpu

F32 = jnp.float32
BF16 = jnp.bfloat16
MXU = jnp.bfloat16
HI = lax.Precision.HIGHEST
EPS = 1e-6
CHUNK_A = 64
LANES = 128
MESH = pl.DeviceIdType.MESH
N_DEV = 8

ADAM_LR = 0.001
ADAM_B1 = 0.9
ADAM_B2 = 0.999
ADAM_EPS = 1e-08
ADAM_WD = 0.01
ADAM_STEP = 10

S = jax.ShapeDtypeStruct
ARB = "arbitrary"


def _cp(*sem, vmem_mib=56):
    return pltpu.CompilerParams(dimension_semantics=tuple(sem), vmem_limit_bytes=vmem_mib * 1024 * 1024)


def _tile(n, cap, mult):
    best = None
    t = mult
    while t <= min(n, cap):
        if n % t == 0:
            best = t
        t += mult
    return best if best is not None else n


def _mm(a, b):
    return jnp.dot(a.astype(MXU), b.astype(MXU), preferred_element_type=F32)


def _mm_nt(a, b):
    return lax.dot_general(a.astype(MXU), b.astype(MXU), (((1,), (1,)), ((), ())), preferred_element_type=F32)


def _mm_tn(a, b):
    return lax.dot_general(a.astype(MXU), b.astype(MXU), (((0,), (0,)), ((), ())), preferred_element_type=F32)


def _mmh(a, b):
    return jnp.dot(a, b, precision=HI, preferred_element_type=F32)


def _sigmoid(x):
    return 1.0 / (1.0 + jnp.exp(-x))


def _silu(x):
    return x * _sigmoid(x)


def _dsilu(x):
    s = _sigmoid(x)
    return s * (1.0 + x * (1.0 - s))


def _softplus(x):
    return jnp.maximum(x, 0.0) + jnp.log(1.0 + jnp.exp(-jnp.abs(x)))


def _pieces(wd, gate_lo, gate_hi, total):
    out = []
    for d in range(N_DEV):
        lo, hi = d * wd, (d + 1) * wd
        for dest, a, b, shift in (("main", 0, gate_lo, 0), ("gate", gate_lo, gate_hi, -gate_lo),
                                  ("main", gate_hi, total, gate_lo - gate_hi)):
            s0, s1 = max(lo, a), min(hi, b)
            if s0 < s1:
                out.append((d, s0 - lo, s1 - lo, dest, s0 + shift))
    return out


def _local_tiles(wd, gate_lo, gate_hi, nm):
    n_tiles = N_DEV - 1
    assert nm % (n_tiles * LANES) == 0
    tn = nm // n_tiles
    plans = []
    for m in range(N_DEV // 2):
        lo, hi = 2 * m * tn, (2 * m + 1) * tn
        plan = []
        for d, s0, s1, dest, c0 in _pieces(wd, gate_lo, gate_hi, N_DEV * wd):
            if dest != "main":
                continue
            a, b = max(c0, lo), min(c0 + (s1 - s0), hi)
            if a < b:
                assert d // 2 == m, "tile 2m must come from chip m's own shards"
                plan.append((d, s0 + (a - c0), b - a, a - lo))
        assert sum(p[2] for p in plan) == tn
        plans.append(plan)
    return tn, plans


def _cast_bf16(a, name):
    R, C = a.shape
    tr = _tile(R, 256, 16)

    def body(a_ref, o_ref):
        o_ref[...] = a_ref[...].astype(BF16)

    spec = pl.BlockSpec((tr, C), lambda i: (i, 0))
    return pl.pallas_call(body, name=name, grid=(R // tr,), in_specs=[spec], out_specs=spec,
                          out_shape=S((R, C), BF16), compiler_params=_cp(ARB))(a)


def _cast_bf16_t(a_t, name):
    C, R = a_t.shape
    tr = _tile(R, 256, LANES)

    def body(a_ref, o_ref):
        o_ref[...] = a_ref[...].T.astype(BF16)

    return pl.pallas_call(body, name=name, grid=(R // tr,), in_specs=[pl.BlockSpec((C, tr), lambda i: (0, i))],
                          out_specs=pl.BlockSpec((tr, C), lambda i: (i, 0)),
                          out_shape=S((R, C), BF16), compiler_params=_cp(ARB))(a_t)


def _relayout_w(g_win, gate_lo, gate_hi):
    _, DM, WD = g_win.shape
    total = N_DEV * WD
    NM = total - (gate_hi - gate_lo)
    tr = _tile(DM, 512, 16)
    plan = _pieces(WD, gate_lo, gate_hi, total)

    def body(g_ref, main_ref, gate_ref):
        gate_ref[...] = jnp.zeros_like(gate_ref)
        for d, s0, s1, dest, c0 in plan:
            dst = main_ref if dest == "main" else gate_ref
            dst[:, c0:c0 + (s1 - s0)] = g_ref[d, :, s0:s1]

    return pl.pallas_call(
        body, name="relayout_w", grid=(DM // tr,),
        in_specs=[pl.BlockSpec((N_DEV, tr, WD), lambda i: (0, i, 0))],
        out_specs=[pl.BlockSpec((tr, NM), lambda i: (i, 0)), pl.BlockSpec((tr, LANES), lambda i: (i, 0))],
        out_shape=[S((DM, NM), g_win.dtype), S((DM, LANES), g_win.dtype)],
        compiler_params=_cp(ARB),
    )(g_win)


def _in_proj(xn, w_main, w_ba, proj_part, tiles, tn, shards):
    T, DM = xn.shape
    NM = w_main.shape[1]
    tm = _tile(T, 2048, 16)
    ni, nj = T // tm, tiles.shape[0]
    ns = len(shards)

    def body(tiles_ref, xn_ref, w_ref, wba_ref, part_ref, *rest):
        del tiles_ref, part_ref
        srcs = rest[:ns]
        proj_ref, ba_ref = rest[ns:ns + 2]
        gath = rest[ns + 2:2 * ns + 2]
        send_sems, recv_sems, local_sems = rest[2 * ns + 2:]
        i = pl.program_id(0)
        j = pl.program_id(1)
        me, cps = _broadcast_copies(srcs, gath, send_sems, recv_sems)
        cps = cps + [pltpu.make_async_copy(srcs[a], gath[a].at[me], local_sems.at[a]) for a in range(ns)]

        @pl.when((i == 0) & (j == 0))
        def _():
            for cp in cps:
                cp.start()

        @pl.when(j == 0)
        def _():
            ba_ref[...] = jnp.dot(xn_ref[...].astype(MXU), wba_ref[...].astype(MXU), preferred_element_type=F32)

        proj_ref[...] = jnp.dot(xn_ref[...].astype(MXU), w_ref[...].astype(MXU), preferred_element_type=F32)

        @pl.when((i == ni - 1) & (j == nj - 1))
        def _():
            for cp in cps:
                cp.wait()

    any_spec = pl.BlockSpec(memory_space=pl.ANY)
    res = pl.pallas_call(
        body, name="in_proj",
        grid_spec=pltpu.PrefetchScalarGridSpec(
            num_scalar_prefetch=1, grid=(ni, nj),
            in_specs=[pl.BlockSpec((tm, DM), lambda i, j, t: (i, 0)),
                      pl.BlockSpec((DM, tn), lambda i, j, t: (0, t[j])),
                      pl.BlockSpec((DM, LANES), lambda i, j, t: (0, 0)), any_spec] + [any_spec] * ns,
            out_specs=[pl.BlockSpec((tm, tn), lambda i, j, t: (i, t[j])),
                       pl.BlockSpec((tm, LANES), lambda i, j, t: (i, 0))] + [any_spec] * ns,
            scratch_shapes=[pltpu.SemaphoreType.DMA((ns, N_DEV - 1)), pltpu.SemaphoreType.DMA((ns, N_DEV - 1)),
                            pltpu.SemaphoreType.DMA((ns,))]),
        out_shape=[S((T, NM), F32), S((T, LANES), F32)] + [S((N_DEV,) + a.shape, a.dtype) for a in shards],
        input_output_aliases={4: 0},
        compiler_params=_cp(ARB, ARB, vmem_mib=58),
    )(tiles, xn, w_main, w_ba, proj_part, *shards)
    return res[0], res[1], res[2:]


def _prep_a_fwd(proj, ba, conv_w, alog_row, dtb_row, H, D):
    T = proj.shape[0]
    AW = H * D
    C3 = 3 * AW
    tb = _tile(T, 256, CHUNK_A)
    nch = tb // CHUNK_A
    nblk = T // tb
    scale = float(D) ** -0.5

    def body(x_ref, halo_ref, ba_ref, cw_ref, al_ref, dt_ref, q_ref, k_ref, v_ref, c_ref, gcol_ref, grow_ref):
        i = pl.program_id(0)
        xv = x_ref[...]
        halo = halo_ref[...] * (i > 0).astype(F32)
        xp = jnp.concatenate([halo, xv], axis=0)
        cw = cw_ref[...]
        c = cw[0:1, :] * xp[5:5 + tb]
        for j in range(1, 4):
            c = c + cw[j:j + 1, :] * xp[5 + j:5 + j + tb]
        c_ref[...] = c
        a = _silu(c)
        for h in range(H):
            qh = a[:, h * D:(h + 1) * D]
            kh = a[:, AW + h * D:AW + (h + 1) * D]
            qr = lax.rsqrt(jnp.sum(qh * qh, axis=-1, keepdims=True) + EPS)
            kr = lax.rsqrt(jnp.sum(kh * kh, axis=-1, keepdims=True) + EPS)
            q_ref[:, h * D:(h + 1) * D] = qh * (qr * scale)
            k_ref[:, h * D:(h + 1) * D] = kh * kr
        v_ref[...] = a[:, 2 * AW:]

        bav = ba_ref[...]
        lane = lax.broadcasted_iota(jnp.int32, (tb, LANES), 1)
        beta = _sigmoid(bav)
        g = -jnp.exp(al_ref[...]) * _softplus(bav + dt_ref[...])
        gates = jnp.where(lane < H, beta, jnp.where(lane < 2 * H, g, 0.0))
        ri = lax.broadcasted_iota(jnp.int32, (CHUNK_A, CHUNK_A), 0)
        ci = lax.broadcasted_iota(jnp.int32, (CHUNK_A, CHUNK_A), 1)
        tri = (ri >= ci).astype(F32)
        lane_c = lax.broadcasted_iota(jnp.int32, (CHUNK_A, LANES), 1)
        for cc in range(nch):
            gch = gates[cc * CHUNK_A:(cc + 1) * CHUNK_A]
            gc = pltpu.roll(_mmh(tri, gch), H, 1)
            full = jnp.where(lane_c < 2 * H, gch, jnp.where(lane_c < 3 * H, gc, 0.0))
            gcol_ref[cc * CHUNK_A:(cc + 1) * CHUNK_A, :] = full
            grow_ref[cc] = full.T[0:32, :]

    return pl.pallas_call(
        body, name="prep_a_fwd", grid=(nblk,),
        in_specs=[pl.BlockSpec((tb, C3), lambda i: (i, 0)),
                  pl.BlockSpec((8, C3), lambda i: (jnp.maximum(i * (tb // 8) - 1, 0), 0)),
                  pl.BlockSpec((tb, LANES), lambda i: (i, 0)),
                  pl.BlockSpec((4, C3), lambda i: (0, 0)),
                  pl.BlockSpec((1, LANES), lambda i: (0, 0)),
                  pl.BlockSpec((1, LANES), lambda i: (0, 0))],
        out_specs=[pl.BlockSpec((tb, AW), lambda i: (i, 0)),
                   pl.BlockSpec((tb, AW), lambda i: (i, 0)),
                   pl.BlockSpec((tb, AW), lambda i: (i, 0)),
                   pl.BlockSpec((tb, C3), lambda i: (i, 0)),
                   pl.BlockSpec((tb, LANES), lambda i: (i, 0)),
                   pl.BlockSpec((nch, 32, CHUNK_A), lambda i: (i, 0, 0))],
        out_shape=[S((T, AW), F32), S((T, AW), F32), S((T, AW), F32), S((T, C3), F32),
                   S((T, LANES), F32), S((T // CHUNK_A, 32, CHUNK_A), F32)],
        compiler_params=_cp(ARB),
    )(proj, proj, ba, conv_w, alog_row, dtb_row)


_NN = (((1,), (0,)), ((), ()))
_TN = (((0,), (0,)), ((), ()))


def _split(a):
    hi = a.astype(BF16)
    return hi, (a - hi.astype(F32)).astype(BF16)


def _mm3(a, b, dims=_NN):
    ah, al = a if isinstance(a, tuple) else _split(a)
    bh, bl = b if isinstance(b, tuple) else _split(b)
    dg = lambda p, r: lax.dot_general(p, r, dims, preferred_element_type=F32)
    return dg(ah, bh) + (dg(ah, bl) + dg(al, bh))


def _interleave(gens):
    gens = list(gens)
    while gens:
        alive = []
        for g in gens:
            try:
                next(g)
                alive.append(g)
            except StopIteration:
                pass
        gens = alive


def _chunk_terms(q, k, v, gcolv, growv, h, H):
    C = CHUNK_A
    beta_c = gcolv[:, h:h + 1]
    g_c = gcolv[:, H + h:H + h + 1]
    gc_c = gcolv[:, 2 * H + h:2 * H + h + 1]
    gc_r = growv[2 * H + h:2 * H + h + 1, :]
    ri = lax.broadcasted_iota(jnp.int32, (C, C), 0)
    ci = lax.broadcasted_iota(jnp.int32, (C, C), 1)
    incl = ri >= ci
    strict = ri > ci
    kb = k * beta_c
    vb = v * beta_c
    p_raw = _mm_nt(kb, k)
    qk_raw = _mm_nt(q, k)
    gam = jnp.where(incl, jnp.exp(jnp.where(incl, gc_c - gc_r, 0.0)), 0.0)
    e_c = jnp.exp(gc_c)
    gl = gc_r[:, C - 1:C]
    edec = jnp.exp(gl - gc_c)
    yield
    lmat = jnp.where(strict, p_raw * gam, 0.0)
    attn = jnp.where(incl, qk_raw * gam, 0.0)
    return dict(beta_c=beta_c, g_c=g_c, gc_c=gc_c, gc_r=gc_r, incl=incl, strict=strict, gam=gam, e_c=e_c,
                kb=kb, vb=vb, lmat=lmat, attn=attn, gl=gl, edec=edec, ri=ri, ci=ci)


INV_BLOCK = 16


def _inv_unit_lower(lmat):
    C = lmat.shape[0]
    ri = lax.broadcasted_iota(jnp.int32, (C, C), 0)
    ci = lax.broadcasted_iota(jnp.int32, (C, C), 1)
    eye = (ri == ci).astype(F32)
    same = (ri // INV_BLOCK) == (ci // INV_BLOCK)

    def neumann(x, order):
        a = eye + x
        n = 1
        while 2 * n < order:
            xs = _split(x)
            x = _mm3(xs, xs)
            yield
            a = a + _mm3(a, x)
            n *= 2
        yield
        return a

    inv_d = yield from neumann(-jnp.where(same, lmat, 0.0), INV_BLOCK)
    m = _mm3(inv_d, jnp.where(same, 0.0, lmat))
    yield
    inv_m = yield from neumann(-m, C // INV_BLOCK)
    a = _mm3(inv_m, inv_d)
    yield
    return a


def _delta_fwd(q, k, v, gcol, grow, H, D):
    T = q.shape[0]
    C = CHUNK_A
    N = T // C
    AW = H * D
    CPS = 2 if N % 2 == 0 else 1

    def body(q_ref, k_ref, v_ref, gcol_ref, grow_ref, o_ref, vn_ref, ssave_ref, asave_ref, s_ref):
        @pl.when(pl.program_id(0) == 0)
        def _():
            s_ref[...] = jnp.zeros_like(s_ref)

        state = {(0, h): s_ref[h] for h in range(H)}

        def head(cc, h):
            rows = slice(cc * C, (cc + 1) * C)
            sl = slice(h * D, (h + 1) * D)
            qv, kv, vv = q_ref[rows, sl], k_ref[rows, sl], v_ref[rows, sl]
            t = yield from _chunk_terms(qv, kv, vv, gcol_ref[rows, :], grow_ref[cc], h, H)
            a = yield from _inv_unit_lower(t["lmat"])
            asave_ref[cc, h] = a
            while (cc, h) not in state:
                yield
            st = state[(cc, h)]
            ssave_ref[cc, h] = st
            ks = _mm(t["kb"] * t["e_c"], st)
            o_inter = _mm(qv * t["e_c"], st)
            yield
            v_new = _mm3(a, t["vb"] - ks)
            yield
            vn_ref[rows, sl] = v_new
            o_intra = _mm(t["attn"], v_new)
            s_upd = _mm_tn(kv * t["edec"], v_new)
            yield
            o_ref[rows, sl] = o_inter + o_intra
            state[(cc + 1, h)] = st * jnp.exp(t["gl"]) + s_upd

        _interleave(head(cc, h) for cc in range(CPS) for h in range(H))
        for h in range(H):
            s_ref[h] = state[(CPS, h)]

    blk = lambda: pl.BlockSpec((CPS * C, AW), lambda n: (n, 0))
    return pl.pallas_call(
        body, name="delta_fwd", grid=(N // CPS,),
        in_specs=[blk(), blk(), blk(),
                  pl.BlockSpec((CPS * C, LANES), lambda n: (n, 0)),
                  pl.BlockSpec((CPS, 32, C), lambda n: (n, 0, 0))],
        out_specs=[blk(), blk(),
                   pl.BlockSpec((CPS, H, D, D), lambda n: (n, 0, 0, 0)),
                   pl.BlockSpec((CPS, H, C, C), lambda n: (n, 0, 0, 0))],
        out_shape=[S((T, AW), F32), S((T, AW), F32), S((N, H, D, D), F32), S((N, H, C, C), F32)],
        scratch_shapes=[pltpu.VMEM((H, D, D), F32)],
        compiler_params=_cp(ARB),
    )(q, k, v, gcol, grow)


def _delta_bwd(q, k, v, gcol, grow, ba, vnew, ssave, asave, d_o, a_log, dt_bias, H, D, carry):
    T = q.shape[0]
    C = CHUNK_A
    N = T // C
    AW = H * D
    nc = len(carry)
    CPS = 2 if N % 2 == 0 else 1
    NS = N // CPS

    def body(al_ref, dt_ref, q_ref, k_ref, v_ref, gcol_ref, grow_ref, ba_ref, vn_ref, ss_ref, as_ref, do_ref, *rest):
        cins = rest[:nc]
        dq_ref, dk_ref, dv_ref, dgate_ref, dpar_ref = rest[nc:nc + 5]
        couts = rest[nc + 5:2 * nc + 5]
        ds_ref, csend, crecv = rest[2 * nc + 5:]
        ccps = _chip_exchange_copies(cins, couts, csend, crecv)

        @pl.when(pl.program_id(0) == 0)
        def _():
            ds_ref[...] = jnp.zeros_like(ds_ref)
            dpar_ref[...] = jnp.zeros_like(dpar_ref)
            for cp in ccps:
                cp.start()

        lane = lax.broadcasted_iota(jnp.int32, (C, LANES), 1)
        rowi = lax.broadcasted_iota(jnp.int32, (C, 1), 0)
        acc = {cc: jnp.zeros((C, LANES), F32) for cc in range(CPS)}
        state = {(0, h): ds_ref[h] for h in range(H)}

        def head(oi, h):
            cc = CPS - 1 - oi
            rows = slice(cc * C, (cc + 1) * C)
            sl = slice(h * D, (h + 1) * D)
            st = ss_ref[cc, h]
            a = as_ref[cc, h]
            qv, kv, vv, dov, v_new = q_ref[rows, sl], k_ref[rows, sl], v_ref[rows, sl], do_ref[rows, sl], vn_ref[rows, sl]
            t = yield from _chunk_terms(qv, kv, vv, gcol_ref[rows, :], grow_ref[cc], h, H)
            beta_c, e_c, gam, kb = t["beta_c"], t["e_c"], t["gam"], t["kb"]
            incl, strict, attn, lmat, edec = t["incl"], t["strict"], t["attn"], t["lmat"], t["edec"]
            kdec = kv * edec
            egl = jnp.exp(t["gl"])
            qe = qv * e_c
            ekb = kb * e_c

            t1 = _mm_nt(dov, st)
            ds_o = _mm_tn(qe, dov)
            dattn_raw = _mm_nt(dov, v_new)
            dv_new_o = _mm_tn(attn, dov)
            yield
            while (oi, h) not in state:
                yield
            ds_next = state[(oi, h)]
            dkdec = _mm_nt(v_new, ds_next)
            dv_new_s = _mm(kdec, ds_next)
            yield
            dgl = egl * jnp.sum(jnp.sum(st * ds_next, axis=1, keepdims=True), axis=0, keepdims=True)
            dk = edec * dkdec
            r = jnp.sum(dkdec * kdec, axis=1, keepdims=True)
            dgc = -r
            dgl = dgl + jnp.sum(r, axis=0, keepdims=True)
            dq = e_c * t1
            dgc = dgc + jnp.sum(t1 * qe, axis=1, keepdims=True)
            dattn = jnp.where(incl, dattn_raw, 0.0)
            dv_new = dv_new_s + dv_new_o
            dqm = dattn * gam
            z = dattn * attn
            dvb = _mm3(a, dv_new, _TN)
            dq_a = _mm(dqm, kv)
            dk_a = _mm_tn(dqm, qv)
            yield
            dq_ref[rows, sl] = dq + dq_a
            dv_ref[rows, sl] = beta_c * dvb
            ds_kb = _mm_tn(ekb, dvb)
            dekb_neg = _mm_nt(dvb, st)
            dl_neg = _mm_nt(dvb, v_new)
            yield
            state[(oi + 1, h)] = egl * ds_next + ds_o - ds_kb
            dekb = -dekb_neg
            dl = jnp.where(strict, -dl_neg, 0.0)
            dp = dl * gam
            z = z + dl * lmat
            dkb_p = _mm(dp, kv)
            dk_p = _mm_tn(dp, kb)
            dgc = dgc + jnp.sum(dekb * ekb, axis=1, keepdims=True)
            dgc = dgc + jnp.sum(z, axis=1, keepdims=True) - jnp.sum(z.T, axis=1, keepdims=True)
            dgc = dgc + jnp.where(rowi == C - 1, dgl, 0.0)
            yield
            dkb = dkb_p + e_c * dekb
            dk_ref[rows, sl] = dk + dk_a + dk_p + beta_c * dkb
            dbeta = jnp.sum(dkb * kv, axis=1, keepdims=True) + jnp.sum(dvb * vv, axis=1, keepdims=True)
            acc[cc] = acc[cc] + jnp.where(lane == h, dbeta, 0.0) + jnp.where(lane == H + h, dgc, 0.0)

        _interleave(head(oi, h) for oi in range(CPS) for h in range(H))
        for h in range(H):
            ds_ref[h] = state[(CPS, h)]
        ri = lax.broadcasted_iota(jnp.int32, (C, C), 0)
        ci = lax.broadcasted_iota(jnp.int32, (C, C), 1)
        upper = (ri <= ci).astype(F32)
        dal = jnp.zeros((1, LANES), F32)
        ddt = jnp.zeros((1, LANES), F32)
        for cc in range(CPS):
            rows = slice(cc * C, (cc + 1) * C)
            gates = gcol_ref[rows, :]
            dg_all = _mm3(upper, acc[cc])
            d_braw = acc[cc] * gates * (1.0 - gates)
            d_araw = dg_all * (-jnp.exp(al_ref[...])) * _sigmoid(ba_ref[rows, :] + dt_ref[...])
            dgate_ref[rows, :] = jnp.where(lane < H, d_braw, jnp.where(lane < 2 * H, d_araw, 0.0)).astype(BF16)
            dal = dal + jnp.sum(dg_all * gates, axis=0, keepdims=True)
            ddt = ddt + jnp.sum(d_araw, axis=0, keepdims=True)
        dpar_ref[0:1, :] += dal
        dpar_ref[1:2, :] += ddt

        @pl.when(pl.program_id(0) == NS - 1)
        def _():
            for cp in ccps:
                cp.wait()

    rev = lambda s: NS - 1 - s
    blk = lambda: pl.BlockSpec((CPS * C, AW), lambda s: (rev(s), 0))
    row = pl.BlockSpec((1, LANES), lambda s: (0, 0))
    any_spec = pl.BlockSpec(memory_space=pl.ANY)
    res = pl.pallas_call(
        body, name="delta_bwd", grid=(NS,),
        in_specs=[row, row, blk(), blk(), blk(),
                  pl.BlockSpec((CPS * C, LANES), lambda s: (rev(s), 0)),
                  pl.BlockSpec((CPS, 32, C), lambda s: (rev(s), 0, 0)),
                  pl.BlockSpec((CPS * C, LANES), lambda s: (rev(s), 0)),
                  blk(),
                  pl.BlockSpec((CPS, H, D, D), lambda s: (rev(s), 0, 0, 0)),
                  pl.BlockSpec((CPS, H, C, C), lambda s: (rev(s), 0, 0, 0)),
                  blk()] + [any_spec] * nc,
        out_specs=[blk(), blk(), blk(),
                   pl.BlockSpec((CPS * C, LANES), lambda s: (rev(s), 0)),
                   pl.BlockSpec((8, LANES), lambda s: (0, 0))] + [any_spec] * nc,
        out_shape=[S((T, AW), F32), S((T, AW), F32), S((T, AW), F32),
                   S((T, LANES), BF16), S((8, LANES), F32)] + [S((3,) + a.shape[1:], a.dtype) for a in carry],
        scratch_shapes=[pltpu.VMEM((H, D, D), F32),
                        pltpu.SemaphoreType.DMA((max(nc, 1), 3)), pltpu.SemaphoreType.DMA((max(nc, 1), 3))],
        compiler_params=_cp(ARB),
    )(a_log, dt_bias, q, k, v, gcol, grow, ba, vnew, ssave, asave, d_o, *carry)
    return res[:5], res[5:]


def _ln_stats(xv):
    mu = jnp.mean(xv, axis=-1, keepdims=True)
    xc = xv - mu
    var = jnp.mean(xc * xc, axis=-1, keepdims=True)
    rstd = lax.rsqrt(var + EPS)
    return xc * rstd, rstd


def _mix_fwd(o, proj, head_norm_w, ln_w, ln_b, w_sp, bs_t, H, D, G, P):
    T = o.shape[0]
    AW, BW = H * D, G * P
    MIX = AW + BW
    nb = AW // BW if AW % BW == 0 else None
    assert nb == 1, "group widths must match the projection column blocks"
    cb = 3

    def body(o_ref, za_ref, ub_ref, vb_ref, zb_ref, hw_ref, lw_ref, lb_ref, w_ref, bs_ref, out_ref):
        hw = hw_ref[...]
        for h in range(H):
            sl = slice(h * D, (h + 1) * D)
            oh = o_ref[:, sl]
            rs = lax.rsqrt(jnp.mean(oh * oh, axis=-1, keepdims=True) + EPS)
            out_ref[:, sl] = (oh * rs * hw * _silu(za_ref[:, sl])).astype(BF16)
        xhat, _ = _ln_stats(vb_ref[...])
        vn = xhat * lw_ref[...] + lb_ref[...]
        ri = lax.broadcasted_iota(jnp.int32, (P, P), 0)
        ci = lax.broadcasted_iota(jnp.int32, (P, P), 1)
        bsv = bs_ref[...]
        for g in range(G):
            sl = slice(g * P, (g + 1) * P)
            wm = jnp.where(ri >= ci, w_ref[g], 0.0)
            s = _mm(wm, vn[:, sl]) + bsv[:, g:g + 1]
            out_ref[:, AW + g * P:AW + (g + 1) * P] = (ub_ref[:, sl] * s * _silu(zb_ref[:, sl])).astype(BF16)

    row = lambda w: pl.BlockSpec((1, w), lambda i: (0, 0))
    return pl.pallas_call(
        body, name="mix_fwd", grid=(T // P,),
        in_specs=[pl.BlockSpec((P, AW), lambda i: (i, 0)),
                  pl.BlockSpec((P, AW), lambda i: (i, cb)),
                  pl.BlockSpec((P, BW), lambda i: (i, cb + 1)),
                  pl.BlockSpec((P, BW), lambda i: (i, cb + 2)),
                  pl.BlockSpec((P, BW), lambda i: (i, cb + 3)),
                  row(D), row(BW), row(BW),
                  pl.BlockSpec((G, P, P), lambda i: (0, 0, 0)),
                  pl.BlockSpec((P, G), lambda i: (0, 0))],
        out_specs=pl.BlockSpec((P, MIX), lambda i: (i, 0)),
        out_shape=S((T, MIX), BF16),
        compiler_params=_cp(ARB),
    )(o, proj, proj, proj, proj, head_norm_w, ln_w, ln_b, w_sp, bs_t)


def _mix_bwd(d_ocat, o, proj, head_norm_w, ln_w, ln_b, w_sp, bs_t, H, D, G, P, carry):
    T = o.shape[0]
    AW, BW = H * D, G * P
    MIX = AW + BW
    cb = 3
    nc = len(carry)

    def body(dc_ref, o_ref, za_ref, ub_ref, vb_ref, zb_ref, hw_ref, lw_ref, lb_ref, w_ref, bs_ref, *rest):
        cins = rest[:nc]
        do_ref, dmain_ref, dhw_ref, dln_ref, dw_ref, dbs_ref = rest[nc:nc + 6]
        couts = rest[nc + 6:2 * nc + 6]
        dvn_ref, drest_ref, out_sems, csend, crecv = rest[2 * nc + 6:]
        i = pl.program_id(0)
        slot = lax.rem(i, 2)
        ccps = _sibling_copies(cins, couts, csend, crecv)

        def out_copy(step, s):
            return pltpu.make_async_copy(
                drest_ref.at[s], dmain_ref.at[pl.ds(step * P, P), pl.ds(cb * AW, AW + 3 * BW)], out_sems.at[s])

        @pl.when(i == 0)
        def _():
            dhw_ref[...] = jnp.zeros_like(dhw_ref)
            dln_ref[...] = jnp.zeros_like(dln_ref)
            dw_ref[...] = jnp.zeros_like(dw_ref)
            dbs_ref[...] = jnp.zeros_like(dbs_ref)
            for cp in ccps:
                cp.start()

        @pl.when(i >= 2)
        def _():
            out_copy(i - 2, slot).wait()

        hw = hw_ref[...]
        dhw = jnp.zeros((1, D), F32)
        for h in range(H):
            sl = slice(h * D, (h + 1) * D)
            oh = o_ref[:, sl]
            za = za_ref[:, sl]
            doa = dc_ref[:, sl]
            rs = lax.rsqrt(jnp.mean(oh * oh, axis=-1, keepdims=True) + EPS)
            xh = oh * rs
            d_on = doa * _silu(za)
            drest_ref[slot, :, sl] = (doa * (xh * hw) * _dsilu(za)).astype(BF16)
            dhw = dhw + jnp.sum(d_on * xh, axis=0, keepdims=True)
            dxh = d_on * hw
            do_ref[:, sl] = rs * (dxh - xh * jnp.mean(dxh * xh, axis=-1, keepdims=True))
        dhw_ref[0:1, :] += dhw

        xhat, rstd = _ln_stats(vb_ref[...])
        lw = lw_ref[...]
        vn = xhat * lw + lb_ref[...]
        ri = lax.broadcasted_iota(jnp.int32, (P, P), 0)
        ci = lax.broadcasted_iota(jnp.int32, (P, P), 1)
        lane = lax.broadcasted_iota(jnp.int32, (P, LANES), 1)
        bsv = bs_ref[...]
        dbs = jnp.zeros((P, LANES), F32)
        for g in range(G):
            sl = slice(g * P, (g + 1) * P)
            wm = jnp.where(ri >= ci, w_ref[g], 0.0)
            vng = vn[:, sl]
            s = _mm(wm, vng) + bsv[:, g:g + 1]
            dob = dc_ref[:, AW + g * P:AW + (g + 1) * P]
            ub = ub_ref[:, sl]
            zb = zb_ref[:, sl]
            szb = _silu(zb)
            drest_ref[slot, :, AW + g * P:AW + (g + 1) * P] = (dob * s * szb).astype(BF16)
            drest_ref[slot, :, AW + 2 * BW + g * P:AW + 2 * BW + (g + 1) * P] = (
                dob * ub * s * _dsilu(zb)).astype(BF16)
            ds = dob * ub * szb
            dvn_ref[:, sl] = _mm_tn(wm, ds)
            dw_ref[g] += jnp.where(ri >= ci, _mm_nt(ds, vng), 0.0)
            dbs = dbs + jnp.where(lane == g, jnp.sum(ds, axis=1, keepdims=True), 0.0)
        dbs_ref[...] += dbs
        dvn = dvn_ref[...]
        dln_ref[0:1, :] += jnp.sum(dvn * xhat, axis=0, keepdims=True)
        dln_ref[1:2, :] += jnp.sum(dvn, axis=0, keepdims=True)
        dxh = dvn * lw
        dvb = rstd * (dxh - jnp.mean(dxh, axis=-1, keepdims=True) - xhat * jnp.mean(dxh * xhat, axis=-1, keepdims=True))
        drest_ref[slot, :, AW + BW:AW + 2 * BW] = dvb.astype(BF16)

        out_copy(i, slot).start()

        @pl.when(i == nstep - 1)
        def _():
            out_copy(i, slot).wait()
            if nstep > 1:
                out_copy(i - 1, 1 - slot).wait()
            for cp in ccps:
                cp.wait()

    nstep = T // P
    row = lambda w: pl.BlockSpec((1, w), lambda i: (0, 0))
    any_spec = pl.BlockSpec(memory_space=pl.ANY)
    res = pl.pallas_call(
        body, name="mix_bwd", grid=(nstep,),
        in_specs=[pl.BlockSpec((P, MIX), lambda i: (i, 0)),
                  pl.BlockSpec((P, AW), lambda i: (i, 0)),
                  pl.BlockSpec((P, AW), lambda i: (i, cb)),
                  pl.BlockSpec((P, BW), lambda i: (i, cb + 1)),
                  pl.BlockSpec((P, BW), lambda i: (i, cb + 2)),
                  pl.BlockSpec((P, BW), lambda i: (i, cb + 3)),
                  row(D), row(BW), row(BW),
                  pl.BlockSpec((G, P, P), lambda i: (0, 0, 0)),
                  pl.BlockSpec((P, G), lambda i: (0, 0))] + [any_spec] * nc,
        out_specs=[pl.BlockSpec((P, AW), lambda i: (i, 0)),
                   any_spec,
                   pl.BlockSpec((8, D), lambda i: (0, 0)),
                   pl.BlockSpec((8, BW), lambda i: (0, 0)),
                   pl.BlockSpec((G, P, P), lambda i: (0, 0, 0)),
                   pl.BlockSpec((P, LANES), lambda i: (0, 0))] + [any_spec] * nc,
        out_shape=[S((T, AW), F32), S((T, cb * AW + AW + 3 * BW), BF16), S((8, D), F32), S((8, BW), F32),
                   S((G, P, P), F32), S((P, LANES), F32)] + [S(a.shape[:1] + a.shape[2:], a.dtype) for a in carry],
        scratch_shapes=[pltpu.VMEM((P, BW), F32), pltpu.VMEM((2, P, AW + 3 * BW), BF16),
                        pltpu.SemaphoreType.DMA((2,))] + _sibling_sems(carry),
        compiler_params=_cp(ARB),
    )(d_ocat, o, proj, proj, proj, proj, head_norm_w, ln_w, ln_b, w_sp, bs_t, *carry)
    return res[:6], res[6:]


def _out_proj_loss(ocat, w_out, x, target, fnw):
    T, MIX = ocat.shape
    DM = x.shape[1]
    tm = _tile(T, 256, 8)

    def body(oc_ref, w_ref, x_ref, t_ref, fw_ref, dh_ref, dhb_ref, doc_ref, loss_ref, gfw_ref):
        @pl.when(pl.program_id(0) == 0)
        def _():
            loss_ref[...] = jnp.zeros_like(loss_ref)
            gfw_ref[...] = jnp.zeros_like(gfw_ref)

        wv = w_ref[...]
        hh = x_ref[...] + jnp.dot(oc_ref[...].astype(MXU), wv.astype(MXU), preferred_element_type=F32)
        rs = lax.rsqrt(jnp.mean(hh * hh, axis=-1, keepdims=True) + EPS)
        hn = hh * rs
        fw = fw_ref[...]
        e = hn * fw - t_ref[...]
        row_loss = 0.5 * jnp.mean(e * e, axis=-1, keepdims=True)
        loss_ref[...] += jnp.sum(row_loss, axis=0, keepdims=True)
        dy = e * (1.0 / DM)
        gfw_ref[0:1, :] += jnp.sum(dy * hn, axis=0, keepdims=True)
        dhn = dy * fw
        dh = rs * (dhn - hn * jnp.mean(dhn * hn, axis=-1, keepdims=True))
        dh_ref[...] = dh
        dhb = dh.astype(BF16)
        dhb_ref[...] = dhb
        doc_ref[...] = _mm_nt(dhb, wv)

    return pl.pallas_call(
        body, name="out_proj_loss", grid=(T // tm,),
        in_specs=[pl.BlockSpec((tm, MIX), lambda i: (i, 0)),
                  pl.BlockSpec((MIX, DM), lambda i: (0, 0)),
                  pl.BlockSpec((tm, DM), lambda i: (i, 0)),
                  pl.BlockSpec((tm, DM), lambda i: (i, 0)),
                  pl.BlockSpec((1, DM), lambda i: (0, 0))],
        out_specs=[pl.BlockSpec((tm, DM), lambda i: (i, 0)),
                   pl.BlockSpec((tm, DM), lambda i: (i, 0)),
                   pl.BlockSpec((tm, MIX), lambda i: (i, 0)),
                   pl.BlockSpec((8, LANES), lambda i: (0, 0)),
                   pl.BlockSpec((8, DM), lambda i: (0, 0))],
        out_shape=[S((T, DM), F32), S((T, DM), BF16), S((T, MIX), F32), S((8, LANES), F32), S((8, DM), F32)],
        compiler_params=_cp(ARB),
    )(ocat, w_out, x, target, fnw)


def _grad_w(lhs, rhs, name):
    T, A = lhs.shape
    B = rhs.shape[1]
    ta = _tile(A, 512, LANES)
    tk = _tile(T, 1024, 16)
    nk = T // tk

    def body(l_ref, r_ref, out_ref, acc_ref):
        k = pl.program_id(1)
        part = _mm_tn(l_ref[...], r_ref[...])

        @pl.when(k == 0)
        def _():
            acc_ref[...] = part

        @pl.when(k > 0)
        def _():
            acc_ref[...] += part

        @pl.when(k == nk - 1)
        def _():
            out_ref[...] = acc_ref[...].astype(BF16)

    return pl.pallas_call(
        body, name=name, grid=(A // ta, nk),
        in_specs=[pl.BlockSpec((tk, ta), lambda i, k: (k, i)),
                  pl.BlockSpec((tk, B), lambda i, k: (k, 0))],
        out_specs=pl.BlockSpec((ta, B), lambda i, k: (i, 0)),
        out_shape=S((A, B), BF16),
        scratch_shapes=[pltpu.VMEM((ta, B), F32)],
        compiler_params=_cp(ARB, ARB),
    )(lhs, rhs)


def _grad_w_in(xn, dmain, dba, WD, gate_lo, gate_hi):
    T, DM = xn.shape
    NM = dmain.shape[1]
    tn = _tile(NM, 1024, LANES)
    tk = _tile(T, 2048, 16)
    nj, nk = NM // tn, T // tk
    ND = N_DEV
    tiles = [[] for _ in range(nj)]
    first_tile, last_tile = {}, {}
    for d, s0, s1, dest, c0 in _pieces(WD, gate_lo, gate_hi, ND * WD):
        if dest != "main":
            continue
        while s0 < s1:
            jj = c0 // tn
            w = min(s1 - s0, (jj + 1) * tn - c0)
            tiles[jj].append((d, s0, w, "main", c0 - jj * tn))
            first_tile.setdefault(d, jj)
            last_tile[d] = jj
            s0, c0 = s0 + w, c0 + w
    for d, s0, s1, dest, c0 in _pieces(WD, gate_lo, gate_hi, ND * WD):
        if dest == "gate":
            tiles[first_tile[d]].append((d, s0, s1 - s0, "gate", c0))
    assert sorted(first_tile) == list(range(ND)) and all(last_tile[d] <= first_tile[d + 2] for d in range(ND - 2))

    def body(xn_ref, dm_ref, dba_ref, keep_ref, recv_ref, acc_ref, gate_ref, buf_ref, lsem, ssem, rsem):
        j = pl.program_id(0)
        k = pl.program_id(1)
        px, py, pc = _position()

        @pl.when(k == 0)
        def _():
            acc_ref[...] = jnp.zeros_like(acc_ref)

        @pl.when((j == 0) & (k == 0))
        def _():
            gate_ref[...] = jnp.zeros_like(gate_ref)

        xv = xn_ref[...]
        acc_ref[...] += _mm_tn(xv, dm_ref[...])

        @pl.when(j == 0)
        def _():
            gate_ref[...] += _mm_tn(xv, dba_ref[...])

        def local(d):
            return pltpu.make_async_copy(buf_ref.at[d % 2], keep_ref.at[d // 2], lsem.at[d // 2])

        def remote(d):
            return pltpu.make_async_remote_copy(
                src_ref=buf_ref.at[d % 2], dst_ref=recv_ref.at[d // 2], send_sem=ssem.at[d // 2],
                recv_sem=rsem.at[d // 2], device_id=(px, py, 1 - pc), device_id_type=MESH)

        def leave(d, start):
            @pl.when(pc == d % 2)
            def _():
                local(d).start() if start else local(d).wait()

            @pl.when(pc != d % 2)
            def _():
                remote(d).start() if start else remote(d).wait_send()

        def emit(jj):
            shards = sorted({p[0] for p in tiles[jj]})
            for d in shards:
                if first_tile[d] == jj and d >= 2:
                    leave(d - 2, False)
                for dd, s0, w, src, c0 in tiles[jj]:
                    if dd == d:
                        ref = acc_ref if src == "main" else gate_ref
                        buf_ref[d % 2, :, s0:s0 + w] = ref[:, c0:c0 + w].astype(BF16)
                if last_tile[d] == jj:
                    leave(d, True)
            if jj == nj - 1:
                for d in (ND - 2, ND - 1):
                    leave(d, False)
                for q in range(ND // 2):
                    remote(2 * q).wait_recv()

        for jj in range(nj):
            @pl.when((j == jj) & (k == nk - 1))
            def _(jj=jj):
                emit(jj)

    any_spec = pl.BlockSpec(memory_space=pl.ANY)
    return pl.pallas_call(
        body, name="grad_w_in", grid=(nj, nk),
        in_specs=[pl.BlockSpec((tk, DM), lambda j, k: (k, 0)),
                  pl.BlockSpec((tk, tn), lambda j, k: (k, j)),
                  pl.BlockSpec((tk, LANES), lambda j, k: (k, 0))],
        out_specs=[any_spec, any_spec],
        out_shape=[S((ND // 2, DM, WD), BF16), S((ND // 2, DM, WD), BF16)],
        scratch_shapes=[pltpu.VMEM((DM, tn), F32), pltpu.VMEM((DM, LANES), F32), pltpu.VMEM((2, DM, WD), BF16),
                        pltpu.SemaphoreType.DMA((ND // 2,)), pltpu.SemaphoreType.DMA((ND // 2,)),
                        pltpu.SemaphoreType.DMA((ND // 2,))],
        compiler_params=_cp(ARB, ARB),
    )(xn, dmain, dba)


def _pair_sum_plain(a, b, name):
    K, R, C = a.shape
    tr = _tile(R, 1024, 16)

    def body(a_ref, b_ref, o_ref):
        o_ref[...] = (a_ref[...].astype(F32) + b_ref[...].astype(F32)).astype(BF16)

    spec = lambda: pl.BlockSpec((1, tr, C), lambda q, i: (q, i, 0))
    return pl.pallas_call(body, name=name, grid=(K, R // tr), in_specs=[spec(), spec()], out_specs=spec(),
                          out_shape=S((K, R, C), BF16), compiler_params=_cp(ARB, ARB))(a, b)


def _dx_rows(T):
    tm = _tile(T, 512, 8)
    return tm if T // tm >= 2 else T // 2


def _dx_part(name, dmain, dba, w_main, w_ba, x, dh, norm_w, blk0, nblk, prev, hbm_in, hbm_alias, hbm_new, make_copies):
    T, NM = dmain.shape
    DM = x.shape[1]
    tm = _dx_rows(T)
    tk = _tile(NM, 1024, LANES)
    nk = NM // tk
    n_in, n_al, n_new = len(hbm_in), len(hbm_alias), len(hbm_new)
    n_prev = 0 if prev is None else 2
    last_step = nblk * nk - 1

    def body(dm_ref, dba_ref, w_ref, wba_ref, x_ref, dh_ref, nw_ref, *rest):
        r = list(rest)
        gnw_prev_ref = r.pop(0) if n_prev else None
        if n_prev:
            r.pop(0)
        in_refs = [r.pop(0) for _ in range(n_in)]
        del r[:n_al]
        gx_ref, gnw_ref = r.pop(0), r.pop(0)
        alias_refs = [r.pop(0) for _ in range(n_al)]
        new_refs = [r.pop(0) for _ in range(n_new)]
        acc_ref, send_sems, recv_sems = r
        i = pl.program_id(0)
        k = pl.program_id(1)
        step = i * nk + k
        cps = make_copies(in_refs, alias_refs, new_refs, send_sems, recv_sems)

        @pl.when(step == 0)
        def _():
            gnw_ref[...] = gnw_prev_ref[...] if n_prev else jnp.zeros_like(gnw_ref)
            for cp in cps:
                cp.start()

        @pl.when(k == 0)
        def _():
            acc_ref[...] = _mm_nt(dba_ref[...], wba_ref[...])

        acc_ref[...] += _mm_nt(dm_ref[...], w_ref[...])

        @pl.when(k == nk - 1)
        def _():
            xv = x_ref[...]
            rs = lax.rsqrt(jnp.mean(xv * xv, axis=-1, keepdims=True) + EPS)
            xh = xv * rs
            dxn = acc_ref[...]
            gnw_ref[0:1, :] += jnp.sum(dxn * xh, axis=0, keepdims=True)
            dxh = dxn * nw_ref[...]
            gx_ref[...] = dh_ref[...] + rs * (dxh - xh * jnp.mean(dxh * xh, axis=-1, keepdims=True))

        @pl.when(step == last_step)
        def _():
            for cp in cps:
                cp.wait()

    any_spec = pl.BlockSpec(memory_space=pl.ANY)
    prev_specs = [pl.BlockSpec((8, DM), lambda i, k: (0, 0)), any_spec] if n_prev else []
    prev_args = [prev[1], prev[0]] if n_prev else []
    aliases = {8: 0} if n_prev else {}
    for q in range(n_al):
        aliases[7 + n_prev + n_in + q] = 2 + q
    res = pl.pallas_call(
        body, name=name, grid=(nblk, nk),
        in_specs=[pl.BlockSpec((tm, tk), lambda i, k: (blk0 + i, k)),
                  pl.BlockSpec((tm, LANES), lambda i, k: (blk0 + i, 0)),
                  pl.BlockSpec((DM, tk), lambda i, k: (0, k)),
                  pl.BlockSpec((DM, LANES), lambda i, k: (0, 0)),
                  pl.BlockSpec((tm, DM), lambda i, k: (blk0 + i, 0)),
                  pl.BlockSpec((tm, DM), lambda i, k: (blk0 + i, 0)),
                  pl.BlockSpec((1, DM), lambda i, k: (0, 0))] + prev_specs + [any_spec] * (n_in + n_al),
        out_specs=[pl.BlockSpec((tm, DM), lambda i, k: (blk0 + i, 0)),
                   pl.BlockSpec((8, DM), lambda i, k: (0, 0))] + [any_spec] * (n_al + n_new),
        out_shape=[S((T, DM), F32), S((8, DM), F32)] + [S(a.shape, a.dtype) for a in hbm_alias] + list(hbm_new),
        scratch_shapes=[pltpu.VMEM((tm, DM), F32), pltpu.SemaphoreType.DMA((10,)), pltpu.SemaphoreType.DMA((10,))],
        input_output_aliases=aliases,
        compiler_params=_cp(ARB, ARB),
    )(dmain, dba, w_main, w_ba, x, dh, norm_w, *prev_args, *hbm_in, *hbm_alias)
    return (res[0], res[1]), res[2:2 + n_al], res[2 + n_al:]


def _remote(kk, src, dst, to, send_sems, recv_sems):
    return pltpu.make_async_remote_copy(src_ref=src, dst_ref=dst, send_sem=send_sems.at[kk], recv_sem=recv_sems.at[kk],
                                        device_id=to, device_id_type=MESH)


def _dx(dmain, dba, w_main, w_ba, x, dh, norm_w, chip_sum, small, cut):
    R, C = chip_sum.shape[1:]
    half = R // 2
    assert half % 16 == 0
    T = x.shape[0]
    ni = T // _dx_rows(T)
    cut = max(1, min(cut, ni - 1))
    upper, lower = pl.ds(0, half), pl.ds(half, half)

    def nbrs():
        px, py, pc = _position()
        return (px, py), (1 - px, py, pc), (px, 1 - py, pc)

    def phase1(ins, als, news, ss, rs):
        (px, py), xn, yn = nbrs()
        cs = ins[0]
        recv, stage = news
        bx, by, bd = cs.at[2 * (1 - px) + py], cs.at[2 * px + (1 - py)], cs.at[2 * (1 - px) + (1 - py)]
        return [_remote(0, bx.at[upper], recv.at[0].at[upper], xn, ss, rs),
                _remote(1, by.at[lower], recv.at[1].at[lower], yn, ss, rs),
                _remote(2, bd.at[upper], stage.at[0], xn, ss, rs),
                _remote(3, bd.at[lower], stage.at[1], yn, ss, rs)]

    def phase2(ins, als, news, ss, rs):
        (px, py), xn, yn = nbrs()
        comb, small_ref = ins
        recv, gath = als[0], news[0]
        me, small_cps = _broadcast_copies([small_ref], [gath], _Sem2(ss, 2), _Sem2(rs, 2))
        return ([_remote(0, comb.at[0], recv.at[1].at[upper], yn, ss, rs),
                 _remote(1, comb.at[1], recv.at[0].at[lower], xn, ss, rs)] + small_cps
                + [pltpu.make_async_copy(small_ref, gath.at[me], ss.at[9])])

    (gx, gnw), _, (recv, stage) = _dx_part(
        "dx_a", dmain, dba, w_main, w_ba, x, dh, norm_w, 0, cut, None, [chip_sum], [],
        [S((2, R, C), chip_sum.dtype), S((2, half, C), chip_sum.dtype)], phase1)
    comb = _relay_add(chip_sum, stage)
    (gx, gnw), (recv,), (gath,) = _dx_part(
        "dx_b", dmain, dba, w_main, w_ba, x, dh, norm_w, cut, ni - cut, (gx, gnw), [comb, small], [recv],
        [S((N_DEV,) + small.shape, F32)], phase2)
    return gx, gnw, gath, recv


class _Sem2:
    def __init__(self, sems, lo):
        self.sems, self.lo = sems, lo

    @property
    def at(self):
        outer = self

        class _At:
            def __getitem__(self, idx):
                a, k = idx
                return outer.sems.at[outer.lo + k]
        return _At()


def _relay_add(chip_sum, stage):
    _, R, C = chip_sum.shape
    half = R // 2
    tr = _tile(half, 256, 16)
    nt = half // tr
    px, py, _ = _position()
    idx = jnp.stack([2 * px + (1 - py), 2 * (1 - px) + py]).astype(jnp.int32)

    def body(idx_ref, p_ref, s_ref, o_ref):
        del idx_ref
        o_ref[0] = (p_ref[0].astype(F32) + s_ref[0].astype(F32)).astype(BF16)

    return pl.pallas_call(
        body, name="relay_add",
        grid_spec=pltpu.PrefetchScalarGridSpec(
            num_scalar_prefetch=1, grid=(2, nt),
            in_specs=[pl.BlockSpec((1, tr, C), lambda s, i, idx_ref: (idx_ref[s], s * nt + i, 0)),
                      pl.BlockSpec((1, tr, C), lambda s, i, idx_ref: (s, i, 0))],
            out_specs=pl.BlockSpec((1, tr, C), lambda s, i, idx_ref: (s, i, 0))),
        out_shape=S((2, half, C), BF16), compiler_params=_cp(ARB, ARB),
    )(idx, chip_sum, stage)


def _copy_rows(a, name):
    R, C = a.shape
    tr = _tile(R, 1024, 8)

    def body(a_ref, o_ref):
        o_ref[...] = a_ref[...]

    spec = pl.BlockSpec((tr, C), lambda i: (i, 0))
    return pl.pallas_call(body, name=name, grid=(R // tr,), in_specs=[spec], out_specs=spec,
                          out_shape=S(a.shape, a.dtype), compiler_params=_cp(ARB))(a)


def _sum_slots(gath, shapes):
    spans, outs, r = [], [], 0
    for shp in shapes:
        n = 1
        for s in shp:
            n *= s
        nr = -(-n // (8 * LANES)) * 8
        spans.append((r, nr, n))
        outs.append(S((1, n), F32) if n < LANES else S((nr, LANES), F32))
        r += nr
    assert r == gath.shape[1] and gath.shape[2] == LANES

    def body(g_ref, *o_refs):
        tot = g_ref[0]
        for d in range(1, N_DEV):
            tot = tot + g_ref[d]
        for (r0, nr, n), o_ref in zip(spans, o_refs):
            o_ref[...] = tot[r0:r0 + 1, :n] if n < LANES else tot[r0:r0 + nr]

    vm = pl.BlockSpec(memory_space=pltpu.VMEM)
    res = pl.pallas_call(body, name="sum_slots", in_specs=[vm], out_specs=[vm] * len(outs), out_shape=outs)(gath)
    return [a.reshape(-1)[:n].reshape(shp) for a, (_, _, n), shp in zip(res, spans, shapes)]


def _prep_a_bwd(dq, dk, dv, c, proj, conv_w, dmain, H, D):
    T = c.shape[0]
    AW = H * D
    C3 = 3 * AW
    tb = _tile(T, 256, 8)
    nblk = T // tb
    r8 = tb // 8
    scale = float(D) ** -0.5

    def body(dq_ref, dk_ref, dv_ref, c_ref, dqn_ref, dkn_ref, dvn_ref, cn_ref, x_ref, halo_ref, cw_ref, dmain_in_ref,
             dx_ref, gcw_ref, dc_ref):
        del dmain_in_ref
        i = pl.program_id(0)

        @pl.when(i == 0)
        def _():
            gcw_ref[...] = jnp.zeros_like(gcw_ref)

        def pointwise(rows, dq_r, dk_r, dv_r, c_r, keep):
            for h in range(H):
                for part, d_r, sc in ((0, dq_r, scale), (1, dk_r, 1.0)):
                    sl = slice(part * AW + h * D, part * AW + (h + 1) * D)
                    cv = c_r[:, sl]
                    raw = _silu(cv)
                    rs = lax.rsqrt(jnp.sum(raw * raw, axis=-1, keepdims=True) + EPS)
                    nrm = raw * rs
                    dn = d_r[:, h * D:(h + 1) * D] * sc
                    draw = rs * (dn - nrm * jnp.sum(dn * nrm, axis=-1, keepdims=True))
                    dc_ref[rows, sl] = draw * _dsilu(cv) * keep
            dc_ref[rows, 2 * AW:] = dv_r[...] * _dsilu(c_r[:, 2 * AW:]) * keep

        pointwise(slice(0, tb), dq_ref, dk_ref, dv_ref, c_ref, 1.0)
        pointwise(slice(tb, tb + 8), dqn_ref, dkn_ref, dvn_ref, cn_ref, (i < nblk - 1).astype(F32))

        cw = cw_ref[...]
        dcv = dc_ref[0:tb, :]
        dx = cw[3:4, :] * dcv
        for j in range(3):
            dx = dx + cw[j:j + 1, :] * dc_ref[3 - j:3 - j + tb, :]
        dx_ref[...] = dx.astype(BF16)
        halo = halo_ref[...] * (i > 0).astype(F32)
        xp = jnp.concatenate([halo, x_ref[...]], axis=0)
        for j in range(4):
            gcw_ref[j:j + 1, :] += jnp.sum(dcv * xp[5 + j:5 + j + tb], axis=0, keepdims=True)

    nxt = lambda i: (jnp.minimum((i + 1) * r8, T // 8 - 1), 0)
    return pl.pallas_call(
        body, name="prep_a_bwd", grid=(nblk,),
        in_specs=[pl.BlockSpec((tb, AW), lambda i: (i, 0)),
                  pl.BlockSpec((tb, AW), lambda i: (i, 0)),
                  pl.BlockSpec((tb, AW), lambda i: (i, 0)),
                  pl.BlockSpec((tb, C3), lambda i: (i, 0)),
                  pl.BlockSpec((8, AW), nxt), pl.BlockSpec((8, AW), nxt), pl.BlockSpec((8, AW), nxt),
                  pl.BlockSpec((8, C3), nxt),
                  pl.BlockSpec((tb, C3), lambda i: (i, 0)),
                  pl.BlockSpec((8, C3), lambda i: (jnp.maximum(i * r8 - 1, 0), 0)),
                  pl.BlockSpec((4, C3), lambda i: (0, 0)),
                  pl.BlockSpec(memory_space=pl.ANY)],
        out_specs=[pl.BlockSpec((tb, C3), lambda i: (i, 0)),
                   pl.BlockSpec((8, C3), lambda i: (0, 0))],
        out_shape=[S(dmain.shape, dmain.dtype), S((8, C3), F32)],
        scratch_shapes=[pltpu.VMEM((tb + 8, C3), F32)],
        input_output_aliases={11: 0},
        compiler_params=_cp(ARB),
    )(dq, dk, dv, c, dq, dk, dv, c, proj, proj, conv_w, dmain)


def _adam_math(w, g, m, v):
    m2 = ADAM_B1 * m + (1.0 - ADAM_B1) * g
    v2 = ADAM_B2 * v + (1.0 - ADAM_B2) * (g * g)
    m_hat = m2 / (1.0 - ADAM_B1 ** ADAM_STEP)
    v_hat = v2 / (1.0 - ADAM_B2 ** ADAM_STEP)
    delta = -ADAM_LR * (m_hat / (jnp.sqrt(v_hat) + ADAM_EPS) + ADAM_WD * w)
    return delta, m2, v2


def _pair_sum(blocks, recv, core, name):
    K, _, R, C = blocks.shape
    tr = _tile(R, 256, 16)

    def body(core_ref, a_ref, b_ref, o_ref):
        del core_ref
        o_ref[0] = (a_ref[0, 0].astype(F32) + b_ref[0].astype(F32)).astype(BF16)

    spec = lambda: pl.BlockSpec((1, tr, C), lambda k, i, core_ref: (k, i, 0))
    return pl.pallas_call(
        body, name=name,
        grid_spec=pltpu.PrefetchScalarGridSpec(
            num_scalar_prefetch=1, grid=(K, R // tr),
            in_specs=[pl.BlockSpec((1, 1, tr, C), lambda k, i, core_ref: (k, core_ref[0], i, 0)), spec()],
            out_specs=spec()),
        out_shape=S((K, R, C), BF16), compiler_params=_cp(ARB, ARB),
    )(core, blocks, recv)


def _sum_adam(chip_sums, recv, w, m, v, chip, name, transposed=False):
    R, C = chip_sums.shape[1:]
    NR = recv.shape[0]
    tr = _tile(R, min(256, max(R // 4, 16)), 16)

    def body(chip_ref, own_ref, r_ref, w_ref, m_ref, v_ref, g_ref, d_ref, m2_ref, v2_ref):
        del chip_ref
        g = own_ref[0].astype(F32)
        for j in range(NR):
            g = g + r_ref[j].astype(F32)
        if transposed:
            g = g.T
        g_ref[...] = g
        d_ref[...], m2_ref[...], v2_ref[...] = _adam_math(w_ref[...], g, m_ref[...], v_ref[...])

    if transposed:
        spec = lambda: pl.BlockSpec((C, tr), lambda i, chip_ref: (0, i))
        shape = (C, R)
    else:
        spec = lambda: pl.BlockSpec((tr, C), lambda i, chip_ref: (i, 0))
        shape = (R, C)
    assert w.shape == shape
    return pl.pallas_call(
        body, name=name,
        grid_spec=pltpu.PrefetchScalarGridSpec(
            num_scalar_prefetch=1, grid=(R // tr,),
            in_specs=[pl.BlockSpec((1, tr, C), lambda i, chip_ref: (chip_ref[0], i, 0)),
                      pl.BlockSpec((NR, tr, C), lambda i, chip_ref: (0, i, 0)), spec(), spec(), spec()],
            out_specs=[spec(), spec(), spec(), spec()]),
        out_shape=[S(shape, F32)] * 4, compiler_params=_cp(ARB),
    )(chip, chip_sums, recv, w, m, v)


def _adam_small(ws, gs, ms, vs):
    n = len(ws)

    def body(*refs):
        ins, outs = refs[:4 * n], refs[4 * n:]
        for p in range(n):
            w_ref, g_ref, m_ref, v_ref = (ins[a * n + p] for a in range(4))
            outs[p][...], outs[n + p][...], outs[2 * n + p][...] = _adam_math(
                w_ref[...], g_ref[...], m_ref[...], v_ref[...])

    vm = pl.BlockSpec(memory_space=pltpu.VMEM)
    res = pl.pallas_call(
        body, name="adam_small", in_specs=[vm] * (4 * n), out_specs=[vm] * (3 * n),
        out_shape=[S(w.shape, F32) for w in ws] * 3,
    )(*ws, *gs, *ms, *vs)
    return res[:n], res[n:2 * n], res[2 * n:]


def _position():
    return lax.axis_index("x"), lax.axis_index("y"), lax.axis_index("c")


def _all_gather_weights(arr, x_in, norm_w, chip, tn, plans, nm):
    R = arr.shape[0]
    half = R // 2
    assert half % 16 == 0
    T, DM = x_in.shape
    tm = _tile(T, 512, 16)
    nstep = T // tm

    def body(chip_ref, x_ref, nw_ref, in_ref, xn_ref, out_ref, proj_ref, wtile_ref, stage_ref,
             send_sems, recv_sems, local_sem, stage_sems):
        i = pl.program_id(0)
        x, y, c = _position()
        me, sibling = (x, y, c), (x, y, 1 - c)
        xn, yn, diag = (1 - x, y), (x, 1 - y), (1 - x, 1 - y)
        upper, lower = pl.ds(0, half), pl.ds(half, half)

        def slot(p, rows=None):
            ref = out_ref.at[4 * p[0] + 2 * p[1] + p[2]]
            return ref if rows is None else ref.at[rows]

        def copy(kk, block, to, rows=None, src=None):
            return pltpu.make_async_remote_copy(
                src_ref=slot(block, rows) if src is None else src, dst_ref=slot(block, rows),
                send_sem=send_sems.at[kk], recv_sem=recv_sems.at[kk], device_id=to, device_id_type=MESH)

        mine = pltpu.make_async_copy(in_ref, slot(me), local_sem)
        first = [copy(0, me, sibling, src=in_ref), copy(1, me, (*xn, c), src=in_ref), copy(2, me, (*yn, c), src=in_ref)]

        @pl.when(i == 0)
        def _():
            mine.start()
            for cp in first:
                cp.start()
            copy(0, sibling, me).wait_recv()
            loads = [pltpu.make_async_copy(in_ref, stage_ref.at[c], stage_sems.at[0]),
                     pltpu.make_async_copy(slot(sibling), stage_ref.at[1 - c], stage_sems.at[1])]
            for cp in loads:
                cp.start()
            for cp in loads:
                cp.wait()
            for m, plan in enumerate(plans):
                @pl.when(chip_ref[0] == m)
                def _(plan=plan):
                    for d, s0, w, c0 in plan:
                        wtile_ref[:, c0:c0 + w] = stage_ref[d % 2, :, s0:s0 + w]

        xv = x_ref[...]
        r = lax.rsqrt(jnp.mean(xv * xv, axis=-1, keepdims=True) + EPS)
        xnv = (xv * r * nw_ref[...]).astype(BF16)
        xn_ref[...] = xnv
        proj_ref[...] = jnp.dot(xnv.astype(MXU), wtile_ref[...].astype(MXU), preferred_element_type=F32)

        @pl.when(i == nstep - 1)
        def _():
            sent = list(first)

            def then(cps):
                for cp in cps:
                    cp.start()
                sent.extend(cps)

            copy(1, (*xn, c), me).wait_recv()
            then([copy(5, (*xn, c), (*yn, c), rows=upper), copy(3, (*xn, c), sibling)])
            copy(2, (*yn, c), me).wait_recv()
            then([copy(6, (*yn, c), (*xn, c), rows=lower), copy(4, (*yn, c), sibling)])
            copy(5, (*diag, c), me, rows=upper).wait_recv()
            then([copy(7, (*diag, c), sibling, rows=upper)])
            copy(6, (*diag, c), me, rows=lower).wait_recv()
            then([copy(8, (*diag, c), sibling, rows=lower)])
            copy(3, (*xn, 1 - c), me).wait_recv()
            copy(4, (*yn, 1 - c), me).wait_recv()
            copy(7, (*diag, 1 - c), me, rows=upper).wait_recv()
            copy(8, (*diag, 1 - c), me, rows=lower).wait_recv()
            for cp in sent:
                cp.wait_send()
            mine.wait()

    any_spec = pl.BlockSpec(memory_space=pl.ANY)
    return pl.pallas_call(
        body, name="all_gather_weights",
        grid_spec=pltpu.PrefetchScalarGridSpec(
            num_scalar_prefetch=1, grid=(nstep,),
            in_specs=[pl.BlockSpec((tm, DM), lambda i, chip_ref: (i, 0)),
                      pl.BlockSpec((1, DM), lambda i, chip_ref: (0, 0)), any_spec],
            out_specs=[pl.BlockSpec((tm, DM), lambda i, chip_ref: (i, 0)), any_spec,
                       pl.BlockSpec((tm, tn), lambda i, chip_ref: (i, 2 * chip_ref[0]))],
            scratch_shapes=[pltpu.VMEM((DM, tn), arr.dtype), pltpu.VMEM((2,) + arr.shape, arr.dtype),
                            pltpu.SemaphoreType.DMA((9,)), pltpu.SemaphoreType.DMA((9,)), pltpu.SemaphoreType.DMA,
                            pltpu.SemaphoreType.DMA((2,))]),
        out_shape=[S((T, DM), BF16), S((N_DEV,) + arr.shape, arr.dtype), S((T, nm), F32)],
        compiler_params=_cp(ARB),
    )(chip, x_in, norm_w, arr)


def _sibling_copies(ins, outs, send_sems, recv_sems):
    x, y, c = _position()
    return [pltpu.make_async_remote_copy(src_ref=ins[a].at[k, 1 - c], dst_ref=outs[a].at[k],
                                         send_sem=send_sems.at[a, k], recv_sem=recv_sems.at[a, k],
                                         device_id=(x, y, 1 - c), device_id_type=MESH)
            for a in range(len(ins)) for k in range(ins[a].shape[0])]


def _sibling_sems(arrs):
    shape = (max(len(arrs), 1), arrs[0].shape[0] if arrs else 1)
    return [pltpu.SemaphoreType.DMA(shape), pltpu.SemaphoreType.DMA(shape)]


def _chip_exchange_copies(ins, outs, send_sems, recv_sems):
    x, y, c = _position()
    chips = [(1 - x, y), (x, 1 - y), (1 - x, 1 - y)]
    return [pltpu.make_async_remote_copy(
        src_ref=ins[a].at[2 * qx + qy], dst_ref=outs[a].at[j], send_sem=send_sems.at[a, j],
        recv_sem=recv_sems.at[a, j], device_id=(qx, qy, c), device_id_type=MESH)
        for a in range(len(ins)) for j, (qx, qy) in enumerate(chips)]


def _broadcast_copies(srcs, dsts, send_sems, recv_sems):
    x, y, c = _position()
    me = 4 * x + 2 * y + c
    cps = []
    for a in range(len(srcs)):
        for k in range(1, N_DEV):
            peer = (1 - x if k & 4 else x, 1 - y if k & 2 else y, 1 - c if k & 1 else c)
            cps.append(pltpu.make_async_remote_copy(
                src_ref=srcs[a], dst_ref=dsts[a].at[me], send_sem=send_sems.at[a, k - 1],
                recv_sem=recv_sems.at[a, k - 1], device_id=peer, device_id_type=MESH))
    return me, cps


def _all_reduce_small(part):
    R, C = part.shape

    def body(p_ref, out_ref, gath_ref, send_sems, recv_sems):
        me, cps = _broadcast_copies([p_ref], [gath_ref], send_sems, recv_sems)
        gath_ref[me] = p_ref[...]
        for cp in cps:
            cp.start()
        for cp in cps:
            cp.wait()
        acc = gath_ref[0]
        for d in range(1, N_DEV):
            acc = acc + gath_ref[d]
        out_ref[...] = acc

    vm = pl.BlockSpec(memory_space=pltpu.VMEM)
    return pl.pallas_call(
        body, name="all_reduce_small", in_specs=[vm], out_specs=vm, out_shape=S((R, C), F32),
        scratch_shapes=[pltpu.VMEM((N_DEV, R, C), F32), pltpu.SemaphoreType.DMA((1, N_DEV - 1)),
                        pltpu.SemaphoreType.DMA((1, N_DEV - 1))],
    )(part)


def _pack(parts):
    rows = []
    for p in parts:
        f = p.reshape(-1).astype(F32)
        pad = (-f.shape[0]) % (8 * LANES)
        rows.append(jnp.pad(f, (0, pad)).reshape(-1, LANES))
    return jnp.concatenate(rows, axis=0)


def _unpack(buf, shapes):
    out, r = [], 0
    for shp in shapes:
        n = 1
        for s in shp:
            n *= s
        nr = -(-n // (8 * LANES)) * 8
        out.append(buf[r:r + nr].reshape(-1)[:n].reshape(shp))
        r += nr
    return out


def kernel(x, norm_w, w_in, conv_w, a_log, dt_bias, head_norm_w, sgu_ln_w, sgu_ln_b, w_spatial, b_spatial, w_out, final_norm_w, loss_target, m_norm_w, m_w_in, m_conv_w, m_a_log, m_dt_bias, m_head_norm_w, m_sgu_ln_w, m_sgu_ln_b, m_w_spatial, m_b_spatial, m_w_out, m_final_norm_w, v_norm_w, v_w_in, v_conv_w, v_a_log, v_dt_bias, v_head_norm_w, v_sgu_ln_w, v_sgu_ln_b, v_w_spatial, v_b_spatial, v_w_out, v_final_norm_w):
    T, DM = x.shape[1], x.shape[2]
    H, D = a_log.shape[1], head_norm_w.shape[1]
    G, P = w_spatial.shape[1], w_spatial.shape[2]
    AW, BW = H * D, G * P
    MIX = AW + BW
    WD = w_in.shape[2]
    IN = N_DEV * WD
    RO = w_out.shape[1]
    CW = conv_w.shape[2]
    sizes = (3 * AW, AW, H, H, BW, BW, BW)
    assert sum(sizes) == IN and 2 * H <= LANES and 3 * H <= 32 and N_DEV * RO == MIX and N_DEV * CW == 3 * AW
    offs = [0]
    for s in sizes:
        offs.append(offs[-1] + s)
    px, py, pc = _position()
    dev = 4 * px + 2 * py + pc
    chip = 2 * px + py

    x2, tgt = x[0], loss_target[0]

    core_idx = jnp.reshape(pc, (1,)).astype(jnp.int32)
    chip_idx = jnp.reshape(chip, (1,)).astype(jnp.int32)
    NM = IN - 2 * H
    tn_loc, tile_plans = _local_tiles(WD, offs[2], offs[4], NM)
    xn, g_win, proj_part = _all_gather_weights(
        _cast_bf16_t(w_in[0].T, "cast_w_in"), x2, norm_w, chip_idx, tn_loc, tile_plans, NM)
    w_main, w_ba = _relayout_w(g_win, offs[2], offs[4])
    alog_row = jnp.pad(a_log, ((0, 0), (H, LANES - 2 * H)))
    dtb_row = jnp.pad(dt_bias, ((0, 0), (H, LANES - 2 * H)))
    bs_t = b_spatial[0].T

    others = jnp.arange(N_DEV - 2, dtype=jnp.int32)
    others = others + (others >= 2 * chip).astype(jnp.int32)
    proj, ba, (g_wout, g_conv) = _in_proj(xn, w_main, w_ba, proj_part, others, tn_loc,
                                          [_cast_bf16(w_out[0], "cast_w_out"), conv_w[0]])
    w_out_full = g_wout.reshape(MIX, DM)
    conv_full = g_conv.transpose(1, 0, 2).reshape(4, 3 * AW)
    q, k, v, c, gcol, grow = _prep_a_fwd(proj, ba, conv_full, alog_row, dtb_row, H, D)
    o, vnew, ssave, asave = _delta_fwd(q, k, v, gcol, grow, H, D)
    ocat = _mix_fwd(o, proj, head_norm_w, sgu_ln_w, sgu_ln_b, w_spatial[0], bs_t, H, D, G, P)
    dh, dh_bf, d_ocat, loss_acc, g_fnw = _out_proj_loss(ocat, w_out_full, x2, tgt, final_norm_w.reshape(1, DM))

    g_wout_blocks = _grad_w(ocat, dh_bf, "grad_w_out").reshape(4, 2, RO, DM)
    (d_o, dmain, g_hnw, g_ln, g_wsp, g_bs_t), (sib_wout,) = _mix_bwd(
        d_ocat, o, proj, head_norm_w, sgu_ln_w, sgu_ln_b, w_spatial[0], bs_t, H, D, G, P, [g_wout_blocks])
    chip_wout = _pair_sum(g_wout_blocks, sib_wout, core_idx, "pair_sum_w_out")
    (dq, dk, dv, dba, dpar), (recv_wout,) = _delta_bwd(
        q, k, v, gcol, grow, ba, vnew, ssave, asave, d_o, alog_row, dtb_row, H, D, [chip_wout])
    dmain, g_conv_part = _prep_a_bwd(dq, dk, dv, c, proj, conv_full, dmain, H, D)
    keep_win, sib_win = _grad_w_in(xn, dmain, dba, WD, offs[2], offs[4])
    chip_win = _pair_sum_plain(keep_win, sib_win, "pair_sum_w_in")
    small_shapes = [a_log.shape, dt_bias.shape, head_norm_w.shape, sgu_ln_w.shape, sgu_ln_b.shape,
                    w_spatial.shape, b_spatial.shape, final_norm_w.shape]
    parts = [dpar[0, H:2 * H], dpar[1, H:2 * H], g_hnw[0], g_ln[0], g_ln[1], g_wsp, g_bs_t[:, :G].T, g_fnw[0],
             g_conv_part[:4], loss_acc[0, :1]]
    grad_x, g_nw, small_gath, recv_win = _dx(dmain, dba, w_main, w_ba, x2, dh, norm_w, chip_win, _pack(parts), 4)
    red = _sum_slots(small_gath, small_shapes + [(4, 3 * AW), (1,)])
    grad_w_in, delta_w_in, new_m_w_in, new_v_w_in = _sum_adam(
        chip_win, recv_win, w_in[0].T, m_w_in[0].T, v_w_in[0].T, chip_idx, "sum_adam_w_in", transposed=True)
    grad_x = _copy_rows(grad_x, "copy_grad_x")
    grad_w_out, delta_w_out, new_m_w_out, new_v_w_out = _sum_adam(
        chip_wout, recv_wout, w_out[0], m_w_out[0], v_w_out[0], chip_idx, "sum_adam_w_out")
    red_nw = _all_reduce_small(_pack([g_nw[0]]))
    grads_small = _unpack(red_nw, [norm_w.shape]) + red
    loss = grads_small.pop()[0]
    g_conv_full = grads_small.pop()
    grad_conv = lax.dynamic_slice_in_dim(g_conv_full, dev * CW, CW, axis=1)[None]
    small_w = [norm_w, a_log, dt_bias, head_norm_w, sgu_ln_w, sgu_ln_b, w_spatial, b_spatial, final_norm_w, conv_w]
    small_m = [m_norm_w, m_a_log, m_dt_bias, m_head_norm_w, m_sgu_ln_w, m_sgu_ln_b, m_w_spatial, m_b_spatial,
               m_final_norm_w, m_conv_w]
    small_v = [v_norm_w, v_a_log, v_dt_bias, v_head_norm_w, v_sgu_ln_w, v_sgu_ln_b, v_w_spatial, v_b_spatial,
               v_final_norm_w, v_conv_w]
    small_g = grads_small + [grad_conv]
    d_s, m_s, v_s = _adam_small(small_w, small_g, small_m, small_v)

    def order(small, win, wout):
        return [small[0], win.T[None], small[9], small[1], small[2], small[3], small[4], small[5], small[6], small[7],
                wout[None], small[8]]

    grads = order(small_g, grad_w_in, grad_w_out)
    deltas = order(d_s, delta_w_in, delta_w_out)
    new_m = order(m_s, new_m_w_in, new_m_w_out)
    new_v = order(v_s, new_v_w_in, new_v_w_out)
    return (loss, grad_x[None], *grads, *deltas, *new_m, *new_v)
```

```python
import jax
import jax.numpy as jnp
from jax import lax
from jax.experimental import pallas as pl
from jax.experimental.pallas import tpu as pltpu

F32 = jnp.float32
BF16 = jnp.bfloat16
MXU = jnp.bfloat16
HI = lax.Precision.HIGHEST
EPS = 1e-6
CHUNK_A = 64
LANES = 128
MESH = pl.DeviceIdType.MESH
N_DEV = 8

ADAM_LR = 0.001
ADAM_B1 = 0.9
ADAM_B2 = 0.999
ADAM_EPS = 1e-08
ADAM_WD = 0.01
ADAM_STEP = 10

S = jax.ShapeDtypeStruct
ARB = "arbitrary"


def _cp(*sem, vmem_mib=56):
    return pltpu.CompilerParams(dimension_semantics=tuple(sem), vmem_limit_bytes=vmem_mib * 1024 * 1024)


def _tile(n, cap, mult):
    best = None
    t = mult
    while t <= min(n, cap):
        if n % t == 0:
            best = t
        t += mult
    return best if best is not None else n


def _mm(a, b):
    return jnp.dot(a.astype(MXU), b.astype(MXU), preferred_element_type=F32)


def _mm_nt(a, b):
    return lax.dot_general(a.astype(MXU), b.astype(MXU), (((1,), (1,)), ((), ())), preferred_element_type=F32)


def _mm_tn(a, b):
    return lax.dot_general(a.astype(MXU), b.astype(MXU), (((0,), (0,)), ((), ())), preferred_element_type=F32)


def _mmh(a, b):
    return jnp.dot(a, b, precision=HI, preferred_element_type=F32)


def _sigmoid(x):
    return 1.0 / (1.0 + jnp.exp(-x))


def _silu(x):
    return x * _sigmoid(x)


def _dsilu(x):
    s = _sigmoid(x)
    return s * (1.0 + x * (1.0 - s))


def _softplus(x):
    return jnp.maximum(x, 0.0) + jnp.log(1.0 + jnp.exp(-jnp.abs(x)))


def _pieces(wd, gate_lo, gate_hi, total):
    out = []
    for d in range(N_DEV):
        lo, hi = d * wd, (d + 1) * wd
        for dest, a, b, shift in (("main", 0, gate_lo, 0), ("gate", gate_lo, gate_hi, -gate_lo),
                                  ("main", gate_hi, total, gate_lo - gate_hi)):
            s0, s1 = max(lo, a), min(hi, b)
            if s0 < s1:
                out.append((d, s0 - lo, s1 - lo, dest, s0 + shift))
    return out


def _local_tiles(wd, gate_lo, gate_hi, nm):
    n_tiles = N_DEV - 1
    assert nm % (n_tiles * LANES) == 0
    tn = nm // n_tiles
    plans = []
    for m in range(N_DEV // 2):
        lo, hi = 2 * m * tn, (2 * m + 1) * tn
        plan = []
        for d, s0, s1, dest, c0 in _pieces(wd, gate_lo, gate_hi, N_DEV * wd):
            if dest != "main":
                continue
            a, b = max(c0, lo), min(c0 + (s1 - s0), hi)
            if a < b:
                assert d // 2 == m, "tile 2m must come from chip m's own shards"
                plan.append((d, s0 + (a - c0), b - a, a - lo))
        assert sum(p[2] for p in plan) == tn
        plans.append(plan)
    return tn, plans


def _cast_bf16(a, name):
    R, C = a.shape
    tr = _tile(R, 256, 16)

    def body(a_ref, o_ref):
        o_ref[...] = a_ref[...].astype(BF16)

    spec = pl.BlockSpec((tr, C), lambda i: (i, 0))
    return pl.pallas_call(body, name=name, grid=(R // tr,), in_specs=[spec], out_specs=spec,
                          out_shape=S((R, C), BF16), compiler_params=_cp(ARB))(a)


def _cast_bf16_t(a_t, name):
    C, R = a_t.shape
    tr = _tile(R, 256, LANES)

    def body(a_ref, o_ref):
        o_ref[...] = a_ref[...].T.astype(BF16)

    return pl.pallas_call(body, name=name, grid=(R // tr,), in_specs=[pl.BlockSpec((C, tr), lambda i: (0, i))],
                          out_specs=pl.BlockSpec((tr, C), lambda i: (i, 0)),
                          out_shape=S((R, C), BF16), compiler_params=_cp(ARB))(a_t)


def _relayout_w(g_win, gate_lo, gate_hi):
    _, DM, WD = g_win.shape
    total = N_DEV * WD
    NM = total - (gate_hi - gate_lo)
    tr = _tile(DM, 512, 16)
    plan = _pieces(WD, gate_lo, gate_hi, total)

    def body(g_ref, main_ref, gate_ref):
        gate_ref[...] = jnp.zeros_like(gate_ref)
        for d, s0, s1, dest, c0 in plan:
            dst = main_ref if dest == "main" else gate_ref
            dst[:, c0:c0 + (s1 - s0)] = g_ref[d, :, s0:s1]

    return pl.pallas_call(
        body, name="relayout_w", grid=(DM // tr,),
        in_specs=[pl.BlockSpec((N_DEV, tr, WD), lambda i: (0, i, 0))],
        out_specs=[pl.BlockSpec((tr, NM), lambda i: (i, 0)), pl.BlockSpec((tr, LANES), lambda i: (i, 0))],
        out_shape=[S((DM, NM), g_win.dtype), S((DM, LANES), g_win.dtype)],
        compiler_params=_cp(ARB),
    )(g_win)


def _in_proj(xn, w_main, w_ba, proj_part, tiles, tn, shards):
    T, DM = xn.shape
    NM = w_main.shape[1]
    tm = _tile(T, 2048, 16)
    ni, nj = T // tm, tiles.shape[0]
    ns = len(shards)

    def body(tiles_ref, xn_ref, w_ref, wba_ref, part_ref, *rest):
        del tiles_ref, part_ref
        srcs = rest[:ns]
        proj_ref, ba_ref = rest[ns:ns + 2]
        gath = rest[ns + 2:2 * ns + 2]
        send_sems, recv_sems, local_sems = rest[2 * ns + 2:]
        i = pl.program_id(0)
        j = pl.program_id(1)
        me, cps = _broadcast_copies(srcs, gath, send_sems, recv_sems)
        cps = cps + [pltpu.make_async_copy(srcs[a], gath[a].at[me], local_sems.at[a]) for a in range(ns)]

        @pl.when((i == 0) & (j == 0))
        def _():
            for cp in cps:
                cp.start()

        @pl.when(j == 0)
        def _():
            ba_ref[...] = jnp.dot(xn_ref[...].astype(MXU), wba_ref[...].astype(MXU), preferred_element_type=F32)

        proj_ref[...] = jnp.dot(xn_ref[...].astype(MXU), w_ref[...].astype(MXU), preferred_element_type=F32)

        @pl.when((i == ni - 1) & (j == nj - 1))
        def _():
            for cp in cps:
                cp.wait()

    any_spec = pl.BlockSpec(memory_space=pl.ANY)
    res = pl.pallas_call(
        body, name="in_proj",
        grid_spec=pltpu.PrefetchScalarGridSpec(
            num_scalar_prefetch=1, grid=(ni, nj),
            in_specs=[pl.BlockSpec((tm, DM), lambda i, j, t: (i, 0)),
                      pl.BlockSpec((DM, tn), lambda i, j, t: (0, t[j])),
                      pl.BlockSpec((DM, LANES), lambda i, j, t: (0, 0)), any_spec] + [any_spec] * ns,
            out_specs=[pl.BlockSpec((tm, tn), lambda i, j, t: (i, t[j])),
                       pl.BlockSpec((tm, LANES), lambda i, j, t: (i, 0))] + [any_spec] * ns,
            scratch_shapes=[pltpu.SemaphoreType.DMA((ns, N_DEV - 1)), pltpu.SemaphoreType.DMA((ns, N_DEV - 1)),
                            pltpu.SemaphoreType.DMA((ns,))]),
        out_shape=[S((T, NM), F32), S((T, LANES), F32)] + [S((N_DEV,) + a.shape, a.dtype) for a in shards],
        input_output_aliases={4: 0},
        compiler_params=_cp(ARB, ARB, vmem_mib=58),
    )(tiles, xn, w_main, w_ba, proj_part, *shards)
    return res[0], res[1], res[2:]


def _prep_a_fwd(proj, ba, conv_w, alog_row, dtb_row, H, D):
    T = proj.shape[0]
    AW = H * D
    C3 = 3 * AW
    tb = _tile(T, 256, CHUNK_A)
    nch = tb // CHUNK_A
    nblk = T // tb
    scale = float(D) ** -0.5

    def body(x_ref, halo_ref, ba_ref, cw_ref, al_ref, dt_ref, q_ref, k_ref, v_ref, c_ref, gcol_ref, grow_ref):
        i = pl.program_id(0)
        xv = x_ref[...]
        halo = halo_ref[...] * (i > 0).astype(F32)
        xp = jnp.concatenate([halo, xv], axis=0)
        cw = cw_ref[...]
        c = cw[0:1, :] * xp[5:5 + tb]
        for j in range(1, 4):
            c = c + cw[j:j + 1, :] * xp[5 + j:5 + j + tb]
        c_ref[...] = c
        a = _silu(c)
        for h in range(H):
            qh = a[:, h * D:(h + 1) * D]
            kh = a[:, AW + h * D:AW + (h + 1) * D]
            qr = lax.rsqrt(jnp.sum(qh * qh, axis=-1, keepdims=True) + EPS)
            kr = lax.rsqrt(jnp.sum(kh * kh, axis=-1, keepdims=True) + EPS)
            q_ref[:, h * D:(h + 1) * D] = qh * (qr * scale)
            k_ref[:, h * D:(h + 1) * D] = kh * kr
        v_ref[...] = a[:, 2 * AW:]

        bav = ba_ref[...]
        lane = lax.broadcasted_iota(jnp.int32, (tb, LANES), 1)
        beta = _sigmoid(bav)
        g = -jnp.exp(al_ref[...]) * _softplus(bav + dt_ref[...])
        gates = jnp.where(lane < H, beta, jnp.where(lane < 2 * H, g, 0.0))
        ri = lax.broadcasted_iota(jnp.int32, (CHUNK_A, CHUNK_A), 0)
        ci = lax.broadcasted_iota(jnp.int32, (CHUNK_A, CHUNK_A), 1)
        tri = (ri >= ci).astype(F32)
        lane_c = lax.broadcasted_iota(jnp.int32, (CHUNK_A, LANES), 1)
        for cc in range(nch):
            gch = gates[cc * CHUNK_A:(cc + 1) * CHUNK_A]
            gc = pltpu.roll(_mmh(tri, gch), H, 1)
            full = jnp.where(lane_c < 2 * H, gch, jnp.where(lane_c < 3 * H, gc, 0.0))
            gcol_ref[cc * CHUNK_A:(cc + 1) * CHUNK_A, :] = full
            grow_ref[cc] = full.T[0:32, :]

    return pl.pallas_call(
        body, name="prep_a_fwd", grid=(nblk,),
        in_specs=[pl.BlockSpec((tb, C3), lambda i: (i, 0)),
                  pl.BlockSpec((8, C3), lambda i: (jnp.maximum(i * (tb // 8) - 1, 0), 0)),
                  pl.BlockSpec((tb, LANES), lambda i: (i, 0)),
                  pl.BlockSpec((4, C3), lambda i: (0, 0)),
                  pl.BlockSpec((1, LANES), lambda i: (0, 0)),
                  pl.BlockSpec((1, LANES), lambda i: (0, 0))],
        out_specs=[pl.BlockSpec((tb, AW), lambda i: (i, 0)),
                   pl.BlockSpec((tb, AW), lambda i: (i, 0)),
                   pl.BlockSpec((tb, AW), lambda i: (i, 0)),
                   pl.BlockSpec((tb, C3), lambda i: (i, 0)),
                   pl.BlockSpec((tb, LANES), lambda i: (i, 0)),
                   pl.BlockSpec((nch, 32, CHUNK_A), lambda i: (i, 0, 0))],
        out_shape=[S((T, AW), F32), S((T, AW), F32), S((T, AW), F32), S((T, C3), F32),
                   S((T, LANES), F32), S((T // CHUNK_A, 32, CHUNK_A), F32)],
        compiler_params=_cp(ARB),
    )(proj, proj, ba, conv_w, alog_row, dtb_row)


_NN = (((1,), (0,)), ((), ()))
_TN = (((0,), (0,)), ((), ()))


def _split(a):
    hi = a.astype(BF16)
    return hi, (a - hi.astype(F32)).astype(BF16)


def _mm3(a, b, dims=_NN):
    ah, al = a if isinstance(a, tuple) else _split(a)
    bh, bl = b if isinstance(b, tuple) else _split(b)
    dg = lambda p, r: lax.dot_general(p, r, dims, preferred_element_type=F32)
    return dg(ah, bh) + (dg(ah, bl) + dg(al, bh))


def _interleave(gens):
    gens = list(gens)
    while gens:
        alive = []
        for g in gens:
            try:
                next(g)
                alive.append(g)
            except StopIteration:
                pass
        gens = alive


def _chunk_terms(q, k, v, gcolv, growv, h, H):
    C = CHUNK_A
    beta_c = gcolv[:, h:h + 1]
    g_c = gcolv[:, H + h:H + h + 1]
    gc_c = gcolv[:, 2 * H + h:2 * H + h + 1]
    gc_r = growv[2 * H + h:2 * H + h + 1, :]
    ri = lax.broadcasted_iota(jnp.int32, (C, C), 0)
    ci = lax.broadcasted_iota(jnp.int32, (C, C), 1)
    incl = ri >= ci
    strict = ri > ci
    kb = k * beta_c
    vb = v * beta_c
    p_raw = _mm_nt(kb, k)
    qk_raw = _mm_nt(q, k)
    gam = jnp.where(incl, jnp.exp(jnp.where(incl, gc_c - gc_r, 0.0)), 0.0)
    e_c = jnp.exp(gc_c)
    gl = gc_r[:, C - 1:C]
    edec = jnp.exp(gl - gc_c)
    yield
    lmat = jnp.where(strict, p_raw * gam, 0.0)
    attn = jnp.where(incl, qk_raw * gam, 0.0)
    return dict(beta_c=beta_c, g_c=g_c, gc_c=gc_c, gc_r=gc_r, incl=incl, strict=strict, gam=gam, e_c=e_c,
                kb=kb, vb=vb, lmat=lmat, attn=attn, gl=gl, edec=edec, ri=ri, ci=ci)


INV_BLOCK = 16


def _inv_unit_lower(lmat):
    C = lmat.shape[0]
    ri = lax.broadcasted_iota(jnp.int32, (C, C), 0)
    ci = lax.broadcasted_iota(jnp.int32, (C, C), 1)
    eye = (ri == ci).astype(F32)
    same = (ri // INV_BLOCK) == (ci // INV_BLOCK)

    def neumann(x, order):
        a = eye + x
        n = 1
        while 2 * n < order:
            xs = _split(x)
            x = _mm3(xs, xs)
            yield
            a = a + _mm3(a, x)
            n *= 2
        yield
        return a

    inv_d = yield from neumann(-jnp.where(same, lmat, 0.0), INV_BLOCK)
    m = _mm3(inv_d, jnp.where(same, 0.0, lmat))
    yield
    inv_m = yield from neumann(-m, C // INV_BLOCK)
    a = _mm3(inv_m, inv_d)
    yield
    return a


def _delta_fwd(q, k, v, gcol, grow, H, D):
    T = q.shape[0]
    C = CHUNK_A
    N = T // C
    AW = H * D
    CPS = 2 if N % 2 == 0 else 1

    def body(q_ref, k_ref, v_ref, gcol_ref, grow_ref, o_ref, vn_ref, ssave_ref, asave_ref, s_ref):
        @pl.when(pl.program_id(0) == 0)
        def _():
            s_ref[...] = jnp.zeros_like(s_ref)

        state = {(0, h): s_ref[h] for h in range(H)}

        def head(cc, h):
            rows = slice(cc * C, (cc + 1) * C)
            sl = slice(h * D, (h + 1) * D)
            qv, kv, vv = q_ref[rows, sl], k_ref[rows, sl], v_ref[rows, sl]
            t = yield from _chunk_terms(qv, kv, vv, gcol_ref[rows, :], grow_ref[cc], h, H)
            a = yield from _inv_unit_lower(t["lmat"])
            asave_ref[cc, h] = a
            while (cc, h) not in state:
                yield
            st = state[(cc, h)]
            ssave_ref[cc, h] = st
            ks = _mm(t["kb"] * t["e_c"], st)
            o_inter = _mm(qv * t["e_c"], st)
            yield
            v_new = _mm3(a, t["vb"] - ks)
            yield
            vn_ref[rows, sl] = v_new
            o_intra = _mm(t["attn"], v_new)
            s_upd = _mm_tn(kv * t["edec"], v_new)
            yield
            o_ref[rows, sl] = o_inter + o_intra
            state[(cc + 1, h)] = st * jnp.exp(t["gl"]) + s_upd

        _interleave(head(cc, h) for cc in range(CPS) for h in range(H))
        for h in range(H):
            s_ref[h] = state[(CPS, h)]

    blk = lambda: pl.BlockSpec((CPS * C, AW), lambda n: (n, 0))
    return pl.pallas_call(
        body, name="delta_fwd", grid=(N // CPS,),
        in_specs=[blk(), blk(), blk(),
                  pl.BlockSpec((CPS * C, LANES), lambda n: (n, 0)),
                  pl.BlockSpec((CPS, 32, C), lambda n: (n, 0, 0))],
        out_specs=[blk(), blk(),
                   pl.BlockSpec((CPS, H, D, D), lambda n: (n, 0, 0, 0)),
                   pl.BlockSpec((CPS, H, C, C), lambda n: (n, 0, 0, 0))],
        out_shape=[S((T, AW), F32), S((T, AW), F32), S((N, H, D, D), F32), S((N, H, C, C), F32)],
        scratch_shapes=[pltpu.VMEM((H, D, D), F32)],
        compiler_params=_cp(ARB),
    )(q, k, v, gcol, grow)


def _delta_bwd(q, k, v, gcol, grow, ba, vnew, ssave, asave, d_o, a_log, dt_bias, H, D, carry):
    T = q.shape[0]
    C = CHUNK_A
    N = T // C
    AW = H * D
    nc = len(carry)
    CPS = 2 if N % 2 == 0 else 1
    NS = N // CPS

    def body(al_ref, dt_ref, q_ref, k_ref, v_ref, gcol_ref, grow_ref, ba_ref, vn_ref, ss_ref, as_ref, do_ref, *rest):
        cins = rest[:nc]
        dq_ref, dk_ref, dv_ref, dgate_ref, dpar_ref = rest[nc:nc + 5]
        couts = rest[nc + 5:2 * nc + 5]
        ds_ref, csend, crecv = rest[2 * nc + 5:]
        ccps = _chip_exchange_copies(cins, couts, csend, crecv)

        @pl.when(pl.program_id(0) == 0)
        def _():
            ds_ref[...] = jnp.zeros_like(ds_ref)
            dpar_ref[...] = jnp.zeros_like(dpar_ref)
            for cp in ccps:
                cp.start()

        lane = lax.broadcasted_iota(jnp.int32, (C, LANES), 1)
        rowi = lax.broadcasted_iota(jnp.int32, (C, 1), 0)
        acc = {cc: jnp.zeros((C, LANES), F32) for cc in range(CPS)}
        state = {(0, h): ds_ref[h] for h in range(H)}

        def head(oi, h):
            cc = CPS - 1 - oi
            rows = slice(cc * C, (cc + 1) * C)
            sl = slice(h * D, (h + 1) * D)
            st = ss_ref[cc, h]
            a = as_ref[cc, h]
            qv, kv, vv, dov, v_new = q_ref[rows, sl], k_ref[rows, sl], v_ref[rows, sl], do_ref[rows, sl], vn_ref[rows, sl]
            t = yield from _chunk_terms(qv, kv, vv, gcol_ref[rows, :], grow_ref[cc], h, H)
            beta_c, e_c, gam, kb = t["beta_c"], t["e_c"], t["gam"], t["kb"]
            incl, strict, attn, lmat, edec = t["incl"], t["strict"], t["attn"], t["lmat"], t["edec"]
            kdec = kv * edec
            egl = jnp.exp(t["gl"])
            qe = qv * e_c
            ekb = kb * e_c

            t1 = _mm_nt(dov, st)
            ds_o = _mm_tn(qe, dov)
            dattn_raw = _mm_nt(dov, v_new)
            dv_new_o = _mm_tn(attn, dov)
            yield
            while (oi, h) not in state:
                yield
            ds_next = state[(oi, h)]
            dkdec = _mm_nt(v_new, ds_next)
            dv_new_s = _mm(kdec, ds_next)
            yield
            dgl = egl * jnp.sum(jnp.sum(st * ds_next, axis=1, keepdims=True), axis=0, keepdims=True)
            dk = edec * dkdec
            r = jnp.sum(dkdec * kdec, axis=1, keepdims=True)
            dgc = -r
            dgl = dgl + jnp.sum(r, axis=0, keepdims=True)
            dq = e_c * t1
            dgc = dgc + jnp.sum(t1 * qe, axis=1, keepdims=True)
            dattn = jnp.where(incl, dattn_raw, 0.0)
            dv_new = dv_new_s + dv_new_o
            dqm = dattn * gam
            z = dattn * attn
            dvb = _mm3(a, dv_new, _TN)
            dq_a = _mm(dqm, kv)
            dk_a = _mm_tn(dqm, qv)
            yield
            dq_ref[rows, sl] = dq + dq_a
            dv_ref[rows, sl] = beta_c * dvb
            ds_kb = _mm_tn(ekb, dvb)
            dekb_neg = _mm_nt(dvb, st)
            dl_neg = _mm_nt(dvb, v_new)
            yield
            state[(oi + 1, h)] = egl * ds_next + ds_o - ds_kb
            dekb = -dekb_neg
            dl = jnp.where(strict, -dl_neg, 0.0)
            dp = dl * gam
            z = z + dl * lmat
            dkb_p = _mm(dp, kv)
            dk_p = _mm_tn(dp, kb)
            dgc = dgc + jnp.sum(dekb * ekb, axis=1, keepdims=True)
            dgc = dgc + jnp.sum(z, axis=1, keepdims=True) - jnp.sum(z.T, axis=1, keepdims=True)
            dgc = dgc + jnp.where(rowi == C - 1, dgl, 0.0)
            yield
            dkb = dkb_p + e_c * dekb
            dk_ref[rows, sl] = dk + dk_a + dk_p + beta_c * dkb
            dbeta = jnp.sum(dkb * kv, axis=1, keepdims=True) + jnp.sum(dvb * vv, axis=1, keepdims=True)
            acc[cc] = acc[cc] + jnp.where(lane == h, dbeta, 0.0) + jnp.where(lane == H + h, dgc, 0.0)

        _interleave(head(oi, h) for oi in range(CPS) for h in range(H))
        for h in range(H):
            ds_ref[h] = state[(CPS, h)]
        ri = lax.broadcasted_iota(jnp.int32, (C, C), 0)
        ci = lax.broadcasted_iota(jnp.int32, (C, C), 1)
        upper = (ri <= ci).astype(F32)
        dal = jnp.zeros((1, LANES), F32)
        ddt = jnp.zeros((1, LANES), F32)
        for cc in range(CPS):
            rows = slice(cc * C, (cc + 1) * C)
            gates = gcol_ref[rows, :]
            dg_all = _mm3(upper, acc[cc])
            d_braw = acc[cc] * gates * (1.0 - gates)
            d_araw = dg_all * (-jnp.exp(al_ref[...])) * _sigmoid(ba_ref[rows, :] + dt_ref[...])
            dgate_ref[rows, :] = jnp.where(lane < H, d_braw, jnp.where(lane < 2 * H, d_araw, 0.0)).astype(BF16)
            dal = dal + jnp.sum(dg_all * gates, axis=0, keepdims=True)
            ddt = ddt + jnp.sum(d_araw, axis=0, keepdims=True)
        dpar_ref[0:1, :] += dal
        dpar_ref[1:2, :] += ddt

        @pl.when(pl.program_id(0) == NS - 1)
        def _():
            for cp in ccps:
                cp.wait()

    rev = lambda s: NS - 1 - s
    blk = lambda: pl.BlockSpec((CPS * C, AW), lambda s: (rev(s), 0))
    row = pl.BlockSpec((1, LANES), lambda s: (0, 0))
    any_spec = pl.BlockSpec(memory_space=pl.ANY)
    res = pl.pallas_call(
        body, name="delta_bwd", grid=(NS,),
        in_specs=[row, row, blk(), blk(), blk(),
                  pl.BlockSpec((CPS * C, LANES), lambda s: (rev(s), 0)),
                  pl.BlockSpec((CPS, 32, C), lambda s: (rev(s), 0, 0)),
                  pl.BlockSpec((CPS * C, LANES), lambda s: (rev(s), 0)),
                  blk(),
                  pl.BlockSpec((CPS, H, D, D), lambda s: (rev(s), 0, 0, 0)),
                  pl.BlockSpec((CPS, H, C, C), lambda s: (rev(s), 0, 0, 0)),
                  blk()] + [any_spec] * nc,
        out_specs=[blk(), blk(), blk(),
                   pl.BlockSpec((CPS * C, LANES), lambda s: (rev(s), 0)),
                   pl.BlockSpec((8, LANES), lambda s: (0, 0))] + [any_spec] * nc,
        out_shape=[S((T, AW), F32), S((T, AW), F32), S((T, AW), F32),
                   S((T, LANES), BF16), S((8, LANES), F32)] + [S((3,) + a.shape[1:], a.dtype) for a in carry],
        scratch_shapes=[pltpu.VMEM((H, D, D), F32),
                        pltpu.SemaphoreType.DMA((max(nc, 1), 3)), pltpu.SemaphoreType.DMA((max(nc, 1), 3))],
        compiler_params=_cp(ARB),
    )(a_log, dt_bias, q, k, v, gcol, grow, ba, vnew, ssave, asave, d_o, *carry)
    return res[:5], res[5:]


def _ln_stats(xv):
    mu = jnp.mean(xv, axis=-1, keepdims=True)
    xc = xv - mu
    var = jnp.mean(xc * xc, axis=-1, keepdims=True)
    rstd = lax.rsqrt(var + EPS)
    return xc * rstd, rstd


def _mix_fwd(o, proj, head_norm_w, ln_w, ln_b, w_sp, bs_t, H, D, G, P):
    T = o.shape[0]
    AW, BW = H * D, G * P
    MIX = AW + BW
    nb = AW // BW if AW % BW == 0 else None
    assert nb == 1, "group widths must match the projection column blocks"
    cb = 3

    def body(o_ref, za_ref, ub_ref, vb_ref, zb_ref, hw_ref, lw_ref, lb_ref, w_ref, bs_ref, out_ref):
        hw = hw_ref[...]
        for h in range(H):
            sl = slice(h * D, (h + 1) * D)
            oh = o_ref[:, sl]
            rs = lax.rsqrt(jnp.mean(oh * oh, axis=-1, keepdims=True) + EPS)
            out_ref[:, sl] = (oh * rs * hw * _silu(za_ref[:, sl])).astype(BF16)
        xhat, _ = _ln_stats(vb_ref[...])
        vn = xhat * lw_ref[...] + lb_ref[...]
        ri = lax.broadcasted_iota(jnp.int32, (P, P), 0)
        ci = lax.broadcasted_iota(jnp.int32, (P, P), 1)
        bsv = bs_ref[...]
        for g in range(G):
            sl = slice(g * P, (g + 1) * P)
            wm = jnp.where(ri >= ci, w_ref[g], 0.0)
            s = _mm(wm, vn[:, sl]) + bsv[:, g:g + 1]
            out_ref[:, AW + g * P:AW + (g + 1) * P] = (ub_ref[:, sl] * s * _silu(zb_ref[:, sl])).astype(BF16)

    row = lambda w: pl.BlockSpec((1, w), lambda i: (0, 0))
    return pl.pallas_call(
        body, name="mix_fwd", grid=(T // P,),
        in_specs=[pl.BlockSpec((P, AW), lambda i: (i, 0)),
                  pl.BlockSpec((P, AW), lambda i: (i, cb)),
                  pl.BlockSpec((P, BW), lambda i: (i, cb + 1)),
                  pl.BlockSpec((P, BW), lambda i: (i, cb + 2)),
                  pl.BlockSpec((P, BW), lambda i: (i, cb + 3)),
                  row(D), row(BW), row(BW),
                  pl.BlockSpec((G, P, P), lambda i: (0, 0, 0)),
                  pl.BlockSpec((P, G), lambda i: (0, 0))],
        out_specs=pl.BlockSpec((P, MIX), lambda i: (i, 0)),
        out_shape=S((T, MIX), BF16),
        compiler_params=_cp(ARB),
    )(o, proj, proj, proj, proj, head_norm_w, ln_w, ln_b, w_sp, bs_t)


def _mix_bwd(d_ocat, o, proj, head_norm_w, ln_w, ln_b, w_sp, bs_t, H, D, G, P, carry):
    T = o.shape[0]
    AW, BW = H * D, G * P
    MIX = AW + BW
    cb = 3
    nc = len(carry)

    def body(dc_ref, o_ref, za_ref, ub_ref, vb_ref, zb_ref, hw_ref, lw_ref, lb_ref, w_ref, bs_ref, *rest):
        cins = rest[:nc]
        do_ref, dmain_ref, dhw_ref, dln_ref, dw_ref, dbs_ref = rest[nc:nc + 6]
        couts = rest[nc + 6:2 * nc + 6]
        dvn_ref, drest_ref, out_sems, csend, crecv = rest[2 * nc + 6:]
        i = pl.program_id(0)
        slot = lax.rem(i, 2)
        ccps = _sibling_copies(cins, couts, csend, crecv)

        def out_copy(step, s):
            return pltpu.make_async_copy(
                drest_ref.at[s], dmain_ref.at[pl.ds(step * P, P), pl.ds(cb * AW, AW + 3 * BW)], out_sems.at[s])

        @pl.when(i == 0)
        def _():
            dhw_ref[...] = jnp.zeros_like(dhw_ref)
            dln_ref[...] = jnp.zeros_like(dln_ref)
            dw_ref[...] = jnp.zeros_like(dw_ref)
            dbs_ref[...] = jnp.zeros_like(dbs_ref)
            for cp in ccps:
                cp.start()

        @pl.when(i >= 2)
        def _():
            out_copy(i - 2, slot).wait()

        hw = hw_ref[...]
        dhw = jnp.zeros((1, D), F32)
        for h in range(H):
            sl = slice(h * D, (h + 1) * D)
            oh = o_ref[:, sl]
            za = za_ref[:, sl]
            doa = dc_ref[:, sl]
            rs = lax.rsqrt(jnp.mean(oh * oh, axis=-1, keepdims=True) + EPS)
            xh = oh * rs
            d_on = doa * _silu(za)
            drest_ref[slot, :, sl] = (doa * (xh * hw) * _dsilu(za)).astype(BF16)
            dhw = dhw + jnp.sum(d_on * xh, axis=0, keepdims=True)
            dxh = d_on * hw
            do_ref[:, sl] = rs * (dxh - xh * jnp.mean(dxh * xh, axis=-1, keepdims=True))
        dhw_ref[0:1, :] += dhw

        xhat, rstd = _ln_stats(vb_ref[...])
        lw = lw_ref[...]
        vn = xhat * lw + lb_ref[...]
        ri = lax.broadcasted_iota(jnp.int32, (P, P), 0)
        ci = lax.broadcasted_iota(jnp.int32, (P, P), 1)
        lane = lax.broadcasted_iota(jnp.int32, (P, LANES), 1)
        bsv = bs_ref[...]
        dbs = jnp.zeros((P, LANES), F32)
        for g in range(G):
            sl = slice(g * P, (g + 1) * P)
            wm = jnp.where(ri >= ci, w_ref[g], 0.0)
            vng = vn[:, sl]
            s = _mm(wm, vng) + bsv[:, g:g + 1]
            dob = dc_ref[:, AW + g * P:AW + (g + 1) * P]
            ub = ub_ref[:, sl]
            zb = zb_ref[:, sl]
            szb = _silu(zb)
            drest_ref[slot, :, AW + g * P:AW + (g + 1) * P] = (dob * s * szb).astype(BF16)
            drest_ref[slot, :, AW + 2 * BW + g * P:AW + 2 * BW + (g + 1) * P] = (
                dob * ub * s * _dsilu(zb)).astype(BF16)
            ds = dob * ub * szb
            dvn_ref[:, sl] = _mm_tn(wm, ds)
            dw_ref[g] += jnp.where(ri >= ci, _mm_nt(ds, vng), 0.0)
            dbs = dbs + jnp.where(lane == g, jnp.sum(ds, axis=1, keepdims=True), 0.0)
        dbs_ref[...] += dbs
        dvn = dvn_ref[...]
        dln_ref[0:1, :] += jnp.sum(dvn * xhat, axis=0, keepdims=True)
        dln_ref[1:2, :] += jnp.sum(dvn, axis=0, keepdims=True)
        dxh = dvn * lw
        dvb = rstd * (dxh - jnp.mean(dxh, axis=-1, keepdims=True) - xhat * jnp.mean(dxh * xhat, axis=-1, keepdims=True))
        drest_ref[slot, :, AW + BW:AW + 2 * BW] = dvb.astype(BF16)

        out_copy(i, slot).start()

        @pl.when(i == nstep - 1)
        def _():
            out_copy(i, slot).wait()
            if nstep > 1:
                out_copy(i - 1, 1 - slot).wait()
            for cp in ccps:
                cp.wait()

    nstep = T // P
    row = lambda w: pl.BlockSpec((1, w), lambda i: (0, 0))
    any_spec = pl.BlockSpec(memory_space=pl.ANY)
    res = pl.pallas_call(
        body, name="mix_bwd", grid=(nstep,),
        in_specs=[pl.BlockSpec((P, MIX), lambda i: (i, 0)),
                  pl.BlockSpec((P, AW), lambda i: (i, 0)),
                  pl.BlockSpec((P, AW), lambda i: (i, cb)),
                  pl.BlockSpec((P, BW), lambda i: (i, cb + 1)),
                  pl.BlockSpec((P, BW), lambda i: (i, cb + 2)),
                  pl.BlockSpec((P, BW), lambda i: (i, cb + 3)),
                  row(D), row(BW), row(BW),
                  pl.BlockSpec((G, P, P), lambda i: (0, 0, 0)),
                  pl.BlockSpec((P, G), lambda i: (0, 0))] + [any_spec] * nc,
        out_specs=[pl.BlockSpec((P, AW), lambda i: (i, 0)),
                   any_spec,
                   pl.BlockSpec((8, D), lambda i: (0, 0)),
                   pl.BlockSpec((8, BW), lambda i: (0, 0)),
                   pl.BlockSpec((G, P, P), lambda i: (0, 0, 0)),
                   pl.BlockSpec((P, LANES), lambda i: (0, 0))] + [any_spec] * nc,
        out_shape=[S((T, AW), F32), S((T, cb * AW + AW + 3 * BW), BF16), S((8, D), F32), S((8, BW), F32),
                   S((G, P, P), F32), S((P, LANES), F32)] + [S(a.shape[:1] + a.shape[2:], a.dtype) for a in carry],
        scratch_shapes=[pltpu.VMEM((P, BW), F32), pltpu.VMEM((2, P, AW + 3 * BW), BF16),
                        pltpu.SemaphoreType.DMA((2,))] + _sibling_sems(carry),
        compiler_params=_cp(ARB),
    )(d_ocat, o, proj, proj, proj, proj, head_norm_w, ln_w, ln_b, w_sp, bs_t, *carry)
    return res[:6], res[6:]


def _out_proj_loss(ocat, w_out, x, target, fnw):
    T, MIX = ocat.shape
    DM = x.shape[1]
    tm = _tile(T, 256, 8)

    def body(oc_ref, w_ref, x_ref, t_ref, fw_ref, dh_ref, dhb_ref, doc_ref, loss_ref, gfw_ref):
        @pl.when(pl.program_id(0) == 0)
        def _():
            loss_ref[...] = jnp.zeros_like(loss_ref)
            gfw_ref[...] = jnp.zeros_like(gfw_ref)

        wv = w_ref[...]
        hh = x_ref[...] + jnp.dot(oc_ref[...].astype(MXU), wv.astype(MXU), preferred_element_type=F32)
        rs = lax.rsqrt(jnp.mean(hh * hh, axis=-1, keepdims=True) + EPS)
        hn = hh * rs
        fw = fw_ref[...]
        e = hn * fw - t_ref[...]
        row_loss = 0.5 * jnp.mean(e * e, axis=-1, keepdims=True)
        loss_ref[...] += jnp.sum(row_loss, axis=0, keepdims=True)
        dy = e * (1.0 / DM)
        gfw_ref[0:1, :] += jnp.sum(dy * hn, axis=0, keepdims=True)
        dhn = dy * fw
        dh = rs * (dhn - hn * jnp.mean(dhn * hn, axis=-1, keepdims=True))
        dh_ref[...] = dh
        dhb = dh.astype(BF16)
        dhb_ref[...] = dhb
        doc_ref[...] = _mm_nt(dhb, wv)

    return pl.pallas_call(
        body, name="out_proj_loss", grid=(T // tm,),
        in_specs=[pl.BlockSpec((tm, MIX), lambda i: (i, 0)),
                  pl.BlockSpec((MIX, DM), lambda i: (0, 0)),
                  pl.BlockSpec((tm, DM), lambda i: (i, 0)),
                  pl.BlockSpec((tm, DM), lambda i: (i, 0)),
                  pl.BlockSpec((1, DM), lambda i: (0, 0))],
        out_specs=[pl.BlockSpec((tm, DM), lambda i: (i, 0)),
                   pl.BlockSpec((tm, DM), lambda i: (i, 0)),
                   pl.BlockSpec((tm, MIX), lambda i: (i, 0)),
                   pl.BlockSpec((8, LANES), lambda i: (0, 0)),
                   pl.BlockSpec((8, DM), lambda i: (0, 0))],
        out_shape=[S((T, DM), F32), S((T, DM), BF16), S((T, MIX), F32), S((8, LANES), F32), S((8, DM), F32)],
        compiler_params=_cp(ARB),
    )(ocat, w_out, x, target, fnw)


def _grad_w(lhs, rhs, name):
    T, A = lhs.shape
    B = rhs.shape[1]
    ta = _tile(A, 512, LANES)
    tk = _tile(T, 1024, 16)
    nk = T // tk

    def body(l_ref, r_ref, out_ref, acc_ref):
        k = pl.program_id(1)
        part = _mm_tn(l_ref[...], r_ref[...])

        @pl.when(k == 0)
        def _():
            acc_ref[...] = part

        @pl.when(k > 0)
        def _():
            acc_ref[...] += part

        @pl.when(k == nk - 1)
        def _():
            out_ref[...] = acc_ref[...].astype(BF16)

    return pl.pallas_call(
        body, name=name, grid=(A // ta, nk),
        in_specs=[pl.BlockSpec((tk, ta), lambda i, k: (k, i)),
                  pl.BlockSpec((tk, B), lambda i, k: (k, 0))],
        out_specs=pl.BlockSpec((ta, B), lambda i, k: (i, 0)),
        out_shape=S((A, B), BF16),
        scratch_shapes=[pltpu.VMEM((ta, B), F32)],
        compiler_params=_cp(ARB, ARB),
    )(lhs, rhs)


def _grad_w_in(xn, dmain, dba, WD, gate_lo, gate_hi):
    T, DM = xn.shape
    NM = dmain.shape[1]
    tn = _tile(NM, 1024, LANES)
    tk = _tile(T, 2048, 16)
    nj, nk = NM // tn, T // tk
    ND = N_DEV
    tiles = [[] for _ in range(nj)]
    first_tile, last_tile = {}, {}
    for d, s0, s1, dest, c0 in _pieces(WD, gate_lo, gate_hi, ND * WD):
        if dest != "main":
            continue
        while s0 < s1:
            jj = c0 // tn
            w = min(s1 - s0, (jj + 1) * tn - c0)
            tiles[jj].append((d, s0, w, "main", c0 - jj * tn))
            first_tile.setdefault(d, jj)
            last_tile[d] = jj
            s0, c0 = s0 + w, c0 + w
    for d, s0, s1, dest, c0 in _pieces(WD, gate_lo, gate_hi, ND * WD):
        if dest == "gate":
            tiles[first_tile[d]].append((d, s0, s1 - s0, "gate", c0))
    assert sorted(first_tile) == list(range(ND)) and all(last_tile[d] <= first_tile[d + 2] for d in range(ND - 2))

    def body(xn_ref, dm_ref, dba_ref, keep_ref, recv_ref, acc_ref, gate_ref, buf_ref, lsem, ssem, rsem):
        j = pl.program_id(0)
        k = pl.program_id(1)
        px, py, pc = _position()

        @pl.when(k == 0)
        def _():
            acc_ref[...] = jnp.zeros_like(acc_ref)

        @pl.when((j == 0) & (k == 0))
        def _():
            gate_ref[...] = jnp.zeros_like(gate_ref)

        xv = xn_ref[...]
        acc_ref[...] += _mm_tn(xv, dm_ref[...])

        @pl.when(j == 0)
        def _():
            gate_ref[...] += _mm_tn(xv, dba_ref[...])

        def local(d):
            return pltpu.make_async_copy(buf_ref.at[d % 2], keep_ref.at[d // 2], lsem.at[d // 2])

        def remote(d):
            return pltpu.make_async_remote_copy(
                src_ref=buf_ref.at[d % 2], dst_ref=recv_ref.at[d // 2], send_sem=ssem.at[d // 2],
                recv_sem=rsem.at[d // 2], device_id=(px, py, 1 - pc), device_id_type=MESH)

        def leave(d, start):
            @pl.when(pc == d % 2)
            def _():
                local(d).start() if start else local(d).wait()

            @pl.when(pc != d % 2)
            def _():
                remote(d).start() if start else remote(d).wait_send()

        def emit(jj):
            shards = sorted({p[0] for p in tiles[jj]})
            for d in shards:
                if first_tile[d] == jj and d >= 2:
                    leave(d - 2, False)
                for dd, s0, w, src, c0 in tiles[jj]:
                    if dd == d:
                        ref = acc_ref if src == "main" else gate_ref
                        buf_ref[d % 2, :, s0:s0 + w] = ref[:, c0:c0 + w].astype(BF16)
                if last_tile[d] == jj:
                    leave(d, True)
            if jj == nj - 1:
                for d in (ND - 2, ND - 1):
                    leave(d, False)
                for q in range(ND // 2):
                    remote(2 * q).wait_recv()

        for jj in range(nj):
            @pl.when((j == jj) & (k == nk - 1))
            def _(jj=jj):
                emit(jj)

    any_spec = pl.BlockSpec(memory_space=pl.ANY)
    return pl.pallas_call(
        body, name="grad_w_in", grid=(nj, nk),
        in_specs=[pl.BlockSpec((tk, DM), lambda j, k: (k, 0)),
                  pl.BlockSpec((tk, tn), lambda j, k: (k, j)),
                  pl.BlockSpec((tk, LANES), lambda j, k: (k, 0))],
        out_specs=[any_spec, any_spec],
        out_shape=[S((ND // 2, DM, WD), BF16), S((ND // 2, DM, WD), BF16)],
        scratch_shapes=[pltpu.VMEM((DM, tn), F32), pltpu.VMEM((DM, LANES), F32), pltpu.VMEM((2, DM, WD), BF16),
                        pltpu.SemaphoreType.DMA((ND // 2,)), pltpu.SemaphoreType.DMA((ND // 2,)),
                        pltpu.SemaphoreType.DMA((ND // 2,))],
        compiler_params=_cp(ARB, ARB),
    )(xn, dmain, dba)


def _pair_sum_plain(a, b, name):
    K, R, C = a.shape
    tr = _tile(R, 2048, 16)

    def body(a_ref, b_ref, o_ref):
        o_ref[...] = (a_ref[...].astype(F32) + b_ref[...].astype(F32)).astype(BF16)

    spec = lambda: pl.BlockSpec((1, tr, C), lambda q, i: (q, i, 0))
    return pl.pallas_call(body, name=name, grid=(K, R // tr), in_specs=[spec(), spec()], out_specs=spec(),
                          out_shape=S((K, R, C), BF16), compiler_params=_cp(ARB, ARB))(a, b)


def _dx_rows(T):
    tm = _tile(T, 512, 8)
    return tm if T // tm >= 2 else T // 2


def _dx_part(name, dmain, dba, w_main, w_ba, x, dh, norm_w, blk0, nblk, prev, hbm_in, hbm_alias, hbm_new, make_copies):
    T, NM = dmain.shape
    DM = x.shape[1]
    tm = _dx_rows(T)
    tk = _tile(NM, 1024, LANES)
    nk = NM // tk
    n_in, n_al, n_new = len(hbm_in), len(hbm_alias), len(hbm_new)
    n_prev = 0 if prev is None else 2
    last_step = nblk * nk - 1

    def body(dm_ref, dba_ref, w_ref, wba_ref, x_ref, dh_ref, nw_ref, *rest):
        r = list(rest)
        gnw_prev_ref = r.pop(0) if n_prev else None
        if n_prev:
            r.pop(0)
        in_refs = [r.pop(0) for _ in range(n_in)]
        del r[:n_al]
        gx_ref, gnw_ref = r.pop(0), r.pop(0)
        alias_refs = [r.pop(0) for _ in range(n_al)]
        new_refs = [r.pop(0) for _ in range(n_new)]
        acc_ref, send_sems, recv_sems = r
        i = pl.program_id(0)
        k = pl.program_id(1)
        step = i * nk + k
        cps = make_copies(in_refs, alias_refs, new_refs, send_sems, recv_sems)

        @pl.when(step == 0)
        def _():
            gnw_ref[...] = gnw_prev_ref[...] if n_prev else jnp.zeros_like(gnw_ref)
            for cp in cps:
                cp.start()

        @pl.when(k == 0)
        def _():
            acc_ref[...] = _mm_nt(dba_ref[...], wba_ref[...])

        acc_ref[...] += _mm_nt(dm_ref[...], w_ref[...])

        @pl.when(k == nk - 1)
        def _():
            xv = x_ref[...]
            rs = lax.rsqrt(jnp.mean(xv * xv, axis=-1, keepdims=True) + EPS)
            xh = xv * rs
            dxn = acc_ref[...]
            gnw_ref[0:1, :] += jnp.sum(dxn * xh, axis=0, keepdims=True)
            dxh = dxn * nw_ref[...]
            gx_ref[...] = dh_ref[...] + rs * (dxh - xh * jnp.mean(dxh * xh, axis=-1, keepdims=True))

        @pl.when(step == last_step)
        def _():
            for cp in cps:
                cp.wait()

    any_spec = pl.BlockSpec(memory_space=pl.ANY)
    prev_specs = [pl.BlockSpec((8, DM), lambda i, k: (0, 0)), any_spec] if n_prev else []
    prev_args = [prev[1], prev[0]] if n_prev else []
    aliases = {8: 0} if n_prev else {}
    for q in range(n_al):
        aliases[7 + n_prev + n_in + q] = 2 + q
    res = pl.pallas_call(
        body, name=name, grid=(nblk, nk),
        in_specs=[pl.BlockSpec((tm, tk), lambda i, k: (blk0 + i, k)),
                  pl.BlockSpec((tm, LANES), lambda i, k: (blk0 + i, 0)),
                  pl.BlockSpec((DM, tk), lambda i, k: (0, k)),
                  pl.BlockSpec((DM, LANES), lambda i, k: (0, 0)),
                  pl.BlockSpec((tm, DM), lambda i, k: (blk0 + i, 0)),
                  pl.BlockSpec((tm, DM), lambda i, k: (blk0 + i, 0)),
                  pl.BlockSpec((1, DM), lambda i, k: (0, 0))] + prev_specs + [any_spec] * (n_in + n_al),
        out_specs=[pl.BlockSpec((tm, DM), lambda i, k: (blk0 + i, 0)),
                   pl.BlockSpec((8, DM), lambda i, k: (0, 0))] + [any_spec] * (n_al + n_new),
        out_shape=[S((T, DM), F32), S((8, DM), F32)] + [S(a.shape, a.dtype) for a in hbm_alias] + list(hbm_new),
        scratch_shapes=[pltpu.VMEM((tm, DM), F32), pltpu.SemaphoreType.DMA((10,)), pltpu.SemaphoreType.DMA((10,))],
        input_output_aliases=aliases,
        compiler_params=_cp(ARB, ARB),
    )(dmain, dba, w_main, w_ba, x, dh, norm_w, *prev_args, *hbm_in, *hbm_alias)
    return (res[0], res[1]), res[2:2 + n_al], res[2 + n_al:]


def _remote(kk, src, dst, to, send_sems, recv_sems):
    return pltpu.make_async_remote_copy(src_ref=src, dst_ref=dst, send_sem=send_sems.at[kk], recv_sem=recv_sems.at[kk],
                                        device_id=to, device_id_type=MESH)


def _dx(dmain, dba, w_main, w_ba, x, dh, norm_w, chip_sum, small, cut):
    R, C = chip_sum.shape[1:]
    half = R // 2
    assert half % 16 == 0
    T = x.shape[0]
    ni = T // _dx_rows(T)
    cut = max(1, min(cut, ni - 1))
    upper, lower = pl.ds(0, half), pl.ds(half, half)

    def nbrs():
        px, py, pc = _position()
        return (px, py), (1 - px, py, pc), (px, 1 - py, pc)

    def phase1(ins, als, news, ss, rs):
        (px, py), xn, yn = nbrs()
        cs = ins[0]
        recv, stage = news
        bx, by, bd = cs.at[2 * (1 - px) + py], cs.at[2 * px + (1 - py)], cs.at[2 * (1 - px) + (1 - py)]
        return [_remote(0, bx.at[upper], recv.at[0].at[upper], xn, ss, rs),
                _remote(1, by.at[lower], recv.at[1].at[lower], yn, ss, rs),
                _remote(2, bd.at[upper], stage.at[0], xn, ss, rs),
                _remote(3, bd.at[lower], stage.at[1], yn, ss, rs)]

    def phase2(ins, als, news, ss, rs):
        (px, py), xn, yn = nbrs()
        comb, small_ref = ins
        recv, gath = als[0], news[0]
        me, small_cps = _broadcast_copies([small_ref], [gath], _Sem2(ss, 2), _Sem2(rs, 2))
        return ([_remote(0, comb.at[0], recv.at[1].at[upper], yn, ss, rs),
                 _remote(1, comb.at[1], recv.at[0].at[lower], xn, ss, rs)] + small_cps
                + [pltpu.make_async_copy(small_ref, gath.at[me], ss.at[9])])

    (gx, gnw), _, (recv, stage) = _dx_part(
        "dx_a", dmain, dba, w_main, w_ba, x, dh, norm_w, 0, cut, None, [chip_sum], [],
        [S((2, R, C), chip_sum.dtype), S((2, half, C), chip_sum.dtype)], phase1)
    comb = _relay_add(chip_sum, stage)
    (gx, gnw), (recv,), (gath,) = _dx_part(
        "dx_b", dmain, dba, w_main, w_ba, x, dh, norm_w, cut, ni - cut, (gx, gnw), [comb, small], [recv],
        [S((N_DEV,) + small.shape, F32)], phase2)
    return gx, gnw, gath, recv


class _Sem2:
    def __init__(self, sems, lo):
        self.sems, self.lo = sems, lo

    @property
    def at(self):
        outer = self

        class _At:
            def __getitem__(self, idx):
                a, k = idx
                return outer.sems.at[outer.lo + k]
        return _At()


def _relay_add(chip_sum, stage):
    _, R, C = chip_sum.shape
    half = R // 2
    tr = _tile(half, 256, 16)
    nt = half // tr
    px, py, _ = _position()
    idx = jnp.stack([2 * px + (1 - py), 2 * (1 - px) + py]).astype(jnp.int32)

    def body(idx_ref, p_ref, s_ref, o_ref):
        del idx_ref
        o_ref[0] = (p_ref[0].astype(F32) + s_ref[0].astype(F32)).astype(BF16)

    return pl.pallas_call(
        body, name="relay_add",
        grid_spec=pltpu.PrefetchScalarGridSpec(
            num_scalar_prefetch=1, grid=(2, nt),
            in_specs=[pl.BlockSpec((1, tr, C), lambda s, i, idx_ref: (idx_ref[s], s * nt + i, 0)),
                      pl.BlockSpec((1, tr, C), lambda s, i, idx_ref: (s, i, 0))],
            out_specs=pl.BlockSpec((1, tr, C), lambda s, i, idx_ref: (s, i, 0))),
        out_shape=S((2, half, C), BF16), compiler_params=_cp(ARB, ARB),
    )(idx, chip_sum, stage)


def _copy_rows(a, name):
    R, C = a.shape
    tr = _tile(R, 1024, 8)

    def body(a_ref, o_ref):
        o_ref[...] = a_ref[...]

    spec = pl.BlockSpec((tr, C), lambda i: (i, 0))
    return pl.pallas_call(body, name=name, grid=(R // tr,), in_specs=[spec], out_specs=spec,
                          out_shape=S(a.shape, a.dtype), compiler_params=_cp(ARB))(a)


def _sum_slots(gath, shapes):
    spans, outs, r = [], [], 0
    for shp in shapes:
        n = 1
        for s in shp:
            n *= s
        nr = -(-n // (8 * LANES)) * 8
        spans.append((r, nr, n))
        outs.append(S((1, n), F32) if n < LANES else S((nr, LANES), F32))
        r += nr
    assert r == gath.shape[1] and gath.shape[2] == LANES

    def body(g_ref, *o_refs):
        tot = g_ref[0]
        for d in range(1, N_DEV):
            tot = tot + g_ref[d]
        for (r0, nr, n), o_ref in zip(spans, o_refs):
            o_ref[...] = tot[r0:r0 + 1, :n] if n < LANES else tot[r0:r0 + nr]

    vm = pl.BlockSpec(memory_space=pltpu.VMEM)
    res = pl.pallas_call(body, name="sum_slots", in_specs=[vm], out_specs=[vm] * len(outs), out_shape=outs)(gath)
    return [a.reshape(-1)[:n].reshape(shp) for a, (_, _, n), shp in zip(res, spans, shapes)]


def _prep_a_bwd(dq, dk, dv, c, proj, conv_w, dmain, H, D):
    T = c.shape[0]
    AW = H * D
    C3 = 3 * AW
    tb = _tile(T, 256, 8)
    nblk = T // tb
    r8 = tb // 8
    scale = float(D) ** -0.5

    def body(dq_ref, dk_ref, dv_ref, c_ref, dqn_ref, dkn_ref, dvn_ref, cn_ref, x_ref, halo_ref, cw_ref, dmain_in_ref,
             dx_ref, gcw_ref, dc_ref):
        del dmain_in_ref
        i = pl.program_id(0)

        @pl.when(i == 0)
        def _():
            gcw_ref[...] = jnp.zeros_like(gcw_ref)

        def pointwise(rows, dq_r, dk_r, dv_r, c_r, keep):
            for h in range(H):
                for part, d_r, sc in ((0, dq_r, scale), (1, dk_r, 1.0)):
                    sl = slice(part * AW + h * D, part * AW + (h + 1) * D)
                    cv = c_r[:, sl]
                    raw = _silu(cv)
                    rs = lax.rsqrt(jnp.sum(raw * raw, axis=-1, keepdims=True) + EPS)
                    nrm = raw * rs
                    dn = d_r[:, h * D:(h + 1) * D] * sc
                    draw = rs * (dn - nrm * jnp.sum(dn * nrm, axis=-1, keepdims=True))
                    dc_ref[rows, sl] = draw * _dsilu(cv) * keep
            dc_ref[rows, 2 * AW:] = dv_r[...] * _dsilu(c_r[:, 2 * AW:]) * keep

        pointwise(slice(0, tb), dq_ref, dk_ref, dv_ref, c_ref, 1.0)
        pointwise(slice(tb, tb + 8), dqn_ref, dkn_ref, dvn_ref, cn_ref, (i < nblk - 1).astype(F32))

        cw = cw_ref[...]
        dcv = dc_ref[0:tb, :]
        dx = cw[3:4, :] * dcv
        for j in range(3):
            dx = dx + cw[j:j + 1, :] * dc_ref[3 - j:3 - j + tb, :]
        dx_ref[...] = dx.astype(BF16)
        halo = halo_ref[...] * (i > 0).astype(F32)
        xp = jnp.concatenate([halo, x_ref[...]], axis=0)
        for j in range(4):
            gcw_ref[j:j + 1, :] += jnp.sum(dcv * xp[5 + j:5 + j + tb], axis=0, keepdims=True)

    nxt = lambda i: (jnp.minimum((i + 1) * r8, T // 8 - 1), 0)
    return pl.pallas_call(
        body, name="prep_a_bwd", grid=(nblk,),
        in_specs=[pl.BlockSpec((tb, AW), lambda i: (i, 0)),
                  pl.BlockSpec((tb, AW), lambda i: (i, 0)),
                  pl.BlockSpec((tb, AW), lambda i: (i, 0)),
                  pl.BlockSpec((tb, C3), lambda i: (i, 0)),
                  pl.BlockSpec((8, AW), nxt), pl.BlockSpec((8, AW), nxt), pl.BlockSpec((8, AW), nxt),
                  pl.BlockSpec((8, C3), nxt),
                  pl.BlockSpec((tb, C3), lambda i: (i, 0)),
                  pl.BlockSpec((8, C3), lambda i: (jnp.maximum(i * r8 - 1, 0), 0)),
                  pl.BlockSpec((4, C3), lambda i: (0, 0)),
                  pl.BlockSpec(memory_space=pl.ANY)],
        out_specs=[pl.BlockSpec((tb, C3), lambda i: (i, 0)),
                   pl.BlockSpec((8, C3), lambda i: (0, 0))],
        out_shape=[S(dmain.shape, dmain.dtype), S((8, C3), F32)],
        scratch_shapes=[pltpu.VMEM((tb + 8, C3), F32)],
        input_output_aliases={11: 0},
        compiler_params=_cp(ARB),
    )(dq, dk, dv, c, dq, dk, dv, c, proj, proj, conv_w, dmain)


def _adam_math(w, g, m, v):
    m2 = ADAM_B1 * m + (1.0 - ADAM_B1) * g
    v2 = ADAM_B2 * v + (1.0 - ADAM_B2) * (g * g)
    m_hat = m2 / (1.0 - ADAM_B1 ** ADAM_STEP)
    v_hat = v2 / (1.0 - ADAM_B2 ** ADAM_STEP)
    delta = -ADAM_LR * (m_hat / (jnp.sqrt(v_hat) + ADAM_EPS) + ADAM_WD * w)
    return delta, m2, v2


def _pair_sum(blocks, recv, core, name):
    K, _, R, C = blocks.shape
    tr = _tile(R, 256, 16)

    def body(core_ref, a_ref, b_ref, o_ref):
        del core_ref
        o_ref[0] = (a_ref[0, 0].astype(F32) + b_ref[0].astype(F32)).astype(BF16)

    spec = lambda: pl.BlockSpec((1, tr, C), lambda k, i, core_ref: (k, i, 0))
    return pl.pallas_call(
        body, name=name,
        grid_spec=pltpu.PrefetchScalarGridSpec(
            num_scalar_prefetch=1, grid=(K, R // tr),
            in_specs=[pl.BlockSpec((1, 1, tr, C), lambda k, i, core_ref: (k, core_ref[0], i, 0)), spec()],
            out_specs=spec()),
        out_shape=S((K, R, C), BF16), compiler_params=_cp(ARB, ARB),
    )(core, blocks, recv)


def _sum_adam(chip_sums, recv, w, m, v, chip, name, transposed=False):
    R, C = chip_sums.shape[1:]
    NR = recv.shape[0]
    tr = _tile(R, min(512, max(R // 4, 16)), 16)

    def body(chip_ref, own_ref, r_ref, w_ref, m_ref, v_ref, g_ref, d_ref, m2_ref, v2_ref):
        del chip_ref
        g = own_ref[0].astype(F32)
        for j in range(NR):
            g = g + r_ref[j].astype(F32)
        if transposed:
            g = g.T
        g_ref[...] = g
        d_ref[...], m2_ref[...], v2_ref[...] = _adam_math(w_ref[...], g, m_ref[...], v_ref[...])

    if transposed:
        spec = lambda: pl.BlockSpec((C, tr), lambda i, chip_ref: (0, i))
        shape = (C, R)
    else:
        spec = lambda: pl.BlockSpec((tr, C), lambda i, chip_ref: (i, 0))
        shape = (R, C)
    assert w.shape == shape
    return pl.pallas_call(
        body, name=name,
        grid_spec=pltpu.PrefetchScalarGridSpec(
            num_scalar_prefetch=1, grid=(R // tr,),
            in_specs=[pl.BlockSpec((1, tr, C), lambda i, chip_ref: (chip_ref[0], i, 0)),
                      pl.BlockSpec((NR, tr, C), lambda i, chip_ref: (0, i, 0)), spec(), spec(), spec()],
            out_specs=[spec(), spec(), spec(), spec()]),
        out_shape=[S(shape, F32)] * 4, compiler_params=_cp(ARB),
    )(chip, chip_sums, recv, w, m, v)


def _adam_small(ws, gs, ms, vs):
    n = len(ws)

    def body(*refs):
        ins, outs = refs[:4 * n], refs[4 * n:]
        for p in range(n):
            w_ref, g_ref, m_ref, v_ref = (ins[a * n + p] for a in range(4))
            outs[p][...], outs[n + p][...], outs[2 * n + p][...] = _adam_math(
                w_ref[...], g_ref[...], m_ref[...], v_ref[...])

    vm = pl.BlockSpec(memory_space=pltpu.VMEM)
    res = pl.pallas_call(
        body, name="adam_small", in_specs=[vm] * (4 * n), out_specs=[vm] * (3 * n),
        out_shape=[S(w.shape, F32) for w in ws] * 3,
    )(*ws, *gs, *ms, *vs)
    return res[:n], res[n:2 * n], res[2 * n:]


def _position():
    return lax.axis_index("x"), lax.axis_index("y"), lax.axis_index("c")


def _all_gather_weights(arr, x_in, norm_w, chip, tn, plans, nm):
    R = arr.shape[0]
    half = R // 2
    assert half % 16 == 0
    T, DM = x_in.shape
    tm = _tile(T, 512, 16)
    nstep = T // tm

    def body(chip_ref, x_ref, nw_ref, in_ref, xn_ref, out_ref, proj_ref, wtile_ref, stage_ref,
             send_sems, recv_sems, local_sem, stage_sems):
        i = pl.program_id(0)
        x, y, c = _position()
        me, sibling = (x, y, c), (x, y, 1 - c)
        xn, yn, diag = (1 - x, y), (x, 1 - y), (1 - x, 1 - y)
        upper, lower = pl.ds(0, half), pl.ds(half, half)

        def slot(p, rows=None):
            ref = out_ref.at[4 * p[0] + 2 * p[1] + p[2]]
            return ref if rows is None else ref.at[rows]

        def copy(kk, block, to, rows=None, src=None):
            return pltpu.make_async_remote_copy(
                src_ref=slot(block, rows) if src is None else src, dst_ref=slot(block, rows),
                send_sem=send_sems.at[kk], recv_sem=recv_sems.at[kk], device_id=to, device_id_type=MESH)

        mine = pltpu.make_async_copy(in_ref, slot(me), local_sem)
        first = [copy(0, me, sibling, src=in_ref), copy(1, me, (*xn, c), src=in_ref), copy(2, me, (*yn, c), src=in_ref)]

        @pl.when(i == 0)
        def _():
            mine.start()
            for cp in first:
                cp.start()
            copy(0, sibling, me).wait_recv()
            loads = [pltpu.make_async_copy(in_ref, stage_ref.at[c], stage_sems.at[0]),
                     pltpu.make_async_copy(slot(sibling), stage_ref.at[1 - c], stage_sems.at[1])]
            for cp in loads:
                cp.start()
            for cp in loads:
                cp.wait()
            for m, plan in enumerate(plans):
                @pl.when(chip_ref[0] == m)
                def _(plan=plan):
                    for d, s0, w, c0 in plan:
                        wtile_ref[:, c0:c0 + w] = stage_ref[d % 2, :, s0:s0 + w]

        xv = x_ref[...]
        r = lax.rsqrt(jnp.mean(xv * xv, axis=-1, keepdims=True) + EPS)
        xnv = (xv * r * nw_ref[...]).astype(BF16)
        xn_ref[...] = xnv
        proj_ref[...] = jnp.dot(xnv.astype(MXU), wtile_ref[...].astype(MXU), preferred_element_type=F32)

        @pl.when(i == nstep - 1)
        def _():
            sent = list(first)

            def then(cps):
                for cp in cps:
                    cp.start()
                sent.extend(cps)

            copy(1, (*xn, c), me).wait_recv()
            then([copy(5, (*xn, c), (*yn, c), rows=upper), copy(3, (*xn, c), sibling)])
            copy(2, (*yn, c), me).wait_recv()
            then([copy(6, (*yn, c), (*xn, c), rows=lower), copy(4, (*yn, c), sibling)])
            copy(5, (*diag, c), me, rows=upper).wait_recv()
            then([copy(7, (*diag, c), sibling, rows=upper)])
            copy(6, (*diag, c), me, rows=lower).wait_recv()
            then([copy(8, (*diag, c), sibling, rows=lower)])
            copy(3, (*xn, 1 - c), me).wait_recv()
            copy(4, (*yn, 1 - c), me).wait_recv()
            copy(7, (*diag, 1 - c), me, rows=upper).wait_recv()
            copy(8, (*diag, 1 - c), me, rows=lower).wait_recv()
            for cp in sent:
                cp.wait_send()
            mine.wait()

    any_spec = pl.BlockSpec(memory_space=pl.ANY)
    return pl.pallas_call(
        body, name="all_gather_weights",
        grid_spec=pltpu.PrefetchScalarGridSpec(
            num_scalar_prefetch=1, grid=(nstep,),
            in_specs=[pl.BlockSpec((tm, DM), lambda i, chip_ref: (i, 0)),
                      pl.BlockSpec((1, DM), lambda i, chip_ref: (0, 0)), any_spec],
            out_specs=[pl.BlockSpec((tm, DM), lambda i, chip_ref: (i, 0)), any_spec,
                       pl.BlockSpec((tm, tn), lambda i, chip_ref: (i, 2 * chip_ref[0]))],
            scratch_shapes=[pltpu.VMEM((DM, tn), arr.dtype), pltpu.VMEM((2,) + arr.shape, arr.dtype),
                            pltpu.SemaphoreType.DMA((9,)), pltpu.SemaphoreType.DMA((9,)), pltpu.SemaphoreType.DMA,
                            pltpu.SemaphoreType.DMA((2,))]),
        out_shape=[S((T, DM), BF16), S((N_DEV,) + arr.shape, arr.dtype), S((T, nm), F32)],
        compiler_params=_cp(ARB),
    )(chip, x_in, norm_w, arr)


def _sibling_copies(ins, outs, send_sems, recv_sems):
    x, y, c = _position()
    return [pltpu.make_async_remote_copy(src_ref=ins[a].at[k, 1 - c], dst_ref=outs[a].at[k],
                                         send_sem=send_sems.at[a, k], recv_sem=recv_sems.at[a, k],
                                         device_id=(x, y, 1 - c), device_id_type=MESH)
            for a in range(len(ins)) for k in range(ins[a].shape[0])]


def _sibling_sems(arrs):
    shape = (max(len(arrs), 1), arrs[0].shape[0] if arrs else 1)
    return [pltpu.SemaphoreType.DMA(shape), pltpu.SemaphoreType.DMA(shape)]


def _chip_exchange_copies(ins, outs, send_sems, recv_sems):
    x, y, c = _position()
    chips = [(1 - x, y), (x, 1 - y), (1 - x, 1 - y)]
    return [pltpu.make_async_remote_copy(
        src_ref=ins[a].at[2 * qx + qy], dst_ref=outs[a].at[j], send_sem=send_sems.at[a, j],
        recv_sem=recv_sems.at[a, j], device_id=(qx, qy, c), device_id_type=MESH)
        for a in range(len(ins)) for j, (qx, qy) in enumerate(chips)]


def _broadcast_copies(srcs, dsts, send_sems, recv_sems):
    x, y, c = _position()
    me = 4 * x + 2 * y + c
    cps = []
    for a in range(len(srcs)):
        for k in range(1, N_DEV):
            peer = (1 - x if k & 4 else x, 1 - y if k & 2 else y, 1 - c if k & 1 else c)
            cps.append(pltpu.make_async_remote_copy(
                src_ref=srcs[a], dst_ref=dsts[a].at[me], send_sem=send_sems.at[a, k - 1],
                recv_sem=recv_sems.at[a, k - 1], device_id=peer, device_id_type=MESH))
    return me, cps


def _all_reduce_small(part):
    R, C = part.shape

    def body(p_ref, out_ref, gath_ref, send_sems, recv_sems):
        me, cps = _broadcast_copies([p_ref], [gath_ref], send_sems, recv_sems)
        gath_ref[me] = p_ref[...]
        for cp in cps:
            cp.start()
        for cp in cps:
            cp.wait()
        acc = gath_ref[0]
        for d in range(1, N_DEV):
            acc = acc + gath_ref[d]
        out_ref[...] = acc

    vm = pl.BlockSpec(memory_space=pltpu.VMEM)
    return pl.pallas_call(
        body, name="all_reduce_small", in_specs=[vm], out_specs=vm, out_shape=S((R, C), F32),
        scratch_shapes=[pltpu.VMEM((N_DEV, R, C), F32), pltpu.SemaphoreType.DMA((1, N_DEV - 1)),
                        pltpu.SemaphoreType.DMA((1, N_DEV - 1))],
    )(part)


def _pack(parts):
    rows = []
    for p in parts:
        f = p.reshape(-1).astype(F32)
        pad = (-f.shape[0]) % (8 * LANES)
        rows.append(jnp.pad(f, (0, pad)).reshape(-1, LANES))
    return jnp.concatenate(rows, axis=0)


def _unpack(buf, shapes):
    out, r = [], 0
    for shp in shapes:
        n = 1
        for s in shp:
            n *= s
        nr = -(-n // (8 * LANES)) * 8
        out.append(buf[r:r + nr].reshape(-1)[:n].reshape(shp))
        r += nr
    return out


def kernel(x, norm_w, w_in, conv_w, a_log, dt_bias, head_norm_w, sgu_ln_w, sgu_ln_b, w_spatial, b_spatial, w_out, final_norm_w, loss_target, m_norm_w, m_w_in, m_conv_w, m_a_log, m_dt_bias, m_head_norm_w, m_sgu_ln_w, m_sgu_ln_b, m_w_spatial, m_b_spatial, m_w_out, m_final_norm_w, v_norm_w, v_w_in, v_conv_w, v_a_log, v_dt_bias, v_head_norm_w, v_sgu_ln_w, v_sgu_ln_b, v_w_spatial, v_b_spatial, v_w_out, v_final_norm_w):
    T, DM = x.shape[1], x.shape[2]
    H, D = a_log.shape[1], head_norm_w.shape[1]
    G, P = w_spatial.shape[1], w_spatial.shape[2]
    AW, BW = H * D, G * P
    MIX = AW + BW
    WD = w_in.shape[2]
    IN = N_DEV * WD
    RO = w_out.shape[1]
    CW = conv_w.shape[2]
    sizes = (3 * AW, AW, H, H, BW, BW, BW)
    assert sum(sizes) == IN and 2 * H <= LANES and 3 * H <= 32 and N_DEV * RO == MIX and N_DEV * CW == 3 * AW
    offs = [0]
    for s in sizes:
        offs.append(offs[-1] + s)
    px, py, pc = _position()
    dev = 4 * px + 2 * py + pc
    chip = 2 * px + py

    x2, tgt = x[0], loss_target[0]

    core_idx = jnp.reshape(pc, (1,)).astype(jnp.int32)
    chip_idx = jnp.reshape(chip, (1,)).astype(jnp.int32)
    NM = IN - 2 * H
    tn_loc, tile_plans = _local_tiles(WD, offs[2], offs[4], NM)
    xn, g_win, proj_part = _all_gather_weights(
        _cast_bf16_t(w_in[0].T, "cast_w_in"), x2, norm_w, chip_idx, tn_loc, tile_plans, NM)
    w_main, w_ba = _relayout_w(g_win, offs[2], offs[4])
    alog_row = jnp.pad(a_log, ((0, 0), (H, LANES - 2 * H)))
    dtb_row = jnp.pad(dt_bias, ((0, 0), (H, LANES - 2 * H)))
    bs_t = b_spatial[0].T

    others = jnp.arange(N_DEV - 2, dtype=jnp.int32)
    others = others + (others >= 2 * chip).astype(jnp.int32)
    proj, ba, (g_wout, g_conv) = _in_proj(xn, w_main, w_ba, proj_part, others, tn_loc,
                                          [_cast_bf16(w_out[0], "cast_w_out"), conv_w[0]])
    w_out_full = g_wout.reshape(MIX, DM)
    conv_full = g_conv.transpose(1, 0, 2).reshape(4, 3 * AW)
    q, k, v, c, gcol, grow = _prep_a_fwd(proj, ba, conv_full, alog_row, dtb_row, H, D)
    o, vnew, ssave, asave = _delta_fwd(q, k, v, gcol, grow, H, D)
    ocat = _mix_fwd(o, proj, head_norm_w, sgu_ln_w, sgu_ln_b, w_spatial[0], bs_t, H, D, G, P)
    dh, dh_bf, d_ocat, loss_acc, g_fnw = _out_proj_loss(ocat, w_out_full, x2, tgt, final_norm_w.reshape(1, DM))

    g_wout_blocks = _grad_w(ocat, dh_bf, "grad_w_out").reshape(4, 2, RO, DM)
    (d_o, dmain, g_hnw, g_ln, g_wsp, g_bs_t), (sib_wout,) = _mix_bwd(
        d_ocat, o, proj, head_norm_w, sgu_ln_w, sgu_ln_b, w_spatial[0], bs_t, H, D, G, P, [g_wout_blocks])
    chip_wout = _pair_sum(g_wout_blocks, sib_wout, core_idx, "pair_sum_w_out")
    (dq, dk, dv, dba, dpar), (recv_wout,) = _delta_bwd(
        q, k, v, gcol, grow, ba, vnew, ssave, asave, d_o, alog_row, dtb_row, H, D, [chip_wout])
    dmain, g_conv_part = _prep_a_bwd(dq, dk, dv, c, proj, conv_full, dmain, H, D)
    keep_win, sib_win = _grad_w_in(xn, dmain, dba, WD, offs[2], offs[4])
    chip_win = _pair_sum_plain(keep_win, sib_win, "pair_sum_w_in")
    small_shapes = [a_log.shape, dt_bias.shape, head_norm_w.shape, sgu_ln_w.shape, sgu_ln_b.shape,
                    w_spatial.shape, b_spatial.shape, final_norm_w.shape]
    parts = [dpar[0, H:2 * H], dpar[1, H:2 * H], g_hnw[0], g_ln[0], g_ln[1], g_wsp, g_bs_t[:, :G].T, g_fnw[0],
             g_conv_part[:4], loss_acc[0, :1]]
    grad_x, g_nw, small_gath, recv_win = _dx(dmain, dba, w_main, w_ba, x2, dh, norm_w, chip_win, _pack(parts), 4)
    red = _sum_slots(small_gath, small_shapes + [(4, 3 * AW), (1,)])
    grad_w_in, delta_w_in, new_m_w_in, new_v_w_in = _sum_adam(
        chip_win, recv_win, w_in[0].T, m_w_in[0].T, v_w_in[0].T, chip_idx, "sum_adam_w_in", transposed=True)
    grad_x = _copy_rows(grad_x, "copy_grad_x")
    grad_w_out, delta_w_out, new_m_w_out, new_v_w_out = _sum_adam(
        chip_wout, recv_wout, w_out[0], m_w_out[0], v_w_out[0], chip_idx, "sum_adam_w_out")
    red_nw = _all_reduce_small(_pack([g_nw[0]]))
    grads_small = _unpack(red_nw, [norm_w.shape]) + red
    loss = grads_small.pop()[0]
    g_conv_full = grads_small.pop()
    grad_conv = lax.dynamic_slice_in_dim(g_conv_full, dev * CW, CW, axis=1)[None]
    small_w = [norm_w, a_log, dt_bias, head_norm_w, sgu_ln_w, sgu_ln_b, w_spatial, b_spatial, final_norm_w, conv_w]
    small_m = [m_norm_w, m_a_log, m_dt_bias, m_head_norm_w, m_sgu_ln_w, m_sgu_ln_b, m_w_spatial, m_b_spatial,
               m_final_norm_w, m_conv_w]
    small_v = [v_norm_w, v_a_log, v_dt_bias, v_head_norm_w, v_sgu_ln_w, v_sgu_ln_b, v_w_spatial, v_b_spatial,
               v_final_norm_w, v_conv_w]
    small_g = grads_small + [grad_conv]
    d_s, m_s, v_s = _adam_small(small_w, small_g, small_m, small_v)

    def order(small, win, wout):
        return [small[0], win.T[None], small[9], small[1], small[2], small[3], small[4], small[5], small[6], small[7],
                wout[None], small[8]]

    grads = order(small_g, grad_w_in, grad_w_out)
    deltas = order(d_s, delta_w_in, delta_w_out)
    new_m = order(m_s, new_m_w_in, new_m_w_out)
    new_v = order(v_s, new_v_w_in, new_v_w_out)
    return (loss, grad_x[None], *grads, *deltas, *new_m, *new_v)
```

```python
import jax
import jax.numpy as jnp
from jax import lax
from jax.experimental import pallas as pl
from jax.experimental.pallas import tpu as pltpu

F32 = jnp.float32
BF16 = jnp.bfloat16
MXU = jnp.bfloat16
HI = lax.Precision.HIGHEST
EPS = 1e-6
CHUNK_A = 64
LANES = 128
MESH = pl.DeviceIdType.MESH
N_DEV = 8

ADAM_LR = 0.001
ADAM_B1 = 0.9
ADAM_B2 = 0.999
ADAM_EPS = 1e-08
ADAM_WD = 0.01
ADAM_STEP = 10

S = jax.ShapeDtypeStruct
ARB = "arbitrary"


def _cp(*sem, vmem_mib=56):
    return pltpu.CompilerParams(dimension_semantics=tuple(sem), vmem_limit_bytes=vmem_mib * 1024 * 1024)


def _tile(n, cap, mult):
    best = None
    t = mult
    while t <= min(n, cap):
        if n % t == 0:
            best = t
        t += mult
    return best if best is not None else n


def _mm(a, b):
    return jnp.dot(a.astype(MXU), b.astype(MXU), preferred_element_type=F32)


def _mm_nt(a, b):
    return lax.dot_general(a.astype(MXU), b.astype(MXU), (((1,), (1,)), ((), ())), preferred_element_type=F32)


def _mm_tn(a, b):
    return lax.dot_general(a.astype(MXU), b.astype(MXU), (((0,), (0,)), ((), ())), preferred_element_type=F32)


def _mmh(a, b):
    return jnp.dot(a, b, precision=HI, preferred_element_type=F32)


def _sigmoid(x):
    return 1.0 / (1.0 + jnp.exp(-x))


def _silu(x):
    return x * _sigmoid(x)


def _dsilu(x):
    s = _sigmoid(x)
    return s * (1.0 + x * (1.0 - s))


def _softplus(x):
    return jnp.maximum(x, 0.0) + jnp.log(1.0 + jnp.exp(-jnp.abs(x)))


def _pieces(wd, gate_lo, gate_hi, total):
    out = []
    for d in range(N_DEV):
        lo, hi = d * wd, (d + 1) * wd
        for dest, a, b, shift in (("main", 0, gate_lo, 0), ("gate", gate_lo, gate_hi, -gate_lo),
                                  ("main", gate_hi, total, gate_lo - gate_hi)):
            s0, s1 = max(lo, a), min(hi, b)
            if s0 < s1:
                out.append((d, s0 - lo, s1 - lo, dest, s0 + shift))
    return out


def _local_tiles(wd, gate_lo, gate_hi, nm):
    n_tiles = N_DEV - 1
    assert nm % (n_tiles * LANES) == 0
    tn = nm // n_tiles
    plans = []
    for m in range(N_DEV // 2):
        lo, hi = 2 * m * tn, (2 * m + 1) * tn
        plan = []
        for d, s0, s1, dest, c0 in _pieces(wd, gate_lo, gate_hi, N_DEV * wd):
            if dest != "main":
                continue
            a, b = max(c0, lo), min(c0 + (s1 - s0), hi)
            if a < b:
                assert d // 2 == m, "tile 2m must come from chip m's own shards"
                plan.append((d, s0 + (a - c0), b - a, a - lo))
        assert sum(p[2] for p in plan) == tn
        plans.append(plan)
    return tn, plans


def _cast_bf16(a, name):
    R, C = a.shape
    tr = _tile(R, 256, 16)

    def body(a_ref, o_ref):
        o_ref[...] = a_ref[...].astype(BF16)

    spec = pl.BlockSpec((tr, C), lambda i: (i, 0))
    return pl.pallas_call(body, name=name, grid=(R // tr,), in_specs=[spec], out_specs=spec,
                          out_shape=S((R, C), BF16), compiler_params=_cp(ARB))(a)


def _cast_bf16_t(a_t, name):
    C, R = a_t.shape
    tr = _tile(R, 256, LANES)

    def body(a_ref, o_ref):
        o_ref[...] = a_ref[...].T.astype(BF16)

    return pl.pallas_call(body, name=name, grid=(R // tr,), in_specs=[pl.BlockSpec((C, tr), lambda i: (0, i))],
                          out_specs=pl.BlockSpec((tr, C), lambda i: (i, 0)),
                          out_shape=S((R, C), BF16), compiler_params=_cp(ARB))(a_t)


def _relayout_w(g_win, gate_lo, gate_hi):
    _, DM, WD = g_win.shape
    total = N_DEV * WD
    NM = total - (gate_hi - gate_lo)
    tr = _tile(DM, 512, 16)
    plan = _pieces(WD, gate_lo, gate_hi, total)

    def body(g_ref, main_ref, gate_ref):
        gate_ref[...] = jnp.zeros_like(gate_ref)
        for d, s0, s1, dest, c0 in plan:
            dst = main_ref if dest == "main" else gate_ref
            dst[:, c0:c0 + (s1 - s0)] = g_ref[d, :, s0:s1]

    return pl.pallas_call(
        body, name="relayout_w", grid=(DM // tr,),
        in_specs=[pl.BlockSpec((N_DEV, tr, WD), lambda i: (0, i, 0))],
        out_specs=[pl.BlockSpec((tr, NM), lambda i: (i, 0)), pl.BlockSpec((tr, LANES), lambda i: (i, 0))],
        out_shape=[S((DM, NM), g_win.dtype), S((DM, LANES), g_win.dtype)],
        compiler_params=_cp(ARB),
    )(g_win)


def _in_proj(xn, w_main, w_ba, proj_part, tiles, tn, shards):
    T, DM = xn.shape
    NM = w_main.shape[1]
    tm = _tile(T, 2048, 16)
    ni, nj = T // tm, tiles.shape[0]
    ns = len(shards)

    def body(tiles_ref, xn_ref, w_ref, wba_ref, part_ref, *rest):
        del tiles_ref, part_ref
        srcs = rest[:ns]
        proj_ref, ba_ref = rest[ns:ns + 2]
        gath = rest[ns + 2:2 * ns + 2]
        send_sems, recv_sems, local_sems = rest[2 * ns + 2:]
        i = pl.program_id(0)
        j = pl.program_id(1)
        me, cps = _broadcast_copies(srcs, gath, send_sems, recv_sems)
        cps = cps + [pltpu.make_async_copy(srcs[a], gath[a].at[me], local_sems.at[a]) for a in range(ns)]

        @pl.when((i == 0) & (j == 0))
        def _():
            for cp in cps:
                cp.start()

        @pl.when(j == 0)
        def _():
            ba_ref[...] = jnp.dot(xn_ref[...].astype(MXU), wba_ref[...].astype(MXU), preferred_element_type=F32)

        proj_ref[...] = jnp.dot(xn_ref[...].astype(MXU), w_ref[...].astype(MXU), preferred_element_type=F32)

        @pl.when((i == ni - 1) & (j == nj - 1))
        def _():
            for cp in cps:
                cp.wait()

    any_spec = pl.BlockSpec(memory_space=pl.ANY)
    res = pl.pallas_call(
        body, name="in_proj",
        grid_spec=pltpu.PrefetchScalarGridSpec(
            num_scalar_prefetch=1, grid=(ni, nj),
            in_specs=[pl.BlockSpec((tm, DM), lambda i, j, t: (i, 0)),
                      pl.BlockSpec((DM, tn), lambda i, j, t: (0, t[j])),
                      pl.BlockSpec((DM, LANES), lambda i, j, t: (0, 0)), any_spec] + [any_spec] * ns,
            out_specs=[pl.BlockSpec((tm, tn), lambda i, j, t: (i, t[j])),
                       pl.BlockSpec((tm, LANES), lambda i, j, t: (i, 0))] + [any_spec] * ns,
            scratch_shapes=[pltpu.SemaphoreType.DMA((ns, N_DEV - 1)), pltpu.SemaphoreType.DMA((ns, N_DEV - 1)),
                            pltpu.SemaphoreType.DMA((ns,))]),
        out_shape=[S((T, NM), F32), S((T, LANES), F32)] + [S((N_DEV,) + a.shape, a.dtype) for a in shards],
        input_output_aliases={4: 0},
        compiler_params=_cp(ARB, ARB, vmem_mib=58),
    )(tiles, xn, w_main, w_ba, proj_part, *shards)
    return res[0], res[1], res[2:]


def _prep_a_fwd(proj, ba, conv_w, alog_row, dtb_row, H, D):
    T = proj.shape[0]
    AW = H * D
    C3 = 3 * AW
    tb = _tile(T, 256, CHUNK_A)
    nch = tb // CHUNK_A
    nblk = T // tb
    scale = float(D) ** -0.5

    def body(x_ref, halo_ref, ba_ref, cw_ref, al_ref, dt_ref, q_ref, k_ref, v_ref, c_ref, gcol_ref, grow_ref):
        i = pl.program_id(0)
        xv = x_ref[...]
        halo = halo_ref[...] * (i > 0).astype(F32)
        xp = jnp.concatenate([halo, xv], axis=0)
        cw = cw_ref[...]
        c = cw[0:1, :] * xp[5:5 + tb]
        for j in range(1, 4):
            c = c + cw[j:j + 1, :] * xp[5 + j:5 + j + tb]
        c_ref[...] = c
        a = _silu(c)
        for h in range(H):
            qh = a[:, h * D:(h + 1) * D]
            kh = a[:, AW + h * D:AW + (h + 1) * D]
            qr = lax.rsqrt(jnp.sum(qh * qh, axis=-1, keepdims=True) + EPS)
            kr = lax.rsqrt(jnp.sum(kh * kh, axis=-1, keepdims=True) + EPS)
            q_ref[:, h * D:(h + 1) * D] = qh * (qr * scale)
            k_ref[:, h * D:(h + 1) * D] = kh * kr
        v_ref[...] = a[:, 2 * AW:]

        bav = ba_ref[...]
        lane = lax.broadcasted_iota(jnp.int32, (tb, LANES), 1)
        beta = _sigmoid(bav)
        g = -jnp.exp(al_ref[...]) * _softplus(bav + dt_ref[...])
        gates = jnp.where(lane < H, beta, jnp.where(lane < 2 * H, g, 0.0))
        ri = lax.broadcasted_iota(jnp.int32, (CHUNK_A, CHUNK_A), 0)
        ci = lax.broadcasted_iota(jnp.int32, (CHUNK_A, CHUNK_A), 1)
        tri = (ri >= ci).astype(F32)
        lane_c = lax.broadcasted_iota(jnp.int32, (CHUNK_A, LANES), 1)
        for cc in range(nch):
            gch = gates[cc * CHUNK_A:(cc + 1) * CHUNK_A]
            gc = pltpu.roll(_mmh(tri, gch), H, 1)
            full = jnp.where(lane_c < 2 * H, gch, jnp.where(lane_c < 3 * H, gc, 0.0))
            gcol_ref[cc * CHUNK_A:(cc + 1) * CHUNK_A, :] = full
            grow_ref[cc] = full.T[0:32, :]

    return pl.pallas_call(
        body, name="prep_a_fwd", grid=(nblk,),
        in_specs=[pl.BlockSpec((tb, C3), lambda i: (i, 0)),
                  pl.BlockSpec((8, C3), lambda i: (jnp.maximum(i * (tb // 8) - 1, 0), 0)),
                  pl.BlockSpec((tb, LANES), lambda i: (i, 0)),
                  pl.BlockSpec((4, C3), lambda i: (0, 0)),
                  pl.BlockSpec((1, LANES), lambda i: (0, 0)),
                  pl.BlockSpec((1, LANES), lambda i: (0, 0))],
        out_specs=[pl.BlockSpec((tb, AW), lambda i: (i, 0)),
                   pl.BlockSpec((tb, AW), lambda i: (i, 0)),
                   pl.BlockSpec((tb, AW), lambda i: (i, 0)),
                   pl.BlockSpec((tb, C3), lambda i: (i, 0)),
                   pl.BlockSpec((tb, LANES), lambda i: (i, 0)),
                   pl.BlockSpec((nch, 32, CHUNK_A), lambda i: (i, 0, 0))],
        out_shape=[S((T, AW), F32), S((T, AW), F32), S((T, AW), F32), S((T, C3), F32),
                   S((T, LANES), F32), S((T // CHUNK_A, 32, CHUNK_A), F32)],
        compiler_params=_cp(ARB),
    )(proj, proj, ba, conv_w, alog_row, dtb_row)


_NN = (((1,), (0,)), ((), ()))
_TN = (((0,), (0,)), ((), ()))


def _split(a):
    hi = a.astype(BF16)
    return hi, (a - hi.astype(F32)).astype(BF16)


def _mm3(a, b, dims=_NN):
    ah, al = a if isinstance(a, tuple) else _split(a)
    bh, bl = b if isinstance(b, tuple) else _split(b)
    dg = lambda p, r: lax.dot_general(p, r, dims, preferred_element_type=F32)
    return dg(ah, bh) + (dg(ah, bl) + dg(al, bh))


def _interleave(gens):
    gens = list(gens)
    while gens:
        alive = []
        for g in gens:
            try:
                next(g)
                alive.append(g)
            except StopIteration:
                pass
        gens = alive


def _chunk_terms(q, k, v, gcolv, growv, h, H):
    C = CHUNK_A
    beta_c = gcolv[:, h:h + 1]
    g_c = gcolv[:, H + h:H + h + 1]
    gc_c = gcolv[:, 2 * H + h:2 * H + h + 1]
    gc_r = growv[2 * H + h:2 * H + h + 1, :]
    ri = lax.broadcasted_iota(jnp.int32, (C, C), 0)
    ci = lax.broadcasted_iota(jnp.int32, (C, C), 1)
    incl = ri >= ci
    strict = ri > ci
    kb = k * beta_c
    vb = v * beta_c
    p_raw = _mm_nt(kb, k)
    qk_raw = _mm_nt(q, k)
    gam = jnp.where(incl, jnp.exp(jnp.where(incl, gc_c - gc_r, 0.0)), 0.0)
    e_c = jnp.exp(gc_c)
    gl = gc_r[:, C - 1:C]
    edec = jnp.exp(gl - gc_c)
    yield
    lmat = jnp.where(strict, p_raw * gam, 0.0)
    attn = jnp.where(incl, qk_raw * gam, 0.0)
    return dict(beta_c=beta_c, g_c=g_c, gc_c=gc_c, gc_r=gc_r, incl=incl, strict=strict, gam=gam, e_c=e_c,
                kb=kb, vb=vb, lmat=lmat, attn=attn, gl=gl, edec=edec, ri=ri, ci=ci)


INV_BLOCK = 16


def _inv_unit_lower(lmat):
    C = lmat.shape[0]
    ri = lax.broadcasted_iota(jnp.int32, (C, C), 0)
    ci = lax.broadcasted_iota(jnp.int32, (C, C), 1)
    eye = (ri == ci).astype(F32)
    same = (ri // INV_BLOCK) == (ci // INV_BLOCK)

    def neumann(x, order):
        a = eye + x
        n = 1
        while 2 * n < order:
            xs = _split(x)
            x = _mm3(xs, xs)
            yield
            a = a + _mm3(a, x)
            n *= 2
        yield
        return a

    inv_d = yield from neumann(-jnp.where(same, lmat, 0.0), INV_BLOCK)
    m = _mm3(inv_d, jnp.where(same, 0.0, lmat))
    yield
    inv_m = yield from neumann(-m, C // INV_BLOCK)
    a = _mm3(inv_m, inv_d)
    yield
    return a


def _delta_fwd(q, k, v, gcol, grow, H, D):
    T = q.shape[0]
    C = CHUNK_A
    N = T // C
    AW = H * D
    CPS = 2 if N % 2 == 0 else 1

    def body(q_ref, k_ref, v_ref, gcol_ref, grow_ref, o_ref, vn_ref, ssave_ref, asave_ref, s_ref):
        @pl.when(pl.program_id(0) == 0)
        def _():
            s_ref[...] = jnp.zeros_like(s_ref)

        state = {(0, h): s_ref[h] for h in range(H)}

        def head(cc, h):
            rows = slice(cc * C, (cc + 1) * C)
            sl = slice(h * D, (h + 1) * D)
            qv, kv, vv = q_ref[rows, sl], k_ref[rows, sl], v_ref[rows, sl]
            t = yield from _chunk_terms(qv, kv, vv, gcol_ref[rows, :], grow_ref[cc], h, H)
            a = yield from _inv_unit_lower(t["lmat"])
            asave_ref[cc, h] = a
            while (cc, h) not in state:
                yield
            st = state[(cc, h)]
            ssave_ref[cc, h] = st
            ks = _mm(t["kb"] * t["e_c"], st)
            o_inter = _mm(qv * t["e_c"], st)
            yield
            v_new = _mm3(a, t["vb"] - ks)
            yield
            vn_ref[rows, sl] = v_new
            o_intra = _mm(t["attn"], v_new)
            s_upd = _mm_tn(kv * t["edec"], v_new)
            yield
            o_ref[rows, sl] = o_inter + o_intra
            state[(cc + 1, h)] = st * jnp.exp(t["gl"]) + s_upd

        _interleave(head(cc, h) for cc in range(CPS) for h in range(H))
        for h in range(H):
            s_ref[h] = state[(CPS, h)]

    blk = lambda: pl.BlockSpec((CPS * C, AW), lambda n: (n, 0))
    return pl.pallas_call(
        body, name="delta_fwd", grid=(N // CPS,),
        in_specs=[blk(), blk(), blk(),
                  pl.BlockSpec((CPS * C, LANES), lambda n: (n, 0)),
                  pl.BlockSpec((CPS, 32, C), lambda n: (n, 0, 0))],
        out_specs=[blk(), blk(),
                   pl.BlockSpec((CPS, H, D, D), lambda n: (n, 0, 0, 0)),
                   pl.BlockSpec((CPS, H, C, C), lambda n: (n, 0, 0, 0))],
        out_shape=[S((T, AW), F32), S((T, AW), F32), S((N, H, D, D), F32), S((N, H, C, C), F32)],
        scratch_shapes=[pltpu.VMEM((H, D, D), F32)],
        compiler_params=_cp(ARB),
    )(q, k, v, gcol, grow)


def _delta_bwd(q, k, v, gcol, grow, ba, vnew, ssave, asave, d_o, a_log, dt_bias, H, D, carry):
    T = q.shape[0]
    C = CHUNK_A
    N = T // C
    AW = H * D
    nc = len(carry)
    CPS = 2 if N % 2 == 0 else 1
    NS = N // CPS

    def body(al_ref, dt_ref, q_ref, k_ref, v_ref, gcol_ref, grow_ref, ba_ref, vn_ref, ss_ref, as_ref, do_ref, *rest):
        cins = rest[:nc]
        dq_ref, dk_ref, dv_ref, dgate_ref, dpar_ref = rest[nc:nc + 5]
        couts = rest[nc + 5:2 * nc + 5]
        ds_ref, csend, crecv = rest[2 * nc + 5:]
        ccps = _chip_exchange_copies(cins, couts, csend, crecv)

        @pl.when(pl.program_id(0) == 0)
        def _():
            ds_ref[...] = jnp.zeros_like(ds_ref)
            dpar_ref[...] = jnp.zeros_like(dpar_ref)
            for cp in ccps:
                cp.start()

        lane = lax.broadcasted_iota(jnp.int32, (C, LANES), 1)
        rowi = lax.broadcasted_iota(jnp.int32, (C, 1), 0)
        acc = {cc: jnp.zeros((C, LANES), F32) for cc in range(CPS)}
        state = {(0, h): ds_ref[h] for h in range(H)}

        def head(oi, h):
            cc = CPS - 1 - oi
            rows = slice(cc * C, (cc + 1) * C)
            sl = slice(h * D, (h + 1) * D)
            st = ss_ref[cc, h]
            a = as_ref[cc, h]
            qv, kv, vv, dov, v_new = q_ref[rows, sl], k_ref[rows, sl], v_ref[rows, sl], do_ref[rows, sl], vn_ref[rows, sl]
            t = yield from _chunk_terms(qv, kv, vv, gcol_ref[rows, :], grow_ref[cc], h, H)
            beta_c, e_c, gam, kb = t["beta_c"], t["e_c"], t["gam"], t["kb"]
            incl, strict, attn, lmat, edec = t["incl"], t["strict"], t["attn"], t["lmat"], t["edec"]
            kdec = kv * edec
            egl = jnp.exp(t["gl"])
            qe = qv * e_c
            ekb = kb * e_c

            t1 = _mm_nt(dov, st)
            ds_o = _mm_tn(qe, dov)
            dattn_raw = _mm_nt(dov, v_new)
            dv_new_o = _mm_tn(attn, dov)
            yield
            while (oi, h) not in state:
                yield
            ds_next = state[(oi, h)]
            dkdec = _mm_nt(v_new, ds_next)
            dv_new_s = _mm(kdec, ds_next)
            yield
            dgl = egl * jnp.sum(jnp.sum(st * ds_next, axis=1, keepdims=True), axis=0, keepdims=True)
            dk = edec * dkdec
            r = jnp.sum(dkdec * kdec, axis=1, keepdims=True)
            dgc = -r
            dgl = dgl + jnp.sum(r, axis=0, keepdims=True)
            dq = e_c * t1
            dgc = dgc + jnp.sum(t1 * qe, axis=1, keepdims=True)
            dattn = jnp.where(incl, dattn_raw, 0.0)
            dv_new = dv_new_s + dv_new_o
            dqm = dattn * gam
            z = dattn * attn
            dvb = _mm3(a, dv_new, _TN)
            dq_a = _mm(dqm, kv)
            dk_a = _mm_tn(dqm, qv)
            yield
            dq_ref[rows, sl] = dq + dq_a
            dv_ref[rows, sl] = beta_c * dvb
            ds_kb = _mm_tn(ekb, dvb)
            dekb_neg = _mm_nt(dvb, st)
            dl_neg = _mm_nt(dvb, v_new)
            yield
            state[(oi + 1, h)] = egl * ds_next + ds_o - ds_kb
            dekb = -dekb_neg
            dl = jnp.where(strict, -dl_neg, 0.0)
            dp = dl * gam
            z = z + dl * lmat
            dkb_p = _mm(dp, kv)
            dk_p = _mm_tn(dp, kb)
            dgc = dgc + jnp.sum(dekb * ekb, axis=1, keepdims=True)
            dgc = dgc + jnp.sum(z, axis=1, keepdims=True) - jnp.sum(z.T, axis=1, keepdims=True)
            dgc = dgc + jnp.where(rowi == C - 1, dgl, 0.0)
            yield
            dkb = dkb_p + e_c * dekb
            dk_ref[rows, sl] = dk + dk_a + dk_p + beta_c * dkb
            dbeta = jnp.sum(dkb * kv, axis=1, keepdims=True) + jnp.sum(dvb * vv, axis=1, keepdims=True)
            acc[cc] = acc[cc] + jnp.where(lane == h, dbeta, 0.0) + jnp.where(lane == H + h, dgc, 0.0)

        _interleave(head(oi, h) for oi in range(CPS) for h in range(H))
        for h in range(H):
            ds_ref[h] = state[(CPS, h)]
        ri = lax.broadcasted_iota(jnp.int32, (C, C), 0)
        ci = lax.broadcasted_iota(jnp.int32, (C, C), 1)
        upper = (ri <= ci).astype(F32)
        dal = jnp.zeros((1, LANES), F32)
        ddt = jnp.zeros((1, LANES), F32)
        for cc in range(CPS):
            rows = slice(cc * C, (cc + 1) * C)
            gates = gcol_ref[rows, :]
            dg_all = _mm3(upper, acc[cc])
            d_braw = acc[cc] * gates * (1.0 - gates)
            d_araw = dg_all * (-jnp.exp(al_ref[...])) * _sigmoid(ba_ref[rows, :] + dt_ref[...])
            dgate_ref[rows, :] = jnp.where(lane < H, d_braw, jnp.where(lane < 2 * H, d_araw, 0.0)).astype(BF16)
            dal = dal + jnp.sum(dg_all * gates, axis=0, keepdims=True)
            ddt = ddt + jnp.sum(d_araw, axis=0, keepdims=True)
        dpar_ref[0:1, :] += dal
        dpar_ref[1:2, :] += ddt

        @pl.when(pl.program_id(0) == NS - 1)
        def _():
            for cp in ccps:
                cp.wait()

    rev = lambda s: NS - 1 - s
    blk = lambda: pl.BlockSpec((CPS * C, AW), lambda s: (rev(s), 0))
    row = pl.BlockSpec((1, LANES), lambda s: (0, 0))
    any_spec = pl.BlockSpec(memory_space=pl.ANY)
    res = pl.pallas_call(
        body, name="delta_bwd", grid=(NS,),
        in_specs=[row, row, blk(), blk(), blk(),
                  pl.BlockSpec((CPS * C, LANES), lambda s: (rev(s), 0)),
                  pl.BlockSpec((CPS, 32, C), lambda s: (rev(s), 0, 0)),
                  pl.BlockSpec((CPS * C, LANES), lambda s: (rev(s), 0)),
                  blk(),
                  pl.BlockSpec((CPS, H, D, D), lambda s: (rev(s), 0, 0, 0)),
                  pl.BlockSpec((CPS, H, C, C), lambda s: (rev(s), 0, 0, 0)),
                  blk()] + [any_spec] * nc,
        out_specs=[blk(), blk(), blk(),
                   pl.BlockSpec((CPS * C, LANES), lambda s: (rev(s), 0)),
                   pl.BlockSpec((8, LANES), lambda s: (0, 0))] + [any_spec] * nc,
        out_shape=[S((T, AW), F32), S((T, AW), F32), S((T, AW), F32),
                   S((T, LANES), BF16), S((8, LANES), F32)] + [S((3,) + a.shape[1:], a.dtype) for a in carry],
        scratch_shapes=[pltpu.VMEM((H, D, D), F32),
                        pltpu.SemaphoreType.DMA((max(nc, 1), 3)), pltpu.SemaphoreType.DMA((max(nc, 1), 3))],
        compiler_params=_cp(ARB),
    )(a_log, dt_bias, q, k, v, gcol, grow, ba, vnew, ssave, asave, d_o, *carry)
    return res[:5], res[5:]


def _ln_stats(xv):
    mu = jnp.mean(xv, axis=-1, keepdims=True)
    xc = xv - mu
    var = jnp.mean(xc * xc, axis=-1, keepdims=True)
    rstd = lax.rsqrt(var + EPS)
    return xc * rstd, rstd


def _mix_fwd(o, proj, head_norm_w, ln_w, ln_b, w_sp, bs_t, H, D, G, P):
    T = o.shape[0]
    AW, BW = H * D, G * P
    MIX = AW + BW
    nb = AW // BW if AW % BW == 0 else None
    assert nb == 1, "group widths must match the projection column blocks"
    cb = 3

    def body(o_ref, za_ref, ub_ref, vb_ref, zb_ref, hw_ref, lw_ref, lb_ref, w_ref, bs_ref, out_ref):
        hw = hw_ref[...]
        for h in range(H):
            sl = slice(h * D, (h + 1) * D)
            oh = o_ref[:, sl]
            rs = lax.rsqrt(jnp.mean(oh * oh, axis=-1, keepdims=True) + EPS)
            out_ref[:, sl] = (oh * rs * hw * _silu(za_ref[:, sl])).astype(BF16)
        xhat, _ = _ln_stats(vb_ref[...])
        vn = xhat * lw_ref[...] + lb_ref[...]
        ri = lax.broadcasted_iota(jnp.int32, (P, P), 0)
        ci = lax.broadcasted_iota(jnp.int32, (P, P), 1)
        bsv = bs_ref[...]
        for g in range(G):
            sl = slice(g * P, (g + 1) * P)
            wm = jnp.where(ri >= ci, w_ref[g], 0.0)
            s = _mm(wm, vn[:, sl]) + bsv[:, g:g + 1]
            out_ref[:, AW + g * P:AW + (g + 1) * P] = (ub_ref[:, sl] * s * _silu(zb_ref[:, sl])).astype(BF16)

    row = lambda w: pl.BlockSpec((1, w), lambda i: (0, 0))
    return pl.pallas_call(
        body, name="mix_fwd", grid=(T // P,),
        in_specs=[pl.BlockSpec((P, AW), lambda i: (i, 0)),
                  pl.BlockSpec((P, AW), lambda i: (i, cb)),
                  pl.BlockSpec((P, BW), lambda i: (i, cb + 1)),
                  pl.BlockSpec((P, BW), lambda i: (i, cb + 2)),
                  pl.BlockSpec((P, BW), lambda i: (i, cb + 3)),
                  row(D), row(BW), row(BW),
                  pl.BlockSpec((G, P, P), lambda i: (0, 0, 0)),
                  pl.BlockSpec((P, G), lambda i: (0, 0))],
        out_specs=pl.BlockSpec((P, MIX), lambda i: (i, 0)),
        out_shape=S((T, MIX), BF16),
        compiler_params=_cp(ARB),
    )(o, proj, proj, proj, proj, head_norm_w, ln_w, ln_b, w_sp, bs_t)


def _mix_bwd(d_ocat, o, proj, head_norm_w, ln_w, ln_b, w_sp, bs_t, H, D, G, P, carry):
    T = o.shape[0]
    AW, BW = H * D, G * P
    MIX = AW + BW
    cb = 3
    nc = len(carry)

    def body(dc_ref, o_ref, za_ref, ub_ref, vb_ref, zb_ref, hw_ref, lw_ref, lb_ref, w_ref, bs_ref, *rest):
        cins = rest[:nc]
        do_ref, dmain_ref, dhw_ref, dln_ref, dw_ref, dbs_ref = rest[nc:nc + 6]
        couts = rest[nc + 6:2 * nc + 6]
        dvn_ref, drest_ref, out_sems, csend, crecv = rest[2 * nc + 6:]
        i = pl.program_id(0)
        slot = lax.rem(i, 2)
        ccps = _sibling_copies(cins, couts, csend, crecv)

        def out_copy(step, s):
            return pltpu.make_async_copy(
                drest_ref.at[s], dmain_ref.at[pl.ds(step * P, P), pl.ds(cb * AW, AW + 3 * BW)], out_sems.at[s])

        @pl.when(i == 0)
        def _():
            dhw_ref[...] = jnp.zeros_like(dhw_ref)
            dln_ref[...] = jnp.zeros_like(dln_ref)
            dw_ref[...] = jnp.zeros_like(dw_ref)
            dbs_ref[...] = jnp.zeros_like(dbs_ref)
            for cp in ccps:
                cp.start()

        @pl.when(i >= 2)
        def _():
            out_copy(i - 2, slot).wait()

        hw = hw_ref[...]
        dhw = jnp.zeros((1, D), F32)
        for h in range(H):
            sl = slice(h * D, (h + 1) * D)
            oh = o_ref[:, sl]
            za = za_ref[:, sl]
            doa = dc_ref[:, sl]
            rs = lax.rsqrt(jnp.mean(oh * oh, axis=-1, keepdims=True) + EPS)
            xh = oh * rs
            d_on = doa * _silu(za)
            drest_ref[slot, :, sl] = (doa * (xh * hw) * _dsilu(za)).astype(BF16)
            dhw = dhw + jnp.sum(d_on * xh, axis=0, keepdims=True)
            dxh = d_on * hw
            do_ref[:, sl] = rs * (dxh - xh * jnp.mean(dxh * xh, axis=-1, keepdims=True))
        dhw_ref[0:1, :] += dhw

        xhat, rstd = _ln_stats(vb_ref[...])
        lw = lw_ref[...]
        vn = xhat * lw + lb_ref[...]
        ri = lax.broadcasted_iota(jnp.int32, (P, P), 0)
        ci = lax.broadcasted_iota(jnp.int32, (P, P), 1)
        lane = lax.broadcasted_iota(jnp.int32, (P, LANES), 1)
        bsv = bs_ref[...]
        dbs = jnp.zeros((P, LANES), F32)
        for g in range(G):
            sl = slice(g * P, (g + 1) * P)
            wm = jnp.where(ri >= ci, w_ref[g], 0.0)
            vng = vn[:, sl]
            s = _mm(wm, vng) + bsv[:, g:g + 1]
            dob = dc_ref[:, AW + g * P:AW + (g + 1) * P]
            ub = ub_ref[:, sl]
            zb = zb_ref[:, sl]
            szb = _silu(zb)
            drest_ref[slot, :, AW + g * P:AW + (g + 1) * P] = (dob * s * szb).astype(BF16)
            drest_ref[slot, :, AW + 2 * BW + g * P:AW + 2 * BW + (g + 1) * P] = (
                dob * ub * s * _dsilu(zb)).astype(BF16)
            ds = dob * ub * szb
            dvn_ref[:, sl] = _mm_tn(wm, ds)
            dw_ref[g] += jnp.where(ri >= ci, _mm_nt(ds, vng), 0.0)
            dbs = dbs + jnp.where(lane == g, jnp.sum(ds, axis=1, keepdims=True), 0.0)
        dbs_ref[...] += dbs
        dvn = dvn_ref[...]
        dln_ref[0:1, :] += jnp.sum(dvn * xhat, axis=0, keepdims=True)
        dln_ref[1:2, :] += jnp.sum(dvn, axis=0, keepdims=True)
        dxh = dvn * lw
        dvb = rstd * (dxh - jnp.mean(dxh, axis=-1, keepdims=True) - xhat * jnp.mean(dxh * xhat, axis=-1, keepdims=True))
        drest_ref[slot, :, AW + BW:AW + 2 * BW] = dvb.astype(BF16)

        out_copy(i, slot).start()

        @pl.when(i == nstep - 1)
        def _():
            out_copy(i, slot).wait()
            if nstep > 1:
                out_copy(i - 1, 1 - slot).wait()
            for cp in ccps:
                cp.wait()

    nstep = T // P
    row = lambda w: pl.BlockSpec((1, w), lambda i: (0, 0))
    any_spec = pl.BlockSpec(memory_space=pl.ANY)
    res = pl.pallas_call(
        body, name="mix_bwd", grid=(nstep,),
        in_specs=[pl.BlockSpec((P, MIX), lambda i: (i, 0)),
                  pl.BlockSpec((P, AW), lambda i: (i, 0)),
                  pl.BlockSpec((P, AW), lambda i: (i, cb)),
                  pl.BlockSpec((P, BW), lambda i: (i, cb + 1)),
                  pl.BlockSpec((P, BW), lambda i: (i, cb + 2)),
                  pl.BlockSpec((P, BW), lambda i: (i, cb + 3)),
                  row(D), row(BW), row(BW),
                  pl.BlockSpec((G, P, P), lambda i: (0, 0, 0)),
                  pl.BlockSpec((P, G), lambda i: (0, 0))] + [any_spec] * nc,
        out_specs=[pl.BlockSpec((P, AW), lambda i: (i, 0)),
                   any_spec,
                   pl.BlockSpec((8, D), lambda i: (0, 0)),
                   pl.BlockSpec((8, BW), lambda i: (0, 0)),
                   pl.BlockSpec((G, P, P), lambda i: (0, 0, 0)),
                   pl.BlockSpec((P, LANES), lambda i: (0, 0))] + [any_spec] * nc,
        out_shape=[S((T, AW), F32), S((T, cb * AW + AW + 3 * BW), BF16), S((8, D), F32), S((8, BW), F32),
                   S((G, P, P), F32), S((P, LANES), F32)] + [S(a.shape[:1] + a.shape[2:], a.dtype) for a in carry],
        scratch_shapes=[pltpu.VMEM((P, BW), F32), pltpu.VMEM((2, P, AW + 3 * BW), BF16),
                        pltpu.SemaphoreType.DMA((2,))] + _sibling_sems(carry),
        compiler_params=_cp(ARB),
    )(d_ocat, o, proj, proj, proj, proj, head_norm_w, ln_w, ln_b, w_sp, bs_t, *carry)
    return res[:6], res[6:]


def _out_proj_loss(ocat, w_out, x, target, fnw):
    T, MIX = ocat.shape
    DM = x.shape[1]
    tm = _tile(T, 256, 8)

    def body(oc_ref, w_ref, x_ref, t_ref, fw_ref, dh_ref, dhb_ref, doc_ref, loss_ref, gfw_ref):
        @pl.when(pl.program_id(0) == 0)
        def _():
            loss_ref[...] = jnp.zeros_like(loss_ref)
            gfw_ref[...] = jnp.zeros_like(gfw_ref)

        wv = w_ref[...]
        hh = x_ref[...] + jnp.dot(oc_ref[...].astype(MXU), wv.astype(MXU), preferred_element_type=F32)
        rs = lax.rsqrt(jnp.mean(hh * hh, axis=-1, keepdims=True) + EPS)
        hn = hh * rs
        fw = fw_ref[...]
        e = hn * fw - t_ref[...]
        row_loss = 0.5 * jnp.mean(e * e, axis=-1, keepdims=True)
        loss_ref[...] += jnp.sum(row_loss, axis=0, keepdims=True)
        dy = e * (1.0 / DM)
        gfw_ref[0:1, :] += jnp.sum(dy * hn, axis=0, keepdims=True)
        dhn = dy * fw
        dh = rs * (dhn - hn * jnp.mean(dhn * hn, axis=-1, keepdims=True))
        dh_ref[...] = dh
        dhb = dh.astype(BF16)
        dhb_ref[...] = dhb
        doc_ref[...] = _mm_nt(dhb, wv)

    return pl.pallas_call(
        body, name="out_proj_loss", grid=(T // tm,),
        in_specs=[pl.BlockSpec((tm, MIX), lambda i: (i, 0)),
                  pl.BlockSpec((MIX, DM), lambda i: (0, 0)),
                  pl.BlockSpec((tm, DM), lambda i: (i, 0)),
                  pl.BlockSpec((tm, DM), lambda i: (i, 0)),
                  pl.BlockSpec((1, DM), lambda i: (0, 0))],
        out_specs=[pl.BlockSpec((tm, DM), lambda i: (i, 0)),
                   pl.BlockSpec((tm, DM), lambda i: (i, 0)),
                   pl.BlockSpec((tm, MIX), lambda i: (i, 0)),
                   pl.BlockSpec((8, LANES), lambda i: (0, 0)),
                   pl.BlockSpec((8, DM), lambda i: (0, 0))],
        out_shape=[S((T, DM), F32), S((T, DM), BF16), S((T, MIX), F32), S((8, LANES), F32), S((8, DM), F32)],
        compiler_params=_cp(ARB),
    )(ocat, w_out, x, target, fnw)


def _grad_w(lhs, rhs, name):
    T, A = lhs.shape
    B = rhs.shape[1]
    ta = _tile(A, 512, LANES)
    tk = _tile(T, 1024, 16)
    nk = T // tk

    def body(l_ref, r_ref, out_ref, acc_ref):
        k = pl.program_id(1)
        part = _mm_tn(l_ref[...], r_ref[...])

        @pl.when(k == 0)
        def _():
            acc_ref[...] = part

        @pl.when(k > 0)
        def _():
            acc_ref[...] += part

        @pl.when(k == nk - 1)
        def _():
            out_ref[...] = acc_ref[...].astype(BF16)

    return pl.pallas_call(
        body, name=name, grid=(A // ta, nk),
        in_specs=[pl.BlockSpec((tk, ta), lambda i, k: (k, i)),
                  pl.BlockSpec((tk, B), lambda i, k: (k, 0))],
        out_specs=pl.BlockSpec((ta, B), lambda i, k: (i, 0)),
        out_shape=S((A, B), BF16),
        scratch_shapes=[pltpu.VMEM((ta, B), F32)],
        compiler_params=_cp(ARB, ARB),
    )(lhs, rhs)


def _grad_w_in(xn, dmain, dba, WD, gate_lo, gate_hi):
    T, DM = xn.shape
    NM = dmain.shape[1]
    tn = _tile(NM, 1024, LANES)
    tk = _tile(T, 2048, 16)
    nj, nk = NM // tn, T // tk
    ND = N_DEV
    tiles = [[] for _ in range(nj)]
    first_tile, last_tile = {}, {}
    for d, s0, s1, dest, c0 in _pieces(WD, gate_lo, gate_hi, ND * WD):
        if dest != "main":
            continue
        while s0 < s1:
            jj = c0 // tn
            w = min(s1 - s0, (jj + 1) * tn - c0)
            tiles[jj].append((d, s0, w, "main", c0 - jj * tn))
            first_tile.setdefault(d, jj)
            last_tile[d] = jj
            s0, c0 = s0 + w, c0 + w
    for d, s0, s1, dest, c0 in _pieces(WD, gate_lo, gate_hi, ND * WD):
        if dest == "gate":
            tiles[first_tile[d]].append((d, s0, s1 - s0, "gate", c0))
    assert sorted(first_tile) == list(range(ND)) and all(last_tile[d] <= first_tile[d + 2] for d in range(ND - 2))

    def body(xn_ref, dm_ref, dba_ref, keep_ref, recv_ref, acc_ref, gate_ref, buf_ref, lsem, ssem, rsem):
        j = pl.program_id(0)
        k = pl.program_id(1)
        px, py, pc = _position()

        @pl.when(k == 0)
        def _():
            acc_ref[...] = jnp.zeros_like(acc_ref)

        @pl.when((j == 0) & (k == 0))
        def _():
            gate_ref[...] = jnp.zeros_like(gate_ref)

        xv = xn_ref[...]
        acc_ref[...] += _mm_tn(xv, dm_ref[...])

        @pl.when(j == 0)
        def _():
            gate_ref[...] += _mm_tn(xv, dba_ref[...])

        def local(d):
            return pltpu.make_async_copy(buf_ref.at[d % 2], keep_ref.at[d // 2], lsem.at[d // 2])

        def remote(d):
            return pltpu.make_async_remote_copy(
                src_ref=buf_ref.at[d % 2], dst_ref=recv_ref.at[d // 2], send_sem=ssem.at[d // 2],
                recv_sem=rsem.at[d // 2], device_id=(px, py, 1 - pc), device_id_type=MESH)

        def leave(d, start):
            @pl.when(pc == d % 2)
            def _():
                local(d).start() if start else local(d).wait()

            @pl.when(pc != d % 2)
            def _():
                remote(d).start() if start else remote(d).wait_send()

        def emit(jj):
            shards = sorted({p[0] for p in tiles[jj]})
            for d in shards:
                if first_tile[d] == jj and d >= 2:
                    leave(d - 2, False)
                for dd, s0, w, src, c0 in tiles[jj]:
                    if dd == d:
                        ref = acc_ref if src == "main" else gate_ref
                        buf_ref[d % 2, :, s0:s0 + w] = ref[:, c0:c0 + w].astype(BF16)
                if last_tile[d] == jj:
                    leave(d, True)
            if jj == nj - 1:
                for d in (ND - 2, ND - 1):
                    leave(d, False)
                for q in range(ND // 2):
                    remote(2 * q).wait_recv()

        for jj in range(nj):
            @pl.when((j == jj) & (k == nk - 1))
            def _(jj=jj):
                emit(jj)

    any_spec = pl.BlockSpec(memory_space=pl.ANY)
    return pl.pallas_call(
        body, name="grad_w_in", grid=(nj, nk),
        in_specs=[pl.BlockSpec((tk, DM), lambda j, k: (k, 0)),
                  pl.BlockSpec((tk, tn), lambda j, k: (k, j)),
                  pl.BlockSpec((tk, LANES), lambda j, k: (k, 0))],
        out_specs=[any_spec, any_spec],
        out_shape=[S((ND // 2, DM, WD), BF16), S((ND // 2, DM, WD), BF16)],
        scratch_shapes=[pltpu.VMEM((DM, tn), F32), pltpu.VMEM((DM, LANES), F32), pltpu.VMEM((2, DM, WD), BF16),
                        pltpu.SemaphoreType.DMA((ND // 2,)), pltpu.SemaphoreType.DMA((ND // 2,)),
                        pltpu.SemaphoreType.DMA((ND // 2,))],
        compiler_params=_cp(ARB, ARB),
    )(xn, dmain, dba)


def _pair_sum_plain(a, b, name):
    K, R, C = a.shape
    tr = _tile(R, 1024, 16)

    def body(a_ref, b_ref, o_ref):
        o_ref[...] = (a_ref[...].astype(F32) + b_ref[...].astype(F32)).astype(BF16)

    spec = lambda: pl.BlockSpec((1, tr, C), lambda q, i: (q, i, 0))
    return pl.pallas_call(body, name=name, grid=(K, R // tr), in_specs=[spec(), spec()], out_specs=spec(),
                          out_shape=S((K, R, C), BF16), compiler_params=_cp(ARB, ARB))(a, b)


def _dx_rows(T):
    tm = _tile(T, 512, 8)
    return tm if T // tm >= 2 else T // 2


def _dx_part(name, dmain, dba, w_main, w_ba, x, dh, norm_w, blk0, nblk, prev, hbm_in, hbm_alias, hbm_new, make_copies):
    T, NM = dmain.shape
    DM = x.shape[1]
    tm = _dx_rows(T)
    tk = _tile(NM, 1024, LANES)
    nk = NM // tk
    n_in, n_al, n_new = len(hbm_in), len(hbm_alias), len(hbm_new)
    n_prev = 0 if prev is None else 2
    last_step = nblk * nk - 1

    def body(dm_ref, dba_ref, w_ref, wba_ref, x_ref, dh_ref, nw_ref, *rest):
        r = list(rest)
        gnw_prev_ref = r.pop(0) if n_prev else None
        if n_prev:
            r.pop(0)
        in_refs = [r.pop(0) for _ in range(n_in)]
        del r[:n_al]
        gx_ref, gnw_ref = r.pop(0), r.pop(0)
        alias_refs = [r.pop(0) for _ in range(n_al)]
        new_refs = [r.pop(0) for _ in range(n_new)]
        acc_ref, send_sems, recv_sems = r
        i = pl.program_id(0)
        k = pl.program_id(1)
        step = i * nk + k
        cps = make_copies(in_refs, alias_refs, new_refs, send_sems, recv_sems)

        @pl.when(step == 0)
        def _():
            gnw_ref[...] = gnw_prev_ref[...] if n_prev else jnp.zeros_like(gnw_ref)
            for cp in cps:
                cp.start()

        @pl.when(k == 0)
        def _():
            acc_ref[...] = _mm_nt(dba_ref[...], wba_ref[...])

        acc_ref[...] += _mm_nt(dm_ref[...], w_ref[...])

        @pl.when(k == nk - 1)
        def _():
            nw = nw_ref[...]

            def rows64(r, g8):
                for q in range(4):
                    rows = pl.ds(pl.multiple_of(r * 64 + q * 16, 16), 16)
                    xv = x_ref[rows, :]
                    rs = lax.rsqrt(jnp.mean(xv * xv, axis=-1, keepdims=True) + EPS)
                    xh = xv * rs
                    dxn = acc_ref[rows, :]
                    dxh = dxn * nw
                    gx_ref[rows, :] = dh_ref[rows, :] + rs * (dxh - xh * jnp.mean(dxh * xh, axis=-1, keepdims=True))
                    p = dxn * xh
                    g8 = g8 + p[0:8] + p[8:16]
                return g8

            g8 = lax.fori_loop(0, tm // 64, rows64, jnp.zeros((8, DM), F32))
            gnw_ref[0:1, :] += jnp.sum(g8, axis=0, keepdims=True)

        @pl.when(step == last_step)
        def _():
            for cp in cps:
                cp.wait()

    any_spec = pl.BlockSpec(memory_space=pl.ANY)
    prev_specs = [pl.BlockSpec((8, DM), lambda i, k: (0, 0)), any_spec] if n_prev else []
    prev_args = [prev[1], prev[0]] if n_prev else []
    aliases = {8: 0} if n_prev else {}
    for q in range(n_al):
        aliases[7 + n_prev + n_in + q] = 2 + q
    res = pl.pallas_call(
        body, name=name, grid=(nblk, nk),
        in_specs=[pl.BlockSpec((tm, tk), lambda i, k: (blk0 + i, k)),
                  pl.BlockSpec((tm, LANES), lambda i, k: (blk0 + i, 0)),
                  pl.BlockSpec((DM, tk), lambda i, k: (0, k)),
                  pl.BlockSpec((DM, LANES), lambda i, k: (0, 0)),
                  pl.BlockSpec((tm, DM), lambda i, k: (blk0 + i, 0)),
                  pl.BlockSpec((tm, DM), lambda i, k: (blk0 + i, 0)),
                  pl.BlockSpec((1, DM), lambda i, k: (0, 0))] + prev_specs + [any_spec] * (n_in + n_al),
        out_specs=[pl.BlockSpec((tm, DM), lambda i, k: (blk0 + i, 0)),
                   pl.BlockSpec((8, DM), lambda i, k: (0, 0))] + [any_spec] * (n_al + n_new),
        out_shape=[S((T, DM), F32), S((8, DM), F32)] + [S(a.shape, a.dtype) for a in hbm_alias] + list(hbm_new),
        scratch_shapes=[pltpu.VMEM((tm, DM), F32), pltpu.SemaphoreType.DMA((10,)), pltpu.SemaphoreType.DMA((10,))],
        input_output_aliases=aliases,
        compiler_params=_cp(ARB, ARB),
    )(dmain, dba, w_main, w_ba, x, dh, norm_w, *prev_args, *hbm_in, *hbm_alias)
    return (res[0], res[1]), res[2:2 + n_al], res[2 + n_al:]


def _remote(kk, src, dst, to, send_sems, recv_sems):
    return pltpu.make_async_remote_copy(src_ref=src, dst_ref=dst, send_sem=send_sems.at[kk], recv_sem=recv_sems.at[kk],
                                        device_id=to, device_id_type=MESH)


def _dx(dmain, dba, w_main, w_ba, x, dh, norm_w, chip_sum, small, cut):
    R, C = chip_sum.shape[1:]
    half = R // 2
    assert half % 16 == 0
    T = x.shape[0]
    ni = T // _dx_rows(T)
    cut = max(1, min(cut, ni - 1))
    upper, lower = pl.ds(0, half), pl.ds(half, half)

    def nbrs():
        px, py, pc = _position()
        return (px, py), (1 - px, py, pc), (px, 1 - py, pc)

    def phase1(ins, als, news, ss, rs):
        (px, py), xn, yn = nbrs()
        cs = ins[0]
        recv, stage = news
        bx, by, bd = cs.at[2 * (1 - px) + py], cs.at[2 * px + (1 - py)], cs.at[2 * (1 - px) + (1 - py)]
        return [_remote(0, bx.at[upper], recv.at[0].at[upper], xn, ss, rs),
                _remote(1, by.at[lower], recv.at[1].at[lower], yn, ss, rs),
                _remote(2, bd.at[upper], stage.at[0], xn, ss, rs),
                _remote(3, bd.at[lower], stage.at[1], yn, ss, rs)]

    def phase2(ins, als, news, ss, rs):
        (px, py), xn, yn = nbrs()
        comb, small_ref = ins
        recv, gath = als[0], news[0]
        me, small_cps = _broadcast_copies([small_ref], [gath], _Sem2(ss, 2), _Sem2(rs, 2))
        return ([_remote(0, comb.at[0], recv.at[1].at[upper], yn, ss, rs),
                 _remote(1, comb.at[1], recv.at[0].at[lower], xn, ss, rs)] + small_cps
                + [pltpu.make_async_copy(small_ref, gath.at[me], ss.at[9])])

    (gx, gnw), _, (recv, stage) = _dx_part(
        "dx_a", dmain, dba, w_main, w_ba, x, dh, norm_w, 0, cut, None, [chip_sum], [],
        [S((2, R, C), chip_sum.dtype), S((2, half, C), chip_sum.dtype)], phase1)
    comb = _relay_add(chip_sum, stage)
    (gx, gnw), (recv,), (gath,) = _dx_part(
        "dx_b", dmain, dba, w_main, w_ba, x, dh, norm_w, cut, ni - cut, (gx, gnw), [comb, small], [recv],
        [S((N_DEV,) + small.shape, F32)], phase2)
    return gx, gnw, gath, recv


class _Sem2:
    def __init__(self, sems, lo):
        self.sems, self.lo = sems, lo

    @property
    def at(self):
        outer = self

        class _At:
            def __getitem__(self, idx):
                a, k = idx
                return outer.sems.at[outer.lo + k]
        return _At()


def _relay_add(chip_sum, stage):
    _, R, C = chip_sum.shape
    half = R // 2
    tr = _tile(half, 256, 16)
    nt = half // tr
    px, py, _ = _position()
    idx = jnp.stack([2 * px + (1 - py), 2 * (1 - px) + py]).astype(jnp.int32)

    def body(idx_ref, p_ref, s_ref, o_ref):
        del idx_ref
        o_ref[0] = (p_ref[0].astype(F32) + s_ref[0].astype(F32)).astype(BF16)

    return pl.pallas_call(
        body, name="relay_add",
        grid_spec=pltpu.PrefetchScalarGridSpec(
            num_scalar_prefetch=1, grid=(2, nt),
            in_specs=[pl.BlockSpec((1, tr, C), lambda s, i, idx_ref: (idx_ref[s], s * nt + i, 0)),
                      pl.BlockSpec((1, tr, C), lambda s, i, idx_ref: (s, i, 0))],
            out_specs=pl.BlockSpec((1, tr, C), lambda s, i, idx_ref: (s, i, 0))),
        out_shape=S((2, half, C), BF16), compiler_params=_cp(ARB, ARB),
    )(idx, chip_sum, stage)


def _copy_rows(a, name):
    R, C = a.shape
    tr = _tile(R, 1024, 8)

    def body(a_ref, o_ref):
        o_ref[...] = a_ref[...]

    spec = pl.BlockSpec((tr, C), lambda i: (i, 0))
    return pl.pallas_call(body, name=name, grid=(R // tr,), in_specs=[spec], out_specs=spec,
                          out_shape=S(a.shape, a.dtype), compiler_params=_cp(ARB))(a)


def _sum_slots(gath, shapes):
    spans, outs, r = [], [], 0
    for shp in shapes:
        n = 1
        for s in shp:
            n *= s
        nr = -(-n // (8 * LANES)) * 8
        spans.append((r, nr, n))
        outs.append(S((1, n), F32) if n < LANES else S((nr, LANES), F32))
        r += nr
    assert r == gath.shape[1] and gath.shape[2] == LANES

    def body(g_ref, *o_refs):
        tot = g_ref[0]
        for d in range(1, N_DEV):
            tot = tot + g_ref[d]
        for (r0, nr, n), o_ref in zip(spans, o_refs):
            o_ref[...] = tot[r0:r0 + 1, :n] if n < LANES else tot[r0:r0 + nr]

    vm = pl.BlockSpec(memory_space=pltpu.VMEM)
    res = pl.pallas_call(body, name="sum_slots", in_specs=[vm], out_specs=[vm] * len(outs), out_shape=outs)(gath)
    return [a.reshape(-1)[:n].reshape(shp) for a, (_, _, n), shp in zip(res, spans, shapes)]


def _prep_a_bwd(dq, dk, dv, c, proj, conv_w, dmain, H, D):
    T = c.shape[0]
    AW = H * D
    C3 = 3 * AW
    tb = _tile(T, 256, 8)
    nblk = T // tb
    r8 = tb // 8
    scale = float(D) ** -0.5

    def body(dq_ref, dk_ref, dv_ref, c_ref, dqn_ref, dkn_ref, dvn_ref, cn_ref, x_ref, halo_ref, cw_ref, dmain_in_ref,
             dx_ref, gcw_ref, dc_ref):
        del dmain_in_ref
        i = pl.program_id(0)

        @pl.when(i == 0)
        def _():
            gcw_ref[...] = jnp.zeros_like(gcw_ref)

        def pointwise(rows, dq_r, dk_r, dv_r, c_r, keep):
            for h in range(H):
                for part, d_r, sc in ((0, dq_r, scale), (1, dk_r, 1.0)):
                    sl = slice(part * AW + h * D, part * AW + (h + 1) * D)
                    cv = c_r[:, sl]
                    raw = _silu(cv)
                    rs = lax.rsqrt(jnp.sum(raw * raw, axis=-1, keepdims=True) + EPS)
                    nrm = raw * rs
                    dn = d_r[:, h * D:(h + 1) * D] * sc
                    draw = rs * (dn - nrm * jnp.sum(dn * nrm, axis=-1, keepdims=True))
                    dc_ref[rows, sl] = draw * _dsilu(cv) * keep
            dc_ref[rows, 2 * AW:] = dv_r[...] * _dsilu(c_r[:, 2 * AW:]) * keep

        pointwise(slice(0, tb), dq_ref, dk_ref, dv_ref, c_ref, 1.0)
        pointwise(slice(tb, tb + 8), dqn_ref, dkn_ref, dvn_ref, cn_ref, (i < nblk - 1).astype(F32))

        cw = cw_ref[...]
        dcv = dc_ref[0:tb, :]
        dx = cw[3:4, :] * dcv
        for j in range(3):
            dx = dx + cw[j:j + 1, :] * dc_ref[3 - j:3 - j + tb, :]
        dx_ref[...] = dx.astype(BF16)
        halo = halo_ref[...] * (i > 0).astype(F32)
        xp = jnp.concatenate([halo, x_ref[...]], axis=0)
        for j in range(4):
            gcw_ref[j:j + 1, :] += jnp.sum(dcv * xp[5 + j:5 + j + tb], axis=0, keepdims=True)

    nxt = lambda i: (jnp.minimum((i + 1) * r8, T // 8 - 1), 0)
    return pl.pallas_call(
        body, name="prep_a_bwd", grid=(nblk,),
        in_specs=[pl.BlockSpec((tb, AW), lambda i: (i, 0)),
                  pl.BlockSpec((tb, AW), lambda i: (i, 0)),
                  pl.BlockSpec((tb, AW), lambda i: (i, 0)),
                  pl.BlockSpec((tb, C3), lambda i: (i, 0)),
                  pl.BlockSpec((8, AW), nxt), pl.BlockSpec((8, AW), nxt), pl.BlockSpec((8, AW), nxt),
                  pl.BlockSpec((8, C3), nxt),
                  pl.BlockSpec((tb, C3), lambda i: (i, 0)),
                  pl.BlockSpec((8, C3), lambda i: (jnp.maximum(i * r8 - 1, 0), 0)),
                  pl.BlockSpec((4, C3), lambda i: (0, 0)),
                  pl.BlockSpec(memory_space=pl.ANY)],
        out_specs=[pl.BlockSpec((tb, C3), lambda i: (i, 0)),
                   pl.BlockSpec((8, C3), lambda i: (0, 0))],
        out_shape=[S(dmain.shape, dmain.dtype), S((8, C3), F32)],
        scratch_shapes=[pltpu.VMEM((tb + 8, C3), F32)],
        input_output_aliases={11: 0},
        compiler_params=_cp(ARB),
    )(dq, dk, dv, c, dq, dk, dv, c, proj, proj, conv_w, dmain)


def _adam_math(w, g, m, v):
    m2 = ADAM_B1 * m + (1.0 - ADAM_B1) * g
    v2 = ADAM_B2 * v + (1.0 - ADAM_B2) * (g * g)
    m_hat = m2 / (1.0 - ADAM_B1 ** ADAM_STEP)
    v_hat = v2 / (1.0 - ADAM_B2 ** ADAM_STEP)
    delta = -ADAM_LR * (m_hat / (jnp.sqrt(v_hat) + ADAM_EPS) + ADAM_WD * w)
    return delta, m2, v2


def _pair_sum(blocks, recv, core, name):
    K, _, R, C = blocks.shape
    tr = _tile(R, 256, 16)

    def body(core_ref, a_ref, b_ref, o_ref):
        del core_ref
        o_ref[0] = (a_ref[0, 0].astype(F32) + b_ref[0].astype(F32)).astype(BF16)

    spec = lambda: pl.BlockSpec((1, tr, C), lambda k, i, core_ref: (k, i, 0))
    return pl.pallas_call(
        body, name=name,
        grid_spec=pltpu.PrefetchScalarGridSpec(
            num_scalar_prefetch=1, grid=(K, R // tr),
            in_specs=[pl.BlockSpec((1, 1, tr, C), lambda k, i, core_ref: (k, core_ref[0], i, 0)), spec()],
            out_specs=spec()),
        out_shape=S((K, R, C), BF16), compiler_params=_cp(ARB, ARB),
    )(core, blocks, recv)


def _sum_adam(chip_sums, recv, w, m, v, chip, name, transposed=False):
    R, C = chip_sums.shape[1:]
    NR = recv.shape[0]
    tr = _tile(R, min(256, max(R // 4, 16)), 16)

    def body(chip_ref, own_ref, r_ref, w_ref, m_ref, v_ref, g_ref, d_ref, m2_ref, v2_ref):
        del chip_ref
        g = own_ref[0].astype(F32)
        for j in range(NR):
            g = g + r_ref[j].astype(F32)
        if transposed:
            g = g.T
        g_ref[...] = g
        d_ref[...], m2_ref[...], v2_ref[...] = _adam_math(w_ref[...], g, m_ref[...], v_ref[...])

    if transposed:
        spec = lambda: pl.BlockSpec((C, tr), lambda i, chip_ref: (0, i))
        shape = (C, R)
    else:
        spec = lambda: pl.BlockSpec((tr, C), lambda i, chip_ref: (i, 0))
        shape = (R, C)
    assert w.shape == shape
    return pl.pallas_call(
        body, name=name,
        grid_spec=pltpu.PrefetchScalarGridSpec(
            num_scalar_prefetch=1, grid=(R // tr,),
            in_specs=[pl.BlockSpec((1, tr, C), lambda i, chip_ref: (chip_ref[0], i, 0)),
                      pl.BlockSpec((NR, tr, C), lambda i, chip_ref: (0, i, 0)), spec(), spec(), spec()],
            out_specs=[spec(), spec(), spec(), spec()]),
        out_shape=[S(shape, F32)] * 4, compiler_params=_cp(ARB),
    )(chip, chip_sums, recv, w, m, v)


def _adam_small(ws, gs, ms, vs):
    n = len(ws)

    def body(*refs):
        ins, outs = refs[:4 * n], refs[4 * n:]
        for p in range(n):
            w_ref, g_ref, m_ref, v_ref = (ins[a * n + p] for a in range(4))
            outs[p][...], outs[n + p][...], outs[2 * n + p][...] = _adam_math(
                w_ref[...], g_ref[...], m_ref[...], v_ref[...])

    vm = pl.BlockSpec(memory_space=pltpu.VMEM)
    res = pl.pallas_call(
        body, name="adam_small", in_specs=[vm] * (4 * n), out_specs=[vm] * (3 * n),
        out_shape=[S(w.shape, F32) for w in ws] * 3,
    )(*ws, *gs, *ms, *vs)
    return res[:n], res[n:2 * n], res[2 * n:]


def _position():
    return lax.axis_index("x"), lax.axis_index("y"), lax.axis_index("c")


def _all_gather_weights(arr, x_in, norm_w, chip, tn, plans, nm):
    R = arr.shape[0]
    half = R // 2
    assert half % 16 == 0
    T, DM = x_in.shape
    tm = _tile(T, 512, 16)
    nstep = T // tm

    def body(chip_ref, x_ref, nw_ref, in_ref, xn_ref, out_ref, proj_ref, wtile_ref, stage_ref,
             send_sems, recv_sems, local_sem, stage_sems):
        i = pl.program_id(0)
        x, y, c = _position()
        me, sibling = (x, y, c), (x, y, 1 - c)
        xn, yn, diag = (1 - x, y), (x, 1 - y), (1 - x, 1 - y)
        upper, lower = pl.ds(0, half), pl.ds(half, half)

        def slot(p, rows=None):
            ref = out_ref.at[4 * p[0] + 2 * p[1] + p[2]]
            return ref if rows is None else ref.at[rows]

        def copy(kk, block, to, rows=None, src=None):
            return pltpu.make_async_remote_copy(
                src_ref=slot(block, rows) if src is None else src, dst_ref=slot(block, rows),
                send_sem=send_sems.at[kk], recv_sem=recv_sems.at[kk], device_id=to, device_id_type=MESH)

        mine = pltpu.make_async_copy(in_ref, slot(me), local_sem)
        first = [copy(0, me, sibling, src=in_ref), copy(1, me, (*xn, c), src=in_ref), copy(2, me, (*yn, c), src=in_ref)]

        @pl.when(i == 0)
        def _():
            mine.start()
            for cp in first:
                cp.start()
            copy(0, sibling, me).wait_recv()
            loads = [pltpu.make_async_copy(in_ref, stage_ref.at[c], stage_sems.at[0]),
                     pltpu.make_async_copy(slot(sibling), stage_ref.at[1 - c], stage_sems.at[1])]
            for cp in loads:
                cp.start()
            for cp in loads:
                cp.wait()
            for m, plan in enumerate(plans):
                @pl.when(chip_ref[0] == m)
                def _(plan=plan):
                    for d, s0, w, c0 in plan:
                        wtile_ref[:, c0:c0 + w] = stage_ref[d % 2, :, s0:s0 + w]

        xv = x_ref[...]
        r = lax.rsqrt(jnp.mean(xv * xv, axis=-1, keepdims=True) + EPS)
        xnv = (xv * r * nw_ref[...]).astype(BF16)
        xn_ref[...] = xnv
        proj_ref[...] = jnp.dot(xnv.astype(MXU), wtile_ref[...].astype(MXU), preferred_element_type=F32)

        @pl.when(i == nstep - 1)
        def _():
            sent = list(first)

            def then(cps):
                for cp in cps:
                    cp.start()
                sent.extend(cps)

            copy(1, (*xn, c), me).wait_recv()
            then([copy(5, (*xn, c), (*yn, c), rows=upper), copy(3, (*xn, c), sibling)])
            copy(2, (*yn, c), me).wait_recv()
            then([copy(6, (*yn, c), (*xn, c), rows=lower), copy(4, (*yn, c), sibling)])
            copy(5, (*diag, c), me, rows=upper).wait_recv()
            then([copy(7, (*diag, c), sibling, rows=upper)])
            copy(6, (*diag, c), me, rows=lower).wait_recv()
            then([copy(8, (*diag, c), sibling, rows=lower)])
            copy(3, (*xn, 1 - c), me).wait_recv()
            copy(4, (*yn, 1 - c), me).wait_recv()
            copy(7, (*diag, 1 - c), me, rows=upper).wait_recv()
            copy(8, (*diag, 1 - c), me, rows=lower).wait_recv()
            for cp in sent:
                cp.wait_send()
            mine.wait()

    any_spec = pl.BlockSpec(memory_space=pl.ANY)
    return pl.pallas_call(
        body, name="all_gather_weights",
        grid_spec=pltpu.PrefetchScalarGridSpec(
            num_scalar_prefetch=1, grid=(nstep,),
            in_specs=[pl.BlockSpec((tm, DM), lambda i, chip_ref: (i, 0)),
                      pl.BlockSpec((1, DM), lambda i, chip_ref: (0, 0)), any_spec],
            out_specs=[pl.BlockSpec((tm, DM), lambda i, chip_ref: (i, 0)), any_spec,
                       pl.BlockSpec((tm, tn), lambda i, chip_ref: (i, 2 * chip_ref[0]))],
            scratch_shapes=[pltpu.VMEM((DM, tn), arr.dtype), pltpu.VMEM((2,) + arr.shape, arr.dtype),
                            pltpu.SemaphoreType.DMA((9,)), pltpu.SemaphoreType.DMA((9,)), pltpu.SemaphoreType.DMA,
                            pltpu.SemaphoreType.DMA((2,))]),
        out_shape=[S((T, DM), BF16), S((N_DEV,) + arr.shape, arr.dtype), S((T, nm), F32)],
        compiler_params=_cp(ARB),
    )(chip, x_in, norm_w, arr)


def _sibling_copies(ins, outs, send_sems, recv_sems):
    x, y, c = _position()
    return [pltpu.make_async_remote_copy(src_ref=ins[a].at[k, 1 - c], dst_ref=outs[a].at[k],
                                         send_sem=send_sems.at[a, k], recv_sem=recv_sems.at[a, k],
                                         device_id=(x, y, 1 - c), device_id_type=MESH)
            for a in range(len(ins)) for k in range(ins[a].shape[0])]


def _sibling_sems(arrs):
    shape = (max(len(arrs), 1), arrs[0].shape[0] if arrs else 1)
    return [pltpu.SemaphoreType.DMA(shape), pltpu.SemaphoreType.DMA(shape)]


def _chip_exchange_copies(ins, outs, send_sems, recv_sems):
    x, y, c = _position()
    chips = [(1 - x, y), (x, 1 - y), (1 - x, 1 - y)]
    return [pltpu.make_async_remote_copy(
        src_ref=ins[a].at[2 * qx + qy], dst_ref=outs[a].at[j], send_sem=send_sems.at[a, j],
        recv_sem=recv_sems.at[a, j], device_id=(qx, qy, c), device_id_type=MESH)
        for a in range(len(ins)) for j, (qx, qy) in enumerate(chips)]


def _broadcast_copies(srcs, dsts, send_sems, recv_sems):
    x, y, c = _position()
    me = 4 * x + 2 * y + c
    cps = []
    for a in range(len(srcs)):
        for k in range(1, N_DEV):
            peer = (1 - x if k & 4 else x, 1 - y if k & 2 else y, 1 - c if k & 1 else c)
            cps.append(pltpu.make_async_remote_copy(
                src_ref=srcs[a], dst_ref=dsts[a].at[me], send_sem=send_sems.at[a, k - 1],
                recv_sem=recv_sems.at[a, k - 1], device_id=peer, device_id_type=MESH))
    return me, cps


def _all_reduce_small(part):
    R, C = part.shape

    def body(p_ref, out_ref, gath_ref, send_sems, recv_sems):
        me, cps = _broadcast_copies([p_ref], [gath_ref], send_sems, recv_sems)
        gath_ref[me] = p_ref[...]
        for cp in cps:
            cp.start()
        for cp in cps:
            cp.wait()
        acc = gath_ref[0]
        for d in range(1, N_DEV):
            acc = acc + gath_ref[d]
        out_ref[...] = acc

    vm = pl.BlockSpec(memory_space=pltpu.VMEM)
    return pl.pallas_call(
        body, name="all_reduce_small", in_specs=[vm], out_specs=vm, out_shape=S((R, C), F32),
        scratch_shapes=[pltpu.VMEM((N_DEV, R, C), F32), pltpu.SemaphoreType.DMA((1, N_DEV - 1)),
                        pltpu.SemaphoreType.DMA((1, N_DEV - 1))],
    )(part)


def _pack(parts):
    rows = []
    for p in parts:
        f = p.reshape(-1).astype(F32)
        pad = (-f.shape[0]) % (8 * LANES)
        rows.append(jnp.pad(f, (0, pad)).reshape(-1, LANES))
    return jnp.concatenate(rows, axis=0)


def _unpack(buf, shapes):
    out, r = [], 0
    for shp in shapes:
        n = 1
        for s in shp:
            n *= s
        nr = -(-n // (8 * LANES)) * 8
        out.append(buf[r:r + nr].reshape(-1)[:n].reshape(shp))
        r += nr
    return out


def kernel(x, norm_w, w_in, conv_w, a_log, dt_bias, head_norm_w, sgu_ln_w, sgu_ln_b, w_spatial, b_spatial, w_out, final_norm_w, loss_target, m_norm_w, m_w_in, m_conv_w, m_a_log, m_dt_bias, m_head_norm_w, m_sgu_ln_w, m_sgu_ln_b, m_w_spatial, m_b_spatial, m_w_out, m_final_norm_w, v_norm_w, v_w_in, v_conv_w, v_a_log, v_dt_bias, v_head_norm_w, v_sgu_ln_w, v_sgu_ln_b, v_w_spatial, v_b_spatial, v_w_out, v_final_norm_w):
    T, DM = x.shape[1], x.shape[2]
    H, D = a_log.shape[1], head_norm_w.shape[1]
    G, P = w_spatial.shape[1], w_spatial.shape[2]
    AW, BW = H * D, G * P
    MIX = AW + BW
    WD = w_in.shape[2]
    IN = N_DEV * WD
    RO = w_out.shape[1]
    CW = conv_w.shape[2]
    sizes = (3 * AW, AW, H, H, BW, BW, BW)
    assert sum(sizes) == IN and 2 * H <= LANES and 3 * H <= 32 and N_DEV * RO == MIX and N_DEV * CW == 3 * AW
    offs = [0]
    for s in sizes:
        offs.append(offs[-1] + s)
    px, py, pc = _position()
    dev = 4 * px + 2 * py + pc
    chip = 2 * px + py

    x2, tgt = x[0], loss_target[0]

    core_idx = jnp.reshape(pc, (1,)).astype(jnp.int32)
    chip_idx = jnp.reshape(chip, (1,)).astype(jnp.int32)
    NM = IN - 2 * H
    tn_loc, tile_plans = _local_tiles(WD, offs[2], offs[4], NM)
    xn, g_win, proj_part = _all_gather_weights(
        _cast_bf16_t(w_in[0].T, "cast_w_in"), x2, norm_w, chip_idx, tn_loc, tile_plans, NM)
    w_main, w_ba = _relayout_w(g_win, offs[2], offs[4])
    alog_row = jnp.pad(a_log, ((0, 0), (H, LANES - 2 * H)))
    dtb_row = jnp.pad(dt_bias, ((0, 0), (H, LANES - 2 * H)))
    bs_t = b_spatial[0].T

    others = jnp.arange(N_DEV - 2, dtype=jnp.int32)
    others = others + (others >= 2 * chip).astype(jnp.int32)
    proj, ba, (g_wout, g_conv) = _in_proj(xn, w_main, w_ba, proj_part, others, tn_loc,
                                          [_cast_bf16(w_out[0], "cast_w_out"), conv_w[0]])
    w_out_full = g_wout.reshape(MIX, DM)
    conv_full = g_conv.transpose(1, 0, 2).reshape(4, 3 * AW)
    q, k, v, c, gcol, grow = _prep_a_fwd(proj, ba, conv_full, alog_row, dtb_row, H, D)
    o, vnew, ssave, asave = _delta_fwd(q, k, v, gcol, grow, H, D)
    ocat = _mix_fwd(o, proj, head_norm_w, sgu_ln_w, sgu_ln_b, w_spatial[0], bs_t, H, D, G, P)
    dh, dh_bf, d_ocat, loss_acc, g_fnw = _out_proj_loss(ocat, w_out_full, x2, tgt, final_norm_w.reshape(1, DM))

    g_wout_blocks = _grad_w(ocat, dh_bf, "grad_w_out").reshape(4, 2, RO, DM)
    (d_o, dmain, g_hnw, g_ln, g_wsp, g_bs_t), (sib_wout,) = _mix_bwd(
        d_ocat, o, proj, head_norm_w, sgu_ln_w, sgu_ln_b, w_spatial[0], bs_t, H, D, G, P, [g_wout_blocks])
    chip_wout = _pair_sum(g_wout_blocks, sib_wout, core_idx, "pair_sum_w_out")
    (dq, dk, dv, dba, dpar), (recv_wout,) = _delta_bwd(
        q, k, v, gcol, grow, ba, vnew, ssave, asave, d_o, alog_row, dtb_row, H, D, [chip_wout])
    dmain, g_conv_part = _prep_a_bwd(dq, dk, dv, c, proj, conv_full, dmain, H, D)
    keep_win, sib_win = _grad_w_in(xn, dmain, dba, WD, offs[2], offs[4])
    chip_win = _pair_sum_plain(keep_win, sib_win, "pair_sum_w_in")
    small_shapes = [a_log.shape, dt_bias.shape, head_norm_w.shape, sgu_ln_w.shape, sgu_ln_b.shape,
                    w_spatial.shape, b_spatial.shape, final_norm_w.shape]
    parts = [dpar[0, H:2 * H], dpar[1, H:2 * H], g_hnw[0], g_ln[0], g_ln[1], g_wsp, g_bs_t[:, :G].T, g_fnw[0],
             g_conv_part[:4], loss_acc[0, :1]]
    grad_x, g_nw, small_gath, recv_win = _dx(dmain, dba, w_main, w_ba, x2, dh, norm_w, chip_win, _pack(parts), 4)
    red = _sum_slots(small_gath, small_shapes + [(4, 3 * AW), (1,)])
    grad_w_in, delta_w_in, new_m_w_in, new_v_w_in = _sum_adam(
        chip_win, recv_win, w_in[0].T, m_w_in[0].T, v_w_in[0].T, chip_idx, "sum_adam_w_in", transposed=True)
    grad_x = _copy_rows(grad_x, "copy_grad_x")
    grad_w_out, delta_w_out, new_m_w_out, new_v_w_out = _sum_adam(
        chip_wout, recv_wout, w_out[0], m_w_out[0], v_w_out[0], chip_idx, "sum_adam_w_out")
    red_nw = _all_reduce_small(_pack([g_nw[0]]))
    grads_small = _unpack(red_nw, [norm_w.shape]) + red
    loss = grads_small.pop()[0]
    g_conv_full = grads_small.pop()
    grad_conv = lax.dynamic_slice_in_dim(g_conv_full, dev * CW, CW, axis=1)[None]
    small_w = [norm_w, a_log, dt_bias, head_norm_w, sgu_ln_w, sgu_ln_b, w_spatial, b_spatial, final_norm_w, conv_w]
    small_m = [m_norm_w, m_a_log, m_dt_bias, m_head_norm_w, m_sgu_ln_w, m_sgu_ln_b, m_w_spatial, m_b_spatial,
               m_final_norm_w, m_conv_w]
    small_v = [v_norm_w, v_a_log, v_dt_bias, v_head_norm_w, v_sgu_ln_w, v_sgu_ln_b, v_w_spatial, v_b_spatial,
               v_final_norm_w, v_conv_w]
    small_g = grads_small + [grad_conv]
    d_s, m_s, v_s = _adam_small(small_w, small_g, small_m, small_v)

    def order(small, win, wout):
        return [small[0], win.T[None], small[9], small[1], small[2], small[3], small[4], small[5], small[6], small[7],
                wout[None], small[8]]

    grads = order(small_g, grad_w_in, grad_w_out)
    deltas = order(d_s, delta_w_in, delta_w_out)
    new_m = order(m_s, new_m_w_in, new_m_w_out)
    new_v = order(v_s, new_v_w_in, new_v_w_out)
    return (loss, grad_x[None], *grads, *deltas, *new_m, *new_v)
```
